```python
import math
import jax, jax.numpy as jnp
from jax import lax
import numpy as np

D_MODEL = 1024
BATCH = 8
SEQ = 4096
DEPTH = 4

N_A_LAYERS = DEPTH // 2
N_B_LAYERS = DEPTH - N_A_LAYERS

HEAD_DIM = 64
SB_HEADS = D_MODEL // HEAD_DIM
SWA_Q_HEADS = D_MODEL // HEAD_DIM
SWA_KV_HEADS = SWA_Q_HEADS // 8
SWA_GROUP = SWA_Q_HEADS // SWA_KV_HEADS
WINDOW = 128
BLOCK = 128
D_FF = 4 * D_MODEL
NUM_BUCKETS = 32
MAX_EXACT = NUM_BUCKETS // 2
MAX_DISTANCE = 128
EPS = 1e-5
NEG_INF = -1e30

kernel_name = "yoco_stickbreaking_swa_sinks_trunk"


def rmsnorm(x, g):
    xf = x.astype(jnp.float32)
    y = xf * lax.rsqrt(jnp.mean(xf * xf, axis=-1, keepdims=True) + EPS)
    return (y * g.astype(jnp.float32)).astype(x.dtype)


def sq_relu_mlp(x, w_up, w_down):
    u = jax.nn.relu(x @ w_up)
    return (u * u) @ w_down


def stick_breaking_attention(x, w_qkv, w_o):
    b, s_len, _ = x.shape
    nb = s_len // BLOCK
    scale = 1.0 / math.sqrt(HEAD_DIM)
    q, k, v = jnp.split(x @ w_qkv, 3, axis=-1)
    q = q.reshape(b, nb, BLOCK, SB_HEADS, HEAD_DIM).transpose(1, 0, 3, 2, 4)
    k = k.reshape(b, s_len, SB_HEADS, HEAD_DIM).transpose(0, 2, 1, 3)
    v = v.reshape(b, s_len, SB_HEADS, HEAD_DIM).transpose(0, 2, 1, 3)
    key_pos = jnp.arange(s_len)

    def one_block(args):
        q_blk, blk = args
        z = jnp.einsum('bhqd,bhkd->bhqk', q_blk, k, preferred_element_type=jnp.float32) * scale
        q_pos = blk * BLOCK + jnp.arange(BLOCK)
        causal = key_pos[None, :] < q_pos[:, None]
        log_1m = jnp.where(causal, jax.nn.log_sigmoid(-z), 0.0)
        after = lax.cumsum(log_1m, axis=3, reverse=True) - log_1m
        w = jnp.where(causal, jnp.exp(jax.nn.log_sigmoid(z) + after), 0.0)
        return jnp.einsum('bhqk,bhkd->bhqd', w.astype(v.dtype), v)

    o = lax.map(one_block, (q, jnp.arange(nb)))
    o = o.transpose(1, 0, 3, 2, 4).reshape(b, s_len, SB_HEADS * HEAD_DIM)
    return o @ w_o


def t5_bucket(n):
    nf = jnp.maximum(n, 1).astype(jnp.float32)
    large = MAX_EXACT + (jnp.log(nf / MAX_EXACT) / math.log(MAX_DISTANCE / MAX_EXACT)
                         * (NUM_BUCKETS - MAX_EXACT)).astype(jnp.int32)
    large = jnp.minimum(large, NUM_BUCKETS - 1)
    return jnp.where(n < MAX_EXACT, n, large)


def band_distance():
    qi = jnp.arange(BLOCK)[:, None]
    kj = jnp.arange(2 * BLOCK)[None, :]
    return qi + BLOCK - kj


def relative_bias_band(rel_bias):
    bucket = t5_bucket(jnp.maximum(band_distance(), 0))
    bias = rel_bias.astype(jnp.float32)[bucket]
    return bias.transpose(2, 0, 1).reshape(SWA_KV_HEADS, SWA_GROUP, BLOCK, 2 * BLOCK)


def band_valid(nb):
    dist = band_distance()
    key_pos = jnp.arange(nb)[:, None, None] * BLOCK - BLOCK + jnp.arange(2 * BLOCK)[None, None, :]
    return (dist >= 0) & (dist < WINDOW) & (key_pos >= 0)


def to_band(t):
    b, s_len, g, dh = t.shape
    nb = s_len // BLOCK
    prev = jnp.pad(t, ((0, 0), (BLOCK, 0), (0, 0), (0, 0)))[:, :s_len].reshape(b, nb, BLOCK, g, dh)
    cur = t.reshape(b, nb, BLOCK, g, dh)
    return jnp.concatenate([prev, cur], axis=2)


def shared_kv(h, kv_norm, w_kv, b_kv):
    b, s_len, _ = h.shape
    kv = rmsnorm(h, kv_norm) @ w_kv + b_kv
    k, v = jnp.split(kv, 2, axis=-1)
    k = k.reshape(b, s_len, SWA_KV_HEADS, HEAD_DIM)
    v = v.reshape(b, s_len, SWA_KV_HEADS, HEAD_DIM)
    return to_band(k), to_band(v)


def swa_sink_attention(x, k_band, v_band, w_q, b_q, sinks, w_o, b_o, bias_band, valid):
    b, s_len, _ = x.shape
    nb = s_len // BLOCK
    scale = 1.0 / math.sqrt(HEAD_DIM)
    q = (x @ w_q + b_q).reshape(b, nb, BLOCK, SWA_KV_HEADS, SWA_GROUP, HEAD_DIM)
    s = jnp.einsum('bnqgrd,bnkgd->bngrqk', q, k_band, preferred_element_type=jnp.float32) * scale
    s = s + bias_band[None, None]
    s = jnp.where(valid[None, :, None, None], s, NEG_INF)
    sink = jnp.broadcast_to(sinks.astype(jnp.float32).reshape(SWA_KV_HEADS, SWA_GROUP, 1, 1),
                            s.shape[:-1] + (1,))
    p = jax.nn.softmax(jnp.concatenate([s, sink], axis=-1), axis=-1)[..., :-1]
    o = jnp.einsum('bngrqk,bnkgd->bnqgrd', p.astype(v_band.dtype), v_band)
    return o.reshape(b, s_len, SWA_Q_HEADS * HEAD_DIM) @ w_o + b_o


def _fwd_setup_inputs(seed: int = 0) -> dict:
    key = jax.random.key(seed)
    ks = jax.random.split(key, 20)
    f32 = jnp.float32

    def nrm(k, shape, fan_in):
        return jax.random.normal(k, shape, f32) * fan_in ** -0.5

    def gain(k, shape):
        return 1.0 + 0.02 * jax.random.normal(k, shape, f32)

    kv_width = 2 * SWA_KV_HEADS * HEAD_DIM
    return {
        "x": jax.random.normal(ks[0], (BATCH, SEQ, D_MODEL), f32),
        "a_norm": gain(ks[1], (N_A_LAYERS, D_MODEL)),
        "a_wqkv": nrm(ks[2], (N_A_LAYERS, D_MODEL, 3 * SB_HEADS * HEAD_DIM), D_MODEL),
        "a_wo": nrm(ks[3], (N_A_LAYERS, SB_HEADS * HEAD_DIM, D_MODEL), SB_HEADS * HEAD_DIM),
        "kv_norm": gain(ks[4], (D_MODEL,)),
        "w_kv": nrm(ks[5], (D_MODEL, kv_width), D_MODEL),
        "b_kv": 0.02 * jax.random.normal(ks[6], (kv_width,), f32),
        "b_norm": gain(ks[7], (N_B_LAYERS, D_MODEL)),
        "b_wq": nrm(ks[8], (N_B_LAYERS, D_MODEL, SWA_Q_HEADS * HEAD_DIM), D_MODEL),
        "b_bq": 0.02 * jax.random.normal(ks[9], (N_B_LAYERS, SWA_Q_HEADS * HEAD_DIM), f32),
        "b_sinks": 0.5 * jax.random.normal(ks[10], (N_B_LAYERS, SWA_Q_HEADS), f32),
        "b_wo": nrm(ks[11], (N_B_LAYERS, SWA_Q_HEADS * HEAD_DIM, D_MODEL), SWA_Q_HEADS * HEAD_DIM),
        "b_bo": 0.02 * jax.random.normal(ks[12], (N_B_LAYERS, D_MODEL), f32),
        "rel_bias": 0.5 * jax.random.normal(ks[13], (NUM_BUCKETS, SWA_Q_HEADS), f32),
        "mlp_norm": gain(ks[14], (DEPTH, D_MODEL)),
        "mlp_up": nrm(ks[15], (DEPTH, D_MODEL, D_FF), D_MODEL),
        "mlp_down": nrm(ks[16], (DEPTH, D_FF, D_MODEL), D_FF),
        "final_norm": gain(ks[17], (D_MODEL,)),
    }


def _fwd_reference(x, a_norm, a_wqkv, a_wo, kv_norm, w_kv, b_kv, b_norm, b_wq, b_bq, b_sinks,
              b_wo, b_bo, rel_bias, mlp_norm, mlp_up, mlp_down, final_norm):
    nb = x.shape[1] // BLOCK
    bias_band = relative_bias_band(rel_bias)
    valid = band_valid(nb)
    h = x
    k_band = v_band = None
    for layer in range(DEPTH):
        if layer < N_A_LAYERS:
            h = h + stick_breaking_attention(rmsnorm(h, a_norm[layer]), a_wqkv[layer], a_wo[layer])
        else:
            j = layer - N_A_LAYERS
            if j == 0:
                k_band, v_band = shared_kv(h, kv_norm, w_kv, b_kv)
            h = h + swa_sink_attention(rmsnorm(h, b_norm[j]), k_band, v_band, b_wq[j], b_bq[j],
                                       b_sinks[j], b_wo[j], b_bo[j], bias_band, valid)
        h = h + sq_relu_mlp(rmsnorm(h, mlp_norm[layer]), mlp_up[layer], mlp_down[layer])
    return rmsnorm(h, final_norm)


import jax as _jax
import jax.numpy as _jnp

TWIN_FORMAT = 'train_step'
FWD_PARAMS = ['x', 'a_norm', 'a_wqkv', 'a_wo', 'kv_norm', 'w_kv', 'b_kv', 'b_norm', 'b_wq', 'b_bq', 'b_sinks', 'b_wo', 'b_bo', 'rel_bias', 'mlp_norm', 'mlp_up', 'mlp_down', 'final_norm']
TWIN_WEIGHTS = ['a_norm', 'a_wqkv', 'a_wo', 'kv_norm', 'w_kv', 'b_kv', 'b_norm', 'b_wq', 'b_bq', 'b_sinks', 'b_wo', 'b_bo', 'rel_bias', 'mlp_norm', 'mlp_up', 'mlp_down', 'final_norm']
TWIN_DIFF_INPUT = 'x'
TWIN_INPUTS = ['x', 'a_norm', 'a_wqkv', 'a_wo', 'kv_norm', 'w_kv', 'b_kv', 'b_norm', 'b_wq', 'b_bq', 'b_sinks', 'b_wo', 'b_bo', 'rel_bias', 'mlp_norm', 'mlp_up', 'mlp_down', 'final_norm', 'loss_target', 'm_a_norm', 'm_a_wqkv', 'm_a_wo', 'm_kv_norm', 'm_w_kv', 'm_b_kv', 'm_b_norm', 'm_b_wq', 'm_b_bq', 'm_b_sinks', 'm_b_wo', 'm_b_bo', 'm_rel_bias', 'm_mlp_norm', 'm_mlp_up', 'm_mlp_down', 'm_final_norm', 'v_a_norm', 'v_a_wqkv', 'v_a_wo', 'v_kv_norm', 'v_w_kv', 'v_b_kv', 'v_b_norm', 'v_b_wq', 'v_b_bq', 'v_b_sinks', 'v_b_wo', 'v_b_bo', 'v_rel_bias', 'v_mlp_norm', 'v_mlp_up', 'v_mlp_down', 'v_final_norm']
TWIN_OUTPUTS = ['loss', 'grad_x', 'grad_a_norm', 'grad_a_wqkv', 'grad_a_wo', 'grad_kv_norm', 'grad_w_kv', 'grad_b_kv', 'grad_b_norm', 'grad_b_wq', 'grad_b_bq', 'grad_b_sinks', 'grad_b_wo', 'grad_b_bo', 'grad_rel_bias', 'grad_mlp_norm', 'grad_mlp_up', 'grad_mlp_down', 'grad_final_norm', 'delta_a_norm', 'delta_a_wqkv', 'delta_a_wo', 'delta_kv_norm', 'delta_w_kv', 'delta_b_kv', 'delta_b_norm', 'delta_b_wq', 'delta_b_bq', 'delta_b_sinks', 'delta_b_wo', 'delta_b_bo', 'delta_rel_bias', 'delta_mlp_norm', 'delta_mlp_up', 'delta_mlp_down', 'delta_final_norm', 'new_m_a_norm', 'new_m_a_wqkv', 'new_m_a_wo', 'new_m_kv_norm', 'new_m_w_kv', 'new_m_b_kv', 'new_m_b_norm', 'new_m_b_wq', 'new_m_b_bq', 'new_m_b_sinks', 'new_m_b_wo', 'new_m_b_bo', 'new_m_rel_bias', 'new_m_mlp_norm', 'new_m_mlp_up', 'new_m_mlp_down', 'new_m_final_norm', 'new_v_a_norm', 'new_v_a_wqkv', 'new_v_a_wo', 'new_v_kv_norm', 'new_v_w_kv', 'new_v_b_kv', 'new_v_b_norm', 'new_v_b_wq', 'new_v_b_bq', 'new_v_b_sinks', 'new_v_b_wo', 'new_v_b_bo', 'new_v_rel_bias', 'new_v_mlp_norm', 'new_v_mlp_up', 'new_v_mlp_down', 'new_v_final_norm']
TWIN_LEAF_KINDS = {'loss': 'loss', 'grad_x': 'grad_x', 'grad_a_norm': 'grad_w', 'grad_a_wqkv': 'grad_w', 'grad_a_wo': 'grad_w', 'grad_kv_norm': 'grad_w', 'grad_w_kv': 'grad_w', 'grad_b_kv': 'grad_w', 'grad_b_norm': 'grad_w', 'grad_b_wq': 'grad_w', 'grad_b_bq': 'grad_w', 'grad_b_sinks': 'grad_w', 'grad_b_wo': 'grad_w', 'grad_b_bo': 'grad_w', 'grad_rel_bias': 'grad_w', 'grad_mlp_norm': 'grad_w', 'grad_mlp_up': 'grad_w', 'grad_mlp_down': 'grad_w', 'grad_final_norm': 'grad_w', 'delta_a_norm': 'delta_w', 'delta_a_wqkv': 'delta_w', 'delta_a_wo': 'delta_w', 'delta_kv_norm': 'delta_w', 'delta_w_kv': 'delta_w', 'delta_b_kv': 'delta_w', 'delta_b_norm': 'delta_w', 'delta_b_wq': 'delta_w', 'delta_b_bq': 'delta_w', 'delta_b_sinks': 'delta_w', 'delta_b_wo': 'delta_w', 'delta_b_bo': 'delta_w', 'delta_rel_bias': 'delta_w', 'delta_mlp_norm': 'delta_w', 'delta_mlp_up': 'delta_w', 'delta_mlp_down': 'delta_w', 'delta_final_norm': 'delta_w', 'new_m_a_norm': 'new_m', 'new_m_a_wqkv': 'new_m', 'new_m_a_wo': 'new_m', 'new_m_kv_norm': 'new_m', 'new_m_w_kv': 'new_m', 'new_m_b_kv': 'new_m', 'new_m_b_norm': 'new_m', 'new_m_b_wq': 'new_m', 'new_m_b_bq': 'new_m', 'new_m_b_sinks': 'new_m', 'new_m_b_wo': 'new_m', 'new_m_b_bo': 'new_m', 'new_m_rel_bias': 'new_m', 'new_m_mlp_norm': 'new_m', 'new_m_mlp_up': 'new_m', 'new_m_mlp_down': 'new_m', 'new_m_final_norm': 'new_m', 'new_v_a_norm': 'new_v', 'new_v_a_wqkv': 'new_v', 'new_v_a_wo': 'new_v', 'new_v_kv_norm': 'new_v', 'new_v_w_kv': 'new_v', 'new_v_b_kv': 'new_v', 'new_v_b_norm': 'new_v', 'new_v_b_wq': 'new_v', 'new_v_b_bq': 'new_v', 'new_v_b_sinks': 'new_v', 'new_v_b_wo': 'new_v', 'new_v_b_bo': 'new_v', 'new_v_rel_bias': 'new_v', 'new_v_mlp_norm': 'new_v', 'new_v_mlp_up': 'new_v', 'new_v_mlp_down': 'new_v', 'new_v_final_norm': 'new_v'}


def _forward(args):
    return _fwd_reference(*[args[k] for k in FWD_PARAMS])


def _output_shape():
    out = _jax.eval_shape(lambda: _forward(_fwd_setup_inputs(0)))
    return out.shape, out.dtype

N_MICROBATCH = 1
ADAM_LR = 0.001
ADAM_B1 = 0.9
ADAM_B2 = 0.999
ADAM_EPS = 1e-08
ADAM_WD = 0.01
ADAM_STEP = 10
PER_EXAMPLE_BATCH_AXIS = {'x': 0, 'loss_target': 0}
SHARED_INPUTS = []
_WEIGHT_DTYPES = {'a_norm': _jnp.float32, 'a_wqkv': _jnp.float32, 'a_wo': _jnp.float32, 'kv_norm': _jnp.float32, 'w_kv': _jnp.float32, 'b_kv': _jnp.float32, 'b_norm': _jnp.float32, 'b_wq': _jnp.float32, 'b_bq': _jnp.float32, 'b_sinks': _jnp.float32, 'b_wo': _jnp.float32, 'b_bo': _jnp.float32, 'rel_bias': _jnp.float32, 'mlp_norm': _jnp.float32, 'mlp_up': _jnp.float32, 'mlp_down': _jnp.float32, 'final_norm': _jnp.float32}
MOMENT_SCALE = {'a_norm': 1.214529e-01, 'a_wqkv': 6.920229e-02, 'a_wo': 1.035014e-01, 'kv_norm': 6.239915e-02, 'w_kv': 1.295738e-01, 'b_kv': 3.812377e-01, 'b_norm': 1.216026e-02, 'b_wq': 1.231507e-02, 'b_bq': 1.244468e-02, 'b_sinks': 1.525458e-02, 'b_wo': 5.432247e-02, 'b_bo': 1.562333e-01, 'rel_bias': 2.566544e-02, 'mlp_norm': 1.270254e-01, 'mlp_up': 6.298264e-02, 'mlp_down': 1.322630e-01, 'final_norm': 3.343407e+01}


def _to_microbatches(a, axis):
    t = _jnp.moveaxis(a, axis, 0)
    t = t.reshape((N_MICROBATCH, t.shape[0] // N_MICROBATCH) + t.shape[1:])
    return _jnp.moveaxis(t, 1, axis + 1)


def setup_inputs(seed: int = 0) -> dict:
    inp = _fwd_setup_inputs(seed)
    key = _jax.random.fold_in(_jax.random.key(seed), 7919)
    shape, _ = _output_shape()
    out = dict(inp)
    out["loss_target"] = _jax.random.normal(_jax.random.fold_in(key, 0), shape, _jnp.float32)
    for i, name in enumerate(TWIN_WEIGHTS):
        w = inp[name].astype(_jnp.float32)
        if MOMENT_SCALE is None:
            s = _jnp.sqrt(_jnp.mean(_jnp.square(w)) + 1e-30)
        else:
            s = MOMENT_SCALE[name]
        km, kv = _jax.random.split(_jax.random.fold_in(key, i + 1))
        out[name] = w
        out["m_" + name] = s * _jax.random.normal(km, w.shape, _jnp.float32)
        out["v_" + name] = (s * s) * _jax.random.uniform(kv, w.shape, _jnp.float32, 0.5, 1.5)
    if N_MICROBATCH > 1:
        for name, axis in PER_EXAMPLE_BATCH_AXIS.items():
            out[name] = _to_microbatches(out[name], axis)
    return {'x': out['x'], 'a_norm': out['a_norm'], 'a_wqkv': out['a_wqkv'], 'a_wo': out['a_wo'], 'kv_norm': out['kv_norm'], 'w_kv': out['w_kv'], 'b_kv': out['b_kv'], 'b_norm': out['b_norm'], 'b_wq': out['b_wq'], 'b_bq': out['b_bq'], 'b_sinks': out['b_sinks'], 'b_wo': out['b_wo'], 'b_bo': out['b_bo'], 'rel_bias': out['rel_bias'], 'mlp_norm': out['mlp_norm'], 'mlp_up': out['mlp_up'], 'mlp_down': out['mlp_down'], 'final_norm': out['final_norm'], 'loss_target': out['loss_target'], 'm_a_norm': out['m_a_norm'], 'm_a_wqkv': out['m_a_wqkv'], 'm_a_wo': out['m_a_wo'], 'm_kv_norm': out['m_kv_norm'], 'm_w_kv': out['m_w_kv'], 'm_b_kv': out['m_b_kv'], 'm_b_norm': out['m_b_norm'], 'm_b_wq': out['m_b_wq'], 'm_b_bq': out['m_b_bq'], 'm_b_sinks': out['m_b_sinks'], 'm_b_wo': out['m_b_wo'], 'm_b_bo': out['m_b_bo'], 'm_rel_bias': out['m_rel_bias'], 'm_mlp_norm': out['m_mlp_norm'], 'm_mlp_up': out['m_mlp_up'], 'm_mlp_down': out['m_mlp_down'], 'm_final_norm': out['m_final_norm'], 'v_a_norm': out['v_a_norm'], 'v_a_wqkv': out['v_a_wqkv'], 'v_a_wo': out['v_a_wo'], 'v_kv_norm': out['v_kv_norm'], 'v_w_kv': out['v_w_kv'], 'v_b_kv': out['v_b_kv'], 'v_b_norm': out['v_b_norm'], 'v_b_wq': out['v_b_wq'], 'v_b_bq': out['v_b_bq'], 'v_b_sinks': out['v_b_sinks'], 'v_b_wo': out['v_b_wo'], 'v_b_bo': out['v_b_bo'], 'v_rel_bias': out['v_rel_bias'], 'v_mlp_norm': out['v_mlp_norm'], 'v_mlp_up': out['v_mlp_up'], 'v_mlp_down': out['v_mlp_down'], 'v_final_norm': out['v_final_norm']}


def _loss(weights, diff, rest, loss_target):
    with _jax.named_scope("forward"):
        args = {**rest, TWIN_DIFF_INPUT: diff, **{k: w.astype(_WEIGHT_DTYPES[k]) for k, w in weights.items()}}
        y = _forward(args)
    with _jax.named_scope("loss_head"):
        err = _jnp.square(y.astype(_jnp.float32) - loss_target)
        return 0.5 * _jnp.sum(_jnp.mean(err, axis=-1)) if err.ndim else 0.5 * err


def _adamw(w, g, m, v):
    m = ADAM_B1 * m + (1.0 - ADAM_B1) * g
    v = ADAM_B2 * v + (1.0 - ADAM_B2) * _jnp.square(g)
    m_hat = m / (1.0 - ADAM_B1 ** ADAM_STEP)
    v_hat = v / (1.0 - ADAM_B2 ** ADAM_STEP)
    delta = -ADAM_LR * (m_hat / (_jnp.sqrt(v_hat) + ADAM_EPS) + ADAM_WD * w)
    return delta, m, v


def reference(x, a_norm, a_wqkv, a_wo, kv_norm, w_kv, b_kv, b_norm, b_wq, b_bq, b_sinks, b_wo, b_bo, rel_bias, mlp_norm, mlp_up, mlp_down, final_norm, loss_target, m_a_norm, m_a_wqkv, m_a_wo, m_kv_norm, m_w_kv, m_b_kv, m_b_norm, m_b_wq, m_b_bq, m_b_sinks, m_b_wo, m_b_bo, m_rel_bias, m_mlp_norm, m_mlp_up, m_mlp_down, m_final_norm, v_a_norm, v_a_wqkv, v_a_wo, v_kv_norm, v_w_kv, v_b_kv, v_b_norm, v_b_wq, v_b_bq, v_b_sinks, v_b_wo, v_b_bo, v_rel_bias, v_mlp_norm, v_mlp_up, v_mlp_down, v_final_norm):
    given = dict(x=x, a_norm=a_norm, a_wqkv=a_wqkv, a_wo=a_wo, kv_norm=kv_norm, w_kv=w_kv, b_kv=b_kv, b_norm=b_norm, b_wq=b_wq, b_bq=b_bq, b_sinks=b_sinks, b_wo=b_wo, b_bo=b_bo, rel_bias=rel_bias, mlp_norm=mlp_norm, mlp_up=mlp_up, mlp_down=mlp_down, final_norm=final_norm, loss_target=loss_target, m_a_norm=m_a_norm, m_a_wqkv=m_a_wqkv, m_a_wo=m_a_wo, m_kv_norm=m_kv_norm, m_w_kv=m_w_kv, m_b_kv=m_b_kv, m_b_norm=m_b_norm, m_b_wq=m_b_wq, m_b_bq=m_b_bq, m_b_sinks=m_b_sinks, m_b_wo=m_b_wo, m_b_bo=m_b_bo, m_rel_bias=m_rel_bias, m_mlp_norm=m_mlp_norm, m_mlp_up=m_mlp_up, m_mlp_down=m_mlp_down, m_final_norm=m_final_norm, v_a_norm=v_a_norm, v_a_wqkv=v_a_wqkv, v_a_wo=v_a_wo, v_kv_norm=v_kv_norm, v_w_kv=v_w_kv, v_b_kv=v_b_kv, v_b_norm=v_b_norm, v_b_wq=v_b_wq, v_b_bq=v_b_bq, v_b_sinks=v_b_sinks, v_b_wo=v_b_wo, v_b_bo=v_b_bo, v_rel_bias=v_rel_bias, v_mlp_norm=v_mlp_norm, v_mlp_up=v_mlp_up, v_mlp_down=v_mlp_down, v_final_norm=v_final_norm)
    weights = {n: given[n] for n in TWIN_WEIGHTS}
    shared = {n: given[n] for n in SHARED_INPUTS}
    per_example = {n: given[n] for n in ['x']}
    grad_fn = _jax.value_and_grad(_loss, argnums=(0, 1))

    def one_microbatch(ex, loss_target):
        ex = dict(ex)
        diff = ex.pop(TWIN_DIFF_INPUT)
        return grad_fn(weights, diff, {**shared, **ex}, loss_target)

    if N_MICROBATCH == 1:
        loss, (grad_w, grad_x) = one_microbatch(per_example, given["loss_target"])
    else:
        def body(carry, xs):
            loss_sum, grad_sum = carry
            l_k, (gw_k, gx_k) = one_microbatch(xs[0], xs[1])
            with _jax.named_scope("update"):
                return (loss_sum + l_k, _jax.tree.map(_jnp.add, grad_sum, gw_k)), gx_k

        init = (_jnp.zeros((), _jnp.float32), _jax.tree.map(_jnp.zeros_like, weights))
        (loss, grad_w), grad_x = _jax.lax.scan(body, init, (per_example, given["loss_target"]))
    with _jax.named_scope("update"):
        delta_w, new_m, new_v = {}, {}, {}
        for n in TWIN_WEIGHTS:
            delta_w[n], new_m[n], new_v[n] = _adamw(weights[n], grad_w[n], given["m_" + n], given["v_" + n])
    return (loss, grad_x, *[grad_w[n] for n in TWIN_WEIGHTS], *[delta_w[n] for n in TWIN_WEIGHTS],
            *[new_m[n] for n in TWIN_WEIGHTS], *[new_v[n] for n in TWIN_WEIGHTS])
```

```python
import functools
import math

import numpy as np
import jax
import jax.numpy as jnp
from jax import lax
from jax.experimental import pallas as pl
from jax.experimental.pallas import tpu as pltpu

F32 = jnp.float32
BF16 = jnp.bfloat16
MESH = pl.DeviceIdType.MESH

N_DEV = 8
HEAD_DIM = 64
WINDOW = 128
N_BUCKETS = 32
EPS = 1e-5
NEG_INF = -1e30
Q_SCALE = 1.0 / math.sqrt(HEAD_DIM)

ADAM_LR, ADAM_B1, ADAM_B2, ADAM_EPS, ADAM_WD, ADAM_STEP = 0.001, 0.9, 0.999, 1e-08, 0.01, 10

SB_BQ = 512
SB_BK = 128
ROW_TILE = 512
VMEM_LIMIT = 56 * 1024 * 1024

WEIGHTS = ['a_norm', 'a_wqkv', 'a_wo', 'kv_norm', 'w_kv', 'b_kv', 'b_norm', 'b_wq', 'b_bq', 'b_sinks', 'b_wo',
           'b_bo', 'rel_bias', 'mlp_norm', 'mlp_up', 'mlp_down', 'final_norm']
BIG = ['a_wqkv', 'a_wo', 'w_kv', 'b_wq', 'b_wo', 'mlp_up', 'mlp_down']
COL_SHARDED = ('a_wqkv', 'mlp_up')
SMALL = ['a_norm', 'kv_norm', 'b_kv', 'b_norm', 'b_bq', 'b_sinks', 'b_bo', 'rel_bias', 'mlp_norm', 'final_norm']


def _params(sem=None):
    return pltpu.CompilerParams(dimension_semantics=sem, vmem_limit_bytes=VMEM_LIMIT)


def _pick(n, cands):
    for c in cands:
        if n % c == 0:
            return c
    raise ValueError(n)


def mm_nn(name, a, w4, layer, epilogue, extras, out_dtypes):
    m, k = a.shape
    nj, _, kw, n = w4.shape
    assert kw == k
    tm = _pick(m, (512, 256, 128))
    tn = n if n <= 512 else 512
    nt = n // tn
    tk = _pick(k, (1024, 512))
    nk = k // tk
    ne, no = len(extras), len(out_dtypes)

    def body(a_ref, w_ref, *rest):
        ex, outs, acc = rest[:ne], rest[ne:ne + no], rest[-1]
        kk = pl.program_id(2)

        @pl.when(kk == 0)
        def _():
            acc[...] = jnp.zeros_like(acc)

        acc[...] += jnp.dot(a_ref[...], w_ref[...], preferred_element_type=F32)

        @pl.when(kk == nk - 1)
        def _():
            res = epilogue(acc[...], *[e[...] for e in ex])
            for o, r in zip(outs, res):
                o[...] = r.astype(o.dtype)

    tile = pl.BlockSpec((tm, tn), lambda i, j, kk: (i, j))
    ex_specs = [tile if e.shape[0] == m else pl.BlockSpec((1, tn), lambda i, j, kk: (0, j)) for e in extras]
    return pl.pallas_call(
        body, name=name, grid=(m // tm, nj * nt, nk),
        in_specs=[pl.BlockSpec((tm, tk), lambda i, j, kk: (i, kk)),
                  pl.BlockSpec((None, None, tk, tn), lambda i, j, kk: (j // nt, layer, kk, j % nt))] + ex_specs,
        out_specs=[tile] * no,
        out_shape=[jax.ShapeDtypeStruct((m, nj * n), d) for d in out_dtypes],
        scratch_shapes=[pltpu.VMEM((tm, tn), F32)],
        compiler_params=_params(("parallel", "parallel", "arbitrary")),
    )(a, w4, *extras)


def mm_nt(name, dy, w4, layer, epilogue, extras, out_dtypes):
    m, ntot = dy.shape
    nj, _, k, n = w4.shape
    assert ntot == nj * n
    tm = _pick(m, (512, 256, 128))
    tn = n if n <= 1024 else 1024
    nt = n // tn
    nr = nj * nt
    tko = _pick(k, (512,))
    ne, no = len(extras), len(out_dtypes)

    def body(a_ref, w_ref, *rest):
        ex, outs, acc = rest[:ne], rest[ne:ne + no], rest[-1]
        r = pl.program_id(2)

        @pl.when(r == 0)
        def _():
            acc[...] = jnp.zeros_like(acc)

        acc[...] += lax.dot_general(a_ref[...], w_ref[...], (((1,), (1,)), ((), ())), preferred_element_type=F32)

        @pl.when(r == nr - 1)
        def _():
            res = epilogue(acc[...], *[e[...] for e in ex])
            for o, v in zip(outs, res):
                o[...] = v.astype(o.dtype)

    tile = pl.BlockSpec((tm, tko), lambda i, ko, r: (i, ko))
    return pl.pallas_call(
        body, name=name, grid=(m // tm, k // tko, nr),
        in_specs=[pl.BlockSpec((tm, tn), lambda i, ko, r: (i, r)),
                  pl.BlockSpec((None, None, tko, tn), lambda i, ko, r: (r // nt, layer, ko, r % nt))] + [tile] * ne,
        out_specs=[tile] * no,
        out_shape=[jax.ShapeDtypeStruct((m, k), d) for d in out_dtypes],
        scratch_shapes=[pltpu.VMEM((tm, tko), F32)],
        compiler_params=_params(("parallel", "parallel", "arbitrary")),
    )(dy, w4, *extras)


def mm_tn(name, x, dy, gbuf, layer):
    s, k = x.shape
    nj, _, kw, n = gbuf.shape
    assert kw == k and dy.shape == (s, nj * n)
    ts = _pick(s, (512,))
    ns = s // ts
    tkk = _pick(k, (512, 256))
    tn = n if n <= 512 else 512
    nt = n // tn

    def body(x_ref, dy_ref, g_in, g_out, acc):
        del g_in
        si = pl.program_id(2)

        @pl.when(si == 0)
        def _():
            acc[...] = jnp.zeros_like(acc)

        acc[...] += lax.dot_general(x_ref[...], dy_ref[...], (((0,), (0,)), ((), ())), preferred_element_type=F32)

        @pl.when(si == ns - 1)
        def _():
            g_out[...] = acc[...].astype(g_out.dtype)

    return pl.pallas_call(
        body, name=name, grid=(k // tkk, nj * nt, ns),
        in_specs=[pl.BlockSpec((ts, tkk), lambda ki, j, si: (si, ki)),
                  pl.BlockSpec((ts, tn), lambda ki, j, si: (si, j)),
                  pl.BlockSpec(memory_space=pl.ANY)],
        out_specs=pl.BlockSpec((None, None, tkk, tn), lambda ki, j, si: (j // nt, layer, ki, j % nt)),
        out_shape=jax.ShapeDtypeStruct(gbuf.shape, gbuf.dtype),
        input_output_aliases={2: 0},
        scratch_shapes=[pltpu.VMEM((tkk, tn), F32)],
        compiler_params=_params(("parallel", "parallel", "arbitrary")),
    )(x, dy, gbuf)


def rms_fwd(name, h, g):
    s, d = h.shape
    tr = _pick(s, (ROW_TILE, 256, 128))

    def body(h_ref, g_ref, o_ref):
        x = h_ref[...]
        r = lax.rsqrt(jnp.mean(x * x, axis=-1, keepdims=True) + EPS)
        o_ref[...] = (x * r * g_ref[...]).astype(o_ref.dtype)

    return pl.pallas_call(
        body, name=name, grid=(s // tr,),
        in_specs=[pl.BlockSpec((tr, d), lambda i: (i, 0)), pl.BlockSpec((1, d), lambda i: (0, 0))],
        out_specs=pl.BlockSpec((tr, d), lambda i: (i, 0)),
        out_shape=jax.ShapeDtypeStruct((s, d), BF16),
        compiler_params=_params(("parallel",)),
    )(h, g.reshape(1, d))


def rms_bwd(name, h, g, dn, dres):
    s, d = h.shape
    tr = _pick(s, (ROW_TILE, 256, 128))

    def body(h_ref, g_ref, dn_ref, dres_ref, dx_ref, dxb_ref, dg_ref, cs_ref):
        i = pl.program_id(0)
        x = h_ref[...]
        r = lax.rsqrt(jnp.mean(x * x, axis=-1, keepdims=True) + EPS)
        xh = x * r
        dn_ = dn_ref[...]
        dyg = dn_ * g_ref[...]
        dx = dres_ref[...] + r * (dyg - xh * jnp.mean(dyg * xh, axis=-1, keepdims=True))
        dx_ref[...] = dx
        dxb_ref[...] = dx.astype(BF16)

        @pl.when(i == 0)
        def _():
            dg_ref[...] = jnp.zeros_like(dg_ref)
            cs_ref[...] = jnp.zeros_like(cs_ref)

        dg_ref[...] += jnp.sum(dn_ * xh, axis=0, keepdims=True)
        cs_ref[...] += jnp.sum(dx, axis=0, keepdims=True)

    row = pl.BlockSpec((tr, d), lambda i: (i, 0))
    vec = pl.BlockSpec((1, d), lambda i: (0, 0))
    return pl.pallas_call(
        body, name=name, grid=(s // tr,),
        in_specs=[row, vec, row, row],
        out_specs=[row, row, vec, vec],
        out_shape=[jax.ShapeDtypeStruct((s, d), F32), jax.ShapeDtypeStruct((s, d), BF16),
                   jax.ShapeDtypeStruct((1, d), F32), jax.ShapeDtypeStruct((1, d), F32)],
        compiler_params=_params(("arbitrary",)),
    )(h, g.reshape(1, d), dn, dres)


def loss_head(h, g, target):
    s, d = h.shape
    tr = _pick(s, (ROW_TILE, 256, 128))

    def body(h_ref, g_ref, t_ref, dx_ref, dxb_ref, dg_ref, loss_ref):
        i = pl.program_id(0)
        x = h_ref[...]
        r = lax.rsqrt(jnp.mean(x * x, axis=-1, keepdims=True) + EPS)
        xh = x * r
        gw = g_ref[...]
        err = xh * gw - t_ref[...]
        dn_ = err * (1.0 / d)
        dyg = dn_ * gw
        dx = r * (dyg - xh * jnp.mean(dyg * xh, axis=-1, keepdims=True))
        dx_ref[...] = dx
        dxb_ref[...] = dx.astype(BF16)

        @pl.when(i == 0)
        def _():
            dg_ref[...] = jnp.zeros_like(dg_ref)
            loss_ref[...] = jnp.zeros_like(loss_ref)

        dg_ref[...] += jnp.sum(dn_ * xh, axis=0, keepdims=True)
        per_row = jnp.sum(err * err, axis=-1, keepdims=True) * (0.5 / d)
        loss_ref[...] += jnp.broadcast_to(jnp.sum(per_row, axis=0, keepdims=True), loss_ref.shape)

    row = pl.BlockSpec((tr, d), lambda i: (i, 0))
    vec = pl.BlockSpec((1, d), lambda i: (0, 0))
    return pl.pallas_call(
        body, name="loss_head", grid=(s // tr,),
        in_specs=[row, vec, row],
        out_specs=[row, row, vec, pl.BlockSpec((1, 128), lambda i: (0, 0))],
        out_shape=[jax.ShapeDtypeStruct((s, d), F32), jax.ShapeDtypeStruct((s, d), BF16),
                   jax.ShapeDtypeStruct((1, d), F32), jax.ShapeDtypeStruct((1, 128), F32)],
        compiler_params=_params(("arbitrary",)),
    )(h, g.reshape(1, d), target)


def colsum(name, x):
    s, n = x.shape
    tr = _pick(s, (ROW_TILE, 256, 128))

    def body(x_ref, o_ref):
        @pl.when(pl.program_id(0) == 0)
        def _():
            o_ref[...] = jnp.zeros_like(o_ref)

        o_ref[...] += jnp.sum(x_ref[...].astype(F32), axis=0, keepdims=True)

    return pl.pallas_call(
        body, name=name, grid=(s // tr,),
        in_specs=[pl.BlockSpec((tr, n), lambda i: (i, 0))],
        out_specs=pl.BlockSpec((1, n), lambda i: (0, 0)),
        out_shape=jax.ShapeDtypeStruct((1, n), F32),
        compiler_params=_params(("arbitrary",)),
    )(x)


def _tri_aug(reverse):
    i = np.arange(SB_BK)
    tri = (i[:, None] >= i[None, :]) if reverse else (i[:, None] <= i[None, :])
    return jnp.asarray(np.concatenate([tri, np.ones((SB_BK, SB_BK), bool)], axis=1), BF16)


def _split_dot(x, t):
    hi = x.astype(BF16)
    lo = (x - hi.astype(F32)).astype(BF16)
    return jnp.dot(hi, t, preferred_element_type=F32) + jnp.dot(lo, t, preferred_element_type=F32)


def _log1m_beta(z):
    return jnp.minimum(-z, 0.0) - jnp.log1p(jnp.exp(-jnp.abs(z)))


def sb_fwd(name, q, k, v):
    nh, s, dh = q.shape
    bq = min(SB_BQ, s)
    per_q = bq // SB_BK
    assert s % bq == 0 and s // SB_BK <= 128

    def body(q_ref, k_ref, v_ref, t_ref, o_ref, rtab_ref, acc, run):
        i = pl.program_id(1)
        qb = q_ref[...] * Q_SCALE
        tri = t_ref[...]
        row = i * bq + lax.broadcasted_iota(jnp.int32, (bq, SB_BK), 0)
        lane = lax.broadcasted_iota(jnp.int32, (bq, SB_BK), 1)
        acc[...] = jnp.zeros_like(acc)
        run[...] = jnp.zeros_like(run)
        rtab_ref[...] = jnp.zeros_like(rtab_ref)
        nk = (i + 1) * per_q

        def step(it, carry):
            j = nk - 1 - it
            off = pl.multiple_of(j * SB_BK, SB_BK)
            kb = k_ref[pl.ds(off, SB_BK), :]
            vb = v_ref[pl.ds(off, SB_BK), :]
            z = lax.dot_general(qb, kb, (((1,), (1,)), ((), ())), preferred_element_type=F32)
            causal = (off + lane) < row
            l1m = jnp.where(causal, _log1m_beta(z), 0.0)
            cr = _split_dot(l1m, tri)
            r_prev = run[...]
            w = jnp.where(causal, jnp.exp(z + cr[:, :SB_BK] + r_prev), 0.0)
            acc[...] += jnp.dot(w.astype(BF16), vb, preferred_element_type=F32)
            rtab_ref[...] = jnp.where(lane == j, r_prev, rtab_ref[...])
            run[...] = r_prev + cr[:, SB_BK:]
            return carry

        lax.fori_loop(0, nk, step, 0)
        o_ref[...] = acc[...].astype(o_ref.dtype)

    qspec = pl.BlockSpec((None, bq, dh), lambda h, i: (h, i, 0))
    kspec = pl.BlockSpec((None, s, dh), lambda h, i: (h, 0, 0))
    return pl.pallas_call(
        body, name=name, grid=(nh, s // bq),
        in_specs=[qspec, kspec, kspec, pl.BlockSpec((SB_BK, 2 * SB_BK), lambda h, i: (0, 0))],
        out_specs=[qspec, pl.BlockSpec((None, bq, SB_BK), lambda h, i: (h, i, 0))],
        out_shape=[jax.ShapeDtypeStruct((nh, s, dh), BF16), jax.ShapeDtypeStruct((nh, s, SB_BK), F32)],
        scratch_shapes=[pltpu.VMEM((bq, dh), F32), pltpu.VMEM((bq, SB_BK), F32)],
        compiler_params=_params(("parallel", "arbitrary")),
    )(q, k, v, _tri_aug(True))


def sb_bwd(name, q, k, v, do, rtab):
    nh, s, dh = q.shape
    bq = min(SB_BQ, s)
    per_q = bq // SB_BK

    def body(q_ref, k_ref, v_ref, do_ref, rtab_ref, tr_ref, tf_ref, dq_ref, dk_ref, dv_ref, dq_acc, g_run):
        i = pl.program_id(1)

        @pl.when(i == 0)
        def _():
            dk_ref[...] = jnp.zeros_like(dk_ref)
            dv_ref[...] = jnp.zeros_like(dv_ref)

        qb = q_ref[...] * Q_SCALE
        dob = do_ref[...]
        tri_rev = tr_ref[...]
        tri_fwd = tf_ref[...]
        row = i * bq + lax.broadcasted_iota(jnp.int32, (bq, SB_BK), 0)
        lane = lax.broadcasted_iota(jnp.int32, (bq, SB_BK), 1)
        dq_acc[...] = jnp.zeros_like(dq_acc)
        g_run[...] = jnp.zeros_like(g_run)

        def step(j, carry):
            off = pl.multiple_of(j * SB_BK, SB_BK)
            kb = k_ref[pl.ds(off, SB_BK), :]
            vb = v_ref[pl.ds(off, SB_BK), :]
            z = lax.dot_general(qb, kb, (((1,), (1,)), ((), ())), preferred_element_type=F32)
            causal = (off + lane) < row
            l1m_raw = _log1m_beta(z)
            l1m = jnp.where(causal, l1m_raw, 0.0)
            c_in = _split_dot(l1m, tri_rev)[:, :SB_BK]
            r_j = jnp.sum(jnp.where(lane == j, rtab_ref[...], 0.0), axis=1, keepdims=True)
            w = jnp.where(causal, jnp.exp(z + c_in + r_j), 0.0)
            dw = lax.dot_general(dob, vb, (((1,), (1,)), ((), ())), preferred_element_type=F32)
            g = w * dw
            gg = _split_dot(g, tri_fwd)
            g_prev = g_run[...]
            beta = jnp.exp(z + l1m_raw)
            dz = jnp.where(causal, g - beta * (gg[:, :SB_BK] + g_prev), 0.0).astype(BF16)
            g_run[...] = g_prev + gg[:, SB_BK:]
            dq_acc[...] += jnp.dot(dz, kb, preferred_element_type=F32)
            dk_ref[pl.ds(off, SB_BK), :] += lax.dot_general(dz, qb, (((0,), (0,)), ((), ())),
                                                           preferred_element_type=F32)
            dv_ref[pl.ds(off, SB_BK), :] += lax.dot_general(w.astype(BF16), dob, (((0,), (0,)), ((), ())),
                                                           preferred_element_type=F32)
            return carry

        lax.fori_loop(0, (i + 1) * per_q, step, 0)
        dq_ref[...] = dq_acc[...] * Q_SCALE

    qspec = pl.BlockSpec((None, bq, dh), lambda h, i: (h, i, 0))
    kspec = pl.BlockSpec((None, s, dh), lambda h, i: (h, 0, 0))
    tspec = pl.BlockSpec((SB_BK, 2 * SB_BK), lambda h, i: (0, 0))
    full = jax.ShapeDtypeStruct((nh, s, dh), F32)
    return pl.pallas_call(
        body, name=name, grid=(nh, s // bq),
        in_specs=[qspec, kspec, kspec, qspec, pl.BlockSpec((None, bq, SB_BK), lambda h, i: (h, i, 0)), tspec, tspec],
        out_specs=[qspec, kspec, kspec],
        out_shape=[full, full, full],
        scratch_shapes=[pltpu.VMEM((bq, dh), F32), pltpu.VMEM((bq, SB_BK), F32)],
        compiler_params=_params(("parallel", "arbitrary")),
    )(q, k, v, do, rtab, _tri_aug(True), _tri_aug(False))


def _swa_scores(q_ref, k_ref, bias_ref, sink_ref, i):
    rows = q_ref.shape[0]
    qb = q_ref[...] * Q_SCALE
    off = pl.multiple_of(i * WINDOW, WINDOW)
    kb = k_ref[pl.ds(off, 2 * WINDOW), :]
    sc = lax.dot_general(qb, kb, (((1,), (1,)), ((), ())), preferred_element_type=F32) + bias_ref[...]
    qi = lax.broadcasted_iota(jnp.int32, (rows, 2 * WINDOW), 0) & (WINDOW - 1)
    kj = lax.broadcasted_iota(jnp.int32, (rows, 2 * WINDOW), 1)
    dist = qi + WINDOW - kj
    valid = (dist >= 0) & (dist < WINDOW) & ((kj >= WINDOW) | (i > 0))
    sc = jnp.where(valid, sc, NEG_INF)
    sink = sink_ref[...]
    mx = jnp.maximum(jnp.max(sc, axis=1, keepdims=True), sink)
    p = jnp.exp(sc - mx)
    p_sink = jnp.exp(sink - mx)
    inv = 1.0 / (jnp.sum(p, axis=1, keepdims=True) + p_sink)
    return qb, kb, off, p, p_sink, inv


def swa_fwd(name, qg, kp, vp, bias, sink_col):
    ng, nb, rows, dh = qg.shape
    sp = kp.shape[1]

    def body(q_ref, k_ref, v_ref, bias_ref, sink_ref, o_ref):
        i = pl.program_id(1)
        _, _, off, p, _, inv = _swa_scores(q_ref, k_ref, bias_ref, sink_ref, i)
        vb = v_ref[pl.ds(off, 2 * WINDOW), :]
        o_ref[...] = (jnp.dot(p.astype(BF16), vb, preferred_element_type=F32) * inv).astype(o_ref.dtype)

    qspec = pl.BlockSpec((None, None, rows, dh), lambda g, i: (g, i, 0, 0))
    kspec = pl.BlockSpec((None, sp, dh), lambda g, i: (g, 0, 0))
    return pl.pallas_call(
        body, name=name, grid=(ng, nb),
        in_specs=[qspec, kspec, kspec, pl.BlockSpec((None, rows, 2 * WINDOW), lambda g, i: (g, 0, 0)),
                  pl.BlockSpec((None, rows, 1), lambda g, i: (g, 0, 0))],
        out_specs=qspec,
        out_shape=jax.ShapeDtypeStruct(qg.shape, BF16),
        compiler_params=_params(("parallel", "arbitrary")),
    )(qg, kp, vp, bias, sink_col)


def swa_bwd(name, qg, kp, vp, bias, sink_col, dog, dk_in, dv_in):
    ng, nb, rows, dh = qg.shape
    sp = kp.shape[1]

    def body(q_ref, k_ref, v_ref, bias_ref, sink_ref, do_ref, dki_ref, dvi_ref, dq_ref, dk_ref, dv_ref, db_ref, ds_ref):
        i = pl.program_id(1)

        @pl.when(i == 0)
        def _():
            dk_ref[...] = dki_ref[...]
            dv_ref[...] = dvi_ref[...]
            db_ref[...] = jnp.zeros_like(db_ref)
            ds_ref[...] = jnp.zeros_like(ds_ref)

        qb, kb, off, p, p_sink, inv = _swa_scores(q_ref, k_ref, bias_ref, sink_ref, i)
        p = p * inv
        dob = do_ref[...]
        vb = v_ref[pl.ds(off, 2 * WINDOW), :]
        dp = lax.dot_general(dob, vb, (((1,), (1,)), ((), ())), preferred_element_type=F32)
        delta = jnp.sum(p * dp, axis=1, keepdims=True)
        dsc = p * (dp - delta)
        ds_ref[...] -= p_sink * inv * delta
        db_ref[...] += dsc
        dscb = dsc.astype(BF16)
        dq_ref[...] = (jnp.dot(dscb, kb, preferred_element_type=F32) * Q_SCALE).astype(dq_ref.dtype)
        dk_ref[pl.ds(off, 2 * WINDOW), :] += lax.dot_general(dscb, qb, (((0,), (0,)), ((), ())),
                                                             preferred_element_type=F32)
        dv_ref[pl.ds(off, 2 * WINDOW), :] += lax.dot_general(p.astype(BF16), dob, (((0,), (0,)), ((), ())),
                                                             preferred_element_type=F32)

    qspec = pl.BlockSpec((None, None, rows, dh), lambda g, i: (g, i, 0, 0))
    kspec = pl.BlockSpec((None, sp, dh), lambda g, i: (g, 0, 0))
    bspec = pl.BlockSpec((None, rows, 2 * WINDOW), lambda g, i: (g, 0, 0))
    sspec = pl.BlockSpec((None, rows, 1), lambda g, i: (g, 0, 0))
    return pl.pallas_call(
        body, name=name, grid=(ng, nb),
        in_specs=[qspec, kspec, kspec, bspec, sspec, qspec, kspec, kspec],
        out_specs=[qspec, kspec, kspec, bspec, sspec],
        out_shape=[jax.ShapeDtypeStruct(qg.shape, BF16), jax.ShapeDtypeStruct(kp.shape, F32),
                   jax.ShapeDtypeStruct(kp.shape, F32), jax.ShapeDtypeStruct(bias.shape, F32),
                   jax.ShapeDtypeStruct(sink_col.shape, F32)],
        compiler_params=_params(("parallel", "arbitrary")),
    )(qg, kp, vp, bias, sink_col, dog, dk_in, dv_in)


def _bucket_onehot():
    qi = np.arange(WINDOW)[:, None]
    kj = np.arange(2 * WINDOW)[None, :]
    n = np.maximum(qi + WINDOW - kj, 0)
    max_exact = N_BUCKETS // 2
    nf = np.maximum(n, 1).astype(np.float64)
    val = np.log(nf / max_exact) / math.log(WINDOW / max_exact) * (N_BUCKETS - max_exact)
    assert np.all(np.abs(val - np.round(val))[(n > max_exact) & (n < WINDOW)] > 1e-3)
    large = np.minimum(max_exact + val.astype(np.int64), N_BUCKETS - 1)
    bucket = np.where(n < max_exact, n, large).reshape(-1)
    onehot = np.zeros((128, bucket.size), np.float32)
    onehot[bucket, np.arange(bucket.size)] = 1.0
    return onehot


def _split3(x):
    a = x.astype(BF16)
    r = x - a.astype(F32)
    b = r.astype(BF16)
    c = (r - b.astype(F32)).astype(BF16)
    return a, b, c


def bias_table(rel_bias):
    nh = rel_bias.shape[1]
    oh = jnp.asarray(_bucket_onehot(), BF16)
    n = oh.shape[1]
    tn = 4096
    rb = jnp.zeros((nh, 128), F32).at[:, :N_BUCKETS].set(rel_bias.T)

    def body(rb_ref, oh_ref, o_ref):
        o_ref[...] = sum(jnp.dot(t, oh_ref[...], preferred_element_type=F32) for t in _split3(rb_ref[...]))

    return pl.pallas_call(
        body, name="bias_table", grid=(n // tn,),
        in_specs=[pl.BlockSpec((nh, 128), lambda i: (0, 0)), pl.BlockSpec((128, tn), lambda i: (0, i))],
        out_specs=pl.BlockSpec((nh, tn), lambda i: (0, i)),
        out_shape=jax.ShapeDtypeStruct((nh, n), F32),
        compiler_params=_params(("parallel",)),
    )(rb, oh)


def bias_table_grad(db0, db1):
    nh, n = db0.shape
    oh = jnp.asarray(_bucket_onehot(), BF16)
    tn = 4096

    def body(a_ref, b_ref, oh_ref, o_ref):
        @pl.when(pl.program_id(0) == 0)
        def _():
            o_ref[...] = jnp.zeros_like(o_ref)

        o_ref[...] += sum(lax.dot_general(t, oh_ref[...], (((1,), (1,)), ((), ())), preferred_element_type=F32)
                          for t in _split3(a_ref[...] + b_ref[...]))

    blk = pl.BlockSpec((nh, tn), lambda i: (0, i))
    return pl.pallas_call(
        body, name="bias_table_grad", grid=(n // tn,),
        in_specs=[blk, blk, pl.BlockSpec((128, tn), lambda i: (0, i))],
        out_specs=pl.BlockSpec((nh, 128), lambda i: (0, 0)),
        out_shape=jax.ShapeDtypeStruct((nh, 128), F32),
        compiler_params=_params(("arbitrary",)),
    )(db0, db1, oh)


def _owner_view(ref, name, d):
    if name in COL_SHARDED or name == 'a_norm':
        return ref.at[d]
    return ref.at[:, d]


def _place():
    return lax.axis_index("x"), lax.axis_index("y"), lax.axis_index("c")


def all_gather_weights(names, shards, full_shapes):
    n = len(names)

    def body(*refs):
        ins, outs = refs[:n], refs[n:2 * n]
        send_sems, recv_sems, local_sems = refs[2 * n:]
        x, y, c = _place()
        me, sibling = (x, y, c), (x, y, 1 - c)
        chips = [(1 - x, y), (x, 1 - y), (1 - x, 1 - y)]

        def dev(p):
            return 4 * p[0] + 2 * p[1] + p[2]

        def copy(t, k, block, to, src=None):
            dst = _owner_view(outs[t], names[t], dev(block))
            return pltpu.make_async_remote_copy(
                src_ref=dst if src is None else src, dst_ref=dst,
                send_sem=send_sems.at[t, k], recv_sem=recv_sems.at[t, k], device_id=to, device_id_type=MESH)

        mine = [pltpu.make_async_copy(ins[t], _owner_view(outs[t], names[t], dev(me)), local_sems.at[t])
                for t in range(n)]
        for cp in mine:
            cp.start()
        first = []
        for t in range(n):
            first.append(copy(t, 0, me, sibling, src=ins[t]))
            first += [copy(t, 1 + j, me, (*chip, c), src=ins[t]) for j, chip in enumerate(chips)]
        for cp in first:
            cp.start()
        passed = []
        for j, chip in enumerate(chips):
            for t in range(n):
                copy(t, 1 + j, (*chip, c), me).wait_recv()
                fwd = copy(t, 4 + j, (*chip, c), sibling)
                fwd.start()
                passed.append(fwd)
        for t in range(n):
            copy(t, 0, sibling, me).wait_recv()
            for j, chip in enumerate(chips):
                copy(t, 4 + j, (*chip, 1 - c), me).wait_recv()
        for cp in first + passed:
            cp.wait_send()
        for cp in mine:
            cp.wait()

    hbm = pl.BlockSpec(memory_space=pl.ANY)
    return pl.pallas_call(
        body, name="all_gather_weights",
        in_specs=[hbm] * n, out_specs=[hbm] * n,
        out_shape=[jax.ShapeDtypeStruct(full_shapes[t], shards[t].dtype) for t in range(n)],
        scratch_shapes=[pltpu.SemaphoreType.DMA((n, 7)), pltpu.SemaphoreType.DMA((n, 7)),
                        pltpu.SemaphoreType.DMA((n,))],
    )(*shards)


def sibling_exchange(names, grads, part_shapes):
    n = len(names)

    def body(*refs):
        ins, outs = refs[:n], refs[n:2 * n]
        send_sems, recv_sems = refs[2 * n:]
        x, y, c = _place()
        sibling = (x, y, 1 - c)
        copies = []
        for t in range(n):
            for q in range(4):
                copies.append(pltpu.make_async_remote_copy(
                    src_ref=_owner_view(ins[t], names[t], 2 * q + 1 - c), dst_ref=outs[t].at[q],
                    send_sem=send_sems.at[t, q], recv_sem=recv_sems.at[t, q], device_id=sibling, device_id_type=MESH))
        for cp in copies:
            cp.start()
        for cp in copies:
            cp.wait()

    hbm = pl.BlockSpec(memory_space=pl.ANY)
    return pl.pallas_call(
        body, name="rs_sibling_exchange",
        in_specs=[hbm] * n, out_specs=[hbm] * n,
        out_shape=[jax.ShapeDtypeStruct((4,) + part_shapes[t], BF16) for t in range(n)],
        scratch_shapes=[pltpu.SemaphoreType.DMA((n, 4)), pltpu.SemaphoreType.DMA((n, 4))],
    )(*grads)


def chip_exchange(names, parts, part_shapes):
    n = len(names)

    def body(*refs):
        ins, outs = refs[:n], refs[n:2 * n]
        send_sems, recv_sems = refs[2 * n:]
        x, y, c = _place()
        chips = [(1 - x, y), (x, 1 - y), (1 - x, 1 - y)]
        copies = []
        for t in range(n):
            for k, chip in enumerate(chips):
                copies.append(pltpu.make_async_remote_copy(
                    src_ref=ins[t].at[2 * chip[0] + chip[1]], dst_ref=outs[t].at[k],
                    send_sem=send_sems.at[t, k], recv_sem=recv_sems.at[t, k], device_id=(*chip, c),
                    device_id_type=MESH))
        for cp in copies:
            cp.start()
        for cp in copies:
            cp.wait()

    hbm = pl.BlockSpec(memory_space=pl.ANY)
    return pl.pallas_call(
        body, name="rs_chip_exchange",
        in_specs=[hbm] * n, out_specs=[hbm] * n,
        out_shape=[jax.ShapeDtypeStruct((3,) + part_shapes[t], BF16) for t in range(n)],
        scratch_shapes=[pltpu.SemaphoreType.DMA((n, 3)), pltpu.SemaphoreType.DMA((n, 3))],
    )(*parts)


def all_gather_rows(x):
    r, w = x.shape

    def body(x_ref, out_ref, send_sems, recv_sems, local_sem):
        px, py, pc = _place()
        me = 4 * px + 2 * py + pc
        mine = pltpu.make_async_copy(x_ref, out_ref.at[me], local_sem)
        mine.start()
        copies = []
        for k in range(1, N_DEV):
            peer = (px ^ (k >> 2), py ^ ((k >> 1) & 1), pc ^ (k & 1))
            copies.append(pltpu.make_async_remote_copy(
                src_ref=x_ref, dst_ref=out_ref.at[me], send_sem=send_sems.at[k - 1], recv_sem=recv_sems.at[k - 1],
                device_id=peer, device_id_type=MESH))
        for cp in copies:
            cp.start()
        for k in range(1, N_DEV):
            peer_idx = me ^ k
            pltpu.make_async_remote_copy(
                src_ref=x_ref, dst_ref=out_ref.at[peer_idx], send_sem=send_sems.at[k - 1],
                recv_sem=recv_sems.at[k - 1], device_id=(px, py, pc), device_id_type=MESH).wait_recv()
        for cp in copies:
            cp.wait_send()
        mine.wait()

    vmem = pl.BlockSpec(memory_space=pltpu.VMEM)
    return pl.pallas_call(
        body, name="all_gather_small_grads",
        in_specs=[vmem], out_specs=vmem,
        out_shape=jax.ShapeDtypeStruct((N_DEV, r, w), x.dtype),
        scratch_shapes=[pltpu.SemaphoreType.DMA((N_DEV - 1,)), pltpu.SemaphoreType.DMA((N_DEV - 1,)),
                        pltpu.SemaphoreType.DMA],
    )(x)


def _adamw(w, g, m, v):
    m = ADAM_B1 * m + (1.0 - ADAM_B1) * g
    v = ADAM_B2 * v + (1.0 - ADAM_B2) * (g * g)
    m_hat = m / (1.0 - ADAM_B1 ** ADAM_STEP)
    v_hat = v / (1.0 - ADAM_B2 ** ADAM_STEP)
    return -ADAM_LR * (m_hat / (jnp.sqrt(v_hat) + ADAM_EPS) + ADAM_WD * w), m, v


def _as_rows(a, lead):
    return a.reshape(a.shape[:lead] + (-1, a.shape[-1]))


def sibling_sum(name, col, grads, recv, core):
    recv2 = _as_rows(recv, 1)
    _, rows, cols = recv2.shape
    tr = _pick(rows, (512, 256, 128))
    if col:
        g3 = grads.reshape(4, 2, rows, cols)
        gspec = pl.BlockSpec((None, None, tr, cols), lambda q, i, c_ref: (q, c_ref[0], i, 0))
        grid = (4, rows // tr)
        rspec = pl.BlockSpec((None, tr, cols), lambda q, i, c_ref: (q, i, 0))
    else:
        nl, _, r_loc, _ = grads.shape
        tr = _pick(r_loc, (512, 256, 128))
        g3 = grads.reshape(nl, 4, 2, r_loc, cols)
        recv2 = recv.reshape(4, nl, r_loc, cols)
        gspec = pl.BlockSpec((None, None, None, tr, cols), lambda q, l, i, c_ref: (l, q, c_ref[0], i, 0))
        rspec = pl.BlockSpec((None, None, tr, cols), lambda q, l, i, c_ref: (q, l, i, 0))
        grid = (4, nl, r_loc // tr)

    def body(c_ref, g_ref, r_ref, o_ref):
        del c_ref
        o_ref[...] = (g_ref[...].astype(F32) + r_ref[...].astype(F32)).astype(BF16)

    out = pl.pallas_call(
        body, name=name,
        grid_spec=pltpu.PrefetchScalarGridSpec(num_scalar_prefetch=1, grid=grid, in_specs=[gspec, rspec],
                                               out_specs=rspec),
        out_shape=jax.ShapeDtypeStruct(recv2.shape, BF16),
        compiler_params=_params(("parallel",) * len(grid)),
    )(core.reshape(1), g3, recv2)
    return out.reshape(recv.shape)


def reduce_adamw(name, parts, recv, chip, w, m, v):
    shape = w.shape
    w2, m2, v2 = (_as_rows(a, 0) for a in (w, m, v))
    rows, cols = w2.shape
    p3 = parts.reshape(4, rows, cols)
    r3 = recv.reshape(3, rows, cols)
    tr = _pick(rows, (256, 128))

    def body(q_ref, p_ref, r_ref, w_ref, m_ref, v_ref, g_out, d_out, m_out, v_out):
        del q_ref
        g = ((p_ref[...].astype(F32) + r_ref[0].astype(F32)) + r_ref[1].astype(F32)) + r_ref[2].astype(F32)
        d, mn, vn = _adamw(w_ref[...], g, m_ref[...], v_ref[...])
        g_out[...] = g
        d_out[...] = d
        m_out[...] = mn
        v_out[...] = vn

    blk = pl.BlockSpec((tr, cols), lambda i, q_ref: (i, 0))
    outs = pl.pallas_call(
        body, name=name,
        grid_spec=pltpu.PrefetchScalarGridSpec(
            num_scalar_prefetch=1, grid=(rows // tr,),
            in_specs=[pl.BlockSpec((None, tr, cols), lambda i, q_ref: (q_ref[0], i, 0)),
                      pl.BlockSpec((3, tr, cols), lambda i, q_ref: (0, i, 0)), blk, blk, blk],
            out_specs=[blk] * 4),
        out_shape=[jax.ShapeDtypeStruct((rows, cols), F32)] * 4,
        compiler_params=_params(("parallel",)),
    )(chip.reshape(1), p3, r3, w2, m2, v2)
    return [o.reshape(shape) for o in outs]


def small_adamw(name, gathered, w, m, v):
    _, r, c = gathered.shape

    def body(ga_ref, w_ref, m_ref, v_ref, g_out, d_out, m_out, v_out):
        g = ga_ref[0]
        for d in range(1, N_DEV):
            g = g + ga_ref[d]
        dl, mn, vn = _adamw(w_ref[...], g, m_ref[...], v_ref[...])
        g_out[...] = g
        d_out[...] = dl
        m_out[...] = mn
        v_out[...] = vn

    return pl.pallas_call(
        body, name=name,
        out_shape=[jax.ShapeDtypeStruct((r, c), F32)] * 4,
        compiler_params=_params(),
    )(gathered, w, m, v)


def _heads(t, nh):
    s = t.shape[0]
    return t.reshape(s, nh, HEAD_DIM).transpose(1, 0, 2)


def _unheads(t):
    nh, s, dh = t.shape
    return t.transpose(1, 0, 2).reshape(s, nh * dh)


def _group_q(t, nb):
    s = t.shape[0]
    return t.reshape(nb, WINDOW, 2, 8, HEAD_DIM).transpose(2, 0, 3, 1, 4).reshape(2, nb, 8 * WINDOW, HEAD_DIM)


def _ungroup_q(t):
    ng, nb, _, dh = t.shape
    return t.reshape(ng, nb, 8, WINDOW, dh).transpose(1, 3, 0, 2, 4).reshape(nb * WINDOW, ng * 8 * dh)


def _front_pad(t):
    return jnp.pad(t, ((0, 0), (WINDOW, 0), (0, 0)))


def _add(acc, *ex):
    return (acc + ex[0],)


def local_step(x, target, wf, small, gbufs):
    s, d = x.shape
    nb = s // WINDOW
    n_a, n_b = small['a_norm'].shape[0], small['b_norm'].shape[0]
    sg = {}
    gb = dict(gbufs)

    bias_flat = bias_table(small['rel_bias'])
    bias_g = bias_flat.reshape(2, 8 * WINDOW, 2 * WINDOW)
    sink_cols = [jnp.repeat(small['b_sinks'][j], WINDOW).reshape(2, 8 * WINDOW, 1) for j in range(n_b)]

    def mlp_fwd(h, layer):
        n2 = rms_fwd(f"mlp_norm_fwd{layer}", h, small['mlp_norm'][layer])
        u, a = mm_nn(f"mlp_up_fwd{layer}", n2, wf['mlp_up'], layer,
                     lambda acc: (acc, jnp.square(jnp.maximum(acc, 0.0))), (), (BF16, BF16))
        (h2,) = mm_nn(f"mlp_down_fwd{layer}", a, wf['mlp_down'], layer, _add, (h,), (F32,))
        return h2, (n2, u, a)

    h = x
    saved = []
    for l in range(n_a):
        n1 = rms_fwd(f"a_norm_fwd{l}", h, small['a_norm'][l])
        (qkv,) = mm_nn(f"a_qkv_fwd{l}", n1, wf['a_wqkv'], l, lambda acc: (acc,), (), (BF16,))
        q, k, v = (_heads(qkv[:, i * d:(i + 1) * d], d // HEAD_DIM) for i in range(3))
        o_h, rtab = sb_fwd(f"sb_fwd{l}", q, k, v)
        o = _unheads(o_h)
        (h_mid,) = mm_nn(f"a_wo_fwd{l}", o, wf['a_wo'], l, _add, (h,), (F32,))
        h_out, mlp_saved = mlp_fwd(h_mid, l)
        saved.append((h, n1, q, k, v, o, rtab, h_mid, mlp_saved))
        h = h_out
    h_kv = h
    nkv = rms_fwd("kv_norm_fwd", h, small['kv_norm'])
    (kv,) = mm_nn("kv_fwd", nkv, wf['w_kv'], 0, lambda acc, b: (acc + b,), (small['b_kv'].reshape(1, -1),), (BF16,))
    kvw = kv.shape[1] // 2
    kp = _front_pad(_heads(kv[:, :kvw], 2))
    vp = _front_pad(_heads(kv[:, kvw:], 2))
    for j in range(n_b):
        layer = n_a + j
        n1 = rms_fwd(f"b_norm_fwd{j}", h, small['b_norm'][j])
        (qb,) = mm_nn(f"b_q_fwd{j}", n1, wf['b_wq'], j, lambda acc, b: (acc + b,),
                      (small['b_bq'][j].reshape(1, -1),), (BF16,))
        qg = _group_q(qb, nb)
        og = swa_fwd(f"swa_fwd{j}", qg, kp, vp, bias_g, sink_cols[j])
        o = _ungroup_q(og)
        (h_mid,) = mm_nn(f"b_wo_fwd{j}", o, wf['b_wo'], j, lambda acc, hh, b: (acc + hh + b,),
                         (h, small['b_bo'][j].reshape(1, -1)), (F32,))
        h_out, mlp_saved = mlp_fwd(h_mid, layer)
        saved.append((h, n1, qg, o, h_mid, mlp_saved))
        h = h_out

    dh, dhb, dg_final, loss_b = loss_head(h, small['final_norm'], target)
    sg['final_norm'] = dg_final[0]
    sg['mlp_norm'] = [None] * (n_a + n_b)
    cs_mid = None

    def mlp_bwd(dh, dhb, h_mid, mlp_saved, layer):
        n2, u, a = mlp_saved
        (du,) = mm_nt(f"mlp_down_dx{layer}", dhb, wf['mlp_down'], layer,
                      lambda acc, uu: (acc * (2.0 * jnp.maximum(uu.astype(F32), 0.0)),), (u,), (BF16,))
        gb['mlp_down'] = mm_tn(f"mlp_down_dw{layer}", a, dhb, gb['mlp_down'], layer)
        (dn2,) = mm_nt(f"mlp_up_dx{layer}", du, wf['mlp_up'], layer, lambda acc: (acc,), (), (F32,))
        gb['mlp_up'] = mm_tn(f"mlp_up_dw{layer}", n2, du, gb['mlp_up'], layer)
        dh2, dh2b, dg, cs = rms_bwd(f"mlp_norm_bwd{layer}", h_mid, small['mlp_norm'][layer], dn2, dh)
        sg['mlp_norm'][layer] = dg[0]
        return dh2, dh2b, cs

    dkp = jnp.zeros(kp.shape, F32)
    dvp = jnp.zeros(vp.shape, F32)
    sg['b_norm'], sg['b_bq'], sg['b_bo'], sg['b_sinks'] = [None] * n_b, [None] * n_b, [None] * n_b, [None] * n_b
    dbias = [None] * n_b
    for j in reversed(range(n_b)):
        layer = n_a + j
        h_in, n1, qg, o, h_mid, mlp_saved = saved[layer]
        dh, dhb, cs = mlp_bwd(dh, dhb, h_mid, mlp_saved, layer)
        sg['b_bo'][j] = cs[0]
        (do,) = mm_nt(f"b_wo_dx{j}", dhb, wf['b_wo'], j, lambda acc: (acc,), (), (BF16,))
        gb['b_wo'] = mm_tn(f"b_wo_dw{j}", o, dhb, gb['b_wo'], j)
        dqg, dkp, dvp, dbias[j], dsink = swa_bwd(f"swa_bwd{j}", qg, kp, vp, bias_g, sink_cols[j],
                                                 _group_q(do, nb), dkp, dvp)
        sg['b_sinks'][j] = colsum(f"sink_grad{j}", dsink.reshape(16, WINDOW).T)[0]
        dq = _ungroup_q(dqg)
        sg['b_bq'][j] = colsum(f"b_bq_grad{j}", dq)[0]
        (dn1,) = mm_nt(f"b_q_dx{j}", dq, wf['b_wq'], j, lambda acc: (acc,), (), (F32,))
        gb['b_wq'] = mm_tn(f"b_q_dw{j}", n1, dq, gb['b_wq'], j)
        dh, dhb, dg, _ = rms_bwd(f"b_norm_bwd{j}", h_in, small['b_norm'][j], dn1, dh)
        sg['b_norm'][j] = dg[0]
    sg['rel_bias'] = bias_table_grad(dbias[0].reshape(bias_flat.shape),
                                     dbias[1].reshape(bias_flat.shape))[:, :N_BUCKETS].T

    dkv = jnp.concatenate([_unheads(dkp[:, WINDOW:]), _unheads(dvp[:, WINDOW:])], axis=1)
    sg['b_kv'] = colsum("b_kv_grad", dkv)[0]
    dkvb = dkv.astype(BF16)
    (dnkv,) = mm_nt("kv_dx", dkvb, wf['w_kv'], 0, lambda acc: (acc,), (), (F32,))
    gb['w_kv'] = mm_tn("kv_dw", nkv, dkvb, gb['w_kv'], 0)
    dh, dhb, dg, _ = rms_bwd("kv_norm_bwd", h_kv, small['kv_norm'], dnkv, dh)
    sg['kv_norm'] = dg[0]

    sg['a_norm'] = [None] * n_a
    for l in reversed(range(n_a)):
        h_in, n1, q, k, v, o, rtab, h_mid, mlp_saved = saved[l]
        dh, dhb, _ = mlp_bwd(dh, dhb, h_mid, mlp_saved, l)
        (do,) = mm_nt(f"a_wo_dx{l}", dhb, wf['a_wo'], l, lambda acc: (acc,), (), (BF16,))
        gb['a_wo'] = mm_tn(f"a_wo_dw{l}", o, dhb, gb['a_wo'], l)
        dq, dk, dv = sb_bwd(f"sb_bwd{l}", q, k, v, _heads(do, d // HEAD_DIM), rtab)
        dqkv = jnp.concatenate([_unheads(dq), _unheads(dk), _unheads(dv)], axis=1).astype(BF16)
        (dn1,) = mm_nt(f"a_qkv_dx{l}", dqkv, wf['a_wqkv'], l, lambda acc: (acc,), (), (F32,))
        gb['a_wqkv'] = mm_tn(f"a_qkv_dw{l}", n1, dqkv, gb['a_wqkv'], l)
        dh, dhb, dg, _ = rms_bwd(f"a_norm_bwd{l}", h_in, small['a_norm'][l], dn1, dh)
        sg['a_norm'][l] = dg[0]

    small_grads = {
        'a_norm': jnp.stack(sg['a_norm']), 'kv_norm': sg['kv_norm'], 'b_kv': sg['b_kv'],
        'b_norm': jnp.stack(sg['b_norm']), 'b_bq': jnp.stack(sg['b_bq']), 'b_sinks': jnp.stack(sg['b_sinks']),
        'b_bo': jnp.stack(sg['b_bo']), 'rel_bias': sg['rel_bias'], 'mlp_norm': jnp.stack(sg['mlp_norm']),
        'final_norm': sg['final_norm'],
    }
    return loss_b, dh, gb, small_grads


def _full_shape(name, shard_shape):
    if name in COL_SHARDED:
        return (N_DEV,) + shard_shape
    nl, r, n = shard_shape
    return (nl, N_DEV, r, n)


def _as_w4(name, full):
    if name in COL_SHARDED:
        return full
    nl, nd, r, n = full.shape
    return full.reshape(1, nl, nd * r, n)


def _pack_small(vals):
    flat = jnp.concatenate([vals[n].reshape(-1).astype(F32) for n in SMALL] + [vals['loss'].reshape(-1)])
    rows = -(-flat.shape[0] // 1024) * 8
    return jnp.pad(flat, (0, rows * 128 - flat.shape[0])).reshape(rows, 128)


def _unpack_small(packed, shapes):
    flat = packed.reshape(-1)
    out, off = {}, 0
    for n in SMALL + ['loss']:
        size = int(np.prod(shapes[n]))
        out[n] = flat[off:off + size].reshape(shapes[n])
        off += size
    return out


def kernel(x, a_norm, a_wqkv, a_wo, kv_norm, w_kv, b_kv, b_norm, b_wq, b_bq, b_sinks, b_wo, b_bo, rel_bias, mlp_norm, mlp_up, mlp_down, final_norm, loss_target, m_a_norm, m_a_wqkv, m_a_wo, m_kv_norm, m_w_kv, m_b_kv, m_b_norm, m_b_wq, m_b_bq, m_b_sinks, m_b_wo, m_b_bo, m_rel_bias, m_mlp_norm, m_mlp_up, m_mlp_down, m_final_norm, v_a_norm, v_a_wqkv, v_a_wo, v_kv_norm, v_w_kv, v_b_kv, v_b_norm, v_b_wq, v_b_bq, v_b_sinks, v_b_wo, v_b_bo, v_rel_bias, v_mlp_norm, v_mlp_up, v_mlp_down, v_final_norm):
    w = dict(a_norm=a_norm, a_wqkv=a_wqkv, a_wo=a_wo, kv_norm=kv_norm, w_kv=w_kv, b_kv=b_kv, b_norm=b_norm,
             b_wq=b_wq, b_bq=b_bq, b_sinks=b_sinks, b_wo=b_wo, b_bo=b_bo, rel_bias=rel_bias, mlp_norm=mlp_norm,
             mlp_up=mlp_up, mlp_down=mlp_down, final_norm=final_norm)
    m = dict(a_norm=m_a_norm, a_wqkv=m_a_wqkv, a_wo=m_a_wo, kv_norm=m_kv_norm, w_kv=m_w_kv, b_kv=m_b_kv,
             b_norm=m_b_norm, b_wq=m_b_wq, b_bq=m_b_bq, b_sinks=m_b_sinks, b_wo=m_b_wo, b_bo=m_b_bo,
             rel_bias=m_rel_bias, mlp_norm=m_mlp_norm, mlp_up=m_mlp_up, mlp_down=m_mlp_down, final_norm=m_final_norm)
    v = dict(a_norm=v_a_norm, a_wqkv=v_a_wqkv, a_wo=v_a_wo, kv_norm=v_kv_norm, w_kv=v_w_kv, b_kv=v_b_kv,
             b_norm=v_b_norm, b_wq=v_b_wq, b_bq=v_b_bq, b_sinks=v_b_sinks, b_wo=v_b_wo, b_bo=v_b_bo,
             rel_bias=v_rel_bias, mlp_norm=v_mlp_norm, mlp_up=v_mlp_up, mlp_down=v_mlp_down, final_norm=v_final_norm)
    px, py, pc = _place()
    me = 4 * px + 2 * py + pc
    chip = (2 * px + py).astype(jnp.int32)
    core = pc.astype(jnp.int32)

    shards = {n: (w[n][None] if w[n].ndim == 2 else w[n]).astype(BF16) for n in BIG}
    an_pad = jnp.zeros((8, 128), F32).at[:a_norm.shape[0]].set(a_norm)
    names = BIG + ['a_norm']
    full = all_gather_weights(names, [shards[n] for n in BIG] + [an_pad],
                              [_full_shape(n, shards[n].shape) for n in BIG] + [(N_DEV, 8, 128)])
    full = dict(zip(names, full))
    wf = {n: _as_w4(n, full[n]) for n in BIG}
    n_a = a_norm.shape[0]
    small = {n: w[n] for n in SMALL}
    small['a_norm'] = full['a_norm'][:, :n_a].transpose(1, 0, 2).reshape(n_a, -1)

    gbufs = {n: jnp.zeros(wf[n].shape, BF16) for n in BIG}
    loss_b, grad_x, gb, sgrads = local_step(x[0], loss_target[0], wf, small, gbufs)

    gfull = [gb[n].reshape(full[n].shape) for n in BIG]
    part_shapes = [shards[n].shape for n in BIG]
    recv1 = sibling_exchange(BIG, gfull, part_shapes)
    parts = [sibling_sum(f"rs_sibling_sum_{n}", n in COL_SHARDED, g, r, core) for n, g, r in zip(BIG, gfull, recv1)]
    recv2 = chip_exchange(BIG, parts, part_shapes)
    out = {}
    for n, p, r in zip(BIG, parts, recv2):
        out[n] = reduce_adamw(f"adamw_{n}", p, r, chip, w[n], m[n], v[n])

    sgrads['loss'] = loss_b[0, :1]
    gathered = all_gather_rows(_pack_small(sgrads))
    shapes = {n: w[n].shape for n in SMALL}
    shapes['a_norm'] = (n_a, a_norm.shape[1] * N_DEV)
    shapes['loss'] = (1,)
    zeros1 = jnp.zeros((1,), F32)

    def packed(src):
        vals = {n: src[n] for n in SMALL}
        vals['a_norm'] = jnp.zeros(shapes['a_norm'], F32)
        vals['loss'] = zeros1
        return _pack_small(vals)

    sm = small_adamw("adamw_small", gathered, packed(w), packed(m), packed(v))
    sm = [_unpack_small(t, shapes) for t in sm]
    g_an = lax.dynamic_slice_in_dim(sm[0]['a_norm'], me * a_norm.shape[1], a_norm.shape[1], axis=1)
    pad = lambda t: jnp.zeros((8, 128), F32).at[:n_a].set(t)
    gathered_an = jnp.zeros((N_DEV, 8, 128), F32).at[0].set(pad(g_an))
    an = small_adamw("adamw_a_norm", gathered_an, pad(a_norm), pad(m_a_norm), pad(v_a_norm))
    for i in range(4):
        sm[i]['a_norm'] = an[i][:n_a]
    for n in BIG:
        for i in range(4):
            sm[i][n] = out[n][i]
    loss = sm[0]['loss'][0]
    return (loss, grad_x[None], *[sm[0][n] for n in WEIGHTS], *[sm[1][n] for n in WEIGHTS],
            *[sm[2][n] for n in WEIGHTS], *[sm[3][n] for n in WEIGHTS])
```

```python
import functools
import math

import numpy as np
import jax
import jax.numpy as jnp
from jax import lax
from jax.experimental import pallas as pl
from jax.experimental.pallas import tpu as pltpu

F32 = jnp.float32
BF16 = jnp.bfloat16
MESH = pl.DeviceIdType.MESH

N_DEV = 8
HEAD_DIM = 64
WINDOW = 128
N_BUCKETS = 32
EPS = 1e-5
NEG_INF = -1e30
Q_SCALE = 1.0 / math.sqrt(HEAD_DIM)

ADAM_LR, ADAM_B1, ADAM_B2, ADAM_EPS, ADAM_WD, ADAM_STEP = 0.001, 0.9, 0.999, 1e-08, 0.01, 10

SB_BQ = 512
SB_BK = 128
ROW_TILE = 512
VMEM_LIMIT = 56 * 1024 * 1024

WEIGHTS = ['a_norm', 'a_wqkv', 'a_wo', 'kv_norm', 'w_kv', 'b_kv', 'b_norm', 'b_wq', 'b_bq', 'b_sinks', 'b_wo',
           'b_bo', 'rel_bias', 'mlp_norm', 'mlp_up', 'mlp_down', 'final_norm']
BIG = ['a_wqkv', 'a_wo', 'w_kv', 'b_wq', 'b_wo', 'mlp_up', 'mlp_down']
COL_SHARDED = ('a_wqkv', 'mlp_up')
SMALL = ['a_norm', 'kv_norm', 'b_kv', 'b_norm', 'b_bq', 'b_sinks', 'b_bo', 'rel_bias', 'mlp_norm', 'final_norm']


def _params(sem=None):
    return pltpu.CompilerParams(dimension_semantics=sem, vmem_limit_bytes=VMEM_LIMIT)


def _pick(n, cands):
    for c in cands:
        if n % c == 0:
            return c
    raise ValueError(n)


def mm_nn(name, a, w4, layer, epilogue, extras, out_dtypes):
    m, k = a.shape
    nj, _, kw, n = w4.shape
    assert kw == k
    tm = _pick(m, (512, 256, 128))
    tn = n if n <= 512 else 512
    nt = n // tn
    tk = _pick(k, (1024, 512))
    nk = k // tk
    ne, no = len(extras), len(out_dtypes)

    def body(a_ref, w_ref, *rest):
        ex, outs, acc = rest[:ne], rest[ne:ne + no], rest[-1]
        kk = pl.program_id(2)

        @pl.when(kk == 0)
        def _():
            acc[...] = jnp.zeros_like(acc)

        acc[...] += jnp.dot(a_ref[...], w_ref[...], preferred_element_type=F32)

        @pl.when(kk == nk - 1)
        def _():
            res = epilogue(acc[...], *[e[...] for e in ex])
            for o, r in zip(outs, res):
                o[...] = r.astype(o.dtype)

    tile = pl.BlockSpec((tm, tn), lambda i, j, kk: (i, j))
    ex_specs = [tile if e.shape[0] == m else pl.BlockSpec((1, tn), lambda i, j, kk: (0, j)) for e in extras]
    return pl.pallas_call(
        body, name=name, grid=(m // tm, nj * nt, nk),
        in_specs=[pl.BlockSpec((tm, tk), lambda i, j, kk: (i, kk)),
                  pl.BlockSpec((None, None, tk, tn), lambda i, j, kk: (j // nt, layer, kk, j % nt))] + ex_specs,
        out_specs=[tile] * no,
        out_shape=[jax.ShapeDtypeStruct((m, nj * n), d) for d in out_dtypes],
        scratch_shapes=[pltpu.VMEM((tm, tn), F32)],
        compiler_params=_params(("parallel", "parallel", "arbitrary")),
    )(a, w4, *extras)


def mm_nt(name, dy, w4, layer, epilogue, extras, out_dtypes):
    m, ntot = dy.shape
    nj, _, k, n = w4.shape
    assert ntot == nj * n
    tm = _pick(m, (512, 256, 128))
    tn = n if n <= 1024 else 1024
    nt = n // tn
    nr = nj * nt
    tko = _pick(k, (512,))
    ne, no = len(extras), len(out_dtypes)

    def body(a_ref, w_ref, *rest):
        ex, outs, acc = rest[:ne], rest[ne:ne + no], rest[-1]
        r = pl.program_id(2)

        @pl.when(r == 0)
        def _():
            acc[...] = jnp.zeros_like(acc)

        acc[...] += lax.dot_general(a_ref[...], w_ref[...], (((1,), (1,)), ((), ())), preferred_element_type=F32)

        @pl.when(r == nr - 1)
        def _():
            res = epilogue(acc[...], *[e[...] for e in ex])
            for o, v in zip(outs, res):
                o[...] = v.astype(o.dtype)

    tile = pl.BlockSpec((tm, tko), lambda i, ko, r: (i, ko))
    return pl.pallas_call(
        body, name=name, grid=(m // tm, k // tko, nr),
        in_specs=[pl.BlockSpec((tm, tn), lambda i, ko, r: (i, r)),
                  pl.BlockSpec((None, None, tko, tn), lambda i, ko, r: (r // nt, layer, ko, r % nt))] + [tile] * ne,
        out_specs=[tile] * no,
        out_shape=[jax.ShapeDtypeStruct((m, k), d) for d in out_dtypes],
        scratch_shapes=[pltpu.VMEM((tm, tko), F32)],
        compiler_params=_params(("parallel", "parallel", "arbitrary")),
    )(dy, w4, *extras)


def mm_tn(name, x, dy, gbuf, layer):
    s, k = x.shape
    nj, _, kw, n = gbuf.shape
    assert kw == k and dy.shape == (s, nj * n)
    ts = _pick(s, (512,))
    ns = s // ts
    tkk = _pick(k, (512, 256))
    tn = n if n <= 512 else 512
    nt = n // tn

    def body(x_ref, dy_ref, g_in, g_out, acc):
        del g_in
        si = pl.program_id(2)

        @pl.when(si == 0)
        def _():
            acc[...] = jnp.zeros_like(acc)

        acc[...] += lax.dot_general(x_ref[...], dy_ref[...], (((0,), (0,)), ((), ())), preferred_element_type=F32)

        @pl.when(si == ns - 1)
        def _():
            g_out[...] = acc[...].astype(g_out.dtype)

    return pl.pallas_call(
        body, name=name, grid=(k // tkk, nj * nt, ns),
        in_specs=[pl.BlockSpec((ts, tkk), lambda ki, j, si: (si, ki)),
                  pl.BlockSpec((ts, tn), lambda ki, j, si: (si, j)),
                  pl.BlockSpec(memory_space=pl.ANY)],
        out_specs=pl.BlockSpec((None, None, tkk, tn), lambda ki, j, si: (j // nt, layer, ki, j % nt)),
        out_shape=jax.ShapeDtypeStruct(gbuf.shape, gbuf.dtype),
        input_output_aliases={2: 0},
        scratch_shapes=[pltpu.VMEM((tkk, tn), F32)],
        compiler_params=_params(("parallel", "parallel", "arbitrary")),
    )(x, dy, gbuf)


def rms_fwd(name, h, g):
    s, d = h.shape
    tr = _pick(s, (ROW_TILE, 256, 128))

    def body(h_ref, g_ref, o_ref):
        x = h_ref[...]
        r = lax.rsqrt(jnp.mean(x * x, axis=-1, keepdims=True) + EPS)
        o_ref[...] = (x * r * g_ref[...]).astype(o_ref.dtype)

    return pl.pallas_call(
        body, name=name, grid=(s // tr,),
        in_specs=[pl.BlockSpec((tr, d), lambda i: (i, 0)), pl.BlockSpec((1, d), lambda i: (0, 0))],
        out_specs=pl.BlockSpec((tr, d), lambda i: (i, 0)),
        out_shape=jax.ShapeDtypeStruct((s, d), BF16),
        compiler_params=_params(("parallel",)),
    )(h, g.reshape(1, d))


def rms_bwd(name, h, g, dn, dres):
    s, d = h.shape
    tr = _pick(s, (ROW_TILE, 256, 128))

    def body(h_ref, g_ref, dn_ref, dres_ref, dx_ref, dxb_ref, dg_ref, cs_ref):
        i = pl.program_id(0)
        x = h_ref[...]
        r = lax.rsqrt(jnp.mean(x * x, axis=-1, keepdims=True) + EPS)
        xh = x * r
        dn_ = dn_ref[...]
        dyg = dn_ * g_ref[...]
        dx = dres_ref[...] + r * (dyg - xh * jnp.mean(dyg * xh, axis=-1, keepdims=True))
        dx_ref[...] = dx
        dxb_ref[...] = dx.astype(BF16)

        @pl.when(i == 0)
        def _():
            dg_ref[...] = jnp.zeros_like(dg_ref)
            cs_ref[...] = jnp.zeros_like(cs_ref)

        dg_ref[...] += jnp.sum(dn_ * xh, axis=0, keepdims=True)
        cs_ref[...] += jnp.sum(dx, axis=0, keepdims=True)

    row = pl.BlockSpec((tr, d), lambda i: (i, 0))
    vec = pl.BlockSpec((1, d), lambda i: (0, 0))
    return pl.pallas_call(
        body, name=name, grid=(s // tr,),
        in_specs=[row, vec, row, row],
        out_specs=[row, row, vec, vec],
        out_shape=[jax.ShapeDtypeStruct((s, d), F32), jax.ShapeDtypeStruct((s, d), BF16),
                   jax.ShapeDtypeStruct((1, d), F32), jax.ShapeDtypeStruct((1, d), F32)],
        compiler_params=_params(("arbitrary",)),
    )(h, g.reshape(1, d), dn, dres)


def loss_head(h, g, target):
    s, d = h.shape
    tr = _pick(s, (ROW_TILE, 256, 128))

    def body(h_ref, g_ref, t_ref, dx_ref, dxb_ref, dg_ref, loss_ref):
        i = pl.program_id(0)
        x = h_ref[...]
        r = lax.rsqrt(jnp.mean(x * x, axis=-1, keepdims=True) + EPS)
        xh = x * r
        gw = g_ref[...]
        err = xh * gw - t_ref[...]
        dn_ = err * (1.0 / d)
        dyg = dn_ * gw
        dx = r * (dyg - xh * jnp.mean(dyg * xh, axis=-1, keepdims=True))
        dx_ref[...] = dx
        dxb_ref[...] = dx.astype(BF16)

        @pl.when(i == 0)
        def _():
            dg_ref[...] = jnp.zeros_like(dg_ref)
            loss_ref[...] = jnp.zeros_like(loss_ref)

        dg_ref[...] += jnp.sum(dn_ * xh, axis=0, keepdims=True)
        per_row = jnp.sum(err * err, axis=-1, keepdims=True) * (0.5 / d)
        loss_ref[...] += jnp.broadcast_to(jnp.sum(per_row, axis=0, keepdims=True), loss_ref.shape)

    row = pl.BlockSpec((tr, d), lambda i: (i, 0))
    vec = pl.BlockSpec((1, d), lambda i: (0, 0))
    return pl.pallas_call(
        body, name="loss_head", grid=(s // tr,),
        in_specs=[row, vec, row],
        out_specs=[row, row, vec, pl.BlockSpec((1, 128), lambda i: (0, 0))],
        out_shape=[jax.ShapeDtypeStruct((s, d), F32), jax.ShapeDtypeStruct((s, d), BF16),
                   jax.ShapeDtypeStruct((1, d), F32), jax.ShapeDtypeStruct((1, 128), F32)],
        compiler_params=_params(("arbitrary",)),
    )(h, g.reshape(1, d), target)


def colsum(name, x):
    s, n = x.shape
    tr = _pick(s, (ROW_TILE, 256, 128))

    def body(x_ref, o_ref):
        @pl.when(pl.program_id(0) == 0)
        def _():
            o_ref[...] = jnp.zeros_like(o_ref)

        o_ref[...] += jnp.sum(x_ref[...].astype(F32), axis=0, keepdims=True)

    return pl.pallas_call(
        body, name=name, grid=(s // tr,),
        in_specs=[pl.BlockSpec((tr, n), lambda i: (i, 0))],
        out_specs=pl.BlockSpec((1, n), lambda i: (0, 0)),
        out_shape=jax.ShapeDtypeStruct((1, n), F32),
        compiler_params=_params(("arbitrary",)),
    )(x)


def _tri_rows(reverse):
    i = np.arange(SB_BK)
    tri = (i[None, :] >= i[:, None]) if reverse else (i[None, :] <= i[:, None])
    tri = np.concatenate([tri, tri], axis=1)
    return jnp.asarray(np.concatenate([tri, np.ones((8, 2 * SB_BK), bool)], axis=0), BF16)


def _hi_lo_rows(x):
    hi = x.astype(BF16)
    lo = (x - hi.astype(F32)).astype(BF16)
    return jnp.concatenate([hi, lo], axis=0)


def _softplus(z):
    return jnp.maximum(z, 0.0) + jnp.log(1.0 + jnp.exp(-jnp.abs(z)))


def _pair_mask(first_rel_block, bq):
    key = lax.broadcasted_iota(jnp.int32, (2 * SB_BK, bq), 0) + first_rel_block * SB_BK
    qry = lax.broadcasted_iota(jnp.int32, (2 * SB_BK, bq), 1)
    return key < qry


def _row_of(table8, sub8, r):
    return jnp.sum(jnp.where(sub8 == r, table8, 0.0), axis=0, keepdims=True)


def sb_fwd(name, qt, k, vt):
    nh, dh, s = qt.shape
    bq = SB_BQ
    per_q = bq // SB_BK
    nkb = s // SB_BK
    assert s % bq == 0 and per_q % 2 == 0 and nkb % 8 == 0

    def body(q_ref, k_ref, v_ref, a_ref, o_ref, rtab_ref, acc):
        i = pl.program_id(1)
        qb = q_ref[...] * Q_SCALE
        tri = a_ref[...]
        sub8 = lax.broadcasted_iota(jnp.int32, (8, bq), 0)
        acc[...] = jnp.zeros_like(acc)
        rtab_ref[...] = jnp.zeros_like(rtab_ref)

        def pair(j0, run, rt8, mask):
            off = pl.multiple_of(j0 * SB_BK, 2 * SB_BK)
            z = jnp.dot(k_ref[pl.ds(off, 2 * SB_BK), :], qb, preferred_element_type=F32)
            p = _softplus(z)
            if mask is not None:
                p = jnp.where(mask, p, 0.0)
            cr1 = jnp.dot(tri, _hi_lo_rows(p[SB_BK:]), preferred_element_type=F32)
            cr0 = jnp.dot(tri, _hi_lo_rows(p[:SB_BK]), preferred_element_type=F32)
            run1 = run + cr1[SB_BK:SB_BK + 1]
            w = jnp.exp(jnp.concatenate([z[:SB_BK] - cr0[:SB_BK] - run1, z[SB_BK:] - cr1[:SB_BK] - run], axis=0))
            if mask is not None:
                w = jnp.where(mask, w, 0.0)
            acc[...] += jnp.dot(v_ref[:, pl.ds(off, 2 * SB_BK)], w.astype(BF16), preferred_element_type=F32)
            rt8 = jnp.where(sub8 == (j0 + 1) % 8, run, jnp.where(sub8 == j0 % 8, run1, rt8))
            rtab_ref[pl.ds(pl.multiple_of((j0 // 8) * 8, 8), 8), :] = rt8
            return run1 + cr0[SB_BK:SB_BK + 1], rt8

        carry = (jnp.zeros((1, bq), F32), jnp.zeros((8, bq), F32))
        for rel in range(per_q - 2, -1, -2):
            carry = pair(i * per_q + rel, *carry, _pair_mask(rel, bq))

        def step(it, c):
            return pair(i * per_q - 2 - 2 * it, *c, None)

        lax.fori_loop(0, i * (per_q // 2), step, carry)
        o_ref[...] = acc[...].astype(o_ref.dtype)

    qspec = pl.BlockSpec((None, dh, bq), lambda h, i: (h, 0, i))
    return pl.pallas_call(
        body, name=name, grid=(nh, s // bq),
        in_specs=[qspec, pl.BlockSpec((None, s, dh), lambda h, i: (h, 0, 0)),
                  pl.BlockSpec((None, dh, s), lambda h, i: (h, 0, 0)),
                  pl.BlockSpec((SB_BK + 8, 2 * SB_BK), lambda h, i: (0, 0))],
        out_specs=[qspec, pl.BlockSpec((None, nkb, bq), lambda h, i: (h, 0, i))],
        out_shape=[jax.ShapeDtypeStruct((nh, dh, s), BF16), jax.ShapeDtypeStruct((nh, nkb, s), F32)],
        scratch_shapes=[pltpu.VMEM((dh, bq), F32)],
        compiler_params=_params(("parallel", "arbitrary")),
    )(qt, k, vt, _tri_rows(True))


def sb_bwd(name, qt, q, k, kt, v, dot_, do, rtab):
    nh, dh, s = qt.shape
    bq = SB_BQ
    per_q = bq // SB_BK
    nkb = s // SB_BK

    def body(qt_ref, q_ref, k_ref, kt_ref, v_ref, dot_ref, do_ref, rtab_ref, ar_ref, af_ref,
             dq_ref, dk_ref, dv_ref, dq_acc):
        i = pl.program_id(1)

        @pl.when(i == 0)
        def _():
            dk_ref[...] = jnp.zeros_like(dk_ref)
            dv_ref[...] = jnp.zeros_like(dv_ref)

        qtb = qt_ref[...] * Q_SCALE
        qrows = q_ref[...] * Q_SCALE
        dotb = dot_ref[...]
        dorows = do_ref[...]
        tri_rev = ar_ref[...][:SB_BK]
        tri_fwd = af_ref[...]
        sub8 = lax.broadcasted_iota(jnp.int32, (8, bq), 0)
        dq_acc[...] = jnp.zeros_like(dq_acc)

        def pair(j0, g_run, mask):
            off = pl.multiple_of(j0 * SB_BK, 2 * SB_BK)
            kb = k_ref[pl.ds(off, 2 * SB_BK), :]
            z = jnp.dot(kb, qtb, preferred_element_type=F32)
            p_raw = _softplus(z)
            p = p_raw if mask is None else jnp.where(mask, p_raw, 0.0)
            c0 = jnp.dot(tri_rev, _hi_lo_rows(p[:SB_BK]), preferred_element_type=F32)
            c1 = jnp.dot(tri_rev, _hi_lo_rows(p[SB_BK:]), preferred_element_type=F32)
            rt8 = rtab_ref[pl.ds(pl.multiple_of((j0 // 8) * 8, 8), 8), :]
            r0 = _row_of(rt8, sub8, j0 % 8)
            r1 = _row_of(rt8, sub8, (j0 + 1) % 8)
            w = jnp.exp(jnp.concatenate([z[:SB_BK] - c0 - r0, z[SB_BK:] - c1 - r1], axis=0))
            if mask is not None:
                w = jnp.where(mask, w, 0.0)
            dw = jnp.dot(v_ref[pl.ds(off, 2 * SB_BK), :], dotb, preferred_element_type=F32)
            g = w * dw
            gg0 = jnp.dot(tri_fwd, _hi_lo_rows(g[:SB_BK]), preferred_element_type=F32)
            gg1 = jnp.dot(tri_fwd, _hi_lo_rows(g[SB_BK:]), preferred_element_type=F32)
            g_run1 = g_run + gg0[SB_BK:SB_BK + 1]
            g_pre = jnp.concatenate([gg0[:SB_BK] + g_run, gg1[:SB_BK] + g_run1], axis=0)
            dz = g - jnp.exp(z - p_raw) * g_pre
            if mask is not None:
                dz = jnp.where(mask, dz, 0.0)
            dzb = dz.astype(BF16)
            dq_acc[...] += jnp.dot(kt_ref[:, pl.ds(off, 2 * SB_BK)], dzb, preferred_element_type=F32)
            dk_ref[pl.ds(off, 2 * SB_BK), :] += jnp.dot(dzb, qrows, preferred_element_type=F32)
            dv_ref[pl.ds(off, 2 * SB_BK), :] += jnp.dot(w.astype(BF16), dorows, preferred_element_type=F32)
            return g_run1 + gg1[SB_BK:SB_BK + 1]

        g_run = lax.fori_loop(0, i * (per_q // 2), lambda it, c: pair(2 * it, c, None), jnp.zeros((1, bq), F32))
        for rel in range(0, per_q, 2):
            g_run = pair(i * per_q + rel, g_run, _pair_mask(rel, bq))
        dq_ref[...] = dq_acc[...] * Q_SCALE

    tspec = pl.BlockSpec((None, dh, bq), lambda h, i: (h, 0, i))
    rspec = pl.BlockSpec((None, bq, dh), lambda h, i: (h, i, 0))
    kspec = pl.BlockSpec((None, s, dh), lambda h, i: (h, 0, 0))
    aspec = pl.BlockSpec((SB_BK + 8, 2 * SB_BK), lambda h, i: (0, 0))
    return pl.pallas_call(
        body, name=name, grid=(nh, s // bq),
        in_specs=[tspec, rspec, kspec, pl.BlockSpec((None, dh, s), lambda h, i: (h, 0, 0)), kspec, tspec, rspec,
                  pl.BlockSpec((None, nkb, bq), lambda h, i: (h, 0, i)), aspec, aspec],
        out_specs=[tspec, kspec, kspec],
        out_shape=[jax.ShapeDtypeStruct((nh, dh, s), F32), jax.ShapeDtypeStruct((nh, s, dh), F32),
                   jax.ShapeDtypeStruct((nh, s, dh), F32)],
        scratch_shapes=[pltpu.VMEM((dh, bq), F32)],
        compiler_params=_params(("parallel", "arbitrary")),
    )(qt, q, k, kt, v, dot_, do, rtab, _tri_rows(True), _tri_rows(False))


def _swa_scores(q_ref, k_ref, bias_ref, sink_ref, i):
    rows = q_ref.shape[0]
    qb = q_ref[...] * Q_SCALE
    off = pl.multiple_of(i * WINDOW, WINDOW)
    kb = k_ref[pl.ds(off, 2 * WINDOW), :]
    sc = lax.dot_general(qb, kb, (((1,), (1,)), ((), ())), preferred_element_type=F32) + bias_ref[...]
    qi = lax.broadcasted_iota(jnp.int32, (rows, 2 * WINDOW), 0) & (WINDOW - 1)
    kj = lax.broadcasted_iota(jnp.int32, (rows, 2 * WINDOW), 1)
    dist = qi + WINDOW - kj
    valid = (dist >= 0) & (dist < WINDOW) & ((kj >= WINDOW) | (i > 0))
    sc = jnp.where(valid, sc, NEG_INF)
    sink = sink_ref[...]
    mx = jnp.maximum(jnp.max(sc, axis=1, keepdims=True), sink)
    p = jnp.exp(sc - mx)
    p_sink = jnp.exp(sink - mx)
    inv = 1.0 / (jnp.sum(p, axis=1, keepdims=True) + p_sink)
    return qb, kb, off, p, p_sink, inv


def swa_fwd(name, qg, kp, vp, bias, sink_col):
    ng, nb, rows, dh = qg.shape
    sp = kp.shape[1]

    def body(q_ref, k_ref, v_ref, bias_ref, sink_ref, o_ref):
        i = pl.program_id(1)
        _, _, off, p, _, inv = _swa_scores(q_ref, k_ref, bias_ref, sink_ref, i)
        vb = v_ref[pl.ds(off, 2 * WINDOW), :]
        o_ref[...] = (jnp.dot(p.astype(BF16), vb, preferred_element_type=F32) * inv).astype(o_ref.dtype)

    qspec = pl.BlockSpec((None, None, rows, dh), lambda g, i: (g, i, 0, 0))
    kspec = pl.BlockSpec((None, sp, dh), lambda g, i: (g, 0, 0))
    return pl.pallas_call(
        body, name=name, grid=(ng, nb),
        in_specs=[qspec, kspec, kspec, pl.BlockSpec((None, rows, 2 * WINDOW), lambda g, i: (g, 0, 0)),
                  pl.BlockSpec((None, rows, 1), lambda g, i: (g, 0, 0))],
        out_specs=qspec,
        out_shape=jax.ShapeDtypeStruct(qg.shape, BF16),
        compiler_params=_params(("parallel", "arbitrary")),
    )(qg, kp, vp, bias, sink_col)


def swa_bwd(name, qg, kp, vp, bias, sink_col, dog, dk_in, dv_in):
    ng, nb, rows, dh = qg.shape
    sp = kp.shape[1]

    def body(q_ref, k_ref, v_ref, bias_ref, sink_ref, do_ref, dki_ref, dvi_ref, dq_ref, dk_ref, dv_ref, db_ref, ds_ref):
        i = pl.program_id(1)

        @pl.when(i == 0)
        def _():
            dk_ref[...] = dki_ref[...]
            dv_ref[...] = dvi_ref[...]
            db_ref[...] = jnp.zeros_like(db_ref)
            ds_ref[...] = jnp.zeros_like(ds_ref)

        qb, kb, off, p, p_sink, inv = _swa_scores(q_ref, k_ref, bias_ref, sink_ref, i)
        p = p * inv
        dob = do_ref[...]
        vb = v_ref[pl.ds(off, 2 * WINDOW), :]
        dp = lax.dot_general(dob, vb, (((1,), (1,)), ((), ())), preferred_element_type=F32)
        delta = jnp.sum(p * dp, axis=1, keepdims=True)
        dsc = p * (dp - delta)
        ds_ref[...] -= p_sink * inv * delta
        db_ref[...] += dsc
        dscb = dsc.astype(BF16)
        dq_ref[...] = (jnp.dot(dscb, kb, preferred_element_type=F32) * Q_SCALE).astype(dq_ref.dtype)
        dk_ref[pl.ds(off, 2 * WINDOW), :] += lax.dot_general(dscb, qb, (((0,), (0,)), ((), ())),
                                                             preferred_element_type=F32)
        dv_ref[pl.ds(off, 2 * WINDOW), :] += lax.dot_general(p.astype(BF16), dob, (((0,), (0,)), ((), ())),
                                                             preferred_element_type=F32)

    qspec = pl.BlockSpec((None, None, rows, dh), lambda g, i: (g, i, 0, 0))
    kspec = pl.BlockSpec((None, sp, dh), lambda g, i: (g, 0, 0))
    bspec = pl.BlockSpec((None, rows, 2 * WINDOW), lambda g, i: (g, 0, 0))
    sspec = pl.BlockSpec((None, rows, 1), lambda g, i: (g, 0, 0))
    return pl.pallas_call(
        body, name=name, grid=(ng, nb),
        in_specs=[qspec, kspec, kspec, bspec, sspec, qspec, kspec, kspec],
        out_specs=[qspec, kspec, kspec, bspec, sspec],
        out_shape=[jax.ShapeDtypeStruct(qg.shape, BF16), jax.ShapeDtypeStruct(kp.shape, F32),
                   jax.ShapeDtypeStruct(kp.shape, F32), jax.ShapeDtypeStruct(bias.shape, F32),
                   jax.ShapeDtypeStruct(sink_col.shape, F32)],
        compiler_params=_params(("parallel", "arbitrary")),
    )(qg, kp, vp, bias, sink_col, dog, dk_in, dv_in)


def _bucket_onehot():
    qi = np.arange(WINDOW)[:, None]
    kj = np.arange(2 * WINDOW)[None, :]
    n = np.maximum(qi + WINDOW - kj, 0)
    max_exact = N_BUCKETS // 2
    nf = np.maximum(n, 1).astype(np.float64)
    val = np.log(nf / max_exact) / math.log(WINDOW / max_exact) * (N_BUCKETS - max_exact)
    assert np.all(np.abs(val - np.round(val))[(n > max_exact) & (n < WINDOW)] > 1e-3)
    large = np.minimum(max_exact + val.astype(np.int64), N_BUCKETS - 1)
    bucket = np.where(n < max_exact, n, large).reshape(-1)
    onehot = np.zeros((128, bucket.size), np.float32)
    onehot[bucket, np.arange(bucket.size)] = 1.0
    return onehot


def _split3(x):
    a = x.astype(BF16)
    r = x - a.astype(F32)
    b = r.astype(BF16)
    c = (r - b.astype(F32)).astype(BF16)
    return a, b, c


def bias_table(rel_bias):
    nh = rel_bias.shape[1]
    oh = jnp.asarray(_bucket_onehot(), BF16)
    n = oh.shape[1]
    tn = 4096
    rb = jnp.zeros((nh, 128), F32).at[:, :N_BUCKETS].set(rel_bias.T)

    def body(rb_ref, oh_ref, o_ref):
        o_ref[...] = sum(jnp.dot(t, oh_ref[...], preferred_element_type=F32) for t in _split3(rb_ref[...]))

    return pl.pallas_call(
        body, name="bias_table", grid=(n // tn,),
        in_specs=[pl.BlockSpec((nh, 128), lambda i: (0, 0)), pl.BlockSpec((128, tn), lambda i: (0, i))],
        out_specs=pl.BlockSpec((nh, tn), lambda i: (0, i)),
        out_shape=jax.ShapeDtypeStruct((nh, n), F32),
        compiler_params=_params(("parallel",)),
    )(rb, oh)


def bias_table_grad(db0, db1):
    nh, n = db0.shape
    oh = jnp.asarray(_bucket_onehot(), BF16)
    tn = 4096

    def body(a_ref, b_ref, oh_ref, o_ref):
        @pl.when(pl.program_id(0) == 0)
        def _():
            o_ref[...] = jnp.zeros_like(o_ref)

        o_ref[...] += sum(lax.dot_general(t, oh_ref[...], (((1,), (1,)), ((), ())), preferred_element_type=F32)
                          for t in _split3(a_ref[...] + b_ref[...]))

    blk = pl.BlockSpec((nh, tn), lambda i: (0, i))
    return pl.pallas_call(
        body, name="bias_table_grad", grid=(n // tn,),
        in_specs=[blk, blk, pl.BlockSpec((128, tn), lambda i: (0, i))],
        out_specs=pl.BlockSpec((nh, 128), lambda i: (0, 0)),
        out_shape=jax.ShapeDtypeStruct((nh, 128), F32),
        compiler_params=_params(("arbitrary",)),
    )(db0, db1, oh)


def _owner_view(ref, name, d):
    if name in COL_SHARDED or name == 'a_norm':
        return ref.at[d]
    return ref.at[:, d]


def _place():
    return lax.axis_index("x"), lax.axis_index("y"), lax.axis_index("c")


def all_gather_weights(names, shards, full_shapes):
    n = len(names)

    def body(*refs):
        ins, outs = refs[:n], refs[n:2 * n]
        send_sems, recv_sems, local_sems = refs[2 * n:]
        x, y, c = _place()
        me, sibling = (x, y, c), (x, y, 1 - c)
        chips = [(1 - x, y), (x, 1 - y), (1 - x, 1 - y)]

        def dev(p):
            return 4 * p[0] + 2 * p[1] + p[2]

        def copy(t, k, block, to, src=None):
            dst = _owner_view(outs[t], names[t], dev(block))
            return pltpu.make_async_remote_copy(
                src_ref=dst if src is None else src, dst_ref=dst,
                send_sem=send_sems.at[t, k], recv_sem=recv_sems.at[t, k], device_id=to, device_id_type=MESH)

        mine = [pltpu.make_async_copy(ins[t], _owner_view(outs[t], names[t], dev(me)), local_sems.at[t])
                for t in range(n)]
        for cp in mine:
            cp.start()
        first = []
        for t in range(n):
            first.append(copy(t, 0, me, sibling, src=ins[t]))
            first += [copy(t, 1 + j, me, (*chip, c), src=ins[t]) for j, chip in enumerate(chips)]
        for cp in first:
            cp.start()
        passed = []
        for j, chip in enumerate(chips):
            for t in range(n):
                copy(t, 1 + j, (*chip, c), me).wait_recv()
                fwd = copy(t, 4 + j, (*chip, c), sibling)
                fwd.start()
                passed.append(fwd)
        for t in range(n):
            copy(t, 0, sibling, me).wait_recv()
            for j, chip in enumerate(chips):
                copy(t, 4 + j, (*chip, 1 - c), me).wait_recv()
        for cp in first + passed:
            cp.wait_send()
        for cp in mine:
            cp.wait()

    hbm = pl.BlockSpec(memory_space=pl.ANY)
    return pl.pallas_call(
        body, name="all_gather_weights",
        in_specs=[hbm] * n, out_specs=[hbm] * n,
        out_shape=[jax.ShapeDtypeStruct(full_shapes[t], shards[t].dtype) for t in range(n)],
        scratch_shapes=[pltpu.SemaphoreType.DMA((n, 7)), pltpu.SemaphoreType.DMA((n, 7)),
                        pltpu.SemaphoreType.DMA((n,))],
    )(*shards)


def sibling_exchange(names, grads, part_shapes):
    n = len(names)

    def body(*refs):
        ins, outs = refs[:n], refs[n:2 * n]
        send_sems, recv_sems = refs[2 * n:]
        x, y, c = _place()
        sibling = (x, y, 1 - c)
        copies = []
        for t in range(n):
            for q in range(4):
                copies.append(pltpu.make_async_remote_copy(
                    src_ref=_owner_view(ins[t], names[t], 2 * q + 1 - c), dst_ref=outs[t].at[q],
                    send_sem=send_sems.at[t, q], recv_sem=recv_sems.at[t, q], device_id=sibling, device_id_type=MESH))
        for cp in copies:
            cp.start()
        for cp in copies:
            cp.wait()

    hbm = pl.BlockSpec(memory_space=pl.ANY)
    return pl.pallas_call(
        body, name="rs_sibling_exchange",
        in_specs=[hbm] * n, out_specs=[hbm] * n,
        out_shape=[jax.ShapeDtypeStruct((4,) + part_shapes[t], BF16) for t in range(n)],
        scratch_shapes=[pltpu.SemaphoreType.DMA((n, 4)), pltpu.SemaphoreType.DMA((n, 4))],
    )(*grads)


def chip_exchange(names, parts, part_shapes):
    n = len(names)

    def body(*refs):
        ins, outs = refs[:n], refs[n:2 * n]
        send_sems, recv_sems = refs[2 * n:]
        x, y, c = _place()
        chips = [(1 - x, y), (x, 1 - y), (1 - x, 1 - y)]
        copies = []
        for t in range(n):
            for k, chip in enumerate(chips):
                copies.append(pltpu.make_async_remote_copy(
                    src_ref=ins[t].at[2 * chip[0] + chip[1]], dst_ref=outs[t].at[k],
                    send_sem=send_sems.at[t, k], recv_sem=recv_sems.at[t, k], device_id=(*chip, c),
                    device_id_type=MESH))
        for cp in copies:
            cp.start()
        for cp in copies:
            cp.wait()

    hbm = pl.BlockSpec(memory_space=pl.ANY)
    return pl.pallas_call(
        body, name="rs_chip_exchange",
        in_specs=[hbm] * n, out_specs=[hbm] * n,
        out_shape=[jax.ShapeDtypeStruct((3,) + part_shapes[t], BF16) for t in range(n)],
        scratch_shapes=[pltpu.SemaphoreType.DMA((n, 3)), pltpu.SemaphoreType.DMA((n, 3))],
    )(*parts)


def all_gather_rows(x):
    r, w = x.shape

    def body(x_ref, out_ref, send_sems, recv_sems, local_sem):
        px, py, pc = _place()
        me = 4 * px + 2 * py + pc
        mine = pltpu.make_async_copy(x_ref, out_ref.at[me], local_sem)
        mine.start()
        copies = []
        for k in range(1, N_DEV):
            peer = (px ^ (k >> 2), py ^ ((k >> 1) & 1), pc ^ (k & 1))
            copies.append(pltpu.make_async_remote_copy(
                src_ref=x_ref, dst_ref=out_ref.at[me], send_sem=send_sems.at[k - 1], recv_sem=recv_sems.at[k - 1],
                device_id=peer, device_id_type=MESH))
        for cp in copies:
            cp.start()
        for k in range(1, N_DEV):
            peer_idx = me ^ k
            pltpu.make_async_remote_copy(
                src_ref=x_ref, dst_ref=out_ref.at[peer_idx], send_sem=send_sems.at[k - 1],
                recv_sem=recv_sems.at[k - 1], device_id=(px, py, pc), device_id_type=MESH).wait_recv()
        for cp in copies:
            cp.wait_send()
        mine.wait()

    vmem = pl.BlockSpec(memory_space=pltpu.VMEM)
    return pl.pallas_call(
        body, name="all_gather_small_grads",
        in_specs=[vmem], out_specs=vmem,
        out_shape=jax.ShapeDtypeStruct((N_DEV, r, w), x.dtype),
        scratch_shapes=[pltpu.SemaphoreType.DMA((N_DEV - 1,)), pltpu.SemaphoreType.DMA((N_DEV - 1,)),
                        pltpu.SemaphoreType.DMA],
    )(x)


def _adamw(w, g, m, v):
    m = ADAM_B1 * m + (1.0 - ADAM_B1) * g
    v = ADAM_B2 * v + (1.0 - ADAM_B2) * (g * g)
    m_hat = m / (1.0 - ADAM_B1 ** ADAM_STEP)
    v_hat = v / (1.0 - ADAM_B2 ** ADAM_STEP)
    return -ADAM_LR * (m_hat / (jnp.sqrt(v_hat) + ADAM_EPS) + ADAM_WD * w), m, v


def _as_rows(a, lead):
    return a.reshape(a.shape[:lead] + (-1, a.shape[-1]))


def sibling_sum(name, col, grads, recv, core):
    recv2 = _as_rows(recv, 1)
    _, rows, cols = recv2.shape
    tr = _pick(rows, (512, 256, 128))
    if col:
        g3 = grads.reshape(4, 2, rows, cols)
        gspec = pl.BlockSpec((None, None, tr, cols), lambda q, i, c_ref: (q, c_ref[0], i, 0))
        grid = (4, rows // tr)
        rspec = pl.BlockSpec((None, tr, cols), lambda q, i, c_ref: (q, i, 0))
    else:
        nl, _, r_loc, _ = grads.shape
        tr = _pick(r_loc, (512, 256, 128))
        g3 = grads.reshape(nl, 4, 2, r_loc, cols)
        recv2 = recv.reshape(4, nl, r_loc, cols)
        gspec = pl.BlockSpec((None, None, None, tr, cols), lambda q, l, i, c_ref: (l, q, c_ref[0], i, 0))
        rspec = pl.BlockSpec((None, None, tr, cols), lambda q, l, i, c_ref: (q, l, i, 0))
        grid = (4, nl, r_loc // tr)

    def body(c_ref, g_ref, r_ref, o_ref):
        del c_ref
        o_ref[...] = (g_ref[...].astype(F32) + r_ref[...].astype(F32)).astype(BF16)

    out = pl.pallas_call(
        body, name=name,
        grid_spec=pltpu.PrefetchScalarGridSpec(num_scalar_prefetch=1, grid=grid, in_specs=[gspec, rspec],
                                               out_specs=rspec),
        out_shape=jax.ShapeDtypeStruct(recv2.shape, BF16),
        compiler_params=_params(("parallel",) * len(grid)),
    )(core.reshape(1), g3, recv2)
    return out.reshape(recv.shape)


def reduce_adamw(name, parts, recv, chip, w, m, v):
    shape = w.shape
    w2, m2, v2 = (_as_rows(a, 0) for a in (w, m, v))
    rows, cols = w2.shape
    p3 = parts.reshape(4, rows, cols)
    r3 = recv.reshape(3, rows, cols)
    tr = _pick(rows, (256, 128))

    def body(q_ref, p_ref, r_ref, w_ref, m_ref, v_ref, g_out, d_out, m_out, v_out):
        del q_ref
        g = ((p_ref[...].astype(F32) + r_ref[0].astype(F32)) + r_ref[1].astype(F32)) + r_ref[2].astype(F32)
        d, mn, vn = _adamw(w_ref[...], g, m_ref[...], v_ref[...])
        g_out[...] = g
        d_out[...] = d
        m_out[...] = mn
        v_out[...] = vn

    blk = pl.BlockSpec((tr, cols), lambda i, q_ref: (i, 0))
    outs = pl.pallas_call(
        body, name=name,
        grid_spec=pltpu.PrefetchScalarGridSpec(
            num_scalar_prefetch=1, grid=(rows // tr,),
            in_specs=[pl.BlockSpec((None, tr, cols), lambda i, q_ref: (q_ref[0], i, 0)),
                      pl.BlockSpec((3, tr, cols), lambda i, q_ref: (0, i, 0)), blk, blk, blk],
            out_specs=[blk] * 4),
        out_shape=[jax.ShapeDtypeStruct((rows, cols), F32)] * 4,
        compiler_params=_params(("parallel",)),
    )(chip.reshape(1), p3, r3, w2, m2, v2)
    return [o.reshape(shape) for o in outs]


def small_adamw(name, gathered, w, m, v):
    _, r, c = gathered.shape

    def body(ga_ref, w_ref, m_ref, v_ref, g_out, d_out, m_out, v_out):
        g = ga_ref[0]
        for d in range(1, N_DEV):
            g = g + ga_ref[d]
        dl, mn, vn = _adamw(w_ref[...], g, m_ref[...], v_ref[...])
        g_out[...] = g
        d_out[...] = dl
        m_out[...] = mn
        v_out[...] = vn

    return pl.pallas_call(
        body, name=name,
        out_shape=[jax.ShapeDtypeStruct((r, c), F32)] * 4,
        compiler_params=_params(),
    )(gathered, w, m, v)


def _heads(t, nh):
    s = t.shape[0]
    return t.reshape(s, nh, HEAD_DIM).transpose(1, 0, 2)


def _heads_t(t, nh):
    return t.T.reshape(nh, HEAD_DIM, t.shape[0])


def _unheads(t):
    nh, s, dh = t.shape
    return t.transpose(1, 0, 2).reshape(s, nh * dh)


def _group_q(t, nb):
    s = t.shape[0]
    return t.reshape(nb, WINDOW, 2, 8, HEAD_DIM).transpose(2, 0, 3, 1, 4).reshape(2, nb, 8 * WINDOW, HEAD_DIM)


def _ungroup_q(t):
    ng, nb, _, dh = t.shape
    return t.reshape(ng, nb, 8, WINDOW, dh).transpose(1, 3, 0, 2, 4).reshape(nb * WINDOW, ng * 8 * dh)


def _front_pad(t):
    return jnp.pad(t, ((0, 0), (WINDOW, 0), (0, 0)))


def _add(acc, *ex):
    return (acc + ex[0],)


def local_step(x, target, wf, small, gbufs):
    s, d = x.shape
    nb = s // WINDOW
    n_a, n_b = small['a_norm'].shape[0], small['b_norm'].shape[0]
    sg = {}
    gb = dict(gbufs)

    bias_flat = bias_table(small['rel_bias'])
    bias_g = bias_flat.reshape(2, 8 * WINDOW, 2 * WINDOW)
    sink_cols = [jnp.repeat(small['b_sinks'][j], WINDOW).reshape(2, 8 * WINDOW, 1) for j in range(n_b)]

    def mlp_fwd(h, layer):
        n2 = rms_fwd(f"mlp_norm_fwd{layer}", h, small['mlp_norm'][layer])
        u, a = mm_nn(f"mlp_up_fwd{layer}", n2, wf['mlp_up'], layer,
                     lambda acc: (acc, jnp.square(jnp.maximum(acc, 0.0))), (), (BF16, BF16))
        (h2,) = mm_nn(f"mlp_down_fwd{layer}", a, wf['mlp_down'], layer, _add, (h,), (F32,))
        return h2, (n2, u, a)

    h = x
    saved = []
    for l in range(n_a):
        n1 = rms_fwd(f"a_norm_fwd{l}", h, small['a_norm'][l])
        (qkv,) = mm_nn(f"a_qkv_fwd{l}", n1, wf['a_wqkv'], l, lambda acc: (acc,), (), (BF16,))
        nh = d // HEAD_DIM
        o_t, rtab = sb_fwd(f"sb_fwd{l}", _heads_t(qkv[:, :d], nh), _heads(qkv[:, d:2 * d], nh),
                           _heads_t(qkv[:, 2 * d:], nh))
        o = o_t.reshape(d, s).T
        (h_mid,) = mm_nn(f"a_wo_fwd{l}", o, wf['a_wo'], l, _add, (h,), (F32,))
        h_out, mlp_saved = mlp_fwd(h_mid, l)
        saved.append((h, n1, qkv, o, rtab, h_mid, mlp_saved))
        h = h_out
    h_kv = h
    nkv = rms_fwd("kv_norm_fwd", h, small['kv_norm'])
    (kv,) = mm_nn("kv_fwd", nkv, wf['w_kv'], 0, lambda acc, b: (acc + b,), (small['b_kv'].reshape(1, -1),), (BF16,))
    kvw = kv.shape[1] // 2
    kp = _front_pad(_heads(kv[:, :kvw], 2))
    vp = _front_pad(_heads(kv[:, kvw:], 2))
    for j in range(n_b):
        layer = n_a + j
        n1 = rms_fwd(f"b_norm_fwd{j}", h, small['b_norm'][j])
        (qb,) = mm_nn(f"b_q_fwd{j}", n1, wf['b_wq'], j, lambda acc, b: (acc + b,),
                      (small['b_bq'][j].reshape(1, -1),), (BF16,))
        qg = _group_q(qb, nb)
        og = swa_fwd(f"swa_fwd{j}", qg, kp, vp, bias_g, sink_cols[j])
        o = _ungroup_q(og)
        (h_mid,) = mm_nn(f"b_wo_fwd{j}", o, wf['b_wo'], j, lambda acc, hh, b: (acc + hh + b,),
                         (h, small['b_bo'][j].reshape(1, -1)), (F32,))
        h_out, mlp_saved = mlp_fwd(h_mid, layer)
        saved.append((h, n1, qg, o, h_mid, mlp_saved))
        h = h_out

    dh, dhb, dg_final, loss_b = loss_head(h, small['final_norm'], target)
    sg['final_norm'] = dg_final[0]
    sg['mlp_norm'] = [None] * (n_a + n_b)
    cs_mid = None

    def mlp_bwd(dh, dhb, h_mid, mlp_saved, layer):
        n2, u, a = mlp_saved
        (du,) = mm_nt(f"mlp_down_dx{layer}", dhb, wf['mlp_down'], layer,
                      lambda acc, uu: (acc * (2.0 * jnp.maximum(uu.astype(F32), 0.0)),), (u,), (BF16,))
        gb['mlp_down'] = mm_tn(f"mlp_down_dw{layer}", a, dhb, gb['mlp_down'], layer)
        (dn2,) = mm_nt(f"mlp_up_dx{layer}", du, wf['mlp_up'], layer, lambda acc: (acc,), (), (F32,))
        gb['mlp_up'] = mm_tn(f"mlp_up_dw{layer}", n2, du, gb['mlp_up'], layer)
        dh2, dh2b, dg, cs = rms_bwd(f"mlp_norm_bwd{layer}", h_mid, small['mlp_norm'][layer], dn2, dh)
        sg['mlp_norm'][layer] = dg[0]
        return dh2, dh2b, cs

    dkp = jnp.zeros(kp.shape, F32)
    dvp = jnp.zeros(vp.shape, F32)
    sg['b_norm'], sg['b_bq'], sg['b_bo'], sg['b_sinks'] = [None] * n_b, [None] * n_b, [None] * n_b, [None] * n_b
    dbias = [None] * n_b
    for j in reversed(range(n_b)):
        layer = n_a + j
        h_in, n1, qg, o, h_mid, mlp_saved = saved[layer]
        dh, dhb, cs = mlp_bwd(dh, dhb, h_mid, mlp_saved, layer)
        sg['b_bo'][j] = cs[0]
        (do,) = mm_nt(f"b_wo_dx{j}", dhb, wf['b_wo'], j, lambda acc: (acc,), (), (BF16,))
        gb['b_wo'] = mm_tn(f"b_wo_dw{j}", o, dhb, gb['b_wo'], j)
        dqg, dkp, dvp, dbias[j], dsink = swa_bwd(f"swa_bwd{j}", qg, kp, vp, bias_g, sink_cols[j],
                                                 _group_q(do, nb), dkp, dvp)
        sg['b_sinks'][j] = colsum(f"sink_grad{j}", dsink.reshape(16, WINDOW).T)[0]
        dq = _ungroup_q(dqg)
        sg['b_bq'][j] = colsum(f"b_bq_grad{j}", dq)[0]
        (dn1,) = mm_nt(f"b_q_dx{j}", dq, wf['b_wq'], j, lambda acc: (acc,), (), (F32,))
        gb['b_wq'] = mm_tn(f"b_q_dw{j}", n1, dq, gb['b_wq'], j)
        dh, dhb, dg, _ = rms_bwd(f"b_norm_bwd{j}", h_in, small['b_norm'][j], dn1, dh)
        sg['b_norm'][j] = dg[0]
    sg['rel_bias'] = bias_table_grad(dbias[0].reshape(bias_flat.shape),
                                     dbias[1].reshape(bias_flat.shape))[:, :N_BUCKETS].T

    dkv = jnp.concatenate([_unheads(dkp[:, WINDOW:]), _unheads(dvp[:, WINDOW:])], axis=1)
    sg['b_kv'] = colsum("b_kv_grad", dkv)[0]
    dkvb = dkv.astype(BF16)
    (dnkv,) = mm_nt("kv_dx", dkvb, wf['w_kv'], 0, lambda acc: (acc,), (), (F32,))
    gb['w_kv'] = mm_tn("kv_dw", nkv, dkvb, gb['w_kv'], 0)
    dh, dhb, dg, _ = rms_bwd("kv_norm_bwd", h_kv, small['kv_norm'], dnkv, dh)
    sg['kv_norm'] = dg[0]

    sg['a_norm'] = [None] * n_a
    for l in reversed(range(n_a)):
        h_in, n1, qkv, o, rtab, h_mid, mlp_saved = saved[l]
        dh, dhb, _ = mlp_bwd(dh, dhb, h_mid, mlp_saved, l)
        (do,) = mm_nt(f"a_wo_dx{l}", dhb, wf['a_wo'], l, lambda acc: (acc,), (), (BF16,))
        gb['a_wo'] = mm_tn(f"a_wo_dw{l}", o, dhb, gb['a_wo'], l)
        nh = d // HEAD_DIM
        qs, ks, vs = qkv[:, :d], qkv[:, d:2 * d], qkv[:, 2 * d:]
        dq_t, dk, dv = sb_bwd(f"sb_bwd{l}", _heads_t(qs, nh), _heads(qs, nh), _heads(ks, nh), _heads_t(ks, nh),
                              _heads(vs, nh), _heads_t(do, nh), _heads(do, nh), rtab)
        dqkv = jnp.concatenate([dq_t.reshape(d, s).T, _unheads(dk), _unheads(dv)], axis=1).astype(BF16)
        (dn1,) = mm_nt(f"a_qkv_dx{l}", dqkv, wf['a_wqkv'], l, lambda acc: (acc,), (), (F32,))
        gb['a_wqkv'] = mm_tn(f"a_qkv_dw{l}", n1, dqkv, gb['a_wqkv'], l)
        dh, dhb, dg, _ = rms_bwd(f"a_norm_bwd{l}", h_in, small['a_norm'][l], dn1, dh)
        sg['a_norm'][l] = dg[0]

    small_grads = {
        'a_norm': jnp.stack(sg['a_norm']), 'kv_norm': sg['kv_norm'], 'b_kv': sg['b_kv'],
        'b_norm': jnp.stack(sg['b_norm']), 'b_bq': jnp.stack(sg['b_bq']), 'b_sinks': jnp.stack(sg['b_sinks']),
        'b_bo': jnp.stack(sg['b_bo']), 'rel_bias': sg['rel_bias'], 'mlp_norm': jnp.stack(sg['mlp_norm']),
        'final_norm': sg['final_norm'],
    }
    return loss_b, dh, gb, small_grads


def _full_shape(name, shard_shape):
    if name in COL_SHARDED:
        return (N_DEV,) + shard_shape
    nl, r, n = shard_shape
    return (nl, N_DEV, r, n)


def _as_w4(name, full):
    if name in COL_SHARDED:
        return full
    nl, nd, r, n = full.shape
    return full.reshape(1, nl, nd * r, n)


def _pack_small(vals):
    flat = jnp.concatenate([vals[n].reshape(-1).astype(F32) for n in SMALL] + [vals['loss'].reshape(-1)])
    rows = -(-flat.shape[0] // 1024) * 8
    return jnp.pad(flat, (0, rows * 128 - flat.shape[0])).reshape(rows, 128)


def _unpack_small(packed, shapes):
    flat = packed.reshape(-1)
    out, off = {}, 0
    for n in SMALL + ['loss']:
        size = int(np.prod(shapes[n]))
        out[n] = flat[off:off + size].reshape(shapes[n])
        off += size
    return out


def kernel(x, a_norm, a_wqkv, a_wo, kv_norm, w_kv, b_kv, b_norm, b_wq, b_bq, b_sinks, b_wo, b_bo, rel_bias, mlp_norm, mlp_up, mlp_down, final_norm, loss_target, m_a_norm, m_a_wqkv, m_a_wo, m_kv_norm, m_w_kv, m_b_kv, m_b_norm, m_b_wq, m_b_bq, m_b_sinks, m_b_wo, m_b_bo, m_rel_bias, m_mlp_norm, m_mlp_up, m_mlp_down, m_final_norm, v_a_norm, v_a_wqkv, v_a_wo, v_kv_norm, v_w_kv, v_b_kv, v_b_norm, v_b_wq, v_b_bq, v_b_sinks, v_b_wo, v_b_bo, v_rel_bias, v_mlp_norm, v_mlp_up, v_mlp_down, v_final_norm):
    w = dict(a_norm=a_norm, a_wqkv=a_wqkv, a_wo=a_wo, kv_norm=kv_norm, w_kv=w_kv, b_kv=b_kv, b_norm=b_norm,
             b_wq=b_wq, b_bq=b_bq, b_sinks=b_sinks, b_wo=b_wo, b_bo=b_bo, rel_bias=rel_bias, mlp_norm=mlp_norm,
             mlp_up=mlp_up, mlp_down=mlp_down, final_norm=final_norm)
    m = dict(a_norm=m_a_norm, a_wqkv=m_a_wqkv, a_wo=m_a_wo, kv_norm=m_kv_norm, w_kv=m_w_kv, b_kv=m_b_kv,
             b_norm=m_b_norm, b_wq=m_b_wq, b_bq=m_b_bq, b_sinks=m_b_sinks, b_wo=m_b_wo, b_bo=m_b_bo,
             rel_bias=m_rel_bias, mlp_norm=m_mlp_norm, mlp_up=m_mlp_up, mlp_down=m_mlp_down, final_norm=m_final_norm)
    v = dict(a_norm=v_a_norm, a_wqkv=v_a_wqkv, a_wo=v_a_wo, kv_norm=v_kv_norm, w_kv=v_w_kv, b_kv=v_b_kv,
             b_norm=v_b_norm, b_wq=v_b_wq, b_bq=v_b_bq, b_sinks=v_b_sinks, b_wo=v_b_wo, b_bo=v_b_bo,
             rel_bias=v_rel_bias, mlp_norm=v_mlp_norm, mlp_up=v_mlp_up, mlp_down=v_mlp_down, final_norm=v_final_norm)
    px, py, pc = _place()
    me = 4 * px + 2 * py + pc
    chip = (2 * px + py).astype(jnp.int32)
    core = pc.astype(jnp.int32)

    shards = {n: (w[n][None] if w[n].ndim == 2 else w[n]).astype(BF16) for n in BIG}
    an_pad = jnp.zeros((8, 128), F32).at[:a_norm.shape[0]].set(a_norm)
    names = BIG + ['a_norm']
    full = all_gather_weights(names, [shards[n] for n in BIG] + [an_pad],
                              [_full_shape(n, shards[n].shape) for n in BIG] + [(N_DEV, 8, 128)])
    full = dict(zip(names, full))
    wf = {n: _as_w4(n, full[n]) for n in BIG}
    n_a = a_norm.shape[0]
    small = {n: w[n] for n in SMALL}
    small['a_norm'] = full['a_norm'][:, :n_a].transpose(1, 0, 2).reshape(n_a, -1)

    gbufs = {n: jnp.zeros(wf[n].shape, BF16) for n in BIG}
    loss_b, grad_x, gb, sgrads = local_step(x[0], loss_target[0], wf, small, gbufs)

    gfull = [gb[n].reshape(full[n].shape) for n in BIG]
    part_shapes = [shards[n].shape for n in BIG]
    recv1 = sibling_exchange(BIG, gfull, part_shapes)
    parts = [sibling_sum(f"rs_sibling_sum_{n}", n in COL_SHARDED, g, r, core) for n, g, r in zip(BIG, gfull, recv1)]
    recv2 = chip_exchange(BIG, parts, part_shapes)
    out = {}
    for n, p, r in zip(BIG, parts, recv2):
        out[n] = reduce_adamw(f"adamw_{n}", p, r, chip, w[n], m[n], v[n])

    sgrads['loss'] = loss_b[0, :1]
    gathered = all_gather_rows(_pack_small(sgrads))
    shapes = {n: w[n].shape for n in SMALL}
    shapes['a_norm'] = (n_a, a_norm.shape[1] * N_DEV)
    shapes['loss'] = (1,)
    zeros1 = jnp.zeros((1,), F32)

    def packed(src):
        vals = {n: src[n] for n in SMALL}
        vals['a_norm'] = jnp.zeros(shapes['a_norm'], F32)
        vals['loss'] = zeros1
        return _pack_small(vals)

    sm = small_adamw("adamw_small", gathered, packed(w), packed(m), packed(v))
    sm = [_unpack_small(t, shapes) for t in sm]
    g_an = lax.dynamic_slice_in_dim(sm[0]['a_norm'], me * a_norm.shape[1], a_norm.shape[1], axis=1)
    pad = lambda t: jnp.zeros((8, 128), F32).at[:n_a].set(t)
    gathered_an = jnp.zeros((N_DEV, 8, 128), F32).at[0].set(pad(g_an))
    an = small_adamw("adamw_a_norm", gathered_an, pad(a_norm), pad(m_a_norm), pad(v_a_norm))
    for i in range(4):
        sm[i]['a_norm'] = an[i][:n_a]
    for n in BIG:
        for i in range(4):
            sm[i][n] = out[n][i]
    loss = sm[0]['loss'][0]
    return (loss, grad_x[None], *[sm[0][n] for n in WEIGHTS], *[sm[1][n] for n in WEIGHTS],
            *[sm[2][n] for n in WEIGHTS], *[sm[3][n] for n in WEIGHTS])
```

```python
import functools
import math

import numpy as np
import jax
import jax.numpy as jnp
from jax import lax
from jax.experimental import pallas as pl
from jax.experimental.pallas import tpu as pltpu

F32 = jnp.float32
BF16 = jnp.bfloat16
MESH = pl.DeviceIdType.MESH

N_DEV = 8
HEAD_DIM = 64
WINDOW = 128
N_BUCKETS = 32
EPS = 1e-5
NEG_INF = -1e30
Q_SCALE = 1.0 / math.sqrt(HEAD_DIM)

ADAM_LR, ADAM_B1, ADAM_B2, ADAM_EPS, ADAM_WD, ADAM_STEP = 0.001, 0.9, 0.999, 1e-08, 0.01, 10

SB_BQ = 512
SB_BK = 128
ROW_TILE = 512
VMEM_LIMIT = 56 * 1024 * 1024

WEIGHTS = ['a_norm', 'a_wqkv', 'a_wo', 'kv_norm', 'w_kv', 'b_kv', 'b_norm', 'b_wq', 'b_bq', 'b_sinks', 'b_wo',
           'b_bo', 'rel_bias', 'mlp_norm', 'mlp_up', 'mlp_down', 'final_norm']
BIG = ['a_wqkv', 'a_wo', 'w_kv', 'b_wq', 'b_wo', 'mlp_up', 'mlp_down']
COL_SHARDED = ('a_wqkv', 'mlp_up')
SMALL = ['a_norm', 'kv_norm', 'b_kv', 'b_norm', 'b_bq', 'b_sinks', 'b_bo', 'rel_bias', 'mlp_norm', 'final_norm']


def _params(sem=None):
    return pltpu.CompilerParams(dimension_semantics=sem, vmem_limit_bytes=VMEM_LIMIT)


def _pick(n, cands):
    for c in cands:
        if n % c == 0:
            return c
    raise ValueError(n)


def _tile(n, want):
    return n if n <= want else _pick(n, (want, want // 2, want // 4))


def mm_nn(name, a, w3, layer, epilogue, extras, out_dtypes):
    m, k = a.shape
    _, kw, n = w3.shape
    assert kw == k
    tm = _tile(m, 1024 if k <= 1024 else 512)
    tn = _tile(n, 1024)
    ne, no = len(extras), len(out_dtypes)

    def body(a_ref, w_ref, *rest):
        ex, outs = rest[:ne], rest[ne:ne + no]
        res = epilogue(jnp.dot(a_ref[...], w_ref[...], preferred_element_type=F32), *[e[...] for e in ex])
        for o, r in zip(outs, res):
            o[...] = r.astype(o.dtype)

    tile = pl.BlockSpec((tm, tn), lambda i, j: (i, j))
    ex_specs = [tile if e.shape[0] == m else pl.BlockSpec((1, tn), lambda i, j: (0, j)) for e in extras]
    return pl.pallas_call(
        body, name=name, grid=(m // tm, n // tn),
        in_specs=[pl.BlockSpec((tm, k), lambda i, j: (i, 0)),
                  pl.BlockSpec((None, k, tn), lambda i, j: (layer, 0, j))] + ex_specs,
        out_specs=[tile] * no,
        out_shape=[jax.ShapeDtypeStruct((m, n), d) for d in out_dtypes],
        compiler_params=_params(("parallel", "parallel")),
    )(a, w3, *extras)


def mm_nt(name, dy, w3, layer, epilogue, extras, out_dtypes):
    m, n = dy.shape
    _, k, nw = w3.shape
    assert nw == n
    tm = _tile(m, 1024 if n <= 1024 else 512)
    tko = _tile(k, 1024)
    ne, no = len(extras), len(out_dtypes)

    def body(a_ref, w_ref, *rest):
        ex, outs = rest[:ne], rest[ne:ne + no]
        acc = lax.dot_general(a_ref[...], w_ref[...], (((1,), (1,)), ((), ())), preferred_element_type=F32)
        res = epilogue(acc, *[e[...] for e in ex])
        for o, v in zip(outs, res):
            o[...] = v.astype(o.dtype)

    tile = pl.BlockSpec((tm, tko), lambda i, ko: (i, ko))
    return pl.pallas_call(
        body, name=name, grid=(m // tm, k // tko),
        in_specs=[pl.BlockSpec((tm, n), lambda i, ko: (i, 0)),
                  pl.BlockSpec((None, tko, n), lambda i, ko: (layer, ko, 0))] + [tile] * ne,
        out_specs=[tile] * no,
        out_shape=[jax.ShapeDtypeStruct((m, k), d) for d in out_dtypes],
        compiler_params=_params(("parallel", "parallel")),
    )(dy, w3, *extras)


def mm_tn(name, x, dy, gbuf, layer):
    s, k = x.shape
    _, kw, n = gbuf.shape
    assert kw == k and dy.shape == (s, n)
    tkk = _tile(k, 512)
    tn = _tile(n, 1024)

    def body(x_ref, dy_ref, g_in, g_out):
        del g_in
        g_out[...] = lax.dot_general(x_ref[...], dy_ref[...], (((0,), (0,)), ((), ())),
                                     preferred_element_type=F32).astype(g_out.dtype)

    return pl.pallas_call(
        body, name=name, grid=(k // tkk, n // tn),
        in_specs=[pl.BlockSpec((s, tkk), lambda ki, j: (0, ki)),
                  pl.BlockSpec((s, tn), lambda ki, j: (0, j)),
                  pl.BlockSpec(memory_space=pl.ANY)],
        out_specs=pl.BlockSpec((None, tkk, tn), lambda ki, j: (layer, ki, j)),
        out_shape=jax.ShapeDtypeStruct(gbuf.shape, gbuf.dtype),
        input_output_aliases={2: 0},
        compiler_params=_params(("parallel", "parallel")),
    )(x, dy, gbuf)


def rms_fwd(name, h, g):
    s, d = h.shape
    tr = _pick(s, (ROW_TILE, 256, 128))

    def body(h_ref, g_ref, o_ref):
        x = h_ref[...]
        r = lax.rsqrt(jnp.mean(x * x, axis=-1, keepdims=True) + EPS)
        o_ref[...] = (x * r * g_ref[...]).astype(o_ref.dtype)

    return pl.pallas_call(
        body, name=name, grid=(s // tr,),
        in_specs=[pl.BlockSpec((tr, d), lambda i: (i, 0)), pl.BlockSpec((1, d), lambda i: (0, 0))],
        out_specs=pl.BlockSpec((tr, d), lambda i: (i, 0)),
        out_shape=jax.ShapeDtypeStruct((s, d), BF16),
        compiler_params=_params(("parallel",)),
    )(h, g.reshape(1, d))


def rms_bwd(name, h, g, dn, dres):
    s, d = h.shape
    tr = _pick(s, (ROW_TILE, 256, 128))

    def body(h_ref, g_ref, dn_ref, dres_ref, dx_ref, dxb_ref, dg_ref, cs_ref):
        i = pl.program_id(0)
        x = h_ref[...]
        r = lax.rsqrt(jnp.mean(x * x, axis=-1, keepdims=True) + EPS)
        xh = x * r
        dn_ = dn_ref[...]
        dyg = dn_ * g_ref[...]
        dx = dres_ref[...] + r * (dyg - xh * jnp.mean(dyg * xh, axis=-1, keepdims=True))
        dx_ref[...] = dx
        dxb_ref[...] = dx.astype(BF16)

        @pl.when(i == 0)
        def _():
            dg_ref[...] = jnp.zeros_like(dg_ref)
            cs_ref[...] = jnp.zeros_like(cs_ref)

        dg_ref[...] += jnp.sum(dn_ * xh, axis=0, keepdims=True)
        cs_ref[...] += jnp.sum(dx, axis=0, keepdims=True)

    row = pl.BlockSpec((tr, d), lambda i: (i, 0))
    vec = pl.BlockSpec((1, d), lambda i: (0, 0))
    return pl.pallas_call(
        body, name=name, grid=(s // tr,),
        in_specs=[row, vec, row, row],
        out_specs=[row, row, vec, vec],
        out_shape=[jax.ShapeDtypeStruct((s, d), F32), jax.ShapeDtypeStruct((s, d), BF16),
                   jax.ShapeDtypeStruct((1, d), F32), jax.ShapeDtypeStruct((1, d), F32)],
        compiler_params=_params(("arbitrary",)),
    )(h, g.reshape(1, d), dn, dres)


def loss_head(h, g, target):
    s, d = h.shape
    tr = _pick(s, (ROW_TILE, 256, 128))

    def body(h_ref, g_ref, t_ref, dx_ref, dxb_ref, dg_ref, loss_ref):
        i = pl.program_id(0)
        x = h_ref[...]
        r = lax.rsqrt(jnp.mean(x * x, axis=-1, keepdims=True) + EPS)
        xh = x * r
        gw = g_ref[...]
        err = xh * gw - t_ref[...]
        dn_ = err * (1.0 / d)
        dyg = dn_ * gw
        dx = r * (dyg - xh * jnp.mean(dyg * xh, axis=-1, keepdims=True))
        dx_ref[...] = dx
        dxb_ref[...] = dx.astype(BF16)

        @pl.when(i == 0)
        def _():
            dg_ref[...] = jnp.zeros_like(dg_ref)
            loss_ref[...] = jnp.zeros_like(loss_ref)

        dg_ref[...] += jnp.sum(dn_ * xh, axis=0, keepdims=True)
        per_row = jnp.sum(err * err, axis=-1, keepdims=True) * (0.5 / d)
        loss_ref[...] += jnp.broadcast_to(jnp.sum(per_row, axis=0, keepdims=True), loss_ref.shape)

    row = pl.BlockSpec((tr, d), lambda i: (i, 0))
    vec = pl.BlockSpec((1, d), lambda i: (0, 0))
    return pl.pallas_call(
        body, name="loss_head", grid=(s // tr,),
        in_specs=[row, vec, row],
        out_specs=[row, row, vec, pl.BlockSpec((1, 128), lambda i: (0, 0))],
        out_shape=[jax.ShapeDtypeStruct((s, d), F32), jax.ShapeDtypeStruct((s, d), BF16),
                   jax.ShapeDtypeStruct((1, d), F32), jax.ShapeDtypeStruct((1, 128), F32)],
        compiler_params=_params(("arbitrary",)),
    )(h, g.reshape(1, d), target)


def colsum(name, x):
    s, n = x.shape
    tr = _pick(s, (ROW_TILE, 256, 128))

    def body(x_ref, o_ref):
        @pl.when(pl.program_id(0) == 0)
        def _():
            o_ref[...] = jnp.zeros_like(o_ref)

        o_ref[...] += jnp.sum(x_ref[...].astype(F32), axis=0, keepdims=True)

    return pl.pallas_call(
        body, name=name, grid=(s // tr,),
        in_specs=[pl.BlockSpec((tr, n), lambda i: (i, 0))],
        out_specs=pl.BlockSpec((1, n), lambda i: (0, 0)),
        out_shape=jax.ShapeDtypeStruct((1, n), F32),
        compiler_params=_params(("arbitrary",)),
    )(x)


def _tri_rows(reverse):
    i = np.arange(SB_BK)
    tri = (i[None, :] >= i[:, None]) if reverse else (i[None, :] <= i[:, None])
    tri = np.concatenate([tri, tri], axis=1)
    return jnp.asarray(np.concatenate([tri, np.ones((8, 2 * SB_BK), bool)], axis=0), BF16)


def _hi_lo_rows(x):
    hi = x.astype(BF16)
    lo = (x - hi.astype(F32)).astype(BF16)
    return jnp.concatenate([hi, lo], axis=0)


def _softplus(z):
    return jnp.maximum(z, 0.0) + jnp.log(1.0 + jnp.exp(-jnp.abs(z)))


def _pair_mask(first_rel_block, bq):
    key = lax.broadcasted_iota(jnp.int32, (2 * SB_BK, bq), 0) + first_rel_block * SB_BK
    qry = lax.broadcasted_iota(jnp.int32, (2 * SB_BK, bq), 1)
    return key < qry


def _row_of(table8, sub8, r):
    return jnp.sum(jnp.where(sub8 == r, table8, 0.0), axis=0, keepdims=True)


def sb_fwd(name, qt, k, vt):
    nh, dh, s = qt.shape
    bq = SB_BQ
    per_q = bq // SB_BK
    nkb = s // SB_BK
    assert s % bq == 0 and per_q % 2 == 0 and nkb % 8 == 0

    def body(q_ref, k_ref, v_ref, a_ref, o_ref, rtab_ref, acc):
        i = pl.program_id(1)
        qb = q_ref[...] * Q_SCALE
        tri = a_ref[...]
        sub8 = lax.broadcasted_iota(jnp.int32, (8, bq), 0)
        acc[...] = jnp.zeros_like(acc)
        rtab_ref[...] = jnp.zeros_like(rtab_ref)

        def pair(j0, run, rt8, mask):
            off = pl.multiple_of(j0 * SB_BK, 2 * SB_BK)
            z = jnp.dot(k_ref[pl.ds(off, 2 * SB_BK), :], qb, preferred_element_type=F32)
            p = _softplus(z)
            if mask is not None:
                p = jnp.where(mask, p, 0.0)
            cr1 = jnp.dot(tri, _hi_lo_rows(p[SB_BK:]), preferred_element_type=F32)
            cr0 = jnp.dot(tri, _hi_lo_rows(p[:SB_BK]), preferred_element_type=F32)
            run1 = run + cr1[SB_BK:SB_BK + 1]
            w = jnp.exp(jnp.concatenate([z[:SB_BK] - cr0[:SB_BK] - run1, z[SB_BK:] - cr1[:SB_BK] - run], axis=0))
            if mask is not None:
                w = jnp.where(mask, w, 0.0)
            acc[...] += jnp.dot(v_ref[:, pl.ds(off, 2 * SB_BK)], w.astype(BF16), preferred_element_type=F32)
            rt8 = jnp.where(sub8 == (j0 + 1) % 8, run, jnp.where(sub8 == j0 % 8, run1, rt8))
            rtab_ref[pl.ds(pl.multiple_of((j0 // 8) * 8, 8), 8), :] = rt8
            return run1 + cr0[SB_BK:SB_BK + 1], rt8

        carry = (jnp.zeros((1, bq), F32), jnp.zeros((8, bq), F32))
        for rel in range(per_q - 2, -1, -2):
            carry = pair(i * per_q + rel, *carry, _pair_mask(rel, bq))

        def step(it, c):
            return pair(i * per_q - 2 - 2 * it, *c, None)

        lax.fori_loop(0, i * (per_q // 2), step, carry)
        o_ref[...] = acc[...].astype(o_ref.dtype)

    qspec = pl.BlockSpec((None, dh, bq), lambda h, i: (h, 0, i))
    return pl.pallas_call(
        body, name=name, grid=(nh, s // bq),
        in_specs=[qspec, pl.BlockSpec((None, s, dh), lambda h, i: (h, 0, 0)),
                  pl.BlockSpec((None, dh, s), lambda h, i: (h, 0, 0)),
                  pl.BlockSpec((SB_BK + 8, 2 * SB_BK), lambda h, i: (0, 0))],
        out_specs=[qspec, pl.BlockSpec((None, nkb, bq), lambda h, i: (h, 0, i))],
        out_shape=[jax.ShapeDtypeStruct((nh, dh, s), BF16), jax.ShapeDtypeStruct((nh, nkb, s), F32)],
        scratch_shapes=[pltpu.VMEM((dh, bq), F32)],
        compiler_params=_params(("parallel", "arbitrary")),
    )(qt, k, vt, _tri_rows(True))


def sb_bwd(name, qt, q, k, kt, v, dot_, do, rtab):
    nh, dh, s = qt.shape
    bq = SB_BQ
    per_q = bq // SB_BK
    nkb = s // SB_BK

    def body(qt_ref, q_ref, k_ref, kt_ref, v_ref, dot_ref, do_ref, rtab_ref, ar_ref, af_ref,
             dq_ref, dk_ref, dv_ref, dq_acc):
        i = pl.program_id(1)

        @pl.when(i == 0)
        def _():
            dk_ref[...] = jnp.zeros_like(dk_ref)
            dv_ref[...] = jnp.zeros_like(dv_ref)

        qtb = qt_ref[...] * Q_SCALE
        qrows = q_ref[...] * Q_SCALE
        dotb = dot_ref[...]
        dorows = do_ref[...]
        tri_rev = ar_ref[...][:SB_BK]
        tri_fwd = af_ref[...]
        sub8 = lax.broadcasted_iota(jnp.int32, (8, bq), 0)
        dq_acc[...] = jnp.zeros_like(dq_acc)

        def pair(j0, g_run, mask):
            off = pl.multiple_of(j0 * SB_BK, 2 * SB_BK)
            kb = k_ref[pl.ds(off, 2 * SB_BK), :]
            z = jnp.dot(kb, qtb, preferred_element_type=F32)
            p_raw = _softplus(z)
            p = p_raw if mask is None else jnp.where(mask, p_raw, 0.0)
            c0 = jnp.dot(tri_rev, _hi_lo_rows(p[:SB_BK]), preferred_element_type=F32)
            c1 = jnp.dot(tri_rev, _hi_lo_rows(p[SB_BK:]), preferred_element_type=F32)
            rt8 = rtab_ref[pl.ds(pl.multiple_of((j0 // 8) * 8, 8), 8), :]
            r0 = _row_of(rt8, sub8, j0 % 8)
            r1 = _row_of(rt8, sub8, (j0 + 1) % 8)
            w = jnp.exp(jnp.concatenate([z[:SB_BK] - c0 - r0, z[SB_BK:] - c1 - r1], axis=0))
            if mask is not None:
                w = jnp.where(mask, w, 0.0)
            dw = jnp.dot(v_ref[pl.ds(off, 2 * SB_BK), :], dotb, preferred_element_type=F32)
            g = w * dw
            gg0 = jnp.dot(tri_fwd, _hi_lo_rows(g[:SB_BK]), preferred_element_type=F32)
            gg1 = jnp.dot(tri_fwd, _hi_lo_rows(g[SB_BK:]), preferred_element_type=F32)
            g_run1 = g_run + gg0[SB_BK:SB_BK + 1]
            g_pre = jnp.concatenate([gg0[:SB_BK] + g_run, gg1[:SB_BK] + g_run1], axis=0)
            dz = g - jnp.exp(z - p_raw) * g_pre
            if mask is not None:
                dz = jnp.where(mask, dz, 0.0)
            dzb = dz.astype(BF16)
            dq_acc[...] += jnp.dot(kt_ref[:, pl.ds(off, 2 * SB_BK)], dzb, preferred_element_type=F32)
            dk_ref[pl.ds(off, 2 * SB_BK), :] += jnp.dot(dzb, qrows, preferred_element_type=F32)
            dv_ref[pl.ds(off, 2 * SB_BK), :] += jnp.dot(w.astype(BF16), dorows, preferred_element_type=F32)
            return g_run1 + gg1[SB_BK:SB_BK + 1]

        g_run = lax.fori_loop(0, i * (per_q // 2), lambda it, c: pair(2 * it, c, None), jnp.zeros((1, bq), F32))
        for rel in range(0, per_q, 2):
            g_run = pair(i * per_q + rel, g_run, _pair_mask(rel, bq))
        dq_ref[...] = dq_acc[...] * Q_SCALE

    tspec = pl.BlockSpec((None, dh, bq), lambda h, i: (h, 0, i))
    rspec = pl.BlockSpec((None, bq, dh), lambda h, i: (h, i, 0))
    kspec = pl.BlockSpec((None, s, dh), lambda h, i: (h, 0, 0))
    aspec = pl.BlockSpec((SB_BK + 8, 2 * SB_BK), lambda h, i: (0, 0))
    return pl.pallas_call(
        body, name=name, grid=(nh, s // bq),
        in_specs=[tspec, rspec, kspec, pl.BlockSpec((None, dh, s), lambda h, i: (h, 0, 0)), kspec, tspec, rspec,
                  pl.BlockSpec((None, nkb, bq), lambda h, i: (h, 0, i)), aspec, aspec],
        out_specs=[tspec, kspec, kspec],
        out_shape=[jax.ShapeDtypeStruct((nh, dh, s), F32), jax.ShapeDtypeStruct((nh, s, dh), F32),
                   jax.ShapeDtypeStruct((nh, s, dh), F32)],
        scratch_shapes=[pltpu.VMEM((dh, bq), F32)],
        compiler_params=_params(("parallel", "arbitrary")),
    )(qt, q, k, kt, v, dot_, do, rtab, _tri_rows(True), _tri_rows(False))


def _swa_scores(q_ref, k_ref, bias_ref, sink_ref, i):
    rows = q_ref.shape[0]
    qb = q_ref[...] * Q_SCALE
    off = pl.multiple_of(i * WINDOW, WINDOW)
    kb = k_ref[pl.ds(off, 2 * WINDOW), :]
    sc = lax.dot_general(qb, kb, (((1,), (1,)), ((), ())), preferred_element_type=F32) + bias_ref[...]
    qi = lax.broadcasted_iota(jnp.int32, (rows, 2 * WINDOW), 0) & (WINDOW - 1)
    kj = lax.broadcasted_iota(jnp.int32, (rows, 2 * WINDOW), 1)
    dist = qi + WINDOW - kj
    valid = (dist >= 0) & (dist < WINDOW) & ((kj >= WINDOW) | (i > 0))
    sc = jnp.where(valid, sc, NEG_INF)
    sink = sink_ref[...]
    mx = jnp.maximum(jnp.max(sc, axis=1, keepdims=True), sink)
    p = jnp.exp(sc - mx)
    p_sink = jnp.exp(sink - mx)
    inv = 1.0 / (jnp.sum(p, axis=1, keepdims=True) + p_sink)
    return qb, kb, off, p, p_sink, inv


def swa_fwd(name, qg, kp, vp, bias, sink_col):
    ng, nb, rows, dh = qg.shape
    sp = kp.shape[1]

    def body(q_ref, k_ref, v_ref, bias_ref, sink_ref, o_ref):
        i = pl.program_id(1)
        _, _, off, p, _, inv = _swa_scores(q_ref, k_ref, bias_ref, sink_ref, i)
        vb = v_ref[pl.ds(off, 2 * WINDOW), :]
        o_ref[...] = (jnp.dot(p.astype(BF16), vb, preferred_element_type=F32) * inv).astype(o_ref.dtype)

    qspec = pl.BlockSpec((None, None, rows, dh), lambda g, i: (g, i, 0, 0))
    kspec = pl.BlockSpec((None, sp, dh), lambda g, i: (g, 0, 0))
    return pl.pallas_call(
        body, name=name, grid=(ng, nb),
        in_specs=[qspec, kspec, kspec, pl.BlockSpec((None, rows, 2 * WINDOW), lambda g, i: (g, 0, 0)),
                  pl.BlockSpec((None, rows, 1), lambda g, i: (g, 0, 0))],
        out_specs=qspec,
        out_shape=jax.ShapeDtypeStruct(qg.shape, BF16),
        compiler_params=_params(("parallel", "arbitrary")),
    )(qg, kp, vp, bias, sink_col)


def swa_bwd(name, qg, kp, vp, bias, sink_col, dog, dk_in, dv_in):
    ng, nb, rows, dh = qg.shape
    sp = kp.shape[1]

    def body(q_ref, k_ref, v_ref, bias_ref, sink_ref, do_ref, dki_ref, dvi_ref, dq_ref, dk_ref, dv_ref, db_ref, ds_ref):
        i = pl.program_id(1)

        @pl.when(i == 0)
        def _():
            dk_ref[...] = dki_ref[...]
            dv_ref[...] = dvi_ref[...]
            db_ref[...] = jnp.zeros_like(db_ref)
            ds_ref[...] = jnp.zeros_like(ds_ref)

        qb, kb, off, p, p_sink, inv = _swa_scores(q_ref, k_ref, bias_ref, sink_ref, i)
        p = p * inv
        dob = do_ref[...]
        vb = v_ref[pl.ds(off, 2 * WINDOW), :]
        dp = lax.dot_general(dob, vb, (((1,), (1,)), ((), ())), preferred_element_type=F32)
        delta = jnp.sum(p * dp, axis=1, keepdims=True)
        dsc = p * (dp - delta)
        ds_ref[...] -= p_sink * inv * delta
        db_ref[...] += dsc
        dscb = dsc.astype(BF16)
        dq_ref[...] = (jnp.dot(dscb, kb, preferred_element_type=F32) * Q_SCALE).astype(dq_ref.dtype)
        dk_ref[pl.ds(off, 2 * WINDOW), :] += lax.dot_general(dscb, qb, (((0,), (0,)), ((), ())),
                                                             preferred_element_type=F32)
        dv_ref[pl.ds(off, 2 * WINDOW), :] += lax.dot_general(p.astype(BF16), dob, (((0,), (0,)), ((), ())),
                                                             preferred_element_type=F32)

    qspec = pl.BlockSpec((None, None, rows, dh), lambda g, i: (g, i, 0, 0))
    kspec = pl.BlockSpec((None, sp, dh), lambda g, i: (g, 0, 0))
    bspec = pl.BlockSpec((None, rows, 2 * WINDOW), lambda g, i: (g, 0, 0))
    sspec = pl.BlockSpec((None, rows, 1), lambda g, i: (g, 0, 0))
    return pl.pallas_call(
        body, name=name, grid=(ng, nb),
        in_specs=[qspec, kspec, kspec, bspec, sspec, qspec, kspec, kspec],
        out_specs=[qspec, kspec, kspec, bspec, sspec],
        out_shape=[jax.ShapeDtypeStruct(qg.shape, BF16), jax.ShapeDtypeStruct(kp.shape, F32),
                   jax.ShapeDtypeStruct(kp.shape, F32), jax.ShapeDtypeStruct(bias.shape, F32),
                   jax.ShapeDtypeStruct(sink_col.shape, F32)],
        compiler_params=_params(("parallel", "arbitrary")),
    )(qg, kp, vp, bias, sink_col, dog, dk_in, dv_in)


def _bucket_onehot():
    qi = np.arange(WINDOW)[:, None]
    kj = np.arange(2 * WINDOW)[None, :]
    n = np.maximum(qi + WINDOW - kj, 0)
    max_exact = N_BUCKETS // 2
    nf = np.maximum(n, 1).astype(np.float64)
    val = np.log(nf / max_exact) / math.log(WINDOW / max_exact) * (N_BUCKETS - max_exact)
    assert np.all(np.abs(val - np.round(val))[(n > max_exact) & (n < WINDOW)] > 1e-3)
    large = np.minimum(max_exact + val.astype(np.int64), N_BUCKETS - 1)
    bucket = np.where(n < max_exact, n, large).reshape(-1)
    onehot = np.zeros((128, bucket.size), np.float32)
    onehot[bucket, np.arange(bucket.size)] = 1.0
    return onehot


def _split3(x):
    a = x.astype(BF16)
    r = x - a.astype(F32)
    b = r.astype(BF16)
    c = (r - b.astype(F32)).astype(BF16)
    return a, b, c


def bias_table(rel_bias):
    nh = rel_bias.shape[1]
    oh = jnp.asarray(_bucket_onehot(), BF16)
    n = oh.shape[1]
    tn = 4096
    rb = jnp.zeros((nh, 128), F32).at[:, :N_BUCKETS].set(rel_bias.T)

    def body(rb_ref, oh_ref, o_ref):
        o_ref[...] = sum(jnp.dot(t, oh_ref[...], preferred_element_type=F32) for t in _split3(rb_ref[...]))

    return pl.pallas_call(
        body, name="bias_table", grid=(n // tn,),
        in_specs=[pl.BlockSpec((nh, 128), lambda i: (0, 0)), pl.BlockSpec((128, tn), lambda i: (0, i))],
        out_specs=pl.BlockSpec((nh, tn), lambda i: (0, i)),
        out_shape=jax.ShapeDtypeStruct((nh, n), F32),
        compiler_params=_params(("parallel",)),
    )(rb, oh)


def bias_table_grad(db0, db1):
    nh, n = db0.shape
    oh = jnp.asarray(_bucket_onehot(), BF16)
    tn = 4096

    def body(a_ref, b_ref, oh_ref, o_ref):
        @pl.when(pl.program_id(0) == 0)
        def _():
            o_ref[...] = jnp.zeros_like(o_ref)

        o_ref[...] += sum(lax.dot_general(t, oh_ref[...], (((1,), (1,)), ((), ())), preferred_element_type=F32)
                          for t in _split3(a_ref[...] + b_ref[...]))

    blk = pl.BlockSpec((nh, tn), lambda i: (0, i))
    return pl.pallas_call(
        body, name="bias_table_grad", grid=(n // tn,),
        in_specs=[blk, blk, pl.BlockSpec((128, tn), lambda i: (0, i))],
        out_specs=pl.BlockSpec((nh, 128), lambda i: (0, 0)),
        out_shape=jax.ShapeDtypeStruct((nh, 128), F32),
        compiler_params=_params(("arbitrary",)),
    )(db0, db1, oh)


def _owner_view(ref, name, d):
    if name == 'a_norm':
        return ref.at[d]
    if name in COL_SHARDED:
        n = ref.shape[2] // N_DEV
        return ref.at[:, :, pl.ds(pl.multiple_of(d * n, 128), n)]
    return ref.at[:, d]


def _place():
    return lax.axis_index("x"), lax.axis_index("y"), lax.axis_index("c")


def all_gather_weights(names, shards, full_shapes):
    n = len(names)

    def body(*refs):
        ins, outs = refs[:n], refs[n:2 * n]
        send_sems, recv_sems, local_sems = refs[2 * n:]
        x, y, c = _place()
        me, sibling = (x, y, c), (x, y, 1 - c)
        chips = [(1 - x, y), (x, 1 - y), (1 - x, 1 - y)]

        def dev(p):
            return 4 * p[0] + 2 * p[1] + p[2]

        def copy(t, k, block, to, src=None):
            dst = _owner_view(outs[t], names[t], dev(block))
            return pltpu.make_async_remote_copy(
                src_ref=dst if src is None else src, dst_ref=dst,
                send_sem=send_sems.at[t, k], recv_sem=recv_sems.at[t, k], device_id=to, device_id_type=MESH)

        mine = [pltpu.make_async_copy(ins[t], _owner_view(outs[t], names[t], dev(me)), local_sems.at[t])
                for t in range(n)]
        for cp in mine:
            cp.start()
        first = []
        for t in range(n):
            first.append(copy(t, 0, me, sibling, src=ins[t]))
            first += [copy(t, 1 + j, me, (*chip, c), src=ins[t]) for j, chip in enumerate(chips)]
        for cp in first:
            cp.start()
        passed = []
        for j, chip in enumerate(chips):
            for t in range(n):
                copy(t, 1 + j, (*chip, c), me).wait_recv()
                fwd = copy(t, 4 + j, (*chip, c), sibling)
                fwd.start()
                passed.append(fwd)
        for t in range(n):
            copy(t, 0, sibling, me).wait_recv()
            for j, chip in enumerate(chips):
                copy(t, 4 + j, (*chip, 1 - c), me).wait_recv()
        for cp in first + passed:
            cp.wait_send()
        for cp in mine:
            cp.wait()

    hbm = pl.BlockSpec(memory_space=pl.ANY)
    return pl.pallas_call(
        body, name="all_gather_weights",
        in_specs=[hbm] * n, out_specs=[hbm] * n,
        out_shape=[jax.ShapeDtypeStruct(full_shapes[t], shards[t].dtype) for t in range(n)],
        scratch_shapes=[pltpu.SemaphoreType.DMA((n, 7)), pltpu.SemaphoreType.DMA((n, 7)),
                        pltpu.SemaphoreType.DMA((n,))],
    )(*shards)


def sibling_exchange(names, grads, part_shapes):
    n = len(names)

    def body(*refs):
        ins, outs = refs[:n], refs[n:2 * n]
        send_sems, recv_sems = refs[2 * n:]
        x, y, c = _place()
        sibling = (x, y, 1 - c)
        copies = []
        for t in range(n):
            for q in range(4):
                copies.append(pltpu.make_async_remote_copy(
                    src_ref=_owner_view(ins[t], names[t], 2 * q + 1 - c), dst_ref=outs[t].at[q],
                    send_sem=send_sems.at[t, q], recv_sem=recv_sems.at[t, q], device_id=sibling, device_id_type=MESH))
        for cp in copies:
            cp.start()
        for cp in copies:
            cp.wait()

    hbm = pl.BlockSpec(memory_space=pl.ANY)
    return pl.pallas_call(
        body, name="rs_sibling_exchange",
        in_specs=[hbm] * n, out_specs=[hbm] * n,
        out_shape=[jax.ShapeDtypeStruct((4,) + part_shapes[t], BF16) for t in range(n)],
        scratch_shapes=[pltpu.SemaphoreType.DMA((n, 4)), pltpu.SemaphoreType.DMA((n, 4))],
    )(*grads)


def chip_exchange(names, parts, part_shapes):
    n = len(names)

    def body(*refs):
        ins, outs = refs[:n], refs[n:2 * n]
        send_sems, recv_sems = refs[2 * n:]
        x, y, c = _place()
        chips = [(1 - x, y), (x, 1 - y), (1 - x, 1 - y)]
        copies = []
        for t in range(n):
            for k, chip in enumerate(chips):
                copies.append(pltpu.make_async_remote_copy(
                    src_ref=ins[t].at[2 * chip[0] + chip[1]], dst_ref=outs[t].at[k],
                    send_sem=send_sems.at[t, k], recv_sem=recv_sems.at[t, k], device_id=(*chip, c),
                    device_id_type=MESH))
        for cp in copies:
            cp.start()
        for cp in copies:
            cp.wait()

    hbm = pl.BlockSpec(memory_space=pl.ANY)
    return pl.pallas_call(
        body, name="rs_chip_exchange",
        in_specs=[hbm] * n, out_specs=[hbm] * n,
        out_shape=[jax.ShapeDtypeStruct((3,) + part_shapes[t], BF16) for t in range(n)],
        scratch_shapes=[pltpu.SemaphoreType.DMA((n, 3)), pltpu.SemaphoreType.DMA((n, 3))],
    )(*parts)


def all_gather_rows(x):
    r, w = x.shape

    def body(x_ref, out_ref, send_sems, recv_sems, local_sem):
        px, py, pc = _place()
        me = 4 * px + 2 * py + pc
        mine = pltpu.make_async_copy(x_ref, out_ref.at[me], local_sem)
        mine.start()
        copies = []
        for k in range(1, N_DEV):
            peer = (px ^ (k >> 2), py ^ ((k >> 1) & 1), pc ^ (k & 1))
            copies.append(pltpu.make_async_remote_copy(
                src_ref=x_ref, dst_ref=out_ref.at[me], send_sem=send_sems.at[k - 1], recv_sem=recv_sems.at[k - 1],
                device_id=peer, device_id_type=MESH))
        for cp in copies:
            cp.start()
        for k in range(1, N_DEV):
            peer_idx = me ^ k
            pltpu.make_async_remote_copy(
                src_ref=x_ref, dst_ref=out_ref.at[peer_idx], send_sem=send_sems.at[k - 1],
                recv_sem=recv_sems.at[k - 1], device_id=(px, py, pc), device_id_type=MESH).wait_recv()
        for cp in copies:
            cp.wait_send()
        mine.wait()

    vmem = pl.BlockSpec(memory_space=pltpu.VMEM)
    return pl.pallas_call(
        body, name="all_gather_small_grads",
        in_specs=[vmem], out_specs=vmem,
        out_shape=jax.ShapeDtypeStruct((N_DEV, r, w), x.dtype),
        scratch_shapes=[pltpu.SemaphoreType.DMA((N_DEV - 1,)), pltpu.SemaphoreType.DMA((N_DEV - 1,)),
                        pltpu.SemaphoreType.DMA],
    )(x)


def _adamw(w, g, m, v):
    m = ADAM_B1 * m + (1.0 - ADAM_B1) * g
    v = ADAM_B2 * v + (1.0 - ADAM_B2) * (g * g)
    m_hat = m / (1.0 - ADAM_B1 ** ADAM_STEP)
    v_hat = v / (1.0 - ADAM_B2 ** ADAM_STEP)
    return -ADAM_LR * (m_hat / (jnp.sqrt(v_hat) + ADAM_EPS) + ADAM_WD * w), m, v


def _as_rows(a, lead):
    return a.reshape(a.shape[:lead] + (-1, a.shape[-1]))


def sibling_sum(name, col, grads, recv, core):
    _, nl, rows, cols = recv.shape
    tr = _tile(rows, 512)
    rspec = pl.BlockSpec((None, None, tr, cols), lambda q, l, i, c_ref: (q, l, i, 0))
    if col:
        gspec = pl.BlockSpec((None, tr, cols), lambda q, l, i, c_ref: (l, i, 2 * q + c_ref[0]))
    else:
        gspec = pl.BlockSpec((None, None, tr, cols), lambda q, l, i, c_ref: (l, 2 * q + c_ref[0], i, 0))

    def body(c_ref, g_ref, r_ref, o_ref):
        del c_ref
        o_ref[...] = (g_ref[...].astype(F32) + r_ref[...].astype(F32)).astype(BF16)

    return pl.pallas_call(
        body, name=name,
        grid_spec=pltpu.PrefetchScalarGridSpec(num_scalar_prefetch=1, grid=(4, nl, rows // tr),
                                               in_specs=[gspec, rspec], out_specs=rspec),
        out_shape=jax.ShapeDtypeStruct(recv.shape, BF16),
        compiler_params=_params(("parallel", "parallel", "parallel")),
    )(core.reshape(1), grads, recv)


def reduce_adamw(name, parts, recv, chip, w, m, v):
    shape = w.shape
    w2, m2, v2 = (_as_rows(a, 0) for a in (w, m, v))
    rows, cols = w2.shape
    p3 = parts.reshape(4, rows, cols)
    r3 = recv.reshape(3, rows, cols)
    tr = _pick(rows, (256, 128))

    def body(q_ref, p_ref, r_ref, w_ref, m_ref, v_ref, g_out, d_out, m_out, v_out):
        del q_ref
        g = ((p_ref[...].astype(F32) + r_ref[0].astype(F32)) + r_ref[1].astype(F32)) + r_ref[2].astype(F32)
        d, mn, vn = _adamw(w_ref[...], g, m_ref[...], v_ref[...])
        g_out[...] = g
        d_out[...] = d
        m_out[...] = mn
        v_out[...] = vn

    blk = pl.BlockSpec((tr, cols), lambda i, q_ref: (i, 0))
    outs = pl.pallas_call(
        body, name=name,
        grid_spec=pltpu.PrefetchScalarGridSpec(
            num_scalar_prefetch=1, grid=(rows // tr,),
            in_specs=[pl.BlockSpec((None, tr, cols), lambda i, q_ref: (q_ref[0], i, 0)),
                      pl.BlockSpec((3, tr, cols), lambda i, q_ref: (0, i, 0)), blk, blk, blk],
            out_specs=[blk] * 4),
        out_shape=[jax.ShapeDtypeStruct((rows, cols), F32)] * 4,
        compiler_params=_params(("parallel",)),
    )(chip.reshape(1), p3, r3, w2, m2, v2)
    return [o.reshape(shape) for o in outs]


def small_adamw(name, gathered, w, m, v):
    _, r, c = gathered.shape

    def body(ga_ref, w_ref, m_ref, v_ref, g_out, d_out, m_out, v_out):
        g = ga_ref[0]
        for d in range(1, N_DEV):
            g = g + ga_ref[d]
        dl, mn, vn = _adamw(w_ref[...], g, m_ref[...], v_ref[...])
        g_out[...] = g
        d_out[...] = dl
        m_out[...] = mn
        v_out[...] = vn

    return pl.pallas_call(
        body, name=name,
        out_shape=[jax.ShapeDtypeStruct((r, c), F32)] * 4,
        compiler_params=_params(),
    )(gathered, w, m, v)


def _heads(t, nh):
    s = t.shape[0]
    return t.reshape(s, nh, HEAD_DIM).transpose(1, 0, 2)


def _heads_t(t, nh):
    return t.T.reshape(nh, HEAD_DIM, t.shape[0])


def _unheads(t):
    nh, s, dh = t.shape
    return t.transpose(1, 0, 2).reshape(s, nh * dh)


def _group_q(t, nb):
    s = t.shape[0]
    return t.reshape(nb, WINDOW, 2, 8, HEAD_DIM).transpose(2, 0, 3, 1, 4).reshape(2, nb, 8 * WINDOW, HEAD_DIM)


def _ungroup_q(t):
    ng, nb, _, dh = t.shape
    return t.reshape(ng, nb, 8, WINDOW, dh).transpose(1, 3, 0, 2, 4).reshape(nb * WINDOW, ng * 8 * dh)


def _front_pad(t):
    return jnp.pad(t, ((0, 0), (WINDOW, 0), (0, 0)))


def _add(acc, *ex):
    return (acc + ex[0],)


def local_step(x, target, wf, small, gbufs):
    s, d = x.shape
    nb = s // WINDOW
    n_a, n_b = small['a_norm'].shape[0], small['b_norm'].shape[0]
    sg = {}
    gb = dict(gbufs)

    bias_flat = bias_table(small['rel_bias'])
    bias_g = bias_flat.reshape(2, 8 * WINDOW, 2 * WINDOW)
    sink_cols = [jnp.repeat(small['b_sinks'][j], WINDOW).reshape(2, 8 * WINDOW, 1) for j in range(n_b)]

    def mlp_fwd(h, layer):
        n2 = rms_fwd(f"mlp_norm_fwd{layer}", h, small['mlp_norm'][layer])
        u, a = mm_nn(f"mlp_up_fwd{layer}", n2, wf['mlp_up'], layer,
                     lambda acc: (acc, jnp.square(jnp.maximum(acc, 0.0))), (), (BF16, BF16))
        (h2,) = mm_nn(f"mlp_down_fwd{layer}", a, wf['mlp_down'], layer, _add, (h,), (F32,))
        return h2, (n2, u, a)

    h = x
    saved = []
    for l in range(n_a):
        n1 = rms_fwd(f"a_norm_fwd{l}", h, small['a_norm'][l])
        (qkv,) = mm_nn(f"a_qkv_fwd{l}", n1, wf['a_wqkv'], l, lambda acc: (acc,), (), (BF16,))
        nh = d // HEAD_DIM
        o_t, rtab = sb_fwd(f"sb_fwd{l}", _heads_t(qkv[:, :d], nh), _heads(qkv[:, d:2 * d], nh),
                           _heads_t(qkv[:, 2 * d:], nh))
        o = o_t.reshape(d, s).T
        (h_mid,) = mm_nn(f"a_wo_fwd{l}", o, wf['a_wo'], l, _add, (h,), (F32,))
        h_out, mlp_saved = mlp_fwd(h_mid, l)
        saved.append((h, n1, qkv, o, rtab, h_mid, mlp_saved))
        h = h_out
    h_kv = h
    nkv = rms_fwd("kv_norm_fwd", h, small['kv_norm'])
    (kv,) = mm_nn("kv_fwd", nkv, wf['w_kv'], 0, lambda acc, b: (acc + b,), (small['b_kv'].reshape(1, -1),), (BF16,))
    kvw = kv.shape[1] // 2
    kp = _front_pad(_heads(kv[:, :kvw], 2))
    vp = _front_pad(_heads(kv[:, kvw:], 2))
    for j in range(n_b):
        layer = n_a + j
        n1 = rms_fwd(f"b_norm_fwd{j}", h, small['b_norm'][j])
        (qb,) = mm_nn(f"b_q_fwd{j}", n1, wf['b_wq'], j, lambda acc, b: (acc + b,),
                      (small['b_bq'][j].reshape(1, -1),), (BF16,))
        qg = _group_q(qb, nb)
        og = swa_fwd(f"swa_fwd{j}", qg, kp, vp, bias_g, sink_cols[j])
        o = _ungroup_q(og)
        (h_mid,) = mm_nn(f"b_wo_fwd{j}", o, wf['b_wo'], j, lambda acc, hh, b: (acc + hh + b,),
                         (h, small['b_bo'][j].reshape(1, -1)), (F32,))
        h_out, mlp_saved = mlp_fwd(h_mid, layer)
        saved.append((h, n1, qg, o, h_mid, mlp_saved))
        h = h_out

    dh, dhb, dg_final, loss_b = loss_head(h, small['final_norm'], target)
    sg['final_norm'] = dg_final[0]
    sg['mlp_norm'] = [None] * (n_a + n_b)
    cs_mid = None

    def mlp_bwd(dh, dhb, h_mid, mlp_saved, layer):
        n2, u, a = mlp_saved
        (du,) = mm_nt(f"mlp_down_dx{layer}", dhb, wf['mlp_down'], layer,
                      lambda acc, uu: (acc * (2.0 * jnp.maximum(uu.astype(F32), 0.0)),), (u,), (BF16,))
        gb['mlp_down'] = mm_tn(f"mlp_down_dw{layer}", a, dhb, gb['mlp_down'], layer)
        (dn2,) = mm_nt(f"mlp_up_dx{layer}", du, wf['mlp_up'], layer, lambda acc: (acc,), (), (F32,))
        gb['mlp_up'] = mm_tn(f"mlp_up_dw{layer}", n2, du, gb['mlp_up'], layer)
        dh2, dh2b, dg, cs = rms_bwd(f"mlp_norm_bwd{layer}", h_mid, small['mlp_norm'][layer], dn2, dh)
        sg['mlp_norm'][layer] = dg[0]
        return dh2, dh2b, cs

    dkp = jnp.zeros(kp.shape, F32)
    dvp = jnp.zeros(vp.shape, F32)
    sg['b_norm'], sg['b_bq'], sg['b_bo'], sg['b_sinks'] = [None] * n_b, [None] * n_b, [None] * n_b, [None] * n_b
    dbias = [None] * n_b
    for j in reversed(range(n_b)):
        layer = n_a + j
        h_in, n1, qg, o, h_mid, mlp_saved = saved[layer]
        dh, dhb, cs = mlp_bwd(dh, dhb, h_mid, mlp_saved, layer)
        sg['b_bo'][j] = cs[0]
        (do,) = mm_nt(f"b_wo_dx{j}", dhb, wf['b_wo'], j, lambda acc: (acc,), (), (BF16,))
        gb['b_wo'] = mm_tn(f"b_wo_dw{j}", o, dhb, gb['b_wo'], j)
        dqg, dkp, dvp, dbias[j], dsink = swa_bwd(f"swa_bwd{j}", qg, kp, vp, bias_g, sink_cols[j],
                                                 _group_q(do, nb), dkp, dvp)
        sg['b_sinks'][j] = colsum(f"sink_grad{j}", dsink.reshape(16, WINDOW).T)[0]
        dq = _ungroup_q(dqg)
        sg['b_bq'][j] = colsum(f"b_bq_grad{j}", dq)[0]
        (dn1,) = mm_nt(f"b_q_dx{j}", dq, wf['b_wq'], j, lambda acc: (acc,), (), (F32,))
        gb['b_wq'] = mm_tn(f"b_q_dw{j}", n1, dq, gb['b_wq'], j)
        dh, dhb, dg, _ = rms_bwd(f"b_norm_bwd{j}", h_in, small['b_norm'][j], dn1, dh)
        sg['b_norm'][j] = dg[0]
    sg['rel_bias'] = bias_table_grad(dbias[0].reshape(bias_flat.shape),
                                     dbias[1].reshape(bias_flat.shape))[:, :N_BUCKETS].T

    dkv = jnp.concatenate([_unheads(dkp[:, WINDOW:]), _unheads(dvp[:, WINDOW:])], axis=1)
    sg['b_kv'] = colsum("b_kv_grad", dkv)[0]
    dkvb = dkv.astype(BF16)
    (dnkv,) = mm_nt("kv_dx", dkvb, wf['w_kv'], 0, lambda acc: (acc,), (), (F32,))
    gb['w_kv'] = mm_tn("kv_dw", nkv, dkvb, gb['w_kv'], 0)
    dh, dhb, dg, _ = rms_bwd("kv_norm_bwd", h_kv, small['kv_norm'], dnkv, dh)
    sg['kv_norm'] = dg[0]

    sg['a_norm'] = [None] * n_a
    for l in reversed(range(n_a)):
        h_in, n1, qkv, o, rtab, h_mid, mlp_saved = saved[l]
        dh, dhb, _ = mlp_bwd(dh, dhb, h_mid, mlp_saved, l)
        (do,) = mm_nt(f"a_wo_dx{l}", dhb, wf['a_wo'], l, lambda acc: (acc,), (), (BF16,))
        gb['a_wo'] = mm_tn(f"a_wo_dw{l}", o, dhb, gb['a_wo'], l)
        nh = d // HEAD_DIM
        qs, ks, vs = qkv[:, :d], qkv[:, d:2 * d], qkv[:, 2 * d:]
        dq_t, dk, dv = sb_bwd(f"sb_bwd{l}", _heads_t(qs, nh), _heads(qs, nh), _heads(ks, nh), _heads_t(ks, nh),
                              _heads(vs, nh), _heads_t(do, nh), _heads(do, nh), rtab)
        dqkv = jnp.concatenate([dq_t.reshape(d, s).T, _unheads(dk), _unheads(dv)], axis=1).astype(BF16)
        (dn1,) = mm_nt(f"a_qkv_dx{l}", dqkv, wf['a_wqkv'], l, lambda acc: (acc,), (), (F32,))
        gb['a_wqkv'] = mm_tn(f"a_qkv_dw{l}", n1, dqkv, gb['a_wqkv'], l)
        dh, dhb, dg, _ = rms_bwd(f"a_norm_bwd{l}", h_in, small['a_norm'][l], dn1, dh)
        sg['a_norm'][l] = dg[0]

    small_grads = {
        'a_norm': jnp.stack(sg['a_norm']), 'kv_norm': sg['kv_norm'], 'b_kv': sg['b_kv'],
        'b_norm': jnp.stack(sg['b_norm']), 'b_bq': jnp.stack(sg['b_bq']), 'b_sinks': jnp.stack(sg['b_sinks']),
        'b_bo': jnp.stack(sg['b_bo']), 'rel_bias': sg['rel_bias'], 'mlp_norm': jnp.stack(sg['mlp_norm']),
        'final_norm': sg['final_norm'],
    }
    return loss_b, dh, gb, small_grads


def _full_shape(name, shard_shape):
    if name in COL_SHARDED:
        return shard_shape[:2] + (N_DEV * shard_shape[2],)
    nl, r, n = shard_shape
    return (nl, N_DEV, r, n)


def _as_w3(name, full):
    if name in COL_SHARDED:
        return full
    nl, nd, r, n = full.shape
    return full.reshape(nl, nd * r, n)


def _pack_small(vals):
    flat = jnp.concatenate([vals[n].reshape(-1).astype(F32) for n in SMALL] + [vals['loss'].reshape(-1)])
    rows = -(-flat.shape[0] // 1024) * 8
    return jnp.pad(flat, (0, rows * 128 - flat.shape[0])).reshape(rows, 128)


def _unpack_small(packed, shapes):
    flat = packed.reshape(-1)
    out, off = {}, 0
    for n in SMALL + ['loss']:
        size = int(np.prod(shapes[n]))
        out[n] = flat[off:off + size].reshape(shapes[n])
        off += size
    return out


def kernel(x, a_norm, a_wqkv, a_wo, kv_norm, w_kv, b_kv, b_norm, b_wq, b_bq, b_sinks, b_wo, b_bo, rel_bias, mlp_norm, mlp_up, mlp_down, final_norm, loss_target, m_a_norm, m_a_wqkv, m_a_wo, m_kv_norm, m_w_kv, m_b_kv, m_b_norm, m_b_wq, m_b_bq, m_b_sinks, m_b_wo, m_b_bo, m_rel_bias, m_mlp_norm, m_mlp_up, m_mlp_down, m_final_norm, v_a_norm, v_a_wqkv, v_a_wo, v_kv_norm, v_w_kv, v_b_kv, v_b_norm, v_b_wq, v_b_bq, v_b_sinks, v_b_wo, v_b_bo, v_rel_bias, v_mlp_norm, v_mlp_up, v_mlp_down, v_final_norm):
    w = dict(a_norm=a_norm, a_wqkv=a_wqkv, a_wo=a_wo, kv_norm=kv_norm, w_kv=w_kv, b_kv=b_kv, b_norm=b_norm,
             b_wq=b_wq, b_bq=b_bq, b_sinks=b_sinks, b_wo=b_wo, b_bo=b_bo, rel_bias=rel_bias, mlp_norm=mlp_norm,
             mlp_up=mlp_up, mlp_down=mlp_down, final_norm=final_norm)
    m = dict(a_norm=m_a_norm, a_wqkv=m_a_wqkv, a_wo=m_a_wo, kv_norm=m_kv_norm, w_kv=m_w_kv, b_kv=m_b_kv,
             b_norm=m_b_norm, b_wq=m_b_wq, b_bq=m_b_bq, b_sinks=m_b_sinks, b_wo=m_b_wo, b_bo=m_b_bo,
             rel_bias=m_rel_bias, mlp_norm=m_mlp_norm, mlp_up=m_mlp_up, mlp_down=m_mlp_down, final_norm=m_final_norm)
    v = dict(a_norm=v_a_norm, a_wqkv=v_a_wqkv, a_wo=v_a_wo, kv_norm=v_kv_norm, w_kv=v_w_kv, b_kv=v_b_kv,
             b_norm=v_b_norm, b_wq=v_b_wq, b_bq=v_b_bq, b_sinks=v_b_sinks, b_wo=v_b_wo, b_bo=v_b_bo,
             rel_bias=v_rel_bias, mlp_norm=v_mlp_norm, mlp_up=v_mlp_up, mlp_down=v_mlp_down, final_norm=v_final_norm)
    px, py, pc = _place()
    me = 4 * px + 2 * py + pc
    chip = (2 * px + py).astype(jnp.int32)
    core = pc.astype(jnp.int32)

    shards = {n: (w[n][None] if w[n].ndim == 2 else w[n]).astype(BF16) for n in BIG}
    an_pad = jnp.zeros((8, 128), F32).at[:a_norm.shape[0]].set(a_norm)
    names = BIG + ['a_norm']
    full = all_gather_weights(names, [shards[n] for n in BIG] + [an_pad],
                              [_full_shape(n, shards[n].shape) for n in BIG] + [(N_DEV, 8, 128)])
    full = dict(zip(names, full))
    wf = {n: _as_w3(n, full[n]) for n in BIG}
    n_a = a_norm.shape[0]
    small = {n: w[n] for n in SMALL}
    small['a_norm'] = full['a_norm'][:, :n_a].transpose(1, 0, 2).reshape(n_a, -1)

    gbufs = {n: jnp.zeros(wf[n].shape, BF16) for n in BIG}
    loss_b, grad_x, gb, sgrads = local_step(x[0], loss_target[0], wf, small, gbufs)

    gfull = [gb[n].reshape(full[n].shape) for n in BIG]
    part_shapes = [shards[n].shape for n in BIG]
    recv1 = sibling_exchange(BIG, gfull, part_shapes)
    parts = [sibling_sum(f"rs_sibling_sum_{n}", n in COL_SHARDED, g, r, core) for n, g, r in zip(BIG, gfull, recv1)]
    recv2 = chip_exchange(BIG, parts, part_shapes)
    out = {}
    for n, p, r in zip(BIG, parts, recv2):
        out[n] = reduce_adamw(f"adamw_{n}", p, r, chip, w[n], m[n], v[n])

    sgrads['loss'] = loss_b[0, :1]
    gathered = all_gather_rows(_pack_small(sgrads))
    shapes = {n: w[n].shape for n in SMALL}
    shapes['a_norm'] = (n_a, a_norm.shape[1] * N_DEV)
    shapes['loss'] = (1,)
    zeros1 = jnp.zeros((1,), F32)

    def packed(src):
        vals = {n: src[n] for n in SMALL}
        vals['a_norm'] = jnp.zeros(shapes['a_norm'], F32)
        vals['loss'] = zeros1
        return _pack_small(vals)

    sm = small_adamw("adamw_small", gathered, packed(w), packed(m), packed(v))
    sm = [_unpack_small(t, shapes) for t in sm]
    g_an = lax.dynamic_slice_in_dim(sm[0]['a_norm'], me * a_norm.shape[1], a_norm.shape[1], axis=1)
    pad = lambda t: jnp.zeros((8, 128), F32).at[:n_a].set(t)
    gathered_an = jnp.zeros((N_DEV, 8, 128), F32).at[0].set(pad(g_an))
    an = small_adamw("adamw_a_norm", gathered_an, pad(a_norm), pad(m_a_norm), pad(v_a_norm))
    for i in range(4):
        sm[i]['a_norm'] = an[i][:n_a]
    for n in BIG:
        for i in range(4):
            sm[i][n] = out[n][i]
    loss = sm[0]['loss'][0]
    return (loss, grad_x[None], *[sm[0][n] for n in WEIGHTS], *[sm[1][n] for n in WEIGHTS],
            *[sm[2][n] for n in WEIGHTS], *[sm[3][n] for n in WEIGHTS])
```

```python
import functools
import math

import numpy as np
import jax
import jax.numpy as jnp
from jax import lax
from jax.experimental import pallas as pl
from jax.experimental.pallas import tpu as pltpu

F32 = jnp.float32
BF16 = jnp.bfloat16
MESH = pl.DeviceIdType.MESH

N_DEV = 8
HEAD_DIM = 64
WINDOW = 128
N_BUCKETS = 32
EPS = 1e-5
NEG_INF = -1e30
Q_SCALE = 1.0 / math.sqrt(HEAD_DIM)
LOG2E = 1.4426950408889634

ADAM_LR, ADAM_B1, ADAM_B2, ADAM_EPS, ADAM_WD, ADAM_STEP = 0.001, 0.9, 0.999, 1e-08, 0.01, 10

SB_BQ = 512
SB_BK = 128
ROW_TILE = 512
VMEM_LIMIT = 56 * 1024 * 1024

WEIGHTS = ['a_norm', 'a_wqkv', 'a_wo', 'kv_norm', 'w_kv', 'b_kv', 'b_norm', 'b_wq', 'b_bq', 'b_sinks', 'b_wo',
           'b_bo', 'rel_bias', 'mlp_norm', 'mlp_up', 'mlp_down', 'final_norm']
BIG = ['a_wqkv', 'a_wo', 'w_kv', 'b_wq', 'b_wo', 'mlp_up', 'mlp_down']
COL_SHARDED = ('a_wqkv', 'mlp_up')
SMALL = ['a_norm', 'kv_norm', 'b_kv', 'b_norm', 'b_bq', 'b_sinks', 'b_bo', 'rel_bias', 'mlp_norm', 'final_norm']


def _params(sem=None):
    return pltpu.CompilerParams(dimension_semantics=sem, vmem_limit_bytes=VMEM_LIMIT)


def _pick(n, cands):
    for c in cands:
        if n % c == 0:
            return c
    raise ValueError(n)


def _tile(n, want):
    return n if n <= want else _pick(n, (want, want // 2, want // 4))


def mm_nn(name, a, w3, layer, epilogue, extras, out_dtypes):
    m, k = a.shape
    _, kw, n = w3.shape
    assert kw == k
    tm = _tile(m, 1024 if k <= 1024 else 512)
    tn = _tile(n, 1024)
    ne, no = len(extras), len(out_dtypes)

    def body(a_ref, w_ref, *rest):
        ex, outs = rest[:ne], rest[ne:ne + no]
        res = epilogue(jnp.dot(a_ref[...], w_ref[...], preferred_element_type=F32), *[e[...] for e in ex])
        for o, r in zip(outs, res):
            o[...] = r.astype(o.dtype)

    tile = pl.BlockSpec((tm, tn), lambda i, j: (i, j))
    ex_specs = [tile if e.shape[0] == m else pl.BlockSpec((1, tn), lambda i, j: (0, j)) for e in extras]
    return pl.pallas_call(
        body, name=name, grid=(m // tm, n // tn),
        in_specs=[pl.BlockSpec((tm, k), lambda i, j: (i, 0)),
                  pl.BlockSpec((None, k, tn), lambda i, j: (layer, 0, j))] + ex_specs,
        out_specs=[tile] * no,
        out_shape=[jax.ShapeDtypeStruct((m, n), d) for d in out_dtypes],
        compiler_params=_params(("parallel", "parallel")),
    )(a, w3, *extras)


def mm_nt(name, dy, w3, layer, epilogue, extras, out_dtypes):
    m, n = dy.shape
    _, k, nw = w3.shape
    assert nw == n
    tm = _tile(m, 1024 if n <= 1024 else 512)
    tko = _tile(k, 1024)
    ne, no = len(extras), len(out_dtypes)

    def body(a_ref, w_ref, *rest):
        ex, outs = rest[:ne], rest[ne:ne + no]
        acc = lax.dot_general(a_ref[...], w_ref[...], (((1,), (1,)), ((), ())), preferred_element_type=F32)
        res = epilogue(acc, *[e[...] for e in ex])
        for o, v in zip(outs, res):
            o[...] = v.astype(o.dtype)

    tile = pl.BlockSpec((tm, tko), lambda i, ko: (i, ko))
    return pl.pallas_call(
        body, name=name, grid=(m // tm, k // tko),
        in_specs=[pl.BlockSpec((tm, n), lambda i, ko: (i, 0)),
                  pl.BlockSpec((None, tko, n), lambda i, ko: (layer, ko, 0))] + [tile] * ne,
        out_specs=[tile] * no,
        out_shape=[jax.ShapeDtypeStruct((m, k), d) for d in out_dtypes],
        compiler_params=_params(("parallel", "parallel")),
    )(dy, w3, *extras)


def mm_tn(name, x, dy, gbuf, layer):
    s, k = x.shape
    _, kw, n = gbuf.shape
    assert kw == k and dy.shape == (s, n)
    tkk = _tile(k, 512)
    tn = _tile(n, 1024)

    def body(x_ref, dy_ref, g_in, g_out):
        del g_in
        g_out[...] = lax.dot_general(x_ref[...], dy_ref[...], (((0,), (0,)), ((), ())),
                                     preferred_element_type=F32).astype(g_out.dtype)

    return pl.pallas_call(
        body, name=name, grid=(k // tkk, n // tn),
        in_specs=[pl.BlockSpec((s, tkk), lambda ki, j: (0, ki)),
                  pl.BlockSpec((s, tn), lambda ki, j: (0, j)),
                  pl.BlockSpec(memory_space=pl.ANY)],
        out_specs=pl.BlockSpec((None, tkk, tn), lambda ki, j: (layer, ki, j)),
        out_shape=jax.ShapeDtypeStruct(gbuf.shape, gbuf.dtype),
        input_output_aliases={2: 0},
        compiler_params=_params(("parallel", "parallel")),
    )(x, dy, gbuf)


def rms_fwd(name, h, g):
    s, d = h.shape
    tr = _pick(s, (ROW_TILE, 256, 128))

    def body(h_ref, g_ref, o_ref):
        x = h_ref[...]
        r = lax.rsqrt(jnp.mean(x * x, axis=-1, keepdims=True) + EPS)
        o_ref[...] = (x * r * g_ref[...]).astype(o_ref.dtype)

    return pl.pallas_call(
        body, name=name, grid=(s // tr,),
        in_specs=[pl.BlockSpec((tr, d), lambda i: (i, 0)), pl.BlockSpec((1, d), lambda i: (0, 0))],
        out_specs=pl.BlockSpec((tr, d), lambda i: (i, 0)),
        out_shape=jax.ShapeDtypeStruct((s, d), BF16),
        compiler_params=_params(("parallel",)),
    )(h, g.reshape(1, d))


def rms_bwd(name, h, g, dn, dres):
    s, d = h.shape
    tr = _pick(s, (ROW_TILE, 256, 128))

    def body(h_ref, g_ref, dn_ref, dres_ref, dx_ref, dxb_ref, dg_ref, cs_ref):
        i = pl.program_id(0)
        x = h_ref[...]
        r = lax.rsqrt(jnp.mean(x * x, axis=-1, keepdims=True) + EPS)
        xh = x * r
        dn_ = dn_ref[...]
        dyg = dn_ * g_ref[...]
        dx = dres_ref[...] + r * (dyg - xh * jnp.mean(dyg * xh, axis=-1, keepdims=True))
        dx_ref[...] = dx
        dxb_ref[...] = dx.astype(BF16)

        @pl.when(i == 0)
        def _():
            dg_ref[...] = jnp.zeros_like(dg_ref)
            cs_ref[...] = jnp.zeros_like(cs_ref)

        dg_ref[...] += jnp.sum(dn_ * xh, axis=0, keepdims=True)
        cs_ref[...] += jnp.sum(dx, axis=0, keepdims=True)

    row = pl.BlockSpec((tr, d), lambda i: (i, 0))
    vec = pl.BlockSpec((1, d), lambda i: (0, 0))
    return pl.pallas_call(
        body, name=name, grid=(s // tr,),
        in_specs=[row, vec, row, row],
        out_specs=[row, row, vec, vec],
        out_shape=[jax.ShapeDtypeStruct((s, d), F32), jax.ShapeDtypeStruct((s, d), BF16),
                   jax.ShapeDtypeStruct((1, d), F32), jax.ShapeDtypeStruct((1, d), F32)],
        compiler_params=_params(("arbitrary",)),
    )(h, g.reshape(1, d), dn, dres)


def loss_head(h, g, target):
    s, d = h.shape
    tr = _pick(s, (ROW_TILE, 256, 128))

    def body(h_ref, g_ref, t_ref, dx_ref, dxb_ref, dg_ref, loss_ref):
        i = pl.program_id(0)
        x = h_ref[...]
        r = lax.rsqrt(jnp.mean(x * x, axis=-1, keepdims=True) + EPS)
        xh = x * r
        gw = g_ref[...]
        err = xh * gw - t_ref[...]
        dn_ = err * (1.0 / d)
        dyg = dn_ * gw
        dx = r * (dyg - xh * jnp.mean(dyg * xh, axis=-1, keepdims=True))
        dx_ref[...] = dx
        dxb_ref[...] = dx.astype(BF16)

        @pl.when(i == 0)
        def _():
            dg_ref[...] = jnp.zeros_like(dg_ref)
            loss_ref[...] = jnp.zeros_like(loss_ref)

        dg_ref[...] += jnp.sum(dn_ * xh, axis=0, keepdims=True)
        per_row = jnp.sum(err * err, axis=-1, keepdims=True) * (0.5 / d)
        loss_ref[...] += jnp.broadcast_to(jnp.sum(per_row, axis=0, keepdims=True), loss_ref.shape)

    row = pl.BlockSpec((tr, d), lambda i: (i, 0))
    vec = pl.BlockSpec((1, d), lambda i: (0, 0))
    return pl.pallas_call(
        body, name="loss_head", grid=(s // tr,),
        in_specs=[row, vec, row],
        out_specs=[row, row, vec, pl.BlockSpec((1, 128), lambda i: (0, 0))],
        out_shape=[jax.ShapeDtypeStruct((s, d), F32), jax.ShapeDtypeStruct((s, d), BF16),
                   jax.ShapeDtypeStruct((1, d), F32), jax.ShapeDtypeStruct((1, 128), F32)],
        compiler_params=_params(("arbitrary",)),
    )(h, g.reshape(1, d), target)


def colsum(name, x):
    s, n = x.shape
    tr = _pick(s, (ROW_TILE, 256, 128))

    def body(x_ref, o_ref):
        @pl.when(pl.program_id(0) == 0)
        def _():
            o_ref[...] = jnp.zeros_like(o_ref)

        o_ref[...] += jnp.sum(x_ref[...].astype(F32), axis=0, keepdims=True)

    return pl.pallas_call(
        body, name=name, grid=(s // tr,),
        in_specs=[pl.BlockSpec((tr, n), lambda i: (i, 0))],
        out_specs=pl.BlockSpec((1, n), lambda i: (0, 0)),
        out_shape=jax.ShapeDtypeStruct((1, n), F32),
        compiler_params=_params(("arbitrary",)),
    )(x)


def _tri_rows(reverse):
    i = np.arange(SB_BK)
    tri = (i[None, :] >= i[:, None]) if reverse else (i[None, :] <= i[:, None])
    tri = np.concatenate([tri, tri], axis=1)
    return jnp.asarray(np.concatenate([tri, np.ones((8, 2 * SB_BK), bool)], axis=0), BF16)


def _hi_lo_rows(x):
    hi = x.astype(BF16)
    lo = (x - hi.astype(F32)).astype(BF16)
    return jnp.concatenate([hi, lo], axis=0)


def _softplus2(zs):
    neg_abs = lax.bitcast_convert_type(lax.bitcast_convert_type(zs, jnp.uint32) | jnp.uint32(0x80000000), F32)
    return jnp.maximum(zs, 0.0) + jnp.log2(1.0 + jnp.exp2(neg_abs))


def _pair_mask(first_rel_block, bq):
    key = lax.broadcasted_iota(jnp.int32, (2 * SB_BK, bq), 0) + first_rel_block * SB_BK
    qry = lax.broadcasted_iota(jnp.int32, (2 * SB_BK, bq), 1)
    return key < qry


def _row_of(table8, sub8, r):
    return jnp.sum(jnp.where(sub8 == r, table8, 0.0), axis=0, keepdims=True)


def _keys(j0):
    return pl.ds(pl.multiple_of(j0 * SB_BK, 2 * SB_BK), 2 * SB_BK)


def sb_fwd(name, qt, k, vt):
    nh, dh, s = qt.shape
    bq = SB_BQ
    per_q = bq // SB_BK
    nkb = s // SB_BK
    assert s % bq == 0 and per_q == 4 and nkb % 8 == 0

    def body(q_ref, k_ref, v_ref, a_ref, o_ref, rtab_ref, acc, zbuf, wbuf):
        i = pl.program_id(1)
        qb = q_ref[...] * Q_SCALE
        tri = a_ref[...]
        sub8 = lax.broadcasted_iota(jnp.int32, (8, bq), 0)
        acc[...] = jnp.zeros_like(acc)
        rtab_ref[...] = jnp.zeros_like(rtab_ref)

        def scores(j0):
            return jnp.dot(k_ref[_keys(j0), :], qb, preferred_element_type=F32) * LOG2E

        def pair(j0, slot, run, rt8, mask, has_prev):
            zs = zbuf[slot]
            zbuf[1 - slot] = scores(jnp.maximum(j0 - 2, 0))
            if has_prev:
                acc[...] += jnp.dot(v_ref[:, _keys(j0 + 2)], wbuf[1 - slot], preferred_element_type=F32)
            p = _softplus2(zs)
            if mask is not None:
                p = jnp.where(mask, p, 0.0)
            cr1 = jnp.dot(tri, _hi_lo_rows(p[SB_BK:]), preferred_element_type=F32)
            cr0 = jnp.dot(tri, _hi_lo_rows(p[:SB_BK]), preferred_element_type=F32)
            run1 = run + cr1[SB_BK:SB_BK + 1]
            w = jnp.exp2(jnp.concatenate([zs[:SB_BK] - cr0[:SB_BK] - run1, zs[SB_BK:] - cr1[:SB_BK] - run], axis=0))
            if mask is not None:
                w = jnp.where(mask, w, 0.0)
            wbuf[slot] = w.astype(BF16)
            rt8 = jnp.where(sub8 == (j0 + 1) % 8, run, jnp.where(sub8 == j0 % 8, run1, rt8))
            rtab_ref[pl.ds(pl.multiple_of((j0 // 8) * 8, 8), 8), :] = rt8
            return run1 + cr0[SB_BK:SB_BK + 1], rt8

        top = i * per_q
        zbuf[0] = scores(top + 2)
        carry = (jnp.zeros((1, bq), F32), jnp.zeros((8, bq), F32))
        carry = pair(top + 2, 0, *carry, _pair_mask(2, bq), False)
        carry = pair(top, 1, *carry, _pair_mask(0, bq), True)

        def step(it, c):
            j0 = top - 2 - 4 * it
            c = pair(j0, 0, *c, None, True)
            return pair(j0 - 2, 1, *c, None, True)

        lax.fori_loop(0, i, step, carry)
        acc[...] += jnp.dot(v_ref[:, _keys(0)], wbuf[1], preferred_element_type=F32)
        o_ref[...] = acc[...].astype(o_ref.dtype)

    qspec = pl.BlockSpec((None, dh, bq), lambda h, i: (h, 0, i))
    return pl.pallas_call(
        body, name=name, grid=(nh, s // bq),
        in_specs=[qspec, pl.BlockSpec((None, s, dh), lambda h, i: (h, 0, 0)),
                  pl.BlockSpec((None, dh, s), lambda h, i: (h, 0, 0)),
                  pl.BlockSpec((SB_BK + 8, 2 * SB_BK), lambda h, i: (0, 0))],
        out_specs=[qspec, pl.BlockSpec((None, nkb, bq), lambda h, i: (h, 0, i))],
        out_shape=[jax.ShapeDtypeStruct((nh, dh, s), BF16), jax.ShapeDtypeStruct((nh, nkb, s), F32)],
        scratch_shapes=[pltpu.VMEM((dh, bq), F32), pltpu.VMEM((2, 2 * SB_BK, bq), F32),
                        pltpu.VMEM((2, 2 * SB_BK, bq), BF16)],
        compiler_params=_params(("parallel", "arbitrary")),
    )(qt, k, vt, _tri_rows(True))


def sb_bwd(name, qt, q, k, kt, v, dot_, do, rtab):
    nh, dh, s = qt.shape
    bq = SB_BQ
    per_q = bq // SB_BK
    nkb = s // SB_BK

    def body(qt_ref, q_ref, k_ref, kt_ref, v_ref, dot_ref, do_ref, rtab_ref, ar_ref, af_ref,
             dq_ref, dk_ref, dv_ref, dq_acc, zbuf, dwbuf, dzbuf, wbuf):
        i = pl.program_id(1)

        @pl.when(i == 0)
        def _():
            dk_ref[...] = jnp.zeros_like(dk_ref)
            dv_ref[...] = jnp.zeros_like(dv_ref)

        qtb = qt_ref[...] * Q_SCALE
        qrows = q_ref[...] * Q_SCALE
        dotb = dot_ref[...]
        dorows = do_ref[...]
        tri_rev = ar_ref[...][:SB_BK]
        tri_fwd = af_ref[...]
        sub8 = lax.broadcasted_iota(jnp.int32, (8, bq), 0)
        dq_acc[...] = jnp.zeros_like(dq_acc)
        last = i * per_q + 2

        def issue(j0, slot):
            zbuf[slot] = jnp.dot(k_ref[_keys(j0), :], qtb, preferred_element_type=F32) * LOG2E
            dwbuf[slot] = jnp.dot(v_ref[_keys(j0), :], dotb, preferred_element_type=F32)

        def retire(j0, slot):
            keys = _keys(j0)
            dq_acc[...] += jnp.dot(kt_ref[:, keys], dzbuf[slot], preferred_element_type=F32)
            dk_ref[keys, :] += jnp.dot(dzbuf[slot], qrows, preferred_element_type=F32)
            dv_ref[keys, :] += jnp.dot(wbuf[slot], dorows, preferred_element_type=F32)

        def pair(j0, slot, g_run, mask):
            zs = zbuf[slot]
            dw = dwbuf[slot]
            issue(jnp.minimum(j0 + 2, last), 1 - slot)
            retire(jnp.maximum(j0 - 2, 0), 1 - slot)
            p_raw = _softplus2(zs)
            p = p_raw if mask is None else jnp.where(mask, p_raw, 0.0)
            c0 = jnp.dot(tri_rev, _hi_lo_rows(p[:SB_BK]), preferred_element_type=F32)
            c1 = jnp.dot(tri_rev, _hi_lo_rows(p[SB_BK:]), preferred_element_type=F32)
            rt8 = rtab_ref[pl.ds(pl.multiple_of((j0 // 8) * 8, 8), 8), :]
            r0 = _row_of(rt8, sub8, j0 % 8)
            r1 = _row_of(rt8, sub8, (j0 + 1) % 8)
            w = jnp.exp2(jnp.concatenate([zs[:SB_BK] - c0 - r0, zs[SB_BK:] - c1 - r1], axis=0))
            if mask is not None:
                w = jnp.where(mask, w, 0.0)
            g = w * dw
            gg0 = jnp.dot(tri_fwd, _hi_lo_rows(g[:SB_BK]), preferred_element_type=F32)
            gg1 = jnp.dot(tri_fwd, _hi_lo_rows(g[SB_BK:]), preferred_element_type=F32)
            g_run1 = g_run + gg0[SB_BK:SB_BK + 1]
            g_pre = jnp.concatenate([gg0[:SB_BK] + g_run, gg1[:SB_BK] + g_run1], axis=0)
            dz = g - jnp.exp2(zs - p_raw) * g_pre
            if mask is not None:
                dz = jnp.where(mask, dz, 0.0)
            dzbuf[slot] = dz.astype(BF16)
            wbuf[slot] = w.astype(BF16)
            return g_run1 + gg1[SB_BK:SB_BK + 1]

        issue(0, 0)
        dzbuf[1] = jnp.zeros((2 * SB_BK, bq), BF16)
        wbuf[1] = jnp.zeros((2 * SB_BK, bq), BF16)

        def step(it, g_run):
            g_run = pair(4 * it, 0, g_run, None)
            return pair(4 * it + 2, 1, g_run, None)

        g_run = lax.fori_loop(0, i, step, jnp.zeros((1, bq), F32))
        g_run = pair(last - 2, 0, g_run, _pair_mask(0, bq))
        pair(last, 1, g_run, _pair_mask(2, bq))
        retire(last, 1)
        dq_ref[...] = dq_acc[...] * Q_SCALE

    tspec = pl.BlockSpec((None, dh, bq), lambda h, i: (h, 0, i))
    rspec = pl.BlockSpec((None, bq, dh), lambda h, i: (h, i, 0))
    kspec = pl.BlockSpec((None, s, dh), lambda h, i: (h, 0, 0))
    aspec = pl.BlockSpec((SB_BK + 8, 2 * SB_BK), lambda h, i: (0, 0))
    pair_f32 = pltpu.VMEM((2, 2 * SB_BK, bq), F32)
    pair_bf16 = pltpu.VMEM((2, 2 * SB_BK, bq), BF16)
    return pl.pallas_call(
        body, name=name, grid=(nh, s // bq),
        in_specs=[tspec, rspec, kspec, pl.BlockSpec((None, dh, s), lambda h, i: (h, 0, 0)), kspec, tspec, rspec,
                  pl.BlockSpec((None, nkb, bq), lambda h, i: (h, 0, i)), aspec, aspec],
        out_specs=[tspec, kspec, kspec],
        out_shape=[jax.ShapeDtypeStruct((nh, dh, s), F32), jax.ShapeDtypeStruct((nh, s, dh), F32),
                   jax.ShapeDtypeStruct((nh, s, dh), F32)],
        scratch_shapes=[pltpu.VMEM((dh, bq), F32), pair_f32, pair_f32, pair_bf16, pair_bf16],
        compiler_params=_params(("parallel", "arbitrary")),
    )(qt, q, k, kt, v, dot_, do, rtab, _tri_rows(True), _tri_rows(False))


def _swa_scores(q_ref, k_ref, bias_ref, sink_ref, i):
    rows = q_ref.shape[0]
    qb = q_ref[...] * Q_SCALE
    off = pl.multiple_of(i * WINDOW, WINDOW)
    kb = k_ref[pl.ds(off, 2 * WINDOW), :]
    sc = lax.dot_general(qb, kb, (((1,), (1,)), ((), ())), preferred_element_type=F32) + bias_ref[...]
    qi = lax.broadcasted_iota(jnp.int32, (rows, 2 * WINDOW), 0) & (WINDOW - 1)
    kj = lax.broadcasted_iota(jnp.int32, (rows, 2 * WINDOW), 1)
    dist = qi + WINDOW - kj
    valid = (dist >= 0) & (dist < WINDOW) & ((kj >= WINDOW) | (i > 0))
    sc = jnp.where(valid, sc, NEG_INF)
    sink = sink_ref[...]
    mx = jnp.maximum(jnp.max(sc, axis=1, keepdims=True), sink)
    p = jnp.exp(sc - mx)
    p_sink = jnp.exp(sink - mx)
    inv = 1.0 / (jnp.sum(p, axis=1, keepdims=True) + p_sink)
    return qb, kb, off, p, p_sink, inv


def swa_fwd(name, qg, kp, vp, bias, sink_col):
    ng, nb, rows, dh = qg.shape
    sp = kp.shape[1]

    def body(q_ref, k_ref, v_ref, bias_ref, sink_ref, o_ref):
        i = pl.program_id(1)
        _, _, off, p, _, inv = _swa_scores(q_ref, k_ref, bias_ref, sink_ref, i)
        vb = v_ref[pl.ds(off, 2 * WINDOW), :]
        o_ref[...] = (jnp.dot(p.astype(BF16), vb, preferred_element_type=F32) * inv).astype(o_ref.dtype)

    qspec = pl.BlockSpec((None, None, rows, dh), lambda g, i: (g, i, 0, 0))
    kspec = pl.BlockSpec((None, sp, dh), lambda g, i: (g, 0, 0))
    return pl.pallas_call(
        body, name=name, grid=(ng, nb),
        in_specs=[qspec, kspec, kspec, pl.BlockSpec((None, rows, 2 * WINDOW), lambda g, i: (g, 0, 0)),
                  pl.BlockSpec((None, rows, 1), lambda g, i: (g, 0, 0))],
        out_specs=qspec,
        out_shape=jax.ShapeDtypeStruct(qg.shape, BF16),
        compiler_params=_params(("parallel", "arbitrary")),
    )(qg, kp, vp, bias, sink_col)


def swa_bwd(name, qg, kp, vp, bias, sink_col, dog, dk_in, dv_in):
    ng, nb, rows, dh = qg.shape
    sp = kp.shape[1]

    def body(q_ref, k_ref, v_ref, bias_ref, sink_ref, do_ref, dki_ref, dvi_ref, dq_ref, dk_ref, dv_ref, db_ref, ds_ref):
        i = pl.program_id(1)

        @pl.when(i == 0)
        def _():
            dk_ref[...] = dki_ref[...]
            dv_ref[...] = dvi_ref[...]
            db_ref[...] = jnp.zeros_like(db_ref)
            ds_ref[...] = jnp.zeros_like(ds_ref)

        qb, kb, off, p, p_sink, inv = _swa_scores(q_ref, k_ref, bias_ref, sink_ref, i)
        p = p * inv
        dob = do_ref[...]
        vb = v_ref[pl.ds(off, 2 * WINDOW), :]
        dp = lax.dot_general(dob, vb, (((1,), (1,)), ((), ())), preferred_element_type=F32)
        delta = jnp.sum(p * dp, axis=1, keepdims=True)
        dsc = p * (dp - delta)
        ds_ref[...] -= p_sink * inv * delta
        db_ref[...] += dsc
        dscb = dsc.astype(BF16)
        dq_ref[...] = (jnp.dot(dscb, kb, preferred_element_type=F32) * Q_SCALE).astype(dq_ref.dtype)
        dk_ref[pl.ds(off, 2 * WINDOW), :] += lax.dot_general(dscb, qb, (((0,), (0,)), ((), ())),
                                                             preferred_element_type=F32)
        dv_ref[pl.ds(off, 2 * WINDOW), :] += lax.dot_general(p.astype(BF16), dob, (((0,), (0,)), ((), ())),
                                                             preferred_element_type=F32)

    qspec = pl.BlockSpec((None, None, rows, dh), lambda g, i: (g, i, 0, 0))
    kspec = pl.BlockSpec((None, sp, dh), lambda g, i: (g, 0, 0))
    bspec = pl.BlockSpec((None, rows, 2 * WINDOW), lambda g, i: (g, 0, 0))
    sspec = pl.BlockSpec((None, rows, 1), lambda g, i: (g, 0, 0))
    return pl.pallas_call(
        body, name=name, grid=(ng, nb),
        in_specs=[qspec, kspec, kspec, bspec, sspec, qspec, kspec, kspec],
        out_specs=[qspec, kspec, kspec, bspec, sspec],
        out_shape=[jax.ShapeDtypeStruct(qg.shape, BF16), jax.ShapeDtypeStruct(kp.shape, F32),
                   jax.ShapeDtypeStruct(kp.shape, F32), jax.ShapeDtypeStruct(bias.shape, F32),
                   jax.ShapeDtypeStruct(sink_col.shape, F32)],
        compiler_params=_params(("parallel", "arbitrary")),
    )(qg, kp, vp, bias, sink_col, dog, dk_in, dv_in)


def _bucket_onehot():
    qi = np.arange(WINDOW)[:, None]
    kj = np.arange(2 * WINDOW)[None, :]
    n = np.maximum(qi + WINDOW - kj, 0)
    max_exact = N_BUCKETS // 2
    nf = np.maximum(n, 1).astype(np.float64)
    val = np.log(nf / max_exact) / math.log(WINDOW / max_exact) * (N_BUCKETS - max_exact)
    assert np.all(np.abs(val - np.round(val))[(n > max_exact) & (n < WINDOW)] > 1e-3)
    large = np.minimum(max_exact + val.astype(np.int64), N_BUCKETS - 1)
    bucket = np.where(n < max_exact, n, large).reshape(-1)
    onehot = np.zeros((128, bucket.size), np.float32)
    onehot[bucket, np.arange(bucket.size)] = 1.0
    return onehot


def _split3(x):
    a = x.astype(BF16)
    r = x - a.astype(F32)
    b = r.astype(BF16)
    c = (r - b.astype(F32)).astype(BF16)
    return a, b, c


def bias_table(rel_bias):
    nh = rel_bias.shape[1]
    oh = jnp.asarray(_bucket_onehot(), BF16)
    n = oh.shape[1]
    tn = 4096
    rb = jnp.zeros((nh, 128), F32).at[:, :N_BUCKETS].set(rel_bias.T)

    def body(rb_ref, oh_ref, o_ref):
        o_ref[...] = sum(jnp.dot(t, oh_ref[...], preferred_element_type=F32) for t in _split3(rb_ref[...]))

    return pl.pallas_call(
        body, name="bias_table", grid=(n // tn,),
        in_specs=[pl.BlockSpec((nh, 128), lambda i: (0, 0)), pl.BlockSpec((128, tn), lambda i: (0, i))],
        out_specs=pl.BlockSpec((nh, tn), lambda i: (0, i)),
        out_shape=jax.ShapeDtypeStruct((nh, n), F32),
        compiler_params=_params(("parallel",)),
    )(rb, oh)


def bias_table_grad(db0, db1):
    nh, n = db0.shape
    oh = jnp.asarray(_bucket_onehot(), BF16)
    tn = 4096

    def body(a_ref, b_ref, oh_ref, o_ref):
        @pl.when(pl.program_id(0) == 0)
        def _():
            o_ref[...] = jnp.zeros_like(o_ref)

        o_ref[...] += sum(lax.dot_general(t, oh_ref[...], (((1,), (1,)), ((), ())), preferred_element_type=F32)
                          for t in _split3(a_ref[...] + b_ref[...]))

    blk = pl.BlockSpec((nh, tn), lambda i: (0, i))
    return pl.pallas_call(
        body, name="bias_table_grad", grid=(n // tn,),
        in_specs=[blk, blk, pl.BlockSpec((128, tn), lambda i: (0, i))],
        out_specs=pl.BlockSpec((nh, 128), lambda i: (0, 0)),
        out_shape=jax.ShapeDtypeStruct((nh, 128), F32),
        compiler_params=_params(("arbitrary",)),
    )(db0, db1, oh)


def _owner_view(ref, name, d):
    if name == 'a_norm':
        return ref.at[d]
    if name in COL_SHARDED:
        n = ref.shape[2] // N_DEV
        return ref.at[:, :, pl.ds(pl.multiple_of(d * n, 128), n)]
    return ref.at[:, d]


def _place():
    return lax.axis_index("x"), lax.axis_index("y"), lax.axis_index("c")


def all_gather_weights(names, shards, full_shapes):
    n = len(names)

    def body(*refs):
        ins, outs = refs[:n], refs[n:2 * n]
        send_sems, recv_sems, local_sems = refs[2 * n:]
        x, y, c = _place()
        me, sibling = (x, y, c), (x, y, 1 - c)
        chips = [(1 - x, y), (x, 1 - y), (1 - x, 1 - y)]

        def dev(p):
            return 4 * p[0] + 2 * p[1] + p[2]

        def copy(t, k, block, to, src=None):
            dst = _owner_view(outs[t], names[t], dev(block))
            return pltpu.make_async_remote_copy(
                src_ref=dst if src is None else src, dst_ref=dst,
                send_sem=send_sems.at[t, k], recv_sem=recv_sems.at[t, k], device_id=to, device_id_type=MESH)

        mine = [pltpu.make_async_copy(ins[t], _owner_view(outs[t], names[t], dev(me)), local_sems.at[t])
                for t in range(n)]
        for cp in mine:
            cp.start()
        first = []
        for t in range(n):
            first.append(copy(t, 0, me, sibling, src=ins[t]))
            first += [copy(t, 1 + j, me, (*chip, c), src=ins[t]) for j, chip in enumerate(chips)]
        for cp in first:
            cp.start()
        passed = []
        for j, chip in enumerate(chips):
            for t in range(n):
                copy(t, 1 + j, (*chip, c), me).wait_recv()
                fwd = copy(t, 4 + j, (*chip, c), sibling)
                fwd.start()
                passed.append(fwd)
        for t in range(n):
            copy(t, 0, sibling, me).wait_recv()
            for j, chip in enumerate(chips):
                copy(t, 4 + j, (*chip, 1 - c), me).wait_recv()
        for cp in first + passed:
            cp.wait_send()
        for cp in mine:
            cp.wait()

    hbm = pl.BlockSpec(memory_space=pl.ANY)
    return pl.pallas_call(
        body, name="all_gather_weights",
        in_specs=[hbm] * n, out_specs=[hbm] * n,
        out_shape=[jax.ShapeDtypeStruct(full_shapes[t], shards[t].dtype) for t in range(n)],
        scratch_shapes=[pltpu.SemaphoreType.DMA((n, 7)), pltpu.SemaphoreType.DMA((n, 7)),
                        pltpu.SemaphoreType.DMA((n,))],
    )(*shards)


def sibling_exchange(names, grads, part_shapes):
    n = len(names)

    def body(*refs):
        ins, outs = refs[:n], refs[n:2 * n]
        send_sems, recv_sems = refs[2 * n:]
        x, y, c = _place()
        sibling = (x, y, 1 - c)
        copies = []
        for t in range(n):
            for q in range(4):
                copies.append(pltpu.make_async_remote_copy(
                    src_ref=_owner_view(ins[t], names[t], 2 * q + 1 - c), dst_ref=outs[t].at[q],
                    send_sem=send_sems.at[t, q], recv_sem=recv_sems.at[t, q], device_id=sibling, device_id_type=MESH))
        for cp in copies:
            cp.start()
        for cp in copies:
            cp.wait()

    hbm = pl.BlockSpec(memory_space=pl.ANY)
    return pl.pallas_call(
        body, name="rs_sibling_exchange",
        in_specs=[hbm] * n, out_specs=[hbm] * n,
        out_shape=[jax.ShapeDtypeStruct((4,) + part_shapes[t], BF16) for t in range(n)],
        scratch_shapes=[pltpu.SemaphoreType.DMA((n, 4)), pltpu.SemaphoreType.DMA((n, 4))],
    )(*grads)


def chip_exchange(names, parts, part_shapes):
    n = len(names)

    def body(*refs):
        ins, outs = refs[:n], refs[n:2 * n]
        send_sems, recv_sems = refs[2 * n:]
        x, y, c = _place()
        chips = [(1 - x, y), (x, 1 - y), (1 - x, 1 - y)]
        copies = []
        for t in range(n):
            for k, chip in enumerate(chips):
                copies.append(pltpu.make_async_remote_copy(
                    src_ref=ins[t].at[2 * chip[0] + chip[1]], dst_ref=outs[t].at[k],
                    send_sem=send_sems.at[t, k], recv_sem=recv_sems.at[t, k], device_id=(*chip, c),
                    device_id_type=MESH))
        for cp in copies:
            cp.start()
        for cp in copies:
            cp.wait()

    hbm = pl.BlockSpec(memory_space=pl.ANY)
    return pl.pallas_call(
        body, name="rs_chip_exchange",
        in_specs=[hbm] * n, out_specs=[hbm] * n,
        out_shape=[jax.ShapeDtypeStruct((3,) + part_shapes[t], BF16) for t in range(n)],
        scratch_shapes=[pltpu.SemaphoreType.DMA((n, 3)), pltpu.SemaphoreType.DMA((n, 3))],
    )(*parts)


def all_gather_rows(x):
    r, w = x.shape

    def body(x_ref, out_ref, send_sems, recv_sems, local_sem):
        px, py, pc = _place()
        me = 4 * px + 2 * py + pc
        mine = pltpu.make_async_copy(x_ref, out_ref.at[me], local_sem)
        mine.start()
        copies = []
        for k in range(1, N_DEV):
            peer = (px ^ (k >> 2), py ^ ((k >> 1) & 1), pc ^ (k & 1))
            copies.append(pltpu.make_async_remote_copy(
                src_ref=x_ref, dst_ref=out_ref.at[me], send_sem=send_sems.at[k - 1], recv_sem=recv_sems.at[k - 1],
                device_id=peer, device_id_type=MESH))
        for cp in copies:
            cp.start()
        for k in range(1, N_DEV):
            peer_idx = me ^ k
            pltpu.make_async_remote_copy(
                src_ref=x_ref, dst_ref=out_ref.at[peer_idx], send_sem=send_sems.at[k - 1],
                recv_sem=recv_sems.at[k - 1], device_id=(px, py, pc), device_id_type=MESH).wait_recv()
        for cp in copies:
            cp.wait_send()
        mine.wait()

    vmem = pl.BlockSpec(memory_space=pltpu.VMEM)
    return pl.pallas_call(
        body, name="all_gather_small_grads",
        in_specs=[vmem], out_specs=vmem,
        out_shape=jax.ShapeDtypeStruct((N_DEV, r, w), x.dtype),
        scratch_shapes=[pltpu.SemaphoreType.DMA((N_DEV - 1,)), pltpu.SemaphoreType.DMA((N_DEV - 1,)),
                        pltpu.SemaphoreType.DMA],
    )(x)


def _adamw(w, g, m, v):
    m = ADAM_B1 * m + (1.0 - ADAM_B1) * g
    v = ADAM_B2 * v + (1.0 - ADAM_B2) * (g * g)
    m_hat = m / (1.0 - ADAM_B1 ** ADAM_STEP)
    v_hat = v / (1.0 - ADAM_B2 ** ADAM_STEP)
    return -ADAM_LR * (m_hat / (jnp.sqrt(v_hat) + ADAM_EPS) + ADAM_WD * w), m, v


def _as_rows(a, lead):
    return a.reshape(a.shape[:lead] + (-1, a.shape[-1]))


def sibling_sum(name, col, grads, recv, core):
    _, nl, rows, cols = recv.shape
    tr = _tile(rows, 512)
    rspec = pl.BlockSpec((None, None, tr, cols), lambda q, l, i, c_ref: (q, l, i, 0))
    if col:
        gspec = pl.BlockSpec((None, tr, cols), lambda q, l, i, c_ref: (l, i, 2 * q + c_ref[0]))
    else:
        gspec = pl.BlockSpec((None, None, tr, cols), lambda q, l, i, c_ref: (l, 2 * q + c_ref[0], i, 0))

    def body(c_ref, g_ref, r_ref, o_ref):
        del c_ref
        o_ref[...] = (g_ref[...].astype(F32) + r_ref[...].astype(F32)).astype(BF16)

    return pl.pallas_call(
        body, name=name,
        grid_spec=pltpu.PrefetchScalarGridSpec(num_scalar_prefetch=1, grid=(4, nl, rows // tr),
                                               in_specs=[gspec, rspec], out_specs=rspec),
        out_shape=jax.ShapeDtypeStruct(recv.shape, BF16),
        compiler_params=_params(("parallel", "parallel", "parallel")),
    )(core.reshape(1), grads, recv)


def reduce_adamw(name, parts, recv, chip, w, m, v):
    shape = w.shape
    w2, m2, v2 = (_as_rows(a, 0) for a in (w, m, v))
    rows, cols = w2.shape
    p3 = parts.reshape(4, rows, cols)
    r3 = recv.reshape(3, rows, cols)
    tr = _pick(rows, (256, 128))

    def body(q_ref, p_ref, r_ref, w_ref, m_ref, v_ref, g_out, d_out, m_out, v_out):
        del q_ref
        g = ((p_ref[...].astype(F32) + r_ref[0].astype(F32)) + r_ref[1].astype(F32)) + r_ref[2].astype(F32)
        d, mn, vn = _adamw(w_ref[...], g, m_ref[...], v_ref[...])
        g_out[...] = g
        d_out[...] = d
        m_out[...] = mn
        v_out[...] = vn

    blk = pl.BlockSpec((tr, cols), lambda i, q_ref: (i, 0))
    outs = pl.pallas_call(
        body, name=name,
        grid_spec=pltpu.PrefetchScalarGridSpec(
            num_scalar_prefetch=1, grid=(rows // tr,),
            in_specs=[pl.BlockSpec((None, tr, cols), lambda i, q_ref: (q_ref[0], i, 0)),
                      pl.BlockSpec((3, tr, cols), lambda i, q_ref: (0, i, 0)), blk, blk, blk],
            out_specs=[blk] * 4),
        out_shape=[jax.ShapeDtypeStruct((rows, cols), F32)] * 4,
        compiler_params=_params(("parallel",)),
    )(chip.reshape(1), p3, r3, w2, m2, v2)
    return [o.reshape(shape) for o in outs]


def small_adamw(name, gathered, w, m, v):
    _, r, c = gathered.shape

    def body(ga_ref, w_ref, m_ref, v_ref, g_out, d_out, m_out, v_out):
        g = ga_ref[0]
        for d in range(1, N_DEV):
            g = g + ga_ref[d]
        dl, mn, vn = _adamw(w_ref[...], g, m_ref[...], v_ref[...])
        g_out[...] = g
        d_out[...] = dl
        m_out[...] = mn
        v_out[...] = vn

    return pl.pallas_call(
        body, name=name,
        out_shape=[jax.ShapeDtypeStruct((r, c), F32)] * 4,
        compiler_params=_params(),
    )(gathered, w, m, v)


def _heads(t, nh):
    s = t.shape[0]
    return t.reshape(s, nh, HEAD_DIM).transpose(1, 0, 2)


def _heads_t(t, nh):
    return t.T.reshape(nh, HEAD_DIM, t.shape[0])


def _unheads(t):
    nh, s, dh = t.shape
    return t.transpose(1, 0, 2).reshape(s, nh * dh)


def _group_q(t, nb):
    s = t.shape[0]
    return t.reshape(nb, WINDOW, 2, 8, HEAD_DIM).transpose(2, 0, 3, 1, 4).reshape(2, nb, 8 * WINDOW, HEAD_DIM)


def _ungroup_q(t):
    ng, nb, _, dh = t.shape
    return t.reshape(ng, nb, 8, WINDOW, dh).transpose(1, 3, 0, 2, 4).reshape(nb * WINDOW, ng * 8 * dh)


def _front_pad(t):
    return jnp.pad(t, ((0, 0), (WINDOW, 0), (0, 0)))


def _add(acc, *ex):
    return (acc + ex[0],)


def local_step(x, target, wf, small, gbufs):
    s, d = x.shape
    nb = s // WINDOW
    n_a, n_b = small['a_norm'].shape[0], small['b_norm'].shape[0]
    sg = {}
    gb = dict(gbufs)

    bias_flat = bias_table(small['rel_bias'])
    bias_g = bias_flat.reshape(2, 8 * WINDOW, 2 * WINDOW)
    sink_cols = [jnp.repeat(small['b_sinks'][j], WINDOW).reshape(2, 8 * WINDOW, 1) for j in range(n_b)]

    def mlp_fwd(h, layer):
        n2 = rms_fwd(f"mlp_norm_fwd{layer}", h, small['mlp_norm'][layer])
        u, a = mm_nn(f"mlp_up_fwd{layer}", n2, wf['mlp_up'], layer,
                     lambda acc: (acc, jnp.square(jnp.maximum(acc, 0.0))), (), (BF16, BF16))
        (h2,) = mm_nn(f"mlp_down_fwd{layer}", a, wf['mlp_down'], layer, _add, (h,), (F32,))
        return h2, (n2, u, a)

    h = x
    saved = []
    for l in range(n_a):
        n1 = rms_fwd(f"a_norm_fwd{l}", h, small['a_norm'][l])
        (qkv,) = mm_nn(f"a_qkv_fwd{l}", n1, wf['a_wqkv'], l, lambda acc: (acc,), (), (BF16,))
        nh = d // HEAD_DIM
        o_t, rtab = sb_fwd(f"sb_fwd{l}", _heads_t(qkv[:, :d], nh), _heads(qkv[:, d:2 * d], nh),
                           _heads_t(qkv[:, 2 * d:], nh))
        o = o_t.reshape(d, s).T
        (h_mid,) = mm_nn(f"a_wo_fwd{l}", o, wf['a_wo'], l, _add, (h,), (F32,))
        h_out, mlp_saved = mlp_fwd(h_mid, l)
        saved.append((h, n1, qkv, o, rtab, h_mid, mlp_saved))
        h = h_out
    h_kv = h
    nkv = rms_fwd("kv_norm_fwd", h, small['kv_norm'])
    (kv,) = mm_nn("kv_fwd", nkv, wf['w_kv'], 0, lambda acc, b: (acc + b,), (small['b_kv'].reshape(1, -1),), (BF16,))
    kvw = kv.shape[1] // 2
    kp = _front_pad(_heads(kv[:, :kvw], 2))
    vp = _front_pad(_heads(kv[:, kvw:], 2))
    for j in range(n_b):
        layer = n_a + j
        n1 = rms_fwd(f"b_norm_fwd{j}", h, small['b_norm'][j])
        (qb,) = mm_nn(f"b_q_fwd{j}", n1, wf['b_wq'], j, lambda acc, b: (acc + b,),
                      (small['b_bq'][j].reshape(1, -1),), (BF16,))
        qg = _group_q(qb, nb)
        og = swa_fwd(f"swa_fwd{j}", qg, kp, vp, bias_g, sink_cols[j])
        o = _ungroup_q(og)
        (h_mid,) = mm_nn(f"b_wo_fwd{j}", o, wf['b_wo'], j, lambda acc, hh, b: (acc + hh + b,),
                         (h, small['b_bo'][j].reshape(1, -1)), (F32,))
        h_out, mlp_saved = mlp_fwd(h_mid, layer)
        saved.append((h, n1, qg, o, h_mid, mlp_saved))
        h = h_out

    dh, dhb, dg_final, loss_b = loss_head(h, small['final_norm'], target)
    sg['final_norm'] = dg_final[0]
    sg['mlp_norm'] = [None] * (n_a + n_b)
    cs_mid = None

    def mlp_bwd(dh, dhb, h_mid, mlp_saved, layer):
        n2, u, a = mlp_saved
        (du,) = mm_nt(f"mlp_down_dx{layer}", dhb, wf['mlp_down'], layer,
                      lambda acc, uu: (acc * (2.0 * jnp.maximum(uu.astype(F32), 0.0)),), (u,), (BF16,))
        gb['mlp_down'] = mm_tn(f"mlp_down_dw{layer}", a, dhb, gb['mlp_down'], layer)
        (dn2,) = mm_nt(f"mlp_up_dx{layer}", du, wf['mlp_up'], layer, lambda acc: (acc,), (), (F32,))
        gb['mlp_up'] = mm_tn(f"mlp_up_dw{layer}", n2, du, gb['mlp_up'], layer)
        dh2, dh2b, dg, cs = rms_bwd(f"mlp_norm_bwd{layer}", h_mid, small['mlp_norm'][layer], dn2, dh)
        sg['mlp_norm'][layer] = dg[0]
        return dh2, dh2b, cs

    dkp = jnp.zeros(kp.shape, F32)
    dvp = jnp.zeros(vp.shape, F32)
    sg['b_norm'], sg['b_bq'], sg['b_bo'], sg['b_sinks'] = [None] * n_b, [None] * n_b, [None] * n_b, [None] * n_b
    dbias = [None] * n_b
    for j in reversed(range(n_b)):
        layer = n_a + j
        h_in, n1, qg, o, h_mid, mlp_saved = saved[layer]
        dh, dhb, cs = mlp_bwd(dh, dhb, h_mid, mlp_saved, layer)
        sg['b_bo'][j] = cs[0]
        (do,) = mm_nt(f"b_wo_dx{j}", dhb, wf['b_wo'], j, lambda acc: (acc,), (), (BF16,))
        gb['b_wo'] = mm_tn(f"b_wo_dw{j}", o, dhb, gb['b_wo'], j)
        dqg, dkp, dvp, dbias[j], dsink = swa_bwd(f"swa_bwd{j}", qg, kp, vp, bias_g, sink_cols[j],
                                                 _group_q(do, nb), dkp, dvp)
        sg['b_sinks'][j] = colsum(f"sink_grad{j}", dsink.reshape(16, WINDOW).T)[0]
        dq = _ungroup_q(dqg)
        sg['b_bq'][j] = colsum(f"b_bq_grad{j}", dq)[0]
        (dn1,) = mm_nt(f"b_q_dx{j}", dq, wf['b_wq'], j, lambda acc: (acc,), (), (F32,))
        gb['b_wq'] = mm_tn(f"b_q_dw{j}", n1, dq, gb['b_wq'], j)
        dh, dhb, dg, _ = rms_bwd(f"b_norm_bwd{j}", h_in, small['b_norm'][j], dn1, dh)
        sg['b_norm'][j] = dg[0]
    sg['rel_bias'] = bias_table_grad(dbias[0].reshape(bias_flat.shape),
                                     dbias[1].reshape(bias_flat.shape))[:, :N_BUCKETS].T

    dkv = jnp.concatenate([_unheads(dkp[:, WINDOW:]), _unheads(dvp[:, WINDOW:])], axis=1)
    sg['b_kv'] = colsum("b_kv_grad", dkv)[0]
    dkvb = dkv.astype(BF16)
    (dnkv,) = mm_nt("kv_dx", dkvb, wf['w_kv'], 0, lambda acc: (acc,), (), (F32,))
    gb['w_kv'] = mm_tn("kv_dw", nkv, dkvb, gb['w_kv'], 0)
    dh, dhb, dg, _ = rms_bwd("kv_norm_bwd", h_kv, small['kv_norm'], dnkv, dh)
    sg['kv_norm'] = dg[0]

    sg['a_norm'] = [None] * n_a
    for l in reversed(range(n_a)):
        h_in, n1, qkv, o, rtab, h_mid, mlp_saved = saved[l]
        dh, dhb, _ = mlp_bwd(dh, dhb, h_mid, mlp_saved, l)
        (do,) = mm_nt(f"a_wo_dx{l}", dhb, wf['a_wo'], l, lambda acc: (acc,), (), (BF16,))
        gb['a_wo'] = mm_tn(f"a_wo_dw{l}", o, dhb, gb['a_wo'], l)
        nh = d // HEAD_DIM
        qs, ks, vs = qkv[:, :d], qkv[:, d:2 * d], qkv[:, 2 * d:]
        dq_t, dk, dv = sb_bwd(f"sb_bwd{l}", _heads_t(qs, nh), _heads(qs, nh), _heads(ks, nh), _heads_t(ks, nh),
                              _heads(vs, nh), _heads_t(do, nh), _heads(do, nh), rtab)
        dqkv = jnp.concatenate([dq_t.reshape(d, s).T, _unheads(dk), _unheads(dv)], axis=1).astype(BF16)
        (dn1,) = mm_nt(f"a_qkv_dx{l}", dqkv, wf['a_wqkv'], l, lambda acc: (acc,), (), (F32,))
        gb['a_wqkv'] = mm_tn(f"a_qkv_dw{l}", n1, dqkv, gb['a_wqkv'], l)
        dh, dhb, dg, _ = rms_bwd(f"a_norm_bwd{l}", h_in, small['a_norm'][l], dn1, dh)
        sg['a_norm'][l] = dg[0]

    small_grads = {
        'a_norm': jnp.stack(sg['a_norm']), 'kv_norm': sg['kv_norm'], 'b_kv': sg['b_kv'],
        'b_norm': jnp.stack(sg['b_norm']), 'b_bq': jnp.stack(sg['b_bq']), 'b_sinks': jnp.stack(sg['b_sinks']),
        'b_bo': jnp.stack(sg['b_bo']), 'rel_bias': sg['rel_bias'], 'mlp_norm': jnp.stack(sg['mlp_norm']),
        'final_norm': sg['final_norm'],
    }
    return loss_b, dh, gb, small_grads


def _full_shape(name, shard_shape):
    if name in COL_SHARDED:
        return shard_shape[:2] + (N_DEV * shard_shape[2],)
    nl, r, n = shard_shape
    return (nl, N_DEV, r, n)


def _as_w3(name, full):
    if name in COL_SHARDED:
        return full
    nl, nd, r, n = full.shape
    return full.reshape(nl, nd * r, n)


def _pack_small(vals):
    flat = jnp.concatenate([vals[n].reshape(-1).astype(F32) for n in SMALL] + [vals['loss'].reshape(-1)])
    rows = -(-flat.shape[0] // 1024) * 8
    return jnp.pad(flat, (0, rows * 128 - flat.shape[0])).reshape(rows, 128)


def _unpack_small(packed, shapes):
    flat = packed.reshape(-1)
    out, off = {}, 0
    for n in SMALL + ['loss']:
        size = int(np.prod(shapes[n]))
        out[n] = flat[off:off + size].reshape(shapes[n])
        off += size
    return out


def kernel(x, a_norm, a_wqkv, a_wo, kv_norm, w_kv, b_kv, b_norm, b_wq, b_bq, b_sinks, b_wo, b_bo, rel_bias, mlp_norm, mlp_up, mlp_down, final_norm, loss_target, m_a_norm, m_a_wqkv, m_a_wo, m_kv_norm, m_w_kv, m_b_kv, m_b_norm, m_b_wq, m_b_bq, m_b_sinks, m_b_wo, m_b_bo, m_rel_bias, m_mlp_norm, m_mlp_up, m_mlp_down, m_final_norm, v_a_norm, v_a_wqkv, v_a_wo, v_kv_norm, v_w_kv, v_b_kv, v_b_norm, v_b_wq, v_b_bq, v_b_sinks, v_b_wo, v_b_bo, v_rel_bias, v_mlp_norm, v_mlp_up, v_mlp_down, v_final_norm):
    w = dict(a_norm=a_norm, a_wqkv=a_wqkv, a_wo=a_wo, kv_norm=kv_norm, w_kv=w_kv, b_kv=b_kv, b_norm=b_norm,
             b_wq=b_wq, b_bq=b_bq, b_sinks=b_sinks, b_wo=b_wo, b_bo=b_bo, rel_bias=rel_bias, mlp_norm=mlp_norm,
             mlp_up=mlp_up, mlp_down=mlp_down, final_norm=final_norm)
    m = dict(a_norm=m_a_norm, a_wqkv=m_a_wqkv, a_wo=m_a_wo, kv_norm=m_kv_norm, w_kv=m_w_kv, b_kv=m_b_kv,
             b_norm=m_b_norm, b_wq=m_b_wq, b_bq=m_b_bq, b_sinks=m_b_sinks, b_wo=m_b_wo, b_bo=m_b_bo,
             rel_bias=m_rel_bias, mlp_norm=m_mlp_norm, mlp_up=m_mlp_up, mlp_down=m_mlp_down, final_norm=m_final_norm)
    v = dict(a_norm=v_a_norm, a_wqkv=v_a_wqkv, a_wo=v_a_wo, kv_norm=v_kv_norm, w_kv=v_w_kv, b_kv=v_b_kv,
             b_norm=v_b_norm, b_wq=v_b_wq, b_bq=v_b_bq, b_sinks=v_b_sinks, b_wo=v_b_wo, b_bo=v_b_bo,
             rel_bias=v_rel_bias, mlp_norm=v_mlp_norm, mlp_up=v_mlp_up, mlp_down=v_mlp_down, final_norm=v_final_norm)
    px, py, pc = _place()
    me = 4 * px + 2 * py + pc
    chip = (2 * px + py).astype(jnp.int32)
    core = pc.astype(jnp.int32)

    shards = {n: (w[n][None] if w[n].ndim == 2 else w[n]).astype(BF16) for n in BIG}
    an_pad = jnp.zeros((8, 128), F32).at[:a_norm.shape[0]].set(a_norm)
    names = BIG + ['a_norm']
    full = all_gather_weights(names, [shards[n] for n in BIG] + [an_pad],
                              [_full_shape(n, shards[n].shape) for n in BIG] + [(N_DEV, 8, 128)])
    full = dict(zip(names, full))
    wf = {n: _as_w3(n, full[n]) for n in BIG}
    n_a = a_norm.shape[0]
    small = {n: w[n] for n in SMALL}
    small['a_norm'] = full['a_norm'][:, :n_a].transpose(1, 0, 2).reshape(n_a, -1)

    gbufs = {n: jnp.zeros(wf[n].shape, BF16) for n in BIG}
    loss_b, grad_x, gb, sgrads = local_step(x[0], loss_target[0], wf, small, gbufs)

    gfull = [gb[n].reshape(full[n].shape) for n in BIG]
    part_shapes = [shards[n].shape for n in BIG]
    recv1 = sibling_exchange(BIG, gfull, part_shapes)
    parts = [sibling_sum(f"rs_sibling_sum_{n}", n in COL_SHARDED, g, r, core) for n, g, r in zip(BIG, gfull, recv1)]
    recv2 = chip_exchange(BIG, parts, part_shapes)
    out = {}
    for n, p, r in zip(BIG, parts, recv2):
        out[n] = reduce_adamw(f"adamw_{n}", p, r, chip, w[n], m[n], v[n])

    sgrads['loss'] = loss_b[0, :1]
    gathered = all_gather_rows(_pack_small(sgrads))
    shapes = {n: w[n].shape for n in SMALL}
    shapes['a_norm'] = (n_a, a_norm.shape[1] * N_DEV)
    shapes['loss'] = (1,)
    zeros1 = jnp.zeros((1,), F32)

    def packed(src):
        vals = {n: src[n] for n in SMALL}
        vals['a_norm'] = jnp.zeros(shapes['a_norm'], F32)
        vals['loss'] = zeros1
        return _pack_small(vals)

    sm = small_adamw("adamw_small", gathered, packed(w), packed(m), packed(v))
    sm = [_unpack_small(t, shapes) for t in sm]
    g_an = lax.dynamic_slice_in_dim(sm[0]['a_norm'], me * a_norm.shape[1], a_norm.shape[1], axis=1)
    pad = lambda t: jnp.zeros((8, 128), F32).at[:n_a].set(t)
    gathered_an = jnp.zeros((N_DEV, 8, 128), F32).at[0].set(pad(g_an))
    an = small_adamw("adamw_a_norm", gathered_an, pad(a_norm), pad(m_a_norm), pad(v_a_norm))
    for i in range(4):
        sm[i]['a_norm'] = an[i][:n_a]
    for n in BIG:
        for i in range(4):
            sm[i][n] = out[n][i]
    loss = sm[0]['loss'][0]
    return (loss, grad_x[None], *[sm[0][n] for n in WEIGHTS], *[sm[1][n] for n in WEIGHTS],
            *[sm[2][n] for n in WEIGHTS], *[sm[3][n] for n in WEIGHTS])
```

```python
import functools
import math

import numpy as np
import jax
import jax.numpy as jnp
from jax import lax
from jax.experimental import pallas as pl
from jax.experimental.pallas import tpu as pltpu

F32 = jnp.float32
BF16 = jnp.bfloat16
MESH = pl.DeviceIdType.MESH

N_DEV = 8
HEAD_DIM = 64
WINDOW = 128
N_BUCKETS = 32
EPS = 1e-5
NEG_INF = -1e30
Q_SCALE = 1.0 / math.sqrt(HEAD_DIM)
LOG2E = 1.4426950408889634

ADAM_LR, ADAM_B1, ADAM_B2, ADAM_EPS, ADAM_WD, ADAM_STEP = 0.001, 0.9, 0.999, 1e-08, 0.01, 10

SB_BQ = 512
SB_BK = 128
ROW_TILE = 512
VMEM_LIMIT = 56 * 1024 * 1024

WEIGHTS = ['a_norm', 'a_wqkv', 'a_wo', 'kv_norm', 'w_kv', 'b_kv', 'b_norm', 'b_wq', 'b_bq', 'b_sinks', 'b_wo',
           'b_bo', 'rel_bias', 'mlp_norm', 'mlp_up', 'mlp_down', 'final_norm']
BIG = ['a_wqkv', 'a_wo', 'w_kv', 'b_wq', 'b_wo', 'mlp_up', 'mlp_down']
COL_SHARDED = ('a_wqkv', 'mlp_up')
SMALL = ['a_norm', 'kv_norm', 'b_kv', 'b_norm', 'b_bq', 'b_sinks', 'b_bo', 'rel_bias', 'mlp_norm', 'final_norm']


def _params(sem=None):
    return pltpu.CompilerParams(dimension_semantics=sem, vmem_limit_bytes=VMEM_LIMIT)


def _pick(n, cands):
    for c in cands:
        if n % c == 0:
            return c
    raise ValueError(n)


def _tile(n, want):
    return n if n <= want else _pick(n, (want, want // 2, want // 4))


def mm_nn(name, a, w3, layer, epilogue, extras, out_dtypes):
    m, k = a.shape
    _, kw, n = w3.shape
    assert kw == k
    tm = _tile(m, 1024 if k <= 1024 else 512)
    tn = _tile(n, 1024)
    ne, no = len(extras), len(out_dtypes)

    def body(a_ref, w_ref, *rest):
        ex, outs = rest[:ne], rest[ne:ne + no]
        res = epilogue(jnp.dot(a_ref[...], w_ref[...], preferred_element_type=F32), *[e[...] for e in ex])
        for o, r in zip(outs, res):
            o[...] = r.astype(o.dtype)

    tile = pl.BlockSpec((tm, tn), lambda i, j: (i, j))
    ex_specs = [tile if e.shape[0] == m else pl.BlockSpec((1, tn), lambda i, j: (0, j)) for e in extras]
    return pl.pallas_call(
        body, name=name, grid=(m // tm, n // tn),
        in_specs=[pl.BlockSpec((tm, k), lambda i, j: (i, 0)),
                  pl.BlockSpec((None, k, tn), lambda i, j: (layer, 0, j))] + ex_specs,
        out_specs=[tile] * no,
        out_shape=[jax.ShapeDtypeStruct((m, n), d) for d in out_dtypes],
        compiler_params=_params(("parallel", "parallel")),
    )(a, w3, *extras)


def mm_nt(name, dy, w3, layer, epilogue, extras, out_dtypes):
    m, n = dy.shape
    _, k, nw = w3.shape
    assert nw == n
    tm = _tile(m, 1024 if n <= 1024 else 512)
    tko = _tile(k, 1024)
    ne, no = len(extras), len(out_dtypes)

    def body(a_ref, w_ref, *rest):
        ex, outs = rest[:ne], rest[ne:ne + no]
        acc = lax.dot_general(a_ref[...], w_ref[...], (((1,), (1,)), ((), ())), preferred_element_type=F32)
        res = epilogue(acc, *[e[...] for e in ex])
        for o, v in zip(outs, res):
            o[...] = v.astype(o.dtype)

    tile = pl.BlockSpec((tm, tko), lambda i, ko: (i, ko))
    return pl.pallas_call(
        body, name=name, grid=(m // tm, k // tko),
        in_specs=[pl.BlockSpec((tm, n), lambda i, ko: (i, 0)),
                  pl.BlockSpec((None, tko, n), lambda i, ko: (layer, ko, 0))] + [tile] * ne,
        out_specs=[tile] * no,
        out_shape=[jax.ShapeDtypeStruct((m, k), d) for d in out_dtypes],
        compiler_params=_params(("parallel", "parallel")),
    )(dy, w3, *extras)


def mm_tn(name, x, dy, gbuf, layer):
    s, k = x.shape
    _, kw, n = gbuf.shape
    assert kw == k and dy.shape == (s, n)
    tkk = _tile(k, 512)
    tn = _tile(n, 1024)

    def body(x_ref, dy_ref, g_in, g_out):
        del g_in
        g_out[...] = lax.dot_general(x_ref[...], dy_ref[...], (((0,), (0,)), ((), ())),
                                     preferred_element_type=F32).astype(g_out.dtype)

    return pl.pallas_call(
        body, name=name, grid=(k // tkk, n // tn),
        in_specs=[pl.BlockSpec((s, tkk), lambda ki, j: (0, ki)),
                  pl.BlockSpec((s, tn), lambda ki, j: (0, j)),
                  pl.BlockSpec(memory_space=pl.ANY)],
        out_specs=pl.BlockSpec((None, tkk, tn), lambda ki, j: (layer, ki, j)),
        out_shape=jax.ShapeDtypeStruct(gbuf.shape, gbuf.dtype),
        input_output_aliases={2: 0},
        compiler_params=_params(("parallel", "parallel")),
    )(x, dy, gbuf)


def rms_fwd(name, h, g):
    s, d = h.shape
    tr = _pick(s, (ROW_TILE, 256, 128))

    def body(h_ref, g_ref, o_ref):
        x = h_ref[...]
        r = lax.rsqrt(jnp.mean(x * x, axis=-1, keepdims=True) + EPS)
        o_ref[...] = (x * r * g_ref[...]).astype(o_ref.dtype)

    return pl.pallas_call(
        body, name=name, grid=(s // tr,),
        in_specs=[pl.BlockSpec((tr, d), lambda i: (i, 0)), pl.BlockSpec((1, d), lambda i: (0, 0))],
        out_specs=pl.BlockSpec((tr, d), lambda i: (i, 0)),
        out_shape=jax.ShapeDtypeStruct((s, d), BF16),
        compiler_params=_params(("parallel",)),
    )(h, g.reshape(1, d))


def rms_bwd(name, h, g, dn, dres):
    s, d = h.shape
    tr = _pick(s, (ROW_TILE, 256, 128))

    def body(h_ref, g_ref, dn_ref, dres_ref, dx_ref, dxb_ref, dg_ref, cs_ref):
        i = pl.program_id(0)
        x = h_ref[...]
        r = lax.rsqrt(jnp.mean(x * x, axis=-1, keepdims=True) + EPS)
        xh = x * r
        dn_ = dn_ref[...]
        dyg = dn_ * g_ref[...]
        dx = dres_ref[...] + r * (dyg - xh * jnp.mean(dyg * xh, axis=-1, keepdims=True))
        dx_ref[...] = dx
        dxb_ref[...] = dx.astype(BF16)

        @pl.when(i == 0)
        def _():
            dg_ref[...] = jnp.zeros_like(dg_ref)
            cs_ref[...] = jnp.zeros_like(cs_ref)

        dg_ref[...] += jnp.sum(dn_ * xh, axis=0, keepdims=True)
        cs_ref[...] += jnp.sum(dx, axis=0, keepdims=True)

    row = pl.BlockSpec((tr, d), lambda i: (i, 0))
    vec = pl.BlockSpec((1, d), lambda i: (0, 0))
    return pl.pallas_call(
        body, name=name, grid=(s // tr,),
        in_specs=[row, vec, row, row],
        out_specs=[row, row, vec, vec],
        out_shape=[jax.ShapeDtypeStruct((s, d), F32), jax.ShapeDtypeStruct((s, d), BF16),
                   jax.ShapeDtypeStruct((1, d), F32), jax.ShapeDtypeStruct((1, d), F32)],
        compiler_params=_params(("arbitrary",)),
    )(h, g.reshape(1, d), dn, dres)


def loss_head(h, g, target):
    s, d = h.shape
    tr = _pick(s, (ROW_TILE, 256, 128))

    def body(h_ref, g_ref, t_ref, dx_ref, dxb_ref, dg_ref, loss_ref):
        i = pl.program_id(0)
        x = h_ref[...]
        r = lax.rsqrt(jnp.mean(x * x, axis=-1, keepdims=True) + EPS)
        xh = x * r
        gw = g_ref[...]
        err = xh * gw - t_ref[...]
        dn_ = err * (1.0 / d)
        dyg = dn_ * gw
        dx = r * (dyg - xh * jnp.mean(dyg * xh, axis=-1, keepdims=True))
        dx_ref[...] = dx
        dxb_ref[...] = dx.astype(BF16)

        @pl.when(i == 0)
        def _():
            dg_ref[...] = jnp.zeros_like(dg_ref)
            loss_ref[...] = jnp.zeros_like(loss_ref)

        dg_ref[...] += jnp.sum(dn_ * xh, axis=0, keepdims=True)
        per_row = jnp.sum(err * err, axis=-1, keepdims=True) * (0.5 / d)
        loss_ref[...] += jnp.broadcast_to(jnp.sum(per_row, axis=0, keepdims=True), loss_ref.shape)

    row = pl.BlockSpec((tr, d), lambda i: (i, 0))
    vec = pl.BlockSpec((1, d), lambda i: (0, 0))
    return pl.pallas_call(
        body, name="loss_head", grid=(s // tr,),
        in_specs=[row, vec, row],
        out_specs=[row, row, vec, pl.BlockSpec((1, 128), lambda i: (0, 0))],
        out_shape=[jax.ShapeDtypeStruct((s, d), F32), jax.ShapeDtypeStruct((s, d), BF16),
                   jax.ShapeDtypeStruct((1, d), F32), jax.ShapeDtypeStruct((1, 128), F32)],
        compiler_params=_params(("arbitrary",)),
    )(h, g.reshape(1, d), target)


def colsum(name, x):
    s, n = x.shape
    tr = _pick(s, (ROW_TILE, 256, 128))

    def body(x_ref, o_ref):
        @pl.when(pl.program_id(0) == 0)
        def _():
            o_ref[...] = jnp.zeros_like(o_ref)

        o_ref[...] += jnp.sum(x_ref[...].astype(F32), axis=0, keepdims=True)

    return pl.pallas_call(
        body, name=name, grid=(s // tr,),
        in_specs=[pl.BlockSpec((tr, n), lambda i: (i, 0))],
        out_specs=pl.BlockSpec((1, n), lambda i: (0, 0)),
        out_shape=jax.ShapeDtypeStruct((1, n), F32),
        compiler_params=_params(("arbitrary",)),
    )(x)


def _tri_rows(reverse):
    i = np.arange(SB_BK)
    tri = (i[None, :] >= i[:, None]) if reverse else (i[None, :] <= i[:, None])
    tri = np.concatenate([tri, tri], axis=1)
    return jnp.asarray(np.concatenate([tri, np.ones((8, 2 * SB_BK), bool)], axis=0), BF16)


def _hi_lo_rows(x):
    hi = x.astype(BF16)
    lo = (x - hi.astype(F32)).astype(BF16)
    return jnp.concatenate([hi, lo], axis=0)


def _softplus2(zs):
    neg_abs = lax.bitcast_convert_type(lax.bitcast_convert_type(zs, jnp.uint32) | jnp.uint32(0x80000000), F32)
    return jnp.maximum(zs, 0.0) + jnp.log2(1.0 + jnp.exp2(neg_abs))


def _pair_mask(first_rel_block, bq):
    key = lax.broadcasted_iota(jnp.int32, (2 * SB_BK, bq), 0) + first_rel_block * SB_BK
    qry = lax.broadcasted_iota(jnp.int32, (2 * SB_BK, bq), 1)
    return key < qry


def _row_of(table8, sub8, r):
    return jnp.sum(jnp.where(sub8 == r, table8, 0.0), axis=0, keepdims=True)


def _keys(j0):
    return pl.ds(pl.multiple_of(j0 * SB_BK, 2 * SB_BK), 2 * SB_BK)


def sb_fwd(name, qt, k, vt):
    nh, dh, s = qt.shape
    bq = SB_BQ
    per_q = bq // SB_BK
    nkb = s // SB_BK
    assert s % bq == 0 and per_q == 4 and nkb % 8 == 0

    def body(q_ref, k_ref, v_ref, a_ref, o_ref, rtab_ref, acc, zbuf, wbuf):
        i = pl.program_id(1)
        qb = q_ref[...] * Q_SCALE
        tri = a_ref[...]
        sub8 = lax.broadcasted_iota(jnp.int32, (8, bq), 0)
        acc[...] = jnp.zeros_like(acc)
        rtab_ref[...] = jnp.zeros_like(rtab_ref)

        def scores(j0):
            return jnp.dot(k_ref[_keys(j0), :], qb, preferred_element_type=F32) * LOG2E

        def pair(j0, slot, run, rt8, mask, has_prev):
            zs = zbuf[slot]
            zbuf[1 - slot] = scores(jnp.maximum(j0 - 2, 0))
            if has_prev:
                acc[...] += jnp.dot(v_ref[:, _keys(j0 + 2)], wbuf[1 - slot], preferred_element_type=F32)
            p = _softplus2(zs)
            if mask is not None:
                p = jnp.where(mask, p, 0.0)
            cr1 = jnp.dot(tri, _hi_lo_rows(p[SB_BK:]), preferred_element_type=F32)
            cr0 = jnp.dot(tri, _hi_lo_rows(p[:SB_BK]), preferred_element_type=F32)
            run1 = run + cr1[SB_BK:SB_BK + 1]
            w = jnp.exp2(jnp.concatenate([zs[:SB_BK] - cr0[:SB_BK] - run1, zs[SB_BK:] - cr1[:SB_BK] - run], axis=0))
            if mask is not None:
                w = jnp.where(mask, w, 0.0)
            wbuf[slot] = w.astype(BF16)
            rt8 = jnp.where(sub8 == (j0 + 1) % 8, run, jnp.where(sub8 == j0 % 8, run1, rt8))
            rtab_ref[pl.ds(pl.multiple_of((j0 // 8) * 8, 8), 8), :] = rt8
            return run1 + cr0[SB_BK:SB_BK + 1], rt8

        top = i * per_q
        zbuf[0] = scores(top + 2)
        carry = (jnp.zeros((1, bq), F32), jnp.zeros((8, bq), F32))
        carry = pair(top + 2, 0, *carry, _pair_mask(2, bq), False)
        carry = pair(top, 1, *carry, _pair_mask(0, bq), True)

        def step(it, c):
            j0 = top - 2 - 4 * it
            c = pair(j0, 0, *c, None, True)
            return pair(j0 - 2, 1, *c, None, True)

        lax.fori_loop(0, i, step, carry)
        acc[...] += jnp.dot(v_ref[:, _keys(0)], wbuf[1], preferred_element_type=F32)
        o_ref[...] = acc[...].astype(o_ref.dtype)

    qspec = pl.BlockSpec((None, dh, bq), lambda h, i: (h, 0, i))
    return pl.pallas_call(
        body, name=name, grid=(nh, s // bq),
        in_specs=[qspec, pl.BlockSpec((None, s, dh), lambda h, i: (h, 0, 0)),
                  pl.BlockSpec((None, dh, s), lambda h, i: (h, 0, 0)),
                  pl.BlockSpec((SB_BK + 8, 2 * SB_BK), lambda h, i: (0, 0))],
        out_specs=[qspec, pl.BlockSpec((None, nkb, bq), lambda h, i: (h, 0, i))],
        out_shape=[jax.ShapeDtypeStruct((nh, dh, s), BF16), jax.ShapeDtypeStruct((nh, nkb, s), F32)],
        scratch_shapes=[pltpu.VMEM((dh, bq), F32), pltpu.VMEM((2, 2 * SB_BK, bq), F32),
                        pltpu.VMEM((2, 2 * SB_BK, bq), BF16)],
        compiler_params=_params(("parallel", "arbitrary")),
    )(qt, k, vt, _tri_rows(True))


def sb_bwd(name, qt, q, k, kt, v, dot_, do, rtab):
    nh, dh, s = qt.shape
    bq = SB_BQ
    per_q = bq // SB_BK
    nkb = s // SB_BK

    def body(qt_ref, q_ref, k_ref, kt_ref, v_ref, dot_ref, do_ref, rtab_ref, ar_ref, af_ref,
             dq_ref, dk_ref, dv_ref, dq_acc, zbuf, dwbuf, dzbuf, wbuf):
        i = pl.program_id(1)

        @pl.when(i == 0)
        def _():
            dk_ref[...] = jnp.zeros_like(dk_ref)
            dv_ref[...] = jnp.zeros_like(dv_ref)

        qtb = qt_ref[...] * Q_SCALE
        qrows = q_ref[...] * Q_SCALE
        dotb = dot_ref[...]
        dorows = do_ref[...]
        tri_rev = ar_ref[...][:SB_BK]
        tri_fwd = af_ref[...]
        sub8 = lax.broadcasted_iota(jnp.int32, (8, bq), 0)
        dq_acc[...] = jnp.zeros_like(dq_acc)
        last = i * per_q + 2

        def issue(j0, slot):
            zbuf[slot] = jnp.dot(k_ref[_keys(j0), :], qtb, preferred_element_type=F32) * LOG2E
            dwbuf[slot] = jnp.dot(v_ref[_keys(j0), :], dotb, preferred_element_type=F32)

        def retire(j0, slot):
            keys = _keys(j0)
            dq_acc[...] += jnp.dot(kt_ref[:, keys], dzbuf[slot], preferred_element_type=F32)
            dk_ref[keys, :] += jnp.dot(dzbuf[slot], qrows, preferred_element_type=F32)
            dv_ref[keys, :] += jnp.dot(wbuf[slot], dorows, preferred_element_type=F32)

        def pair(j0, slot, g_run, mask):
            zs = zbuf[slot]
            dw = dwbuf[slot]
            issue(jnp.minimum(j0 + 2, last), 1 - slot)
            retire(jnp.maximum(j0 - 2, 0), 1 - slot)
            p_raw = _softplus2(zs)
            p = p_raw if mask is None else jnp.where(mask, p_raw, 0.0)
            c0 = jnp.dot(tri_rev, _hi_lo_rows(p[:SB_BK]), preferred_element_type=F32)
            c1 = jnp.dot(tri_rev, _hi_lo_rows(p[SB_BK:]), preferred_element_type=F32)
            rt8 = rtab_ref[pl.ds(pl.multiple_of((j0 // 8) * 8, 8), 8), :]
            r0 = _row_of(rt8, sub8, j0 % 8)
            r1 = _row_of(rt8, sub8, (j0 + 1) % 8)
            w = jnp.exp2(jnp.concatenate([zs[:SB_BK] - c0 - r0, zs[SB_BK:] - c1 - r1], axis=0))
            if mask is not None:
                w = jnp.where(mask, w, 0.0)
            g = w * dw
            gg0 = jnp.dot(tri_fwd, _hi_lo_rows(g[:SB_BK]), preferred_element_type=F32)
            gg1 = jnp.dot(tri_fwd, _hi_lo_rows(g[SB_BK:]), preferred_element_type=F32)
            g_run1 = g_run + gg0[SB_BK:SB_BK + 1]
            g_pre = jnp.concatenate([gg0[:SB_BK] + g_run, gg1[:SB_BK] + g_run1], axis=0)
            dz = g - jnp.exp2(zs - p_raw) * g_pre
            if mask is not None:
                dz = jnp.where(mask, dz, 0.0)
            dzbuf[slot] = dz.astype(BF16)
            wbuf[slot] = w.astype(BF16)
            return g_run1 + gg1[SB_BK:SB_BK + 1]

        issue(0, 0)
        dzbuf[1] = jnp.zeros((2 * SB_BK, bq), BF16)
        wbuf[1] = jnp.zeros((2 * SB_BK, bq), BF16)

        def step(it, g_run):
            g_run = pair(4 * it, 0, g_run, None)
            return pair(4 * it + 2, 1, g_run, None)

        g_run = lax.fori_loop(0, i, step, jnp.zeros((1, bq), F32))
        g_run = pair(last - 2, 0, g_run, _pair_mask(0, bq))
        pair(last, 1, g_run, _pair_mask(2, bq))
        retire(last, 1)
        dq_ref[...] = dq_acc[...] * Q_SCALE

    tspec = pl.BlockSpec((None, dh, bq), lambda h, i: (h, 0, i))
    rspec = pl.BlockSpec((None, bq, dh), lambda h, i: (h, i, 0))
    kspec = pl.BlockSpec((None, s, dh), lambda h, i: (h, 0, 0))
    aspec = pl.BlockSpec((SB_BK + 8, 2 * SB_BK), lambda h, i: (0, 0))
    pair_f32 = pltpu.VMEM((2, 2 * SB_BK, bq), F32)
    pair_bf16 = pltpu.VMEM((2, 2 * SB_BK, bq), BF16)
    return pl.pallas_call(
        body, name=name, grid=(nh, s // bq),
        in_specs=[tspec, rspec, kspec, pl.BlockSpec((None, dh, s), lambda h, i: (h, 0, 0)), kspec, tspec, rspec,
                  pl.BlockSpec((None, nkb, bq), lambda h, i: (h, 0, i)), aspec, aspec],
        out_specs=[tspec, kspec, kspec],
        out_shape=[jax.ShapeDtypeStruct((nh, dh, s), F32), jax.ShapeDtypeStruct((nh, s, dh), F32),
                   jax.ShapeDtypeStruct((nh, s, dh), F32)],
        scratch_shapes=[pltpu.VMEM((dh, bq), F32), pair_f32, pair_f32, pair_bf16, pair_bf16],
        compiler_params=_params(("parallel", "arbitrary")),
    )(qt, q, k, kt, v, dot_, do, rtab, _tri_rows(True), _tri_rows(False))


SWA_QB = 2


def _swa_probs(qt, kb, bias_t, sink, i):
    cols = qt.shape[1]
    sc = jnp.dot(kb, qt, preferred_element_type=F32) + bias_t
    kj = lax.broadcasted_iota(jnp.int32, (2 * WINDOW, cols), 0)
    qi = lax.broadcasted_iota(jnp.int32, (2 * WINDOW, cols), 1) & (WINDOW - 1)
    dist = qi + WINDOW - kj
    valid = (dist >= 0) & (dist < WINDOW) & ((kj >= WINDOW) | (i > 0))
    sc = jnp.where(valid, sc, NEG_INF)
    mx = jnp.maximum(jnp.max(sc, axis=0, keepdims=True), sink)
    p = jnp.exp(sc - mx)
    p_sink = jnp.exp(sink - mx)
    inv = 1.0 / (jnp.sum(p, axis=0, keepdims=True) + p_sink)
    return p, p_sink, inv


def _band(i):
    return pl.ds(pl.multiple_of(i * WINDOW, WINDOW), 2 * WINDOW)


def swa_fwd(name, qgt, kp, vpt, bias_t, sink_row):
    ng, nb, dh, cols = qgt.shape
    sp = kp.shape[1]
    assert nb % SWA_QB == 0

    def body(q_ref, k_ref, v_ref, bias_ref, sink_ref, o_ref):
        for u in range(SWA_QB):
            i = pl.program_id(1) * SWA_QB + u
            p, _, inv = _swa_probs(q_ref[u] * Q_SCALE, k_ref[_band(i), :], bias_ref[...], sink_ref[...], i)
            o_ref[u] = (jnp.dot(v_ref[:, _band(i)], p.astype(BF16), preferred_element_type=F32) * inv
                        ).astype(o_ref.dtype)

    qspec = pl.BlockSpec((None, SWA_QB, dh, cols), lambda g, i: (g, i, 0, 0))
    return pl.pallas_call(
        body, name=name, grid=(ng, nb // SWA_QB),
        in_specs=[qspec, pl.BlockSpec((None, sp, dh), lambda g, i: (g, 0, 0)),
                  pl.BlockSpec((None, dh, sp), lambda g, i: (g, 0, 0)),
                  pl.BlockSpec((None, 2 * WINDOW, cols), lambda g, i: (g, 0, 0)),
                  pl.BlockSpec((None, 1, cols), lambda g, i: (g, 0, 0))],
        out_specs=qspec,
        out_shape=jax.ShapeDtypeStruct(qgt.shape, BF16),
        compiler_params=_params(("parallel", "arbitrary")),
    )(qgt, kp, vpt, bias_t, sink_row)


def swa_bwd(name, qgt, qg, kp, kpt, vp, bias_t, sink_row, dogt, dog, dk_in, dv_in):
    ng, nb, dh, cols = qgt.shape
    sp = kp.shape[1]

    def body(qt_ref, q_ref, k_ref, kt_ref, v_ref, bias_ref, sink_ref, dot_ref, do_ref, dki_ref, dvi_ref,
             dq_ref, dk_ref, dv_ref, db_ref, ds_ref):
        @pl.when(pl.program_id(1) == 0)
        def _():
            dk_ref[...] = dki_ref[...]
            dv_ref[...] = dvi_ref[...]
            db_ref[...] = jnp.zeros_like(db_ref)
            ds_ref[...] = jnp.zeros_like(ds_ref)

        for u in range(SWA_QB):
            i = pl.program_id(1) * SWA_QB + u
            band = _band(i)
            kb = k_ref[band, :]
            p, p_sink, inv = _swa_probs(qt_ref[u] * Q_SCALE, kb, bias_ref[...], sink_ref[...], i)
            p = p * inv
            dp = jnp.dot(v_ref[band, :], dot_ref[u], preferred_element_type=F32)
            delta = jnp.sum(p * dp, axis=0, keepdims=True)
            dsc = p * (dp - delta)
            ds_ref[...] -= p_sink * inv * delta
            db_ref[...] += dsc
            dscb = dsc.astype(BF16)
            dq_ref[u] = (jnp.dot(kt_ref[:, band], dscb, preferred_element_type=F32) * Q_SCALE).astype(dq_ref.dtype)
            dk_ref[band, :] += jnp.dot(dscb, q_ref[u] * Q_SCALE, preferred_element_type=F32)
            dv_ref[band, :] += jnp.dot(p.astype(BF16), do_ref[u], preferred_element_type=F32)

    tspec = pl.BlockSpec((None, SWA_QB, dh, cols), lambda g, i: (g, i, 0, 0))
    rspec = pl.BlockSpec((None, SWA_QB, cols, dh), lambda g, i: (g, i, 0, 0))
    kspec = pl.BlockSpec((None, sp, dh), lambda g, i: (g, 0, 0))
    ktspec = pl.BlockSpec((None, dh, sp), lambda g, i: (g, 0, 0))
    bspec = pl.BlockSpec((None, 2 * WINDOW, cols), lambda g, i: (g, 0, 0))
    sspec = pl.BlockSpec((None, 1, cols), lambda g, i: (g, 0, 0))
    return pl.pallas_call(
        body, name=name, grid=(ng, nb // SWA_QB),
        in_specs=[tspec, rspec, kspec, ktspec, kspec, bspec, sspec, tspec, rspec, kspec, kspec],
        out_specs=[tspec, kspec, kspec, bspec, sspec],
        out_shape=[jax.ShapeDtypeStruct(qgt.shape, BF16), jax.ShapeDtypeStruct(kp.shape, F32),
                   jax.ShapeDtypeStruct(kp.shape, F32), jax.ShapeDtypeStruct(bias_t.shape, F32),
                   jax.ShapeDtypeStruct(sink_row.shape, F32)],
        compiler_params=_params(("parallel", "arbitrary")),
    )(qgt, qg, kp, kpt, vp, bias_t, sink_row, dogt, dog, dk_in, dv_in)


def _bucket_onehot():
    qi = np.arange(WINDOW)[:, None]
    kj = np.arange(2 * WINDOW)[None, :]
    n = np.maximum(qi + WINDOW - kj, 0)
    max_exact = N_BUCKETS // 2
    nf = np.maximum(n, 1).astype(np.float64)
    val = np.log(nf / max_exact) / math.log(WINDOW / max_exact) * (N_BUCKETS - max_exact)
    assert np.all(np.abs(val - np.round(val))[(n > max_exact) & (n < WINDOW)] > 1e-3)
    large = np.minimum(max_exact + val.astype(np.int64), N_BUCKETS - 1)
    bucket = np.where(n < max_exact, n, large).reshape(-1)
    onehot = np.zeros((128, bucket.size), np.float32)
    onehot[bucket, np.arange(bucket.size)] = 1.0
    return onehot


def _split3(x):
    a = x.astype(BF16)
    r = x - a.astype(F32)
    b = r.astype(BF16)
    c = (r - b.astype(F32)).astype(BF16)
    return a, b, c


def bias_table(rel_bias):
    nh = rel_bias.shape[1]
    oh = jnp.asarray(_bucket_onehot(), BF16)
    n = oh.shape[1]
    tn = 4096
    rb = jnp.zeros((nh, 128), F32).at[:, :N_BUCKETS].set(rel_bias.T)

    def body(rb_ref, oh_ref, o_ref):
        o_ref[...] = sum(jnp.dot(t, oh_ref[...], preferred_element_type=F32) for t in _split3(rb_ref[...]))

    return pl.pallas_call(
        body, name="bias_table", grid=(n // tn,),
        in_specs=[pl.BlockSpec((nh, 128), lambda i: (0, 0)), pl.BlockSpec((128, tn), lambda i: (0, i))],
        out_specs=pl.BlockSpec((nh, tn), lambda i: (0, i)),
        out_shape=jax.ShapeDtypeStruct((nh, n), F32),
        compiler_params=_params(("parallel",)),
    )(rb, oh)


def bias_table_grad(db0, db1):
    nh, n = db0.shape
    oh = jnp.asarray(_bucket_onehot(), BF16)
    tn = 4096

    def body(a_ref, b_ref, oh_ref, o_ref):
        @pl.when(pl.program_id(0) == 0)
        def _():
            o_ref[...] = jnp.zeros_like(o_ref)

        o_ref[...] += sum(lax.dot_general(t, oh_ref[...], (((1,), (1,)), ((), ())), preferred_element_type=F32)
                          for t in _split3(a_ref[...] + b_ref[...]))

    blk = pl.BlockSpec((nh, tn), lambda i: (0, i))
    return pl.pallas_call(
        body, name="bias_table_grad", grid=(n // tn,),
        in_specs=[blk, blk, pl.BlockSpec((128, tn), lambda i: (0, i))],
        out_specs=pl.BlockSpec((nh, 128), lambda i: (0, 0)),
        out_shape=jax.ShapeDtypeStruct((nh, 128), F32),
        compiler_params=_params(("arbitrary",)),
    )(db0, db1, oh)


def _owner_view(ref, name, d):
    if name == 'a_norm':
        return ref.at[d]
    if name in COL_SHARDED:
        n = ref.shape[2] // N_DEV
        return ref.at[:, :, pl.ds(pl.multiple_of(d * n, 128), n)]
    return ref.at[:, d]


def _place():
    return lax.axis_index("x"), lax.axis_index("y"), lax.axis_index("c")


def all_gather_weights(names, shards, full_shapes):
    n = len(names)

    def body(*refs):
        ins, outs = refs[:n], refs[n:2 * n]
        send_sems, recv_sems, local_sems = refs[2 * n:]
        x, y, c = _place()
        me, sibling = (x, y, c), (x, y, 1 - c)
        chips = [(1 - x, y), (x, 1 - y), (1 - x, 1 - y)]

        def dev(p):
            return 4 * p[0] + 2 * p[1] + p[2]

        def copy(t, k, block, to, src=None):
            dst = _owner_view(outs[t], names[t], dev(block))
            return pltpu.make_async_remote_copy(
                src_ref=dst if src is None else src, dst_ref=dst,
                send_sem=send_sems.at[t, k], recv_sem=recv_sems.at[t, k], device_id=to, device_id_type=MESH)

        mine = [pltpu.make_async_copy(ins[t], _owner_view(outs[t], names[t], dev(me)), local_sems.at[t])
                for t in range(n)]
        for cp in mine:
            cp.start()
        first = []
        for t in range(n):
            first.append(copy(t, 0, me, sibling, src=ins[t]))
            first += [copy(t, 1 + j, me, (*chip, c), src=ins[t]) for j, chip in enumerate(chips)]
        for cp in first:
            cp.start()
        passed = []
        for j, chip in enumerate(chips):
            for t in range(n):
                copy(t, 1 + j, (*chip, c), me).wait_recv()
                fwd = copy(t, 4 + j, (*chip, c), sibling)
                fwd.start()
                passed.append(fwd)
        for t in range(n):
            copy(t, 0, sibling, me).wait_recv()
            for j, chip in enumerate(chips):
                copy(t, 4 + j, (*chip, 1 - c), me).wait_recv()
        for cp in first + passed:
            cp.wait_send()
        for cp in mine:
            cp.wait()

    hbm = pl.BlockSpec(memory_space=pl.ANY)
    return pl.pallas_call(
        body, name="all_gather_weights",
        in_specs=[hbm] * n, out_specs=[hbm] * n,
        out_shape=[jax.ShapeDtypeStruct(full_shapes[t], shards[t].dtype) for t in range(n)],
        scratch_shapes=[pltpu.SemaphoreType.DMA((n, 7)), pltpu.SemaphoreType.DMA((n, 7)),
                        pltpu.SemaphoreType.DMA((n,))],
    )(*shards)


def sibling_exchange(names, grads, part_shapes):
    n = len(names)

    def body(*refs):
        ins, outs = refs[:n], refs[n:2 * n]
        send_sems, recv_sems = refs[2 * n:]
        x, y, c = _place()
        sibling = (x, y, 1 - c)
        copies = []
        for t in range(n):
            for q in range(4):
                copies.append(pltpu.make_async_remote_copy(
                    src_ref=_owner_view(ins[t], names[t], 2 * q + 1 - c), dst_ref=outs[t].at[q],
                    send_sem=send_sems.at[t, q], recv_sem=recv_sems.at[t, q], device_id=sibling, device_id_type=MESH))
        for cp in copies:
            cp.start()
        for cp in copies:
            cp.wait()

    hbm = pl.BlockSpec(memory_space=pl.ANY)
    return pl.pallas_call(
        body, name="rs_sibling_exchange",
        in_specs=[hbm] * n, out_specs=[hbm] * n,
        out_shape=[jax.ShapeDtypeStruct((4,) + part_shapes[t], BF16) for t in range(n)],
        scratch_shapes=[pltpu.SemaphoreType.DMA((n, 4)), pltpu.SemaphoreType.DMA((n, 4))],
    )(*grads)


def chip_exchange(names, parts, part_shapes):
    n = len(names)

    def body(*refs):
        ins, outs = refs[:n], refs[n:2 * n]
        send_sems, recv_sems = refs[2 * n:]
        x, y, c = _place()
        chips = [(1 - x, y), (x, 1 - y), (1 - x, 1 - y)]
        copies = []
        for t in range(n):
            for k, chip in enumerate(chips):
                copies.append(pltpu.make_async_remote_copy(
                    src_ref=ins[t].at[2 * chip[0] + chip[1]], dst_ref=outs[t].at[k],
                    send_sem=send_sems.at[t, k], recv_sem=recv_sems.at[t, k], device_id=(*chip, c),
                    device_id_type=MESH))
        for cp in copies:
            cp.start()
        for cp in copies:
            cp.wait()

    hbm = pl.BlockSpec(memory_space=pl.ANY)
    return pl.pallas_call(
        body, name="rs_chip_exchange",
        in_specs=[hbm] * n, out_specs=[hbm] * n,
        out_shape=[jax.ShapeDtypeStruct((3,) + part_shapes[t], BF16) for t in range(n)],
        scratch_shapes=[pltpu.SemaphoreType.DMA((n, 3)), pltpu.SemaphoreType.DMA((n, 3))],
    )(*parts)


def all_gather_rows(x):
    r, w = x.shape

    def body(x_ref, out_ref, send_sems, recv_sems, local_sem):
        px, py, pc = _place()
        me = 4 * px + 2 * py + pc
        mine = pltpu.make_async_copy(x_ref, out_ref.at[me], local_sem)
        mine.start()
        copies = []
        for k in range(1, N_DEV):
            peer = (px ^ (k >> 2), py ^ ((k >> 1) & 1), pc ^ (k & 1))
            copies.append(pltpu.make_async_remote_copy(
                src_ref=x_ref, dst_ref=out_ref.at[me], send_sem=send_sems.at[k - 1], recv_sem=recv_sems.at[k - 1],
                device_id=peer, device_id_type=MESH))
        for cp in copies:
            cp.start()
        for k in range(1, N_DEV):
            peer_idx = me ^ k
            pltpu.make_async_remote_copy(
                src_ref=x_ref, dst_ref=out_ref.at[peer_idx], send_sem=send_sems.at[k - 1],
                recv_sem=recv_sems.at[k - 1], device_id=(px, py, pc), device_id_type=MESH).wait_recv()
        for cp in copies:
            cp.wait_send()
        mine.wait()

    vmem = pl.BlockSpec(memory_space=pltpu.VMEM)
    return pl.pallas_call(
        body, name="all_gather_small_grads",
        in_specs=[vmem], out_specs=vmem,
        out_shape=jax.ShapeDtypeStruct((N_DEV, r, w), x.dtype),
        scratch_shapes=[pltpu.SemaphoreType.DMA((N_DEV - 1,)), pltpu.SemaphoreType.DMA((N_DEV - 1,)),
                        pltpu.SemaphoreType.DMA],
    )(x)


def _adamw(w, g, m, v):
    m = ADAM_B1 * m + (1.0 - ADAM_B1) * g
    v = ADAM_B2 * v + (1.0 - ADAM_B2) * (g * g)
    m_hat = m / (1.0 - ADAM_B1 ** ADAM_STEP)
    v_hat = v / (1.0 - ADAM_B2 ** ADAM_STEP)
    return -ADAM_LR * (m_hat / (jnp.sqrt(v_hat) + ADAM_EPS) + ADAM_WD * w), m, v


def _as_rows(a, lead):
    return a.reshape(a.shape[:lead] + (-1, a.shape[-1]))


def sibling_sum(name, col, grads, recv, core):
    _, nl, rows, cols = recv.shape
    tr = _tile(rows, 512)
    rspec = pl.BlockSpec((None, None, tr, cols), lambda q, l, i, c_ref: (q, l, i, 0))
    if col:
        gspec = pl.BlockSpec((None, tr, cols), lambda q, l, i, c_ref: (l, i, 2 * q + c_ref[0]))
    else:
        gspec = pl.BlockSpec((None, None, tr, cols), lambda q, l, i, c_ref: (l, 2 * q + c_ref[0], i, 0))

    def body(c_ref, g_ref, r_ref, o_ref):
        del c_ref
        o_ref[...] = (g_ref[...].astype(F32) + r_ref[...].astype(F32)).astype(BF16)

    return pl.pallas_call(
        body, name=name,
        grid_spec=pltpu.PrefetchScalarGridSpec(num_scalar_prefetch=1, grid=(4, nl, rows // tr),
                                               in_specs=[gspec, rspec], out_specs=rspec),
        out_shape=jax.ShapeDtypeStruct(recv.shape, BF16),
        compiler_params=_params(("parallel", "parallel", "parallel")),
    )(core.reshape(1), grads, recv)


def reduce_adamw(name, parts, recv, chip, w, m, v):
    shape = w.shape
    w2, m2, v2 = (_as_rows(a, 0) for a in (w, m, v))
    rows, cols = w2.shape
    p3 = parts.reshape(4, rows, cols)
    r3 = recv.reshape(3, rows, cols)
    tr = _pick(rows, (256, 128))

    def body(q_ref, p_ref, r_ref, w_ref, m_ref, v_ref, g_out, d_out, m_out, v_out):
        del q_ref
        g = ((p_ref[...].astype(F32) + r_ref[0].astype(F32)) + r_ref[1].astype(F32)) + r_ref[2].astype(F32)
        d, mn, vn = _adamw(w_ref[...], g, m_ref[...], v_ref[...])
        g_out[...] = g
        d_out[...] = d
        m_out[...] = mn
        v_out[...] = vn

    blk = pl.BlockSpec((tr, cols), lambda i, q_ref: (i, 0))
    outs = pl.pallas_call(
        body, name=name,
        grid_spec=pltpu.PrefetchScalarGridSpec(
            num_scalar_prefetch=1, grid=(rows // tr,),
            in_specs=[pl.BlockSpec((None, tr, cols), lambda i, q_ref: (q_ref[0], i, 0)),
                      pl.BlockSpec((3, tr, cols), lambda i, q_ref: (0, i, 0)), blk, blk, blk],
            out_specs=[blk] * 4),
        out_shape=[jax.ShapeDtypeStruct((rows, cols), F32)] * 4,
        compiler_params=_params(("parallel",)),
    )(chip.reshape(1), p3, r3, w2, m2, v2)
    return [o.reshape(shape) for o in outs]


def small_adamw(name, gathered, w, m, v):
    _, r, c = gathered.shape

    def body(ga_ref, w_ref, m_ref, v_ref, g_out, d_out, m_out, v_out):
        g = ga_ref[0]
        for d in range(1, N_DEV):
            g = g + ga_ref[d]
        dl, mn, vn = _adamw(w_ref[...], g, m_ref[...], v_ref[...])
        g_out[...] = g
        d_out[...] = dl
        m_out[...] = mn
        v_out[...] = vn

    return pl.pallas_call(
        body, name=name,
        out_shape=[jax.ShapeDtypeStruct((r, c), F32)] * 4,
        compiler_params=_params(),
    )(gathered, w, m, v)


def _heads(t, nh):
    s = t.shape[0]
    return t.reshape(s, nh, HEAD_DIM).transpose(1, 0, 2)


def _heads_t(t, nh):
    return t.T.reshape(nh, HEAD_DIM, t.shape[0])


def _unheads(t):
    nh, s, dh = t.shape
    return t.transpose(1, 0, 2).reshape(s, nh * dh)


def _group_q(t, nb):
    s = t.shape[0]
    return t.reshape(nb, WINDOW, 2, 8, HEAD_DIM).transpose(2, 0, 3, 1, 4).reshape(2, nb, 8 * WINDOW, HEAD_DIM)


def _group_qt(t, nb):
    return t.reshape(nb, WINDOW, 2, 8, HEAD_DIM).transpose(2, 0, 4, 3, 1).reshape(2, nb, HEAD_DIM, 8 * WINDOW)


def _ungroup_qt(t):
    ng, nb, dh, _ = t.shape
    return t.reshape(ng, nb, dh, 8, WINDOW).transpose(1, 4, 0, 3, 2).reshape(nb * WINDOW, ng * 8 * dh)


def _ungroup_q(t):
    ng, nb, _, dh = t.shape
    return t.reshape(ng, nb, 8, WINDOW, dh).transpose(1, 3, 0, 2, 4).reshape(nb * WINDOW, ng * 8 * dh)


def _front_pad(t):
    return jnp.pad(t, ((0, 0), (WINDOW, 0), (0, 0)))


def _add(acc, *ex):
    return (acc + ex[0],)


def local_step(x, target, wf, small, gbufs):
    s, d = x.shape
    nb = s // WINDOW
    n_a, n_b = small['a_norm'].shape[0], small['b_norm'].shape[0]
    sg = {}
    gb = dict(gbufs)

    bias_flat = bias_table(small['rel_bias'])
    bias_t = bias_flat.reshape(2, 8, WINDOW, 2 * WINDOW).transpose(0, 3, 1, 2).reshape(2, 2 * WINDOW, 8 * WINDOW)
    sink_rows = [jnp.repeat(small['b_sinks'][j], WINDOW).reshape(2, 1, 8 * WINDOW) for j in range(n_b)]

    def mlp_fwd(h, layer):
        n2 = rms_fwd(f"mlp_norm_fwd{layer}", h, small['mlp_norm'][layer])
        u, a = mm_nn(f"mlp_up_fwd{layer}", n2, wf['mlp_up'], layer,
                     lambda acc: (acc, jnp.square(jnp.maximum(acc, 0.0))), (), (BF16, BF16))
        (h2,) = mm_nn(f"mlp_down_fwd{layer}", a, wf['mlp_down'], layer, _add, (h,), (F32,))
        return h2, (n2, u, a)

    h = x
    saved = []
    for l in range(n_a):
        n1 = rms_fwd(f"a_norm_fwd{l}", h, small['a_norm'][l])
        (qkv,) = mm_nn(f"a_qkv_fwd{l}", n1, wf['a_wqkv'], l, lambda acc: (acc,), (), (BF16,))
        nh = d // HEAD_DIM
        o_t, rtab = sb_fwd(f"sb_fwd{l}", _heads_t(qkv[:, :d], nh), _heads(qkv[:, d:2 * d], nh),
                           _heads_t(qkv[:, 2 * d:], nh))
        o = o_t.reshape(d, s).T
        (h_mid,) = mm_nn(f"a_wo_fwd{l}", o, wf['a_wo'], l, _add, (h,), (F32,))
        h_out, mlp_saved = mlp_fwd(h_mid, l)
        saved.append((h, n1, qkv, o, rtab, h_mid, mlp_saved))
        h = h_out
    h_kv = h
    nkv = rms_fwd("kv_norm_fwd", h, small['kv_norm'])
    (kv,) = mm_nn("kv_fwd", nkv, wf['w_kv'], 0, lambda acc, b: (acc + b,), (small['b_kv'].reshape(1, -1),), (BF16,))
    kvw = kv.shape[1] // 2
    kp = _front_pad(_heads(kv[:, :kvw], 2))
    vp = _front_pad(_heads(kv[:, kvw:], 2))
    kpt, vpt = kp.transpose(0, 2, 1), vp.transpose(0, 2, 1)
    for j in range(n_b):
        layer = n_a + j
        n1 = rms_fwd(f"b_norm_fwd{j}", h, small['b_norm'][j])
        (qb,) = mm_nn(f"b_q_fwd{j}", n1, wf['b_wq'], j, lambda acc, b: (acc + b,),
                      (small['b_bq'][j].reshape(1, -1),), (BF16,))
        og = swa_fwd(f"swa_fwd{j}", _group_qt(qb, nb), kp, vpt, bias_t, sink_rows[j])
        o = _ungroup_qt(og)
        (h_mid,) = mm_nn(f"b_wo_fwd{j}", o, wf['b_wo'], j, lambda acc, hh, b: (acc + hh + b,),
                         (h, small['b_bo'][j].reshape(1, -1)), (F32,))
        h_out, mlp_saved = mlp_fwd(h_mid, layer)
        saved.append((h, n1, qb, o, h_mid, mlp_saved))
        h = h_out

    dh, dhb, dg_final, loss_b = loss_head(h, small['final_norm'], target)
    sg['final_norm'] = dg_final[0]
    sg['mlp_norm'] = [None] * (n_a + n_b)
    cs_mid = None

    def mlp_bwd(dh, dhb, h_mid, mlp_saved, layer):
        n2, u, a = mlp_saved
        (du,) = mm_nt(f"mlp_down_dx{layer}", dhb, wf['mlp_down'], layer,
                      lambda acc, uu: (acc * (2.0 * jnp.maximum(uu.astype(F32), 0.0)),), (u,), (BF16,))
        gb['mlp_down'] = mm_tn(f"mlp_down_dw{layer}", a, dhb, gb['mlp_down'], layer)
        (dn2,) = mm_nt(f"mlp_up_dx{layer}", du, wf['mlp_up'], layer, lambda acc: (acc,), (), (F32,))
        gb['mlp_up'] = mm_tn(f"mlp_up_dw{layer}", n2, du, gb['mlp_up'], layer)
        dh2, dh2b, dg, cs = rms_bwd(f"mlp_norm_bwd{layer}", h_mid, small['mlp_norm'][layer], dn2, dh)
        sg['mlp_norm'][layer] = dg[0]
        return dh2, dh2b, cs

    dkp = jnp.zeros(kp.shape, F32)
    dvp = jnp.zeros(vp.shape, F32)
    sg['b_norm'], sg['b_bq'], sg['b_bo'], sg['b_sinks'] = [None] * n_b, [None] * n_b, [None] * n_b, [None] * n_b
    dbias = [None] * n_b
    for j in reversed(range(n_b)):
        layer = n_a + j
        h_in, n1, qb, o, h_mid, mlp_saved = saved[layer]
        dh, dhb, cs = mlp_bwd(dh, dhb, h_mid, mlp_saved, layer)
        sg['b_bo'][j] = cs[0]
        (do,) = mm_nt(f"b_wo_dx{j}", dhb, wf['b_wo'], j, lambda acc: (acc,), (), (BF16,))
        gb['b_wo'] = mm_tn(f"b_wo_dw{j}", o, dhb, gb['b_wo'], j)
        dqg, dkp, dvp, dbias[j], dsink = swa_bwd(f"swa_bwd{j}", _group_qt(qb, nb), _group_q(qb, nb), kp, kpt, vp,
                                                 bias_t, sink_rows[j], _group_qt(do, nb), _group_q(do, nb), dkp, dvp)
        sg['b_sinks'][j] = colsum(f"sink_grad{j}", dsink.reshape(16, WINDOW).T)[0]
        dq = _ungroup_qt(dqg)
        sg['b_bq'][j] = colsum(f"b_bq_grad{j}", dq)[0]
        (dn1,) = mm_nt(f"b_q_dx{j}", dq, wf['b_wq'], j, lambda acc: (acc,), (), (F32,))
        gb['b_wq'] = mm_tn(f"b_q_dw{j}", n1, dq, gb['b_wq'], j)
        dh, dhb, dg, _ = rms_bwd(f"b_norm_bwd{j}", h_in, small['b_norm'][j], dn1, dh)
        sg['b_norm'][j] = dg[0]
    unt = lambda t: t.reshape(2, 2 * WINDOW, 8, WINDOW).transpose(0, 2, 3, 1).reshape(bias_flat.shape)
    sg['rel_bias'] = bias_table_grad(unt(dbias[0]), unt(dbias[1]))[:, :N_BUCKETS].T

    dkv = jnp.concatenate([_unheads(dkp[:, WINDOW:]), _unheads(dvp[:, WINDOW:])], axis=1)
    sg['b_kv'] = colsum("b_kv_grad", dkv)[0]
    dkvb = dkv.astype(BF16)
    (dnkv,) = mm_nt("kv_dx", dkvb, wf['w_kv'], 0, lambda acc: (acc,), (), (F32,))
    gb['w_kv'] = mm_tn("kv_dw", nkv, dkvb, gb['w_kv'], 0)
    dh, dhb, dg, _ = rms_bwd("kv_norm_bwd", h_kv, small['kv_norm'], dnkv, dh)
    sg['kv_norm'] = dg[0]

    sg['a_norm'] = [None] * n_a
    for l in reversed(range(n_a)):
        h_in, n1, qkv, o, rtab, h_mid, mlp_saved = saved[l]
        dh, dhb, _ = mlp_bwd(dh, dhb, h_mid, mlp_saved, l)
        (do,) = mm_nt(f"a_wo_dx{l}", dhb, wf['a_wo'], l, lambda acc: (acc,), (), (BF16,))
        gb['a_wo'] = mm_tn(f"a_wo_dw{l}", o, dhb, gb['a_wo'], l)
        nh = d // HEAD_DIM
        qs, ks, vs = qkv[:, :d], qkv[:, d:2 * d], qkv[:, 2 * d:]
        dq_t, dk, dv = sb_bwd(f"sb_bwd{l}", _heads_t(qs, nh), _heads(qs, nh), _heads(ks, nh), _heads_t(ks, nh),
                              _heads(vs, nh), _heads_t(do, nh), _heads(do, nh), rtab)
        dqkv = jnp.concatenate([dq_t.reshape(d, s).T, _unheads(dk), _unheads(dv)], axis=1).astype(BF16)
        (dn1,) = mm_nt(f"a_qkv_dx{l}", dqkv, wf['a_wqkv'], l, lambda acc: (acc,), (), (F32,))
        gb['a_wqkv'] = mm_tn(f"a_qkv_dw{l}", n1, dqkv, gb['a_wqkv'], l)
        dh, dhb, dg, _ = rms_bwd(f"a_norm_bwd{l}", h_in, small['a_norm'][l], dn1, dh)
        sg['a_norm'][l] = dg[0]

    small_grads = {
        'a_norm': jnp.stack(sg['a_norm']), 'kv_norm': sg['kv_norm'], 'b_kv': sg['b_kv'],
        'b_norm': jnp.stack(sg['b_norm']), 'b_bq': jnp.stack(sg['b_bq']), 'b_sinks': jnp.stack(sg['b_sinks']),
        'b_bo': jnp.stack(sg['b_bo']), 'rel_bias': sg['rel_bias'], 'mlp_norm': jnp.stack(sg['mlp_norm']),
        'final_norm': sg['final_norm'],
    }
    return loss_b, dh, gb, small_grads


def _full_shape(name, shard_shape):
    if name in COL_SHARDED:
        return shard_shape[:2] + (N_DEV * shard_shape[2],)
    nl, r, n = shard_shape
    return (nl, N_DEV, r, n)


def _as_w3(name, full):
    if name in COL_SHARDED:
        return full
    nl, nd, r, n = full.shape
    return full.reshape(nl, nd * r, n)


def _pack_small(vals):
    flat = jnp.concatenate([vals[n].reshape(-1).astype(F32) for n in SMALL] + [vals['loss'].reshape(-1)])
    rows = -(-flat.shape[0] // 1024) * 8
    return jnp.pad(flat, (0, rows * 128 - flat.shape[0])).reshape(rows, 128)


def _unpack_small(packed, shapes):
    flat = packed.reshape(-1)
    out, off = {}, 0
    for n in SMALL + ['loss']:
        size = int(np.prod(shapes[n]))
        out[n] = flat[off:off + size].reshape(shapes[n])
        off += size
    return out


def kernel(x, a_norm, a_wqkv, a_wo, kv_norm, w_kv, b_kv, b_norm, b_wq, b_bq, b_sinks, b_wo, b_bo, rel_bias, mlp_norm, mlp_up, mlp_down, final_norm, loss_target, m_a_norm, m_a_wqkv, m_a_wo, m_kv_norm, m_w_kv, m_b_kv, m_b_norm, m_b_wq, m_b_bq, m_b_sinks, m_b_wo, m_b_bo, m_rel_bias, m_mlp_norm, m_mlp_up, m_mlp_down, m_final_norm, v_a_norm, v_a_wqkv, v_a_wo, v_kv_norm, v_w_kv, v_b_kv, v_b_norm, v_b_wq, v_b_bq, v_b_sinks, v_b_wo, v_b_bo, v_rel_bias, v_mlp_norm, v_mlp_up, v_mlp_down, v_final_norm):
    w = dict(a_norm=a_norm, a_wqkv=a_wqkv, a_wo=a_wo, kv_norm=kv_norm, w_kv=w_kv, b_kv=b_kv, b_norm=b_norm,
             b_wq=b_wq, b_bq=b_bq, b_sinks=b_sinks, b_wo=b_wo, b_bo=b_bo, rel_bias=rel_bias, mlp_norm=mlp_norm,
             mlp_up=mlp_up, mlp_down=mlp_down, final_norm=final_norm)
    m = dict(a_norm=m_a_norm, a_wqkv=m_a_wqkv, a_wo=m_a_wo, kv_norm=m_kv_norm, w_kv=m_w_kv, b_kv=m_b_kv,
             b_norm=m_b_norm, b_wq=m_b_wq, b_bq=m_b_bq, b_sinks=m_b_sinks, b_wo=m_b_wo, b_bo=m_b_bo,
             rel_bias=m_rel_bias, mlp_norm=m_mlp_norm, mlp_up=m_mlp_up, mlp_down=m_mlp_down, final_norm=m_final_norm)
    v = dict(a_norm=v_a_norm, a_wqkv=v_a_wqkv, a_wo=v_a_wo, kv_norm=v_kv_norm, w_kv=v_w_kv, b_kv=v_b_kv,
             b_norm=v_b_norm, b_wq=v_b_wq, b_bq=v_b_bq, b_sinks=v_b_sinks, b_wo=v_b_wo, b_bo=v_b_bo,
             rel_bias=v_rel_bias, mlp_norm=v_mlp_norm, mlp_up=v_mlp_up, mlp_down=v_mlp_down, final_norm=v_final_norm)
    px, py, pc = _place()
    me = 4 * px + 2 * py + pc
    chip = (2 * px + py).astype(jnp.int32)
    core = pc.astype(jnp.int32)

    shards = {n: (w[n][None] if w[n].ndim == 2 else w[n]).astype(BF16) for n in BIG}
    an_pad = jnp.zeros((8, 128), F32).at[:a_norm.shape[0]].set(a_norm)
    names = BIG + ['a_norm']
    full = all_gather_weights(names, [shards[n] for n in BIG] + [an_pad],
                              [_full_shape(n, shards[n].shape) for n in BIG] + [(N_DEV, 8, 128)])
    full = dict(zip(names, full))
    wf = {n: _as_w3(n, full[n]) for n in BIG}
    n_a = a_norm.shape[0]
    small = {n: w[n] for n in SMALL}
    small['a_norm'] = full['a_norm'][:, :n_a].transpose(1, 0, 2).reshape(n_a, -1)

    gbufs = {n: jnp.zeros(wf[n].shape, BF16) for n in BIG}
    loss_b, grad_x, gb, sgrads = local_step(x[0], loss_target[0], wf, small, gbufs)

    gfull = [gb[n].reshape(full[n].shape) for n in BIG]
    part_shapes = [shards[n].shape for n in BIG]
    recv1 = sibling_exchange(BIG, gfull, part_shapes)
    parts = [sibling_sum(f"rs_sibling_sum_{n}", n in COL_SHARDED, g, r, core) for n, g, r in zip(BIG, gfull, recv1)]
    recv2 = chip_exchange(BIG, parts, part_shapes)
    out = {}
    for n, p, r in zip(BIG, parts, recv2):
        out[n] = reduce_adamw(f"adamw_{n}", p, r, chip, w[n], m[n], v[n])

    sgrads['loss'] = loss_b[0, :1]
    gathered = all_gather_rows(_pack_small(sgrads))
    shapes = {n: w[n].shape for n in SMALL}
    shapes['a_norm'] = (n_a, a_norm.shape[1] * N_DEV)
    shapes['loss'] = (1,)
    zeros1 = jnp.zeros((1,), F32)

    def packed(src):
        vals = {n: src[n] for n in SMALL}
        vals['a_norm'] = jnp.zeros(shapes['a_norm'], F32)
        vals['loss'] = zeros1
        return _pack_small(vals)

    sm = small_adamw("adamw_small", gathered, packed(w), packed(m), packed(v))
    sm = [_unpack_small(t, shapes) for t in sm]
    g_an = lax.dynamic_slice_in_dim(sm[0]['a_norm'], me * a_norm.shape[1], a_norm.shape[1], axis=1)
    pad = lambda t: jnp.zeros((8, 128), F32).at[:n_a].set(t)
    gathered_an = jnp.zeros((N_DEV, 8, 128), F32).at[0].set(pad(g_an))
    an = small_adamw("adamw_a_norm", gathered_an, pad(a_norm), pad(m_a_norm), pad(v_a_norm))
    for i in range(4):
        sm[i]['a_norm'] = an[i][:n_a]
    for n in BIG:
        for i in range(4):
            sm[i][n] = out[n][i]
    loss = sm[0]['loss'][0]
    return (loss, grad_x[None], *[sm[0][n] for n in WEIGHTS], *[sm[1][n] for n in WEIGHTS],
            *[sm[2][n] for n in WEIGHTS], *[sm[3][n] for n in WEIGHTS])
```

```python
import functools
import math

import numpy as np
import jax
import jax.numpy as jnp
from jax import lax
from jax.experimental import pallas as pl
from jax.experimental.pallas import tpu as pltpu

F32 = jnp.float32
BF16 = jnp.bfloat16
MESH = pl.DeviceIdType.MESH

N_DEV = 8
HEAD_DIM = 64
WINDOW = 128
N_BUCKETS = 32
EPS = 1e-5
NEG_INF = -1e30
Q_SCALE = 1.0 / math.sqrt(HEAD_DIM)
LOG2E = 1.4426950408889634

ADAM_LR, ADAM_B1, ADAM_B2, ADAM_EPS, ADAM_WD, ADAM_STEP = 0.001, 0.9, 0.999, 1e-08, 0.01, 10

SB_BQ = 512
SB_BK = 128
ROW_TILE = 512
VMEM_LIMIT = 56 * 1024 * 1024

WEIGHTS = ['a_norm', 'a_wqkv', 'a_wo', 'kv_norm', 'w_kv', 'b_kv', 'b_norm', 'b_wq', 'b_bq', 'b_sinks', 'b_wo',
           'b_bo', 'rel_bias', 'mlp_norm', 'mlp_up', 'mlp_down', 'final_norm']
BIG = ['a_wqkv', 'a_wo', 'w_kv', 'b_wq', 'b_wo', 'mlp_up', 'mlp_down']
COL_SHARDED = ('a_wqkv', 'mlp_up')
SMALL = ['a_norm', 'kv_norm', 'b_kv', 'b_norm', 'b_bq', 'b_sinks', 'b_bo', 'rel_bias', 'mlp_norm', 'final_norm']


def _params(sem=None):
    return pltpu.CompilerParams(dimension_semantics=sem, vmem_limit_bytes=VMEM_LIMIT)


def _pick(n, cands):
    for c in cands:
        if n % c == 0:
            return c
    raise ValueError(n)


def _tile(n, want):
    return n if n <= want else _pick(n, (want, want // 2, want // 4))


def mm_nn(name, a, w3, layer, epilogue, extras, out_dtypes):
    m, k = a.shape
    _, kw, n = w3.shape
    assert kw == k
    tm = _tile(m, 1024 if k <= 1024 else 512)
    tn = _tile(n, 1024)
    ne, no = len(extras), len(out_dtypes)

    def body(a_ref, w_ref, *rest):
        ex, outs = rest[:ne], rest[ne:ne + no]
        res = epilogue(jnp.dot(a_ref[...], w_ref[...], preferred_element_type=F32), *[e[...] for e in ex])
        for o, r in zip(outs, res):
            o[...] = r.astype(o.dtype)

    tile = pl.BlockSpec((tm, tn), lambda i, j: (i, j))
    ex_specs = [tile if e.shape[0] == m else pl.BlockSpec((1, tn), lambda i, j: (0, j)) for e in extras]
    return pl.pallas_call(
        body, name=name, grid=(m // tm, n // tn),
        in_specs=[pl.BlockSpec((tm, k), lambda i, j: (i, 0)),
                  pl.BlockSpec((None, k, tn), lambda i, j: (layer, 0, j))] + ex_specs,
        out_specs=[tile] * no,
        out_shape=[jax.ShapeDtypeStruct((m, n), d) for d in out_dtypes],
        compiler_params=_params(("parallel", "parallel")),
    )(a, w3, *extras)


def mm_nt(name, dy, w3, layer, epilogue, extras, out_dtypes):
    m, n = dy.shape
    _, k, nw = w3.shape
    assert nw == n
    tm = _tile(m, 1024 if n <= 1024 else 512)
    tko = _tile(k, 1024)
    ne, no = len(extras), len(out_dtypes)

    def body(a_ref, w_ref, *rest):
        ex, outs = rest[:ne], rest[ne:ne + no]
        acc = lax.dot_general(a_ref[...], w_ref[...], (((1,), (1,)), ((), ())), preferred_element_type=F32)
        res = epilogue(acc, *[e[...] for e in ex])
        for o, v in zip(outs, res):
            o[...] = v.astype(o.dtype)

    tile = pl.BlockSpec((tm, tko), lambda i, ko: (i, ko))
    return pl.pallas_call(
        body, name=name, grid=(m // tm, k // tko),
        in_specs=[pl.BlockSpec((tm, n), lambda i, ko: (i, 0)),
                  pl.BlockSpec((None, tko, n), lambda i, ko: (layer, ko, 0))] + [tile] * ne,
        out_specs=[tile] * no,
        out_shape=[jax.ShapeDtypeStruct((m, k), d) for d in out_dtypes],
        compiler_params=_params(("parallel", "parallel")),
    )(dy, w3, *extras)


def mm_tn(name, x, dy, gbuf, shape, layer):
    s, k = x.shape
    _, kw, n = shape
    assert kw == k and dy.shape == (s, n)
    tkk = _tile(k, 512)
    tn = _tile(n, 1024)

    def body(x_ref, dy_ref, *rest):
        g_out = rest[-1]
        g_out[...] = lax.dot_general(x_ref[...], dy_ref[...], (((0,), (0,)), ((), ())),
                                     preferred_element_type=F32).astype(g_out.dtype)

    prev = [] if gbuf is None else [gbuf]
    return pl.pallas_call(
        body, name=name, grid=(k // tkk, n // tn),
        in_specs=[pl.BlockSpec((s, tkk), lambda ki, j: (0, ki)),
                  pl.BlockSpec((s, tn), lambda ki, j: (0, j))] + [pl.BlockSpec(memory_space=pl.ANY)] * len(prev),
        out_specs=pl.BlockSpec((None, tkk, tn), lambda ki, j: (layer, ki, j)),
        out_shape=jax.ShapeDtypeStruct(shape, BF16),
        input_output_aliases={2: 0} if prev else {},
        compiler_params=_params(("parallel", "parallel")),
    )(x, dy, *prev)


def rms_fwd(name, h, g):
    s, d = h.shape
    tr = _pick(s, (ROW_TILE, 256, 128))

    def body(h_ref, g_ref, o_ref):
        x = h_ref[...]
        r = lax.rsqrt(jnp.mean(x * x, axis=-1, keepdims=True) + EPS)
        o_ref[...] = (x * r * g_ref[...]).astype(o_ref.dtype)

    return pl.pallas_call(
        body, name=name, grid=(s // tr,),
        in_specs=[pl.BlockSpec((tr, d), lambda i: (i, 0)), pl.BlockSpec((1, d), lambda i: (0, 0))],
        out_specs=pl.BlockSpec((tr, d), lambda i: (i, 0)),
        out_shape=jax.ShapeDtypeStruct((s, d), BF16),
        compiler_params=_params(("parallel",)),
    )(h, g.reshape(1, d))


def rms_bwd(name, h, g, dn, dres):
    s, d = h.shape
    tr = _pick(s, (ROW_TILE, 256, 128))

    def body(h_ref, g_ref, dn_ref, dres_ref, dx_ref, dxb_ref, dg_ref, cs_ref):
        i = pl.program_id(0)
        x = h_ref[...]
        r = lax.rsqrt(jnp.mean(x * x, axis=-1, keepdims=True) + EPS)
        xh = x * r
        dn_ = dn_ref[...]
        dyg = dn_ * g_ref[...]
        dx = dres_ref[...] + r * (dyg - xh * jnp.mean(dyg * xh, axis=-1, keepdims=True))
        dx_ref[...] = dx
        dxb_ref[...] = dx.astype(BF16)

        @pl.when(i == 0)
        def _():
            dg_ref[...] = jnp.zeros_like(dg_ref)
            cs_ref[...] = jnp.zeros_like(cs_ref)

        dg_ref[...] += jnp.sum(dn_ * xh, axis=0, keepdims=True)
        cs_ref[...] += jnp.sum(dx, axis=0, keepdims=True)

    row = pl.BlockSpec((tr, d), lambda i: (i, 0))
    vec = pl.BlockSpec((1, d), lambda i: (0, 0))
    return pl.pallas_call(
        body, name=name, grid=(s // tr,),
        in_specs=[row, vec, row, row],
        out_specs=[row, row, vec, vec],
        out_shape=[jax.ShapeDtypeStruct((s, d), F32), jax.ShapeDtypeStruct((s, d), BF16),
                   jax.ShapeDtypeStruct((1, d), F32), jax.ShapeDtypeStruct((1, d), F32)],
        compiler_params=_params(("arbitrary",)),
    )(h, g.reshape(1, d), dn, dres)


def loss_head(h, g, target):
    s, d = h.shape
    tr = _pick(s, (ROW_TILE, 256, 128))

    def body(h_ref, g_ref, t_ref, dx_ref, dxb_ref, dg_ref, loss_ref):
        i = pl.program_id(0)
        x = h_ref[...]
        r = lax.rsqrt(jnp.mean(x * x, axis=-1, keepdims=True) + EPS)
        xh = x * r
        gw = g_ref[...]
        err = xh * gw - t_ref[...]
        dn_ = err * (1.0 / d)
        dyg = dn_ * gw
        dx = r * (dyg - xh * jnp.mean(dyg * xh, axis=-1, keepdims=True))
        dx_ref[...] = dx
        dxb_ref[...] = dx.astype(BF16)

        @pl.when(i == 0)
        def _():
            dg_ref[...] = jnp.zeros_like(dg_ref)
            loss_ref[...] = jnp.zeros_like(loss_ref)

        dg_ref[...] += jnp.sum(dn_ * xh, axis=0, keepdims=True)
        per_row = jnp.sum(err * err, axis=-1, keepdims=True) * (0.5 / d)
        loss_ref[...] += jnp.broadcast_to(jnp.sum(per_row, axis=0, keepdims=True), loss_ref.shape)

    row = pl.BlockSpec((tr, d), lambda i: (i, 0))
    vec = pl.BlockSpec((1, d), lambda i: (0, 0))
    return pl.pallas_call(
        body, name="loss_head", grid=(s // tr,),
        in_specs=[row, vec, row],
        out_specs=[row, row, vec, pl.BlockSpec((1, 128), lambda i: (0, 0))],
        out_shape=[jax.ShapeDtypeStruct((s, d), F32), jax.ShapeDtypeStruct((s, d), BF16),
                   jax.ShapeDtypeStruct((1, d), F32), jax.ShapeDtypeStruct((1, 128), F32)],
        compiler_params=_params(("arbitrary",)),
    )(h, g.reshape(1, d), target)


def colsum(name, x):
    s, n = x.shape
    tr = _pick(s, (ROW_TILE, 256, 128))

    def body(x_ref, o_ref):
        @pl.when(pl.program_id(0) == 0)
        def _():
            o_ref[...] = jnp.zeros_like(o_ref)

        o_ref[...] += jnp.sum(x_ref[...].astype(F32), axis=0, keepdims=True)

    return pl.pallas_call(
        body, name=name, grid=(s // tr,),
        in_specs=[pl.BlockSpec((tr, n), lambda i: (i, 0))],
        out_specs=pl.BlockSpec((1, n), lambda i: (0, 0)),
        out_shape=jax.ShapeDtypeStruct((1, n), F32),
        compiler_params=_params(("arbitrary",)),
    )(x)


def _tri_rows(reverse):
    i = np.arange(SB_BK)
    tri = (i[None, :] >= i[:, None]) if reverse else (i[None, :] <= i[:, None])
    tri = np.concatenate([tri, tri], axis=1)
    return jnp.asarray(np.concatenate([tri, np.ones((8, 2 * SB_BK), bool)], axis=0), BF16)


def _hi_lo_rows(x):
    hi = x.astype(BF16)
    lo = (x - hi.astype(F32)).astype(BF16)
    return jnp.concatenate([hi, lo], axis=0)


def _softplus2(zs):
    neg_abs = lax.bitcast_convert_type(lax.bitcast_convert_type(zs, jnp.uint32) | jnp.uint32(0x80000000), F32)
    return jnp.maximum(zs, 0.0) + jnp.log2(1.0 + jnp.exp2(neg_abs))


def _pair_mask(first_rel_block, bq):
    key = lax.broadcasted_iota(jnp.int32, (2 * SB_BK, bq), 0) + first_rel_block * SB_BK
    qry = lax.broadcasted_iota(jnp.int32, (2 * SB_BK, bq), 1)
    return key < qry


def _row_of(table8, sub8, r):
    return jnp.sum(jnp.where(sub8 == r, table8, 0.0), axis=0, keepdims=True)


def _keys(j0):
    return pl.ds(pl.multiple_of(j0 * SB_BK, 2 * SB_BK), 2 * SB_BK)


class Carry:
    def __init__(self, build, ins, out_shapes, sems):
        self.build, self.ins, self.out_shapes, self.sems = build, list(ins), list(out_shapes), list(sems)


def _carried(carry, rest, n_out, n_scratch, first, last):
    n_ci = len(carry.ins) if carry else 0
    n_co = len(carry.out_shapes) if carry else 0
    cin, outs = rest[:n_ci], rest[n_ci:n_ci + n_out]
    cout = rest[n_ci + n_out:n_ci + n_out + n_co]
    scratch = rest[n_ci + n_out + n_co:n_ci + n_out + n_co + n_scratch]
    csems = rest[n_ci + n_out + n_co + n_scratch:]

    def start():
        if carry:
            @pl.when(first)
            def _():
                for cp in carry.build(cin, cout, *csems):
                    cp.start()

    def wait():
        if carry:
            @pl.when(last)
            def _():
                for cp in carry.build(cin, cout, *csems):
                    cp.wait()

    return outs, scratch, start, wait


def sb_fwd(name, qt, k, vt, exchange=None):
    nh, dh, s = qt.shape
    bq = SB_BQ
    per_q = bq // SB_BK
    nkb = s // SB_BK
    assert s % bq == 0 and per_q == 4 and nkb % 8 == 0

    def body(q_ref, k_ref, v_ref, a_ref, *rest):
        i = pl.program_id(1)
        first = (pl.program_id(0) == 0) & (i == 0)
        last = (pl.program_id(0) == nh - 1) & (i == s // bq - 1)
        (o_ref, rtab_ref), (acc, zbuf, wbuf), start_carried, wait_carried = _carried(exchange, rest, 2, 3, first, last)
        start_carried()
        qb = q_ref[...] * Q_SCALE
        tri = a_ref[...]
        sub8 = lax.broadcasted_iota(jnp.int32, (8, bq), 0)
        acc[...] = jnp.zeros_like(acc)
        rtab_ref[...] = jnp.zeros_like(rtab_ref)

        def scores(j0):
            return jnp.dot(k_ref[_keys(j0), :], qb, preferred_element_type=F32) * LOG2E

        def pair(j0, slot, run, rt8, mask, has_prev):
            zs = zbuf[slot]
            zbuf[1 - slot] = scores(jnp.maximum(j0 - 2, 0))
            if has_prev:
                acc[...] += jnp.dot(v_ref[:, _keys(j0 + 2)], wbuf[1 - slot], preferred_element_type=F32)
            p = _softplus2(zs)
            if mask is not None:
                p = jnp.where(mask, p, 0.0)
            cr1 = jnp.dot(tri, _hi_lo_rows(p[SB_BK:]), preferred_element_type=F32)
            cr0 = jnp.dot(tri, _hi_lo_rows(p[:SB_BK]), preferred_element_type=F32)
            run1 = run + cr1[SB_BK:SB_BK + 1]
            w = jnp.exp2(jnp.concatenate([zs[:SB_BK] - cr0[:SB_BK] - run1, zs[SB_BK:] - cr1[:SB_BK] - run], axis=0))
            if mask is not None:
                w = jnp.where(mask, w, 0.0)
            wbuf[slot] = w.astype(BF16)
            rt8 = jnp.where(sub8 == (j0 + 1) % 8, run, jnp.where(sub8 == j0 % 8, run1, rt8))
            rtab_ref[pl.ds(pl.multiple_of((j0 // 8) * 8, 8), 8), :] = rt8
            return run1 + cr0[SB_BK:SB_BK + 1], rt8

        top = i * per_q
        zbuf[0] = scores(top + 2)
        carry = (jnp.zeros((1, bq), F32), jnp.zeros((8, bq), F32))
        carry = pair(top + 2, 0, *carry, _pair_mask(2, bq), False)
        carry = pair(top, 1, *carry, _pair_mask(0, bq), True)

        def step(it, c):
            j0 = top - 2 - 4 * it
            c = pair(j0, 0, *c, None, True)
            return pair(j0 - 2, 1, *c, None, True)

        lax.fori_loop(0, i, step, carry)
        acc[...] += jnp.dot(v_ref[:, _keys(0)], wbuf[1], preferred_element_type=F32)
        o_ref[...] = acc[...].astype(o_ref.dtype)
        wait_carried()

    qspec = pl.BlockSpec((None, dh, bq), lambda h, i: (h, 0, i))
    hbm = pl.BlockSpec(memory_space=pl.ANY)
    c_ins, c_outs, c_sems = (exchange.ins, exchange.out_shapes, exchange.sems) if exchange else ([], [], [])
    outs = pl.pallas_call(
        body, name=name, grid=(nh, s // bq),
        in_specs=[qspec, pl.BlockSpec((None, s, dh), lambda h, i: (h, 0, 0)),
                  pl.BlockSpec((None, dh, s), lambda h, i: (h, 0, 0)),
                  pl.BlockSpec((SB_BK + 8, 2 * SB_BK), lambda h, i: (0, 0))] + [hbm] * len(c_ins),
        out_specs=[qspec, pl.BlockSpec((None, nkb, bq), lambda h, i: (h, 0, i))] + [hbm] * len(c_outs),
        out_shape=[jax.ShapeDtypeStruct((nh, dh, s), BF16), jax.ShapeDtypeStruct((nh, nkb, s), F32)] + c_outs,
        scratch_shapes=[pltpu.VMEM((dh, bq), F32), pltpu.VMEM((2, 2 * SB_BK, bq), F32),
                        pltpu.VMEM((2, 2 * SB_BK, bq), BF16)] + c_sems,
        compiler_params=_params(("arbitrary", "arbitrary")),
    )(qt, k, vt, _tri_rows(True), *c_ins)
    return outs[0], outs[1], outs[2:]


def sb_bwd(name, qt, q, k, kt, v, dot_, do, rtab, exchange=None):
    nh, dh, s = qt.shape
    bq = SB_BQ
    per_q = bq // SB_BK
    nkb = s // SB_BK

    def body(qt_ref, q_ref, k_ref, kt_ref, v_ref, dot_ref, do_ref, rtab_ref, ar_ref, af_ref, *rest):
        i = pl.program_id(1)
        first = (pl.program_id(0) == 0) & (i == 0)
        last = (pl.program_id(0) == nh - 1) & (i == s // bq - 1)
        (dq_ref, dk_ref, dv_ref), (dq_acc, zbuf, dwbuf, dzbuf, wbuf), start_carried, wait_carried = _carried(
            exchange, rest, 3, 5, first, last)
        start_carried()

        @pl.when(i == 0)
        def _():
            dk_ref[...] = jnp.zeros_like(dk_ref)
            dv_ref[...] = jnp.zeros_like(dv_ref)

        qtb = qt_ref[...] * Q_SCALE
        qrows = q_ref[...] * Q_SCALE
        dotb = dot_ref[...]
        dorows = do_ref[...]
        tri_rev = ar_ref[...][:SB_BK]
        tri_fwd = af_ref[...]
        sub8 = lax.broadcasted_iota(jnp.int32, (8, bq), 0)
        dq_acc[...] = jnp.zeros_like(dq_acc)
        last_j = i * per_q + 2

        def issue(j0, slot):
            zbuf[slot] = jnp.dot(k_ref[_keys(j0), :], qtb, preferred_element_type=F32) * LOG2E
            dwbuf[slot] = jnp.dot(v_ref[_keys(j0), :], dotb, preferred_element_type=F32)

        def retire(j0, slot):
            keys = _keys(j0)
            dq_acc[...] += jnp.dot(kt_ref[:, keys], dzbuf[slot], preferred_element_type=F32)
            dk_ref[keys, :] += jnp.dot(dzbuf[slot], qrows, preferred_element_type=F32)
            dv_ref[keys, :] += jnp.dot(wbuf[slot], dorows, preferred_element_type=F32)

        def pair(j0, slot, g_run, mask):
            zs = zbuf[slot]
            dw = dwbuf[slot]
            issue(jnp.minimum(j0 + 2, last_j), 1 - slot)
            retire(jnp.maximum(j0 - 2, 0), 1 - slot)
            p_raw = _softplus2(zs)
            p = p_raw if mask is None else jnp.where(mask, p_raw, 0.0)
            c0 = jnp.dot(tri_rev, _hi_lo_rows(p[:SB_BK]), preferred_element_type=F32)
            c1 = jnp.dot(tri_rev, _hi_lo_rows(p[SB_BK:]), preferred_element_type=F32)
            rt8 = rtab_ref[pl.ds(pl.multiple_of((j0 // 8) * 8, 8), 8), :]
            r0 = _row_of(rt8, sub8, j0 % 8)
            r1 = _row_of(rt8, sub8, (j0 + 1) % 8)
            w = jnp.exp2(jnp.concatenate([zs[:SB_BK] - c0 - r0, zs[SB_BK:] - c1 - r1], axis=0))
            if mask is not None:
                w = jnp.where(mask, w, 0.0)
            g = w * dw
            gg0 = jnp.dot(tri_fwd, _hi_lo_rows(g[:SB_BK]), preferred_element_type=F32)
            gg1 = jnp.dot(tri_fwd, _hi_lo_rows(g[SB_BK:]), preferred_element_type=F32)
            g_run1 = g_run + gg0[SB_BK:SB_BK + 1]
            g_pre = jnp.concatenate([gg0[:SB_BK] + g_run, gg1[:SB_BK] + g_run1], axis=0)
            dz = g - jnp.exp2(zs - p_raw) * g_pre
            if mask is not None:
                dz = jnp.where(mask, dz, 0.0)
            dzbuf[slot] = dz.astype(BF16)
            wbuf[slot] = w.astype(BF16)
            return g_run1 + gg1[SB_BK:SB_BK + 1]

        issue(0, 0)
        dzbuf[1] = jnp.zeros((2 * SB_BK, bq), BF16)
        wbuf[1] = jnp.zeros((2 * SB_BK, bq), BF16)

        def step(it, g_run):
            g_run = pair(4 * it, 0, g_run, None)
            return pair(4 * it + 2, 1, g_run, None)

        g_run = lax.fori_loop(0, i, step, jnp.zeros((1, bq), F32))
        g_run = pair(last_j - 2, 0, g_run, _pair_mask(0, bq))
        pair(last_j, 1, g_run, _pair_mask(2, bq))
        retire(last_j, 1)
        dq_ref[...] = dq_acc[...] * Q_SCALE
        wait_carried()

    tspec = pl.BlockSpec((None, dh, bq), lambda h, i: (h, 0, i))
    rspec = pl.BlockSpec((None, bq, dh), lambda h, i: (h, i, 0))
    kspec = pl.BlockSpec((None, s, dh), lambda h, i: (h, 0, 0))
    aspec = pl.BlockSpec((SB_BK + 8, 2 * SB_BK), lambda h, i: (0, 0))
    pair_f32 = pltpu.VMEM((2, 2 * SB_BK, bq), F32)
    pair_bf16 = pltpu.VMEM((2, 2 * SB_BK, bq), BF16)
    hbm = pl.BlockSpec(memory_space=pl.ANY)
    c_ins, c_outs, c_sems = (exchange.ins, exchange.out_shapes, exchange.sems) if exchange else ([], [], [])
    outs = pl.pallas_call(
        body, name=name, grid=(nh, s // bq),
        in_specs=[tspec, rspec, kspec, pl.BlockSpec((None, dh, s), lambda h, i: (h, 0, 0)), kspec, tspec, rspec,
                  pl.BlockSpec((None, nkb, bq), lambda h, i: (h, 0, i)), aspec, aspec] + [hbm] * len(c_ins),
        out_specs=[tspec, kspec, kspec] + [hbm] * len(c_outs),
        out_shape=[jax.ShapeDtypeStruct((nh, dh, s), F32), jax.ShapeDtypeStruct((nh, s, dh), F32),
                   jax.ShapeDtypeStruct((nh, s, dh), F32)] + c_outs,
        scratch_shapes=[pltpu.VMEM((dh, bq), F32), pair_f32, pair_f32, pair_bf16, pair_bf16] + c_sems,
        compiler_params=_params(("arbitrary", "arbitrary")),
    )(qt, q, k, kt, v, dot_, do, rtab, _tri_rows(True), _tri_rows(False), *c_ins)
    return outs[0], outs[1], outs[2], outs[3:]


SWA_QB = 2


def _swa_probs(qt, kb, bias_t, sink, i):
    cols = qt.shape[1]
    sc = jnp.dot(kb, qt, preferred_element_type=F32) + bias_t
    kj = lax.broadcasted_iota(jnp.int32, (2 * WINDOW, cols), 0)
    qi = lax.broadcasted_iota(jnp.int32, (2 * WINDOW, cols), 1) & (WINDOW - 1)
    dist = qi + WINDOW - kj
    valid = (dist >= 0) & (dist < WINDOW) & ((kj >= WINDOW) | (i > 0))
    sc = jnp.where(valid, sc, NEG_INF)
    mx = jnp.maximum(jnp.max(sc, axis=0, keepdims=True), sink)
    p = jnp.exp(sc - mx)
    p_sink = jnp.exp(sink - mx)
    inv = 1.0 / (jnp.sum(p, axis=0, keepdims=True) + p_sink)
    return p, p_sink, inv


def _band(i):
    return pl.ds(pl.multiple_of(i * WINDOW, WINDOW), 2 * WINDOW)


def swa_fwd(name, qgt, kp, vpt, bias_t, sink_row):
    ng, nb, dh, cols = qgt.shape
    sp = kp.shape[1]
    assert nb % SWA_QB == 0

    def body(q_ref, k_ref, v_ref, bias_ref, sink_ref, o_ref):
        for u in range(SWA_QB):
            i = pl.program_id(1) * SWA_QB + u
            p, _, inv = _swa_probs(q_ref[u] * Q_SCALE, k_ref[_band(i), :], bias_ref[...], sink_ref[...], i)
            o_ref[u] = (jnp.dot(v_ref[:, _band(i)], p.astype(BF16), preferred_element_type=F32) * inv
                        ).astype(o_ref.dtype)

    qspec = pl.BlockSpec((None, SWA_QB, dh, cols), lambda g, i: (g, i, 0, 0))
    return pl.pallas_call(
        body, name=name, grid=(ng, nb // SWA_QB),
        in_specs=[qspec, pl.BlockSpec((None, sp, dh), lambda g, i: (g, 0, 0)),
                  pl.BlockSpec((None, dh, sp), lambda g, i: (g, 0, 0)),
                  pl.BlockSpec((None, 2 * WINDOW, cols), lambda g, i: (g, 0, 0)),
                  pl.BlockSpec((None, 1, cols), lambda g, i: (g, 0, 0))],
        out_specs=qspec,
        out_shape=jax.ShapeDtypeStruct(qgt.shape, BF16),
        compiler_params=_params(("parallel", "arbitrary")),
    )(qgt, kp, vpt, bias_t, sink_row)


def swa_bwd(name, qgt, qg, kp, kpt, vp, bias_t, sink_row, dogt, dog, dk_in, dv_in):
    ng, nb, dh, cols = qgt.shape
    sp = kp.shape[1]

    def body(qt_ref, q_ref, k_ref, kt_ref, v_ref, bias_ref, sink_ref, dot_ref, do_ref, dki_ref, dvi_ref,
             dq_ref, dk_ref, dv_ref, db_ref, ds_ref):
        @pl.when(pl.program_id(1) == 0)
        def _():
            dk_ref[...] = dki_ref[...]
            dv_ref[...] = dvi_ref[...]
            db_ref[...] = jnp.zeros_like(db_ref)
            ds_ref[...] = jnp.zeros_like(ds_ref)

        for u in range(SWA_QB):
            i = pl.program_id(1) * SWA_QB + u
            band = _band(i)
            kb = k_ref[band, :]
            p, p_sink, inv = _swa_probs(qt_ref[u] * Q_SCALE, kb, bias_ref[...], sink_ref[...], i)
            p = p * inv
            dp = jnp.dot(v_ref[band, :], dot_ref[u], preferred_element_type=F32)
            delta = jnp.sum(p * dp, axis=0, keepdims=True)
            dsc = p * (dp - delta)
            ds_ref[...] -= p_sink * inv * delta
            db_ref[...] += dsc
            dscb = dsc.astype(BF16)
            dq_ref[u] = (jnp.dot(kt_ref[:, band], dscb, preferred_element_type=F32) * Q_SCALE).astype(dq_ref.dtype)
            dk_ref[band, :] += jnp.dot(dscb, q_ref[u] * Q_SCALE, preferred_element_type=F32)
            dv_ref[band, :] += jnp.dot(p.astype(BF16), do_ref[u], preferred_element_type=F32)

    tspec = pl.BlockSpec((None, SWA_QB, dh, cols), lambda g, i: (g, i, 0, 0))
    rspec = pl.BlockSpec((None, SWA_QB, cols, dh), lambda g, i: (g, i, 0, 0))
    kspec = pl.BlockSpec((None, sp, dh), lambda g, i: (g, 0, 0))
    ktspec = pl.BlockSpec((None, dh, sp), lambda g, i: (g, 0, 0))
    bspec = pl.BlockSpec((None, 2 * WINDOW, cols), lambda g, i: (g, 0, 0))
    sspec = pl.BlockSpec((None, 1, cols), lambda g, i: (g, 0, 0))
    return pl.pallas_call(
        body, name=name, grid=(ng, nb // SWA_QB),
        in_specs=[tspec, rspec, kspec, ktspec, kspec, bspec, sspec, tspec, rspec, kspec, kspec],
        out_specs=[tspec, kspec, kspec, bspec, sspec],
        out_shape=[jax.ShapeDtypeStruct(qgt.shape, BF16), jax.ShapeDtypeStruct(kp.shape, F32),
                   jax.ShapeDtypeStruct(kp.shape, F32), jax.ShapeDtypeStruct(bias_t.shape, F32),
                   jax.ShapeDtypeStruct(sink_row.shape, F32)],
        compiler_params=_params(("parallel", "arbitrary")),
    )(qgt, qg, kp, kpt, vp, bias_t, sink_row, dogt, dog, dk_in, dv_in)


def _bucket_onehot():
    qi = np.arange(WINDOW)[:, None]
    kj = np.arange(2 * WINDOW)[None, :]
    n = np.maximum(qi + WINDOW - kj, 0)
    max_exact = N_BUCKETS // 2
    nf = np.maximum(n, 1).astype(np.float64)
    val = np.log(nf / max_exact) / math.log(WINDOW / max_exact) * (N_BUCKETS - max_exact)
    assert np.all(np.abs(val - np.round(val))[(n > max_exact) & (n < WINDOW)] > 1e-3)
    large = np.minimum(max_exact + val.astype(np.int64), N_BUCKETS - 1)
    bucket = np.where(n < max_exact, n, large).reshape(-1)
    onehot = np.zeros((128, bucket.size), np.float32)
    onehot[bucket, np.arange(bucket.size)] = 1.0
    return onehot


def _split3(x):
    a = x.astype(BF16)
    r = x - a.astype(F32)
    b = r.astype(BF16)
    c = (r - b.astype(F32)).astype(BF16)
    return a, b, c


def bias_table(rel_bias):
    nh = rel_bias.shape[1]
    oh = jnp.asarray(_bucket_onehot(), BF16)
    n = oh.shape[1]
    tn = 4096
    rb = jnp.zeros((nh, 128), F32).at[:, :N_BUCKETS].set(rel_bias.T)

    def body(rb_ref, oh_ref, o_ref):
        o_ref[...] = sum(jnp.dot(t, oh_ref[...], preferred_element_type=F32) for t in _split3(rb_ref[...]))

    return pl.pallas_call(
        body, name="bias_table", grid=(n // tn,),
        in_specs=[pl.BlockSpec((nh, 128), lambda i: (0, 0)), pl.BlockSpec((128, tn), lambda i: (0, i))],
        out_specs=pl.BlockSpec((nh, tn), lambda i: (0, i)),
        out_shape=jax.ShapeDtypeStruct((nh, n), F32),
        compiler_params=_params(("parallel",)),
    )(rb, oh)


def bias_table_grad(db0, db1):
    nh, n = db0.shape
    oh = jnp.asarray(_bucket_onehot(), BF16)
    tn = 4096

    def body(a_ref, b_ref, oh_ref, o_ref):
        @pl.when(pl.program_id(0) == 0)
        def _():
            o_ref[...] = jnp.zeros_like(o_ref)

        o_ref[...] += sum(lax.dot_general(t, oh_ref[...], (((1,), (1,)), ((), ())), preferred_element_type=F32)
                          for t in _split3(a_ref[...] + b_ref[...]))

    blk = pl.BlockSpec((nh, tn), lambda i: (0, i))
    return pl.pallas_call(
        body, name="bias_table_grad", grid=(n // tn,),
        in_specs=[blk, blk, pl.BlockSpec((128, tn), lambda i: (0, i))],
        out_specs=pl.BlockSpec((nh, 128), lambda i: (0, 0)),
        out_shape=jax.ShapeDtypeStruct((nh, 128), F32),
        compiler_params=_params(("arbitrary",)),
    )(db0, db1, oh)


def _owner_view(ref, name, d):
    if name == 'a_norm':
        return ref.at[d]
    if name in COL_SHARDED:
        n = ref.shape[2] // N_DEV
        return ref.at[:, :, pl.ds(pl.multiple_of(d * n, 128), n)]
    return ref.at[:, d]


def _place():
    return lax.axis_index("x"), lax.axis_index("y"), lax.axis_index("c")


def _dev(p):
    return 4 * p[0] + 2 * p[1] + p[2]


def _remote(src, dst, send_sem, recv_sem, to):
    return pltpu.make_async_remote_copy(src_ref=src, dst_ref=dst, send_sem=send_sem, recv_sem=recv_sem,
                                        device_id=to, device_id_type=MESH)


def _dma_sems(*shapes):
    return [pltpu.SemaphoreType.DMA(sh) for sh in shapes]


def comm_call(name, build, ins, out_shapes, sems, aliases=None):
    n_in, n_out = len(ins), len(out_shapes)

    def body(*refs):
        copies = build(refs[:n_in], refs[n_in:n_in + n_out], *refs[n_in + n_out:])
        for cp in copies:
            cp.start()
        for cp in copies:
            cp.wait()

    hbm = pl.BlockSpec(memory_space=pl.ANY)
    return pl.pallas_call(
        body, name=name, in_specs=[hbm] * n_in, out_specs=[hbm] * n_out, out_shape=list(out_shapes),
        scratch_shapes=sems, input_output_aliases=aliases or {},
    )(*ins)


def all_gather_weights(names, shards, full_shapes):
    n = len(names)

    def body(*refs):
        ins, outs = refs[:n], refs[n:2 * n]
        send_sems, recv_sems, local_sems = refs[2 * n:]
        x, y, c = _place()
        me, sibling = (x, y, c), (x, y, 1 - c)
        chips = [(1 - x, y), (x, 1 - y), (1 - x, 1 - y)]

        def copy(t, k, block, to, src=None):
            dst = _owner_view(outs[t], names[t], _dev(block))
            return _remote(dst if src is None else src, dst, send_sems.at[t, k], recv_sems.at[t, k], to)

        mine = [pltpu.make_async_copy(ins[t], _owner_view(outs[t], names[t], _dev(me)), local_sems.at[t])
                for t in range(n)]
        for cp in mine:
            cp.start()
        first = []
        for t in range(n):
            first.append(copy(t, 0, me, sibling, src=ins[t]))
            first += [copy(t, 1 + j, me, (*chip, c), src=ins[t]) for j, chip in enumerate(chips)]
        for cp in first:
            cp.start()
        passed = []
        for j, chip in enumerate(chips):
            for t in range(n):
                copy(t, 1 + j, (*chip, c), me).wait_recv()
                fwd = copy(t, 4 + j, (*chip, c), sibling)
                fwd.start()
                passed.append(fwd)
        for t in range(n):
            copy(t, 0, sibling, me).wait_recv()
            for j, chip in enumerate(chips):
                copy(t, 4 + j, (*chip, 1 - c), me).wait_recv()
        for cp in first + passed:
            cp.wait_send()
        for cp in mine:
            cp.wait()

    hbm = pl.BlockSpec(memory_space=pl.ANY)
    return pl.pallas_call(
        body, name="all_gather_layer0",
        in_specs=[hbm] * n, out_specs=[hbm] * n,
        out_shape=[jax.ShapeDtypeStruct(full_shapes[t], shards[t].dtype) for t in range(n)],
        scratch_shapes=_dma_sems((n, 7), (n, 7), (n,)),
    )(*shards)


def ag_direct(names, shards, full_shapes):
    n = len(names)

    def build(ins, outs, send_sems, recv_sems, local_sems):
        x, y, c = _place()
        peers = [(x, y, 1 - c), (1 - x, y, c), (x, 1 - y, c), (1 - x, 1 - y, c)]
        copies = []
        for t in range(n):
            dst = _owner_view(outs[t], names[t], _dev((x, y, c)))
            copies.append(pltpu.make_async_copy(ins[t], dst, local_sems.at[t]))
            copies += [_remote(ins[t], dst, send_sems.at[t, k], recv_sems.at[t, k], to) for k, to in enumerate(peers)]
        return copies

    return Carry(build, shards, [jax.ShapeDtypeStruct(full_shapes[t], shards[t].dtype) for t in range(n)],
                 _dma_sems((n, 4), (n, 4), (n,)))


def ag_forward(names, partial):
    n = len(names)

    def build(ins, outs, send_sems, recv_sems):
        del ins
        x, y, c = _place()
        copies = []
        for t in range(n):
            for k, chip in enumerate([(1 - x, y), (x, 1 - y), (1 - x, 1 - y)]):
                view = _owner_view(outs[t], names[t], _dev((*chip, c)))
                copies.append(_remote(view, view, send_sems.at[t, k], recv_sems.at[t, k], (x, y, 1 - c)))
        return copies

    return comm_call("all_gather_forward", build, partial, [jax.ShapeDtypeStruct(p.shape, p.dtype) for p in partial],
                     _dma_sems((n, 3), (n, 3)), aliases={t: t for t in range(n)})


def sibling_exchange(tag, names, grads, part_shapes):
    n = len(names)

    def build(ins, outs, send_sems, recv_sems):
        x, y, c = _place()
        return [_remote(_owner_view(ins[t], names[t], 2 * q + 1 - c), outs[t].at[q], send_sems.at[t, q],
                        recv_sems.at[t, q], (x, y, 1 - c)) for t in range(n) for q in range(4)]

    return comm_call(f"rs_sibling_exchange_{tag}", build, grads,
                     [jax.ShapeDtypeStruct((4,) + part_shapes[t], BF16) for t in range(n)], _dma_sems((n, 4), (n, 4)))


def chip_exchange(names, parts, part_shapes):
    n = len(names)

    def build(ins, outs, send_sems, recv_sems):
        x, y, c = _place()
        chips = [(1 - x, y), (x, 1 - y), (1 - x, 1 - y)]
        return [_remote(ins[t].at[2 * chip[0] + chip[1]], outs[t].at[k], send_sems.at[t, k], recv_sems.at[t, k],
                        (*chip, c)) for t in range(n) for k, chip in enumerate(chips)]

    return Carry(build, parts, [jax.ShapeDtypeStruct((3,) + part_shapes[t], BF16) for t in range(n)],
                 _dma_sems((n, 3), (n, 3)))


def all_gather_rows(x):
    r, w = x.shape

    def body(x_ref, out_ref, send_sems, recv_sems, local_sem):
        px, py, pc = _place()
        me = 4 * px + 2 * py + pc
        mine = pltpu.make_async_copy(x_ref, out_ref.at[me], local_sem)
        mine.start()
        copies = []
        for k in range(1, N_DEV):
            peer = (px ^ (k >> 2), py ^ ((k >> 1) & 1), pc ^ (k & 1))
            copies.append(pltpu.make_async_remote_copy(
                src_ref=x_ref, dst_ref=out_ref.at[me], send_sem=send_sems.at[k - 1], recv_sem=recv_sems.at[k - 1],
                device_id=peer, device_id_type=MESH))
        for cp in copies:
            cp.start()
        for k in range(1, N_DEV):
            peer_idx = me ^ k
            pltpu.make_async_remote_copy(
                src_ref=x_ref, dst_ref=out_ref.at[peer_idx], send_sem=send_sems.at[k - 1],
                recv_sem=recv_sems.at[k - 1], device_id=(px, py, pc), device_id_type=MESH).wait_recv()
        for cp in copies:
            cp.wait_send()
        mine.wait()

    vmem = pl.BlockSpec(memory_space=pltpu.VMEM)
    return pl.pallas_call(
        body, name="all_gather_small_grads",
        in_specs=[vmem], out_specs=vmem,
        out_shape=jax.ShapeDtypeStruct((N_DEV, r, w), x.dtype),
        scratch_shapes=[pltpu.SemaphoreType.DMA((N_DEV - 1,)), pltpu.SemaphoreType.DMA((N_DEV - 1,)),
                        pltpu.SemaphoreType.DMA],
    )(x)


def _adamw(w, g, m, v):
    m = ADAM_B1 * m + (1.0 - ADAM_B1) * g
    v = ADAM_B2 * v + (1.0 - ADAM_B2) * (g * g)
    m_hat = m / (1.0 - ADAM_B1 ** ADAM_STEP)
    v_hat = v / (1.0 - ADAM_B2 ** ADAM_STEP)
    return -ADAM_LR * (m_hat / (jnp.sqrt(v_hat) + ADAM_EPS) + ADAM_WD * w), m, v


def sibling_sum(name, col, grads, recv, core):
    _, nl, rows, cols = recv.shape
    tr = _tile(rows, 512)
    rspec = pl.BlockSpec((None, None, tr, cols), lambda q, l, i, c_ref: (q, l, i, 0))
    if col:
        gspec = pl.BlockSpec((None, tr, cols), lambda q, l, i, c_ref: (l, i, 2 * q + c_ref[0]))
    else:
        gspec = pl.BlockSpec((None, None, tr, cols), lambda q, l, i, c_ref: (l, 2 * q + c_ref[0], i, 0))

    def body(c_ref, g_ref, r_ref, o_ref):
        del c_ref
        o_ref[...] = (g_ref[...].astype(F32) + r_ref[...].astype(F32)).astype(BF16)

    return pl.pallas_call(
        body, name=name,
        grid_spec=pltpu.PrefetchScalarGridSpec(num_scalar_prefetch=1, grid=(4, nl, rows // tr),
                                               in_specs=[gspec, rspec], out_specs=rspec),
        out_shape=jax.ShapeDtypeStruct(recv.shape, BF16),
        compiler_params=_params(("parallel", "parallel", "parallel")),
    )(core.reshape(1), grads, recv)


def reduce_adamw(name, parts, recv, chip, w, m, v, l0, prev):
    _, nl, rows, cols = parts.shape
    tr = _tile(rows, 256)

    def body(q_ref, p_ref, r_ref, w_ref, m_ref, v_ref, *rest):
        del q_ref
        g_out, d_out, m_out, v_out = rest[-4:]
        g = ((p_ref[...].astype(F32) + r_ref[0].astype(F32)) + r_ref[1].astype(F32)) + r_ref[2].astype(F32)
        d, mn, vn = _adamw(w_ref[...], g, m_ref[...], v_ref[...])
        g_out[...] = g
        d_out[...] = d
        m_out[...] = mn
        v_out[...] = vn

    blk = pl.BlockSpec((None, tr, cols), lambda l, i, q_ref: (l0 + l, i, 0))
    prev = list(prev) if prev else []
    return pl.pallas_call(
        body, name=name,
        grid_spec=pltpu.PrefetchScalarGridSpec(
            num_scalar_prefetch=1, grid=(nl, rows // tr),
            in_specs=[pl.BlockSpec((None, None, tr, cols), lambda l, i, q_ref: (q_ref[0], l, i, 0)),
                      pl.BlockSpec((3, None, tr, cols), lambda l, i, q_ref: (0, l, i, 0)), blk, blk, blk]
            + [pl.BlockSpec(memory_space=pl.ANY)] * len(prev),
            out_specs=[blk] * 4),
        out_shape=[jax.ShapeDtypeStruct(w.shape, F32)] * 4,
        input_output_aliases={6 + i: i for i in range(len(prev))},
        compiler_params=_params(("parallel", "parallel")),
    )(chip.reshape(1), parts, recv, w, m, v, *prev)


def small_adamw(name, gathered, w, m, v):
    _, r, c = gathered.shape

    def body(ga_ref, w_ref, m_ref, v_ref, g_out, d_out, m_out, v_out):
        g = ga_ref[0]
        for d in range(1, N_DEV):
            g = g + ga_ref[d]
        dl, mn, vn = _adamw(w_ref[...], g, m_ref[...], v_ref[...])
        g_out[...] = g
        d_out[...] = dl
        m_out[...] = mn
        v_out[...] = vn

    return pl.pallas_call(
        body, name=name,
        out_shape=[jax.ShapeDtypeStruct((r, c), F32)] * 4,
        compiler_params=_params(),
    )(gathered, w, m, v)


def _heads(t, nh):
    s = t.shape[0]
    return t.reshape(s, nh, HEAD_DIM).transpose(1, 0, 2)


def _heads_t(t, nh):
    return t.T.reshape(nh, HEAD_DIM, t.shape[0])


def _unheads(t):
    nh, s, dh = t.shape
    return t.transpose(1, 0, 2).reshape(s, nh * dh)


def _group_q(t, nb):
    return t.reshape(nb, WINDOW, 2, 8, HEAD_DIM).transpose(2, 0, 3, 1, 4).reshape(2, nb, 8 * WINDOW, HEAD_DIM)


def _group_qt(t, nb):
    return t.reshape(nb, WINDOW, 2, 8, HEAD_DIM).transpose(2, 0, 4, 3, 1).reshape(2, nb, HEAD_DIM, 8 * WINDOW)


def _ungroup_qt(t):
    ng, nb, dh, _ = t.shape
    return t.reshape(ng, nb, dh, 8, WINDOW).transpose(1, 4, 0, 3, 2).reshape(nb * WINDOW, ng * 8 * dh)


def _front_pad(t):
    return jnp.pad(t, ((0, 0), (WINDOW, 0), (0, 0)))


def _add(acc, *ex):
    return (acc + ex[0],)


def local_step(x, target, small, ex):
    s, d = x.shape
    nb = s // WINDOW
    n_a, n_b = small['a_norm'].shape[0], small['b_norm'].shape[0]
    sg = {}
    gb = {}

    def fwd_mm(name, a, wname, layer, epilogue, extras, out_dtypes):
        return mm_nn(name, a, *ex.weight(wname, layer), epilogue, extras, out_dtypes)

    def dx_mm(name, dy, wname, layer, epilogue, extras, out_dtypes):
        return mm_nt(name, dy, *ex.weight(wname, layer), epilogue, extras, out_dtypes)

    def dw_mm(name, a, dy, wname, layer):
        key, slab, shape = ex.grad(wname, layer)
        gb[key] = mm_tn(name, a, dy, gb.get(key), shape, slab)

    bias_flat = bias_table(small['rel_bias'])
    bias_t = bias_flat.reshape(2, 8, WINDOW, 2 * WINDOW).transpose(0, 3, 1, 2).reshape(2, 2 * WINDOW, 8 * WINDOW)
    sink_rows = [jnp.repeat(small['b_sinks'][j], WINDOW).reshape(2, 1, 8 * WINDOW) for j in range(n_b)]

    def mlp_fwd(h, layer):
        n2 = rms_fwd(f"mlp_norm_fwd{layer}", h, small['mlp_norm'][layer])
        u, a = fwd_mm(f"mlp_up_fwd{layer}", n2, 'mlp_up', layer,
                      lambda acc: (acc, jnp.square(jnp.maximum(acc, 0.0))), (), (BF16, BF16))
        (h2,) = fwd_mm(f"mlp_down_fwd{layer}", a, 'mlp_down', layer, _add, (h,), (F32,))
        return h2, (n2, u, a)

    h = x
    saved = []
    for l in range(n_a):
        n1 = rms_fwd(f"a_norm_fwd{l}", h, small['a_norm'][l])
        (qkv,) = fwd_mm(f"a_qkv_fwd{l}", n1, 'a_wqkv', l, lambda acc: (acc,), (), (BF16,))
        nh = d // HEAD_DIM
        o_t, rtab, carried = sb_fwd(f"sb_fwd{l}", _heads_t(qkv[:, :d], nh), _heads(qkv[:, d:2 * d], nh),
                                    _heads_t(qkv[:, 2 * d:], nh), ex.fwd_carry(l))
        ex.fwd_done(l, carried)
        o = o_t.reshape(d, s).T
        (h_mid,) = fwd_mm(f"a_wo_fwd{l}", o, 'a_wo', l, _add, (h,), (F32,))
        h_out, mlp_saved = mlp_fwd(h_mid, l)
        saved.append((h, n1, qkv, o, rtab, h_mid, mlp_saved))
        h = h_out
    h_kv = h
    nkv = rms_fwd("kv_norm_fwd", h, small['kv_norm'])
    (kv,) = fwd_mm("kv_fwd", nkv, 'w_kv', 0, lambda acc, b: (acc + b,), (small['b_kv'].reshape(1, -1),), (BF16,))
    kvw = kv.shape[1] // 2
    kp = _front_pad(_heads(kv[:, :kvw], 2))
    vp = _front_pad(_heads(kv[:, kvw:], 2))
    kpt, vpt = kp.transpose(0, 2, 1), vp.transpose(0, 2, 1)
    for j in range(n_b):
        layer = n_a + j
        n1 = rms_fwd(f"b_norm_fwd{j}", h, small['b_norm'][j])
        (qb,) = fwd_mm(f"b_q_fwd{j}", n1, 'b_wq', j, lambda acc, b: (acc + b,),
                       (small['b_bq'][j].reshape(1, -1),), (BF16,))
        og = swa_fwd(f"swa_fwd{j}", _group_qt(qb, nb), kp, vpt, bias_t, sink_rows[j])
        o = _ungroup_qt(og)
        (h_mid,) = fwd_mm(f"b_wo_fwd{j}", o, 'b_wo', j, lambda acc, hh, b: (acc + hh + b,),
                          (h, small['b_bo'][j].reshape(1, -1)), (F32,))
        h_out, mlp_saved = mlp_fwd(h_mid, layer)
        saved.append((h, n1, qb, o, h_mid, mlp_saved))
        h = h_out

    dh, dhb, dg_final, loss_b = loss_head(h, small['final_norm'], target)
    sg['final_norm'] = dg_final[0]
    sg['mlp_norm'] = [None] * (n_a + n_b)

    def mlp_bwd(dh, dhb, h_mid, mlp_saved, layer):
        n2, u, a = mlp_saved
        (du,) = dx_mm(f"mlp_down_dx{layer}", dhb, 'mlp_down', layer,
                      lambda acc, uu: (acc * (2.0 * jnp.maximum(uu.astype(F32), 0.0)),), (u,), (BF16,))
        dw_mm(f"mlp_down_dw{layer}", a, dhb, 'mlp_down', layer)
        (dn2,) = dx_mm(f"mlp_up_dx{layer}", du, 'mlp_up', layer, lambda acc: (acc,), (), (F32,))
        dw_mm(f"mlp_up_dw{layer}", n2, du, 'mlp_up', layer)
        dh2, dh2b, dg, cs = rms_bwd(f"mlp_norm_bwd{layer}", h_mid, small['mlp_norm'][layer], dn2, dh)
        sg['mlp_norm'][layer] = dg[0]
        return dh2, dh2b, cs

    dkp = jnp.zeros(kp.shape, F32)
    dvp = jnp.zeros(vp.shape, F32)
    sg['b_norm'], sg['b_bq'], sg['b_bo'], sg['b_sinks'] = [None] * n_b, [None] * n_b, [None] * n_b, [None] * n_b
    dbias = [None] * n_b
    for j in reversed(range(n_b)):
        layer = n_a + j
        h_in, n1, qb, o, h_mid, mlp_saved = saved[layer]
        dh, dhb, cs = mlp_bwd(dh, dhb, h_mid, mlp_saved, layer)
        sg['b_bo'][j] = cs[0]
        (do,) = dx_mm(f"b_wo_dx{j}", dhb, 'b_wo', j, lambda acc: (acc,), (), (BF16,))
        dw_mm(f"b_wo_dw{j}", o, dhb, 'b_wo', j)
        dqg, dkp, dvp, dbias[j], dsink = swa_bwd(f"swa_bwd{j}", _group_qt(qb, nb), _group_q(qb, nb), kp, kpt, vp,
                                                 bias_t, sink_rows[j], _group_qt(do, nb), _group_q(do, nb), dkp, dvp)
        sg['b_sinks'][j] = colsum(f"sink_grad{j}", dsink.reshape(16, WINDOW).T)[0]
        dq = _ungroup_qt(dqg)
        sg['b_bq'][j] = colsum(f"b_bq_grad{j}", dq)[0]
        (dn1,) = dx_mm(f"b_q_dx{j}", dq, 'b_wq', j, lambda acc: (acc,), (), (F32,))
        dw_mm(f"b_q_dw{j}", n1, dq, 'b_wq', j)
        dh, dhb, dg, _ = rms_bwd(f"b_norm_bwd{j}", h_in, small['b_norm'][j], dn1, dh)
        sg['b_norm'][j] = dg[0]
    unt = lambda t: t.reshape(2, 2 * WINDOW, 8, WINDOW).transpose(0, 2, 3, 1).reshape(bias_flat.shape)
    sg['rel_bias'] = bias_table_grad(unt(dbias[0]), unt(dbias[1]))[:, :N_BUCKETS].T

    dkv = jnp.concatenate([_unheads(dkp[:, WINDOW:]), _unheads(dvp[:, WINDOW:])], axis=1)
    sg['b_kv'] = colsum("b_kv_grad", dkv)[0]
    dkvb = dkv.astype(BF16)
    (dnkv,) = dx_mm("kv_dx", dkvb, 'w_kv', 0, lambda acc: (acc,), (), (F32,))
    dw_mm("kv_dw", nkv, dkvb, 'w_kv', 0)
    dh, dhb, dg, _ = rms_bwd("kv_norm_bwd", h_kv, small['kv_norm'], dnkv, dh)
    sg['kv_norm'] = dg[0]

    sg['a_norm'] = [None] * n_a
    for l in reversed(range(n_a)):
        h_in, n1, qkv, o, rtab, h_mid, mlp_saved = saved[l]
        dh, dhb, _ = mlp_bwd(dh, dhb, h_mid, mlp_saved, l)
        (do,) = dx_mm(f"a_wo_dx{l}", dhb, 'a_wo', l, lambda acc: (acc,), (), (BF16,))
        dw_mm(f"a_wo_dw{l}", o, dhb, 'a_wo', l)
        nh = d // HEAD_DIM
        qs, ks, vs = qkv[:, :d], qkv[:, d:2 * d], qkv[:, 2 * d:]
        dq_t, dk, dv, carried = sb_bwd(f"sb_bwd{l}", _heads_t(qs, nh), _heads(qs, nh), _heads(ks, nh),
                                       _heads_t(ks, nh), _heads(vs, nh), _heads_t(do, nh), _heads(do, nh), rtab,
                                       ex.bwd_carry(l, gb))
        ex.bwd_done(l, carried)
        dqkv = jnp.concatenate([dq_t.reshape(d, s).T, _unheads(dk), _unheads(dv)], axis=1).astype(BF16)
        (dn1,) = dx_mm(f"a_qkv_dx{l}", dqkv, 'a_wqkv', l, lambda acc: (acc,), (), (F32,))
        dw_mm(f"a_qkv_dw{l}", n1, dqkv, 'a_wqkv', l)
        dh, dhb, dg, _ = rms_bwd(f"a_norm_bwd{l}", h_in, small['a_norm'][l], dn1, dh)
        sg['a_norm'][l] = dg[0]

    small_grads = {
        'a_norm': jnp.stack(sg['a_norm']), 'kv_norm': sg['kv_norm'], 'b_kv': sg['b_kv'],
        'b_norm': jnp.stack(sg['b_norm']), 'b_bq': jnp.stack(sg['b_bq']), 'b_sinks': jnp.stack(sg['b_sinks']),
        'b_bo': jnp.stack(sg['b_bo']), 'rel_bias': sg['rel_bias'], 'mlp_norm': jnp.stack(sg['mlp_norm']),
        'final_norm': sg['final_norm'],
    }
    return loss_b, dh, gb, small_grads


def _full_shape(name, shard_shape):
    if name in COL_SHARDED:
        return shard_shape[:2] + (N_DEV * shard_shape[2],)
    nl, r, n = shard_shape
    return (nl, N_DEV, r, n)


def _as_w3(name, full):
    if name in COL_SHARDED:
        return full
    nl, nd, r, n = full.shape
    return full.reshape(nl, nd * r, n)


LAYERED = ('a_wqkv', 'a_wo', 'mlp_up', 'mlp_down')
RS_GROUPS = {
    'A': (('mlp_up', 1, 3), ('mlp_down', 1, 3), ('b_wq', 0, 2), ('b_wo', 0, 2), ('w_kv', 0, 1)),
    'B': (('a_wqkv', 1, 1), ('a_wo', 1, 1), ('mlp_up', 0, 1), ('mlp_down', 0, 1)),
    'C': (('a_wqkv', 0, 1), ('a_wo', 0, 1)),
}


class _Exchanges:
    def __init__(self, full0, shards1, core, chip, w3, m3, v3):
        self.wbuf = {0: {n: _as_w3(n, full0[n]) for n in LAYERED}}
        self.shards1, self.core, self.chip = shards1, core, chip
        self.w3, self.m3, self.v3 = w3, m3, v3
        self.shard_dims = {n: w3[n].shape[1:] for n in BIG}
        self.parts = {}
        self.out = {}

    def weight(self, name, layer):
        if name in LAYERED:
            return (self.wbuf[0][name], 0) if layer == 0 else (self.wbuf[1][name], layer - 1)
        return self.wbuf[1][name], layer

    def fwd_carry(self, layer):
        if layer != 0:
            return None
        shards = [self.shards1[n] for n in BIG]
        return ag_direct(BIG, shards, [_full_shape(n, sh.shape) for n, sh in zip(BIG, shards)])

    def fwd_done(self, layer, carried):
        if layer == 0:
            self.wbuf[1] = {n: _as_w3(n, f) for n, f in zip(BIG, ag_forward(BIG, list(carried)))}

    def grad(self, name, layer):
        for group, members in RS_GROUPS.items():
            for n, l0, nl in members:
                if n == name and l0 <= layer < l0 + nl:
                    k_full, n_full = self.wbuf[0 if name in LAYERED else 1][name].shape[1:]
                    return (group, name), layer - l0, (nl, k_full, n_full)
        raise KeyError((name, layer))

    def _sibling_stage(self, group, gb):
        names = [n for n, _, _ in RS_GROUPS[group]]
        shapes = [(nl,) + self.shard_dims[n] for n, _, nl in RS_GROUPS[group]]
        gfull = [gb[(group, n)].reshape(_full_shape(n, sh)) for n, sh in zip(names, shapes)]
        recv = sibling_exchange(group, names, gfull, shapes)
        self.parts[group] = [sibling_sum(f"rs_sibling_sum_{group}_{n}", n in COL_SHARDED, g, r, self.core)
                             for n, g, r in zip(names, gfull, recv)]
        return names, shapes

    def bwd_carry(self, layer, gb):
        names, shapes = self._sibling_stage('A' if layer == 1 else 'B', gb)
        return chip_exchange(names, self.parts['A' if layer == 1 else 'B'], shapes)

    def bwd_done(self, layer, carried):
        self._adamw('A' if layer == 1 else 'B', carried)

    def finish(self, gb):
        names, shapes = self._sibling_stage('C', gb)
        ce = chip_exchange(names, self.parts['C'], shapes)
        self._adamw('C', comm_call("rs_chip_exchange_C", ce.build, ce.ins, ce.out_shapes, ce.sems))
        return self.out

    def _adamw(self, group, recv2):
        for (n, l0, _), p, r in zip(RS_GROUPS[group], self.parts[group], recv2):
            self.out[n] = reduce_adamw(f"adamw_{group}_{n}", p, r, self.chip, self.w3[n], self.m3[n], self.v3[n],
                                       l0, self.out.get(n))


def _pack_small(vals):
    flat = jnp.concatenate([vals[n].reshape(-1).astype(F32) for n in SMALL] + [vals['loss'].reshape(-1)])
    rows = -(-flat.shape[0] // 1024) * 8
    return jnp.pad(flat, (0, rows * 128 - flat.shape[0])).reshape(rows, 128)


def _unpack_small(packed, shapes):
    flat = packed.reshape(-1)
    out, off = {}, 0
    for n in SMALL + ['loss']:
        size = int(np.prod(shapes[n]))
        out[n] = flat[off:off + size].reshape(shapes[n])
        off += size
    return out


def kernel(x, a_norm, a_wqkv, a_wo, kv_norm, w_kv, b_kv, b_norm, b_wq, b_bq, b_sinks, b_wo, b_bo, rel_bias, mlp_norm, mlp_up, mlp_down, final_norm, loss_target, m_a_norm, m_a_wqkv, m_a_wo, m_kv_norm, m_w_kv, m_b_kv, m_b_norm, m_b_wq, m_b_bq, m_b_sinks, m_b_wo, m_b_bo, m_rel_bias, m_mlp_norm, m_mlp_up, m_mlp_down, m_final_norm, v_a_norm, v_a_wqkv, v_a_wo, v_kv_norm, v_w_kv, v_b_kv, v_b_norm, v_b_wq, v_b_bq, v_b_sinks, v_b_wo, v_b_bo, v_rel_bias, v_mlp_norm, v_mlp_up, v_mlp_down, v_final_norm):
    w = dict(a_norm=a_norm, a_wqkv=a_wqkv, a_wo=a_wo, kv_norm=kv_norm, w_kv=w_kv, b_kv=b_kv, b_norm=b_norm,
             b_wq=b_wq, b_bq=b_bq, b_sinks=b_sinks, b_wo=b_wo, b_bo=b_bo, rel_bias=rel_bias, mlp_norm=mlp_norm,
             mlp_up=mlp_up, mlp_down=mlp_down, final_norm=final_norm)
    m = dict(a_norm=m_a_norm, a_wqkv=m_a_wqkv, a_wo=m_a_wo, kv_norm=m_kv_norm, w_kv=m_w_kv, b_kv=m_b_kv,
             b_norm=m_b_norm, b_wq=m_b_wq, b_bq=m_b_bq, b_sinks=m_b_sinks, b_wo=m_b_wo, b_bo=m_b_bo,
             rel_bias=m_rel_bias, mlp_norm=m_mlp_norm, mlp_up=m_mlp_up, mlp_down=m_mlp_down, final_norm=m_final_norm)
    v = dict(a_norm=v_a_norm, a_wqkv=v_a_wqkv, a_wo=v_a_wo, kv_norm=v_kv_norm, w_kv=v_w_kv, b_kv=v_b_kv,
             b_norm=v_b_norm, b_wq=v_b_wq, b_bq=v_b_bq, b_sinks=v_b_sinks, b_wo=v_b_wo, b_bo=v_b_bo,
             rel_bias=v_rel_bias, mlp_norm=v_mlp_norm, mlp_up=v_mlp_up, mlp_down=v_mlp_down, final_norm=v_final_norm)
    px, py, pc = _place()
    me = 4 * px + 2 * py + pc
    chip = (2 * px + py).astype(jnp.int32)
    core = pc.astype(jnp.int32)

    as3 = lambda t: t[None] if t.ndim == 2 else t
    w3, m3, v3 = ({n: as3(src[n]) for n in BIG} for src in (w, m, v))
    shards0 = {n: w3[n][:1].astype(BF16) for n in LAYERED}
    shards1 = {n: (w3[n][1:] if n in LAYERED else w3[n]).astype(BF16) for n in BIG}
    an_pad = jnp.zeros((8, 128), F32).at[:a_norm.shape[0]].set(a_norm)
    names0 = list(LAYERED) + ['a_norm']
    full0 = all_gather_weights(names0, [shards0[n] for n in LAYERED] + [an_pad],
                               [_full_shape(n, shards0[n].shape) for n in LAYERED] + [(N_DEV, 8, 128)])
    full0 = dict(zip(names0, full0))
    n_a = a_norm.shape[0]
    small = {n: w[n] for n in SMALL}
    small['a_norm'] = full0['a_norm'][:, :n_a].transpose(1, 0, 2).reshape(n_a, -1)

    ex = _Exchanges(full0, shards1, core, chip, w3, m3, v3)
    loss_b, grad_x, gb, sgrads = local_step(x[0], loss_target[0], small, ex)
    out = {n: [t.reshape(w[n].shape) for t in bufs] for n, bufs in ex.finish(gb).items()}

    sgrads['loss'] = loss_b[0, :1]
    gathered = all_gather_rows(_pack_small(sgrads))
    shapes = {n: w[n].shape for n in SMALL}
    shapes['a_norm'] = (n_a, a_norm.shape[1] * N_DEV)
    shapes['loss'] = (1,)
    zeros1 = jnp.zeros((1,), F32)

    def packed(src):
        vals = {n: src[n] for n in SMALL}
        vals['a_norm'] = jnp.zeros(shapes['a_norm'], F32)
        vals['loss'] = zeros1
        return _pack_small(vals)

    sm = small_adamw("adamw_small", gathered, packed(w), packed(m), packed(v))
    sm = [_unpack_small(t, shapes) for t in sm]
    g_an = lax.dynamic_slice_in_dim(sm[0]['a_norm'], me * a_norm.shape[1], a_norm.shape[1], axis=1)
    pad = lambda t: jnp.zeros((8, 128), F32).at[:n_a].set(t)
    gathered_an = jnp.zeros((N_DEV, 8, 128), F32).at[0].set(pad(g_an))
    an = small_adamw("adamw_a_norm", gathered_an, pad(a_norm), pad(m_a_norm), pad(v_a_norm))
    for i in range(4):
        sm[i]['a_norm'] = an[i][:n_a]
    for n in BIG:
        for i in range(4):
            sm[i][n] = out[n][i]
    loss = sm[0]['loss'][0]
    return (loss, grad_x[None], *[sm[0][n] for n in WEIGHTS], *[sm[1][n] for n in WEIGHTS],
            *[sm[2][n] for n in WEIGHTS], *[sm[3][n] for n in WEIGHTS])
```

```python
import functools
import math

import numpy as np
import jax
import jax.numpy as jnp
from jax import lax
from jax.experimental import pallas as pl
from jax.experimental.pallas import tpu as pltpu

F32 = jnp.float32
BF16 = jnp.bfloat16
MESH = pl.DeviceIdType.MESH

N_DEV = 8
HEAD_DIM = 64
WINDOW = 128
N_BUCKETS = 32
EPS = 1e-5
NEG_INF = -1e30
Q_SCALE = 1.0 / math.sqrt(HEAD_DIM)
LOG2E = 1.4426950408889634

ADAM_LR, ADAM_B1, ADAM_B2, ADAM_EPS, ADAM_WD, ADAM_STEP = 0.001, 0.9, 0.999, 1e-08, 0.01, 10

SB_BQ = 512
SB_BK = 128
ROW_TILE = 512
VMEM_LIMIT = 56 * 1024 * 1024

WEIGHTS = ['a_norm', 'a_wqkv', 'a_wo', 'kv_norm', 'w_kv', 'b_kv', 'b_norm', 'b_wq', 'b_bq', 'b_sinks', 'b_wo',
           'b_bo', 'rel_bias', 'mlp_norm', 'mlp_up', 'mlp_down', 'final_norm']
BIG = ['a_wqkv', 'a_wo', 'w_kv', 'b_wq', 'b_wo', 'mlp_up', 'mlp_down']
COL_SHARDED = ('a_wqkv', 'mlp_up')
SMALL = ['a_norm', 'kv_norm', 'b_kv', 'b_norm', 'b_bq', 'b_sinks', 'b_bo', 'rel_bias', 'mlp_norm', 'final_norm']


def _params(sem=None):
    return pltpu.CompilerParams(dimension_semantics=sem, vmem_limit_bytes=VMEM_LIMIT)


def _pick(n, cands):
    for c in cands:
        if n % c == 0:
            return c
    raise ValueError(n)


def _tile(n, want):
    return n if n <= want else _pick(n, (want, want // 2, want // 4))


def mm_nn(name, a, w3, layer, epilogue, extras, out_dtypes):
    m, k = a.shape
    _, kw, n = w3.shape
    assert kw == k
    tm = _tile(m, 1024 if k <= 1024 else 512)
    tn = _tile(n, 1024)
    ne, no = len(extras), len(out_dtypes)

    def body(a_ref, w_ref, *rest):
        ex, outs = rest[:ne], rest[ne:ne + no]
        res = epilogue(jnp.dot(a_ref[...], w_ref[...], preferred_element_type=F32), *[e[...] for e in ex])
        for o, r in zip(outs, res):
            o[...] = r.astype(o.dtype)

    tile = pl.BlockSpec((tm, tn), lambda i, j: (i, j))
    ex_specs = [tile if e.shape[0] == m else pl.BlockSpec((1, tn), lambda i, j: (0, j)) for e in extras]
    return pl.pallas_call(
        body, name=name, grid=(m // tm, n // tn),
        in_specs=[pl.BlockSpec((tm, k), lambda i, j: (i, 0)),
                  pl.BlockSpec((None, k, tn), lambda i, j: (layer, 0, j))] + ex_specs,
        out_specs=[tile] * no,
        out_shape=[jax.ShapeDtypeStruct((m, n), d) for d in out_dtypes],
        compiler_params=_params(("parallel", "parallel")),
    )(a, w3, *extras)


def mm_nt(name, dy, w3, layer, epilogue, extras, out_dtypes):
    m, n = dy.shape
    _, k, nw = w3.shape
    assert nw == n
    tm = _tile(m, 1024 if n <= 1024 else 512)
    tko = _tile(k, 1024)
    ne, no = len(extras), len(out_dtypes)

    def body(a_ref, w_ref, *rest):
        ex, outs = rest[:ne], rest[ne:ne + no]
        acc = lax.dot_general(a_ref[...], w_ref[...], (((1,), (1,)), ((), ())), preferred_element_type=F32)
        res = epilogue(acc, *[e[...] for e in ex])
        for o, v in zip(outs, res):
            o[...] = v.astype(o.dtype)

    tile = pl.BlockSpec((tm, tko), lambda i, ko: (i, ko))
    return pl.pallas_call(
        body, name=name, grid=(m // tm, k // tko),
        in_specs=[pl.BlockSpec((tm, n), lambda i, ko: (i, 0)),
                  pl.BlockSpec((None, tko, n), lambda i, ko: (layer, ko, 0))] + [tile] * ne,
        out_specs=[tile] * no,
        out_shape=[jax.ShapeDtypeStruct((m, k), d) for d in out_dtypes],
        compiler_params=_params(("parallel", "parallel")),
    )(dy, w3, *extras)


def mm_tn(name, x, dy, gbuf, shape, layer):
    s, k = x.shape
    _, kw, n = shape
    assert kw == k and dy.shape == (s, n)
    tkk = _tile(k, 512)
    tn = _tile(n, 1024)

    def body(x_ref, dy_ref, *rest):
        g_out = rest[-1]
        g_out[...] = lax.dot_general(x_ref[...], dy_ref[...], (((0,), (0,)), ((), ())),
                                     preferred_element_type=F32).astype(g_out.dtype)

    prev = [] if gbuf is None else [gbuf]
    return pl.pallas_call(
        body, name=name, grid=(k // tkk, n // tn),
        in_specs=[pl.BlockSpec((s, tkk), lambda ki, j: (0, ki)),
                  pl.BlockSpec((s, tn), lambda ki, j: (0, j))] + [pl.BlockSpec(memory_space=pl.ANY)] * len(prev),
        out_specs=pl.BlockSpec((None, tkk, tn), lambda ki, j: (layer, ki, j)),
        out_shape=jax.ShapeDtypeStruct(shape, BF16),
        input_output_aliases={2: 0} if prev else {},
        compiler_params=_params(("parallel", "parallel")),
    )(x, dy, *prev)


def rms_fwd(name, h, g):
    s, d = h.shape
    tr = _pick(s, (ROW_TILE, 256, 128))

    def body(h_ref, g_ref, o_ref):
        x = h_ref[...]
        r = lax.rsqrt(jnp.mean(x * x, axis=-1, keepdims=True) + EPS)
        o_ref[...] = (x * r * g_ref[...]).astype(o_ref.dtype)

    return pl.pallas_call(
        body, name=name, grid=(s // tr,),
        in_specs=[pl.BlockSpec((tr, d), lambda i: (i, 0)), pl.BlockSpec((1, d), lambda i: (0, 0))],
        out_specs=pl.BlockSpec((tr, d), lambda i: (i, 0)),
        out_shape=jax.ShapeDtypeStruct((s, d), BF16),
        compiler_params=_params(("parallel",)),
    )(h, g.reshape(1, d))


def rms_bwd(name, h, g, dn, dres):
    s, d = h.shape
    tr = _pick(s, (ROW_TILE, 256, 128))

    def body(h_ref, g_ref, dn_ref, dres_ref, dx_ref, dxb_ref, dg_ref, cs_ref):
        i = pl.program_id(0)
        x = h_ref[...]
        r = lax.rsqrt(jnp.mean(x * x, axis=-1, keepdims=True) + EPS)
        xh = x * r
        dn_ = dn_ref[...]
        dyg = dn_ * g_ref[...]
        dx = dres_ref[...] + r * (dyg - xh * jnp.mean(dyg * xh, axis=-1, keepdims=True))
        dx_ref[...] = dx
        dxb_ref[...] = dx.astype(BF16)

        @pl.when(i == 0)
        def _():
            dg_ref[...] = jnp.zeros_like(dg_ref)
            cs_ref[...] = jnp.zeros_like(cs_ref)

        dg_ref[...] += jnp.sum(dn_ * xh, axis=0, keepdims=True)
        cs_ref[...] += jnp.sum(dx, axis=0, keepdims=True)

    row = pl.BlockSpec((tr, d), lambda i: (i, 0))
    vec = pl.BlockSpec((1, d), lambda i: (0, 0))
    return pl.pallas_call(
        body, name=name, grid=(s // tr,),
        in_specs=[row, vec, row, row],
        out_specs=[row, row, vec, vec],
        out_shape=[jax.ShapeDtypeStruct((s, d), F32), jax.ShapeDtypeStruct((s, d), BF16),
                   jax.ShapeDtypeStruct((1, d), F32), jax.ShapeDtypeStruct((1, d), F32)],
        compiler_params=_params(("arbitrary",)),
    )(h, g.reshape(1, d), dn, dres)


def loss_head(h, g, target):
    s, d = h.shape
    tr = _pick(s, (ROW_TILE, 256, 128))

    def body(h_ref, g_ref, t_ref, dx_ref, dxb_ref, dg_ref, loss_ref):
        i = pl.program_id(0)
        x = h_ref[...]
        r = lax.rsqrt(jnp.mean(x * x, axis=-1, keepdims=True) + EPS)
        xh = x * r
        gw = g_ref[...]
        err = xh * gw - t_ref[...]
        dn_ = err * (1.0 / d)
        dyg = dn_ * gw
        dx = r * (dyg - xh * jnp.mean(dyg * xh, axis=-1, keepdims=True))
        dx_ref[...] = dx
        dxb_ref[...] = dx.astype(BF16)

        @pl.when(i == 0)
        def _():
            dg_ref[...] = jnp.zeros_like(dg_ref)
            loss_ref[...] = jnp.zeros_like(loss_ref)

        dg_ref[...] += jnp.sum(dn_ * xh, axis=0, keepdims=True)
        per_row = jnp.sum(err * err, axis=-1, keepdims=True) * (0.5 / d)
        loss_ref[...] += jnp.broadcast_to(jnp.sum(per_row, axis=0, keepdims=True), loss_ref.shape)

    row = pl.BlockSpec((tr, d), lambda i: (i, 0))
    vec = pl.BlockSpec((1, d), lambda i: (0, 0))
    return pl.pallas_call(
        body, name="loss_head", grid=(s // tr,),
        in_specs=[row, vec, row],
        out_specs=[row, row, vec, pl.BlockSpec((1, 128), lambda i: (0, 0))],
        out_shape=[jax.ShapeDtypeStruct((s, d), F32), jax.ShapeDtypeStruct((s, d), BF16),
                   jax.ShapeDtypeStruct((1, d), F32), jax.ShapeDtypeStruct((1, 128), F32)],
        compiler_params=_params(("arbitrary",)),
    )(h, g.reshape(1, d), target)


def colsum(name, x):
    s, n = x.shape
    tr = _pick(s, (ROW_TILE, 256, 128))

    def body(x_ref, o_ref):
        @pl.when(pl.program_id(0) == 0)
        def _():
            o_ref[...] = jnp.zeros_like(o_ref)

        o_ref[...] += jnp.sum(x_ref[...].astype(F32), axis=0, keepdims=True)

    return pl.pallas_call(
        body, name=name, grid=(s // tr,),
        in_specs=[pl.BlockSpec((tr, n), lambda i: (i, 0))],
        out_specs=pl.BlockSpec((1, n), lambda i: (0, 0)),
        out_shape=jax.ShapeDtypeStruct((1, n), F32),
        compiler_params=_params(("arbitrary",)),
    )(x)


def _tri_rows(reverse):
    i = np.arange(SB_BK)
    tri = (i[None, :] >= i[:, None]) if reverse else (i[None, :] <= i[:, None])
    tri = np.concatenate([tri, tri], axis=1)
    return jnp.asarray(np.concatenate([tri, np.ones((8, 2 * SB_BK), bool)], axis=0), BF16)


def _hi_lo_rows(x):
    hi = x.astype(BF16)
    lo = (x - hi.astype(F32)).astype(BF16)
    return jnp.concatenate([hi, lo], axis=0)


def _softplus2(zs):
    neg_abs = lax.bitcast_convert_type(lax.bitcast_convert_type(zs, jnp.uint32) | jnp.uint32(0x80000000), F32)
    return jnp.maximum(zs, 0.0) + jnp.log2(1.0 + jnp.exp2(neg_abs))


def _pair_mask(first_rel_block, bq):
    key = lax.broadcasted_iota(jnp.int32, (2 * SB_BK, bq), 0) + first_rel_block * SB_BK
    qry = lax.broadcasted_iota(jnp.int32, (2 * SB_BK, bq), 1)
    return key < qry


def _row_of(table8, sub8, r):
    return jnp.sum(jnp.where(sub8 == r, table8, 0.0), axis=0, keepdims=True)


def _keys(j0):
    return pl.ds(pl.multiple_of(j0 * SB_BK, 2 * SB_BK), 2 * SB_BK)


class Carry:
    def __init__(self, build, ins, out_shapes, sems):
        self.build, self.ins, self.out_shapes, self.sems = build, list(ins), list(out_shapes), list(sems)


def _carried(carry, rest, n_out, n_scratch, first, last):
    n_ci = len(carry.ins) if carry else 0
    n_co = len(carry.out_shapes) if carry else 0
    cin, outs = rest[:n_ci], rest[n_ci:n_ci + n_out]
    cout = rest[n_ci + n_out:n_ci + n_out + n_co]
    scratch = rest[n_ci + n_out + n_co:n_ci + n_out + n_co + n_scratch]
    csems = rest[n_ci + n_out + n_co + n_scratch:]

    def start():
        if carry:
            @pl.when(first)
            def _():
                for cp in carry.build(cin, cout, *csems):
                    cp.start()

    def wait():
        if carry:
            @pl.when(last)
            def _():
                for cp in carry.build(cin, cout, *csems):
                    cp.wait()

    return outs, scratch, start, wait


def _contract0(a, b):
    return lax.dot_general(a, b, (((0,), (0,)), ((), ())), preferred_element_type=F32)


def _contract1(a, b):
    return lax.dot_general(a, b, (((1,), (1,)), ((), ())), preferred_element_type=F32)


def sb_fwd(name, qkvt, exchange=None):
    nh, dh, s = qkvt.shape[0] // 3, qkvt.shape[1], qkvt.shape[2]
    bq = SB_BQ
    per_q = bq // SB_BK
    nkb = s // SB_BK
    assert s % bq == 0 and per_q == 4 and nkb % 8 == 0

    def body(q_ref, k_ref, v_ref, a_ref, *rest):
        i = pl.program_id(1)
        first = (pl.program_id(0) == 0) & (i == 0)
        last = (pl.program_id(0) == nh - 1) & (i == s // bq - 1)
        (o_ref, rtab_ref), (acc, zbuf, wbuf), start_carried, wait_carried = _carried(exchange, rest, 2, 3, first, last)
        start_carried()
        qb = q_ref[...] * Q_SCALE
        tri = a_ref[...]
        sub8 = lax.broadcasted_iota(jnp.int32, (8, bq), 0)
        acc[...] = jnp.zeros_like(acc)
        rtab_ref[...] = jnp.zeros_like(rtab_ref)

        def scores(j0):
            return _contract0(k_ref[:, _keys(j0)], qb) * LOG2E

        def pair(j0, slot, run, rt8, mask, has_prev):
            zs = zbuf[slot]
            zbuf[1 - slot] = scores(jnp.maximum(j0 - 2, 0))
            if has_prev:
                acc[...] += jnp.dot(v_ref[:, _keys(j0 + 2)], wbuf[1 - slot], preferred_element_type=F32)
            p = _softplus2(zs)
            if mask is not None:
                p = jnp.where(mask, p, 0.0)
            cr1 = jnp.dot(tri, _hi_lo_rows(p[SB_BK:]), preferred_element_type=F32)
            cr0 = jnp.dot(tri, _hi_lo_rows(p[:SB_BK]), preferred_element_type=F32)
            run1 = run + cr1[SB_BK:SB_BK + 1]
            w = jnp.exp2(jnp.concatenate([zs[:SB_BK] - cr0[:SB_BK] - run1, zs[SB_BK:] - cr1[:SB_BK] - run], axis=0))
            if mask is not None:
                w = jnp.where(mask, w, 0.0)
            wbuf[slot] = w.astype(BF16)
            rt8 = jnp.where(sub8 == (j0 + 1) % 8, run, jnp.where(sub8 == j0 % 8, run1, rt8))
            rtab_ref[pl.ds(pl.multiple_of((j0 // 8) * 8, 8), 8), :] = rt8
            return run1 + cr0[SB_BK:SB_BK + 1], rt8

        top = i * per_q
        zbuf[0] = scores(top + 2)
        carry = (jnp.zeros((1, bq), F32), jnp.zeros((8, bq), F32))
        carry = pair(top + 2, 0, *carry, _pair_mask(2, bq), False)
        carry = pair(top, 1, *carry, _pair_mask(0, bq), True)

        def step(it, c):
            j0 = top - 2 - 4 * it
            c = pair(j0, 0, *c, None, True)
            return pair(j0 - 2, 1, *c, None, True)

        lax.fori_loop(0, i, step, carry)
        acc[...] += jnp.dot(v_ref[:, _keys(0)], wbuf[1], preferred_element_type=F32)
        o_ref[...] = acc[...].astype(o_ref.dtype)
        wait_carried()

    qspec = pl.BlockSpec((None, dh, bq), lambda h, i: (h, 0, i))
    hbm = pl.BlockSpec(memory_space=pl.ANY)
    c_ins, c_outs, c_sems = (exchange.ins, exchange.out_shapes, exchange.sems) if exchange else ([], [], [])
    outs = pl.pallas_call(
        body, name=name, grid=(nh, s // bq),
        in_specs=[qspec, pl.BlockSpec((None, dh, s), lambda h, i: (h + nh, 0, 0)),
                  pl.BlockSpec((None, dh, s), lambda h, i: (h + 2 * nh, 0, 0)),
                  pl.BlockSpec((SB_BK + 8, 2 * SB_BK), lambda h, i: (0, 0))] + [hbm] * len(c_ins),
        out_specs=[qspec, pl.BlockSpec((None, nkb, bq), lambda h, i: (h, 0, i))] + [hbm] * len(c_outs),
        out_shape=[jax.ShapeDtypeStruct((nh, dh, s), BF16), jax.ShapeDtypeStruct((nh, nkb, s), F32)] + c_outs,
        scratch_shapes=[pltpu.VMEM((dh, bq), F32), pltpu.VMEM((2, 2 * SB_BK, bq), F32),
                        pltpu.VMEM((2, 2 * SB_BK, bq), BF16)] + c_sems,
        compiler_params=_params(("arbitrary", "arbitrary")),
    )(qkvt, qkvt, qkvt, _tri_rows(True), *c_ins)
    return outs[0], outs[1], outs[2:]


def sb_bwd(name, qkvt, dot_, rtab, exchange=None):
    nh, dh, s = qkvt.shape[0] // 3, qkvt.shape[1], qkvt.shape[2]
    bq = SB_BQ
    per_q = bq // SB_BK
    nkb = s // SB_BK
    nq = s // bq

    def body(qt_ref, kt_ref, vt_ref, dot_ref, rtab_ref, ar_ref, af_ref, *rest):
        i = pl.program_id(1)
        first = (pl.program_id(0) == 0) & (i == 0)
        last = (pl.program_id(0) == nh - 1) & (i == nq - 1)
        (dq_ref, dk_ref, dv_ref), (dq_acc, dk_acc, dv_acc, zbuf, dwbuf, dzbuf, wbuf), start_carried, wait_carried = \
            _carried(exchange, rest, 3, 7, first, last)
        start_carried()

        @pl.when(i == 0)
        def _():
            dk_acc[...] = jnp.zeros_like(dk_acc)
            dv_acc[...] = jnp.zeros_like(dv_acc)

        qtb = qt_ref[...] * Q_SCALE
        dotb = dot_ref[...]
        tri_rev = ar_ref[...][:SB_BK]
        tri_fwd = af_ref[...]
        sub8 = lax.broadcasted_iota(jnp.int32, (8, bq), 0)
        dq_acc[...] = jnp.zeros_like(dq_acc)
        last_j = i * per_q + 2

        def issue(j0, slot):
            zbuf[slot] = _contract0(kt_ref[:, _keys(j0)], qtb) * LOG2E
            dwbuf[slot] = _contract0(vt_ref[:, _keys(j0)], dotb)

        def retire(j0, slot):
            keys = _keys(j0)
            dq_acc[...] += jnp.dot(kt_ref[:, keys], dzbuf[slot], preferred_element_type=F32)
            dk_acc[:, keys] += _contract1(qtb, dzbuf[slot])
            dv_acc[:, keys] += _contract1(dotb, wbuf[slot])

        def pair(j0, slot, g_run, mask):
            zs = zbuf[slot]
            dw = dwbuf[slot]
            issue(jnp.minimum(j0 + 2, last_j), 1 - slot)
            retire(jnp.maximum(j0 - 2, 0), 1 - slot)
            p_raw = _softplus2(zs)
            p = p_raw if mask is None else jnp.where(mask, p_raw, 0.0)
            c0 = jnp.dot(tri_rev, _hi_lo_rows(p[:SB_BK]), preferred_element_type=F32)
            c1 = jnp.dot(tri_rev, _hi_lo_rows(p[SB_BK:]), preferred_element_type=F32)
            rt8 = rtab_ref[pl.ds(pl.multiple_of((j0 // 8) * 8, 8), 8), :]
            r0 = _row_of(rt8, sub8, j0 % 8)
            r1 = _row_of(rt8, sub8, (j0 + 1) % 8)
            w = jnp.exp2(jnp.concatenate([zs[:SB_BK] - c0 - r0, zs[SB_BK:] - c1 - r1], axis=0))
            if mask is not None:
                w = jnp.where(mask, w, 0.0)
            g = w * dw
            gg0 = jnp.dot(tri_fwd, _hi_lo_rows(g[:SB_BK]), preferred_element_type=F32)
            gg1 = jnp.dot(tri_fwd, _hi_lo_rows(g[SB_BK:]), preferred_element_type=F32)
            g_run1 = g_run + gg0[SB_BK:SB_BK + 1]
            g_pre = jnp.concatenate([gg0[:SB_BK] + g_run, gg1[:SB_BK] + g_run1], axis=0)
            dz = g - jnp.exp2(zs - p_raw) * g_pre
            if mask is not None:
                dz = jnp.where(mask, dz, 0.0)
            dzbuf[slot] = dz.astype(BF16)
            wbuf[slot] = w.astype(BF16)
            return g_run1 + gg1[SB_BK:SB_BK + 1]

        issue(0, 0)
        dzbuf[1] = jnp.zeros((2 * SB_BK, bq), BF16)
        wbuf[1] = jnp.zeros((2 * SB_BK, bq), BF16)

        def step(it, g_run):
            g_run = pair(4 * it, 0, g_run, None)
            return pair(4 * it + 2, 1, g_run, None)

        g_run = lax.fori_loop(0, i, step, jnp.zeros((1, bq), F32))
        g_run = pair(last_j - 2, 0, g_run, _pair_mask(0, bq))
        pair(last_j, 1, g_run, _pair_mask(2, bq))
        retire(last_j, 1)
        dq_ref[...] = (dq_acc[...] * Q_SCALE).astype(dq_ref.dtype)

        @pl.when(i == nq - 1)
        def _():
            dk_ref[...] = dk_acc[...].astype(dk_ref.dtype)
            dv_ref[...] = dv_acc[...].astype(dv_ref.dtype)

        wait_carried()

    tspec = pl.BlockSpec((None, dh, bq), lambda h, i: (h, 0, i))
    fullspec = pl.BlockSpec((None, dh, s), lambda h, i: (h, 0, 0))
    aspec = pl.BlockSpec((SB_BK + 8, 2 * SB_BK), lambda h, i: (0, 0))
    pair_f32 = pltpu.VMEM((2, 2 * SB_BK, bq), F32)
    pair_bf16 = pltpu.VMEM((2, 2 * SB_BK, bq), BF16)
    hbm = pl.BlockSpec(memory_space=pl.ANY)
    c_ins, c_outs, c_sems = (exchange.ins, exchange.out_shapes, exchange.sems) if exchange else ([], [], [])
    outs = pl.pallas_call(
        body, name=name, grid=(nh, s // bq),
        in_specs=[tspec, pl.BlockSpec((None, dh, s), lambda h, i: (h + nh, 0, 0)),
                  pl.BlockSpec((None, dh, s), lambda h, i: (h + 2 * nh, 0, 0)), tspec,
                  pl.BlockSpec((None, nkb, bq), lambda h, i: (h, 0, i)), aspec, aspec] + [hbm] * len(c_ins),
        out_specs=[tspec, fullspec, fullspec] + [hbm] * len(c_outs),
        out_shape=[jax.ShapeDtypeStruct((nh, dh, s), BF16)] * 3 + c_outs,
        scratch_shapes=[pltpu.VMEM((dh, bq), F32), pltpu.VMEM((dh, s), F32), pltpu.VMEM((dh, s), F32),
                        pair_f32, pair_f32, pair_bf16, pair_bf16] + c_sems,
        compiler_params=_params(("arbitrary", "arbitrary")),
    )(qkvt, qkvt, qkvt, dot_, rtab, _tri_rows(True), _tri_rows(False), *c_ins)
    return outs[0], outs[1], outs[2], outs[3:]


SWA_QB = 2


def _swa_probs(qt, kt, bias_t, sink, i):
    cols = qt.shape[1]
    sc = _contract0(kt, qt) + bias_t
    kj = lax.broadcasted_iota(jnp.int32, (2 * WINDOW, cols), 0)
    qi = lax.broadcasted_iota(jnp.int32, (2 * WINDOW, cols), 1) & (WINDOW - 1)
    dist = qi + WINDOW - kj
    valid = (dist >= 0) & (dist < WINDOW) & ((kj >= WINDOW) | (i > 0))
    sc = jnp.where(valid, sc, NEG_INF)
    mx = jnp.maximum(jnp.max(sc, axis=0, keepdims=True), sink)
    p = jnp.exp(sc - mx)
    p_sink = jnp.exp(sink - mx)
    inv = 1.0 / (jnp.sum(p, axis=0, keepdims=True) + p_sink)
    return p, p_sink, inv


def _band(i):
    return pl.ds(pl.multiple_of(i * WINDOW, WINDOW), 2 * WINDOW)


def _heads_to_lanes(blk):
    return jnp.concatenate([blk[r * HEAD_DIM:(r + 1) * HEAD_DIM] for r in range(8)], axis=1)


def _lanes_to_heads(t):
    return jnp.concatenate([t[:, r * WINDOW:(r + 1) * WINDOW] for r in range(8)], axis=0)


def swa_fwd(name, qt, kpt, vpt, bias_t, sink_row):
    d, s = qt.shape
    ng, dh, sp = kpt.shape
    rows, cols = d // ng, SWA_QB * WINDOW
    assert (s // WINDOW) % SWA_QB == 0

    def body(q_ref, k_ref, v_ref, bias_ref, sink_ref, o_ref):
        for u in range(SWA_QB):
            i = pl.program_id(1) * SWA_QB + u
            lanes = slice(u * WINDOW, (u + 1) * WINDOW)
            qb = _heads_to_lanes(q_ref[:, lanes]) * Q_SCALE
            p, _, inv = _swa_probs(qb, k_ref[:, _band(i)], bias_ref[...], sink_ref[...], i)
            o_t = jnp.dot(v_ref[:, _band(i)], p.astype(BF16), preferred_element_type=F32) * inv
            o_ref[:, lanes] = _lanes_to_heads(o_t).astype(o_ref.dtype)

    qspec = pl.BlockSpec((rows, cols), lambda g, i: (g, i))
    kspec = pl.BlockSpec((None, dh, sp), lambda g, i: (g, 0, 0))
    return pl.pallas_call(
        body, name=name, grid=(ng, s // cols),
        in_specs=[qspec, kspec, kspec, pl.BlockSpec((None, 2 * WINDOW, 8 * WINDOW), lambda g, i: (g, 0, 0)),
                  pl.BlockSpec((None, 1, 8 * WINDOW), lambda g, i: (g, 0, 0))],
        out_specs=qspec,
        out_shape=jax.ShapeDtypeStruct(qt.shape, BF16),
        compiler_params=_params(("parallel", "arbitrary")),
    )(qt, kpt, vpt, bias_t, sink_row)


def swa_bwd(name, qt, kpt, vpt, bias_t, sink_row, dot_, dk_in, dv_in):
    d, s = qt.shape
    ng, dh, sp = kpt.shape
    rows, cols = d // ng, SWA_QB * WINDOW

    def body(q_ref, k_ref, v_ref, bias_ref, sink_ref, do_ref, dki_ref, dvi_ref, dq_ref, dk_ref, dv_ref, db_ref, ds_ref):
        @pl.when(pl.program_id(1) == 0)
        def _():
            dk_ref[...] = dki_ref[...]
            dv_ref[...] = dvi_ref[...]
            db_ref[...] = jnp.zeros_like(db_ref)
            ds_ref[...] = jnp.zeros_like(ds_ref)

        for u in range(SWA_QB):
            i = pl.program_id(1) * SWA_QB + u
            band = _band(i)
            lanes = slice(u * WINDOW, (u + 1) * WINDOW)
            qb = _heads_to_lanes(q_ref[:, lanes]) * Q_SCALE
            dob = _heads_to_lanes(do_ref[:, lanes])
            kt = k_ref[:, band]
            p, p_sink, inv = _swa_probs(qb, kt, bias_ref[...], sink_ref[...], i)
            p = p * inv
            dp = _contract0(v_ref[:, band], dob)
            delta = jnp.sum(p * dp, axis=0, keepdims=True)
            dsc = p * (dp - delta)
            ds_ref[...] -= p_sink * inv * delta
            db_ref[...] += dsc
            dscb = dsc.astype(BF16)
            dq_t = jnp.dot(kt, dscb, preferred_element_type=F32) * Q_SCALE
            dq_ref[:, lanes] = _lanes_to_heads(dq_t).astype(dq_ref.dtype)
            dk_ref[:, band] += _contract1(qb, dscb)
            dv_ref[:, band] += _contract1(dob, p.astype(BF16))

    qspec = pl.BlockSpec((rows, cols), lambda g, i: (g, i))
    kspec = pl.BlockSpec((None, dh, sp), lambda g, i: (g, 0, 0))
    bspec = pl.BlockSpec((None, 2 * WINDOW, 8 * WINDOW), lambda g, i: (g, 0, 0))
    sspec = pl.BlockSpec((None, 1, 8 * WINDOW), lambda g, i: (g, 0, 0))
    return pl.pallas_call(
        body, name=name, grid=(ng, s // cols),
        in_specs=[qspec, kspec, kspec, bspec, sspec, qspec, kspec, kspec],
        out_specs=[qspec, kspec, kspec, bspec, sspec],
        out_shape=[jax.ShapeDtypeStruct(qt.shape, BF16), jax.ShapeDtypeStruct(kpt.shape, F32),
                   jax.ShapeDtypeStruct(kpt.shape, F32), jax.ShapeDtypeStruct(bias_t.shape, F32),
                   jax.ShapeDtypeStruct(sink_row.shape, F32)],
        compiler_params=_params(("parallel", "arbitrary")),
    )(qt, kpt, vpt, bias_t, sink_row, dot_, dk_in, dv_in)


def _bucket_onehot():
    qi = np.arange(WINDOW)[:, None]
    kj = np.arange(2 * WINDOW)[None, :]
    n = np.maximum(qi + WINDOW - kj, 0)
    max_exact = N_BUCKETS // 2
    nf = np.maximum(n, 1).astype(np.float64)
    val = np.log(nf / max_exact) / math.log(WINDOW / max_exact) * (N_BUCKETS - max_exact)
    assert np.all(np.abs(val - np.round(val))[(n > max_exact) & (n < WINDOW)] > 1e-3)
    large = np.minimum(max_exact + val.astype(np.int64), N_BUCKETS - 1)
    bucket = np.where(n < max_exact, n, large).reshape(-1)
    onehot = np.zeros((128, bucket.size), np.float32)
    onehot[bucket, np.arange(bucket.size)] = 1.0
    return onehot


def _split3(x):
    a = x.astype(BF16)
    r = x - a.astype(F32)
    b = r.astype(BF16)
    c = (r - b.astype(F32)).astype(BF16)
    return a, b, c


def bias_table(rel_bias):
    nh = rel_bias.shape[1]
    oh = jnp.asarray(_bucket_onehot(), BF16)
    n = oh.shape[1]
    tn = 4096
    rb = jnp.zeros((nh, 128), F32).at[:, :N_BUCKETS].set(rel_bias.T)

    def body(rb_ref, oh_ref, o_ref):
        o_ref[...] = sum(jnp.dot(t, oh_ref[...], preferred_element_type=F32) for t in _split3(rb_ref[...]))

    return pl.pallas_call(
        body, name="bias_table", grid=(n // tn,),
        in_specs=[pl.BlockSpec((nh, 128), lambda i: (0, 0)), pl.BlockSpec((128, tn), lambda i: (0, i))],
        out_specs=pl.BlockSpec((nh, tn), lambda i: (0, i)),
        out_shape=jax.ShapeDtypeStruct((nh, n), F32),
        compiler_params=_params(("parallel",)),
    )(rb, oh)


def bias_table_grad(db0, db1):
    nh, n = db0.shape
    oh = jnp.asarray(_bucket_onehot(), BF16)
    tn = 4096

    def body(a_ref, b_ref, oh_ref, o_ref):
        @pl.when(pl.program_id(0) == 0)
        def _():
            o_ref[...] = jnp.zeros_like(o_ref)

        o_ref[...] += sum(lax.dot_general(t, oh_ref[...], (((1,), (1,)), ((), ())), preferred_element_type=F32)
                          for t in _split3(a_ref[...] + b_ref[...]))

    blk = pl.BlockSpec((nh, tn), lambda i: (0, i))
    return pl.pallas_call(
        body, name="bias_table_grad", grid=(n // tn,),
        in_specs=[blk, blk, pl.BlockSpec((128, tn), lambda i: (0, i))],
        out_specs=pl.BlockSpec((nh, 128), lambda i: (0, 0)),
        out_shape=jax.ShapeDtypeStruct((nh, 128), F32),
        compiler_params=_params(("arbitrary",)),
    )(db0, db1, oh)


def _owner_view(ref, name, d):
    if name == 'a_norm':
        return ref.at[d]
    if name in COL_SHARDED:
        n = ref.shape[2] // N_DEV
        return ref.at[:, :, pl.ds(pl.multiple_of(d * n, 128), n)]
    return ref.at[:, d]


def _place():
    return lax.axis_index("x"), lax.axis_index("y"), lax.axis_index("c")


def _dev(p):
    return 4 * p[0] + 2 * p[1] + p[2]


def _remote(src, dst, send_sem, recv_sem, to):
    return pltpu.make_async_remote_copy(src_ref=src, dst_ref=dst, send_sem=send_sem, recv_sem=recv_sem,
                                        device_id=to, device_id_type=MESH)


def _dma_sems(*shapes):
    return [pltpu.SemaphoreType.DMA(sh) for sh in shapes]


def comm_call(name, build, ins, out_shapes, sems, aliases=None):
    n_in, n_out = len(ins), len(out_shapes)

    def body(*refs):
        copies = build(refs[:n_in], refs[n_in:n_in + n_out], *refs[n_in + n_out:])
        for cp in copies:
            cp.start()
        for cp in copies:
            cp.wait()

    hbm = pl.BlockSpec(memory_space=pl.ANY)
    return pl.pallas_call(
        body, name=name, in_specs=[hbm] * n_in, out_specs=[hbm] * n_out, out_shape=list(out_shapes),
        scratch_shapes=sems, input_output_aliases=aliases or {},
    )(*ins)


def all_gather_weights(names, shards, full_shapes):
    n = len(names)

    def body(*refs):
        ins, outs = refs[:n], refs[n:2 * n]
        send_sems, recv_sems, local_sems = refs[2 * n:]
        x, y, c = _place()
        me, sibling = (x, y, c), (x, y, 1 - c)
        chips = [(1 - x, y), (x, 1 - y), (1 - x, 1 - y)]

        def copy(t, k, block, to, src=None):
            dst = _owner_view(outs[t], names[t], _dev(block))
            return _remote(dst if src is None else src, dst, send_sems.at[t, k], recv_sems.at[t, k], to)

        mine = [pltpu.make_async_copy(ins[t], _owner_view(outs[t], names[t], _dev(me)), local_sems.at[t])
                for t in range(n)]
        for cp in mine:
            cp.start()
        first = []
        for t in range(n):
            first.append(copy(t, 0, me, sibling, src=ins[t]))
            first += [copy(t, 1 + j, me, (*chip, c), src=ins[t]) for j, chip in enumerate(chips)]
        for cp in first:
            cp.start()
        passed = []
        for j, chip in enumerate(chips):
            for t in range(n):
                copy(t, 1 + j, (*chip, c), me).wait_recv()
                fwd = copy(t, 4 + j, (*chip, c), sibling)
                fwd.start()
                passed.append(fwd)
        for t in range(n):
            copy(t, 0, sibling, me).wait_recv()
            for j, chip in enumerate(chips):
                copy(t, 4 + j, (*chip, 1 - c), me).wait_recv()
        for cp in first + passed:
            cp.wait_send()
        for cp in mine:
            cp.wait()

    hbm = pl.BlockSpec(memory_space=pl.ANY)
    return pl.pallas_call(
        body, name="all_gather_layer0",
        in_specs=[hbm] * n, out_specs=[hbm] * n,
        out_shape=[jax.ShapeDtypeStruct(full_shapes[t], shards[t].dtype) for t in range(n)],
        scratch_shapes=_dma_sems((n, 7), (n, 7), (n,)),
    )(*shards)


def ag_direct(names, shards, full_shapes):
    n = len(names)

    def build(ins, outs, send_sems, recv_sems, local_sems):
        x, y, c = _place()
        peers = [(x, y, 1 - c), (1 - x, y, c), (x, 1 - y, c), (1 - x, 1 - y, c)]
        copies = []
        for t in range(n):
            dst = _owner_view(outs[t], names[t], _dev((x, y, c)))
            copies.append(pltpu.make_async_copy(ins[t], dst, local_sems.at[t]))
            copies += [_remote(ins[t], dst, send_sems.at[t, k], recv_sems.at[t, k], to) for k, to in enumerate(peers)]
        return copies

    return Carry(build, shards, [jax.ShapeDtypeStruct(full_shapes[t], shards[t].dtype) for t in range(n)],
                 _dma_sems((n, 4), (n, 4), (n,)))


def ag_forward(names, partial):
    n = len(names)

    def build(ins, outs, send_sems, recv_sems):
        del ins
        x, y, c = _place()
        copies = []
        for t in range(n):
            for k, chip in enumerate([(1 - x, y), (x, 1 - y), (1 - x, 1 - y)]):
                view = _owner_view(outs[t], names[t], _dev((*chip, c)))
                copies.append(_remote(view, view, send_sems.at[t, k], recv_sems.at[t, k], (x, y, 1 - c)))
        return copies

    return comm_call("all_gather_forward", build, partial, [jax.ShapeDtypeStruct(p.shape, p.dtype) for p in partial],
                     _dma_sems((n, 3), (n, 3)), aliases={t: t for t in range(n)})


def sibling_exchange(tag, names, grads, part_shapes):
    n = len(names)

    def build(ins, outs, send_sems, recv_sems):
        x, y, c = _place()
        return [_remote(_owner_view(ins[t], names[t], 2 * q + 1 - c), outs[t].at[q], send_sems.at[t, q],
                        recv_sems.at[t, q], (x, y, 1 - c)) for t in range(n) for q in range(4)]

    return comm_call(f"rs_sibling_exchange_{tag}", build, grads,
                     [jax.ShapeDtypeStruct((4,) + part_shapes[t], BF16) for t in range(n)], _dma_sems((n, 4), (n, 4)))


def chip_exchange(names, parts, part_shapes):
    n = len(names)

    def build(ins, outs, send_sems, recv_sems):
        x, y, c = _place()
        chips = [(1 - x, y), (x, 1 - y), (1 - x, 1 - y)]
        return [_remote(ins[t].at[2 * chip[0] + chip[1]], outs[t].at[k], send_sems.at[t, k], recv_sems.at[t, k],
                        (*chip, c)) for t in range(n) for k, chip in enumerate(chips)]

    return Carry(build, parts, [jax.ShapeDtypeStruct((3,) + part_shapes[t], BF16) for t in range(n)],
                 _dma_sems((n, 3), (n, 3)))


def all_gather_rows(x):
    r, w = x.shape

    def body(x_ref, out_ref, send_sems, recv_sems, local_sem):
        px, py, pc = _place()
        me = 4 * px + 2 * py + pc
        mine = pltpu.make_async_copy(x_ref, out_ref.at[me], local_sem)
        mine.start()
        copies = []
        for k in range(1, N_DEV):
            peer = (px ^ (k >> 2), py ^ ((k >> 1) & 1), pc ^ (k & 1))
            copies.append(pltpu.make_async_remote_copy(
                src_ref=x_ref, dst_ref=out_ref.at[me], send_sem=send_sems.at[k - 1], recv_sem=recv_sems.at[k - 1],
                device_id=peer, device_id_type=MESH))
        for cp in copies:
            cp.start()
        for k in range(1, N_DEV):
            peer_idx = me ^ k
            pltpu.make_async_remote_copy(
                src_ref=x_ref, dst_ref=out_ref.at[peer_idx], send_sem=send_sems.at[k - 1],
                recv_sem=recv_sems.at[k - 1], device_id=(px, py, pc), device_id_type=MESH).wait_recv()
        for cp in copies:
            cp.wait_send()
        mine.wait()

    vmem = pl.BlockSpec(memory_space=pltpu.VMEM)
    return pl.pallas_call(
        body, name="all_gather_small_grads",
        in_specs=[vmem], out_specs=vmem,
        out_shape=jax.ShapeDtypeStruct((N_DEV, r, w), x.dtype),
        scratch_shapes=[pltpu.SemaphoreType.DMA((N_DEV - 1,)), pltpu.SemaphoreType.DMA((N_DEV - 1,)),
                        pltpu.SemaphoreType.DMA],
    )(x)


def _adamw(w, g, m, v):
    m = ADAM_B1 * m + (1.0 - ADAM_B1) * g
    v = ADAM_B2 * v + (1.0 - ADAM_B2) * (g * g)
    m_hat = m / (1.0 - ADAM_B1 ** ADAM_STEP)
    v_hat = v / (1.0 - ADAM_B2 ** ADAM_STEP)
    return -ADAM_LR * (m_hat / (jnp.sqrt(v_hat) + ADAM_EPS) + ADAM_WD * w), m, v


def sibling_sum(name, col, grads, recv, core):
    _, nl, rows, cols = recv.shape
    tr = _tile(rows, 512)
    rspec = pl.BlockSpec((None, None, tr, cols), lambda q, l, i, c_ref: (q, l, i, 0))
    if col:
        gspec = pl.BlockSpec((None, tr, cols), lambda q, l, i, c_ref: (l, i, 2 * q + c_ref[0]))
    else:
        gspec = pl.BlockSpec((None, None, tr, cols), lambda q, l, i, c_ref: (l, 2 * q + c_ref[0], i, 0))

    def body(c_ref, g_ref, r_ref, o_ref):
        del c_ref
        o_ref[...] = (g_ref[...].astype(F32) + r_ref[...].astype(F32)).astype(BF16)

    return pl.pallas_call(
        body, name=name,
        grid_spec=pltpu.PrefetchScalarGridSpec(num_scalar_prefetch=1, grid=(4, nl, rows // tr),
                                               in_specs=[gspec, rspec], out_specs=rspec),
        out_shape=jax.ShapeDtypeStruct(recv.shape, BF16),
        compiler_params=_params(("parallel", "parallel", "parallel")),
    )(core.reshape(1), grads, recv)


def reduce_adamw(name, parts, recv, chip, w, m, v, l0, prev):
    _, nl, rows, cols = parts.shape
    tr = _tile(rows, 256)

    def body(q_ref, p_ref, r_ref, w_ref, m_ref, v_ref, *rest):
        del q_ref
        g_out, d_out, m_out, v_out = rest[-4:]
        g = ((p_ref[...].astype(F32) + r_ref[0].astype(F32)) + r_ref[1].astype(F32)) + r_ref[2].astype(F32)
        d, mn, vn = _adamw(w_ref[...], g, m_ref[...], v_ref[...])
        g_out[...] = g
        d_out[...] = d
        m_out[...] = mn
        v_out[...] = vn

    blk = pl.BlockSpec((None, tr, cols), lambda l, i, q_ref: (l0 + l, i, 0))
    prev = list(prev) if prev else []
    return pl.pallas_call(
        body, name=name,
        grid_spec=pltpu.PrefetchScalarGridSpec(
            num_scalar_prefetch=1, grid=(nl, rows // tr),
            in_specs=[pl.BlockSpec((None, None, tr, cols), lambda l, i, q_ref: (q_ref[0], l, i, 0)),
                      pl.BlockSpec((3, None, tr, cols), lambda l, i, q_ref: (0, l, i, 0)), blk, blk, blk]
            + [pl.BlockSpec(memory_space=pl.ANY)] * len(prev),
            out_specs=[blk] * 4),
        out_shape=[jax.ShapeDtypeStruct(w.shape, F32)] * 4,
        input_output_aliases={6 + i: i for i in range(len(prev))},
        compiler_params=_params(("parallel", "parallel")),
    )(chip.reshape(1), parts, recv, w, m, v, *prev)


def small_adamw(name, gathered, w, m, v):
    _, r, c = gathered.shape

    def body(ga_ref, w_ref, m_ref, v_ref, g_out, d_out, m_out, v_out):
        g = ga_ref[0]
        for d in range(1, N_DEV):
            g = g + ga_ref[d]
        dl, mn, vn = _adamw(w_ref[...], g, m_ref[...], v_ref[...])
        g_out[...] = g
        d_out[...] = dl
        m_out[...] = mn
        v_out[...] = vn

    return pl.pallas_call(
        body, name=name,
        out_shape=[jax.ShapeDtypeStruct((r, c), F32)] * 4,
        compiler_params=_params(),
    )(gathered, w, m, v)


def _add(acc, *ex):
    return (acc + ex[0],)


def local_step(x, target, small, ex):
    s, d = x.shape
    n_a, n_b = small['a_norm'].shape[0], small['b_norm'].shape[0]
    sg = {}
    gb = {}

    def fwd_mm(name, a, wname, layer, epilogue, extras, out_dtypes):
        return mm_nn(name, a, *ex.weight(wname, layer), epilogue, extras, out_dtypes)

    def dx_mm(name, dy, wname, layer, epilogue, extras, out_dtypes):
        return mm_nt(name, dy, *ex.weight(wname, layer), epilogue, extras, out_dtypes)

    def dw_mm(name, a, dy, wname, layer):
        key, slab, shape = ex.grad(wname, layer)
        gb[key] = mm_tn(name, a, dy, gb.get(key), shape, slab)

    bias_flat = bias_table(small['rel_bias'])
    bias_t = bias_flat.reshape(2, 8, WINDOW, 2 * WINDOW).transpose(0, 3, 1, 2).reshape(2, 2 * WINDOW, 8 * WINDOW)
    sink_rows = [jnp.repeat(small['b_sinks'][j], WINDOW).reshape(2, 1, 8 * WINDOW) for j in range(n_b)]

    def mlp_fwd(h, layer):
        n2 = rms_fwd(f"mlp_norm_fwd{layer}", h, small['mlp_norm'][layer])
        u, a = fwd_mm(f"mlp_up_fwd{layer}", n2, 'mlp_up', layer,
                      lambda acc: (acc, jnp.square(jnp.maximum(acc, 0.0))), (), (BF16, BF16))
        (h2,) = fwd_mm(f"mlp_down_fwd{layer}", a, 'mlp_down', layer, _add, (h,), (F32,))
        return h2, (n2, u, a)

    h = x
    saved = []
    for l in range(n_a):
        n1 = rms_fwd(f"a_norm_fwd{l}", h, small['a_norm'][l])
        (qkv,) = fwd_mm(f"a_qkv_fwd{l}", n1, 'a_wqkv', l, lambda acc: (acc,), (), (BF16,))
        qkvt = qkv.T.reshape(3 * d // HEAD_DIM, HEAD_DIM, s)
        o_t, rtab, carried = sb_fwd(f"sb_fwd{l}", qkvt, ex.fwd_carry(l))
        ex.fwd_done(l, carried)
        o = o_t.reshape(d, s).T
        (h_mid,) = fwd_mm(f"a_wo_fwd{l}", o, 'a_wo', l, _add, (h,), (F32,))
        h_out, mlp_saved = mlp_fwd(h_mid, l)
        saved.append((h, n1, qkvt, o, rtab, h_mid, mlp_saved))
        h = h_out
    h_kv = h
    nkv = rms_fwd("kv_norm_fwd", h, small['kv_norm'])
    (kv,) = fwd_mm("kv_fwd", nkv, 'w_kv', 0, lambda acc, b: (acc + b,), (small['b_kv'].reshape(1, -1),), (BF16,))
    kvt = kv.T.reshape(2, 2, HEAD_DIM, s)
    kpt, vpt = (jnp.pad(t, ((0, 0), (0, 0), (WINDOW, 0))) for t in (kvt[0], kvt[1]))
    for j in range(n_b):
        layer = n_a + j
        n1 = rms_fwd(f"b_norm_fwd{j}", h, small['b_norm'][j])
        (qb,) = fwd_mm(f"b_q_fwd{j}", n1, 'b_wq', j, lambda acc, b: (acc + b,),
                       (small['b_bq'][j].reshape(1, -1),), (BF16,))
        qbt = qb.T
        o = swa_fwd(f"swa_fwd{j}", qbt, kpt, vpt, bias_t, sink_rows[j]).T
        (h_mid,) = fwd_mm(f"b_wo_fwd{j}", o, 'b_wo', j, lambda acc, hh, b: (acc + hh + b,),
                          (h, small['b_bo'][j].reshape(1, -1)), (F32,))
        h_out, mlp_saved = mlp_fwd(h_mid, layer)
        saved.append((h, n1, qbt, o, h_mid, mlp_saved))
        h = h_out

    dh, dhb, dg_final, loss_b = loss_head(h, small['final_norm'], target)
    sg['final_norm'] = dg_final[0]
    sg['mlp_norm'] = [None] * (n_a + n_b)

    def mlp_bwd(dh, dhb, h_mid, mlp_saved, layer):
        n2, u, a = mlp_saved
        (du,) = dx_mm(f"mlp_down_dx{layer}", dhb, 'mlp_down', layer,
                      lambda acc, uu: (acc * (2.0 * jnp.maximum(uu.astype(F32), 0.0)),), (u,), (BF16,))
        dw_mm(f"mlp_down_dw{layer}", a, dhb, 'mlp_down', layer)
        (dn2,) = dx_mm(f"mlp_up_dx{layer}", du, 'mlp_up', layer, lambda acc: (acc,), (), (F32,))
        dw_mm(f"mlp_up_dw{layer}", n2, du, 'mlp_up', layer)
        dh2, dh2b, dg, cs = rms_bwd(f"mlp_norm_bwd{layer}", h_mid, small['mlp_norm'][layer], dn2, dh)
        sg['mlp_norm'][layer] = dg[0]
        return dh2, dh2b, cs

    dkp = jnp.zeros(kpt.shape, F32)
    dvp = jnp.zeros(vpt.shape, F32)
    sg['b_norm'], sg['b_bq'], sg['b_bo'], sg['b_sinks'] = [None] * n_b, [None] * n_b, [None] * n_b, [None] * n_b
    dbias = [None] * n_b
    for j in reversed(range(n_b)):
        layer = n_a + j
        h_in, n1, qbt, o, h_mid, mlp_saved = saved[layer]
        dh, dhb, cs = mlp_bwd(dh, dhb, h_mid, mlp_saved, layer)
        sg['b_bo'][j] = cs[0]
        (do,) = dx_mm(f"b_wo_dx{j}", dhb, 'b_wo', j, lambda acc: (acc,), (), (BF16,))
        dw_mm(f"b_wo_dw{j}", o, dhb, 'b_wo', j)
        dq_t, dkp, dvp, dbias[j], dsink = swa_bwd(f"swa_bwd{j}", qbt, kpt, vpt, bias_t, sink_rows[j], do.T, dkp, dvp)
        sg['b_sinks'][j] = colsum(f"sink_grad{j}", dsink.reshape(16, WINDOW).T)[0]
        dq = dq_t.T
        sg['b_bq'][j] = colsum(f"b_bq_grad{j}", dq)[0]
        (dn1,) = dx_mm(f"b_q_dx{j}", dq, 'b_wq', j, lambda acc: (acc,), (), (F32,))
        dw_mm(f"b_q_dw{j}", n1, dq, 'b_wq', j)
        dh, dhb, dg, _ = rms_bwd(f"b_norm_bwd{j}", h_in, small['b_norm'][j], dn1, dh)
        sg['b_norm'][j] = dg[0]
    unt = lambda t: t.reshape(2, 2 * WINDOW, 8, WINDOW).transpose(0, 2, 3, 1).reshape(bias_flat.shape)
    sg['rel_bias'] = bias_table_grad(unt(dbias[0]), unt(dbias[1]))[:, :N_BUCKETS].T

    dkv = jnp.concatenate([dkp[:, :, WINDOW:], dvp[:, :, WINDOW:]], axis=0).reshape(-1, s).T
    sg['b_kv'] = colsum("b_kv_grad", dkv)[0]
    dkvb = dkv.astype(BF16)
    (dnkv,) = dx_mm("kv_dx", dkvb, 'w_kv', 0, lambda acc: (acc,), (), (F32,))
    dw_mm("kv_dw", nkv, dkvb, 'w_kv', 0)
    dh, dhb, dg, _ = rms_bwd("kv_norm_bwd", h_kv, small['kv_norm'], dnkv, dh)
    sg['kv_norm'] = dg[0]

    sg['a_norm'] = [None] * n_a
    for l in reversed(range(n_a)):
        h_in, n1, qkvt, o, rtab, h_mid, mlp_saved = saved[l]
        dh, dhb, _ = mlp_bwd(dh, dhb, h_mid, mlp_saved, l)
        (do,) = dx_mm(f"a_wo_dx{l}", dhb, 'a_wo', l, lambda acc: (acc,), (), (BF16,))
        dw_mm(f"a_wo_dw{l}", o, dhb, 'a_wo', l)
        dq_t, dk_t, dv_t, carried = sb_bwd(f"sb_bwd{l}", qkvt, do.T.reshape(d // HEAD_DIM, HEAD_DIM, s), rtab,
                                           ex.bwd_carry(l, gb))
        ex.bwd_done(l, carried)
        dqkv = jnp.concatenate([dq_t, dk_t, dv_t], axis=0).reshape(3 * d, s).T
        (dn1,) = dx_mm(f"a_qkv_dx{l}", dqkv, 'a_wqkv', l, lambda acc: (acc,), (), (F32,))
        dw_mm(f"a_qkv_dw{l}", n1, dqkv, 'a_wqkv', l)
        dh, dhb, dg, _ = rms_bwd(f"a_norm_bwd{l}", h_in, small['a_norm'][l], dn1, dh)
        sg['a_norm'][l] = dg[0]

    small_grads = {
        'a_norm': jnp.stack(sg['a_norm']), 'kv_norm': sg['kv_norm'], 'b_kv': sg['b_kv'],
        'b_norm': jnp.stack(sg['b_norm']), 'b_bq': jnp.stack(sg['b_bq']), 'b_sinks': jnp.stack(sg['b_sinks']),
        'b_bo': jnp.stack(sg['b_bo']), 'rel_bias': sg['rel_bias'], 'mlp_norm': jnp.stack(sg['mlp_norm']),
        'final_norm': sg['final_norm'],
    }
    return loss_b, dh, gb, small_grads


def _full_shape(name, shard_shape):
    if name in COL_SHARDED:
        return shard_shape[:2] + (N_DEV * shard_shape[2],)
    nl, r, n = shard_shape
    return (nl, N_DEV, r, n)


def _as_w3(name, full):
    if name in COL_SHARDED:
        return full
    nl, nd, r, n = full.shape
    return full.reshape(nl, nd * r, n)


LAYERED = ('a_wqkv', 'a_wo', 'mlp_up', 'mlp_down')
RS_GROUPS = {
    'A': (('mlp_up', 1, 3), ('mlp_down', 1, 3), ('b_wq', 0, 2), ('b_wo', 0, 2), ('w_kv', 0, 1)),
    'B': (('a_wqkv', 1, 1), ('a_wo', 1, 1), ('mlp_up', 0, 1), ('mlp_down', 0, 1)),
    'C': (('a_wqkv', 0, 1), ('a_wo', 0, 1)),
}


class _Exchanges:
    def __init__(self, full0, shards1, core, chip, w3, m3, v3):
        self.wbuf = {0: {n: _as_w3(n, full0[n]) for n in LAYERED}}
        self.shards1, self.core, self.chip = shards1, core, chip
        self.w3, self.m3, self.v3 = w3, m3, v3
        self.shard_dims = {n: w3[n].shape[1:] for n in BIG}
        self.parts = {}
        self.out = {}

    def weight(self, name, layer):
        if name in LAYERED:
            return (self.wbuf[0][name], 0) if layer == 0 else (self.wbuf[1][name], layer - 1)
        return self.wbuf[1][name], layer

    def fwd_carry(self, layer):
        if layer != 0:
            return None
        shards = [self.shards1[n] for n in BIG]
        return ag_direct(BIG, shards, [_full_shape(n, sh.shape) for n, sh in zip(BIG, shards)])

    def fwd_done(self, layer, carried):
        if layer == 0:
            self.wbuf[1] = {n: _as_w3(n, f) for n, f in zip(BIG, ag_forward(BIG, list(carried)))}

    def grad(self, name, layer):
        for group, members in RS_GROUPS.items():
            for n, l0, nl in members:
                if n == name and l0 <= layer < l0 + nl:
                    k_full, n_full = self.wbuf[0 if name in LAYERED else 1][name].shape[1:]
                    return (group, name), layer - l0, (nl, k_full, n_full)
        raise KeyError((name, layer))

    def _sibling_stage(self, group, gb):
        names = [n for n, _, _ in RS_GROUPS[group]]
        shapes = [(nl,) + self.shard_dims[n] for n, _, nl in RS_GROUPS[group]]
        gfull = [gb[(group, n)].reshape(_full_shape(n, sh)) for n, sh in zip(names, shapes)]
        recv = sibling_exchange(group, names, gfull, shapes)
        self.parts[group] = [sibling_sum(f"rs_sibling_sum_{group}_{n}", n in COL_SHARDED, g, r, self.core)
                             for n, g, r in zip(names, gfull, recv)]
        return names, shapes

    def bwd_carry(self, layer, gb):
        names, shapes = self._sibling_stage('A' if layer == 1 else 'B', gb)
        return chip_exchange(names, self.parts['A' if layer == 1 else 'B'], shapes)

    def bwd_done(self, layer, carried):
        self._adamw('A' if layer == 1 else 'B', carried)

    def finish(self, gb):
        names, shapes = self._sibling_stage('C', gb)
        ce = chip_exchange(names, self.parts['C'], shapes)
        self._adamw('C', comm_call("rs_chip_exchange_C", ce.build, ce.ins, ce.out_shapes, ce.sems))
        return self.out

    def _adamw(self, group, recv2):
        for (n, l0, _), p, r in zip(RS_GROUPS[group], self.parts[group], recv2):
            self.out[n] = reduce_adamw(f"adamw_{group}_{n}", p, r, self.chip, self.w3[n], self.m3[n], self.v3[n],
                                       l0, self.out.get(n))


def _pack_small(vals):
    flat = jnp.concatenate([vals[n].reshape(-1).astype(F32) for n in SMALL] + [vals['loss'].reshape(-1)])
    rows = -(-flat.shape[0] // 1024) * 8
    return jnp.pad(flat, (0, rows * 128 - flat.shape[0])).reshape(rows, 128)


def _unpack_small(packed, shapes):
    flat = packed.reshape(-1)
    out, off = {}, 0
    for n in SMALL + ['loss']:
        size = int(np.prod(shapes[n]))
        out[n] = flat[off:off + size].reshape(shapes[n])
        off += size
    return out


def kernel(x, a_norm, a_wqkv, a_wo, kv_norm, w_kv, b_kv, b_norm, b_wq, b_bq, b_sinks, b_wo, b_bo, rel_bias, mlp_norm, mlp_up, mlp_down, final_norm, loss_target, m_a_norm, m_a_wqkv, m_a_wo, m_kv_norm, m_w_kv, m_b_kv, m_b_norm, m_b_wq, m_b_bq, m_b_sinks, m_b_wo, m_b_bo, m_rel_bias, m_mlp_norm, m_mlp_up, m_mlp_down, m_final_norm, v_a_norm, v_a_wqkv, v_a_wo, v_kv_norm, v_w_kv, v_b_kv, v_b_norm, v_b_wq, v_b_bq, v_b_sinks, v_b_wo, v_b_bo, v_rel_bias, v_mlp_norm, v_mlp_up, v_mlp_down, v_final_norm):
    w = dict(a_norm=a_norm, a_wqkv=a_wqkv, a_wo=a_wo, kv_norm=kv_norm, w_kv=w_kv, b_kv=b_kv, b_norm=b_norm,
             b_wq=b_wq, b_bq=b_bq, b_sinks=b_sinks, b_wo=b_wo, b_bo=b_bo, rel_bias=rel_bias, mlp_norm=mlp_norm,
             mlp_up=mlp_up, mlp_down=mlp_down, final_norm=final_norm)
    m = dict(a_norm=m_a_norm, a_wqkv=m_a_wqkv, a_wo=m_a_wo, kv_norm=m_kv_norm, w_kv=m_w_kv, b_kv=m_b_kv,
             b_norm=m_b_norm, b_wq=m_b_wq, b_bq=m_b_bq, b_sinks=m_b_sinks, b_wo=m_b_wo, b_bo=m_b_bo,
             rel_bias=m_rel_bias, mlp_norm=m_mlp_norm, mlp_up=m_mlp_up, mlp_down=m_mlp_down, final_norm=m_final_norm)
    v = dict(a_norm=v_a_norm, a_wqkv=v_a_wqkv, a_wo=v_a_wo, kv_norm=v_kv_norm, w_kv=v_w_kv, b_kv=v_b_kv,
             b_norm=v_b_norm, b_wq=v_b_wq, b_bq=v_b_bq, b_sinks=v_b_sinks, b_wo=v_b_wo, b_bo=v_b_bo,
             rel_bias=v_rel_bias, mlp_norm=v_mlp_norm, mlp_up=v_mlp_up, mlp_down=v_mlp_down, final_norm=v_final_norm)
    px, py, pc = _place()
    me = 4 * px + 2 * py + pc
    chip = (2 * px + py).astype(jnp.int32)
    core = pc.astype(jnp.int32)

    as3 = lambda t: t[None] if t.ndim == 2 else t
    w3, m3, v3 = ({n: as3(src[n]) for n in BIG} for src in (w, m, v))
    shards0 = {n: w3[n][:1].astype(BF16) for n in LAYERED}
    shards1 = {n: (w3[n][1:] if n in LAYERED else w3[n]).astype(BF16) for n in BIG}
    an_pad = jnp.zeros((8, 128), F32).at[:a_norm.shape[0]].set(a_norm)
    names0 = list(LAYERED) + ['a_norm']
    full0 = all_gather_weights(names0, [shards0[n] for n in LAYERED] + [an_pad],
                               [_full_shape(n, shards0[n].shape) for n in LAYERED] + [(N_DEV, 8, 128)])
    full0 = dict(zip(names0, full0))
    n_a = a_norm.shape[0]
    small = {n: w[n] for n in SMALL}
    small['a_norm'] = full0['a_norm'][:, :n_a].transpose(1, 0, 2).reshape(n_a, -1)

    ex = _Exchanges(full0, shards1, core, chip, w3, m3, v3)
    loss_b, grad_x, gb, sgrads = local_step(x[0], loss_target[0], small, ex)
    out = {n: [t.reshape(w[n].shape) for t in bufs] for n, bufs in ex.finish(gb).items()}

    sgrads['loss'] = loss_b[0, :1]
    gathered = all_gather_rows(_pack_small(sgrads))
    shapes = {n: w[n].shape for n in SMALL}
    shapes['a_norm'] = (n_a, a_norm.shape[1] * N_DEV)
    shapes['loss'] = (1,)
    zeros1 = jnp.zeros((1,), F32)

    def packed(src):
        vals = {n: src[n] for n in SMALL}
        vals['a_norm'] = jnp.zeros(shapes['a_norm'], F32)
        vals['loss'] = zeros1
        return _pack_small(vals)

    sm = small_adamw("adamw_small", gathered, packed(w), packed(m), packed(v))
    sm = [_unpack_small(t, shapes) for t in sm]
    g_an = lax.dynamic_slice_in_dim(sm[0]['a_norm'], me * a_norm.shape[1], a_norm.shape[1], axis=1)
    pad = lambda t: jnp.zeros((8, 128), F32).at[:n_a].set(t)
    gathered_an = jnp.zeros((N_DEV, 8, 128), F32).at[0].set(pad(g_an))
    an = small_adamw("adamw_a_norm", gathered_an, pad(a_norm), pad(m_a_norm), pad(v_a_norm))
    for i in range(4):
        sm[i]['a_norm'] = an[i][:n_a]
    for n in BIG:
        for i in range(4):
            sm[i][n] = out[n][i]
    loss = sm[0]['loss'][0]
    return (loss, grad_x[None], *[sm[0][n] for n in WEIGHTS], *[sm[1][n] for n in WEIGHTS],
            *[sm[2][n] for n in WEIGHTS], *[sm[3][n] for n in WEIGHTS])
```

```python
import functools
import math

import numpy as np
import jax
import jax.numpy as jnp
from jax import lax
from jax.experimental import pallas as pl
from jax.experimental.pallas import tpu as pltpu

F32 = jnp.float32
BF16 = jnp.bfloat16
MESH = pl.DeviceIdType.MESH

N_DEV = 8
HEAD_DIM = 64
WINDOW = 128
N_BUCKETS = 32
EPS = 1e-5
NEG_INF = -1e30
Q_SCALE = 1.0 / math.sqrt(HEAD_DIM)
LOG2E = 1.4426950408889634

ADAM_LR, ADAM_B1, ADAM_B2, ADAM_EPS, ADAM_WD, ADAM_STEP = 0.001, 0.9, 0.999, 1e-08, 0.01, 10

SB_BQ = 512
SB_BK = 128
SB_DEAD = 160.0
SB_UNSEEN = 1e30
ROW_TILE = 512
VMEM_LIMIT = 56 * 1024 * 1024

WEIGHTS = ['a_norm', 'a_wqkv', 'a_wo', 'kv_norm', 'w_kv', 'b_kv', 'b_norm', 'b_wq', 'b_bq', 'b_sinks', 'b_wo',
           'b_bo', 'rel_bias', 'mlp_norm', 'mlp_up', 'mlp_down', 'final_norm']
BIG = ['a_wqkv', 'a_wo', 'w_kv', 'b_wq', 'b_wo', 'mlp_up', 'mlp_down']
COL_SHARDED = ('a_wqkv', 'mlp_up')
SMALL = ['a_norm', 'kv_norm', 'b_kv', 'b_norm', 'b_bq', 'b_sinks', 'b_bo', 'rel_bias', 'mlp_norm', 'final_norm']


def _params(sem=None):
    return pltpu.CompilerParams(dimension_semantics=sem, vmem_limit_bytes=VMEM_LIMIT)


def _pick(n, cands):
    for c in cands:
        if n % c == 0:
            return c
    raise ValueError(n)


def _tile(n, want):
    return n if n <= want else _pick(n, (want, want // 2, want // 4))


def mm_nn(name, a, w3, layer, epilogue, extras, out_dtypes):
    m, k = a.shape
    _, kw, n = w3.shape
    assert kw == k
    tm = _tile(m, 1024 if k <= 1024 else 512)
    tn = _tile(n, 1024)
    ne, no = len(extras), len(out_dtypes)

    def body(a_ref, w_ref, *rest):
        ex, outs = rest[:ne], rest[ne:ne + no]
        res = epilogue(jnp.dot(a_ref[...], w_ref[...], preferred_element_type=F32), *[e[...] for e in ex])
        for o, r in zip(outs, res):
            o[...] = r.astype(o.dtype)

    tile = pl.BlockSpec((tm, tn), lambda i, j: (i, j))
    ex_specs = [tile if e.shape[0] == m else pl.BlockSpec((1, tn), lambda i, j: (0, j)) for e in extras]
    return pl.pallas_call(
        body, name=name, grid=(m // tm, n // tn),
        in_specs=[pl.BlockSpec((tm, k), lambda i, j: (i, 0)),
                  pl.BlockSpec((None, k, tn), lambda i, j: (layer, 0, j))] + ex_specs,
        out_specs=[tile] * no,
        out_shape=[jax.ShapeDtypeStruct((m, n), d) for d in out_dtypes],
        compiler_params=_params(("parallel", "parallel")),
    )(a, w3, *extras)


def mm_nt(name, dy, w3, layer, epilogue, extras, out_dtypes):
    m, n = dy.shape
    _, k, nw = w3.shape
    assert nw == n
    tm = _tile(m, 1024 if n <= 1024 else 512)
    tko = _tile(k, 1024)
    ne, no = len(extras), len(out_dtypes)

    def body(a_ref, w_ref, *rest):
        ex, outs = rest[:ne], rest[ne:ne + no]
        acc = lax.dot_general(a_ref[...], w_ref[...], (((1,), (1,)), ((), ())), preferred_element_type=F32)
        res = epilogue(acc, *[e[...] for e in ex])
        for o, v in zip(outs, res):
            o[...] = v.astype(o.dtype)

    tile = pl.BlockSpec((tm, tko), lambda i, ko: (i, ko))
    return pl.pallas_call(
        body, name=name, grid=(m // tm, k // tko),
        in_specs=[pl.BlockSpec((tm, n), lambda i, ko: (i, 0)),
                  pl.BlockSpec((None, tko, n), lambda i, ko: (layer, ko, 0))] + [tile] * ne,
        out_specs=[tile] * no,
        out_shape=[jax.ShapeDtypeStruct((m, k), d) for d in out_dtypes],
        compiler_params=_params(("parallel", "parallel")),
    )(dy, w3, *extras)


def mm_tn(name, x, dy, gbuf, shape, layer):
    s, k = x.shape
    _, kw, n = shape
    assert kw == k and dy.shape == (s, n)
    tkk = _tile(k, 512)
    tn = _tile(n, 1024)

    def body(x_ref, dy_ref, *rest):
        g_out = rest[-1]
        g_out[...] = lax.dot_general(x_ref[...], dy_ref[...], (((0,), (0,)), ((), ())),
                                     preferred_element_type=F32).astype(g_out.dtype)

    prev = [] if gbuf is None else [gbuf]
    return pl.pallas_call(
        body, name=name, grid=(k // tkk, n // tn),
        in_specs=[pl.BlockSpec((s, tkk), lambda ki, j: (0, ki)),
                  pl.BlockSpec((s, tn), lambda ki, j: (0, j))] + [pl.BlockSpec(memory_space=pl.ANY)] * len(prev),
        out_specs=pl.BlockSpec((None, tkk, tn), lambda ki, j: (layer, ki, j)),
        out_shape=jax.ShapeDtypeStruct(shape, BF16),
        input_output_aliases={2: 0} if prev else {},
        compiler_params=_params(("parallel", "parallel")),
    )(x, dy, *prev)


def rms_fwd(name, h, g):
    s, d = h.shape
    tr = _pick(s, (ROW_TILE, 256, 128))

    def body(h_ref, g_ref, o_ref):
        x = h_ref[...]
        r = lax.rsqrt(jnp.mean(x * x, axis=-1, keepdims=True) + EPS)
        o_ref[...] = (x * r * g_ref[...]).astype(o_ref.dtype)

    return pl.pallas_call(
        body, name=name, grid=(s // tr,),
        in_specs=[pl.BlockSpec((tr, d), lambda i: (i, 0)), pl.BlockSpec((1, d), lambda i: (0, 0))],
        out_specs=pl.BlockSpec((tr, d), lambda i: (i, 0)),
        out_shape=jax.ShapeDtypeStruct((s, d), BF16),
        compiler_params=_params(("parallel",)),
    )(h, g.reshape(1, d))


def rms_bwd(name, h, g, dn, dres):
    s, d = h.shape
    tr = _pick(s, (ROW_TILE, 256, 128))

    def body(h_ref, g_ref, dn_ref, dres_ref, dx_ref, dxb_ref, dg_ref, cs_ref):
        i = pl.program_id(0)
        x = h_ref[...]
        r = lax.rsqrt(jnp.mean(x * x, axis=-1, keepdims=True) + EPS)
        xh = x * r
        dn_ = dn_ref[...]
        dyg = dn_ * g_ref[...]
        dx = dres_ref[...] + r * (dyg - xh * jnp.mean(dyg * xh, axis=-1, keepdims=True))
        dx_ref[...] = dx
        dxb_ref[...] = dx.astype(BF16)

        @pl.when(i == 0)
        def _():
            dg_ref[...] = jnp.zeros_like(dg_ref)
            cs_ref[...] = jnp.zeros_like(cs_ref)

        dg_ref[...] += jnp.sum(dn_ * xh, axis=0, keepdims=True)
        cs_ref[...] += jnp.sum(dx, axis=0, keepdims=True)

    row = pl.BlockSpec((tr, d), lambda i: (i, 0))
    vec = pl.BlockSpec((1, d), lambda i: (0, 0))
    return pl.pallas_call(
        body, name=name, grid=(s // tr,),
        in_specs=[row, vec, row, row],
        out_specs=[row, row, vec, vec],
        out_shape=[jax.ShapeDtypeStruct((s, d), F32), jax.ShapeDtypeStruct((s, d), BF16),
                   jax.ShapeDtypeStruct((1, d), F32), jax.ShapeDtypeStruct((1, d), F32)],
        compiler_params=_params(("arbitrary",)),
    )(h, g.reshape(1, d), dn, dres)


def loss_head(h, g, target):
    s, d = h.shape
    tr = _pick(s, (ROW_TILE, 256, 128))

    def body(h_ref, g_ref, t_ref, dx_ref, dxb_ref, dg_ref, loss_ref):
        i = pl.program_id(0)
        x = h_ref[...]
        r = lax.rsqrt(jnp.mean(x * x, axis=-1, keepdims=True) + EPS)
        xh = x * r
        gw = g_ref[...]
        err = xh * gw - t_ref[...]
        dn_ = err * (1.0 / d)
        dyg = dn_ * gw
        dx = r * (dyg - xh * jnp.mean(dyg * xh, axis=-1, keepdims=True))
        dx_ref[...] = dx
        dxb_ref[...] = dx.astype(BF16)

        @pl.when(i == 0)
        def _():
            dg_ref[...] = jnp.zeros_like(dg_ref)
            loss_ref[...] = jnp.zeros_like(loss_ref)

        dg_ref[...] += jnp.sum(dn_ * xh, axis=0, keepdims=True)
        per_row = jnp.sum(err * err, axis=-1, keepdims=True) * (0.5 / d)
        loss_ref[...] += jnp.broadcast_to(jnp.sum(per_row, axis=0, keepdims=True), loss_ref.shape)

    row = pl.BlockSpec((tr, d), lambda i: (i, 0))
    vec = pl.BlockSpec((1, d), lambda i: (0, 0))
    return pl.pallas_call(
        body, name="loss_head", grid=(s // tr,),
        in_specs=[row, vec, row],
        out_specs=[row, row, vec, pl.BlockSpec((1, 128), lambda i: (0, 0))],
        out_shape=[jax.ShapeDtypeStruct((s, d), F32), jax.ShapeDtypeStruct((s, d), BF16),
                   jax.ShapeDtypeStruct((1, d), F32), jax.ShapeDtypeStruct((1, 128), F32)],
        compiler_params=_params(("arbitrary",)),
    )(h, g.reshape(1, d), target)


def colsum(name, x):
    s, n = x.shape
    tr = _pick(s, (ROW_TILE, 256, 128))

    def body(x_ref, o_ref):
        @pl.when(pl.program_id(0) == 0)
        def _():
            o_ref[...] = jnp.zeros_like(o_ref)

        o_ref[...] += jnp.sum(x_ref[...].astype(F32), axis=0, keepdims=True)

    return pl.pallas_call(
        body, name=name, grid=(s // tr,),
        in_specs=[pl.BlockSpec((tr, n), lambda i: (i, 0))],
        out_specs=pl.BlockSpec((1, n), lambda i: (0, 0)),
        out_shape=jax.ShapeDtypeStruct((1, n), F32),
        compiler_params=_params(("arbitrary",)),
    )(x)


def _tri_rows(reverse):
    i = np.arange(SB_BK)
    tri = (i[None, :] >= i[:, None]) if reverse else (i[None, :] <= i[:, None])
    tri = np.concatenate([tri, tri], axis=1)
    return jnp.asarray(np.concatenate([tri, np.ones((8, 2 * SB_BK), bool)], axis=0), BF16)


def _hi_lo_rows(x):
    hi = x.astype(BF16)
    lo = (x - hi.astype(F32)).astype(BF16)
    return jnp.concatenate([hi, lo], axis=0)


def _softplus2(zs):
    neg_abs = lax.bitcast_convert_type(lax.bitcast_convert_type(zs, jnp.uint32) | jnp.uint32(0x80000000), F32)
    return jnp.maximum(zs, 0.0) + jnp.log2(1.0 + jnp.exp2(neg_abs))


def _pair_mask(first_rel_block, bq):
    key = lax.broadcasted_iota(jnp.int32, (2 * SB_BK, bq), 0) + first_rel_block * SB_BK
    qry = lax.broadcasted_iota(jnp.int32, (2 * SB_BK, bq), 1)
    return key < qry


def _row_of(table8, sub8, r):
    return jnp.sum(jnp.where(sub8 == r, table8, 0.0), axis=0, keepdims=True)


def _keys(j0):
    return pl.ds(pl.multiple_of(j0 * SB_BK, 2 * SB_BK), 2 * SB_BK)


class Carry:
    def __init__(self, build, ins, out_shapes, sems):
        self.build, self.ins, self.out_shapes, self.sems = build, list(ins), list(out_shapes), list(sems)


def _carried(carry, rest, n_out, n_scratch, first, last):
    n_ci = len(carry.ins) if carry else 0
    n_co = len(carry.out_shapes) if carry else 0
    cin, outs = rest[:n_ci], rest[n_ci:n_ci + n_out]
    cout = rest[n_ci + n_out:n_ci + n_out + n_co]
    scratch = rest[n_ci + n_out + n_co:n_ci + n_out + n_co + n_scratch]
    csems = rest[n_ci + n_out + n_co + n_scratch:]

    def start():
        if carry:
            @pl.when(first)
            def _():
                for cp in carry.build(cin, cout, *csems):
                    cp.start()

    def wait():
        if carry:
            @pl.when(last)
            def _():
                for cp in carry.build(cin, cout, *csems):
                    cp.wait()

    return outs, scratch, start, wait


def _contract0(a, b):
    return lax.dot_general(a, b, (((0,), (0,)), ((), ())), preferred_element_type=F32)


def _contract1(a, b):
    return lax.dot_general(a, b, (((1,), (1,)), ((), ())), preferred_element_type=F32)


def sb_fwd(name, qkvt, exchange=None):
    nh, dh, s = qkvt.shape[0] // 3, qkvt.shape[1], qkvt.shape[2]
    bq = SB_BQ
    per_q = bq // SB_BK
    nkb = s // SB_BK
    assert s % bq == 0 and per_q == 4 and nkb % 8 == 0

    def body(q_ref, k_ref, v_ref, a_ref, *rest):
        i = pl.program_id(1)
        first = (pl.program_id(0) == 0) & (i == 0)
        last = (pl.program_id(0) == nh - 1) & (i == s // bq - 1)
        (o_ref, rtab_ref), (acc, zbuf, wbuf), start_carried, wait_carried = _carried(exchange, rest, 2, 3, first, last)
        start_carried()
        qb = q_ref[...] * Q_SCALE
        tri = a_ref[...]
        sub8 = lax.broadcasted_iota(jnp.int32, (8, bq), 0)
        acc[...] = jnp.zeros_like(acc)
        rtab_ref[...] = jnp.full(rtab_ref.shape, SB_UNSEEN, F32)
        kf = k_ref[...].astype(F32)
        qf = qb.astype(F32)
        k_max2 = jnp.max(jnp.sum(kf * kf, axis=0, keepdims=True), axis=1, keepdims=True)
        bound = jnp.sqrt(jnp.sum(qf * qf, axis=0, keepdims=True) * k_max2) * (1.001 * LOG2E)

        def scores(j0):
            return _contract0(k_ref[:, _keys(j0)], qb) * LOG2E

        def pair(j0, slot, run, rt8, mask, has_prev):
            zs = zbuf[slot]
            zbuf[1 - slot] = scores(jnp.maximum(j0 - 2, 0))
            if has_prev:
                acc[...] += jnp.dot(v_ref[:, _keys(j0 + 2)], wbuf[1 - slot], preferred_element_type=F32)
            p = _softplus2(zs)
            if mask is not None:
                p = jnp.where(mask, p, 0.0)
            cr1 = jnp.dot(tri, _hi_lo_rows(p[SB_BK:]), preferred_element_type=F32)
            cr0 = jnp.dot(tri, _hi_lo_rows(p[:SB_BK]), preferred_element_type=F32)
            run1 = run + cr1[SB_BK:SB_BK + 1]
            w = jnp.exp2(jnp.concatenate([zs[:SB_BK] - cr0[:SB_BK] - run1, zs[SB_BK:] - cr1[:SB_BK] - run], axis=0))
            if mask is not None:
                w = jnp.where(mask, w, 0.0)
            wbuf[slot] = w.astype(BF16)
            rt8 = jnp.where(j0 % 8 == 6, SB_UNSEEN, rt8)
            rt8 = jnp.where(sub8 == (j0 + 1) % 8, run, jnp.where(sub8 == j0 % 8, run1, rt8))
            rtab_ref[pl.ds(pl.multiple_of((j0 // 8) * 8, 8), 8), :] = rt8
            return run1 + cr0[SB_BK:SB_BK + 1], rt8

        def alive(run):
            return jnp.min(run - bound) < SB_DEAD

        top = i * per_q
        zbuf[0] = scores(top + 2)
        state = (jnp.zeros((1, bq), F32), jnp.full((8, bq), SB_UNSEEN, F32))
        state = pair(top + 2, 0, *state, _pair_mask(2, bq), False)
        state = pair(top, 1, *state, _pair_mask(0, bq), True)

        def step(c):
            it, _, run, rt8 = c
            j0 = top - 2 - 4 * it
            run, rt8 = pair(j0, 0, run, rt8, None, True)
            run, rt8 = pair(j0 - 2, 1, run, rt8, None, True)
            return it + 1, alive(run), run, rt8

        trips = lax.while_loop(lambda c: (c[0] < i) & c[1], step, (0, alive(state[0]), *state))[0]
        acc[...] += jnp.dot(v_ref[:, _keys(top - 4 * trips)], wbuf[1], preferred_element_type=F32)
        o_ref[...] = acc[...].astype(o_ref.dtype)
        wait_carried()

    qspec = pl.BlockSpec((None, dh, bq), lambda h, i: (h, 0, i))
    hbm = pl.BlockSpec(memory_space=pl.ANY)
    c_ins, c_outs, c_sems = (exchange.ins, exchange.out_shapes, exchange.sems) if exchange else ([], [], [])
    outs = pl.pallas_call(
        body, name=name, grid=(nh, s // bq),
        in_specs=[qspec, pl.BlockSpec((None, dh, s), lambda h, i: (h + nh, 0, 0)),
                  pl.BlockSpec((None, dh, s), lambda h, i: (h + 2 * nh, 0, 0)),
                  pl.BlockSpec((SB_BK + 8, 2 * SB_BK), lambda h, i: (0, 0))] + [hbm] * len(c_ins),
        out_specs=[qspec, pl.BlockSpec((None, nkb, bq), lambda h, i: (h, 0, i))] + [hbm] * len(c_outs),
        out_shape=[jax.ShapeDtypeStruct((nh, dh, s), BF16), jax.ShapeDtypeStruct((nh, nkb, s), F32)] + c_outs,
        scratch_shapes=[pltpu.VMEM((dh, bq), F32), pltpu.VMEM((2, 2 * SB_BK, bq), F32),
                        pltpu.VMEM((2, 2 * SB_BK, bq), BF16)] + c_sems,
        compiler_params=_params(("arbitrary", "arbitrary")),
    )(qkvt, qkvt, qkvt, _tri_rows(True), *c_ins)
    return outs[0], outs[1], outs[2:]


def sb_bwd(name, qkvt, dot_, rtab, exchange=None):
    nh, dh, s = qkvt.shape[0] // 3, qkvt.shape[1], qkvt.shape[2]
    bq = SB_BQ
    per_q = bq // SB_BK
    nkb = s // SB_BK
    nq = s // bq

    def body(qt_ref, kt_ref, vt_ref, dot_ref, rtab_ref, ar_ref, af_ref, *rest):
        i = pl.program_id(1)
        first = (pl.program_id(0) == 0) & (i == 0)
        last = (pl.program_id(0) == nh - 1) & (i == nq - 1)
        (dq_ref, dk_ref, dv_ref), (dq_acc, dk_acc, dv_acc, zbuf, dwbuf, dzbuf, wbuf), start_carried, wait_carried = \
            _carried(exchange, rest, 3, 7, first, last)
        start_carried()

        @pl.when(i == 0)
        def _():
            dk_acc[...] = jnp.zeros_like(dk_acc)
            dv_acc[...] = jnp.zeros_like(dv_acc)

        qtb = qt_ref[...] * Q_SCALE
        dotb = dot_ref[...]
        tri_rev = ar_ref[...][:SB_BK]
        tri_fwd = af_ref[...]
        sub8 = lax.broadcasted_iota(jnp.int32, (8, bq), 0)
        dq_acc[...] = jnp.zeros_like(dq_acc)
        last_j = i * per_q + 2
        seen = jnp.max(jnp.where(rtab_ref[...] < 0.1 * SB_UNSEEN, 1.0, 0.0), axis=1, keepdims=True)
        trips = jnp.clip((jnp.sum(seen).astype(jnp.int32) - per_q) // per_q, 0, i)
        first_j = (i - trips) * per_q

        def issue(j0, slot):
            zbuf[slot] = _contract0(kt_ref[:, _keys(j0)], qtb) * LOG2E
            dwbuf[slot] = _contract0(vt_ref[:, _keys(j0)], dotb)

        def retire(j0, slot):
            keys = _keys(j0)
            dq_acc[...] += jnp.dot(kt_ref[:, keys], dzbuf[slot], preferred_element_type=F32)
            dk_acc[:, keys] += _contract1(qtb, dzbuf[slot])
            dv_acc[:, keys] += _contract1(dotb, wbuf[slot])

        def pair(j0, slot, g_run, mask):
            zs = zbuf[slot]
            dw = dwbuf[slot]
            issue(jnp.minimum(j0 + 2, last_j), 1 - slot)
            retire(jnp.maximum(j0 - 2, first_j), 1 - slot)
            p_raw = _softplus2(zs)
            p = p_raw if mask is None else jnp.where(mask, p_raw, 0.0)
            c0 = jnp.dot(tri_rev, _hi_lo_rows(p[:SB_BK]), preferred_element_type=F32)
            c1 = jnp.dot(tri_rev, _hi_lo_rows(p[SB_BK:]), preferred_element_type=F32)
            rt8 = rtab_ref[pl.ds(pl.multiple_of((j0 // 8) * 8, 8), 8), :]
            r0 = _row_of(rt8, sub8, j0 % 8)
            r1 = _row_of(rt8, sub8, (j0 + 1) % 8)
            w = jnp.exp2(jnp.concatenate([zs[:SB_BK] - c0 - r0, zs[SB_BK:] - c1 - r1], axis=0))
            if mask is not None:
                w = jnp.where(mask, w, 0.0)
            g = w * dw
            gg0 = jnp.dot(tri_fwd, _hi_lo_rows(g[:SB_BK]), preferred_element_type=F32)
            gg1 = jnp.dot(tri_fwd, _hi_lo_rows(g[SB_BK:]), preferred_element_type=F32)
            g_run1 = g_run + gg0[SB_BK:SB_BK + 1]
            g_pre = jnp.concatenate([gg0[:SB_BK] + g_run, gg1[:SB_BK] + g_run1], axis=0)
            dz = g - jnp.exp2(zs - p_raw) * g_pre
            if mask is not None:
                dz = jnp.where(mask, dz, 0.0)
            dzbuf[slot] = dz.astype(BF16)
            wbuf[slot] = w.astype(BF16)
            return g_run1 + gg1[SB_BK:SB_BK + 1]

        issue(first_j, 0)
        dzbuf[1] = jnp.zeros((2 * SB_BK, bq), BF16)
        wbuf[1] = jnp.zeros((2 * SB_BK, bq), BF16)

        def step(it, g_run):
            g_run = pair(4 * it, 0, g_run, None)
            return pair(4 * it + 2, 1, g_run, None)

        g_run = lax.fori_loop(i - trips, i, step, jnp.zeros((1, bq), F32))
        g_run = pair(last_j - 2, 0, g_run, _pair_mask(0, bq))
        pair(last_j, 1, g_run, _pair_mask(2, bq))
        retire(last_j, 1)
        dq_ref[...] = (dq_acc[...] * Q_SCALE).astype(dq_ref.dtype)

        @pl.when(i == nq - 1)
        def _():
            dk_ref[...] = dk_acc[...].astype(dk_ref.dtype)
            dv_ref[...] = dv_acc[...].astype(dv_ref.dtype)

        wait_carried()

    tspec = pl.BlockSpec((None, dh, bq), lambda h, i: (h, 0, i))
    fullspec = pl.BlockSpec((None, dh, s), lambda h, i: (h, 0, 0))
    aspec = pl.BlockSpec((SB_BK + 8, 2 * SB_BK), lambda h, i: (0, 0))
    pair_f32 = pltpu.VMEM((2, 2 * SB_BK, bq), F32)
    pair_bf16 = pltpu.VMEM((2, 2 * SB_BK, bq), BF16)
    hbm = pl.BlockSpec(memory_space=pl.ANY)
    c_ins, c_outs, c_sems = (exchange.ins, exchange.out_shapes, exchange.sems) if exchange else ([], [], [])
    outs = pl.pallas_call(
        body, name=name, grid=(nh, s // bq),
        in_specs=[tspec, pl.BlockSpec((None, dh, s), lambda h, i: (h + nh, 0, 0)),
                  pl.BlockSpec((None, dh, s), lambda h, i: (h + 2 * nh, 0, 0)), tspec,
                  pl.BlockSpec((None, nkb, bq), lambda h, i: (h, 0, i)), aspec, aspec] + [hbm] * len(c_ins),
        out_specs=[tspec, fullspec, fullspec] + [hbm] * len(c_outs),
        out_shape=[jax.ShapeDtypeStruct((nh, dh, s), BF16)] * 3 + c_outs,
        scratch_shapes=[pltpu.VMEM((dh, bq), F32), pltpu.VMEM((dh, s), F32), pltpu.VMEM((dh, s), F32),
                        pair_f32, pair_f32, pair_bf16, pair_bf16] + c_sems,
        compiler_params=_params(("arbitrary", "arbitrary")),
    )(qkvt, qkvt, qkvt, dot_, rtab, _tri_rows(True), _tri_rows(False), *c_ins)
    return outs[0], outs[1], outs[2], outs[3:]


SWA_QB = 2


def _swa_probs(qt, kt, bias_t, sink, i):
    cols = qt.shape[1]
    sc = _contract0(kt, qt) + bias_t
    kj = lax.broadcasted_iota(jnp.int32, (2 * WINDOW, cols), 0)
    qi = lax.broadcasted_iota(jnp.int32, (2 * WINDOW, cols), 1) & (WINDOW - 1)
    dist = qi + WINDOW - kj
    valid = (dist >= 0) & (dist < WINDOW) & ((kj >= WINDOW) | (i > 0))
    sc = jnp.where(valid, sc, NEG_INF)
    mx = jnp.maximum(jnp.max(sc, axis=0, keepdims=True), sink)
    p = jnp.exp(sc - mx)
    p_sink = jnp.exp(sink - mx)
    inv = 1.0 / (jnp.sum(p, axis=0, keepdims=True) + p_sink)
    return p, p_sink, inv


def _band(i):
    return pl.ds(pl.multiple_of(i * WINDOW, WINDOW), 2 * WINDOW)


def _heads_to_lanes(blk):
    return jnp.concatenate([blk[r * HEAD_DIM:(r + 1) * HEAD_DIM] for r in range(8)], axis=1)


def _lanes_to_heads(t):
    return jnp.concatenate([t[:, r * WINDOW:(r + 1) * WINDOW] for r in range(8)], axis=0)


def swa_fwd(name, qt, kpt, vpt, bias_t, sink_row):
    d, s = qt.shape
    ng, dh, sp = kpt.shape
    rows, cols = d // ng, SWA_QB * WINDOW
    assert (s // WINDOW) % SWA_QB == 0

    def body(q_ref, k_ref, v_ref, bias_ref, sink_ref, o_ref):
        for u in range(SWA_QB):
            i = pl.program_id(1) * SWA_QB + u
            lanes = slice(u * WINDOW, (u + 1) * WINDOW)
            qb = _heads_to_lanes(q_ref[:, lanes]) * Q_SCALE
            p, _, inv = _swa_probs(qb, k_ref[:, _band(i)], bias_ref[...], sink_ref[...], i)
            o_t = jnp.dot(v_ref[:, _band(i)], p.astype(BF16), preferred_element_type=F32) * inv
            o_ref[:, lanes] = _lanes_to_heads(o_t).astype(o_ref.dtype)

    qspec = pl.BlockSpec((rows, cols), lambda g, i: (g, i))
    kspec = pl.BlockSpec((None, dh, sp), lambda g, i: (g, 0, 0))
    return pl.pallas_call(
        body, name=name, grid=(ng, s // cols),
        in_specs=[qspec, kspec, kspec, pl.BlockSpec((None, 2 * WINDOW, 8 * WINDOW), lambda g, i: (g, 0, 0)),
                  pl.BlockSpec((None, 1, 8 * WINDOW), lambda g, i: (g, 0, 0))],
        out_specs=qspec,
        out_shape=jax.ShapeDtypeStruct(qt.shape, BF16),
        compiler_params=_params(("parallel", "arbitrary")),
    )(qt, kpt, vpt, bias_t, sink_row)


def swa_bwd(name, qt, kpt, vpt, bias_t, sink_row, dot_, dk_in, dv_in):
    d, s = qt.shape
    ng, dh, sp = kpt.shape
    rows, cols = d // ng, SWA_QB * WINDOW

    def body(q_ref, k_ref, v_ref, bias_ref, sink_ref, do_ref, dki_ref, dvi_ref, dq_ref, dk_ref, dv_ref, db_ref, ds_ref):
        @pl.when(pl.program_id(1) == 0)
        def _():
            dk_ref[...] = dki_ref[...]
            dv_ref[...] = dvi_ref[...]
            db_ref[...] = jnp.zeros_like(db_ref)
            ds_ref[...] = jnp.zeros_like(ds_ref)

        for u in range(SWA_QB):
            i = pl.program_id(1) * SWA_QB + u
            band = _band(i)
            lanes = slice(u * WINDOW, (u + 1) * WINDOW)
            qb = _heads_to_lanes(q_ref[:, lanes]) * Q_SCALE
            dob = _heads_to_lanes(do_ref[:, lanes])
            kt = k_ref[:, band]
            p, p_sink, inv = _swa_probs(qb, kt, bias_ref[...], sink_ref[...], i)
            p = p * inv
            dp = _contract0(v_ref[:, band], dob)
            delta = jnp.sum(p * dp, axis=0, keepdims=True)
            dsc = p * (dp - delta)
            ds_ref[...] -= p_sink * inv * delta
            db_ref[...] += dsc
            dscb = dsc.astype(BF16)
            dq_t = jnp.dot(kt, dscb, preferred_element_type=F32) * Q_SCALE
            dq_ref[:, lanes] = _lanes_to_heads(dq_t).astype(dq_ref.dtype)
            dk_ref[:, band] += _contract1(qb, dscb)
            dv_ref[:, band] += _contract1(dob, p.astype(BF16))

    qspec = pl.BlockSpec((rows, cols), lambda g, i: (g, i))
    kspec = pl.BlockSpec((None, dh, sp), lambda g, i: (g, 0, 0))
    bspec = pl.BlockSpec((None, 2 * WINDOW, 8 * WINDOW), lambda g, i: (g, 0, 0))
    sspec = pl.BlockSpec((None, 1, 8 * WINDOW), lambda g, i: (g, 0, 0))
    return pl.pallas_call(
        body, name=name, grid=(ng, s // cols),
        in_specs=[qspec, kspec, kspec, bspec, sspec, qspec, kspec, kspec],
        out_specs=[qspec, kspec, kspec, bspec, sspec],
        out_shape=[jax.ShapeDtypeStruct(qt.shape, BF16), jax.ShapeDtypeStruct(kpt.shape, F32),
                   jax.ShapeDtypeStruct(kpt.shape, F32), jax.ShapeDtypeStruct(bias_t.shape, F32),
                   jax.ShapeDtypeStruct(sink_row.shape, F32)],
        compiler_params=_params(("parallel", "arbitrary")),
    )(qt, kpt, vpt, bias_t, sink_row, dot_, dk_in, dv_in)


def _bucket_onehot():
    qi = np.arange(WINDOW)[:, None]
    kj = np.arange(2 * WINDOW)[None, :]
    n = np.maximum(qi + WINDOW - kj, 0)
    max_exact = N_BUCKETS // 2
    nf = np.maximum(n, 1).astype(np.float64)
    val = np.log(nf / max_exact) / math.log(WINDOW / max_exact) * (N_BUCKETS - max_exact)
    assert np.all(np.abs(val - np.round(val))[(n > max_exact) & (n < WINDOW)] > 1e-3)
    large = np.minimum(max_exact + val.astype(np.int64), N_BUCKETS - 1)
    bucket = np.where(n < max_exact, n, large).reshape(-1)
    onehot = np.zeros((128, bucket.size), np.float32)
    onehot[bucket, np.arange(bucket.size)] = 1.0
    return onehot


def _split3(x):
    a = x.astype(BF16)
    r = x - a.astype(F32)
    b = r.astype(BF16)
    c = (r - b.astype(F32)).astype(BF16)
    return a, b, c


def bias_table(rel_bias):
    nh = rel_bias.shape[1]
    oh = jnp.asarray(_bucket_onehot(), BF16)
    n = oh.shape[1]
    tn = 4096
    rb = jnp.zeros((nh, 128), F32).at[:, :N_BUCKETS].set(rel_bias.T)

    def body(rb_ref, oh_ref, o_ref):
        o_ref[...] = sum(jnp.dot(t, oh_ref[...], preferred_element_type=F32) for t in _split3(rb_ref[...]))

    return pl.pallas_call(
        body, name="bias_table", grid=(n // tn,),
        in_specs=[pl.BlockSpec((nh, 128), lambda i: (0, 0)), pl.BlockSpec((128, tn), lambda i: (0, i))],
        out_specs=pl.BlockSpec((nh, tn), lambda i: (0, i)),
        out_shape=jax.ShapeDtypeStruct((nh, n), F32),
        compiler_params=_params(("parallel",)),
    )(rb, oh)


def bias_table_grad(db0, db1):
    nh, n = db0.shape
    oh = jnp.asarray(_bucket_onehot(), BF16)
    tn = 4096

    def body(a_ref, b_ref, oh_ref, o_ref):
        @pl.when(pl.program_id(0) == 0)
        def _():
            o_ref[...] = jnp.zeros_like(o_ref)

        o_ref[...] += sum(lax.dot_general(t, oh_ref[...], (((1,), (1,)), ((), ())), preferred_element_type=F32)
                          for t in _split3(a_ref[...] + b_ref[...]))

    blk = pl.BlockSpec((nh, tn), lambda i: (0, i))
    return pl.pallas_call(
        body, name="bias_table_grad", grid=(n // tn,),
        in_specs=[blk, blk, pl.BlockSpec((128, tn), lambda i: (0, i))],
        out_specs=pl.BlockSpec((nh, 128), lambda i: (0, 0)),
        out_shape=jax.ShapeDtypeStruct((nh, 128), F32),
        compiler_params=_params(("arbitrary",)),
    )(db0, db1, oh)


def _owner_view(ref, name, d):
    if name == 'a_norm':
        return ref.at[d]
    if name in COL_SHARDED:
        n = ref.shape[2] // N_DEV
        return ref.at[:, :, pl.ds(pl.multiple_of(d * n, 128), n)]
    return ref.at[:, d]


def _place():
    return lax.axis_index("x"), lax.axis_index("y"), lax.axis_index("c")


def _dev(p):
    return 4 * p[0] + 2 * p[1] + p[2]


def _remote(src, dst, send_sem, recv_sem, to):
    return pltpu.make_async_remote_copy(src_ref=src, dst_ref=dst, send_sem=send_sem, recv_sem=recv_sem,
                                        device_id=to, device_id_type=MESH)


def _dma_sems(*shapes):
    return [pltpu.SemaphoreType.DMA(sh) for sh in shapes]


def comm_call(name, build, ins, out_shapes, sems, aliases=None):
    n_in, n_out = len(ins), len(out_shapes)

    def body(*refs):
        copies = build(refs[:n_in], refs[n_in:n_in + n_out], *refs[n_in + n_out:])
        for cp in copies:
            cp.start()
        for cp in copies:
            cp.wait()

    hbm = pl.BlockSpec(memory_space=pl.ANY)
    return pl.pallas_call(
        body, name=name, in_specs=[hbm] * n_in, out_specs=[hbm] * n_out, out_shape=list(out_shapes),
        scratch_shapes=sems, input_output_aliases=aliases or {},
    )(*ins)


def all_gather_weights(names, shards, full_shapes):
    n = len(names)

    def body(*refs):
        ins, outs = refs[:n], refs[n:2 * n]
        send_sems, recv_sems, local_sems = refs[2 * n:]
        x, y, c = _place()
        me, sibling = (x, y, c), (x, y, 1 - c)
        chips = [(1 - x, y), (x, 1 - y), (1 - x, 1 - y)]

        def copy(t, k, block, to, src=None):
            dst = _owner_view(outs[t], names[t], _dev(block))
            return _remote(dst if src is None else src, dst, send_sems.at[t, k], recv_sems.at[t, k], to)

        mine = [pltpu.make_async_copy(ins[t], _owner_view(outs[t], names[t], _dev(me)), local_sems.at[t])
                for t in range(n)]
        for cp in mine:
            cp.start()
        first = []
        for t in range(n):
            first.append(copy(t, 0, me, sibling, src=ins[t]))
            first += [copy(t, 1 + j, me, (*chip, c), src=ins[t]) for j, chip in enumerate(chips)]
        for cp in first:
            cp.start()
        passed = []
        for j, chip in enumerate(chips):
            for t in range(n):
                copy(t, 1 + j, (*chip, c), me).wait_recv()
                fwd = copy(t, 4 + j, (*chip, c), sibling)
                fwd.start()
                passed.append(fwd)
        for t in range(n):
            copy(t, 0, sibling, me).wait_recv()
            for j, chip in enumerate(chips):
                copy(t, 4 + j, (*chip, 1 - c), me).wait_recv()
        for cp in first + passed:
            cp.wait_send()
        for cp in mine:
            cp.wait()

    hbm = pl.BlockSpec(memory_space=pl.ANY)
    return pl.pallas_call(
        body, name="all_gather_layer0",
        in_specs=[hbm] * n, out_specs=[hbm] * n,
        out_shape=[jax.ShapeDtypeStruct(full_shapes[t], shards[t].dtype) for t in range(n)],
        scratch_shapes=_dma_sems((n, 7), (n, 7), (n,)),
    )(*shards)


def ag_direct(names, shards, full_shapes):
    n = len(names)

    def build(ins, outs, send_sems, recv_sems, local_sems):
        x, y, c = _place()
        peers = [(x, y, 1 - c), (1 - x, y, c), (x, 1 - y, c), (1 - x, 1 - y, c)]
        copies = []
        for t in range(n):
            dst = _owner_view(outs[t], names[t], _dev((x, y, c)))
            copies.append(pltpu.make_async_copy(ins[t], dst, local_sems.at[t]))
            copies += [_remote(ins[t], dst, send_sems.at[t, k], recv_sems.at[t, k], to) for k, to in enumerate(peers)]
        return copies

    return Carry(build, shards, [jax.ShapeDtypeStruct(full_shapes[t], shards[t].dtype) for t in range(n)],
                 _dma_sems((n, 4), (n, 4), (n,)))


def ag_forward(names, partial):
    n = len(names)

    def build(ins, outs, send_sems, recv_sems):
        del ins
        x, y, c = _place()
        copies = []
        for t in range(n):
            for k, chip in enumerate([(1 - x, y), (x, 1 - y), (1 - x, 1 - y)]):
                view = _owner_view(outs[t], names[t], _dev((*chip, c)))
                copies.append(_remote(view, view, send_sems.at[t, k], recv_sems.at[t, k], (x, y, 1 - c)))
        return copies

    return comm_call("all_gather_forward", build, partial, [jax.ShapeDtypeStruct(p.shape, p.dtype) for p in partial],
                     _dma_sems((n, 3), (n, 3)), aliases={t: t for t in range(n)})


def sibling_exchange(tag, names, grads, part_shapes):
    n = len(names)

    def build(ins, outs, send_sems, recv_sems):
        x, y, c = _place()
        return [_remote(_owner_view(ins[t], names[t], 2 * q + 1 - c), outs[t].at[q], send_sems.at[t, q],
                        recv_sems.at[t, q], (x, y, 1 - c)) for t in range(n) for q in range(4)]

    return comm_call(f"rs_sibling_exchange_{tag}", build, grads,
                     [jax.ShapeDtypeStruct((4,) + part_shapes[t], BF16) for t in range(n)], _dma_sems((n, 4), (n, 4)))


def chip_exchange(names, parts, part_shapes):
    n = len(names)

    def build(ins, outs, send_sems, recv_sems):
        x, y, c = _place()
        chips = [(1 - x, y), (x, 1 - y), (1 - x, 1 - y)]
        return [_remote(ins[t].at[2 * chip[0] + chip[1]], outs[t].at[k], send_sems.at[t, k], recv_sems.at[t, k],
                        (*chip, c)) for t in range(n) for k, chip in enumerate(chips)]

    return Carry(build, parts, [jax.ShapeDtypeStruct((3,) + part_shapes[t], BF16) for t in range(n)],
                 _dma_sems((n, 3), (n, 3)))


def all_gather_rows(x):
    r, w = x.shape

    def body(x_ref, out_ref, send_sems, recv_sems, local_sem):
        px, py, pc = _place()
        me = 4 * px + 2 * py + pc
        mine = pltpu.make_async_copy(x_ref, out_ref.at[me], local_sem)
        mine.start()
        copies = []
        for k in range(1, N_DEV):
            peer = (px ^ (k >> 2), py ^ ((k >> 1) & 1), pc ^ (k & 1))
            copies.append(pltpu.make_async_remote_copy(
                src_ref=x_ref, dst_ref=out_ref.at[me], send_sem=send_sems.at[k - 1], recv_sem=recv_sems.at[k - 1],
                device_id=peer, device_id_type=MESH))
        for cp in copies:
            cp.start()
        for k in range(1, N_DEV):
            peer_idx = me ^ k
            pltpu.make_async_remote_copy(
                src_ref=x_ref, dst_ref=out_ref.at[peer_idx], send_sem=send_sems.at[k - 1],
                recv_sem=recv_sems.at[k - 1], device_id=(px, py, pc), device_id_type=MESH).wait_recv()
        for cp in copies:
            cp.wait_send()
        mine.wait()

    vmem = pl.BlockSpec(memory_space=pltpu.VMEM)
    return pl.pallas_call(
        body, name="all_gather_small_grads",
        in_specs=[vmem], out_specs=vmem,
        out_shape=jax.ShapeDtypeStruct((N_DEV, r, w), x.dtype),
        scratch_shapes=[pltpu.SemaphoreType.DMA((N_DEV - 1,)), pltpu.SemaphoreType.DMA((N_DEV - 1,)),
                        pltpu.SemaphoreType.DMA],
    )(x)


def _adamw(w, g, m, v):
    m = ADAM_B1 * m + (1.0 - ADAM_B1) * g
    v = ADAM_B2 * v + (1.0 - ADAM_B2) * (g * g)
    m_hat = m / (1.0 - ADAM_B1 ** ADAM_STEP)
    v_hat = v / (1.0 - ADAM_B2 ** ADAM_STEP)
    return -ADAM_LR * (m_hat / (jnp.sqrt(v_hat) + ADAM_EPS) + ADAM_WD * w), m, v


def sibling_sum(name, col, grads, recv, core):
    _, nl, rows, cols = recv.shape
    tr = _tile(rows, 512)
    rspec = pl.BlockSpec((None, None, tr, cols), lambda q, l, i, c_ref: (q, l, i, 0))
    if col:
        gspec = pl.BlockSpec((None, tr, cols), lambda q, l, i, c_ref: (l, i, 2 * q + c_ref[0]))
    else:
        gspec = pl.BlockSpec((None, None, tr, cols), lambda q, l, i, c_ref: (l, 2 * q + c_ref[0], i, 0))

    def body(c_ref, g_ref, r_ref, o_ref):
        del c_ref
        o_ref[...] = (g_ref[...].astype(F32) + r_ref[...].astype(F32)).astype(BF16)

    return pl.pallas_call(
        body, name=name,
        grid_spec=pltpu.PrefetchScalarGridSpec(num_scalar_prefetch=1, grid=(4, nl, rows // tr),
                                               in_specs=[gspec, rspec], out_specs=rspec),
        out_shape=jax.ShapeDtypeStruct(recv.shape, BF16),
        compiler_params=_params(("parallel", "parallel", "parallel")),
    )(core.reshape(1), grads, recv)


def reduce_adamw(name, parts, recv, chip, w, m, v, l0, prev):
    _, nl, rows, cols = parts.shape
    tr = _tile(rows, 256)

    def body(q_ref, p_ref, r_ref, w_ref, m_ref, v_ref, *rest):
        del q_ref
        g_out, d_out, m_out, v_out = rest[-4:]
        g = ((p_ref[...].astype(F32) + r_ref[0].astype(F32)) + r_ref[1].astype(F32)) + r_ref[2].astype(F32)
        d, mn, vn = _adamw(w_ref[...], g, m_ref[...], v_ref[...])
        g_out[...] = g
        d_out[...] = d
        m_out[...] = mn
        v_out[...] = vn

    blk = pl.BlockSpec((None, tr, cols), lambda l, i, q_ref: (l0 + l, i, 0))
    prev = list(prev) if prev else []
    return pl.pallas_call(
        body, name=name,
        grid_spec=pltpu.PrefetchScalarGridSpec(
            num_scalar_prefetch=1, grid=(nl, rows // tr),
            in_specs=[pl.BlockSpec((None, None, tr, cols), lambda l, i, q_ref: (q_ref[0], l, i, 0)),
                      pl.BlockSpec((3, None, tr, cols), lambda l, i, q_ref: (0, l, i, 0)), blk, blk, blk]
            + [pl.BlockSpec(memory_space=pl.ANY)] * len(prev),
            out_specs=[blk] * 4),
        out_shape=[jax.ShapeDtypeStruct(w.shape, F32)] * 4,
        input_output_aliases={6 + i: i for i in range(len(prev))},
        compiler_params=_params(("parallel", "parallel")),
    )(chip.reshape(1), parts, recv, w, m, v, *prev)


def small_adamw(name, gathered, w, m, v):
    _, r, c = gathered.shape

    def body(ga_ref, w_ref, m_ref, v_ref, g_out, d_out, m_out, v_out):
        g = ga_ref[0]
        for d in range(1, N_DEV):
            g = g + ga_ref[d]
        dl, mn, vn = _adamw(w_ref[...], g, m_ref[...], v_ref[...])
        g_out[...] = g
        d_out[...] = dl
        m_out[...] = mn
        v_out[...] = vn

    return pl.pallas_call(
        body, name=name,
        out_shape=[jax.ShapeDtypeStruct((r, c), F32)] * 4,
        compiler_params=_params(),
    )(gathered, w, m, v)


def _add(acc, *ex):
    return (acc + ex[0],)


def local_step(x, target, small, ex):
    s, d = x.shape
    n_a, n_b = small['a_norm'].shape[0], small['b_norm'].shape[0]
    sg = {}
    gb = {}

    def fwd_mm(name, a, wname, layer, epilogue, extras, out_dtypes):
        return mm_nn(name, a, *ex.weight(wname, layer), epilogue, extras, out_dtypes)

    def dx_mm(name, dy, wname, layer, epilogue, extras, out_dtypes):
        return mm_nt(name, dy, *ex.weight(wname, layer), epilogue, extras, out_dtypes)

    def dw_mm(name, a, dy, wname, layer):
        key, slab, shape = ex.grad(wname, layer)
        gb[key] = mm_tn(name, a, dy, gb.get(key), shape, slab)

    bias_flat = bias_table(small['rel_bias'])
    bias_t = bias_flat.reshape(2, 8, WINDOW, 2 * WINDOW).transpose(0, 3, 1, 2).reshape(2, 2 * WINDOW, 8 * WINDOW)
    sink_rows = [jnp.repeat(small['b_sinks'][j], WINDOW).reshape(2, 1, 8 * WINDOW) for j in range(n_b)]

    def mlp_fwd(h, layer):
        n2 = rms_fwd(f"mlp_norm_fwd{layer}", h, small['mlp_norm'][layer])
        u, a = fwd_mm(f"mlp_up_fwd{layer}", n2, 'mlp_up', layer,
                      lambda acc: (acc, jnp.square(jnp.maximum(acc, 0.0))), (), (BF16, BF16))
        (h2,) = fwd_mm(f"mlp_down_fwd{layer}", a, 'mlp_down', layer, _add, (h,), (F32,))
        return h2, (n2, u, a)

    h = x
    saved = []
    for l in range(n_a):
        n1 = rms_fwd(f"a_norm_fwd{l}", h, small['a_norm'][l])
        (qkv,) = fwd_mm(f"a_qkv_fwd{l}", n1, 'a_wqkv', l, lambda acc: (acc,), (), (BF16,))
        qkvt = qkv.T.reshape(3 * d // HEAD_DIM, HEAD_DIM, s)
        o_t, rtab, carried = sb_fwd(f"sb_fwd{l}", qkvt, ex.fwd_carry(l))
        ex.fwd_done(l, carried)
        o = o_t.reshape(d, s).T
        (h_mid,) = fwd_mm(f"a_wo_fwd{l}", o, 'a_wo', l, _add, (h,), (F32,))
        h_out, mlp_saved = mlp_fwd(h_mid, l)
        saved.append((h, n1, qkvt, o, rtab, h_mid, mlp_saved))
        h = h_out
    h_kv = h
    nkv = rms_fwd("kv_norm_fwd", h, small['kv_norm'])
    (kv,) = fwd_mm("kv_fwd", nkv, 'w_kv', 0, lambda acc, b: (acc + b,), (small['b_kv'].reshape(1, -1),), (BF16,))
    kvt = kv.T.reshape(2, 2, HEAD_DIM, s)
    kpt, vpt = (jnp.pad(t, ((0, 0), (0, 0), (WINDOW, 0))) for t in (kvt[0], kvt[1]))
    for j in range(n_b):
        layer = n_a + j
        n1 = rms_fwd(f"b_norm_fwd{j}", h, small['b_norm'][j])
        (qb,) = fwd_mm(f"b_q_fwd{j}", n1, 'b_wq', j, lambda acc, b: (acc + b,),
                       (small['b_bq'][j].reshape(1, -1),), (BF16,))
        qbt = qb.T
        o = swa_fwd(f"swa_fwd{j}", qbt, kpt, vpt, bias_t, sink_rows[j]).T
        (h_mid,) = fwd_mm(f"b_wo_fwd{j}", o, 'b_wo', j, lambda acc, hh, b: (acc + hh + b,),
                          (h, small['b_bo'][j].reshape(1, -1)), (F32,))
        h_out, mlp_saved = mlp_fwd(h_mid, layer)
        saved.append((h, n1, qbt, o, h_mid, mlp_saved))
        h = h_out

    dh, dhb, dg_final, loss_b = loss_head(h, small['final_norm'], target)
    sg['final_norm'] = dg_final[0]
    sg['mlp_norm'] = [None] * (n_a + n_b)

    def mlp_bwd(dh, dhb, h_mid, mlp_saved, layer):
        n2, u, a = mlp_saved
        (du,) = dx_mm(f"mlp_down_dx{layer}", dhb, 'mlp_down', layer,
                      lambda acc, uu: (acc * (2.0 * jnp.maximum(uu.astype(F32), 0.0)),), (u,), (BF16,))
        dw_mm(f"mlp_down_dw{layer}", a, dhb, 'mlp_down', layer)
        (dn2,) = dx_mm(f"mlp_up_dx{layer}", du, 'mlp_up', layer, lambda acc: (acc,), (), (F32,))
        dw_mm(f"mlp_up_dw{layer}", n2, du, 'mlp_up', layer)
        dh2, dh2b, dg, cs = rms_bwd(f"mlp_norm_bwd{layer}", h_mid, small['mlp_norm'][layer], dn2, dh)
        sg['mlp_norm'][layer] = dg[0]
        return dh2, dh2b, cs

    dkp = jnp.zeros(kpt.shape, F32)
    dvp = jnp.zeros(vpt.shape, F32)
    sg['b_norm'], sg['b_bq'], sg['b_bo'], sg['b_sinks'] = [None] * n_b, [None] * n_b, [None] * n_b, [None] * n_b
    dbias = [None] * n_b
    for j in reversed(range(n_b)):
        layer = n_a + j
        h_in, n1, qbt, o, h_mid, mlp_saved = saved[layer]
        dh, dhb, cs = mlp_bwd(dh, dhb, h_mid, mlp_saved, layer)
        sg['b_bo'][j] = cs[0]
        (do,) = dx_mm(f"b_wo_dx{j}", dhb, 'b_wo', j, lambda acc: (acc,), (), (BF16,))
        dw_mm(f"b_wo_dw{j}", o, dhb, 'b_wo', j)
        dq_t, dkp, dvp, dbias[j], dsink = swa_bwd(f"swa_bwd{j}", qbt, kpt, vpt, bias_t, sink_rows[j], do.T, dkp, dvp)
        sg['b_sinks'][j] = colsum(f"sink_grad{j}", dsink.reshape(16, WINDOW).T)[0]
        dq = dq_t.T
        sg['b_bq'][j] = colsum(f"b_bq_grad{j}", dq)[0]
        (dn1,) = dx_mm(f"b_q_dx{j}", dq, 'b_wq', j, lambda acc: (acc,), (), (F32,))
        dw_mm(f"b_q_dw{j}", n1, dq, 'b_wq', j)
        dh, dhb, dg, _ = rms_bwd(f"b_norm_bwd{j}", h_in, small['b_norm'][j], dn1, dh)
        sg['b_norm'][j] = dg[0]
    unt = lambda t: t.reshape(2, 2 * WINDOW, 8, WINDOW).transpose(0, 2, 3, 1).reshape(bias_flat.shape)
    sg['rel_bias'] = bias_table_grad(unt(dbias[0]), unt(dbias[1]))[:, :N_BUCKETS].T

    dkv = jnp.concatenate([dkp[:, :, WINDOW:], dvp[:, :, WINDOW:]], axis=0).reshape(-1, s).T
    sg['b_kv'] = colsum("b_kv_grad", dkv)[0]
    dkvb = dkv.astype(BF16)
    (dnkv,) = dx_mm("kv_dx", dkvb, 'w_kv', 0, lambda acc: (acc,), (), (F32,))
    dw_mm("kv_dw", nkv, dkvb, 'w_kv', 0)
    dh, dhb, dg, _ = rms_bwd("kv_norm_bwd", h_kv, small['kv_norm'], dnkv, dh)
    sg['kv_norm'] = dg[0]

    sg['a_norm'] = [None] * n_a
    for l in reversed(range(n_a)):
        h_in, n1, qkvt, o, rtab, h_mid, mlp_saved = saved[l]
        dh, dhb, _ = mlp_bwd(dh, dhb, h_mid, mlp_saved, l)
        (do,) = dx_mm(f"a_wo_dx{l}", dhb, 'a_wo', l, lambda acc: (acc,), (), (BF16,))
        dw_mm(f"a_wo_dw{l}", o, dhb, 'a_wo', l)
        dq_t, dk_t, dv_t, carried = sb_bwd(f"sb_bwd{l}", qkvt, do.T.reshape(d // HEAD_DIM, HEAD_DIM, s), rtab,
                                           ex.bwd_carry(l, gb))
        ex.bwd_done(l, carried)
        dqkv = jnp.concatenate([dq_t, dk_t, dv_t], axis=0).reshape(3 * d, s).T
        (dn1,) = dx_mm(f"a_qkv_dx{l}", dqkv, 'a_wqkv', l, lambda acc: (acc,), (), (F32,))
        dw_mm(f"a_qkv_dw{l}", n1, dqkv, 'a_wqkv', l)
        dh, dhb, dg, _ = rms_bwd(f"a_norm_bwd{l}", h_in, small['a_norm'][l], dn1, dh)
        sg['a_norm'][l] = dg[0]

    small_grads = {
        'a_norm': jnp.stack(sg['a_norm']), 'kv_norm': sg['kv_norm'], 'b_kv': sg['b_kv'],
        'b_norm': jnp.stack(sg['b_norm']), 'b_bq': jnp.stack(sg['b_bq']), 'b_sinks': jnp.stack(sg['b_sinks']),
        'b_bo': jnp.stack(sg['b_bo']), 'rel_bias': sg['rel_bias'], 'mlp_norm': jnp.stack(sg['mlp_norm']),
        'final_norm': sg['final_norm'],
    }
    return loss_b, dh, gb, small_grads


def _full_shape(name, shard_shape):
    if name in COL_SHARDED:
        return shard_shape[:2] + (N_DEV * shard_shape[2],)
    nl, r, n = shard_shape
    return (nl, N_DEV, r, n)


def _as_w3(name, full):
    if name in COL_SHARDED:
        return full
    nl, nd, r, n = full.shape
    return full.reshape(nl, nd * r, n)


LAYERED = ('a_wqkv', 'a_wo', 'mlp_up', 'mlp_down')
RS_GROUPS = {
    'A': (('mlp_up', 1, 3), ('mlp_down', 1, 3), ('b_wq', 0, 2), ('b_wo', 0, 2), ('w_kv', 0, 1)),
    'B': (('a_wqkv', 1, 1), ('a_wo', 1, 1), ('mlp_up', 0, 1), ('mlp_down', 0, 1)),
    'C': (('a_wqkv', 0, 1), ('a_wo', 0, 1)),
}


class _Exchanges:
    def __init__(self, full0, shards1, core, chip, w3, m3, v3):
        self.wbuf = {0: {n: _as_w3(n, full0[n]) for n in LAYERED}}
        self.shards1, self.core, self.chip = shards1, core, chip
        self.w3, self.m3, self.v3 = w3, m3, v3
        self.shard_dims = {n: w3[n].shape[1:] for n in BIG}
        self.parts = {}
        self.out = {}

    def weight(self, name, layer):
        if name in LAYERED:
            return (self.wbuf[0][name], 0) if layer == 0 else (self.wbuf[1][name], layer - 1)
        return self.wbuf[1][name], layer

    def fwd_carry(self, layer):
        if layer != 0:
            return None
        shards = [self.shards1[n] for n in BIG]
        return ag_direct(BIG, shards, [_full_shape(n, sh.shape) for n, sh in zip(BIG, shards)])

    def fwd_done(self, layer, carried):
        if layer == 0:
            self.wbuf[1] = {n: _as_w3(n, f) for n, f in zip(BIG, ag_forward(BIG, list(carried)))}

    def grad(self, name, layer):
        for group, members in RS_GROUPS.items():
            for n, l0, nl in members:
                if n == name and l0 <= layer < l0 + nl:
                    k_full, n_full = self.wbuf[0 if name in LAYERED else 1][name].shape[1:]
                    return (group, name), layer - l0, (nl, k_full, n_full)
        raise KeyError((name, layer))

    def _sibling_stage(self, group, gb):
        names = [n for n, _, _ in RS_GROUPS[group]]
        shapes = [(nl,) + self.shard_dims[n] for n, _, nl in RS_GROUPS[group]]
        gfull = [gb[(group, n)].reshape(_full_shape(n, sh)) for n, sh in zip(names, shapes)]
        recv = sibling_exchange(group, names, gfull, shapes)
        self.parts[group] = [sibling_sum(f"rs_sibling_sum_{group}_{n}", n in COL_SHARDED, g, r, self.core)
                             for n, g, r in zip(names, gfull, recv)]
        return names, shapes

    def bwd_carry(self, layer, gb):
        names, shapes = self._sibling_stage('A' if layer == 1 else 'B', gb)
        return chip_exchange(names, self.parts['A' if layer == 1 else 'B'], shapes)

    def bwd_done(self, layer, carried):
        self._adamw('A' if layer == 1 else 'B', carried)

    def finish(self, gb):
        names, shapes = self._sibling_stage('C', gb)
        ce = chip_exchange(names, self.parts['C'], shapes)
        self._adamw('C', comm_call("rs_chip_exchange_C", ce.build, ce.ins, ce.out_shapes, ce.sems))
        return self.out

    def _adamw(self, group, recv2):
        for (n, l0, _), p, r in zip(RS_GROUPS[group], self.parts[group], recv2):
            self.out[n] = reduce_adamw(f"adamw_{group}_{n}", p, r, self.chip, self.w3[n], self.m3[n], self.v3[n],
                                       l0, self.out.get(n))


def _pack_small(vals):
    flat = jnp.concatenate([vals[n].reshape(-1).astype(F32) for n in SMALL] + [vals['loss'].reshape(-1)])
    rows = -(-flat.shape[0] // 1024) * 8
    return jnp.pad(flat, (0, rows * 128 - flat.shape[0])).reshape(rows, 128)


def _unpack_small(packed, shapes):
    flat = packed.reshape(-1)
    out, off = {}, 0
    for n in SMALL + ['loss']:
        size = int(np.prod(shapes[n]))
        out[n] = flat[off:off + size].reshape(shapes[n])
        off += size
    return out


def kernel(x, a_norm, a_wqkv, a_wo, kv_norm, w_kv, b_kv, b_norm, b_wq, b_bq, b_sinks, b_wo, b_bo, rel_bias, mlp_norm, mlp_up, mlp_down, final_norm, loss_target, m_a_norm, m_a_wqkv, m_a_wo, m_kv_norm, m_w_kv, m_b_kv, m_b_norm, m_b_wq, m_b_bq, m_b_sinks, m_b_wo, m_b_bo, m_rel_bias, m_mlp_norm, m_mlp_up, m_mlp_down, m_final_norm, v_a_norm, v_a_wqkv, v_a_wo, v_kv_norm, v_w_kv, v_b_kv, v_b_norm, v_b_wq, v_b_bq, v_b_sinks, v_b_wo, v_b_bo, v_rel_bias, v_mlp_norm, v_mlp_up, v_mlp_down, v_final_norm):
    w = dict(a_norm=a_norm, a_wqkv=a_wqkv, a_wo=a_wo, kv_norm=kv_norm, w_kv=w_kv, b_kv=b_kv, b_norm=b_norm,
             b_wq=b_wq, b_bq=b_bq, b_sinks=b_sinks, b_wo=b_wo, b_bo=b_bo, rel_bias=rel_bias, mlp_norm=mlp_norm,
             mlp_up=mlp_up, mlp_down=mlp_down, final_norm=final_norm)
    m = dict(a_norm=m_a_norm, a_wqkv=m_a_wqkv, a_wo=m_a_wo, kv_norm=m_kv_norm, w_kv=m_w_kv, b_kv=m_b_kv,
             b_norm=m_b_norm, b_wq=m_b_wq, b_bq=m_b_bq, b_sinks=m_b_sinks, b_wo=m_b_wo, b_bo=m_b_bo,
             rel_bias=m_rel_bias, mlp_norm=m_mlp_norm, mlp_up=m_mlp_up, mlp_down=m_mlp_down, final_norm=m_final_norm)
    v = dict(a_norm=v_a_norm, a_wqkv=v_a_wqkv, a_wo=v_a_wo, kv_norm=v_kv_norm, w_kv=v_w_kv, b_kv=v_b_kv,
             b_norm=v_b_norm, b_wq=v_b_wq, b_bq=v_b_bq, b_sinks=v_b_sinks, b_wo=v_b_wo, b_bo=v_b_bo,
             rel_bias=v_rel_bias, mlp_norm=v_mlp_norm, mlp_up=v_mlp_up, mlp_down=v_mlp_down, final_norm=v_final_norm)
    px, py, pc = _place()
    me = 4 * px + 2 * py + pc
    chip = (2 * px + py).astype(jnp.int32)
    core = pc.astype(jnp.int32)

    as3 = lambda t: t[None] if t.ndim == 2 else t
    w3, m3, v3 = ({n: as3(src[n]) for n in BIG} for src in (w, m, v))
    shards0 = {n: w3[n][:1].astype(BF16) for n in LAYERED}
    shards1 = {n: (w3[n][1:] if n in LAYERED else w3[n]).astype(BF16) for n in BIG}
    an_pad = jnp.zeros((8, 128), F32).at[:a_norm.shape[0]].set(a_norm)
    names0 = list(LAYERED) + ['a_norm']
    full0 = all_gather_weights(names0, [shards0[n] for n in LAYERED] + [an_pad],
                               [_full_shape(n, shards0[n].shape) for n in LAYERED] + [(N_DEV, 8, 128)])
    full0 = dict(zip(names0, full0))
    n_a = a_norm.shape[0]
    small = {n: w[n] for n in SMALL}
    small['a_norm'] = full0['a_norm'][:, :n_a].transpose(1, 0, 2).reshape(n_a, -1)

    ex = _Exchanges(full0, shards1, core, chip, w3, m3, v3)
    loss_b, grad_x, gb, sgrads = local_step(x[0], loss_target[0], small, ex)
    out = {n: [t.reshape(w[n].shape) for t in bufs] for n, bufs in ex.finish(gb).items()}

    sgrads['loss'] = loss_b[0, :1]
    gathered = all_gather_rows(_pack_small(sgrads))
    shapes = {n: w[n].shape for n in SMALL}
    shapes['a_norm'] = (n_a, a_norm.shape[1] * N_DEV)
    shapes['loss'] = (1,)
    zeros1 = jnp.zeros((1,), F32)

    def packed(src):
        vals = {n: src[n] for n in SMALL}
        vals['a_norm'] = jnp.zeros(shapes['a_norm'], F32)
        vals['loss'] = zeros1
        return _pack_small(vals)

    sm = small_adamw("adamw_small", gathered, packed(w), packed(m), packed(v))
    sm = [_unpack_small(t, shapes) for t in sm]
    g_an = lax.dynamic_slice_in_dim(sm[0]['a_norm'], me * a_norm.shape[1], a_norm.shape[1], axis=1)
    pad = lambda t: jnp.zeros((8, 128), F32).at[:n_a].set(t)
    gathered_an = jnp.zeros((N_DEV, 8, 128), F32).at[0].set(pad(g_an))
    an = small_adamw("adamw_a_norm", gathered_an, pad(a_norm), pad(m_a_norm), pad(v_a_norm))
    for i in range(4):
        sm[i]['a_norm'] = an[i][:n_a]
    for n in BIG:
        for i in range(4):
            sm[i][n] = out[n][i]
    loss = sm[0]['loss'][0]
    return (loss, grad_x[None], *[sm[0][n] for n in WEIGHTS], *[sm[1][n] for n in WEIGHTS],
            *[sm[2][n] for n in WEIGHTS], *[sm[3][n] for n in WEIGHTS])
```

```python
import functools
import math

import numpy as np
import jax
import jax.numpy as jnp
from jax import lax
from jax.experimental import pallas as pl
from jax.experimental.pallas import tpu as pltpu

F32 = jnp.float32
BF16 = jnp.bfloat16
MESH = pl.DeviceIdType.MESH

N_DEV = 8
HEAD_DIM = 64
WINDOW = 128
N_BUCKETS = 32
EPS = 1e-5
NEG_INF = -1e30
Q_SCALE = 1.0 / math.sqrt(HEAD_DIM)
LOG2E = 1.4426950408889634

ADAM_LR, ADAM_B1, ADAM_B2, ADAM_EPS, ADAM_WD, ADAM_STEP = 0.001, 0.9, 0.999, 1e-08, 0.01, 10

SB_BQ = 512
SB_BK = 128
SB_DEAD = 160.0
SB_UNSEEN = 1e30
ROW_TILE = 512
VMEM_LIMIT = 56 * 1024 * 1024

WEIGHTS = ['a_norm', 'a_wqkv', 'a_wo', 'kv_norm', 'w_kv', 'b_kv', 'b_norm', 'b_wq', 'b_bq', 'b_sinks', 'b_wo',
           'b_bo', 'rel_bias', 'mlp_norm', 'mlp_up', 'mlp_down', 'final_norm']
BIG = ['a_wqkv', 'a_wo', 'w_kv', 'b_wq', 'b_wo', 'mlp_up', 'mlp_down']
COL_SHARDED = ('a_wqkv', 'mlp_up')
SMALL = ['a_norm', 'kv_norm', 'b_kv', 'b_norm', 'b_bq', 'b_sinks', 'b_bo', 'rel_bias', 'mlp_norm', 'final_norm']


def _params(sem=None):
    return pltpu.CompilerParams(dimension_semantics=sem, vmem_limit_bytes=VMEM_LIMIT)


def _pick(n, cands):
    for c in cands:
        if n % c == 0:
            return c
    raise ValueError(n)


def _tile(n, want):
    return n if n <= want else _pick(n, (want, want // 2, want // 4))


def mm_nn(name, a, w3, layer, epilogue, extras, out_dtypes):
    m, k = a.shape
    _, kw, n = w3.shape
    assert kw == k
    tm = _tile(m, 512)
    tn = _tile(n, 1024)
    ne, no = len(extras), len(out_dtypes)

    def body(a_ref, w_ref, *rest):
        ex, outs = rest[:ne], rest[ne:ne + no]
        res = epilogue(jnp.dot(a_ref[...], w_ref[...], preferred_element_type=F32), *[e[...] for e in ex])
        for o, r in zip(outs, res):
            o[...] = r.astype(o.dtype)

    tile = pl.BlockSpec((tm, tn), lambda i, j: (i, j))
    ex_specs = [tile if e.shape[0] == m else pl.BlockSpec((1, tn), lambda i, j: (0, j)) for e in extras]
    return pl.pallas_call(
        body, name=name, grid=(m // tm, n // tn),
        in_specs=[pl.BlockSpec((tm, k), lambda i, j: (i, 0)),
                  pl.BlockSpec((None, k, tn), lambda i, j: (layer, 0, j))] + ex_specs,
        out_specs=[tile] * no,
        out_shape=[jax.ShapeDtypeStruct((m, n), d) for d in out_dtypes],
        compiler_params=_params(("parallel", "parallel")),
    )(a, w3, *extras)


def mm_nt(name, dy, w3, layer, epilogue, extras, out_dtypes):
    m, n = dy.shape
    _, k, nw = w3.shape
    assert nw == n
    tm = _tile(m, 512)
    tko = _tile(k, 1024)
    ne, no = len(extras), len(out_dtypes)

    def body(a_ref, w_ref, *rest):
        ex, outs = rest[:ne], rest[ne:ne + no]
        acc = lax.dot_general(a_ref[...], w_ref[...], (((1,), (1,)), ((), ())), preferred_element_type=F32)
        res = epilogue(acc, *[e[...] for e in ex])
        for o, v in zip(outs, res):
            o[...] = v.astype(o.dtype)

    tile = pl.BlockSpec((tm, tko), lambda i, ko: (i, ko))
    return pl.pallas_call(
        body, name=name, grid=(m // tm, k // tko),
        in_specs=[pl.BlockSpec((tm, n), lambda i, ko: (i, 0)),
                  pl.BlockSpec((None, tko, n), lambda i, ko: (layer, ko, 0))] + [tile] * ne,
        out_specs=[tile] * no,
        out_shape=[jax.ShapeDtypeStruct((m, k), d) for d in out_dtypes],
        compiler_params=_params(("parallel", "parallel")),
    )(dy, w3, *extras)


def mm_tn(name, x, dy, gbuf, shape, layer):
    s, k = x.shape
    _, kw, n = shape
    assert kw == k and dy.shape == (s, n)
    tkk = _tile(k, 512)
    tn = _tile(n, 1024)

    def body(x_ref, dy_ref, *rest):
        g_out = rest[-1]
        g_out[...] = lax.dot_general(x_ref[...], dy_ref[...], (((0,), (0,)), ((), ())),
                                     preferred_element_type=F32).astype(g_out.dtype)

    prev = [] if gbuf is None else [gbuf]
    return pl.pallas_call(
        body, name=name, grid=(k // tkk, n // tn),
        in_specs=[pl.BlockSpec((s, tkk), lambda ki, j: (0, ki)),
                  pl.BlockSpec((s, tn), lambda ki, j: (0, j))] + [pl.BlockSpec(memory_space=pl.ANY)] * len(prev),
        out_specs=pl.BlockSpec((None, tkk, tn), lambda ki, j: (layer, ki, j)),
        out_shape=jax.ShapeDtypeStruct(shape, BF16),
        input_output_aliases={2: 0} if prev else {},
        compiler_params=_params(("parallel", "parallel")),
    )(x, dy, *prev)


def rms_fwd(name, h, g):
    s, d = h.shape
    tr = _pick(s, (ROW_TILE, 256, 128))

    def body(h_ref, g_ref, o_ref):
        x = h_ref[...]
        r = lax.rsqrt(jnp.mean(x * x, axis=-1, keepdims=True) + EPS)
        o_ref[...] = (x * r * g_ref[...]).astype(o_ref.dtype)

    return pl.pallas_call(
        body, name=name, grid=(s // tr,),
        in_specs=[pl.BlockSpec((tr, d), lambda i: (i, 0)), pl.BlockSpec((1, d), lambda i: (0, 0))],
        out_specs=pl.BlockSpec((tr, d), lambda i: (i, 0)),
        out_shape=jax.ShapeDtypeStruct((s, d), BF16),
        compiler_params=_params(("parallel",)),
    )(h, g.reshape(1, d))


def rms_bwd(name, h, g, dn, dres):
    s, d = h.shape
    tr = _pick(s, (ROW_TILE, 256, 128))

    def body(h_ref, g_ref, dn_ref, dres_ref, dx_ref, dxb_ref, dg_ref, cs_ref):
        i = pl.program_id(0)
        x = h_ref[...]
        r = lax.rsqrt(jnp.mean(x * x, axis=-1, keepdims=True) + EPS)
        xh = x * r
        dn_ = dn_ref[...]
        dyg = dn_ * g_ref[...]
        dx = dres_ref[...] + r * (dyg - xh * jnp.mean(dyg * xh, axis=-1, keepdims=True))
        dx_ref[...] = dx
        dxb_ref[...] = dx.astype(BF16)

        @pl.when(i == 0)
        def _():
            dg_ref[...] = jnp.zeros_like(dg_ref)
            cs_ref[...] = jnp.zeros_like(cs_ref)

        dg_ref[...] += jnp.sum(dn_ * xh, axis=0, keepdims=True)
        cs_ref[...] += jnp.sum(dx, axis=0, keepdims=True)

    row = pl.BlockSpec((tr, d), lambda i: (i, 0))
    vec = pl.BlockSpec((1, d), lambda i: (0, 0))
    return pl.pallas_call(
        body, name=name, grid=(s // tr,),
        in_specs=[row, vec, row, row],
        out_specs=[row, row, vec, vec],
        out_shape=[jax.ShapeDtypeStruct((s, d), F32), jax.ShapeDtypeStruct((s, d), BF16),
                   jax.ShapeDtypeStruct((1, d), F32), jax.ShapeDtypeStruct((1, d), F32)],
        compiler_params=_params(("arbitrary",)),
    )(h, g.reshape(1, d), dn, dres)


def loss_head(h, g, target):
    s, d = h.shape
    tr = _pick(s, (ROW_TILE, 256, 128))

    def body(h_ref, g_ref, t_ref, dx_ref, dxb_ref, dg_ref, loss_ref):
        i = pl.program_id(0)
        x = h_ref[...]
        r = lax.rsqrt(jnp.mean(x * x, axis=-1, keepdims=True) + EPS)
        xh = x * r
        gw = g_ref[...]
        err = xh * gw - t_ref[...]
        dn_ = err * (1.0 / d)
        dyg = dn_ * gw
        dx = r * (dyg - xh * jnp.mean(dyg * xh, axis=-1, keepdims=True))
        dx_ref[...] = dx
        dxb_ref[...] = dx.astype(BF16)

        @pl.when(i == 0)
        def _():
            dg_ref[...] = jnp.zeros_like(dg_ref)
            loss_ref[...] = jnp.zeros_like(loss_ref)

        dg_ref[...] += jnp.sum(dn_ * xh, axis=0, keepdims=True)
        per_row = jnp.sum(err * err, axis=-1, keepdims=True) * (0.5 / d)
        loss_ref[...] += jnp.broadcast_to(jnp.sum(per_row, axis=0, keepdims=True), loss_ref.shape)

    row = pl.BlockSpec((tr, d), lambda i: (i, 0))
    vec = pl.BlockSpec((1, d), lambda i: (0, 0))
    return pl.pallas_call(
        body, name="loss_head", grid=(s // tr,),
        in_specs=[row, vec, row],
        out_specs=[row, row, vec, pl.BlockSpec((1, 128), lambda i: (0, 0))],
        out_shape=[jax.ShapeDtypeStruct((s, d), F32), jax.ShapeDtypeStruct((s, d), BF16),
                   jax.ShapeDtypeStruct((1, d), F32), jax.ShapeDtypeStruct((1, 128), F32)],
        compiler_params=_params(("arbitrary",)),
    )(h, g.reshape(1, d), target)


def colsum(name, x):
    s, n = x.shape
    tr = _pick(s, (ROW_TILE, 256, 128))

    def body(x_ref, o_ref):
        @pl.when(pl.program_id(0) == 0)
        def _():
            o_ref[...] = jnp.zeros_like(o_ref)

        o_ref[...] += jnp.sum(x_ref[...].astype(F32), axis=0, keepdims=True)

    return pl.pallas_call(
        body, name=name, grid=(s // tr,),
        in_specs=[pl.BlockSpec((tr, n), lambda i: (i, 0))],
        out_specs=pl.BlockSpec((1, n), lambda i: (0, 0)),
        out_shape=jax.ShapeDtypeStruct((1, n), F32),
        compiler_params=_params(("arbitrary",)),
    )(x)


def _tri_rows(reverse):
    i = np.arange(SB_BK)
    tri = (i[None, :] >= i[:, None]) if reverse else (i[None, :] <= i[:, None])
    tri = np.concatenate([tri, tri], axis=1)
    return jnp.asarray(np.concatenate([tri, np.ones((8, 2 * SB_BK), bool)], axis=0), BF16)


def _hi_lo_rows(x):
    hi = x.astype(BF16)
    lo = (x - hi.astype(F32)).astype(BF16)
    return jnp.concatenate([hi, lo], axis=0)


def _softplus2(zs):
    neg_abs = lax.bitcast_convert_type(lax.bitcast_convert_type(zs, jnp.uint32) | jnp.uint32(0x80000000), F32)
    return jnp.maximum(zs, 0.0) + jnp.log2(1.0 + jnp.exp2(neg_abs))


def _pair_mask(first_rel_block, bq):
    key = lax.broadcasted_iota(jnp.int32, (2 * SB_BK, bq), 0) + first_rel_block * SB_BK
    qry = lax.broadcasted_iota(jnp.int32, (2 * SB_BK, bq), 1)
    return key < qry


def _row_of(table8, sub8, r):
    return jnp.sum(jnp.where(sub8 == r, table8, 0.0), axis=0, keepdims=True)


def _keys(j0):
    return pl.ds(pl.multiple_of(j0 * SB_BK, 2 * SB_BK), 2 * SB_BK)


class Carry:
    def __init__(self, build, ins, out_shapes, sems):
        self.build, self.ins, self.out_shapes, self.sems = build, list(ins), list(out_shapes), list(sems)


def _carried(carry, rest, n_out, n_scratch, first, last):
    n_ci = len(carry.ins) if carry else 0
    n_co = len(carry.out_shapes) if carry else 0
    cin, outs = rest[:n_ci], rest[n_ci:n_ci + n_out]
    cout = rest[n_ci + n_out:n_ci + n_out + n_co]
    scratch = rest[n_ci + n_out + n_co:n_ci + n_out + n_co + n_scratch]
    csems = rest[n_ci + n_out + n_co + n_scratch:]

    def start():
        if carry:
            @pl.when(first)
            def _():
                for cp in carry.build(cin, cout, *csems):
                    cp.start()

    def wait():
        if carry:
            @pl.when(last)
            def _():
                for cp in carry.build(cin, cout, *csems):
                    cp.wait()

    return outs, scratch, start, wait


def _contract0(a, b):
    return lax.dot_general(a, b, (((0,), (0,)), ((), ())), preferred_element_type=F32)


def _contract1(a, b):
    return lax.dot_general(a, b, (((1,), (1,)), ((), ())), preferred_element_type=F32)


def sb_fwd(name, qkvt, exchange=None):
    nh, dh, s = qkvt.shape[0] // 3, qkvt.shape[1], qkvt.shape[2]
    bq = SB_BQ
    per_q = bq // SB_BK
    nkb = s // SB_BK
    assert s % bq == 0 and per_q == 4 and nkb % 8 == 0

    def body(q_ref, k_ref, v_ref, a_ref, *rest):
        i = pl.program_id(1)
        first = (pl.program_id(0) == 0) & (i == 0)
        last = (pl.program_id(0) == nh - 1) & (i == s // bq - 1)
        (o_ref, rtab_ref), (acc, zbuf, wbuf), start_carried, wait_carried = _carried(exchange, rest, 2, 3, first, last)
        start_carried()
        qb = q_ref[...] * Q_SCALE
        tri = a_ref[...]
        sub8 = lax.broadcasted_iota(jnp.int32, (8, bq), 0)
        acc[...] = jnp.zeros_like(acc)
        rtab_ref[...] = jnp.full(rtab_ref.shape, SB_UNSEEN, F32)
        kf = k_ref[...].astype(F32)
        qf = qb.astype(F32)
        k_max2 = jnp.max(jnp.sum(kf * kf, axis=0, keepdims=True), axis=1, keepdims=True)
        bound = jnp.sqrt(jnp.sum(qf * qf, axis=0, keepdims=True) * k_max2) * (1.001 * LOG2E)

        def scores(j0):
            return _contract0(k_ref[:, _keys(j0)], qb) * LOG2E

        def pair(j0, slot, run, rt8, mask, has_prev):
            zs = zbuf[slot]
            zbuf[1 - slot] = scores(jnp.maximum(j0 - 2, 0))
            if has_prev:
                acc[...] += jnp.dot(v_ref[:, _keys(j0 + 2)], wbuf[1 - slot], preferred_element_type=F32)
            p = _softplus2(zs)
            if mask is not None:
                p = jnp.where(mask, p, 0.0)
            cr1 = jnp.dot(tri, _hi_lo_rows(p[SB_BK:]), preferred_element_type=F32)
            cr0 = jnp.dot(tri, _hi_lo_rows(p[:SB_BK]), preferred_element_type=F32)
            run1 = run + cr1[SB_BK:SB_BK + 1]
            w = jnp.exp2(jnp.concatenate([zs[:SB_BK] - cr0[:SB_BK] - run1, zs[SB_BK:] - cr1[:SB_BK] - run], axis=0))
            if mask is not None:
                w = jnp.where(mask, w, 0.0)
            wbuf[slot] = w.astype(BF16)
            rt8 = jnp.where(j0 % 8 == 6, SB_UNSEEN, rt8)
            rt8 = jnp.where(sub8 == (j0 + 1) % 8, run, jnp.where(sub8 == j0 % 8, run1, rt8))
            rtab_ref[pl.ds(pl.multiple_of((j0 // 8) * 8, 8), 8), :] = rt8
            return run1 + cr0[SB_BK:SB_BK + 1], rt8

        def alive(run):
            return jnp.min(run - bound) < SB_DEAD

        top = i * per_q
        zbuf[0] = scores(top + 2)
        state = (jnp.zeros((1, bq), F32), jnp.full((8, bq), SB_UNSEEN, F32))
        state = pair(top + 2, 0, *state, _pair_mask(2, bq), False)
        state = pair(top, 1, *state, _pair_mask(0, bq), True)

        def step(c):
            it, _, run, rt8 = c
            j0 = top - 2 - 4 * it
            run, rt8 = pair(j0, 0, run, rt8, None, True)
            run, rt8 = pair(j0 - 2, 1, run, rt8, None, True)
            return it + 1, alive(run), run, rt8

        trips = lax.while_loop(lambda c: (c[0] < i) & c[1], step, (0, alive(state[0]), *state))[0]
        acc[...] += jnp.dot(v_ref[:, _keys(top - 4 * trips)], wbuf[1], preferred_element_type=F32)
        o_ref[...] = acc[...].astype(o_ref.dtype)
        wait_carried()

    qspec = pl.BlockSpec((None, dh, bq), lambda h, i: (h, 0, i))
    hbm = pl.BlockSpec(memory_space=pl.ANY)
    c_ins, c_outs, c_sems = (exchange.ins, exchange.out_shapes, exchange.sems) if exchange else ([], [], [])
    outs = pl.pallas_call(
        body, name=name, grid=(nh, s // bq),
        in_specs=[qspec, pl.BlockSpec((None, dh, s), lambda h, i: (h + nh, 0, 0)),
                  pl.BlockSpec((None, dh, s), lambda h, i: (h + 2 * nh, 0, 0)),
                  pl.BlockSpec((SB_BK + 8, 2 * SB_BK), lambda h, i: (0, 0))] + [hbm] * len(c_ins),
        out_specs=[qspec, pl.BlockSpec((None, nkb, bq), lambda h, i: (h, 0, i))] + [hbm] * len(c_outs),
        out_shape=[jax.ShapeDtypeStruct((nh, dh, s), BF16), jax.ShapeDtypeStruct((nh, nkb, s), F32)] + c_outs,
        scratch_shapes=[pltpu.VMEM((dh, bq), F32), pltpu.VMEM((2, 2 * SB_BK, bq), F32),
                        pltpu.VMEM((2, 2 * SB_BK, bq), BF16)] + c_sems,
        compiler_params=_params(("arbitrary", "arbitrary")),
    )(qkvt, qkvt, qkvt, _tri_rows(True), *c_ins)
    return outs[0], outs[1], outs[2:]


def sb_bwd(name, qkvt, dot_, rtab, exchange=None):
    nh, dh, s = qkvt.shape[0] // 3, qkvt.shape[1], qkvt.shape[2]
    bq = SB_BQ
    per_q = bq // SB_BK
    nkb = s // SB_BK
    nq = s // bq

    def body(qt_ref, kt_ref, vt_ref, dot_ref, rtab_ref, ar_ref, af_ref, *rest):
        i = pl.program_id(1)
        first = (pl.program_id(0) == 0) & (i == 0)
        last = (pl.program_id(0) == nh - 1) & (i == nq - 1)
        (dq_ref, dk_ref, dv_ref), (dq_acc, dk_acc, dv_acc, zbuf, dwbuf, dzbuf, wbuf), start_carried, wait_carried = \
            _carried(exchange, rest, 3, 7, first, last)
        start_carried()

        @pl.when(i == 0)
        def _():
            dk_acc[...] = jnp.zeros_like(dk_acc)
            dv_acc[...] = jnp.zeros_like(dv_acc)

        qtb = qt_ref[...] * Q_SCALE
        dotb = dot_ref[...]
        tri_rev = ar_ref[...][:SB_BK]
        tri_fwd = af_ref[...]
        sub8 = lax.broadcasted_iota(jnp.int32, (8, bq), 0)
        dq_acc[...] = jnp.zeros_like(dq_acc)
        last_j = i * per_q + 2
        seen = jnp.max(jnp.where(rtab_ref[...] < 0.1 * SB_UNSEEN, 1.0, 0.0), axis=1, keepdims=True)
        trips = jnp.clip((jnp.sum(seen).astype(jnp.int32) - per_q) // per_q, 0, i)
        first_j = (i - trips) * per_q

        def issue(j0, slot):
            zbuf[slot] = _contract0(kt_ref[:, _keys(j0)], qtb) * LOG2E
            dwbuf[slot] = _contract0(vt_ref[:, _keys(j0)], dotb)

        def retire(j0, slot):
            keys = _keys(j0)
            dq_acc[...] += jnp.dot(kt_ref[:, keys], dzbuf[slot], preferred_element_type=F32)
            dk_acc[:, keys] += _contract1(qtb, dzbuf[slot])
            dv_acc[:, keys] += _contract1(dotb, wbuf[slot])

        def pair(j0, slot, g_run, mask):
            zs = zbuf[slot]
            dw = dwbuf[slot]
            issue(jnp.minimum(j0 + 2, last_j), 1 - slot)
            retire(jnp.maximum(j0 - 2, first_j), 1 - slot)
            p_raw = _softplus2(zs)
            p = p_raw if mask is None else jnp.where(mask, p_raw, 0.0)
            c0 = jnp.dot(tri_rev, _hi_lo_rows(p[:SB_BK]), preferred_element_type=F32)
            c1 = jnp.dot(tri_rev, _hi_lo_rows(p[SB_BK:]), preferred_element_type=F32)
            rt8 = rtab_ref[pl.ds(pl.multiple_of((j0 // 8) * 8, 8), 8), :]
            r0 = _row_of(rt8, sub8, j0 % 8)
            r1 = _row_of(rt8, sub8, (j0 + 1) % 8)
            w = jnp.exp2(jnp.concatenate([zs[:SB_BK] - c0 - r0, zs[SB_BK:] - c1 - r1], axis=0))
            if mask is not None:
                w = jnp.where(mask, w, 0.0)
            g = w * dw
            gg0 = jnp.dot(tri_fwd, _hi_lo_rows(g[:SB_BK]), preferred_element_type=F32)
            gg1 = jnp.dot(tri_fwd, _hi_lo_rows(g[SB_BK:]), preferred_element_type=F32)
            g_run1 = g_run + gg0[SB_BK:SB_BK + 1]
            g_pre = jnp.concatenate([gg0[:SB_BK] + g_run, gg1[:SB_BK] + g_run1], axis=0)
            dz = g - jnp.exp2(zs - p_raw) * g_pre
            if mask is not None:
                dz = jnp.where(mask, dz, 0.0)
            dzbuf[slot] = dz.astype(BF16)
            wbuf[slot] = w.astype(BF16)
            return g_run1 + gg1[SB_BK:SB_BK + 1]

        issue(first_j, 0)
        dzbuf[1] = jnp.zeros((2 * SB_BK, bq), BF16)
        wbuf[1] = jnp.zeros((2 * SB_BK, bq), BF16)

        def step(it, g_run):
            g_run = pair(4 * it, 0, g_run, None)
            return pair(4 * it + 2, 1, g_run, None)

        g_run = lax.fori_loop(i - trips, i, step, jnp.zeros((1, bq), F32))
        g_run = pair(last_j - 2, 0, g_run, _pair_mask(0, bq))
        pair(last_j, 1, g_run, _pair_mask(2, bq))
        retire(last_j, 1)
        dq_ref[...] = (dq_acc[...] * Q_SCALE).astype(dq_ref.dtype)

        @pl.when(i == nq - 1)
        def _():
            dk_ref[...] = dk_acc[...].astype(dk_ref.dtype)
            dv_ref[...] = dv_acc[...].astype(dv_ref.dtype)

        wait_carried()

    tspec = pl.BlockSpec((None, dh, bq), lambda h, i: (h, 0, i))
    fullspec = pl.BlockSpec((None, dh, s), lambda h, i: (h, 0, 0))
    aspec = pl.BlockSpec((SB_BK + 8, 2 * SB_BK), lambda h, i: (0, 0))
    pair_f32 = pltpu.VMEM((2, 2 * SB_BK, bq), F32)
    pair_bf16 = pltpu.VMEM((2, 2 * SB_BK, bq), BF16)
    hbm = pl.BlockSpec(memory_space=pl.ANY)
    c_ins, c_outs, c_sems = (exchange.ins, exchange.out_shapes, exchange.sems) if exchange else ([], [], [])
    outs = pl.pallas_call(
        body, name=name, grid=(nh, s // bq),
        in_specs=[tspec, pl.BlockSpec((None, dh, s), lambda h, i: (h + nh, 0, 0)),
                  pl.BlockSpec((None, dh, s), lambda h, i: (h + 2 * nh, 0, 0)), tspec,
                  pl.BlockSpec((None, nkb, bq), lambda h, i: (h, 0, i)), aspec, aspec] + [hbm] * len(c_ins),
        out_specs=[tspec, fullspec, fullspec] + [hbm] * len(c_outs),
        out_shape=[jax.ShapeDtypeStruct((nh, dh, s), BF16)] * 3 + c_outs,
        scratch_shapes=[pltpu.VMEM((dh, bq), F32), pltpu.VMEM((dh, s), F32), pltpu.VMEM((dh, s), F32),
                        pair_f32, pair_f32, pair_bf16, pair_bf16] + c_sems,
        compiler_params=_params(("arbitrary", "arbitrary")),
    )(qkvt, qkvt, qkvt, dot_, rtab, _tri_rows(True), _tri_rows(False), *c_ins)
    return outs[0], outs[1], outs[2], outs[3:]


SWA_QB = 2


def _swa_probs(qt, kt, bias_t, sink, i):
    cols = qt.shape[1]
    sc = _contract0(kt, qt) + bias_t
    kj = lax.broadcasted_iota(jnp.int32, (2 * WINDOW, cols), 0)
    qi = lax.broadcasted_iota(jnp.int32, (2 * WINDOW, cols), 1) & (WINDOW - 1)
    dist = qi + WINDOW - kj
    valid = (dist >= 0) & (dist < WINDOW) & ((kj >= WINDOW) | (i > 0))
    sc = jnp.where(valid, sc, NEG_INF)
    mx = jnp.maximum(jnp.max(sc, axis=0, keepdims=True), sink)
    p = jnp.exp(sc - mx)
    p_sink = jnp.exp(sink - mx)
    inv = 1.0 / (jnp.sum(p, axis=0, keepdims=True) + p_sink)
    return p, p_sink, inv


def _band(i):
    return pl.ds(pl.multiple_of(i * WINDOW, WINDOW), 2 * WINDOW)


def _heads_to_lanes(blk):
    return jnp.concatenate([blk[r * HEAD_DIM:(r + 1) * HEAD_DIM] for r in range(8)], axis=1)


def _lanes_to_heads(t):
    return jnp.concatenate([t[:, r * WINDOW:(r + 1) * WINDOW] for r in range(8)], axis=0)


def swa_fwd(name, qt, kpt, vpt, bias_t, sink_row):
    d, s = qt.shape
    ng, dh, sp = kpt.shape
    rows, cols = d // ng, SWA_QB * WINDOW
    assert (s // WINDOW) % SWA_QB == 0

    def body(q_ref, k_ref, v_ref, bias_ref, sink_ref, o_ref):
        for u in range(SWA_QB):
            i = pl.program_id(1) * SWA_QB + u
            lanes = slice(u * WINDOW, (u + 1) * WINDOW)
            qb = _heads_to_lanes(q_ref[:, lanes]) * Q_SCALE
            p, _, inv = _swa_probs(qb, k_ref[:, _band(i)], bias_ref[...], sink_ref[...], i)
            o_t = jnp.dot(v_ref[:, _band(i)], p.astype(BF16), preferred_element_type=F32) * inv
            o_ref[:, lanes] = _lanes_to_heads(o_t).astype(o_ref.dtype)

    qspec = pl.BlockSpec((rows, cols), lambda g, i: (g, i))
    kspec = pl.BlockSpec((None, dh, sp), lambda g, i: (g, 0, 0))
    return pl.pallas_call(
        body, name=name, grid=(ng, s // cols),
        in_specs=[qspec, kspec, kspec, pl.BlockSpec((None, 2 * WINDOW, 8 * WINDOW), lambda g, i: (g, 0, 0)),
                  pl.BlockSpec((None, 1, 8 * WINDOW), lambda g, i: (g, 0, 0))],
        out_specs=qspec,
        out_shape=jax.ShapeDtypeStruct(qt.shape, BF16),
        compiler_params=_params(("parallel", "arbitrary")),
    )(qt, kpt, vpt, bias_t, sink_row)


def swa_bwd(name, qt, kpt, vpt, bias_t, sink_row, dot_, dk_in, dv_in):
    d, s = qt.shape
    ng, dh, sp = kpt.shape
    rows, cols = d // ng, SWA_QB * WINDOW

    def body(q_ref, k_ref, v_ref, bias_ref, sink_ref, do_ref, dki_ref, dvi_ref, dq_ref, dk_ref, dv_ref, db_ref, ds_ref):
        @pl.when(pl.program_id(1) == 0)
        def _():
            dk_ref[...] = dki_ref[...]
            dv_ref[...] = dvi_ref[...]
            db_ref[...] = jnp.zeros_like(db_ref)
            ds_ref[...] = jnp.zeros_like(ds_ref)

        for u in range(SWA_QB):
            i = pl.program_id(1) * SWA_QB + u
            band = _band(i)
            lanes = slice(u * WINDOW, (u + 1) * WINDOW)
            qb = _heads_to_lanes(q_ref[:, lanes]) * Q_SCALE
            dob = _heads_to_lanes(do_ref[:, lanes])
            kt = k_ref[:, band]
            p, p_sink, inv = _swa_probs(qb, kt, bias_ref[...], sink_ref[...], i)
            p = p * inv
            dp = _contract0(v_ref[:, band], dob)
            delta = jnp.sum(p * dp, axis=0, keepdims=True)
            dsc = p * (dp - delta)
            ds_ref[...] -= p_sink * inv * delta
            db_ref[...] += dsc
            dscb = dsc.astype(BF16)
            dq_t = jnp.dot(kt, dscb, preferred_element_type=F32) * Q_SCALE
            dq_ref[:, lanes] = _lanes_to_heads(dq_t).astype(dq_ref.dtype)
            dk_ref[:, band] += _contract1(qb, dscb)
            dv_ref[:, band] += _contract1(dob, p.astype(BF16))

    qspec = pl.BlockSpec((rows, cols), lambda g, i: (g, i))
    kspec = pl.BlockSpec((None, dh, sp), lambda g, i: (g, 0, 0))
    bspec = pl.BlockSpec((None, 2 * WINDOW, 8 * WINDOW), lambda g, i: (g, 0, 0))
    sspec = pl.BlockSpec((None, 1, 8 * WINDOW), lambda g, i: (g, 0, 0))
    return pl.pallas_call(
        body, name=name, grid=(ng, s // cols),
        in_specs=[qspec, kspec, kspec, bspec, sspec, qspec, kspec, kspec],
        out_specs=[qspec, kspec, kspec, bspec, sspec],
        out_shape=[jax.ShapeDtypeStruct(qt.shape, BF16), jax.ShapeDtypeStruct(kpt.shape, F32),
                   jax.ShapeDtypeStruct(kpt.shape, F32), jax.ShapeDtypeStruct(bias_t.shape, F32),
                   jax.ShapeDtypeStruct(sink_row.shape, F32)],
        compiler_params=_params(("parallel", "arbitrary")),
    )(qt, kpt, vpt, bias_t, sink_row, dot_, dk_in, dv_in)


def _bucket_onehot():
    qi = np.arange(WINDOW)[:, None]
    kj = np.arange(2 * WINDOW)[None, :]
    n = np.maximum(qi + WINDOW - kj, 0)
    max_exact = N_BUCKETS // 2
    nf = np.maximum(n, 1).astype(np.float64)
    val = np.log(nf / max_exact) / math.log(WINDOW / max_exact) * (N_BUCKETS - max_exact)
    assert np.all(np.abs(val - np.round(val))[(n > max_exact) & (n < WINDOW)] > 1e-3)
    large = np.minimum(max_exact + val.astype(np.int64), N_BUCKETS - 1)
    bucket = np.where(n < max_exact, n, large).reshape(-1)
    onehot = np.zeros((128, bucket.size), np.float32)
    onehot[bucket, np.arange(bucket.size)] = 1.0
    return onehot


def _split3(x):
    a = x.astype(BF16)
    r = x - a.astype(F32)
    b = r.astype(BF16)
    c = (r - b.astype(F32)).astype(BF16)
    return a, b, c


def bias_table(rel_bias):
    nh = rel_bias.shape[1]
    oh = jnp.asarray(_bucket_onehot(), BF16)
    n = oh.shape[1]
    tn = 4096
    rb = jnp.zeros((nh, 128), F32).at[:, :N_BUCKETS].set(rel_bias.T)

    def body(rb_ref, oh_ref, o_ref):
        o_ref[...] = sum(jnp.dot(t, oh_ref[...], preferred_element_type=F32) for t in _split3(rb_ref[...]))

    return pl.pallas_call(
        body, name="bias_table", grid=(n // tn,),
        in_specs=[pl.BlockSpec((nh, 128), lambda i: (0, 0)), pl.BlockSpec((128, tn), lambda i: (0, i))],
        out_specs=pl.BlockSpec((nh, tn), lambda i: (0, i)),
        out_shape=jax.ShapeDtypeStruct((nh, n), F32),
        compiler_params=_params(("parallel",)),
    )(rb, oh)


def bias_table_grad(db0, db1):
    nh, n = db0.shape
    oh = jnp.asarray(_bucket_onehot(), BF16)
    tn = 4096

    def body(a_ref, b_ref, oh_ref, o_ref):
        @pl.when(pl.program_id(0) == 0)
        def _():
            o_ref[...] = jnp.zeros_like(o_ref)

        o_ref[...] += sum(lax.dot_general(t, oh_ref[...], (((1,), (1,)), ((), ())), preferred_element_type=F32)
                          for t in _split3(a_ref[...] + b_ref[...]))

    blk = pl.BlockSpec((nh, tn), lambda i: (0, i))
    return pl.pallas_call(
        body, name="bias_table_grad", grid=(n // tn,),
        in_specs=[blk, blk, pl.BlockSpec((128, tn), lambda i: (0, i))],
        out_specs=pl.BlockSpec((nh, 128), lambda i: (0, 0)),
        out_shape=jax.ShapeDtypeStruct((nh, 128), F32),
        compiler_params=_params(("arbitrary",)),
    )(db0, db1, oh)


def _owner_view(ref, name, d):
    if name == 'a_norm':
        return ref.at[d]
    if name in COL_SHARDED:
        n = ref.shape[2] // N_DEV
        return ref.at[:, :, pl.ds(pl.multiple_of(d * n, 128), n)]
    return ref.at[:, d]


def _place():
    return lax.axis_index("x"), lax.axis_index("y"), lax.axis_index("c")


def _dev(p):
    return 4 * p[0] + 2 * p[1] + p[2]


def _remote(src, dst, send_sem, recv_sem, to):
    return pltpu.make_async_remote_copy(src_ref=src, dst_ref=dst, send_sem=send_sem, recv_sem=recv_sem,
                                        device_id=to, device_id_type=MESH)


def _dma_sems(*shapes):
    return [pltpu.SemaphoreType.DMA(sh) for sh in shapes]


def comm_call(name, build, ins, out_shapes, sems, aliases=None):
    n_in, n_out = len(ins), len(out_shapes)

    def body(*refs):
        copies = build(refs[:n_in], refs[n_in:n_in + n_out], *refs[n_in + n_out:])
        for cp in copies:
            cp.start()
        for cp in copies:
            cp.wait()

    hbm = pl.BlockSpec(memory_space=pl.ANY)
    return pl.pallas_call(
        body, name=name, in_specs=[hbm] * n_in, out_specs=[hbm] * n_out, out_shape=list(out_shapes),
        scratch_shapes=sems, input_output_aliases=aliases or {},
    )(*ins)


def all_gather_weights(names, shards, full_shapes):
    n = len(names)

    def body(*refs):
        ins, outs = refs[:n], refs[n:2 * n]
        send_sems, recv_sems, local_sems = refs[2 * n:]
        x, y, c = _place()
        me, sibling = (x, y, c), (x, y, 1 - c)
        chips = [(1 - x, y), (x, 1 - y), (1 - x, 1 - y)]

        def copy(t, k, block, to, src=None):
            dst = _owner_view(outs[t], names[t], _dev(block))
            return _remote(dst if src is None else src, dst, send_sems.at[t, k], recv_sems.at[t, k], to)

        mine = [pltpu.make_async_copy(ins[t], _owner_view(outs[t], names[t], _dev(me)), local_sems.at[t])
                for t in range(n)]
        for cp in mine:
            cp.start()
        first = []
        for t in range(n):
            first.append(copy(t, 0, me, sibling, src=ins[t]))
            first += [copy(t, 1 + j, me, (*chip, c), src=ins[t]) for j, chip in enumerate(chips)]
        for cp in first:
            cp.start()
        passed = []
        for j, chip in enumerate(chips):
            for t in range(n):
                copy(t, 1 + j, (*chip, c), me).wait_recv()
                fwd = copy(t, 4 + j, (*chip, c), sibling)
                fwd.start()
                passed.append(fwd)
        for t in range(n):
            copy(t, 0, sibling, me).wait_recv()
            for j, chip in enumerate(chips):
                copy(t, 4 + j, (*chip, 1 - c), me).wait_recv()
        for cp in first + passed:
            cp.wait_send()
        for cp in mine:
            cp.wait()

    hbm = pl.BlockSpec(memory_space=pl.ANY)
    return pl.pallas_call(
        body, name="all_gather_layer0",
        in_specs=[hbm] * n, out_specs=[hbm] * n,
        out_shape=[jax.ShapeDtypeStruct(full_shapes[t], shards[t].dtype) for t in range(n)],
        scratch_shapes=_dma_sems((n, 7), (n, 7), (n,)),
    )(*shards)


def ag_direct(names, shards, full_shapes):
    n = len(names)

    def build(ins, outs, send_sems, recv_sems, local_sems):
        x, y, c = _place()
        peers = [(x, y, 1 - c), (1 - x, y, c), (x, 1 - y, c), (1 - x, 1 - y, c)]
        copies = []
        for t in range(n):
            dst = _owner_view(outs[t], names[t], _dev((x, y, c)))
            copies.append(pltpu.make_async_copy(ins[t], dst, local_sems.at[t]))
            copies += [_remote(ins[t], dst, send_sems.at[t, k], recv_sems.at[t, k], to) for k, to in enumerate(peers)]
        return copies

    return Carry(build, shards, [jax.ShapeDtypeStruct(full_shapes[t], shards[t].dtype) for t in range(n)],
                 _dma_sems((n, 4), (n, 4), (n,)))


def ag_forward(tag, names, partial):
    n = len(names)

    def build(ins, outs, send_sems, recv_sems):
        del ins
        x, y, c = _place()
        copies = []
        for t in range(n):
            for k, chip in enumerate([(1 - x, y), (x, 1 - y), (1 - x, 1 - y)]):
                view = _owner_view(outs[t], names[t], _dev((*chip, c)))
                copies.append(_remote(view, view, send_sems.at[t, k], recv_sems.at[t, k], (x, y, 1 - c)))
        return copies

    return comm_call(f"all_gather_forward_{tag}", build, partial, [jax.ShapeDtypeStruct(p.shape, p.dtype) for p in partial],
                     _dma_sems((n, 3), (n, 3)), aliases={t: t for t in range(n)})


def sibling_exchange(tag, names, grads, part_shapes):
    n = len(names)

    def build(ins, outs, send_sems, recv_sems):
        x, y, c = _place()
        return [_remote(_owner_view(ins[t], names[t], 2 * q + 1 - c), outs[t].at[q], send_sems.at[t, q],
                        recv_sems.at[t, q], (x, y, 1 - c)) for t in range(n) for q in range(4)]

    return comm_call(f"rs_sibling_exchange_{tag}", build, grads,
                     [jax.ShapeDtypeStruct((4,) + part_shapes[t], BF16) for t in range(n)], _dma_sems((n, 4), (n, 4)))


def chip_exchange(names, parts, part_shapes):
    n = len(names)

    def build(ins, outs, send_sems, recv_sems):
        x, y, c = _place()
        chips = [(1 - x, y), (x, 1 - y), (1 - x, 1 - y)]
        return [_remote(ins[t].at[2 * chip[0] + chip[1]], outs[t].at[k], send_sems.at[t, k], recv_sems.at[t, k],
                        (*chip, c)) for t in range(n) for k, chip in enumerate(chips)]

    return Carry(build, parts, [jax.ShapeDtypeStruct((3,) + part_shapes[t], BF16) for t in range(n)],
                 _dma_sems((n, 3), (n, 3)))


def all_gather_rows(x):
    r, w = x.shape

    def body(x_ref, out_ref, send_sems, recv_sems, local_sem):
        px, py, pc = _place()
        me = 4 * px + 2 * py + pc
        mine = pltpu.make_async_copy(x_ref, out_ref.at[me], local_sem)
        mine.start()
        copies = []
        for k in range(1, N_DEV):
            peer = (px ^ (k >> 2), py ^ ((k >> 1) & 1), pc ^ (k & 1))
            copies.append(pltpu.make_async_remote_copy(
                src_ref=x_ref, dst_ref=out_ref.at[me], send_sem=send_sems.at[k - 1], recv_sem=recv_sems.at[k - 1],
                device_id=peer, device_id_type=MESH))
        for cp in copies:
            cp.start()
        for k in range(1, N_DEV):
            peer_idx = me ^ k
            pltpu.make_async_remote_copy(
                src_ref=x_ref, dst_ref=out_ref.at[peer_idx], send_sem=send_sems.at[k - 1],
                recv_sem=recv_sems.at[k - 1], device_id=(px, py, pc), device_id_type=MESH).wait_recv()
        for cp in copies:
            cp.wait_send()
        mine.wait()

    vmem = pl.BlockSpec(memory_space=pltpu.VMEM)
    return pl.pallas_call(
        body, name="all_gather_small_grads",
        in_specs=[vmem], out_specs=vmem,
        out_shape=jax.ShapeDtypeStruct((N_DEV, r, w), x.dtype),
        scratch_shapes=[pltpu.SemaphoreType.DMA((N_DEV - 1,)), pltpu.SemaphoreType.DMA((N_DEV - 1,)),
                        pltpu.SemaphoreType.DMA],
    )(x)


def _adamw(w, g, m, v):
    m = ADAM_B1 * m + (1.0 - ADAM_B1) * g
    v = ADAM_B2 * v + (1.0 - ADAM_B2) * (g * g)
    m_hat = m / (1.0 - ADAM_B1 ** ADAM_STEP)
    v_hat = v / (1.0 - ADAM_B2 ** ADAM_STEP)
    return -ADAM_LR * (m_hat / (jnp.sqrt(v_hat) + ADAM_EPS) + ADAM_WD * w), m, v


def sibling_sum(name, col, grads, recv, core):
    _, nl, rows, cols = recv.shape
    tr = _tile(rows, 512)
    rspec = pl.BlockSpec((None, None, tr, cols), lambda q, l, i, c_ref: (q, l, i, 0))
    if col:
        gspec = pl.BlockSpec((None, tr, cols), lambda q, l, i, c_ref: (l, i, 2 * q + c_ref[0]))
    else:
        gspec = pl.BlockSpec((None, None, tr, cols), lambda q, l, i, c_ref: (l, 2 * q + c_ref[0], i, 0))

    def body(c_ref, g_ref, r_ref, o_ref):
        del c_ref
        o_ref[...] = (g_ref[...].astype(F32) + r_ref[...].astype(F32)).astype(BF16)

    return pl.pallas_call(
        body, name=name,
        grid_spec=pltpu.PrefetchScalarGridSpec(num_scalar_prefetch=1, grid=(4, nl, rows // tr),
                                               in_specs=[gspec, rspec], out_specs=rspec),
        out_shape=jax.ShapeDtypeStruct(recv.shape, BF16),
        compiler_params=_params(("parallel", "parallel", "parallel")),
    )(core.reshape(1), grads, recv)


def reduce_adamw(name, parts, recv, chip, w, m, v, l0, prev):
    _, nl, rows, cols = parts.shape
    tr = _tile(rows, 256)

    def body(q_ref, p_ref, r_ref, w_ref, m_ref, v_ref, *rest):
        del q_ref
        g_out, d_out, m_out, v_out = rest[-4:]
        g = ((p_ref[...].astype(F32) + r_ref[0].astype(F32)) + r_ref[1].astype(F32)) + r_ref[2].astype(F32)
        d, mn, vn = _adamw(w_ref[...], g, m_ref[...], v_ref[...])
        g_out[...] = g
        d_out[...] = d
        m_out[...] = mn
        v_out[...] = vn

    blk = pl.BlockSpec((None, tr, cols), lambda l, i, q_ref: (l0 + l, i, 0))
    prev = list(prev) if prev else []
    return pl.pallas_call(
        body, name=name,
        grid_spec=pltpu.PrefetchScalarGridSpec(
            num_scalar_prefetch=1, grid=(nl, rows // tr),
            in_specs=[pl.BlockSpec((None, None, tr, cols), lambda l, i, q_ref: (q_ref[0], l, i, 0)),
                      pl.BlockSpec((3, None, tr, cols), lambda l, i, q_ref: (0, l, i, 0)), blk, blk, blk]
            + [pl.BlockSpec(memory_space=pl.ANY)] * len(prev),
            out_specs=[blk] * 4),
        out_shape=[jax.ShapeDtypeStruct(w.shape, F32)] * 4,
        input_output_aliases={6 + i: i for i in range(len(prev))},
        compiler_params=_params(("parallel", "parallel")),
    )(chip.reshape(1), parts, recv, w, m, v, *prev)


def small_adamw(name, gathered, w, m, v):
    _, r, c = gathered.shape

    def body(ga_ref, w_ref, m_ref, v_ref, g_out, d_out, m_out, v_out):
        g = ga_ref[0]
        for d in range(1, N_DEV):
            g = g + ga_ref[d]
        dl, mn, vn = _adamw(w_ref[...], g, m_ref[...], v_ref[...])
        g_out[...] = g
        d_out[...] = dl
        m_out[...] = mn
        v_out[...] = vn

    return pl.pallas_call(
        body, name=name,
        out_shape=[jax.ShapeDtypeStruct((r, c), F32)] * 4,
        compiler_params=_params(),
    )(gathered, w, m, v)


def _add(acc, *ex):
    return (acc + ex[0],)


def local_step(x, target, small, ex):
    s, d = x.shape
    n_a, n_b = small['a_norm'].shape[0], small['b_norm'].shape[0]
    sg = {}
    gb = {}

    def fwd_mm(name, a, wname, layer, epilogue, extras, out_dtypes):
        return mm_nn(name, a, *ex.weight(wname, layer), epilogue, extras, out_dtypes)

    def dx_mm(name, dy, wname, layer, epilogue, extras, out_dtypes):
        return mm_nt(name, dy, *ex.weight(wname, layer), epilogue, extras, out_dtypes)

    def dw_mm(name, a, dy, wname, layer):
        key, slab, shape = ex.grad(wname, layer)
        gb[key] = mm_tn(name, a, dy, gb.get(key), shape, slab)

    bias_flat = bias_table(small['rel_bias'])
    bias_t = bias_flat.reshape(2, 8, WINDOW, 2 * WINDOW).transpose(0, 3, 1, 2).reshape(2, 2 * WINDOW, 8 * WINDOW)
    sink_rows = [jnp.repeat(small['b_sinks'][j], WINDOW).reshape(2, 1, 8 * WINDOW) for j in range(n_b)]

    def mlp_fwd(h, layer):
        n2 = rms_fwd(f"mlp_norm_fwd{layer}", h, small['mlp_norm'][layer])
        u, a = fwd_mm(f"mlp_up_fwd{layer}", n2, 'mlp_up', layer,
                      lambda acc: (acc, jnp.square(jnp.maximum(acc, 0.0))), (), (BF16, BF16))
        (h2,) = fwd_mm(f"mlp_down_fwd{layer}", a, 'mlp_down', layer, _add, (h,), (F32,))
        return h2, (n2, u, a)

    h = x
    saved = []
    for l in range(n_a):
        n1 = rms_fwd(f"a_norm_fwd{l}", h, small['a_norm'][l])
        (qkv,) = fwd_mm(f"a_qkv_fwd{l}", n1, 'a_wqkv', l, lambda acc: (acc,), (), (BF16,))
        qkvt = qkv.T.reshape(3 * d // HEAD_DIM, HEAD_DIM, s)
        o_t, rtab, carried = sb_fwd(f"sb_fwd{l}", qkvt, ex.fwd_carry(l))
        ex.fwd_done(l, carried)
        o = o_t.reshape(d, s).T
        (h_mid,) = fwd_mm(f"a_wo_fwd{l}", o, 'a_wo', l, _add, (h,), (F32,))
        h_out, mlp_saved = mlp_fwd(h_mid, l)
        saved.append((h, n1, qkvt, o, rtab, h_mid, mlp_saved))
        h = h_out
    h_kv = h
    nkv = rms_fwd("kv_norm_fwd", h, small['kv_norm'])
    (kv,) = fwd_mm("kv_fwd", nkv, 'w_kv', 0, lambda acc, b: (acc + b,), (small['b_kv'].reshape(1, -1),), (BF16,))
    kvt = kv.T.reshape(2, 2, HEAD_DIM, s)
    kpt, vpt = (jnp.pad(t, ((0, 0), (0, 0), (WINDOW, 0))) for t in (kvt[0], kvt[1]))
    for j in range(n_b):
        layer = n_a + j
        n1 = rms_fwd(f"b_norm_fwd{j}", h, small['b_norm'][j])
        (qb,) = fwd_mm(f"b_q_fwd{j}", n1, 'b_wq', j, lambda acc, b: (acc + b,),
                       (small['b_bq'][j].reshape(1, -1),), (BF16,))
        qbt = qb.T
        o = swa_fwd(f"swa_fwd{j}", qbt, kpt, vpt, bias_t, sink_rows[j]).T
        (h_mid,) = fwd_mm(f"b_wo_fwd{j}", o, 'b_wo', j, lambda acc, hh, b: (acc + hh + b,),
                          (h, small['b_bo'][j].reshape(1, -1)), (F32,))
        h_out, mlp_saved = mlp_fwd(h_mid, layer)
        saved.append((h, n1, qbt, o, h_mid, mlp_saved))
        h = h_out

    dh, dhb, dg_final, loss_b = loss_head(h, small['final_norm'], target)
    sg['final_norm'] = dg_final[0]
    sg['mlp_norm'] = [None] * (n_a + n_b)

    def mlp_bwd(dh, dhb, h_mid, mlp_saved, layer):
        n2, u, a = mlp_saved
        (du,) = dx_mm(f"mlp_down_dx{layer}", dhb, 'mlp_down', layer,
                      lambda acc, uu: (acc * (2.0 * jnp.maximum(uu.astype(F32), 0.0)),), (u,), (BF16,))
        dw_mm(f"mlp_down_dw{layer}", a, dhb, 'mlp_down', layer)
        (dn2,) = dx_mm(f"mlp_up_dx{layer}", du, 'mlp_up', layer, lambda acc: (acc,), (), (F32,))
        dw_mm(f"mlp_up_dw{layer}", n2, du, 'mlp_up', layer)
        dh2, dh2b, dg, cs = rms_bwd(f"mlp_norm_bwd{layer}", h_mid, small['mlp_norm'][layer], dn2, dh)
        sg['mlp_norm'][layer] = dg[0]
        return dh2, dh2b, cs

    dkp = jnp.zeros(kpt.shape, F32)
    dvp = jnp.zeros(vpt.shape, F32)
    sg['b_norm'], sg['b_bq'], sg['b_bo'], sg['b_sinks'] = [None] * n_b, [None] * n_b, [None] * n_b, [None] * n_b
    dbias = [None] * n_b
    for j in reversed(range(n_b)):
        layer = n_a + j
        h_in, n1, qbt, o, h_mid, mlp_saved = saved[layer]
        dh, dhb, cs = mlp_bwd(dh, dhb, h_mid, mlp_saved, layer)
        sg['b_bo'][j] = cs[0]
        (do,) = dx_mm(f"b_wo_dx{j}", dhb, 'b_wo', j, lambda acc: (acc,), (), (BF16,))
        dw_mm(f"b_wo_dw{j}", o, dhb, 'b_wo', j)
        dq_t, dkp, dvp, dbias[j], dsink = swa_bwd(f"swa_bwd{j}", qbt, kpt, vpt, bias_t, sink_rows[j], do.T, dkp, dvp)
        sg['b_sinks'][j] = colsum(f"sink_grad{j}", dsink.reshape(16, WINDOW).T)[0]
        dq = dq_t.T
        sg['b_bq'][j] = colsum(f"b_bq_grad{j}", dq)[0]
        (dn1,) = dx_mm(f"b_q_dx{j}", dq, 'b_wq', j, lambda acc: (acc,), (), (F32,))
        dw_mm(f"b_q_dw{j}", n1, dq, 'b_wq', j)
        dh, dhb, dg, _ = rms_bwd(f"b_norm_bwd{j}", h_in, small['b_norm'][j], dn1, dh)
        sg['b_norm'][j] = dg[0]
    unt = lambda t: t.reshape(2, 2 * WINDOW, 8, WINDOW).transpose(0, 2, 3, 1).reshape(bias_flat.shape)
    sg['rel_bias'] = bias_table_grad(unt(dbias[0]), unt(dbias[1]))[:, :N_BUCKETS].T

    dkv = jnp.concatenate([dkp[:, :, WINDOW:], dvp[:, :, WINDOW:]], axis=0).reshape(-1, s).T
    sg['b_kv'] = colsum("b_kv_grad", dkv)[0]
    dkvb = dkv.astype(BF16)
    (dnkv,) = dx_mm("kv_dx", dkvb, 'w_kv', 0, lambda acc: (acc,), (), (F32,))
    dw_mm("kv_dw", nkv, dkvb, 'w_kv', 0)
    dh, dhb, dg, _ = rms_bwd("kv_norm_bwd", h_kv, small['kv_norm'], dnkv, dh)
    sg['kv_norm'] = dg[0]

    sg['a_norm'] = [None] * n_a
    for l in reversed(range(n_a)):
        h_in, n1, qkvt, o, rtab, h_mid, mlp_saved = saved[l]
        dh, dhb, _ = mlp_bwd(dh, dhb, h_mid, mlp_saved, l)
        (do,) = dx_mm(f"a_wo_dx{l}", dhb, 'a_wo', l, lambda acc: (acc,), (), (BF16,))
        dw_mm(f"a_wo_dw{l}", o, dhb, 'a_wo', l)
        dq_t, dk_t, dv_t, carried = sb_bwd(f"sb_bwd{l}", qkvt, do.T.reshape(d // HEAD_DIM, HEAD_DIM, s), rtab,
                                           ex.bwd_carry(l, gb))
        ex.bwd_done(l, carried)
        dqkv = jnp.concatenate([dq_t, dk_t, dv_t], axis=0).reshape(3 * d, s).T
        (dn1,) = dx_mm(f"a_qkv_dx{l}", dqkv, 'a_wqkv', l, lambda acc: (acc,), (), (F32,))
        dw_mm(f"a_qkv_dw{l}", n1, dqkv, 'a_wqkv', l)
        dh, dhb, dg, _ = rms_bwd(f"a_norm_bwd{l}", h_in, small['a_norm'][l], dn1, dh)
        sg['a_norm'][l] = dg[0]

    small_grads = {
        'a_norm': jnp.stack(sg['a_norm']), 'kv_norm': sg['kv_norm'], 'b_kv': sg['b_kv'],
        'b_norm': jnp.stack(sg['b_norm']), 'b_bq': jnp.stack(sg['b_bq']), 'b_sinks': jnp.stack(sg['b_sinks']),
        'b_bo': jnp.stack(sg['b_bo']), 'rel_bias': sg['rel_bias'], 'mlp_norm': jnp.stack(sg['mlp_norm']),
        'final_norm': sg['final_norm'],
    }
    return loss_b, dh, gb, small_grads


def _full_shape(name, shard_shape):
    if name in COL_SHARDED:
        return shard_shape[:2] + (N_DEV * shard_shape[2],)
    nl, r, n = shard_shape
    return (nl, N_DEV, r, n)


def _as_w3(name, full):
    if name in COL_SHARDED:
        return full
    nl, nd, r, n = full.shape
    return full.reshape(nl, nd * r, n)


LAYERED = ('a_wqkv', 'a_wo', 'mlp_up', 'mlp_down')
RS_GROUPS = {
    'A': (('mlp_up', 1, 3), ('mlp_down', 1, 3), ('b_wq', 0, 2), ('b_wo', 0, 2), ('w_kv', 0, 1)),
    'B': (('a_wqkv', 1, 1), ('a_wo', 1, 1), ('mlp_up', 0, 1), ('mlp_down', 0, 1)),
    'C': (('a_wqkv', 0, 1), ('a_wo', 0, 1)),
}


class _Exchanges:
    def __init__(self, full0, shards, core, chip, w3, m3, v3):
        self.wbuf = {0: {n: _as_w3(n, full0[n]) for n in LAYERED}}
        self.shards, self.core, self.chip = shards, core, chip
        self.w3, self.m3, self.v3 = w3, m3, v3
        self.shard_dims = {n: w3[n].shape[1:] for n in BIG}
        self.parts = {}
        self.out = {}

    def weight(self, name, layer):
        if name in LAYERED:
            return (self.wbuf[min(layer, 2)][name], 0 if layer < 2 else layer - 2)
        return self.wbuf[2][name], layer

    def fwd_carry(self, layer):
        names = list(self.shards[layer + 1])
        shards = [self.shards[layer + 1][n] for n in names]
        return ag_direct(names, shards, [_full_shape(n, sh.shape) for n, sh in zip(names, shards)])

    def fwd_done(self, layer, carried):
        names = list(self.shards[layer + 1])
        self.wbuf[layer + 1] = {n: _as_w3(n, f) for n, f in zip(names, ag_forward(layer + 1, names, list(carried)))}

    def grad(self, name, layer):
        for group, members in RS_GROUPS.items():
            for n, l0, nl in members:
                if n == name and l0 <= layer < l0 + nl:
                    return (group, name), layer - l0, (nl,) + self.wbuf[0 if name in LAYERED else 2][name].shape[1:]
        raise KeyError((name, layer))

    def _sibling_stage(self, group, gb):
        names = [n for n, _, _ in RS_GROUPS[group]]
        shapes = [(nl,) + self.shard_dims[n] for n, _, nl in RS_GROUPS[group]]
        gfull = [gb[(group, n)].reshape(_full_shape(n, sh)) for n, sh in zip(names, shapes)]
        recv = sibling_exchange(group, names, gfull, shapes)
        self.parts[group] = [sibling_sum(f"rs_sibling_sum_{group}_{n}", n in COL_SHARDED, g, r, self.core)
                             for n, g, r in zip(names, gfull, recv)]
        return names, shapes

    def bwd_carry(self, layer, gb):
        names, shapes = self._sibling_stage('A' if layer == 1 else 'B', gb)
        return chip_exchange(names, self.parts['A' if layer == 1 else 'B'], shapes)

    def bwd_done(self, layer, carried):
        self._adamw('A' if layer == 1 else 'B', carried)

    def finish(self, gb):
        names, shapes = self._sibling_stage('C', gb)
        ce = chip_exchange(names, self.parts['C'], shapes)
        self._adamw('C', comm_call("rs_chip_exchange_C", ce.build, ce.ins, ce.out_shapes, ce.sems))
        return self.out

    def _adamw(self, group, recv2):
        for (n, l0, _), p, r in zip(RS_GROUPS[group], self.parts[group], recv2):
            self.out[n] = reduce_adamw(f"adamw_{group}_{n}", p, r, self.chip, self.w3[n], self.m3[n], self.v3[n],
                                       l0, self.out.get(n))


def _pack_small(vals):
    flat = jnp.concatenate([vals[n].reshape(-1).astype(F32) for n in SMALL] + [vals['loss'].reshape(-1)])
    rows = -(-flat.shape[0] // 1024) * 8
    return jnp.pad(flat, (0, rows * 128 - flat.shape[0])).reshape(rows, 128)


def _unpack_small(packed, shapes):
    flat = packed.reshape(-1)
    out, off = {}, 0
    for n in SMALL + ['loss']:
        size = int(np.prod(shapes[n]))
        out[n] = flat[off:off + size].reshape(shapes[n])
        off += size
    return out


def kernel(x, a_norm, a_wqkv, a_wo, kv_norm, w_kv, b_kv, b_norm, b_wq, b_bq, b_sinks, b_wo, b_bo, rel_bias, mlp_norm, mlp_up, mlp_down, final_norm, loss_target, m_a_norm, m_a_wqkv, m_a_wo, m_kv_norm, m_w_kv, m_b_kv, m_b_norm, m_b_wq, m_b_bq, m_b_sinks, m_b_wo, m_b_bo, m_rel_bias, m_mlp_norm, m_mlp_up, m_mlp_down, m_final_norm, v_a_norm, v_a_wqkv, v_a_wo, v_kv_norm, v_w_kv, v_b_kv, v_b_norm, v_b_wq, v_b_bq, v_b_sinks, v_b_wo, v_b_bo, v_rel_bias, v_mlp_norm, v_mlp_up, v_mlp_down, v_final_norm):
    w = dict(a_norm=a_norm, a_wqkv=a_wqkv, a_wo=a_wo, kv_norm=kv_norm, w_kv=w_kv, b_kv=b_kv, b_norm=b_norm,
             b_wq=b_wq, b_bq=b_bq, b_sinks=b_sinks, b_wo=b_wo, b_bo=b_bo, rel_bias=rel_bias, mlp_norm=mlp_norm,
             mlp_up=mlp_up, mlp_down=mlp_down, final_norm=final_norm)
    m = dict(a_norm=m_a_norm, a_wqkv=m_a_wqkv, a_wo=m_a_wo, kv_norm=m_kv_norm, w_kv=m_w_kv, b_kv=m_b_kv,
             b_norm=m_b_norm, b_wq=m_b_wq, b_bq=m_b_bq, b_sinks=m_b_sinks, b_wo=m_b_wo, b_bo=m_b_bo,
             rel_bias=m_rel_bias, mlp_norm=m_mlp_norm, mlp_up=m_mlp_up, mlp_down=m_mlp_down, final_norm=m_final_norm)
    v = dict(a_norm=v_a_norm, a_wqkv=v_a_wqkv, a_wo=v_a_wo, kv_norm=v_kv_norm, w_kv=v_w_kv, b_kv=v_b_kv,
             b_norm=v_b_norm, b_wq=v_b_wq, b_bq=v_b_bq, b_sinks=v_b_sinks, b_wo=v_b_wo, b_bo=v_b_bo,
             rel_bias=v_rel_bias, mlp_norm=v_mlp_norm, mlp_up=v_mlp_up, mlp_down=v_mlp_down, final_norm=v_final_norm)
    px, py, pc = _place()
    me = 4 * px + 2 * py + pc
    chip = (2 * px + py).astype(jnp.int32)
    core = pc.astype(jnp.int32)

    as3 = lambda t: t[None] if t.ndim == 2 else t
    w3, m3, v3 = ({n: as3(src[n]) for n in BIG} for src in (w, m, v))
    shards0 = {n: w3[n][:1].astype(BF16) for n in LAYERED}
    later = {1: {n: w3[n][1:2].astype(BF16) for n in LAYERED},
             2: {n: (w3[n][2:] if n in LAYERED else w3[n]).astype(BF16) for n in BIG if w3[n].shape[0] > 2
                 or n not in LAYERED}}
    an_pad = jnp.zeros((8, 128), F32).at[:a_norm.shape[0]].set(a_norm)
    names0 = list(LAYERED) + ['a_norm']
    full0 = all_gather_weights(names0, [shards0[n] for n in LAYERED] + [an_pad],
                               [_full_shape(n, shards0[n].shape) for n in LAYERED] + [(N_DEV, 8, 128)])
    full0 = dict(zip(names0, full0))
    n_a = a_norm.shape[0]
    small = {n: w[n] for n in SMALL}
    small['a_norm'] = full0['a_norm'][:, :n_a].transpose(1, 0, 2).reshape(n_a, -1)

    ex = _Exchanges(full0, later, core, chip, w3, m3, v3)
    loss_b, grad_x, gb, sgrads = local_step(x[0], loss_target[0], small, ex)
    out = {n: [t.reshape(w[n].shape) for t in bufs] for n, bufs in ex.finish(gb).items()}

    sgrads['loss'] = loss_b[0, :1]
    gathered = all_gather_rows(_pack_small(sgrads))
    shapes = {n: w[n].shape for n in SMALL}
    shapes['a_norm'] = (n_a, a_norm.shape[1] * N_DEV)
    shapes['loss'] = (1,)
    zeros1 = jnp.zeros((1,), F32)

    def packed(src):
        vals = {n: src[n] for n in SMALL}
        vals['a_norm'] = jnp.zeros(shapes['a_norm'], F32)
        vals['loss'] = zeros1
        return _pack_small(vals)

    sm = small_adamw("adamw_small", gathered, packed(w), packed(m), packed(v))
    sm = [_unpack_small(t, shapes) for t in sm]
    g_an = lax.dynamic_slice_in_dim(sm[0]['a_norm'], me * a_norm.shape[1], a_norm.shape[1], axis=1)
    pad = lambda t: jnp.zeros((8, 128), F32).at[:n_a].set(t)
    gathered_an = jnp.zeros((N_DEV, 8, 128), F32).at[0].set(pad(g_an))
    an = small_adamw("adamw_a_norm", gathered_an, pad(a_norm), pad(m_a_norm), pad(v_a_norm))
    for i in range(4):
        sm[i]['a_norm'] = an[i][:n_a]
    for n in BIG:
        for i in range(4):
            sm[i][n] = out[n][i]
    loss = sm[0]['loss'][0]
    return (loss, grad_x[None], *[sm[0][n] for n in WEIGHTS], *[sm[1][n] for n in WEIGHTS],
            *[sm[2][n] for n in WEIGHTS], *[sm[3][n] for n in WEIGHTS])
```

```python
import functools
import math

import numpy as np
import jax
import jax.numpy as jnp
from jax import lax
from jax.experimental import pallas as pl
from jax.experimental.pallas import tpu as pltpu

F32 = jnp.float32
BF16 = jnp.bfloat16
MESH = pl.DeviceIdType.MESH

N_DEV = 8
HEAD_DIM = 64
WINDOW = 128
N_BUCKETS = 32
EPS = 1e-5
NEG_INF = -1e30
Q_SCALE = 1.0 / math.sqrt(HEAD_DIM)
LOG2E = 1.4426950408889634

ADAM_LR, ADAM_B1, ADAM_B2, ADAM_EPS, ADAM_WD, ADAM_STEP = 0.001, 0.9, 0.999, 1e-08, 0.01, 10

SB_BQ = 512
SB_BK = 128
SB_DEAD = 160.0
SB_UNSEEN = 1e30
ROW_TILE = 512
VMEM_LIMIT = 56 * 1024 * 1024

WEIGHTS = ['a_norm', 'a_wqkv', 'a_wo', 'kv_norm', 'w_kv', 'b_kv', 'b_norm', 'b_wq', 'b_bq', 'b_sinks', 'b_wo',
           'b_bo', 'rel_bias', 'mlp_norm', 'mlp_up', 'mlp_down', 'final_norm']
BIG = ['a_wqkv', 'a_wo', 'w_kv', 'b_wq', 'b_wo', 'mlp_up', 'mlp_down']
COL_SHARDED = ('a_wqkv', 'mlp_up')
SMALL = ['a_norm', 'kv_norm', 'b_kv', 'b_norm', 'b_bq', 'b_sinks', 'b_bo', 'rel_bias', 'mlp_norm', 'final_norm']


def _params(sem=None):
    return pltpu.CompilerParams(dimension_semantics=sem, vmem_limit_bytes=VMEM_LIMIT)


def _pick(n, cands):
    for c in cands:
        if n % c == 0:
            return c
    raise ValueError(n)


def _tile(n, want):
    return n if n <= want else _pick(n, (want, want // 2, want // 4))


def mm_nn(name, a, w3, layer, epilogue, extras, out_dtypes):
    m, k = a.shape
    _, kw, n = w3.shape
    assert kw == k
    tm = _tile(m, 1024 if k <= 1024 else 512)
    tn = _tile(n, 1024)
    ne, no = len(extras), len(out_dtypes)

    def body(a_ref, w_ref, *rest):
        ex, outs = rest[:ne], rest[ne:ne + no]
        res = epilogue(jnp.dot(a_ref[...], w_ref[...], preferred_element_type=F32), *[e[...] for e in ex])
        for o, r in zip(outs, res):
            o[...] = r.astype(o.dtype)

    tile = pl.BlockSpec((tm, tn), lambda i, j: (i, j))
    ex_specs = [tile if e.shape[0] == m else pl.BlockSpec((1, tn), lambda i, j: (0, j)) for e in extras]
    return pl.pallas_call(
        body, name=name, grid=(m // tm, n // tn),
        in_specs=[pl.BlockSpec((tm, k), lambda i, j: (i, 0)),
                  pl.BlockSpec((None, k, tn), lambda i, j: (layer, 0, j))] + ex_specs,
        out_specs=[tile] * no,
        out_shape=[jax.ShapeDtypeStruct((m, n), d) for d in out_dtypes],
        compiler_params=_params(("parallel", "parallel")),
    )(a, w3, *extras)


def mm_nt(name, dy, w3, layer, epilogue, extras, out_dtypes):
    m, n = dy.shape
    _, k, nw = w3.shape
    assert nw == n
    tm = _tile(m, 1024 if n <= 1024 else 512)
    tko = _tile(k, 1024)
    ne, no = len(extras), len(out_dtypes)

    def body(a_ref, w_ref, *rest):
        ex, outs = rest[:ne], rest[ne:ne + no]
        acc = lax.dot_general(a_ref[...], w_ref[...], (((1,), (1,)), ((), ())), preferred_element_type=F32)
        res = epilogue(acc, *[e[...] for e in ex])
        for o, v in zip(outs, res):
            o[...] = v.astype(o.dtype)

    tile = pl.BlockSpec((tm, tko), lambda i, ko: (i, ko))
    return pl.pallas_call(
        body, name=name, grid=(m // tm, k // tko),
        in_specs=[pl.BlockSpec((tm, n), lambda i, ko: (i, 0)),
                  pl.BlockSpec((None, tko, n), lambda i, ko: (layer, ko, 0))] + [tile] * ne,
        out_specs=[tile] * no,
        out_shape=[jax.ShapeDtypeStruct((m, k), d) for d in out_dtypes],
        compiler_params=_params(("parallel", "parallel")),
    )(dy, w3, *extras)


def mm_tn(name, x, dy, gbuf, shape, layer):
    s, k = x.shape
    _, kw, n = shape
    assert kw == k and dy.shape == (s, n)
    tkk = _tile(k, 512)
    tn = _tile(n, 1024)

    def body(x_ref, dy_ref, *rest):
        g_out = rest[-1]
        g_out[...] = lax.dot_general(x_ref[...], dy_ref[...], (((0,), (0,)), ((), ())),
                                     preferred_element_type=F32).astype(g_out.dtype)

    prev = [] if gbuf is None else [gbuf]
    return pl.pallas_call(
        body, name=name, grid=(k // tkk, n // tn),
        in_specs=[pl.BlockSpec((s, tkk), lambda ki, j: (0, ki)),
                  pl.BlockSpec((s, tn), lambda ki, j: (0, j))] + [pl.BlockSpec(memory_space=pl.ANY)] * len(prev),
        out_specs=pl.BlockSpec((None, tkk, tn), lambda ki, j: (layer, ki, j)),
        out_shape=jax.ShapeDtypeStruct(shape, BF16),
        input_output_aliases={2: 0} if prev else {},
        compiler_params=_params(("parallel", "parallel")),
    )(x, dy, *prev)


def rms_fwd(name, h, g):
    s, d = h.shape
    tr = _pick(s, (ROW_TILE, 256, 128))

    def body(h_ref, g_ref, o_ref):
        x = h_ref[...]
        r = lax.rsqrt(jnp.mean(x * x, axis=-1, keepdims=True) + EPS)
        o_ref[...] = (x * r * g_ref[...]).astype(o_ref.dtype)

    return pl.pallas_call(
        body, name=name, grid=(s // tr,),
        in_specs=[pl.BlockSpec((tr, d), lambda i: (i, 0)), pl.BlockSpec((1, d), lambda i: (0, 0))],
        out_specs=pl.BlockSpec((tr, d), lambda i: (i, 0)),
        out_shape=jax.ShapeDtypeStruct((s, d), BF16),
        compiler_params=_params(("parallel",)),
    )(h, g.reshape(1, d))


def rms_bwd(name, h, g, dn, dres):
    s, d = h.shape
    tr = _pick(s, (ROW_TILE, 256, 128))

    def body(h_ref, g_ref, dn_ref, dres_ref, dx_ref, dxb_ref, dg_ref, cs_ref):
        i = pl.program_id(0)
        x = h_ref[...]
        r = lax.rsqrt(jnp.mean(x * x, axis=-1, keepdims=True) + EPS)
        xh = x * r
        dn_ = dn_ref[...]
        dyg = dn_ * g_ref[...]
        dx = dres_ref[...] + r * (dyg - xh * jnp.mean(dyg * xh, axis=-1, keepdims=True))
        dx_ref[...] = dx
        dxb_ref[...] = dx.astype(BF16)

        @pl.when(i == 0)
        def _():
            dg_ref[...] = jnp.zeros_like(dg_ref)
            cs_ref[...] = jnp.zeros_like(cs_ref)

        dg_ref[...] += jnp.sum(dn_ * xh, axis=0, keepdims=True)
        cs_ref[...] += jnp.sum(dx, axis=0, keepdims=True)

    row = pl.BlockSpec((tr, d), lambda i: (i, 0))
    vec = pl.BlockSpec((1, d), lambda i: (0, 0))
    return pl.pallas_call(
        body, name=name, grid=(s // tr,),
        in_specs=[row, vec, row, row],
        out_specs=[row, row, vec, vec],
        out_shape=[jax.ShapeDtypeStruct((s, d), F32), jax.ShapeDtypeStruct((s, d), BF16),
                   jax.ShapeDtypeStruct((1, d), F32), jax.ShapeDtypeStruct((1, d), F32)],
        compiler_params=_params(("arbitrary",)),
    )(h, g.reshape(1, d), dn, dres)


def loss_head(h, g, target):
    s, d = h.shape
    tr = _pick(s, (ROW_TILE, 256, 128))

    def body(h_ref, g_ref, t_ref, dx_ref, dxb_ref, dg_ref, loss_ref):
        i = pl.program_id(0)
        x = h_ref[...]
        r = lax.rsqrt(jnp.mean(x * x, axis=-1, keepdims=True) + EPS)
        xh = x * r
        gw = g_ref[...]
        err = xh * gw - t_ref[...]
        dn_ = err * (1.0 / d)
        dyg = dn_ * gw
        dx = r * (dyg - xh * jnp.mean(dyg * xh, axis=-1, keepdims=True))
        dx_ref[...] = dx
        dxb_ref[...] = dx.astype(BF16)

        @pl.when(i == 0)
        def _():
            dg_ref[...] = jnp.zeros_like(dg_ref)
            loss_ref[...] = jnp.zeros_like(loss_ref)

        dg_ref[...] += jnp.sum(dn_ * xh, axis=0, keepdims=True)
        per_row = jnp.sum(err * err, axis=-1, keepdims=True) * (0.5 / d)
        loss_ref[...] += jnp.broadcast_to(jnp.sum(per_row, axis=0, keepdims=True), loss_ref.shape)

    row = pl.BlockSpec((tr, d), lambda i: (i, 0))
    vec = pl.BlockSpec((1, d), lambda i: (0, 0))
    return pl.pallas_call(
        body, name="loss_head", grid=(s // tr,),
        in_specs=[row, vec, row],
        out_specs=[row, row, vec, pl.BlockSpec((1, 128), lambda i: (0, 0))],
        out_shape=[jax.ShapeDtypeStruct((s, d), F32), jax.ShapeDtypeStruct((s, d), BF16),
                   jax.ShapeDtypeStruct((1, d), F32), jax.ShapeDtypeStruct((1, 128), F32)],
        compiler_params=_params(("arbitrary",)),
    )(h, g.reshape(1, d), target)


def colsum(name, x):
    s, n = x.shape
    tr = _pick(s, (ROW_TILE, 256, 128))

    def body(x_ref, o_ref):
        @pl.when(pl.program_id(0) == 0)
        def _():
            o_ref[...] = jnp.zeros_like(o_ref)

        o_ref[...] += jnp.sum(x_ref[...].astype(F32), axis=0, keepdims=True)

    return pl.pallas_call(
        body, name=name, grid=(s // tr,),
        in_specs=[pl.BlockSpec((tr, n), lambda i: (i, 0))],
        out_specs=pl.BlockSpec((1, n), lambda i: (0, 0)),
        out_shape=jax.ShapeDtypeStruct((1, n), F32),
        compiler_params=_params(("arbitrary",)),
    )(x)


def _tri_rows(reverse):
    i = np.arange(SB_BK)
    tri = (i[None, :] >= i[:, None]) if reverse else (i[None, :] <= i[:, None])
    tri = np.concatenate([tri, tri], axis=1)
    return jnp.asarray(np.concatenate([tri, np.ones((8, 2 * SB_BK), bool)], axis=0), BF16)


def _hi_lo_rows(x):
    hi = x.astype(BF16)
    lo = (x - hi.astype(F32)).astype(BF16)
    return jnp.concatenate([hi, lo], axis=0)


def _softplus2(zs):
    neg_abs = lax.bitcast_convert_type(lax.bitcast_convert_type(zs, jnp.uint32) | jnp.uint32(0x80000000), F32)
    return jnp.maximum(zs, 0.0) + jnp.log2(1.0 + jnp.exp2(neg_abs))


def _pair_mask(first_rel_block, bq):
    key = lax.broadcasted_iota(jnp.int32, (2 * SB_BK, bq), 0) + first_rel_block * SB_BK
    qry = lax.broadcasted_iota(jnp.int32, (2 * SB_BK, bq), 1)
    return key < qry


def _row_of(table8, sub8, r):
    return jnp.sum(jnp.where(sub8 == r, table8, 0.0), axis=0, keepdims=True)


def _keys(j0):
    return pl.ds(pl.multiple_of(j0 * SB_BK, 2 * SB_BK), 2 * SB_BK)


class Carry:
    def __init__(self, build, ins, out_shapes, sems):
        self.build, self.ins, self.out_shapes, self.sems = build, list(ins), list(out_shapes), list(sems)


def _carried(carry, rest, n_out, n_scratch, first, last):
    n_ci = len(carry.ins) if carry else 0
    n_co = len(carry.out_shapes) if carry else 0
    cin, outs = rest[:n_ci], rest[n_ci:n_ci + n_out]
    cout = rest[n_ci + n_out:n_ci + n_out + n_co]
    scratch = rest[n_ci + n_out + n_co:n_ci + n_out + n_co + n_scratch]
    csems = rest[n_ci + n_out + n_co + n_scratch:]

    def start():
        if carry:
            @pl.when(first)
            def _():
                for cp in carry.build(cin, cout, *csems):
                    cp.start()

    def wait():
        if carry:
            @pl.when(last)
            def _():
                for cp in carry.build(cin, cout, *csems):
                    cp.wait()

    return outs, scratch, start, wait


def _contract0(a, b):
    return lax.dot_general(a, b, (((0,), (0,)), ((), ())), preferred_element_type=F32)


def _contract1(a, b):
    return lax.dot_general(a, b, (((1,), (1,)), ((), ())), preferred_element_type=F32)


def sb_fwd(name, qkvt, exchange=None):
    nh, dh, s = qkvt.shape[0] // 3, qkvt.shape[1], qkvt.shape[2]
    bq = SB_BQ
    per_q = bq // SB_BK
    nkb = s // SB_BK
    assert s % bq == 0 and per_q == 4 and nkb % 8 == 0

    def body(q_ref, k_ref, v_ref, a_ref, *rest):
        i = pl.program_id(1)
        first = (pl.program_id(0) == 0) & (i == 0)
        last = (pl.program_id(0) == nh - 1) & (i == s // bq - 1)
        (o_ref, rtab_ref), (acc, zbuf, wbuf), start_carried, wait_carried = _carried(exchange, rest, 2, 3, first, last)
        start_carried()
        qb = q_ref[...] * Q_SCALE
        tri = a_ref[...]
        sub8 = lax.broadcasted_iota(jnp.int32, (8, bq), 0)
        acc[...] = jnp.zeros_like(acc)
        rtab_ref[...] = jnp.full(rtab_ref.shape, SB_UNSEEN, F32)
        kf = k_ref[...].astype(F32)
        qf = qb.astype(F32)
        k_max2 = jnp.max(jnp.sum(kf * kf, axis=0, keepdims=True), axis=1, keepdims=True)
        bound = jnp.sqrt(jnp.sum(qf * qf, axis=0, keepdims=True) * k_max2) * (1.001 * LOG2E)

        def scores(j0):
            return _contract0(k_ref[:, _keys(j0)], qb) * LOG2E

        def pair(j0, slot, run, rt8, mask, has_prev):
            zs = zbuf[slot]
            zbuf[1 - slot] = scores(jnp.maximum(j0 - 2, 0))
            if has_prev:
                acc[...] += jnp.dot(v_ref[:, _keys(j0 + 2)], wbuf[1 - slot], preferred_element_type=F32)
            p = _softplus2(zs)
            if mask is not None:
                p = jnp.where(mask, p, 0.0)
            cr1 = jnp.dot(tri, _hi_lo_rows(p[SB_BK:]), preferred_element_type=F32)
            cr0 = jnp.dot(tri, _hi_lo_rows(p[:SB_BK]), preferred_element_type=F32)
            run1 = run + cr1[SB_BK:SB_BK + 1]
            w = jnp.exp2(jnp.concatenate([zs[:SB_BK] - cr0[:SB_BK] - run1, zs[SB_BK:] - cr1[:SB_BK] - run], axis=0))
            if mask is not None:
                w = jnp.where(mask, w, 0.0)
            wbuf[slot] = w.astype(BF16)
            rt8 = jnp.where(j0 % 8 == 6, SB_UNSEEN, rt8)
            rt8 = jnp.where(sub8 == (j0 + 1) % 8, run, jnp.where(sub8 == j0 % 8, run1, rt8))
            rtab_ref[pl.ds(pl.multiple_of((j0 // 8) * 8, 8), 8), :] = rt8
            return run1 + cr0[SB_BK:SB_BK + 1], rt8

        def alive(run):
            return jnp.min(run - bound) < SB_DEAD

        top = i * per_q
        zbuf[0] = scores(top + 2)
        state = (jnp.zeros((1, bq), F32), jnp.full((8, bq), SB_UNSEEN, F32))
        state = pair(top + 2, 0, *state, _pair_mask(2, bq), False)
        state = pair(top, 1, *state, _pair_mask(0, bq), True)

        def step(c):
            it, _, run, rt8 = c
            j0 = top - 2 - 4 * it
            run, rt8 = pair(j0, 0, run, rt8, None, True)
            run, rt8 = pair(j0 - 2, 1, run, rt8, None, True)
            return it + 1, alive(run), run, rt8

        trips = lax.while_loop(lambda c: (c[0] < i) & c[1], step, (0, alive(state[0]), *state))[0]
        acc[...] += jnp.dot(v_ref[:, _keys(top - 4 * trips)], wbuf[1], preferred_element_type=F32)
        o_ref[...] = acc[...].astype(o_ref.dtype)
        wait_carried()

    qspec = pl.BlockSpec((None, dh, bq), lambda h, i: (h, 0, i))
    hbm = pl.BlockSpec(memory_space=pl.ANY)
    c_ins, c_outs, c_sems = (exchange.ins, exchange.out_shapes, exchange.sems) if exchange else ([], [], [])
    outs = pl.pallas_call(
        body, name=name, grid=(nh, s // bq),
        in_specs=[qspec, pl.BlockSpec((None, dh, s), lambda h, i: (h + nh, 0, 0)),
                  pl.BlockSpec((None, dh, s), lambda h, i: (h + 2 * nh, 0, 0)),
                  pl.BlockSpec((SB_BK + 8, 2 * SB_BK), lambda h, i: (0, 0))] + [hbm] * len(c_ins),
        out_specs=[qspec, pl.BlockSpec((None, nkb, bq), lambda h, i: (h, 0, i))] + [hbm] * len(c_outs),
        out_shape=[jax.ShapeDtypeStruct((nh, dh, s), BF16), jax.ShapeDtypeStruct((nh, nkb, s), F32)] + c_outs,
        scratch_shapes=[pltpu.VMEM((dh, bq), F32), pltpu.VMEM((2, 2 * SB_BK, bq), F32),
                        pltpu.VMEM((2, 2 * SB_BK, bq), BF16)] + c_sems,
        compiler_params=_params(("arbitrary", "arbitrary")),
    )(qkvt, qkvt, qkvt, _tri_rows(True), *c_ins)
    return outs[0], outs[1], outs[2:]


def sb_bwd(name, qkvt, dot_, rtab, exchange=None):
    nh, dh, s = qkvt.shape[0] // 3, qkvt.shape[1], qkvt.shape[2]
    bq = SB_BQ
    per_q = bq // SB_BK
    nkb = s // SB_BK
    nq = s // bq

    def body(qt_ref, kt_ref, vt_ref, dot_ref, rtab_ref, ar_ref, af_ref, *rest):
        i = pl.program_id(1)
        first = (pl.program_id(0) == 0) & (i == 0)
        last = (pl.program_id(0) == nh - 1) & (i == nq - 1)
        (dq_ref, dk_ref, dv_ref), (dq_acc, dk_acc, dv_acc, zbuf, dwbuf, dzbuf, wbuf), start_carried, wait_carried = \
            _carried(exchange, rest, 3, 7, first, last)
        start_carried()

        @pl.when(i == 0)
        def _():
            dk_acc[...] = jnp.zeros_like(dk_acc)
            dv_acc[...] = jnp.zeros_like(dv_acc)

        qtb = qt_ref[...] * Q_SCALE
        dotb = dot_ref[...]
        tri_rev = ar_ref[...][:SB_BK]
        tri_fwd = af_ref[...]
        sub8 = lax.broadcasted_iota(jnp.int32, (8, bq), 0)
        dq_acc[...] = jnp.zeros_like(dq_acc)
        last_j = i * per_q + 2
        seen = jnp.max(jnp.where(rtab_ref[...] < 0.1 * SB_UNSEEN, 1.0, 0.0), axis=1, keepdims=True)
        trips = jnp.clip((jnp.sum(seen).astype(jnp.int32) - per_q) // per_q, 0, i)
        first_j = (i - trips) * per_q

        def issue(j0, slot):
            zbuf[slot] = _contract0(kt_ref[:, _keys(j0)], qtb) * LOG2E
            dwbuf[slot] = _contract0(vt_ref[:, _keys(j0)], dotb)

        def retire(j0, slot):
            keys = _keys(j0)
            dq_acc[...] += jnp.dot(kt_ref[:, keys], dzbuf[slot], preferred_element_type=F32)
            dk_acc[:, keys] += _contract1(qtb, dzbuf[slot])
            dv_acc[:, keys] += _contract1(dotb, wbuf[slot])

        def pair(j0, slot, g_run, mask):
            zs = zbuf[slot]
            dw = dwbuf[slot]
            issue(jnp.minimum(j0 + 2, last_j), 1 - slot)
            retire(jnp.maximum(j0 - 2, first_j), 1 - slot)
            p_raw = _softplus2(zs)
            p = p_raw if mask is None else jnp.where(mask, p_raw, 0.0)
            c0 = jnp.dot(tri_rev, _hi_lo_rows(p[:SB_BK]), preferred_element_type=F32)
            c1 = jnp.dot(tri_rev, _hi_lo_rows(p[SB_BK:]), preferred_element_type=F32)
            rt8 = rtab_ref[pl.ds(pl.multiple_of((j0 // 8) * 8, 8), 8), :]
            r0 = _row_of(rt8, sub8, j0 % 8)
            r1 = _row_of(rt8, sub8, (j0 + 1) % 8)
            w = jnp.exp2(jnp.concatenate([zs[:SB_BK] - c0 - r0, zs[SB_BK:] - c1 - r1], axis=0))
            if mask is not None:
                w = jnp.where(mask, w, 0.0)
            g = w * dw
            gg0 = jnp.dot(tri_fwd, _hi_lo_rows(g[:SB_BK]), preferred_element_type=F32)
            gg1 = jnp.dot(tri_fwd, _hi_lo_rows(g[SB_BK:]), preferred_element_type=F32)
            g_run1 = g_run + gg0[SB_BK:SB_BK + 1]
            g_pre = jnp.concatenate([gg0[:SB_BK] + g_run, gg1[:SB_BK] + g_run1], axis=0)
            dz = g - jnp.exp2(zs - p_raw) * g_pre
            if mask is not None:
                dz = jnp.where(mask, dz, 0.0)
            dzbuf[slot] = dz.astype(BF16)
            wbuf[slot] = w.astype(BF16)
            return g_run1 + gg1[SB_BK:SB_BK + 1]

        issue(first_j, 0)
        dzbuf[1] = jnp.zeros((2 * SB_BK, bq), BF16)
        wbuf[1] = jnp.zeros((2 * SB_BK, bq), BF16)

        def step(it, g_run):
            g_run = pair(4 * it, 0, g_run, None)
            return pair(4 * it + 2, 1, g_run, None)

        g_run = lax.fori_loop(i - trips, i, step, jnp.zeros((1, bq), F32))
        g_run = pair(last_j - 2, 0, g_run, _pair_mask(0, bq))
        pair(last_j, 1, g_run, _pair_mask(2, bq))
        retire(last_j, 1)
        dq_ref[...] = (dq_acc[...] * Q_SCALE).astype(dq_ref.dtype)

        @pl.when(i == nq - 1)
        def _():
            dk_ref[...] = dk_acc[...].astype(dk_ref.dtype)
            dv_ref[...] = dv_acc[...].astype(dv_ref.dtype)

        wait_carried()

    tspec = pl.BlockSpec((None, dh, bq), lambda h, i: (h, 0, i))
    fullspec = pl.BlockSpec((None, dh, s), lambda h, i: (h, 0, 0))
    aspec = pl.BlockSpec((SB_BK + 8, 2 * SB_BK), lambda h, i: (0, 0))
    pair_f32 = pltpu.VMEM((2, 2 * SB_BK, bq), F32)
    pair_bf16 = pltpu.VMEM((2, 2 * SB_BK, bq), BF16)
    hbm = pl.BlockSpec(memory_space=pl.ANY)
    c_ins, c_outs, c_sems = (exchange.ins, exchange.out_shapes, exchange.sems) if exchange else ([], [], [])
    outs = pl.pallas_call(
        body, name=name, grid=(nh, s // bq),
        in_specs=[tspec, pl.BlockSpec((None, dh, s), lambda h, i: (h + nh, 0, 0)),
                  pl.BlockSpec((None, dh, s), lambda h, i: (h + 2 * nh, 0, 0)), tspec,
                  pl.BlockSpec((None, nkb, bq), lambda h, i: (h, 0, i)), aspec, aspec] + [hbm] * len(c_ins),
        out_specs=[tspec, fullspec, fullspec] + [hbm] * len(c_outs),
        out_shape=[jax.ShapeDtypeStruct((nh, dh, s), BF16)] * 3 + c_outs,
        scratch_shapes=[pltpu.VMEM((dh, bq), F32), pltpu.VMEM((dh, s), F32), pltpu.VMEM((dh, s), F32),
                        pair_f32, pair_f32, pair_bf16, pair_bf16] + c_sems,
        compiler_params=_params(("arbitrary", "arbitrary")),
    )(qkvt, qkvt, qkvt, dot_, rtab, _tri_rows(True), _tri_rows(False), *c_ins)
    return outs[0], outs[1], outs[2], outs[3:]


SWA_QB = 2


def _swa_probs(qt, kt, bias_t, sink, i):
    cols = qt.shape[1]
    sc = _contract0(kt, qt) + bias_t
    kj = lax.broadcasted_iota(jnp.int32, (2 * WINDOW, cols), 0)
    qi = lax.broadcasted_iota(jnp.int32, (2 * WINDOW, cols), 1) & (WINDOW - 1)
    dist = qi + WINDOW - kj
    valid = (dist >= 0) & (dist < WINDOW) & ((kj >= WINDOW) | (i > 0))
    sc = jnp.where(valid, sc, NEG_INF)
    mx = jnp.maximum(jnp.max(sc, axis=0, keepdims=True), sink)
    p = jnp.exp(sc - mx)
    p_sink = jnp.exp(sink - mx)
    inv = 1.0 / (jnp.sum(p, axis=0, keepdims=True) + p_sink)
    return p, p_sink, inv


def _band(i):
    return pl.ds(pl.multiple_of(i * WINDOW, WINDOW), 2 * WINDOW)


def _heads_to_lanes(blk):
    return jnp.concatenate([blk[r * HEAD_DIM:(r + 1) * HEAD_DIM] for r in range(8)], axis=1)


def _lanes_to_heads(t):
    return jnp.concatenate([t[:, r * WINDOW:(r + 1) * WINDOW] for r in range(8)], axis=0)


def swa_fwd(name, qt, kpt, vpt, bias_t, sink_row):
    d, s = qt.shape
    ng, dh, sp = kpt.shape
    rows, cols = d // ng, SWA_QB * WINDOW
    assert (s // WINDOW) % SWA_QB == 0

    def body(q_ref, k_ref, v_ref, bias_ref, sink_ref, o_ref):
        for u in range(SWA_QB):
            i = pl.program_id(1) * SWA_QB + u
            lanes = slice(u * WINDOW, (u + 1) * WINDOW)
            qb = _heads_to_lanes(q_ref[:, lanes]) * Q_SCALE
            p, _, inv = _swa_probs(qb, k_ref[:, _band(i)], bias_ref[...], sink_ref[...], i)
            o_t = jnp.dot(v_ref[:, _band(i)], p.astype(BF16), preferred_element_type=F32) * inv
            o_ref[:, lanes] = _lanes_to_heads(o_t).astype(o_ref.dtype)

    qspec = pl.BlockSpec((rows, cols), lambda g, i: (g, i))
    kspec = pl.BlockSpec((None, dh, sp), lambda g, i: (g, 0, 0))
    return pl.pallas_call(
        body, name=name, grid=(ng, s // cols),
        in_specs=[qspec, kspec, kspec, pl.BlockSpec((None, 2 * WINDOW, 8 * WINDOW), lambda g, i: (g, 0, 0)),
                  pl.BlockSpec((None, 1, 8 * WINDOW), lambda g, i: (g, 0, 0))],
        out_specs=qspec,
        out_shape=jax.ShapeDtypeStruct(qt.shape, BF16),
        compiler_params=_params(("parallel", "arbitrary")),
    )(qt, kpt, vpt, bias_t, sink_row)


def swa_bwd(name, qt, kpt, vpt, bias_t, sink_row, dot_, dk_in, dv_in):
    d, s = qt.shape
    ng, dh, sp = kpt.shape
    rows, cols = d // ng, SWA_QB * WINDOW

    def body(q_ref, k_ref, v_ref, bias_ref, sink_ref, do_ref, dki_ref, dvi_ref, dq_ref, dk_ref, dv_ref, db_ref, ds_ref):
        @pl.when(pl.program_id(1) == 0)
        def _():
            dk_ref[...] = dki_ref[...]
            dv_ref[...] = dvi_ref[...]
            db_ref[...] = jnp.zeros_like(db_ref)
            ds_ref[...] = jnp.zeros_like(ds_ref)

        for u in range(SWA_QB):
            i = pl.program_id(1) * SWA_QB + u
            band = _band(i)
            lanes = slice(u * WINDOW, (u + 1) * WINDOW)
            qb = _heads_to_lanes(q_ref[:, lanes]) * Q_SCALE
            dob = _heads_to_lanes(do_ref[:, lanes])
            kt = k_ref[:, band]
            p, p_sink, inv = _swa_probs(qb, kt, bias_ref[...], sink_ref[...], i)
            p = p * inv
            dp = _contract0(v_ref[:, band], dob)
            delta = jnp.sum(p * dp, axis=0, keepdims=True)
            dsc = p * (dp - delta)
            ds_ref[...] -= p_sink * inv * delta
            db_ref[...] += dsc
            dscb = dsc.astype(BF16)
            dq_t = jnp.dot(kt, dscb, preferred_element_type=F32) * Q_SCALE
            dq_ref[:, lanes] = _lanes_to_heads(dq_t).astype(dq_ref.dtype)
            dk_ref[:, band] += _contract1(qb, dscb)
            dv_ref[:, band] += _contract1(dob, p.astype(BF16))

    qspec = pl.BlockSpec((rows, cols), lambda g, i: (g, i))
    kspec = pl.BlockSpec((None, dh, sp), lambda g, i: (g, 0, 0))
    bspec = pl.BlockSpec((None, 2 * WINDOW, 8 * WINDOW), lambda g, i: (g, 0, 0))
    sspec = pl.BlockSpec((None, 1, 8 * WINDOW), lambda g, i: (g, 0, 0))
    return pl.pallas_call(
        body, name=name, grid=(ng, s // cols),
        in_specs=[qspec, kspec, kspec, bspec, sspec, qspec, kspec, kspec],
        out_specs=[qspec, kspec, kspec, bspec, sspec],
        out_shape=[jax.ShapeDtypeStruct(qt.shape, BF16), jax.ShapeDtypeStruct(kpt.shape, F32),
                   jax.ShapeDtypeStruct(kpt.shape, F32), jax.ShapeDtypeStruct(bias_t.shape, F32),
                   jax.ShapeDtypeStruct(sink_row.shape, F32)],
        compiler_params=_params(("parallel", "arbitrary")),
    )(qt, kpt, vpt, bias_t, sink_row, dot_, dk_in, dv_in)


def _bucket_onehot():
    qi = np.arange(WINDOW)[:, None]
    kj = np.arange(2 * WINDOW)[None, :]
    n = np.maximum(qi + WINDOW - kj, 0)
    max_exact = N_BUCKETS // 2
    nf = np.maximum(n, 1).astype(np.float64)
    val = np.log(nf / max_exact) / math.log(WINDOW / max_exact) * (N_BUCKETS - max_exact)
    assert np.all(np.abs(val - np.round(val))[(n > max_exact) & (n < WINDOW)] > 1e-3)
    large = np.minimum(max_exact + val.astype(np.int64), N_BUCKETS - 1)
    bucket = np.where(n < max_exact, n, large).reshape(-1)
    onehot = np.zeros((128, bucket.size), np.float32)
    onehot[bucket, np.arange(bucket.size)] = 1.0
    return onehot


def _split3(x):
    a = x.astype(BF16)
    r = x - a.astype(F32)
    b = r.astype(BF16)
    c = (r - b.astype(F32)).astype(BF16)
    return a, b, c


def bias_table(rel_bias):
    nh = rel_bias.shape[1]
    oh = jnp.asarray(_bucket_onehot(), BF16)
    n = oh.shape[1]
    tn = 4096
    rb = jnp.zeros((nh, 128), F32).at[:, :N_BUCKETS].set(rel_bias.T)

    def body(rb_ref, oh_ref, o_ref):
        o_ref[...] = sum(jnp.dot(t, oh_ref[...], preferred_element_type=F32) for t in _split3(rb_ref[...]))

    return pl.pallas_call(
        body, name="bias_table", grid=(n // tn,),
        in_specs=[pl.BlockSpec((nh, 128), lambda i: (0, 0)), pl.BlockSpec((128, tn), lambda i: (0, i))],
        out_specs=pl.BlockSpec((nh, tn), lambda i: (0, i)),
        out_shape=jax.ShapeDtypeStruct((nh, n), F32),
        compiler_params=_params(("parallel",)),
    )(rb, oh)


def bias_table_grad(db0, db1):
    nh, n = db0.shape
    oh = jnp.asarray(_bucket_onehot(), BF16)
    tn = 4096

    def body(a_ref, b_ref, oh_ref, o_ref):
        @pl.when(pl.program_id(0) == 0)
        def _():
            o_ref[...] = jnp.zeros_like(o_ref)

        o_ref[...] += sum(lax.dot_general(t, oh_ref[...], (((1,), (1,)), ((), ())), preferred_element_type=F32)
                          for t in _split3(a_ref[...] + b_ref[...]))

    blk = pl.BlockSpec((nh, tn), lambda i: (0, i))
    return pl.pallas_call(
        body, name="bias_table_grad", grid=(n // tn,),
        in_specs=[blk, blk, pl.BlockSpec((128, tn), lambda i: (0, i))],
        out_specs=pl.BlockSpec((nh, 128), lambda i: (0, 0)),
        out_shape=jax.ShapeDtypeStruct((nh, 128), F32),
        compiler_params=_params(("arbitrary",)),
    )(db0, db1, oh)


def _owner_view(ref, name, d):
    if name == 'a_norm':
        return ref.at[d]
    if name in COL_SHARDED:
        n = ref.shape[2] // N_DEV
        return ref.at[:, :, pl.ds(pl.multiple_of(d * n, 128), n)]
    return ref.at[:, d]


def _place():
    return lax.axis_index("x"), lax.axis_index("y"), lax.axis_index("c")


def _dev(p):
    return 4 * p[0] + 2 * p[1] + p[2]


def _remote(src, dst, send_sem, recv_sem, to):
    return pltpu.make_async_remote_copy(src_ref=src, dst_ref=dst, send_sem=send_sem, recv_sem=recv_sem,
                                        device_id=to, device_id_type=MESH)


def _dma_sems(*shapes):
    return [pltpu.SemaphoreType.DMA(sh) for sh in shapes]


def comm_call(name, build, ins, out_shapes, sems, aliases=None):
    n_in, n_out = len(ins), len(out_shapes)

    def body(*refs):
        copies = build(refs[:n_in], refs[n_in:n_in + n_out], *refs[n_in + n_out:])
        for cp in copies:
            cp.start()
        for cp in copies:
            cp.wait()

    hbm = pl.BlockSpec(memory_space=pl.ANY)
    return pl.pallas_call(
        body, name=name, in_specs=[hbm] * n_in, out_specs=[hbm] * n_out, out_shape=list(out_shapes),
        scratch_shapes=sems, input_output_aliases=aliases or {},
    )(*ins)


def all_gather_weights(names, shards, full_shapes):
    n = len(names)

    def body(*refs):
        ins, outs = refs[:n], refs[n:2 * n]
        send_sems, recv_sems, local_sems = refs[2 * n:]
        x, y, c = _place()
        me, sibling = (x, y, c), (x, y, 1 - c)
        chips = [(1 - x, y), (x, 1 - y), (1 - x, 1 - y)]

        def copy(t, k, block, to, src=None):
            dst = _owner_view(outs[t], names[t], _dev(block))
            return _remote(dst if src is None else src, dst, send_sems.at[t, k], recv_sems.at[t, k], to)

        mine = [pltpu.make_async_copy(ins[t], _owner_view(outs[t], names[t], _dev(me)), local_sems.at[t])
                for t in range(n)]
        for cp in mine:
            cp.start()
        first = []
        for t in range(n):
            first.append(copy(t, 0, me, sibling, src=ins[t]))
            first += [copy(t, 1 + j, me, (*chip, c), src=ins[t]) for j, chip in enumerate(chips)]
        for cp in first:
            cp.start()
        passed = []
        for j, chip in enumerate(chips):
            for t in range(n):
                copy(t, 1 + j, (*chip, c), me).wait_recv()
                fwd = copy(t, 4 + j, (*chip, c), sibling)
                fwd.start()
                passed.append(fwd)
        for t in range(n):
            copy(t, 0, sibling, me).wait_recv()
            for j, chip in enumerate(chips):
                copy(t, 4 + j, (*chip, 1 - c), me).wait_recv()
        for cp in first + passed:
            cp.wait_send()
        for cp in mine:
            cp.wait()

    hbm = pl.BlockSpec(memory_space=pl.ANY)
    return pl.pallas_call(
        body, name="all_gather_layer0",
        in_specs=[hbm] * n, out_specs=[hbm] * n,
        out_shape=[jax.ShapeDtypeStruct(full_shapes[t], shards[t].dtype) for t in range(n)],
        scratch_shapes=_dma_sems((n, 7), (n, 7), (n,)),
    )(*shards)


def ag_direct(names, shards, full_shapes):
    n = len(names)

    def build(ins, outs, send_sems, recv_sems, local_sems):
        x, y, c = _place()
        peers = [(x, y, 1 - c), (1 - x, y, c), (x, 1 - y, c), (1 - x, 1 - y, c)]
        copies = []
        for t in range(n):
            dst = _owner_view(outs[t], names[t], _dev((x, y, c)))
            copies.append(pltpu.make_async_copy(ins[t], dst, local_sems.at[t]))
            copies += [_remote(ins[t], dst, send_sems.at[t, k], recv_sems.at[t, k], to) for k, to in enumerate(peers)]
        return copies

    return Carry(build, shards, [jax.ShapeDtypeStruct(full_shapes[t], shards[t].dtype) for t in range(n)],
                 _dma_sems((n, 4), (n, 4), (n,)))


def ag_forward(tag, names, partial):
    n = len(names)

    def build(ins, outs, send_sems, recv_sems):
        del ins
        x, y, c = _place()
        copies = []
        for t in range(n):
            for k, chip in enumerate([(1 - x, y), (x, 1 - y), (1 - x, 1 - y)]):
                view = _owner_view(outs[t], names[t], _dev((*chip, c)))
                copies.append(_remote(view, view, send_sems.at[t, k], recv_sems.at[t, k], (x, y, 1 - c)))
        return copies

    return comm_call(f"all_gather_forward_{tag}", build, partial, [jax.ShapeDtypeStruct(p.shape, p.dtype) for p in partial],
                     _dma_sems((n, 3), (n, 3)), aliases={t: t for t in range(n)})


def sibling_exchange(tag, names, grads, part_shapes):
    n = len(names)

    def build(ins, outs, send_sems, recv_sems):
        x, y, c = _place()
        return [_remote(_owner_view(ins[t], names[t], 2 * q + 1 - c), outs[t].at[q], send_sems.at[t, q],
                        recv_sems.at[t, q], (x, y, 1 - c)) for t in range(n) for q in range(4)]

    return comm_call(f"rs_sibling_exchange_{tag}", build, grads,
                     [jax.ShapeDtypeStruct((4,) + part_shapes[t], BF16) for t in range(n)], _dma_sems((n, 4), (n, 4)))


def chip_exchange(names, parts, part_shapes):
    n = len(names)

    def build(ins, outs, send_sems, recv_sems):
        x, y, c = _place()
        chips = [(1 - x, y), (x, 1 - y), (1 - x, 1 - y)]
        return [_remote(ins[t].at[2 * chip[0] + chip[1]], outs[t].at[k], send_sems.at[t, k], recv_sems.at[t, k],
                        (*chip, c)) for t in range(n) for k, chip in enumerate(chips)]

    return Carry(build, parts, [jax.ShapeDtypeStruct((3,) + part_shapes[t], BF16) for t in range(n)],
                 _dma_sems((n, 3), (n, 3)))


def all_gather_rows(x):
    r, w = x.shape

    def body(x_ref, out_ref, send_sems, recv_sems, local_sem):
        px, py, pc = _place()
        me = 4 * px + 2 * py + pc
        mine = pltpu.make_async_copy(x_ref, out_ref.at[me], local_sem)
        mine.start()
        copies = []
        for k in range(1, N_DEV):
            peer = (px ^ (k >> 2), py ^ ((k >> 1) & 1), pc ^ (k & 1))
            copies.append(pltpu.make_async_remote_copy(
                src_ref=x_ref, dst_ref=out_ref.at[me], send_sem=send_sems.at[k - 1], recv_sem=recv_sems.at[k - 1],
                device_id=peer, device_id_type=MESH))
        for cp in copies:
            cp.start()
        for k in range(1, N_DEV):
            peer_idx = me ^ k
            pltpu.make_async_remote_copy(
                src_ref=x_ref, dst_ref=out_ref.at[peer_idx], send_sem=send_sems.at[k - 1],
                recv_sem=recv_sems.at[k - 1], device_id=(px, py, pc), device_id_type=MESH).wait_recv()
        for cp in copies:
            cp.wait_send()
        mine.wait()

    vmem = pl.BlockSpec(memory_space=pltpu.VMEM)
    return pl.pallas_call(
        body, name="all_gather_small_grads",
        in_specs=[vmem], out_specs=vmem,
        out_shape=jax.ShapeDtypeStruct((N_DEV, r, w), x.dtype),
        scratch_shapes=[pltpu.SemaphoreType.DMA((N_DEV - 1,)), pltpu.SemaphoreType.DMA((N_DEV - 1,)),
                        pltpu.SemaphoreType.DMA],
    )(x)


def _adamw(w, g, m, v):
    m = ADAM_B1 * m + (1.0 - ADAM_B1) * g
    v = ADAM_B2 * v + (1.0 - ADAM_B2) * (g * g)
    m_hat = m / (1.0 - ADAM_B1 ** ADAM_STEP)
    v_hat = v / (1.0 - ADAM_B2 ** ADAM_STEP)
    return -ADAM_LR * (m_hat / (jnp.sqrt(v_hat) + ADAM_EPS) + ADAM_WD * w), m, v


def sibling_sum(name, col, grads, recv, core):
    _, nl, rows, cols = recv.shape
    tr = _tile(rows, 512)
    rspec = pl.BlockSpec((None, None, tr, cols), lambda q, l, i, c_ref: (q, l, i, 0))
    if col:
        gspec = pl.BlockSpec((None, tr, cols), lambda q, l, i, c_ref: (l, i, 2 * q + c_ref[0]))
    else:
        gspec = pl.BlockSpec((None, None, tr, cols), lambda q, l, i, c_ref: (l, 2 * q + c_ref[0], i, 0))

    def body(c_ref, g_ref, r_ref, o_ref):
        del c_ref
        o_ref[...] = (g_ref[...].astype(F32) + r_ref[...].astype(F32)).astype(BF16)

    return pl.pallas_call(
        body, name=name,
        grid_spec=pltpu.PrefetchScalarGridSpec(num_scalar_prefetch=1, grid=(4, nl, rows // tr),
                                               in_specs=[gspec, rspec], out_specs=rspec),
        out_shape=jax.ShapeDtypeStruct(recv.shape, BF16),
        compiler_params=_params(("parallel", "parallel", "parallel")),
    )(core.reshape(1), grads, recv)


def reduce_adamw(name, parts, recv, chip, w, m, v, l0, prev):
    _, nl, rows, cols = parts.shape
    tr = _tile(rows, 256)

    def body(q_ref, p_ref, r_ref, w_ref, m_ref, v_ref, *rest):
        del q_ref
        g_out, d_out, m_out, v_out = rest[-4:]
        g = ((p_ref[...].astype(F32) + r_ref[0].astype(F32)) + r_ref[1].astype(F32)) + r_ref[2].astype(F32)
        d, mn, vn = _adamw(w_ref[...], g, m_ref[...], v_ref[...])
        g_out[...] = g
        d_out[...] = d
        m_out[...] = mn
        v_out[...] = vn

    blk = pl.BlockSpec((None, tr, cols), lambda l, i, q_ref: (l0 + l, i, 0))
    prev = list(prev) if prev else []
    return pl.pallas_call(
        body, name=name,
        grid_spec=pltpu.PrefetchScalarGridSpec(
            num_scalar_prefetch=1, grid=(nl, rows // tr),
            in_specs=[pl.BlockSpec((None, None, tr, cols), lambda l, i, q_ref: (q_ref[0], l, i, 0)),
                      pl.BlockSpec((3, None, tr, cols), lambda l, i, q_ref: (0, l, i, 0)), blk, blk, blk]
            + [pl.BlockSpec(memory_space=pl.ANY)] * len(prev),
            out_specs=[blk] * 4),
        out_shape=[jax.ShapeDtypeStruct(w.shape, F32)] * 4,
        input_output_aliases={6 + i: i for i in range(len(prev))},
        compiler_params=_params(("parallel", "parallel")),
    )(chip.reshape(1), parts, recv, w, m, v, *prev)


def small_adamw(name, gathered, w, m, v):
    _, r, c = gathered.shape

    def body(ga_ref, w_ref, m_ref, v_ref, g_out, d_out, m_out, v_out):
        g = ga_ref[0]
        for d in range(1, N_DEV):
            g = g + ga_ref[d]
        dl, mn, vn = _adamw(w_ref[...], g, m_ref[...], v_ref[...])
        g_out[...] = g
        d_out[...] = dl
        m_out[...] = mn
        v_out[...] = vn

    return pl.pallas_call(
        body, name=name,
        out_shape=[jax.ShapeDtypeStruct((r, c), F32)] * 4,
        compiler_params=_params(),
    )(gathered, w, m, v)


def _add(acc, *ex):
    return (acc + ex[0],)


def local_step(x, target, small, ex):
    s, d = x.shape
    n_a, n_b = small['a_norm'].shape[0], small['b_norm'].shape[0]
    sg = {}
    gb = {}

    def fwd_mm(name, a, wname, layer, epilogue, extras, out_dtypes):
        return mm_nn(name, a, *ex.weight(wname, layer), epilogue, extras, out_dtypes)

    def dx_mm(name, dy, wname, layer, epilogue, extras, out_dtypes):
        return mm_nt(name, dy, *ex.weight(wname, layer), epilogue, extras, out_dtypes)

    def dw_mm(name, a, dy, wname, layer):
        key, slab, shape = ex.grad(wname, layer)
        gb[key] = mm_tn(name, a, dy, gb.get(key), shape, slab)

    bias_flat = bias_table(small['rel_bias'])
    bias_t = bias_flat.reshape(2, 8, WINDOW, 2 * WINDOW).transpose(0, 3, 1, 2).reshape(2, 2 * WINDOW, 8 * WINDOW)
    sink_rows = [jnp.repeat(small['b_sinks'][j], WINDOW).reshape(2, 1, 8 * WINDOW) for j in range(n_b)]

    def mlp_fwd(h, layer):
        n2 = rms_fwd(f"mlp_norm_fwd{layer}", h, small['mlp_norm'][layer])
        u, a = fwd_mm(f"mlp_up_fwd{layer}", n2, 'mlp_up', layer,
                      lambda acc: (acc, jnp.square(jnp.maximum(acc, 0.0))), (), (BF16, BF16))
        (h2,) = fwd_mm(f"mlp_down_fwd{layer}", a, 'mlp_down', layer, _add, (h,), (F32,))
        return h2, (n2, u, a)

    h = x
    saved = []
    for l in range(n_a):
        n1 = rms_fwd(f"a_norm_fwd{l}", h, small['a_norm'][l])
        (qkv,) = fwd_mm(f"a_qkv_fwd{l}", n1, 'a_wqkv', l, lambda acc: (acc,), (), (BF16,))
        qkvt = qkv.T.reshape(3 * d // HEAD_DIM, HEAD_DIM, s)
        o_t, rtab, carried = sb_fwd(f"sb_fwd{l}", qkvt, ex.fwd_carry(l))
        ex.fwd_done(l, carried)
        o = o_t.reshape(d, s).T
        (h_mid,) = fwd_mm(f"a_wo_fwd{l}", o, 'a_wo', l, _add, (h,), (F32,))
        h_out, mlp_saved = mlp_fwd(h_mid, l)
        saved.append((h, n1, qkvt, o, rtab, h_mid, mlp_saved))
        h = h_out
    h_kv = h
    nkv = rms_fwd("kv_norm_fwd", h, small['kv_norm'])
    (kv,) = fwd_mm("kv_fwd", nkv, 'w_kv', 0, lambda acc, b: (acc + b,), (small['b_kv'].reshape(1, -1),), (BF16,))
    kvt = kv.T.reshape(2, 2, HEAD_DIM, s)
    kpt, vpt = (jnp.pad(t, ((0, 0), (0, 0), (WINDOW, 0))) for t in (kvt[0], kvt[1]))
    for j in range(n_b):
        layer = n_a + j
        n1 = rms_fwd(f"b_norm_fwd{j}", h, small['b_norm'][j])
        (qb,) = fwd_mm(f"b_q_fwd{j}", n1, 'b_wq', j, lambda acc, b: (acc + b,),
                       (small['b_bq'][j].reshape(1, -1),), (BF16,))
        qbt = qb.T
        o = swa_fwd(f"swa_fwd{j}", qbt, kpt, vpt, bias_t, sink_rows[j]).T
        (h_mid,) = fwd_mm(f"b_wo_fwd{j}", o, 'b_wo', j, lambda acc, hh, b: (acc + hh + b,),
                          (h, small['b_bo'][j].reshape(1, -1)), (F32,))
        h_out, mlp_saved = mlp_fwd(h_mid, layer)
        saved.append((h, n1, qbt, o, h_mid, mlp_saved))
        h = h_out

    dh, dhb, dg_final, loss_b = loss_head(h, small['final_norm'], target)
    sg['final_norm'] = dg_final[0]
    sg['mlp_norm'] = [None] * (n_a + n_b)

    def mlp_bwd(dh, dhb, h_mid, mlp_saved, layer):
        n2, u, a = mlp_saved
        (du,) = dx_mm(f"mlp_down_dx{layer}", dhb, 'mlp_down', layer,
                      lambda acc, uu: (acc * (2.0 * jnp.maximum(uu.astype(F32), 0.0)),), (u,), (BF16,))
        dw_mm(f"mlp_down_dw{layer}", a, dhb, 'mlp_down', layer)
        (dn2,) = dx_mm(f"mlp_up_dx{layer}", du, 'mlp_up', layer, lambda acc: (acc,), (), (F32,))
        dw_mm(f"mlp_up_dw{layer}", n2, du, 'mlp_up', layer)
        dh2, dh2b, dg, cs = rms_bwd(f"mlp_norm_bwd{layer}", h_mid, small['mlp_norm'][layer], dn2, dh)
        sg['mlp_norm'][layer] = dg[0]
        return dh2, dh2b, cs

    dkp = jnp.zeros(kpt.shape, F32)
    dvp = jnp.zeros(vpt.shape, F32)
    sg['b_norm'], sg['b_bq'], sg['b_bo'], sg['b_sinks'] = [None] * n_b, [None] * n_b, [None] * n_b, [None] * n_b
    dbias = [None] * n_b
    for j in reversed(range(n_b)):
        layer = n_a + j
        h_in, n1, qbt, o, h_mid, mlp_saved = saved[layer]
        dh, dhb, cs = mlp_bwd(dh, dhb, h_mid, mlp_saved, layer)
        sg['b_bo'][j] = cs[0]
        (do,) = dx_mm(f"b_wo_dx{j}", dhb, 'b_wo', j, lambda acc: (acc,), (), (BF16,))
        dw_mm(f"b_wo_dw{j}", o, dhb, 'b_wo', j)
        dq_t, dkp, dvp, dbias[j], dsink = swa_bwd(f"swa_bwd{j}", qbt, kpt, vpt, bias_t, sink_rows[j], do.T, dkp, dvp)
        sg['b_sinks'][j] = colsum(f"sink_grad{j}", dsink.reshape(16, WINDOW).T)[0]
        dq = dq_t.T
        sg['b_bq'][j] = colsum(f"b_bq_grad{j}", dq)[0]
        (dn1,) = dx_mm(f"b_q_dx{j}", dq, 'b_wq', j, lambda acc: (acc,), (), (F32,))
        dw_mm(f"b_q_dw{j}", n1, dq, 'b_wq', j)
        dh, dhb, dg, _ = rms_bwd(f"b_norm_bwd{j}", h_in, small['b_norm'][j], dn1, dh)
        sg['b_norm'][j] = dg[0]
    unt = lambda t: t.reshape(2, 2 * WINDOW, 8, WINDOW).transpose(0, 2, 3, 1).reshape(bias_flat.shape)
    sg['rel_bias'] = bias_table_grad(unt(dbias[0]), unt(dbias[1]))[:, :N_BUCKETS].T

    dkv = jnp.concatenate([dkp[:, :, WINDOW:], dvp[:, :, WINDOW:]], axis=0).reshape(-1, s).T
    sg['b_kv'] = colsum("b_kv_grad", dkv)[0]
    dkvb = dkv.astype(BF16)
    (dnkv,) = dx_mm("kv_dx", dkvb, 'w_kv', 0, lambda acc: (acc,), (), (F32,))
    dw_mm("kv_dw", nkv, dkvb, 'w_kv', 0)
    dh, dhb, dg, _ = rms_bwd("kv_norm_bwd", h_kv, small['kv_norm'], dnkv, dh)
    sg['kv_norm'] = dg[0]

    sg['a_norm'] = [None] * n_a
    for l in reversed(range(n_a)):
        h_in, n1, qkvt, o, rtab, h_mid, mlp_saved = saved[l]
        dh, dhb, _ = mlp_bwd(dh, dhb, h_mid, mlp_saved, l)
        (do,) = dx_mm(f"a_wo_dx{l}", dhb, 'a_wo', l, lambda acc: (acc,), (), (BF16,))
        dw_mm(f"a_wo_dw{l}", o, dhb, 'a_wo', l)
        dq_t, dk_t, dv_t, carried = sb_bwd(f"sb_bwd{l}", qkvt, do.T.reshape(d // HEAD_DIM, HEAD_DIM, s), rtab,
                                           ex.bwd_carry(l, gb))
        ex.bwd_done(l, carried)
        dqkv = jnp.concatenate([dq_t, dk_t, dv_t], axis=0).reshape(3 * d, s).T
        (dn1,) = dx_mm(f"a_qkv_dx{l}", dqkv, 'a_wqkv', l, lambda acc: (acc,), (), (F32,))
        dw_mm(f"a_qkv_dw{l}", n1, dqkv, 'a_wqkv', l)
        dh, dhb, dg, _ = rms_bwd(f"a_norm_bwd{l}", h_in, small['a_norm'][l], dn1, dh)
        sg['a_norm'][l] = dg[0]

    small_grads = {
        'a_norm': jnp.stack(sg['a_norm']), 'kv_norm': sg['kv_norm'], 'b_kv': sg['b_kv'],
        'b_norm': jnp.stack(sg['b_norm']), 'b_bq': jnp.stack(sg['b_bq']), 'b_sinks': jnp.stack(sg['b_sinks']),
        'b_bo': jnp.stack(sg['b_bo']), 'rel_bias': sg['rel_bias'], 'mlp_norm': jnp.stack(sg['mlp_norm']),
        'final_norm': sg['final_norm'],
    }
    return loss_b, dh, gb, small_grads


def _full_shape(name, shard_shape):
    if name in COL_SHARDED:
        return shard_shape[:2] + (N_DEV * shard_shape[2],)
    nl, r, n = shard_shape
    return (nl, N_DEV, r, n)


def _as_w3_shape(name, shard_shape):
    full = _full_shape(name, shard_shape)
    return full if name in COL_SHARDED else (full[0], full[1] * full[2], full[3])


def _as_w3(name, full):
    if name in COL_SHARDED:
        return full
    nl, nd, r, n = full.shape
    return full.reshape(nl, nd * r, n)


AG_GROUPS = {
    0: (('a_wqkv', 0, 1),),
    1: (('a_wo', 0, 2), ('mlp_up', 0, 2), ('mlp_down', 0, 2), ('a_wqkv', 1, 1)),
    2: (('mlp_up', 2, 2), ('mlp_down', 2, 2), ('b_wq', 0, 2), ('b_wo', 0, 2), ('w_kv', 0, 1)),
}
RS_GROUPS = {
    'A': (('mlp_up', 1, 3), ('mlp_down', 1, 3), ('b_wq', 0, 2), ('b_wo', 0, 2), ('w_kv', 0, 1)),
    'B': (('a_wqkv', 1, 1), ('a_wo', 1, 1), ('mlp_up', 0, 1), ('mlp_down', 0, 1)),
    'C': (('a_wqkv', 0, 1), ('a_wo', 0, 1)),
}


class _Exchanges:
    def __init__(self, full0, shards, core, chip, w3, m3, v3):
        self.wbuf = {0: {n: _as_w3(n, full0[n]) for n, _, _ in AG_GROUPS[0]}}
        self.shards, self.core, self.chip = shards, core, chip
        self.w3, self.m3, self.v3 = w3, m3, v3
        self.shard_dims = {n: w3[n].shape[1:] for n in BIG}
        self.parts = {}
        self.out = {}

    def weight(self, name, layer):
        for group, members in AG_GROUPS.items():
            for n, l0, nl in members:
                if n == name and l0 <= layer < l0 + nl:
                    return self.wbuf[group][name], layer - l0
        raise KeyError((name, layer))

    def fwd_carry(self, layer):
        names = [n for n, _, _ in AG_GROUPS[layer + 1]]
        shards = [self.shards[layer + 1][n] for n in names]
        return ag_direct(names, shards, [_full_shape(n, sh.shape) for n, sh in zip(names, shards)])

    def fwd_done(self, layer, carried):
        names = [n for n, _, _ in AG_GROUPS[layer + 1]]
        self.wbuf[layer + 1] = {n: _as_w3(n, f) for n, f in zip(names, ag_forward(layer + 1, names, list(carried)))}

    def grad(self, name, layer):
        for group, members in RS_GROUPS.items():
            for n, l0, nl in members:
                if n == name and l0 <= layer < l0 + nl:
                    return (group, name), layer - l0, _as_w3_shape(name, (nl,) + self.shard_dims[name])
        raise KeyError((name, layer))

    def _sibling_stage(self, group, gb):
        names = [n for n, _, _ in RS_GROUPS[group]]
        shapes = [(nl,) + self.shard_dims[n] for n, _, nl in RS_GROUPS[group]]
        gfull = [gb[(group, n)].reshape(_full_shape(n, sh)) for n, sh in zip(names, shapes)]
        recv = sibling_exchange(group, names, gfull, shapes)
        self.parts[group] = [sibling_sum(f"rs_sibling_sum_{group}_{n}", n in COL_SHARDED, g, r, self.core)
                             for n, g, r in zip(names, gfull, recv)]
        return names, shapes

    def bwd_carry(self, layer, gb):
        names, shapes = self._sibling_stage('A' if layer == 1 else 'B', gb)
        return chip_exchange(names, self.parts['A' if layer == 1 else 'B'], shapes)

    def bwd_done(self, layer, carried):
        self._adamw('A' if layer == 1 else 'B', carried)

    def finish(self, gb):
        names, shapes = self._sibling_stage('C', gb)
        ce = chip_exchange(names, self.parts['C'], shapes)
        self._adamw('C', comm_call("rs_chip_exchange_C", ce.build, ce.ins, ce.out_shapes, ce.sems))
        return self.out

    def _adamw(self, group, recv2):
        for (n, l0, _), p, r in zip(RS_GROUPS[group], self.parts[group], recv2):
            self.out[n] = reduce_adamw(f"adamw_{group}_{n}", p, r, self.chip, self.w3[n], self.m3[n], self.v3[n],
                                       l0, self.out.get(n))


def _pack_small(vals):
    flat = jnp.concatenate([vals[n].reshape(-1).astype(F32) for n in SMALL] + [vals['loss'].reshape(-1)])
    rows = -(-flat.shape[0] // 1024) * 8
    return jnp.pad(flat, (0, rows * 128 - flat.shape[0])).reshape(rows, 128)


def _unpack_small(packed, shapes):
    flat = packed.reshape(-1)
    out, off = {}, 0
    for n in SMALL + ['loss']:
        size = int(np.prod(shapes[n]))
        out[n] = flat[off:off + size].reshape(shapes[n])
        off += size
    return out


def kernel(x, a_norm, a_wqkv, a_wo, kv_norm, w_kv, b_kv, b_norm, b_wq, b_bq, b_sinks, b_wo, b_bo, rel_bias, mlp_norm, mlp_up, mlp_down, final_norm, loss_target, m_a_norm, m_a_wqkv, m_a_wo, m_kv_norm, m_w_kv, m_b_kv, m_b_norm, m_b_wq, m_b_bq, m_b_sinks, m_b_wo, m_b_bo, m_rel_bias, m_mlp_norm, m_mlp_up, m_mlp_down, m_final_norm, v_a_norm, v_a_wqkv, v_a_wo, v_kv_norm, v_w_kv, v_b_kv, v_b_norm, v_b_wq, v_b_bq, v_b_sinks, v_b_wo, v_b_bo, v_rel_bias, v_mlp_norm, v_mlp_up, v_mlp_down, v_final_norm):
    w = dict(a_norm=a_norm, a_wqkv=a_wqkv, a_wo=a_wo, kv_norm=kv_norm, w_kv=w_kv, b_kv=b_kv, b_norm=b_norm,
             b_wq=b_wq, b_bq=b_bq, b_sinks=b_sinks, b_wo=b_wo, b_bo=b_bo, rel_bias=rel_bias, mlp_norm=mlp_norm,
             mlp_up=mlp_up, mlp_down=mlp_down, final_norm=final_norm)
    m = dict(a_norm=m_a_norm, a_wqkv=m_a_wqkv, a_wo=m_a_wo, kv_norm=m_kv_norm, w_kv=m_w_kv, b_kv=m_b_kv,
             b_norm=m_b_norm, b_wq=m_b_wq, b_bq=m_b_bq, b_sinks=m_b_sinks, b_wo=m_b_wo, b_bo=m_b_bo,
             rel_bias=m_rel_bias, mlp_norm=m_mlp_norm, mlp_up=m_mlp_up, mlp_down=m_mlp_down, final_norm=m_final_norm)
    v = dict(a_norm=v_a_norm, a_wqkv=v_a_wqkv, a_wo=v_a_wo, kv_norm=v_kv_norm, w_kv=v_w_kv, b_kv=v_b_kv,
             b_norm=v_b_norm, b_wq=v_b_wq, b_bq=v_b_bq, b_sinks=v_b_sinks, b_wo=v_b_wo, b_bo=v_b_bo,
             rel_bias=v_rel_bias, mlp_norm=v_mlp_norm, mlp_up=v_mlp_up, mlp_down=v_mlp_down, final_norm=v_final_norm)
    px, py, pc = _place()
    me = 4 * px + 2 * py + pc
    chip = (2 * px + py).astype(jnp.int32)
    core = pc.astype(jnp.int32)

    as3 = lambda t: t[None] if t.ndim == 2 else t
    w3, m3, v3 = ({n: as3(src[n]) for n in BIG} for src in (w, m, v))
    shards = {g: {n: w3[n][l0:l0 + nl].astype(BF16) for n, l0, nl in members} for g, members in AG_GROUPS.items()}
    an_pad = jnp.zeros((8, 128), F32).at[:a_norm.shape[0]].set(a_norm)
    names0 = [n for n, _, _ in AG_GROUPS[0]]
    full0 = all_gather_weights(names0 + ['a_norm'], [shards[0][n] for n in names0] + [an_pad],
                               [_full_shape(n, shards[0][n].shape) for n in names0] + [(N_DEV, 8, 128)])
    full0 = dict(zip(names0 + ['a_norm'], full0))
    n_a = a_norm.shape[0]
    small = {n: w[n] for n in SMALL}
    small['a_norm'] = full0['a_norm'][:, :n_a].transpose(1, 0, 2).reshape(n_a, -1)

    ex = _Exchanges(full0, shards, core, chip, w3, m3, v3)
    loss_b, grad_x, gb, sgrads = local_step(x[0], loss_target[0], small, ex)
    out = {n: [t.reshape(w[n].shape) for t in bufs] for n, bufs in ex.finish(gb).items()}

    sgrads['loss'] = loss_b[0, :1]
    gathered = all_gather_rows(_pack_small(sgrads))
    shapes = {n: w[n].shape for n in SMALL}
    shapes['a_norm'] = (n_a, a_norm.shape[1] * N_DEV)
    shapes['loss'] = (1,)
    zeros1 = jnp.zeros((1,), F32)

    def packed(src):
        vals = {n: src[n] for n in SMALL}
        vals['a_norm'] = jnp.zeros(shapes['a_norm'], F32)
        vals['loss'] = zeros1
        return _pack_small(vals)

    sm = small_adamw("adamw_small", gathered, packed(w), packed(m), packed(v))
    sm = [_unpack_small(t, shapes) for t in sm]
    g_an = lax.dynamic_slice_in_dim(sm[0]['a_norm'], me * a_norm.shape[1], a_norm.shape[1], axis=1)
    pad = lambda t: jnp.zeros((8, 128), F32).at[:n_a].set(t)
    gathered_an = jnp.zeros((N_DEV, 8, 128), F32).at[0].set(pad(g_an))
    an = small_adamw("adamw_a_norm", gathered_an, pad(a_norm), pad(m_a_norm), pad(v_a_norm))
    for i in range(4):
        sm[i]['a_norm'] = an[i][:n_a]
    for n in BIG:
        for i in range(4):
            sm[i][n] = out[n][i]
    loss = sm[0]['loss'][0]
    return (loss, grad_x[None], *[sm[0][n] for n in WEIGHTS], *[sm[1][n] for n in WEIGHTS],
            *[sm[2][n] for n in WEIGHTS], *[sm[3][n] for n in WEIGHTS])
```

```python
import functools
import math

import numpy as np
import jax
import jax.numpy as jnp
from jax import lax
from jax.experimental import pallas as pl
from jax.experimental.pallas import tpu as pltpu

F32 = jnp.float32
BF16 = jnp.bfloat16
MESH = pl.DeviceIdType.MESH

N_DEV = 8
HEAD_DIM = 64
WINDOW = 128
N_BUCKETS = 32
EPS = 1e-5
NEG_INF = -1e30
Q_SCALE = 1.0 / math.sqrt(HEAD_DIM)
LOG2E = 1.4426950408889634

ADAM_LR, ADAM_B1, ADAM_B2, ADAM_EPS, ADAM_WD, ADAM_STEP = 0.001, 0.9, 0.999, 1e-08, 0.01, 10

SB_BQ = 512
SB_BK = 128
SB_DEAD = 160.0
SB_UNSEEN = 1e30
ROW_TILE = 512
VMEM_LIMIT = 56 * 1024 * 1024

WEIGHTS = ['a_norm', 'a_wqkv', 'a_wo', 'kv_norm', 'w_kv', 'b_kv', 'b_norm', 'b_wq', 'b_bq', 'b_sinks', 'b_wo',
           'b_bo', 'rel_bias', 'mlp_norm', 'mlp_up', 'mlp_down', 'final_norm']
BIG = ['a_wqkv', 'a_wo', 'w_kv', 'b_wq', 'b_wo', 'mlp_up', 'mlp_down']
COL_SHARDED = ('a_wqkv', 'mlp_up')
SMALL = ['a_norm', 'kv_norm', 'b_kv', 'b_norm', 'b_bq', 'b_sinks', 'b_bo', 'rel_bias', 'mlp_norm', 'final_norm']


def _params(sem=None):
    return pltpu.CompilerParams(dimension_semantics=sem, vmem_limit_bytes=VMEM_LIMIT)


def _pick(n, cands):
    for c in cands:
        if n % c == 0:
            return c
    raise ValueError(n)


def _tile(n, want):
    return n if n <= want else _pick(n, (want, want // 2, want // 4))


def mm_nn(name, a, w3, layer, epilogue, extras, out_dtypes):
    m, k = a.shape
    _, kw, n = w3.shape
    assert kw == k
    tm = _tile(m, 1024 if k <= 1024 else 512)
    tn = _tile(n, 1024)
    ne, no = len(extras), len(out_dtypes)

    def body(a_ref, w_ref, *rest):
        ex, outs = rest[:ne], rest[ne:ne + no]
        res = epilogue(jnp.dot(a_ref[...], w_ref[...], preferred_element_type=F32), *[e[...] for e in ex])
        for o, r in zip(outs, res):
            o[...] = r.astype(o.dtype)

    tile = pl.BlockSpec((tm, tn), lambda i, j: (i, j))
    ex_specs = [tile if e.shape[0] == m else pl.BlockSpec((1, tn), lambda i, j: (0, j)) for e in extras]
    return pl.pallas_call(
        body, name=name, grid=(m // tm, n // tn),
        in_specs=[pl.BlockSpec((tm, k), lambda i, j: (i, 0)),
                  pl.BlockSpec((None, k, tn), lambda i, j: (layer, 0, j))] + ex_specs,
        out_specs=[tile] * no,
        out_shape=[jax.ShapeDtypeStruct((m, n), d) for d in out_dtypes],
        compiler_params=_params(("parallel", "parallel")),
    )(a, w3, *extras)


def mm_nt(name, dy, w3, layer, epilogue, extras, out_dtypes):
    m, n = dy.shape
    _, k, nw = w3.shape
    assert nw == n
    tm = _tile(m, 1024 if n <= 1024 else 512)
    tko = _tile(k, 1024)
    ne, no = len(extras), len(out_dtypes)

    def body(a_ref, w_ref, *rest):
        ex, outs = rest[:ne], rest[ne:ne + no]
        acc = lax.dot_general(a_ref[...], w_ref[...], (((1,), (1,)), ((), ())), preferred_element_type=F32)
        res = epilogue(acc, *[e[...] for e in ex])
        for o, v in zip(outs, res):
            o[...] = v.astype(o.dtype)

    tile = pl.BlockSpec((tm, tko), lambda i, ko: (i, ko))
    return pl.pallas_call(
        body, name=name, grid=(m // tm, k // tko),
        in_specs=[pl.BlockSpec((tm, n), lambda i, ko: (i, 0)),
                  pl.BlockSpec((None, tko, n), lambda i, ko: (layer, ko, 0))] + [tile] * ne,
        out_specs=[tile] * no,
        out_shape=[jax.ShapeDtypeStruct((m, k), d) for d in out_dtypes],
        compiler_params=_params(("parallel", "parallel")),
    )(dy, w3, *extras)


def mm_tn(name, x, dy, gbuf, shape, layer):
    s, k = x.shape
    _, kw, n = shape
    assert kw == k and dy.shape == (s, n)
    tkk = _tile(k, 512)
    tn = _tile(n, 1024)

    def body(x_ref, dy_ref, *rest):
        g_out = rest[-1]
        g_out[...] = lax.dot_general(x_ref[...], dy_ref[...], (((0,), (0,)), ((), ())),
                                     preferred_element_type=F32).astype(g_out.dtype)

    prev = [] if gbuf is None else [gbuf]
    return pl.pallas_call(
        body, name=name, grid=(k // tkk, n // tn),
        in_specs=[pl.BlockSpec((s, tkk), lambda ki, j: (0, ki)),
                  pl.BlockSpec((s, tn), lambda ki, j: (0, j))] + [pl.BlockSpec(memory_space=pl.ANY)] * len(prev),
        out_specs=pl.BlockSpec((None, tkk, tn), lambda ki, j: (layer, ki, j)),
        out_shape=jax.ShapeDtypeStruct(shape, BF16),
        input_output_aliases={2: 0} if prev else {},
        compiler_params=_params(("parallel", "parallel")),
    )(x, dy, *prev)


def rms_fwd(name, h, g):
    s, d = h.shape
    tr = _pick(s, (ROW_TILE, 256, 128))

    def body(h_ref, g_ref, o_ref):
        x = h_ref[...]
        r = lax.rsqrt(jnp.mean(x * x, axis=-1, keepdims=True) + EPS)
        o_ref[...] = (x * r * g_ref[...]).astype(o_ref.dtype)

    return pl.pallas_call(
        body, name=name, grid=(s // tr,),
        in_specs=[pl.BlockSpec((tr, d), lambda i: (i, 0)), pl.BlockSpec((1, d), lambda i: (0, 0))],
        out_specs=pl.BlockSpec((tr, d), lambda i: (i, 0)),
        out_shape=jax.ShapeDtypeStruct((s, d), BF16),
        compiler_params=_params(("parallel",)),
    )(h, g.reshape(1, d))


def rms_bwd(name, h, g, dn, dres):
    s, d = h.shape
    tr = _pick(s, (ROW_TILE, 256, 128))

    def body(h_ref, g_ref, dn_ref, dres_ref, dx_ref, dxb_ref, dg_ref, cs_ref):
        i = pl.program_id(0)
        x = h_ref[...]
        r = lax.rsqrt(jnp.mean(x * x, axis=-1, keepdims=True) + EPS)
        xh = x * r
        dn_ = dn_ref[...]
        dyg = dn_ * g_ref[...]
        dx = dres_ref[...] + r * (dyg - xh * jnp.mean(dyg * xh, axis=-1, keepdims=True))
        dx_ref[...] = dx
        dxb_ref[...] = dx.astype(BF16)

        @pl.when(i == 0)
        def _():
            dg_ref[...] = jnp.zeros_like(dg_ref)
            cs_ref[...] = jnp.zeros_like(cs_ref)

        dg_ref[...] += jnp.sum(dn_ * xh, axis=0, keepdims=True)
        cs_ref[...] += jnp.sum(dx, axis=0, keepdims=True)

    row = pl.BlockSpec((tr, d), lambda i: (i, 0))
    vec = pl.BlockSpec((1, d), lambda i: (0, 0))
    return pl.pallas_call(
        body, name=name, grid=(s // tr,),
        in_specs=[row, vec, row, row],
        out_specs=[row, row, vec, vec],
        out_shape=[jax.ShapeDtypeStruct((s, d), F32), jax.ShapeDtypeStruct((s, d), BF16),
                   jax.ShapeDtypeStruct((1, d), F32), jax.ShapeDtypeStruct((1, d), F32)],
        compiler_params=_params(("arbitrary",)),
    )(h, g.reshape(1, d), dn, dres)


def loss_head(h, g, target):
    s, d = h.shape
    tr = _pick(s, (ROW_TILE, 256, 128))

    def body(h_ref, g_ref, t_ref, dx_ref, dxb_ref, dg_ref, loss_ref):
        i = pl.program_id(0)
        x = h_ref[...]
        r = lax.rsqrt(jnp.mean(x * x, axis=-1, keepdims=True) + EPS)
        xh = x * r
        gw = g_ref[...]
        err = xh * gw - t_ref[...]
        dn_ = err * (1.0 / d)
        dyg = dn_ * gw
        dx = r * (dyg - xh * jnp.mean(dyg * xh, axis=-1, keepdims=True))
        dx_ref[...] = dx
        dxb_ref[...] = dx.astype(BF16)

        @pl.when(i == 0)
        def _():
            dg_ref[...] = jnp.zeros_like(dg_ref)
            loss_ref[...] = jnp.zeros_like(loss_ref)

        dg_ref[...] += jnp.sum(dn_ * xh, axis=0, keepdims=True)
        per_row = jnp.sum(err * err, axis=-1, keepdims=True) * (0.5 / d)
        loss_ref[...] += jnp.broadcast_to(jnp.sum(per_row, axis=0, keepdims=True), loss_ref.shape)

    row = pl.BlockSpec((tr, d), lambda i: (i, 0))
    vec = pl.BlockSpec((1, d), lambda i: (0, 0))
    return pl.pallas_call(
        body, name="loss_head", grid=(s // tr,),
        in_specs=[row, vec, row],
        out_specs=[row, row, vec, pl.BlockSpec((1, 128), lambda i: (0, 0))],
        out_shape=[jax.ShapeDtypeStruct((s, d), F32), jax.ShapeDtypeStruct((s, d), BF16),
                   jax.ShapeDtypeStruct((1, d), F32), jax.ShapeDtypeStruct((1, 128), F32)],
        compiler_params=_params(("arbitrary",)),
    )(h, g.reshape(1, d), target)


def colsum(name, x):
    s, n = x.shape
    tr = _pick(s, (ROW_TILE, 256, 128))

    def body(x_ref, o_ref):
        @pl.when(pl.program_id(0) == 0)
        def _():
            o_ref[...] = jnp.zeros_like(o_ref)

        o_ref[...] += jnp.sum(x_ref[...].astype(F32), axis=0, keepdims=True)

    return pl.pallas_call(
        body, name=name, grid=(s // tr,),
        in_specs=[pl.BlockSpec((tr, n), lambda i: (i, 0))],
        out_specs=pl.BlockSpec((1, n), lambda i: (0, 0)),
        out_shape=jax.ShapeDtypeStruct((1, n), F32),
        compiler_params=_params(("arbitrary",)),
    )(x)


def _tri_rows(reverse):
    i = np.arange(SB_BK)
    tri = (i[None, :] >= i[:, None]) if reverse else (i[None, :] <= i[:, None])
    tri = np.concatenate([tri, tri], axis=1)
    return jnp.asarray(np.concatenate([tri, np.ones((8, 2 * SB_BK), bool)], axis=0), BF16)


def _hi_lo_rows(x):
    hi = x.astype(BF16)
    lo = (x - hi.astype(F32)).astype(BF16)
    return jnp.concatenate([hi, lo], axis=0)


def _softplus2(zs):
    neg_abs = lax.bitcast_convert_type(lax.bitcast_convert_type(zs, jnp.uint32) | jnp.uint32(0x80000000), F32)
    return jnp.maximum(zs, 0.0) + jnp.log2(1.0 + jnp.exp2(neg_abs))


def _pair_mask(first_rel_block, bq):
    key = lax.broadcasted_iota(jnp.int32, (2 * SB_BK, bq), 0) + first_rel_block * SB_BK
    qry = lax.broadcasted_iota(jnp.int32, (2 * SB_BK, bq), 1)
    return key < qry


def _row_of(table8, sub8, r):
    return jnp.sum(jnp.where(sub8 == r, table8, 0.0), axis=0, keepdims=True)


def _keys(j0):
    return pl.ds(pl.multiple_of(j0 * SB_BK, 2 * SB_BK), 2 * SB_BK)


class Carry:
    def __init__(self, build, ins, out_shapes, sems):
        self.build, self.ins, self.out_shapes, self.sems = build, list(ins), list(out_shapes), list(sems)


def _carried(carry, rest, n_out, n_scratch, first, last):
    n_ci = len(carry.ins) if carry else 0
    n_co = len(carry.out_shapes) if carry else 0
    cin, outs = rest[:n_ci], rest[n_ci:n_ci + n_out]
    cout = rest[n_ci + n_out:n_ci + n_out + n_co]
    scratch = rest[n_ci + n_out + n_co:n_ci + n_out + n_co + n_scratch]
    csems = rest[n_ci + n_out + n_co + n_scratch:]

    def start():
        if carry:
            @pl.when(first)
            def _():
                for cp in carry.build(cin, cout, *csems):
                    cp.start()

    def wait():
        if carry:
            @pl.when(last)
            def _():
                for cp in carry.build(cin, cout, *csems):
                    cp.wait()

    return outs, scratch, start, wait


def _contract0(a, b):
    return lax.dot_general(a, b, (((0,), (0,)), ((), ())), preferred_element_type=F32)


def _contract1(a, b):
    return lax.dot_general(a, b, (((1,), (1,)), ((), ())), preferred_element_type=F32)


def sb_fwd(name, qkvt, exchange=None):
    nh, dh, s = qkvt.shape[0] // 3, qkvt.shape[1], qkvt.shape[2]
    bq = SB_BQ
    per_q = bq // SB_BK
    nkb = s // SB_BK
    assert s % bq == 0 and per_q == 4 and nkb % 8 == 0

    def body(q_ref, k_ref, v_ref, a_ref, *rest):
        i = pl.program_id(1)
        first = (pl.program_id(0) == 0) & (i == 0)
        last = (pl.program_id(0) == nh - 1) & (i == s // bq - 1)
        (o_ref, rtab_ref), (acc, zbuf, wbuf), start_carried, wait_carried = _carried(exchange, rest, 2, 3, first, last)
        start_carried()
        qb = q_ref[...] * Q_SCALE
        tri = a_ref[...]
        sub8 = lax.broadcasted_iota(jnp.int32, (8, bq), 0)
        acc[...] = jnp.zeros_like(acc)
        rtab_ref[...] = jnp.full(rtab_ref.shape, SB_UNSEEN, F32)
        kf = k_ref[...].astype(F32)
        qf = qb.astype(F32)
        k_max2 = jnp.max(jnp.sum(kf * kf, axis=0, keepdims=True), axis=1, keepdims=True)
        bound = jnp.sqrt(jnp.sum(qf * qf, axis=0, keepdims=True) * k_max2) * (1.001 * LOG2E)

        def scores(j0):
            return _contract0(k_ref[:, _keys(j0)], qb) * LOG2E

        def pair(j0, slot, run, rt8, mask, has_prev):
            zs = zbuf[slot]
            zbuf[1 - slot] = scores(jnp.maximum(j0 - 2, 0))
            if has_prev:
                acc[...] += jnp.dot(v_ref[:, _keys(j0 + 2)], wbuf[1 - slot], preferred_element_type=F32)
            p = _softplus2(zs)
            if mask is not None:
                p = jnp.where(mask, p, 0.0)
            cr1 = jnp.dot(tri, _hi_lo_rows(p[SB_BK:]), preferred_element_type=F32)
            cr0 = jnp.dot(tri, _hi_lo_rows(p[:SB_BK]), preferred_element_type=F32)
            run1 = run + cr1[SB_BK:SB_BK + 1]
            w = jnp.exp2(jnp.concatenate([zs[:SB_BK] - cr0[:SB_BK] - run1, zs[SB_BK:] - cr1[:SB_BK] - run], axis=0))
            if mask is not None:
                w = jnp.where(mask, w, 0.0)
            wbuf[slot] = w.astype(BF16)
            rt8 = jnp.where(j0 % 8 == 6, SB_UNSEEN, rt8)
            rt8 = jnp.where(sub8 == (j0 + 1) % 8, run, jnp.where(sub8 == j0 % 8, run1, rt8))
            rtab_ref[pl.ds(pl.multiple_of((j0 // 8) * 8, 8), 8), :] = rt8
            return run1 + cr0[SB_BK:SB_BK + 1], rt8

        def alive(run):
            return jnp.min(run - bound) < SB_DEAD

        top = i * per_q
        zbuf[0] = scores(top + 2)
        state = (jnp.zeros((1, bq), F32), jnp.full((8, bq), SB_UNSEEN, F32))
        state = pair(top + 2, 0, *state, _pair_mask(2, bq), False)
        state = pair(top, 1, *state, _pair_mask(0, bq), True)

        def step(c):
            it, pairs, _, run, rt8 = c
            j0 = top - 2 - 4 * it
            run, rt8 = pair(j0, 0, run, rt8, None, True)
            go = alive(run)
            run, rt8 = lax.cond(go, lambda r, t: pair(j0 - 2, 1, r, t, None, True), lambda r, t: (r, t), run, rt8)
            return it + 1, pairs + 1 + go.astype(jnp.int32), go & alive(run), run, rt8

        pairs = lax.while_loop(lambda c: (c[0] < i) & c[2], step, (0, 0, alive(state[0]), *state))[1]
        acc[...] += jnp.dot(v_ref[:, _keys(top - 2 * pairs)], wbuf[(pairs + 1) % 2], preferred_element_type=F32)
        o_ref[...] = acc[...].astype(o_ref.dtype)
        wait_carried()

    qspec = pl.BlockSpec((None, dh, bq), lambda h, i: (h, 0, i))
    hbm = pl.BlockSpec(memory_space=pl.ANY)
    c_ins, c_outs, c_sems = (exchange.ins, exchange.out_shapes, exchange.sems) if exchange else ([], [], [])
    outs = pl.pallas_call(
        body, name=name, grid=(nh, s // bq),
        in_specs=[qspec, pl.BlockSpec((None, dh, s), lambda h, i: (h + nh, 0, 0)),
                  pl.BlockSpec((None, dh, s), lambda h, i: (h + 2 * nh, 0, 0)),
                  pl.BlockSpec((SB_BK + 8, 2 * SB_BK), lambda h, i: (0, 0))] + [hbm] * len(c_ins),
        out_specs=[qspec, pl.BlockSpec((None, nkb, bq), lambda h, i: (h, 0, i))] + [hbm] * len(c_outs),
        out_shape=[jax.ShapeDtypeStruct((nh, dh, s), BF16), jax.ShapeDtypeStruct((nh, nkb, s), F32)] + c_outs,
        scratch_shapes=[pltpu.VMEM((dh, bq), F32), pltpu.VMEM((2, 2 * SB_BK, bq), F32),
                        pltpu.VMEM((2, 2 * SB_BK, bq), BF16)] + c_sems,
        compiler_params=_params(("arbitrary", "arbitrary")),
    )(qkvt, qkvt, qkvt, _tri_rows(True), *c_ins)
    return outs[0], outs[1], outs[2:]


def sb_bwd(name, qkvt, dot_, rtab, exchange=None):
    nh, dh, s = qkvt.shape[0] // 3, qkvt.shape[1], qkvt.shape[2]
    bq = SB_BQ
    per_q = bq // SB_BK
    nkb = s // SB_BK
    nq = s // bq

    def body(qt_ref, kt_ref, vt_ref, dot_ref, rtab_ref, ar_ref, af_ref, *rest):
        i = pl.program_id(1)
        first = (pl.program_id(0) == 0) & (i == 0)
        last = (pl.program_id(0) == nh - 1) & (i == nq - 1)
        (dq_ref, dk_ref, dv_ref), (dq_acc, dk_acc, dv_acc, zbuf, dwbuf, dzbuf, wbuf), start_carried, wait_carried = \
            _carried(exchange, rest, 3, 7, first, last)
        start_carried()

        @pl.when(i == 0)
        def _():
            dk_acc[...] = jnp.zeros_like(dk_acc)
            dv_acc[...] = jnp.zeros_like(dv_acc)

        qtb = qt_ref[...] * Q_SCALE
        dotb = dot_ref[...]
        tri_rev = ar_ref[...][:SB_BK]
        tri_fwd = af_ref[...]
        sub8 = lax.broadcasted_iota(jnp.int32, (8, bq), 0)
        dq_acc[...] = jnp.zeros_like(dq_acc)
        last_j = i * per_q + 2
        seen = jnp.max(jnp.where(rtab_ref[...] < 0.1 * SB_UNSEEN, 1.0, 0.0), axis=1, keepdims=True)
        pairs = jnp.clip((jnp.sum(seen).astype(jnp.int32) - per_q) // 2, 0, 2 * i)
        odd = pairs % 2
        first_j = i * per_q - 2 * pairs

        def issue(j0, slot):
            zbuf[slot] = _contract0(kt_ref[:, _keys(j0)], qtb) * LOG2E
            dwbuf[slot] = _contract0(vt_ref[:, _keys(j0)], dotb)

        def retire(j0, slot):
            keys = _keys(j0)
            dq_acc[...] += jnp.dot(kt_ref[:, keys], dzbuf[slot], preferred_element_type=F32)
            dk_acc[:, keys] += _contract1(qtb, dzbuf[slot])
            dv_acc[:, keys] += _contract1(dotb, wbuf[slot])

        def pair(j0, slot, g_run, mask):
            zs = zbuf[slot]
            dw = dwbuf[slot]
            issue(jnp.minimum(j0 + 2, last_j), 1 - slot)
            retire(jnp.maximum(j0 - 2, first_j), 1 - slot)
            p_raw = _softplus2(zs)
            p = p_raw if mask is None else jnp.where(mask, p_raw, 0.0)
            c0 = jnp.dot(tri_rev, _hi_lo_rows(p[:SB_BK]), preferred_element_type=F32)
            c1 = jnp.dot(tri_rev, _hi_lo_rows(p[SB_BK:]), preferred_element_type=F32)
            rt8 = rtab_ref[pl.ds(pl.multiple_of((j0 // 8) * 8, 8), 8), :]
            r0 = _row_of(rt8, sub8, j0 % 8)
            r1 = _row_of(rt8, sub8, (j0 + 1) % 8)
            w = jnp.exp2(jnp.concatenate([zs[:SB_BK] - c0 - r0, zs[SB_BK:] - c1 - r1], axis=0))
            if mask is not None:
                w = jnp.where(mask, w, 0.0)
            g = w * dw
            gg0 = jnp.dot(tri_fwd, _hi_lo_rows(g[:SB_BK]), preferred_element_type=F32)
            gg1 = jnp.dot(tri_fwd, _hi_lo_rows(g[SB_BK:]), preferred_element_type=F32)
            g_run1 = g_run + gg0[SB_BK:SB_BK + 1]
            g_pre = jnp.concatenate([gg0[:SB_BK] + g_run, gg1[:SB_BK] + g_run1], axis=0)
            dz = g - jnp.exp2(zs - p_raw) * g_pre
            if mask is not None:
                dz = jnp.where(mask, dz, 0.0)
            dzbuf[slot] = dz.astype(BF16)
            wbuf[slot] = w.astype(BF16)
            return g_run1 + gg1[SB_BK:SB_BK + 1]

        issue(first_j, odd)
        dzbuf[...] = jnp.zeros(dzbuf.shape, BF16)
        wbuf[...] = jnp.zeros(wbuf.shape, BF16)

        def step(it, g_run):
            g_run = pair(4 * it, 0, g_run, None)
            return pair(4 * it + 2, 1, g_run, None)

        g_run = lax.cond(odd == 1, lambda g: pair(first_j, 1, g, None), lambda g: g, jnp.zeros((1, bq), F32))
        g_run = lax.fori_loop(i - pairs // 2, i, step, g_run)
        g_run = pair(last_j - 2, 0, g_run, _pair_mask(0, bq))
        pair(last_j, 1, g_run, _pair_mask(2, bq))
        retire(last_j, 1)
        dq_ref[...] = (dq_acc[...] * Q_SCALE).astype(dq_ref.dtype)

        @pl.when(i == nq - 1)
        def _():
            dk_ref[...] = dk_acc[...].astype(dk_ref.dtype)
            dv_ref[...] = dv_acc[...].astype(dv_ref.dtype)

        wait_carried()

    tspec = pl.BlockSpec((None, dh, bq), lambda h, i: (h, 0, i))
    fullspec = pl.BlockSpec((None, dh, s), lambda h, i: (h, 0, 0))
    aspec = pl.BlockSpec((SB_BK + 8, 2 * SB_BK), lambda h, i: (0, 0))
    pair_f32 = pltpu.VMEM((2, 2 * SB_BK, bq), F32)
    pair_bf16 = pltpu.VMEM((2, 2 * SB_BK, bq), BF16)
    hbm = pl.BlockSpec(memory_space=pl.ANY)
    c_ins, c_outs, c_sems = (exchange.ins, exchange.out_shapes, exchange.sems) if exchange else ([], [], [])
    outs = pl.pallas_call(
        body, name=name, grid=(nh, s // bq),
        in_specs=[tspec, pl.BlockSpec((None, dh, s), lambda h, i: (h + nh, 0, 0)),
                  pl.BlockSpec((None, dh, s), lambda h, i: (h + 2 * nh, 0, 0)), tspec,
                  pl.BlockSpec((None, nkb, bq), lambda h, i: (h, 0, i)), aspec, aspec] + [hbm] * len(c_ins),
        out_specs=[tspec, fullspec, fullspec] + [hbm] * len(c_outs),
        out_shape=[jax.ShapeDtypeStruct((nh, dh, s), BF16)] * 3 + c_outs,
        scratch_shapes=[pltpu.VMEM((dh, bq), F32), pltpu.VMEM((dh, s), F32), pltpu.VMEM((dh, s), F32),
                        pair_f32, pair_f32, pair_bf16, pair_bf16] + c_sems,
        compiler_params=_params(("arbitrary", "arbitrary")),
    )(qkvt, qkvt, qkvt, dot_, rtab, _tri_rows(True), _tri_rows(False), *c_ins)
    return outs[0], outs[1], outs[2], outs[3:]


SWA_QB = 2


def _swa_probs(qt, kt, bias_t, sink, i):
    cols = qt.shape[1]
    sc = _contract0(kt, qt) + bias_t
    kj = lax.broadcasted_iota(jnp.int32, (2 * WINDOW, cols), 0)
    qi = lax.broadcasted_iota(jnp.int32, (2 * WINDOW, cols), 1) & (WINDOW - 1)
    dist = qi + WINDOW - kj
    valid = (dist >= 0) & (dist < WINDOW) & ((kj >= WINDOW) | (i > 0))
    sc = jnp.where(valid, sc, NEG_INF)
    mx = jnp.maximum(jnp.max(sc, axis=0, keepdims=True), sink)
    p = jnp.exp(sc - mx)
    p_sink = jnp.exp(sink - mx)
    inv = 1.0 / (jnp.sum(p, axis=0, keepdims=True) + p_sink)
    return p, p_sink, inv


def _band(i):
    return pl.ds(pl.multiple_of(i * WINDOW, WINDOW), 2 * WINDOW)


def _heads_to_lanes(blk):
    return jnp.concatenate([blk[r * HEAD_DIM:(r + 1) * HEAD_DIM] for r in range(8)], axis=1)


def _lanes_to_heads(t):
    return jnp.concatenate([t[:, r * WINDOW:(r + 1) * WINDOW] for r in range(8)], axis=0)


def swa_fwd(name, qt, kpt, vpt, bias_t, sink_row):
    d, s = qt.shape
    ng, dh, sp = kpt.shape
    rows, cols = d // ng, SWA_QB * WINDOW
    assert (s // WINDOW) % SWA_QB == 0

    def body(q_ref, k_ref, v_ref, bias_ref, sink_ref, o_ref):
        for u in range(SWA_QB):
            i = pl.program_id(1) * SWA_QB + u
            lanes = slice(u * WINDOW, (u + 1) * WINDOW)
            qb = _heads_to_lanes(q_ref[:, lanes]) * Q_SCALE
            p, _, inv = _swa_probs(qb, k_ref[:, _band(i)], bias_ref[...], sink_ref[...], i)
            o_t = jnp.dot(v_ref[:, _band(i)], p.astype(BF16), preferred_element_type=F32) * inv
            o_ref[:, lanes] = _lanes_to_heads(o_t).astype(o_ref.dtype)

    qspec = pl.BlockSpec((rows, cols), lambda g, i: (g, i))
    kspec = pl.BlockSpec((None, dh, sp), lambda g, i: (g, 0, 0))
    return pl.pallas_call(
        body, name=name, grid=(ng, s // cols),
        in_specs=[qspec, kspec, kspec, pl.BlockSpec((None, 2 * WINDOW, 8 * WINDOW), lambda g, i: (g, 0, 0)),
                  pl.BlockSpec((None, 1, 8 * WINDOW), lambda g, i: (g, 0, 0))],
        out_specs=qspec,
        out_shape=jax.ShapeDtypeStruct(qt.shape, BF16),
        compiler_params=_params(("parallel", "arbitrary")),
    )(qt, kpt, vpt, bias_t, sink_row)


def swa_bwd(name, qt, kpt, vpt, bias_t, sink_row, dot_, dk_in, dv_in):
    d, s = qt.shape
    ng, dh, sp = kpt.shape
    rows, cols = d // ng, SWA_QB * WINDOW

    def body(q_ref, k_ref, v_ref, bias_ref, sink_ref, do_ref, dki_ref, dvi_ref, dq_ref, dk_ref, dv_ref, db_ref, ds_ref):
        @pl.when(pl.program_id(1) == 0)
        def _():
            dk_ref[...] = dki_ref[...]
            dv_ref[...] = dvi_ref[...]
            db_ref[...] = jnp.zeros_like(db_ref)
            ds_ref[...] = jnp.zeros_like(ds_ref)

        for u in range(SWA_QB):
            i = pl.program_id(1) * SWA_QB + u
            band = _band(i)
            lanes = slice(u * WINDOW, (u + 1) * WINDOW)
            qb = _heads_to_lanes(q_ref[:, lanes]) * Q_SCALE
            dob = _heads_to_lanes(do_ref[:, lanes])
            kt = k_ref[:, band]
            p, p_sink, inv = _swa_probs(qb, kt, bias_ref[...], sink_ref[...], i)
            p = p * inv
            dp = _contract0(v_ref[:, band], dob)
            delta = jnp.sum(p * dp, axis=0, keepdims=True)
            dsc = p * (dp - delta)
            ds_ref[...] -= p_sink * inv * delta
            db_ref[...] += dsc
            dscb = dsc.astype(BF16)
            dq_t = jnp.dot(kt, dscb, preferred_element_type=F32) * Q_SCALE
            dq_ref[:, lanes] = _lanes_to_heads(dq_t).astype(dq_ref.dtype)
            dk_ref[:, band] += _contract1(qb, dscb)
            dv_ref[:, band] += _contract1(dob, p.astype(BF16))

    qspec = pl.BlockSpec((rows, cols), lambda g, i: (g, i))
    kspec = pl.BlockSpec((None, dh, sp), lambda g, i: (g, 0, 0))
    bspec = pl.BlockSpec((None, 2 * WINDOW, 8 * WINDOW), lambda g, i: (g, 0, 0))
    sspec = pl.BlockSpec((None, 1, 8 * WINDOW), lambda g, i: (g, 0, 0))
    return pl.pallas_call(
        body, name=name, grid=(ng, s // cols),
        in_specs=[qspec, kspec, kspec, bspec, sspec, qspec, kspec, kspec],
        out_specs=[qspec, kspec, kspec, bspec, sspec],
        out_shape=[jax.ShapeDtypeStruct(qt.shape, BF16), jax.ShapeDtypeStruct(kpt.shape, F32),
                   jax.ShapeDtypeStruct(kpt.shape, F32), jax.ShapeDtypeStruct(bias_t.shape, F32),
                   jax.ShapeDtypeStruct(sink_row.shape, F32)],
        compiler_params=_params(("parallel", "arbitrary")),
    )(qt, kpt, vpt, bias_t, sink_row, dot_, dk_in, dv_in)


def _bucket_onehot():
    qi = np.arange(WINDOW)[:, None]
    kj = np.arange(2 * WINDOW)[None, :]
    n = np.maximum(qi + WINDOW - kj, 0)
    max_exact = N_BUCKETS // 2
    nf = np.maximum(n, 1).astype(np.float64)
    val = np.log(nf / max_exact) / math.log(WINDOW / max_exact) * (N_BUCKETS - max_exact)
    assert np.all(np.abs(val - np.round(val))[(n > max_exact) & (n < WINDOW)] > 1e-3)
    large = np.minimum(max_exact + val.astype(np.int64), N_BUCKETS - 1)
    bucket = np.where(n < max_exact, n, large).reshape(-1)
    onehot = np.zeros((128, bucket.size), np.float32)
    onehot[bucket, np.arange(bucket.size)] = 1.0
    return onehot


def _split3(x):
    a = x.astype(BF16)
    r = x - a.astype(F32)
    b = r.astype(BF16)
    c = (r - b.astype(F32)).astype(BF16)
    return a, b, c


def bias_table(rel_bias):
    nh = rel_bias.shape[1]
    oh = jnp.asarray(_bucket_onehot(), BF16)
    n = oh.shape[1]
    tn = 4096
    rb = jnp.zeros((nh, 128), F32).at[:, :N_BUCKETS].set(rel_bias.T)

    def body(rb_ref, oh_ref, o_ref):
        o_ref[...] = sum(jnp.dot(t, oh_ref[...], preferred_element_type=F32) for t in _split3(rb_ref[...]))

    return pl.pallas_call(
        body, name="bias_table", grid=(n // tn,),
        in_specs=[pl.BlockSpec((nh, 128), lambda i: (0, 0)), pl.BlockSpec((128, tn), lambda i: (0, i))],
        out_specs=pl.BlockSpec((nh, tn), lambda i: (0, i)),
        out_shape=jax.ShapeDtypeStruct((nh, n), F32),
        compiler_params=_params(("parallel",)),
    )(rb, oh)


def bias_table_grad(db0, db1):
    nh, n = db0.shape
    oh = jnp.asarray(_bucket_onehot(), BF16)
    tn = 4096

    def body(a_ref, b_ref, oh_ref, o_ref):
        @pl.when(pl.program_id(0) == 0)
        def _():
            o_ref[...] = jnp.zeros_like(o_ref)

        o_ref[...] += sum(lax.dot_general(t, oh_ref[...], (((1,), (1,)), ((), ())), preferred_element_type=F32)
                          for t in _split3(a_ref[...] + b_ref[...]))

    blk = pl.BlockSpec((nh, tn), lambda i: (0, i))
    return pl.pallas_call(
        body, name="bias_table_grad", grid=(n // tn,),
        in_specs=[blk, blk, pl.BlockSpec((128, tn), lambda i: (0, i))],
        out_specs=pl.BlockSpec((nh, 128), lambda i: (0, 0)),
        out_shape=jax.ShapeDtypeStruct((nh, 128), F32),
        compiler_params=_params(("arbitrary",)),
    )(db0, db1, oh)


def _owner_view(ref, name, d):
    if name == 'a_norm':
        return ref.at[d]
    if name in COL_SHARDED:
        n = ref.shape[2] // N_DEV
        return ref.at[:, :, pl.ds(pl.multiple_of(d * n, 128), n)]
    return ref.at[:, d]


def _place():
    return lax.axis_index("x"), lax.axis_index("y"), lax.axis_index("c")


def _dev(p):
    return 4 * p[0] + 2 * p[1] + p[2]


def _remote(src, dst, send_sem, recv_sem, to):
    return pltpu.make_async_remote_copy(src_ref=src, dst_ref=dst, send_sem=send_sem, recv_sem=recv_sem,
                                        device_id=to, device_id_type=MESH)


def _dma_sems(*shapes):
    return [pltpu.SemaphoreType.DMA(sh) for sh in shapes]


def comm_call(name, build, ins, out_shapes, sems, aliases=None):
    n_in, n_out = len(ins), len(out_shapes)

    def body(*refs):
        copies = build(refs[:n_in], refs[n_in:n_in + n_out], *refs[n_in + n_out:])
        for cp in copies:
            cp.start()
        for cp in copies:
            cp.wait()

    hbm = pl.BlockSpec(memory_space=pl.ANY)
    return pl.pallas_call(
        body, name=name, in_specs=[hbm] * n_in, out_specs=[hbm] * n_out, out_shape=list(out_shapes),
        scratch_shapes=sems, input_output_aliases=aliases or {},
    )(*ins)


def all_gather_weights(names, shards, full_shapes):
    n = len(names)

    def body(*refs):
        ins, outs = refs[:n], refs[n:2 * n]
        send_sems, recv_sems, local_sems = refs[2 * n:]
        x, y, c = _place()
        me, sibling = (x, y, c), (x, y, 1 - c)
        chips = [(1 - x, y), (x, 1 - y), (1 - x, 1 - y)]

        def copy(t, k, block, to, src=None):
            dst = _owner_view(outs[t], names[t], _dev(block))
            return _remote(dst if src is None else src, dst, send_sems.at[t, k], recv_sems.at[t, k], to)

        mine = [pltpu.make_async_copy(ins[t], _owner_view(outs[t], names[t], _dev(me)), local_sems.at[t])
                for t in range(n)]
        for cp in mine:
            cp.start()
        first = []
        for t in range(n):
            first.append(copy(t, 0, me, sibling, src=ins[t]))
            first += [copy(t, 1 + j, me, (*chip, c), src=ins[t]) for j, chip in enumerate(chips)]
        for cp in first:
            cp.start()
        passed = []
        for j, chip in enumerate(chips):
            for t in range(n):
                copy(t, 1 + j, (*chip, c), me).wait_recv()
                fwd = copy(t, 4 + j, (*chip, c), sibling)
                fwd.start()
                passed.append(fwd)
        for t in range(n):
            copy(t, 0, sibling, me).wait_recv()
            for j, chip in enumerate(chips):
                copy(t, 4 + j, (*chip, 1 - c), me).wait_recv()
        for cp in first + passed:
            cp.wait_send()
        for cp in mine:
            cp.wait()

    hbm = pl.BlockSpec(memory_space=pl.ANY)
    return pl.pallas_call(
        body, name="all_gather_layer0",
        in_specs=[hbm] * n, out_specs=[hbm] * n,
        out_shape=[jax.ShapeDtypeStruct(full_shapes[t], shards[t].dtype) for t in range(n)],
        scratch_shapes=_dma_sems((n, 7), (n, 7), (n,)),
    )(*shards)


def ag_direct(names, shards, full_shapes):
    n = len(names)

    def build(ins, outs, send_sems, recv_sems, local_sems):
        x, y, c = _place()
        peers = [(x, y, 1 - c), (1 - x, y, c), (x, 1 - y, c), (1 - x, 1 - y, c)]
        copies = []
        for t in range(n):
            dst = _owner_view(outs[t], names[t], _dev((x, y, c)))
            copies.append(pltpu.make_async_copy(ins[t], dst, local_sems.at[t]))
            copies += [_remote(ins[t], dst, send_sems.at[t, k], recv_sems.at[t, k], to) for k, to in enumerate(peers)]
        return copies

    return Carry(build, shards, [jax.ShapeDtypeStruct(full_shapes[t], shards[t].dtype) for t in range(n)],
                 _dma_sems((n, 4), (n, 4), (n,)))


def ag_forward(tag, names, partial):
    n = len(names)

    def build(ins, outs, send_sems, recv_sems):
        del ins
        x, y, c = _place()
        copies = []
        for t in range(n):
            for k, chip in enumerate([(1 - x, y), (x, 1 - y), (1 - x, 1 - y)]):
                view = _owner_view(outs[t], names[t], _dev((*chip, c)))
                copies.append(_remote(view, view, send_sems.at[t, k], recv_sems.at[t, k], (x, y, 1 - c)))
        return copies

    return comm_call(f"all_gather_forward_{tag}", build, partial, [jax.ShapeDtypeStruct(p.shape, p.dtype) for p in partial],
                     _dma_sems((n, 3), (n, 3)), aliases={t: t for t in range(n)})


def sibling_exchange(tag, names, grads, part_shapes):
    n = len(names)

    def build(ins, outs, send_sems, recv_sems):
        x, y, c = _place()
        return [_remote(_owner_view(ins[t], names[t], 2 * q + 1 - c), outs[t].at[q], send_sems.at[t, q],
                        recv_sems.at[t, q], (x, y, 1 - c)) for t in range(n) for q in range(4)]

    return comm_call(f"rs_sibling_exchange_{tag}", build, grads,
                     [jax.ShapeDtypeStruct((4,) + part_shapes[t], BF16) for t in range(n)], _dma_sems((n, 4), (n, 4)))


def chip_exchange(names, parts, part_shapes):
    n = len(names)

    def build(ins, outs, send_sems, recv_sems):
        x, y, c = _place()
        chips = [(1 - x, y), (x, 1 - y), (1 - x, 1 - y)]
        return [_remote(ins[t].at[2 * chip[0] + chip[1]], outs[t].at[k], send_sems.at[t, k], recv_sems.at[t, k],
                        (*chip, c)) for t in range(n) for k, chip in enumerate(chips)]

    return Carry(build, parts, [jax.ShapeDtypeStruct((3,) + part_shapes[t], BF16) for t in range(n)],
                 _dma_sems((n, 3), (n, 3)))


def all_gather_rows(x):
    r, w = x.shape

    def body(x_ref, out_ref, send_sems, recv_sems, local_sem):
        px, py, pc = _place()
        me = 4 * px + 2 * py + pc
        mine = pltpu.make_async_copy(x_ref, out_ref.at[me], local_sem)
        mine.start()
        copies = []
        for k in range(1, N_DEV):
            peer = (px ^ (k >> 2), py ^ ((k >> 1) & 1), pc ^ (k & 1))
            copies.append(pltpu.make_async_remote_copy(
                src_ref=x_ref, dst_ref=out_ref.at[me], send_sem=send_sems.at[k - 1], recv_sem=recv_sems.at[k - 1],
                device_id=peer, device_id_type=MESH))
        for cp in copies:
            cp.start()
        for k in range(1, N_DEV):
            peer_idx = me ^ k
            pltpu.make_async_remote_copy(
                src_ref=x_ref, dst_ref=out_ref.at[peer_idx], send_sem=send_sems.at[k - 1],
                recv_sem=recv_sems.at[k - 1], device_id=(px, py, pc), device_id_type=MESH).wait_recv()
        for cp in copies:
            cp.wait_send()
        mine.wait()

    vmem = pl.BlockSpec(memory_space=pltpu.VMEM)
    return pl.pallas_call(
        body, name="all_gather_small_grads",
        in_specs=[vmem], out_specs=vmem,
        out_shape=jax.ShapeDtypeStruct((N_DEV, r, w), x.dtype),
        scratch_shapes=[pltpu.SemaphoreType.DMA((N_DEV - 1,)), pltpu.SemaphoreType.DMA((N_DEV - 1,)),
                        pltpu.SemaphoreType.DMA],
    )(x)


def _adamw(w, g, m, v):
    m = ADAM_B1 * m + (1.0 - ADAM_B1) * g
    v = ADAM_B2 * v + (1.0 - ADAM_B2) * (g * g)
    m_hat = m / (1.0 - ADAM_B1 ** ADAM_STEP)
    v_hat = v / (1.0 - ADAM_B2 ** ADAM_STEP)
    return -ADAM_LR * (m_hat / (jnp.sqrt(v_hat) + ADAM_EPS) + ADAM_WD * w), m, v


def sibling_sum(name, col, grads, recv, core):
    _, nl, rows, cols = recv.shape
    tr = _tile(rows, 512)
    rspec = pl.BlockSpec((None, None, tr, cols), lambda q, l, i, c_ref: (q, l, i, 0))
    if col:
        gspec = pl.BlockSpec((None, tr, cols), lambda q, l, i, c_ref: (l, i, 2 * q + c_ref[0]))
    else:
        gspec = pl.BlockSpec((None, None, tr, cols), lambda q, l, i, c_ref: (l, 2 * q + c_ref[0], i, 0))

    def body(c_ref, g_ref, r_ref, o_ref):
        del c_ref
        o_ref[...] = (g_ref[...].astype(F32) + r_ref[...].astype(F32)).astype(BF16)

    return pl.pallas_call(
        body, name=name,
        grid_spec=pltpu.PrefetchScalarGridSpec(num_scalar_prefetch=1, grid=(4, nl, rows // tr),
                                               in_specs=[gspec, rspec], out_specs=rspec),
        out_shape=jax.ShapeDtypeStruct(recv.shape, BF16),
        compiler_params=_params(("parallel", "parallel", "parallel")),
    )(core.reshape(1), grads, recv)


def reduce_adamw(name, parts, recv, chip, w, m, v, l0, prev):
    _, nl, rows, cols = parts.shape
    tr = _tile(rows, 256)

    def body(q_ref, p_ref, r_ref, w_ref, m_ref, v_ref, *rest):
        del q_ref
        g_out, d_out, m_out, v_out = rest[-4:]
        g = ((p_ref[...].astype(F32) + r_ref[0].astype(F32)) + r_ref[1].astype(F32)) + r_ref[2].astype(F32)
        d, mn, vn = _adamw(w_ref[...], g, m_ref[...], v_ref[...])
        g_out[...] = g
        d_out[...] = d
        m_out[...] = mn
        v_out[...] = vn

    blk = pl.BlockSpec((None, tr, cols), lambda l, i, q_ref: (l0 + l, i, 0))
    prev = list(prev) if prev else []
    return pl.pallas_call(
        body, name=name,
        grid_spec=pltpu.PrefetchScalarGridSpec(
            num_scalar_prefetch=1, grid=(nl, rows // tr),
            in_specs=[pl.BlockSpec((None, None, tr, cols), lambda l, i, q_ref: (q_ref[0], l, i, 0)),
                      pl.BlockSpec((3, None, tr, cols), lambda l, i, q_ref: (0, l, i, 0)), blk, blk, blk]
            + [pl.BlockSpec(memory_space=pl.ANY)] * len(prev),
            out_specs=[blk] * 4),
        out_shape=[jax.ShapeDtypeStruct(w.shape, F32)] * 4,
        input_output_aliases={6 + i: i for i in range(len(prev))},
        compiler_params=_params(("parallel", "parallel")),
    )(chip.reshape(1), parts, recv, w, m, v, *prev)


def small_adamw(name, gathered, w, m, v):
    _, r, c = gathered.shape

    def body(ga_ref, w_ref, m_ref, v_ref, g_out, d_out, m_out, v_out):
        g = ga_ref[0]
        for d in range(1, N_DEV):
            g = g + ga_ref[d]
        dl, mn, vn = _adamw(w_ref[...], g, m_ref[...], v_ref[...])
        g_out[...] = g
        d_out[...] = dl
        m_out[...] = mn
        v_out[...] = vn

    return pl.pallas_call(
        body, name=name,
        out_shape=[jax.ShapeDtypeStruct((r, c), F32)] * 4,
        compiler_params=_params(),
    )(gathered, w, m, v)


def _add(acc, *ex):
    return (acc + ex[0],)


def local_step(x, target, small, ex):
    s, d = x.shape
    n_a, n_b = small['a_norm'].shape[0], small['b_norm'].shape[0]
    sg = {}
    gb = {}

    def fwd_mm(name, a, wname, layer, epilogue, extras, out_dtypes):
        return mm_nn(name, a, *ex.weight(wname, layer), epilogue, extras, out_dtypes)

    def dx_mm(name, dy, wname, layer, epilogue, extras, out_dtypes):
        return mm_nt(name, dy, *ex.weight(wname, layer), epilogue, extras, out_dtypes)

    def dw_mm(name, a, dy, wname, layer):
        key, slab, shape = ex.grad(wname, layer)
        gb[key] = mm_tn(name, a, dy, gb.get(key), shape, slab)

    bias_flat = bias_table(small['rel_bias'])
    bias_t = bias_flat.reshape(2, 8, WINDOW, 2 * WINDOW).transpose(0, 3, 1, 2).reshape(2, 2 * WINDOW, 8 * WINDOW)
    sink_rows = [jnp.repeat(small['b_sinks'][j], WINDOW).reshape(2, 1, 8 * WINDOW) for j in range(n_b)]

    def mlp_fwd(h, layer):
        n2 = rms_fwd(f"mlp_norm_fwd{layer}", h, small['mlp_norm'][layer])
        u, a = fwd_mm(f"mlp_up_fwd{layer}", n2, 'mlp_up', layer,
                      lambda acc: (acc, jnp.square(jnp.maximum(acc, 0.0))), (), (BF16, BF16))
        (h2,) = fwd_mm(f"mlp_down_fwd{layer}", a, 'mlp_down', layer, _add, (h,), (F32,))
        return h2, (n2, u, a)

    h = x
    saved = []
    for l in range(n_a):
        n1 = rms_fwd(f"a_norm_fwd{l}", h, small['a_norm'][l])
        (qkv,) = fwd_mm(f"a_qkv_fwd{l}", n1, 'a_wqkv', l, lambda acc: (acc,), (), (BF16,))
        qkvt = qkv.T.reshape(3 * d // HEAD_DIM, HEAD_DIM, s)
        o_t, rtab, carried = sb_fwd(f"sb_fwd{l}", qkvt, ex.fwd_carry(l))
        ex.fwd_done(l, carried)
        o = o_t.reshape(d, s).T
        (h_mid,) = fwd_mm(f"a_wo_fwd{l}", o, 'a_wo', l, _add, (h,), (F32,))
        h_out, mlp_saved = mlp_fwd(h_mid, l)
        saved.append((h, n1, qkvt, o, rtab, h_mid, mlp_saved))
        h = h_out
    h_kv = h
    nkv = rms_fwd("kv_norm_fwd", h, small['kv_norm'])
    (kv,) = fwd_mm("kv_fwd", nkv, 'w_kv', 0, lambda acc, b: (acc + b,), (small['b_kv'].reshape(1, -1),), (BF16,))
    kvt = kv.T.reshape(2, 2, HEAD_DIM, s)
    kpt, vpt = (jnp.pad(t, ((0, 0), (0, 0), (WINDOW, 0))) for t in (kvt[0], kvt[1]))
    for j in range(n_b):
        layer = n_a + j
        n1 = rms_fwd(f"b_norm_fwd{j}", h, small['b_norm'][j])
        (qb,) = fwd_mm(f"b_q_fwd{j}", n1, 'b_wq', j, lambda acc, b: (acc + b,),
                       (small['b_bq'][j].reshape(1, -1),), (BF16,))
        qbt = qb.T
        o = swa_fwd(f"swa_fwd{j}", qbt, kpt, vpt, bias_t, sink_rows[j]).T
        (h_mid,) = fwd_mm(f"b_wo_fwd{j}", o, 'b_wo', j, lambda acc, hh, b: (acc + hh + b,),
                          (h, small['b_bo'][j].reshape(1, -1)), (F32,))
        h_out, mlp_saved = mlp_fwd(h_mid, layer)
        saved.append((h, n1, qbt, o, h_mid, mlp_saved))
        h = h_out

    dh, dhb, dg_final, loss_b = loss_head(h, small['final_norm'], target)
    sg['final_norm'] = dg_final[0]
    sg['mlp_norm'] = [None] * (n_a + n_b)

    def mlp_bwd(dh, dhb, h_mid, mlp_saved, layer):
        n2, u, a = mlp_saved
        (du,) = dx_mm(f"mlp_down_dx{layer}", dhb, 'mlp_down', layer,
                      lambda acc, uu: (acc * (2.0 * jnp.maximum(uu.astype(F32), 0.0)),), (u,), (BF16,))
        dw_mm(f"mlp_down_dw{layer}", a, dhb, 'mlp_down', layer)
        (dn2,) = dx_mm(f"mlp_up_dx{layer}", du, 'mlp_up', layer, lambda acc: (acc,), (), (F32,))
        dw_mm(f"mlp_up_dw{layer}", n2, du, 'mlp_up', layer)
        dh2, dh2b, dg, cs = rms_bwd(f"mlp_norm_bwd{layer}", h_mid, small['mlp_norm'][layer], dn2, dh)
        sg['mlp_norm'][layer] = dg[0]
        return dh2, dh2b, cs

    dkp = jnp.zeros(kpt.shape, F32)
    dvp = jnp.zeros(vpt.shape, F32)
    sg['b_norm'], sg['b_bq'], sg['b_bo'], sg['b_sinks'] = [None] * n_b, [None] * n_b, [None] * n_b, [None] * n_b
    dbias = [None] * n_b
    for j in reversed(range(n_b)):
        layer = n_a + j
        h_in, n1, qbt, o, h_mid, mlp_saved = saved[layer]
        dh, dhb, cs = mlp_bwd(dh, dhb, h_mid, mlp_saved, layer)
        sg['b_bo'][j] = cs[0]
        (do,) = dx_mm(f"b_wo_dx{j}", dhb, 'b_wo', j, lambda acc: (acc,), (), (BF16,))
        dw_mm(f"b_wo_dw{j}", o, dhb, 'b_wo', j)
        dq_t, dkp, dvp, dbias[j], dsink = swa_bwd(f"swa_bwd{j}", qbt, kpt, vpt, bias_t, sink_rows[j], do.T, dkp, dvp)
        sg['b_sinks'][j] = colsum(f"sink_grad{j}", dsink.reshape(16, WINDOW).T)[0]
        dq = dq_t.T
        sg['b_bq'][j] = colsum(f"b_bq_grad{j}", dq)[0]
        (dn1,) = dx_mm(f"b_q_dx{j}", dq, 'b_wq', j, lambda acc: (acc,), (), (F32,))
        dw_mm(f"b_q_dw{j}", n1, dq, 'b_wq', j)
        dh, dhb, dg, _ = rms_bwd(f"b_norm_bwd{j}", h_in, small['b_norm'][j], dn1, dh)
        sg['b_norm'][j] = dg[0]
    unt = lambda t: t.reshape(2, 2 * WINDOW, 8, WINDOW).transpose(0, 2, 3, 1).reshape(bias_flat.shape)
    sg['rel_bias'] = bias_table_grad(unt(dbias[0]), unt(dbias[1]))[:, :N_BUCKETS].T

    dkv = jnp.concatenate([dkp[:, :, WINDOW:], dvp[:, :, WINDOW:]], axis=0).reshape(-1, s).T
    sg['b_kv'] = colsum("b_kv_grad", dkv)[0]
    dkvb = dkv.astype(BF16)
    (dnkv,) = dx_mm("kv_dx", dkvb, 'w_kv', 0, lambda acc: (acc,), (), (F32,))
    dw_mm("kv_dw", nkv, dkvb, 'w_kv', 0)
    dh, dhb, dg, _ = rms_bwd("kv_norm_bwd", h_kv, small['kv_norm'], dnkv, dh)
    sg['kv_norm'] = dg[0]

    sg['a_norm'] = [None] * n_a
    for l in reversed(range(n_a)):
        h_in, n1, qkvt, o, rtab, h_mid, mlp_saved = saved[l]
        dh, dhb, _ = mlp_bwd(dh, dhb, h_mid, mlp_saved, l)
        (do,) = dx_mm(f"a_wo_dx{l}", dhb, 'a_wo', l, lambda acc: (acc,), (), (BF16,))
        dw_mm(f"a_wo_dw{l}", o, dhb, 'a_wo', l)
        dq_t, dk_t, dv_t, carried = sb_bwd(f"sb_bwd{l}", qkvt, do.T.reshape(d // HEAD_DIM, HEAD_DIM, s), rtab,
                                           ex.bwd_carry(l, gb))
        ex.bwd_done(l, carried)
        dqkv = jnp.concatenate([dq_t, dk_t, dv_t], axis=0).reshape(3 * d, s).T
        (dn1,) = dx_mm(f"a_qkv_dx{l}", dqkv, 'a_wqkv', l, lambda acc: (acc,), (), (F32,))
        dw_mm(f"a_qkv_dw{l}", n1, dqkv, 'a_wqkv', l)
        dh, dhb, dg, _ = rms_bwd(f"a_norm_bwd{l}", h_in, small['a_norm'][l], dn1, dh)
        sg['a_norm'][l] = dg[0]

    small_grads = {
        'a_norm': jnp.stack(sg['a_norm']), 'kv_norm': sg['kv_norm'], 'b_kv': sg['b_kv'],
        'b_norm': jnp.stack(sg['b_norm']), 'b_bq': jnp.stack(sg['b_bq']), 'b_sinks': jnp.stack(sg['b_sinks']),
        'b_bo': jnp.stack(sg['b_bo']), 'rel_bias': sg['rel_bias'], 'mlp_norm': jnp.stack(sg['mlp_norm']),
        'final_norm': sg['final_norm'],
    }
    return loss_b, dh, gb, small_grads


def _full_shape(name, shard_shape):
    if name in COL_SHARDED:
        return shard_shape[:2] + (N_DEV * shard_shape[2],)
    nl, r, n = shard_shape
    return (nl, N_DEV, r, n)


def _as_w3_shape(name, shard_shape):
    full = _full_shape(name, shard_shape)
    return full if name in COL_SHARDED else (full[0], full[1] * full[2], full[3])


def _as_w3(name, full):
    if name in COL_SHARDED:
        return full
    nl, nd, r, n = full.shape
    return full.reshape(nl, nd * r, n)


AG_GROUPS = {
    0: (('a_wqkv', 0, 1),),
    1: (('a_wo', 0, 2), ('mlp_up', 0, 2), ('mlp_down', 0, 2), ('a_wqkv', 1, 1)),
    2: (('mlp_up', 2, 2), ('mlp_down', 2, 2), ('b_wq', 0, 2), ('b_wo', 0, 2), ('w_kv', 0, 1)),
}
RS_GROUPS = {
    'A': (('mlp_up', 1, 3), ('mlp_down', 1, 3), ('b_wq', 0, 2), ('b_wo', 0, 2), ('w_kv', 0, 1)),
    'B': (('a_wqkv', 1, 1), ('a_wo', 1, 1), ('mlp_up', 0, 1), ('mlp_down', 0, 1)),
    'C': (('a_wqkv', 0, 1), ('a_wo', 0, 1)),
}


class _Exchanges:
    def __init__(self, full0, shards, core, chip, w3, m3, v3):
        self.wbuf = {0: {n: _as_w3(n, full0[n]) for n, _, _ in AG_GROUPS[0]}}
        self.shards, self.core, self.chip = shards, core, chip
        self.w3, self.m3, self.v3 = w3, m3, v3
        self.shard_dims = {n: w3[n].shape[1:] for n in BIG}
        self.parts = {}
        self.out = {}

    def weight(self, name, layer):
        for group, members in AG_GROUPS.items():
            for n, l0, nl in members:
                if n == name and l0 <= layer < l0 + nl:
                    return self.wbuf[group][name], layer - l0
        raise KeyError((name, layer))

    def fwd_carry(self, layer):
        names = [n for n, _, _ in AG_GROUPS[layer + 1]]
        shards = [self.shards[layer + 1][n] for n in names]
        return ag_direct(names, shards, [_full_shape(n, sh.shape) for n, sh in zip(names, shards)])

    def fwd_done(self, layer, carried):
        names = [n for n, _, _ in AG_GROUPS[layer + 1]]
        self.wbuf[layer + 1] = {n: _as_w3(n, f) for n, f in zip(names, ag_forward(layer + 1, names, list(carried)))}

    def grad(self, name, layer):
        for group, members in RS_GROUPS.items():
            for n, l0, nl in members:
                if n == name and l0 <= layer < l0 + nl:
                    return (group, name), layer - l0, _as_w3_shape(name, (nl,) + self.shard_dims[name])
        raise KeyError((name, layer))

    def _sibling_stage(self, group, gb):
        names = [n for n, _, _ in RS_GROUPS[group]]
        shapes = [(nl,) + self.shard_dims[n] for n, _, nl in RS_GROUPS[group]]
        gfull = [gb[(group, n)].reshape(_full_shape(n, sh)) for n, sh in zip(names, shapes)]
        recv = sibling_exchange(group, names, gfull, shapes)
        self.parts[group] = [sibling_sum(f"rs_sibling_sum_{group}_{n}", n in COL_SHARDED, g, r, self.core)
                             for n, g, r in zip(names, gfull, recv)]
        return names, shapes

    def bwd_carry(self, layer, gb):
        names, shapes = self._sibling_stage('A' if layer == 1 else 'B', gb)
        return chip_exchange(names, self.parts['A' if layer == 1 else 'B'], shapes)

    def bwd_done(self, layer, carried):
        self._adamw('A' if layer == 1 else 'B', carried)

    def finish(self, gb):
        names, shapes = self._sibling_stage('C', gb)
        ce = chip_exchange(names, self.parts['C'], shapes)
        self._adamw('C', comm_call("rs_chip_exchange_C", ce.build, ce.ins, ce.out_shapes, ce.sems))
        return self.out

    def _adamw(self, group, recv2):
        for (n, l0, _), p, r in zip(RS_GROUPS[group], self.parts[group], recv2):
            self.out[n] = reduce_adamw(f"adamw_{group}_{n}", p, r, self.chip, self.w3[n], self.m3[n], self.v3[n],
                                       l0, self.out.get(n))


def _pack_small(vals):
    flat = jnp.concatenate([vals[n].reshape(-1).astype(F32) for n in SMALL] + [vals['loss'].reshape(-1)])
    rows = -(-flat.shape[0] // 1024) * 8
    return jnp.pad(flat, (0, rows * 128 - flat.shape[0])).reshape(rows, 128)


def _unpack_small(packed, shapes):
    flat = packed.reshape(-1)
    out, off = {}, 0
    for n in SMALL + ['loss']:
        size = int(np.prod(shapes[n]))
        out[n] = flat[off:off + size].reshape(shapes[n])
        off += size
    return out


def kernel(x, a_norm, a_wqkv, a_wo, kv_norm, w_kv, b_kv, b_norm, b_wq, b_bq, b_sinks, b_wo, b_bo, rel_bias, mlp_norm, mlp_up, mlp_down, final_norm, loss_target, m_a_norm, m_a_wqkv, m_a_wo, m_kv_norm, m_w_kv, m_b_kv, m_b_norm, m_b_wq, m_b_bq, m_b_sinks, m_b_wo, m_b_bo, m_rel_bias, m_mlp_norm, m_mlp_up, m_mlp_down, m_final_norm, v_a_norm, v_a_wqkv, v_a_wo, v_kv_norm, v_w_kv, v_b_kv, v_b_norm, v_b_wq, v_b_bq, v_b_sinks, v_b_wo, v_b_bo, v_rel_bias, v_mlp_norm, v_mlp_up, v_mlp_down, v_final_norm):
    w = dict(a_norm=a_norm, a_wqkv=a_wqkv, a_wo=a_wo, kv_norm=kv_norm, w_kv=w_kv, b_kv=b_kv, b_norm=b_norm,
             b_wq=b_wq, b_bq=b_bq, b_sinks=b_sinks, b_wo=b_wo, b_bo=b_bo, rel_bias=rel_bias, mlp_norm=mlp_norm,
             mlp_up=mlp_up, mlp_down=mlp_down, final_norm=final_norm)
    m = dict(a_norm=m_a_norm, a_wqkv=m_a_wqkv, a_wo=m_a_wo, kv_norm=m_kv_norm, w_kv=m_w_kv, b_kv=m_b_kv,
             b_norm=m_b_norm, b_wq=m_b_wq, b_bq=m_b_bq, b_sinks=m_b_sinks, b_wo=m_b_wo, b_bo=m_b_bo,
             rel_bias=m_rel_bias, mlp_norm=m_mlp_norm, mlp_up=m_mlp_up, mlp_down=m_mlp_down, final_norm=m_final_norm)
    v = dict(a_norm=v_a_norm, a_wqkv=v_a_wqkv, a_wo=v_a_wo, kv_norm=v_kv_norm, w_kv=v_w_kv, b_kv=v_b_kv,
             b_norm=v_b_norm, b_wq=v_b_wq, b_bq=v_b_bq, b_sinks=v_b_sinks, b_wo=v_b_wo, b_bo=v_b_bo,
             rel_bias=v_rel_bias, mlp_norm=v_mlp_norm, mlp_up=v_mlp_up, mlp_down=v_mlp_down, final_norm=v_final_norm)
    px, py, pc = _place()
    me = 4 * px + 2 * py + pc
    chip = (2 * px + py).astype(jnp.int32)
    core = pc.astype(jnp.int32)

    as3 = lambda t: t[None] if t.ndim == 2 else t
    w3, m3, v3 = ({n: as3(src[n]) for n in BIG} for src in (w, m, v))
    shards = {g: {n: w3[n][l0:l0 + nl].astype(BF16) for n, l0, nl in members} for g, members in AG_GROUPS.items()}
    an_pad = jnp.zeros((8, 128), F32).at[:a_norm.shape[0]].set(a_norm)
    names0 = [n for n, _, _ in AG_GROUPS[0]]
    full0 = all_gather_weights(names0 + ['a_norm'], [shards[0][n] for n in names0] + [an_pad],
                               [_full_shape(n, shards[0][n].shape) for n in names0] + [(N_DEV, 8, 128)])
    full0 = dict(zip(names0 + ['a_norm'], full0))
    n_a = a_norm.shape[0]
    small = {n: w[n] for n in SMALL}
    small['a_norm'] = full0['a_norm'][:, :n_a].transpose(1, 0, 2).reshape(n_a, -1)

    ex = _Exchanges(full0, shards, core, chip, w3, m3, v3)
    loss_b, grad_x, gb, sgrads = local_step(x[0], loss_target[0], small, ex)
    out = {n: [t.reshape(w[n].shape) for t in bufs] for n, bufs in ex.finish(gb).items()}

    sgrads['loss'] = loss_b[0, :1]
    gathered = all_gather_rows(_pack_small(sgrads))
    shapes = {n: w[n].shape for n in SMALL}
    shapes['a_norm'] = (n_a, a_norm.shape[1] * N_DEV)
    shapes['loss'] = (1,)
    zeros1 = jnp.zeros((1,), F32)

    def packed(src):
        vals = {n: src[n] for n in SMALL}
        vals['a_norm'] = jnp.zeros(shapes['a_norm'], F32)
        vals['loss'] = zeros1
        return _pack_small(vals)

    sm = small_adamw("adamw_small", gathered, packed(w), packed(m), packed(v))
    sm = [_unpack_small(t, shapes) for t in sm]
    g_an = lax.dynamic_slice_in_dim(sm[0]['a_norm'], me * a_norm.shape[1], a_norm.shape[1], axis=1)
    pad = lambda t: jnp.zeros((8, 128), F32).at[:n_a].set(t)
    gathered_an = jnp.zeros((N_DEV, 8, 128), F32).at[0].set(pad(g_an))
    an = small_adamw("adamw_a_norm", gathered_an, pad(a_norm), pad(m_a_norm), pad(v_a_norm))
    for i in range(4):
        sm[i]['a_norm'] = an[i][:n_a]
    for n in BIG:
        for i in range(4):
            sm[i][n] = out[n][i]
    loss = sm[0]['loss'][0]
    return (loss, grad_x[None], *[sm[0][n] for n in WEIGHTS], *[sm[1][n] for n in WEIGHTS],
            *[sm[2][n] for n in WEIGHTS], *[sm[3][n] for n in WEIGHTS])
```

```python
import functools
import math

import numpy as np
import jax
import jax.numpy as jnp
from jax import lax
from jax.experimental import pallas as pl
from jax.experimental.pallas import tpu as pltpu

F32 = jnp.float32
BF16 = jnp.bfloat16
MESH = pl.DeviceIdType.MESH

N_DEV = 8
HEAD_DIM = 64
WINDOW = 128
N_BUCKETS = 32
EPS = 1e-5
NEG_INF = -1e30
Q_SCALE = 1.0 / math.sqrt(HEAD_DIM)
LOG2E = 1.4426950408889634

ADAM_LR, ADAM_B1, ADAM_B2, ADAM_EPS, ADAM_WD, ADAM_STEP = 0.001, 0.9, 0.999, 1e-08, 0.01, 10

SB_BQ = 512
SB_BK = 128
SB_DEAD = 160.0
SB_UNSEEN = 1e30
ROW_TILE = 512
VMEM_LIMIT = 56 * 1024 * 1024

WEIGHTS = ['a_norm', 'a_wqkv', 'a_wo', 'kv_norm', 'w_kv', 'b_kv', 'b_norm', 'b_wq', 'b_bq', 'b_sinks', 'b_wo',
           'b_bo', 'rel_bias', 'mlp_norm', 'mlp_up', 'mlp_down', 'final_norm']
BIG = ['a_wqkv', 'a_wo', 'w_kv', 'b_wq', 'b_wo', 'mlp_up', 'mlp_down']
COL_SHARDED = ('a_wqkv', 'mlp_up')
SMALL = ['a_norm', 'kv_norm', 'b_kv', 'b_norm', 'b_bq', 'b_sinks', 'b_bo', 'rel_bias', 'mlp_norm', 'final_norm']


def _params(sem=None):
    return pltpu.CompilerParams(dimension_semantics=sem, vmem_limit_bytes=VMEM_LIMIT)


def _pick(n, cands):
    for c in cands:
        if n % c == 0:
            return c
    raise ValueError(n)


def _tile(n, want):
    return n if n <= want else _pick(n, (want, want // 2, want // 4))


def mm_nn(name, a, w3, layer, epilogue, extras, out_dtypes, a_t=False, out_t=False):
    k, m = a.shape if a_t else a.shape[::-1]
    _, kw, n = w3.shape
    assert kw == k
    tm = _tile(m, 1024 if k <= 1024 else 512)
    tn = _tile(n, 1024)
    ne, no = len(extras), len(out_dtypes)
    a_dim = 0 if a_t else 1

    def body(a_ref, w_ref, *rest):
        ex, outs = rest[:ne], rest[ne:ne + no]
        if out_t:
            acc = lax.dot_general(w_ref[...], a_ref[...], (((0,), (a_dim,)), ((), ())), preferred_element_type=F32)
        else:
            acc = lax.dot_general(a_ref[...], w_ref[...], (((a_dim,), (0,)), ((), ())), preferred_element_type=F32)
        for o, r in zip(outs, epilogue(acc, *[e[...] for e in ex])):
            o[...] = r.astype(o.dtype)

    if out_t:
        tile = pl.BlockSpec((tn, tm), lambda i, j: (j, i))
        vec = pl.BlockSpec((tn, 1), lambda i, j: (j, 0))
        out_shape = (n, m)
    else:
        tile = pl.BlockSpec((tm, tn), lambda i, j: (i, j))
        vec = pl.BlockSpec((1, tn), lambda i, j: (0, j))
        out_shape = (m, n)
    a_spec = pl.BlockSpec((k, tm), lambda i, j: (0, i)) if a_t else pl.BlockSpec((tm, k), lambda i, j: (i, 0))
    return pl.pallas_call(
        body, name=name, grid=(m // tm, n // tn),
        in_specs=[a_spec, pl.BlockSpec((None, k, tn), lambda i, j: (layer, 0, j))]
        + [tile if e.shape == out_shape else vec for e in extras],
        out_specs=[tile] * no,
        out_shape=[jax.ShapeDtypeStruct(out_shape, d) for d in out_dtypes],
        compiler_params=_params(("parallel", "parallel")),
    )(a, w3, *extras)


def mm_nt(name, dy, w3, layer, epilogue, extras, out_dtypes, a_t=False, out_t=False):
    n, m = dy.shape if a_t else dy.shape[::-1]
    _, k, nw = w3.shape
    assert nw == n
    tm = _tile(m, 1024 if n <= 1024 else 512)
    tko = _tile(k, 1024)
    ne, no = len(extras), len(out_dtypes)
    a_dim = 0 if a_t else 1

    def body(a_ref, w_ref, *rest):
        ex, outs = rest[:ne], rest[ne:ne + no]
        if out_t:
            acc = lax.dot_general(w_ref[...], a_ref[...], (((1,), (a_dim,)), ((), ())), preferred_element_type=F32)
        else:
            acc = lax.dot_general(a_ref[...], w_ref[...], (((a_dim,), (1,)), ((), ())), preferred_element_type=F32)
        for o, v in zip(outs, epilogue(acc, *[e[...] for e in ex])):
            o[...] = v.astype(o.dtype)

    if out_t:
        tile = pl.BlockSpec((tko, tm), lambda i, ko: (ko, i))
        out_shape = (k, m)
    else:
        tile = pl.BlockSpec((tm, tko), lambda i, ko: (i, ko))
        out_shape = (m, k)
    a_spec = pl.BlockSpec((n, tm), lambda i, ko: (0, i)) if a_t else pl.BlockSpec((tm, n), lambda i, ko: (i, 0))
    return pl.pallas_call(
        body, name=name, grid=(m // tm, k // tko),
        in_specs=[a_spec, pl.BlockSpec((None, tko, n), lambda i, ko: (layer, ko, 0))] + [tile] * ne,
        out_specs=[tile] * no,
        out_shape=[jax.ShapeDtypeStruct(out_shape, d) for d in out_dtypes],
        compiler_params=_params(("parallel", "parallel")),
    )(dy, w3, *extras)


def mm_tn(name, x, dy, gbuf, shape, layer, x_t=False, dy_t=False):
    k, s = x.shape if x_t else x.shape[::-1]
    _, kw, n = shape
    assert kw == k and dy.shape == ((n, s) if dy_t else (s, n))
    tkk = _tile(k, 512)
    tn = _tile(n, 1024)

    def body(x_ref, dy_ref, *rest):
        g_out = rest[-1]
        g_out[...] = lax.dot_general(x_ref[...], dy_ref[...], (((1 if x_t else 0,), (1 if dy_t else 0,)), ((), ())),
                                     preferred_element_type=F32).astype(g_out.dtype)

    prev = [] if gbuf is None else [gbuf]
    x_spec = pl.BlockSpec((tkk, s), lambda ki, j: (ki, 0)) if x_t else pl.BlockSpec((s, tkk), lambda ki, j: (0, ki))
    dy_spec = pl.BlockSpec((tn, s), lambda ki, j: (j, 0)) if dy_t else pl.BlockSpec((s, tn), lambda ki, j: (0, j))
    return pl.pallas_call(
        body, name=name, grid=(k // tkk, n // tn),
        in_specs=[x_spec, dy_spec] + [pl.BlockSpec(memory_space=pl.ANY)] * len(prev),
        out_specs=pl.BlockSpec((None, tkk, tn), lambda ki, j: (layer, ki, j)),
        out_shape=jax.ShapeDtypeStruct(shape, BF16),
        input_output_aliases={2: 0} if prev else {},
        compiler_params=_params(("parallel", "parallel")),
    )(x, dy, *prev)


def rms_fwd(name, h, g):
    s, d = h.shape
    tr = _pick(s, (ROW_TILE, 256, 128))

    def body(h_ref, g_ref, o_ref):
        x = h_ref[...]
        r = lax.rsqrt(jnp.mean(x * x, axis=-1, keepdims=True) + EPS)
        o_ref[...] = (x * r * g_ref[...]).astype(o_ref.dtype)

    return pl.pallas_call(
        body, name=name, grid=(s // tr,),
        in_specs=[pl.BlockSpec((tr, d), lambda i: (i, 0)), pl.BlockSpec((1, d), lambda i: (0, 0))],
        out_specs=pl.BlockSpec((tr, d), lambda i: (i, 0)),
        out_shape=jax.ShapeDtypeStruct((s, d), BF16),
        compiler_params=_params(("parallel",)),
    )(h, g.reshape(1, d))


def rms_bwd(name, h, g, dn, dres):
    s, d = h.shape
    tr = _pick(s, (ROW_TILE, 256, 128))

    def body(h_ref, g_ref, dn_ref, dres_ref, dx_ref, dxb_ref, dg_ref, cs_ref):
        i = pl.program_id(0)
        x = h_ref[...]
        r = lax.rsqrt(jnp.mean(x * x, axis=-1, keepdims=True) + EPS)
        xh = x * r
        dn_ = dn_ref[...]
        dyg = dn_ * g_ref[...]
        dx = dres_ref[...] + r * (dyg - xh * jnp.mean(dyg * xh, axis=-1, keepdims=True))
        dx_ref[...] = dx
        dxb_ref[...] = dx.astype(BF16)

        @pl.when(i == 0)
        def _():
            dg_ref[...] = jnp.zeros_like(dg_ref)
            cs_ref[...] = jnp.zeros_like(cs_ref)

        dg_ref[...] += jnp.sum(dn_ * xh, axis=0, keepdims=True)
        cs_ref[...] += jnp.sum(dx, axis=0, keepdims=True)

    row = pl.BlockSpec((tr, d), lambda i: (i, 0))
    vec = pl.BlockSpec((1, d), lambda i: (0, 0))
    return pl.pallas_call(
        body, name=name, grid=(s // tr,),
        in_specs=[row, vec, row, row],
        out_specs=[row, row, vec, vec],
        out_shape=[jax.ShapeDtypeStruct((s, d), F32), jax.ShapeDtypeStruct((s, d), BF16),
                   jax.ShapeDtypeStruct((1, d), F32), jax.ShapeDtypeStruct((1, d), F32)],
        compiler_params=_params(("arbitrary",)),
    )(h, g.reshape(1, d), dn, dres)


def loss_head(h, g, target):
    s, d = h.shape
    tr = _pick(s, (ROW_TILE, 256, 128))

    def body(h_ref, g_ref, t_ref, dx_ref, dxb_ref, dg_ref, loss_ref):
        i = pl.program_id(0)
        x = h_ref[...]
        r = lax.rsqrt(jnp.mean(x * x, axis=-1, keepdims=True) + EPS)
        xh = x * r
        gw = g_ref[...]
        err = xh * gw - t_ref[...]
        dn_ = err * (1.0 / d)
        dyg = dn_ * gw
        dx = r * (dyg - xh * jnp.mean(dyg * xh, axis=-1, keepdims=True))
        dx_ref[...] = dx
        dxb_ref[...] = dx.astype(BF16)

        @pl.when(i == 0)
        def _():
            dg_ref[...] = jnp.zeros_like(dg_ref)
            loss_ref[...] = jnp.zeros_like(loss_ref)

        dg_ref[...] += jnp.sum(dn_ * xh, axis=0, keepdims=True)
        per_row = jnp.sum(err * err, axis=-1, keepdims=True) * (0.5 / d)
        loss_ref[...] += jnp.broadcast_to(jnp.sum(per_row, axis=0, keepdims=True), loss_ref.shape)

    row = pl.BlockSpec((tr, d), lambda i: (i, 0))
    vec = pl.BlockSpec((1, d), lambda i: (0, 0))
    return pl.pallas_call(
        body, name="loss_head", grid=(s // tr,),
        in_specs=[row, vec, row],
        out_specs=[row, row, vec, pl.BlockSpec((1, 128), lambda i: (0, 0))],
        out_shape=[jax.ShapeDtypeStruct((s, d), F32), jax.ShapeDtypeStruct((s, d), BF16),
                   jax.ShapeDtypeStruct((1, d), F32), jax.ShapeDtypeStruct((1, 128), F32)],
        compiler_params=_params(("arbitrary",)),
    )(h, g.reshape(1, d), target)


def colsum(name, x):
    s, n = x.shape
    tr = _pick(s, (ROW_TILE, 256, 128))

    def body(x_ref, o_ref):
        @pl.when(pl.program_id(0) == 0)
        def _():
            o_ref[...] = jnp.zeros_like(o_ref)

        o_ref[...] += jnp.sum(x_ref[...].astype(F32), axis=0, keepdims=True)

    return pl.pallas_call(
        body, name=name, grid=(s // tr,),
        in_specs=[pl.BlockSpec((tr, n), lambda i: (i, 0))],
        out_specs=pl.BlockSpec((1, n), lambda i: (0, 0)),
        out_shape=jax.ShapeDtypeStruct((1, n), F32),
        compiler_params=_params(("arbitrary",)),
    )(x)


def rowsum(name, x):
    n, s = x.shape
    ts = _pick(s, (1024, 512, 256, 128))

    def body(x_ref, o_ref):
        @pl.when(pl.program_id(0) == 0)
        def _():
            o_ref[...] = jnp.zeros_like(o_ref)

        o_ref[...] += jnp.sum(x_ref[...].astype(F32), axis=1, keepdims=True)

    return pl.pallas_call(
        body, name=name, grid=(s // ts,),
        in_specs=[pl.BlockSpec((n, ts), lambda i: (0, i))],
        out_specs=pl.BlockSpec((n, 1), lambda i: (0, 0)),
        out_shape=jax.ShapeDtypeStruct((n, 1), F32),
        compiler_params=_params(("arbitrary",)),
    )(x)[:, 0]


def _tri_rows(reverse):
    i = np.arange(SB_BK)
    tri = (i[None, :] >= i[:, None]) if reverse else (i[None, :] <= i[:, None])
    tri = np.concatenate([tri, tri], axis=1)
    return jnp.asarray(np.concatenate([tri, np.ones((8, 2 * SB_BK), bool)], axis=0), BF16)


def _hi_lo_rows(x):
    hi = x.astype(BF16)
    lo = (x - hi.astype(F32)).astype(BF16)
    return jnp.concatenate([hi, lo], axis=0)


def _softplus2(zs):
    neg_abs = lax.bitcast_convert_type(lax.bitcast_convert_type(zs, jnp.uint32) | jnp.uint32(0x80000000), F32)
    return jnp.maximum(zs, 0.0) + jnp.log2(1.0 + jnp.exp2(neg_abs))


def _pair_mask(first_rel_block, bq):
    key = lax.broadcasted_iota(jnp.int32, (2 * SB_BK, bq), 0) + first_rel_block * SB_BK
    qry = lax.broadcasted_iota(jnp.int32, (2 * SB_BK, bq), 1)
    return key < qry


def _row_of(table8, sub8, r):
    return jnp.sum(jnp.where(sub8 == r, table8, 0.0), axis=0, keepdims=True)


def _keys(j0):
    return pl.ds(pl.multiple_of(j0 * SB_BK, 2 * SB_BK), 2 * SB_BK)


class Carry:
    def __init__(self, build, ins, out_shapes, sems):
        self.build, self.ins, self.out_shapes, self.sems = build, list(ins), list(out_shapes), list(sems)


def _carried(carry, rest, n_out, n_scratch, first, last):
    n_ci = len(carry.ins) if carry else 0
    n_co = len(carry.out_shapes) if carry else 0
    cin, outs = rest[:n_ci], rest[n_ci:n_ci + n_out]
    cout = rest[n_ci + n_out:n_ci + n_out + n_co]
    scratch = rest[n_ci + n_out + n_co:n_ci + n_out + n_co + n_scratch]
    csems = rest[n_ci + n_out + n_co + n_scratch:]

    def start():
        if carry:
            @pl.when(first)
            def _():
                for cp in carry.build(cin, cout, *csems):
                    cp.start()

    def wait():
        if carry:
            @pl.when(last)
            def _():
                for cp in carry.build(cin, cout, *csems):
                    cp.wait()

    return outs, scratch, start, wait


def _contract0(a, b):
    return lax.dot_general(a, b, (((0,), (0,)), ((), ())), preferred_element_type=F32)


def _contract1(a, b):
    return lax.dot_general(a, b, (((1,), (1,)), ((), ())), preferred_element_type=F32)


def sb_fwd(name, qkvt, exchange=None):
    nh, dh, s = qkvt.shape[0] // 3, qkvt.shape[1], qkvt.shape[2]
    bq = SB_BQ
    per_q = bq // SB_BK
    nkb = s // SB_BK
    assert s % bq == 0 and per_q == 4 and nkb % 8 == 0

    def body(q_ref, k_ref, v_ref, a_ref, *rest):
        i = pl.program_id(1)
        first = (pl.program_id(0) == 0) & (i == 0)
        last = (pl.program_id(0) == nh - 1) & (i == s // bq - 1)
        (o_ref, rtab_ref), (acc, zbuf, wbuf), start_carried, wait_carried = _carried(exchange, rest, 2, 3, first, last)
        start_carried()
        qb = q_ref[...] * Q_SCALE
        tri = a_ref[...]
        sub8 = lax.broadcasted_iota(jnp.int32, (8, bq), 0)
        acc[...] = jnp.zeros_like(acc)
        rtab_ref[...] = jnp.full(rtab_ref.shape, SB_UNSEEN, F32)
        kf = k_ref[...].astype(F32)
        qf = qb.astype(F32)
        k_max2 = jnp.max(jnp.sum(kf * kf, axis=0, keepdims=True), axis=1, keepdims=True)
        bound = jnp.sqrt(jnp.sum(qf * qf, axis=0, keepdims=True) * k_max2) * (1.001 * LOG2E)

        def scores(j0):
            return _contract0(k_ref[:, _keys(j0)], qb) * LOG2E

        def pair(j0, slot, run, rt8, mask, has_prev):
            zs = zbuf[slot]
            zbuf[1 - slot] = scores(jnp.maximum(j0 - 2, 0))
            if has_prev:
                acc[...] += jnp.dot(v_ref[:, _keys(j0 + 2)], wbuf[1 - slot], preferred_element_type=F32)
            p = _softplus2(zs)
            if mask is not None:
                p = jnp.where(mask, p, 0.0)
            cr1 = jnp.dot(tri, _hi_lo_rows(p[SB_BK:]), preferred_element_type=F32)
            cr0 = jnp.dot(tri, _hi_lo_rows(p[:SB_BK]), preferred_element_type=F32)
            run1 = run + cr1[SB_BK:SB_BK + 1]
            w = jnp.exp2(jnp.concatenate([zs[:SB_BK] - cr0[:SB_BK] - run1, zs[SB_BK:] - cr1[:SB_BK] - run], axis=0))
            if mask is not None:
                w = jnp.where(mask, w, 0.0)
            wbuf[slot] = w.astype(BF16)
            rt8 = jnp.where(j0 % 8 == 6, SB_UNSEEN, rt8)
            rt8 = jnp.where(sub8 == (j0 + 1) % 8, run, jnp.where(sub8 == j0 % 8, run1, rt8))
            rtab_ref[pl.ds(pl.multiple_of((j0 // 8) * 8, 8), 8), :] = rt8
            return run1 + cr0[SB_BK:SB_BK + 1], rt8

        def alive(run):
            return jnp.min(run - bound) < SB_DEAD

        top = i * per_q
        zbuf[0] = scores(top + 2)
        state = (jnp.zeros((1, bq), F32), jnp.full((8, bq), SB_UNSEEN, F32))
        state = pair(top + 2, 0, *state, _pair_mask(2, bq), False)
        state = pair(top, 1, *state, _pair_mask(0, bq), True)

        def step(c):
            it, pairs, _, run, rt8 = c
            j0 = top - 2 - 4 * it
            run, rt8 = pair(j0, 0, run, rt8, None, True)
            go = alive(run)
            run, rt8 = lax.cond(go, lambda r, t: pair(j0 - 2, 1, r, t, None, True), lambda r, t: (r, t), run, rt8)
            return it + 1, pairs + 1 + go.astype(jnp.int32), go & alive(run), run, rt8

        pairs = lax.while_loop(lambda c: (c[0] < i) & c[2], step, (0, 0, alive(state[0]), *state))[1]
        acc[...] += jnp.dot(v_ref[:, _keys(top - 2 * pairs)], wbuf[(pairs + 1) % 2], preferred_element_type=F32)
        o_ref[...] = acc[...].astype(o_ref.dtype)
        wait_carried()

    qspec = pl.BlockSpec((None, dh, bq), lambda h, i: (h, 0, i))
    hbm = pl.BlockSpec(memory_space=pl.ANY)
    c_ins, c_outs, c_sems = (exchange.ins, exchange.out_shapes, exchange.sems) if exchange else ([], [], [])
    outs = pl.pallas_call(
        body, name=name, grid=(nh, s // bq),
        in_specs=[qspec, pl.BlockSpec((None, dh, s), lambda h, i: (h + nh, 0, 0)),
                  pl.BlockSpec((None, dh, s), lambda h, i: (h + 2 * nh, 0, 0)),
                  pl.BlockSpec((SB_BK + 8, 2 * SB_BK), lambda h, i: (0, 0))] + [hbm] * len(c_ins),
        out_specs=[qspec, pl.BlockSpec((None, nkb, bq), lambda h, i: (h, 0, i))] + [hbm] * len(c_outs),
        out_shape=[jax.ShapeDtypeStruct((nh, dh, s), BF16), jax.ShapeDtypeStruct((nh, nkb, s), F32)] + c_outs,
        scratch_shapes=[pltpu.VMEM((dh, bq), F32), pltpu.VMEM((2, 2 * SB_BK, bq), F32),
                        pltpu.VMEM((2, 2 * SB_BK, bq), BF16)] + c_sems,
        compiler_params=_params(("arbitrary", "arbitrary")),
    )(qkvt, qkvt, qkvt, _tri_rows(True), *c_ins)
    return outs[0], outs[1], outs[2:]


def sb_bwd(name, qkvt, dot_, rtab, exchange=None):
    nh, dh, s = qkvt.shape[0] // 3, qkvt.shape[1], qkvt.shape[2]
    bq = SB_BQ
    per_q = bq // SB_BK
    nkb = s // SB_BK
    nq = s // bq

    def body(qt_ref, kt_ref, vt_ref, dot_ref, rtab_ref, ar_ref, af_ref, *rest):
        i = pl.program_id(1)
        first = (pl.program_id(0) == 0) & (i == 0)
        last = (pl.program_id(0) == nh - 1) & (i == nq - 1)
        (dq_ref, dk_ref, dv_ref), (dq_acc, dk_acc, dv_acc, zbuf, dwbuf, dzbuf, wbuf), start_carried, wait_carried = \
            _carried(exchange, rest, 3, 7, first, last)
        start_carried()

        @pl.when(i == 0)
        def _():
            dk_acc[...] = jnp.zeros_like(dk_acc)
            dv_acc[...] = jnp.zeros_like(dv_acc)

        qtb = qt_ref[...] * Q_SCALE
        dotb = dot_ref[...]
        tri_rev = ar_ref[...][:SB_BK]
        tri_fwd = af_ref[...]
        sub8 = lax.broadcasted_iota(jnp.int32, (8, bq), 0)
        dq_acc[...] = jnp.zeros_like(dq_acc)
        last_j = i * per_q + 2
        seen = jnp.max(jnp.where(rtab_ref[...] < 0.1 * SB_UNSEEN, 1.0, 0.0), axis=1, keepdims=True)
        pairs = jnp.clip((jnp.sum(seen).astype(jnp.int32) - per_q) // 2, 0, 2 * i)
        odd = pairs % 2
        first_j = i * per_q - 2 * pairs

        def issue(j0, slot):
            zbuf[slot] = _contract0(kt_ref[:, _keys(j0)], qtb) * LOG2E
            dwbuf[slot] = _contract0(vt_ref[:, _keys(j0)], dotb)

        def retire(j0, slot):
            keys = _keys(j0)
            dq_acc[...] += jnp.dot(kt_ref[:, keys], dzbuf[slot], preferred_element_type=F32)
            dk_acc[:, keys] += _contract1(qtb, dzbuf[slot])
            dv_acc[:, keys] += _contract1(dotb, wbuf[slot])

        def pair(j0, slot, g_run, mask):
            zs = zbuf[slot]
            dw = dwbuf[slot]
            issue(jnp.minimum(j0 + 2, last_j), 1 - slot)
            retire(jnp.maximum(j0 - 2, first_j), 1 - slot)
            p_raw = _softplus2(zs)
            p = p_raw if mask is None else jnp.where(mask, p_raw, 0.0)
            c0 = jnp.dot(tri_rev, _hi_lo_rows(p[:SB_BK]), preferred_element_type=F32)
            c1 = jnp.dot(tri_rev, _hi_lo_rows(p[SB_BK:]), preferred_element_type=F32)
            rt8 = rtab_ref[pl.ds(pl.multiple_of((j0 // 8) * 8, 8), 8), :]
            r0 = _row_of(rt8, sub8, j0 % 8)
            r1 = _row_of(rt8, sub8, (j0 + 1) % 8)
            w = jnp.exp2(jnp.concatenate([zs[:SB_BK] - c0 - r0, zs[SB_BK:] - c1 - r1], axis=0))
            if mask is not None:
                w = jnp.where(mask, w, 0.0)
            g = w * dw
            gg0 = jnp.dot(tri_fwd, _hi_lo_rows(g[:SB_BK]), preferred_element_type=F32)
            gg1 = jnp.dot(tri_fwd, _hi_lo_rows(g[SB_BK:]), preferred_element_type=F32)
            g_run1 = g_run + gg0[SB_BK:SB_BK + 1]
            g_pre = jnp.concatenate([gg0[:SB_BK] + g_run, gg1[:SB_BK] + g_run1], axis=0)
            dz = g - jnp.exp2(zs - p_raw) * g_pre
            if mask is not None:
                dz = jnp.where(mask, dz, 0.0)
            dzbuf[slot] = dz.astype(BF16)
            wbuf[slot] = w.astype(BF16)
            return g_run1 + gg1[SB_BK:SB_BK + 1]

        issue(first_j, odd)
        dzbuf[...] = jnp.zeros(dzbuf.shape, BF16)
        wbuf[...] = jnp.zeros(wbuf.shape, BF16)

        def step(it, g_run):
            g_run = pair(4 * it, 0, g_run, None)
            return pair(4 * it + 2, 1, g_run, None)

        g_run = lax.cond(odd == 1, lambda g: pair(first_j, 1, g, None), lambda g: g, jnp.zeros((1, bq), F32))
        g_run = lax.fori_loop(i - pairs // 2, i, step, g_run)
        g_run = pair(last_j - 2, 0, g_run, _pair_mask(0, bq))
        pair(last_j, 1, g_run, _pair_mask(2, bq))
        retire(last_j, 1)
        dq_ref[...] = (dq_acc[...] * Q_SCALE).astype(dq_ref.dtype)

        @pl.when(i == nq - 1)
        def _():
            dk_ref[...] = dk_acc[...].astype(dk_ref.dtype)
            dv_ref[...] = dv_acc[...].astype(dv_ref.dtype)

        wait_carried()

    tspec = pl.BlockSpec((None, dh, bq), lambda h, i: (h, 0, i))
    fullspec = pl.BlockSpec((None, dh, s), lambda h, i: (h, 0, 0))
    aspec = pl.BlockSpec((SB_BK + 8, 2 * SB_BK), lambda h, i: (0, 0))
    pair_f32 = pltpu.VMEM((2, 2 * SB_BK, bq), F32)
    pair_bf16 = pltpu.VMEM((2, 2 * SB_BK, bq), BF16)
    hbm = pl.BlockSpec(memory_space=pl.ANY)
    c_ins, c_outs, c_sems = (exchange.ins, exchange.out_shapes, exchange.sems) if exchange else ([], [], [])
    outs = pl.pallas_call(
        body, name=name, grid=(nh, s // bq),
        in_specs=[tspec, pl.BlockSpec((None, dh, s), lambda h, i: (h + nh, 0, 0)),
                  pl.BlockSpec((None, dh, s), lambda h, i: (h + 2 * nh, 0, 0)), tspec,
                  pl.BlockSpec((None, nkb, bq), lambda h, i: (h, 0, i)), aspec, aspec] + [hbm] * len(c_ins),
        out_specs=[tspec, fullspec, fullspec] + [hbm] * len(c_outs),
        out_shape=[jax.ShapeDtypeStruct((nh, dh, s), BF16)] * 3 + c_outs,
        scratch_shapes=[pltpu.VMEM((dh, bq), F32), pltpu.VMEM((dh, s), F32), pltpu.VMEM((dh, s), F32),
                        pair_f32, pair_f32, pair_bf16, pair_bf16] + c_sems,
        compiler_params=_params(("arbitrary", "arbitrary")),
    )(qkvt, qkvt, qkvt, dot_, rtab, _tri_rows(True), _tri_rows(False), *c_ins)
    return outs[0], outs[1], outs[2], outs[3:]


SWA_QB = 2


def _swa_probs(qt, kt, bias_t, sink, i):
    cols = qt.shape[1]
    sc = _contract0(kt, qt) + bias_t
    kj = lax.broadcasted_iota(jnp.int32, (2 * WINDOW, cols), 0)
    qi = lax.broadcasted_iota(jnp.int32, (2 * WINDOW, cols), 1) & (WINDOW - 1)
    dist = qi + WINDOW - kj
    valid = (dist >= 0) & (dist < WINDOW) & ((kj >= WINDOW) | (i > 0))
    sc = jnp.where(valid, sc, NEG_INF)
    mx = jnp.maximum(jnp.max(sc, axis=0, keepdims=True), sink)
    p = jnp.exp(sc - mx)
    p_sink = jnp.exp(sink - mx)
    inv = 1.0 / (jnp.sum(p, axis=0, keepdims=True) + p_sink)
    return p, p_sink, inv


def _band(i):
    return pl.ds(pl.multiple_of(i * WINDOW, WINDOW), 2 * WINDOW)


def _heads_to_lanes(blk):
    return jnp.concatenate([blk[r * HEAD_DIM:(r + 1) * HEAD_DIM] for r in range(8)], axis=1)


def _lanes_to_heads(t):
    return jnp.concatenate([t[:, r * WINDOW:(r + 1) * WINDOW] for r in range(8)], axis=0)


def swa_fwd(name, qt, kpt, vpt, bias_t, sink_row):
    d, s = qt.shape
    ng, dh, sp = kpt.shape
    rows, cols = d // ng, SWA_QB * WINDOW
    assert (s // WINDOW) % SWA_QB == 0

    def body(q_ref, k_ref, v_ref, bias_ref, sink_ref, o_ref):
        for u in range(SWA_QB):
            i = pl.program_id(1) * SWA_QB + u
            lanes = slice(u * WINDOW, (u + 1) * WINDOW)
            qb = _heads_to_lanes(q_ref[:, lanes]) * Q_SCALE
            p, _, inv = _swa_probs(qb, k_ref[:, _band(i)], bias_ref[...], sink_ref[...], i)
            o_t = jnp.dot(v_ref[:, _band(i)], p.astype(BF16), preferred_element_type=F32) * inv
            o_ref[:, lanes] = _lanes_to_heads(o_t).astype(o_ref.dtype)

    qspec = pl.BlockSpec((rows, cols), lambda g, i: (g, i))
    kspec = pl.BlockSpec((None, dh, sp), lambda g, i: (g, 0, 0))
    return pl.pallas_call(
        body, name=name, grid=(ng, s // cols),
        in_specs=[qspec, kspec, kspec, pl.BlockSpec((None, 2 * WINDOW, 8 * WINDOW), lambda g, i: (g, 0, 0)),
                  pl.BlockSpec((None, 1, 8 * WINDOW), lambda g, i: (g, 0, 0))],
        out_specs=qspec,
        out_shape=jax.ShapeDtypeStruct(qt.shape, BF16),
        compiler_params=_params(("parallel", "arbitrary")),
    )(qt, kpt, vpt, bias_t, sink_row)


def swa_bwd(name, qt, kpt, vpt, bias_t, sink_row, dot_, dk_in, dv_in):
    d, s = qt.shape
    ng, dh, sp = kpt.shape
    rows, cols = d // ng, SWA_QB * WINDOW

    def body(q_ref, k_ref, v_ref, bias_ref, sink_ref, do_ref, dki_ref, dvi_ref, dq_ref, dk_ref, dv_ref, db_ref, ds_ref):
        @pl.when(pl.program_id(1) == 0)
        def _():
            dk_ref[...] = dki_ref[...]
            dv_ref[...] = dvi_ref[...]
            db_ref[...] = jnp.zeros_like(db_ref)
            ds_ref[...] = jnp.zeros_like(ds_ref)

        for u in range(SWA_QB):
            i = pl.program_id(1) * SWA_QB + u
            band = _band(i)
            lanes = slice(u * WINDOW, (u + 1) * WINDOW)
            qb = _heads_to_lanes(q_ref[:, lanes]) * Q_SCALE
            dob = _heads_to_lanes(do_ref[:, lanes])
            kt = k_ref[:, band]
            p, p_sink, inv = _swa_probs(qb, kt, bias_ref[...], sink_ref[...], i)
            p = p * inv
            dp = _contract0(v_ref[:, band], dob)
            delta = jnp.sum(p * dp, axis=0, keepdims=True)
            dsc = p * (dp - delta)
            ds_ref[...] -= p_sink * inv * delta
            db_ref[...] += dsc
            dscb = dsc.astype(BF16)
            dq_t = jnp.dot(kt, dscb, preferred_element_type=F32) * Q_SCALE
            dq_ref[:, lanes] = _lanes_to_heads(dq_t).astype(dq_ref.dtype)
            dk_ref[:, band] += _contract1(qb, dscb)
            dv_ref[:, band] += _contract1(dob, p.astype(BF16))

    qspec = pl.BlockSpec((rows, cols), lambda g, i: (g, i))
    kspec = pl.BlockSpec((None, dh, sp), lambda g, i: (g, 0, 0))
    bspec = pl.BlockSpec((None, 2 * WINDOW, 8 * WINDOW), lambda g, i: (g, 0, 0))
    sspec = pl.BlockSpec((None, 1, 8 * WINDOW), lambda g, i: (g, 0, 0))
    return pl.pallas_call(
        body, name=name, grid=(ng, s // cols),
        in_specs=[qspec, kspec, kspec, bspec, sspec, qspec, kspec, kspec],
        out_specs=[qspec, kspec, kspec, bspec, sspec],
        out_shape=[jax.ShapeDtypeStruct(qt.shape, BF16), jax.ShapeDtypeStruct(kpt.shape, F32),
                   jax.ShapeDtypeStruct(kpt.shape, F32), jax.ShapeDtypeStruct(bias_t.shape, F32),
                   jax.ShapeDtypeStruct(sink_row.shape, F32)],
        compiler_params=_params(("parallel", "arbitrary")),
    )(qt, kpt, vpt, bias_t, sink_row, dot_, dk_in, dv_in)


def _bucket_onehot():
    qi = np.arange(WINDOW)[:, None]
    kj = np.arange(2 * WINDOW)[None, :]
    n = np.maximum(qi + WINDOW - kj, 0)
    max_exact = N_BUCKETS // 2
    nf = np.maximum(n, 1).astype(np.float64)
    val = np.log(nf / max_exact) / math.log(WINDOW / max_exact) * (N_BUCKETS - max_exact)
    assert np.all(np.abs(val - np.round(val))[(n > max_exact) & (n < WINDOW)] > 1e-3)
    large = np.minimum(max_exact + val.astype(np.int64), N_BUCKETS - 1)
    bucket = np.where(n < max_exact, n, large).reshape(-1)
    onehot = np.zeros((128, bucket.size), np.float32)
    onehot[bucket, np.arange(bucket.size)] = 1.0
    return onehot


def _split3(x):
    a = x.astype(BF16)
    r = x - a.astype(F32)
    b = r.astype(BF16)
    c = (r - b.astype(F32)).astype(BF16)
    return a, b, c


def bias_table(rel_bias):
    nh = rel_bias.shape[1]
    oh = jnp.asarray(_bucket_onehot(), BF16)
    n = oh.shape[1]
    tn = 4096
    rb = jnp.zeros((nh, 128), F32).at[:, :N_BUCKETS].set(rel_bias.T)

    def body(rb_ref, oh_ref, o_ref):
        o_ref[...] = sum(jnp.dot(t, oh_ref[...], preferred_element_type=F32) for t in _split3(rb_ref[...]))

    return pl.pallas_call(
        body, name="bias_table", grid=(n // tn,),
        in_specs=[pl.BlockSpec((nh, 128), lambda i: (0, 0)), pl.BlockSpec((128, tn), lambda i: (0, i))],
        out_specs=pl.BlockSpec((nh, tn), lambda i: (0, i)),
        out_shape=jax.ShapeDtypeStruct((nh, n), F32),
        compiler_params=_params(("parallel",)),
    )(rb, oh)


def bias_table_grad(db0, db1):
    nh, n = db0.shape
    oh = jnp.asarray(_bucket_onehot(), BF16)
    tn = 4096

    def body(a_ref, b_ref, oh_ref, o_ref):
        @pl.when(pl.program_id(0) == 0)
        def _():
            o_ref[...] = jnp.zeros_like(o_ref)

        o_ref[...] += sum(lax.dot_general(t, oh_ref[...], (((1,), (1,)), ((), ())), preferred_element_type=F32)
                          for t in _split3(a_ref[...] + b_ref[...]))

    blk = pl.BlockSpec((nh, tn), lambda i: (0, i))
    return pl.pallas_call(
        body, name="bias_table_grad", grid=(n // tn,),
        in_specs=[blk, blk, pl.BlockSpec((128, tn), lambda i: (0, i))],
        out_specs=pl.BlockSpec((nh, 128), lambda i: (0, 0)),
        out_shape=jax.ShapeDtypeStruct((nh, 128), F32),
        compiler_params=_params(("arbitrary",)),
    )(db0, db1, oh)


def _owner_view(ref, name, d):
    if name == 'a_norm':
        return ref.at[d]
    if name in COL_SHARDED:
        n = ref.shape[2] // N_DEV
        return ref.at[:, :, pl.ds(pl.multiple_of(d * n, 128), n)]
    return ref.at[:, d]


def _place():
    return lax.axis_index("x"), lax.axis_index("y"), lax.axis_index("c")


def _dev(p):
    return 4 * p[0] + 2 * p[1] + p[2]


def _remote(src, dst, send_sem, recv_sem, to):
    return pltpu.make_async_remote_copy(src_ref=src, dst_ref=dst, send_sem=send_sem, recv_sem=recv_sem,
                                        device_id=to, device_id_type=MESH)


def _dma_sems(*shapes):
    return [pltpu.SemaphoreType.DMA(sh) for sh in shapes]


def comm_call(name, build, ins, out_shapes, sems, aliases=None):
    n_in, n_out = len(ins), len(out_shapes)

    def body(*refs):
        copies = build(refs[:n_in], refs[n_in:n_in + n_out], *refs[n_in + n_out:])
        for cp in copies:
            cp.start()
        for cp in copies:
            cp.wait()

    hbm = pl.BlockSpec(memory_space=pl.ANY)
    return pl.pallas_call(
        body, name=name, in_specs=[hbm] * n_in, out_specs=[hbm] * n_out, out_shape=list(out_shapes),
        scratch_shapes=sems, input_output_aliases=aliases or {},
    )(*ins)


def all_gather_weights(names, shards, full_shapes):
    n = len(names)

    def body(*refs):
        ins, outs = refs[:n], refs[n:2 * n]
        send_sems, recv_sems, local_sems = refs[2 * n:]
        x, y, c = _place()
        me, sibling = (x, y, c), (x, y, 1 - c)
        chips = [(1 - x, y), (x, 1 - y), (1 - x, 1 - y)]

        def copy(t, k, block, to, src=None):
            dst = _owner_view(outs[t], names[t], _dev(block))
            return _remote(dst if src is None else src, dst, send_sems.at[t, k], recv_sems.at[t, k], to)

        mine = [pltpu.make_async_copy(ins[t], _owner_view(outs[t], names[t], _dev(me)), local_sems.at[t])
                for t in range(n)]
        for cp in mine:
            cp.start()
        first = []
        for t in range(n):
            first.append(copy(t, 0, me, sibling, src=ins[t]))
            first += [copy(t, 1 + j, me, (*chip, c), src=ins[t]) for j, chip in enumerate(chips)]
        for cp in first:
            cp.start()
        passed = []
        for j, chip in enumerate(chips):
            for t in range(n):
                copy(t, 1 + j, (*chip, c), me).wait_recv()
                fwd = copy(t, 4 + j, (*chip, c), sibling)
                fwd.start()
                passed.append(fwd)
        for t in range(n):
            copy(t, 0, sibling, me).wait_recv()
            for j, chip in enumerate(chips):
                copy(t, 4 + j, (*chip, 1 - c), me).wait_recv()
        for cp in first + passed:
            cp.wait_send()
        for cp in mine:
            cp.wait()

    hbm = pl.BlockSpec(memory_space=pl.ANY)
    return pl.pallas_call(
        body, name="all_gather_layer0",
        in_specs=[hbm] * n, out_specs=[hbm] * n,
        out_shape=[jax.ShapeDtypeStruct(full_shapes[t], shards[t].dtype) for t in range(n)],
        scratch_shapes=_dma_sems((n, 7), (n, 7), (n,)),
    )(*shards)


def ag_direct(names, shards, full_shapes):
    n = len(names)

    def build(ins, outs, send_sems, recv_sems, local_sems):
        x, y, c = _place()
        peers = [(x, y, 1 - c), (1 - x, y, c), (x, 1 - y, c), (1 - x, 1 - y, c)]
        copies = []
        for t in range(n):
            dst = _owner_view(outs[t], names[t], _dev((x, y, c)))
            copies.append(pltpu.make_async_copy(ins[t], dst, local_sems.at[t]))
            copies += [_remote(ins[t], dst, send_sems.at[t, k], recv_sems.at[t, k], to) for k, to in enumerate(peers)]
        return copies

    return Carry(build, shards, [jax.ShapeDtypeStruct(full_shapes[t], shards[t].dtype) for t in range(n)],
                 _dma_sems((n, 4), (n, 4), (n,)))


def ag_forward(tag, names, partial):
    n = len(names)

    def build(ins, outs, send_sems, recv_sems):
        del ins
        x, y, c = _place()
        copies = []
        for t in range(n):
            for k, chip in enumerate([(1 - x, y), (x, 1 - y), (1 - x, 1 - y)]):
                view = _owner_view(outs[t], names[t], _dev((*chip, c)))
                copies.append(_remote(view, view, send_sems.at[t, k], recv_sems.at[t, k], (x, y, 1 - c)))
        return copies

    return comm_call(f"all_gather_forward_{tag}", build, partial, [jax.ShapeDtypeStruct(p.shape, p.dtype) for p in partial],
                     _dma_sems((n, 3), (n, 3)), aliases={t: t for t in range(n)})


def sibling_exchange(tag, names, grads, part_shapes):
    n = len(names)

    def build(ins, outs, send_sems, recv_sems):
        x, y, c = _place()
        return [_remote(_owner_view(ins[t], names[t], 2 * q + 1 - c), outs[t].at[q], send_sems.at[t, q],
                        recv_sems.at[t, q], (x, y, 1 - c)) for t in range(n) for q in range(4)]

    return comm_call(f"rs_sibling_exchange_{tag}", build, grads,
                     [jax.ShapeDtypeStruct((4,) + part_shapes[t], BF16) for t in range(n)], _dma_sems((n, 4), (n, 4)))


def chip_exchange(names, parts, part_shapes):
    n = len(names)

    def build(ins, outs, send_sems, recv_sems):
        x, y, c = _place()
        chips = [(1 - x, y), (x, 1 - y), (1 - x, 1 - y)]
        return [_remote(ins[t].at[2 * chip[0] + chip[1]], outs[t].at[k], send_sems.at[t, k], recv_sems.at[t, k],
                        (*chip, c)) for t in range(n) for k, chip in enumerate(chips)]

    return Carry(build, parts, [jax.ShapeDtypeStruct((3,) + part_shapes[t], BF16) for t in range(n)],
                 _dma_sems((n, 3), (n, 3)))


def all_gather_rows(x):
    r, w = x.shape

    def body(x_ref, out_ref, send_sems, recv_sems, local_sem):
        px, py, pc = _place()
        me = 4 * px + 2 * py + pc
        mine = pltpu.make_async_copy(x_ref, out_ref.at[me], local_sem)
        mine.start()
        copies = []
        for k in range(1, N_DEV):
            peer = (px ^ (k >> 2), py ^ ((k >> 1) & 1), pc ^ (k & 1))
            copies.append(pltpu.make_async_remote_copy(
                src_ref=x_ref, dst_ref=out_ref.at[me], send_sem=send_sems.at[k - 1], recv_sem=recv_sems.at[k - 1],
                device_id=peer, device_id_type=MESH))
        for cp in copies:
            cp.start()
        for k in range(1, N_DEV):
            peer_idx = me ^ k
            pltpu.make_async_remote_copy(
                src_ref=x_ref, dst_ref=out_ref.at[peer_idx], send_sem=send_sems.at[k - 1],
                recv_sem=recv_sems.at[k - 1], device_id=(px, py, pc), device_id_type=MESH).wait_recv()
        for cp in copies:
            cp.wait_send()
        mine.wait()

    vmem = pl.BlockSpec(memory_space=pltpu.VMEM)
    return pl.pallas_call(
        body, name="all_gather_small_grads",
        in_specs=[vmem], out_specs=vmem,
        out_shape=jax.ShapeDtypeStruct((N_DEV, r, w), x.dtype),
        scratch_shapes=[pltpu.SemaphoreType.DMA((N_DEV - 1,)), pltpu.SemaphoreType.DMA((N_DEV - 1,)),
                        pltpu.SemaphoreType.DMA],
    )(x)


def _adamw(w, g, m, v):
    m = ADAM_B1 * m + (1.0 - ADAM_B1) * g
    v = ADAM_B2 * v + (1.0 - ADAM_B2) * (g * g)
    m_hat = m / (1.0 - ADAM_B1 ** ADAM_STEP)
    v_hat = v / (1.0 - ADAM_B2 ** ADAM_STEP)
    return -ADAM_LR * (m_hat / (jnp.sqrt(v_hat) + ADAM_EPS) + ADAM_WD * w), m, v


def sibling_sum(name, col, grads, recv, core):
    _, nl, rows, cols = recv.shape
    tr = _tile(rows, 512)
    rspec = pl.BlockSpec((None, None, tr, cols), lambda q, l, i, c_ref: (q, l, i, 0))
    if col:
        gspec = pl.BlockSpec((None, tr, cols), lambda q, l, i, c_ref: (l, i, 2 * q + c_ref[0]))
    else:
        gspec = pl.BlockSpec((None, None, tr, cols), lambda q, l, i, c_ref: (l, 2 * q + c_ref[0], i, 0))

    def body(c_ref, g_ref, r_ref, o_ref):
        del c_ref
        o_ref[...] = (g_ref[...].astype(F32) + r_ref[...].astype(F32)).astype(BF16)

    return pl.pallas_call(
        body, name=name,
        grid_spec=pltpu.PrefetchScalarGridSpec(num_scalar_prefetch=1, grid=(4, nl, rows // tr),
                                               in_specs=[gspec, rspec], out_specs=rspec),
        out_shape=jax.ShapeDtypeStruct(recv.shape, BF16),
        compiler_params=_params(("parallel", "parallel", "parallel")),
    )(core.reshape(1), grads, recv)


def reduce_adamw(name, parts, recv, chip, w, m, v, l0, prev):
    _, nl, rows, cols = parts.shape
    tr = _tile(rows, 256)

    def body(q_ref, p_ref, r_ref, w_ref, m_ref, v_ref, *rest):
        del q_ref
        g_out, d_out, m_out, v_out = rest[-4:]
        g = ((p_ref[...].astype(F32) + r_ref[0].astype(F32)) + r_ref[1].astype(F32)) + r_ref[2].astype(F32)
        d, mn, vn = _adamw(w_ref[...], g, m_ref[...], v_ref[...])
        g_out[...] = g
        d_out[...] = d
        m_out[...] = mn
        v_out[...] = vn

    blk = pl.BlockSpec((None, tr, cols), lambda l, i, q_ref: (l0 + l, i, 0))
    prev = list(prev) if prev else []
    return pl.pallas_call(
        body, name=name,
        grid_spec=pltpu.PrefetchScalarGridSpec(
            num_scalar_prefetch=1, grid=(nl, rows // tr),
            in_specs=[pl.BlockSpec((None, None, tr, cols), lambda l, i, q_ref: (q_ref[0], l, i, 0)),
                      pl.BlockSpec((3, None, tr, cols), lambda l, i, q_ref: (0, l, i, 0)), blk, blk, blk]
            + [pl.BlockSpec(memory_space=pl.ANY)] * len(prev),
            out_specs=[blk] * 4),
        out_shape=[jax.ShapeDtypeStruct(w.shape, F32)] * 4,
        input_output_aliases={6 + i: i for i in range(len(prev))},
        compiler_params=_params(("parallel", "parallel")),
    )(chip.reshape(1), parts, recv, w, m, v, *prev)


def small_adamw(name, gathered, w, m, v):
    _, r, c = gathered.shape

    def body(ga_ref, w_ref, m_ref, v_ref, g_out, d_out, m_out, v_out):
        g = ga_ref[0]
        for d in range(1, N_DEV):
            g = g + ga_ref[d]
        dl, mn, vn = _adamw(w_ref[...], g, m_ref[...], v_ref[...])
        g_out[...] = g
        d_out[...] = dl
        m_out[...] = mn
        v_out[...] = vn

    return pl.pallas_call(
        body, name=name,
        out_shape=[jax.ShapeDtypeStruct((r, c), F32)] * 4,
        compiler_params=_params(),
    )(gathered, w, m, v)


def _add(acc, *ex):
    return (acc + ex[0],)


def local_step(x, target, small, ex):
    s, d = x.shape
    n_a, n_b = small['a_norm'].shape[0], small['b_norm'].shape[0]
    sg = {}
    gb = {}

    def fwd_mm(name, a, wname, layer, epilogue, extras, out_dtypes, **kw):
        return mm_nn(name, a, *ex.weight(wname, layer), epilogue, extras, out_dtypes, **kw)

    def dx_mm(name, dy, wname, layer, epilogue, extras, out_dtypes, **kw):
        return mm_nt(name, dy, *ex.weight(wname, layer), epilogue, extras, out_dtypes, **kw)

    def dw_mm(name, a, dy, wname, layer, **kw):
        key, slab, shape = ex.grad(wname, layer)
        gb[key] = mm_tn(name, a, dy, gb.get(key), shape, slab, **kw)

    plain = lambda acc: (acc,)
    plus_col = lambda acc, b: (acc + b,)

    bias_flat = bias_table(small['rel_bias'])
    bias_t = bias_flat.reshape(2, 8, WINDOW, 2 * WINDOW).transpose(0, 3, 1, 2).reshape(2, 2 * WINDOW, 8 * WINDOW)
    sink_rows = [jnp.repeat(small['b_sinks'][j], WINDOW).reshape(2, 1, 8 * WINDOW) for j in range(n_b)]

    def mlp_fwd(h, layer):
        n2 = rms_fwd(f"mlp_norm_fwd{layer}", h, small['mlp_norm'][layer])
        u, a = fwd_mm(f"mlp_up_fwd{layer}", n2, 'mlp_up', layer,
                      lambda acc: (acc, jnp.square(jnp.maximum(acc, 0.0))), (), (BF16, BF16))
        (h2,) = fwd_mm(f"mlp_down_fwd{layer}", a, 'mlp_down', layer, _add, (h,), (F32,))
        return h2, (n2, u, a)

    h = x
    saved = []
    for l in range(n_a):
        n1 = rms_fwd(f"a_norm_fwd{l}", h, small['a_norm'][l])
        (qkvt,) = fwd_mm(f"a_qkv_fwd{l}", n1, 'a_wqkv', l, plain, (), (BF16,), out_t=True)
        qkvt = qkvt.reshape(3 * d // HEAD_DIM, HEAD_DIM, s)
        o_t, rtab, carried = sb_fwd(f"sb_fwd{l}", qkvt, ex.fwd_carry(l))
        ex.fwd_done(l, carried)
        o_t = o_t.reshape(d, s)
        (h_mid,) = fwd_mm(f"a_wo_fwd{l}", o_t, 'a_wo', l, _add, (h,), (F32,), a_t=True)
        h_out, mlp_saved = mlp_fwd(h_mid, l)
        saved.append((h, n1, qkvt, o_t, rtab, h_mid, mlp_saved))
        h = h_out
    h_kv = h
    nkv = rms_fwd("kv_norm_fwd", h, small['kv_norm'])
    (kvt,) = fwd_mm("kv_fwd", nkv, 'w_kv', 0, plus_col, (small['b_kv'].reshape(-1, 1),), (BF16,), out_t=True)
    kvt = kvt.reshape(2, 2, HEAD_DIM, s)
    kpt, vpt = (jnp.pad(t, ((0, 0), (0, 0), (WINDOW, 0))) for t in (kvt[0], kvt[1]))
    for j in range(n_b):
        layer = n_a + j
        n1 = rms_fwd(f"b_norm_fwd{j}", h, small['b_norm'][j])
        (qbt,) = fwd_mm(f"b_q_fwd{j}", n1, 'b_wq', j, plus_col, (small['b_bq'][j].reshape(-1, 1),), (BF16,),
                        out_t=True)
        o_t = swa_fwd(f"swa_fwd{j}", qbt, kpt, vpt, bias_t, sink_rows[j])
        (h_mid,) = fwd_mm(f"b_wo_fwd{j}", o_t, 'b_wo', j, lambda acc, hh, b: (acc + hh + b,),
                          (h, small['b_bo'][j].reshape(1, -1)), (F32,), a_t=True)
        h_out, mlp_saved = mlp_fwd(h_mid, layer)
        saved.append((h, n1, qbt, o_t, h_mid, mlp_saved))
        h = h_out

    dh, dhb, dg_final, loss_b = loss_head(h, small['final_norm'], target)
    sg['final_norm'] = dg_final[0]
    sg['mlp_norm'] = [None] * (n_a + n_b)

    def mlp_bwd(dh, dhb, h_mid, mlp_saved, layer):
        n2, u, a = mlp_saved
        (du,) = dx_mm(f"mlp_down_dx{layer}", dhb, 'mlp_down', layer,
                      lambda acc, uu: (acc * (2.0 * jnp.maximum(uu.astype(F32), 0.0)),), (u,), (BF16,))
        dw_mm(f"mlp_down_dw{layer}", a, dhb, 'mlp_down', layer)
        (dn2,) = dx_mm(f"mlp_up_dx{layer}", du, 'mlp_up', layer, plain, (), (F32,))
        dw_mm(f"mlp_up_dw{layer}", n2, du, 'mlp_up', layer)
        dh2, dh2b, dg, cs = rms_bwd(f"mlp_norm_bwd{layer}", h_mid, small['mlp_norm'][layer], dn2, dh)
        sg['mlp_norm'][layer] = dg[0]
        return dh2, dh2b, cs

    dkp = jnp.zeros(kpt.shape, F32)
    dvp = jnp.zeros(vpt.shape, F32)
    sg['b_norm'], sg['b_bq'], sg['b_bo'], sg['b_sinks'] = [None] * n_b, [None] * n_b, [None] * n_b, [None] * n_b
    dbias = [None] * n_b
    for j in reversed(range(n_b)):
        layer = n_a + j
        h_in, n1, qbt, o_t, h_mid, mlp_saved = saved[layer]
        dh, dhb, cs = mlp_bwd(dh, dhb, h_mid, mlp_saved, layer)
        sg['b_bo'][j] = cs[0]
        (do_t,) = dx_mm(f"b_wo_dx{j}", dhb, 'b_wo', j, plain, (), (BF16,), out_t=True)
        dw_mm(f"b_wo_dw{j}", o_t, dhb, 'b_wo', j, x_t=True)
        dq_t, dkp, dvp, dbias[j], dsink = swa_bwd(f"swa_bwd{j}", qbt, kpt, vpt, bias_t, sink_rows[j], do_t, dkp, dvp)
        sg['b_sinks'][j] = colsum(f"sink_grad{j}", dsink.reshape(16, WINDOW).T)[0]
        sg['b_bq'][j] = rowsum(f"b_bq_grad{j}", dq_t)
        (dn1,) = dx_mm(f"b_q_dx{j}", dq_t, 'b_wq', j, plain, (), (F32,), a_t=True)
        dw_mm(f"b_q_dw{j}", n1, dq_t, 'b_wq', j, dy_t=True)
        dh, dhb, dg, _ = rms_bwd(f"b_norm_bwd{j}", h_in, small['b_norm'][j], dn1, dh)
        sg['b_norm'][j] = dg[0]
    unt = lambda t: t.reshape(2, 2 * WINDOW, 8, WINDOW).transpose(0, 2, 3, 1).reshape(bias_flat.shape)
    sg['rel_bias'] = bias_table_grad(unt(dbias[0]), unt(dbias[1]))[:, :N_BUCKETS].T

    dkv_t = jnp.concatenate([dkp[:, :, WINDOW:], dvp[:, :, WINDOW:]], axis=0).reshape(-1, s)
    sg['b_kv'] = rowsum("b_kv_grad", dkv_t)
    dkvb = dkv_t.astype(BF16)
    (dnkv,) = dx_mm("kv_dx", dkvb, 'w_kv', 0, plain, (), (F32,), a_t=True)
    dw_mm("kv_dw", nkv, dkvb, 'w_kv', 0, dy_t=True)
    dh, dhb, dg, _ = rms_bwd("kv_norm_bwd", h_kv, small['kv_norm'], dnkv, dh)
    sg['kv_norm'] = dg[0]

    sg['a_norm'] = [None] * n_a
    for l in reversed(range(n_a)):
        h_in, n1, qkvt, o_t, rtab, h_mid, mlp_saved = saved[l]
        dh, dhb, _ = mlp_bwd(dh, dhb, h_mid, mlp_saved, l)
        (do_t,) = dx_mm(f"a_wo_dx{l}", dhb, 'a_wo', l, plain, (), (BF16,), out_t=True)
        dw_mm(f"a_wo_dw{l}", o_t, dhb, 'a_wo', l, x_t=True)
        dq_t, dk_t, dv_t, carried = sb_bwd(f"sb_bwd{l}", qkvt, do_t.reshape(d // HEAD_DIM, HEAD_DIM, s), rtab,
                                           ex.bwd_carry(l, gb))
        ex.bwd_done(l, carried)
        dqkv_t = jnp.concatenate([dq_t, dk_t, dv_t], axis=0).reshape(3 * d, s)
        (dn1,) = dx_mm(f"a_qkv_dx{l}", dqkv_t, 'a_wqkv', l, plain, (), (F32,), a_t=True)
        dw_mm(f"a_qkv_dw{l}", n1, dqkv_t, 'a_wqkv', l, dy_t=True)
        dh, dhb, dg, _ = rms_bwd(f"a_norm_bwd{l}", h_in, small['a_norm'][l], dn1, dh)
        sg['a_norm'][l] = dg[0]

    small_grads = {
        'a_norm': jnp.stack(sg['a_norm']), 'kv_norm': sg['kv_norm'], 'b_kv': sg['b_kv'],
        'b_norm': jnp.stack(sg['b_norm']), 'b_bq': jnp.stack(sg['b_bq']), 'b_sinks': jnp.stack(sg['b_sinks']),
        'b_bo': jnp.stack(sg['b_bo']), 'rel_bias': sg['rel_bias'], 'mlp_norm': jnp.stack(sg['mlp_norm']),
        'final_norm': sg['final_norm'],
    }
    return loss_b, dh, gb, small_grads


def _full_shape(name, shard_shape):
    if name in COL_SHARDED:
        return shard_shape[:2] + (N_DEV * shard_shape[2],)
    nl, r, n = shard_shape
    return (nl, N_DEV, r, n)


def _as_w3_shape(name, shard_shape):
    full = _full_shape(name, shard_shape)
    return full if name in COL_SHARDED else (full[0], full[1] * full[2], full[3])


def _as_w3(name, full):
    if name in COL_SHARDED:
        return full
    nl, nd, r, n = full.shape
    return full.reshape(nl, nd * r, n)


AG_GROUPS = {
    0: (('a_wqkv', 0, 1),),
    1: (('a_wo', 0, 2), ('mlp_up', 0, 2), ('mlp_down', 0, 2), ('a_wqkv', 1, 1)),
    2: (('mlp_up', 2, 2), ('mlp_down', 2, 2), ('b_wq', 0, 2), ('b_wo', 0, 2), ('w_kv', 0, 1)),
}
RS_GROUPS = {
    'A': (('mlp_up', 1, 3), ('mlp_down', 1, 3), ('b_wq', 0, 2), ('b_wo', 0, 2), ('w_kv', 0, 1)),
    'B': (('a_wqkv', 1, 1), ('a_wo', 1, 1), ('mlp_up', 0, 1), ('mlp_down', 0, 1)),
    'C': (('a_wqkv', 0, 1), ('a_wo', 0, 1)),
}


class _Exchanges:
    def __init__(self, full0, shards, core, chip, w3, m3, v3):
        self.wbuf = {0: {n: _as_w3(n, full0[n]) for n, _, _ in AG_GROUPS[0]}}
        self.shards, self.core, self.chip = shards, core, chip
        self.w3, self.m3, self.v3 = w3, m3, v3
        self.shard_dims = {n: w3[n].shape[1:] for n in BIG}
        self.parts = {}
        self.out = {}

    def weight(self, name, layer):
        for group, members in AG_GROUPS.items():
            for n, l0, nl in members:
                if n == name and l0 <= layer < l0 + nl:
                    return self.wbuf[group][name], layer - l0
        raise KeyError((name, layer))

    def fwd_carry(self, layer):
        names = [n for n, _, _ in AG_GROUPS[layer + 1]]
        shards = [self.shards[layer + 1][n] for n in names]
        return ag_direct(names, shards, [_full_shape(n, sh.shape) for n, sh in zip(names, shards)])

    def fwd_done(self, layer, carried):
        names = [n for n, _, _ in AG_GROUPS[layer + 1]]
        self.wbuf[layer + 1] = {n: _as_w3(n, f) for n, f in zip(names, ag_forward(layer + 1, names, list(carried)))}

    def grad(self, name, layer):
        for group, members in RS_GROUPS.items():
            for n, l0, nl in members:
                if n == name and l0 <= layer < l0 + nl:
                    return (group, name), layer - l0, _as_w3_shape(name, (nl,) + self.shard_dims[name])
        raise KeyError((name, layer))

    def _sibling_stage(self, group, gb):
        names = [n for n, _, _ in RS_GROUPS[group]]
        shapes = [(nl,) + self.shard_dims[n] for n, _, nl in RS_GROUPS[group]]
        gfull = [gb[(group, n)].reshape(_full_shape(n, sh)) for n, sh in zip(names, shapes)]
        recv = sibling_exchange(group, names, gfull, shapes)
        self.parts[group] = [sibling_sum(f"rs_sibling_sum_{group}_{n}", n in COL_SHARDED, g, r, self.core)
                             for n, g, r in zip(names, gfull, recv)]
        return names, shapes

    def bwd_carry(self, layer, gb):
        names, shapes = self._sibling_stage('A' if layer == 1 else 'B', gb)
        return chip_exchange(names, self.parts['A' if layer == 1 else 'B'], shapes)

    def bwd_done(self, layer, carried):
        self._adamw('A' if layer == 1 else 'B', carried)

    def finish(self, gb):
        names, shapes = self._sibling_stage('C', gb)
        ce = chip_exchange(names, self.parts['C'], shapes)
        self._adamw('C', comm_call("rs_chip_exchange_C", ce.build, ce.ins, ce.out_shapes, ce.sems))
        return self.out

    def _adamw(self, group, recv2):
        for (n, l0, _), p, r in zip(RS_GROUPS[group], self.parts[group], recv2):
            self.out[n] = reduce_adamw(f"adamw_{group}_{n}", p, r, self.chip, self.w3[n], self.m3[n], self.v3[n],
                                       l0, self.out.get(n))


def _pack_small(vals):
    flat = jnp.concatenate([vals[n].reshape(-1).astype(F32) for n in SMALL] + [vals['loss'].reshape(-1)])
    rows = -(-flat.shape[0] // 1024) * 8
    return jnp.pad(flat, (0, rows * 128 - flat.shape[0])).reshape(rows, 128)


def _unpack_small(packed, shapes):
    flat = packed.reshape(-1)
    out, off = {}, 0
    for n in SMALL + ['loss']:
        size = int(np.prod(shapes[n]))
        out[n] = flat[off:off + size].reshape(shapes[n])
        off += size
    return out


def kernel(x, a_norm, a_wqkv, a_wo, kv_norm, w_kv, b_kv, b_norm, b_wq, b_bq, b_sinks, b_wo, b_bo, rel_bias, mlp_norm, mlp_up, mlp_down, final_norm, loss_target, m_a_norm, m_a_wqkv, m_a_wo, m_kv_norm, m_w_kv, m_b_kv, m_b_norm, m_b_wq, m_b_bq, m_b_sinks, m_b_wo, m_b_bo, m_rel_bias, m_mlp_norm, m_mlp_up, m_mlp_down, m_final_norm, v_a_norm, v_a_wqkv, v_a_wo, v_kv_norm, v_w_kv, v_b_kv, v_b_norm, v_b_wq, v_b_bq, v_b_sinks, v_b_wo, v_b_bo, v_rel_bias, v_mlp_norm, v_mlp_up, v_mlp_down, v_final_norm):
    w = dict(a_norm=a_norm, a_wqkv=a_wqkv, a_wo=a_wo, kv_norm=kv_norm, w_kv=w_kv, b_kv=b_kv, b_norm=b_norm,
             b_wq=b_wq, b_bq=b_bq, b_sinks=b_sinks, b_wo=b_wo, b_bo=b_bo, rel_bias=rel_bias, mlp_norm=mlp_norm,
             mlp_up=mlp_up, mlp_down=mlp_down, final_norm=final_norm)
    m = dict(a_norm=m_a_norm, a_wqkv=m_a_wqkv, a_wo=m_a_wo, kv_norm=m_kv_norm, w_kv=m_w_kv, b_kv=m_b_kv,
             b_norm=m_b_norm, b_wq=m_b_wq, b_bq=m_b_bq, b_sinks=m_b_sinks, b_wo=m_b_wo, b_bo=m_b_bo,
             rel_bias=m_rel_bias, mlp_norm=m_mlp_norm, mlp_up=m_mlp_up, mlp_down=m_mlp_down, final_norm=m_final_norm)
    v = dict(a_norm=v_a_norm, a_wqkv=v_a_wqkv, a_wo=v_a_wo, kv_norm=v_kv_norm, w_kv=v_w_kv, b_kv=v_b_kv,
             b_norm=v_b_norm, b_wq=v_b_wq, b_bq=v_b_bq, b_sinks=v_b_sinks, b_wo=v_b_wo, b_bo=v_b_bo,
             rel_bias=v_rel_bias, mlp_norm=v_mlp_norm, mlp_up=v_mlp_up, mlp_down=v_mlp_down, final_norm=v_final_norm)
    px, py, pc = _place()
    me = 4 * px + 2 * py + pc
    chip = (2 * px + py).astype(jnp.int32)
    core = pc.astype(jnp.int32)

    as3 = lambda t: t[None] if t.ndim == 2 else t
    w3, m3, v3 = ({n: as3(src[n]) for n in BIG} for src in (w, m, v))
    shards = {g: {n: w3[n][l0:l0 + nl].astype(BF16) for n, l0, nl in members} for g, members in AG_GROUPS.items()}
    an_pad = jnp.zeros((8, 128), F32).at[:a_norm.shape[0]].set(a_norm)
    names0 = [n for n, _, _ in AG_GROUPS[0]]
    full0 = all_gather_weights(names0 + ['a_norm'], [shards[0][n] for n in names0] + [an_pad],
                               [_full_shape(n, shards[0][n].shape) for n in names0] + [(N_DEV, 8, 128)])
    full0 = dict(zip(names0 + ['a_norm'], full0))
    n_a = a_norm.shape[0]
    small = {n: w[n] for n in SMALL}
    small['a_norm'] = full0['a_norm'][:, :n_a].transpose(1, 0, 2).reshape(n_a, -1)

    ex = _Exchanges(full0, shards, core, chip, w3, m3, v3)
    loss_b, grad_x, gb, sgrads = local_step(x[0], loss_target[0], small, ex)
    out = {n: [t.reshape(w[n].shape) for t in bufs] for n, bufs in ex.finish(gb).items()}

    sgrads['loss'] = loss_b[0, :1]
    gathered = all_gather_rows(_pack_small(sgrads))
    shapes = {n: w[n].shape for n in SMALL}
    shapes['a_norm'] = (n_a, a_norm.shape[1] * N_DEV)
    shapes['loss'] = (1,)
    zeros1 = jnp.zeros((1,), F32)

    def packed(src):
        vals = {n: src[n] for n in SMALL}
        vals['a_norm'] = jnp.zeros(shapes['a_norm'], F32)
        vals['loss'] = zeros1
        return _pack_small(vals)

    sm = small_adamw("adamw_small", gathered, packed(w), packed(m), packed(v))
    sm = [_unpack_small(t, shapes) for t in sm]
    g_an = lax.dynamic_slice_in_dim(sm[0]['a_norm'], me * a_norm.shape[1], a_norm.shape[1], axis=1)
    pad = lambda t: jnp.zeros((8, 128), F32).at[:n_a].set(t)
    gathered_an = jnp.zeros((N_DEV, 8, 128), F32).at[0].set(pad(g_an))
    an = small_adamw("adamw_a_norm", gathered_an, pad(a_norm), pad(m_a_norm), pad(v_a_norm))
    for i in range(4):
        sm[i]['a_norm'] = an[i][:n_a]
    for n in BIG:
        for i in range(4):
            sm[i][n] = out[n][i]
    loss = sm[0]['loss'][0]
    return (loss, grad_x[None], *[sm[0][n] for n in WEIGHTS], *[sm[1][n] for n in WEIGHTS],
            *[sm[2][n] for n in WEIGHTS], *[sm[3][n] for n in WEIGHTS])
```

```python
import functools
import math

import numpy as np
import jax
import jax.numpy as jnp
from jax import lax
from jax.experimental import pallas as pl
from jax.experimental.pallas import tpu as pltpu

F32 = jnp.float32
BF16 = jnp.bfloat16
MESH = pl.DeviceIdType.MESH

N_DEV = 8
HEAD_DIM = 64
WINDOW = 128
N_BUCKETS = 32
EPS = 1e-5
NEG_INF = -1e30
Q_SCALE = 1.0 / math.sqrt(HEAD_DIM)
LOG2E = 1.4426950408889634

ADAM_LR, ADAM_B1, ADAM_B2, ADAM_EPS, ADAM_WD, ADAM_STEP = 0.001, 0.9, 0.999, 1e-08, 0.01, 10

SB_BQ = 512
SB_BK = 128
SB_DEAD = 160.0
SB_UNSEEN = 1e30
ROW_TILE = 512
VMEM_LIMIT = 56 * 1024 * 1024

WEIGHTS = ['a_norm', 'a_wqkv', 'a_wo', 'kv_norm', 'w_kv', 'b_kv', 'b_norm', 'b_wq', 'b_bq', 'b_sinks', 'b_wo',
           'b_bo', 'rel_bias', 'mlp_norm', 'mlp_up', 'mlp_down', 'final_norm']
BIG = ['a_wqkv', 'a_wo', 'w_kv', 'b_wq', 'b_wo', 'mlp_up', 'mlp_down']
COL_SHARDED = ('a_wqkv', 'mlp_up')
SMALL = ['a_norm', 'kv_norm', 'b_kv', 'b_norm', 'b_bq', 'b_sinks', 'b_bo', 'rel_bias', 'mlp_norm', 'final_norm']


def _params(sem=None):
    return pltpu.CompilerParams(dimension_semantics=sem, vmem_limit_bytes=VMEM_LIMIT)


def _pick(n, cands):
    for c in cands:
        if n % c == 0:
            return c
    raise ValueError(n)


def _tile(n, want):
    return n if n <= want else _pick(n, (want, want // 2, want // 4))


def mm_nn(name, a, w3, layer, epilogue, extras, out_dtypes, a_t=False, out_t=False):
    k, m = a.shape if a_t else a.shape[::-1]
    _, kw, n = w3.shape
    assert kw == k
    tm = _tile(m, 1024 if k <= 1024 else 512)
    tn = _tile(n, 1024)
    ne, no = len(extras), len(out_dtypes)
    a_dim = 0 if a_t else 1

    def body(a_ref, w_ref, *rest):
        ex, outs = rest[:ne], rest[ne:ne + no]
        if out_t:
            acc = lax.dot_general(w_ref[...], a_ref[...], (((0,), (a_dim,)), ((), ())), preferred_element_type=F32)
        else:
            acc = lax.dot_general(a_ref[...], w_ref[...], (((a_dim,), (0,)), ((), ())), preferred_element_type=F32)
        for o, r in zip(outs, epilogue(acc, *[e[...] for e in ex])):
            o[...] = r.astype(o.dtype)

    if out_t:
        tile = pl.BlockSpec((tn, tm), lambda i, j: (j, i))
        vec = pl.BlockSpec((tn, 1), lambda i, j: (j, 0))
        out_shape = (n, m)
    else:
        tile = pl.BlockSpec((tm, tn), lambda i, j: (i, j))
        vec = pl.BlockSpec((1, tn), lambda i, j: (0, j))
        out_shape = (m, n)
    a_spec = pl.BlockSpec((k, tm), lambda i, j: (0, i)) if a_t else pl.BlockSpec((tm, k), lambda i, j: (i, 0))
    return pl.pallas_call(
        body, name=name, grid=(m // tm, n // tn),
        in_specs=[a_spec, pl.BlockSpec((None, k, tn), lambda i, j: (layer, 0, j))]
        + [tile if e.shape == out_shape else vec for e in extras],
        out_specs=[tile] * no,
        out_shape=[jax.ShapeDtypeStruct(out_shape, d) for d in out_dtypes],
        compiler_params=_params(("parallel", "parallel")),
    )(a, w3, *extras)


def mm_nt(name, dy, w3, layer, epilogue, extras, out_dtypes, a_t=False, out_t=False):
    n, m = dy.shape if a_t else dy.shape[::-1]
    _, k, nw = w3.shape
    assert nw == n
    tm = _tile(m, 1024 if n <= 1024 else 512)
    tko = _tile(k, 1024)
    ne, no = len(extras), len(out_dtypes)
    a_dim = 0 if a_t else 1

    def body(a_ref, w_ref, *rest):
        ex, outs = rest[:ne], rest[ne:ne + no]
        if out_t:
            acc = lax.dot_general(w_ref[...], a_ref[...], (((1,), (a_dim,)), ((), ())), preferred_element_type=F32)
        else:
            acc = lax.dot_general(a_ref[...], w_ref[...], (((a_dim,), (1,)), ((), ())), preferred_element_type=F32)
        for o, v in zip(outs, epilogue(acc, *[e[...] for e in ex])):
            o[...] = v.astype(o.dtype)

    if out_t:
        tile = pl.BlockSpec((tko, tm), lambda i, ko: (ko, i))
        out_shape = (k, m)
    else:
        tile = pl.BlockSpec((tm, tko), lambda i, ko: (i, ko))
        out_shape = (m, k)
    a_spec = pl.BlockSpec((n, tm), lambda i, ko: (0, i)) if a_t else pl.BlockSpec((tm, n), lambda i, ko: (i, 0))
    return pl.pallas_call(
        body, name=name, grid=(m // tm, k // tko),
        in_specs=[a_spec, pl.BlockSpec((None, tko, n), lambda i, ko: (layer, ko, 0))] + [tile] * ne,
        out_specs=[tile] * no,
        out_shape=[jax.ShapeDtypeStruct(out_shape, d) for d in out_dtypes],
        compiler_params=_params(("parallel", "parallel")),
    )(dy, w3, *extras)


def mm_tn(name, x, dy, gbuf, shape, layer, x_t=False, dy_t=False):
    k, s = x.shape if x_t else x.shape[::-1]
    _, kw, n = shape
    assert kw == k and dy.shape == ((n, s) if dy_t else (s, n))
    tkk = _tile(k, 512)
    tn = _tile(n, 1024)

    def body(x_ref, dy_ref, *rest):
        g_out = rest[-1]
        g_out[...] = lax.dot_general(x_ref[...], dy_ref[...], (((1 if x_t else 0,), (1 if dy_t else 0,)), ((), ())),
                                     preferred_element_type=F32).astype(g_out.dtype)

    prev = [] if gbuf is None else [gbuf]
    x_spec = pl.BlockSpec((tkk, s), lambda ki, j: (ki, 0)) if x_t else pl.BlockSpec((s, tkk), lambda ki, j: (0, ki))
    dy_spec = pl.BlockSpec((tn, s), lambda ki, j: (j, 0)) if dy_t else pl.BlockSpec((s, tn), lambda ki, j: (0, j))
    return pl.pallas_call(
        body, name=name, grid=(k // tkk, n // tn),
        in_specs=[x_spec, dy_spec] + [pl.BlockSpec(memory_space=pl.ANY)] * len(prev),
        out_specs=pl.BlockSpec((None, tkk, tn), lambda ki, j: (layer, ki, j)),
        out_shape=jax.ShapeDtypeStruct(shape, BF16),
        input_output_aliases={2: 0} if prev else {},
        compiler_params=_params(("parallel", "parallel")),
    )(x, dy, *prev)


def rms_fwd(name, h, g):
    s, d = h.shape
    tr = _pick(s, (ROW_TILE, 256, 128))

    def body(h_ref, g_ref, o_ref):
        x = h_ref[...]
        r = lax.rsqrt(jnp.mean(x * x, axis=-1, keepdims=True) + EPS)
        o_ref[...] = (x * r * g_ref[...]).astype(o_ref.dtype)

    return pl.pallas_call(
        body, name=name, grid=(s // tr,),
        in_specs=[pl.BlockSpec((tr, d), lambda i: (i, 0)), pl.BlockSpec((1, d), lambda i: (0, 0))],
        out_specs=pl.BlockSpec((tr, d), lambda i: (i, 0)),
        out_shape=jax.ShapeDtypeStruct((s, d), BF16),
        compiler_params=_params(("parallel",)),
    )(h, g.reshape(1, d))


def rms_bwd(name, h, g, dn, dres):
    s, d = h.shape
    tr = _pick(s, (ROW_TILE, 256, 128))

    def body(h_ref, g_ref, dn_ref, dres_ref, dx_ref, dxb_ref, dg_ref, cs_ref):
        i = pl.program_id(0)
        x = h_ref[...]
        r = lax.rsqrt(jnp.mean(x * x, axis=-1, keepdims=True) + EPS)
        xh = x * r
        dn_ = dn_ref[...]
        dyg = dn_ * g_ref[...]
        dx = dres_ref[...] + r * (dyg - xh * jnp.mean(dyg * xh, axis=-1, keepdims=True))
        dx_ref[...] = dx
        dxb_ref[...] = dx.astype(BF16)

        @pl.when(i == 0)
        def _():
            dg_ref[...] = jnp.zeros_like(dg_ref)
            cs_ref[...] = jnp.zeros_like(cs_ref)

        dg_ref[...] += jnp.sum(dn_ * xh, axis=0, keepdims=True)
        cs_ref[...] += jnp.sum(dx, axis=0, keepdims=True)

    row = pl.BlockSpec((tr, d), lambda i: (i, 0))
    vec = pl.BlockSpec((1, d), lambda i: (0, 0))
    return pl.pallas_call(
        body, name=name, grid=(s // tr,),
        in_specs=[row, vec, row, row],
        out_specs=[row, row, vec, vec],
        out_shape=[jax.ShapeDtypeStruct((s, d), F32), jax.ShapeDtypeStruct((s, d), BF16),
                   jax.ShapeDtypeStruct((1, d), F32), jax.ShapeDtypeStruct((1, d), F32)],
        compiler_params=_params(("arbitrary",)),
    )(h, g.reshape(1, d), dn, dres)


def loss_head(h, g, target):
    s, d = h.shape
    tr = _pick(s, (ROW_TILE, 256, 128))

    def body(h_ref, g_ref, t_ref, dx_ref, dxb_ref, dg_ref, loss_ref):
        i = pl.program_id(0)
        x = h_ref[...]
        r = lax.rsqrt(jnp.mean(x * x, axis=-1, keepdims=True) + EPS)
        xh = x * r
        gw = g_ref[...]
        err = xh * gw - t_ref[...]
        dn_ = err * (1.0 / d)
        dyg = dn_ * gw
        dx = r * (dyg - xh * jnp.mean(dyg * xh, axis=-1, keepdims=True))
        dx_ref[...] = dx
        dxb_ref[...] = dx.astype(BF16)

        @pl.when(i == 0)
        def _():
            dg_ref[...] = jnp.zeros_like(dg_ref)
            loss_ref[...] = jnp.zeros_like(loss_ref)

        dg_ref[...] += jnp.sum(dn_ * xh, axis=0, keepdims=True)
        per_row = jnp.sum(err * err, axis=-1, keepdims=True) * (0.5 / d)
        loss_ref[...] += jnp.broadcast_to(jnp.sum(per_row, axis=0, keepdims=True), loss_ref.shape)

    row = pl.BlockSpec((tr, d), lambda i: (i, 0))
    vec = pl.BlockSpec((1, d), lambda i: (0, 0))
    return pl.pallas_call(
        body, name="loss_head", grid=(s // tr,),
        in_specs=[row, vec, row],
        out_specs=[row, row, vec, pl.BlockSpec((1, 128), lambda i: (0, 0))],
        out_shape=[jax.ShapeDtypeStruct((s, d), F32), jax.ShapeDtypeStruct((s, d), BF16),
                   jax.ShapeDtypeStruct((1, d), F32), jax.ShapeDtypeStruct((1, 128), F32)],
        compiler_params=_params(("arbitrary",)),
    )(h, g.reshape(1, d), target)


def colsum(name, x):
    s, n = x.shape
    tr = _pick(s, (ROW_TILE, 256, 128))

    def body(x_ref, o_ref):
        @pl.when(pl.program_id(0) == 0)
        def _():
            o_ref[...] = jnp.zeros_like(o_ref)

        o_ref[...] += jnp.sum(x_ref[...].astype(F32), axis=0, keepdims=True)

    return pl.pallas_call(
        body, name=name, grid=(s // tr,),
        in_specs=[pl.BlockSpec((tr, n), lambda i: (i, 0))],
        out_specs=pl.BlockSpec((1, n), lambda i: (0, 0)),
        out_shape=jax.ShapeDtypeStruct((1, n), F32),
        compiler_params=_params(("arbitrary",)),
    )(x)


def rowsum(name, x):
    n, s = x.shape
    ts = _pick(s, (1024, 512, 256, 128))

    def body(x_ref, o_ref):
        @pl.when(pl.program_id(0) == 0)
        def _():
            o_ref[...] = jnp.zeros_like(o_ref)

        o_ref[...] += jnp.sum(x_ref[...].astype(F32), axis=1, keepdims=True)

    return pl.pallas_call(
        body, name=name, grid=(s // ts,),
        in_specs=[pl.BlockSpec((n, ts), lambda i: (0, i))],
        out_specs=pl.BlockSpec((n, 1), lambda i: (0, 0)),
        out_shape=jax.ShapeDtypeStruct((n, 1), F32),
        compiler_params=_params(("arbitrary",)),
    )(x)[:, 0]


def _tri_rows(reverse):
    i = np.arange(SB_BK)
    tri = (i[None, :] >= i[:, None]) if reverse else (i[None, :] <= i[:, None])
    tri = np.concatenate([tri, tri], axis=1)
    return jnp.asarray(np.concatenate([tri, np.ones((8, 2 * SB_BK), bool)], axis=0), BF16)


def _hi_lo_rows(x):
    hi = x.astype(BF16)
    lo = (x - hi.astype(F32)).astype(BF16)
    return jnp.concatenate([hi, lo], axis=0)


def _softplus2(zs):
    neg_abs = lax.bitcast_convert_type(lax.bitcast_convert_type(zs, jnp.uint32) | jnp.uint32(0x80000000), F32)
    return jnp.maximum(zs, 0.0) + jnp.log2(1.0 + jnp.exp2(neg_abs))


def _pair_mask(first_rel_block, bq):
    key = lax.broadcasted_iota(jnp.int32, (2 * SB_BK, bq), 0) + first_rel_block * SB_BK
    qry = lax.broadcasted_iota(jnp.int32, (2 * SB_BK, bq), 1)
    return key < qry


def _row_of(table8, sub8, r):
    return jnp.sum(jnp.where(sub8 == r, table8, 0.0), axis=0, keepdims=True)


def _keys(j0):
    return pl.ds(pl.multiple_of(j0 * SB_BK, 2 * SB_BK), 2 * SB_BK)


class Carry:
    def __init__(self, build, ins, out_shapes, sems):
        self.build, self.ins, self.out_shapes, self.sems = build, list(ins), list(out_shapes), list(sems)


def _carried(carry, rest, n_out, n_scratch, first, last):
    n_ci = len(carry.ins) if carry else 0
    n_co = len(carry.out_shapes) if carry else 0
    cin, outs = rest[:n_ci], rest[n_ci:n_ci + n_out]
    cout = rest[n_ci + n_out:n_ci + n_out + n_co]
    scratch = rest[n_ci + n_out + n_co:n_ci + n_out + n_co + n_scratch]
    csems = rest[n_ci + n_out + n_co + n_scratch:]

    def start():
        if carry:
            @pl.when(first)
            def _():
                for cp in carry.build(cin, cout, *csems):
                    cp.start()

    def wait():
        if carry:
            @pl.when(last)
            def _():
                for cp in carry.build(cin, cout, *csems):
                    cp.wait()

    return outs, scratch, start, wait


def _contract0(a, b):
    return lax.dot_general(a, b, (((0,), (0,)), ((), ())), preferred_element_type=F32)


def _contract1(a, b):
    return lax.dot_general(a, b, (((1,), (1,)), ((), ())), preferred_element_type=F32)


def sb_fwd(name, qkvt, exchange=None):
    nh, dh, s = qkvt.shape[0] // 3, qkvt.shape[1], qkvt.shape[2]
    bq = SB_BQ
    per_q = bq // SB_BK
    nkb = s // SB_BK
    assert s % bq == 0 and per_q == 4 and nkb % 8 == 0

    def body(q_ref, k_ref, v_ref, a_ref, *rest):
        head = pl.program_id(0)
        (o_ref, rtab_ref), (acc, zbuf, wbuf), start_carried, wait_carried = _carried(
            exchange, rest, 2, 3, head == 0, head == nh - 1)
        start_carried()
        tri = a_ref[...]
        sub8 = lax.broadcasted_iota(jnp.int32, (8, bq), 0)
        rtab_ref[...] = jnp.full(rtab_ref.shape, SB_UNSEEN, F32)
        kf = k_ref[...].astype(F32)
        k_max2 = jnp.max(jnp.sum(kf * kf, axis=0, keepdims=True), axis=1, keepdims=True)

        def query_block(i, _):
            lanes = pl.ds(pl.multiple_of(i * bq, bq), bq)
            qb = q_ref[:, lanes] * Q_SCALE
            acc[...] = jnp.zeros_like(acc)
            qf = qb.astype(F32)
            bound = jnp.sqrt(jnp.sum(qf * qf, axis=0, keepdims=True) * k_max2) * (1.001 * LOG2E)

            def scores(j0):
                return _contract0(k_ref[:, _keys(j0)], qb) * LOG2E

            def pair(j0, slot, run, rt8, mask, has_prev):
                zs = zbuf[slot]
                zbuf[1 - slot] = scores(jnp.maximum(j0 - 2, 0))
                if has_prev:
                    acc[...] += jnp.dot(v_ref[:, _keys(j0 + 2)], wbuf[1 - slot], preferred_element_type=F32)
                p = _softplus2(zs)
                if mask is not None:
                    p = jnp.where(mask, p, 0.0)
                cr1 = jnp.dot(tri, _hi_lo_rows(p[SB_BK:]), preferred_element_type=F32)
                cr0 = jnp.dot(tri, _hi_lo_rows(p[:SB_BK]), preferred_element_type=F32)
                run1 = run + cr1[SB_BK:SB_BK + 1]
                w = jnp.exp2(jnp.concatenate([zs[:SB_BK] - cr0[:SB_BK] - run1, zs[SB_BK:] - cr1[:SB_BK] - run],
                                             axis=0))
                if mask is not None:
                    w = jnp.where(mask, w, 0.0)
                wbuf[slot] = w.astype(BF16)
                rt8 = jnp.where(j0 % 8 == 6, SB_UNSEEN, rt8)
                rt8 = jnp.where(sub8 == (j0 + 1) % 8, run, jnp.where(sub8 == j0 % 8, run1, rt8))
                rtab_ref[pl.ds(pl.multiple_of((j0 // 8) * 8, 8), 8), lanes] = rt8
                return run1 + cr0[SB_BK:SB_BK + 1], rt8

            def alive(run):
                return jnp.min(run - bound) < SB_DEAD

            top = i * per_q
            zbuf[0] = scores(top + 2)
            state = (jnp.zeros((1, bq), F32), jnp.full((8, bq), SB_UNSEEN, F32))
            state = pair(top + 2, 0, *state, _pair_mask(2, bq), False)
            state = pair(top, 1, *state, _pair_mask(0, bq), True)

            def step(c):
                it, pairs, _, run, rt8 = c
                j0 = top - 2 - 4 * it
                run, rt8 = pair(j0, 0, run, rt8, None, True)
                go = alive(run)
                run, rt8 = lax.cond(go, lambda r, t: pair(j0 - 2, 1, r, t, None, True), lambda r, t: (r, t), run, rt8)
                return it + 1, pairs + 1 + go.astype(jnp.int32), go & alive(run), run, rt8

            pairs = lax.while_loop(lambda c: (c[0] < i) & c[2], step, (0, 0, alive(state[0]), *state))[1]
            acc[...] += jnp.dot(v_ref[:, _keys(top - 2 * pairs)], wbuf[(pairs + 1) % 2], preferred_element_type=F32)
            o_ref[:, lanes] = acc[...].astype(o_ref.dtype)
            return 0

        lax.fori_loop(0, s // bq, query_block, 0)
        wait_carried()

    def head_spec(offset, rows):
        return pl.BlockSpec((None, rows, s), lambda h: (h + offset, 0, 0))

    hbm = pl.BlockSpec(memory_space=pl.ANY)
    c_ins, c_outs, c_sems = (exchange.ins, exchange.out_shapes, exchange.sems) if exchange else ([], [], [])
    outs = pl.pallas_call(
        body, name=name, grid=(nh,),
        in_specs=[head_spec(0, dh), head_spec(nh, dh), head_spec(2 * nh, dh),
                  pl.BlockSpec((SB_BK + 8, 2 * SB_BK), lambda h: (0, 0))] + [hbm] * len(c_ins),
        out_specs=[head_spec(0, dh), head_spec(0, nkb)] + [hbm] * len(c_outs),
        out_shape=[jax.ShapeDtypeStruct((nh, dh, s), BF16), jax.ShapeDtypeStruct((nh, nkb, s), F32)] + c_outs,
        scratch_shapes=[pltpu.VMEM((dh, bq), F32), pltpu.VMEM((2, 2 * SB_BK, bq), F32),
                        pltpu.VMEM((2, 2 * SB_BK, bq), BF16)] + c_sems,
        compiler_params=_params(("arbitrary",)),
    )(qkvt, qkvt, qkvt, _tri_rows(True), *c_ins)
    return outs[0], outs[1], outs[2:]


def sb_bwd(name, qkvt, dot_, rtab, exchange=None):
    nh, dh, s = qkvt.shape[0] // 3, qkvt.shape[1], qkvt.shape[2]
    bq = SB_BQ
    per_q = bq // SB_BK
    nkb = s // SB_BK

    def body(qt_ref, kt_ref, vt_ref, dot_ref, rtab_ref, ar_ref, af_ref, *rest):
        head = pl.program_id(0)
        (dq_ref, dk_ref, dv_ref), (dq_acc, dk_acc, dv_acc, zbuf, dwbuf, dzbuf, wbuf), start_carried, wait_carried = \
            _carried(exchange, rest, 3, 7, head == 0, head == nh - 1)
        start_carried()
        dk_acc[...] = jnp.zeros_like(dk_acc)
        dv_acc[...] = jnp.zeros_like(dv_acc)
        tri_rev = ar_ref[...][:SB_BK]
        tri_fwd = af_ref[...]
        sub8 = lax.broadcasted_iota(jnp.int32, (8, bq), 0)

        def query_block(i, _):
            lanes = pl.ds(pl.multiple_of(i * bq, bq), bq)
            qtb = qt_ref[:, lanes] * Q_SCALE
            dotb = dot_ref[:, lanes]
            dq_acc[...] = jnp.zeros_like(dq_acc)
            last_j = i * per_q + 2
            seen = jnp.max(jnp.where(rtab_ref[:, lanes] < 0.1 * SB_UNSEEN, 1.0, 0.0), axis=1, keepdims=True)
            pairs = jnp.clip((jnp.sum(seen).astype(jnp.int32) - per_q) // 2, 0, 2 * i)
            odd = pairs % 2
            first_j = i * per_q - 2 * pairs

            def issue(j0, slot):
                zbuf[slot] = _contract0(kt_ref[:, _keys(j0)], qtb) * LOG2E
                dwbuf[slot] = _contract0(vt_ref[:, _keys(j0)], dotb)

            def retire(j0, slot):
                keys = _keys(j0)
                dq_acc[...] += jnp.dot(kt_ref[:, keys], dzbuf[slot], preferred_element_type=F32)
                dk_acc[:, keys] += _contract1(qtb, dzbuf[slot])
                dv_acc[:, keys] += _contract1(dotb, wbuf[slot])

            def pair(j0, slot, g_run, mask):
                zs = zbuf[slot]
                dw = dwbuf[slot]
                issue(jnp.minimum(j0 + 2, last_j), 1 - slot)
                retire(jnp.maximum(j0 - 2, first_j), 1 - slot)
                p_raw = _softplus2(zs)
                p = p_raw if mask is None else jnp.where(mask, p_raw, 0.0)
                c0 = jnp.dot(tri_rev, _hi_lo_rows(p[:SB_BK]), preferred_element_type=F32)
                c1 = jnp.dot(tri_rev, _hi_lo_rows(p[SB_BK:]), preferred_element_type=F32)
                rt8 = rtab_ref[pl.ds(pl.multiple_of((j0 // 8) * 8, 8), 8), lanes]
                r0 = _row_of(rt8, sub8, j0 % 8)
                r1 = _row_of(rt8, sub8, (j0 + 1) % 8)
                w = jnp.exp2(jnp.concatenate([zs[:SB_BK] - c0 - r0, zs[SB_BK:] - c1 - r1], axis=0))
                if mask is not None:
                    w = jnp.where(mask, w, 0.0)
                g = w * dw
                gg0 = jnp.dot(tri_fwd, _hi_lo_rows(g[:SB_BK]), preferred_element_type=F32)
                gg1 = jnp.dot(tri_fwd, _hi_lo_rows(g[SB_BK:]), preferred_element_type=F32)
                g_run1 = g_run + gg0[SB_BK:SB_BK + 1]
                g_pre = jnp.concatenate([gg0[:SB_BK] + g_run, gg1[:SB_BK] + g_run1], axis=0)
                dz = g - jnp.exp2(zs - p_raw) * g_pre
                if mask is not None:
                    dz = jnp.where(mask, dz, 0.0)
                dzbuf[slot] = dz.astype(BF16)
                wbuf[slot] = w.astype(BF16)
                return g_run1 + gg1[SB_BK:SB_BK + 1]

            issue(first_j, odd)
            dzbuf[...] = jnp.zeros(dzbuf.shape, BF16)
            wbuf[...] = jnp.zeros(wbuf.shape, BF16)

            def step(it, g_run):
                g_run = pair(4 * it, 0, g_run, None)
                return pair(4 * it + 2, 1, g_run, None)

            g_run = lax.cond(odd == 1, lambda g: pair(first_j, 1, g, None), lambda g: g, jnp.zeros((1, bq), F32))
            g_run = lax.fori_loop(i - pairs // 2, i, step, g_run)
            g_run = pair(last_j - 2, 0, g_run, _pair_mask(0, bq))
            pair(last_j, 1, g_run, _pair_mask(2, bq))
            retire(last_j, 1)
            dq_ref[:, lanes] = (dq_acc[...] * Q_SCALE).astype(dq_ref.dtype)
            return 0

        lax.fori_loop(0, s // bq, query_block, 0)
        dk_ref[...] = dk_acc[...].astype(dk_ref.dtype)
        dv_ref[...] = dv_acc[...].astype(dv_ref.dtype)
        wait_carried()

    def head_spec(offset, rows):
        return pl.BlockSpec((None, rows, s), lambda h: (h + offset, 0, 0))

    aspec = pl.BlockSpec((SB_BK + 8, 2 * SB_BK), lambda h: (0, 0))
    pair_f32 = pltpu.VMEM((2, 2 * SB_BK, bq), F32)
    pair_bf16 = pltpu.VMEM((2, 2 * SB_BK, bq), BF16)
    hbm = pl.BlockSpec(memory_space=pl.ANY)
    c_ins, c_outs, c_sems = (exchange.ins, exchange.out_shapes, exchange.sems) if exchange else ([], [], [])
    outs = pl.pallas_call(
        body, name=name, grid=(nh,),
        in_specs=[head_spec(0, dh), head_spec(nh, dh), head_spec(2 * nh, dh), head_spec(0, dh), head_spec(0, nkb),
                  aspec, aspec] + [hbm] * len(c_ins),
        out_specs=[head_spec(0, dh)] * 3 + [hbm] * len(c_outs),
        out_shape=[jax.ShapeDtypeStruct((nh, dh, s), BF16)] * 3 + c_outs,
        scratch_shapes=[pltpu.VMEM((dh, bq), F32), pltpu.VMEM((dh, s), F32), pltpu.VMEM((dh, s), F32),
                        pair_f32, pair_f32, pair_bf16, pair_bf16] + c_sems,
        compiler_params=_params(("arbitrary",)),
    )(qkvt, qkvt, qkvt, dot_, rtab, _tri_rows(True), _tri_rows(False), *c_ins)
    return outs[0], outs[1], outs[2], outs[3:]


SWA_QB = 2


def _swa_probs(qt, kt, bias_t, sink, i):
    cols = qt.shape[1]
    sc = _contract0(kt, qt) + bias_t
    kj = lax.broadcasted_iota(jnp.int32, (2 * WINDOW, cols), 0)
    qi = lax.broadcasted_iota(jnp.int32, (2 * WINDOW, cols), 1) & (WINDOW - 1)
    dist = qi + WINDOW - kj
    valid = (dist >= 0) & (dist < WINDOW) & ((kj >= WINDOW) | (i > 0))
    sc = jnp.where(valid, sc, NEG_INF)
    mx = jnp.maximum(jnp.max(sc, axis=0, keepdims=True), sink)
    p = jnp.exp(sc - mx)
    p_sink = jnp.exp(sink - mx)
    inv = 1.0 / (jnp.sum(p, axis=0, keepdims=True) + p_sink)
    return p, p_sink, inv


def _band(i):
    return pl.ds(pl.multiple_of(i * WINDOW, WINDOW), 2 * WINDOW)


def _heads_to_lanes(blk):
    return jnp.concatenate([blk[r * HEAD_DIM:(r + 1) * HEAD_DIM] for r in range(8)], axis=1)


def _lanes_to_heads(t):
    return jnp.concatenate([t[:, r * WINDOW:(r + 1) * WINDOW] for r in range(8)], axis=0)


def swa_fwd(name, qt, kpt, vpt, bias_t, sink_row):
    d, s = qt.shape
    ng, dh, sp = kpt.shape
    rows, cols = d // ng, SWA_QB * WINDOW
    assert (s // WINDOW) % SWA_QB == 0

    def body(q_ref, k_ref, v_ref, bias_ref, sink_ref, o_ref):
        for u in range(SWA_QB):
            i = pl.program_id(1) * SWA_QB + u
            lanes = slice(u * WINDOW, (u + 1) * WINDOW)
            qb = _heads_to_lanes(q_ref[:, lanes]) * Q_SCALE
            p, _, inv = _swa_probs(qb, k_ref[:, _band(i)], bias_ref[...], sink_ref[...], i)
            o_t = jnp.dot(v_ref[:, _band(i)], p.astype(BF16), preferred_element_type=F32) * inv
            o_ref[:, lanes] = _lanes_to_heads(o_t).astype(o_ref.dtype)

    qspec = pl.BlockSpec((rows, cols), lambda g, i: (g, i))
    kspec = pl.BlockSpec((None, dh, sp), lambda g, i: (g, 0, 0))
    return pl.pallas_call(
        body, name=name, grid=(ng, s // cols),
        in_specs=[qspec, kspec, kspec, pl.BlockSpec((None, 2 * WINDOW, 8 * WINDOW), lambda g, i: (g, 0, 0)),
                  pl.BlockSpec((None, 1, 8 * WINDOW), lambda g, i: (g, 0, 0))],
        out_specs=qspec,
        out_shape=jax.ShapeDtypeStruct(qt.shape, BF16),
        compiler_params=_params(("parallel", "arbitrary")),
    )(qt, kpt, vpt, bias_t, sink_row)


def swa_bwd(name, qt, kpt, vpt, bias_t, sink_row, dot_, dk_in, dv_in):
    d, s = qt.shape
    ng, dh, sp = kpt.shape
    rows, cols = d // ng, SWA_QB * WINDOW

    def body(q_ref, k_ref, v_ref, bias_ref, sink_ref, do_ref, dki_ref, dvi_ref, dq_ref, dk_ref, dv_ref, db_ref, ds_ref):
        @pl.when(pl.program_id(1) == 0)
        def _():
            dk_ref[...] = dki_ref[...]
            dv_ref[...] = dvi_ref[...]
            db_ref[...] = jnp.zeros_like(db_ref)
            ds_ref[...] = jnp.zeros_like(ds_ref)

        for u in range(SWA_QB):
            i = pl.program_id(1) * SWA_QB + u
            band = _band(i)
            lanes = slice(u * WINDOW, (u + 1) * WINDOW)
            qb = _heads_to_lanes(q_ref[:, lanes]) * Q_SCALE
            dob = _heads_to_lanes(do_ref[:, lanes])
            kt = k_ref[:, band]
            p, p_sink, inv = _swa_probs(qb, kt, bias_ref[...], sink_ref[...], i)
            p = p * inv
            dp = _contract0(v_ref[:, band], dob)
            delta = jnp.sum(p * dp, axis=0, keepdims=True)
            dsc = p * (dp - delta)
            ds_ref[...] -= p_sink * inv * delta
            db_ref[...] += dsc
            dscb = dsc.astype(BF16)
            dq_t = jnp.dot(kt, dscb, preferred_element_type=F32) * Q_SCALE
            dq_ref[:, lanes] = _lanes_to_heads(dq_t).astype(dq_ref.dtype)
            dk_ref[:, band] += _contract1(qb, dscb)
            dv_ref[:, band] += _contract1(dob, p.astype(BF16))

    qspec = pl.BlockSpec((rows, cols), lambda g, i: (g, i))
    kspec = pl.BlockSpec((None, dh, sp), lambda g, i: (g, 0, 0))
    bspec = pl.BlockSpec((None, 2 * WINDOW, 8 * WINDOW), lambda g, i: (g, 0, 0))
    sspec = pl.BlockSpec((None, 1, 8 * WINDOW), lambda g, i: (g, 0, 0))
    return pl.pallas_call(
        body, name=name, grid=(ng, s // cols),
        in_specs=[qspec, kspec, kspec, bspec, sspec, qspec, kspec, kspec],
        out_specs=[qspec, kspec, kspec, bspec, sspec],
        out_shape=[jax.ShapeDtypeStruct(qt.shape, BF16), jax.ShapeDtypeStruct(kpt.shape, F32),
                   jax.ShapeDtypeStruct(kpt.shape, F32), jax.ShapeDtypeStruct(bias_t.shape, F32),
                   jax.ShapeDtypeStruct(sink_row.shape, F32)],
        compiler_params=_params(("parallel", "arbitrary")),
    )(qt, kpt, vpt, bias_t, sink_row, dot_, dk_in, dv_in)


def _bucket_onehot():
    qi = np.arange(WINDOW)[:, None]
    kj = np.arange(2 * WINDOW)[None, :]
    n = np.maximum(qi + WINDOW - kj, 0)
    max_exact = N_BUCKETS // 2
    nf = np.maximum(n, 1).astype(np.float64)
    val = np.log(nf / max_exact) / math.log(WINDOW / max_exact) * (N_BUCKETS - max_exact)
    assert np.all(np.abs(val - np.round(val))[(n > max_exact) & (n < WINDOW)] > 1e-3)
    large = np.minimum(max_exact + val.astype(np.int64), N_BUCKETS - 1)
    bucket = np.where(n < max_exact, n, large).reshape(-1)
    onehot = np.zeros((128, bucket.size), np.float32)
    onehot[bucket, np.arange(bucket.size)] = 1.0
    return onehot


def _split3(x):
    a = x.astype(BF16)
    r = x - a.astype(F32)
    b = r.astype(BF16)
    c = (r - b.astype(F32)).astype(BF16)
    return a, b, c


def bias_table(rel_bias):
    nh = rel_bias.shape[1]
    oh = jnp.asarray(_bucket_onehot(), BF16)
    n = oh.shape[1]
    tn = 4096
    rb = jnp.zeros((nh, 128), F32).at[:, :N_BUCKETS].set(rel_bias.T)

    def body(rb_ref, oh_ref, o_ref):
        o_ref[...] = sum(jnp.dot(t, oh_ref[...], preferred_element_type=F32) for t in _split3(rb_ref[...]))

    return pl.pallas_call(
        body, name="bias_table", grid=(n // tn,),
        in_specs=[pl.BlockSpec((nh, 128), lambda i: (0, 0)), pl.BlockSpec((128, tn), lambda i: (0, i))],
        out_specs=pl.BlockSpec((nh, tn), lambda i: (0, i)),
        out_shape=jax.ShapeDtypeStruct((nh, n), F32),
        compiler_params=_params(("parallel",)),
    )(rb, oh)


def bias_table_grad(db0, db1):
    nh, n = db0.shape
    oh = jnp.asarray(_bucket_onehot(), BF16)
    tn = 4096

    def body(a_ref, b_ref, oh_ref, o_ref):
        @pl.when(pl.program_id(0) == 0)
        def _():
            o_ref[...] = jnp.zeros_like(o_ref)

        o_ref[...] += sum(lax.dot_general(t, oh_ref[...], (((1,), (1,)), ((), ())), preferred_element_type=F32)
                          for t in _split3(a_ref[...] + b_ref[...]))

    blk = pl.BlockSpec((nh, tn), lambda i: (0, i))
    return pl.pallas_call(
        body, name="bias_table_grad", grid=(n // tn,),
        in_specs=[blk, blk, pl.BlockSpec((128, tn), lambda i: (0, i))],
        out_specs=pl.BlockSpec((nh, 128), lambda i: (0, 0)),
        out_shape=jax.ShapeDtypeStruct((nh, 128), F32),
        compiler_params=_params(("arbitrary",)),
    )(db0, db1, oh)


def _owner_view(ref, name, d):
    if name == 'a_norm':
        return ref.at[d]
    if name in COL_SHARDED:
        n = ref.shape[2] // N_DEV
        return ref.at[:, :, pl.ds(pl.multiple_of(d * n, 128), n)]
    return ref.at[:, d]


def _place():
    return lax.axis_index("x"), lax.axis_index("y"), lax.axis_index("c")


def _dev(p):
    return 4 * p[0] + 2 * p[1] + p[2]


def _remote(src, dst, send_sem, recv_sem, to):
    return pltpu.make_async_remote_copy(src_ref=src, dst_ref=dst, send_sem=send_sem, recv_sem=recv_sem,
                                        device_id=to, device_id_type=MESH)


def _dma_sems(*shapes):
    return [pltpu.SemaphoreType.DMA(sh) for sh in shapes]


def comm_call(name, build, ins, out_shapes, sems, aliases=None):
    n_in, n_out = len(ins), len(out_shapes)

    def body(*refs):
        copies = build(refs[:n_in], refs[n_in:n_in + n_out], *refs[n_in + n_out:])
        for cp in copies:
            cp.start()
        for cp in copies:
            cp.wait()

    hbm = pl.BlockSpec(memory_space=pl.ANY)
    return pl.pallas_call(
        body, name=name, in_specs=[hbm] * n_in, out_specs=[hbm] * n_out, out_shape=list(out_shapes),
        scratch_shapes=sems, input_output_aliases=aliases or {},
    )(*ins)


def all_gather_weights(names, shards, full_shapes):
    n = len(names)

    def body(*refs):
        ins, outs = refs[:n], refs[n:2 * n]
        send_sems, recv_sems, local_sems = refs[2 * n:]
        x, y, c = _place()
        me, sibling = (x, y, c), (x, y, 1 - c)
        chips = [(1 - x, y), (x, 1 - y), (1 - x, 1 - y)]

        def copy(t, k, block, to, src=None):
            dst = _owner_view(outs[t], names[t], _dev(block))
            return _remote(dst if src is None else src, dst, send_sems.at[t, k], recv_sems.at[t, k], to)

        mine = [pltpu.make_async_copy(ins[t], _owner_view(outs[t], names[t], _dev(me)), local_sems.at[t])
                for t in range(n)]
        for cp in mine:
            cp.start()
        first = []
        for t in range(n):
            first.append(copy(t, 0, me, sibling, src=ins[t]))
            first += [copy(t, 1 + j, me, (*chip, c), src=ins[t]) for j, chip in enumerate(chips)]
        for cp in first:
            cp.start()
        passed = []
        for j, chip in enumerate(chips):
            for t in range(n):
                copy(t, 1 + j, (*chip, c), me).wait_recv()
                fwd = copy(t, 4 + j, (*chip, c), sibling)
                fwd.start()
                passed.append(fwd)
        for t in range(n):
            copy(t, 0, sibling, me).wait_recv()
            for j, chip in enumerate(chips):
                copy(t, 4 + j, (*chip, 1 - c), me).wait_recv()
        for cp in first + passed:
            cp.wait_send()
        for cp in mine:
            cp.wait()

    hbm = pl.BlockSpec(memory_space=pl.ANY)
    return pl.pallas_call(
        body, name="all_gather_layer0",
        in_specs=[hbm] * n, out_specs=[hbm] * n,
        out_shape=[jax.ShapeDtypeStruct(full_shapes[t], shards[t].dtype) for t in range(n)],
        scratch_shapes=_dma_sems((n, 7), (n, 7), (n,)),
    )(*shards)


def ag_direct(names, shards, full_shapes):
    n = len(names)

    def build(ins, outs, send_sems, recv_sems, local_sems):
        x, y, c = _place()
        peers = [(x, y, 1 - c), (1 - x, y, c), (x, 1 - y, c), (1 - x, 1 - y, c)]
        copies = []
        for t in range(n):
            dst = _owner_view(outs[t], names[t], _dev((x, y, c)))
            copies.append(pltpu.make_async_copy(ins[t], dst, local_sems.at[t]))
            copies += [_remote(ins[t], dst, send_sems.at[t, k], recv_sems.at[t, k], to) for k, to in enumerate(peers)]
        return copies

    return Carry(build, shards, [jax.ShapeDtypeStruct(full_shapes[t], shards[t].dtype) for t in range(n)],
                 _dma_sems((n, 4), (n, 4), (n,)))


def ag_forward(tag, names, partial):
    n = len(names)

    def build(ins, outs, send_sems, recv_sems):
        del ins
        x, y, c = _place()
        copies = []
        for t in range(n):
            for k, chip in enumerate([(1 - x, y), (x, 1 - y), (1 - x, 1 - y)]):
                view = _owner_view(outs[t], names[t], _dev((*chip, c)))
                copies.append(_remote(view, view, send_sems.at[t, k], recv_sems.at[t, k], (x, y, 1 - c)))
        return copies

    return comm_call(f"all_gather_forward_{tag}", build, partial, [jax.ShapeDtypeStruct(p.shape, p.dtype) for p in partial],
                     _dma_sems((n, 3), (n, 3)), aliases={t: t for t in range(n)})


def sibling_exchange(tag, names, grads, part_shapes):
    n = len(names)

    def build(ins, outs, send_sems, recv_sems):
        x, y, c = _place()
        return [_remote(_owner_view(ins[t], names[t], 2 * q + 1 - c), outs[t].at[q], send_sems.at[t, q],
                        recv_sems.at[t, q], (x, y, 1 - c)) for t in range(n) for q in range(4)]

    return comm_call(f"rs_sibling_exchange_{tag}", build, grads,
                     [jax.ShapeDtypeStruct((4,) + part_shapes[t], BF16) for t in range(n)], _dma_sems((n, 4), (n, 4)))


def chip_exchange(names, parts, part_shapes):
    n = len(names)

    def build(ins, outs, send_sems, recv_sems):
        x, y, c = _place()
        chips = [(1 - x, y), (x, 1 - y), (1 - x, 1 - y)]
        return [_remote(ins[t].at[2 * chip[0] + chip[1]], outs[t].at[k], send_sems.at[t, k], recv_sems.at[t, k],
                        (*chip, c)) for t in range(n) for k, chip in enumerate(chips)]

    return Carry(build, parts, [jax.ShapeDtypeStruct((3,) + part_shapes[t], BF16) for t in range(n)],
                 _dma_sems((n, 3), (n, 3)))


def all_gather_rows(x):
    r, w = x.shape

    def body(x_ref, out_ref, send_sems, recv_sems, local_sem):
        px, py, pc = _place()
        me = 4 * px + 2 * py + pc
        mine = pltpu.make_async_copy(x_ref, out_ref.at[me], local_sem)
        mine.start()
        copies = []
        for k in range(1, N_DEV):
            peer = (px ^ (k >> 2), py ^ ((k >> 1) & 1), pc ^ (k & 1))
            copies.append(pltpu.make_async_remote_copy(
                src_ref=x_ref, dst_ref=out_ref.at[me], send_sem=send_sems.at[k - 1], recv_sem=recv_sems.at[k - 1],
                device_id=peer, device_id_type=MESH))
        for cp in copies:
            cp.start()
        for k in range(1, N_DEV):
            peer_idx = me ^ k
            pltpu.make_async_remote_copy(
                src_ref=x_ref, dst_ref=out_ref.at[peer_idx], send_sem=send_sems.at[k - 1],
                recv_sem=recv_sems.at[k - 1], device_id=(px, py, pc), device_id_type=MESH).wait_recv()
        for cp in copies:
            cp.wait_send()
        mine.wait()

    vmem = pl.BlockSpec(memory_space=pltpu.VMEM)
    return pl.pallas_call(
        body, name="all_gather_small_grads",
        in_specs=[vmem], out_specs=vmem,
        out_shape=jax.ShapeDtypeStruct((N_DEV, r, w), x.dtype),
        scratch_shapes=[pltpu.SemaphoreType.DMA((N_DEV - 1,)), pltpu.SemaphoreType.DMA((N_DEV - 1,)),
                        pltpu.SemaphoreType.DMA],
    )(x)


def _adamw(w, g, m, v):
    m = ADAM_B1 * m + (1.0 - ADAM_B1) * g
    v = ADAM_B2 * v + (1.0 - ADAM_B2) * (g * g)
    m_hat = m / (1.0 - ADAM_B1 ** ADAM_STEP)
    v_hat = v / (1.0 - ADAM_B2 ** ADAM_STEP)
    return -ADAM_LR * (m_hat / (jnp.sqrt(v_hat) + ADAM_EPS) + ADAM_WD * w), m, v


def sibling_sum(name, col, grads, recv, core):
    _, nl, rows, cols = recv.shape
    tr = _tile(rows, 512)
    rspec = pl.BlockSpec((None, None, tr, cols), lambda q, l, i, c_ref: (q, l, i, 0))
    if col:
        gspec = pl.BlockSpec((None, tr, cols), lambda q, l, i, c_ref: (l, i, 2 * q + c_ref[0]))
    else:
        gspec = pl.BlockSpec((None, None, tr, cols), lambda q, l, i, c_ref: (l, 2 * q + c_ref[0], i, 0))

    def body(c_ref, g_ref, r_ref, o_ref):
        del c_ref
        o_ref[...] = (g_ref[...].astype(F32) + r_ref[...].astype(F32)).astype(BF16)

    return pl.pallas_call(
        body, name=name,
        grid_spec=pltpu.PrefetchScalarGridSpec(num_scalar_prefetch=1, grid=(4, nl, rows // tr),
                                               in_specs=[gspec, rspec], out_specs=rspec),
        out_shape=jax.ShapeDtypeStruct(recv.shape, BF16),
        compiler_params=_params(("parallel", "parallel", "parallel")),
    )(core.reshape(1), grads, recv)


def reduce_adamw(name, parts, recv, chip, w, m, v, l0, prev):
    _, nl, rows, cols = parts.shape
    tr = _tile(rows, 256)

    def body(q_ref, p_ref, r_ref, w_ref, m_ref, v_ref, *rest):
        del q_ref
        g_out, d_out, m_out, v_out = rest[-4:]
        g = ((p_ref[...].astype(F32) + r_ref[0].astype(F32)) + r_ref[1].astype(F32)) + r_ref[2].astype(F32)
        d, mn, vn = _adamw(w_ref[...], g, m_ref[...], v_ref[...])
        g_out[...] = g
        d_out[...] = d
        m_out[...] = mn
        v_out[...] = vn

    blk = pl.BlockSpec((None, tr, cols), lambda l, i, q_ref: (l0 + l, i, 0))
    prev = list(prev) if prev else []
    return pl.pallas_call(
        body, name=name,
        grid_spec=pltpu.PrefetchScalarGridSpec(
            num_scalar_prefetch=1, grid=(nl, rows // tr),
            in_specs=[pl.BlockSpec((None, None, tr, cols), lambda l, i, q_ref: (q_ref[0], l, i, 0)),
                      pl.BlockSpec((3, None, tr, cols), lambda l, i, q_ref: (0, l, i, 0)), blk, blk, blk]
            + [pl.BlockSpec(memory_space=pl.ANY)] * len(prev),
            out_specs=[blk] * 4),
        out_shape=[jax.ShapeDtypeStruct(w.shape, F32)] * 4,
        input_output_aliases={6 + i: i for i in range(len(prev))},
        compiler_params=_params(("parallel", "parallel")),
    )(chip.reshape(1), parts, recv, w, m, v, *prev)


def small_adamw(name, gathered, w, m, v):
    _, r, c = gathered.shape

    def body(ga_ref, w_ref, m_ref, v_ref, g_out, d_out, m_out, v_out):
        g = ga_ref[0]
        for d in range(1, N_DEV):
            g = g + ga_ref[d]
        dl, mn, vn = _adamw(w_ref[...], g, m_ref[...], v_ref[...])
        g_out[...] = g
        d_out[...] = dl
        m_out[...] = mn
        v_out[...] = vn

    return pl.pallas_call(
        body, name=name,
        out_shape=[jax.ShapeDtypeStruct((r, c), F32)] * 4,
        compiler_params=_params(),
    )(gathered, w, m, v)


def _add(acc, *ex):
    return (acc + ex[0],)


def local_step(x, target, small, ex):
    s, d = x.shape
    n_a, n_b = small['a_norm'].shape[0], small['b_norm'].shape[0]
    sg = {}
    gb = {}

    def fwd_mm(name, a, wname, layer, epilogue, extras, out_dtypes, **kw):
        return mm_nn(name, a, *ex.weight(wname, layer), epilogue, extras, out_dtypes, **kw)

    def dx_mm(name, dy, wname, layer, epilogue, extras, out_dtypes, **kw):
        return mm_nt(name, dy, *ex.weight(wname, layer), epilogue, extras, out_dtypes, **kw)

    def dw_mm(name, a, dy, wname, layer, **kw):
        key, slab, shape = ex.grad(wname, layer)
        gb[key] = mm_tn(name, a, dy, gb.get(key), shape, slab, **kw)

    plain = lambda acc: (acc,)
    plus_col = lambda acc, b: (acc + b,)

    bias_flat = bias_table(small['rel_bias'])
    bias_t = bias_flat.reshape(2, 8, WINDOW, 2 * WINDOW).transpose(0, 3, 1, 2).reshape(2, 2 * WINDOW, 8 * WINDOW)
    sink_rows = [jnp.repeat(small['b_sinks'][j], WINDOW).reshape(2, 1, 8 * WINDOW) for j in range(n_b)]

    def mlp_fwd(h, layer):
        n2 = rms_fwd(f"mlp_norm_fwd{layer}", h, small['mlp_norm'][layer])
        u, a = fwd_mm(f"mlp_up_fwd{layer}", n2, 'mlp_up', layer,
                      lambda acc: (acc, jnp.square(jnp.maximum(acc, 0.0))), (), (BF16, BF16))
        (h2,) = fwd_mm(f"mlp_down_fwd{layer}", a, 'mlp_down', layer, _add, (h,), (F32,))
        return h2, (n2, u, a)

    h = x
    saved = []
    for l in range(n_a):
        n1 = rms_fwd(f"a_norm_fwd{l}", h, small['a_norm'][l])
        (qkvt,) = fwd_mm(f"a_qkv_fwd{l}", n1, 'a_wqkv', l, plain, (), (BF16,), out_t=True)
        qkvt = qkvt.reshape(3 * d // HEAD_DIM, HEAD_DIM, s)
        o_t, rtab, carried = sb_fwd(f"sb_fwd{l}", qkvt, ex.fwd_carry(l))
        ex.fwd_done(l, carried)
        o_t = o_t.reshape(d, s)
        (h_mid,) = fwd_mm(f"a_wo_fwd{l}", o_t, 'a_wo', l, _add, (h,), (F32,), a_t=True)
        h_out, mlp_saved = mlp_fwd(h_mid, l)
        saved.append((h, n1, qkvt, o_t, rtab, h_mid, mlp_saved))
        h = h_out
    h_kv = h
    nkv = rms_fwd("kv_norm_fwd", h, small['kv_norm'])
    (kvt,) = fwd_mm("kv_fwd", nkv, 'w_kv', 0, plus_col, (small['b_kv'].reshape(-1, 1),), (BF16,), out_t=True)
    kvt = kvt.reshape(2, 2, HEAD_DIM, s)
    kpt, vpt = (jnp.pad(t, ((0, 0), (0, 0), (WINDOW, 0))) for t in (kvt[0], kvt[1]))
    for j in range(n_b):
        layer = n_a + j
        n1 = rms_fwd(f"b_norm_fwd{j}", h, small['b_norm'][j])
        (qbt,) = fwd_mm(f"b_q_fwd{j}", n1, 'b_wq', j, plus_col, (small['b_bq'][j].reshape(-1, 1),), (BF16,),
                        out_t=True)
        o_t = swa_fwd(f"swa_fwd{j}", qbt, kpt, vpt, bias_t, sink_rows[j])
        (h_mid,) = fwd_mm(f"b_wo_fwd{j}", o_t, 'b_wo', j, lambda acc, hh, b: (acc + hh + b,),
                          (h, small['b_bo'][j].reshape(1, -1)), (F32,), a_t=True)
        h_out, mlp_saved = mlp_fwd(h_mid, layer)
        saved.append((h, n1, qbt, o_t, h_mid, mlp_saved))
        h = h_out

    dh, dhb, dg_final, loss_b = loss_head(h, small['final_norm'], target)
    sg['final_norm'] = dg_final[0]
    sg['mlp_norm'] = [None] * (n_a + n_b)

    def mlp_bwd(dh, dhb, h_mid, mlp_saved, layer):
        n2, u, a = mlp_saved
        (du,) = dx_mm(f"mlp_down_dx{layer}", dhb, 'mlp_down', layer,
                      lambda acc, uu: (acc * (2.0 * jnp.maximum(uu.astype(F32), 0.0)),), (u,), (BF16,))
        dw_mm(f"mlp_down_dw{layer}", a, dhb, 'mlp_down', layer)
        (dn2,) = dx_mm(f"mlp_up_dx{layer}", du, 'mlp_up', layer, plain, (), (F32,))
        dw_mm(f"mlp_up_dw{layer}", n2, du, 'mlp_up', layer)
        dh2, dh2b, dg, cs = rms_bwd(f"mlp_norm_bwd{layer}", h_mid, small['mlp_norm'][layer], dn2, dh)
        sg['mlp_norm'][layer] = dg[0]
        return dh2, dh2b, cs

    dkp = jnp.zeros(kpt.shape, F32)
    dvp = jnp.zeros(vpt.shape, F32)
    sg['b_norm'], sg['b_bq'], sg['b_bo'], sg['b_sinks'] = [None] * n_b, [None] * n_b, [None] * n_b, [None] * n_b
    dbias = [None] * n_b
    for j in reversed(range(n_b)):
        layer = n_a + j
        h_in, n1, qbt, o_t, h_mid, mlp_saved = saved[layer]
        dh, dhb, cs = mlp_bwd(dh, dhb, h_mid, mlp_saved, layer)
        sg['b_bo'][j] = cs[0]
        (do_t,) = dx_mm(f"b_wo_dx{j}", dhb, 'b_wo', j, plain, (), (BF16,), out_t=True)
        dw_mm(f"b_wo_dw{j}", o_t, dhb, 'b_wo', j, x_t=True)
        dq_t, dkp, dvp, dbias[j], dsink = swa_bwd(f"swa_bwd{j}", qbt, kpt, vpt, bias_t, sink_rows[j], do_t, dkp, dvp)
        sg['b_sinks'][j] = colsum(f"sink_grad{j}", dsink.reshape(16, WINDOW).T)[0]
        sg['b_bq'][j] = rowsum(f"b_bq_grad{j}", dq_t)
        (dn1,) = dx_mm(f"b_q_dx{j}", dq_t, 'b_wq', j, plain, (), (F32,), a_t=True)
        dw_mm(f"b_q_dw{j}", n1, dq_t, 'b_wq', j, dy_t=True)
        dh, dhb, dg, _ = rms_bwd(f"b_norm_bwd{j}", h_in, small['b_norm'][j], dn1, dh)
        sg['b_norm'][j] = dg[0]
    unt = lambda t: t.reshape(2, 2 * WINDOW, 8, WINDOW).transpose(0, 2, 3, 1).reshape(bias_flat.shape)
    sg['rel_bias'] = bias_table_grad(unt(dbias[0]), unt(dbias[1]))[:, :N_BUCKETS].T

    dkv_t = jnp.concatenate([dkp[:, :, WINDOW:], dvp[:, :, WINDOW:]], axis=0).reshape(-1, s)
    sg['b_kv'] = rowsum("b_kv_grad", dkv_t)
    dkvb = dkv_t.astype(BF16)
    (dnkv,) = dx_mm("kv_dx", dkvb, 'w_kv', 0, plain, (), (F32,), a_t=True)
    dw_mm("kv_dw", nkv, dkvb, 'w_kv', 0, dy_t=True)
    dh, dhb, dg, _ = rms_bwd("kv_norm_bwd", h_kv, small['kv_norm'], dnkv, dh)
    sg['kv_norm'] = dg[0]

    sg['a_norm'] = [None] * n_a
    for l in reversed(range(n_a)):
        h_in, n1, qkvt, o_t, rtab, h_mid, mlp_saved = saved[l]
        dh, dhb, _ = mlp_bwd(dh, dhb, h_mid, mlp_saved, l)
        (do_t,) = dx_mm(f"a_wo_dx{l}", dhb, 'a_wo', l, plain, (), (BF16,), out_t=True)
        dw_mm(f"a_wo_dw{l}", o_t, dhb, 'a_wo', l, x_t=True)
        dq_t, dk_t, dv_t, carried = sb_bwd(f"sb_bwd{l}", qkvt, do_t.reshape(d // HEAD_DIM, HEAD_DIM, s), rtab,
                                           ex.bwd_carry(l, gb))
        ex.bwd_done(l, carried)
        dqkv_t = jnp.concatenate([dq_t, dk_t, dv_t], axis=0).reshape(3 * d, s)
        (dn1,) = dx_mm(f"a_qkv_dx{l}", dqkv_t, 'a_wqkv', l, plain, (), (F32,), a_t=True)
        dw_mm(f"a_qkv_dw{l}", n1, dqkv_t, 'a_wqkv', l, dy_t=True)
        dh, dhb, dg, _ = rms_bwd(f"a_norm_bwd{l}", h_in, small['a_norm'][l], dn1, dh)
        sg['a_norm'][l] = dg[0]

    small_grads = {
        'a_norm': jnp.stack(sg['a_norm']), 'kv_norm': sg['kv_norm'], 'b_kv': sg['b_kv'],
        'b_norm': jnp.stack(sg['b_norm']), 'b_bq': jnp.stack(sg['b_bq']), 'b_sinks': jnp.stack(sg['b_sinks']),
        'b_bo': jnp.stack(sg['b_bo']), 'rel_bias': sg['rel_bias'], 'mlp_norm': jnp.stack(sg['mlp_norm']),
        'final_norm': sg['final_norm'],
    }
    return loss_b, dh, gb, small_grads


def _full_shape(name, shard_shape):
    if name in COL_SHARDED:
        return shard_shape[:2] + (N_DEV * shard_shape[2],)
    nl, r, n = shard_shape
    return (nl, N_DEV, r, n)


def _as_w3_shape(name, shard_shape):
    full = _full_shape(name, shard_shape)
    return full if name in COL_SHARDED else (full[0], full[1] * full[2], full[3])


def _as_w3(name, full):
    if name in COL_SHARDED:
        return full
    nl, nd, r, n = full.shape
    return full.reshape(nl, nd * r, n)


AG_GROUPS = {
    0: (('a_wqkv', 0, 1),),
    1: (('a_wo', 0, 2), ('mlp_up', 0, 2), ('mlp_down', 0, 2), ('a_wqkv', 1, 1)),
    2: (('mlp_up', 2, 2), ('mlp_down', 2, 2), ('b_wq', 0, 2), ('b_wo', 0, 2), ('w_kv', 0, 1)),
}
RS_GROUPS = {
    'A': (('mlp_up', 1, 3), ('mlp_down', 1, 3), ('b_wq', 0, 2), ('b_wo', 0, 2), ('w_kv', 0, 1)),
    'B': (('a_wqkv', 1, 1), ('a_wo', 1, 1), ('mlp_up', 0, 1), ('mlp_down', 0, 1)),
    'C': (('a_wqkv', 0, 1), ('a_wo', 0, 1)),
}


class _Exchanges:
    def __init__(self, full0, shards, core, chip, w3, m3, v3):
        self.wbuf = {0: {n: _as_w3(n, full0[n]) for n, _, _ in AG_GROUPS[0]}}
        self.shards, self.core, self.chip = shards, core, chip
        self.w3, self.m3, self.v3 = w3, m3, v3
        self.shard_dims = {n: w3[n].shape[1:] for n in BIG}
        self.parts = {}
        self.out = {}

    def weight(self, name, layer):
        for group, members in AG_GROUPS.items():
            for n, l0, nl in members:
                if n == name and l0 <= layer < l0 + nl:
                    return self.wbuf[group][name], layer - l0
        raise KeyError((name, layer))

    def fwd_carry(self, layer):
        names = [n for n, _, _ in AG_GROUPS[layer + 1]]
        shards = [self.shards[layer + 1][n] for n in names]
        return ag_direct(names, shards, [_full_shape(n, sh.shape) for n, sh in zip(names, shards)])

    def fwd_done(self, layer, carried):
        names = [n for n, _, _ in AG_GROUPS[layer + 1]]
        self.wbuf[layer + 1] = {n: _as_w3(n, f) for n, f in zip(names, ag_forward(layer + 1, names, list(carried)))}

    def grad(self, name, layer):
        for group, members in RS_GROUPS.items():
            for n, l0, nl in members:
                if n == name and l0 <= layer < l0 + nl:
                    return (group, name), layer - l0, _as_w3_shape(name, (nl,) + self.shard_dims[name])
        raise KeyError((name, layer))

    def _sibling_stage(self, group, gb):
        names = [n for n, _, _ in RS_GROUPS[group]]
        shapes = [(nl,) + self.shard_dims[n] for n, _, nl in RS_GROUPS[group]]
        gfull = [gb[(group, n)].reshape(_full_shape(n, sh)) for n, sh in zip(names, shapes)]
        recv = sibling_exchange(group, names, gfull, shapes)
        self.parts[group] = [sibling_sum(f"rs_sibling_sum_{group}_{n}", n in COL_SHARDED, g, r, self.core)
                             for n, g, r in zip(names, gfull, recv)]
        return names, shapes

    def bwd_carry(self, layer, gb):
        names, shapes = self._sibling_stage('A' if layer == 1 else 'B', gb)
        return chip_exchange(names, self.parts['A' if layer == 1 else 'B'], shapes)

    def bwd_done(self, layer, carried):
        self._adamw('A' if layer == 1 else 'B', carried)

    def finish(self, gb):
        names, shapes = self._sibling_stage('C', gb)
        ce = chip_exchange(names, self.parts['C'], shapes)
        self._adamw('C', comm_call("rs_chip_exchange_C", ce.build, ce.ins, ce.out_shapes, ce.sems))
        return self.out

    def _adamw(self, group, recv2):
        for (n, l0, _), p, r in zip(RS_GROUPS[group], self.parts[group], recv2):
            self.out[n] = reduce_adamw(f"adamw_{group}_{n}", p, r, self.chip, self.w3[n], self.m3[n], self.v3[n],
                                       l0, self.out.get(n))


def _pack_small(vals):
    flat = jnp.concatenate([vals[n].reshape(-1).astype(F32) for n in SMALL] + [vals['loss'].reshape(-1)])
    rows = -(-flat.shape[0] // 1024) * 8
    return jnp.pad(flat, (0, rows * 128 - flat.shape[0])).reshape(rows, 128)


def _unpack_small(packed, shapes):
    flat = packed.reshape(-1)
    out, off = {}, 0
    for n in SMALL + ['loss']:
        size = int(np.prod(shapes[n]))
        out[n] = flat[off:off + size].reshape(shapes[n])
        off += size
    return out


def kernel(x, a_norm, a_wqkv, a_wo, kv_norm, w_kv, b_kv, b_norm, b_wq, b_bq, b_sinks, b_wo, b_bo, rel_bias, mlp_norm, mlp_up, mlp_down, final_norm, loss_target, m_a_norm, m_a_wqkv, m_a_wo, m_kv_norm, m_w_kv, m_b_kv, m_b_norm, m_b_wq, m_b_bq, m_b_sinks, m_b_wo, m_b_bo, m_rel_bias, m_mlp_norm, m_mlp_up, m_mlp_down, m_final_norm, v_a_norm, v_a_wqkv, v_a_wo, v_kv_norm, v_w_kv, v_b_kv, v_b_norm, v_b_wq, v_b_bq, v_b_sinks, v_b_wo, v_b_bo, v_rel_bias, v_mlp_norm, v_mlp_up, v_mlp_down, v_final_norm):
    w = dict(a_norm=a_norm, a_wqkv=a_wqkv, a_wo=a_wo, kv_norm=kv_norm, w_kv=w_kv, b_kv=b_kv, b_norm=b_norm,
             b_wq=b_wq, b_bq=b_bq, b_sinks=b_sinks, b_wo=b_wo, b_bo=b_bo, rel_bias=rel_bias, mlp_norm=mlp_norm,
             mlp_up=mlp_up, mlp_down=mlp_down, final_norm=final_norm)
    m = dict(a_norm=m_a_norm, a_wqkv=m_a_wqkv, a_wo=m_a_wo, kv_norm=m_kv_norm, w_kv=m_w_kv, b_kv=m_b_kv,
             b_norm=m_b_norm, b_wq=m_b_wq, b_bq=m_b_bq, b_sinks=m_b_sinks, b_wo=m_b_wo, b_bo=m_b_bo,
             rel_bias=m_rel_bias, mlp_norm=m_mlp_norm, mlp_up=m_mlp_up, mlp_down=m_mlp_down, final_norm=m_final_norm)
    v = dict(a_norm=v_a_norm, a_wqkv=v_a_wqkv, a_wo=v_a_wo, kv_norm=v_kv_norm, w_kv=v_w_kv, b_kv=v_b_kv,
             b_norm=v_b_norm, b_wq=v_b_wq, b_bq=v_b_bq, b_sinks=v_b_sinks, b_wo=v_b_wo, b_bo=v_b_bo,
             rel_bias=v_rel_bias, mlp_norm=v_mlp_norm, mlp_up=v_mlp_up, mlp_down=v_mlp_down, final_norm=v_final_norm)
    px, py, pc = _place()
    me = 4 * px + 2 * py + pc
    chip = (2 * px + py).astype(jnp.int32)
    core = pc.astype(jnp.int32)

    as3 = lambda t: t[None] if t.ndim == 2 else t
    w3, m3, v3 = ({n: as3(src[n]) for n in BIG} for src in (w, m, v))
    shards = {g: {n: w3[n][l0:l0 + nl].astype(BF16) for n, l0, nl in members} for g, members in AG_GROUPS.items()}
    an_pad = jnp.zeros((8, 128), F32).at[:a_norm.shape[0]].set(a_norm)
    names0 = [n for n, _, _ in AG_GROUPS[0]]
    full0 = all_gather_weights(names0 + ['a_norm'], [shards[0][n] for n in names0] + [an_pad],
                               [_full_shape(n, shards[0][n].shape) for n in names0] + [(N_DEV, 8, 128)])
    full0 = dict(zip(names0 + ['a_norm'], full0))
    n_a = a_norm.shape[0]
    small = {n: w[n] for n in SMALL}
    small['a_norm'] = full0['a_norm'][:, :n_a].transpose(1, 0, 2).reshape(n_a, -1)

    ex = _Exchanges(full0, shards, core, chip, w3, m3, v3)
    loss_b, grad_x, gb, sgrads = local_step(x[0], loss_target[0], small, ex)
    out = {n: [t.reshape(w[n].shape) for t in bufs] for n, bufs in ex.finish(gb).items()}

    sgrads['loss'] = loss_b[0, :1]
    gathered = all_gather_rows(_pack_small(sgrads))
    shapes = {n: w[n].shape for n in SMALL}
    shapes['a_norm'] = (n_a, a_norm.shape[1] * N_DEV)
    shapes['loss'] = (1,)
    zeros1 = jnp.zeros((1,), F32)

    def packed(src):
        vals = {n: src[n] for n in SMALL}
        vals['a_norm'] = jnp.zeros(shapes['a_norm'], F32)
        vals['loss'] = zeros1
        return _pack_small(vals)

    sm = small_adamw("adamw_small", gathered, packed(w), packed(m), packed(v))
    sm = [_unpack_small(t, shapes) for t in sm]
    g_an = lax.dynamic_slice_in_dim(sm[0]['a_norm'], me * a_norm.shape[1], a_norm.shape[1], axis=1)
    pad = lambda t: jnp.zeros((8, 128), F32).at[:n_a].set(t)
    gathered_an = jnp.zeros((N_DEV, 8, 128), F32).at[0].set(pad(g_an))
    an = small_adamw("adamw_a_norm", gathered_an, pad(a_norm), pad(m_a_norm), pad(v_a_norm))
    for i in range(4):
        sm[i]['a_norm'] = an[i][:n_a]
    for n in BIG:
        for i in range(4):
            sm[i][n] = out[n][i]
    loss = sm[0]['loss'][0]
    return (loss, grad_x[None], *[sm[0][n] for n in WEIGHTS], *[sm[1][n] for n in WEIGHTS],
            *[sm[2][n] for n in WEIGHTS], *[sm[3][n] for n in WEIGHTS])
```

```python
import functools
import math

import numpy as np
import jax
import jax.numpy as jnp
from jax import lax
from jax.experimental import pallas as pl
from jax.experimental.pallas import tpu as pltpu

F32 = jnp.float32
BF16 = jnp.bfloat16
MESH = pl.DeviceIdType.MESH

N_DEV = 8
HEAD_DIM = 64
WINDOW = 128
N_BUCKETS = 32
EPS = 1e-5
NEG_INF = -1e30
Q_SCALE = 1.0 / math.sqrt(HEAD_DIM)
LOG2E = 1.4426950408889634

ADAM_LR, ADAM_B1, ADAM_B2, ADAM_EPS, ADAM_WD, ADAM_STEP = 0.001, 0.9, 0.999, 1e-08, 0.01, 10

SB_BQ = 512
SB_BK = 128
SB_DEAD = 160.0
SB_UNSEEN = 1e30
ROW_TILE = 512
VMEM_LIMIT = 56 * 1024 * 1024

WEIGHTS = ['a_norm', 'a_wqkv', 'a_wo', 'kv_norm', 'w_kv', 'b_kv', 'b_norm', 'b_wq', 'b_bq', 'b_sinks', 'b_wo',
           'b_bo', 'rel_bias', 'mlp_norm', 'mlp_up', 'mlp_down', 'final_norm']
BIG = ['a_wqkv', 'a_wo', 'w_kv', 'b_wq', 'b_wo', 'mlp_up', 'mlp_down']
COL_SHARDED = ('a_wqkv', 'mlp_up')
SMALL = ['a_norm', 'kv_norm', 'b_kv', 'b_norm', 'b_bq', 'b_sinks', 'b_bo', 'rel_bias', 'mlp_norm', 'final_norm']


def _params(sem=None):
    return pltpu.CompilerParams(dimension_semantics=sem, vmem_limit_bytes=VMEM_LIMIT)


def _pick(n, cands):
    for c in cands:
        if n % c == 0:
            return c
    raise ValueError(n)


def _tile(n, want):
    return n if n <= want else _pick(n, (want, want // 2, want // 4))


def mm_nn(name, a, w3, layer, epilogue, extras, out_dtypes, a_t=False, out_t=False):
    k, m = a.shape if a_t else a.shape[::-1]
    _, kw, n = w3.shape
    assert kw == k
    tm = _tile(m, 2048 if k <= 1024 else 512)
    tn = _tile(n, 1024)
    ne, no = len(extras), len(out_dtypes)
    a_dim = 0 if a_t else 1

    def body(a_ref, w_ref, *rest):
        ex, outs = rest[:ne], rest[ne:ne + no]
        if out_t:
            acc = lax.dot_general(w_ref[...], a_ref[...], (((0,), (a_dim,)), ((), ())), preferred_element_type=F32)
        else:
            acc = lax.dot_general(a_ref[...], w_ref[...], (((a_dim,), (0,)), ((), ())), preferred_element_type=F32)
        for o, r in zip(outs, epilogue(acc, *[e[...] for e in ex])):
            o[...] = r.astype(o.dtype)

    if out_t:
        tile = pl.BlockSpec((tn, tm), lambda i, j: (j, i))
        vec = pl.BlockSpec((tn, 1), lambda i, j: (j, 0))
        out_shape = (n, m)
    else:
        tile = pl.BlockSpec((tm, tn), lambda i, j: (i, j))
        vec = pl.BlockSpec((1, tn), lambda i, j: (0, j))
        out_shape = (m, n)
    a_spec = pl.BlockSpec((k, tm), lambda i, j: (0, i)) if a_t else pl.BlockSpec((tm, k), lambda i, j: (i, 0))
    return pl.pallas_call(
        body, name=name, grid=(m // tm, n // tn),
        in_specs=[a_spec, pl.BlockSpec((None, k, tn), lambda i, j: (layer, 0, j))]
        + [tile if e.shape == out_shape else vec for e in extras],
        out_specs=[tile] * no,
        out_shape=[jax.ShapeDtypeStruct(out_shape, d) for d in out_dtypes],
        compiler_params=_params(("parallel", "parallel")),
    )(a, w3, *extras)


def mm_nt(name, dy, w3, layer, epilogue, extras, out_dtypes, a_t=False, out_t=False):
    n, m = dy.shape if a_t else dy.shape[::-1]
    _, k, nw = w3.shape
    assert nw == n
    tm = _tile(m, 2048 if n <= 1024 else 512)
    tko = _tile(k, 1024)
    ne, no = len(extras), len(out_dtypes)
    a_dim = 0 if a_t else 1

    def body(a_ref, w_ref, *rest):
        ex, outs = rest[:ne], rest[ne:ne + no]
        if out_t:
            acc = lax.dot_general(w_ref[...], a_ref[...], (((1,), (a_dim,)), ((), ())), preferred_element_type=F32)
        else:
            acc = lax.dot_general(a_ref[...], w_ref[...], (((a_dim,), (1,)), ((), ())), preferred_element_type=F32)
        for o, v in zip(outs, epilogue(acc, *[e[...] for e in ex])):
            o[...] = v.astype(o.dtype)

    if out_t:
        tile = pl.BlockSpec((tko, tm), lambda i, ko: (ko, i))
        out_shape = (k, m)
    else:
        tile = pl.BlockSpec((tm, tko), lambda i, ko: (i, ko))
        out_shape = (m, k)
    a_spec = pl.BlockSpec((n, tm), lambda i, ko: (0, i)) if a_t else pl.BlockSpec((tm, n), lambda i, ko: (i, 0))
    return pl.pallas_call(
        body, name=name, grid=(m // tm, k // tko),
        in_specs=[a_spec, pl.BlockSpec((None, tko, n), lambda i, ko: (layer, ko, 0))] + [tile] * ne,
        out_specs=[tile] * no,
        out_shape=[jax.ShapeDtypeStruct(out_shape, d) for d in out_dtypes],
        compiler_params=_params(("parallel", "parallel")),
    )(dy, w3, *extras)


def mm_tn(name, x, dy, gbuf, shape, layer, x_t=False, dy_t=False):
    k, s = x.shape if x_t else x.shape[::-1]
    _, kw, n = shape
    assert kw == k and dy.shape == ((n, s) if dy_t else (s, n))
    tkk = _tile(k, 512)
    tn = _tile(n, 1024)

    def body(x_ref, dy_ref, *rest):
        g_out = rest[-1]
        g_out[...] = lax.dot_general(x_ref[...], dy_ref[...], (((1 if x_t else 0,), (1 if dy_t else 0,)), ((), ())),
                                     preferred_element_type=F32).astype(g_out.dtype)

    prev = [] if gbuf is None else [gbuf]
    x_spec = pl.BlockSpec((tkk, s), lambda ki, j: (ki, 0)) if x_t else pl.BlockSpec((s, tkk), lambda ki, j: (0, ki))
    dy_spec = pl.BlockSpec((tn, s), lambda ki, j: (j, 0)) if dy_t else pl.BlockSpec((s, tn), lambda ki, j: (0, j))
    return pl.pallas_call(
        body, name=name, grid=(k // tkk, n // tn),
        in_specs=[x_spec, dy_spec] + [pl.BlockSpec(memory_space=pl.ANY)] * len(prev),
        out_specs=pl.BlockSpec((None, tkk, tn), lambda ki, j: (layer, ki, j)),
        out_shape=jax.ShapeDtypeStruct(shape, BF16),
        input_output_aliases={2: 0} if prev else {},
        compiler_params=_params(("parallel", "parallel")),
    )(x, dy, *prev)


def rms_fwd(name, h, g):
    s, d = h.shape
    tr = _pick(s, (ROW_TILE, 256, 128))

    def body(h_ref, g_ref, o_ref):
        x = h_ref[...]
        r = lax.rsqrt(jnp.mean(x * x, axis=-1, keepdims=True) + EPS)
        o_ref[...] = (x * r * g_ref[...]).astype(o_ref.dtype)

    return pl.pallas_call(
        body, name=name, grid=(s // tr,),
        in_specs=[pl.BlockSpec((tr, d), lambda i: (i, 0)), pl.BlockSpec((1, d), lambda i: (0, 0))],
        out_specs=pl.BlockSpec((tr, d), lambda i: (i, 0)),
        out_shape=jax.ShapeDtypeStruct((s, d), BF16),
        compiler_params=_params(("parallel",)),
    )(h, g.reshape(1, d))


def rms_bwd(name, h, g, dn, dres):
    s, d = h.shape
    tr = _pick(s, (ROW_TILE, 256, 128))

    def body(h_ref, g_ref, dn_ref, dres_ref, dx_ref, dxb_ref, dg_ref, cs_ref):
        i = pl.program_id(0)
        x = h_ref[...]
        r = lax.rsqrt(jnp.mean(x * x, axis=-1, keepdims=True) + EPS)
        xh = x * r
        dn_ = dn_ref[...]
        dyg = dn_ * g_ref[...]
        dx = dres_ref[...] + r * (dyg - xh * jnp.mean(dyg * xh, axis=-1, keepdims=True))
        dx_ref[...] = dx
        dxb_ref[...] = dx.astype(BF16)

        @pl.when(i == 0)
        def _():
            dg_ref[...] = jnp.zeros_like(dg_ref)
            cs_ref[...] = jnp.zeros_like(cs_ref)

        dg_ref[...] += jnp.sum(dn_ * xh, axis=0, keepdims=True)
        cs_ref[...] += jnp.sum(dx, axis=0, keepdims=True)

    row = pl.BlockSpec((tr, d), lambda i: (i, 0))
    vec = pl.BlockSpec((1, d), lambda i: (0, 0))
    return pl.pallas_call(
        body, name=name, grid=(s // tr,),
        in_specs=[row, vec, row, row],
        out_specs=[row, row, vec, vec],
        out_shape=[jax.ShapeDtypeStruct((s, d), F32), jax.ShapeDtypeStruct((s, d), BF16),
                   jax.ShapeDtypeStruct((1, d), F32), jax.ShapeDtypeStruct((1, d), F32)],
        compiler_params=_params(("arbitrary",)),
    )(h, g.reshape(1, d), dn, dres)


def loss_head(h, g, target):
    s, d = h.shape
    tr = _pick(s, (ROW_TILE, 256, 128))

    def body(h_ref, g_ref, t_ref, dx_ref, dxb_ref, dg_ref, loss_ref):
        i = pl.program_id(0)
        x = h_ref[...]
        r = lax.rsqrt(jnp.mean(x * x, axis=-1, keepdims=True) + EPS)
        xh = x * r
        gw = g_ref[...]
        err = xh * gw - t_ref[...]
        dn_ = err * (1.0 / d)
        dyg = dn_ * gw
        dx = r * (dyg - xh * jnp.mean(dyg * xh, axis=-1, keepdims=True))
        dx_ref[...] = dx
        dxb_ref[...] = dx.astype(BF16)

        @pl.when(i == 0)
        def _():
            dg_ref[...] = jnp.zeros_like(dg_ref)
            loss_ref[...] = jnp.zeros_like(loss_ref)

        dg_ref[...] += jnp.sum(dn_ * xh, axis=0, keepdims=True)
        per_row = jnp.sum(err * err, axis=-1, keepdims=True) * (0.5 / d)
        loss_ref[...] += jnp.broadcast_to(jnp.sum(per_row, axis=0, keepdims=True), loss_ref.shape)

    row = pl.BlockSpec((tr, d), lambda i: (i, 0))
    vec = pl.BlockSpec((1, d), lambda i: (0, 0))
    return pl.pallas_call(
        body, name="loss_head", grid=(s // tr,),
        in_specs=[row, vec, row],
        out_specs=[row, row, vec, pl.BlockSpec((1, 128), lambda i: (0, 0))],
        out_shape=[jax.ShapeDtypeStruct((s, d), F32), jax.ShapeDtypeStruct((s, d), BF16),
                   jax.ShapeDtypeStruct((1, d), F32), jax.ShapeDtypeStruct((1, 128), F32)],
        compiler_params=_params(("arbitrary",)),
    )(h, g.reshape(1, d), target)


def colsum(name, x):
    s, n = x.shape
    tr = _pick(s, (ROW_TILE, 256, 128))

    def body(x_ref, o_ref):
        @pl.when(pl.program_id(0) == 0)
        def _():
            o_ref[...] = jnp.zeros_like(o_ref)

        o_ref[...] += jnp.sum(x_ref[...].astype(F32), axis=0, keepdims=True)

    return pl.pallas_call(
        body, name=name, grid=(s // tr,),
        in_specs=[pl.BlockSpec((tr, n), lambda i: (i, 0))],
        out_specs=pl.BlockSpec((1, n), lambda i: (0, 0)),
        out_shape=jax.ShapeDtypeStruct((1, n), F32),
        compiler_params=_params(("arbitrary",)),
    )(x)


def rowsum(name, x):
    n, s = x.shape
    ts = _pick(s, (1024, 512, 256, 128))

    def body(x_ref, o_ref):
        @pl.when(pl.program_id(0) == 0)
        def _():
            o_ref[...] = jnp.zeros_like(o_ref)

        o_ref[...] += jnp.sum(x_ref[...].astype(F32), axis=1, keepdims=True)

    return pl.pallas_call(
        body, name=name, grid=(s // ts,),
        in_specs=[pl.BlockSpec((n, ts), lambda i: (0, i))],
        out_specs=pl.BlockSpec((n, 1), lambda i: (0, 0)),
        out_shape=jax.ShapeDtypeStruct((n, 1), F32),
        compiler_params=_params(("arbitrary",)),
    )(x)[:, 0]


def _tri_rows(reverse):
    i = np.arange(SB_BK)
    tri = (i[None, :] >= i[:, None]) if reverse else (i[None, :] <= i[:, None])
    tri = np.concatenate([tri, tri], axis=1)
    return jnp.asarray(np.concatenate([tri, np.ones((8, 2 * SB_BK), bool)], axis=0), BF16)


def _hi_lo_rows(x):
    hi = x.astype(BF16)
    lo = (x - hi.astype(F32)).astype(BF16)
    return jnp.concatenate([hi, lo], axis=0)


def _softplus2(zs):
    neg_abs = lax.bitcast_convert_type(lax.bitcast_convert_type(zs, jnp.uint32) | jnp.uint32(0x80000000), F32)
    return jnp.maximum(zs, 0.0) + jnp.log2(1.0 + jnp.exp2(neg_abs))


def _pair_mask(first_rel_block, bq):
    key = lax.broadcasted_iota(jnp.int32, (2 * SB_BK, bq), 0) + first_rel_block * SB_BK
    qry = lax.broadcasted_iota(jnp.int32, (2 * SB_BK, bq), 1)
    return key < qry


def _row_of(table8, sub8, r):
    return jnp.sum(jnp.where(sub8 == r, table8, 0.0), axis=0, keepdims=True)


def _keys(j0):
    return pl.ds(pl.multiple_of(j0 * SB_BK, 2 * SB_BK), 2 * SB_BK)


class Carry:
    def __init__(self, build, ins, out_shapes, sems):
        self.build, self.ins, self.out_shapes, self.sems = build, list(ins), list(out_shapes), list(sems)


def _carried(carry, rest, n_out, n_scratch, first, last):
    n_ci = len(carry.ins) if carry else 0
    n_co = len(carry.out_shapes) if carry else 0
    cin, outs = rest[:n_ci], rest[n_ci:n_ci + n_out]
    cout = rest[n_ci + n_out:n_ci + n_out + n_co]
    scratch = rest[n_ci + n_out + n_co:n_ci + n_out + n_co + n_scratch]
    csems = rest[n_ci + n_out + n_co + n_scratch:]

    def start():
        if carry:
            @pl.when(first)
            def _():
                for cp in carry.build(cin, cout, *csems):
                    cp.start()

    def wait():
        if carry:
            @pl.when(last)
            def _():
                for cp in carry.build(cin, cout, *csems):
                    cp.wait()

    return outs, scratch, start, wait


def _contract0(a, b):
    return lax.dot_general(a, b, (((0,), (0,)), ((), ())), preferred_element_type=F32)


def _contract1(a, b):
    return lax.dot_general(a, b, (((1,), (1,)), ((), ())), preferred_element_type=F32)


def sb_fwd(name, qkvt, exchange=None):
    nh, dh, s = qkvt.shape[0] // 3, qkvt.shape[1], qkvt.shape[2]
    bq = SB_BQ
    per_q = bq // SB_BK
    nkb = s // SB_BK
    assert s % bq == 0 and per_q == 4 and nkb % 8 == 0

    def body(q_ref, k_ref, v_ref, a_ref, *rest):
        head = pl.program_id(0)
        (o_ref, rtab_ref), (acc, zbuf, wbuf), start_carried, wait_carried = _carried(
            exchange, rest, 2, 3, head == 0, head == nh - 1)
        start_carried()
        tri = a_ref[...]
        sub8 = lax.broadcasted_iota(jnp.int32, (8, bq), 0)
        rtab_ref[...] = jnp.full(rtab_ref.shape, SB_UNSEEN, F32)
        kf = k_ref[...].astype(F32)
        k_max2 = jnp.max(jnp.sum(kf * kf, axis=0, keepdims=True), axis=1, keepdims=True)

        def query_block(i, _):
            lanes = pl.ds(pl.multiple_of(i * bq, bq), bq)
            qb = q_ref[:, lanes] * Q_SCALE
            acc[...] = jnp.zeros_like(acc)
            qf = qb.astype(F32)
            bound = jnp.sqrt(jnp.sum(qf * qf, axis=0, keepdims=True) * k_max2) * (1.001 * LOG2E)

            def scores(j0):
                return _contract0(k_ref[:, _keys(j0)], qb) * LOG2E

            def pair(j0, slot, run, rt8, mask, has_prev):
                zs = zbuf[slot]
                zbuf[1 - slot] = scores(jnp.maximum(j0 - 2, 0))
                if has_prev:
                    acc[...] += jnp.dot(v_ref[:, _keys(j0 + 2)], wbuf[1 - slot], preferred_element_type=F32)
                p = _softplus2(zs)
                if mask is not None:
                    p = jnp.where(mask, p, 0.0)
                cr1 = jnp.dot(tri, _hi_lo_rows(p[SB_BK:]), preferred_element_type=F32)
                cr0 = jnp.dot(tri, _hi_lo_rows(p[:SB_BK]), preferred_element_type=F32)
                run1 = run + cr1[SB_BK:SB_BK + 1]
                w = jnp.exp2(jnp.concatenate([zs[:SB_BK] - cr0[:SB_BK] - run1, zs[SB_BK:] - cr1[:SB_BK] - run],
                                             axis=0))
                if mask is not None:
                    w = jnp.where(mask, w, 0.0)
                wbuf[slot] = w.astype(BF16)
                rt8 = jnp.where(j0 % 8 == 6, SB_UNSEEN, rt8)
                rt8 = jnp.where(sub8 == (j0 + 1) % 8, run, jnp.where(sub8 == j0 % 8, run1, rt8))
                rtab_ref[pl.ds(pl.multiple_of((j0 // 8) * 8, 8), 8), lanes] = rt8
                return run1 + cr0[SB_BK:SB_BK + 1], rt8

            def alive(run):
                return jnp.min(run - bound) < SB_DEAD

            top = i * per_q
            zbuf[0] = scores(top + 2)
            state = (jnp.zeros((1, bq), F32), jnp.full((8, bq), SB_UNSEEN, F32))
            state = pair(top + 2, 0, *state, _pair_mask(2, bq), False)
            state = pair(top, 1, *state, _pair_mask(0, bq), True)

            def step(c):
                it, pairs, _, run, rt8 = c
                j0 = top - 2 - 4 * it
                run, rt8 = pair(j0, 0, run, rt8, None, True)
                go = alive(run)
                run, rt8 = lax.cond(go, lambda r, t: pair(j0 - 2, 1, r, t, None, True), lambda r, t: (r, t), run, rt8)
                return it + 1, pairs + 1 + go.astype(jnp.int32), go & alive(run), run, rt8

            pairs = lax.while_loop(lambda c: (c[0] < i) & c[2], step, (0, 0, alive(state[0]), *state))[1]
            acc[...] += jnp.dot(v_ref[:, _keys(top - 2 * pairs)], wbuf[(pairs + 1) % 2], preferred_element_type=F32)
            o_ref[:, lanes] = acc[...].astype(o_ref.dtype)
            return 0

        lax.fori_loop(0, s // bq, query_block, 0)
        wait_carried()

    def head_spec(offset, rows):
        return pl.BlockSpec((None, rows, s), lambda h: (h + offset, 0, 0))

    hbm = pl.BlockSpec(memory_space=pl.ANY)
    c_ins, c_outs, c_sems = (exchange.ins, exchange.out_shapes, exchange.sems) if exchange else ([], [], [])
    outs = pl.pallas_call(
        body, name=name, grid=(nh,),
        in_specs=[head_spec(0, dh), head_spec(nh, dh), head_spec(2 * nh, dh),
                  pl.BlockSpec((SB_BK + 8, 2 * SB_BK), lambda h: (0, 0))] + [hbm] * len(c_ins),
        out_specs=[head_spec(0, dh), head_spec(0, nkb)] + [hbm] * len(c_outs),
        out_shape=[jax.ShapeDtypeStruct((nh, dh, s), BF16), jax.ShapeDtypeStruct((nh, nkb, s), F32)] + c_outs,
        scratch_shapes=[pltpu.VMEM((dh, bq), F32), pltpu.VMEM((2, 2 * SB_BK, bq), F32),
                        pltpu.VMEM((2, 2 * SB_BK, bq), BF16)] + c_sems,
        compiler_params=_params(("arbitrary",)),
    )(qkvt, qkvt, qkvt, _tri_rows(True), *c_ins)
    return outs[0], outs[1], outs[2:]


def sb_bwd(name, qkvt, dot_, rtab, exchange=None):
    nh, dh, s = qkvt.shape[0] // 3, qkvt.shape[1], qkvt.shape[2]
    bq = SB_BQ
    per_q = bq // SB_BK
    nkb = s // SB_BK

    def body(qt_ref, kt_ref, vt_ref, dot_ref, rtab_ref, ar_ref, af_ref, *rest):
        head = pl.program_id(0)
        (dq_ref, dk_ref, dv_ref), (dq_acc, dk_acc, dv_acc, zbuf, dwbuf, dzbuf, wbuf), start_carried, wait_carried = \
            _carried(exchange, rest, 3, 7, head == 0, head == nh - 1)
        start_carried()
        dk_acc[...] = jnp.zeros_like(dk_acc)
        dv_acc[...] = jnp.zeros_like(dv_acc)
        tri_rev = ar_ref[...][:SB_BK]
        tri_fwd = af_ref[...]
        sub8 = lax.broadcasted_iota(jnp.int32, (8, bq), 0)

        def query_block(i, _):
            lanes = pl.ds(pl.multiple_of(i * bq, bq), bq)
            qtb = qt_ref[:, lanes] * Q_SCALE
            dotb = dot_ref[:, lanes]
            dq_acc[...] = jnp.zeros_like(dq_acc)
            last_j = i * per_q + 2
            seen = jnp.max(jnp.where(rtab_ref[:, lanes] < 0.1 * SB_UNSEEN, 1.0, 0.0), axis=1, keepdims=True)
            pairs = jnp.clip((jnp.sum(seen).astype(jnp.int32) - per_q) // 2, 0, 2 * i)
            odd = pairs % 2
            first_j = i * per_q - 2 * pairs

            def issue(j0, slot):
                zbuf[slot] = _contract0(kt_ref[:, _keys(j0)], qtb) * LOG2E
                dwbuf[slot] = _contract0(vt_ref[:, _keys(j0)], dotb)

            def retire(j0, slot):
                keys = _keys(j0)
                dq_acc[...] += jnp.dot(kt_ref[:, keys], dzbuf[slot], preferred_element_type=F32)
                dk_acc[:, keys] += _contract1(qtb, dzbuf[slot])
                dv_acc[:, keys] += _contract1(dotb, wbuf[slot])

            def pair(j0, slot, g_run, mask):
                zs = zbuf[slot]
                dw = dwbuf[slot]
                issue(jnp.minimum(j0 + 2, last_j), 1 - slot)
                retire(jnp.maximum(j0 - 2, first_j), 1 - slot)
                p_raw = _softplus2(zs)
                p = p_raw if mask is None else jnp.where(mask, p_raw, 0.0)
                c0 = jnp.dot(tri_rev, _hi_lo_rows(p[:SB_BK]), preferred_element_type=F32)
                c1 = jnp.dot(tri_rev, _hi_lo_rows(p[SB_BK:]), preferred_element_type=F32)
                rt8 = rtab_ref[pl.ds(pl.multiple_of((j0 // 8) * 8, 8), 8), lanes]
                r0 = _row_of(rt8, sub8, j0 % 8)
                r1 = _row_of(rt8, sub8, (j0 + 1) % 8)
                w = jnp.exp2(jnp.concatenate([zs[:SB_BK] - c0 - r0, zs[SB_BK:] - c1 - r1], axis=0))
                if mask is not None:
                    w = jnp.where(mask, w, 0.0)
                g = w * dw
                gg0 = jnp.dot(tri_fwd, _hi_lo_rows(g[:SB_BK]), preferred_element_type=F32)
                gg1 = jnp.dot(tri_fwd, _hi_lo_rows(g[SB_BK:]), preferred_element_type=F32)
                g_run1 = g_run + gg0[SB_BK:SB_BK + 1]
                g_pre = jnp.concatenate([gg0[:SB_BK] + g_run, gg1[:SB_BK] + g_run1], axis=0)
                dz = g - jnp.exp2(zs - p_raw) * g_pre
                if mask is not None:
                    dz = jnp.where(mask, dz, 0.0)
                dzbuf[slot] = dz.astype(BF16)
                wbuf[slot] = w.astype(BF16)
                return g_run1 + gg1[SB_BK:SB_BK + 1]

            issue(first_j, odd)
            dzbuf[...] = jnp.zeros(dzbuf.shape, BF16)
            wbuf[...] = jnp.zeros(wbuf.shape, BF16)

            def step(it, g_run):
                g_run = pair(4 * it, 0, g_run, None)
                return pair(4 * it + 2, 1, g_run, None)

            g_run = lax.cond(odd == 1, lambda g: pair(first_j, 1, g, None), lambda g: g, jnp.zeros((1, bq), F32))
            g_run = lax.fori_loop(i - pairs // 2, i, step, g_run)
            g_run = pair(last_j - 2, 0, g_run, _pair_mask(0, bq))
            pair(last_j, 1, g_run, _pair_mask(2, bq))
            retire(last_j, 1)
            dq_ref[:, lanes] = (dq_acc[...] * Q_SCALE).astype(dq_ref.dtype)
            return 0

        lax.fori_loop(0, s // bq, query_block, 0)
        dk_ref[...] = dk_acc[...].astype(dk_ref.dtype)
        dv_ref[...] = dv_acc[...].astype(dv_ref.dtype)
        wait_carried()

    def head_spec(offset, rows):
        return pl.BlockSpec((None, rows, s), lambda h: (h + offset, 0, 0))

    aspec = pl.BlockSpec((SB_BK + 8, 2 * SB_BK), lambda h: (0, 0))
    pair_f32 = pltpu.VMEM((2, 2 * SB_BK, bq), F32)
    pair_bf16 = pltpu.VMEM((2, 2 * SB_BK, bq), BF16)
    hbm = pl.BlockSpec(memory_space=pl.ANY)
    c_ins, c_outs, c_sems = (exchange.ins, exchange.out_shapes, exchange.sems) if exchange else ([], [], [])
    outs = pl.pallas_call(
        body, name=name, grid=(nh,),
        in_specs=[head_spec(0, dh), head_spec(nh, dh), head_spec(2 * nh, dh), head_spec(0, dh), head_spec(0, nkb),
                  aspec, aspec] + [hbm] * len(c_ins),
        out_specs=[head_spec(0, dh)] * 3 + [hbm] * len(c_outs),
        out_shape=[jax.ShapeDtypeStruct((nh, dh, s), BF16)] * 3 + c_outs,
        scratch_shapes=[pltpu.VMEM((dh, bq), F32), pltpu.VMEM((dh, s), F32), pltpu.VMEM((dh, s), F32),
                        pair_f32, pair_f32, pair_bf16, pair_bf16] + c_sems,
        compiler_params=_params(("arbitrary",)),
    )(qkvt, qkvt, qkvt, dot_, rtab, _tri_rows(True), _tri_rows(False), *c_ins)
    return outs[0], outs[1], outs[2], outs[3:]


SWA_QB = 2


def _swa_probs(qt, kt, bias_t, sink, i):
    cols = qt.shape[1]
    sc = _contract0(kt, qt) + bias_t
    kj = lax.broadcasted_iota(jnp.int32, (2 * WINDOW, cols), 0)
    qi = lax.broadcasted_iota(jnp.int32, (2 * WINDOW, cols), 1) & (WINDOW - 1)
    dist = qi + WINDOW - kj
    valid = (dist >= 0) & (dist < WINDOW) & ((kj >= WINDOW) | (i > 0))
    sc = jnp.where(valid, sc, NEG_INF)
    mx = jnp.maximum(jnp.max(sc, axis=0, keepdims=True), sink)
    p = jnp.exp(sc - mx)
    p_sink = jnp.exp(sink - mx)
    inv = 1.0 / (jnp.sum(p, axis=0, keepdims=True) + p_sink)
    return p, p_sink, inv


def _band(i):
    return pl.ds(pl.multiple_of(i * WINDOW, WINDOW), 2 * WINDOW)


def _heads_to_lanes(blk):
    return jnp.concatenate([blk[r * HEAD_DIM:(r + 1) * HEAD_DIM] for r in range(8)], axis=1)


def _lanes_to_heads(t):
    return jnp.concatenate([t[:, r * WINDOW:(r + 1) * WINDOW] for r in range(8)], axis=0)


def swa_fwd(name, qt, kpt, vpt, bias_t, sink_row):
    d, s = qt.shape
    ng, dh, sp = kpt.shape
    rows, cols = d // ng, SWA_QB * WINDOW
    assert (s // WINDOW) % SWA_QB == 0

    def body(q_ref, k_ref, v_ref, bias_ref, sink_ref, o_ref):
        for u in range(SWA_QB):
            i = pl.program_id(1) * SWA_QB + u
            lanes = slice(u * WINDOW, (u + 1) * WINDOW)
            qb = _heads_to_lanes(q_ref[:, lanes]) * Q_SCALE
            p, _, inv = _swa_probs(qb, k_ref[:, _band(i)], bias_ref[...], sink_ref[...], i)
            o_t = jnp.dot(v_ref[:, _band(i)], p.astype(BF16), preferred_element_type=F32) * inv
            o_ref[:, lanes] = _lanes_to_heads(o_t).astype(o_ref.dtype)

    qspec = pl.BlockSpec((rows, cols), lambda g, i: (g, i))
    kspec = pl.BlockSpec((None, dh, sp), lambda g, i: (g, 0, 0))
    return pl.pallas_call(
        body, name=name, grid=(ng, s // cols),
        in_specs=[qspec, kspec, kspec, pl.BlockSpec((None, 2 * WINDOW, 8 * WINDOW), lambda g, i: (g, 0, 0)),
                  pl.BlockSpec((None, 1, 8 * WINDOW), lambda g, i: (g, 0, 0))],
        out_specs=qspec,
        out_shape=jax.ShapeDtypeStruct(qt.shape, BF16),
        compiler_params=_params(("parallel", "arbitrary")),
    )(qt, kpt, vpt, bias_t, sink_row)


def swa_bwd(name, qt, kpt, vpt, bias_t, sink_row, dot_, dk_in, dv_in):
    d, s = qt.shape
    ng, dh, sp = kpt.shape
    rows, cols = d // ng, SWA_QB * WINDOW

    def body(q_ref, k_ref, v_ref, bias_ref, sink_ref, do_ref, dki_ref, dvi_ref, dq_ref, dk_ref, dv_ref, db_ref, ds_ref):
        @pl.when(pl.program_id(1) == 0)
        def _():
            dk_ref[...] = dki_ref[...]
            dv_ref[...] = dvi_ref[...]
            db_ref[...] = jnp.zeros_like(db_ref)
            ds_ref[...] = jnp.zeros_like(ds_ref)

        for u in range(SWA_QB):
            i = pl.program_id(1) * SWA_QB + u
            band = _band(i)
            lanes = slice(u * WINDOW, (u + 1) * WINDOW)
            qb = _heads_to_lanes(q_ref[:, lanes]) * Q_SCALE
            dob = _heads_to_lanes(do_ref[:, lanes])
            kt = k_ref[:, band]
            p, p_sink, inv = _swa_probs(qb, kt, bias_ref[...], sink_ref[...], i)
            p = p * inv
            dp = _contract0(v_ref[:, band], dob)
            delta = jnp.sum(p * dp, axis=0, keepdims=True)
            dsc = p * (dp - delta)
            ds_ref[...] -= p_sink * inv * delta
            db_ref[...] += dsc
            dscb = dsc.astype(BF16)
            dq_t = jnp.dot(kt, dscb, preferred_element_type=F32) * Q_SCALE
            dq_ref[:, lanes] = _lanes_to_heads(dq_t).astype(dq_ref.dtype)
            dk_ref[:, band] += _contract1(qb, dscb)
            dv_ref[:, band] += _contract1(dob, p.astype(BF16))

    qspec = pl.BlockSpec((rows, cols), lambda g, i: (g, i))
    kspec = pl.BlockSpec((None, dh, sp), lambda g, i: (g, 0, 0))
    bspec = pl.BlockSpec((None, 2 * WINDOW, 8 * WINDOW), lambda g, i: (g, 0, 0))
    sspec = pl.BlockSpec((None, 1, 8 * WINDOW), lambda g, i: (g, 0, 0))
    return pl.pallas_call(
        body, name=name, grid=(ng, s // cols),
        in_specs=[qspec, kspec, kspec, bspec, sspec, qspec, kspec, kspec],
        out_specs=[qspec, kspec, kspec, bspec, sspec],
        out_shape=[jax.ShapeDtypeStruct(qt.shape, BF16), jax.ShapeDtypeStruct(kpt.shape, F32),
                   jax.ShapeDtypeStruct(kpt.shape, F32), jax.ShapeDtypeStruct(bias_t.shape, F32),
                   jax.ShapeDtypeStruct(sink_row.shape, F32)],
        compiler_params=_params(("parallel", "arbitrary")),
    )(qt, kpt, vpt, bias_t, sink_row, dot_, dk_in, dv_in)


def _bucket_onehot():
    qi = np.arange(WINDOW)[:, None]
    kj = np.arange(2 * WINDOW)[None, :]
    n = np.maximum(qi + WINDOW - kj, 0)
    max_exact = N_BUCKETS // 2
    nf = np.maximum(n, 1).astype(np.float64)
    val = np.log(nf / max_exact) / math.log(WINDOW / max_exact) * (N_BUCKETS - max_exact)
    assert np.all(np.abs(val - np.round(val))[(n > max_exact) & (n < WINDOW)] > 1e-3)
    large = np.minimum(max_exact + val.astype(np.int64), N_BUCKETS - 1)
    bucket = np.where(n < max_exact, n, large).reshape(-1)
    onehot = np.zeros((128, bucket.size), np.float32)
    onehot[bucket, np.arange(bucket.size)] = 1.0
    return onehot


def _split3(x):
    a = x.astype(BF16)
    r = x - a.astype(F32)
    b = r.astype(BF16)
    c = (r - b.astype(F32)).astype(BF16)
    return a, b, c


def bias_table(rel_bias):
    nh = rel_bias.shape[1]
    oh = jnp.asarray(_bucket_onehot(), BF16)
    n = oh.shape[1]
    tn = 4096
    rb = jnp.zeros((nh, 128), F32).at[:, :N_BUCKETS].set(rel_bias.T)

    def body(rb_ref, oh_ref, o_ref):
        o_ref[...] = sum(jnp.dot(t, oh_ref[...], preferred_element_type=F32) for t in _split3(rb_ref[...]))

    return pl.pallas_call(
        body, name="bias_table", grid=(n // tn,),
        in_specs=[pl.BlockSpec((nh, 128), lambda i: (0, 0)), pl.BlockSpec((128, tn), lambda i: (0, i))],
        out_specs=pl.BlockSpec((nh, tn), lambda i: (0, i)),
        out_shape=jax.ShapeDtypeStruct((nh, n), F32),
        compiler_params=_params(("parallel",)),
    )(rb, oh)


def bias_table_grad(db0, db1):
    nh, n = db0.shape
    oh = jnp.asarray(_bucket_onehot(), BF16)
    tn = 4096

    def body(a_ref, b_ref, oh_ref, o_ref):
        @pl.when(pl.program_id(0) == 0)
        def _():
            o_ref[...] = jnp.zeros_like(o_ref)

        o_ref[...] += sum(lax.dot_general(t, oh_ref[...], (((1,), (1,)), ((), ())), preferred_element_type=F32)
                          for t in _split3(a_ref[...] + b_ref[...]))

    blk = pl.BlockSpec((nh, tn), lambda i: (0, i))
    return pl.pallas_call(
        body, name="bias_table_grad", grid=(n // tn,),
        in_specs=[blk, blk, pl.BlockSpec((128, tn), lambda i: (0, i))],
        out_specs=pl.BlockSpec((nh, 128), lambda i: (0, 0)),
        out_shape=jax.ShapeDtypeStruct((nh, 128), F32),
        compiler_params=_params(("arbitrary",)),
    )(db0, db1, oh)


def _owner_view(ref, name, d):
    if name == 'a_norm':
        return ref.at[d]
    if name in COL_SHARDED:
        n = ref.shape[2] // N_DEV
        return ref.at[:, :, pl.ds(pl.multiple_of(d * n, 128), n)]
    return ref.at[:, d]


def _place():
    return lax.axis_index("x"), lax.axis_index("y"), lax.axis_index("c")


def _dev(p):
    return 4 * p[0] + 2 * p[1] + p[2]


def _remote(src, dst, send_sem, recv_sem, to):
    return pltpu.make_async_remote_copy(src_ref=src, dst_ref=dst, send_sem=send_sem, recv_sem=recv_sem,
                                        device_id=to, device_id_type=MESH)


def _dma_sems(*shapes):
    return [pltpu.SemaphoreType.DMA(sh) for sh in shapes]


def comm_call(name, build, ins, out_shapes, sems, aliases=None):
    n_in, n_out = len(ins), len(out_shapes)

    def body(*refs):
        copies = build(refs[:n_in], refs[n_in:n_in + n_out], *refs[n_in + n_out:])
        for cp in copies:
            cp.start()
        for cp in copies:
            cp.wait()

    hbm = pl.BlockSpec(memory_space=pl.ANY)
    return pl.pallas_call(
        body, name=name, in_specs=[hbm] * n_in, out_specs=[hbm] * n_out, out_shape=list(out_shapes),
        scratch_shapes=sems, input_output_aliases=aliases or {},
    )(*ins)


def all_gather_weights(names, shards, full_shapes):
    n = len(names)

    def body(*refs):
        ins, outs = refs[:n], refs[n:2 * n]
        send_sems, recv_sems, local_sems = refs[2 * n:]
        x, y, c = _place()
        me, sibling = (x, y, c), (x, y, 1 - c)
        chips = [(1 - x, y), (x, 1 - y), (1 - x, 1 - y)]

        def copy(t, k, block, to, src=None):
            dst = _owner_view(outs[t], names[t], _dev(block))
            return _remote(dst if src is None else src, dst, send_sems.at[t, k], recv_sems.at[t, k], to)

        mine = [pltpu.make_async_copy(ins[t], _owner_view(outs[t], names[t], _dev(me)), local_sems.at[t])
                for t in range(n)]
        for cp in mine:
            cp.start()
        first = []
        for t in range(n):
            first.append(copy(t, 0, me, sibling, src=ins[t]))
            first += [copy(t, 1 + j, me, (*chip, c), src=ins[t]) for j, chip in enumerate(chips)]
        for cp in first:
            cp.start()
        passed = []
        for j, chip in enumerate(chips):
            for t in range(n):
                copy(t, 1 + j, (*chip, c), me).wait_recv()
                fwd = copy(t, 4 + j, (*chip, c), sibling)
                fwd.start()
                passed.append(fwd)
        for t in range(n):
            copy(t, 0, sibling, me).wait_recv()
            for j, chip in enumerate(chips):
                copy(t, 4 + j, (*chip, 1 - c), me).wait_recv()
        for cp in first + passed:
            cp.wait_send()
        for cp in mine:
            cp.wait()

    hbm = pl.BlockSpec(memory_space=pl.ANY)
    return pl.pallas_call(
        body, name="all_gather_layer0",
        in_specs=[hbm] * n, out_specs=[hbm] * n,
        out_shape=[jax.ShapeDtypeStruct(full_shapes[t], shards[t].dtype) for t in range(n)],
        scratch_shapes=_dma_sems((n, 7), (n, 7), (n,)),
    )(*shards)


def ag_direct(names, shards, full_shapes):
    n = len(names)

    def build(ins, outs, send_sems, recv_sems, local_sems):
        x, y, c = _place()
        peers = [(x, y, 1 - c), (1 - x, y, c), (x, 1 - y, c), (1 - x, 1 - y, c)]
        copies = []
        for t in range(n):
            dst = _owner_view(outs[t], names[t], _dev((x, y, c)))
            copies.append(pltpu.make_async_copy(ins[t], dst, local_sems.at[t]))
            copies += [_remote(ins[t], dst, send_sems.at[t, k], recv_sems.at[t, k], to) for k, to in enumerate(peers)]
        return copies

    return Carry(build, shards, [jax.ShapeDtypeStruct(full_shapes[t], shards[t].dtype) for t in range(n)],
                 _dma_sems((n, 4), (n, 4), (n,)))


def ag_forward(tag, names, partial):
    n = len(names)

    def build(ins, outs, send_sems, recv_sems):
        del ins
        x, y, c = _place()
        copies = []
        for t in range(n):
            for k, chip in enumerate([(1 - x, y), (x, 1 - y), (1 - x, 1 - y)]):
                view = _owner_view(outs[t], names[t], _dev((*chip, c)))
                copies.append(_remote(view, view, send_sems.at[t, k], recv_sems.at[t, k], (x, y, 1 - c)))
        return copies

    return comm_call(f"all_gather_forward_{tag}", build, partial, [jax.ShapeDtypeStruct(p.shape, p.dtype) for p in partial],
                     _dma_sems((n, 3), (n, 3)), aliases={t: t for t in range(n)})


def sibling_exchange(tag, names, grads, part_shapes):
    n = len(names)

    def build(ins, outs, send_sems, recv_sems):
        x, y, c = _place()
        return [_remote(_owner_view(ins[t], names[t], 2 * q + 1 - c), outs[t].at[q], send_sems.at[t, q],
                        recv_sems.at[t, q], (x, y, 1 - c)) for t in range(n) for q in range(4)]

    return comm_call(f"rs_sibling_exchange_{tag}", build, grads,
                     [jax.ShapeDtypeStruct((4,) + part_shapes[t], BF16) for t in range(n)], _dma_sems((n, 4), (n, 4)))


def chip_exchange(names, parts, part_shapes):
    n = len(names)

    def build(ins, outs, send_sems, recv_sems):
        x, y, c = _place()
        chips = [(1 - x, y), (x, 1 - y), (1 - x, 1 - y)]
        return [_remote(ins[t].at[2 * chip[0] + chip[1]], outs[t].at[k], send_sems.at[t, k], recv_sems.at[t, k],
                        (*chip, c)) for t in range(n) for k, chip in enumerate(chips)]

    return Carry(build, parts, [jax.ShapeDtypeStruct((3,) + part_shapes[t], BF16) for t in range(n)],
                 _dma_sems((n, 3), (n, 3)))


def all_gather_rows(x):
    r, w = x.shape

    def body(x_ref, out_ref, send_sems, recv_sems, local_sem):
        px, py, pc = _place()
        me = 4 * px + 2 * py + pc
        mine = pltpu.make_async_copy(x_ref, out_ref.at[me], local_sem)
        mine.start()
        copies = []
        for k in range(1, N_DEV):
            peer = (px ^ (k >> 2), py ^ ((k >> 1) & 1), pc ^ (k & 1))
            copies.append(pltpu.make_async_remote_copy(
                src_ref=x_ref, dst_ref=out_ref.at[me], send_sem=send_sems.at[k - 1], recv_sem=recv_sems.at[k - 1],
                device_id=peer, device_id_type=MESH))
        for cp in copies:
            cp.start()
        for k in range(1, N_DEV):
            peer_idx = me ^ k
            pltpu.make_async_remote_copy(
                src_ref=x_ref, dst_ref=out_ref.at[peer_idx], send_sem=send_sems.at[k - 1],
                recv_sem=recv_sems.at[k - 1], device_id=(px, py, pc), device_id_type=MESH).wait_recv()
        for cp in copies:
            cp.wait_send()
        mine.wait()

    vmem = pl.BlockSpec(memory_space=pltpu.VMEM)
    return pl.pallas_call(
        body, name="all_gather_small_grads",
        in_specs=[vmem], out_specs=vmem,
        out_shape=jax.ShapeDtypeStruct((N_DEV, r, w), x.dtype),
        scratch_shapes=[pltpu.SemaphoreType.DMA((N_DEV - 1,)), pltpu.SemaphoreType.DMA((N_DEV - 1,)),
                        pltpu.SemaphoreType.DMA],
    )(x)


def _adamw(w, g, m, v):
    m = ADAM_B1 * m + (1.0 - ADAM_B1) * g
    v = ADAM_B2 * v + (1.0 - ADAM_B2) * (g * g)
    m_hat = m / (1.0 - ADAM_B1 ** ADAM_STEP)
    v_hat = v / (1.0 - ADAM_B2 ** ADAM_STEP)
    return -ADAM_LR * (m_hat / (jnp.sqrt(v_hat) + ADAM_EPS) + ADAM_WD * w), m, v


def sibling_sum(name, col, grads, recv, core):
    _, nl, rows, cols = recv.shape
    tr = _tile(rows, 512)
    rspec = pl.BlockSpec((None, None, tr, cols), lambda q, l, i, c_ref: (q, l, i, 0))
    if col:
        gspec = pl.BlockSpec((None, tr, cols), lambda q, l, i, c_ref: (l, i, 2 * q + c_ref[0]))
    else:
        gspec = pl.BlockSpec((None, None, tr, cols), lambda q, l, i, c_ref: (l, 2 * q + c_ref[0], i, 0))

    def body(c_ref, g_ref, r_ref, o_ref):
        del c_ref
        o_ref[...] = (g_ref[...].astype(F32) + r_ref[...].astype(F32)).astype(BF16)

    return pl.pallas_call(
        body, name=name,
        grid_spec=pltpu.PrefetchScalarGridSpec(num_scalar_prefetch=1, grid=(4, nl, rows // tr),
                                               in_specs=[gspec, rspec], out_specs=rspec),
        out_shape=jax.ShapeDtypeStruct(recv.shape, BF16),
        compiler_params=_params(("parallel", "parallel", "parallel")),
    )(core.reshape(1), grads, recv)


def reduce_adamw(name, parts, recv, chip, w, m, v, l0, prev):
    _, nl, rows, cols = parts.shape
    tr = _tile(rows, 256)

    def body(q_ref, p_ref, r_ref, w_ref, m_ref, v_ref, *rest):
        del q_ref
        g_out, d_out, m_out, v_out = rest[-4:]
        g = ((p_ref[...].astype(F32) + r_ref[0].astype(F32)) + r_ref[1].astype(F32)) + r_ref[2].astype(F32)
        d, mn, vn = _adamw(w_ref[...], g, m_ref[...], v_ref[...])
        g_out[...] = g
        d_out[...] = d
        m_out[...] = mn
        v_out[...] = vn

    blk = pl.BlockSpec((None, tr, cols), lambda l, i, q_ref: (l0 + l, i, 0))
    prev = list(prev) if prev else []
    return pl.pallas_call(
        body, name=name,
        grid_spec=pltpu.PrefetchScalarGridSpec(
            num_scalar_prefetch=1, grid=(nl, rows // tr),
            in_specs=[pl.BlockSpec((None, None, tr, cols), lambda l, i, q_ref: (q_ref[0], l, i, 0)),
                      pl.BlockSpec((3, None, tr, cols), lambda l, i, q_ref: (0, l, i, 0)), blk, blk, blk]
            + [pl.BlockSpec(memory_space=pl.ANY)] * len(prev),
            out_specs=[blk] * 4),
        out_shape=[jax.ShapeDtypeStruct(w.shape, F32)] * 4,
        input_output_aliases={6 + i: i for i in range(len(prev))},
        compiler_params=_params(("parallel", "parallel")),
    )(chip.reshape(1), parts, recv, w, m, v, *prev)


def small_adamw(name, gathered, w, m, v):
    _, r, c = gathered.shape

    def body(ga_ref, w_ref, m_ref, v_ref, g_out, d_out, m_out, v_out):
        g = ga_ref[0]
        for d in range(1, N_DEV):
            g = g + ga_ref[d]
        dl, mn, vn = _adamw(w_ref[...], g, m_ref[...], v_ref[...])
        g_out[...] = g
        d_out[...] = dl
        m_out[...] = mn
        v_out[...] = vn

    return pl.pallas_call(
        body, name=name,
        out_shape=[jax.ShapeDtypeStruct((r, c), F32)] * 4,
        compiler_params=_params(),
    )(gathered, w, m, v)


def _add(acc, *ex):
    return (acc + ex[0],)


def local_step(x, target, small, ex):
    s, d = x.shape
    n_a, n_b = small['a_norm'].shape[0], small['b_norm'].shape[0]
    sg = {}
    gb = {}

    def fwd_mm(name, a, wname, layer, epilogue, extras, out_dtypes, **kw):
        return mm_nn(name, a, *ex.weight(wname, layer), epilogue, extras, out_dtypes, **kw)

    def dx_mm(name, dy, wname, layer, epilogue, extras, out_dtypes, **kw):
        return mm_nt(name, dy, *ex.weight(wname, layer), epilogue, extras, out_dtypes, **kw)

    def dw_mm(name, a, dy, wname, layer, **kw):
        key, slab, shape = ex.grad(wname, layer)
        gb[key] = mm_tn(name, a, dy, gb.get(key), shape, slab, **kw)

    plain = lambda acc: (acc,)
    plus_col = lambda acc, b: (acc + b,)

    bias_flat = bias_table(small['rel_bias'])
    bias_t = bias_flat.reshape(2, 8, WINDOW, 2 * WINDOW).transpose(0, 3, 1, 2).reshape(2, 2 * WINDOW, 8 * WINDOW)
    sink_rows = [jnp.repeat(small['b_sinks'][j], WINDOW).reshape(2, 1, 8 * WINDOW) for j in range(n_b)]

    def mlp_fwd(h, layer):
        n2 = rms_fwd(f"mlp_norm_fwd{layer}", h, small['mlp_norm'][layer])
        u, a = fwd_mm(f"mlp_up_fwd{layer}", n2, 'mlp_up', layer,
                      lambda acc: (acc, jnp.square(jnp.maximum(acc, 0.0))), (), (BF16, BF16))
        (h2,) = fwd_mm(f"mlp_down_fwd{layer}", a, 'mlp_down', layer, _add, (h,), (F32,))
        return h2, (n2, u, a)

    h = x
    saved = []
    for l in range(n_a):
        n1 = rms_fwd(f"a_norm_fwd{l}", h, small['a_norm'][l])
        (qkvt,) = fwd_mm(f"a_qkv_fwd{l}", n1, 'a_wqkv', l, plain, (), (BF16,), out_t=True)
        qkvt = qkvt.reshape(3 * d // HEAD_DIM, HEAD_DIM, s)
        o_t, rtab, carried = sb_fwd(f"sb_fwd{l}", qkvt, ex.fwd_carry(l))
        ex.fwd_done(l, carried)
        o_t = o_t.reshape(d, s)
        (h_mid,) = fwd_mm(f"a_wo_fwd{l}", o_t, 'a_wo', l, _add, (h,), (F32,), a_t=True)
        h_out, mlp_saved = mlp_fwd(h_mid, l)
        saved.append((h, n1, qkvt, o_t, rtab, h_mid, mlp_saved))
        h = h_out
    h_kv = h
    nkv = rms_fwd("kv_norm_fwd", h, small['kv_norm'])
    (kvt,) = fwd_mm("kv_fwd", nkv, 'w_kv', 0, plus_col, (small['b_kv'].reshape(-1, 1),), (BF16,), out_t=True)
    kvt = kvt.reshape(2, 2, HEAD_DIM, s)
    kpt, vpt = (jnp.pad(t, ((0, 0), (0, 0), (WINDOW, 0))) for t in (kvt[0], kvt[1]))
    for j in range(n_b):
        layer = n_a + j
        n1 = rms_fwd(f"b_norm_fwd{j}", h, small['b_norm'][j])
        (qbt,) = fwd_mm(f"b_q_fwd{j}", n1, 'b_wq', j, plus_col, (small['b_bq'][j].reshape(-1, 1),), (BF16,),
                        out_t=True)
        o_t = swa_fwd(f"swa_fwd{j}", qbt, kpt, vpt, bias_t, sink_rows[j])
        (h_mid,) = fwd_mm(f"b_wo_fwd{j}", o_t, 'b_wo', j, lambda acc, hh, b: (acc + hh + b,),
                          (h, small['b_bo'][j].reshape(1, -1)), (F32,), a_t=True)
        h_out, mlp_saved = mlp_fwd(h_mid, layer)
        saved.append((h, n1, qbt, o_t, h_mid, mlp_saved))
        h = h_out

    dh, dhb, dg_final, loss_b = loss_head(h, small['final_norm'], target)
    sg['final_norm'] = dg_final[0]
    sg['mlp_norm'] = [None] * (n_a + n_b)

    def mlp_bwd(dh, dhb, h_mid, mlp_saved, layer):
        n2, u, a = mlp_saved
        (du,) = dx_mm(f"mlp_down_dx{layer}", dhb, 'mlp_down', layer,
                      lambda acc, uu: (acc * (2.0 * jnp.maximum(uu.astype(F32), 0.0)),), (u,), (BF16,))
        dw_mm(f"mlp_down_dw{layer}", a, dhb, 'mlp_down', layer)
        (dn2,) = dx_mm(f"mlp_up_dx{layer}", du, 'mlp_up', layer, plain, (), (F32,))
        dw_mm(f"mlp_up_dw{layer}", n2, du, 'mlp_up', layer)
        dh2, dh2b, dg, cs = rms_bwd(f"mlp_norm_bwd{layer}", h_mid, small['mlp_norm'][layer], dn2, dh)
        sg['mlp_norm'][layer] = dg[0]
        return dh2, dh2b, cs

    dkp = jnp.zeros(kpt.shape, F32)
    dvp = jnp.zeros(vpt.shape, F32)
    sg['b_norm'], sg['b_bq'], sg['b_bo'], sg['b_sinks'] = [None] * n_b, [None] * n_b, [None] * n_b, [None] * n_b
    dbias = [None] * n_b
    for j in reversed(range(n_b)):
        layer = n_a + j
        h_in, n1, qbt, o_t, h_mid, mlp_saved = saved[layer]
        dh, dhb, cs = mlp_bwd(dh, dhb, h_mid, mlp_saved, layer)
        sg['b_bo'][j] = cs[0]
        (do_t,) = dx_mm(f"b_wo_dx{j}", dhb, 'b_wo', j, plain, (), (BF16,), out_t=True)
        dw_mm(f"b_wo_dw{j}", o_t, dhb, 'b_wo', j, x_t=True)
        dq_t, dkp, dvp, dbias[j], dsink = swa_bwd(f"swa_bwd{j}", qbt, kpt, vpt, bias_t, sink_rows[j], do_t, dkp, dvp)
        sg['b_sinks'][j] = colsum(f"sink_grad{j}", dsink.reshape(16, WINDOW).T)[0]
        sg['b_bq'][j] = rowsum(f"b_bq_grad{j}", dq_t)
        (dn1,) = dx_mm(f"b_q_dx{j}", dq_t, 'b_wq', j, plain, (), (F32,), a_t=True)
        dw_mm(f"b_q_dw{j}", n1, dq_t, 'b_wq', j, dy_t=True)
        dh, dhb, dg, _ = rms_bwd(f"b_norm_bwd{j}", h_in, small['b_norm'][j], dn1, dh)
        sg['b_norm'][j] = dg[0]
    unt = lambda t: t.reshape(2, 2 * WINDOW, 8, WINDOW).transpose(0, 2, 3, 1).reshape(bias_flat.shape)
    sg['rel_bias'] = bias_table_grad(unt(dbias[0]), unt(dbias[1]))[:, :N_BUCKETS].T

    dkv_t = jnp.concatenate([dkp[:, :, WINDOW:], dvp[:, :, WINDOW:]], axis=0).reshape(-1, s)
    sg['b_kv'] = rowsum("b_kv_grad", dkv_t)
    dkvb = dkv_t.astype(BF16)
    (dnkv,) = dx_mm("kv_dx", dkvb, 'w_kv', 0, plain, (), (F32,), a_t=True)
    dw_mm("kv_dw", nkv, dkvb, 'w_kv', 0, dy_t=True)
    dh, dhb, dg, _ = rms_bwd("kv_norm_bwd", h_kv, small['kv_norm'], dnkv, dh)
    sg['kv_norm'] = dg[0]

    sg['a_norm'] = [None] * n_a
    for l in reversed(range(n_a)):
        h_in, n1, qkvt, o_t, rtab, h_mid, mlp_saved = saved[l]
        dh, dhb, _ = mlp_bwd(dh, dhb, h_mid, mlp_saved, l)
        (do_t,) = dx_mm(f"a_wo_dx{l}", dhb, 'a_wo', l, plain, (), (BF16,), out_t=True)
        dw_mm(f"a_wo_dw{l}", o_t, dhb, 'a_wo', l, x_t=True)
        dq_t, dk_t, dv_t, carried = sb_bwd(f"sb_bwd{l}", qkvt, do_t.reshape(d // HEAD_DIM, HEAD_DIM, s), rtab,
                                           ex.bwd_carry(l, gb))
        ex.bwd_done(l, carried)
        dqkv_t = jnp.concatenate([dq_t, dk_t, dv_t], axis=0).reshape(3 * d, s)
        (dn1,) = dx_mm(f"a_qkv_dx{l}", dqkv_t, 'a_wqkv', l, plain, (), (F32,), a_t=True)
        dw_mm(f"a_qkv_dw{l}", n1, dqkv_t, 'a_wqkv', l, dy_t=True)
        dh, dhb, dg, _ = rms_bwd(f"a_norm_bwd{l}", h_in, small['a_norm'][l], dn1, dh)
        sg['a_norm'][l] = dg[0]

    small_grads = {
        'a_norm': jnp.stack(sg['a_norm']), 'kv_norm': sg['kv_norm'], 'b_kv': sg['b_kv'],
        'b_norm': jnp.stack(sg['b_norm']), 'b_bq': jnp.stack(sg['b_bq']), 'b_sinks': jnp.stack(sg['b_sinks']),
        'b_bo': jnp.stack(sg['b_bo']), 'rel_bias': sg['rel_bias'], 'mlp_norm': jnp.stack(sg['mlp_norm']),
        'final_norm': sg['final_norm'],
    }
    return loss_b, dh, gb, small_grads


def _full_shape(name, shard_shape):
    if name in COL_SHARDED:
        return shard_shape[:2] + (N_DEV * shard_shape[2],)
    nl, r, n = shard_shape
    return (nl, N_DEV, r, n)


def _as_w3_shape(name, shard_shape):
    full = _full_shape(name, shard_shape)
    return full if name in COL_SHARDED else (full[0], full[1] * full[2], full[3])


def _as_w3(name, full):
    if name in COL_SHARDED:
        return full
    nl, nd, r, n = full.shape
    return full.reshape(nl, nd * r, n)


AG_GROUPS = {
    0: (('a_wqkv', 0, 1),),
    1: (('a_wo', 0, 2), ('mlp_up', 0, 2), ('mlp_down', 0, 2), ('a_wqkv', 1, 1)),
    2: (('mlp_up', 2, 2), ('mlp_down', 2, 2), ('b_wq', 0, 2), ('b_wo', 0, 2), ('w_kv', 0, 1)),
}
RS_GROUPS = {
    'A': (('mlp_up', 1, 3), ('mlp_down', 1, 3), ('b_wq', 0, 2), ('b_wo', 0, 2), ('w_kv', 0, 1)),
    'B': (('a_wqkv', 1, 1), ('a_wo', 1, 1), ('mlp_up', 0, 1), ('mlp_down', 0, 1)),
    'C': (('a_wqkv', 0, 1), ('a_wo', 0, 1)),
}


class _Exchanges:
    def __init__(self, full0, shards, core, chip, w3, m3, v3):
        self.wbuf = {0: {n: _as_w3(n, full0[n]) for n, _, _ in AG_GROUPS[0]}}
        self.shards, self.core, self.chip = shards, core, chip
        self.w3, self.m3, self.v3 = w3, m3, v3
        self.shard_dims = {n: w3[n].shape[1:] for n in BIG}
        self.parts = {}
        self.out = {}

    def weight(self, name, layer):
        for group, members in AG_GROUPS.items():
            for n, l0, nl in members:
                if n == name and l0 <= layer < l0 + nl:
                    return self.wbuf[group][name], layer - l0
        raise KeyError((name, layer))

    def fwd_carry(self, layer):
        names = [n for n, _, _ in AG_GROUPS[layer + 1]]
        shards = [self.shards[layer + 1][n] for n in names]
        return ag_direct(names, shards, [_full_shape(n, sh.shape) for n, sh in zip(names, shards)])

    def fwd_done(self, layer, carried):
        names = [n for n, _, _ in AG_GROUPS[layer + 1]]
        self.wbuf[layer + 1] = {n: _as_w3(n, f) for n, f in zip(names, ag_forward(layer + 1, names, list(carried)))}

    def grad(self, name, layer):
        for group, members in RS_GROUPS.items():
            for n, l0, nl in members:
                if n == name and l0 <= layer < l0 + nl:
                    return (group, name), layer - l0, _as_w3_shape(name, (nl,) + self.shard_dims[name])
        raise KeyError((name, layer))

    def _sibling_stage(self, group, gb):
        names = [n for n, _, _ in RS_GROUPS[group]]
        shapes = [(nl,) + self.shard_dims[n] for n, _, nl in RS_GROUPS[group]]
        gfull = [gb[(group, n)].reshape(_full_shape(n, sh)) for n, sh in zip(names, shapes)]
        recv = sibling_exchange(group, names, gfull, shapes)
        self.parts[group] = [sibling_sum(f"rs_sibling_sum_{group}_{n}", n in COL_SHARDED, g, r, self.core)
                             for n, g, r in zip(names, gfull, recv)]
        return names, shapes

    def bwd_carry(self, layer, gb):
        names, shapes = self._sibling_stage('A' if layer == 1 else 'B', gb)
        return chip_exchange(names, self.parts['A' if layer == 1 else 'B'], shapes)

    def bwd_done(self, layer, carried):
        self._adamw('A' if layer == 1 else 'B', carried)

    def finish(self, gb):
        names, shapes = self._sibling_stage('C', gb)
        ce = chip_exchange(names, self.parts['C'], shapes)
        self._adamw('C', comm_call("rs_chip_exchange_C", ce.build, ce.ins, ce.out_shapes, ce.sems))
        return self.out

    def _adamw(self, group, recv2):
        for (n, l0, _), p, r in zip(RS_GROUPS[group], self.parts[group], recv2):
            self.out[n] = reduce_adamw(f"adamw_{group}_{n}", p, r, self.chip, self.w3[n], self.m3[n], self.v3[n],
                                       l0, self.out.get(n))


def _pack_small(vals):
    flat = jnp.concatenate([vals[n].reshape(-1).astype(F32) for n in SMALL] + [vals['loss'].reshape(-1)])
    rows = -(-flat.shape[0] // 1024) * 8
    return jnp.pad(flat, (0, rows * 128 - flat.shape[0])).reshape(rows, 128)


def _unpack_small(packed, shapes):
    flat = packed.reshape(-1)
    out, off = {}, 0
    for n in SMALL + ['loss']:
        size = int(np.prod(shapes[n]))
        out[n] = flat[off:off + size].reshape(shapes[n])
        off += size
    return out


def kernel(x, a_norm, a_wqkv, a_wo, kv_norm, w_kv, b_kv, b_norm, b_wq, b_bq, b_sinks, b_wo, b_bo, rel_bias, mlp_norm, mlp_up, mlp_down, final_norm, loss_target, m_a_norm, m_a_wqkv, m_a_wo, m_kv_norm, m_w_kv, m_b_kv, m_b_norm, m_b_wq, m_b_bq, m_b_sinks, m_b_wo, m_b_bo, m_rel_bias, m_mlp_norm, m_mlp_up, m_mlp_down, m_final_norm, v_a_norm, v_a_wqkv, v_a_wo, v_kv_norm, v_w_kv, v_b_kv, v_b_norm, v_b_wq, v_b_bq, v_b_sinks, v_b_wo, v_b_bo, v_rel_bias, v_mlp_norm, v_mlp_up, v_mlp_down, v_final_norm):
    w = dict(a_norm=a_norm, a_wqkv=a_wqkv, a_wo=a_wo, kv_norm=kv_norm, w_kv=w_kv, b_kv=b_kv, b_norm=b_norm,
             b_wq=b_wq, b_bq=b_bq, b_sinks=b_sinks, b_wo=b_wo, b_bo=b_bo, rel_bias=rel_bias, mlp_norm=mlp_norm,
             mlp_up=mlp_up, mlp_down=mlp_down, final_norm=final_norm)
    m = dict(a_norm=m_a_norm, a_wqkv=m_a_wqkv, a_wo=m_a_wo, kv_norm=m_kv_norm, w_kv=m_w_kv, b_kv=m_b_kv,
             b_norm=m_b_norm, b_wq=m_b_wq, b_bq=m_b_bq, b_sinks=m_b_sinks, b_wo=m_b_wo, b_bo=m_b_bo,
             rel_bias=m_rel_bias, mlp_norm=m_mlp_norm, mlp_up=m_mlp_up, mlp_down=m_mlp_down, final_norm=m_final_norm)
    v = dict(a_norm=v_a_norm, a_wqkv=v_a_wqkv, a_wo=v_a_wo, kv_norm=v_kv_norm, w_kv=v_w_kv, b_kv=v_b_kv,
             b_norm=v_b_norm, b_wq=v_b_wq, b_bq=v_b_bq, b_sinks=v_b_sinks, b_wo=v_b_wo, b_bo=v_b_bo,
             rel_bias=v_rel_bias, mlp_norm=v_mlp_norm, mlp_up=v_mlp_up, mlp_down=v_mlp_down, final_norm=v_final_norm)
    px, py, pc = _place()
    me = 4 * px + 2 * py + pc
    chip = (2 * px + py).astype(jnp.int32)
    core = pc.astype(jnp.int32)

    as3 = lambda t: t[None] if t.ndim == 2 else t
    w3, m3, v3 = ({n: as3(src[n]) for n in BIG} for src in (w, m, v))
    shards = {g: {n: w3[n][l0:l0 + nl].astype(BF16) for n, l0, nl in members} for g, members in AG_GROUPS.items()}
    an_pad = jnp.zeros((8, 128), F32).at[:a_norm.shape[0]].set(a_norm)
    names0 = [n for n, _, _ in AG_GROUPS[0]]
    full0 = all_gather_weights(names0 + ['a_norm'], [shards[0][n] for n in names0] + [an_pad],
                               [_full_shape(n, shards[0][n].shape) for n in names0] + [(N_DEV, 8, 128)])
    full0 = dict(zip(names0 + ['a_norm'], full0))
    n_a = a_norm.shape[0]
    small = {n: w[n] for n in SMALL}
    small['a_norm'] = full0['a_norm'][:, :n_a].transpose(1, 0, 2).reshape(n_a, -1)

    ex = _Exchanges(full0, shards, core, chip, w3, m3, v3)
    loss_b, grad_x, gb, sgrads = local_step(x[0], loss_target[0], small, ex)
    out = {n: [t.reshape(w[n].shape) for t in bufs] for n, bufs in ex.finish(gb).items()}

    sgrads['loss'] = loss_b[0, :1]
    gathered = all_gather_rows(_pack_small(sgrads))
    shapes = {n: w[n].shape for n in SMALL}
    shapes['a_norm'] = (n_a, a_norm.shape[1] * N_DEV)
    shapes['loss'] = (1,)
    zeros1 = jnp.zeros((1,), F32)

    def packed(src):
        vals = {n: src[n] for n in SMALL}
        vals['a_norm'] = jnp.zeros(shapes['a_norm'], F32)
        vals['loss'] = zeros1
        return _pack_small(vals)

    sm = small_adamw("adamw_small", gathered, packed(w), packed(m), packed(v))
    sm = [_unpack_small(t, shapes) for t in sm]
    g_an = lax.dynamic_slice_in_dim(sm[0]['a_norm'], me * a_norm.shape[1], a_norm.shape[1], axis=1)
    pad = lambda t: jnp.zeros((8, 128), F32).at[:n_a].set(t)
    gathered_an = jnp.zeros((N_DEV, 8, 128), F32).at[0].set(pad(g_an))
    an = small_adamw("adamw_a_norm", gathered_an, pad(a_norm), pad(m_a_norm), pad(v_a_norm))
    for i in range(4):
        sm[i]['a_norm'] = an[i][:n_a]
    for n in BIG:
        for i in range(4):
            sm[i][n] = out[n][i]
    loss = sm[0]['loss'][0]
    return (loss, grad_x[None], *[sm[0][n] for n in WEIGHTS], *[sm[1][n] for n in WEIGHTS],
            *[sm[2][n] for n in WEIGHTS], *[sm[3][n] for n in WEIGHTS])
```

```python
import math

import numpy as np
import jax
import jax.numpy as jnp
from jax import lax
from jax.experimental import pallas as pl
from jax.experimental.pallas import tpu as pltpu

F32 = jnp.float32
BF16 = jnp.bfloat16
MESH = pl.DeviceIdType.MESH

N_DEV = 8
HEAD_DIM = 64
WINDOW = 128
N_BUCKETS = 32
EPS = 1e-5
NEG_INF = -1e30
Q_SCALE = 1.0 / math.sqrt(HEAD_DIM)
LOG2E = 1.4426950408889634

ADAM_LR, ADAM_B1, ADAM_B2, ADAM_EPS, ADAM_WD, ADAM_STEP = 0.001, 0.9, 0.999, 1e-08, 0.01, 10

SB_BQ = 512
SB_BK = 128
SB_DEAD = 160.0
SB_UNSEEN = 1e30
ROW_TILE = 512
VMEM_LIMIT = 56 * 1024 * 1024

WEIGHTS = ['a_norm', 'a_wqkv', 'a_wo', 'kv_norm', 'w_kv', 'b_kv', 'b_norm', 'b_wq', 'b_bq', 'b_sinks', 'b_wo',
           'b_bo', 'rel_bias', 'mlp_norm', 'mlp_up', 'mlp_down', 'final_norm']
BIG = ['a_wqkv', 'a_wo', 'w_kv', 'b_wq', 'b_wo', 'mlp_up', 'mlp_down']
COL_SHARDED = ('a_wqkv', 'mlp_up')
SMALL = ['a_norm', 'kv_norm', 'b_kv', 'b_norm', 'b_bq', 'b_sinks', 'b_bo', 'rel_bias', 'mlp_norm', 'final_norm']


def _params(sem=None):
    return pltpu.CompilerParams(dimension_semantics=sem, vmem_limit_bytes=VMEM_LIMIT)


def _pick(n, cands):
    for c in cands:
        if n % c == 0:
            return c
    raise ValueError(n)


def _tile(n, want):
    return n if n <= want else _pick(n, (want, want // 2, want // 4))


MM_TILE_BUDGET = 36 * 1024 * 1024


def _row_tile(m, contraction, cols, streams):
    weight = 2 * contraction * cols * 2
    for rows in (2048, 1024, 512):
        if m % rows == 0 and weight + 2 * rows * (2 * contraction + cols * sum(streams)) <= MM_TILE_BUDGET:
            return rows
    return _tile(m, 512)


def mm_nn(name, a, w3, layer, epilogue, extras, out_dtypes, a_t=False, out_t=False):
    k, m = a.shape if a_t else a.shape[::-1]
    _, kw, n = w3.shape
    assert kw == k
    tn = _tile(n, 1024)
    tm = _row_tile(m, k, tn, [jnp.dtype(t).itemsize for t in out_dtypes]
                   + [e.dtype.itemsize for e in extras if e.size == m * n])
    ne, no = len(extras), len(out_dtypes)
    a_dim = 0 if a_t else 1

    def body(a_ref, w_ref, *rest):
        ex, outs = rest[:ne], rest[ne:ne + no]
        if out_t:
            acc = lax.dot_general(w_ref[...], a_ref[...], (((0,), (a_dim,)), ((), ())), preferred_element_type=F32)
        else:
            acc = lax.dot_general(a_ref[...], w_ref[...], (((a_dim,), (0,)), ((), ())), preferred_element_type=F32)
        for o, r in zip(outs, epilogue(acc, *[e[...] for e in ex])):
            o[...] = r.astype(o.dtype)

    if out_t:
        tile = pl.BlockSpec((tn, tm), lambda i, j: (j, i))
        vec = pl.BlockSpec((tn, 1), lambda i, j: (j, 0))
        out_shape = (n, m)
    else:
        tile = pl.BlockSpec((tm, tn), lambda i, j: (i, j))
        vec = pl.BlockSpec((1, tn), lambda i, j: (0, j))
        out_shape = (m, n)
    a_spec = pl.BlockSpec((k, tm), lambda i, j: (0, i)) if a_t else pl.BlockSpec((tm, k), lambda i, j: (i, 0))
    return pl.pallas_call(
        body, name=name, grid=(m // tm, n // tn),
        in_specs=[a_spec, pl.BlockSpec((None, k, tn), lambda i, j: (layer, 0, j))]
        + [tile if e.shape == out_shape else vec for e in extras],
        out_specs=[tile] * no,
        out_shape=[jax.ShapeDtypeStruct(out_shape, d) for d in out_dtypes],
        compiler_params=_params(("parallel", "parallel")),
    )(a, w3, *extras)


def mm_nt(name, dy, w3, layer, epilogue, extras, out_dtypes, a_t=False, out_t=False):
    n, m = dy.shape if a_t else dy.shape[::-1]
    _, k, nw = w3.shape
    assert nw == n
    tko = _tile(k, 1024)
    tm = _row_tile(m, n, tko, [jnp.dtype(t).itemsize for t in out_dtypes] + [e.dtype.itemsize for e in extras])
    ne, no = len(extras), len(out_dtypes)
    a_dim = 0 if a_t else 1

    def body(a_ref, w_ref, *rest):
        ex, outs = rest[:ne], rest[ne:ne + no]
        if out_t:
            acc = lax.dot_general(w_ref[...], a_ref[...], (((1,), (a_dim,)), ((), ())), preferred_element_type=F32)
        else:
            acc = lax.dot_general(a_ref[...], w_ref[...], (((a_dim,), (1,)), ((), ())), preferred_element_type=F32)
        for o, v in zip(outs, epilogue(acc, *[e[...] for e in ex])):
            o[...] = v.astype(o.dtype)

    if out_t:
        tile = pl.BlockSpec((tko, tm), lambda i, ko: (ko, i))
        out_shape = (k, m)
    else:
        tile = pl.BlockSpec((tm, tko), lambda i, ko: (i, ko))
        out_shape = (m, k)
    a_spec = pl.BlockSpec((n, tm), lambda i, ko: (0, i)) if a_t else pl.BlockSpec((tm, n), lambda i, ko: (i, 0))
    return pl.pallas_call(
        body, name=name, grid=(m // tm, k // tko),
        in_specs=[a_spec, pl.BlockSpec((None, tko, n), lambda i, ko: (layer, ko, 0))] + [tile] * ne,
        out_specs=[tile] * no,
        out_shape=[jax.ShapeDtypeStruct(out_shape, d) for d in out_dtypes],
        compiler_params=_params(("parallel", "parallel")),
    )(dy, w3, *extras)


def mm_tn(name, x, dy, gbuf, shape, layer, x_t=False, dy_t=False):
    k, s = x.shape if x_t else x.shape[::-1]
    _, kw, n = shape
    assert kw == k and dy.shape == ((n, s) if dy_t else (s, n))
    tkk = _tile(k, 512)
    tn = _tile(n, 1024)

    def body(x_ref, dy_ref, *rest):
        g_out = rest[-1]
        g_out[...] = lax.dot_general(x_ref[...], dy_ref[...], (((1 if x_t else 0,), (1 if dy_t else 0,)), ((), ())),
                                     preferred_element_type=F32).astype(g_out.dtype)

    prev = [] if gbuf is None else [gbuf]
    x_spec = pl.BlockSpec((tkk, s), lambda ki, j: (ki, 0)) if x_t else pl.BlockSpec((s, tkk), lambda ki, j: (0, ki))
    dy_spec = pl.BlockSpec((tn, s), lambda ki, j: (j, 0)) if dy_t else pl.BlockSpec((s, tn), lambda ki, j: (0, j))
    return pl.pallas_call(
        body, name=name, grid=(k // tkk, n // tn),
        in_specs=[x_spec, dy_spec] + [pl.BlockSpec(memory_space=pl.ANY)] * len(prev),
        out_specs=pl.BlockSpec((None, tkk, tn), lambda ki, j: (layer, ki, j)),
        out_shape=jax.ShapeDtypeStruct(shape, BF16),
        input_output_aliases={2: 0} if prev else {},
        compiler_params=_params(("parallel", "parallel")),
    )(x, dy, *prev)


def rms_fwd(name, h, g):
    s, d = h.shape
    tr = _pick(s, (ROW_TILE, 256, 128))

    def body(h_ref, g_ref, o_ref):
        x = h_ref[...]
        r = lax.rsqrt(jnp.mean(x * x, axis=-1, keepdims=True) + EPS)
        o_ref[...] = (x * r * g_ref[...]).astype(o_ref.dtype)

    return pl.pallas_call(
        body, name=name, grid=(s // tr,),
        in_specs=[pl.BlockSpec((tr, d), lambda i: (i, 0)), pl.BlockSpec((1, d), lambda i: (0, 0))],
        out_specs=pl.BlockSpec((tr, d), lambda i: (i, 0)),
        out_shape=jax.ShapeDtypeStruct((s, d), BF16),
        compiler_params=_params(("parallel",)),
    )(h, g.reshape(1, d))


def rms_bwd(name, h, g, dn, dres):
    s, d = h.shape
    tr = _pick(s, (ROW_TILE, 256, 128))

    def body(h_ref, g_ref, dn_ref, dres_ref, dx_ref, dxb_ref, dg_ref, cs_ref):
        i = pl.program_id(0)
        x = h_ref[...]
        r = lax.rsqrt(jnp.mean(x * x, axis=-1, keepdims=True) + EPS)
        xh = x * r
        dn_ = dn_ref[...]
        dyg = dn_ * g_ref[...]
        dx = dres_ref[...] + r * (dyg - xh * jnp.mean(dyg * xh, axis=-1, keepdims=True))
        dx_ref[...] = dx
        dxb_ref[...] = dx.astype(BF16)

        @pl.when(i == 0)
        def _():
            dg_ref[...] = jnp.zeros_like(dg_ref)
            cs_ref[...] = jnp.zeros_like(cs_ref)

        dg_ref[...] += jnp.sum(dn_ * xh, axis=0, keepdims=True)
        cs_ref[...] += jnp.sum(dx, axis=0, keepdims=True)

    row = pl.BlockSpec((tr, d), lambda i: (i, 0))
    vec = pl.BlockSpec((1, d), lambda i: (0, 0))
    return pl.pallas_call(
        body, name=name, grid=(s // tr,),
        in_specs=[row, vec, row, row],
        out_specs=[row, row, vec, vec],
        out_shape=[jax.ShapeDtypeStruct((s, d), F32), jax.ShapeDtypeStruct((s, d), BF16),
                   jax.ShapeDtypeStruct((1, d), F32), jax.ShapeDtypeStruct((1, d), F32)],
        compiler_params=_params(("arbitrary",)),
    )(h, g.reshape(1, d), dn, dres)


def loss_head(h, g, target):
    s, d = h.shape
    tr = _pick(s, (ROW_TILE, 256, 128))

    def body(h_ref, g_ref, t_ref, dx_ref, dxb_ref, dg_ref, loss_ref):
        i = pl.program_id(0)
        x = h_ref[...]
        r = lax.rsqrt(jnp.mean(x * x, axis=-1, keepdims=True) + EPS)
        xh = x * r
        gw = g_ref[...]
        err = xh * gw - t_ref[...]
        dn_ = err * (1.0 / d)
        dyg = dn_ * gw
        dx = r * (dyg - xh * jnp.mean(dyg * xh, axis=-1, keepdims=True))
        dx_ref[...] = dx
        dxb_ref[...] = dx.astype(BF16)

        @pl.when(i == 0)
        def _():
            dg_ref[...] = jnp.zeros_like(dg_ref)
            loss_ref[...] = jnp.zeros_like(loss_ref)

        dg_ref[...] += jnp.sum(dn_ * xh, axis=0, keepdims=True)
        per_row = jnp.sum(err * err, axis=-1, keepdims=True) * (0.5 / d)
        loss_ref[...] += jnp.broadcast_to(jnp.sum(per_row, axis=0, keepdims=True), loss_ref.shape)

    row = pl.BlockSpec((tr, d), lambda i: (i, 0))
    vec = pl.BlockSpec((1, d), lambda i: (0, 0))
    return pl.pallas_call(
        body, name="loss_head", grid=(s // tr,),
        in_specs=[row, vec, row],
        out_specs=[row, row, vec, pl.BlockSpec((1, 128), lambda i: (0, 0))],
        out_shape=[jax.ShapeDtypeStruct((s, d), F32), jax.ShapeDtypeStruct((s, d), BF16),
                   jax.ShapeDtypeStruct((1, d), F32), jax.ShapeDtypeStruct((1, 128), F32)],
        compiler_params=_params(("arbitrary",)),
    )(h, g.reshape(1, d), target)


def colsum(name, x):
    s, n = x.shape
    tr = _pick(s, (ROW_TILE, 256, 128))

    def body(x_ref, o_ref):
        @pl.when(pl.program_id(0) == 0)
        def _():
            o_ref[...] = jnp.zeros_like(o_ref)

        o_ref[...] += jnp.sum(x_ref[...].astype(F32), axis=0, keepdims=True)

    return pl.pallas_call(
        body, name=name, grid=(s // tr,),
        in_specs=[pl.BlockSpec((tr, n), lambda i: (i, 0))],
        out_specs=pl.BlockSpec((1, n), lambda i: (0, 0)),
        out_shape=jax.ShapeDtypeStruct((1, n), F32),
        compiler_params=_params(("arbitrary",)),
    )(x)


def rowsum(name, x):
    n, s = x.shape
    ts = _pick(s, (1024, 512, 256, 128))

    def body(x_ref, o_ref):
        @pl.when(pl.program_id(0) == 0)
        def _():
            o_ref[...] = jnp.zeros_like(o_ref)

        o_ref[...] += jnp.sum(x_ref[...].astype(F32), axis=1, keepdims=True)

    return pl.pallas_call(
        body, name=name, grid=(s // ts,),
        in_specs=[pl.BlockSpec((n, ts), lambda i: (0, i))],
        out_specs=pl.BlockSpec((n, 1), lambda i: (0, 0)),
        out_shape=jax.ShapeDtypeStruct((n, 1), F32),
        compiler_params=_params(("arbitrary",)),
    )(x)[:, 0]


def _tri_rows(reverse):
    i = np.arange(SB_BK)
    tri = (i[None, :] >= i[:, None]) if reverse else (i[None, :] <= i[:, None])
    tri = np.concatenate([tri, tri], axis=1)
    return jnp.asarray(np.concatenate([tri, np.ones((8, 2 * SB_BK), bool)], axis=0), BF16)


def _hi_lo_rows(x):
    hi = x.astype(BF16)
    lo = (x - hi.astype(F32)).astype(BF16)
    return jnp.concatenate([hi, lo], axis=0)


def _softplus2(zs):
    neg_abs = lax.bitcast_convert_type(lax.bitcast_convert_type(zs, jnp.uint32) | jnp.uint32(0x80000000), F32)
    return jnp.maximum(zs, 0.0) + jnp.log2(1.0 + jnp.exp2(neg_abs))


def _pair_mask(first_rel_block, bq):
    key = lax.broadcasted_iota(jnp.int32, (2 * SB_BK, bq), 0) + first_rel_block * SB_BK
    qry = lax.broadcasted_iota(jnp.int32, (2 * SB_BK, bq), 1)
    return key < qry


def _row_of(table8, sub8, r):
    return jnp.sum(jnp.where(sub8 == r, table8, 0.0), axis=0, keepdims=True)


def _keys(j0):
    return pl.ds(pl.multiple_of(j0 * SB_BK, 2 * SB_BK), 2 * SB_BK)


class Carry:
    def __init__(self, build, ins, out_shapes, sems):
        self.build, self.ins, self.out_shapes, self.sems = build, list(ins), list(out_shapes), list(sems)


def _carried(carry, rest, n_out, n_scratch, first, last):
    n_ci = len(carry.ins) if carry else 0
    n_co = len(carry.out_shapes) if carry else 0
    cin, outs = rest[:n_ci], rest[n_ci:n_ci + n_out]
    cout = rest[n_ci + n_out:n_ci + n_out + n_co]
    scratch = rest[n_ci + n_out + n_co:n_ci + n_out + n_co + n_scratch]
    csems = rest[n_ci + n_out + n_co + n_scratch:]

    def start():
        if carry:
            @pl.when(first)
            def _():
                for cp in carry.build(cin, cout, *csems):
                    cp.start()

    def wait():
        if carry:
            @pl.when(last)
            def _():
                for cp in carry.build(cin, cout, *csems):
                    cp.wait()

    return outs, scratch, start, wait


def _contract0(a, b):
    return lax.dot_general(a, b, (((0,), (0,)), ((), ())), preferred_element_type=F32)


def _contract1(a, b):
    return lax.dot_general(a, b, (((1,), (1,)), ((), ())), preferred_element_type=F32)


def sb_fwd(name, qkvt, exchange=None):
    nh, dh, s = qkvt.shape[0] // 3, qkvt.shape[1], qkvt.shape[2]
    bq = SB_BQ
    per_q = bq // SB_BK
    nkb = s // SB_BK
    assert s % bq == 0 and per_q == 4 and nkb % 8 == 0

    def body(q_ref, k_ref, v_ref, a_ref, *rest):
        head = pl.program_id(0)
        (o_ref, rtab_ref), (acc, zbuf, wbuf), start_carried, wait_carried = _carried(
            exchange, rest, 2, 3, head == 0, head == nh - 1)
        start_carried()
        tri = a_ref[...]
        sub8 = lax.broadcasted_iota(jnp.int32, (8, bq), 0)
        rtab_ref[...] = jnp.full(rtab_ref.shape, SB_UNSEEN, F32)
        kf = k_ref[...].astype(F32)
        k_max2 = jnp.max(jnp.sum(kf * kf, axis=0, keepdims=True), axis=1, keepdims=True)

        def query_block(i, _):
            lanes = pl.ds(pl.multiple_of(i * bq, bq), bq)
            qb = q_ref[:, lanes] * Q_SCALE
            acc[...] = jnp.zeros_like(acc)
            qf = qb.astype(F32)
            bound = jnp.sqrt(jnp.sum(qf * qf, axis=0, keepdims=True) * k_max2) * (1.001 * LOG2E)

            def scores(j0):
                return _contract0(k_ref[:, _keys(j0)], qb) * LOG2E

            def pair(j0, slot, run, rt8, mask, has_prev):
                zs = zbuf[slot]
                zbuf[1 - slot] = scores(jnp.maximum(j0 - 2, 0))
                if has_prev:
                    acc[...] += jnp.dot(v_ref[:, _keys(j0 + 2)], wbuf[1 - slot], preferred_element_type=F32)
                p = _softplus2(zs)
                if mask is not None:
                    p = jnp.where(mask, p, 0.0)
                cr1 = jnp.dot(tri, _hi_lo_rows(p[SB_BK:]), preferred_element_type=F32)
                cr0 = jnp.dot(tri, _hi_lo_rows(p[:SB_BK]), preferred_element_type=F32)
                run1 = run + cr1[SB_BK:SB_BK + 1]
                w = jnp.exp2(jnp.concatenate([zs[:SB_BK] - cr0[:SB_BK] - run1, zs[SB_BK:] - cr1[:SB_BK] - run],
                                             axis=0))
                if mask is not None:
                    w = jnp.where(mask, w, 0.0)
                wbuf[slot] = w.astype(BF16)
                rt8 = jnp.where(j0 % 8 == 6, SB_UNSEEN, rt8)
                rt8 = jnp.where(sub8 == (j0 + 1) % 8, run, jnp.where(sub8 == j0 % 8, run1, rt8))
                rtab_ref[pl.ds(pl.multiple_of((j0 // 8) * 8, 8), 8), lanes] = rt8
                return run1 + cr0[SB_BK:SB_BK + 1], rt8

            def alive(run):
                return jnp.min(run - bound) < SB_DEAD

            top = i * per_q
            zbuf[0] = scores(top + 2)
            state = (jnp.zeros((1, bq), F32), jnp.full((8, bq), SB_UNSEEN, F32))
            state = pair(top + 2, 0, *state, _pair_mask(2, bq), False)
            state = pair(top, 1, *state, _pair_mask(0, bq), True)

            def step(c):
                it, pairs, _, run, rt8 = c
                j0 = top - 2 - 4 * it
                run, rt8 = pair(j0, 0, run, rt8, None, True)
                go = alive(run)
                run, rt8 = lax.cond(go, lambda r, t: pair(j0 - 2, 1, r, t, None, True), lambda r, t: (r, t), run, rt8)
                return it + 1, pairs + 1 + go.astype(jnp.int32), go & alive(run), run, rt8

            pairs = lax.while_loop(lambda c: (c[0] < i) & c[2], step, (0, 0, alive(state[0]), *state))[1]
            acc[...] += jnp.dot(v_ref[:, _keys(top - 2 * pairs)], wbuf[(pairs + 1) % 2], preferred_element_type=F32)
            o_ref[:, lanes] = acc[...].astype(o_ref.dtype)
            return 0

        lax.fori_loop(0, s // bq, query_block, 0)
        wait_carried()

    def head_spec(offset, rows):
        return pl.BlockSpec((None, rows, s), lambda h: (h + offset, 0, 0))

    hbm = pl.BlockSpec(memory_space=pl.ANY)
    c_ins, c_outs, c_sems = (exchange.ins, exchange.out_shapes, exchange.sems) if exchange else ([], [], [])
    outs = pl.pallas_call(
        body, name=name, grid=(nh,),
        in_specs=[head_spec(0, dh), head_spec(nh, dh), head_spec(2 * nh, dh),
                  pl.BlockSpec((SB_BK + 8, 2 * SB_BK), lambda h: (0, 0))] + [hbm] * len(c_ins),
        out_specs=[head_spec(0, dh), head_spec(0, nkb)] + [hbm] * len(c_outs),
        out_shape=[jax.ShapeDtypeStruct((nh, dh, s), BF16), jax.ShapeDtypeStruct((nh, nkb, s), F32)] + c_outs,
        scratch_shapes=[pltpu.VMEM((dh, bq), F32), pltpu.VMEM((2, 2 * SB_BK, bq), F32),
                        pltpu.VMEM((2, 2 * SB_BK, bq), BF16)] + c_sems,
        compiler_params=_params(("arbitrary",)),
    )(qkvt, qkvt, qkvt, _tri_rows(True), *c_ins)
    return outs[0], outs[1], outs[2:]


def sb_bwd(name, qkvt, dot_, rtab, exchange=None):
    nh, dh, s = qkvt.shape[0] // 3, qkvt.shape[1], qkvt.shape[2]
    bq = SB_BQ
    per_q = bq // SB_BK
    nkb = s // SB_BK

    def body(qt_ref, kt_ref, vt_ref, dot_ref, rtab_ref, ar_ref, af_ref, *rest):
        head = pl.program_id(0)
        (dq_ref, dk_ref, dv_ref), (dq_acc, dk_acc, dv_acc, zbuf, dwbuf, dzbuf, wbuf), start_carried, wait_carried = \
            _carried(exchange, rest, 3, 7, head == 0, head == nh - 1)
        start_carried()
        dk_acc[...] = jnp.zeros_like(dk_acc)
        dv_acc[...] = jnp.zeros_like(dv_acc)
        tri_rev = ar_ref[...][:SB_BK]
        tri_fwd = af_ref[...]
        sub8 = lax.broadcasted_iota(jnp.int32, (8, bq), 0)

        def query_block(i, _):
            lanes = pl.ds(pl.multiple_of(i * bq, bq), bq)
            qtb = qt_ref[:, lanes] * Q_SCALE
            dotb = dot_ref[:, lanes]
            dq_acc[...] = jnp.zeros_like(dq_acc)
            last_j = i * per_q + 2
            seen = jnp.max(jnp.where(rtab_ref[:, lanes] < 0.1 * SB_UNSEEN, 1.0, 0.0), axis=1, keepdims=True)
            pairs = jnp.clip((jnp.sum(seen).astype(jnp.int32) - per_q) // 2, 0, 2 * i)
            odd = pairs % 2
            first_j = i * per_q - 2 * pairs

            def issue(j0, slot):
                zbuf[slot] = _contract0(kt_ref[:, _keys(j0)], qtb) * LOG2E
                dwbuf[slot] = _contract0(vt_ref[:, _keys(j0)], dotb)

            def retire(j0, slot):
                keys = _keys(j0)
                dq_acc[...] += jnp.dot(kt_ref[:, keys], dzbuf[slot], preferred_element_type=F32)
                dk_acc[:, keys] += _contract1(qtb, dzbuf[slot])
                dv_acc[:, keys] += _contract1(dotb, wbuf[slot])

            def pair(j0, slot, g_run, mask):
                zs = zbuf[slot]
                dw = dwbuf[slot]
                issue(jnp.minimum(j0 + 2, last_j), 1 - slot)
                retire(jnp.maximum(j0 - 2, first_j), 1 - slot)
                p_raw = _softplus2(zs)
                p = p_raw if mask is None else jnp.where(mask, p_raw, 0.0)
                c0 = jnp.dot(tri_rev, _hi_lo_rows(p[:SB_BK]), preferred_element_type=F32)
                c1 = jnp.dot(tri_rev, _hi_lo_rows(p[SB_BK:]), preferred_element_type=F32)
                rt8 = rtab_ref[pl.ds(pl.multiple_of((j0 // 8) * 8, 8), 8), lanes]
                r0 = _row_of(rt8, sub8, j0 % 8)
                r1 = _row_of(rt8, sub8, (j0 + 1) % 8)
                w = jnp.exp2(jnp.concatenate([zs[:SB_BK] - c0 - r0, zs[SB_BK:] - c1 - r1], axis=0))
                if mask is not None:
                    w = jnp.where(mask, w, 0.0)
                g = w * dw
                gg0 = jnp.dot(tri_fwd, _hi_lo_rows(g[:SB_BK]), preferred_element_type=F32)
                gg1 = jnp.dot(tri_fwd, _hi_lo_rows(g[SB_BK:]), preferred_element_type=F32)
                g_run1 = g_run + gg0[SB_BK:SB_BK + 1]
                g_pre = jnp.concatenate([gg0[:SB_BK] + g_run, gg1[:SB_BK] + g_run1], axis=0)
                dz = g - jnp.exp2(zs - p_raw) * g_pre
                if mask is not None:
                    dz = jnp.where(mask, dz, 0.0)
                dzbuf[slot] = dz.astype(BF16)
                wbuf[slot] = w.astype(BF16)
                return g_run1 + gg1[SB_BK:SB_BK + 1]

            issue(first_j, odd)
            dzbuf[...] = jnp.zeros(dzbuf.shape, BF16)
            wbuf[...] = jnp.zeros(wbuf.shape, BF16)

            def step(it, g_run):
                g_run = pair(4 * it, 0, g_run, None)
                return pair(4 * it + 2, 1, g_run, None)

            g_run = lax.cond(odd == 1, lambda g: pair(first_j, 1, g, None), lambda g: g, jnp.zeros((1, bq), F32))
            g_run = lax.fori_loop(i - pairs // 2, i, step, g_run)
            g_run = pair(last_j - 2, 0, g_run, _pair_mask(0, bq))
            pair(last_j, 1, g_run, _pair_mask(2, bq))
            retire(last_j, 1)
            dq_ref[:, lanes] = (dq_acc[...] * Q_SCALE).astype(dq_ref.dtype)
            return 0

        lax.fori_loop(0, s // bq, query_block, 0)
        dk_ref[...] = dk_acc[...].astype(dk_ref.dtype)
        dv_ref[...] = dv_acc[...].astype(dv_ref.dtype)
        wait_carried()

    def head_spec(offset, rows):
        return pl.BlockSpec((None, rows, s), lambda h: (h + offset, 0, 0))

    aspec = pl.BlockSpec((SB_BK + 8, 2 * SB_BK), lambda h: (0, 0))
    pair_f32 = pltpu.VMEM((2, 2 * SB_BK, bq), F32)
    pair_bf16 = pltpu.VMEM((2, 2 * SB_BK, bq), BF16)
    hbm = pl.BlockSpec(memory_space=pl.ANY)
    c_ins, c_outs, c_sems = (exchange.ins, exchange.out_shapes, exchange.sems) if exchange else ([], [], [])
    outs = pl.pallas_call(
        body, name=name, grid=(nh,),
        in_specs=[head_spec(0, dh), head_spec(nh, dh), head_spec(2 * nh, dh), head_spec(0, dh), head_spec(0, nkb),
                  aspec, aspec] + [hbm] * len(c_ins),
        out_specs=[head_spec(0, dh)] * 3 + [hbm] * len(c_outs),
        out_shape=[jax.ShapeDtypeStruct((nh, dh, s), BF16)] * 3 + c_outs,
        scratch_shapes=[pltpu.VMEM((dh, bq), F32), pltpu.VMEM((dh, s), F32), pltpu.VMEM((dh, s), F32),
                        pair_f32, pair_f32, pair_bf16, pair_bf16] + c_sems,
        compiler_params=_params(("arbitrary",)),
    )(qkvt, qkvt, qkvt, dot_, rtab, _tri_rows(True), _tri_rows(False), *c_ins)
    return outs[0], outs[1], outs[2], outs[3:]


SWA_QB = 2


def _swa_probs(qt, kt, bias_t, sink, i):
    cols = qt.shape[1]
    sc = _contract0(kt, qt) + bias_t
    kj = lax.broadcasted_iota(jnp.int32, (2 * WINDOW, cols), 0)
    qi = lax.broadcasted_iota(jnp.int32, (2 * WINDOW, cols), 1) & (WINDOW - 1)
    dist = qi + WINDOW - kj
    valid = (dist >= 0) & (dist < WINDOW) & ((kj >= WINDOW) | (i > 0))
    sc = jnp.where(valid, sc, NEG_INF)
    mx = jnp.maximum(jnp.max(sc, axis=0, keepdims=True), sink)
    p = jnp.exp(sc - mx)
    p_sink = jnp.exp(sink - mx)
    inv = 1.0 / (jnp.sum(p, axis=0, keepdims=True) + p_sink)
    return p, p_sink, inv


def _band(i):
    return pl.ds(pl.multiple_of(i * WINDOW, WINDOW), 2 * WINDOW)


def _heads_to_lanes(blk):
    return jnp.concatenate([blk[r * HEAD_DIM:(r + 1) * HEAD_DIM] for r in range(8)], axis=1)


def _lanes_to_heads(t):
    return jnp.concatenate([t[:, r * WINDOW:(r + 1) * WINDOW] for r in range(8)], axis=0)


def swa_fwd(name, qt, kpt, vpt, bias_t, sink_row):
    d, s = qt.shape
    ng, dh, sp = kpt.shape
    rows, cols = d // ng, SWA_QB * WINDOW
    assert (s // WINDOW) % SWA_QB == 0

    def body(q_ref, k_ref, v_ref, bias_ref, sink_ref, o_ref):
        for u in range(SWA_QB):
            i = pl.program_id(1) * SWA_QB + u
            lanes = slice(u * WINDOW, (u + 1) * WINDOW)
            qb = _heads_to_lanes(q_ref[:, lanes]) * Q_SCALE
            p, _, inv = _swa_probs(qb, k_ref[:, _band(i)], bias_ref[...], sink_ref[...], i)
            o_t = jnp.dot(v_ref[:, _band(i)], p.astype(BF16), preferred_element_type=F32) * inv
            o_ref[:, lanes] = _lanes_to_heads(o_t).astype(o_ref.dtype)

    qspec = pl.BlockSpec((rows, cols), lambda g, i: (g, i))
    kspec = pl.BlockSpec((None, dh, sp), lambda g, i: (g, 0, 0))
    return pl.pallas_call(
        body, name=name, grid=(ng, s // cols),
        in_specs=[qspec, kspec, kspec, pl.BlockSpec((None, 2 * WINDOW, 8 * WINDOW), lambda g, i: (g, 0, 0)),
                  pl.BlockSpec((None, 1, 8 * WINDOW), lambda g, i: (g, 0, 0))],
        out_specs=qspec,
        out_shape=jax.ShapeDtypeStruct(qt.shape, BF16),
        compiler_params=_params(("parallel", "arbitrary")),
    )(qt, kpt, vpt, bias_t, sink_row)


def swa_bwd(name, qt, kpt, vpt, bias_t, sink_row, dot_, dk_in, dv_in):
    d, s = qt.shape
    ng, dh, sp = kpt.shape
    rows, cols = d // ng, SWA_QB * WINDOW

    def body(q_ref, k_ref, v_ref, bias_ref, sink_ref, do_ref, dki_ref, dvi_ref, dq_ref, dk_ref, dv_ref, db_ref, ds_ref):
        @pl.when(pl.program_id(1) == 0)
        def _():
            dk_ref[...] = dki_ref[...]
            dv_ref[...] = dvi_ref[...]
            db_ref[...] = jnp.zeros_like(db_ref)
            ds_ref[...] = jnp.zeros_like(ds_ref)

        for u in range(SWA_QB):
            i = pl.program_id(1) * SWA_QB + u
            band = _band(i)
            lanes = slice(u * WINDOW, (u + 1) * WINDOW)
            qb = _heads_to_lanes(q_ref[:, lanes]) * Q_SCALE
            dob = _heads_to_lanes(do_ref[:, lanes])
            kt = k_ref[:, band]
            p, p_sink, inv = _swa_probs(qb, kt, bias_ref[...], sink_ref[...], i)
            p = p * inv
            dp = _contract0(v_ref[:, band], dob)
            delta = jnp.sum(p * dp, axis=0, keepdims=True)
            dsc = p * (dp - delta)
            ds_ref[...] -= p_sink * inv * delta
            db_ref[...] += dsc
            dscb = dsc.astype(BF16)
            dq_t = jnp.dot(kt, dscb, preferred_element_type=F32) * Q_SCALE
            dq_ref[:, lanes] = _lanes_to_heads(dq_t).astype(dq_ref.dtype)
            dk_ref[:, band] += _contract1(qb, dscb)
            dv_ref[:, band] += _contract1(dob, p.astype(BF16))

    qspec = pl.BlockSpec((rows, cols), lambda g, i: (g, i))
    kspec = pl.BlockSpec((None, dh, sp), lambda g, i: (g, 0, 0))
    bspec = pl.BlockSpec((None, 2 * WINDOW, 8 * WINDOW), lambda g, i: (g, 0, 0))
    sspec = pl.BlockSpec((None, 1, 8 * WINDOW), lambda g, i: (g, 0, 0))
    return pl.pallas_call(
        body, name=name, grid=(ng, s // cols),
        in_specs=[qspec, kspec, kspec, bspec, sspec, qspec, kspec, kspec],
        out_specs=[qspec, kspec, kspec, bspec, sspec],
        out_shape=[jax.ShapeDtypeStruct(qt.shape, BF16), jax.ShapeDtypeStruct(kpt.shape, F32),
                   jax.ShapeDtypeStruct(kpt.shape, F32), jax.ShapeDtypeStruct(bias_t.shape, F32),
                   jax.ShapeDtypeStruct(sink_row.shape, F32)],
        compiler_params=_params(("parallel", "arbitrary")),
    )(qt, kpt, vpt, bias_t, sink_row, dot_, dk_in, dv_in)


def _bucket_onehot():
    qi = np.arange(WINDOW)[:, None]
    kj = np.arange(2 * WINDOW)[None, :]
    n = np.maximum(qi + WINDOW - kj, 0)
    max_exact = N_BUCKETS // 2
    nf = np.maximum(n, 1).astype(np.float64)
    val = np.log(nf / max_exact) / math.log(WINDOW / max_exact) * (N_BUCKETS - max_exact)
    assert np.all(np.abs(val - np.round(val))[(n > max_exact) & (n < WINDOW)] > 1e-3)
    large = np.minimum(max_exact + val.astype(np.int64), N_BUCKETS - 1)
    bucket = np.where(n < max_exact, n, large).reshape(-1)
    onehot = np.zeros((128, bucket.size), np.float32)
    onehot[bucket, np.arange(bucket.size)] = 1.0
    return onehot


def _split3(x):
    a = x.astype(BF16)
    r = x - a.astype(F32)
    b = r.astype(BF16)
    c = (r - b.astype(F32)).astype(BF16)
    return a, b, c


def bias_table(rel_bias):
    nh = rel_bias.shape[1]
    oh = jnp.asarray(_bucket_onehot(), BF16)
    n = oh.shape[1]
    tn = 4096
    rb = jnp.zeros((nh, 128), F32).at[:, :N_BUCKETS].set(rel_bias.T)

    def body(rb_ref, oh_ref, o_ref):
        o_ref[...] = sum(jnp.dot(t, oh_ref[...], preferred_element_type=F32) for t in _split3(rb_ref[...]))

    return pl.pallas_call(
        body, name="bias_table", grid=(n // tn,),
        in_specs=[pl.BlockSpec((nh, 128), lambda i: (0, 0)), pl.BlockSpec((128, tn), lambda i: (0, i))],
        out_specs=pl.BlockSpec((nh, tn), lambda i: (0, i)),
        out_shape=jax.ShapeDtypeStruct((nh, n), F32),
        compiler_params=_params(("parallel",)),
    )(rb, oh)


def bias_table_grad(db0, db1):
    nh, n = db0.shape
    oh = jnp.asarray(_bucket_onehot(), BF16)
    tn = 4096

    def body(a_ref, b_ref, oh_ref, o_ref):
        @pl.when(pl.program_id(0) == 0)
        def _():
            o_ref[...] = jnp.zeros_like(o_ref)

        o_ref[...] += sum(lax.dot_general(t, oh_ref[...], (((1,), (1,)), ((), ())), preferred_element_type=F32)
                          for t in _split3(a_ref[...] + b_ref[...]))

    blk = pl.BlockSpec((nh, tn), lambda i: (0, i))
    return pl.pallas_call(
        body, name="bias_table_grad", grid=(n // tn,),
        in_specs=[blk, blk, pl.BlockSpec((128, tn), lambda i: (0, i))],
        out_specs=pl.BlockSpec((nh, 128), lambda i: (0, 0)),
        out_shape=jax.ShapeDtypeStruct((nh, 128), F32),
        compiler_params=_params(("arbitrary",)),
    )(db0, db1, oh)


def _owner_view(ref, name, d):
    if name == 'a_norm':
        return ref.at[d]
    if name in COL_SHARDED:
        n = ref.shape[2] // N_DEV
        return ref.at[:, :, pl.ds(pl.multiple_of(d * n, 128), n)]
    return ref.at[:, d]


def _place():
    return lax.axis_index("x"), lax.axis_index("y"), lax.axis_index("c")


def _dev(p):
    return 4 * p[0] + 2 * p[1] + p[2]


def _remote(src, dst, send_sem, recv_sem, to):
    return pltpu.make_async_remote_copy(src_ref=src, dst_ref=dst, send_sem=send_sem, recv_sem=recv_sem,
                                        device_id=to, device_id_type=MESH)


def _dma_sems(*shapes):
    return [pltpu.SemaphoreType.DMA(sh) for sh in shapes]


def comm_call(name, build, ins, out_shapes, sems, aliases=None):
    n_in, n_out = len(ins), len(out_shapes)

    def body(*refs):
        copies = build(refs[:n_in], refs[n_in:n_in + n_out], *refs[n_in + n_out:])
        for cp in copies:
            cp.start()
        for cp in copies:
            cp.wait()

    hbm = pl.BlockSpec(memory_space=pl.ANY)
    return pl.pallas_call(
        body, name=name, in_specs=[hbm] * n_in, out_specs=[hbm] * n_out, out_shape=list(out_shapes),
        scratch_shapes=sems, input_output_aliases=aliases or {},
    )(*ins)


def all_gather_weights(names, shards, full_shapes):
    n = len(names)

    def body(*refs):
        ins, outs = refs[:n], refs[n:2 * n]
        send_sems, recv_sems, local_sems = refs[2 * n:]
        x, y, c = _place()
        me, sibling = (x, y, c), (x, y, 1 - c)
        chips = [(1 - x, y), (x, 1 - y), (1 - x, 1 - y)]

        def copy(t, k, block, to, src=None):
            dst = _owner_view(outs[t], names[t], _dev(block))
            return _remote(dst if src is None else src, dst, send_sems.at[t, k], recv_sems.at[t, k], to)

        mine = [pltpu.make_async_copy(ins[t], _owner_view(outs[t], names[t], _dev(me)), local_sems.at[t])
                for t in range(n)]
        for cp in mine:
            cp.start()
        first = []
        for t in range(n):
            first.append(copy(t, 0, me, sibling, src=ins[t]))
            first += [copy(t, 1 + j, me, (*chip, c), src=ins[t]) for j, chip in enumerate(chips)]
        for cp in first:
            cp.start()
        passed = []
        for j, chip in enumerate(chips):
            for t in range(n):
                copy(t, 1 + j, (*chip, c), me).wait_recv()
                fwd = copy(t, 4 + j, (*chip, c), sibling)
                fwd.start()
                passed.append(fwd)
        for t in range(n):
            copy(t, 0, sibling, me).wait_recv()
            for j, chip in enumerate(chips):
                copy(t, 4 + j, (*chip, 1 - c), me).wait_recv()
        for cp in first + passed:
            cp.wait_send()
        for cp in mine:
            cp.wait()

    hbm = pl.BlockSpec(memory_space=pl.ANY)
    return pl.pallas_call(
        body, name="all_gather_layer0",
        in_specs=[hbm] * n, out_specs=[hbm] * n,
        out_shape=[jax.ShapeDtypeStruct(full_shapes[t], shards[t].dtype) for t in range(n)],
        scratch_shapes=_dma_sems((n, 7), (n, 7), (n,)),
    )(*shards)


def ag_direct(names, shards, full_shapes):
    n = len(names)

    def build(ins, outs, send_sems, recv_sems, local_sems):
        x, y, c = _place()
        peers = [(x, y, 1 - c), (1 - x, y, c), (x, 1 - y, c), (1 - x, 1 - y, c)]
        copies = []
        for t in range(n):
            dst = _owner_view(outs[t], names[t], _dev((x, y, c)))
            copies.append(pltpu.make_async_copy(ins[t], dst, local_sems.at[t]))
            copies += [_remote(ins[t], dst, send_sems.at[t, k], recv_sems.at[t, k], to) for k, to in enumerate(peers)]
        return copies

    return Carry(build, shards, [jax.ShapeDtypeStruct(full_shapes[t], shards[t].dtype) for t in range(n)],
                 _dma_sems((n, 4), (n, 4), (n,)))


def ag_forward(tag, names, partial):
    n = len(names)

    def build(ins, outs, send_sems, recv_sems):
        del ins
        x, y, c = _place()
        copies = []
        for t in range(n):
            for k, chip in enumerate([(1 - x, y), (x, 1 - y), (1 - x, 1 - y)]):
                view = _owner_view(outs[t], names[t], _dev((*chip, c)))
                copies.append(_remote(view, view, send_sems.at[t, k], recv_sems.at[t, k], (x, y, 1 - c)))
        return copies

    return comm_call(f"all_gather_forward_{tag}", build, partial, [jax.ShapeDtypeStruct(p.shape, p.dtype) for p in partial],
                     _dma_sems((n, 3), (n, 3)), aliases={t: t for t in range(n)})


def sibling_exchange(tag, names, grads, part_shapes):
    n = len(names)

    def build(ins, outs, send_sems, recv_sems):
        x, y, c = _place()
        return [_remote(_owner_view(ins[t], names[t], 2 * q + 1 - c), outs[t].at[q], send_sems.at[t, q],
                        recv_sems.at[t, q], (x, y, 1 - c)) for t in range(n) for q in range(4)]

    return comm_call(f"rs_sibling_exchange_{tag}", build, grads,
                     [jax.ShapeDtypeStruct((4,) + part_shapes[t], BF16) for t in range(n)], _dma_sems((n, 4), (n, 4)))


def chip_exchange(names, parts, part_shapes):
    n = len(names)

    def build(ins, outs, send_sems, recv_sems):
        x, y, c = _place()
        chips = [(1 - x, y), (x, 1 - y), (1 - x, 1 - y)]
        return [_remote(ins[t].at[2 * chip[0] + chip[1]], outs[t].at[k], send_sems.at[t, k], recv_sems.at[t, k],
                        (*chip, c)) for t in range(n) for k, chip in enumerate(chips)]

    return Carry(build, parts, [jax.ShapeDtypeStruct((3,) + part_shapes[t], BF16) for t in range(n)],
                 _dma_sems((n, 3), (n, 3)))


def all_gather_rows(x):
    r, w = x.shape

    def body(x_ref, out_ref, send_sems, recv_sems, local_sem):
        px, py, pc = _place()
        me = 4 * px + 2 * py + pc
        mine = pltpu.make_async_copy(x_ref, out_ref.at[me], local_sem)
        mine.start()
        copies = []
        for k in range(1, N_DEV):
            peer = (px ^ (k >> 2), py ^ ((k >> 1) & 1), pc ^ (k & 1))
            copies.append(pltpu.make_async_remote_copy(
                src_ref=x_ref, dst_ref=out_ref.at[me], send_sem=send_sems.at[k - 1], recv_sem=recv_sems.at[k - 1],
                device_id=peer, device_id_type=MESH))
        for cp in copies:
            cp.start()
        for k in range(1, N_DEV):
            peer_idx = me ^ k
            pltpu.make_async_remote_copy(
                src_ref=x_ref, dst_ref=out_ref.at[peer_idx], send_sem=send_sems.at[k - 1],
                recv_sem=recv_sems.at[k - 1], device_id=(px, py, pc), device_id_type=MESH).wait_recv()
        for cp in copies:
            cp.wait_send()
        mine.wait()

    vmem = pl.BlockSpec(memory_space=pltpu.VMEM)
    return pl.pallas_call(
        body, name="all_gather_small_grads",
        in_specs=[vmem], out_specs=vmem,
        out_shape=jax.ShapeDtypeStruct((N_DEV, r, w), x.dtype),
        scratch_shapes=[pltpu.SemaphoreType.DMA((N_DEV - 1,)), pltpu.SemaphoreType.DMA((N_DEV - 1,)),
                        pltpu.SemaphoreType.DMA],
    )(x)


def _adamw(w, g, m, v):
    m = ADAM_B1 * m + (1.0 - ADAM_B1) * g
    v = ADAM_B2 * v + (1.0 - ADAM_B2) * (g * g)
    m_hat = m / (1.0 - ADAM_B1 ** ADAM_STEP)
    v_hat = v / (1.0 - ADAM_B2 ** ADAM_STEP)
    return -ADAM_LR * (m_hat / (jnp.sqrt(v_hat) + ADAM_EPS) + ADAM_WD * w), m, v


def sibling_sum(name, col, grads, recv, core):
    _, nl, rows, cols = recv.shape
    tr = _tile(rows, 512)
    rspec = pl.BlockSpec((None, None, tr, cols), lambda q, l, i, c_ref: (q, l, i, 0))
    if col:
        gspec = pl.BlockSpec((None, tr, cols), lambda q, l, i, c_ref: (l, i, 2 * q + c_ref[0]))
    else:
        gspec = pl.BlockSpec((None, None, tr, cols), lambda q, l, i, c_ref: (l, 2 * q + c_ref[0], i, 0))

    def body(c_ref, g_ref, r_ref, o_ref):
        del c_ref
        o_ref[...] = (g_ref[...].astype(F32) + r_ref[...].astype(F32)).astype(BF16)

    return pl.pallas_call(
        body, name=name,
        grid_spec=pltpu.PrefetchScalarGridSpec(num_scalar_prefetch=1, grid=(4, nl, rows // tr),
                                               in_specs=[gspec, rspec], out_specs=rspec),
        out_shape=jax.ShapeDtypeStruct(recv.shape, BF16),
        compiler_params=_params(("parallel", "parallel", "parallel")),
    )(core.reshape(1), grads, recv)


def reduce_adamw(name, parts, recv, chip, w, m, v, l0, prev):
    _, nl, rows, cols = parts.shape
    tr = _tile(rows, 256)

    def body(q_ref, p_ref, r_ref, w_ref, m_ref, v_ref, *rest):
        del q_ref
        g_out, d_out, m_out, v_out = rest[-4:]
        g = ((p_ref[...].astype(F32) + r_ref[0].astype(F32)) + r_ref[1].astype(F32)) + r_ref[2].astype(F32)
        d, mn, vn = _adamw(w_ref[...], g, m_ref[...], v_ref[...])
        g_out[...] = g
        d_out[...] = d
        m_out[...] = mn
        v_out[...] = vn

    blk = pl.BlockSpec((None, tr, cols), lambda l, i, q_ref: (l0 + l, i, 0))
    prev = list(prev) if prev else []
    return pl.pallas_call(
        body, name=name,
        grid_spec=pltpu.PrefetchScalarGridSpec(
            num_scalar_prefetch=1, grid=(nl, rows // tr),
            in_specs=[pl.BlockSpec((None, None, tr, cols), lambda l, i, q_ref: (q_ref[0], l, i, 0)),
                      pl.BlockSpec((3, None, tr, cols), lambda l, i, q_ref: (0, l, i, 0)), blk, blk, blk]
            + [pl.BlockSpec(memory_space=pl.ANY)] * len(prev),
            out_specs=[blk] * 4),
        out_shape=[jax.ShapeDtypeStruct(w.shape, F32)] * 4,
        input_output_aliases={6 + i: i for i in range(len(prev))},
        compiler_params=_params(("parallel", "parallel")),
    )(chip.reshape(1), parts, recv, w, m, v, *prev)


def small_adamw(name, gathered, w, m, v):
    _, r, c = gathered.shape

    def body(ga_ref, w_ref, m_ref, v_ref, g_out, d_out, m_out, v_out):
        g = ga_ref[0]
        for d in range(1, N_DEV):
            g = g + ga_ref[d]
        dl, mn, vn = _adamw(w_ref[...], g, m_ref[...], v_ref[...])
        g_out[...] = g
        d_out[...] = dl
        m_out[...] = mn
        v_out[...] = vn

    return pl.pallas_call(
        body, name=name,
        out_shape=[jax.ShapeDtypeStruct((r, c), F32)] * 4,
        compiler_params=_params(),
    )(gathered, w, m, v)


def _rms(x, g):
    return x * lax.rsqrt(jnp.mean(x * x, axis=-1, keepdims=True) + EPS) * g


def _residual_then_norms(n_terms):
    def epilogue(acc, *ex):
        h = acc
        for t in ex[:n_terms]:
            h = h + t
        return (h,) + tuple(_rms(h, g) for g in ex[n_terms:])
    return epilogue


def local_step(x, target, small, ex):
    s, d = x.shape
    n_a, n_b = small['a_norm'].shape[0], small['b_norm'].shape[0]
    sg = {}
    gb = {}

    def fwd_mm(name, a, wname, layer, epilogue, extras, out_dtypes, **kw):
        return mm_nn(name, a, *ex.weight(wname, layer), epilogue, extras, out_dtypes, **kw)

    def dx_mm(name, dy, wname, layer, epilogue, extras, out_dtypes, **kw):
        return mm_nt(name, dy, *ex.weight(wname, layer), epilogue, extras, out_dtypes, **kw)

    def dw_mm(name, a, dy, wname, layer, **kw):
        key, slab, shape = ex.grad(wname, layer)
        gb[key] = mm_tn(name, a, dy, gb.get(key), shape, slab, **kw)

    plain = lambda acc: (acc,)
    plus_col = lambda acc, b: (acc + b,)

    bias_flat = bias_table(small['rel_bias'])
    bias_t = bias_flat.reshape(2, 8, WINDOW, 2 * WINDOW).transpose(0, 3, 1, 2).reshape(2, 2 * WINDOW, 8 * WINDOW)
    sink_rows = [jnp.repeat(small['b_sinks'][j], WINDOW).reshape(2, 1, 8 * WINDOW) for j in range(n_b)]

    gain = lambda g: g.reshape(1, -1)

    def mlp_fwd(h, n2, layer, next_gains):
        u, a = fwd_mm(f"mlp_up_fwd{layer}", n2, 'mlp_up', layer,
                      lambda acc: (acc, jnp.square(jnp.maximum(acc, 0.0))), (), (BF16, BF16))
        h2, *nexts = fwd_mm(f"mlp_down_fwd{layer}", a, 'mlp_down', layer, _residual_then_norms(1),
                            (h, *[gain(g) for g in next_gains]), (F32,) + (BF16,) * len(next_gains))
        return h2, nexts, (n2, u, a)

    h = x
    saved = []
    n1 = rms_fwd("a_norm_fwd0", h, small['a_norm'][0])
    for l in range(n_a):
        (qkvt,) = fwd_mm(f"a_qkv_fwd{l}", n1, 'a_wqkv', l, plain, (), (BF16,), out_t=True)
        qkvt = qkvt.reshape(3 * d // HEAD_DIM, HEAD_DIM, s)
        o_t, rtab, carried = sb_fwd(f"sb_fwd{l}", qkvt, ex.fwd_carry(l))
        ex.fwd_done(l, carried)
        o_t = o_t.reshape(d, s)
        h_mid, n2 = fwd_mm(f"a_wo_fwd{l}", o_t, 'a_wo', l, _residual_then_norms(1),
                           (h, gain(small['mlp_norm'][l])), (F32, BF16), a_t=True)
        next_gains = [small['a_norm'][l + 1]] if l + 1 < n_a else [small['b_norm'][0], small['kv_norm']]
        h_out, nexts, mlp_saved = mlp_fwd(h_mid, n2, l, next_gains)
        saved.append((h, n1, qkvt, o_t, rtab, h_mid, mlp_saved))
        h, n1 = h_out, nexts[0]
    h_kv, nkv = h, nexts[1]
    (kvt,) = fwd_mm("kv_fwd", nkv, 'w_kv', 0, plus_col, (small['b_kv'].reshape(-1, 1),), (BF16,), out_t=True)
    kvt = kvt.reshape(2, 2, HEAD_DIM, s)
    kpt, vpt = (jnp.pad(t, ((0, 0), (0, 0), (WINDOW, 0))) for t in (kvt[0], kvt[1]))
    for j in range(n_b):
        layer = n_a + j
        (qbt,) = fwd_mm(f"b_q_fwd{j}", n1, 'b_wq', j, plus_col, (small['b_bq'][j].reshape(-1, 1),), (BF16,),
                        out_t=True)
        o_t = swa_fwd(f"swa_fwd{j}", qbt, kpt, vpt, bias_t, sink_rows[j])
        h_mid, n2 = fwd_mm(f"b_wo_fwd{j}", o_t, 'b_wo', j, _residual_then_norms(2),
                           (h, gain(small['b_bo'][j]), gain(small['mlp_norm'][layer])), (F32, BF16), a_t=True)
        h_out, nexts, mlp_saved = mlp_fwd(h_mid, n2, layer, [small['b_norm'][j + 1]] if j + 1 < n_b else [])
        saved.append((h, n1, qbt, o_t, h_mid, mlp_saved))
        h, n1 = h_out, (nexts[0] if nexts else None)

    dh, dhb, dg_final, loss_b = loss_head(h, small['final_norm'], target)
    sg['final_norm'] = dg_final[0]
    sg['mlp_norm'] = [None] * (n_a + n_b)

    def mlp_bwd(dh, dhb, h_mid, mlp_saved, layer):
        n2, u, a = mlp_saved
        (du,) = dx_mm(f"mlp_down_dx{layer}", dhb, 'mlp_down', layer,
                      lambda acc, uu: (acc * (2.0 * jnp.maximum(uu.astype(F32), 0.0)),), (u,), (BF16,))
        dw_mm(f"mlp_down_dw{layer}", a, dhb, 'mlp_down', layer)
        (dn2,) = dx_mm(f"mlp_up_dx{layer}", du, 'mlp_up', layer, plain, (), (F32,))
        dw_mm(f"mlp_up_dw{layer}", n2, du, 'mlp_up', layer)
        dh2, dh2b, dg, cs = rms_bwd(f"mlp_norm_bwd{layer}", h_mid, small['mlp_norm'][layer], dn2, dh)
        sg['mlp_norm'][layer] = dg[0]
        return dh2, dh2b, cs

    dkp = jnp.zeros(kpt.shape, F32)
    dvp = jnp.zeros(vpt.shape, F32)
    sg['b_norm'], sg['b_bq'], sg['b_bo'], sg['b_sinks'] = [None] * n_b, [None] * n_b, [None] * n_b, [None] * n_b
    dbias = [None] * n_b
    for j in reversed(range(n_b)):
        layer = n_a + j
        h_in, n1, qbt, o_t, h_mid, mlp_saved = saved[layer]
        dh, dhb, cs = mlp_bwd(dh, dhb, h_mid, mlp_saved, layer)
        sg['b_bo'][j] = cs[0]
        (do_t,) = dx_mm(f"b_wo_dx{j}", dhb, 'b_wo', j, plain, (), (BF16,), out_t=True)
        dw_mm(f"b_wo_dw{j}", o_t, dhb, 'b_wo', j, x_t=True)
        dq_t, dkp, dvp, dbias[j], dsink = swa_bwd(f"swa_bwd{j}", qbt, kpt, vpt, bias_t, sink_rows[j], do_t, dkp, dvp)
        sg['b_sinks'][j] = colsum(f"sink_grad{j}", dsink.reshape(16, WINDOW).T)[0]
        sg['b_bq'][j] = rowsum(f"b_bq_grad{j}", dq_t)
        (dn1,) = dx_mm(f"b_q_dx{j}", dq_t, 'b_wq', j, plain, (), (F32,), a_t=True)
        dw_mm(f"b_q_dw{j}", n1, dq_t, 'b_wq', j, dy_t=True)
        dh, dhb, dg, _ = rms_bwd(f"b_norm_bwd{j}", h_in, small['b_norm'][j], dn1, dh)
        sg['b_norm'][j] = dg[0]
    unt = lambda t: t.reshape(2, 2 * WINDOW, 8, WINDOW).transpose(0, 2, 3, 1).reshape(bias_flat.shape)
    sg['rel_bias'] = bias_table_grad(unt(dbias[0]), unt(dbias[1]))[:, :N_BUCKETS].T

    dkv_t = jnp.concatenate([dkp[:, :, WINDOW:], dvp[:, :, WINDOW:]], axis=0).reshape(-1, s)
    sg['b_kv'] = rowsum("b_kv_grad", dkv_t)
    dkvb = dkv_t.astype(BF16)
    (dnkv,) = dx_mm("kv_dx", dkvb, 'w_kv', 0, plain, (), (F32,), a_t=True)
    dw_mm("kv_dw", nkv, dkvb, 'w_kv', 0, dy_t=True)
    dh, dhb, dg, _ = rms_bwd("kv_norm_bwd", h_kv, small['kv_norm'], dnkv, dh)
    sg['kv_norm'] = dg[0]

    sg['a_norm'] = [None] * n_a
    for l in reversed(range(n_a)):
        h_in, n1, qkvt, o_t, rtab, h_mid, mlp_saved = saved[l]
        dh, dhb, _ = mlp_bwd(dh, dhb, h_mid, mlp_saved, l)
        (do_t,) = dx_mm(f"a_wo_dx{l}", dhb, 'a_wo', l, plain, (), (BF16,), out_t=True)
        dw_mm(f"a_wo_dw{l}", o_t, dhb, 'a_wo', l, x_t=True)
        dq_t, dk_t, dv_t, carried = sb_bwd(f"sb_bwd{l}", qkvt, do_t.reshape(d // HEAD_DIM, HEAD_DIM, s), rtab,
                                           ex.bwd_carry(l, gb))
        ex.bwd_done(l, carried)
        dqkv_t = jnp.concatenate([dq_t, dk_t, dv_t], axis=0).reshape(3 * d, s)
        (dn1,) = dx_mm(f"a_qkv_dx{l}", dqkv_t, 'a_wqkv', l, plain, (), (F32,), a_t=True)
        dw_mm(f"a_qkv_dw{l}", n1, dqkv_t, 'a_wqkv', l, dy_t=True)
        dh, dhb, dg, _ = rms_bwd(f"a_norm_bwd{l}", h_in, small['a_norm'][l], dn1, dh)
        sg['a_norm'][l] = dg[0]

    small_grads = {
        'a_norm': jnp.stack(sg['a_norm']), 'kv_norm': sg['kv_norm'], 'b_kv': sg['b_kv'],
        'b_norm': jnp.stack(sg['b_norm']), 'b_bq': jnp.stack(sg['b_bq']), 'b_sinks': jnp.stack(sg['b_sinks']),
        'b_bo': jnp.stack(sg['b_bo']), 'rel_bias': sg['rel_bias'], 'mlp_norm': jnp.stack(sg['mlp_norm']),
        'final_norm': sg['final_norm'],
    }
    return loss_b, dh, gb, small_grads


def _full_shape(name, shard_shape):
    if name in COL_SHARDED:
        return shard_shape[:2] + (N_DEV * shard_shape[2],)
    nl, r, n = shard_shape
    return (nl, N_DEV, r, n)


def _as_w3_shape(name, shard_shape):
    full = _full_shape(name, shard_shape)
    return full if name in COL_SHARDED else (full[0], full[1] * full[2], full[3])


def _as_w3(name, full):
    if name in COL_SHARDED:
        return full
    nl, nd, r, n = full.shape
    return full.reshape(nl, nd * r, n)


AG_GROUPS = {
    0: (('a_wqkv', 0, 1),),
    1: (('a_wo', 0, 2), ('mlp_up', 0, 2), ('mlp_down', 0, 2), ('a_wqkv', 1, 1)),
    2: (('mlp_up', 2, 2), ('mlp_down', 2, 2), ('b_wq', 0, 2), ('b_wo', 0, 2), ('w_kv', 0, 1)),
}
RS_GROUPS = {
    'A': (('mlp_up', 1, 3), ('mlp_down', 1, 3), ('b_wq', 0, 2), ('b_wo', 0, 2), ('w_kv', 0, 1)),
    'B': (('a_wqkv', 1, 1), ('a_wo', 1, 1), ('mlp_up', 0, 1), ('mlp_down', 0, 1)),
    'C': (('a_wqkv', 0, 1), ('a_wo', 0, 1)),
}


class _Exchanges:
    def __init__(self, full0, shards, core, chip, w3, m3, v3):
        self.wbuf = {0: {n: _as_w3(n, full0[n]) for n, _, _ in AG_GROUPS[0]}}
        self.shards, self.core, self.chip = shards, core, chip
        self.w3, self.m3, self.v3 = w3, m3, v3
        self.shard_dims = {n: w3[n].shape[1:] for n in BIG}
        self.parts = {}
        self.out = {}

    def weight(self, name, layer):
        for group, members in AG_GROUPS.items():
            for n, l0, nl in members:
                if n == name and l0 <= layer < l0 + nl:
                    return self.wbuf[group][name], layer - l0
        raise KeyError((name, layer))

    def fwd_carry(self, layer):
        names = [n for n, _, _ in AG_GROUPS[layer + 1]]
        shards = [self.shards[layer + 1][n] for n in names]
        return ag_direct(names, shards, [_full_shape(n, sh.shape) for n, sh in zip(names, shards)])

    def fwd_done(self, layer, carried):
        names = [n for n, _, _ in AG_GROUPS[layer + 1]]
        self.wbuf[layer + 1] = {n: _as_w3(n, f) for n, f in zip(names, ag_forward(layer + 1, names, list(carried)))}

    def grad(self, name, layer):
        for group, members in RS_GROUPS.items():
            for n, l0, nl in members:
                if n == name and l0 <= layer < l0 + nl:
                    return (group, name), layer - l0, _as_w3_shape(name, (nl,) + self.shard_dims[name])
        raise KeyError((name, layer))

    def _sibling_stage(self, group, gb):
        names = [n for n, _, _ in RS_GROUPS[group]]
        shapes = [(nl,) + self.shard_dims[n] for n, _, nl in RS_GROUPS[group]]
        gfull = [gb[(group, n)].reshape(_full_shape(n, sh)) for n, sh in zip(names, shapes)]
        recv = sibling_exchange(group, names, gfull, shapes)
        self.parts[group] = [sibling_sum(f"rs_sibling_sum_{group}_{n}", n in COL_SHARDED, g, r, self.core)
                             for n, g, r in zip(names, gfull, recv)]
        return names, shapes

    def bwd_carry(self, layer, gb):
        names, shapes = self._sibling_stage('A' if layer == 1 else 'B', gb)
        return chip_exchange(names, self.parts['A' if layer == 1 else 'B'], shapes)

    def bwd_done(self, layer, carried):
        self._adamw('A' if layer == 1 else 'B', carried)

    def finish(self, gb):
        names, shapes = self._sibling_stage('C', gb)
        ce = chip_exchange(names, self.parts['C'], shapes)
        self._adamw('C', comm_call("rs_chip_exchange_C", ce.build, ce.ins, ce.out_shapes, ce.sems))
        return self.out

    def _adamw(self, group, recv2):
        for (n, l0, _), p, r in zip(RS_GROUPS[group], self.parts[group], recv2):
            self.out[n] = reduce_adamw(f"adamw_{group}_{n}", p, r, self.chip, self.w3[n], self.m3[n], self.v3[n],
                                       l0, self.out.get(n))


def _pack_small(vals):
    flat = jnp.concatenate([vals[n].reshape(-1).astype(F32) for n in SMALL] + [vals['loss'].reshape(-1)])
    rows = -(-flat.shape[0] // 1024) * 8
    return jnp.pad(flat, (0, rows * 128 - flat.shape[0])).reshape(rows, 128)


def _unpack_small(packed, shapes):
    flat = packed.reshape(-1)
    out, off = {}, 0
    for n in SMALL + ['loss']:
        size = int(np.prod(shapes[n]))
        out[n] = flat[off:off + size].reshape(shapes[n])
        off += size
    return out


def kernel(x, a_norm, a_wqkv, a_wo, kv_norm, w_kv, b_kv, b_norm, b_wq, b_bq, b_sinks, b_wo, b_bo, rel_bias, mlp_norm, mlp_up, mlp_down, final_norm, loss_target, m_a_norm, m_a_wqkv, m_a_wo, m_kv_norm, m_w_kv, m_b_kv, m_b_norm, m_b_wq, m_b_bq, m_b_sinks, m_b_wo, m_b_bo, m_rel_bias, m_mlp_norm, m_mlp_up, m_mlp_down, m_final_norm, v_a_norm, v_a_wqkv, v_a_wo, v_kv_norm, v_w_kv, v_b_kv, v_b_norm, v_b_wq, v_b_bq, v_b_sinks, v_b_wo, v_b_bo, v_rel_bias, v_mlp_norm, v_mlp_up, v_mlp_down, v_final_norm):
    w = dict(a_norm=a_norm, a_wqkv=a_wqkv, a_wo=a_wo, kv_norm=kv_norm, w_kv=w_kv, b_kv=b_kv, b_norm=b_norm,
             b_wq=b_wq, b_bq=b_bq, b_sinks=b_sinks, b_wo=b_wo, b_bo=b_bo, rel_bias=rel_bias, mlp_norm=mlp_norm,
             mlp_up=mlp_up, mlp_down=mlp_down, final_norm=final_norm)
    m = dict(a_norm=m_a_norm, a_wqkv=m_a_wqkv, a_wo=m_a_wo, kv_norm=m_kv_norm, w_kv=m_w_kv, b_kv=m_b_kv,
             b_norm=m_b_norm, b_wq=m_b_wq, b_bq=m_b_bq, b_sinks=m_b_sinks, b_wo=m_b_wo, b_bo=m_b_bo,
             rel_bias=m_rel_bias, mlp_norm=m_mlp_norm, mlp_up=m_mlp_up, mlp_down=m_mlp_down, final_norm=m_final_norm)
    v = dict(a_norm=v_a_norm, a_wqkv=v_a_wqkv, a_wo=v_a_wo, kv_norm=v_kv_norm, w_kv=v_w_kv, b_kv=v_b_kv,
             b_norm=v_b_norm, b_wq=v_b_wq, b_bq=v_b_bq, b_sinks=v_b_sinks, b_wo=v_b_wo, b_bo=v_b_bo,
             rel_bias=v_rel_bias, mlp_norm=v_mlp_norm, mlp_up=v_mlp_up, mlp_down=v_mlp_down, final_norm=v_final_norm)
    px, py, pc = _place()
    me = 4 * px + 2 * py + pc
    chip = (2 * px + py).astype(jnp.int32)
    core = pc.astype(jnp.int32)

    as3 = lambda t: t[None] if t.ndim == 2 else t
    w3, m3, v3 = ({n: as3(src[n]) for n in BIG} for src in (w, m, v))
    shards = {g: {n: w3[n][l0:l0 + nl].astype(BF16) for n, l0, nl in members} for g, members in AG_GROUPS.items()}
    an_pad = jnp.zeros((8, 128), F32).at[:a_norm.shape[0]].set(a_norm)
    names0 = [n for n, _, _ in AG_GROUPS[0]]
    full0 = all_gather_weights(names0 + ['a_norm'], [shards[0][n] for n in names0] + [an_pad],
                               [_full_shape(n, shards[0][n].shape) for n in names0] + [(N_DEV, 8, 128)])
    full0 = dict(zip(names0 + ['a_norm'], full0))
    n_a = a_norm.shape[0]
    small = {n: w[n] for n in SMALL}
    small['a_norm'] = full0['a_norm'][:, :n_a].transpose(1, 0, 2).reshape(n_a, -1)

    ex = _Exchanges(full0, shards, core, chip, w3, m3, v3)
    loss_b, grad_x, gb, sgrads = local_step(x[0], loss_target[0], small, ex)
    out = {n: [t.reshape(w[n].shape) for t in bufs] for n, bufs in ex.finish(gb).items()}

    sgrads['loss'] = loss_b[0, :1]
    gathered = all_gather_rows(_pack_small(sgrads))
    shapes = {n: w[n].shape for n in SMALL}
    shapes['a_norm'] = (n_a, a_norm.shape[1] * N_DEV)
    shapes['loss'] = (1,)
    zeros1 = jnp.zeros((1,), F32)

    def packed(src):
        vals = {n: src[n] for n in SMALL}
        vals['a_norm'] = jnp.zeros(shapes['a_norm'], F32)
        vals['loss'] = zeros1
        return _pack_small(vals)

    sm = small_adamw("adamw_small", gathered, packed(w), packed(m), packed(v))
    sm = [_unpack_small(t, shapes) for t in sm]
    g_an = lax.dynamic_slice_in_dim(sm[0]['a_norm'], me * a_norm.shape[1], a_norm.shape[1], axis=1)
    pad = lambda t: jnp.zeros((8, 128), F32).at[:n_a].set(t)
    gathered_an = jnp.zeros((N_DEV, 8, 128), F32).at[0].set(pad(g_an))
    an = small_adamw("adamw_a_norm", gathered_an, pad(a_norm), pad(m_a_norm), pad(v_a_norm))
    for i in range(4):
        sm[i]['a_norm'] = an[i][:n_a]
    for n in BIG:
        for i in range(4):
            sm[i][n] = out[n][i]
    loss = sm[0]['loss'][0]
    return (loss, grad_x[None], *[sm[0][n] for n in WEIGHTS], *[sm[1][n] for n in WEIGHTS],
            *[sm[2][n] for n in WEIGHTS], *[sm[3][n] for n in WEIGHTS])
```

```python
import math

import numpy as np
import jax
import jax.numpy as jnp
from jax import lax
from jax.experimental import pallas as pl
from jax.experimental.pallas import tpu as pltpu

F32 = jnp.float32
BF16 = jnp.bfloat16
MESH = pl.DeviceIdType.MESH

N_DEV = 8
HEAD_DIM = 64
WINDOW = 128
N_BUCKETS = 32
EPS = 1e-5
NEG_INF = -1e30
Q_SCALE = 1.0 / math.sqrt(HEAD_DIM)
LOG2E = 1.4426950408889634

ADAM_LR, ADAM_B1, ADAM_B2, ADAM_EPS, ADAM_WD, ADAM_STEP = 0.001, 0.9, 0.999, 1e-08, 0.01, 10

SB_BQ = 512
SB_BK = 128
SB_DEAD = 160.0
SB_UNSEEN = 1e30
ROW_TILE = 512
VMEM_LIMIT = 56 * 1024 * 1024

WEIGHTS = ['a_norm', 'a_wqkv', 'a_wo', 'kv_norm', 'w_kv', 'b_kv', 'b_norm', 'b_wq', 'b_bq', 'b_sinks', 'b_wo',
           'b_bo', 'rel_bias', 'mlp_norm', 'mlp_up', 'mlp_down', 'final_norm']
BIG = ['a_wqkv', 'a_wo', 'w_kv', 'b_wq', 'b_wo', 'mlp_up', 'mlp_down']
COL_SHARDED = ('a_wqkv', 'mlp_up')
SMALL = ['a_norm', 'kv_norm', 'b_kv', 'b_norm', 'b_bq', 'b_sinks', 'b_bo', 'rel_bias', 'mlp_norm', 'final_norm']


def _params(sem=None):
    return pltpu.CompilerParams(dimension_semantics=sem, vmem_limit_bytes=VMEM_LIMIT)


def _pick(n, cands):
    for c in cands:
        if n % c == 0:
            return c
    raise ValueError(n)


def _tile(n, want):
    return n if n <= want else _pick(n, (want, want // 2, want // 4))


MM_TILE_BUDGET = 36 * 1024 * 1024


def _row_tile(m, contraction, cols, streams):
    weight = 2 * contraction * cols * 2
    for rows in (2048, 1024, 512):
        if m % rows == 0 and weight + 2 * rows * (2 * contraction + cols * sum(streams)) <= MM_TILE_BUDGET:
            return rows
    return _tile(m, 512)


def mm_nn(name, a, w3, layer, epilogue, extras, out_dtypes, a_t=False, out_t=False):
    k, m = a.shape if a_t else a.shape[::-1]
    _, kw, n = w3.shape
    assert kw == k
    tn = _tile(n, 1024)
    tm = _row_tile(m, k, tn, [jnp.dtype(t).itemsize for t in out_dtypes]
                   + [e.dtype.itemsize for e in extras if e.size == m * n])
    ne, no = len(extras), len(out_dtypes)
    a_dim = 0 if a_t else 1

    def body(a_ref, w_ref, *rest):
        ex, outs = rest[:ne], rest[ne:ne + no]
        if out_t:
            acc = lax.dot_general(w_ref[...], a_ref[...], (((0,), (a_dim,)), ((), ())), preferred_element_type=F32)
        else:
            acc = lax.dot_general(a_ref[...], w_ref[...], (((a_dim,), (0,)), ((), ())), preferred_element_type=F32)
        for o, r in zip(outs, epilogue(acc, *[e[...] for e in ex])):
            o[...] = r.astype(o.dtype)

    if out_t:
        tile = pl.BlockSpec((tn, tm), lambda i, j: (j, i))
        vec = pl.BlockSpec((tn, 1), lambda i, j: (j, 0))
        out_shape = (n, m)
    else:
        tile = pl.BlockSpec((tm, tn), lambda i, j: (i, j))
        vec = pl.BlockSpec((1, tn), lambda i, j: (0, j))
        out_shape = (m, n)
    a_spec = pl.BlockSpec((k, tm), lambda i, j: (0, i)) if a_t else pl.BlockSpec((tm, k), lambda i, j: (i, 0))
    return pl.pallas_call(
        body, name=name, grid=(m // tm, n // tn),
        in_specs=[a_spec, pl.BlockSpec((None, k, tn), lambda i, j: (layer, 0, j))]
        + [tile if e.shape == out_shape else vec for e in extras],
        out_specs=[tile] * no,
        out_shape=[jax.ShapeDtypeStruct(out_shape, d) for d in out_dtypes],
        compiler_params=_params(("parallel", "parallel")),
    )(a, w3, *extras)


def mm_nt(name, dy, w3, layer, epilogue, extras, out_dtypes, a_t=False, out_t=False, n_sums=0):
    n, m = dy.shape if a_t else dy.shape[::-1]
    _, k, nw = w3.shape
    assert nw == n and not (out_t and n_sums)
    tko = _tile(k, 1024)
    tm = _row_tile(m, n, tko, [jnp.dtype(t).itemsize for t in out_dtypes]
                   + [e.dtype.itemsize for e in extras if e.size == m * k])
    ne, no = len(extras), len(out_dtypes)
    a_dim = 0 if a_t else 1

    def body(a_ref, w_ref, *rest):
        ex, outs, sums = rest[:ne], rest[ne:ne + no], rest[ne + no:]
        if out_t:
            acc = lax.dot_general(w_ref[...], a_ref[...], (((1,), (a_dim,)), ((), ())), preferred_element_type=F32)
        else:
            acc = lax.dot_general(a_ref[...], w_ref[...], (((a_dim,), (1,)), ((), ())), preferred_element_type=F32)
        res = epilogue(acc, *[e[...] for e in ex])
        for o, v in zip(outs, res):
            o[...] = v.astype(o.dtype)
        if n_sums:
            @pl.when(pl.program_id(0) == 0)
            def _():
                for o in sums:
                    o[...] = jnp.zeros_like(o)

            for o, v in zip(sums, res[no:]):
                o[...] += v

    if out_t:
        tile = pl.BlockSpec((tko, tm), lambda i, ko: (ko, i))
        out_shape = (k, m)
    else:
        tile = pl.BlockSpec((tm, tko), lambda i, ko: (i, ko))
        out_shape = (m, k)
    vec = pl.BlockSpec((1, tko), lambda i, ko: (0, ko))
    a_spec = pl.BlockSpec((n, tm), lambda i, ko: (0, i)) if a_t else pl.BlockSpec((tm, n), lambda i, ko: (i, 0))
    return pl.pallas_call(
        body, name=name, grid=(m // tm, k // tko),
        in_specs=[a_spec, pl.BlockSpec((None, tko, n), lambda i, ko: (layer, ko, 0))]
        + [tile if e.shape == out_shape else vec for e in extras],
        out_specs=[tile] * no + [vec] * n_sums,
        out_shape=[jax.ShapeDtypeStruct(out_shape, d) for d in out_dtypes] + [jax.ShapeDtypeStruct((1, k), F32)] * n_sums,
        compiler_params=_params(("arbitrary" if n_sums else "parallel", "parallel")),
    )(dy, w3, *extras)


def mm_tn(name, x, dy, gbuf, shape, layer, x_t=False, dy_t=False):
    k, s = x.shape if x_t else x.shape[::-1]
    _, kw, n = shape
    assert kw == k and dy.shape == ((n, s) if dy_t else (s, n))
    tkk = _tile(k, 512)
    tn = _tile(n, 1024)

    def body(x_ref, dy_ref, *rest):
        g_out = rest[-1]
        g_out[...] = lax.dot_general(x_ref[...], dy_ref[...], (((1 if x_t else 0,), (1 if dy_t else 0,)), ((), ())),
                                     preferred_element_type=F32).astype(g_out.dtype)

    prev = [] if gbuf is None else [gbuf]
    x_spec = pl.BlockSpec((tkk, s), lambda ki, j: (ki, 0)) if x_t else pl.BlockSpec((s, tkk), lambda ki, j: (0, ki))
    dy_spec = pl.BlockSpec((tn, s), lambda ki, j: (j, 0)) if dy_t else pl.BlockSpec((s, tn), lambda ki, j: (0, j))
    return pl.pallas_call(
        body, name=name, grid=(k // tkk, n // tn),
        in_specs=[x_spec, dy_spec] + [pl.BlockSpec(memory_space=pl.ANY)] * len(prev),
        out_specs=pl.BlockSpec((None, tkk, tn), lambda ki, j: (layer, ki, j)),
        out_shape=jax.ShapeDtypeStruct(shape, BF16),
        input_output_aliases={2: 0} if prev else {},
        compiler_params=_params(("parallel", "parallel")),
    )(x, dy, *prev)


def rms_fwd(name, h, g):
    s, d = h.shape
    tr = _pick(s, (ROW_TILE, 256, 128))

    def body(h_ref, g_ref, o_ref):
        x = h_ref[...]
        r = lax.rsqrt(jnp.mean(x * x, axis=-1, keepdims=True) + EPS)
        o_ref[...] = (x * r * g_ref[...]).astype(o_ref.dtype)

    return pl.pallas_call(
        body, name=name, grid=(s // tr,),
        in_specs=[pl.BlockSpec((tr, d), lambda i: (i, 0)), pl.BlockSpec((1, d), lambda i: (0, 0))],
        out_specs=pl.BlockSpec((tr, d), lambda i: (i, 0)),
        out_shape=jax.ShapeDtypeStruct((s, d), BF16),
        compiler_params=_params(("parallel",)),
    )(h, g.reshape(1, d))


def loss_head(h, g, target):
    s, d = h.shape
    tr = _pick(s, (ROW_TILE, 256, 128))

    def body(h_ref, g_ref, t_ref, dx_ref, dxb_ref, dg_ref, loss_ref):
        i = pl.program_id(0)
        x = h_ref[...]
        r = lax.rsqrt(jnp.mean(x * x, axis=-1, keepdims=True) + EPS)
        xh = x * r
        gw = g_ref[...]
        err = xh * gw - t_ref[...]
        dn_ = err * (1.0 / d)
        dyg = dn_ * gw
        dx = r * (dyg - xh * jnp.mean(dyg * xh, axis=-1, keepdims=True))
        dx_ref[...] = dx
        dxb_ref[...] = dx.astype(BF16)

        @pl.when(i == 0)
        def _():
            dg_ref[...] = jnp.zeros_like(dg_ref)
            loss_ref[...] = jnp.zeros_like(loss_ref)

        dg_ref[...] += jnp.sum(dn_ * xh, axis=0, keepdims=True)
        per_row = jnp.sum(err * err, axis=-1, keepdims=True) * (0.5 / d)
        loss_ref[...] += jnp.broadcast_to(jnp.sum(per_row, axis=0, keepdims=True), loss_ref.shape)

    row = pl.BlockSpec((tr, d), lambda i: (i, 0))
    vec = pl.BlockSpec((1, d), lambda i: (0, 0))
    return pl.pallas_call(
        body, name="loss_head", grid=(s // tr,),
        in_specs=[row, vec, row],
        out_specs=[row, row, vec, pl.BlockSpec((1, 128), lambda i: (0, 0))],
        out_shape=[jax.ShapeDtypeStruct((s, d), F32), jax.ShapeDtypeStruct((s, d), BF16),
                   jax.ShapeDtypeStruct((1, d), F32), jax.ShapeDtypeStruct((1, 128), F32)],
        compiler_params=_params(("arbitrary",)),
    )(h, g.reshape(1, d), target)


def colsum(name, x):
    s, n = x.shape
    tr = _pick(s, (ROW_TILE, 256, 128))

    def body(x_ref, o_ref):
        @pl.when(pl.program_id(0) == 0)
        def _():
            o_ref[...] = jnp.zeros_like(o_ref)

        o_ref[...] += jnp.sum(x_ref[...].astype(F32), axis=0, keepdims=True)

    return pl.pallas_call(
        body, name=name, grid=(s // tr,),
        in_specs=[pl.BlockSpec((tr, n), lambda i: (i, 0))],
        out_specs=pl.BlockSpec((1, n), lambda i: (0, 0)),
        out_shape=jax.ShapeDtypeStruct((1, n), F32),
        compiler_params=_params(("arbitrary",)),
    )(x)


def rowsum(name, x):
    n, s = x.shape
    ts = _pick(s, (1024, 512, 256, 128))

    def body(x_ref, o_ref):
        @pl.when(pl.program_id(0) == 0)
        def _():
            o_ref[...] = jnp.zeros_like(o_ref)

        o_ref[...] += jnp.sum(x_ref[...].astype(F32), axis=1, keepdims=True)

    return pl.pallas_call(
        body, name=name, grid=(s // ts,),
        in_specs=[pl.BlockSpec((n, ts), lambda i: (0, i))],
        out_specs=pl.BlockSpec((n, 1), lambda i: (0, 0)),
        out_shape=jax.ShapeDtypeStruct((n, 1), F32),
        compiler_params=_params(("arbitrary",)),
    )(x)[:, 0]


def _tri_rows(reverse):
    i = np.arange(SB_BK)
    tri = (i[None, :] >= i[:, None]) if reverse else (i[None, :] <= i[:, None])
    tri = np.concatenate([tri, tri], axis=1)
    return jnp.asarray(np.concatenate([tri, np.ones((8, 2 * SB_BK), bool)], axis=0), BF16)


def _hi_lo_rows(x):
    hi = x.astype(BF16)
    lo = (x - hi.astype(F32)).astype(BF16)
    return jnp.concatenate([hi, lo], axis=0)


def _softplus2(zs):
    neg_abs = lax.bitcast_convert_type(lax.bitcast_convert_type(zs, jnp.uint32) | jnp.uint32(0x80000000), F32)
    return jnp.maximum(zs, 0.0) + jnp.log2(1.0 + jnp.exp2(neg_abs))


def _pair_mask(first_rel_block, bq):
    key = lax.broadcasted_iota(jnp.int32, (2 * SB_BK, bq), 0) + first_rel_block * SB_BK
    qry = lax.broadcasted_iota(jnp.int32, (2 * SB_BK, bq), 1)
    return key < qry


def _row_of(table8, sub8, r):
    return jnp.sum(jnp.where(sub8 == r, table8, 0.0), axis=0, keepdims=True)


def _keys(j0):
    return pl.ds(pl.multiple_of(j0 * SB_BK, 2 * SB_BK), 2 * SB_BK)


class Carry:
    def __init__(self, build, ins, out_shapes, sems):
        self.build, self.ins, self.out_shapes, self.sems = build, list(ins), list(out_shapes), list(sems)


def _carried(carry, rest, n_out, n_scratch, first, last):
    n_ci = len(carry.ins) if carry else 0
    n_co = len(carry.out_shapes) if carry else 0
    cin, outs = rest[:n_ci], rest[n_ci:n_ci + n_out]
    cout = rest[n_ci + n_out:n_ci + n_out + n_co]
    scratch = rest[n_ci + n_out + n_co:n_ci + n_out + n_co + n_scratch]
    csems = rest[n_ci + n_out + n_co + n_scratch:]

    def start():
        if carry:
            @pl.when(first)
            def _():
                for cp in carry.build(cin, cout, *csems):
                    cp.start()

    def wait():
        if carry:
            @pl.when(last)
            def _():
                for cp in carry.build(cin, cout, *csems):
                    cp.wait()

    return outs, scratch, start, wait


def _contract0(a, b):
    return lax.dot_general(a, b, (((0,), (0,)), ((), ())), preferred_element_type=F32)


def _contract1(a, b):
    return lax.dot_general(a, b, (((1,), (1,)), ((), ())), preferred_element_type=F32)


def sb_fwd(name, qkvt, exchange=None):
    nh, dh, s = qkvt.shape[0] // 3, qkvt.shape[1], qkvt.shape[2]
    bq = SB_BQ
    per_q = bq // SB_BK
    nkb = s // SB_BK
    assert s % bq == 0 and per_q == 4 and nkb % 8 == 0

    def body(q_ref, k_ref, v_ref, a_ref, *rest):
        head = pl.program_id(0)
        (o_ref, rtab_ref), (acc, zbuf, wbuf), start_carried, wait_carried = _carried(
            exchange, rest, 2, 3, head == 0, head == nh - 1)
        start_carried()
        tri = a_ref[...]
        sub8 = lax.broadcasted_iota(jnp.int32, (8, bq), 0)
        rtab_ref[...] = jnp.full(rtab_ref.shape, SB_UNSEEN, F32)
        kf = k_ref[...].astype(F32)
        k_max2 = jnp.max(jnp.sum(kf * kf, axis=0, keepdims=True), axis=1, keepdims=True)

        def query_block(i, _):
            lanes = pl.ds(pl.multiple_of(i * bq, bq), bq)
            qb = q_ref[:, lanes] * Q_SCALE
            acc[...] = jnp.zeros_like(acc)
            qf = qb.astype(F32)
            bound = jnp.sqrt(jnp.sum(qf * qf, axis=0, keepdims=True) * k_max2) * (1.001 * LOG2E)

            def scores(j0):
                return _contract0(k_ref[:, _keys(j0)], qb) * LOG2E

            def pair(j0, slot, run, rt8, mask, has_prev):
                zs = zbuf[slot]
                zbuf[1 - slot] = scores(jnp.maximum(j0 - 2, 0))
                if has_prev:
                    acc[...] += jnp.dot(v_ref[:, _keys(j0 + 2)], wbuf[1 - slot], preferred_element_type=F32)
                p = _softplus2(zs)
                if mask is not None:
                    p = jnp.where(mask, p, 0.0)
                cr1 = jnp.dot(tri, _hi_lo_rows(p[SB_BK:]), preferred_element_type=F32)
                cr0 = jnp.dot(tri, _hi_lo_rows(p[:SB_BK]), preferred_element_type=F32)
                run1 = run + cr1[SB_BK:SB_BK + 1]
                w = jnp.exp2(jnp.concatenate([zs[:SB_BK] - cr0[:SB_BK] - run1, zs[SB_BK:] - cr1[:SB_BK] - run],
                                             axis=0))
                if mask is not None:
                    w = jnp.where(mask, w, 0.0)
                wbuf[slot] = w.astype(BF16)
                rt8 = jnp.where(j0 % 8 == 6, SB_UNSEEN, rt8)
                rt8 = jnp.where(sub8 == (j0 + 1) % 8, run, jnp.where(sub8 == j0 % 8, run1, rt8))
                rtab_ref[pl.ds(pl.multiple_of((j0 // 8) * 8, 8), 8), lanes] = rt8
                return run1 + cr0[SB_BK:SB_BK + 1], rt8

            def alive(run):
                return jnp.min(run - bound) < SB_DEAD

            top = i * per_q
            zbuf[0] = scores(top + 2)
            state = (jnp.zeros((1, bq), F32), jnp.full((8, bq), SB_UNSEEN, F32))
            state = pair(top + 2, 0, *state, _pair_mask(2, bq), False)
            state = pair(top, 1, *state, _pair_mask(0, bq), True)

            def step(c):
                it, pairs, _, run, rt8 = c
                j0 = top - 2 - 4 * it
                run, rt8 = pair(j0, 0, run, rt8, None, True)
                go = alive(run)
                run, rt8 = lax.cond(go, lambda r, t: pair(j0 - 2, 1, r, t, None, True), lambda r, t: (r, t), run, rt8)
                return it + 1, pairs + 1 + go.astype(jnp.int32), go & alive(run), run, rt8

            pairs = lax.while_loop(lambda c: (c[0] < i) & c[2], step, (0, 0, alive(state[0]), *state))[1]
            acc[...] += jnp.dot(v_ref[:, _keys(top - 2 * pairs)], wbuf[(pairs + 1) % 2], preferred_element_type=F32)
            o_ref[:, lanes] = acc[...].astype(o_ref.dtype)
            return 0

        lax.fori_loop(0, s // bq, query_block, 0)
        wait_carried()

    def head_spec(offset, rows):
        return pl.BlockSpec((None, rows, s), lambda h: (h + offset, 0, 0))

    hbm = pl.BlockSpec(memory_space=pl.ANY)
    c_ins, c_outs, c_sems = (exchange.ins, exchange.out_shapes, exchange.sems) if exchange else ([], [], [])
    outs = pl.pallas_call(
        body, name=name, grid=(nh,),
        in_specs=[head_spec(0, dh), head_spec(nh, dh), head_spec(2 * nh, dh),
                  pl.BlockSpec((SB_BK + 8, 2 * SB_BK), lambda h: (0, 0))] + [hbm] * len(c_ins),
        out_specs=[head_spec(0, dh), head_spec(0, nkb)] + [hbm] * len(c_outs),
        out_shape=[jax.ShapeDtypeStruct((nh, dh, s), BF16), jax.ShapeDtypeStruct((nh, nkb, s), F32)] + c_outs,
        scratch_shapes=[pltpu.VMEM((dh, bq), F32), pltpu.VMEM((2, 2 * SB_BK, bq), F32),
                        pltpu.VMEM((2, 2 * SB_BK, bq), BF16)] + c_sems,
        compiler_params=_params(("arbitrary",)),
    )(qkvt, qkvt, qkvt, _tri_rows(True), *c_ins)
    return outs[0], outs[1], outs[2:]


def sb_bwd(name, qkvt, dot_, rtab, exchange=None):
    nh, dh, s = qkvt.shape[0] // 3, qkvt.shape[1], qkvt.shape[2]
    bq = SB_BQ
    per_q = bq // SB_BK
    nkb = s // SB_BK

    def body(qt_ref, kt_ref, vt_ref, dot_ref, rtab_ref, ar_ref, af_ref, *rest):
        head = pl.program_id(0)
        (dq_ref, dk_ref, dv_ref), (dq_acc, dk_acc, dv_acc, zbuf, dwbuf, dzbuf, wbuf), start_carried, wait_carried = \
            _carried(exchange, rest, 3, 7, head == 0, head == nh - 1)
        start_carried()
        dk_acc[...] = jnp.zeros_like(dk_acc)
        dv_acc[...] = jnp.zeros_like(dv_acc)
        tri_rev = ar_ref[...][:SB_BK]
        tri_fwd = af_ref[...]
        sub8 = lax.broadcasted_iota(jnp.int32, (8, bq), 0)

        def query_block(i, _):
            lanes = pl.ds(pl.multiple_of(i * bq, bq), bq)
            qtb = qt_ref[:, lanes] * Q_SCALE
            dotb = dot_ref[:, lanes]
            dq_acc[...] = jnp.zeros_like(dq_acc)
            last_j = i * per_q + 2
            seen = jnp.max(jnp.where(rtab_ref[:, lanes] < 0.1 * SB_UNSEEN, 1.0, 0.0), axis=1, keepdims=True)
            pairs = jnp.clip((jnp.sum(seen).astype(jnp.int32) - per_q) // 2, 0, 2 * i)
            odd = pairs % 2
            first_j = i * per_q - 2 * pairs

            def issue(j0, slot):
                zbuf[slot] = _contract0(kt_ref[:, _keys(j0)], qtb) * LOG2E
                dwbuf[slot] = _contract0(vt_ref[:, _keys(j0)], dotb)

            def retire(j0, slot):
                keys = _keys(j0)
                dq_acc[...] += jnp.dot(kt_ref[:, keys], dzbuf[slot], preferred_element_type=F32)
                dk_acc[:, keys] += _contract1(qtb, dzbuf[slot])
                dv_acc[:, keys] += _contract1(dotb, wbuf[slot])

            def pair(j0, slot, g_run, mask):
                zs = zbuf[slot]
                dw = dwbuf[slot]
                issue(jnp.minimum(j0 + 2, last_j), 1 - slot)
                retire(jnp.maximum(j0 - 2, first_j), 1 - slot)
                p_raw = _softplus2(zs)
                p = p_raw if mask is None else jnp.where(mask, p_raw, 0.0)
                c0 = jnp.dot(tri_rev, _hi_lo_rows(p[:SB_BK]), preferred_element_type=F32)
                c1 = jnp.dot(tri_rev, _hi_lo_rows(p[SB_BK:]), preferred_element_type=F32)
                rt8 = rtab_ref[pl.ds(pl.multiple_of((j0 // 8) * 8, 8), 8), lanes]
                r0 = _row_of(rt8, sub8, j0 % 8)
                r1 = _row_of(rt8, sub8, (j0 + 1) % 8)
                w = jnp.exp2(jnp.concatenate([zs[:SB_BK] - c0 - r0, zs[SB_BK:] - c1 - r1], axis=0))
                if mask is not None:
                    w = jnp.where(mask, w, 0.0)
                g = w * dw
                gg0 = jnp.dot(tri_fwd, _hi_lo_rows(g[:SB_BK]), preferred_element_type=F32)
                gg1 = jnp.dot(tri_fwd, _hi_lo_rows(g[SB_BK:]), preferred_element_type=F32)
                g_run1 = g_run + gg0[SB_BK:SB_BK + 1]
                g_pre = jnp.concatenate([gg0[:SB_BK] + g_run, gg1[:SB_BK] + g_run1], axis=0)
                dz = g - jnp.exp2(zs - p_raw) * g_pre
                if mask is not None:
                    dz = jnp.where(mask, dz, 0.0)
                dzbuf[slot] = dz.astype(BF16)
                wbuf[slot] = w.astype(BF16)
                return g_run1 + gg1[SB_BK:SB_BK + 1]

            issue(first_j, odd)
            dzbuf[...] = jnp.zeros(dzbuf.shape, BF16)
            wbuf[...] = jnp.zeros(wbuf.shape, BF16)

            def step(it, g_run):
                g_run = pair(4 * it, 0, g_run, None)
                return pair(4 * it + 2, 1, g_run, None)

            g_run = lax.cond(odd == 1, lambda g: pair(first_j, 1, g, None), lambda g: g, jnp.zeros((1, bq), F32))
            g_run = lax.fori_loop(i - pairs // 2, i, step, g_run)
            g_run = pair(last_j - 2, 0, g_run, _pair_mask(0, bq))
            pair(last_j, 1, g_run, _pair_mask(2, bq))
            retire(last_j, 1)
            dq_ref[:, lanes] = (dq_acc[...] * Q_SCALE).astype(dq_ref.dtype)
            return 0

        lax.fori_loop(0, s // bq, query_block, 0)
        dk_ref[...] = dk_acc[...].astype(dk_ref.dtype)
        dv_ref[...] = dv_acc[...].astype(dv_ref.dtype)
        wait_carried()

    def head_spec(offset, rows):
        return pl.BlockSpec((None, rows, s), lambda h: (h + offset, 0, 0))

    aspec = pl.BlockSpec((SB_BK + 8, 2 * SB_BK), lambda h: (0, 0))
    pair_f32 = pltpu.VMEM((2, 2 * SB_BK, bq), F32)
    pair_bf16 = pltpu.VMEM((2, 2 * SB_BK, bq), BF16)
    hbm = pl.BlockSpec(memory_space=pl.ANY)
    c_ins, c_outs, c_sems = (exchange.ins, exchange.out_shapes, exchange.sems) if exchange else ([], [], [])
    outs = pl.pallas_call(
        body, name=name, grid=(nh,),
        in_specs=[head_spec(0, dh), head_spec(nh, dh), head_spec(2 * nh, dh), head_spec(0, dh), head_spec(0, nkb),
                  aspec, aspec] + [hbm] * len(c_ins),
        out_specs=[head_spec(0, dh)] * 3 + [hbm] * len(c_outs),
        out_shape=[jax.ShapeDtypeStruct((nh, dh, s), BF16)] * 3 + c_outs,
        scratch_shapes=[pltpu.VMEM((dh, bq), F32), pltpu.VMEM((dh, s), F32), pltpu.VMEM((dh, s), F32),
                        pair_f32, pair_f32, pair_bf16, pair_bf16] + c_sems,
        compiler_params=_params(("arbitrary",)),
    )(qkvt, qkvt, qkvt, dot_, rtab, _tri_rows(True), _tri_rows(False), *c_ins)
    return outs[0], outs[1], outs[2], outs[3:]


SWA_QB = 2


def _swa_probs(qt, kt, bias_t, sink, i):
    cols = qt.shape[1]
    sc = _contract0(kt, qt) + bias_t
    kj = lax.broadcasted_iota(jnp.int32, (2 * WINDOW, cols), 0)
    qi = lax.broadcasted_iota(jnp.int32, (2 * WINDOW, cols), 1) & (WINDOW - 1)
    dist = qi + WINDOW - kj
    valid = (dist >= 0) & (dist < WINDOW) & ((kj >= WINDOW) | (i > 0))
    sc = jnp.where(valid, sc, NEG_INF)
    mx = jnp.maximum(jnp.max(sc, axis=0, keepdims=True), sink)
    p = jnp.exp(sc - mx)
    p_sink = jnp.exp(sink - mx)
    inv = 1.0 / (jnp.sum(p, axis=0, keepdims=True) + p_sink)
    return p, p_sink, inv


def _band(i):
    return pl.ds(pl.multiple_of(i * WINDOW, WINDOW), 2 * WINDOW)


def _heads_to_lanes(blk):
    return jnp.concatenate([blk[r * HEAD_DIM:(r + 1) * HEAD_DIM] for r in range(8)], axis=1)


def _lanes_to_heads(t):
    return jnp.concatenate([t[:, r * WINDOW:(r + 1) * WINDOW] for r in range(8)], axis=0)


def swa_fwd(name, qt, kpt, vpt, bias_t, sink_row):
    d, s = qt.shape
    ng, dh, sp = kpt.shape
    rows, cols = d // ng, SWA_QB * WINDOW
    assert (s // WINDOW) % SWA_QB == 0

    def body(q_ref, k_ref, v_ref, bias_ref, sink_ref, o_ref):
        for u in range(SWA_QB):
            i = pl.program_id(1) * SWA_QB + u
            lanes = slice(u * WINDOW, (u + 1) * WINDOW)
            qb = _heads_to_lanes(q_ref[:, lanes]) * Q_SCALE
            p, _, inv = _swa_probs(qb, k_ref[:, _band(i)], bias_ref[...], sink_ref[...], i)
            o_t = jnp.dot(v_ref[:, _band(i)], p.astype(BF16), preferred_element_type=F32) * inv
            o_ref[:, lanes] = _lanes_to_heads(o_t).astype(o_ref.dtype)

    qspec = pl.BlockSpec((rows, cols), lambda g, i: (g, i))
    kspec = pl.BlockSpec((None, dh, sp), lambda g, i: (g, 0, 0))
    return pl.pallas_call(
        body, name=name, grid=(ng, s // cols),
        in_specs=[qspec, kspec, kspec, pl.BlockSpec((None, 2 * WINDOW, 8 * WINDOW), lambda g, i: (g, 0, 0)),
                  pl.BlockSpec((None, 1, 8 * WINDOW), lambda g, i: (g, 0, 0))],
        out_specs=qspec,
        out_shape=jax.ShapeDtypeStruct(qt.shape, BF16),
        compiler_params=_params(("parallel", "arbitrary")),
    )(qt, kpt, vpt, bias_t, sink_row)


def swa_bwd(name, qt, kpt, vpt, bias_t, sink_row, dot_, dk_in, dv_in):
    d, s = qt.shape
    ng, dh, sp = kpt.shape
    rows, cols = d // ng, SWA_QB * WINDOW

    def body(q_ref, k_ref, v_ref, bias_ref, sink_ref, do_ref, dki_ref, dvi_ref, dq_ref, dk_ref, dv_ref, db_ref, ds_ref):
        @pl.when(pl.program_id(1) == 0)
        def _():
            dk_ref[...] = dki_ref[...]
            dv_ref[...] = dvi_ref[...]
            db_ref[...] = jnp.zeros_like(db_ref)
            ds_ref[...] = jnp.zeros_like(ds_ref)

        for u in range(SWA_QB):
            i = pl.program_id(1) * SWA_QB + u
            band = _band(i)
            lanes = slice(u * WINDOW, (u + 1) * WINDOW)
            qb = _heads_to_lanes(q_ref[:, lanes]) * Q_SCALE
            dob = _heads_to_lanes(do_ref[:, lanes])
            kt = k_ref[:, band]
            p, p_sink, inv = _swa_probs(qb, kt, bias_ref[...], sink_ref[...], i)
            p = p * inv
            dp = _contract0(v_ref[:, band], dob)
            delta = jnp.sum(p * dp, axis=0, keepdims=True)
            dsc = p * (dp - delta)
            ds_ref[...] -= p_sink * inv * delta
            db_ref[...] += dsc
            dscb = dsc.astype(BF16)
            dq_t = jnp.dot(kt, dscb, preferred_element_type=F32) * Q_SCALE
            dq_ref[:, lanes] = _lanes_to_heads(dq_t).astype(dq_ref.dtype)
            dk_ref[:, band] += _contract1(qb, dscb)
            dv_ref[:, band] += _contract1(dob, p.astype(BF16))

    qspec = pl.BlockSpec((rows, cols), lambda g, i: (g, i))
    kspec = pl.BlockSpec((None, dh, sp), lambda g, i: (g, 0, 0))
    bspec = pl.BlockSpec((None, 2 * WINDOW, 8 * WINDOW), lambda g, i: (g, 0, 0))
    sspec = pl.BlockSpec((None, 1, 8 * WINDOW), lambda g, i: (g, 0, 0))
    return pl.pallas_call(
        body, name=name, grid=(ng, s // cols),
        in_specs=[qspec, kspec, kspec, bspec, sspec, qspec, kspec, kspec],
        out_specs=[qspec, kspec, kspec, bspec, sspec],
        out_shape=[jax.ShapeDtypeStruct(qt.shape, BF16), jax.ShapeDtypeStruct(kpt.shape, F32),
                   jax.ShapeDtypeStruct(kpt.shape, F32), jax.ShapeDtypeStruct(bias_t.shape, F32),
                   jax.ShapeDtypeStruct(sink_row.shape, F32)],
        compiler_params=_params(("parallel", "arbitrary")),
    )(qt, kpt, vpt, bias_t, sink_row, dot_, dk_in, dv_in)


def _bucket_onehot():
    qi = np.arange(WINDOW)[:, None]
    kj = np.arange(2 * WINDOW)[None, :]
    n = np.maximum(qi + WINDOW - kj, 0)
    max_exact = N_BUCKETS // 2
    nf = np.maximum(n, 1).astype(np.float64)
    val = np.log(nf / max_exact) / math.log(WINDOW / max_exact) * (N_BUCKETS - max_exact)
    assert np.all(np.abs(val - np.round(val))[(n > max_exact) & (n < WINDOW)] > 1e-3)
    large = np.minimum(max_exact + val.astype(np.int64), N_BUCKETS - 1)
    bucket = np.where(n < max_exact, n, large).reshape(-1)
    onehot = np.zeros((128, bucket.size), np.float32)
    onehot[bucket, np.arange(bucket.size)] = 1.0
    return onehot


def _split3(x):
    a = x.astype(BF16)
    r = x - a.astype(F32)
    b = r.astype(BF16)
    c = (r - b.astype(F32)).astype(BF16)
    return a, b, c


def bias_table(rel_bias):
    nh = rel_bias.shape[1]
    oh = jnp.asarray(_bucket_onehot(), BF16)
    n = oh.shape[1]
    tn = 4096
    rb = jnp.zeros((nh, 128), F32).at[:, :N_BUCKETS].set(rel_bias.T)

    def body(rb_ref, oh_ref, o_ref):
        o_ref[...] = sum(jnp.dot(t, oh_ref[...], preferred_element_type=F32) for t in _split3(rb_ref[...]))

    return pl.pallas_call(
        body, name="bias_table", grid=(n // tn,),
        in_specs=[pl.BlockSpec((nh, 128), lambda i: (0, 0)), pl.BlockSpec((128, tn), lambda i: (0, i))],
        out_specs=pl.BlockSpec((nh, tn), lambda i: (0, i)),
        out_shape=jax.ShapeDtypeStruct((nh, n), F32),
        compiler_params=_params(("parallel",)),
    )(rb, oh)


def bias_table_grad(db0, db1):
    nh, n = db0.shape
    oh = jnp.asarray(_bucket_onehot(), BF16)
    tn = 4096

    def body(a_ref, b_ref, oh_ref, o_ref):
        @pl.when(pl.program_id(0) == 0)
        def _():
            o_ref[...] = jnp.zeros_like(o_ref)

        o_ref[...] += sum(lax.dot_general(t, oh_ref[...], (((1,), (1,)), ((), ())), preferred_element_type=F32)
                          for t in _split3(a_ref[...] + b_ref[...]))

    blk = pl.BlockSpec((nh, tn), lambda i: (0, i))
    return pl.pallas_call(
        body, name="bias_table_grad", grid=(n // tn,),
        in_specs=[blk, blk, pl.BlockSpec((128, tn), lambda i: (0, i))],
        out_specs=pl.BlockSpec((nh, 128), lambda i: (0, 0)),
        out_shape=jax.ShapeDtypeStruct((nh, 128), F32),
        compiler_params=_params(("arbitrary",)),
    )(db0, db1, oh)


def _owner_view(ref, name, d):
    if name == 'a_norm':
        return ref.at[d]
    if name in COL_SHARDED:
        n = ref.shape[2] // N_DEV
        return ref.at[:, :, pl.ds(pl.multiple_of(d * n, 128), n)]
    return ref.at[:, d]


def _place():
    return lax.axis_index("x"), lax.axis_index("y"), lax.axis_index("c")


def _dev(p):
    return 4 * p[0] + 2 * p[1] + p[2]


def _remote(src, dst, send_sem, recv_sem, to):
    return pltpu.make_async_remote_copy(src_ref=src, dst_ref=dst, send_sem=send_sem, recv_sem=recv_sem,
                                        device_id=to, device_id_type=MESH)


def _dma_sems(*shapes):
    return [pltpu.SemaphoreType.DMA(sh) for sh in shapes]


def comm_call(name, build, ins, out_shapes, sems, aliases=None):
    n_in, n_out = len(ins), len(out_shapes)

    def body(*refs):
        copies = build(refs[:n_in], refs[n_in:n_in + n_out], *refs[n_in + n_out:])
        for cp in copies:
            cp.start()
        for cp in copies:
            cp.wait()

    hbm = pl.BlockSpec(memory_space=pl.ANY)
    return pl.pallas_call(
        body, name=name, in_specs=[hbm] * n_in, out_specs=[hbm] * n_out, out_shape=list(out_shapes),
        scratch_shapes=sems, input_output_aliases=aliases or {},
    )(*ins)


def all_gather_weights(names, shards, full_shapes):
    n = len(names)

    def body(*refs):
        ins, outs = refs[:n], refs[n:2 * n]
        send_sems, recv_sems, local_sems = refs[2 * n:]
        x, y, c = _place()
        me, sibling = (x, y, c), (x, y, 1 - c)
        chips = [(1 - x, y), (x, 1 - y), (1 - x, 1 - y)]

        def copy(t, k, block, to, src=None):
            dst = _owner_view(outs[t], names[t], _dev(block))
            return _remote(dst if src is None else src, dst, send_sems.at[t, k], recv_sems.at[t, k], to)

        mine = [pltpu.make_async_copy(ins[t], _owner_view(outs[t], names[t], _dev(me)), local_sems.at[t])
                for t in range(n)]
        for cp in mine:
            cp.start()
        first = []
        for t in range(n):
            first.append(copy(t, 0, me, sibling, src=ins[t]))
            first += [copy(t, 1 + j, me, (*chip, c), src=ins[t]) for j, chip in enumerate(chips)]
        for cp in first:
            cp.start()
        passed = []
        for j, chip in enumerate(chips):
            for t in range(n):
                copy(t, 1 + j, (*chip, c), me).wait_recv()
                fwd = copy(t, 4 + j, (*chip, c), sibling)
                fwd.start()
                passed.append(fwd)
        for t in range(n):
            copy(t, 0, sibling, me).wait_recv()
            for j, chip in enumerate(chips):
                copy(t, 4 + j, (*chip, 1 - c), me).wait_recv()
        for cp in first + passed:
            cp.wait_send()
        for cp in mine:
            cp.wait()

    hbm = pl.BlockSpec(memory_space=pl.ANY)
    return pl.pallas_call(
        body, name="all_gather_layer0",
        in_specs=[hbm] * n, out_specs=[hbm] * n,
        out_shape=[jax.ShapeDtypeStruct(full_shapes[t], shards[t].dtype) for t in range(n)],
        scratch_shapes=_dma_sems((n, 7), (n, 7), (n,)),
    )(*shards)


def ag_direct(names, shards, full_shapes):
    n = len(names)

    def build(ins, outs, send_sems, recv_sems, local_sems):
        x, y, c = _place()
        peers = [(x, y, 1 - c), (1 - x, y, c), (x, 1 - y, c), (1 - x, 1 - y, c)]
        copies = []
        for t in range(n):
            dst = _owner_view(outs[t], names[t], _dev((x, y, c)))
            copies.append(pltpu.make_async_copy(ins[t], dst, local_sems.at[t]))
            copies += [_remote(ins[t], dst, send_sems.at[t, k], recv_sems.at[t, k], to) for k, to in enumerate(peers)]
        return copies

    return Carry(build, shards, [jax.ShapeDtypeStruct(full_shapes[t], shards[t].dtype) for t in range(n)],
                 _dma_sems((n, 4), (n, 4), (n,)))


def ag_forward(tag, names, partial):
    n = len(names)

    def build(ins, outs, send_sems, recv_sems):
        del ins
        x, y, c = _place()
        copies = []
        for t in range(n):
            for k, chip in enumerate([(1 - x, y), (x, 1 - y), (1 - x, 1 - y)]):
                view = _owner_view(outs[t], names[t], _dev((*chip, c)))
                copies.append(_remote(view, view, send_sems.at[t, k], recv_sems.at[t, k], (x, y, 1 - c)))
        return copies

    return comm_call(f"all_gather_forward_{tag}", build, partial, [jax.ShapeDtypeStruct(p.shape, p.dtype) for p in partial],
                     _dma_sems((n, 3), (n, 3)), aliases={t: t for t in range(n)})


def sibling_exchange(tag, names, grads, part_shapes):
    n = len(names)

    def build(ins, outs, send_sems, recv_sems):
        x, y, c = _place()
        return [_remote(_owner_view(ins[t], names[t], 2 * q + 1 - c), outs[t].at[q], send_sems.at[t, q],
                        recv_sems.at[t, q], (x, y, 1 - c)) for t in range(n) for q in range(4)]

    return comm_call(f"rs_sibling_exchange_{tag}", build, grads,
                     [jax.ShapeDtypeStruct((4,) + part_shapes[t], BF16) for t in range(n)], _dma_sems((n, 4), (n, 4)))


def chip_exchange(names, parts, part_shapes):
    n = len(names)

    def build(ins, outs, send_sems, recv_sems):
        x, y, c = _place()
        chips = [(1 - x, y), (x, 1 - y), (1 - x, 1 - y)]
        return [_remote(ins[t].at[2 * chip[0] + chip[1]], outs[t].at[k], send_sems.at[t, k], recv_sems.at[t, k],
                        (*chip, c)) for t in range(n) for k, chip in enumerate(chips)]

    return Carry(build, parts, [jax.ShapeDtypeStruct((3,) + part_shapes[t], BF16) for t in range(n)],
                 _dma_sems((n, 3), (n, 3)))


def all_gather_rows(x):
    r, w = x.shape

    def body(x_ref, out_ref, send_sems, recv_sems, local_sem):
        px, py, pc = _place()
        me = 4 * px + 2 * py + pc
        mine = pltpu.make_async_copy(x_ref, out_ref.at[me], local_sem)
        mine.start()
        copies = []
        for k in range(1, N_DEV):
            peer = (px ^ (k >> 2), py ^ ((k >> 1) & 1), pc ^ (k & 1))
            copies.append(pltpu.make_async_remote_copy(
                src_ref=x_ref, dst_ref=out_ref.at[me], send_sem=send_sems.at[k - 1], recv_sem=recv_sems.at[k - 1],
                device_id=peer, device_id_type=MESH))
        for cp in copies:
            cp.start()
        for k in range(1, N_DEV):
            peer_idx = me ^ k
            pltpu.make_async_remote_copy(
                src_ref=x_ref, dst_ref=out_ref.at[peer_idx], send_sem=send_sems.at[k - 1],
                recv_sem=recv_sems.at[k - 1], device_id=(px, py, pc), device_id_type=MESH).wait_recv()
        for cp in copies:
            cp.wait_send()
        mine.wait()

    vmem = pl.BlockSpec(memory_space=pltpu.VMEM)
    return pl.pallas_call(
        body, name="all_gather_small_grads",
        in_specs=[vmem], out_specs=vmem,
        out_shape=jax.ShapeDtypeStruct((N_DEV, r, w), x.dtype),
        scratch_shapes=[pltpu.SemaphoreType.DMA((N_DEV - 1,)), pltpu.SemaphoreType.DMA((N_DEV - 1,)),
                        pltpu.SemaphoreType.DMA],
    )(x)


def _adamw(w, g, m, v):
    m = ADAM_B1 * m + (1.0 - ADAM_B1) * g
    v = ADAM_B2 * v + (1.0 - ADAM_B2) * (g * g)
    m_hat = m / (1.0 - ADAM_B1 ** ADAM_STEP)
    v_hat = v / (1.0 - ADAM_B2 ** ADAM_STEP)
    return -ADAM_LR * (m_hat / (jnp.sqrt(v_hat) + ADAM_EPS) + ADAM_WD * w), m, v


def sibling_sum(name, col, grads, recv, core):
    _, nl, rows, cols = recv.shape
    tr = _tile(rows, 512)
    rspec = pl.BlockSpec((None, None, tr, cols), lambda q, l, i, c_ref: (q, l, i, 0))
    if col:
        gspec = pl.BlockSpec((None, tr, cols), lambda q, l, i, c_ref: (l, i, 2 * q + c_ref[0]))
    else:
        gspec = pl.BlockSpec((None, None, tr, cols), lambda q, l, i, c_ref: (l, 2 * q + c_ref[0], i, 0))

    def body(c_ref, g_ref, r_ref, o_ref):
        del c_ref
        o_ref[...] = (g_ref[...].astype(F32) + r_ref[...].astype(F32)).astype(BF16)

    return pl.pallas_call(
        body, name=name,
        grid_spec=pltpu.PrefetchScalarGridSpec(num_scalar_prefetch=1, grid=(4, nl, rows // tr),
                                               in_specs=[gspec, rspec], out_specs=rspec),
        out_shape=jax.ShapeDtypeStruct(recv.shape, BF16),
        compiler_params=_params(("parallel", "parallel", "parallel")),
    )(core.reshape(1), grads, recv)


def reduce_adamw(name, parts, recv, chip, w, m, v, l0, prev):
    _, nl, rows, cols = parts.shape
    tr = _tile(rows, 256)

    def body(q_ref, p_ref, r_ref, w_ref, m_ref, v_ref, *rest):
        del q_ref
        g_out, d_out, m_out, v_out = rest[-4:]
        g = ((p_ref[...].astype(F32) + r_ref[0].astype(F32)) + r_ref[1].astype(F32)) + r_ref[2].astype(F32)
        d, mn, vn = _adamw(w_ref[...], g, m_ref[...], v_ref[...])
        g_out[...] = g
        d_out[...] = d
        m_out[...] = mn
        v_out[...] = vn

    blk = pl.BlockSpec((None, tr, cols), lambda l, i, q_ref: (l0 + l, i, 0))
    prev = list(prev) if prev else []
    return pl.pallas_call(
        body, name=name,
        grid_spec=pltpu.PrefetchScalarGridSpec(
            num_scalar_prefetch=1, grid=(nl, rows // tr),
            in_specs=[pl.BlockSpec((None, None, tr, cols), lambda l, i, q_ref: (q_ref[0], l, i, 0)),
                      pl.BlockSpec((3, None, tr, cols), lambda l, i, q_ref: (0, l, i, 0)), blk, blk, blk]
            + [pl.BlockSpec(memory_space=pl.ANY)] * len(prev),
            out_specs=[blk] * 4),
        out_shape=[jax.ShapeDtypeStruct(w.shape, F32)] * 4,
        input_output_aliases={6 + i: i for i in range(len(prev))},
        compiler_params=_params(("parallel", "parallel")),
    )(chip.reshape(1), parts, recv, w, m, v, *prev)


def small_adamw(name, gathered, w, m, v):
    _, r, c = gathered.shape

    def body(ga_ref, w_ref, m_ref, v_ref, g_out, d_out, m_out, v_out):
        g = ga_ref[0]
        for d in range(1, N_DEV):
            g = g + ga_ref[d]
        dl, mn, vn = _adamw(w_ref[...], g, m_ref[...], v_ref[...])
        g_out[...] = g
        d_out[...] = dl
        m_out[...] = mn
        v_out[...] = vn

    return pl.pallas_call(
        body, name=name,
        out_shape=[jax.ShapeDtypeStruct((r, c), F32)] * 4,
        compiler_params=_params(),
    )(gathered, w, m, v)


def _rms(x, g):
    return x * lax.rsqrt(jnp.mean(x * x, axis=-1, keepdims=True) + EPS) * g


def _rms_bwd_epilogue(dn, x, dres, g):
    r = lax.rsqrt(jnp.mean(x * x, axis=-1, keepdims=True) + EPS)
    xh = x * r
    dyg = dn * g
    dx = dres + r * (dyg - xh * jnp.mean(dyg * xh, axis=-1, keepdims=True))
    return dx, dx, jnp.sum(dn * xh, axis=0, keepdims=True), jnp.sum(dx, axis=0, keepdims=True)


def _residual_then_norms(n_terms):
    def epilogue(acc, *ex):
        h = acc
        for t in ex[:n_terms]:
            h = h + t
        return (h,) + tuple(_rms(h, g) for g in ex[n_terms:])
    return epilogue


def local_step(x, target, small, ex):
    s, d = x.shape
    n_a, n_b = small['a_norm'].shape[0], small['b_norm'].shape[0]
    sg = {}
    gb = {}

    def fwd_mm(name, a, wname, layer, epilogue, extras, out_dtypes, **kw):
        return mm_nn(name, a, *ex.weight(wname, layer), epilogue, extras, out_dtypes, **kw)

    def dx_mm(name, dy, wname, layer, epilogue, extras, out_dtypes, **kw):
        return mm_nt(name, dy, *ex.weight(wname, layer), epilogue, extras, out_dtypes, **kw)

    def dw_mm(name, a, dy, wname, layer, **kw):
        key, slab, shape = ex.grad(wname, layer)
        gb[key] = mm_tn(name, a, dy, gb.get(key), shape, slab, **kw)

    plain = lambda acc: (acc,)
    plus_col = lambda acc, b: (acc + b,)

    bias_flat = bias_table(small['rel_bias'])
    bias_t = bias_flat.reshape(2, 8, WINDOW, 2 * WINDOW).transpose(0, 3, 1, 2).reshape(2, 2 * WINDOW, 8 * WINDOW)
    sink_rows = [jnp.repeat(small['b_sinks'][j], WINDOW).reshape(2, 1, 8 * WINDOW) for j in range(n_b)]

    gain = lambda g: g.reshape(1, -1)

    def mlp_fwd(h, n2, layer, next_gains):
        u, a = fwd_mm(f"mlp_up_fwd{layer}", n2, 'mlp_up', layer,
                      lambda acc: (acc, jnp.square(jnp.maximum(acc, 0.0))), (), (BF16, BF16))
        h2, *nexts = fwd_mm(f"mlp_down_fwd{layer}", a, 'mlp_down', layer, _residual_then_norms(1),
                            (h, *[gain(g) for g in next_gains]), (F32,) + (BF16,) * len(next_gains))
        return h2, nexts, (n2, u, a)

    h = x
    saved = []
    n1 = rms_fwd("a_norm_fwd0", h, small['a_norm'][0])
    for l in range(n_a):
        (qkvt,) = fwd_mm(f"a_qkv_fwd{l}", n1, 'a_wqkv', l, plain, (), (BF16,), out_t=True)
        qkvt = qkvt.reshape(3 * d // HEAD_DIM, HEAD_DIM, s)
        o_t, rtab, carried = sb_fwd(f"sb_fwd{l}", qkvt, ex.fwd_carry(l))
        ex.fwd_done(l, carried)
        o_t = o_t.reshape(d, s)
        h_mid, n2 = fwd_mm(f"a_wo_fwd{l}", o_t, 'a_wo', l, _residual_then_norms(1),
                           (h, gain(small['mlp_norm'][l])), (F32, BF16), a_t=True)
        next_gains = [small['a_norm'][l + 1]] if l + 1 < n_a else [small['b_norm'][0], small['kv_norm']]
        h_out, nexts, mlp_saved = mlp_fwd(h_mid, n2, l, next_gains)
        saved.append((h, n1, qkvt, o_t, rtab, h_mid, mlp_saved))
        h, n1 = h_out, nexts[0]
    h_kv, nkv = h, nexts[1]
    (kvt,) = fwd_mm("kv_fwd", nkv, 'w_kv', 0, plus_col, (small['b_kv'].reshape(-1, 1),), (BF16,), out_t=True)
    kvt = kvt.reshape(2, 2, HEAD_DIM, s)
    kpt, vpt = (jnp.pad(t, ((0, 0), (0, 0), (WINDOW, 0))) for t in (kvt[0], kvt[1]))
    for j in range(n_b):
        layer = n_a + j
        (qbt,) = fwd_mm(f"b_q_fwd{j}", n1, 'b_wq', j, plus_col, (small['b_bq'][j].reshape(-1, 1),), (BF16,),
                        out_t=True)
        o_t = swa_fwd(f"swa_fwd{j}", qbt, kpt, vpt, bias_t, sink_rows[j])
        h_mid, n2 = fwd_mm(f"b_wo_fwd{j}", o_t, 'b_wo', j, _residual_then_norms(2),
                           (h, gain(small['b_bo'][j]), gain(small['mlp_norm'][layer])), (F32, BF16), a_t=True)
        h_out, nexts, mlp_saved = mlp_fwd(h_mid, n2, layer, [small['b_norm'][j + 1]] if j + 1 < n_b else [])
        saved.append((h, n1, qbt, o_t, h_mid, mlp_saved))
        h, n1 = h_out, (nexts[0] if nexts else None)

    dh, dhb, dg_final, loss_b = loss_head(h, small['final_norm'], target)
    sg['final_norm'] = dg_final[0]
    sg['mlp_norm'] = [None] * (n_a + n_b)

    def mlp_bwd(dh, dhb, h_mid, mlp_saved, layer):
        n2, u, a = mlp_saved
        (du,) = dx_mm(f"mlp_down_dx{layer}", dhb, 'mlp_down', layer,
                      lambda acc, uu: (acc * (2.0 * jnp.maximum(uu.astype(F32), 0.0)),), (u,), (BF16,))
        dw_mm(f"mlp_down_dw{layer}", a, dhb, 'mlp_down', layer)
        dh2, dh2b, dg, cs = dx_mm(f"mlp_up_dx{layer}", du, 'mlp_up', layer, _rms_bwd_epilogue,
                                  (h_mid, dh, gain(small['mlp_norm'][layer])), (F32, BF16), n_sums=2)
        dw_mm(f"mlp_up_dw{layer}", n2, du, 'mlp_up', layer)
        sg['mlp_norm'][layer] = dg[0]
        return dh2, dh2b, cs

    dkp = jnp.zeros(kpt.shape, F32)
    dvp = jnp.zeros(vpt.shape, F32)
    sg['b_norm'], sg['b_bq'], sg['b_bo'], sg['b_sinks'] = [None] * n_b, [None] * n_b, [None] * n_b, [None] * n_b
    dbias = [None] * n_b
    for j in reversed(range(n_b)):
        layer = n_a + j
        h_in, n1, qbt, o_t, h_mid, mlp_saved = saved[layer]
        dh, dhb, cs = mlp_bwd(dh, dhb, h_mid, mlp_saved, layer)
        sg['b_bo'][j] = cs[0]
        (do_t,) = dx_mm(f"b_wo_dx{j}", dhb, 'b_wo', j, plain, (), (BF16,), out_t=True)
        dw_mm(f"b_wo_dw{j}", o_t, dhb, 'b_wo', j, x_t=True)
        dq_t, dkp, dvp, dbias[j], dsink = swa_bwd(f"swa_bwd{j}", qbt, kpt, vpt, bias_t, sink_rows[j], do_t, dkp, dvp)
        sg['b_sinks'][j] = colsum(f"sink_grad{j}", dsink.reshape(16, WINDOW).T)[0]
        sg['b_bq'][j] = rowsum(f"b_bq_grad{j}", dq_t)
        dh, dhb, dg, _ = dx_mm(f"b_q_dx{j}", dq_t, 'b_wq', j, _rms_bwd_epilogue,
                               (h_in, dh, gain(small['b_norm'][j])), (F32, BF16), a_t=True, n_sums=2)
        dw_mm(f"b_q_dw{j}", n1, dq_t, 'b_wq', j, dy_t=True)
        sg['b_norm'][j] = dg[0]
    unt = lambda t: t.reshape(2, 2 * WINDOW, 8, WINDOW).transpose(0, 2, 3, 1).reshape(bias_flat.shape)
    sg['rel_bias'] = bias_table_grad(unt(dbias[0]), unt(dbias[1]))[:, :N_BUCKETS].T

    dkv_t = jnp.concatenate([dkp[:, :, WINDOW:], dvp[:, :, WINDOW:]], axis=0).reshape(-1, s)
    sg['b_kv'] = rowsum("b_kv_grad", dkv_t)
    dkvb = dkv_t.astype(BF16)
    dh, dhb, dg, _ = dx_mm("kv_dx", dkvb, 'w_kv', 0, _rms_bwd_epilogue, (h_kv, dh, gain(small['kv_norm'])),
                           (F32, BF16), a_t=True, n_sums=2)
    dw_mm("kv_dw", nkv, dkvb, 'w_kv', 0, dy_t=True)
    sg['kv_norm'] = dg[0]

    sg['a_norm'] = [None] * n_a
    for l in reversed(range(n_a)):
        h_in, n1, qkvt, o_t, rtab, h_mid, mlp_saved = saved[l]
        dh, dhb, _ = mlp_bwd(dh, dhb, h_mid, mlp_saved, l)
        (do_t,) = dx_mm(f"a_wo_dx{l}", dhb, 'a_wo', l, plain, (), (BF16,), out_t=True)
        dw_mm(f"a_wo_dw{l}", o_t, dhb, 'a_wo', l, x_t=True)
        dq_t, dk_t, dv_t, carried = sb_bwd(f"sb_bwd{l}", qkvt, do_t.reshape(d // HEAD_DIM, HEAD_DIM, s), rtab,
                                           ex.bwd_carry(l, gb))
        ex.bwd_done(l, carried)
        dqkv_t = jnp.concatenate([dq_t, dk_t, dv_t], axis=0).reshape(3 * d, s)
        dh, dhb, dg, _ = dx_mm(f"a_qkv_dx{l}", dqkv_t, 'a_wqkv', l, _rms_bwd_epilogue,
                               (h_in, dh, gain(small['a_norm'][l])), (F32, BF16), a_t=True, n_sums=2)
        dw_mm(f"a_qkv_dw{l}", n1, dqkv_t, 'a_wqkv', l, dy_t=True)
        sg['a_norm'][l] = dg[0]

    small_grads = {
        'a_norm': jnp.stack(sg['a_norm']), 'kv_norm': sg['kv_norm'], 'b_kv': sg['b_kv'],
        'b_norm': jnp.stack(sg['b_norm']), 'b_bq': jnp.stack(sg['b_bq']), 'b_sinks': jnp.stack(sg['b_sinks']),
        'b_bo': jnp.stack(sg['b_bo']), 'rel_bias': sg['rel_bias'], 'mlp_norm': jnp.stack(sg['mlp_norm']),
        'final_norm': sg['final_norm'],
    }
    return loss_b, dh, gb, small_grads


def _full_shape(name, shard_shape):
    if name in COL_SHARDED:
        return shard_shape[:2] + (N_DEV * shard_shape[2],)
    nl, r, n = shard_shape
    return (nl, N_DEV, r, n)


def _as_w3_shape(name, shard_shape):
    full = _full_shape(name, shard_shape)
    return full if name in COL_SHARDED else (full[0], full[1] * full[2], full[3])


def _as_w3(name, full):
    if name in COL_SHARDED:
        return full
    nl, nd, r, n = full.shape
    return full.reshape(nl, nd * r, n)


AG_GROUPS = {
    0: (('a_wqkv', 0, 1),),
    1: (('a_wo', 0, 2), ('mlp_up', 0, 2), ('mlp_down', 0, 2), ('a_wqkv', 1, 1)),
    2: (('mlp_up', 2, 2), ('mlp_down', 2, 2), ('b_wq', 0, 2), ('b_wo', 0, 2), ('w_kv', 0, 1)),
}
RS_GROUPS = {
    'A': (('mlp_up', 1, 3), ('mlp_down', 1, 3), ('b_wq', 0, 2), ('b_wo', 0, 2), ('w_kv', 0, 1)),
    'B': (('a_wqkv', 1, 1), ('a_wo', 1, 1), ('mlp_up', 0, 1), ('mlp_down', 0, 1)),
    'C': (('a_wqkv', 0, 1), ('a_wo', 0, 1)),
}


class _Exchanges:
    def __init__(self, full0, shards, core, chip, w3, m3, v3):
        self.wbuf = {0: {n: _as_w3(n, full0[n]) for n, _, _ in AG_GROUPS[0]}}
        self.shards, self.core, self.chip = shards, core, chip
        self.w3, self.m3, self.v3 = w3, m3, v3
        self.shard_dims = {n: w3[n].shape[1:] for n in BIG}
        self.parts = {}
        self.out = {}

    def weight(self, name, layer):
        for group, members in AG_GROUPS.items():
            for n, l0, nl in members:
                if n == name and l0 <= layer < l0 + nl:
                    return self.wbuf[group][name], layer - l0
        raise KeyError((name, layer))

    def fwd_carry(self, layer):
        names = [n for n, _, _ in AG_GROUPS[layer + 1]]
        shards = [self.shards[layer + 1][n] for n in names]
        return ag_direct(names, shards, [_full_shape(n, sh.shape) for n, sh in zip(names, shards)])

    def fwd_done(self, layer, carried):
        names = [n for n, _, _ in AG_GROUPS[layer + 1]]
        self.wbuf[layer + 1] = {n: _as_w3(n, f) for n, f in zip(names, ag_forward(layer + 1, names, list(carried)))}

    def grad(self, name, layer):
        for group, members in RS_GROUPS.items():
            for n, l0, nl in members:
                if n == name and l0 <= layer < l0 + nl:
                    return (group, name), layer - l0, _as_w3_shape(name, (nl,) + self.shard_dims[name])
        raise KeyError((name, layer))

    def _sibling_stage(self, group, gb):
        names = [n for n, _, _ in RS_GROUPS[group]]
        shapes = [(nl,) + self.shard_dims[n] for n, _, nl in RS_GROUPS[group]]
        gfull = [gb[(group, n)].reshape(_full_shape(n, sh)) for n, sh in zip(names, shapes)]
        recv = sibling_exchange(group, names, gfull, shapes)
        self.parts[group] = [sibling_sum(f"rs_sibling_sum_{group}_{n}", n in COL_SHARDED, g, r, self.core)
                             for n, g, r in zip(names, gfull, recv)]
        return names, shapes

    def bwd_carry(self, layer, gb):
        names, shapes = self._sibling_stage('A' if layer == 1 else 'B', gb)
        return chip_exchange(names, self.parts['A' if layer == 1 else 'B'], shapes)

    def bwd_done(self, layer, carried):
        self._adamw('A' if layer == 1 else 'B', carried)

    def finish(self, gb):
        names, shapes = self._sibling_stage('C', gb)
        ce = chip_exchange(names, self.parts['C'], shapes)
        self._adamw('C', comm_call("rs_chip_exchange_C", ce.build, ce.ins, ce.out_shapes, ce.sems))
        return self.out

    def _adamw(self, group, recv2):
        for (n, l0, _), p, r in zip(RS_GROUPS[group], self.parts[group], recv2):
            self.out[n] = reduce_adamw(f"adamw_{group}_{n}", p, r, self.chip, self.w3[n], self.m3[n], self.v3[n],
                                       l0, self.out.get(n))


def _pack_small(vals):
    flat = jnp.concatenate([vals[n].reshape(-1).astype(F32) for n in SMALL] + [vals['loss'].reshape(-1)])
    rows = -(-flat.shape[0] // 1024) * 8
    return jnp.pad(flat, (0, rows * 128 - flat.shape[0])).reshape(rows, 128)


def _unpack_small(packed, shapes):
    flat = packed.reshape(-1)
    out, off = {}, 0
    for n in SMALL + ['loss']:
        size = int(np.prod(shapes[n]))
        out[n] = flat[off:off + size].reshape(shapes[n])
        off += size
    return out


def kernel(x, a_norm, a_wqkv, a_wo, kv_norm, w_kv, b_kv, b_norm, b_wq, b_bq, b_sinks, b_wo, b_bo, rel_bias, mlp_norm, mlp_up, mlp_down, final_norm, loss_target, m_a_norm, m_a_wqkv, m_a_wo, m_kv_norm, m_w_kv, m_b_kv, m_b_norm, m_b_wq, m_b_bq, m_b_sinks, m_b_wo, m_b_bo, m_rel_bias, m_mlp_norm, m_mlp_up, m_mlp_down, m_final_norm, v_a_norm, v_a_wqkv, v_a_wo, v_kv_norm, v_w_kv, v_b_kv, v_b_norm, v_b_wq, v_b_bq, v_b_sinks, v_b_wo, v_b_bo, v_rel_bias, v_mlp_norm, v_mlp_up, v_mlp_down, v_final_norm):
    w = dict(a_norm=a_norm, a_wqkv=a_wqkv, a_wo=a_wo, kv_norm=kv_norm, w_kv=w_kv, b_kv=b_kv, b_norm=b_norm,
             b_wq=b_wq, b_bq=b_bq, b_sinks=b_sinks, b_wo=b_wo, b_bo=b_bo, rel_bias=rel_bias, mlp_norm=mlp_norm,
             mlp_up=mlp_up, mlp_down=mlp_down, final_norm=final_norm)
    m = dict(a_norm=m_a_norm, a_wqkv=m_a_wqkv, a_wo=m_a_wo, kv_norm=m_kv_norm, w_kv=m_w_kv, b_kv=m_b_kv,
             b_norm=m_b_norm, b_wq=m_b_wq, b_bq=m_b_bq, b_sinks=m_b_sinks, b_wo=m_b_wo, b_bo=m_b_bo,
             rel_bias=m_rel_bias, mlp_norm=m_mlp_norm, mlp_up=m_mlp_up, mlp_down=m_mlp_down, final_norm=m_final_norm)
    v = dict(a_norm=v_a_norm, a_wqkv=v_a_wqkv, a_wo=v_a_wo, kv_norm=v_kv_norm, w_kv=v_w_kv, b_kv=v_b_kv,
             b_norm=v_b_norm, b_wq=v_b_wq, b_bq=v_b_bq, b_sinks=v_b_sinks, b_wo=v_b_wo, b_bo=v_b_bo,
             rel_bias=v_rel_bias, mlp_norm=v_mlp_norm, mlp_up=v_mlp_up, mlp_down=v_mlp_down, final_norm=v_final_norm)
    px, py, pc = _place()
    me = 4 * px + 2 * py + pc
    chip = (2 * px + py).astype(jnp.int32)
    core = pc.astype(jnp.int32)

    as3 = lambda t: t[None] if t.ndim == 2 else t
    w3, m3, v3 = ({n: as3(src[n]) for n in BIG} for src in (w, m, v))
    shards = {g: {n: w3[n][l0:l0 + nl].astype(BF16) for n, l0, nl in members} for g, members in AG_GROUPS.items()}
    an_pad = jnp.zeros((8, 128), F32).at[:a_norm.shape[0]].set(a_norm)
    names0 = [n for n, _, _ in AG_GROUPS[0]]
    full0 = all_gather_weights(names0 + ['a_norm'], [shards[0][n] for n in names0] + [an_pad],
                               [_full_shape(n, shards[0][n].shape) for n in names0] + [(N_DEV, 8, 128)])
    full0 = dict(zip(names0 + ['a_norm'], full0))
    n_a = a_norm.shape[0]
    small = {n: w[n] for n in SMALL}
    small['a_norm'] = full0['a_norm'][:, :n_a].transpose(1, 0, 2).reshape(n_a, -1)

    ex = _Exchanges(full0, shards, core, chip, w3, m3, v3)
    loss_b, grad_x, gb, sgrads = local_step(x[0], loss_target[0], small, ex)
    out = {n: [t.reshape(w[n].shape) for t in bufs] for n, bufs in ex.finish(gb).items()}

    sgrads['loss'] = loss_b[0, :1]
    gathered = all_gather_rows(_pack_small(sgrads))
    shapes = {n: w[n].shape for n in SMALL}
    shapes['a_norm'] = (n_a, a_norm.shape[1] * N_DEV)
    shapes['loss'] = (1,)
    zeros1 = jnp.zeros((1,), F32)

    def packed(src):
        vals = {n: src[n] for n in SMALL}
        vals['a_norm'] = jnp.zeros(shapes['a_norm'], F32)
        vals['loss'] = zeros1
        return _pack_small(vals)

    sm = small_adamw("adamw_small", gathered, packed(w), packed(m), packed(v))
    sm = [_unpack_small(t, shapes) for t in sm]
    g_an = lax.dynamic_slice_in_dim(sm[0]['a_norm'], me * a_norm.shape[1], a_norm.shape[1], axis=1)
    pad = lambda t: jnp.zeros((8, 128), F32).at[:n_a].set(t)
    gathered_an = jnp.zeros((N_DEV, 8, 128), F32).at[0].set(pad(g_an))
    an = small_adamw("adamw_a_norm", gathered_an, pad(a_norm), pad(m_a_norm), pad(v_a_norm))
    for i in range(4):
        sm[i]['a_norm'] = an[i][:n_a]
    for n in BIG:
        for i in range(4):
            sm[i][n] = out[n][i]
    loss = sm[0]['loss'][0]
    return (loss, grad_x[None], *[sm[0][n] for n in WEIGHTS], *[sm[1][n] for n in WEIGHTS],
            *[sm[2][n] for n in WEIGHTS], *[sm[3][n] for n in WEIGHTS])
```

```python
import math

import numpy as np
import jax
import jax.numpy as jnp
from jax import lax
from jax.experimental import pallas as pl
from jax.experimental.pallas import tpu as pltpu

F32 = jnp.float32
BF16 = jnp.bfloat16
MESH = pl.DeviceIdType.MESH

N_DEV = 8
HEAD_DIM = 64
WINDOW = 128
N_BUCKETS = 32
EPS = 1e-5
NEG_INF = -1e30
Q_SCALE = 1.0 / math.sqrt(HEAD_DIM)
LOG2E = 1.4426950408889634

ADAM_LR, ADAM_B1, ADAM_B2, ADAM_EPS, ADAM_WD, ADAM_STEP = 0.001, 0.9, 0.999, 1e-08, 0.01, 10

SB_BQ = 512
SB_BK = 128
SB_DEAD = 160.0
SB_UNSEEN = 1e30
ROW_TILE = 512
VMEM_LIMIT = 56 * 1024 * 1024

WEIGHTS = ['a_norm', 'a_wqkv', 'a_wo', 'kv_norm', 'w_kv', 'b_kv', 'b_norm', 'b_wq', 'b_bq', 'b_sinks', 'b_wo',
           'b_bo', 'rel_bias', 'mlp_norm', 'mlp_up', 'mlp_down', 'final_norm']
BIG = ['a_wqkv', 'a_wo', 'w_kv', 'b_wq', 'b_wo', 'mlp_up', 'mlp_down']
COL_SHARDED = ('a_wqkv', 'mlp_up')
SMALL = ['a_norm', 'kv_norm', 'b_kv', 'b_norm', 'b_bq', 'b_sinks', 'b_bo', 'rel_bias', 'mlp_norm', 'final_norm']


def _params(sem=None):
    return pltpu.CompilerParams(dimension_semantics=sem, vmem_limit_bytes=VMEM_LIMIT)


def _pick(n, cands):
    for c in cands:
        if n % c == 0:
            return c
    raise ValueError(n)


def _tile(n, want):
    return n if n <= want else _pick(n, (want, want // 2, want // 4))


MM_TILE_BUDGET = 36 * 1024 * 1024


def _row_tile(m, contraction, cols, streams):
    weight = 2 * contraction * cols * 2
    for rows in (2048, 1024, 512):
        if m % rows == 0 and weight + 2 * rows * (2 * contraction + cols * sum(streams)) <= MM_TILE_BUDGET:
            return rows
    return _tile(m, 512)


def mm_nn(name, a, w3, layer, epilogue, extras, out_dtypes, a_t=False, out_t=False):
    k, m = a.shape if a_t else a.shape[::-1]
    _, kw, n = w3.shape
    assert kw == k
    tn = _tile(n, 1024)
    tm = _row_tile(m, k, tn, [jnp.dtype(t).itemsize for t in out_dtypes]
                   + [e.dtype.itemsize for e in extras if e.size == m * n])
    ne, no = len(extras), len(out_dtypes)
    a_dim = 0 if a_t else 1

    def body(a_ref, w_ref, *rest):
        ex, outs = rest[:ne], rest[ne:ne + no]
        if out_t:
            acc = lax.dot_general(w_ref[...], a_ref[...], (((0,), (a_dim,)), ((), ())), preferred_element_type=F32)
        else:
            acc = lax.dot_general(a_ref[...], w_ref[...], (((a_dim,), (0,)), ((), ())), preferred_element_type=F32)
        for o, r in zip(outs, epilogue(acc, *[e[...] for e in ex])):
            o[...] = r.astype(o.dtype)

    if out_t:
        tile = pl.BlockSpec((tn, tm), lambda i, j: (j, i))
        vec = pl.BlockSpec((tn, 1), lambda i, j: (j, 0))
        out_shape = (n, m)
    else:
        tile = pl.BlockSpec((tm, tn), lambda i, j: (i, j))
        vec = pl.BlockSpec((1, tn), lambda i, j: (0, j))
        out_shape = (m, n)
    a_spec = pl.BlockSpec((k, tm), lambda i, j: (0, i)) if a_t else pl.BlockSpec((tm, k), lambda i, j: (i, 0))
    return pl.pallas_call(
        body, name=name, grid=(m // tm, n // tn),
        in_specs=[a_spec, pl.BlockSpec((None, k, tn), lambda i, j: (layer, 0, j))]
        + [tile if e.shape == out_shape else vec for e in extras],
        out_specs=[tile] * no,
        out_shape=[jax.ShapeDtypeStruct(out_shape, d) for d in out_dtypes],
        compiler_params=_params(("parallel", "parallel")),
    )(a, w3, *extras)


def mm_nt(name, dy, w3, layer, epilogue, extras, out_dtypes, a_t=False, out_t=False, n_sums=0):
    n, m = dy.shape if a_t else dy.shape[::-1]
    _, k, nw = w3.shape
    assert nw == n and not (out_t and n_sums)
    tko = _tile(k, 1024)
    tm = _row_tile(m, n, tko, [jnp.dtype(t).itemsize for t in out_dtypes]
                   + [e.dtype.itemsize for e in extras if e.size == m * k])
    ne, no = len(extras), len(out_dtypes)
    a_dim = 0 if a_t else 1

    def body(a_ref, w_ref, *rest):
        ex, outs, sums = rest[:ne], rest[ne:ne + no], rest[ne + no:]
        if out_t:
            acc = lax.dot_general(w_ref[...], a_ref[...], (((1,), (a_dim,)), ((), ())), preferred_element_type=F32)
        else:
            acc = lax.dot_general(a_ref[...], w_ref[...], (((a_dim,), (1,)), ((), ())), preferred_element_type=F32)
        res = epilogue(acc, *[e[...] for e in ex])
        for o, v in zip(outs, res):
            o[...] = v.astype(o.dtype)
        if n_sums:
            @pl.when(pl.program_id(0) == 0)
            def _():
                for o in sums:
                    o[...] = jnp.zeros_like(o)

            for o, v in zip(sums, res[no:]):
                o[...] += v

    if out_t:
        tile = pl.BlockSpec((tko, tm), lambda i, ko: (ko, i))
        out_shape = (k, m)
    else:
        tile = pl.BlockSpec((tm, tko), lambda i, ko: (i, ko))
        out_shape = (m, k)
    vec = pl.BlockSpec((1, tko), lambda i, ko: (0, ko))
    a_spec = pl.BlockSpec((n, tm), lambda i, ko: (0, i)) if a_t else pl.BlockSpec((tm, n), lambda i, ko: (i, 0))
    return pl.pallas_call(
        body, name=name, grid=(m // tm, k // tko),
        in_specs=[a_spec, pl.BlockSpec((None, tko, n), lambda i, ko: (layer, ko, 0))]
        + [tile if e.shape == out_shape else vec for e in extras],
        out_specs=[tile] * no + [vec] * n_sums,
        out_shape=[jax.ShapeDtypeStruct(out_shape, d) for d in out_dtypes] + [jax.ShapeDtypeStruct((1, k), F32)] * n_sums,
        compiler_params=_params(("arbitrary" if n_sums else "parallel", "parallel")),
    )(dy, w3, *extras)


def mm_tn(name, x, dy, gbuf, shape, layer, x_t=False, dy_t=False):
    k, s = x.shape if x_t else x.shape[::-1]
    _, kw, n = shape
    assert kw == k and dy.shape == ((n, s) if dy_t else (s, n))
    tkk = _tile(k, 512)
    tn = _tile(n, 1024)

    def body(x_ref, dy_ref, *rest):
        g_out = rest[-1]
        g_out[...] = lax.dot_general(x_ref[...], dy_ref[...], (((1 if x_t else 0,), (1 if dy_t else 0,)), ((), ())),
                                     preferred_element_type=F32).astype(g_out.dtype)

    prev = [] if gbuf is None else [gbuf]
    x_spec = pl.BlockSpec((tkk, s), lambda ki, j: (ki, 0)) if x_t else pl.BlockSpec((s, tkk), lambda ki, j: (0, ki))
    dy_spec = pl.BlockSpec((tn, s), lambda ki, j: (j, 0)) if dy_t else pl.BlockSpec((s, tn), lambda ki, j: (0, j))
    return pl.pallas_call(
        body, name=name, grid=(k // tkk, n // tn),
        in_specs=[x_spec, dy_spec] + [pl.BlockSpec(memory_space=pl.ANY)] * len(prev),
        out_specs=pl.BlockSpec((None, tkk, tn), lambda ki, j: (layer, ki, j)),
        out_shape=jax.ShapeDtypeStruct(shape, BF16),
        input_output_aliases={2: 0} if prev else {},
        compiler_params=_params(("parallel", "parallel")),
    )(x, dy, *prev)


def rms_fwd(name, h, g):
    s, d = h.shape
    tr = _pick(s, (ROW_TILE, 256, 128))

    def body(h_ref, g_ref, o_ref):
        x = h_ref[...]
        r = lax.rsqrt(jnp.mean(x * x, axis=-1, keepdims=True) + EPS)
        o_ref[...] = (x * r * g_ref[...]).astype(o_ref.dtype)

    return pl.pallas_call(
        body, name=name, grid=(s // tr,),
        in_specs=[pl.BlockSpec((tr, d), lambda i: (i, 0)), pl.BlockSpec((1, d), lambda i: (0, 0))],
        out_specs=pl.BlockSpec((tr, d), lambda i: (i, 0)),
        out_shape=jax.ShapeDtypeStruct((s, d), BF16),
        compiler_params=_params(("parallel",)),
    )(h, g.reshape(1, d))


def loss_head(h, g, target):
    s, d = h.shape
    tr = _pick(s, (ROW_TILE, 256, 128))

    def body(h_ref, g_ref, t_ref, dx_ref, dxb_ref, dg_ref, loss_ref):
        i = pl.program_id(0)
        x = h_ref[...]
        r = lax.rsqrt(jnp.mean(x * x, axis=-1, keepdims=True) + EPS)
        xh = x * r
        gw = g_ref[...]
        err = xh * gw - t_ref[...]
        dn_ = err * (1.0 / d)
        dyg = dn_ * gw
        dx = r * (dyg - xh * jnp.mean(dyg * xh, axis=-1, keepdims=True))
        dx_ref[...] = dx
        dxb_ref[...] = dx.astype(BF16)

        @pl.when(i == 0)
        def _():
            dg_ref[...] = jnp.zeros_like(dg_ref)
            loss_ref[...] = jnp.zeros_like(loss_ref)

        dg_ref[...] += jnp.sum(dn_ * xh, axis=0, keepdims=True)
        per_row = jnp.sum(err * err, axis=-1, keepdims=True) * (0.5 / d)
        loss_ref[...] += jnp.broadcast_to(jnp.sum(per_row, axis=0, keepdims=True), loss_ref.shape)

    row = pl.BlockSpec((tr, d), lambda i: (i, 0))
    vec = pl.BlockSpec((1, d), lambda i: (0, 0))
    return pl.pallas_call(
        body, name="loss_head", grid=(s // tr,),
        in_specs=[row, vec, row],
        out_specs=[row, row, vec, pl.BlockSpec((1, 128), lambda i: (0, 0))],
        out_shape=[jax.ShapeDtypeStruct((s, d), F32), jax.ShapeDtypeStruct((s, d), BF16),
                   jax.ShapeDtypeStruct((1, d), F32), jax.ShapeDtypeStruct((1, 128), F32)],
        compiler_params=_params(("arbitrary",)),
    )(h, g.reshape(1, d), target)


def colsum(name, x):
    s, n = x.shape
    tr = _pick(s, (ROW_TILE, 256, 128))

    def body(x_ref, o_ref):
        @pl.when(pl.program_id(0) == 0)
        def _():
            o_ref[...] = jnp.zeros_like(o_ref)

        o_ref[...] += jnp.sum(x_ref[...].astype(F32), axis=0, keepdims=True)

    return pl.pallas_call(
        body, name=name, grid=(s // tr,),
        in_specs=[pl.BlockSpec((tr, n), lambda i: (i, 0))],
        out_specs=pl.BlockSpec((1, n), lambda i: (0, 0)),
        out_shape=jax.ShapeDtypeStruct((1, n), F32),
        compiler_params=_params(("arbitrary",)),
    )(x)


def rowsum(name, x):
    n, s = x.shape
    ts = _pick(s, (1024, 512, 256, 128))

    def body(x_ref, o_ref):
        @pl.when(pl.program_id(0) == 0)
        def _():
            o_ref[...] = jnp.zeros_like(o_ref)

        o_ref[...] += jnp.sum(x_ref[...].astype(F32), axis=1, keepdims=True)

    return pl.pallas_call(
        body, name=name, grid=(s // ts,),
        in_specs=[pl.BlockSpec((n, ts), lambda i: (0, i))],
        out_specs=pl.BlockSpec((n, 1), lambda i: (0, 0)),
        out_shape=jax.ShapeDtypeStruct((n, 1), F32),
        compiler_params=_params(("arbitrary",)),
    )(x)[:, 0]


def _tri_rows(reverse):
    i = np.arange(SB_BK)
    tri = (i[None, :] >= i[:, None]) if reverse else (i[None, :] <= i[:, None])
    tri = np.concatenate([tri, tri], axis=1)
    return jnp.asarray(np.concatenate([tri, np.ones((8, 2 * SB_BK), bool)], axis=0), BF16)


def _hi_lo_rows(x):
    hi = x.astype(BF16)
    lo = (x - hi.astype(F32)).astype(BF16)
    return jnp.concatenate([hi, lo], axis=0)


def _softplus2(zs):
    neg_abs = lax.bitcast_convert_type(lax.bitcast_convert_type(zs, jnp.uint32) | jnp.uint32(0x80000000), F32)
    return jnp.maximum(zs, 0.0) + jnp.log2(1.0 + jnp.exp2(neg_abs))


def _pair_mask(first_rel_block, bq):
    key = lax.broadcasted_iota(jnp.int32, (2 * SB_BK, bq), 0) + first_rel_block * SB_BK
    qry = lax.broadcasted_iota(jnp.int32, (2 * SB_BK, bq), 1)
    return key < qry


def _row_of(table8, sub8, r):
    return jnp.sum(jnp.where(sub8 == r, table8, 0.0), axis=0, keepdims=True)


def _keys(j0):
    return pl.ds(pl.multiple_of(j0 * SB_BK, 2 * SB_BK), 2 * SB_BK)


class Carry:
    def __init__(self, build, ins, out_shapes, sems):
        self.build, self.ins, self.out_shapes, self.sems = build, list(ins), list(out_shapes), list(sems)


def _carried(carry, rest, n_out, n_scratch, first, last):
    n_ci = len(carry.ins) if carry else 0
    n_co = len(carry.out_shapes) if carry else 0
    cin, outs = rest[:n_ci], rest[n_ci:n_ci + n_out]
    cout = rest[n_ci + n_out:n_ci + n_out + n_co]
    scratch = rest[n_ci + n_out + n_co:n_ci + n_out + n_co + n_scratch]
    csems = rest[n_ci + n_out + n_co + n_scratch:]

    def start():
        if carry:
            @pl.when(first)
            def _():
                for cp in carry.build(cin, cout, *csems):
                    cp.start()

    def wait():
        if carry:
            @pl.when(last)
            def _():
                for cp in carry.build(cin, cout, *csems):
                    cp.wait()

    return outs, scratch, start, wait


def _contract0(a, b):
    return lax.dot_general(a, b, (((0,), (0,)), ((), ())), preferred_element_type=F32)


def _contract1(a, b):
    return lax.dot_general(a, b, (((1,), (1,)), ((), ())), preferred_element_type=F32)


def sb_fwd(name, qkvt, exchange=None):
    nh, dh, s = qkvt.shape[0] // 3, qkvt.shape[1], qkvt.shape[2]
    bq = SB_BQ
    per_q = bq // SB_BK
    nkb = s // SB_BK
    assert s % bq == 0 and per_q == 4 and nkb % 8 == 0

    def body(q_ref, k_ref, v_ref, a_ref, *rest):
        head = pl.program_id(0)
        (o_ref, rtab_ref), (acc, zbuf, wbuf), start_carried, wait_carried = _carried(
            exchange, rest, 2, 3, head == 0, head == nh - 1)
        start_carried()
        tri = a_ref[...]
        sub8 = lax.broadcasted_iota(jnp.int32, (8, bq), 0)
        rtab_ref[...] = jnp.full(rtab_ref.shape, SB_UNSEEN, F32)
        kf = k_ref[...].astype(F32)
        k_max2 = jnp.max(jnp.sum(kf * kf, axis=0, keepdims=True), axis=1, keepdims=True)

        def query_block(i, _):
            lanes = pl.ds(pl.multiple_of(i * bq, bq), bq)
            qb = q_ref[:, lanes] * Q_SCALE
            acc[...] = jnp.zeros_like(acc)
            qf = qb.astype(F32)
            bound = jnp.sqrt(jnp.sum(qf * qf, axis=0, keepdims=True) * k_max2) * (1.001 * LOG2E)

            def scores(j0):
                return _contract0(k_ref[:, _keys(j0)], qb) * LOG2E

            def pair(j0, slot, run, rt8, mask, has_prev):
                zs = zbuf[slot]
                zbuf[1 - slot] = scores(jnp.maximum(j0 - 2, 0))
                if has_prev:
                    acc[...] += jnp.dot(v_ref[:, _keys(j0 + 2)], wbuf[1 - slot], preferred_element_type=F32)
                p = _softplus2(zs)
                if mask is not None:
                    p = jnp.where(mask, p, 0.0)
                cr1 = jnp.dot(tri, _hi_lo_rows(p[SB_BK:]), preferred_element_type=F32)
                cr0 = jnp.dot(tri, _hi_lo_rows(p[:SB_BK]), preferred_element_type=F32)
                run1 = run + cr1[SB_BK:SB_BK + 1]
                w = jnp.exp2(jnp.concatenate([zs[:SB_BK] - cr0[:SB_BK] - run1, zs[SB_BK:] - cr1[:SB_BK] - run],
                                             axis=0))
                if mask is not None:
                    w = jnp.where(mask, w, 0.0)
                wbuf[slot] = w.astype(BF16)
                rt8 = jnp.where(j0 % 8 == 6, SB_UNSEEN, rt8)
                rt8 = jnp.where(sub8 == (j0 + 1) % 8, run, jnp.where(sub8 == j0 % 8, run1, rt8))
                rtab_ref[pl.ds(pl.multiple_of((j0 // 8) * 8, 8), 8), lanes] = rt8
                return run1 + cr0[SB_BK:SB_BK + 1], rt8

            def alive(run):
                return jnp.min(run - bound) < SB_DEAD

            top = i * per_q
            half = bq // 2

            def upper_diagonal_pair():
                j0 = top + 2
                zs = _contract0(k_ref[:, _keys(j0)], qb[:, half:]) * LOG2E
                zbuf[1] = scores(top)
                mask = _pair_mask(2, bq)[:, half:]
                p = jnp.where(mask, _softplus2(zs), 0.0)
                cr1 = jnp.dot(tri, _hi_lo_rows(p[SB_BK:]), preferred_element_type=F32)
                cr0 = jnp.dot(tri, _hi_lo_rows(p[:SB_BK]), preferred_element_type=F32)
                run1 = cr1[SB_BK:SB_BK + 1]
                w = jnp.exp2(jnp.concatenate([zs[:SB_BK] - cr0[:SB_BK] - run1, zs[SB_BK:] - cr1[:SB_BK]], axis=0))
                w = jnp.where(mask, w, 0.0).astype(BF16)
                acc[:, half:] += jnp.dot(v_ref[:, _keys(j0)], w, preferred_element_type=F32)
                unseen_half = jnp.zeros((1, half), F32)
                row0 = jnp.concatenate([unseen_half, run1], axis=1)
                rt8 = jnp.where(sub8 == j0 % 8, row0, jnp.where(sub8 == (j0 + 1) % 8, 0.0, SB_UNSEEN))
                rtab_ref[pl.ds(pl.multiple_of((j0 // 8) * 8, 8), 8), lanes] = rt8
                return jnp.concatenate([unseen_half, run1 + cr0[SB_BK:SB_BK + 1]], axis=1), rt8

            state = upper_diagonal_pair()
            state = pair(top, 1, *state, _pair_mask(0, bq), False)

            def step(c):
                it, pairs, _, run, rt8 = c
                j0 = top - 2 - 4 * it
                run, rt8 = pair(j0, 0, run, rt8, None, True)
                go = alive(run)
                run, rt8 = lax.cond(go, lambda r, t: pair(j0 - 2, 1, r, t, None, True), lambda r, t: (r, t), run, rt8)
                return it + 1, pairs + 1 + go.astype(jnp.int32), go & alive(run), run, rt8

            pairs = lax.while_loop(lambda c: (c[0] < i) & c[2], step, (0, 0, alive(state[0]), *state))[1]
            acc[...] += jnp.dot(v_ref[:, _keys(top - 2 * pairs)], wbuf[(pairs + 1) % 2], preferred_element_type=F32)
            o_ref[:, lanes] = acc[...].astype(o_ref.dtype)
            return 0

        lax.fori_loop(0, s // bq, query_block, 0)
        wait_carried()

    def head_spec(offset, rows):
        return pl.BlockSpec((None, rows, s), lambda h: (h + offset, 0, 0))

    hbm = pl.BlockSpec(memory_space=pl.ANY)
    c_ins, c_outs, c_sems = (exchange.ins, exchange.out_shapes, exchange.sems) if exchange else ([], [], [])
    outs = pl.pallas_call(
        body, name=name, grid=(nh,),
        in_specs=[head_spec(0, dh), head_spec(nh, dh), head_spec(2 * nh, dh),
                  pl.BlockSpec((SB_BK + 8, 2 * SB_BK), lambda h: (0, 0))] + [hbm] * len(c_ins),
        out_specs=[head_spec(0, dh), head_spec(0, nkb)] + [hbm] * len(c_outs),
        out_shape=[jax.ShapeDtypeStruct((nh, dh, s), BF16), jax.ShapeDtypeStruct((nh, nkb, s), F32)] + c_outs,
        scratch_shapes=[pltpu.VMEM((dh, bq), F32), pltpu.VMEM((2, 2 * SB_BK, bq), F32),
                        pltpu.VMEM((2, 2 * SB_BK, bq), BF16)] + c_sems,
        compiler_params=_params(("arbitrary",)),
    )(qkvt, qkvt, qkvt, _tri_rows(True), *c_ins)
    return outs[0], outs[1], outs[2:]


def sb_bwd(name, qkvt, dot_, rtab, exchange=None):
    nh, dh, s = qkvt.shape[0] // 3, qkvt.shape[1], qkvt.shape[2]
    bq = SB_BQ
    per_q = bq // SB_BK
    nkb = s // SB_BK

    def body(qt_ref, kt_ref, vt_ref, dot_ref, rtab_ref, ar_ref, af_ref, *rest):
        head = pl.program_id(0)
        (dq_ref, dk_ref, dv_ref), (dq_acc, dk_acc, dv_acc, zbuf, dwbuf, dzbuf, wbuf), start_carried, wait_carried = \
            _carried(exchange, rest, 3, 7, head == 0, head == nh - 1)
        start_carried()
        dk_acc[...] = jnp.zeros_like(dk_acc)
        dv_acc[...] = jnp.zeros_like(dv_acc)
        tri_rev = ar_ref[...][:SB_BK]
        tri_fwd = af_ref[...]
        sub8 = lax.broadcasted_iota(jnp.int32, (8, bq), 0)

        def query_block(i, _):
            lanes = pl.ds(pl.multiple_of(i * bq, bq), bq)
            qtb = qt_ref[:, lanes] * Q_SCALE
            dotb = dot_ref[:, lanes]
            dq_acc[...] = jnp.zeros_like(dq_acc)
            last_j = i * per_q + 2
            seen = jnp.max(jnp.where(rtab_ref[:, lanes] < 0.1 * SB_UNSEEN, 1.0, 0.0), axis=1, keepdims=True)
            pairs = jnp.clip((jnp.sum(seen).astype(jnp.int32) - per_q) // 2, 0, 2 * i)
            odd = pairs % 2
            first_j = i * per_q - 2 * pairs

            def issue(j0, slot):
                zbuf[slot] = _contract0(kt_ref[:, _keys(j0)], qtb) * LOG2E
                dwbuf[slot] = _contract0(vt_ref[:, _keys(j0)], dotb)

            def retire(j0, slot):
                keys = _keys(j0)
                dq_acc[...] += jnp.dot(kt_ref[:, keys], dzbuf[slot], preferred_element_type=F32)
                dk_acc[:, keys] += _contract1(qtb, dzbuf[slot])
                dv_acc[:, keys] += _contract1(dotb, wbuf[slot])

            def pair(j0, slot, g_run, mask):
                zs = zbuf[slot]
                dw = dwbuf[slot]
                issue(jnp.minimum(j0 + 2, last_j), 1 - slot)
                retire(jnp.maximum(j0 - 2, first_j), 1 - slot)
                p_raw = _softplus2(zs)
                p = p_raw if mask is None else jnp.where(mask, p_raw, 0.0)
                c0 = jnp.dot(tri_rev, _hi_lo_rows(p[:SB_BK]), preferred_element_type=F32)
                c1 = jnp.dot(tri_rev, _hi_lo_rows(p[SB_BK:]), preferred_element_type=F32)
                rt8 = rtab_ref[pl.ds(pl.multiple_of((j0 // 8) * 8, 8), 8), lanes]
                r0 = _row_of(rt8, sub8, j0 % 8)
                r1 = _row_of(rt8, sub8, (j0 + 1) % 8)
                w = jnp.exp2(jnp.concatenate([zs[:SB_BK] - c0 - r0, zs[SB_BK:] - c1 - r1], axis=0))
                if mask is not None:
                    w = jnp.where(mask, w, 0.0)
                g = w * dw
                gg0 = jnp.dot(tri_fwd, _hi_lo_rows(g[:SB_BK]), preferred_element_type=F32)
                gg1 = jnp.dot(tri_fwd, _hi_lo_rows(g[SB_BK:]), preferred_element_type=F32)
                g_run1 = g_run + gg0[SB_BK:SB_BK + 1]
                g_pre = jnp.concatenate([gg0[:SB_BK] + g_run, gg1[:SB_BK] + g_run1], axis=0)
                dz = g - jnp.exp2(zs - p_raw) * g_pre
                if mask is not None:
                    dz = jnp.where(mask, dz, 0.0)
                dzbuf[slot] = dz.astype(BF16)
                wbuf[slot] = w.astype(BF16)
                return g_run1 + gg1[SB_BK:SB_BK + 1]

            issue(first_j, odd)
            dzbuf[...] = jnp.zeros(dzbuf.shape, BF16)
            wbuf[...] = jnp.zeros(wbuf.shape, BF16)

            def step(it, g_run):
                g_run = pair(4 * it, 0, g_run, None)
                return pair(4 * it + 2, 1, g_run, None)

            g_run = lax.cond(odd == 1, lambda g: pair(first_j, 1, g, None), lambda g: g, jnp.zeros((1, bq), F32))
            g_run = lax.fori_loop(i - pairs // 2, i, step, g_run)
            g_run = pair(last_j - 2, 0, g_run, _pair_mask(0, bq))
            retire(last_j - 2, 0)
            half = bq // 2
            keys = _keys(last_j)
            q_h, do_h, mask = qtb[:, half:], dotb[:, half:], _pair_mask(2, bq)[:, half:]
            zs = _contract0(kt_ref[:, keys], q_h) * LOG2E
            dw = _contract0(vt_ref[:, keys], do_h)
            p_raw = _softplus2(zs)
            p = jnp.where(mask, p_raw, 0.0)
            c0 = jnp.dot(tri_rev, _hi_lo_rows(p[:SB_BK]), preferred_element_type=F32)
            c1 = jnp.dot(tri_rev, _hi_lo_rows(p[SB_BK:]), preferred_element_type=F32)
            rt8 = rtab_ref[pl.ds(pl.multiple_of((last_j // 8) * 8, 8), 8), lanes][:, half:]
            sub8_h = sub8[:, half:]
            r0 = _row_of(rt8, sub8_h, last_j % 8)
            r1 = _row_of(rt8, sub8_h, (last_j + 1) % 8)
            w = jnp.where(mask, jnp.exp2(jnp.concatenate([zs[:SB_BK] - c0 - r0, zs[SB_BK:] - c1 - r1], axis=0)), 0.0)
            g = w * dw
            gg0 = jnp.dot(tri_fwd, _hi_lo_rows(g[:SB_BK]), preferred_element_type=F32)
            gg1 = jnp.dot(tri_fwd, _hi_lo_rows(g[SB_BK:]), preferred_element_type=F32)
            g_h = g_run[:, half:]
            g_pre = jnp.concatenate([gg0[:SB_BK] + g_h, gg1[:SB_BK] + (g_h + gg0[SB_BK:SB_BK + 1])], axis=0)
            dz = jnp.where(mask, g - jnp.exp2(zs - p_raw) * g_pre, 0.0).astype(BF16)
            dq_acc[:, half:] += jnp.dot(kt_ref[:, keys], dz, preferred_element_type=F32)
            dk_acc[:, keys] += _contract1(q_h, dz)
            dv_acc[:, keys] += _contract1(do_h, w.astype(BF16))
            dq_ref[:, lanes] = (dq_acc[...] * Q_SCALE).astype(dq_ref.dtype)
            return 0

        lax.fori_loop(0, s // bq, query_block, 0)
        dk_ref[...] = dk_acc[...].astype(dk_ref.dtype)
        dv_ref[...] = dv_acc[...].astype(dv_ref.dtype)
        wait_carried()

    def head_spec(offset, rows):
        return pl.BlockSpec((None, rows, s), lambda h: (h + offset, 0, 0))

    aspec = pl.BlockSpec((SB_BK + 8, 2 * SB_BK), lambda h: (0, 0))
    pair_f32 = pltpu.VMEM((2, 2 * SB_BK, bq), F32)
    pair_bf16 = pltpu.VMEM((2, 2 * SB_BK, bq), BF16)
    hbm = pl.BlockSpec(memory_space=pl.ANY)
    c_ins, c_outs, c_sems = (exchange.ins, exchange.out_shapes, exchange.sems) if exchange else ([], [], [])
    outs = pl.pallas_call(
        body, name=name, grid=(nh,),
        in_specs=[head_spec(0, dh), head_spec(nh, dh), head_spec(2 * nh, dh), head_spec(0, dh), head_spec(0, nkb),
                  aspec, aspec] + [hbm] * len(c_ins),
        out_specs=[head_spec(0, dh)] * 3 + [hbm] * len(c_outs),
        out_shape=[jax.ShapeDtypeStruct((nh, dh, s), BF16)] * 3 + c_outs,
        scratch_shapes=[pltpu.VMEM((dh, bq), F32), pltpu.VMEM((dh, s), F32), pltpu.VMEM((dh, s), F32),
                        pair_f32, pair_f32, pair_bf16, pair_bf16] + c_sems,
        compiler_params=_params(("arbitrary",)),
    )(qkvt, qkvt, qkvt, dot_, rtab, _tri_rows(True), _tri_rows(False), *c_ins)
    return outs[0], outs[1], outs[2], outs[3:]


SWA_QB = 2


def _swa_probs(qt, kt, bias_t, sink, i):
    cols = qt.shape[1]
    sc = _contract0(kt, qt) + bias_t
    kj = lax.broadcasted_iota(jnp.int32, (2 * WINDOW, cols), 0)
    qi = lax.broadcasted_iota(jnp.int32, (2 * WINDOW, cols), 1) & (WINDOW - 1)
    dist = qi + WINDOW - kj
    valid = (dist >= 0) & (dist < WINDOW) & ((kj >= WINDOW) | (i > 0))
    sc = jnp.where(valid, sc, NEG_INF)
    mx = jnp.maximum(jnp.max(sc, axis=0, keepdims=True), sink)
    p = jnp.exp(sc - mx)
    p_sink = jnp.exp(sink - mx)
    inv = 1.0 / (jnp.sum(p, axis=0, keepdims=True) + p_sink)
    return p, p_sink, inv


def _band(i):
    return pl.ds(pl.multiple_of(i * WINDOW, WINDOW), 2 * WINDOW)


def _heads_to_lanes(blk):
    return jnp.concatenate([blk[r * HEAD_DIM:(r + 1) * HEAD_DIM] for r in range(8)], axis=1)


def _lanes_to_heads(t):
    return jnp.concatenate([t[:, r * WINDOW:(r + 1) * WINDOW] for r in range(8)], axis=0)


def swa_fwd(name, qt, kpt, vpt, bias_t, sink_row):
    d, s = qt.shape
    ng, dh, sp = kpt.shape
    rows, cols = d // ng, SWA_QB * WINDOW
    assert (s // WINDOW) % SWA_QB == 0

    def body(q_ref, k_ref, v_ref, bias_ref, sink_ref, o_ref):
        for u in range(SWA_QB):
            i = pl.program_id(1) * SWA_QB + u
            lanes = slice(u * WINDOW, (u + 1) * WINDOW)
            qb = _heads_to_lanes(q_ref[:, lanes]) * Q_SCALE
            p, _, inv = _swa_probs(qb, k_ref[:, _band(i)], bias_ref[...], sink_ref[...], i)
            o_t = jnp.dot(v_ref[:, _band(i)], p.astype(BF16), preferred_element_type=F32) * inv
            o_ref[:, lanes] = _lanes_to_heads(o_t).astype(o_ref.dtype)

    qspec = pl.BlockSpec((rows, cols), lambda g, i: (g, i))
    kspec = pl.BlockSpec((None, dh, sp), lambda g, i: (g, 0, 0))
    return pl.pallas_call(
        body, name=name, grid=(ng, s // cols),
        in_specs=[qspec, kspec, kspec, pl.BlockSpec((None, 2 * WINDOW, 8 * WINDOW), lambda g, i: (g, 0, 0)),
                  pl.BlockSpec((None, 1, 8 * WINDOW), lambda g, i: (g, 0, 0))],
        out_specs=qspec,
        out_shape=jax.ShapeDtypeStruct(qt.shape, BF16),
        compiler_params=_params(("parallel", "arbitrary")),
    )(qt, kpt, vpt, bias_t, sink_row)


def swa_bwd(name, qt, kpt, vpt, bias_t, sink_row, dot_, dk_in, dv_in):
    d, s = qt.shape
    ng, dh, sp = kpt.shape
    rows, cols = d // ng, SWA_QB * WINDOW

    def body(q_ref, k_ref, v_ref, bias_ref, sink_ref, do_ref, dki_ref, dvi_ref, dq_ref, dk_ref, dv_ref, db_ref, ds_ref):
        @pl.when(pl.program_id(1) == 0)
        def _():
            dk_ref[...] = dki_ref[...]
            dv_ref[...] = dvi_ref[...]
            db_ref[...] = jnp.zeros_like(db_ref)
            ds_ref[...] = jnp.zeros_like(ds_ref)

        for u in range(SWA_QB):
            i = pl.program_id(1) * SWA_QB + u
            band = _band(i)
            lanes = slice(u * WINDOW, (u + 1) * WINDOW)
            qb = _heads_to_lanes(q_ref[:, lanes]) * Q_SCALE
            dob = _heads_to_lanes(do_ref[:, lanes])
            kt = k_ref[:, band]
            p, p_sink, inv = _swa_probs(qb, kt, bias_ref[...], sink_ref[...], i)
            p = p * inv
            dp = _contract0(v_ref[:, band], dob)
            delta = jnp.sum(p * dp, axis=0, keepdims=True)
            dsc = p * (dp - delta)
            ds_ref[...] -= p_sink * inv * delta
            db_ref[...] += dsc
            dscb = dsc.astype(BF16)
            dq_t = jnp.dot(kt, dscb, preferred_element_type=F32) * Q_SCALE
            dq_ref[:, lanes] = _lanes_to_heads(dq_t).astype(dq_ref.dtype)
            dk_ref[:, band] += _contract1(qb, dscb)
            dv_ref[:, band] += _contract1(dob, p.astype(BF16))

    qspec = pl.BlockSpec((rows, cols), lambda g, i: (g, i))
    kspec = pl.BlockSpec((None, dh, sp), lambda g, i: (g, 0, 0))
    bspec = pl.BlockSpec((None, 2 * WINDOW, 8 * WINDOW), lambda g, i: (g, 0, 0))
    sspec = pl.BlockSpec((None, 1, 8 * WINDOW), lambda g, i: (g, 0, 0))
    return pl.pallas_call(
        body, name=name, grid=(ng, s // cols),
        in_specs=[qspec, kspec, kspec, bspec, sspec, qspec, kspec, kspec],
        out_specs=[qspec, kspec, kspec, bspec, sspec],
        out_shape=[jax.ShapeDtypeStruct(qt.shape, BF16), jax.ShapeDtypeStruct(kpt.shape, F32),
                   jax.ShapeDtypeStruct(kpt.shape, F32), jax.ShapeDtypeStruct(bias_t.shape, F32),
                   jax.ShapeDtypeStruct(sink_row.shape, F32)],
        compiler_params=_params(("parallel", "arbitrary")),
    )(qt, kpt, vpt, bias_t, sink_row, dot_, dk_in, dv_in)


def _bucket_onehot():
    qi = np.arange(WINDOW)[:, None]
    kj = np.arange(2 * WINDOW)[None, :]
    n = np.maximum(qi + WINDOW - kj, 0)
    max_exact = N_BUCKETS // 2
    nf = np.maximum(n, 1).astype(np.float64)
    val = np.log(nf / max_exact) / math.log(WINDOW / max_exact) * (N_BUCKETS - max_exact)
    assert np.all(np.abs(val - np.round(val))[(n > max_exact) & (n < WINDOW)] > 1e-3)
    large = np.minimum(max_exact + val.astype(np.int64), N_BUCKETS - 1)
    bucket = np.where(n < max_exact, n, large).reshape(-1)
    onehot = np.zeros((128, bucket.size), np.float32)
    onehot[bucket, np.arange(bucket.size)] = 1.0
    return onehot


def _split3(x):
    a = x.astype(BF16)
    r = x - a.astype(F32)
    b = r.astype(BF16)
    c = (r - b.astype(F32)).astype(BF16)
    return a, b, c


def bias_table(rel_bias):
    nh = rel_bias.shape[1]
    oh = jnp.asarray(_bucket_onehot(), BF16)
    n = oh.shape[1]
    tn = 4096
    rb = jnp.zeros((nh, 128), F32).at[:, :N_BUCKETS].set(rel_bias.T)

    def body(rb_ref, oh_ref, o_ref):
        o_ref[...] = sum(jnp.dot(t, oh_ref[...], preferred_element_type=F32) for t in _split3(rb_ref[...]))

    return pl.pallas_call(
        body, name="bias_table", grid=(n // tn,),
        in_specs=[pl.BlockSpec((nh, 128), lambda i: (0, 0)), pl.BlockSpec((128, tn), lambda i: (0, i))],
        out_specs=pl.BlockSpec((nh, tn), lambda i: (0, i)),
        out_shape=jax.ShapeDtypeStruct((nh, n), F32),
        compiler_params=_params(("parallel",)),
    )(rb, oh)


def bias_table_grad(db0, db1):
    nh, n = db0.shape
    oh = jnp.asarray(_bucket_onehot(), BF16)
    tn = 4096

    def body(a_ref, b_ref, oh_ref, o_ref):
        @pl.when(pl.program_id(0) == 0)
        def _():
            o_ref[...] = jnp.zeros_like(o_ref)

        o_ref[...] += sum(lax.dot_general(t, oh_ref[...], (((1,), (1,)), ((), ())), preferred_element_type=F32)
                          for t in _split3(a_ref[...] + b_ref[...]))

    blk = pl.BlockSpec((nh, tn), lambda i: (0, i))
    return pl.pallas_call(
        body, name="bias_table_grad", grid=(n // tn,),
        in_specs=[blk, blk, pl.BlockSpec((128, tn), lambda i: (0, i))],
        out_specs=pl.BlockSpec((nh, 128), lambda i: (0, 0)),
        out_shape=jax.ShapeDtypeStruct((nh, 128), F32),
        compiler_params=_params(("arbitrary",)),
    )(db0, db1, oh)


def _owner_view(ref, name, d):
    if name == 'a_norm':
        return ref.at[d]
    if name in COL_SHARDED:
        n = ref.shape[2] // N_DEV
        return ref.at[:, :, pl.ds(pl.multiple_of(d * n, 128), n)]
    return ref.at[:, d]


def _place():
    return lax.axis_index("x"), lax.axis_index("y"), lax.axis_index("c")


def _dev(p):
    return 4 * p[0] + 2 * p[1] + p[2]


def _remote(src, dst, send_sem, recv_sem, to):
    return pltpu.make_async_remote_copy(src_ref=src, dst_ref=dst, send_sem=send_sem, recv_sem=recv_sem,
                                        device_id=to, device_id_type=MESH)


def _dma_sems(*shapes):
    return [pltpu.SemaphoreType.DMA(sh) for sh in shapes]


def comm_call(name, build, ins, out_shapes, sems, aliases=None):
    n_in, n_out = len(ins), len(out_shapes)

    def body(*refs):
        copies = build(refs[:n_in], refs[n_in:n_in + n_out], *refs[n_in + n_out:])
        for cp in copies:
            cp.start()
        for cp in copies:
            cp.wait()

    hbm = pl.BlockSpec(memory_space=pl.ANY)
    return pl.pallas_call(
        body, name=name, in_specs=[hbm] * n_in, out_specs=[hbm] * n_out, out_shape=list(out_shapes),
        scratch_shapes=sems, input_output_aliases=aliases or {},
    )(*ins)


def all_gather_weights(names, shards, full_shapes):
    n = len(names)

    def body(*refs):
        ins, outs = refs[:n], refs[n:2 * n]
        send_sems, recv_sems, local_sems = refs[2 * n:]
        x, y, c = _place()
        me, sibling = (x, y, c), (x, y, 1 - c)
        chips = [(1 - x, y), (x, 1 - y), (1 - x, 1 - y)]

        def copy(t, k, block, to, src=None):
            dst = _owner_view(outs[t], names[t], _dev(block))
            return _remote(dst if src is None else src, dst, send_sems.at[t, k], recv_sems.at[t, k], to)

        mine = [pltpu.make_async_copy(ins[t], _owner_view(outs[t], names[t], _dev(me)), local_sems.at[t])
                for t in range(n)]
        for cp in mine:
            cp.start()
        first = []
        for t in range(n):
            first.append(copy(t, 0, me, sibling, src=ins[t]))
            first += [copy(t, 1 + j, me, (*chip, c), src=ins[t]) for j, chip in enumerate(chips)]
        for cp in first:
            cp.start()
        passed = []
        for j, chip in enumerate(chips):
            for t in range(n):
                copy(t, 1 + j, (*chip, c), me).wait_recv()
                fwd = copy(t, 4 + j, (*chip, c), sibling)
                fwd.start()
                passed.append(fwd)
        for t in range(n):
            copy(t, 0, sibling, me).wait_recv()
            for j, chip in enumerate(chips):
                copy(t, 4 + j, (*chip, 1 - c), me).wait_recv()
        for cp in first + passed:
            cp.wait_send()
        for cp in mine:
            cp.wait()

    hbm = pl.BlockSpec(memory_space=pl.ANY)
    return pl.pallas_call(
        body, name="all_gather_layer0",
        in_specs=[hbm] * n, out_specs=[hbm] * n,
        out_shape=[jax.ShapeDtypeStruct(full_shapes[t], shards[t].dtype) for t in range(n)],
        scratch_shapes=_dma_sems((n, 7), (n, 7), (n,)),
    )(*shards)


def ag_direct(names, shards, full_shapes):
    n = len(names)

    def build(ins, outs, send_sems, recv_sems, local_sems):
        x, y, c = _place()
        peers = [(x, y, 1 - c), (1 - x, y, c), (x, 1 - y, c), (1 - x, 1 - y, c)]
        copies = []
        for t in range(n):
            dst = _owner_view(outs[t], names[t], _dev((x, y, c)))
            copies.append(pltpu.make_async_copy(ins[t], dst, local_sems.at[t]))
            copies += [_remote(ins[t], dst, send_sems.at[t, k], recv_sems.at[t, k], to) for k, to in enumerate(peers)]
        return copies

    return Carry(build, shards, [jax.ShapeDtypeStruct(full_shapes[t], shards[t].dtype) for t in range(n)],
                 _dma_sems((n, 4), (n, 4), (n,)))


def ag_forward(tag, names, partial):
    n = len(names)

    def build(ins, outs, send_sems, recv_sems):
        del ins
        x, y, c = _place()
        copies = []
        for t in range(n):
            for k, chip in enumerate([(1 - x, y), (x, 1 - y), (1 - x, 1 - y)]):
                view = _owner_view(outs[t], names[t], _dev((*chip, c)))
                copies.append(_remote(view, view, send_sems.at[t, k], recv_sems.at[t, k], (x, y, 1 - c)))
        return copies

    return comm_call(f"all_gather_forward_{tag}", build, partial, [jax.ShapeDtypeStruct(p.shape, p.dtype) for p in partial],
                     _dma_sems((n, 3), (n, 3)), aliases={t: t for t in range(n)})


def sibling_exchange(tag, names, grads, part_shapes):
    n = len(names)

    def build(ins, outs, send_sems, recv_sems):
        x, y, c = _place()
        return [_remote(_owner_view(ins[t], names[t], 2 * q + 1 - c), outs[t].at[q], send_sems.at[t, q],
                        recv_sems.at[t, q], (x, y, 1 - c)) for t in range(n) for q in range(4)]

    return comm_call(f"rs_sibling_exchange_{tag}", build, grads,
                     [jax.ShapeDtypeStruct((4,) + part_shapes[t], BF16) for t in range(n)], _dma_sems((n, 4), (n, 4)))


def chip_exchange(names, parts, part_shapes):
    n = len(names)

    def build(ins, outs, send_sems, recv_sems):
        x, y, c = _place()
        chips = [(1 - x, y), (x, 1 - y), (1 - x, 1 - y)]
        return [_remote(ins[t].at[2 * chip[0] + chip[1]], outs[t].at[k], send_sems.at[t, k], recv_sems.at[t, k],
                        (*chip, c)) for t in range(n) for k, chip in enumerate(chips)]

    return Carry(build, parts, [jax.ShapeDtypeStruct((3,) + part_shapes[t], BF16) for t in range(n)],
                 _dma_sems((n, 3), (n, 3)))


def all_gather_rows(x):
    r, w = x.shape

    def body(x_ref, out_ref, send_sems, recv_sems, local_sem):
        px, py, pc = _place()
        me = 4 * px + 2 * py + pc
        mine = pltpu.make_async_copy(x_ref, out_ref.at[me], local_sem)
        mine.start()
        copies = []
        for k in range(1, N_DEV):
            peer = (px ^ (k >> 2), py ^ ((k >> 1) & 1), pc ^ (k & 1))
            copies.append(pltpu.make_async_remote_copy(
                src_ref=x_ref, dst_ref=out_ref.at[me], send_sem=send_sems.at[k - 1], recv_sem=recv_sems.at[k - 1],
                device_id=peer, device_id_type=MESH))
        for cp in copies:
            cp.start()
        for k in range(1, N_DEV):
            peer_idx = me ^ k
            pltpu.make_async_remote_copy(
                src_ref=x_ref, dst_ref=out_ref.at[peer_idx], send_sem=send_sems.at[k - 1],
                recv_sem=recv_sems.at[k - 1], device_id=(px, py, pc), device_id_type=MESH).wait_recv()
        for cp in copies:
            cp.wait_send()
        mine.wait()

    vmem = pl.BlockSpec(memory_space=pltpu.VMEM)
    return pl.pallas_call(
        body, name="all_gather_small_grads",
        in_specs=[vmem], out_specs=vmem,
        out_shape=jax.ShapeDtypeStruct((N_DEV, r, w), x.dtype),
        scratch_shapes=[pltpu.SemaphoreType.DMA((N_DEV - 1,)), pltpu.SemaphoreType.DMA((N_DEV - 1,)),
                        pltpu.SemaphoreType.DMA],
    )(x)


def _adamw(w, g, m, v):
    m = ADAM_B1 * m + (1.0 - ADAM_B1) * g
    v = ADAM_B2 * v + (1.0 - ADAM_B2) * (g * g)
    m_hat = m / (1.0 - ADAM_B1 ** ADAM_STEP)
    v_hat = v / (1.0 - ADAM_B2 ** ADAM_STEP)
    return -ADAM_LR * (m_hat / (jnp.sqrt(v_hat) + ADAM_EPS) + ADAM_WD * w), m, v


def sibling_sum(name, col, grads, recv, core):
    _, nl, rows, cols = recv.shape
    tr = _tile(rows, 512)
    rspec = pl.BlockSpec((None, None, tr, cols), lambda q, l, i, c_ref: (q, l, i, 0))
    if col:
        gspec = pl.BlockSpec((None, tr, cols), lambda q, l, i, c_ref: (l, i, 2 * q + c_ref[0]))
    else:
        gspec = pl.BlockSpec((None, None, tr, cols), lambda q, l, i, c_ref: (l, 2 * q + c_ref[0], i, 0))

    def body(c_ref, g_ref, r_ref, o_ref):
        del c_ref
        o_ref[...] = (g_ref[...].astype(F32) + r_ref[...].astype(F32)).astype(BF16)

    return pl.pallas_call(
        body, name=name,
        grid_spec=pltpu.PrefetchScalarGridSpec(num_scalar_prefetch=1, grid=(4, nl, rows // tr),
                                               in_specs=[gspec, rspec], out_specs=rspec),
        out_shape=jax.ShapeDtypeStruct(recv.shape, BF16),
        compiler_params=_params(("parallel", "parallel", "parallel")),
    )(core.reshape(1), grads, recv)


def reduce_adamw(name, parts, recv, chip, w, m, v, l0, prev):
    _, nl, rows, cols = parts.shape
    tr = _tile(rows, 256)

    def body(q_ref, p_ref, r_ref, w_ref, m_ref, v_ref, *rest):
        del q_ref
        g_out, d_out, m_out, v_out = rest[-4:]
        g = ((p_ref[...].astype(F32) + r_ref[0].astype(F32)) + r_ref[1].astype(F32)) + r_ref[2].astype(F32)
        d, mn, vn = _adamw(w_ref[...], g, m_ref[...], v_ref[...])
        g_out[...] = g
        d_out[...] = d
        m_out[...] = mn
        v_out[...] = vn

    blk = pl.BlockSpec((None, tr, cols), lambda l, i, q_ref: (l0 + l, i, 0))
    prev = list(prev) if prev else []
    return pl.pallas_call(
        body, name=name,
        grid_spec=pltpu.PrefetchScalarGridSpec(
            num_scalar_prefetch=1, grid=(nl, rows // tr),
            in_specs=[pl.BlockSpec((None, None, tr, cols), lambda l, i, q_ref: (q_ref[0], l, i, 0)),
                      pl.BlockSpec((3, None, tr, cols), lambda l, i, q_ref: (0, l, i, 0)), blk, blk, blk]
            + [pl.BlockSpec(memory_space=pl.ANY)] * len(prev),
            out_specs=[blk] * 4),
        out_shape=[jax.ShapeDtypeStruct(w.shape, F32)] * 4,
        input_output_aliases={6 + i: i for i in range(len(prev))},
        compiler_params=_params(("parallel", "parallel")),
    )(chip.reshape(1), parts, recv, w, m, v, *prev)


def small_adamw(name, gathered, w, m, v):
    _, r, c = gathered.shape

    def body(ga_ref, w_ref, m_ref, v_ref, g_out, d_out, m_out, v_out):
        g = ga_ref[0]
        for d in range(1, N_DEV):
            g = g + ga_ref[d]
        dl, mn, vn = _adamw(w_ref[...], g, m_ref[...], v_ref[...])
        g_out[...] = g
        d_out[...] = dl
        m_out[...] = mn
        v_out[...] = vn

    return pl.pallas_call(
        body, name=name,
        out_shape=[jax.ShapeDtypeStruct((r, c), F32)] * 4,
        compiler_params=_params(),
    )(gathered, w, m, v)


def _rms(x, g):
    return x * lax.rsqrt(jnp.mean(x * x, axis=-1, keepdims=True) + EPS) * g


def _rms_bwd_epilogue(dn, x, dres, g):
    r = lax.rsqrt(jnp.mean(x * x, axis=-1, keepdims=True) + EPS)
    xh = x * r
    dyg = dn * g
    dx = dres + r * (dyg - xh * jnp.mean(dyg * xh, axis=-1, keepdims=True))
    return dx, dx, jnp.sum(dn * xh, axis=0, keepdims=True), jnp.sum(dx, axis=0, keepdims=True)


def _residual_then_norms(n_terms):
    def epilogue(acc, *ex):
        h = acc
        for t in ex[:n_terms]:
            h = h + t
        return (h,) + tuple(_rms(h, g) for g in ex[n_terms:])
    return epilogue


def local_step(x, target, small, ex):
    s, d = x.shape
    n_a, n_b = small['a_norm'].shape[0], small['b_norm'].shape[0]
    sg = {}
    gb = {}

    def fwd_mm(name, a, wname, layer, epilogue, extras, out_dtypes, **kw):
        return mm_nn(name, a, *ex.weight(wname, layer), epilogue, extras, out_dtypes, **kw)

    def dx_mm(name, dy, wname, layer, epilogue, extras, out_dtypes, **kw):
        return mm_nt(name, dy, *ex.weight(wname, layer), epilogue, extras, out_dtypes, **kw)

    def dw_mm(name, a, dy, wname, layer, **kw):
        key, slab, shape = ex.grad(wname, layer)
        gb[key] = mm_tn(name, a, dy, gb.get(key), shape, slab, **kw)

    plain = lambda acc: (acc,)
    plus_col = lambda acc, b: (acc + b,)

    bias_flat = bias_table(small['rel_bias'])
    bias_t = bias_flat.reshape(2, 8, WINDOW, 2 * WINDOW).transpose(0, 3, 1, 2).reshape(2, 2 * WINDOW, 8 * WINDOW)
    sink_rows = [jnp.repeat(small['b_sinks'][j], WINDOW).reshape(2, 1, 8 * WINDOW) for j in range(n_b)]

    gain = lambda g: g.reshape(1, -1)

    def mlp_fwd(h, n2, layer, next_gains):
        u, a = fwd_mm(f"mlp_up_fwd{layer}", n2, 'mlp_up', layer,
                      lambda acc: (acc, jnp.square(jnp.maximum(acc, 0.0))), (), (BF16, BF16))
        h2, *nexts = fwd_mm(f"mlp_down_fwd{layer}", a, 'mlp_down', layer, _residual_then_norms(1),
                            (h, *[gain(g) for g in next_gains]), (F32,) + (BF16,) * len(next_gains))
        return h2, nexts, (n2, u, a)

    h = x
    saved = []
    n1 = rms_fwd("a_norm_fwd0", h, small['a_norm'][0])
    for l in range(n_a):
        (qkvt,) = fwd_mm(f"a_qkv_fwd{l}", n1, 'a_wqkv', l, plain, (), (BF16,), out_t=True)
        qkvt = qkvt.reshape(3 * d // HEAD_DIM, HEAD_DIM, s)
        o_t, rtab, carried = sb_fwd(f"sb_fwd{l}", qkvt, ex.fwd_carry(l))
        ex.fwd_done(l, carried)
        o_t = o_t.reshape(d, s)
        h_mid, n2 = fwd_mm(f"a_wo_fwd{l}", o_t, 'a_wo', l, _residual_then_norms(1),
                           (h, gain(small['mlp_norm'][l])), (F32, BF16), a_t=True)
        next_gains = [small['a_norm'][l + 1]] if l + 1 < n_a else [small['b_norm'][0], small['kv_norm']]
        h_out, nexts, mlp_saved = mlp_fwd(h_mid, n2, l, next_gains)
        saved.append((h, n1, qkvt, o_t, rtab, h_mid, mlp_saved))
        h, n1 = h_out, nexts[0]
    h_kv, nkv = h, nexts[1]
    (kvt,) = fwd_mm("kv_fwd", nkv, 'w_kv', 0, plus_col, (small['b_kv'].reshape(-1, 1),), (BF16,), out_t=True)
    kvt = kvt.reshape(2, 2, HEAD_DIM, s)
    kpt, vpt = (jnp.pad(t, ((0, 0), (0, 0), (WINDOW, 0))) for t in (kvt[0], kvt[1]))
    for j in range(n_b):
        layer = n_a + j
        (qbt,) = fwd_mm(f"b_q_fwd{j}", n1, 'b_wq', j, plus_col, (small['b_bq'][j].reshape(-1, 1),), (BF16,),
                        out_t=True)
        o_t = swa_fwd(f"swa_fwd{j}", qbt, kpt, vpt, bias_t, sink_rows[j])
        h_mid, n2 = fwd_mm(f"b_wo_fwd{j}", o_t, 'b_wo', j, _residual_then_norms(2),
                           (h, gain(small['b_bo'][j]), gain(small['mlp_norm'][layer])), (F32, BF16), a_t=True)
        h_out, nexts, mlp_saved = mlp_fwd(h_mid, n2, layer, [small['b_norm'][j + 1]] if j + 1 < n_b else [])
        saved.append((h, n1, qbt, o_t, h_mid, mlp_saved))
        h, n1 = h_out, (nexts[0] if nexts else None)

    dh, dhb, dg_final, loss_b = loss_head(h, small['final_norm'], target)
    sg['final_norm'] = dg_final[0]
    sg['mlp_norm'] = [None] * (n_a + n_b)

    def mlp_bwd(dh, dhb, h_mid, mlp_saved, layer):
        n2, u, a = mlp_saved
        (du,) = dx_mm(f"mlp_down_dx{layer}", dhb, 'mlp_down', layer,
                      lambda acc, uu: (acc * (2.0 * jnp.maximum(uu.astype(F32), 0.0)),), (u,), (BF16,))
        dw_mm(f"mlp_down_dw{layer}", a, dhb, 'mlp_down', layer)
        dh2, dh2b, dg, cs = dx_mm(f"mlp_up_dx{layer}", du, 'mlp_up', layer, _rms_bwd_epilogue,
                                  (h_mid, dh, gain(small['mlp_norm'][layer])), (F32, BF16), n_sums=2)
        dw_mm(f"mlp_up_dw{layer}", n2, du, 'mlp_up', layer)
        sg['mlp_norm'][layer] = dg[0]
        return dh2, dh2b, cs

    dkp = jnp.zeros(kpt.shape, F32)
    dvp = jnp.zeros(vpt.shape, F32)
    sg['b_norm'], sg['b_bq'], sg['b_bo'], sg['b_sinks'] = [None] * n_b, [None] * n_b, [None] * n_b, [None] * n_b
    dbias = [None] * n_b
    for j in reversed(range(n_b)):
        layer = n_a + j
        h_in, n1, qbt, o_t, h_mid, mlp_saved = saved[layer]
        dh, dhb, cs = mlp_bwd(dh, dhb, h_mid, mlp_saved, layer)
        sg['b_bo'][j] = cs[0]
        (do_t,) = dx_mm(f"b_wo_dx{j}", dhb, 'b_wo', j, plain, (), (BF16,), out_t=True)
        dw_mm(f"b_wo_dw{j}", o_t, dhb, 'b_wo', j, x_t=True)
        dq_t, dkp, dvp, dbias[j], dsink = swa_bwd(f"swa_bwd{j}", qbt, kpt, vpt, bias_t, sink_rows[j], do_t, dkp, dvp)
        sg['b_sinks'][j] = colsum(f"sink_grad{j}", dsink.reshape(16, WINDOW).T)[0]
        sg['b_bq'][j] = rowsum(f"b_bq_grad{j}", dq_t)
        dh, dhb, dg, _ = dx_mm(f"b_q_dx{j}", dq_t, 'b_wq', j, _rms_bwd_epilogue,
                               (h_in, dh, gain(small['b_norm'][j])), (F32, BF16), a_t=True, n_sums=2)
        dw_mm(f"b_q_dw{j}", n1, dq_t, 'b_wq', j, dy_t=True)
        sg['b_norm'][j] = dg[0]
    unt = lambda t: t.reshape(2, 2 * WINDOW, 8, WINDOW).transpose(0, 2, 3, 1).reshape(bias_flat.shape)
    sg['rel_bias'] = bias_table_grad(unt(dbias[0]), unt(dbias[1]))[:, :N_BUCKETS].T

    dkv_t = jnp.concatenate([dkp[:, :, WINDOW:], dvp[:, :, WINDOW:]], axis=0).reshape(-1, s)
    sg['b_kv'] = rowsum("b_kv_grad", dkv_t)
    dkvb = dkv_t.astype(BF16)
    dh, dhb, dg, _ = dx_mm("kv_dx", dkvb, 'w_kv', 0, _rms_bwd_epilogue, (h_kv, dh, gain(small['kv_norm'])),
                           (F32, BF16), a_t=True, n_sums=2)
    dw_mm("kv_dw", nkv, dkvb, 'w_kv', 0, dy_t=True)
    sg['kv_norm'] = dg[0]

    sg['a_norm'] = [None] * n_a
    for l in reversed(range(n_a)):
        h_in, n1, qkvt, o_t, rtab, h_mid, mlp_saved = saved[l]
        dh, dhb, _ = mlp_bwd(dh, dhb, h_mid, mlp_saved, l)
        (do_t,) = dx_mm(f"a_wo_dx{l}", dhb, 'a_wo', l, plain, (), (BF16,), out_t=True)
        dw_mm(f"a_wo_dw{l}", o_t, dhb, 'a_wo', l, x_t=True)
        dq_t, dk_t, dv_t, carried = sb_bwd(f"sb_bwd{l}", qkvt, do_t.reshape(d // HEAD_DIM, HEAD_DIM, s), rtab,
                                           ex.bwd_carry(l, gb))
        ex.bwd_done(l, carried)
        dqkv_t = jnp.concatenate([dq_t, dk_t, dv_t], axis=0).reshape(3 * d, s)
        dh, dhb, dg, _ = dx_mm(f"a_qkv_dx{l}", dqkv_t, 'a_wqkv', l, _rms_bwd_epilogue,
                               (h_in, dh, gain(small['a_norm'][l])), (F32, BF16), a_t=True, n_sums=2)
        dw_mm(f"a_qkv_dw{l}", n1, dqkv_t, 'a_wqkv', l, dy_t=True)
        sg['a_norm'][l] = dg[0]

    small_grads = {
        'a_norm': jnp.stack(sg['a_norm']), 'kv_norm': sg['kv_norm'], 'b_kv': sg['b_kv'],
        'b_norm': jnp.stack(sg['b_norm']), 'b_bq': jnp.stack(sg['b_bq']), 'b_sinks': jnp.stack(sg['b_sinks']),
        'b_bo': jnp.stack(sg['b_bo']), 'rel_bias': sg['rel_bias'], 'mlp_norm': jnp.stack(sg['mlp_norm']),
        'final_norm': sg['final_norm'],
    }
    return loss_b, dh, gb, small_grads


def _full_shape(name, shard_shape):
    if name in COL_SHARDED:
        return shard_shape[:2] + (N_DEV * shard_shape[2],)
    nl, r, n = shard_shape
    return (nl, N_DEV, r, n)


def _as_w3_shape(name, shard_shape):
    full = _full_shape(name, shard_shape)
    return full if name in COL_SHARDED else (full[0], full[1] * full[2], full[3])


def _as_w3(name, full):
    if name in COL_SHARDED:
        return full
    nl, nd, r, n = full.shape
    return full.reshape(nl, nd * r, n)


AG_GROUPS = {
    0: (('a_wqkv', 0, 1),),
    1: (('a_wo', 0, 2), ('mlp_up', 0, 2), ('mlp_down', 0, 2), ('a_wqkv', 1, 1)),
    2: (('mlp_up', 2, 2), ('mlp_down', 2, 2), ('b_wq', 0, 2), ('b_wo', 0, 2), ('w_kv', 0, 1)),
}
RS_GROUPS = {
    'A': (('mlp_up', 1, 3), ('mlp_down', 1, 3), ('b_wq', 0, 2), ('b_wo', 0, 2), ('w_kv', 0, 1)),
    'B': (('a_wqkv', 1, 1), ('a_wo', 1, 1), ('mlp_up', 0, 1), ('mlp_down', 0, 1)),
    'C': (('a_wqkv', 0, 1), ('a_wo', 0, 1)),
}


class _Exchanges:
    def __init__(self, full0, shards, core, chip, w3, m3, v3):
        self.wbuf = {0: {n: _as_w3(n, full0[n]) for n, _, _ in AG_GROUPS[0]}}
        self.shards, self.core, self.chip = shards, core, chip
        self.w3, self.m3, self.v3 = w3, m3, v3
        self.shard_dims = {n: w3[n].shape[1:] for n in BIG}
        self.parts = {}
        self.out = {}

    def weight(self, name, layer):
        for group, members in AG_GROUPS.items():
            for n, l0, nl in members:
                if n == name and l0 <= layer < l0 + nl:
                    return self.wbuf[group][name], layer - l0
        raise KeyError((name, layer))

    def fwd_carry(self, layer):
        names = [n for n, _, _ in AG_GROUPS[layer + 1]]
        shards = [self.shards[layer + 1][n] for n in names]
        return ag_direct(names, shards, [_full_shape(n, sh.shape) for n, sh in zip(names, shards)])

    def fwd_done(self, layer, carried):
        names = [n for n, _, _ in AG_GROUPS[layer + 1]]
        self.wbuf[layer + 1] = {n: _as_w3(n, f) for n, f in zip(names, ag_forward(layer + 1, names, list(carried)))}

    def grad(self, name, layer):
        for group, members in RS_GROUPS.items():
            for n, l0, nl in members:
                if n == name and l0 <= layer < l0 + nl:
                    return (group, name), layer - l0, _as_w3_shape(name, (nl,) + self.shard_dims[name])
        raise KeyError((name, layer))

    def _sibling_stage(self, group, gb):
        names = [n for n, _, _ in RS_GROUPS[group]]
        shapes = [(nl,) + self.shard_dims[n] for n, _, nl in RS_GROUPS[group]]
        gfull = [gb[(group, n)].reshape(_full_shape(n, sh)) for n, sh in zip(names, shapes)]
        recv = sibling_exchange(group, names, gfull, shapes)
        self.parts[group] = [sibling_sum(f"rs_sibling_sum_{group}_{n}", n in COL_SHARDED, g, r, self.core)
                             for n, g, r in zip(names, gfull, recv)]
        return names, shapes

    def bwd_carry(self, layer, gb):
        names, shapes = self._sibling_stage('A' if layer == 1 else 'B', gb)
        return chip_exchange(names, self.parts['A' if layer == 1 else 'B'], shapes)

    def bwd_done(self, layer, carried):
        self._adamw('A' if layer == 1 else 'B', carried)

    def finish(self, gb):
        names, shapes = self._sibling_stage('C', gb)
        ce = chip_exchange(names, self.parts['C'], shapes)
        self._adamw('C', comm_call("rs_chip_exchange_C", ce.build, ce.ins, ce.out_shapes, ce.sems))
        return self.out

    def _adamw(self, group, recv2):
        for (n, l0, _), p, r in zip(RS_GROUPS[group], self.parts[group], recv2):
            self.out[n] = reduce_adamw(f"adamw_{group}_{n}", p, r, self.chip, self.w3[n], self.m3[n], self.v3[n],
                                       l0, self.out.get(n))


def _pack_small(vals):
    flat = jnp.concatenate([vals[n].reshape(-1).astype(F32) for n in SMALL] + [vals['loss'].reshape(-1)])
    rows = -(-flat.shape[0] // 1024) * 8
    return jnp.pad(flat, (0, rows * 128 - flat.shape[0])).reshape(rows, 128)


def _unpack_small(packed, shapes):
    flat = packed.reshape(-1)
    out, off = {}, 0
    for n in SMALL + ['loss']:
        size = int(np.prod(shapes[n]))
        out[n] = flat[off:off + size].reshape(shapes[n])
        off += size
    return out


def kernel(x, a_norm, a_wqkv, a_wo, kv_norm, w_kv, b_kv, b_norm, b_wq, b_bq, b_sinks, b_wo, b_bo, rel_bias, mlp_norm, mlp_up, mlp_down, final_norm, loss_target, m_a_norm, m_a_wqkv, m_a_wo, m_kv_norm, m_w_kv, m_b_kv, m_b_norm, m_b_wq, m_b_bq, m_b_sinks, m_b_wo, m_b_bo, m_rel_bias, m_mlp_norm, m_mlp_up, m_mlp_down, m_final_norm, v_a_norm, v_a_wqkv, v_a_wo, v_kv_norm, v_w_kv, v_b_kv, v_b_norm, v_b_wq, v_b_bq, v_b_sinks, v_b_wo, v_b_bo, v_rel_bias, v_mlp_norm, v_mlp_up, v_mlp_down, v_final_norm):
    w = dict(a_norm=a_norm, a_wqkv=a_wqkv, a_wo=a_wo, kv_norm=kv_norm, w_kv=w_kv, b_kv=b_kv, b_norm=b_norm,
             b_wq=b_wq, b_bq=b_bq, b_sinks=b_sinks, b_wo=b_wo, b_bo=b_bo, rel_bias=rel_bias, mlp_norm=mlp_norm,
             mlp_up=mlp_up, mlp_down=mlp_down, final_norm=final_norm)
    m = dict(a_norm=m_a_norm, a_wqkv=m_a_wqkv, a_wo=m_a_wo, kv_norm=m_kv_norm, w_kv=m_w_kv, b_kv=m_b_kv,
             b_norm=m_b_norm, b_wq=m_b_wq, b_bq=m_b_bq, b_sinks=m_b_sinks, b_wo=m_b_wo, b_bo=m_b_bo,
             rel_bias=m_rel_bias, mlp_norm=m_mlp_norm, mlp_up=m_mlp_up, mlp_down=m_mlp_down, final_norm=m_final_norm)
    v = dict(a_norm=v_a_norm, a_wqkv=v_a_wqkv, a_wo=v_a_wo, kv_norm=v_kv_norm, w_kv=v_w_kv, b_kv=v_b_kv,
             b_norm=v_b_norm, b_wq=v_b_wq, b_bq=v_b_bq, b_sinks=v_b_sinks, b_wo=v_b_wo, b_bo=v_b_bo,
             rel_bias=v_rel_bias, mlp_norm=v_mlp_norm, mlp_up=v_mlp_up, mlp_down=v_mlp_down, final_norm=v_final_norm)
    px, py, pc = _place()
    me = 4 * px + 2 * py + pc
    chip = (2 * px + py).astype(jnp.int32)
    core = pc.astype(jnp.int32)

    as3 = lambda t: t[None] if t.ndim == 2 else t
    w3, m3, v3 = ({n: as3(src[n]) for n in BIG} for src in (w, m, v))
    shards = {g: {n: w3[n][l0:l0 + nl].astype(BF16) for n, l0, nl in members} for g, members in AG_GROUPS.items()}
    an_pad = jnp.zeros((8, 128), F32).at[:a_norm.shape[0]].set(a_norm)
    names0 = [n for n, _, _ in AG_GROUPS[0]]
    full0 = all_gather_weights(names0 + ['a_norm'], [shards[0][n] for n in names0] + [an_pad],
                               [_full_shape(n, shards[0][n].shape) for n in names0] + [(N_DEV, 8, 128)])
    full0 = dict(zip(names0 + ['a_norm'], full0))
    n_a = a_norm.shape[0]
    small = {n: w[n] for n in SMALL}
    small['a_norm'] = full0['a_norm'][:, :n_a].transpose(1, 0, 2).reshape(n_a, -1)

    ex = _Exchanges(full0, shards, core, chip, w3, m3, v3)
    loss_b, grad_x, gb, sgrads = local_step(x[0], loss_target[0], small, ex)
    out = {n: [t.reshape(w[n].shape) for t in bufs] for n, bufs in ex.finish(gb).items()}

    sgrads['loss'] = loss_b[0, :1]
    gathered = all_gather_rows(_pack_small(sgrads))
    shapes = {n: w[n].shape for n in SMALL}
    shapes['a_norm'] = (n_a, a_norm.shape[1] * N_DEV)
    shapes['loss'] = (1,)
    zeros1 = jnp.zeros((1,), F32)

    def packed(src):
        vals = {n: src[n] for n in SMALL}
        vals['a_norm'] = jnp.zeros(shapes['a_norm'], F32)
        vals['loss'] = zeros1
        return _pack_small(vals)

    sm = small_adamw("adamw_small", gathered, packed(w), packed(m), packed(v))
    sm = [_unpack_small(t, shapes) for t in sm]
    g_an = lax.dynamic_slice_in_dim(sm[0]['a_norm'], me * a_norm.shape[1], a_norm.shape[1], axis=1)
    pad = lambda t: jnp.zeros((8, 128), F32).at[:n_a].set(t)
    gathered_an = jnp.zeros((N_DEV, 8, 128), F32).at[0].set(pad(g_an))
    an = small_adamw("adamw_a_norm", gathered_an, pad(a_norm), pad(m_a_norm), pad(v_a_norm))
    for i in range(4):
        sm[i]['a_norm'] = an[i][:n_a]
    for n in BIG:
        for i in range(4):
            sm[i][n] = out[n][i]
    loss = sm[0]['loss'][0]
    return (loss, grad_x[None], *[sm[0][n] for n in WEIGHTS], *[sm[1][n] for n in WEIGHTS],
            *[sm[2][n] for n in WEIGHTS], *[sm[3][n] for n in WEIGHTS])
```

```python
import math

import numpy as np
import jax
import jax.numpy as jnp
from jax import lax
from jax.experimental import pallas as pl
from jax.experimental.pallas import tpu as pltpu

F32 = jnp.float32
BF16 = jnp.bfloat16
MESH = pl.DeviceIdType.MESH

N_DEV = 8
HEAD_DIM = 64
WINDOW = 128
N_BUCKETS = 32
EPS = 1e-5
NEG_INF = -1e30
Q_SCALE = 1.0 / math.sqrt(HEAD_DIM)
LOG2E = 1.4426950408889634

ADAM_LR, ADAM_B1, ADAM_B2, ADAM_EPS, ADAM_WD, ADAM_STEP = 0.001, 0.9, 0.999, 1e-08, 0.01, 10

SB_BQ = 512
SB_BK = 128
SB_DEAD = 160.0
SB_UNSEEN = 1e30
ROW_TILE = 512
VMEM_LIMIT = 56 * 1024 * 1024

WEIGHTS = ['a_norm', 'a_wqkv', 'a_wo', 'kv_norm', 'w_kv', 'b_kv', 'b_norm', 'b_wq', 'b_bq', 'b_sinks', 'b_wo',
           'b_bo', 'rel_bias', 'mlp_norm', 'mlp_up', 'mlp_down', 'final_norm']
BIG = ['a_wqkv', 'a_wo', 'w_kv', 'b_wq', 'b_wo', 'mlp_up', 'mlp_down']
COL_SHARDED = ('a_wqkv', 'mlp_up')
SMALL = ['a_norm', 'kv_norm', 'b_kv', 'b_norm', 'b_bq', 'b_sinks', 'b_bo', 'rel_bias', 'mlp_norm', 'final_norm']


def _params(sem=None):
    return pltpu.CompilerParams(dimension_semantics=sem, vmem_limit_bytes=VMEM_LIMIT)


def _pick(n, cands):
    for c in cands:
        if n % c == 0:
            return c
    raise ValueError(n)


def _tile(n, want):
    return n if n <= want else _pick(n, (want, want // 2, want // 4))


MM_TILE_BUDGET = 36 * 1024 * 1024


def _row_tile(m, contraction, cols, streams):
    weight = 2 * contraction * cols * 2
    for rows in (2048, 1024, 512):
        if m % rows == 0 and weight + 2 * rows * (2 * contraction + cols * sum(streams)) <= MM_TILE_BUDGET:
            return rows
    return _tile(m, 512)


def mm_nn(name, a, w3, layer, epilogue, extras, out_dtypes, a_t=False, out_t=False):
    k, m = a.shape if a_t else a.shape[::-1]
    _, kw, n = w3.shape
    assert kw == k
    tn = _tile(n, 1024)
    tm = _row_tile(m, k, tn, [jnp.dtype(t).itemsize for t in out_dtypes]
                   + [e.dtype.itemsize for e in extras if e.size == m * n])
    ne, no = len(extras), len(out_dtypes)
    a_dim = 0 if a_t else 1

    def body(a_ref, w_ref, *rest):
        ex, outs = rest[:ne], rest[ne:ne + no]
        if out_t:
            acc = lax.dot_general(w_ref[...], a_ref[...], (((0,), (a_dim,)), ((), ())), preferred_element_type=F32)
        else:
            acc = lax.dot_general(a_ref[...], w_ref[...], (((a_dim,), (0,)), ((), ())), preferred_element_type=F32)
        for o, r in zip(outs, epilogue(acc, *[e[...] for e in ex])):
            o[...] = r.astype(o.dtype)

    if out_t:
        tile = pl.BlockSpec((tn, tm), lambda i, j: (j, i))
        vec = pl.BlockSpec((tn, 1), lambda i, j: (j, 0))
        out_shape = (n, m)
    else:
        tile = pl.BlockSpec((tm, tn), lambda i, j: (i, j))
        vec = pl.BlockSpec((1, tn), lambda i, j: (0, j))
        out_shape = (m, n)
    a_spec = pl.BlockSpec((k, tm), lambda i, j: (0, i)) if a_t else pl.BlockSpec((tm, k), lambda i, j: (i, 0))
    return pl.pallas_call(
        body, name=name, grid=(m // tm, n // tn),
        in_specs=[a_spec, pl.BlockSpec((None, k, tn), lambda i, j: (layer, 0, j))]
        + [tile if e.shape == out_shape else vec for e in extras],
        out_specs=[tile] * no,
        out_shape=[jax.ShapeDtypeStruct(out_shape, d) for d in out_dtypes],
        compiler_params=_params(("parallel", "parallel")),
    )(a, w3, *extras)


def mm_nt(name, dy, w3, layer, epilogue, extras, out_dtypes, a_t=False, out_t=False, n_sums=0):
    n, m = dy.shape if a_t else dy.shape[::-1]
    _, k, nw = w3.shape
    assert nw == n and not (out_t and n_sums)
    tko = _tile(k, 1024)
    tm = _row_tile(m, n, tko, [jnp.dtype(t).itemsize for t in out_dtypes]
                   + [e.dtype.itemsize for e in extras if e.size == m * k])
    ne, no = len(extras), len(out_dtypes)
    a_dim = 0 if a_t else 1

    def body(a_ref, w_ref, *rest):
        ex, outs, sums = rest[:ne], rest[ne:ne + no], rest[ne + no:]
        if out_t:
            acc = lax.dot_general(w_ref[...], a_ref[...], (((1,), (a_dim,)), ((), ())), preferred_element_type=F32)
        else:
            acc = lax.dot_general(a_ref[...], w_ref[...], (((a_dim,), (1,)), ((), ())), preferred_element_type=F32)
        res = epilogue(acc, *[e[...] for e in ex])
        for o, v in zip(outs, res):
            o[...] = v.astype(o.dtype)
        if n_sums:
            @pl.when(pl.program_id(0) == 0)
            def _():
                for o in sums:
                    o[...] = jnp.zeros_like(o)

            for o, v in zip(sums, res[no:]):
                o[...] += v

    if out_t:
        tile = pl.BlockSpec((tko, tm), lambda i, ko: (ko, i))
        out_shape = (k, m)
    else:
        tile = pl.BlockSpec((tm, tko), lambda i, ko: (i, ko))
        out_shape = (m, k)
    vec = pl.BlockSpec((1, tko), lambda i, ko: (0, ko))
    a_spec = pl.BlockSpec((n, tm), lambda i, ko: (0, i)) if a_t else pl.BlockSpec((tm, n), lambda i, ko: (i, 0))
    return pl.pallas_call(
        body, name=name, grid=(m // tm, k // tko),
        in_specs=[a_spec, pl.BlockSpec((None, tko, n), lambda i, ko: (layer, ko, 0))]
        + [tile if e.shape == out_shape else vec for e in extras],
        out_specs=[tile] * no + [vec] * n_sums,
        out_shape=[jax.ShapeDtypeStruct(out_shape, d) for d in out_dtypes] + [jax.ShapeDtypeStruct((1, k), F32)] * n_sums,
        compiler_params=_params(("arbitrary" if n_sums else "parallel", "parallel")),
    )(dy, w3, *extras)


def mm_tn(name, x, dy, gbuf, shape, layer, x_t=False, dy_t=False):
    k, s = x.shape if x_t else x.shape[::-1]
    _, kw, n = shape
    assert kw == k and dy.shape == ((n, s) if dy_t else (s, n))
    tkk = _tile(k, 512)
    tn = _tile(n, 1024)

    def body(x_ref, dy_ref, *rest):
        g_out = rest[-1]
        g_out[...] = lax.dot_general(x_ref[...], dy_ref[...], (((1 if x_t else 0,), (1 if dy_t else 0,)), ((), ())),
                                     preferred_element_type=F32).astype(g_out.dtype)

    prev = [] if gbuf is None else [gbuf]
    x_spec = pl.BlockSpec((tkk, s), lambda ki, j: (ki, 0)) if x_t else pl.BlockSpec((s, tkk), lambda ki, j: (0, ki))
    dy_spec = pl.BlockSpec((tn, s), lambda ki, j: (j, 0)) if dy_t else pl.BlockSpec((s, tn), lambda ki, j: (0, j))
    return pl.pallas_call(
        body, name=name, grid=(k // tkk, n // tn),
        in_specs=[x_spec, dy_spec] + [pl.BlockSpec(memory_space=pl.ANY)] * len(prev),
        out_specs=pl.BlockSpec((None, tkk, tn), lambda ki, j: (layer, ki, j)),
        out_shape=jax.ShapeDtypeStruct(shape, BF16),
        input_output_aliases={2: 0} if prev else {},
        compiler_params=_params(("parallel", "parallel")),
    )(x, dy, *prev)


def rms_fwd(name, h, g):
    s, d = h.shape
    tr = _pick(s, (ROW_TILE, 256, 128))

    def body(h_ref, g_ref, o_ref):
        x = h_ref[...]
        r = lax.rsqrt(jnp.mean(x * x, axis=-1, keepdims=True) + EPS)
        o_ref[...] = (x * r * g_ref[...]).astype(o_ref.dtype)

    return pl.pallas_call(
        body, name=name, grid=(s // tr,),
        in_specs=[pl.BlockSpec((tr, d), lambda i: (i, 0)), pl.BlockSpec((1, d), lambda i: (0, 0))],
        out_specs=pl.BlockSpec((tr, d), lambda i: (i, 0)),
        out_shape=jax.ShapeDtypeStruct((s, d), BF16),
        compiler_params=_params(("parallel",)),
    )(h, g.reshape(1, d))


def loss_head(h, g, target):
    s, d = h.shape
    tr = _pick(s, (ROW_TILE, 256, 128))

    def body(h_ref, g_ref, t_ref, dx_ref, dxb_ref, dg_ref, loss_ref):
        i = pl.program_id(0)
        x = h_ref[...]
        r = lax.rsqrt(jnp.mean(x * x, axis=-1, keepdims=True) + EPS)
        xh = x * r
        gw = g_ref[...]
        err = xh * gw - t_ref[...]
        dn_ = err * (1.0 / d)
        dyg = dn_ * gw
        dx = r * (dyg - xh * jnp.mean(dyg * xh, axis=-1, keepdims=True))
        dx_ref[...] = dx
        dxb_ref[...] = dx.astype(BF16)

        @pl.when(i == 0)
        def _():
            dg_ref[...] = jnp.zeros_like(dg_ref)
            loss_ref[...] = jnp.zeros_like(loss_ref)

        dg_ref[...] += jnp.sum(dn_ * xh, axis=0, keepdims=True)
        per_row = jnp.sum(err * err, axis=-1, keepdims=True) * (0.5 / d)
        loss_ref[...] += jnp.broadcast_to(jnp.sum(per_row, axis=0, keepdims=True), loss_ref.shape)

    row = pl.BlockSpec((tr, d), lambda i: (i, 0))
    vec = pl.BlockSpec((1, d), lambda i: (0, 0))
    return pl.pallas_call(
        body, name="loss_head", grid=(s // tr,),
        in_specs=[row, vec, row],
        out_specs=[row, row, vec, pl.BlockSpec((1, 128), lambda i: (0, 0))],
        out_shape=[jax.ShapeDtypeStruct((s, d), F32), jax.ShapeDtypeStruct((s, d), BF16),
                   jax.ShapeDtypeStruct((1, d), F32), jax.ShapeDtypeStruct((1, 128), F32)],
        compiler_params=_params(("arbitrary",)),
    )(h, g.reshape(1, d), target)


def colsum(name, x):
    s, n = x.shape
    tr = _pick(s, (ROW_TILE, 256, 128))

    def body(x_ref, o_ref):
        @pl.when(pl.program_id(0) == 0)
        def _():
            o_ref[...] = jnp.zeros_like(o_ref)

        o_ref[...] += jnp.sum(x_ref[...].astype(F32), axis=0, keepdims=True)

    return pl.pallas_call(
        body, name=name, grid=(s // tr,),
        in_specs=[pl.BlockSpec((tr, n), lambda i: (i, 0))],
        out_specs=pl.BlockSpec((1, n), lambda i: (0, 0)),
        out_shape=jax.ShapeDtypeStruct((1, n), F32),
        compiler_params=_params(("arbitrary",)),
    )(x)


def rowsum(name, x):
    n, s = x.shape
    ts = _pick(s, (1024, 512, 256, 128))

    def body(x_ref, o_ref):
        @pl.when(pl.program_id(0) == 0)
        def _():
            o_ref[...] = jnp.zeros_like(o_ref)

        o_ref[...] += jnp.sum(x_ref[...].astype(F32), axis=1, keepdims=True)

    return pl.pallas_call(
        body, name=name, grid=(s // ts,),
        in_specs=[pl.BlockSpec((n, ts), lambda i: (0, i))],
        out_specs=pl.BlockSpec((n, 1), lambda i: (0, 0)),
        out_shape=jax.ShapeDtypeStruct((n, 1), F32),
        compiler_params=_params(("arbitrary",)),
    )(x)[:, 0]


def _tri_rows(reverse):
    i = np.arange(SB_BK)
    tri = (i[None, :] >= i[:, None]) if reverse else (i[None, :] <= i[:, None])
    tri = np.concatenate([tri, tri], axis=1)
    return jnp.asarray(np.concatenate([tri, np.ones((8, 2 * SB_BK), bool)], axis=0), BF16)


def _hi_lo_rows(x):
    hi = x.astype(BF16)
    lo = (x - hi.astype(F32)).astype(BF16)
    return jnp.concatenate([hi, lo], axis=0)


def _softplus2(zs):
    neg_abs = lax.bitcast_convert_type(lax.bitcast_convert_type(zs, jnp.uint32) | jnp.uint32(0x80000000), F32)
    return jnp.maximum(zs, 0.0) + jnp.log2(1.0 + jnp.exp2(neg_abs))


def _pair_mask(first_rel_block, bq):
    key = lax.broadcasted_iota(jnp.int32, (2 * SB_BK, bq), 0) + first_rel_block * SB_BK
    qry = lax.broadcasted_iota(jnp.int32, (2 * SB_BK, bq), 1)
    return key < qry


def _row_of(table8, sub8, r):
    return jnp.sum(jnp.where(sub8 == r, table8, 0.0), axis=0, keepdims=True)


def _keys(j0):
    return pl.ds(pl.multiple_of(j0 * SB_BK, 2 * SB_BK), 2 * SB_BK)


class Carry:
    def __init__(self, build, ins, out_shapes, sems):
        self.build, self.ins, self.out_shapes, self.sems = build, list(ins), list(out_shapes), list(sems)


def _carried(carry, rest, n_out, n_scratch, first, last):
    n_ci = len(carry.ins) if carry else 0
    n_co = len(carry.out_shapes) if carry else 0
    cin, outs = rest[:n_ci], rest[n_ci:n_ci + n_out]
    cout = rest[n_ci + n_out:n_ci + n_out + n_co]
    scratch = rest[n_ci + n_out + n_co:n_ci + n_out + n_co + n_scratch]
    csems = rest[n_ci + n_out + n_co + n_scratch:]

    def start():
        if carry:
            @pl.when(first)
            def _():
                for cp in carry.build(cin, cout, *csems):
                    cp.start()

    def wait():
        if carry:
            @pl.when(last)
            def _():
                for cp in carry.build(cin, cout, *csems):
                    cp.wait()

    return outs, scratch, start, wait


def _contract0(a, b):
    return lax.dot_general(a, b, (((0,), (0,)), ((), ())), preferred_element_type=F32)


def _contract1(a, b):
    return lax.dot_general(a, b, (((1,), (1,)), ((), ())), preferred_element_type=F32)


def sb_fwd(name, qkvt, exchange=None):
    nh, dh, s = qkvt.shape[0] // 3, qkvt.shape[1], qkvt.shape[2]
    bq = SB_BQ
    per_q = bq // SB_BK
    nkb = s // SB_BK
    assert s % bq == 0 and per_q == 4 and nkb % 8 == 0

    def body(q_ref, k_ref, v_ref, a_ref, *rest):
        head = pl.program_id(0)
        (o_ref, rtab_ref), (acc, zbuf, wbuf), start_carried, wait_carried = _carried(
            exchange, rest, 2, 3, head == 0, head == nh - 1)
        start_carried()
        tri = a_ref[...]
        sub8 = lax.broadcasted_iota(jnp.int32, (8, bq), 0)
        rtab_ref[...] = jnp.full(rtab_ref.shape, SB_UNSEEN, F32)
        kf = k_ref[...].astype(F32)
        k_max2 = jnp.max(jnp.sum(kf * kf, axis=0, keepdims=True), axis=1, keepdims=True)

        def query_block(i, _):
            lanes = pl.ds(pl.multiple_of(i * bq, bq), bq)
            qb = q_ref[:, lanes] * Q_SCALE
            acc[...] = jnp.zeros_like(acc)
            qf = qb.astype(F32)
            bound = jnp.sqrt(jnp.sum(qf * qf, axis=0, keepdims=True) * k_max2) * (1.001 * LOG2E)

            def scores(j0):
                return _contract0(k_ref[:, _keys(j0)], qb) * LOG2E

            def pair(j0, slot, run, rt8, mask, has_prev):
                zs = zbuf[slot]
                zbuf[1 - slot] = scores(jnp.maximum(j0 - 2, 0))
                if has_prev:
                    acc[...] += jnp.dot(v_ref[:, _keys(j0 + 2)], wbuf[1 - slot], preferred_element_type=F32)
                p = _softplus2(zs)
                if mask is not None:
                    p = jnp.where(mask, p, 0.0)
                cr1 = jnp.dot(tri, _hi_lo_rows(p[SB_BK:]), preferred_element_type=F32)
                cr0 = jnp.dot(tri, _hi_lo_rows(p[:SB_BK]), preferred_element_type=F32)
                run1 = run + cr1[SB_BK:SB_BK + 1]
                w = jnp.exp2(jnp.concatenate([zs[:SB_BK] - cr0[:SB_BK] - run1, zs[SB_BK:] - cr1[:SB_BK] - run],
                                             axis=0))
                if mask is not None:
                    w = jnp.where(mask, w, 0.0)
                wbuf[slot] = w.astype(BF16)
                rt8 = jnp.where(j0 % 8 == 6, SB_UNSEEN, rt8)
                rt8 = jnp.where(sub8 == (j0 + 1) % 8, run, jnp.where(sub8 == j0 % 8, run1, rt8))
                rtab_ref[pl.ds(pl.multiple_of((j0 // 8) * 8, 8), 8), lanes] = rt8
                return run1 + cr0[SB_BK:SB_BK + 1], rt8

            def alive(run):
                return jnp.min(run - bound) < SB_DEAD

            top = i * per_q
            zbuf[0] = scores(top + 2)
            state = (jnp.zeros((1, bq), F32), jnp.full((8, bq), SB_UNSEEN, F32))
            state = pair(top + 2, 0, *state, _pair_mask(2, bq), False)
            state = pair(top, 1, *state, _pair_mask(0, bq), True)

            def step(c):
                it, pairs, _, run, rt8 = c
                j0 = top - 2 - 4 * it
                run, rt8 = pair(j0, 0, run, rt8, None, True)
                go = alive(run)
                run, rt8 = lax.cond(go, lambda r, t: pair(j0 - 2, 1, r, t, None, True), lambda r, t: (r, t), run, rt8)
                return it + 1, pairs + 1 + go.astype(jnp.int32), go & alive(run), run, rt8

            pairs = lax.while_loop(lambda c: (c[0] < i) & c[2], step, (0, 0, alive(state[0]), *state))[1]
            acc[...] += jnp.dot(v_ref[:, _keys(top - 2 * pairs)], wbuf[(pairs + 1) % 2], preferred_element_type=F32)
            o_ref[:, lanes] = acc[...].astype(o_ref.dtype)
            return 0

        lax.fori_loop(0, s // bq, query_block, 0)
        wait_carried()

    def head_spec(offset, rows):
        return pl.BlockSpec((None, rows, s), lambda h: (h + offset, 0, 0))

    hbm = pl.BlockSpec(memory_space=pl.ANY)
    c_ins, c_outs, c_sems = (exchange.ins, exchange.out_shapes, exchange.sems) if exchange else ([], [], [])
    outs = pl.pallas_call(
        body, name=name, grid=(nh,),
        in_specs=[head_spec(0, dh), head_spec(nh, dh), head_spec(2 * nh, dh),
                  pl.BlockSpec((SB_BK + 8, 2 * SB_BK), lambda h: (0, 0))] + [hbm] * len(c_ins),
        out_specs=[head_spec(0, dh), head_spec(0, nkb)] + [hbm] * len(c_outs),
        out_shape=[jax.ShapeDtypeStruct((nh, dh, s), BF16), jax.ShapeDtypeStruct((nh, nkb, s), F32)] + c_outs,
        scratch_shapes=[pltpu.VMEM((dh, bq), F32), pltpu.VMEM((2, 2 * SB_BK, bq), F32),
                        pltpu.VMEM((2, 2 * SB_BK, bq), BF16)] + c_sems,
        compiler_params=_params(("arbitrary",)),
    )(qkvt, qkvt, qkvt, _tri_rows(True), *c_ins)
    return outs[0], outs[1], outs[2:]


def sb_bwd(name, qkvt, dot_, rtab, exchange=None):
    nh, dh, s = qkvt.shape[0] // 3, qkvt.shape[1], qkvt.shape[2]
    bq = SB_BQ
    per_q = bq // SB_BK
    nkb = s // SB_BK

    def body(qt_ref, kt_ref, vt_ref, dot_ref, rtab_ref, ar_ref, af_ref, *rest):
        head = pl.program_id(0)
        (dqkv_ref,), (dq_acc, dk_acc, dv_acc, zbuf, dwbuf, dzbuf, wbuf), start_carried, wait_carried = \
            _carried(exchange, rest, 1, 7, head == 0, head == nh - 1)
        dq_ref, dk_ref, dv_ref = dqkv_ref.at[0], dqkv_ref.at[1], dqkv_ref.at[2]
        start_carried()
        dk_acc[...] = jnp.zeros_like(dk_acc)
        dv_acc[...] = jnp.zeros_like(dv_acc)
        tri_rev = ar_ref[...][:SB_BK]
        tri_fwd = af_ref[...]
        sub8 = lax.broadcasted_iota(jnp.int32, (8, bq), 0)

        def query_block(i, _):
            lanes = pl.ds(pl.multiple_of(i * bq, bq), bq)
            qtb = qt_ref[:, lanes] * Q_SCALE
            dotb = dot_ref[:, lanes]
            dq_acc[...] = jnp.zeros_like(dq_acc)
            last_j = i * per_q + 2
            seen = jnp.max(jnp.where(rtab_ref[:, lanes] < 0.1 * SB_UNSEEN, 1.0, 0.0), axis=1, keepdims=True)
            pairs = jnp.clip((jnp.sum(seen).astype(jnp.int32) - per_q) // 2, 0, 2 * i)
            odd = pairs % 2
            first_j = i * per_q - 2 * pairs

            def issue(j0, slot):
                zbuf[slot] = _contract0(kt_ref[:, _keys(j0)], qtb) * LOG2E
                dwbuf[slot] = _contract0(vt_ref[:, _keys(j0)], dotb)

            def retire(j0, slot):
                keys = _keys(j0)
                dq_acc[...] += jnp.dot(kt_ref[:, keys], dzbuf[slot], preferred_element_type=F32)
                dk_acc[:, keys] += _contract1(qtb, dzbuf[slot])
                dv_acc[:, keys] += _contract1(dotb, wbuf[slot])

            def pair(j0, slot, g_run, mask):
                zs = zbuf[slot]
                dw = dwbuf[slot]
                issue(jnp.minimum(j0 + 2, last_j), 1 - slot)
                retire(jnp.maximum(j0 - 2, first_j), 1 - slot)
                p_raw = _softplus2(zs)
                p = p_raw if mask is None else jnp.where(mask, p_raw, 0.0)
                c0 = jnp.dot(tri_rev, _hi_lo_rows(p[:SB_BK]), preferred_element_type=F32)
                c1 = jnp.dot(tri_rev, _hi_lo_rows(p[SB_BK:]), preferred_element_type=F32)
                rt8 = rtab_ref[pl.ds(pl.multiple_of((j0 // 8) * 8, 8), 8), lanes]
                r0 = _row_of(rt8, sub8, j0 % 8)
                r1 = _row_of(rt8, sub8, (j0 + 1) % 8)
                w = jnp.exp2(jnp.concatenate([zs[:SB_BK] - c0 - r0, zs[SB_BK:] - c1 - r1], axis=0))
                if mask is not None:
                    w = jnp.where(mask, w, 0.0)
                g = w * dw
                gg0 = jnp.dot(tri_fwd, _hi_lo_rows(g[:SB_BK]), preferred_element_type=F32)
                gg1 = jnp.dot(tri_fwd, _hi_lo_rows(g[SB_BK:]), preferred_element_type=F32)
                g_run1 = g_run + gg0[SB_BK:SB_BK + 1]
                g_pre = jnp.concatenate([gg0[:SB_BK] + g_run, gg1[:SB_BK] + g_run1], axis=0)
                dz = g - jnp.exp2(zs - p_raw) * g_pre
                if mask is not None:
                    dz = jnp.where(mask, dz, 0.0)
                dzbuf[slot] = dz.astype(BF16)
                wbuf[slot] = w.astype(BF16)
                return g_run1 + gg1[SB_BK:SB_BK + 1]

            issue(first_j, odd)
            dzbuf[...] = jnp.zeros(dzbuf.shape, BF16)
            wbuf[...] = jnp.zeros(wbuf.shape, BF16)

            def step(it, g_run):
                g_run = pair(4 * it, 0, g_run, None)
                return pair(4 * it + 2, 1, g_run, None)

            g_run = lax.cond(odd == 1, lambda g: pair(first_j, 1, g, None), lambda g: g, jnp.zeros((1, bq), F32))
            g_run = lax.fori_loop(i - pairs // 2, i, step, g_run)
            g_run = pair(last_j - 2, 0, g_run, _pair_mask(0, bq))
            pair(last_j, 1, g_run, _pair_mask(2, bq))
            retire(last_j, 1)
            dq_ref[:, lanes] = (dq_acc[...] * Q_SCALE).astype(dq_ref.dtype)
            return 0

        lax.fori_loop(0, s // bq, query_block, 0)
        dk_ref[...] = dk_acc[...].astype(dk_ref.dtype)
        dv_ref[...] = dv_acc[...].astype(dv_ref.dtype)
        wait_carried()

    def head_spec(offset, rows):
        return pl.BlockSpec((None, rows, s), lambda h: (h + offset, 0, 0))

    aspec = pl.BlockSpec((SB_BK + 8, 2 * SB_BK), lambda h: (0, 0))
    pair_f32 = pltpu.VMEM((2, 2 * SB_BK, bq), F32)
    pair_bf16 = pltpu.VMEM((2, 2 * SB_BK, bq), BF16)
    hbm = pl.BlockSpec(memory_space=pl.ANY)
    c_ins, c_outs, c_sems = (exchange.ins, exchange.out_shapes, exchange.sems) if exchange else ([], [], [])
    outs = pl.pallas_call(
        body, name=name, grid=(nh,),
        in_specs=[head_spec(0, dh), head_spec(nh, dh), head_spec(2 * nh, dh), head_spec(0, dh), head_spec(0, nkb),
                  aspec, aspec] + [hbm] * len(c_ins),
        out_specs=[pl.BlockSpec((3, None, dh, s), lambda h: (0, h, 0, 0))] + [hbm] * len(c_outs),
        out_shape=[jax.ShapeDtypeStruct((3, nh, dh, s), BF16)] + c_outs,
        scratch_shapes=[pltpu.VMEM((dh, bq), F32), pltpu.VMEM((dh, s), F32), pltpu.VMEM((dh, s), F32),
                        pair_f32, pair_f32, pair_bf16, pair_bf16] + c_sems,
        compiler_params=_params(("arbitrary",)),
    )(qkvt, qkvt, qkvt, dot_, rtab, _tri_rows(True), _tri_rows(False), *c_ins)
    return outs[0], outs[1:]


SWA_QB = 2


def _swa_probs(qt, kt, bias_t, sink, i):
    cols = qt.shape[1]
    sc = _contract0(kt, qt) + bias_t
    kj = lax.broadcasted_iota(jnp.int32, (2 * WINDOW, cols), 0)
    qi = lax.broadcasted_iota(jnp.int32, (2 * WINDOW, cols), 1) & (WINDOW - 1)
    dist = qi + WINDOW - kj
    valid = (dist >= 0) & (dist < WINDOW) & ((kj >= WINDOW) | (i > 0))
    sc = jnp.where(valid, sc, NEG_INF)
    mx = jnp.maximum(jnp.max(sc, axis=0, keepdims=True), sink)
    p = jnp.exp(sc - mx)
    p_sink = jnp.exp(sink - mx)
    inv = 1.0 / (jnp.sum(p, axis=0, keepdims=True) + p_sink)
    return p, p_sink, inv


def _band(i):
    return pl.ds(pl.multiple_of(i * WINDOW, WINDOW), 2 * WINDOW)


def _heads_to_lanes(blk):
    return jnp.concatenate([blk[r * HEAD_DIM:(r + 1) * HEAD_DIM] for r in range(8)], axis=1)


def _lanes_to_heads(t):
    return jnp.concatenate([t[:, r * WINDOW:(r + 1) * WINDOW] for r in range(8)], axis=0)


def swa_fwd(name, qt, kpt, vpt, bias_t, sink_row):
    d, s = qt.shape
    ng, dh, sp = kpt.shape
    rows, cols = d // ng, SWA_QB * WINDOW
    assert (s // WINDOW) % SWA_QB == 0

    def body(q_ref, k_ref, v_ref, bias_ref, sink_ref, o_ref):
        for u in range(SWA_QB):
            i = pl.program_id(1) * SWA_QB + u
            lanes = slice(u * WINDOW, (u + 1) * WINDOW)
            qb = _heads_to_lanes(q_ref[:, lanes]) * Q_SCALE
            p, _, inv = _swa_probs(qb, k_ref[:, _band(i)], bias_ref[...], sink_ref[...], i)
            o_t = jnp.dot(v_ref[:, _band(i)], p.astype(BF16), preferred_element_type=F32) * inv
            o_ref[:, lanes] = _lanes_to_heads(o_t).astype(o_ref.dtype)

    qspec = pl.BlockSpec((rows, cols), lambda g, i: (g, i))
    kspec = pl.BlockSpec((None, dh, sp), lambda g, i: (g, 0, 0))
    return pl.pallas_call(
        body, name=name, grid=(ng, s // cols),
        in_specs=[qspec, kspec, kspec, pl.BlockSpec((None, 2 * WINDOW, 8 * WINDOW), lambda g, i: (g, 0, 0)),
                  pl.BlockSpec((None, 1, 8 * WINDOW), lambda g, i: (g, 0, 0))],
        out_specs=qspec,
        out_shape=jax.ShapeDtypeStruct(qt.shape, BF16),
        compiler_params=_params(("parallel", "arbitrary")),
    )(qt, kpt, vpt, bias_t, sink_row)


def swa_bwd(name, qt, kpt, vpt, bias_t, sink_row, dot_, dk_in, dv_in):
    d, s = qt.shape
    ng, dh, sp = kpt.shape
    rows, cols = d // ng, SWA_QB * WINDOW

    def body(q_ref, k_ref, v_ref, bias_ref, sink_ref, do_ref, dki_ref, dvi_ref, dq_ref, dk_ref, dv_ref, db_ref, ds_ref):
        @pl.when(pl.program_id(1) == 0)
        def _():
            dk_ref[...] = dki_ref[...]
            dv_ref[...] = dvi_ref[...]
            db_ref[...] = jnp.zeros_like(db_ref)
            ds_ref[...] = jnp.zeros_like(ds_ref)

        for u in range(SWA_QB):
            i = pl.program_id(1) * SWA_QB + u
            band = _band(i)
            lanes = slice(u * WINDOW, (u + 1) * WINDOW)
            qb = _heads_to_lanes(q_ref[:, lanes]) * Q_SCALE
            dob = _heads_to_lanes(do_ref[:, lanes])
            kt = k_ref[:, band]
            p, p_sink, inv = _swa_probs(qb, kt, bias_ref[...], sink_ref[...], i)
            p = p * inv
            dp = _contract0(v_ref[:, band], dob)
            delta = jnp.sum(p * dp, axis=0, keepdims=True)
            dsc = p * (dp - delta)
            ds_ref[...] -= p_sink * inv * delta
            db_ref[...] += dsc
            dscb = dsc.astype(BF16)
            dq_t = jnp.dot(kt, dscb, preferred_element_type=F32) * Q_SCALE
            dq_ref[:, lanes] = _lanes_to_heads(dq_t).astype(dq_ref.dtype)
            dk_ref[:, band] += _contract1(qb, dscb)
            dv_ref[:, band] += _contract1(dob, p.astype(BF16))

    qspec = pl.BlockSpec((rows, cols), lambda g, i: (g, i))
    kspec = pl.BlockSpec((None, dh, sp), lambda g, i: (g, 0, 0))
    bspec = pl.BlockSpec((None, 2 * WINDOW, 8 * WINDOW), lambda g, i: (g, 0, 0))
    sspec = pl.BlockSpec((None, 1, 8 * WINDOW), lambda g, i: (g, 0, 0))
    return pl.pallas_call(
        body, name=name, grid=(ng, s // cols),
        in_specs=[qspec, kspec, kspec, bspec, sspec, qspec, kspec, kspec],
        out_specs=[qspec, kspec, kspec, bspec, sspec],
        out_shape=[jax.ShapeDtypeStruct(qt.shape, BF16), jax.ShapeDtypeStruct(kpt.shape, F32),
                   jax.ShapeDtypeStruct(kpt.shape, F32), jax.ShapeDtypeStruct(bias_t.shape, F32),
                   jax.ShapeDtypeStruct(sink_row.shape, F32)],
        compiler_params=_params(("parallel", "arbitrary")),
    )(qt, kpt, vpt, bias_t, sink_row, dot_, dk_in, dv_in)


def _bucket_onehot():
    qi = np.arange(WINDOW)[:, None]
    kj = np.arange(2 * WINDOW)[None, :]
    n = np.maximum(qi + WINDOW - kj, 0)
    max_exact = N_BUCKETS // 2
    nf = np.maximum(n, 1).astype(np.float64)
    val = np.log(nf / max_exact) / math.log(WINDOW / max_exact) * (N_BUCKETS - max_exact)
    assert np.all(np.abs(val - np.round(val))[(n > max_exact) & (n < WINDOW)] > 1e-3)
    large = np.minimum(max_exact + val.astype(np.int64), N_BUCKETS - 1)
    bucket = np.where(n < max_exact, n, large).reshape(-1)
    onehot = np.zeros((128, bucket.size), np.float32)
    onehot[bucket, np.arange(bucket.size)] = 1.0
    return onehot


def _split3(x):
    a = x.astype(BF16)
    r = x - a.astype(F32)
    b = r.astype(BF16)
    c = (r - b.astype(F32)).astype(BF16)
    return a, b, c


def bias_table(rel_bias):
    nh = rel_bias.shape[1]
    oh = jnp.asarray(_bucket_onehot(), BF16)
    n = oh.shape[1]
    tn = 4096
    rb = jnp.zeros((nh, 128), F32).at[:, :N_BUCKETS].set(rel_bias.T)

    def body(rb_ref, oh_ref, o_ref):
        o_ref[...] = sum(jnp.dot(t, oh_ref[...], preferred_element_type=F32) for t in _split3(rb_ref[...]))

    return pl.pallas_call(
        body, name="bias_table", grid=(n // tn,),
        in_specs=[pl.BlockSpec((nh, 128), lambda i: (0, 0)), pl.BlockSpec((128, tn), lambda i: (0, i))],
        out_specs=pl.BlockSpec((nh, tn), lambda i: (0, i)),
        out_shape=jax.ShapeDtypeStruct((nh, n), F32),
        compiler_params=_params(("parallel",)),
    )(rb, oh)


def bias_table_grad(db0, db1):
    nh, n = db0.shape
    oh = jnp.asarray(_bucket_onehot(), BF16)
    tn = 4096

    def body(a_ref, b_ref, oh_ref, o_ref):
        @pl.when(pl.program_id(0) == 0)
        def _():
            o_ref[...] = jnp.zeros_like(o_ref)

        o_ref[...] += sum(lax.dot_general(t, oh_ref[...], (((1,), (1,)), ((), ())), preferred_element_type=F32)
                          for t in _split3(a_ref[...] + b_ref[...]))

    blk = pl.BlockSpec((nh, tn), lambda i: (0, i))
    return pl.pallas_call(
        body, name="bias_table_grad", grid=(n // tn,),
        in_specs=[blk, blk, pl.BlockSpec((128, tn), lambda i: (0, i))],
        out_specs=pl.BlockSpec((nh, 128), lambda i: (0, 0)),
        out_shape=jax.ShapeDtypeStruct((nh, 128), F32),
        compiler_params=_params(("arbitrary",)),
    )(db0, db1, oh)


def _owner_view(ref, name, d):
    if name == 'a_norm':
        return ref.at[d]
    if name in COL_SHARDED:
        n = ref.shape[2] // N_DEV
        return ref.at[:, :, pl.ds(pl.multiple_of(d * n, 128), n)]
    return ref.at[:, d]


def _place():
    return lax.axis_index("x"), lax.axis_index("y"), lax.axis_index("c")


def _dev(p):
    return 4 * p[0] + 2 * p[1] + p[2]


def _remote(src, dst, send_sem, recv_sem, to):
    return pltpu.make_async_remote_copy(src_ref=src, dst_ref=dst, send_sem=send_sem, recv_sem=recv_sem,
                                        device_id=to, device_id_type=MESH)


def _dma_sems(*shapes):
    return [pltpu.SemaphoreType.DMA(sh) for sh in shapes]


def comm_call(name, build, ins, out_shapes, sems, aliases=None):
    n_in, n_out = len(ins), len(out_shapes)

    def body(*refs):
        copies = build(refs[:n_in], refs[n_in:n_in + n_out], *refs[n_in + n_out:])
        for cp in copies:
            cp.start()
        for cp in copies:
            cp.wait()

    hbm = pl.BlockSpec(memory_space=pl.ANY)
    return pl.pallas_call(
        body, name=name, in_specs=[hbm] * n_in, out_specs=[hbm] * n_out, out_shape=list(out_shapes),
        scratch_shapes=sems, input_output_aliases=aliases or {},
    )(*ins)


def all_gather_weights(names, shards, full_shapes):
    n = len(names)

    def body(*refs):
        ins, outs = refs[:n], refs[n:2 * n]
        send_sems, recv_sems, local_sems = refs[2 * n:]
        x, y, c = _place()
        me, sibling = (x, y, c), (x, y, 1 - c)
        chips = [(1 - x, y), (x, 1 - y), (1 - x, 1 - y)]

        def copy(t, k, block, to, src=None):
            dst = _owner_view(outs[t], names[t], _dev(block))
            return _remote(dst if src is None else src, dst, send_sems.at[t, k], recv_sems.at[t, k], to)

        mine = [pltpu.make_async_copy(ins[t], _owner_view(outs[t], names[t], _dev(me)), local_sems.at[t])
                for t in range(n)]
        for cp in mine:
            cp.start()
        first = []
        for t in range(n):
            first.append(copy(t, 0, me, sibling, src=ins[t]))
            first += [copy(t, 1 + j, me, (*chip, c), src=ins[t]) for j, chip in enumerate(chips)]
        for cp in first:
            cp.start()
        passed = []
        for j, chip in enumerate(chips):
            for t in range(n):
                copy(t, 1 + j, (*chip, c), me).wait_recv()
                fwd = copy(t, 4 + j, (*chip, c), sibling)
                fwd.start()
                passed.append(fwd)
        for t in range(n):
            copy(t, 0, sibling, me).wait_recv()
            for j, chip in enumerate(chips):
                copy(t, 4 + j, (*chip, 1 - c), me).wait_recv()
        for cp in first + passed:
            cp.wait_send()
        for cp in mine:
            cp.wait()

    hbm = pl.BlockSpec(memory_space=pl.ANY)
    return pl.pallas_call(
        body, name="all_gather_layer0",
        in_specs=[hbm] * n, out_specs=[hbm] * n,
        out_shape=[jax.ShapeDtypeStruct(full_shapes[t], shards[t].dtype) for t in range(n)],
        scratch_shapes=_dma_sems((n, 7), (n, 7), (n,)),
    )(*shards)


def ag_direct(names, shards, full_shapes):
    n = len(names)

    def build(ins, outs, send_sems, recv_sems, local_sems):
        x, y, c = _place()
        peers = [(x, y, 1 - c), (1 - x, y, c), (x, 1 - y, c), (1 - x, 1 - y, c)]
        copies = []
        for t in range(n):
            dst = _owner_view(outs[t], names[t], _dev((x, y, c)))
            copies.append(pltpu.make_async_copy(ins[t], dst, local_sems.at[t]))
            copies += [_remote(ins[t], dst, send_sems.at[t, k], recv_sems.at[t, k], to) for k, to in enumerate(peers)]
        return copies

    return Carry(build, shards, [jax.ShapeDtypeStruct(full_shapes[t], shards[t].dtype) for t in range(n)],
                 _dma_sems((n, 4), (n, 4), (n,)))


def ag_forward(tag, names, partial):
    n = len(names)

    def build(ins, outs, send_sems, recv_sems):
        del ins
        x, y, c = _place()
        copies = []
        for t in range(n):
            for k, chip in enumerate([(1 - x, y), (x, 1 - y), (1 - x, 1 - y)]):
                view = _owner_view(outs[t], names[t], _dev((*chip, c)))
                copies.append(_remote(view, view, send_sems.at[t, k], recv_sems.at[t, k], (x, y, 1 - c)))
        return copies

    return comm_call(f"all_gather_forward_{tag}", build, partial, [jax.ShapeDtypeStruct(p.shape, p.dtype) for p in partial],
                     _dma_sems((n, 3), (n, 3)), aliases={t: t for t in range(n)})


def sibling_exchange(tag, names, grads, part_shapes):
    n = len(names)

    def build(ins, outs, send_sems, recv_sems):
        x, y, c = _place()
        return [_remote(_owner_view(ins[t], names[t], 2 * q + 1 - c), outs[t].at[q], send_sems.at[t, q],
                        recv_sems.at[t, q], (x, y, 1 - c)) for t in range(n) for q in range(4)]

    return comm_call(f"rs_sibling_exchange_{tag}", build, grads,
                     [jax.ShapeDtypeStruct((4,) + part_shapes[t], BF16) for t in range(n)], _dma_sems((n, 4), (n, 4)))


def chip_exchange(names, parts, part_shapes):
    n = len(names)

    def build(ins, outs, send_sems, recv_sems):
        x, y, c = _place()
        chips = [(1 - x, y), (x, 1 - y), (1 - x, 1 - y)]
        return [_remote(ins[t].at[2 * chip[0] + chip[1]], outs[t].at[k], send_sems.at[t, k], recv_sems.at[t, k],
                        (*chip, c)) for t in range(n) for k, chip in enumerate(chips)]

    return Carry(build, parts, [jax.ShapeDtypeStruct((3,) + part_shapes[t], BF16) for t in range(n)],
                 _dma_sems((n, 3), (n, 3)))


def all_gather_rows(x):
    r, w = x.shape

    def body(x_ref, out_ref, send_sems, recv_sems, local_sem):
        px, py, pc = _place()
        me = 4 * px + 2 * py + pc
        mine = pltpu.make_async_copy(x_ref, out_ref.at[me], local_sem)
        mine.start()
        copies = []
        for k in range(1, N_DEV):
            peer = (px ^ (k >> 2), py ^ ((k >> 1) & 1), pc ^ (k & 1))
            copies.append(pltpu.make_async_remote_copy(
                src_ref=x_ref, dst_ref=out_ref.at[me], send_sem=send_sems.at[k - 1], recv_sem=recv_sems.at[k - 1],
                device_id=peer, device_id_type=MESH))
        for cp in copies:
            cp.start()
        for k in range(1, N_DEV):
            peer_idx = me ^ k
            pltpu.make_async_remote_copy(
                src_ref=x_ref, dst_ref=out_ref.at[peer_idx], send_sem=send_sems.at[k - 1],
                recv_sem=recv_sems.at[k - 1], device_id=(px, py, pc), device_id_type=MESH).wait_recv()
        for cp in copies:
            cp.wait_send()
        mine.wait()

    vmem = pl.BlockSpec(memory_space=pltpu.VMEM)
    return pl.pallas_call(
        body, name="all_gather_small_grads",
        in_specs=[vmem], out_specs=vmem,
        out_shape=jax.ShapeDtypeStruct((N_DEV, r, w), x.dtype),
        scratch_shapes=[pltpu.SemaphoreType.DMA((N_DEV - 1,)), pltpu.SemaphoreType.DMA((N_DEV - 1,)),
                        pltpu.SemaphoreType.DMA],
    )(x)


def _adamw(w, g, m, v):
    m = ADAM_B1 * m + (1.0 - ADAM_B1) * g
    v = ADAM_B2 * v + (1.0 - ADAM_B2) * (g * g)
    m_hat = m / (1.0 - ADAM_B1 ** ADAM_STEP)
    v_hat = v / (1.0 - ADAM_B2 ** ADAM_STEP)
    return -ADAM_LR * (m_hat / (jnp.sqrt(v_hat) + ADAM_EPS) + ADAM_WD * w), m, v


def sibling_sum(name, col, grads, recv, core):
    _, nl, rows, cols = recv.shape
    tr = _tile(rows, 512)
    rspec = pl.BlockSpec((None, None, tr, cols), lambda q, l, i, c_ref: (q, l, i, 0))
    if col:
        gspec = pl.BlockSpec((None, tr, cols), lambda q, l, i, c_ref: (l, i, 2 * q + c_ref[0]))
    else:
        gspec = pl.BlockSpec((None, None, tr, cols), lambda q, l, i, c_ref: (l, 2 * q + c_ref[0], i, 0))

    def body(c_ref, g_ref, r_ref, o_ref):
        del c_ref
        o_ref[...] = (g_ref[...].astype(F32) + r_ref[...].astype(F32)).astype(BF16)

    return pl.pallas_call(
        body, name=name,
        grid_spec=pltpu.PrefetchScalarGridSpec(num_scalar_prefetch=1, grid=(4, nl, rows // tr),
                                               in_specs=[gspec, rspec], out_specs=rspec),
        out_shape=jax.ShapeDtypeStruct(recv.shape, BF16),
        compiler_params=_params(("parallel", "parallel", "parallel")),
    )(core.reshape(1), grads, recv)


def reduce_adamw(name, parts, recv, chip, w, m, v, l0, prev):
    _, nl, rows, cols = parts.shape
    tr = _tile(rows, 256)

    def body(q_ref, p_ref, r_ref, w_ref, m_ref, v_ref, *rest):
        del q_ref
        g_out, d_out, m_out, v_out = rest[-4:]
        g = ((p_ref[...].astype(F32) + r_ref[0].astype(F32)) + r_ref[1].astype(F32)) + r_ref[2].astype(F32)
        d, mn, vn = _adamw(w_ref[...], g, m_ref[...], v_ref[...])
        g_out[...] = g
        d_out[...] = d
        m_out[...] = mn
        v_out[...] = vn

    blk = pl.BlockSpec((None, tr, cols), lambda l, i, q_ref: (l0 + l, i, 0))
    prev = list(prev) if prev else []
    return pl.pallas_call(
        body, name=name,
        grid_spec=pltpu.PrefetchScalarGridSpec(
            num_scalar_prefetch=1, grid=(nl, rows // tr),
            in_specs=[pl.BlockSpec((None, None, tr, cols), lambda l, i, q_ref: (q_ref[0], l, i, 0)),
                      pl.BlockSpec((3, None, tr, cols), lambda l, i, q_ref: (0, l, i, 0)), blk, blk, blk]
            + [pl.BlockSpec(memory_space=pl.ANY)] * len(prev),
            out_specs=[blk] * 4),
        out_shape=[jax.ShapeDtypeStruct(w.shape, F32)] * 4,
        input_output_aliases={6 + i: i for i in range(len(prev))},
        compiler_params=_params(("parallel", "parallel")),
    )(chip.reshape(1), parts, recv, w, m, v, *prev)


def small_adamw(name, gathered, w, m, v):
    _, r, c = gathered.shape

    def body(ga_ref, w_ref, m_ref, v_ref, g_out, d_out, m_out, v_out):
        g = ga_ref[0]
        for d in range(1, N_DEV):
            g = g + ga_ref[d]
        dl, mn, vn = _adamw(w_ref[...], g, m_ref[...], v_ref[...])
        g_out[...] = g
        d_out[...] = dl
        m_out[...] = mn
        v_out[...] = vn

    return pl.pallas_call(
        body, name=name,
        out_shape=[jax.ShapeDtypeStruct((r, c), F32)] * 4,
        compiler_params=_params(),
    )(gathered, w, m, v)


def _rms(x, g):
    return x * lax.rsqrt(jnp.mean(x * x, axis=-1, keepdims=True) + EPS) * g


def _rms_bwd_epilogue(dn, x, dres, g):
    r = lax.rsqrt(jnp.mean(x * x, axis=-1, keepdims=True) + EPS)
    xh = x * r
    dyg = dn * g
    dx = dres + r * (dyg - xh * jnp.mean(dyg * xh, axis=-1, keepdims=True))
    return dx, dx, jnp.sum(dn * xh, axis=0, keepdims=True), jnp.sum(dx, axis=0, keepdims=True)


def _residual_then_norms(n_terms):
    def epilogue(acc, *ex):
        h = acc
        for t in ex[:n_terms]:
            h = h + t
        return (h,) + tuple(_rms(h, g) for g in ex[n_terms:])
    return epilogue


def local_step(x, target, small, ex):
    s, d = x.shape
    n_a, n_b = small['a_norm'].shape[0], small['b_norm'].shape[0]
    sg = {}
    gb = {}

    def fwd_mm(name, a, wname, layer, epilogue, extras, out_dtypes, **kw):
        return mm_nn(name, a, *ex.weight(wname, layer), epilogue, extras, out_dtypes, **kw)

    def dx_mm(name, dy, wname, layer, epilogue, extras, out_dtypes, **kw):
        return mm_nt(name, dy, *ex.weight(wname, layer), epilogue, extras, out_dtypes, **kw)

    def dw_mm(name, a, dy, wname, layer, **kw):
        key, slab, shape = ex.grad(wname, layer)
        gb[key] = mm_tn(name, a, dy, gb.get(key), shape, slab, **kw)

    plain = lambda acc: (acc,)
    plus_col = lambda acc, b: (acc + b,)

    bias_flat = bias_table(small['rel_bias'])
    bias_t = bias_flat.reshape(2, 8, WINDOW, 2 * WINDOW).transpose(0, 3, 1, 2).reshape(2, 2 * WINDOW, 8 * WINDOW)
    sink_rows = [jnp.repeat(small['b_sinks'][j], WINDOW).reshape(2, 1, 8 * WINDOW) for j in range(n_b)]

    gain = lambda g: g.reshape(1, -1)

    def mlp_fwd(h, n2, layer, next_gains):
        u, a = fwd_mm(f"mlp_up_fwd{layer}", n2, 'mlp_up', layer,
                      lambda acc: (acc, jnp.square(jnp.maximum(acc, 0.0))), (), (BF16, BF16))
        h2, *nexts = fwd_mm(f"mlp_down_fwd{layer}", a, 'mlp_down', layer, _residual_then_norms(1),
                            (h, *[gain(g) for g in next_gains]), (F32,) + (BF16,) * len(next_gains))
        return h2, nexts, (n2, u, a)

    h = x
    saved = []
    n1 = rms_fwd("a_norm_fwd0", h, small['a_norm'][0])
    for l in range(n_a):
        (qkvt,) = fwd_mm(f"a_qkv_fwd{l}", n1, 'a_wqkv', l, plain, (), (BF16,), out_t=True)
        qkvt = qkvt.reshape(3 * d // HEAD_DIM, HEAD_DIM, s)
        o_t, rtab, carried = sb_fwd(f"sb_fwd{l}", qkvt, ex.fwd_carry(l))
        ex.fwd_done(l, carried)
        o_t = o_t.reshape(d, s)
        h_mid, n2 = fwd_mm(f"a_wo_fwd{l}", o_t, 'a_wo', l, _residual_then_norms(1),
                           (h, gain(small['mlp_norm'][l])), (F32, BF16), a_t=True)
        next_gains = [small['a_norm'][l + 1]] if l + 1 < n_a else [small['b_norm'][0], small['kv_norm']]
        h_out, nexts, mlp_saved = mlp_fwd(h_mid, n2, l, next_gains)
        saved.append((h, n1, qkvt, o_t, rtab, h_mid, mlp_saved))
        h, n1 = h_out, nexts[0]
    h_kv, nkv = h, nexts[1]
    (kvt,) = fwd_mm("kv_fwd", nkv, 'w_kv', 0, plus_col, (small['b_kv'].reshape(-1, 1),), (BF16,), out_t=True)
    kvt = kvt.reshape(2, 2, HEAD_DIM, s)
    kpt, vpt = (jnp.pad(t, ((0, 0), (0, 0), (WINDOW, 0))) for t in (kvt[0], kvt[1]))
    for j in range(n_b):
        layer = n_a + j
        (qbt,) = fwd_mm(f"b_q_fwd{j}", n1, 'b_wq', j, plus_col, (small['b_bq'][j].reshape(-1, 1),), (BF16,),
                        out_t=True)
        o_t = swa_fwd(f"swa_fwd{j}", qbt, kpt, vpt, bias_t, sink_rows[j])
        h_mid, n2 = fwd_mm(f"b_wo_fwd{j}", o_t, 'b_wo', j, _residual_then_norms(2),
                           (h, gain(small['b_bo'][j]), gain(small['mlp_norm'][layer])), (F32, BF16), a_t=True)
        h_out, nexts, mlp_saved = mlp_fwd(h_mid, n2, layer, [small['b_norm'][j + 1]] if j + 1 < n_b else [])
        saved.append((h, n1, qbt, o_t, h_mid, mlp_saved))
        h, n1 = h_out, (nexts[0] if nexts else None)

    dh, dhb, dg_final, loss_b = loss_head(h, small['final_norm'], target)
    sg['final_norm'] = dg_final[0]
    sg['mlp_norm'] = [None] * (n_a + n_b)

    def mlp_bwd(dh, dhb, h_mid, mlp_saved, layer):
        n2, u, a = mlp_saved
        (du,) = dx_mm(f"mlp_down_dx{layer}", dhb, 'mlp_down', layer,
                      lambda acc, uu: (acc * (2.0 * jnp.maximum(uu.astype(F32), 0.0)),), (u,), (BF16,))
        dw_mm(f"mlp_down_dw{layer}", a, dhb, 'mlp_down', layer)
        dh2, dh2b, dg, cs = dx_mm(f"mlp_up_dx{layer}", du, 'mlp_up', layer, _rms_bwd_epilogue,
                                  (h_mid, dh, gain(small['mlp_norm'][layer])), (F32, BF16), n_sums=2)
        dw_mm(f"mlp_up_dw{layer}", n2, du, 'mlp_up', layer)
        sg['mlp_norm'][layer] = dg[0]
        return dh2, dh2b, cs

    dkp = jnp.zeros(kpt.shape, F32)
    dvp = jnp.zeros(vpt.shape, F32)
    sg['b_norm'], sg['b_bq'], sg['b_bo'], sg['b_sinks'] = [None] * n_b, [None] * n_b, [None] * n_b, [None] * n_b
    dbias = [None] * n_b
    for j in reversed(range(n_b)):
        layer = n_a + j
        h_in, n1, qbt, o_t, h_mid, mlp_saved = saved[layer]
        dh, dhb, cs = mlp_bwd(dh, dhb, h_mid, mlp_saved, layer)
        sg['b_bo'][j] = cs[0]
        (do_t,) = dx_mm(f"b_wo_dx{j}", dhb, 'b_wo', j, plain, (), (BF16,), out_t=True)
        dw_mm(f"b_wo_dw{j}", o_t, dhb, 'b_wo', j, x_t=True)
        dq_t, dkp, dvp, dbias[j], dsink = swa_bwd(f"swa_bwd{j}", qbt, kpt, vpt, bias_t, sink_rows[j], do_t, dkp, dvp)
        sg['b_sinks'][j] = colsum(f"sink_grad{j}", dsink.reshape(16, WINDOW).T)[0]
        sg['b_bq'][j] = rowsum(f"b_bq_grad{j}", dq_t)
        dh, dhb, dg, _ = dx_mm(f"b_q_dx{j}", dq_t, 'b_wq', j, _rms_bwd_epilogue,
                               (h_in, dh, gain(small['b_norm'][j])), (F32, BF16), a_t=True, n_sums=2)
        dw_mm(f"b_q_dw{j}", n1, dq_t, 'b_wq', j, dy_t=True)
        sg['b_norm'][j] = dg[0]
    unt = lambda t: t.reshape(2, 2 * WINDOW, 8, WINDOW).transpose(0, 2, 3, 1).reshape(bias_flat.shape)
    sg['rel_bias'] = bias_table_grad(unt(dbias[0]), unt(dbias[1]))[:, :N_BUCKETS].T

    dkv_t = jnp.concatenate([dkp[:, :, WINDOW:], dvp[:, :, WINDOW:]], axis=0).reshape(-1, s)
    sg['b_kv'] = rowsum("b_kv_grad", dkv_t)
    dkvb = dkv_t.astype(BF16)
    dh, dhb, dg, _ = dx_mm("kv_dx", dkvb, 'w_kv', 0, _rms_bwd_epilogue, (h_kv, dh, gain(small['kv_norm'])),
                           (F32, BF16), a_t=True, n_sums=2)
    dw_mm("kv_dw", nkv, dkvb, 'w_kv', 0, dy_t=True)
    sg['kv_norm'] = dg[0]

    sg['a_norm'] = [None] * n_a
    for l in reversed(range(n_a)):
        h_in, n1, qkvt, o_t, rtab, h_mid, mlp_saved = saved[l]
        dh, dhb, _ = mlp_bwd(dh, dhb, h_mid, mlp_saved, l)
        (do_t,) = dx_mm(f"a_wo_dx{l}", dhb, 'a_wo', l, plain, (), (BF16,), out_t=True)
        dw_mm(f"a_wo_dw{l}", o_t, dhb, 'a_wo', l, x_t=True)
        dqkv_t, carried = sb_bwd(f"sb_bwd{l}", qkvt, do_t.reshape(d // HEAD_DIM, HEAD_DIM, s), rtab,
                                 ex.bwd_carry(l, gb))
        ex.bwd_done(l, carried)
        dqkv_t = dqkv_t.reshape(3 * d, s)
        dh, dhb, dg, _ = dx_mm(f"a_qkv_dx{l}", dqkv_t, 'a_wqkv', l, _rms_bwd_epilogue,
                               (h_in, dh, gain(small['a_norm'][l])), (F32, BF16), a_t=True, n_sums=2)
        dw_mm(f"a_qkv_dw{l}", n1, dqkv_t, 'a_wqkv', l, dy_t=True)
        sg['a_norm'][l] = dg[0]

    small_grads = {
        'a_norm': jnp.stack(sg['a_norm']), 'kv_norm': sg['kv_norm'], 'b_kv': sg['b_kv'],
        'b_norm': jnp.stack(sg['b_norm']), 'b_bq': jnp.stack(sg['b_bq']), 'b_sinks': jnp.stack(sg['b_sinks']),
        'b_bo': jnp.stack(sg['b_bo']), 'rel_bias': sg['rel_bias'], 'mlp_norm': jnp.stack(sg['mlp_norm']),
        'final_norm': sg['final_norm'],
    }
    return loss_b, dh, gb, small_grads


def _full_shape(name, shard_shape):
    if name in COL_SHARDED:
        return shard_shape[:2] + (N_DEV * shard_shape[2],)
    nl, r, n = shard_shape
    return (nl, N_DEV, r, n)


def _as_w3_shape(name, shard_shape):
    full = _full_shape(name, shard_shape)
    return full if name in COL_SHARDED else (full[0], full[1] * full[2], full[3])


def _as_w3(name, full):
    if name in COL_SHARDED:
        return full
    nl, nd, r, n = full.shape
    return full.reshape(nl, nd * r, n)


AG_GROUPS = {
    0: (('a_wqkv', 0, 1),),
    1: (('a_wo', 0, 2), ('mlp_up', 0, 2), ('mlp_down', 0, 2), ('a_wqkv', 1, 1)),
    2: (('mlp_up', 2, 2), ('mlp_down', 2, 2), ('b_wq', 0, 2), ('b_wo', 0, 2), ('w_kv', 0, 1)),
}
RS_GROUPS = {
    'A': (('mlp_up', 1, 3), ('mlp_down', 1, 3), ('b_wq', 0, 2), ('b_wo', 0, 2), ('w_kv', 0, 1)),
    'B': (('a_wqkv', 1, 1), ('a_wo', 0, 2), ('mlp_up', 0, 1), ('mlp_down', 0, 1)),
    'C': (('a_wqkv', 0, 1),),
}


class _Exchanges:
    def __init__(self, full0, shards, core, chip, w3, m3, v3):
        self.wbuf = {0: {n: _as_w3(n, full0[n]) for n, _, _ in AG_GROUPS[0]}}
        self.shards, self.core, self.chip = shards, core, chip
        self.w3, self.m3, self.v3 = w3, m3, v3
        self.shard_dims = {n: w3[n].shape[1:] for n in BIG}
        self.parts = {}
        self.out = {}

    def weight(self, name, layer):
        for group, members in AG_GROUPS.items():
            for n, l0, nl in members:
                if n == name and l0 <= layer < l0 + nl:
                    return self.wbuf[group][name], layer - l0
        raise KeyError((name, layer))

    def fwd_carry(self, layer):
        names = [n for n, _, _ in AG_GROUPS[layer + 1]]
        shards = [self.shards[layer + 1][n] for n in names]
        return ag_direct(names, shards, [_full_shape(n, sh.shape) for n, sh in zip(names, shards)])

    def fwd_done(self, layer, carried):
        names = [n for n, _, _ in AG_GROUPS[layer + 1]]
        self.wbuf[layer + 1] = {n: _as_w3(n, f) for n, f in zip(names, ag_forward(layer + 1, names, list(carried)))}

    def grad(self, name, layer):
        for group, members in RS_GROUPS.items():
            for n, l0, nl in members:
                if n == name and l0 <= layer < l0 + nl:
                    return (group, name), layer - l0, _as_w3_shape(name, (nl,) + self.shard_dims[name])
        raise KeyError((name, layer))

    def _sibling_stage(self, group, gb):
        names = [n for n, _, _ in RS_GROUPS[group]]
        shapes = [(nl,) + self.shard_dims[n] for n, _, nl in RS_GROUPS[group]]
        gfull = [gb[(group, n)].reshape(_full_shape(n, sh)) for n, sh in zip(names, shapes)]
        recv = sibling_exchange(group, names, gfull, shapes)
        self.parts[group] = [sibling_sum(f"rs_sibling_sum_{group}_{n}", n in COL_SHARDED, g, r, self.core)
                             for n, g, r in zip(names, gfull, recv)]
        return names, shapes

    def bwd_carry(self, layer, gb):
        names, shapes = self._sibling_stage('A' if layer == 1 else 'B', gb)
        return chip_exchange(names, self.parts['A' if layer == 1 else 'B'], shapes)

    def bwd_done(self, layer, carried):
        self._adamw('A' if layer == 1 else 'B', carried)

    def finish(self, gb):
        names, shapes = self._sibling_stage('C', gb)
        ce = chip_exchange(names, self.parts['C'], shapes)
        self._adamw('C', comm_call("rs_chip_exchange_C", ce.build, ce.ins, ce.out_shapes, ce.sems))
        return self.out

    def _adamw(self, group, recv2):
        for (n, l0, _), p, r in zip(RS_GROUPS[group], self.parts[group], recv2):
            self.out[n] = reduce_adamw(f"adamw_{group}_{n}", p, r, self.chip, self.w3[n], self.m3[n], self.v3[n],
                                       l0, self.out.get(n))


def _pack_small(vals):
    flat = jnp.concatenate([vals[n].reshape(-1).astype(F32) for n in SMALL] + [vals['loss'].reshape(-1)])
    rows = -(-flat.shape[0] // 1024) * 8
    return jnp.pad(flat, (0, rows * 128 - flat.shape[0])).reshape(rows, 128)


def _unpack_small(packed, shapes):
    flat = packed.reshape(-1)
    out, off = {}, 0
    for n in SMALL + ['loss']:
        size = int(np.prod(shapes[n]))
        out[n] = flat[off:off + size].reshape(shapes[n])
        off += size
    return out


def kernel(x, a_norm, a_wqkv, a_wo, kv_norm, w_kv, b_kv, b_norm, b_wq, b_bq, b_sinks, b_wo, b_bo, rel_bias, mlp_norm, mlp_up, mlp_down, final_norm, loss_target, m_a_norm, m_a_wqkv, m_a_wo, m_kv_norm, m_w_kv, m_b_kv, m_b_norm, m_b_wq, m_b_bq, m_b_sinks, m_b_wo, m_b_bo, m_rel_bias, m_mlp_norm, m_mlp_up, m_mlp_down, m_final_norm, v_a_norm, v_a_wqkv, v_a_wo, v_kv_norm, v_w_kv, v_b_kv, v_b_norm, v_b_wq, v_b_bq, v_b_sinks, v_b_wo, v_b_bo, v_rel_bias, v_mlp_norm, v_mlp_up, v_mlp_down, v_final_norm):
    w = dict(a_norm=a_norm, a_wqkv=a_wqkv, a_wo=a_wo, kv_norm=kv_norm, w_kv=w_kv, b_kv=b_kv, b_norm=b_norm,
             b_wq=b_wq, b_bq=b_bq, b_sinks=b_sinks, b_wo=b_wo, b_bo=b_bo, rel_bias=rel_bias, mlp_norm=mlp_norm,
             mlp_up=mlp_up, mlp_down=mlp_down, final_norm=final_norm)
    m = dict(a_norm=m_a_norm, a_wqkv=m_a_wqkv, a_wo=m_a_wo, kv_norm=m_kv_norm, w_kv=m_w_kv, b_kv=m_b_kv,
             b_norm=m_b_norm, b_wq=m_b_wq, b_bq=m_b_bq, b_sinks=m_b_sinks, b_wo=m_b_wo, b_bo=m_b_bo,
             rel_bias=m_rel_bias, mlp_norm=m_mlp_norm, mlp_up=m_mlp_up, mlp_down=m_mlp_down, final_norm=m_final_norm)
    v = dict(a_norm=v_a_norm, a_wqkv=v_a_wqkv, a_wo=v_a_wo, kv_norm=v_kv_norm, w_kv=v_w_kv, b_kv=v_b_kv,
             b_norm=v_b_norm, b_wq=v_b_wq, b_bq=v_b_bq, b_sinks=v_b_sinks, b_wo=v_b_wo, b_bo=v_b_bo,
             rel_bias=v_rel_bias, mlp_norm=v_mlp_norm, mlp_up=v_mlp_up, mlp_down=v_mlp_down, final_norm=v_final_norm)
    px, py, pc = _place()
    me = 4 * px + 2 * py + pc
    chip = (2 * px + py).astype(jnp.int32)
    core = pc.astype(jnp.int32)

    as3 = lambda t: t[None] if t.ndim == 2 else t
    w3, m3, v3 = ({n: as3(src[n]) for n in BIG} for src in (w, m, v))
    shards = {g: {n: w3[n][l0:l0 + nl].astype(BF16) for n, l0, nl in members} for g, members in AG_GROUPS.items()}
    an_pad = jnp.zeros((8, 128), F32).at[:a_norm.shape[0]].set(a_norm)
    names0 = [n for n, _, _ in AG_GROUPS[0]]
    full0 = all_gather_weights(names0 + ['a_norm'], [shards[0][n] for n in names0] + [an_pad],
                               [_full_shape(n, shards[0][n].shape) for n in names0] + [(N_DEV, 8, 128)])
    full0 = dict(zip(names0 + ['a_norm'], full0))
    n_a = a_norm.shape[0]
    small = {n: w[n] for n in SMALL}
    small['a_norm'] = full0['a_norm'][:, :n_a].transpose(1, 0, 2).reshape(n_a, -1)

    ex = _Exchanges(full0, shards, core, chip, w3, m3, v3)
    loss_b, grad_x, gb, sgrads = local_step(x[0], loss_target[0], small, ex)
    out = {n: [t.reshape(w[n].shape) for t in bufs] for n, bufs in ex.finish(gb).items()}

    sgrads['loss'] = loss_b[0, :1]
    gathered = all_gather_rows(_pack_small(sgrads))
    shapes = {n: w[n].shape for n in SMALL}
    shapes['a_norm'] = (n_a, a_norm.shape[1] * N_DEV)
    shapes['loss'] = (1,)
    zeros1 = jnp.zeros((1,), F32)

    def packed(src):
        vals = {n: src[n] for n in SMALL}
        vals['a_norm'] = jnp.zeros(shapes['a_norm'], F32)
        vals['loss'] = zeros1
        return _pack_small(vals)

    sm = small_adamw("adamw_small", gathered, packed(w), packed(m), packed(v))
    sm = [_unpack_small(t, shapes) for t in sm]
    g_an = lax.dynamic_slice_in_dim(sm[0]['a_norm'], me * a_norm.shape[1], a_norm.shape[1], axis=1)
    pad = lambda t: jnp.zeros((8, 128), F32).at[:n_a].set(t)
    gathered_an = jnp.zeros((N_DEV, 8, 128), F32).at[0].set(pad(g_an))
    an = small_adamw("adamw_a_norm", gathered_an, pad(a_norm), pad(m_a_norm), pad(v_a_norm))
    for i in range(4):
        sm[i]['a_norm'] = an[i][:n_a]
    for n in BIG:
        for i in range(4):
            sm[i][n] = out[n][i]
    loss = sm[0]['loss'][0]
    return (loss, grad_x[None], *[sm[0][n] for n in WEIGHTS], *[sm[1][n] for n in WEIGHTS],
            *[sm[2][n] for n in WEIGHTS], *[sm[3][n] for n in WEIGHTS])
```

```python
import math

import numpy as np
import jax
import jax.numpy as jnp
from jax import lax
from jax.experimental import pallas as pl
from jax.experimental.pallas import tpu as pltpu

F32 = jnp.float32
BF16 = jnp.bfloat16
MESH = pl.DeviceIdType.MESH

N_DEV = 8
HEAD_DIM = 64
WINDOW = 128
N_BUCKETS = 32
EPS = 1e-5
NEG_INF = -1e30
Q_SCALE = 1.0 / math.sqrt(HEAD_DIM)
LOG2E = 1.4426950408889634

ADAM_LR, ADAM_B1, ADAM_B2, ADAM_EPS, ADAM_WD, ADAM_STEP = 0.001, 0.9, 0.999, 1e-08, 0.01, 10

SB_BQ = 512
SB_BK = 128
SB_DEAD = 160.0
SB_UNSEEN = 1e30
ROW_TILE = 512
VMEM_LIMIT = 56 * 1024 * 1024

WEIGHTS = ['a_norm', 'a_wqkv', 'a_wo', 'kv_norm', 'w_kv', 'b_kv', 'b_norm', 'b_wq', 'b_bq', 'b_sinks', 'b_wo',
           'b_bo', 'rel_bias', 'mlp_norm', 'mlp_up', 'mlp_down', 'final_norm']
BIG = ['a_wqkv', 'a_wo', 'w_kv', 'b_wq', 'b_wo', 'mlp_up', 'mlp_down']
COL_SHARDED = ('a_wqkv', 'mlp_up')
SMALL = ['a_norm', 'kv_norm', 'b_kv', 'b_norm', 'b_bq', 'b_sinks', 'b_bo', 'rel_bias', 'mlp_norm', 'final_norm']


def _params(sem=None):
    return pltpu.CompilerParams(dimension_semantics=sem, vmem_limit_bytes=VMEM_LIMIT)


def _pick(n, cands):
    for c in cands:
        if n % c == 0:
            return c
    raise ValueError(n)


def _tile(n, want):
    return n if n <= want else _pick(n, (want, want // 2, want // 4))


MM_TILE_BUDGET = 36 * 1024 * 1024


def _row_tile(m, contraction, cols, streams):
    weight = 2 * contraction * cols * 2
    for rows in (2048, 1024, 512):
        if m % rows == 0 and weight + 2 * rows * (2 * contraction + cols * sum(streams)) <= MM_TILE_BUDGET:
            return rows
    return _tile(m, 512)


def mm_nn(name, a, w3, layer, epilogue, extras, out_dtypes, a_t=False, out_t=False):
    k, m = a.shape if a_t else a.shape[::-1]
    _, kw, n = w3.shape
    assert kw == k
    tn = _tile(n, 1024)
    tm = _row_tile(m, k, tn, [jnp.dtype(t).itemsize for t in out_dtypes]
                   + [e.dtype.itemsize for e in extras if e.size == m * n])
    ne, no = len(extras), len(out_dtypes)
    a_dim = 0 if a_t else 1

    def body(a_ref, w_ref, *rest):
        ex, outs = rest[:ne], rest[ne:ne + no]
        if out_t:
            acc = lax.dot_general(w_ref[...], a_ref[...], (((0,), (a_dim,)), ((), ())), preferred_element_type=F32)
        else:
            acc = lax.dot_general(a_ref[...], w_ref[...], (((a_dim,), (0,)), ((), ())), preferred_element_type=F32)
        for o, r in zip(outs, epilogue(acc, *[e[...] for e in ex])):
            o[...] = r.astype(o.dtype)

    if out_t:
        tile = pl.BlockSpec((tn, tm), lambda i, j: (j, i))
        vec = pl.BlockSpec((tn, 1), lambda i, j: (j, 0))
        out_shape = (n, m)
    else:
        tile = pl.BlockSpec((tm, tn), lambda i, j: (i, j))
        vec = pl.BlockSpec((1, tn), lambda i, j: (0, j))
        out_shape = (m, n)
    a_spec = pl.BlockSpec((k, tm), lambda i, j: (0, i)) if a_t else pl.BlockSpec((tm, k), lambda i, j: (i, 0))
    return pl.pallas_call(
        body, name=name, grid=(m // tm, n // tn),
        in_specs=[a_spec, pl.BlockSpec((None, k, tn), lambda i, j: (layer, 0, j))]
        + [tile if e.shape == out_shape else vec for e in extras],
        out_specs=[tile] * no,
        out_shape=[jax.ShapeDtypeStruct(out_shape, d) for d in out_dtypes],
        compiler_params=_params(("parallel", "parallel")),
    )(a, w3, *extras)


def mm_nt(name, dy, w3, layer, epilogue, extras, out_dtypes, a_t=False, out_t=False, n_sums=0, exchange=None):
    n, m = dy.shape if a_t else dy.shape[::-1]
    _, k, nw = w3.shape
    assert nw == n and not (out_t and n_sums)
    tko = _tile(k, 1024)
    tm = _row_tile(m, n, tko, [jnp.dtype(t).itemsize for t in out_dtypes]
                   + [e.dtype.itemsize for e in extras if e.size == m * k])
    ne, no = len(extras), len(out_dtypes)
    a_dim = 0 if a_t else 1

    def body(a_ref, w_ref, *rest):
        ex = rest[:ne]
        at = lambda step: (pl.program_id(0) == step[0]) & (pl.program_id(1) == step[1])
        results, _, start_carried, wait_carried = _carried(exchange, rest[ne:], no + n_sums, 0, at((0, 0)),
                                                           at((m // tm - 1, k // tko - 1)))
        outs, sums = results[:no], results[no:]
        start_carried()
        if out_t:
            acc = lax.dot_general(w_ref[...], a_ref[...], (((1,), (a_dim,)), ((), ())), preferred_element_type=F32)
        else:
            acc = lax.dot_general(a_ref[...], w_ref[...], (((a_dim,), (1,)), ((), ())), preferred_element_type=F32)
        res = epilogue(acc, *[e[...] for e in ex])
        for o, v in zip(outs, res):
            o[...] = v.astype(o.dtype)
        if n_sums:
            @pl.when(pl.program_id(0) == 0)
            def _():
                for o in sums:
                    o[...] = jnp.zeros_like(o)

            for o, v in zip(sums, res[no:]):
                o[...] += v
        wait_carried()

    if out_t:
        tile = pl.BlockSpec((tko, tm), lambda i, ko: (ko, i))
        out_shape = (k, m)
    else:
        tile = pl.BlockSpec((tm, tko), lambda i, ko: (i, ko))
        out_shape = (m, k)
    vec = pl.BlockSpec((1, tko), lambda i, ko: (0, ko))
    a_spec = pl.BlockSpec((n, tm), lambda i, ko: (0, i)) if a_t else pl.BlockSpec((tm, n), lambda i, ko: (i, 0))
    hbm = pl.BlockSpec(memory_space=pl.ANY)
    c_ins, c_outs, c_sems = (exchange.ins, exchange.out_shapes, exchange.sems) if exchange else ([], [], [])
    sequential = n_sums or exchange
    return pl.pallas_call(
        body, name=name, grid=(m // tm, k // tko),
        in_specs=[a_spec, pl.BlockSpec((None, tko, n), lambda i, ko: (layer, ko, 0))]
        + [tile if e.shape == out_shape else vec for e in extras] + [hbm] * len(c_ins),
        out_specs=[tile] * no + [vec] * n_sums + [hbm] * len(c_outs),
        out_shape=[jax.ShapeDtypeStruct(out_shape, d) for d in out_dtypes] + [jax.ShapeDtypeStruct((1, k), F32)] * n_sums
        + c_outs,
        scratch_shapes=c_sems,
        compiler_params=_params(("arbitrary" if sequential else "parallel", "arbitrary" if exchange else "parallel")),
    )(dy, w3, *extras, *c_ins)


def mm_tn(name, x, dy, gbuf, shape, layer, x_t=False, dy_t=False):
    k, s = x.shape if x_t else x.shape[::-1]
    _, kw, n = shape
    assert kw == k and dy.shape == ((n, s) if dy_t else (s, n))
    tkk = _tile(k, 512)
    tn = _tile(n, 1024)

    def body(x_ref, dy_ref, *rest):
        g_out = rest[-1]
        g_out[...] = lax.dot_general(x_ref[...], dy_ref[...], (((1 if x_t else 0,), (1 if dy_t else 0,)), ((), ())),
                                     preferred_element_type=F32).astype(g_out.dtype)

    prev = [] if gbuf is None else [gbuf]
    x_spec = pl.BlockSpec((tkk, s), lambda ki, j: (ki, 0)) if x_t else pl.BlockSpec((s, tkk), lambda ki, j: (0, ki))
    dy_spec = pl.BlockSpec((tn, s), lambda ki, j: (j, 0)) if dy_t else pl.BlockSpec((s, tn), lambda ki, j: (0, j))
    return pl.pallas_call(
        body, name=name, grid=(k // tkk, n // tn),
        in_specs=[x_spec, dy_spec] + [pl.BlockSpec(memory_space=pl.ANY)] * len(prev),
        out_specs=pl.BlockSpec((None, tkk, tn), lambda ki, j: (layer, ki, j)),
        out_shape=jax.ShapeDtypeStruct(shape, BF16),
        input_output_aliases={2: 0} if prev else {},
        compiler_params=_params(("parallel", "parallel")),
    )(x, dy, *prev)


def rms_fwd(name, h, g):
    s, d = h.shape
    tr = _pick(s, (ROW_TILE, 256, 128))

    def body(h_ref, g_ref, o_ref):
        x = h_ref[...]
        r = lax.rsqrt(jnp.mean(x * x, axis=-1, keepdims=True) + EPS)
        o_ref[...] = (x * r * g_ref[...]).astype(o_ref.dtype)

    return pl.pallas_call(
        body, name=name, grid=(s // tr,),
        in_specs=[pl.BlockSpec((tr, d), lambda i: (i, 0)), pl.BlockSpec((1, d), lambda i: (0, 0))],
        out_specs=pl.BlockSpec((tr, d), lambda i: (i, 0)),
        out_shape=jax.ShapeDtypeStruct((s, d), BF16),
        compiler_params=_params(("parallel",)),
    )(h, g.reshape(1, d))


def loss_head(h, g, target):
    s, d = h.shape
    tr = _pick(s, (ROW_TILE, 256, 128))

    def body(h_ref, g_ref, t_ref, dx_ref, dxb_ref, dg_ref, loss_ref):
        i = pl.program_id(0)
        x = h_ref[...]
        r = lax.rsqrt(jnp.mean(x * x, axis=-1, keepdims=True) + EPS)
        xh = x * r
        gw = g_ref[...]
        err = xh * gw - t_ref[...]
        dn_ = err * (1.0 / d)
        dyg = dn_ * gw
        dx = r * (dyg - xh * jnp.mean(dyg * xh, axis=-1, keepdims=True))
        dx_ref[...] = dx
        dxb_ref[...] = dx.astype(BF16)

        @pl.when(i == 0)
        def _():
            dg_ref[...] = jnp.zeros_like(dg_ref)
            loss_ref[...] = jnp.zeros_like(loss_ref)

        dg_ref[...] += jnp.sum(dn_ * xh, axis=0, keepdims=True)
        per_row = jnp.sum(err * err, axis=-1, keepdims=True) * (0.5 / d)
        loss_ref[...] += jnp.broadcast_to(jnp.sum(per_row, axis=0, keepdims=True), loss_ref.shape)

    row = pl.BlockSpec((tr, d), lambda i: (i, 0))
    vec = pl.BlockSpec((1, d), lambda i: (0, 0))
    return pl.pallas_call(
        body, name="loss_head", grid=(s // tr,),
        in_specs=[row, vec, row],
        out_specs=[row, row, vec, pl.BlockSpec((1, 128), lambda i: (0, 0))],
        out_shape=[jax.ShapeDtypeStruct((s, d), F32), jax.ShapeDtypeStruct((s, d), BF16),
                   jax.ShapeDtypeStruct((1, d), F32), jax.ShapeDtypeStruct((1, 128), F32)],
        compiler_params=_params(("arbitrary",)),
    )(h, g.reshape(1, d), target)


def colsum(name, x):
    s, n = x.shape
    tr = _pick(s, (ROW_TILE, 256, 128))

    def body(x_ref, o_ref):
        @pl.when(pl.program_id(0) == 0)
        def _():
            o_ref[...] = jnp.zeros_like(o_ref)

        o_ref[...] += jnp.sum(x_ref[...].astype(F32), axis=0, keepdims=True)

    return pl.pallas_call(
        body, name=name, grid=(s // tr,),
        in_specs=[pl.BlockSpec((tr, n), lambda i: (i, 0))],
        out_specs=pl.BlockSpec((1, n), lambda i: (0, 0)),
        out_shape=jax.ShapeDtypeStruct((1, n), F32),
        compiler_params=_params(("arbitrary",)),
    )(x)


def rowsum(name, x):
    n, s = x.shape
    ts = _pick(s, (1024, 512, 256, 128))

    def body(x_ref, o_ref):
        @pl.when(pl.program_id(0) == 0)
        def _():
            o_ref[...] = jnp.zeros_like(o_ref)

        o_ref[...] += jnp.sum(x_ref[...].astype(F32), axis=1, keepdims=True)

    return pl.pallas_call(
        body, name=name, grid=(s // ts,),
        in_specs=[pl.BlockSpec((n, ts), lambda i: (0, i))],
        out_specs=pl.BlockSpec((n, 1), lambda i: (0, 0)),
        out_shape=jax.ShapeDtypeStruct((n, 1), F32),
        compiler_params=_params(("arbitrary",)),
    )(x)[:, 0]


def _tri_rows(reverse):
    i = np.arange(SB_BK)
    tri = (i[None, :] >= i[:, None]) if reverse else (i[None, :] <= i[:, None])
    tri = np.concatenate([tri, tri], axis=1)
    return jnp.asarray(np.concatenate([tri, np.ones((8, 2 * SB_BK), bool)], axis=0), BF16)


def _hi_lo_rows(x):
    hi = x.astype(BF16)
    lo = (x - hi.astype(F32)).astype(BF16)
    return jnp.concatenate([hi, lo], axis=0)


def _softplus2(zs):
    neg_abs = lax.bitcast_convert_type(lax.bitcast_convert_type(zs, jnp.uint32) | jnp.uint32(0x80000000), F32)
    return jnp.maximum(zs, 0.0) + jnp.log2(1.0 + jnp.exp2(neg_abs))


def _pair_mask(first_rel_block, bq):
    key = lax.broadcasted_iota(jnp.int32, (2 * SB_BK, bq), 0) + first_rel_block * SB_BK
    qry = lax.broadcasted_iota(jnp.int32, (2 * SB_BK, bq), 1)
    return key < qry


def _row_of(table8, sub8, r):
    return jnp.sum(jnp.where(sub8 == r, table8, 0.0), axis=0, keepdims=True)


def _keys(j0):
    return pl.ds(pl.multiple_of(j0 * SB_BK, 2 * SB_BK), 2 * SB_BK)


class Carry:
    def __init__(self, build, ins, out_shapes, sems, then=None):
        self.build, self.ins, self.out_shapes, self.sems = build, list(ins), list(out_shapes), list(sems)
        self.then = then


def _carried(carry, rest, n_out, n_scratch, first, last):
    n_ci = len(carry.ins) if carry else 0
    n_co = len(carry.out_shapes) if carry else 0
    cin, outs = rest[:n_ci], rest[n_ci:n_ci + n_out]
    cout = rest[n_ci + n_out:n_ci + n_out + n_co]
    scratch = rest[n_ci + n_out + n_co:n_ci + n_out + n_co + n_scratch]
    csems = rest[n_ci + n_out + n_co + n_scratch:]

    def start():
        if carry:
            @pl.when(first)
            def _():
                for cp in carry.build(cin, cout, *csems):
                    cp.start()

    def wait():
        if carry:
            @pl.when(last)
            def _():
                for cp in carry.build(cin, cout, *csems):
                    cp.wait()
                if carry.then:
                    second = carry.then(cin, cout, *csems)
                    for cp in second:
                        cp.start()
                    for cp in second:
                        cp.wait()

    return outs, scratch, start, wait


def _contract0(a, b):
    return lax.dot_general(a, b, (((0,), (0,)), ((), ())), preferred_element_type=F32)


def _contract1(a, b):
    return lax.dot_general(a, b, (((1,), (1,)), ((), ())), preferred_element_type=F32)


def sb_fwd(name, qkvt, exchange=None):
    nh, dh, s = qkvt.shape[0] // 3, qkvt.shape[1], qkvt.shape[2]
    bq = SB_BQ
    per_q = bq // SB_BK
    nkb = s // SB_BK
    assert s % bq == 0 and per_q == 4 and nkb % 8 == 0

    def body(q_ref, k_ref, v_ref, a_ref, *rest):
        head = pl.program_id(0)
        (o_ref, rtab_ref), (acc, zbuf, wbuf), start_carried, wait_carried = _carried(
            exchange, rest, 2, 3, head == 0, head == nh - 1)
        start_carried()
        tri = a_ref[...]
        sub8 = lax.broadcasted_iota(jnp.int32, (8, bq), 0)
        rtab_ref[...] = jnp.full(rtab_ref.shape, SB_UNSEEN, F32)
        kf = k_ref[...].astype(F32)
        k_max2 = jnp.max(jnp.sum(kf * kf, axis=0, keepdims=True), axis=1, keepdims=True)

        def query_block(i, _):
            lanes = pl.ds(pl.multiple_of(i * bq, bq), bq)
            qb = q_ref[:, lanes] * Q_SCALE
            acc[...] = jnp.zeros_like(acc)
            qf = qb.astype(F32)
            bound = jnp.sqrt(jnp.sum(qf * qf, axis=0, keepdims=True) * k_max2) * (1.001 * LOG2E)

            def scores(j0):
                return _contract0(k_ref[:, _keys(j0)], qb) * LOG2E

            def pair(j0, slot, run, rt8, mask, has_prev):
                zs = zbuf[slot]
                zbuf[1 - slot] = scores(jnp.maximum(j0 - 2, 0))
                if has_prev:
                    acc[...] += jnp.dot(v_ref[:, _keys(j0 + 2)], wbuf[1 - slot], preferred_element_type=F32)
                p = _softplus2(zs)
                if mask is not None:
                    p = jnp.where(mask, p, 0.0)
                cr1 = jnp.dot(tri, _hi_lo_rows(p[SB_BK:]), preferred_element_type=F32)
                cr0 = jnp.dot(tri, _hi_lo_rows(p[:SB_BK]), preferred_element_type=F32)
                run1 = run + cr1[SB_BK:SB_BK + 1]
                w = jnp.exp2(jnp.concatenate([zs[:SB_BK] - cr0[:SB_BK] - run1, zs[SB_BK:] - cr1[:SB_BK] - run],
                                             axis=0))
                if mask is not None:
                    w = jnp.where(mask, w, 0.0)
                wbuf[slot] = w.astype(BF16)
                rt8 = jnp.where(j0 % 8 == 6, SB_UNSEEN, rt8)
                rt8 = jnp.where(sub8 == (j0 + 1) % 8, run, jnp.where(sub8 == j0 % 8, run1, rt8))
                rtab_ref[pl.ds(pl.multiple_of((j0 // 8) * 8, 8), 8), lanes] = rt8
                return run1 + cr0[SB_BK:SB_BK + 1], rt8

            def alive(run):
                return jnp.min(run - bound) < SB_DEAD

            top = i * per_q
            zbuf[0] = scores(top + 2)
            state = (jnp.zeros((1, bq), F32), jnp.full((8, bq), SB_UNSEEN, F32))
            state = pair(top + 2, 0, *state, _pair_mask(2, bq), False)
            state = pair(top, 1, *state, _pair_mask(0, bq), True)

            def step(c):
                it, pairs, _, run, rt8 = c
                j0 = top - 2 - 4 * it
                run, rt8 = pair(j0, 0, run, rt8, None, True)
                go = alive(run)
                run, rt8 = lax.cond(go, lambda r, t: pair(j0 - 2, 1, r, t, None, True), lambda r, t: (r, t), run, rt8)
                return it + 1, pairs + 1 + go.astype(jnp.int32), go & alive(run), run, rt8

            pairs = lax.while_loop(lambda c: (c[0] < i) & c[2], step, (0, 0, alive(state[0]), *state))[1]
            acc[...] += jnp.dot(v_ref[:, _keys(top - 2 * pairs)], wbuf[(pairs + 1) % 2], preferred_element_type=F32)
            o_ref[:, lanes] = acc[...].astype(o_ref.dtype)
            return 0

        lax.fori_loop(0, s // bq, query_block, 0)
        wait_carried()

    def head_spec(offset, rows):
        return pl.BlockSpec((None, rows, s), lambda h: (h + offset, 0, 0))

    hbm = pl.BlockSpec(memory_space=pl.ANY)
    c_ins, c_outs, c_sems = (exchange.ins, exchange.out_shapes, exchange.sems) if exchange else ([], [], [])
    outs = pl.pallas_call(
        body, name=name, grid=(nh,),
        in_specs=[head_spec(0, dh), head_spec(nh, dh), head_spec(2 * nh, dh),
                  pl.BlockSpec((SB_BK + 8, 2 * SB_BK), lambda h: (0, 0))] + [hbm] * len(c_ins),
        out_specs=[head_spec(0, dh), head_spec(0, nkb)] + [hbm] * len(c_outs),
        out_shape=[jax.ShapeDtypeStruct((nh, dh, s), BF16), jax.ShapeDtypeStruct((nh, nkb, s), F32)] + c_outs,
        scratch_shapes=[pltpu.VMEM((dh, bq), F32), pltpu.VMEM((2, 2 * SB_BK, bq), F32),
                        pltpu.VMEM((2, 2 * SB_BK, bq), BF16)] + c_sems,
        compiler_params=_params(("arbitrary",)),
    )(qkvt, qkvt, qkvt, _tri_rows(True), *c_ins)
    return outs[0], outs[1], outs[2:]


def sb_bwd(name, qkvt, dot_, rtab, exchange=None):
    nh, dh, s = qkvt.shape[0] // 3, qkvt.shape[1], qkvt.shape[2]
    bq = SB_BQ
    per_q = bq // SB_BK
    nkb = s // SB_BK

    def body(qt_ref, kt_ref, vt_ref, dot_ref, rtab_ref, ar_ref, af_ref, *rest):
        head = pl.program_id(0)
        (dqkv_ref,), (dq_acc, dk_acc, dv_acc, zbuf, dwbuf, dzbuf, wbuf), start_carried, wait_carried = \
            _carried(exchange, rest, 1, 7, head == 0, head == nh - 1)
        dq_ref, dk_ref, dv_ref = dqkv_ref.at[0], dqkv_ref.at[1], dqkv_ref.at[2]
        start_carried()
        dk_acc[...] = jnp.zeros_like(dk_acc)
        dv_acc[...] = jnp.zeros_like(dv_acc)
        tri_rev = ar_ref[...][:SB_BK]
        tri_fwd = af_ref[...]
        sub8 = lax.broadcasted_iota(jnp.int32, (8, bq), 0)

        def query_block(i, _):
            lanes = pl.ds(pl.multiple_of(i * bq, bq), bq)
            qtb = qt_ref[:, lanes] * Q_SCALE
            dotb = dot_ref[:, lanes]
            dq_acc[...] = jnp.zeros_like(dq_acc)
            last_j = i * per_q + 2
            seen = jnp.max(jnp.where(rtab_ref[:, lanes] < 0.1 * SB_UNSEEN, 1.0, 0.0), axis=1, keepdims=True)
            pairs = jnp.clip((jnp.sum(seen).astype(jnp.int32) - per_q) // 2, 0, 2 * i)
            odd = pairs % 2
            first_j = i * per_q - 2 * pairs

            def issue(j0, slot):
                zbuf[slot] = _contract0(kt_ref[:, _keys(j0)], qtb) * LOG2E
                dwbuf[slot] = _contract0(vt_ref[:, _keys(j0)], dotb)

            def retire(j0, slot):
                keys = _keys(j0)
                dq_acc[...] += jnp.dot(kt_ref[:, keys], dzbuf[slot], preferred_element_type=F32)
                dk_acc[:, keys] += _contract1(qtb, dzbuf[slot])
                dv_acc[:, keys] += _contract1(dotb, wbuf[slot])

            def pair(j0, slot, g_run, mask):
                zs = zbuf[slot]
                dw = dwbuf[slot]
                issue(jnp.minimum(j0 + 2, last_j), 1 - slot)
                retire(jnp.maximum(j0 - 2, first_j), 1 - slot)
                p_raw = _softplus2(zs)
                p = p_raw if mask is None else jnp.where(mask, p_raw, 0.0)
                c0 = jnp.dot(tri_rev, _hi_lo_rows(p[:SB_BK]), preferred_element_type=F32)
                c1 = jnp.dot(tri_rev, _hi_lo_rows(p[SB_BK:]), preferred_element_type=F32)
                rt8 = rtab_ref[pl.ds(pl.multiple_of((j0 // 8) * 8, 8), 8), lanes]
                r0 = _row_of(rt8, sub8, j0 % 8)
                r1 = _row_of(rt8, sub8, (j0 + 1) % 8)
                w = jnp.exp2(jnp.concatenate([zs[:SB_BK] - c0 - r0, zs[SB_BK:] - c1 - r1], axis=0))
                if mask is not None:
                    w = jnp.where(mask, w, 0.0)
                g = w * dw
                gg0 = jnp.dot(tri_fwd, _hi_lo_rows(g[:SB_BK]), preferred_element_type=F32)
                gg1 = jnp.dot(tri_fwd, _hi_lo_rows(g[SB_BK:]), preferred_element_type=F32)
                g_run1 = g_run + gg0[SB_BK:SB_BK + 1]
                g_pre = jnp.concatenate([gg0[:SB_BK] + g_run, gg1[:SB_BK] + g_run1], axis=0)
                dz = g - jnp.exp2(zs - p_raw) * g_pre
                if mask is not None:
                    dz = jnp.where(mask, dz, 0.0)
                dzbuf[slot] = dz.astype(BF16)
                wbuf[slot] = w.astype(BF16)
                return g_run1 + gg1[SB_BK:SB_BK + 1]

            issue(first_j, odd)
            dzbuf[...] = jnp.zeros(dzbuf.shape, BF16)
            wbuf[...] = jnp.zeros(wbuf.shape, BF16)

            def step(it, g_run):
                g_run = pair(4 * it, 0, g_run, None)
                return pair(4 * it + 2, 1, g_run, None)

            g_run = lax.cond(odd == 1, lambda g: pair(first_j, 1, g, None), lambda g: g, jnp.zeros((1, bq), F32))
            g_run = lax.fori_loop(i - pairs // 2, i, step, g_run)
            g_run = pair(last_j - 2, 0, g_run, _pair_mask(0, bq))
            pair(last_j, 1, g_run, _pair_mask(2, bq))
            retire(last_j, 1)
            dq_ref[:, lanes] = (dq_acc[...] * Q_SCALE).astype(dq_ref.dtype)
            return 0

        lax.fori_loop(0, s // bq, query_block, 0)
        dk_ref[...] = dk_acc[...].astype(dk_ref.dtype)
        dv_ref[...] = dv_acc[...].astype(dv_ref.dtype)
        wait_carried()

    def head_spec(offset, rows):
        return pl.BlockSpec((None, rows, s), lambda h: (h + offset, 0, 0))

    aspec = pl.BlockSpec((SB_BK + 8, 2 * SB_BK), lambda h: (0, 0))
    pair_f32 = pltpu.VMEM((2, 2 * SB_BK, bq), F32)
    pair_bf16 = pltpu.VMEM((2, 2 * SB_BK, bq), BF16)
    hbm = pl.BlockSpec(memory_space=pl.ANY)
    c_ins, c_outs, c_sems = (exchange.ins, exchange.out_shapes, exchange.sems) if exchange else ([], [], [])
    outs = pl.pallas_call(
        body, name=name, grid=(nh,),
        in_specs=[head_spec(0, dh), head_spec(nh, dh), head_spec(2 * nh, dh), head_spec(0, dh), head_spec(0, nkb),
                  aspec, aspec] + [hbm] * len(c_ins),
        out_specs=[pl.BlockSpec((3, None, dh, s), lambda h: (0, h, 0, 0))] + [hbm] * len(c_outs),
        out_shape=[jax.ShapeDtypeStruct((3, nh, dh, s), BF16)] + c_outs,
        scratch_shapes=[pltpu.VMEM((dh, bq), F32), pltpu.VMEM((dh, s), F32), pltpu.VMEM((dh, s), F32),
                        pair_f32, pair_f32, pair_bf16, pair_bf16] + c_sems,
        compiler_params=_params(("arbitrary",)),
    )(qkvt, qkvt, qkvt, dot_, rtab, _tri_rows(True), _tri_rows(False), *c_ins)
    return outs[0], outs[1:]


SWA_QB = 2


def _swa_probs(qt, kt, bias_t, sink, i):
    cols = qt.shape[1]
    sc = _contract0(kt, qt) + bias_t
    kj = lax.broadcasted_iota(jnp.int32, (2 * WINDOW, cols), 0)
    qi = lax.broadcasted_iota(jnp.int32, (2 * WINDOW, cols), 1) & (WINDOW - 1)
    dist = qi + WINDOW - kj
    valid = (dist >= 0) & (dist < WINDOW) & ((kj >= WINDOW) | (i > 0))
    sc = jnp.where(valid, sc, NEG_INF)
    mx = jnp.maximum(jnp.max(sc, axis=0, keepdims=True), sink)
    p = jnp.exp(sc - mx)
    p_sink = jnp.exp(sink - mx)
    inv = 1.0 / (jnp.sum(p, axis=0, keepdims=True) + p_sink)
    return p, p_sink, inv


def _band(i):
    return pl.ds(pl.multiple_of(i * WINDOW, WINDOW), 2 * WINDOW)


def _heads_to_lanes(blk):
    return jnp.concatenate([blk[r * HEAD_DIM:(r + 1) * HEAD_DIM] for r in range(8)], axis=1)


def _lanes_to_heads(t):
    return jnp.concatenate([t[:, r * WINDOW:(r + 1) * WINDOW] for r in range(8)], axis=0)


def swa_fwd(name, qt, kpt, vpt, bias_t, sink_row):
    d, s = qt.shape
    ng, dh, sp = kpt.shape
    rows, cols = d // ng, SWA_QB * WINDOW
    assert (s // WINDOW) % SWA_QB == 0

    def body(q_ref, k_ref, v_ref, bias_ref, sink_ref, o_ref):
        for u in range(SWA_QB):
            i = pl.program_id(1) * SWA_QB + u
            lanes = slice(u * WINDOW, (u + 1) * WINDOW)
            qb = _heads_to_lanes(q_ref[:, lanes]) * Q_SCALE
            p, _, inv = _swa_probs(qb, k_ref[:, _band(i)], bias_ref[...], sink_ref[...], i)
            o_t = jnp.dot(v_ref[:, _band(i)], p.astype(BF16), preferred_element_type=F32) * inv
            o_ref[:, lanes] = _lanes_to_heads(o_t).astype(o_ref.dtype)

    qspec = pl.BlockSpec((rows, cols), lambda g, i: (g, i))
    kspec = pl.BlockSpec((None, dh, sp), lambda g, i: (g, 0, 0))
    return pl.pallas_call(
        body, name=name, grid=(ng, s // cols),
        in_specs=[qspec, kspec, kspec, pl.BlockSpec((None, 2 * WINDOW, 8 * WINDOW), lambda g, i: (g, 0, 0)),
                  pl.BlockSpec((None, 1, 8 * WINDOW), lambda g, i: (g, 0, 0))],
        out_specs=qspec,
        out_shape=jax.ShapeDtypeStruct(qt.shape, BF16),
        compiler_params=_params(("parallel", "arbitrary")),
    )(qt, kpt, vpt, bias_t, sink_row)


def swa_bwd(name, qt, kpt, vpt, bias_t, sink_row, dot_, dk_in, dv_in):
    d, s = qt.shape
    ng, dh, sp = kpt.shape
    rows, cols = d // ng, SWA_QB * WINDOW

    def body(q_ref, k_ref, v_ref, bias_ref, sink_ref, do_ref, dki_ref, dvi_ref, dq_ref, dk_ref, dv_ref, db_ref, ds_ref):
        @pl.when(pl.program_id(1) == 0)
        def _():
            dk_ref[...] = dki_ref[...]
            dv_ref[...] = dvi_ref[...]
            db_ref[...] = jnp.zeros_like(db_ref)
            ds_ref[...] = jnp.zeros_like(ds_ref)

        for u in range(SWA_QB):
            i = pl.program_id(1) * SWA_QB + u
            band = _band(i)
            lanes = slice(u * WINDOW, (u + 1) * WINDOW)
            qb = _heads_to_lanes(q_ref[:, lanes]) * Q_SCALE
            dob = _heads_to_lanes(do_ref[:, lanes])
            kt = k_ref[:, band]
            p, p_sink, inv = _swa_probs(qb, kt, bias_ref[...], sink_ref[...], i)
            p = p * inv
            dp = _contract0(v_ref[:, band], dob)
            delta = jnp.sum(p * dp, axis=0, keepdims=True)
            dsc = p * (dp - delta)
            ds_ref[...] -= p_sink * inv * delta
            db_ref[...] += dsc
            dscb = dsc.astype(BF16)
            dq_t = jnp.dot(kt, dscb, preferred_element_type=F32) * Q_SCALE
            dq_ref[:, lanes] = _lanes_to_heads(dq_t).astype(dq_ref.dtype)
            dk_ref[:, band] += _contract1(qb, dscb)
            dv_ref[:, band] += _contract1(dob, p.astype(BF16))

    qspec = pl.BlockSpec((rows, cols), lambda g, i: (g, i))
    kspec = pl.BlockSpec((None, dh, sp), lambda g, i: (g, 0, 0))
    bspec = pl.BlockSpec((None, 2 * WINDOW, 8 * WINDOW), lambda g, i: (g, 0, 0))
    sspec = pl.BlockSpec((None, 1, 8 * WINDOW), lambda g, i: (g, 0, 0))
    return pl.pallas_call(
        body, name=name, grid=(ng, s // cols),
        in_specs=[qspec, kspec, kspec, bspec, sspec, qspec, kspec, kspec],
        out_specs=[qspec, kspec, kspec, bspec, sspec],
        out_shape=[jax.ShapeDtypeStruct(qt.shape, BF16), jax.ShapeDtypeStruct(kpt.shape, F32),
                   jax.ShapeDtypeStruct(kpt.shape, F32), jax.ShapeDtypeStruct(bias_t.shape, F32),
                   jax.ShapeDtypeStruct(sink_row.shape, F32)],
        compiler_params=_params(("parallel", "arbitrary")),
    )(qt, kpt, vpt, bias_t, sink_row, dot_, dk_in, dv_in)


def _bucket_onehot():
    qi = np.arange(WINDOW)[:, None]
    kj = np.arange(2 * WINDOW)[None, :]
    n = np.maximum(qi + WINDOW - kj, 0)
    max_exact = N_BUCKETS // 2
    nf = np.maximum(n, 1).astype(np.float64)
    val = np.log(nf / max_exact) / math.log(WINDOW / max_exact) * (N_BUCKETS - max_exact)
    assert np.all(np.abs(val - np.round(val))[(n > max_exact) & (n < WINDOW)] > 1e-3)
    large = np.minimum(max_exact + val.astype(np.int64), N_BUCKETS - 1)
    bucket = np.where(n < max_exact, n, large).reshape(-1)
    onehot = np.zeros((128, bucket.size), np.float32)
    onehot[bucket, np.arange(bucket.size)] = 1.0
    return onehot


def _split3(x):
    a = x.astype(BF16)
    r = x - a.astype(F32)
    b = r.astype(BF16)
    c = (r - b.astype(F32)).astype(BF16)
    return a, b, c


def bias_table(rel_bias):
    nh = rel_bias.shape[1]
    oh = jnp.asarray(_bucket_onehot(), BF16)
    n = oh.shape[1]
    tn = 4096
    rb = jnp.zeros((nh, 128), F32).at[:, :N_BUCKETS].set(rel_bias.T)

    def body(rb_ref, oh_ref, o_ref):
        o_ref[...] = sum(jnp.dot(t, oh_ref[...], preferred_element_type=F32) for t in _split3(rb_ref[...]))

    return pl.pallas_call(
        body, name="bias_table", grid=(n // tn,),
        in_specs=[pl.BlockSpec((nh, 128), lambda i: (0, 0)), pl.BlockSpec((128, tn), lambda i: (0, i))],
        out_specs=pl.BlockSpec((nh, tn), lambda i: (0, i)),
        out_shape=jax.ShapeDtypeStruct((nh, n), F32),
        compiler_params=_params(("parallel",)),
    )(rb, oh)


def bias_table_grad(db0, db1):
    nh, n = db0.shape
    oh = jnp.asarray(_bucket_onehot(), BF16)
    tn = 4096

    def body(a_ref, b_ref, oh_ref, o_ref):
        @pl.when(pl.program_id(0) == 0)
        def _():
            o_ref[...] = jnp.zeros_like(o_ref)

        o_ref[...] += sum(lax.dot_general(t, oh_ref[...], (((1,), (1,)), ((), ())), preferred_element_type=F32)
                          for t in _split3(a_ref[...] + b_ref[...]))

    blk = pl.BlockSpec((nh, tn), lambda i: (0, i))
    return pl.pallas_call(
        body, name="bias_table_grad", grid=(n // tn,),
        in_specs=[blk, blk, pl.BlockSpec((128, tn), lambda i: (0, i))],
        out_specs=pl.BlockSpec((nh, 128), lambda i: (0, 0)),
        out_shape=jax.ShapeDtypeStruct((nh, 128), F32),
        compiler_params=_params(("arbitrary",)),
    )(db0, db1, oh)


def _owner_view(ref, name, d):
    if name == 'a_norm':
        return ref.at[d]
    if name in COL_SHARDED:
        n = ref.shape[2] // N_DEV
        return ref.at[:, :, pl.ds(pl.multiple_of(d * n, 128), n)]
    return ref.at[:, d]


def _place():
    return lax.axis_index("x"), lax.axis_index("y"), lax.axis_index("c")


def _dev(p):
    return 4 * p[0] + 2 * p[1] + p[2]


def _remote(src, dst, send_sem, recv_sem, to):
    return pltpu.make_async_remote_copy(src_ref=src, dst_ref=dst, send_sem=send_sem, recv_sem=recv_sem,
                                        device_id=to, device_id_type=MESH)


def _dma_sems(*shapes):
    return [pltpu.SemaphoreType.DMA(sh) for sh in shapes]


def comm_call(name, build, ins, out_shapes, sems, aliases=None):
    n_in, n_out = len(ins), len(out_shapes)

    def body(*refs):
        copies = build(refs[:n_in], refs[n_in:n_in + n_out], *refs[n_in + n_out:])
        for cp in copies:
            cp.start()
        for cp in copies:
            cp.wait()

    hbm = pl.BlockSpec(memory_space=pl.ANY)
    return pl.pallas_call(
        body, name=name, in_specs=[hbm] * n_in, out_specs=[hbm] * n_out, out_shape=list(out_shapes),
        scratch_shapes=sems, input_output_aliases=aliases or {},
    )(*ins)


def all_gather_weights(names, shards, full_shapes):
    n = len(names)

    def body(*refs):
        ins, outs = refs[:n], refs[n:2 * n]
        send_sems, recv_sems, local_sems = refs[2 * n:]
        x, y, c = _place()
        me, sibling = (x, y, c), (x, y, 1 - c)
        chips = [(1 - x, y), (x, 1 - y), (1 - x, 1 - y)]

        def copy(t, k, block, to, src=None):
            dst = _owner_view(outs[t], names[t], _dev(block))
            return _remote(dst if src is None else src, dst, send_sems.at[t, k], recv_sems.at[t, k], to)

        mine = [pltpu.make_async_copy(ins[t], _owner_view(outs[t], names[t], _dev(me)), local_sems.at[t])
                for t in range(n)]
        for cp in mine:
            cp.start()
        first = []
        for t in range(n):
            first.append(copy(t, 0, me, sibling, src=ins[t]))
            first += [copy(t, 1 + j, me, (*chip, c), src=ins[t]) for j, chip in enumerate(chips)]
        for cp in first:
            cp.start()
        passed = []
        for j, chip in enumerate(chips):
            for t in range(n):
                copy(t, 1 + j, (*chip, c), me).wait_recv()
                fwd = copy(t, 4 + j, (*chip, c), sibling)
                fwd.start()
                passed.append(fwd)
        for t in range(n):
            copy(t, 0, sibling, me).wait_recv()
            for j, chip in enumerate(chips):
                copy(t, 4 + j, (*chip, 1 - c), me).wait_recv()
        for cp in first + passed:
            cp.wait_send()
        for cp in mine:
            cp.wait()

    hbm = pl.BlockSpec(memory_space=pl.ANY)
    return pl.pallas_call(
        body, name="all_gather_layer0",
        in_specs=[hbm] * n, out_specs=[hbm] * n,
        out_shape=[jax.ShapeDtypeStruct(full_shapes[t], shards[t].dtype) for t in range(n)],
        scratch_shapes=_dma_sems((n, 7), (n, 7), (n,)),
    )(*shards)


def ag_direct(names, shards, full_shapes):
    n = len(names)

    def build(ins, outs, send_sems, recv_sems, local_sems, fwd_send_sems, fwd_recv_sems):
        x, y, c = _place()
        peers = [(x, y, 1 - c), (1 - x, y, c), (x, 1 - y, c), (1 - x, 1 - y, c)]
        copies = []
        for t in range(n):
            dst = _owner_view(outs[t], names[t], _dev((x, y, c)))
            copies.append(pltpu.make_async_copy(ins[t], dst, local_sems.at[t]))
            copies += [_remote(ins[t], dst, send_sems.at[t, k], recv_sems.at[t, k], to) for k, to in enumerate(peers)]
        return copies

    def forward(ins, outs, send_sems, recv_sems, local_sems, fwd_send_sems, fwd_recv_sems):
        x, y, c = _place()
        copies = []
        for t in range(n):
            for k, chip in enumerate([(1 - x, y), (x, 1 - y), (1 - x, 1 - y)]):
                view = _owner_view(outs[t], names[t], _dev((*chip, c)))
                copies.append(_remote(view, view, fwd_send_sems.at[t, k], fwd_recv_sems.at[t, k], (x, y, 1 - c)))
        return copies

    return Carry(build, shards, [jax.ShapeDtypeStruct(full_shapes[t], shards[t].dtype) for t in range(n)],
                 _dma_sems((n, 4), (n, 4), (n,), (n, 3), (n, 3)), then=forward)


def sibling_exchange(tag, names, grads, part_shapes):
    n = len(names)

    def build(ins, outs, send_sems, recv_sems):
        x, y, c = _place()
        return [_remote(_owner_view(ins[t], names[t], 2 * q + 1 - c), outs[t].at[q], send_sems.at[t, q],
                        recv_sems.at[t, q], (x, y, 1 - c)) for t in range(n) for q in range(4)]

    return comm_call(f"rs_sibling_exchange_{tag}", build, grads,
                     [jax.ShapeDtypeStruct((4,) + part_shapes[t], BF16) for t in range(n)], _dma_sems((n, 4), (n, 4)))


def chip_exchange(names, parts, part_shapes):
    n = len(names)

    def build(ins, outs, send_sems, recv_sems):
        x, y, c = _place()
        chips = [(1 - x, y), (x, 1 - y), (1 - x, 1 - y)]
        return [_remote(ins[t].at[2 * chip[0] + chip[1]], outs[t].at[k], send_sems.at[t, k], recv_sems.at[t, k],
                        (*chip, c)) for t in range(n) for k, chip in enumerate(chips)]

    return Carry(build, parts, [jax.ShapeDtypeStruct((3,) + part_shapes[t], BF16) for t in range(n)],
                 _dma_sems((n, 3), (n, 3)))


def all_gather_rows(x):
    r, w = x.shape

    def body(x_ref, out_ref, send_sems, recv_sems, local_sem):
        px, py, pc = _place()
        me = 4 * px + 2 * py + pc
        mine = pltpu.make_async_copy(x_ref, out_ref.at[me], local_sem)
        mine.start()
        copies = []
        for k in range(1, N_DEV):
            peer = (px ^ (k >> 2), py ^ ((k >> 1) & 1), pc ^ (k & 1))
            copies.append(pltpu.make_async_remote_copy(
                src_ref=x_ref, dst_ref=out_ref.at[me], send_sem=send_sems.at[k - 1], recv_sem=recv_sems.at[k - 1],
                device_id=peer, device_id_type=MESH))
        for cp in copies:
            cp.start()
        for k in range(1, N_DEV):
            peer_idx = me ^ k
            pltpu.make_async_remote_copy(
                src_ref=x_ref, dst_ref=out_ref.at[peer_idx], send_sem=send_sems.at[k - 1],
                recv_sem=recv_sems.at[k - 1], device_id=(px, py, pc), device_id_type=MESH).wait_recv()
        for cp in copies:
            cp.wait_send()
        mine.wait()

    vmem = pl.BlockSpec(memory_space=pltpu.VMEM)
    return pl.pallas_call(
        body, name="all_gather_small_grads",
        in_specs=[vmem], out_specs=vmem,
        out_shape=jax.ShapeDtypeStruct((N_DEV, r, w), x.dtype),
        scratch_shapes=[pltpu.SemaphoreType.DMA((N_DEV - 1,)), pltpu.SemaphoreType.DMA((N_DEV - 1,)),
                        pltpu.SemaphoreType.DMA],
    )(x)


def _adamw(w, g, m, v):
    m = ADAM_B1 * m + (1.0 - ADAM_B1) * g
    v = ADAM_B2 * v + (1.0 - ADAM_B2) * (g * g)
    m_hat = m / (1.0 - ADAM_B1 ** ADAM_STEP)
    v_hat = v / (1.0 - ADAM_B2 ** ADAM_STEP)
    return -ADAM_LR * (m_hat / (jnp.sqrt(v_hat) + ADAM_EPS) + ADAM_WD * w), m, v


def sibling_sum(name, col, grads, recv, core):
    _, nl, rows, cols = recv.shape
    tr = _tile(rows, 512)
    rspec = pl.BlockSpec((None, None, tr, cols), lambda q, l, i, c_ref: (q, l, i, 0))
    if col:
        gspec = pl.BlockSpec((None, tr, cols), lambda q, l, i, c_ref: (l, i, 2 * q + c_ref[0]))
    else:
        gspec = pl.BlockSpec((None, None, tr, cols), lambda q, l, i, c_ref: (l, 2 * q + c_ref[0], i, 0))

    def body(c_ref, g_ref, r_ref, o_ref):
        del c_ref
        o_ref[...] = (g_ref[...].astype(F32) + r_ref[...].astype(F32)).astype(BF16)

    return pl.pallas_call(
        body, name=name,
        grid_spec=pltpu.PrefetchScalarGridSpec(num_scalar_prefetch=1, grid=(4, nl, rows // tr),
                                               in_specs=[gspec, rspec], out_specs=rspec),
        out_shape=jax.ShapeDtypeStruct(recv.shape, BF16),
        compiler_params=_params(("parallel", "parallel", "parallel")),
    )(core.reshape(1), grads, recv)


def reduce_adamw(name, parts, recv, chip, w, m, v, l0, prev):
    _, nl, rows, cols = parts.shape
    tr = _tile(rows, 256)

    def body(q_ref, p_ref, r_ref, w_ref, m_ref, v_ref, *rest):
        del q_ref
        g_out, d_out, m_out, v_out = rest[-4:]
        g = ((p_ref[...].astype(F32) + r_ref[0].astype(F32)) + r_ref[1].astype(F32)) + r_ref[2].astype(F32)
        d, mn, vn = _adamw(w_ref[...], g, m_ref[...], v_ref[...])
        g_out[...] = g
        d_out[...] = d
        m_out[...] = mn
        v_out[...] = vn

    blk = pl.BlockSpec((None, tr, cols), lambda l, i, q_ref: (l0 + l, i, 0))
    prev = list(prev) if prev else []
    return pl.pallas_call(
        body, name=name,
        grid_spec=pltpu.PrefetchScalarGridSpec(
            num_scalar_prefetch=1, grid=(nl, rows // tr),
            in_specs=[pl.BlockSpec((None, None, tr, cols), lambda l, i, q_ref: (q_ref[0], l, i, 0)),
                      pl.BlockSpec((3, None, tr, cols), lambda l, i, q_ref: (0, l, i, 0)), blk, blk, blk]
            + [pl.BlockSpec(memory_space=pl.ANY)] * len(prev),
            out_specs=[blk] * 4),
        out_shape=[jax.ShapeDtypeStruct(w.shape, F32)] * 4,
        input_output_aliases={6 + i: i for i in range(len(prev))},
        compiler_params=_params(("parallel", "parallel")),
    )(chip.reshape(1), parts, recv, w, m, v, *prev)


def small_adamw(name, gathered, w, m, v):
    _, r, c = gathered.shape

    def body(ga_ref, w_ref, m_ref, v_ref, g_out, d_out, m_out, v_out):
        g = ga_ref[0]
        for d in range(1, N_DEV):
            g = g + ga_ref[d]
        dl, mn, vn = _adamw(w_ref[...], g, m_ref[...], v_ref[...])
        g_out[...] = g
        d_out[...] = dl
        m_out[...] = mn
        v_out[...] = vn

    return pl.pallas_call(
        body, name=name,
        out_shape=[jax.ShapeDtypeStruct((r, c), F32)] * 4,
        compiler_params=_params(),
    )(gathered, w, m, v)


def _rms(x, g):
    return x * lax.rsqrt(jnp.mean(x * x, axis=-1, keepdims=True) + EPS) * g


def _rms_bwd_epilogue(dn, x, dres, g):
    r = lax.rsqrt(jnp.mean(x * x, axis=-1, keepdims=True) + EPS)
    xh = x * r
    dyg = dn * g
    dx = dres + r * (dyg - xh * jnp.mean(dyg * xh, axis=-1, keepdims=True))
    return dx, dx, jnp.sum(dn * xh, axis=0, keepdims=True), jnp.sum(dx, axis=0, keepdims=True)


def _residual_then_norms(n_terms):
    def epilogue(acc, *ex):
        h = acc
        for t in ex[:n_terms]:
            h = h + t
        return (h,) + tuple(_rms(h, g) for g in ex[n_terms:])
    return epilogue


def local_step(x, target, small, ex):
    s, d = x.shape
    n_a, n_b = small['a_norm'].shape[0], small['b_norm'].shape[0]
    sg = {}
    gb = {}

    def fwd_mm(name, a, wname, layer, epilogue, extras, out_dtypes, **kw):
        return mm_nn(name, a, *ex.weight(wname, layer), epilogue, extras, out_dtypes, **kw)

    def dx_mm(name, dy, wname, layer, epilogue, extras, out_dtypes, **kw):
        return mm_nt(name, dy, *ex.weight(wname, layer), epilogue, extras, out_dtypes, **kw)

    def dw_mm(name, a, dy, wname, layer, **kw):
        key, slab, shape = ex.grad(wname, layer)
        gb[key] = mm_tn(name, a, dy, gb.get(key), shape, slab, **kw)

    plain = lambda acc: (acc,)
    plus_col = lambda acc, b: (acc + b,)

    bias_flat = bias_table(small['rel_bias'])
    bias_t = bias_flat.reshape(2, 8, WINDOW, 2 * WINDOW).transpose(0, 3, 1, 2).reshape(2, 2 * WINDOW, 8 * WINDOW)
    sink_rows = [jnp.repeat(small['b_sinks'][j], WINDOW).reshape(2, 1, 8 * WINDOW) for j in range(n_b)]

    gain = lambda g: g.reshape(1, -1)

    def mlp_fwd(h, n2, layer, next_gains):
        u, a = fwd_mm(f"mlp_up_fwd{layer}", n2, 'mlp_up', layer,
                      lambda acc: (acc, jnp.square(jnp.maximum(acc, 0.0))), (), (BF16, BF16))
        h2, *nexts = fwd_mm(f"mlp_down_fwd{layer}", a, 'mlp_down', layer, _residual_then_norms(1),
                            (h, *[gain(g) for g in next_gains]), (F32,) + (BF16,) * len(next_gains))
        return h2, nexts, (n2, u, a)

    h = x
    saved = []
    n1 = rms_fwd("a_norm_fwd0", h, small['a_norm'][0])
    for l in range(n_a):
        (qkvt,) = fwd_mm(f"a_qkv_fwd{l}", n1, 'a_wqkv', l, plain, (), (BF16,), out_t=True)
        qkvt = qkvt.reshape(3 * d // HEAD_DIM, HEAD_DIM, s)
        o_t, rtab, carried = sb_fwd(f"sb_fwd{l}", qkvt, ex.fwd_carry(l))
        ex.fwd_done(l, carried)
        o_t = o_t.reshape(d, s)
        h_mid, n2 = fwd_mm(f"a_wo_fwd{l}", o_t, 'a_wo', l, _residual_then_norms(1),
                           (h, gain(small['mlp_norm'][l])), (F32, BF16), a_t=True)
        next_gains = [small['a_norm'][l + 1]] if l + 1 < n_a else [small['b_norm'][0], small['kv_norm']]
        h_out, nexts, mlp_saved = mlp_fwd(h_mid, n2, l, next_gains)
        saved.append((h, n1, qkvt, o_t, rtab, h_mid, mlp_saved))
        h, n1 = h_out, nexts[0]
    h_kv, nkv = h, nexts[1]
    (kvt,) = fwd_mm("kv_fwd", nkv, 'w_kv', 0, plus_col, (small['b_kv'].reshape(-1, 1),), (BF16,), out_t=True)
    kvt = kvt.reshape(2, 2, HEAD_DIM, s)
    kpt, vpt = (jnp.pad(t, ((0, 0), (0, 0), (WINDOW, 0))) for t in (kvt[0], kvt[1]))
    for j in range(n_b):
        layer = n_a + j
        (qbt,) = fwd_mm(f"b_q_fwd{j}", n1, 'b_wq', j, plus_col, (small['b_bq'][j].reshape(-1, 1),), (BF16,),
                        out_t=True)
        o_t = swa_fwd(f"swa_fwd{j}", qbt, kpt, vpt, bias_t, sink_rows[j])
        h_mid, n2 = fwd_mm(f"b_wo_fwd{j}", o_t, 'b_wo', j, _residual_then_norms(2),
                           (h, gain(small['b_bo'][j]), gain(small['mlp_norm'][layer])), (F32, BF16), a_t=True)
        h_out, nexts, mlp_saved = mlp_fwd(h_mid, n2, layer, [small['b_norm'][j + 1]] if j + 1 < n_b else [])
        saved.append((h, n1, qbt, o_t, h_mid, mlp_saved))
        h, n1 = h_out, (nexts[0] if nexts else None)

    dh, dhb, dg_final, loss_b = loss_head(h, small['final_norm'], target)
    sg['final_norm'] = dg_final[0]
    sg['mlp_norm'] = [None] * (n_a + n_b)

    def mlp_bwd(dh, dhb, h_mid, mlp_saved, layer):
        n2, u, a = mlp_saved
        (du,) = dx_mm(f"mlp_down_dx{layer}", dhb, 'mlp_down', layer,
                      lambda acc, uu: (acc * (2.0 * jnp.maximum(uu.astype(F32), 0.0)),), (u,), (BF16,))
        dw_mm(f"mlp_down_dw{layer}", a, dhb, 'mlp_down', layer)
        dh2, dh2b, dg, cs = dx_mm(f"mlp_up_dx{layer}", du, 'mlp_up', layer, _rms_bwd_epilogue,
                                  (h_mid, dh, gain(small['mlp_norm'][layer])), (F32, BF16), n_sums=2)
        dw_mm(f"mlp_up_dw{layer}", n2, du, 'mlp_up', layer)
        sg['mlp_norm'][layer] = dg[0]
        return dh2, dh2b, cs

    dkp = jnp.zeros(kpt.shape, F32)
    dvp = jnp.zeros(vpt.shape, F32)
    sg['b_norm'], sg['b_bq'], sg['b_bo'], sg['b_sinks'] = [None] * n_b, [None] * n_b, [None] * n_b, [None] * n_b
    dbias = [None] * n_b
    for j in reversed(range(n_b)):
        layer = n_a + j
        h_in, n1, qbt, o_t, h_mid, mlp_saved = saved[layer]
        dh, dhb, cs = mlp_bwd(dh, dhb, h_mid, mlp_saved, layer)
        sg['b_bo'][j] = cs[0]
        (do_t,) = dx_mm(f"b_wo_dx{j}", dhb, 'b_wo', j, plain, (), (BF16,), out_t=True)
        dw_mm(f"b_wo_dw{j}", o_t, dhb, 'b_wo', j, x_t=True)
        dq_t, dkp, dvp, dbias[j], dsink = swa_bwd(f"swa_bwd{j}", qbt, kpt, vpt, bias_t, sink_rows[j], do_t, dkp, dvp)
        sg['b_sinks'][j] = colsum(f"sink_grad{j}", dsink.reshape(16, WINDOW).T)[0]
        sg['b_bq'][j] = rowsum(f"b_bq_grad{j}", dq_t)
        dh, dhb, dg, _ = dx_mm(f"b_q_dx{j}", dq_t, 'b_wq', j, _rms_bwd_epilogue,
                               (h_in, dh, gain(small['b_norm'][j])), (F32, BF16), a_t=True, n_sums=2)
        dw_mm(f"b_q_dw{j}", n1, dq_t, 'b_wq', j, dy_t=True)
        sg['b_norm'][j] = dg[0]
    unt = lambda t: t.reshape(2, 2 * WINDOW, 8, WINDOW).transpose(0, 2, 3, 1).reshape(bias_flat.shape)
    sg['rel_bias'] = bias_table_grad(unt(dbias[0]), unt(dbias[1]))[:, :N_BUCKETS].T

    dkv_t = jnp.concatenate([dkp[:, :, WINDOW:], dvp[:, :, WINDOW:]], axis=0).reshape(-1, s)
    sg['b_kv'] = rowsum("b_kv_grad", dkv_t)
    dkvb = dkv_t.astype(BF16)
    dh, dhb, dg, _ = dx_mm("kv_dx", dkvb, 'w_kv', 0, _rms_bwd_epilogue, (h_kv, dh, gain(small['kv_norm'])),
                           (F32, BF16), a_t=True, n_sums=2)
    dw_mm("kv_dw", nkv, dkvb, 'w_kv', 0, dy_t=True)
    sg['kv_norm'] = dg[0]

    sg['a_norm'] = [None] * n_a
    for l in reversed(range(n_a)):
        h_in, n1, qkvt, o_t, rtab, h_mid, mlp_saved = saved[l]
        dh, dhb, _ = mlp_bwd(dh, dhb, h_mid, mlp_saved, l)
        (do_t,) = dx_mm(f"a_wo_dx{l}", dhb, 'a_wo', l, plain, (), (BF16,), out_t=True)
        dw_mm(f"a_wo_dw{l}", o_t, dhb, 'a_wo', l, x_t=True)
        dqkv_t, carried = sb_bwd(f"sb_bwd{l}", qkvt, do_t.reshape(d // HEAD_DIM, HEAD_DIM, s), rtab,
                                 ex.bwd_carry(l, gb))
        ex.bwd_done(l, carried)
        dqkv_t = dqkv_t.reshape(3 * d, s)
        dw_mm(f"a_qkv_dw{l}", n1, dqkv_t, 'a_wqkv', l, dy_t=True)
        dh, dhb, dg, _, *carried = dx_mm(f"a_qkv_dx{l}", dqkv_t, 'a_wqkv', l, _rms_bwd_epilogue,
                                         (h_in, dh, gain(small['a_norm'][l])), (F32, BF16), a_t=True, n_sums=2,
                                         exchange=ex.last_carry(gb) if l == 0 else None)
        if l == 0:
            ex.last_done(carried)
        sg['a_norm'][l] = dg[0]

    small_grads = {
        'a_norm': jnp.stack(sg['a_norm']), 'kv_norm': sg['kv_norm'], 'b_kv': sg['b_kv'],
        'b_norm': jnp.stack(sg['b_norm']), 'b_bq': jnp.stack(sg['b_bq']), 'b_sinks': jnp.stack(sg['b_sinks']),
        'b_bo': jnp.stack(sg['b_bo']), 'rel_bias': sg['rel_bias'], 'mlp_norm': jnp.stack(sg['mlp_norm']),
        'final_norm': sg['final_norm'],
    }
    return loss_b, dh, gb, small_grads


def _full_shape(name, shard_shape):
    if name in COL_SHARDED:
        return shard_shape[:2] + (N_DEV * shard_shape[2],)
    nl, r, n = shard_shape
    return (nl, N_DEV, r, n)


def _as_w3_shape(name, shard_shape):
    full = _full_shape(name, shard_shape)
    return full if name in COL_SHARDED else (full[0], full[1] * full[2], full[3])


def _as_w3(name, full):
    if name in COL_SHARDED:
        return full
    nl, nd, r, n = full.shape
    return full.reshape(nl, nd * r, n)


AG_GROUPS = {
    0: (('a_wqkv', 0, 1),),
    1: (('a_wo', 0, 2), ('mlp_up', 0, 2), ('mlp_down', 0, 2), ('a_wqkv', 1, 1)),
    2: (('mlp_up', 2, 2), ('mlp_down', 2, 2), ('b_wq', 0, 2), ('b_wo', 0, 2), ('w_kv', 0, 1)),
}
RS_GROUPS = {
    'A': (('mlp_up', 1, 3), ('mlp_down', 1, 3), ('b_wq', 0, 2), ('b_wo', 0, 2), ('w_kv', 0, 1)),
    'B': (('a_wqkv', 1, 1), ('a_wo', 0, 2), ('mlp_up', 0, 1), ('mlp_down', 0, 1)),
    'C': (('a_wqkv', 0, 1),),
}


class _Exchanges:
    def __init__(self, full0, shards, core, chip, w3, m3, v3):
        self.wbuf = {0: {n: _as_w3(n, full0[n]) for n, _, _ in AG_GROUPS[0]}}
        self.shards, self.core, self.chip = shards, core, chip
        self.w3, self.m3, self.v3 = w3, m3, v3
        self.shard_dims = {n: w3[n].shape[1:] for n in BIG}
        self.parts = {}
        self.out = {}

    def weight(self, name, layer):
        for group, members in AG_GROUPS.items():
            for n, l0, nl in members:
                if n == name and l0 <= layer < l0 + nl:
                    return self.wbuf[group][name], layer - l0
        raise KeyError((name, layer))

    def fwd_carry(self, layer):
        names = [n for n, _, _ in AG_GROUPS[layer + 1]]
        shards = [self.shards[layer + 1][n] for n in names]
        return ag_direct(names, shards, [_full_shape(n, sh.shape) for n, sh in zip(names, shards)])

    def fwd_done(self, layer, carried):
        names = [n for n, _, _ in AG_GROUPS[layer + 1]]
        self.wbuf[layer + 1] = {n: _as_w3(n, f) for n, f in zip(names, carried)}

    def grad(self, name, layer):
        for group, members in RS_GROUPS.items():
            for n, l0, nl in members:
                if n == name and l0 <= layer < l0 + nl:
                    return (group, name), layer - l0, _as_w3_shape(name, (nl,) + self.shard_dims[name])
        raise KeyError((name, layer))

    def _sibling_stage(self, group, gb):
        names = [n for n, _, _ in RS_GROUPS[group]]
        shapes = [(nl,) + self.shard_dims[n] for n, _, nl in RS_GROUPS[group]]
        gfull = [gb[(group, n)].reshape(_full_shape(n, sh)) for n, sh in zip(names, shapes)]
        recv = sibling_exchange(group, names, gfull, shapes)
        self.parts[group] = [sibling_sum(f"rs_sibling_sum_{group}_{n}", n in COL_SHARDED, g, r, self.core)
                             for n, g, r in zip(names, gfull, recv)]
        return names, shapes

    def bwd_carry(self, layer, gb):
        names, shapes = self._sibling_stage('A' if layer == 1 else 'B', gb)
        return chip_exchange(names, self.parts['A' if layer == 1 else 'B'], shapes)

    def bwd_done(self, layer, carried):
        self._adamw('A' if layer == 1 else 'B', carried)

    def last_carry(self, gb):
        names, shapes = self._sibling_stage('C', gb)
        return chip_exchange(names, self.parts['C'], shapes)

    def last_done(self, carried):
        self._adamw('C', carried)

    def _adamw(self, group, recv2):
        for (n, l0, _), p, r in zip(RS_GROUPS[group], self.parts[group], recv2):
            self.out[n] = reduce_adamw(f"adamw_{group}_{n}", p, r, self.chip, self.w3[n], self.m3[n], self.v3[n],
                                       l0, self.out.get(n))


def _pack_small(vals):
    flat = jnp.concatenate([vals[n].reshape(-1).astype(F32) for n in SMALL] + [vals['loss'].reshape(-1)])
    rows = -(-flat.shape[0] // 1024) * 8
    return jnp.pad(flat, (0, rows * 128 - flat.shape[0])).reshape(rows, 128)


def _unpack_small(packed, shapes):
    flat = packed.reshape(-1)
    out, off = {}, 0
    for n in SMALL + ['loss']:
        size = int(np.prod(shapes[n]))
        out[n] = flat[off:off + size].reshape(shapes[n])
        off += size
    return out


def kernel(x, a_norm, a_wqkv, a_wo, kv_norm, w_kv, b_kv, b_norm, b_wq, b_bq, b_sinks, b_wo, b_bo, rel_bias, mlp_norm, mlp_up, mlp_down, final_norm, loss_target, m_a_norm, m_a_wqkv, m_a_wo, m_kv_norm, m_w_kv, m_b_kv, m_b_norm, m_b_wq, m_b_bq, m_b_sinks, m_b_wo, m_b_bo, m_rel_bias, m_mlp_norm, m_mlp_up, m_mlp_down, m_final_norm, v_a_norm, v_a_wqkv, v_a_wo, v_kv_norm, v_w_kv, v_b_kv, v_b_norm, v_b_wq, v_b_bq, v_b_sinks, v_b_wo, v_b_bo, v_rel_bias, v_mlp_norm, v_mlp_up, v_mlp_down, v_final_norm):
    w = dict(a_norm=a_norm, a_wqkv=a_wqkv, a_wo=a_wo, kv_norm=kv_norm, w_kv=w_kv, b_kv=b_kv, b_norm=b_norm,
             b_wq=b_wq, b_bq=b_bq, b_sinks=b_sinks, b_wo=b_wo, b_bo=b_bo, rel_bias=rel_bias, mlp_norm=mlp_norm,
             mlp_up=mlp_up, mlp_down=mlp_down, final_norm=final_norm)
    m = dict(a_norm=m_a_norm, a_wqkv=m_a_wqkv, a_wo=m_a_wo, kv_norm=m_kv_norm, w_kv=m_w_kv, b_kv=m_b_kv,
             b_norm=m_b_norm, b_wq=m_b_wq, b_bq=m_b_bq, b_sinks=m_b_sinks, b_wo=m_b_wo, b_bo=m_b_bo,
             rel_bias=m_rel_bias, mlp_norm=m_mlp_norm, mlp_up=m_mlp_up, mlp_down=m_mlp_down, final_norm=m_final_norm)
    v = dict(a_norm=v_a_norm, a_wqkv=v_a_wqkv, a_wo=v_a_wo, kv_norm=v_kv_norm, w_kv=v_w_kv, b_kv=v_b_kv,
             b_norm=v_b_norm, b_wq=v_b_wq, b_bq=v_b_bq, b_sinks=v_b_sinks, b_wo=v_b_wo, b_bo=v_b_bo,
             rel_bias=v_rel_bias, mlp_norm=v_mlp_norm, mlp_up=v_mlp_up, mlp_down=v_mlp_down, final_norm=v_final_norm)
    px, py, pc = _place()
    me = 4 * px + 2 * py + pc
    chip = (2 * px + py).astype(jnp.int32)
    core = pc.astype(jnp.int32)

    as3 = lambda t: t[None] if t.ndim == 2 else t
    w3, m3, v3 = ({n: as3(src[n]) for n in BIG} for src in (w, m, v))
    shards = {g: {n: w3[n][l0:l0 + nl].astype(BF16) for n, l0, nl in members} for g, members in AG_GROUPS.items()}
    an_pad = jnp.zeros((8, 128), F32).at[:a_norm.shape[0]].set(a_norm)
    names0 = [n for n, _, _ in AG_GROUPS[0]]
    full0 = all_gather_weights(names0 + ['a_norm'], [shards[0][n] for n in names0] + [an_pad],
                               [_full_shape(n, shards[0][n].shape) for n in names0] + [(N_DEV, 8, 128)])
    full0 = dict(zip(names0 + ['a_norm'], full0))
    n_a = a_norm.shape[0]
    small = {n: w[n] for n in SMALL}
    small['a_norm'] = full0['a_norm'][:, :n_a].transpose(1, 0, 2).reshape(n_a, -1)

    ex = _Exchanges(full0, shards, core, chip, w3, m3, v3)
    loss_b, grad_x, gb, sgrads = local_step(x[0], loss_target[0], small, ex)
    out = {n: [t.reshape(w[n].shape) for t in bufs] for n, bufs in ex.out.items()}

    sgrads['loss'] = loss_b[0, :1]
    gathered = all_gather_rows(_pack_small(sgrads))
    shapes = {n: w[n].shape for n in SMALL}
    shapes['a_norm'] = (n_a, a_norm.shape[1] * N_DEV)
    shapes['loss'] = (1,)
    zeros1 = jnp.zeros((1,), F32)

    def packed(src):
        vals = {n: src[n] for n in SMALL}
        vals['a_norm'] = jnp.zeros(shapes['a_norm'], F32)
        vals['loss'] = zeros1
        return _pack_small(vals)

    sm = small_adamw("adamw_small", gathered, packed(w), packed(m), packed(v))
    sm = [_unpack_small(t, shapes) for t in sm]
    g_an = lax.dynamic_slice_in_dim(sm[0]['a_norm'], me * a_norm.shape[1], a_norm.shape[1], axis=1)
    pad = lambda t: jnp.zeros((8, 128), F32).at[:n_a].set(t)
    gathered_an = jnp.zeros((N_DEV, 8, 128), F32).at[0].set(pad(g_an))
    an = small_adamw("adamw_a_norm", gathered_an, pad(a_norm), pad(m_a_norm), pad(v_a_norm))
    for i in range(4):
        sm[i]['a_norm'] = an[i][:n_a]
    for n in BIG:
        for i in range(4):
            sm[i][n] = out[n][i]
    loss = sm[0]['loss'][0]
    return (loss, grad_x[None], *[sm[0][n] for n in WEIGHTS], *[sm[1][n] for n in WEIGHTS],
            *[sm[2][n] for n in WEIGHTS], *[sm[3][n] for n in WEIGHTS])
```

```python
import math

import numpy as np
import jax
import jax.numpy as jnp
from jax import lax
from jax.experimental import pallas as pl
from jax.experimental.pallas import tpu as pltpu

F32 = jnp.float32
BF16 = jnp.bfloat16
MESH = pl.DeviceIdType.MESH

N_DEV = 8
HEAD_DIM = 64
WINDOW = 128
N_BUCKETS = 32
EPS = 1e-5
NEG_INF = -1e30
Q_SCALE = 1.0 / math.sqrt(HEAD_DIM)
LOG2E = 1.4426950408889634

ADAM_LR, ADAM_B1, ADAM_B2, ADAM_EPS, ADAM_WD, ADAM_STEP = 0.001, 0.9, 0.999, 1e-08, 0.01, 10

SB_BQ = 512
SB_BK = 128
SB_DEAD = 160.0
SB_UNSEEN = 1e30
ROW_TILE = 512
VMEM_LIMIT = 56 * 1024 * 1024

WEIGHTS = ['a_norm', 'a_wqkv', 'a_wo', 'kv_norm', 'w_kv', 'b_kv', 'b_norm', 'b_wq', 'b_bq', 'b_sinks', 'b_wo',
           'b_bo', 'rel_bias', 'mlp_norm', 'mlp_up', 'mlp_down', 'final_norm']
BIG = ['a_wqkv', 'a_wo', 'w_kv', 'b_wq', 'b_wo', 'mlp_up', 'mlp_down']
COL_SHARDED = ('a_wqkv', 'mlp_up')
SMALL = ['a_norm', 'kv_norm', 'b_kv', 'b_norm', 'b_bq', 'b_sinks', 'b_bo', 'rel_bias', 'mlp_norm', 'final_norm']


def _params(sem=None):
    return pltpu.CompilerParams(dimension_semantics=sem, vmem_limit_bytes=VMEM_LIMIT)


def _pick(n, cands):
    for c in cands:
        if n % c == 0:
            return c
    raise ValueError(n)


def _tile(n, want):
    return n if n <= want else _pick(n, (want, want // 2, want // 4))


MM_TILE_BUDGET = 36 * 1024 * 1024


def _row_tile(m, contraction, cols, streams):
    weight = 2 * contraction * cols * 2
    for rows in (2048, 1024, 512):
        if m % rows == 0 and weight + 2 * rows * (2 * contraction + cols * sum(streams)) <= MM_TILE_BUDGET:
            return rows
    return _tile(m, 512)


def mm_nn(name, a, w3, layer, epilogue, extras, out_dtypes, a_t=False, out_t=False):
    k, m = a.shape if a_t else a.shape[::-1]
    _, kw, n = w3.shape
    assert kw == k
    tn = _tile(n, 1024)
    tm = _row_tile(m, k, tn, [jnp.dtype(t).itemsize for t in out_dtypes]
                   + [e.dtype.itemsize for e in extras if e.size == m * n])
    ne, no = len(extras), len(out_dtypes)
    a_dim = 0 if a_t else 1

    def body(a_ref, w_ref, *rest):
        ex, outs = rest[:ne], rest[ne:ne + no]
        if out_t:
            acc = lax.dot_general(w_ref[...], a_ref[...], (((0,), (a_dim,)), ((), ())), preferred_element_type=F32)
        else:
            acc = lax.dot_general(a_ref[...], w_ref[...], (((a_dim,), (0,)), ((), ())), preferred_element_type=F32)
        for o, r in zip(outs, epilogue(acc, *[e[...] for e in ex])):
            o[...] = r.astype(o.dtype)

    if out_t:
        tile = pl.BlockSpec((tn, tm), lambda i, j: (j, i))
        vec = pl.BlockSpec((tn, 1), lambda i, j: (j, 0))
        out_shape = (n, m)
    else:
        tile = pl.BlockSpec((tm, tn), lambda i, j: (i, j))
        vec = pl.BlockSpec((1, tn), lambda i, j: (0, j))
        out_shape = (m, n)
    a_spec = pl.BlockSpec((k, tm), lambda i, j: (0, i)) if a_t else pl.BlockSpec((tm, k), lambda i, j: (i, 0))
    return pl.pallas_call(
        body, name=name, grid=(m // tm, n // tn),
        in_specs=[a_spec, pl.BlockSpec((None, k, tn), lambda i, j: (layer, 0, j))]
        + [tile if e.shape == out_shape else vec for e in extras],
        out_specs=[tile] * no,
        out_shape=[jax.ShapeDtypeStruct(out_shape, d) for d in out_dtypes],
        compiler_params=_params(("parallel", "parallel")),
    )(a, w3, *extras)


def mm_nt(name, dy, w3, layer, epilogue, extras, out_dtypes, a_t=False, out_t=False, n_sums=0, exchange=None):
    n, m = dy.shape if a_t else dy.shape[::-1]
    _, k, nw = w3.shape
    assert nw == n and not (out_t and n_sums)
    tko = _tile(k, 1024)
    tm = _row_tile(m, n, tko, [jnp.dtype(t).itemsize for t in out_dtypes]
                   + [e.dtype.itemsize for e in extras if e.size == m * k])
    ne, no = len(extras), len(out_dtypes)
    a_dim = 0 if a_t else 1

    def body(a_ref, w_ref, *rest):
        ex = rest[:ne]
        at = lambda step: (pl.program_id(0) == step[0]) & (pl.program_id(1) == step[1])
        results, _, start_carried, wait_carried = _carried(exchange, rest[ne:], no + n_sums, 0, at((0, 0)),
                                                           at((m // tm - 1, k // tko - 1)))
        outs, sums = results[:no], results[no:]
        start_carried()
        if out_t:
            acc = lax.dot_general(w_ref[...], a_ref[...], (((1,), (a_dim,)), ((), ())), preferred_element_type=F32)
        else:
            acc = lax.dot_general(a_ref[...], w_ref[...], (((a_dim,), (1,)), ((), ())), preferred_element_type=F32)
        res = epilogue(acc, *[e[...] for e in ex])
        for o, v in zip(outs, res):
            o[...] = v.astype(o.dtype)
        if n_sums:
            @pl.when(pl.program_id(0) == 0)
            def _():
                for o in sums:
                    o[...] = jnp.zeros_like(o)

            for o, v in zip(sums, res[no:]):
                o[...] += v
        wait_carried()

    if out_t:
        tile = pl.BlockSpec((tko, tm), lambda i, ko: (ko, i))
        out_shape = (k, m)
    else:
        tile = pl.BlockSpec((tm, tko), lambda i, ko: (i, ko))
        out_shape = (m, k)
    vec = pl.BlockSpec((1, tko), lambda i, ko: (0, ko))
    a_spec = pl.BlockSpec((n, tm), lambda i, ko: (0, i)) if a_t else pl.BlockSpec((tm, n), lambda i, ko: (i, 0))
    hbm = pl.BlockSpec(memory_space=pl.ANY)
    c_ins, c_outs, c_sems = (exchange.ins, exchange.out_shapes, exchange.sems) if exchange else ([], [], [])
    sequential = n_sums or exchange
    return pl.pallas_call(
        body, name=name, grid=(m // tm, k // tko),
        in_specs=[a_spec, pl.BlockSpec((None, tko, n), lambda i, ko: (layer, ko, 0))]
        + [tile if e.shape == out_shape else vec for e in extras] + [hbm] * len(c_ins),
        out_specs=[tile] * no + [vec] * n_sums + [hbm] * len(c_outs),
        out_shape=[jax.ShapeDtypeStruct(out_shape, d) for d in out_dtypes] + [jax.ShapeDtypeStruct((1, k), F32)] * n_sums
        + c_outs,
        scratch_shapes=c_sems,
        compiler_params=_params(("arbitrary" if sequential else "parallel", "arbitrary" if exchange else "parallel")),
    )(dy, w3, *extras, *c_ins)


def mm_tn(name, x, dy, gbuf, shape, layer, x_t=False, dy_t=False):
    k, s = x.shape if x_t else x.shape[::-1]
    _, kw, n = shape
    assert kw == k and dy.shape == ((n, s) if dy_t else (s, n))
    tkk = _tile(k, 512)
    tn = _tile(n, 1024)

    def body(x_ref, dy_ref, *rest):
        g_out = rest[-1]
        g_out[...] = lax.dot_general(x_ref[...], dy_ref[...], (((1 if x_t else 0,), (1 if dy_t else 0,)), ((), ())),
                                     preferred_element_type=F32).astype(g_out.dtype)

    prev = [] if gbuf is None else [gbuf]
    x_spec = pl.BlockSpec((tkk, s), lambda ki, j: (ki, 0)) if x_t else pl.BlockSpec((s, tkk), lambda ki, j: (0, ki))
    dy_spec = pl.BlockSpec((tn, s), lambda ki, j: (j, 0)) if dy_t else pl.BlockSpec((s, tn), lambda ki, j: (0, j))
    return pl.pallas_call(
        body, name=name, grid=(k // tkk, n // tn),
        in_specs=[x_spec, dy_spec] + [pl.BlockSpec(memory_space=pl.ANY)] * len(prev),
        out_specs=pl.BlockSpec((None, tkk, tn), lambda ki, j: (layer, ki, j)),
        out_shape=jax.ShapeDtypeStruct(shape, BF16),
        input_output_aliases={2: 0} if prev else {},
        compiler_params=_params(("parallel", "parallel")),
    )(x, dy, *prev)


def rms_fwd(name, h, g):
    s, d = h.shape
    tr = _pick(s, (ROW_TILE, 256, 128))

    def body(h_ref, g_ref, o_ref):
        x = h_ref[...]
        r = lax.rsqrt(jnp.mean(x * x, axis=-1, keepdims=True) + EPS)
        o_ref[...] = (x * r * g_ref[...]).astype(o_ref.dtype)

    return pl.pallas_call(
        body, name=name, grid=(s // tr,),
        in_specs=[pl.BlockSpec((tr, d), lambda i: (i, 0)), pl.BlockSpec((1, d), lambda i: (0, 0))],
        out_specs=pl.BlockSpec((tr, d), lambda i: (i, 0)),
        out_shape=jax.ShapeDtypeStruct((s, d), BF16),
        compiler_params=_params(("parallel",)),
    )(h, g.reshape(1, d))


def loss_head(h, g, target):
    s, d = h.shape
    tr = _pick(s, (ROW_TILE, 256, 128))

    def body(h_ref, g_ref, t_ref, dx_ref, dxb_ref, dg_ref, loss_ref):
        i = pl.program_id(0)
        x = h_ref[...]
        r = lax.rsqrt(jnp.mean(x * x, axis=-1, keepdims=True) + EPS)
        xh = x * r
        gw = g_ref[...]
        err = xh * gw - t_ref[...]
        dn_ = err * (1.0 / d)
        dyg = dn_ * gw
        dx = r * (dyg - xh * jnp.mean(dyg * xh, axis=-1, keepdims=True))
        dx_ref[...] = dx
        dxb_ref[...] = dx.astype(BF16)

        @pl.when(i == 0)
        def _():
            dg_ref[...] = jnp.zeros_like(dg_ref)
            loss_ref[...] = jnp.zeros_like(loss_ref)

        dg_ref[...] += jnp.sum(dn_ * xh, axis=0, keepdims=True)
        per_row = jnp.sum(err * err, axis=-1, keepdims=True) * (0.5 / d)
        loss_ref[...] += jnp.broadcast_to(jnp.sum(per_row, axis=0, keepdims=True), loss_ref.shape)

    row = pl.BlockSpec((tr, d), lambda i: (i, 0))
    vec = pl.BlockSpec((1, d), lambda i: (0, 0))
    return pl.pallas_call(
        body, name="loss_head", grid=(s // tr,),
        in_specs=[row, vec, row],
        out_specs=[row, row, vec, pl.BlockSpec((1, 128), lambda i: (0, 0))],
        out_shape=[jax.ShapeDtypeStruct((s, d), F32), jax.ShapeDtypeStruct((s, d), BF16),
                   jax.ShapeDtypeStruct((1, d), F32), jax.ShapeDtypeStruct((1, 128), F32)],
        compiler_params=_params(("arbitrary",)),
    )(h, g.reshape(1, d), target)


def colsum(name, x):
    s, n = x.shape
    tr = _pick(s, (ROW_TILE, 256, 128))

    def body(x_ref, o_ref):
        @pl.when(pl.program_id(0) == 0)
        def _():
            o_ref[...] = jnp.zeros_like(o_ref)

        o_ref[...] += jnp.sum(x_ref[...].astype(F32), axis=0, keepdims=True)

    return pl.pallas_call(
        body, name=name, grid=(s // tr,),
        in_specs=[pl.BlockSpec((tr, n), lambda i: (i, 0))],
        out_specs=pl.BlockSpec((1, n), lambda i: (0, 0)),
        out_shape=jax.ShapeDtypeStruct((1, n), F32),
        compiler_params=_params(("arbitrary",)),
    )(x)


def rowsum(name, x):
    n, s = x.shape
    ts = _pick(s, (1024, 512, 256, 128))

    def body(x_ref, o_ref):
        @pl.when(pl.program_id(0) == 0)
        def _():
            o_ref[...] = jnp.zeros_like(o_ref)

        o_ref[...] += jnp.sum(x_ref[...].astype(F32), axis=1, keepdims=True)

    return pl.pallas_call(
        body, name=name, grid=(s // ts,),
        in_specs=[pl.BlockSpec((n, ts), lambda i: (0, i))],
        out_specs=pl.BlockSpec((n, 1), lambda i: (0, 0)),
        out_shape=jax.ShapeDtypeStruct((n, 1), F32),
        compiler_params=_params(("arbitrary",)),
    )(x)[:, 0]


def _tri_rows(reverse):
    i = np.arange(SB_BK)
    tri = (i[None, :] >= i[:, None]) if reverse else (i[None, :] <= i[:, None])
    tri = np.concatenate([tri, tri], axis=1)
    return jnp.asarray(np.concatenate([tri, np.ones((8, 2 * SB_BK), bool)], axis=0), BF16)


def _hi_lo_rows(x):
    hi = x.astype(BF16)
    lo = (x - hi.astype(F32)).astype(BF16)
    return jnp.concatenate([hi, lo], axis=0)


def _softplus2(zs):
    neg_abs = lax.bitcast_convert_type(lax.bitcast_convert_type(zs, jnp.uint32) | jnp.uint32(0x80000000), F32)
    return jnp.maximum(zs, 0.0) + jnp.log2(1.0 + jnp.exp2(neg_abs))


def _pair_mask(first_rel_block, bq):
    key = lax.broadcasted_iota(jnp.int32, (2 * SB_BK, bq), 0) + first_rel_block * SB_BK
    qry = lax.broadcasted_iota(jnp.int32, (2 * SB_BK, bq), 1)
    return key < qry


def _row_of(table8, sub8, r):
    return jnp.sum(jnp.where(sub8 == r, table8, 0.0), axis=0, keepdims=True)


def _keys(j0):
    return pl.ds(pl.multiple_of(j0 * SB_BK, 2 * SB_BK), 2 * SB_BK)


class Carry:
    def __init__(self, build, ins, out_shapes, sems, then=None):
        self.build, self.ins, self.out_shapes, self.sems = build, list(ins), list(out_shapes), list(sems)
        self.then = then


def _carried(carry, rest, n_out, n_scratch, first, last):
    n_ci = len(carry.ins) if carry else 0
    n_co = len(carry.out_shapes) if carry else 0
    cin, outs = rest[:n_ci], rest[n_ci:n_ci + n_out]
    cout = rest[n_ci + n_out:n_ci + n_out + n_co]
    scratch = rest[n_ci + n_out + n_co:n_ci + n_out + n_co + n_scratch]
    csems = rest[n_ci + n_out + n_co + n_scratch:]

    def start():
        if carry:
            @pl.when(first)
            def _():
                for cp in carry.build(cin, cout, *csems):
                    cp.start()

    def wait():
        if carry:
            @pl.when(last)
            def _():
                for cp in carry.build(cin, cout, *csems):
                    cp.wait()
                if carry.then:
                    second = carry.then(cin, cout, *csems)
                    for cp in second:
                        cp.start()
                    for cp in second:
                        cp.wait()

    return outs, scratch, start, wait


def _contract0(a, b):
    return lax.dot_general(a, b, (((0,), (0,)), ((), ())), preferred_element_type=F32)


def _contract1(a, b):
    return lax.dot_general(a, b, (((1,), (1,)), ((), ())), preferred_element_type=F32)


def sb_fwd(name, qkvt, exchange=None):
    nh, dh, s = qkvt.shape[0] // 3, qkvt.shape[1], qkvt.shape[2]
    bq = SB_BQ
    per_q = bq // SB_BK
    nkb = s // SB_BK
    assert s % bq == 0 and per_q == 4 and nkb % 8 == 0

    def body(q_ref, k_ref, v_ref, a_ref, *rest):
        head = pl.program_id(0)
        (o_ref, rtab_ref), (acc, zbuf, wbuf), start_carried, wait_carried = _carried(
            exchange, rest, 2, 3, head == 0, head == nh - 1)
        start_carried()
        tri = a_ref[...]
        sub8 = lax.broadcasted_iota(jnp.int32, (8, bq), 0)
        rtab_ref[...] = jnp.full(rtab_ref.shape, SB_UNSEEN, F32)
        kf = k_ref[...].astype(F32)
        k_max2 = jnp.max(jnp.sum(kf * kf, axis=0, keepdims=True), axis=1, keepdims=True)

        def query_block(i, _):
            lanes = pl.ds(pl.multiple_of(i * bq, bq), bq)
            qb = q_ref[:, lanes] * Q_SCALE
            acc[...] = jnp.zeros_like(acc)
            qf = qb.astype(F32)
            bound = jnp.sqrt(jnp.sum(qf * qf, axis=0, keepdims=True) * k_max2) * (1.001 * LOG2E)

            def scores(j0):
                return _contract0(k_ref[:, _keys(j0)], qb) * LOG2E

            def pair(j0, slot, run, rt8, mask, has_prev):
                zs = zbuf[slot]
                zbuf[1 - slot] = scores(jnp.maximum(j0 - 2, 0))
                if has_prev:
                    acc[...] += jnp.dot(v_ref[:, _keys(j0 + 2)], wbuf[1 - slot], preferred_element_type=F32)
                p = _softplus2(zs)
                if mask is not None:
                    p = jnp.where(mask, p, 0.0)
                cr1 = jnp.dot(tri, _hi_lo_rows(p[SB_BK:]), preferred_element_type=F32)
                cr0 = jnp.dot(tri, _hi_lo_rows(p[:SB_BK]), preferred_element_type=F32)
                run1 = run + cr1[SB_BK:SB_BK + 1]
                w = jnp.exp2(jnp.concatenate([zs[:SB_BK] - cr0[:SB_BK] - run1, zs[SB_BK:] - cr1[:SB_BK] - run],
                                             axis=0))
                if mask is not None:
                    w = jnp.where(mask, w, 0.0)
                wbuf[slot] = w.astype(BF16)
                rt8 = jnp.where(j0 % 8 == 6, SB_UNSEEN, rt8)
                rt8 = jnp.where(sub8 == (j0 + 1) % 8, run, jnp.where(sub8 == j0 % 8, run1, rt8))
                rtab_ref[pl.ds(pl.multiple_of((j0 // 8) * 8, 8), 8), lanes] = rt8
                return run1 + cr0[SB_BK:SB_BK + 1], rt8

            def alive(run):
                return jnp.min(run - bound) < SB_DEAD

            top = i * per_q
            zbuf[0] = scores(top + 2)
            state = (jnp.zeros((1, bq), F32), jnp.full((8, bq), SB_UNSEEN, F32))
            state = pair(top + 2, 0, *state, _pair_mask(2, bq), False)
            state = pair(top, 1, *state, _pair_mask(0, bq), True)

            def step(c):
                it, pairs, _, run, rt8 = c
                j0 = top - 2 - 4 * it
                run, rt8 = pair(j0, 0, run, rt8, None, True)
                go = alive(run)
                run, rt8 = lax.cond(go, lambda r, t: pair(j0 - 2, 1, r, t, None, True), lambda r, t: (r, t), run, rt8)
                return it + 1, pairs + 1 + go.astype(jnp.int32), go & alive(run), run, rt8

            pairs = lax.while_loop(lambda c: (c[0] < i) & c[2], step, (0, 0, alive(state[0]), *state))[1]
            acc[...] += jnp.dot(v_ref[:, _keys(top - 2 * pairs)], wbuf[(pairs + 1) % 2], preferred_element_type=F32)
            o_ref[:, lanes] = acc[...].astype(o_ref.dtype)
            return 0

        lax.fori_loop(0, s // bq, query_block, 0)
        wait_carried()

    def head_spec(offset, rows):
        return pl.BlockSpec((None, rows, s), lambda h: (h + offset, 0, 0))

    hbm = pl.BlockSpec(memory_space=pl.ANY)
    c_ins, c_outs, c_sems = (exchange.ins, exchange.out_shapes, exchange.sems) if exchange else ([], [], [])
    outs = pl.pallas_call(
        body, name=name, grid=(nh,),
        in_specs=[head_spec(0, dh), head_spec(nh, dh), head_spec(2 * nh, dh),
                  pl.BlockSpec((SB_BK + 8, 2 * SB_BK), lambda h: (0, 0))] + [hbm] * len(c_ins),
        out_specs=[head_spec(0, dh), head_spec(0, nkb)] + [hbm] * len(c_outs),
        out_shape=[jax.ShapeDtypeStruct((nh, dh, s), BF16), jax.ShapeDtypeStruct((nh, nkb, s), F32)] + c_outs,
        scratch_shapes=[pltpu.VMEM((dh, bq), F32), pltpu.VMEM((2, 2 * SB_BK, bq), F32),
                        pltpu.VMEM((2, 2 * SB_BK, bq), BF16)] + c_sems,
        compiler_params=_params(("arbitrary",)),
    )(qkvt, qkvt, qkvt, _tri_rows(True), *c_ins)
    return outs[0], outs[1], outs[2:]


def sb_bwd(name, qkvt, dot_, rtab, exchange=None):
    nh, dh, s = qkvt.shape[0] // 3, qkvt.shape[1], qkvt.shape[2]
    bq = SB_BQ
    per_q = bq // SB_BK
    nkb = s // SB_BK

    def body(qt_ref, kt_ref, vt_ref, dot_ref, rtab_ref, ar_ref, af_ref, *rest):
        head = pl.program_id(0)
        (dqkv_ref,), (dq_acc, dk_acc, dv_acc, zbuf, dwbuf, dzbuf, wbuf), start_carried, wait_carried = \
            _carried(exchange, rest, 1, 7, head == 0, head == nh - 1)
        dq_ref, dk_ref, dv_ref = dqkv_ref.at[0], dqkv_ref.at[1], dqkv_ref.at[2]
        start_carried()
        dk_acc[...] = jnp.zeros_like(dk_acc)
        dv_acc[...] = jnp.zeros_like(dv_acc)
        tri_rev = ar_ref[...][:SB_BK]
        tri_fwd = af_ref[...]
        sub8 = lax.broadcasted_iota(jnp.int32, (8, bq), 0)

        def query_block(i, _):
            lanes = pl.ds(pl.multiple_of(i * bq, bq), bq)
            qtb = qt_ref[:, lanes] * Q_SCALE
            dotb = dot_ref[:, lanes]
            dq_acc[...] = jnp.zeros_like(dq_acc)
            last_j = i * per_q + 2
            seen = jnp.max(jnp.where(rtab_ref[:, lanes] < 0.1 * SB_UNSEEN, 1.0, 0.0), axis=1, keepdims=True)
            pairs = jnp.clip((jnp.sum(seen).astype(jnp.int32) - per_q) // 2, 0, 2 * i)
            odd = pairs % 2
            first_j = i * per_q - 2 * pairs

            def issue(j0, slot):
                zbuf[slot] = _contract0(kt_ref[:, _keys(j0)], qtb) * LOG2E
                dwbuf[slot] = _contract0(vt_ref[:, _keys(j0)], dotb)

            def retire(j0, slot):
                keys = _keys(j0)
                dq_acc[...] += jnp.dot(kt_ref[:, keys], dzbuf[slot], preferred_element_type=F32)
                dk_acc[:, keys] += _contract1(qtb, dzbuf[slot])
                dv_acc[:, keys] += _contract1(dotb, wbuf[slot])

            def pair(j0, slot, g_run, mask):
                zs = zbuf[slot]
                dw = dwbuf[slot]
                issue(jnp.minimum(j0 + 2, last_j), 1 - slot)
                retire(jnp.maximum(j0 - 2, first_j), 1 - slot)
                p_raw = _softplus2(zs)
                p = p_raw if mask is None else jnp.where(mask, p_raw, 0.0)
                c0 = jnp.dot(tri_rev, _hi_lo_rows(p[:SB_BK]), preferred_element_type=F32)
                c1 = jnp.dot(tri_rev, _hi_lo_rows(p[SB_BK:]), preferred_element_type=F32)
                rt8 = rtab_ref[pl.ds(pl.multiple_of((j0 // 8) * 8, 8), 8), lanes]
                r0 = _row_of(rt8, sub8, j0 % 8)
                r1 = _row_of(rt8, sub8, (j0 + 1) % 8)
                w = jnp.exp2(jnp.concatenate([zs[:SB_BK] - c0 - r0, zs[SB_BK:] - c1 - r1], axis=0))
                if mask is not None:
                    w = jnp.where(mask, w, 0.0)
                g = w * dw
                gg0 = jnp.dot(tri_fwd, _hi_lo_rows(g[:SB_BK]), preferred_element_type=F32)
                gg1 = jnp.dot(tri_fwd, _hi_lo_rows(g[SB_BK:]), preferred_element_type=F32)
                g_run1 = g_run + gg0[SB_BK:SB_BK + 1]
                g_pre = jnp.concatenate([gg0[:SB_BK] + g_run, gg1[:SB_BK] + g_run1], axis=0)
                dz = g - jnp.exp2(zs - p_raw) * g_pre
                if mask is not None:
                    dz = jnp.where(mask, dz, 0.0)
                dzbuf[slot] = dz.astype(BF16)
                wbuf[slot] = w.astype(BF16)
                return g_run1 + gg1[SB_BK:SB_BK + 1]

            issue(first_j, odd)
            dzbuf[...] = jnp.zeros(dzbuf.shape, BF16)
            wbuf[...] = jnp.zeros(wbuf.shape, BF16)

            def step(it, g_run):
                g_run = pair(4 * it, 0, g_run, None)
                return pair(4 * it + 2, 1, g_run, None)

            g_run = lax.cond(odd == 1, lambda g: pair(first_j, 1, g, None), lambda g: g, jnp.zeros((1, bq), F32))
            g_run = lax.fori_loop(i - pairs // 2, i, step, g_run)
            g_run = pair(last_j - 2, 0, g_run, _pair_mask(0, bq))
            pair(last_j, 1, g_run, _pair_mask(2, bq))
            retire(last_j, 1)
            dq_ref[:, lanes] = (dq_acc[...] * Q_SCALE).astype(dq_ref.dtype)
            return 0

        lax.fori_loop(0, s // bq, query_block, 0)
        dk_ref[...] = dk_acc[...].astype(dk_ref.dtype)
        dv_ref[...] = dv_acc[...].astype(dv_ref.dtype)
        wait_carried()

    def head_spec(offset, rows):
        return pl.BlockSpec((None, rows, s), lambda h: (h + offset, 0, 0))

    aspec = pl.BlockSpec((SB_BK + 8, 2 * SB_BK), lambda h: (0, 0))
    pair_f32 = pltpu.VMEM((2, 2 * SB_BK, bq), F32)
    pair_bf16 = pltpu.VMEM((2, 2 * SB_BK, bq), BF16)
    hbm = pl.BlockSpec(memory_space=pl.ANY)
    c_ins, c_outs, c_sems = (exchange.ins, exchange.out_shapes, exchange.sems) if exchange else ([], [], [])
    outs = pl.pallas_call(
        body, name=name, grid=(nh,),
        in_specs=[head_spec(0, dh), head_spec(nh, dh), head_spec(2 * nh, dh), head_spec(0, dh), head_spec(0, nkb),
                  aspec, aspec] + [hbm] * len(c_ins),
        out_specs=[pl.BlockSpec((3, None, dh, s), lambda h: (0, h, 0, 0))] + [hbm] * len(c_outs),
        out_shape=[jax.ShapeDtypeStruct((3, nh, dh, s), BF16)] + c_outs,
        scratch_shapes=[pltpu.VMEM((dh, bq), F32), pltpu.VMEM((dh, s), F32), pltpu.VMEM((dh, s), F32),
                        pair_f32, pair_f32, pair_bf16, pair_bf16] + c_sems,
        compiler_params=_params(("arbitrary",)),
    )(qkvt, qkvt, qkvt, dot_, rtab, _tri_rows(True), _tri_rows(False), *c_ins)
    return outs[0], outs[1:]


SWA_QB = 2


def _swa_probs(qt, kt, bias_t, sink, i):
    cols = qt.shape[1]
    sc = _contract0(kt, qt) + bias_t
    kj = lax.broadcasted_iota(jnp.int32, (2 * WINDOW, cols), 0)
    qi = lax.broadcasted_iota(jnp.int32, (2 * WINDOW, cols), 1) & (WINDOW - 1)
    dist = qi + WINDOW - kj
    valid = (dist >= 0) & (dist < WINDOW) & ((kj >= WINDOW) | (i > 0))
    sc = jnp.where(valid, sc, NEG_INF)
    mx = jnp.maximum(jnp.max(sc, axis=0, keepdims=True), sink)
    p = jnp.exp(sc - mx)
    p_sink = jnp.exp(sink - mx)
    inv = 1.0 / (jnp.sum(p, axis=0, keepdims=True) + p_sink)
    return p, p_sink, inv


def _band(i):
    return pl.ds(pl.multiple_of(i * WINDOW, WINDOW), 2 * WINDOW)


def _heads_to_lanes(blk):
    return jnp.concatenate([blk[r * HEAD_DIM:(r + 1) * HEAD_DIM] for r in range(8)], axis=1)


def _lanes_to_heads(t):
    return jnp.concatenate([t[:, r * WINDOW:(r + 1) * WINDOW] for r in range(8)], axis=0)


def swa_fwd(name, qt, kpt, vpt, bias_t, sink_row):
    d, s = qt.shape
    ng, dh, sp = kpt.shape
    rows, cols = d // ng, SWA_QB * WINDOW
    assert (s // WINDOW) % SWA_QB == 0

    def body(q_ref, k_ref, v_ref, bias_ref, sink_ref, o_ref):
        for u in range(SWA_QB):
            i = pl.program_id(1) * SWA_QB + u
            lanes = slice(u * WINDOW, (u + 1) * WINDOW)
            qb = _heads_to_lanes(q_ref[:, lanes]) * Q_SCALE
            p, _, inv = _swa_probs(qb, k_ref[:, _band(i)], bias_ref[...], sink_ref[...], i)
            o_t = jnp.dot(v_ref[:, _band(i)], p.astype(BF16), preferred_element_type=F32) * inv
            o_ref[:, lanes] = _lanes_to_heads(o_t).astype(o_ref.dtype)

    qspec = pl.BlockSpec((rows, cols), lambda g, i: (g, i))
    kspec = pl.BlockSpec((None, dh, sp), lambda g, i: (g, 0, 0))
    return pl.pallas_call(
        body, name=name, grid=(ng, s // cols),
        in_specs=[qspec, kspec, kspec, pl.BlockSpec((None, 2 * WINDOW, 8 * WINDOW), lambda g, i: (g, 0, 0)),
                  pl.BlockSpec((None, 1, 8 * WINDOW), lambda g, i: (g, 0, 0))],
        out_specs=qspec,
        out_shape=jax.ShapeDtypeStruct(qt.shape, BF16),
        compiler_params=_params(("parallel", "arbitrary")),
    )(qt, kpt, vpt, bias_t, sink_row)


def swa_bwd(name, qt, kpt, vpt, bias_t, sink_row, dot_, dk_in, dv_in):
    d, s = qt.shape
    ng, dh, sp = kpt.shape
    rows, cols = d // ng, SWA_QB * WINDOW

    def body(q_ref, k_ref, v_ref, bias_ref, sink_ref, do_ref, dki_ref, dvi_ref, dq_ref, dk_ref, dv_ref, db_ref, ds_ref):
        @pl.when(pl.program_id(1) == 0)
        def _():
            dk_ref[...] = dki_ref[...]
            dv_ref[...] = dvi_ref[...]
            db_ref[...] = jnp.zeros_like(db_ref)
            ds_ref[...] = jnp.zeros_like(ds_ref)

        for u in range(SWA_QB):
            i = pl.program_id(1) * SWA_QB + u
            band = _band(i)
            lanes = slice(u * WINDOW, (u + 1) * WINDOW)
            qb = _heads_to_lanes(q_ref[:, lanes]) * Q_SCALE
            dob = _heads_to_lanes(do_ref[:, lanes])
            kt = k_ref[:, band]
            p, p_sink, inv = _swa_probs(qb, kt, bias_ref[...], sink_ref[...], i)
            p = p * inv
            dp = _contract0(v_ref[:, band], dob)
            delta = jnp.sum(p * dp, axis=0, keepdims=True)
            dsc = p * (dp - delta)
            ds_ref[...] -= p_sink * inv * delta
            db_ref[...] += dsc
            dscb = dsc.astype(BF16)
            dq_t = jnp.dot(kt, dscb, preferred_element_type=F32) * Q_SCALE
            dq_ref[:, lanes] = _lanes_to_heads(dq_t).astype(dq_ref.dtype)
            dk_ref[:, band] += _contract1(qb, dscb)
            dv_ref[:, band] += _contract1(dob, p.astype(BF16))

    qspec = pl.BlockSpec((rows, cols), lambda g, i: (g, i))
    kspec = pl.BlockSpec((None, dh, sp), lambda g, i: (g, 0, 0))
    bspec = pl.BlockSpec((None, 2 * WINDOW, 8 * WINDOW), lambda g, i: (g, 0, 0))
    sspec = pl.BlockSpec((None, 1, 8 * WINDOW), lambda g, i: (g, 0, 0))
    return pl.pallas_call(
        body, name=name, grid=(ng, s // cols),
        in_specs=[qspec, kspec, kspec, bspec, sspec, qspec, kspec, kspec],
        out_specs=[qspec, kspec, kspec, bspec, sspec],
        out_shape=[jax.ShapeDtypeStruct(qt.shape, BF16), jax.ShapeDtypeStruct(kpt.shape, F32),
                   jax.ShapeDtypeStruct(kpt.shape, F32), jax.ShapeDtypeStruct(bias_t.shape, F32),
                   jax.ShapeDtypeStruct(sink_row.shape, F32)],
        compiler_params=_params(("parallel", "arbitrary")),
    )(qt, kpt, vpt, bias_t, sink_row, dot_, dk_in, dv_in)


def _bucket_onehot():
    qi = np.arange(WINDOW)[:, None]
    kj = np.arange(2 * WINDOW)[None, :]
    n = np.maximum(qi + WINDOW - kj, 0)
    max_exact = N_BUCKETS // 2
    nf = np.maximum(n, 1).astype(np.float64)
    val = np.log(nf / max_exact) / math.log(WINDOW / max_exact) * (N_BUCKETS - max_exact)
    assert np.all(np.abs(val - np.round(val))[(n > max_exact) & (n < WINDOW)] > 1e-3)
    large = np.minimum(max_exact + val.astype(np.int64), N_BUCKETS - 1)
    bucket = np.where(n < max_exact, n, large).reshape(-1)
    onehot = np.zeros((128, bucket.size), np.float32)
    onehot[bucket, np.arange(bucket.size)] = 1.0
    return onehot


def _split3(x):
    a = x.astype(BF16)
    r = x - a.astype(F32)
    b = r.astype(BF16)
    c = (r - b.astype(F32)).astype(BF16)
    return a, b, c


def bias_table(rel_bias):
    nh = rel_bias.shape[1]
    oh = jnp.asarray(_bucket_onehot(), BF16)
    n = oh.shape[1]
    tn = 4096
    rb = jnp.zeros((nh, 128), F32).at[:, :N_BUCKETS].set(rel_bias.T)

    def body(rb_ref, oh_ref, o_ref):
        o_ref[...] = sum(jnp.dot(t, oh_ref[...], preferred_element_type=F32) for t in _split3(rb_ref[...]))

    return pl.pallas_call(
        body, name="bias_table", grid=(n // tn,),
        in_specs=[pl.BlockSpec((nh, 128), lambda i: (0, 0)), pl.BlockSpec((128, tn), lambda i: (0, i))],
        out_specs=pl.BlockSpec((nh, tn), lambda i: (0, i)),
        out_shape=jax.ShapeDtypeStruct((nh, n), F32),
        compiler_params=_params(("parallel",)),
    )(rb, oh)


def bias_table_grad(db0, db1):
    nh, n = db0.shape
    oh = jnp.asarray(_bucket_onehot(), BF16)
    tn = 4096

    def body(a_ref, b_ref, oh_ref, o_ref):
        @pl.when(pl.program_id(0) == 0)
        def _():
            o_ref[...] = jnp.zeros_like(o_ref)

        o_ref[...] += sum(lax.dot_general(t, oh_ref[...], (((1,), (1,)), ((), ())), preferred_element_type=F32)
                          for t in _split3(a_ref[...] + b_ref[...]))

    blk = pl.BlockSpec((nh, tn), lambda i: (0, i))
    return pl.pallas_call(
        body, name="bias_table_grad", grid=(n // tn,),
        in_specs=[blk, blk, pl.BlockSpec((128, tn), lambda i: (0, i))],
        out_specs=pl.BlockSpec((nh, 128), lambda i: (0, 0)),
        out_shape=jax.ShapeDtypeStruct((nh, 128), F32),
        compiler_params=_params(("arbitrary",)),
    )(db0, db1, oh)


def _owner_view(ref, name, d):
    if name == 'a_norm':
        return ref.at[d]
    if name in COL_SHARDED:
        n = ref.shape[2] // N_DEV
        return ref.at[:, :, pl.ds(pl.multiple_of(d * n, 128), n)]
    return ref.at[:, d]


def _place():
    return lax.axis_index("x"), lax.axis_index("y"), lax.axis_index("c")


def _dev(p):
    return 4 * p[0] + 2 * p[1] + p[2]


def _remote(src, dst, send_sem, recv_sem, to):
    return pltpu.make_async_remote_copy(src_ref=src, dst_ref=dst, send_sem=send_sem, recv_sem=recv_sem,
                                        device_id=to, device_id_type=MESH)


def _dma_sems(*shapes):
    return [pltpu.SemaphoreType.DMA(sh) for sh in shapes]


def comm_call(name, build, ins, out_shapes, sems, aliases=None):
    n_in, n_out = len(ins), len(out_shapes)

    def body(*refs):
        copies = build(refs[:n_in], refs[n_in:n_in + n_out], *refs[n_in + n_out:])
        for cp in copies:
            cp.start()
        for cp in copies:
            cp.wait()

    hbm = pl.BlockSpec(memory_space=pl.ANY)
    return pl.pallas_call(
        body, name=name, in_specs=[hbm] * n_in, out_specs=[hbm] * n_out, out_shape=list(out_shapes),
        scratch_shapes=sems, input_output_aliases=aliases or {},
    )(*ins)


def all_gather_weights(names, shards, full_shapes):
    n = len(names)

    def body(*refs):
        ins, outs = refs[:n], refs[n:2 * n]
        send_sems, recv_sems, local_sems = refs[2 * n:]
        x, y, c = _place()
        me, sibling = (x, y, c), (x, y, 1 - c)
        chips = [(1 - x, y), (x, 1 - y), (1 - x, 1 - y)]

        def copy(t, k, block, to, src=None):
            dst = _owner_view(outs[t], names[t], _dev(block))
            return _remote(dst if src is None else src, dst, send_sems.at[t, k], recv_sems.at[t, k], to)

        mine = [pltpu.make_async_copy(ins[t], _owner_view(outs[t], names[t], _dev(me)), local_sems.at[t])
                for t in range(n)]
        for cp in mine:
            cp.start()
        first = []
        for t in range(n):
            first.append(copy(t, 0, me, sibling, src=ins[t]))
            first += [copy(t, 1 + j, me, (*chip, c), src=ins[t]) for j, chip in enumerate(chips)]
        for cp in first:
            cp.start()
        passed = []
        for j, chip in enumerate(chips):
            for t in range(n):
                copy(t, 1 + j, (*chip, c), me).wait_recv()
                fwd = copy(t, 4 + j, (*chip, c), sibling)
                fwd.start()
                passed.append(fwd)
        for t in range(n):
            copy(t, 0, sibling, me).wait_recv()
            for j, chip in enumerate(chips):
                copy(t, 4 + j, (*chip, 1 - c), me).wait_recv()
        for cp in first + passed:
            cp.wait_send()
        for cp in mine:
            cp.wait()

    hbm = pl.BlockSpec(memory_space=pl.ANY)
    return pl.pallas_call(
        body, name="all_gather_layer0",
        in_specs=[hbm] * n, out_specs=[hbm] * n,
        out_shape=[jax.ShapeDtypeStruct(full_shapes[t], shards[t].dtype) for t in range(n)],
        scratch_shapes=_dma_sems((n, 7), (n, 7), (n,)),
    )(*shards)


def ag_direct(names, shards, full_shapes):
    n = len(names)

    def build(ins, outs, send_sems, recv_sems, local_sems, fwd_send_sems, fwd_recv_sems):
        x, y, c = _place()
        peers = [(x, y, 1 - c), (1 - x, y, c), (x, 1 - y, c), (1 - x, 1 - y, c)]
        copies = []
        for t in range(n):
            dst = _owner_view(outs[t], names[t], _dev((x, y, c)))
            copies.append(pltpu.make_async_copy(ins[t], dst, local_sems.at[t]))
            copies += [_remote(ins[t], dst, send_sems.at[t, k], recv_sems.at[t, k], to) for k, to in enumerate(peers)]
        return copies

    def forward(ins, outs, send_sems, recv_sems, local_sems, fwd_send_sems, fwd_recv_sems):
        x, y, c = _place()
        copies = []
        for t in range(n):
            for k, chip in enumerate([(1 - x, y), (x, 1 - y), (1 - x, 1 - y)]):
                view = _owner_view(outs[t], names[t], _dev((*chip, c)))
                copies.append(_remote(view, view, fwd_send_sems.at[t, k], fwd_recv_sems.at[t, k], (x, y, 1 - c)))
        return copies

    return Carry(build, shards, [jax.ShapeDtypeStruct(full_shapes[t], shards[t].dtype) for t in range(n)],
                 _dma_sems((n, 4), (n, 4), (n,), (n, 3), (n, 3)), then=forward)


def sibling_exchange(names, grads, part_shapes):
    n = len(names)

    def build(ins, outs, send_sems, recv_sems):
        x, y, c = _place()
        return [_remote(_owner_view(ins[t], names[t], 2 * q + 1 - c), outs[t].at[q], send_sems.at[t, q],
                        recv_sems.at[t, q], (x, y, 1 - c)) for t in range(n) for q in range(4)]

    return Carry(build, grads, [jax.ShapeDtypeStruct((4,) + part_shapes[t], BF16) for t in range(n)],
                 _dma_sems((n, 4), (n, 4)))


def chip_exchange(names, parts, part_shapes):
    n = len(names)

    def build(ins, outs, send_sems, recv_sems):
        x, y, c = _place()
        chips = [(1 - x, y), (x, 1 - y), (1 - x, 1 - y)]
        return [_remote(ins[t].at[2 * chip[0] + chip[1]], outs[t].at[k], send_sems.at[t, k], recv_sems.at[t, k],
                        (*chip, c)) for t in range(n) for k, chip in enumerate(chips)]

    return Carry(build, parts, [jax.ShapeDtypeStruct((3,) + part_shapes[t], BF16) for t in range(n)],
                 _dma_sems((n, 3), (n, 3)))


def all_gather_rows(x):
    r, w = x.shape

    def body(x_ref, out_ref, send_sems, recv_sems, local_sem):
        px, py, pc = _place()
        me = 4 * px + 2 * py + pc
        mine = pltpu.make_async_copy(x_ref, out_ref.at[me], local_sem)
        mine.start()
        copies = []
        for k in range(1, N_DEV):
            peer = (px ^ (k >> 2), py ^ ((k >> 1) & 1), pc ^ (k & 1))
            copies.append(pltpu.make_async_remote_copy(
                src_ref=x_ref, dst_ref=out_ref.at[me], send_sem=send_sems.at[k - 1], recv_sem=recv_sems.at[k - 1],
                device_id=peer, device_id_type=MESH))
        for cp in copies:
            cp.start()
        for k in range(1, N_DEV):
            peer_idx = me ^ k
            pltpu.make_async_remote_copy(
                src_ref=x_ref, dst_ref=out_ref.at[peer_idx], send_sem=send_sems.at[k - 1],
                recv_sem=recv_sems.at[k - 1], device_id=(px, py, pc), device_id_type=MESH).wait_recv()
        for cp in copies:
            cp.wait_send()
        mine.wait()

    vmem = pl.BlockSpec(memory_space=pltpu.VMEM)
    return pl.pallas_call(
        body, name="all_gather_small_grads",
        in_specs=[vmem], out_specs=vmem,
        out_shape=jax.ShapeDtypeStruct((N_DEV, r, w), x.dtype),
        scratch_shapes=[pltpu.SemaphoreType.DMA((N_DEV - 1,)), pltpu.SemaphoreType.DMA((N_DEV - 1,)),
                        pltpu.SemaphoreType.DMA],
    )(x)


def _adamw(w, g, m, v):
    m = ADAM_B1 * m + (1.0 - ADAM_B1) * g
    v = ADAM_B2 * v + (1.0 - ADAM_B2) * (g * g)
    m_hat = m / (1.0 - ADAM_B1 ** ADAM_STEP)
    v_hat = v / (1.0 - ADAM_B2 ** ADAM_STEP)
    return -ADAM_LR * (m_hat / (jnp.sqrt(v_hat) + ADAM_EPS) + ADAM_WD * w), m, v


def sibling_sum(name, col, grads, recv, core):
    _, nl, rows, cols = recv.shape
    tr = _tile(rows, 512)
    rspec = pl.BlockSpec((None, None, tr, cols), lambda q, l, i, c_ref: (q, l, i, 0))
    if col:
        gspec = pl.BlockSpec((None, tr, cols), lambda q, l, i, c_ref: (l, i, 2 * q + c_ref[0]))
    else:
        gspec = pl.BlockSpec((None, None, tr, cols), lambda q, l, i, c_ref: (l, 2 * q + c_ref[0], i, 0))

    def body(c_ref, g_ref, r_ref, o_ref):
        del c_ref
        o_ref[...] = (g_ref[...].astype(F32) + r_ref[...].astype(F32)).astype(BF16)

    return pl.pallas_call(
        body, name=name,
        grid_spec=pltpu.PrefetchScalarGridSpec(num_scalar_prefetch=1, grid=(4, nl, rows // tr),
                                               in_specs=[gspec, rspec], out_specs=rspec),
        out_shape=jax.ShapeDtypeStruct(recv.shape, BF16),
        compiler_params=_params(("parallel", "parallel", "parallel")),
    )(core.reshape(1), grads, recv)


def reduce_adamw(name, parts, recv, chip, w, m, v, l0, prev):
    _, nl, rows, cols = parts.shape
    tr = _tile(rows, 256)

    def body(q_ref, p_ref, r_ref, w_ref, m_ref, v_ref, *rest):
        del q_ref
        g_out, d_out, m_out, v_out = rest[-4:]
        g = ((p_ref[...].astype(F32) + r_ref[0].astype(F32)) + r_ref[1].astype(F32)) + r_ref[2].astype(F32)
        d, mn, vn = _adamw(w_ref[...], g, m_ref[...], v_ref[...])
        g_out[...] = g
        d_out[...] = d
        m_out[...] = mn
        v_out[...] = vn

    blk = pl.BlockSpec((None, tr, cols), lambda l, i, q_ref: (l0 + l, i, 0))
    prev = list(prev) if prev else []
    return pl.pallas_call(
        body, name=name,
        grid_spec=pltpu.PrefetchScalarGridSpec(
            num_scalar_prefetch=1, grid=(nl, rows // tr),
            in_specs=[pl.BlockSpec((None, None, tr, cols), lambda l, i, q_ref: (q_ref[0], l, i, 0)),
                      pl.BlockSpec((3, None, tr, cols), lambda l, i, q_ref: (0, l, i, 0)), blk, blk, blk]
            + [pl.BlockSpec(memory_space=pl.ANY)] * len(prev),
            out_specs=[blk] * 4),
        out_shape=[jax.ShapeDtypeStruct(w.shape, F32)] * 4,
        input_output_aliases={6 + i: i for i in range(len(prev))},
        compiler_params=_params(("parallel", "parallel")),
    )(chip.reshape(1), parts, recv, w, m, v, *prev)


def small_adamw(name, gathered, w, m, v):
    _, r, c = gathered.shape

    def body(ga_ref, w_ref, m_ref, v_ref, g_out, d_out, m_out, v_out):
        g = ga_ref[0]
        for d in range(1, N_DEV):
            g = g + ga_ref[d]
        dl, mn, vn = _adamw(w_ref[...], g, m_ref[...], v_ref[...])
        g_out[...] = g
        d_out[...] = dl
        m_out[...] = mn
        v_out[...] = vn

    return pl.pallas_call(
        body, name=name,
        out_shape=[jax.ShapeDtypeStruct((r, c), F32)] * 4,
        compiler_params=_params(),
    )(gathered, w, m, v)


def _rms(x, g):
    return x * lax.rsqrt(jnp.mean(x * x, axis=-1, keepdims=True) + EPS) * g


def _rms_bwd_epilogue(dn, x, dres, g):
    r = lax.rsqrt(jnp.mean(x * x, axis=-1, keepdims=True) + EPS)
    xh = x * r
    dyg = dn * g
    dx = dres + r * (dyg - xh * jnp.mean(dyg * xh, axis=-1, keepdims=True))
    return dx, dx, jnp.sum(dn * xh, axis=0, keepdims=True), jnp.sum(dx, axis=0, keepdims=True)


def _residual_then_norms(n_terms):
    def epilogue(acc, *ex):
        h = acc
        for t in ex[:n_terms]:
            h = h + t
        return (h,) + tuple(_rms(h, g) for g in ex[n_terms:])
    return epilogue


def local_step(x, target, small, ex):
    s, d = x.shape
    n_a, n_b = small['a_norm'].shape[0], small['b_norm'].shape[0]
    sg = {}
    gb = {}

    def fwd_mm(name, a, wname, layer, epilogue, extras, out_dtypes, **kw):
        return mm_nn(name, a, *ex.weight(wname, layer), epilogue, extras, out_dtypes, **kw)

    def dx_mm(name, dy, wname, layer, epilogue, extras, out_dtypes, **kw):
        return mm_nt(name, dy, *ex.weight(wname, layer), epilogue, extras, out_dtypes, **kw)

    def dw_mm(name, a, dy, wname, layer, **kw):
        key, slab, shape = ex.grad(wname, layer)
        gb[key] = mm_tn(name, a, dy, gb.get(key), shape, slab, **kw)

    plain = lambda acc: (acc,)
    plus_col = lambda acc, b: (acc + b,)

    bias_flat = bias_table(small['rel_bias'])
    bias_t = bias_flat.reshape(2, 8, WINDOW, 2 * WINDOW).transpose(0, 3, 1, 2).reshape(2, 2 * WINDOW, 8 * WINDOW)
    sink_rows = [jnp.repeat(small['b_sinks'][j], WINDOW).reshape(2, 1, 8 * WINDOW) for j in range(n_b)]

    gain = lambda g: g.reshape(1, -1)

    def mlp_fwd(h, n2, layer, next_gains):
        u, a = fwd_mm(f"mlp_up_fwd{layer}", n2, 'mlp_up', layer,
                      lambda acc: (acc, jnp.square(jnp.maximum(acc, 0.0))), (), (BF16, BF16))
        h2, *nexts = fwd_mm(f"mlp_down_fwd{layer}", a, 'mlp_down', layer, _residual_then_norms(1),
                            (h, *[gain(g) for g in next_gains]), (F32,) + (BF16,) * len(next_gains))
        return h2, nexts, (n2, u, a)

    h = x
    saved = []
    n1 = rms_fwd("a_norm_fwd0", h, small['a_norm'][0])
    for l in range(n_a):
        (qkvt,) = fwd_mm(f"a_qkv_fwd{l}", n1, 'a_wqkv', l, plain, (), (BF16,), out_t=True)
        qkvt = qkvt.reshape(3 * d // HEAD_DIM, HEAD_DIM, s)
        o_t, rtab, carried = sb_fwd(f"sb_fwd{l}", qkvt, ex.fwd_carry(l))
        ex.fwd_done(l, carried)
        o_t = o_t.reshape(d, s)
        h_mid, n2 = fwd_mm(f"a_wo_fwd{l}", o_t, 'a_wo', l, _residual_then_norms(1),
                           (h, gain(small['mlp_norm'][l])), (F32, BF16), a_t=True)
        next_gains = [small['a_norm'][l + 1]] if l + 1 < n_a else [small['b_norm'][0], small['kv_norm']]
        h_out, nexts, mlp_saved = mlp_fwd(h_mid, n2, l, next_gains)
        saved.append((h, n1, qkvt, o_t, rtab, h_mid, mlp_saved))
        h, n1 = h_out, nexts[0]
    h_kv, nkv = h, nexts[1]
    (kvt,) = fwd_mm("kv_fwd", nkv, 'w_kv', 0, plus_col, (small['b_kv'].reshape(-1, 1),), (BF16,), out_t=True)
    kvt = kvt.reshape(2, 2, HEAD_DIM, s)
    kpt, vpt = (jnp.pad(t, ((0, 0), (0, 0), (WINDOW, 0))) for t in (kvt[0], kvt[1]))
    for j in range(n_b):
        layer = n_a + j
        (qbt,) = fwd_mm(f"b_q_fwd{j}", n1, 'b_wq', j, plus_col, (small['b_bq'][j].reshape(-1, 1),), (BF16,),
                        out_t=True)
        o_t = swa_fwd(f"swa_fwd{j}", qbt, kpt, vpt, bias_t, sink_rows[j])
        h_mid, n2 = fwd_mm(f"b_wo_fwd{j}", o_t, 'b_wo', j, _residual_then_norms(2),
                           (h, gain(small['b_bo'][j]), gain(small['mlp_norm'][layer])), (F32, BF16), a_t=True)
        h_out, nexts, mlp_saved = mlp_fwd(h_mid, n2, layer, [small['b_norm'][j + 1]] if j + 1 < n_b else [])
        saved.append((h, n1, qbt, o_t, h_mid, mlp_saved))
        h, n1 = h_out, (nexts[0] if nexts else None)

    dh, dhb, dg_final, loss_b = loss_head(h, small['final_norm'], target)
    sg['final_norm'] = dg_final[0]
    sg['mlp_norm'] = [None] * (n_a + n_b)

    def mlp_bwd(dh, dhb, h_mid, mlp_saved, layer):
        n2, u, a = mlp_saved
        du, *carried = dx_mm(f"mlp_down_dx{layer}", dhb, 'mlp_down', layer,
                             lambda acc, uu: (acc * (2.0 * jnp.maximum(uu.astype(F32), 0.0)),), (u,), (BF16,),
                             exchange=ex.mlp_carry(layer, gb))
        ex.mlp_done(layer, carried)
        dw_mm(f"mlp_down_dw{layer}", a, dhb, 'mlp_down', layer)
        dh2, dh2b, dg, cs = dx_mm(f"mlp_up_dx{layer}", du, 'mlp_up', layer, _rms_bwd_epilogue,
                                  (h_mid, dh, gain(small['mlp_norm'][layer])), (F32, BF16), n_sums=2)
        dw_mm(f"mlp_up_dw{layer}", n2, du, 'mlp_up', layer)
        sg['mlp_norm'][layer] = dg[0]
        return dh2, dh2b, cs

    dkp = jnp.zeros(kpt.shape, F32)
    dvp = jnp.zeros(vpt.shape, F32)
    sg['b_norm'], sg['b_bq'], sg['b_bo'], sg['b_sinks'] = [None] * n_b, [None] * n_b, [None] * n_b, [None] * n_b
    dbias = [None] * n_b
    for j in reversed(range(n_b)):
        layer = n_a + j
        h_in, n1, qbt, o_t, h_mid, mlp_saved = saved[layer]
        dh, dhb, cs = mlp_bwd(dh, dhb, h_mid, mlp_saved, layer)
        sg['b_bo'][j] = cs[0]
        (do_t,) = dx_mm(f"b_wo_dx{j}", dhb, 'b_wo', j, plain, (), (BF16,), out_t=True)
        dw_mm(f"b_wo_dw{j}", o_t, dhb, 'b_wo', j, x_t=True)
        dq_t, dkp, dvp, dbias[j], dsink = swa_bwd(f"swa_bwd{j}", qbt, kpt, vpt, bias_t, sink_rows[j], do_t, dkp, dvp)
        sg['b_sinks'][j] = colsum(f"sink_grad{j}", dsink.reshape(16, WINDOW).T)[0]
        sg['b_bq'][j] = rowsum(f"b_bq_grad{j}", dq_t)
        dh, dhb, dg, _ = dx_mm(f"b_q_dx{j}", dq_t, 'b_wq', j, _rms_bwd_epilogue,
                               (h_in, dh, gain(small['b_norm'][j])), (F32, BF16), a_t=True, n_sums=2)
        dw_mm(f"b_q_dw{j}", n1, dq_t, 'b_wq', j, dy_t=True)
        sg['b_norm'][j] = dg[0]
    unt = lambda t: t.reshape(2, 2 * WINDOW, 8, WINDOW).transpose(0, 2, 3, 1).reshape(bias_flat.shape)
    sg['rel_bias'] = bias_table_grad(unt(dbias[0]), unt(dbias[1]))[:, :N_BUCKETS].T

    dkv_t = jnp.concatenate([dkp[:, :, WINDOW:], dvp[:, :, WINDOW:]], axis=0).reshape(-1, s)
    sg['b_kv'] = rowsum("b_kv_grad", dkv_t)
    dkvb = dkv_t.astype(BF16)
    dh, dhb, dg, _ = dx_mm("kv_dx", dkvb, 'w_kv', 0, _rms_bwd_epilogue, (h_kv, dh, gain(small['kv_norm'])),
                           (F32, BF16), a_t=True, n_sums=2)
    dw_mm("kv_dw", nkv, dkvb, 'w_kv', 0, dy_t=True)
    sg['kv_norm'] = dg[0]

    sg['a_norm'] = [None] * n_a
    for l in reversed(range(n_a)):
        h_in, n1, qkvt, o_t, rtab, h_mid, mlp_saved = saved[l]
        dh, dhb, _ = mlp_bwd(dh, dhb, h_mid, mlp_saved, l)
        (do_t,) = dx_mm(f"a_wo_dx{l}", dhb, 'a_wo', l, plain, (), (BF16,), out_t=True)
        dw_mm(f"a_wo_dw{l}", o_t, dhb, 'a_wo', l, x_t=True)
        dqkv_t, carried = sb_bwd(f"sb_bwd{l}", qkvt, do_t.reshape(d // HEAD_DIM, HEAD_DIM, s), rtab,
                                 ex.bwd_carry(l, gb))
        ex.bwd_done(l, carried)
        dqkv_t = dqkv_t.reshape(3 * d, s)
        dw_mm(f"a_qkv_dw{l}", n1, dqkv_t, 'a_wqkv', l, dy_t=True)
        dh, dhb, dg, _, *carried = dx_mm(f"a_qkv_dx{l}", dqkv_t, 'a_wqkv', l, _rms_bwd_epilogue,
                                         (h_in, dh, gain(small['a_norm'][l])), (F32, BF16), a_t=True, n_sums=2,
                                         exchange=ex.last_carry(gb) if l == 0 else None)
        if l == 0:
            ex.last_done(carried)
        sg['a_norm'][l] = dg[0]

    small_grads = {
        'a_norm': jnp.stack(sg['a_norm']), 'kv_norm': sg['kv_norm'], 'b_kv': sg['b_kv'],
        'b_norm': jnp.stack(sg['b_norm']), 'b_bq': jnp.stack(sg['b_bq']), 'b_sinks': jnp.stack(sg['b_sinks']),
        'b_bo': jnp.stack(sg['b_bo']), 'rel_bias': sg['rel_bias'], 'mlp_norm': jnp.stack(sg['mlp_norm']),
        'final_norm': sg['final_norm'],
    }
    return loss_b, dh, gb, small_grads


def _full_shape(name, shard_shape):
    if name in COL_SHARDED:
        return shard_shape[:2] + (N_DEV * shard_shape[2],)
    nl, r, n = shard_shape
    return (nl, N_DEV, r, n)


def _as_w3_shape(name, shard_shape):
    full = _full_shape(name, shard_shape)
    return full if name in COL_SHARDED else (full[0], full[1] * full[2], full[3])


def _as_w3(name, full):
    if name in COL_SHARDED:
        return full
    nl, nd, r, n = full.shape
    return full.reshape(nl, nd * r, n)


AG_GROUPS = {
    0: (('a_wqkv', 0, 1),),
    1: (('a_wo', 0, 2), ('mlp_up', 0, 2), ('mlp_down', 0, 2), ('a_wqkv', 1, 1)),
    2: (('mlp_up', 2, 2), ('mlp_down', 2, 2), ('b_wq', 0, 2), ('b_wo', 0, 2), ('w_kv', 0, 1)),
}
RS_GROUPS = {
    'A': (('mlp_up', 2, 2), ('mlp_down', 2, 2), ('b_wq', 0, 2), ('b_wo', 0, 2), ('w_kv', 0, 1)),
    'B': (('a_wqkv', 1, 1), ('a_wo', 0, 2), ('mlp_up', 0, 2), ('mlp_down', 0, 2)),
    'C': (('a_wqkv', 0, 1),),
}


class _Exchanges:
    def __init__(self, full0, shards, core, chip, w3, m3, v3):
        self.wbuf = {0: {n: _as_w3(n, full0[n]) for n, _, _ in AG_GROUPS[0]}}
        self.shards, self.core, self.chip = shards, core, chip
        self.w3, self.m3, self.v3 = w3, m3, v3
        self.shard_dims = {n: w3[n].shape[1:] for n in BIG}
        self.parts = {}
        self.out = {}

    def weight(self, name, layer):
        for group, members in AG_GROUPS.items():
            for n, l0, nl in members:
                if n == name and l0 <= layer < l0 + nl:
                    return self.wbuf[group][name], layer - l0
        raise KeyError((name, layer))

    def fwd_carry(self, layer):
        names = [n for n, _, _ in AG_GROUPS[layer + 1]]
        shards = [self.shards[layer + 1][n] for n in names]
        return ag_direct(names, shards, [_full_shape(n, sh.shape) for n, sh in zip(names, shards)])

    def fwd_done(self, layer, carried):
        names = [n for n, _, _ in AG_GROUPS[layer + 1]]
        self.wbuf[layer + 1] = {n: _as_w3(n, f) for n, f in zip(names, carried)}

    def grad(self, name, layer):
        for group, members in RS_GROUPS.items():
            for n, l0, nl in members:
                if n == name and l0 <= layer < l0 + nl:
                    return (group, name), layer - l0, _as_w3_shape(name, (nl,) + self.shard_dims[name])
        raise KeyError((name, layer))

    def _members(self, group):
        names = [n for n, _, _ in RS_GROUPS[group]]
        return names, [(nl,) + self.shard_dims[n] for n, _, nl in RS_GROUPS[group]]

    def _sibling_carry(self, group, gb):
        names, shapes = self._members(group)
        self.gfull = [gb[(group, n)].reshape(_full_shape(n, sh)) for n, sh in zip(names, shapes)]
        return sibling_exchange(names, self.gfull, shapes)

    def _sibling_done(self, group, recv):
        names, _ = self._members(group)
        self.parts[group] = [sibling_sum(f"rs_sibling_sum_{group}_{n}", n in COL_SHARDED, g, r, self.core)
                             for n, g, r in zip(names, self.gfull, recv)]

    def _sibling_stage(self, group, gb):
        ce = self._sibling_carry(group, gb)
        self._sibling_done(group, comm_call(f"rs_sibling_exchange_{group}", ce.build, ce.ins, ce.out_shapes, ce.sems))

    def mlp_carry(self, layer, gb):
        return self._sibling_carry('A', gb) if layer == 1 else None

    def mlp_done(self, layer, carried):
        if layer == 1:
            self._sibling_done('A', carried)

    def bwd_carry(self, layer, gb):
        group = 'A' if layer == 1 else 'B'
        if group not in self.parts:
            self._sibling_stage(group, gb)
        names, shapes = self._members(group)
        return chip_exchange(names, self.parts[group], shapes)

    def bwd_done(self, layer, carried):
        self._adamw('A' if layer == 1 else 'B', carried)

    def last_carry(self, gb):
        self._sibling_stage('C', gb)
        names, shapes = self._members('C')
        return chip_exchange(names, self.parts['C'], shapes)

    def last_done(self, carried):
        self._adamw('C', carried)

    def _adamw(self, group, recv2):
        for (n, l0, _), p, r in zip(RS_GROUPS[group], self.parts[group], recv2):
            self.out[n] = reduce_adamw(f"adamw_{group}_{n}", p, r, self.chip, self.w3[n], self.m3[n], self.v3[n],
                                       l0, self.out.get(n))


def _pack_small(vals):
    flat = jnp.concatenate([vals[n].reshape(-1).astype(F32) for n in SMALL] + [vals['loss'].reshape(-1)])
    rows = -(-flat.shape[0] // 1024) * 8
    return jnp.pad(flat, (0, rows * 128 - flat.shape[0])).reshape(rows, 128)


def _unpack_small(packed, shapes):
    flat = packed.reshape(-1)
    out, off = {}, 0
    for n in SMALL + ['loss']:
        size = int(np.prod(shapes[n]))
        out[n] = flat[off:off + size].reshape(shapes[n])
        off += size
    return out


def kernel(x, a_norm, a_wqkv, a_wo, kv_norm, w_kv, b_kv, b_norm, b_wq, b_bq, b_sinks, b_wo, b_bo, rel_bias, mlp_norm, mlp_up, mlp_down, final_norm, loss_target, m_a_norm, m_a_wqkv, m_a_wo, m_kv_norm, m_w_kv, m_b_kv, m_b_norm, m_b_wq, m_b_bq, m_b_sinks, m_b_wo, m_b_bo, m_rel_bias, m_mlp_norm, m_mlp_up, m_mlp_down, m_final_norm, v_a_norm, v_a_wqkv, v_a_wo, v_kv_norm, v_w_kv, v_b_kv, v_b_norm, v_b_wq, v_b_bq, v_b_sinks, v_b_wo, v_b_bo, v_rel_bias, v_mlp_norm, v_mlp_up, v_mlp_down, v_final_norm):
    w = dict(a_norm=a_norm, a_wqkv=a_wqkv, a_wo=a_wo, kv_norm=kv_norm, w_kv=w_kv, b_kv=b_kv, b_norm=b_norm,
             b_wq=b_wq, b_bq=b_bq, b_sinks=b_sinks, b_wo=b_wo, b_bo=b_bo, rel_bias=rel_bias, mlp_norm=mlp_norm,
             mlp_up=mlp_up, mlp_down=mlp_down, final_norm=final_norm)
    m = dict(a_norm=m_a_norm, a_wqkv=m_a_wqkv, a_wo=m_a_wo, kv_norm=m_kv_norm, w_kv=m_w_kv, b_kv=m_b_kv,
             b_norm=m_b_norm, b_wq=m_b_wq, b_bq=m_b_bq, b_sinks=m_b_sinks, b_wo=m_b_wo, b_bo=m_b_bo,
             rel_bias=m_rel_bias, mlp_norm=m_mlp_norm, mlp_up=m_mlp_up, mlp_down=m_mlp_down, final_norm=m_final_norm)
    v = dict(a_norm=v_a_norm, a_wqkv=v_a_wqkv, a_wo=v_a_wo, kv_norm=v_kv_norm, w_kv=v_w_kv, b_kv=v_b_kv,
             b_norm=v_b_norm, b_wq=v_b_wq, b_bq=v_b_bq, b_sinks=v_b_sinks, b_wo=v_b_wo, b_bo=v_b_bo,
             rel_bias=v_rel_bias, mlp_norm=v_mlp_norm, mlp_up=v_mlp_up, mlp_down=v_mlp_down, final_norm=v_final_norm)
    px, py, pc = _place()
    me = 4 * px + 2 * py + pc
    chip = (2 * px + py).astype(jnp.int32)
    core = pc.astype(jnp.int32)

    as3 = lambda t: t[None] if t.ndim == 2 else t
    w3, m3, v3 = ({n: as3(src[n]) for n in BIG} for src in (w, m, v))
    shards = {g: {n: w3[n][l0:l0 + nl].astype(BF16) for n, l0, nl in members} for g, members in AG_GROUPS.items()}
    an_pad = jnp.zeros((8, 128), F32).at[:a_norm.shape[0]].set(a_norm)
    names0 = [n for n, _, _ in AG_GROUPS[0]]
    full0 = all_gather_weights(names0 + ['a_norm'], [shards[0][n] for n in names0] + [an_pad],
                               [_full_shape(n, shards[0][n].shape) for n in names0] + [(N_DEV, 8, 128)])
    full0 = dict(zip(names0 + ['a_norm'], full0))
    n_a = a_norm.shape[0]
    small = {n: w[n] for n in SMALL}
    small['a_norm'] = full0['a_norm'][:, :n_a].transpose(1, 0, 2).reshape(n_a, -1)

    ex = _Exchanges(full0, shards, core, chip, w3, m3, v3)
    loss_b, grad_x, gb, sgrads = local_step(x[0], loss_target[0], small, ex)
    out = {n: [t.reshape(w[n].shape) for t in bufs] for n, bufs in ex.out.items()}

    sgrads['loss'] = loss_b[0, :1]
    gathered = all_gather_rows(_pack_small(sgrads))
    shapes = {n: w[n].shape for n in SMALL}
    shapes['a_norm'] = (n_a, a_norm.shape[1] * N_DEV)
    shapes['loss'] = (1,)
    zeros1 = jnp.zeros((1,), F32)

    def packed(src):
        vals = {n: src[n] for n in SMALL}
        vals['a_norm'] = jnp.zeros(shapes['a_norm'], F32)
        vals['loss'] = zeros1
        return _pack_small(vals)

    sm = small_adamw("adamw_small", gathered, packed(w), packed(m), packed(v))
    sm = [_unpack_small(t, shapes) for t in sm]
    g_an = lax.dynamic_slice_in_dim(sm[0]['a_norm'], me * a_norm.shape[1], a_norm.shape[1], axis=1)
    pad = lambda t: jnp.zeros((8, 128), F32).at[:n_a].set(t)
    gathered_an = jnp.zeros((N_DEV, 8, 128), F32).at[0].set(pad(g_an))
    an = small_adamw("adamw_a_norm", gathered_an, pad(a_norm), pad(m_a_norm), pad(v_a_norm))
    for i in range(4):
        sm[i]['a_norm'] = an[i][:n_a]
    for n in BIG:
        for i in range(4):
            sm[i][n] = out[n][i]
    loss = sm[0]['loss'][0]
    return (loss, grad_x[None], *[sm[0][n] for n in WEIGHTS], *[sm[1][n] for n in WEIGHTS],
            *[sm[2][n] for n in WEIGHTS], *[sm[3][n] for n in WEIGHTS])
```

```python
import math

import numpy as np
import jax
import jax.numpy as jnp
from jax import lax
from jax.experimental import pallas as pl
from jax.experimental.pallas import tpu as pltpu

F32 = jnp.float32
BF16 = jnp.bfloat16
MESH = pl.DeviceIdType.MESH

N_DEV = 8
HEAD_DIM = 64
WINDOW = 128
N_BUCKETS = 32
EPS = 1e-5
NEG_INF = -1e30
Q_SCALE = 1.0 / math.sqrt(HEAD_DIM)
LOG2E = 1.4426950408889634

ADAM_LR, ADAM_B1, ADAM_B2, ADAM_EPS, ADAM_WD, ADAM_STEP = 0.001, 0.9, 0.999, 1e-08, 0.01, 10

SB_BQ = 512
SB_BK = 128
SB_DEAD = 160.0
SB_UNSEEN = 1e30
ROW_TILE = 512
VMEM_LIMIT = 56 * 1024 * 1024

WEIGHTS = ['a_norm', 'a_wqkv', 'a_wo', 'kv_norm', 'w_kv', 'b_kv', 'b_norm', 'b_wq', 'b_bq', 'b_sinks', 'b_wo',
           'b_bo', 'rel_bias', 'mlp_norm', 'mlp_up', 'mlp_down', 'final_norm']
BIG = ['a_wqkv', 'a_wo', 'w_kv', 'b_wq', 'b_wo', 'mlp_up', 'mlp_down']
COL_SHARDED = ('a_wqkv', 'mlp_up')
SMALL = ['a_norm', 'kv_norm', 'b_kv', 'b_norm', 'b_bq', 'b_sinks', 'b_bo', 'rel_bias', 'mlp_norm', 'final_norm']


def _params(sem=None):
    return pltpu.CompilerParams(dimension_semantics=sem, vmem_limit_bytes=VMEM_LIMIT)


def _pick(n, cands):
    for c in cands:
        if n % c == 0:
            return c
    raise ValueError(n)


def _tile(n, want):
    return n if n <= want else _pick(n, (want, want // 2, want // 4))


MM_TILE_BUDGET = 36 * 1024 * 1024


def _row_tile(m, contraction, cols, streams):
    weight = 2 * contraction * cols * 2
    for rows in (2048, 1024, 512):
        if m % rows == 0 and weight + 2 * rows * (2 * contraction + cols * sum(streams)) <= MM_TILE_BUDGET:
            return rows
    return _tile(m, 512)


def mm_nn(name, a, w3, layer, epilogue, extras, out_dtypes, a_t=False, out_t=False):
    k, m = a.shape if a_t else a.shape[::-1]
    _, kw, n = w3.shape
    assert kw == k
    tn = _tile(n, 1024)
    tm = _row_tile(m, k, tn, [jnp.dtype(t).itemsize for t in out_dtypes]
                   + [e.dtype.itemsize for e in extras if e.size == m * n])
    ne, no = len(extras), len(out_dtypes)
    a_dim = 0 if a_t else 1

    def body(a_ref, w_ref, *rest):
        ex, outs = rest[:ne], rest[ne:ne + no]
        if out_t:
            acc = lax.dot_general(w_ref[...], a_ref[...], (((0,), (a_dim,)), ((), ())), preferred_element_type=F32)
        else:
            acc = lax.dot_general(a_ref[...], w_ref[...], (((a_dim,), (0,)), ((), ())), preferred_element_type=F32)
        for o, r in zip(outs, epilogue(acc, *[e[...] for e in ex])):
            o[...] = r.astype(o.dtype)

    if out_t:
        tile = pl.BlockSpec((tn, tm), lambda i, j: (j, i))
        vec = pl.BlockSpec((tn, 1), lambda i, j: (j, 0))
        out_shape = (n, m)
    else:
        tile = pl.BlockSpec((tm, tn), lambda i, j: (i, j))
        vec = pl.BlockSpec((1, tn), lambda i, j: (0, j))
        out_shape = (m, n)
    a_spec = pl.BlockSpec((k, tm), lambda i, j: (0, i)) if a_t else pl.BlockSpec((tm, k), lambda i, j: (i, 0))
    return pl.pallas_call(
        body, name=name, grid=(m // tm, n // tn),
        in_specs=[a_spec, pl.BlockSpec((None, k, tn), lambda i, j: (layer, 0, j))]
        + [tile if e.shape == out_shape else vec for e in extras],
        out_specs=[tile] * no,
        out_shape=[jax.ShapeDtypeStruct(out_shape, d) for d in out_dtypes],
        compiler_params=_params(("parallel", "parallel")),
    )(a, w3, *extras)


def mm_nt(name, dy, w3, layer, epilogue, extras, out_dtypes, a_t=False, out_t=False, n_sums=0, exchange=None):
    n, m = dy.shape if a_t else dy.shape[::-1]
    _, k, nw = w3.shape
    assert nw == n and not (out_t and n_sums)
    tko = _tile(k, 1024)
    tm = _row_tile(m, n, tko, [jnp.dtype(t).itemsize for t in out_dtypes]
                   + [e.dtype.itemsize for e in extras if e.size == m * k])
    ne, no = len(extras), len(out_dtypes)
    a_dim = 0 if a_t else 1

    def body(a_ref, w_ref, *rest):
        ex = rest[:ne]
        at = lambda step: (pl.program_id(0) == step[0]) & (pl.program_id(1) == step[1])
        results, _, start_carried, wait_carried = _carried(exchange, rest[ne:], no + n_sums, 0, at((0, 0)),
                                                           at((m // tm - 1, k // tko - 1)))
        outs, sums = results[:no], results[no:]
        start_carried()
        if out_t:
            acc = lax.dot_general(w_ref[...], a_ref[...], (((1,), (a_dim,)), ((), ())), preferred_element_type=F32)
        else:
            acc = lax.dot_general(a_ref[...], w_ref[...], (((a_dim,), (1,)), ((), ())), preferred_element_type=F32)
        res = epilogue(acc, *[e[...] for e in ex])
        for o, v in zip(outs, res):
            o[...] = v.astype(o.dtype)
        if n_sums:
            @pl.when(pl.program_id(0) == 0)
            def _():
                for o in sums:
                    o[...] = jnp.zeros_like(o)

            for o, v in zip(sums, res[no:]):
                o[...] += v
        wait_carried()

    if out_t:
        tile = pl.BlockSpec((tko, tm), lambda i, ko: (ko, i))
        out_shape = (k, m)
    else:
        tile = pl.BlockSpec((tm, tko), lambda i, ko: (i, ko))
        out_shape = (m, k)
    vec = pl.BlockSpec((1, tko), lambda i, ko: (0, ko))
    a_spec = pl.BlockSpec((n, tm), lambda i, ko: (0, i)) if a_t else pl.BlockSpec((tm, n), lambda i, ko: (i, 0))
    hbm = pl.BlockSpec(memory_space=pl.ANY)
    c_ins, c_outs, c_sems = (exchange.ins, exchange.out_shapes, exchange.sems) if exchange else ([], [], [])
    sequential = n_sums or exchange
    return pl.pallas_call(
        body, name=name, grid=(m // tm, k // tko),
        in_specs=[a_spec, pl.BlockSpec((None, tko, n), lambda i, ko: (layer, ko, 0))]
        + [tile if e.shape == out_shape else vec for e in extras] + [hbm] * len(c_ins),
        out_specs=[tile] * no + [vec] * n_sums + [hbm] * len(c_outs),
        out_shape=[jax.ShapeDtypeStruct(out_shape, d) for d in out_dtypes] + [jax.ShapeDtypeStruct((1, k), F32)] * n_sums
        + c_outs,
        scratch_shapes=c_sems,
        compiler_params=_params(("arbitrary" if sequential else "parallel", "arbitrary" if exchange else "parallel")),
    )(dy, w3, *extras, *c_ins)


def mm_tn(name, x, dy, gbuf, shape, layer, x_t=False, dy_t=False):
    k, s = x.shape if x_t else x.shape[::-1]
    _, kw, n = shape
    assert kw == k and dy.shape == ((n, s) if dy_t else (s, n))
    tkk = _tile(k, 512)
    tn = _tile(n, 1024)

    def body(x_ref, dy_ref, *rest):
        g_out = rest[-1]
        g_out[...] = lax.dot_general(x_ref[...], dy_ref[...], (((1 if x_t else 0,), (1 if dy_t else 0,)), ((), ())),
                                     preferred_element_type=F32).astype(g_out.dtype)

    prev = [] if gbuf is None else [gbuf]
    x_spec = pl.BlockSpec((tkk, s), lambda ki, j: (ki, 0)) if x_t else pl.BlockSpec((s, tkk), lambda ki, j: (0, ki))
    dy_spec = pl.BlockSpec((tn, s), lambda ki, j: (j, 0)) if dy_t else pl.BlockSpec((s, tn), lambda ki, j: (0, j))
    return pl.pallas_call(
        body, name=name, grid=(k // tkk, n // tn),
        in_specs=[x_spec, dy_spec] + [pl.BlockSpec(memory_space=pl.ANY)] * len(prev),
        out_specs=pl.BlockSpec((None, tkk, tn), lambda ki, j: (layer, ki, j)),
        out_shape=jax.ShapeDtypeStruct(shape, BF16),
        input_output_aliases={2: 0} if prev else {},
        compiler_params=_params(("parallel", "parallel")),
    )(x, dy, *prev)


def rms_fwd(name, h, g):
    s, d = h.shape
    tr = _pick(s, (ROW_TILE, 256, 128))

    def body(h_ref, g_ref, o_ref):
        x = h_ref[...]
        r = lax.rsqrt(jnp.mean(x * x, axis=-1, keepdims=True) + EPS)
        o_ref[...] = (x * r * g_ref[...]).astype(o_ref.dtype)

    return pl.pallas_call(
        body, name=name, grid=(s // tr,),
        in_specs=[pl.BlockSpec((tr, d), lambda i: (i, 0)), pl.BlockSpec((1, d), lambda i: (0, 0))],
        out_specs=pl.BlockSpec((tr, d), lambda i: (i, 0)),
        out_shape=jax.ShapeDtypeStruct((s, d), BF16),
        compiler_params=_params(("parallel",)),
    )(h, g.reshape(1, d))


def loss_head(h, g, target):
    s, d = h.shape
    tr = _pick(s, (ROW_TILE, 256, 128))

    def body(h_ref, g_ref, t_ref, dx_ref, dxb_ref, dg_ref, loss_ref):
        i = pl.program_id(0)
        x = h_ref[...]
        r = lax.rsqrt(jnp.mean(x * x, axis=-1, keepdims=True) + EPS)
        xh = x * r
        gw = g_ref[...]
        err = xh * gw - t_ref[...]
        dn_ = err * (1.0 / d)
        dyg = dn_ * gw
        dx = r * (dyg - xh * jnp.mean(dyg * xh, axis=-1, keepdims=True))
        dx_ref[...] = dx
        dxb_ref[...] = dx.astype(BF16)

        @pl.when(i == 0)
        def _():
            dg_ref[...] = jnp.zeros_like(dg_ref)
            loss_ref[...] = jnp.zeros_like(loss_ref)

        dg_ref[...] += jnp.sum(dn_ * xh, axis=0, keepdims=True)
        per_row = jnp.sum(err * err, axis=-1, keepdims=True) * (0.5 / d)
        loss_ref[...] += jnp.broadcast_to(jnp.sum(per_row, axis=0, keepdims=True), loss_ref.shape)

    row = pl.BlockSpec((tr, d), lambda i: (i, 0))
    vec = pl.BlockSpec((1, d), lambda i: (0, 0))
    return pl.pallas_call(
        body, name="loss_head", grid=(s // tr,),
        in_specs=[row, vec, row],
        out_specs=[row, row, vec, pl.BlockSpec((1, 128), lambda i: (0, 0))],
        out_shape=[jax.ShapeDtypeStruct((s, d), F32), jax.ShapeDtypeStruct((s, d), BF16),
                   jax.ShapeDtypeStruct((1, d), F32), jax.ShapeDtypeStruct((1, 128), F32)],
        compiler_params=_params(("arbitrary",)),
    )(h, g.reshape(1, d), target)


def colsum(name, x):
    s, n = x.shape
    tr = _pick(s, (ROW_TILE, 256, 128))

    def body(x_ref, o_ref):
        @pl.when(pl.program_id(0) == 0)
        def _():
            o_ref[...] = jnp.zeros_like(o_ref)

        o_ref[...] += jnp.sum(x_ref[...].astype(F32), axis=0, keepdims=True)

    return pl.pallas_call(
        body, name=name, grid=(s // tr,),
        in_specs=[pl.BlockSpec((tr, n), lambda i: (i, 0))],
        out_specs=pl.BlockSpec((1, n), lambda i: (0, 0)),
        out_shape=jax.ShapeDtypeStruct((1, n), F32),
        compiler_params=_params(("arbitrary",)),
    )(x)


def rowsum(name, x):
    n, s = x.shape
    ts = _pick(s, (1024, 512, 256, 128))

    def body(x_ref, o_ref):
        @pl.when(pl.program_id(0) == 0)
        def _():
            o_ref[...] = jnp.zeros_like(o_ref)

        o_ref[...] += jnp.sum(x_ref[...].astype(F32), axis=1, keepdims=True)

    return pl.pallas_call(
        body, name=name, grid=(s // ts,),
        in_specs=[pl.BlockSpec((n, ts), lambda i: (0, i))],
        out_specs=pl.BlockSpec((n, 1), lambda i: (0, 0)),
        out_shape=jax.ShapeDtypeStruct((n, 1), F32),
        compiler_params=_params(("arbitrary",)),
    )(x)[:, 0]


def _tri_rows(reverse):
    i = np.arange(SB_BK)
    tri = (i[None, :] >= i[:, None]) if reverse else (i[None, :] <= i[:, None])
    tri = np.concatenate([tri, tri], axis=1)
    return jnp.asarray(np.concatenate([tri, np.ones((8, 2 * SB_BK), bool)], axis=0), BF16)


def _hi_lo_rows(x):
    hi = x.astype(BF16)
    lo = (x - hi.astype(F32)).astype(BF16)
    return jnp.concatenate([hi, lo], axis=0)


def _softplus2(zs):
    neg_abs = lax.bitcast_convert_type(lax.bitcast_convert_type(zs, jnp.uint32) | jnp.uint32(0x80000000), F32)
    return jnp.maximum(zs, 0.0) + jnp.log2(1.0 + jnp.exp2(neg_abs))


def _pair_mask(first_rel_block, bq):
    key = lax.broadcasted_iota(jnp.int32, (2 * SB_BK, bq), 0) + first_rel_block * SB_BK
    qry = lax.broadcasted_iota(jnp.int32, (2 * SB_BK, bq), 1)
    return key < qry


def _row_of(table8, sub8, r):
    return jnp.sum(jnp.where(sub8 == r, table8, 0.0), axis=0, keepdims=True)


def _keys(j0):
    return pl.ds(pl.multiple_of(j0 * SB_BK, 2 * SB_BK), 2 * SB_BK)


class Carry:
    def __init__(self, build, ins, out_shapes, sems, then=None):
        self.build, self.ins, self.out_shapes, self.sems = build, list(ins), list(out_shapes), list(sems)
        self.then = then


def _carried(carry, rest, n_out, n_scratch, first, last):
    n_ci = len(carry.ins) if carry else 0
    n_co = len(carry.out_shapes) if carry else 0
    cin, outs = rest[:n_ci], rest[n_ci:n_ci + n_out]
    cout = rest[n_ci + n_out:n_ci + n_out + n_co]
    scratch = rest[n_ci + n_out + n_co:n_ci + n_out + n_co + n_scratch]
    csems = rest[n_ci + n_out + n_co + n_scratch:]

    def start():
        if carry:
            @pl.when(first)
            def _():
                for cp in carry.build(cin, cout, *csems):
                    cp.start()

    def wait():
        if carry:
            @pl.when(last)
            def _():
                for cp in carry.build(cin, cout, *csems):
                    cp.wait()
                if carry.then:
                    second = carry.then(cin, cout, *csems)
                    for cp in second:
                        cp.start()
                    for cp in second:
                        cp.wait()

    return outs, scratch, start, wait


def _contract0(a, b):
    return lax.dot_general(a, b, (((0,), (0,)), ((), ())), preferred_element_type=F32)


def _contract1(a, b):
    return lax.dot_general(a, b, (((1,), (1,)), ((), ())), preferred_element_type=F32)


def sb_fwd(name, qkvt, exchange=None):
    nh, dh, s = qkvt.shape[0] // 3, qkvt.shape[1], qkvt.shape[2]
    bq = SB_BQ
    per_q = bq // SB_BK
    nkb = s // SB_BK
    assert s % bq == 0 and per_q == 4 and nkb % 8 == 0

    def body(q_ref, k_ref, v_ref, a_ref, *rest):
        head = pl.program_id(0)
        (o_ref, rtab_ref), (acc, zbuf, wbuf), start_carried, wait_carried = _carried(
            exchange, rest, 2, 3, head == 0, head == nh - 1)
        start_carried()
        tri = a_ref[...]
        sub8 = lax.broadcasted_iota(jnp.int32, (8, bq), 0)
        rtab_ref[...] = jnp.full(rtab_ref.shape, SB_UNSEEN, F32)
        kf = k_ref[...].astype(F32)
        k_max2 = jnp.max(jnp.sum(kf * kf, axis=0, keepdims=True), axis=1, keepdims=True)

        def query_block(i, _):
            lanes = pl.ds(pl.multiple_of(i * bq, bq), bq)
            qb = q_ref[:, lanes] * Q_SCALE
            acc[...] = jnp.zeros_like(acc)
            qf = qb.astype(F32)
            bound = jnp.sqrt(jnp.sum(qf * qf, axis=0, keepdims=True) * k_max2) * (1.001 * LOG2E)

            def scores(j0):
                return _contract0(k_ref[:, _keys(j0)], qb) * LOG2E

            def pair(j0, slot, run, rt8, mask, has_prev):
                zs = zbuf[slot]
                zbuf[1 - slot] = scores(jnp.maximum(j0 - 2, 0))
                if has_prev:
                    acc[...] += jnp.dot(v_ref[:, _keys(j0 + 2)], wbuf[1 - slot], preferred_element_type=F32)
                p = _softplus2(zs)
                if mask is not None:
                    p = jnp.where(mask, p, 0.0)
                cr1 = jnp.dot(tri, _hi_lo_rows(p[SB_BK:]), preferred_element_type=F32)
                cr0 = jnp.dot(tri, _hi_lo_rows(p[:SB_BK]), preferred_element_type=F32)
                run1 = run + cr1[SB_BK:SB_BK + 1]
                w = jnp.exp2(jnp.concatenate([zs[:SB_BK] - cr0[:SB_BK] - run1, zs[SB_BK:] - cr1[:SB_BK] - run],
                                             axis=0))
                if mask is not None:
                    w = jnp.where(mask, w, 0.0)
                wbuf[slot] = w.astype(BF16)
                rt8 = jnp.where(j0 % 8 == 6, SB_UNSEEN, rt8)
                rt8 = jnp.where(sub8 == (j0 + 1) % 8, run, jnp.where(sub8 == j0 % 8, run1, rt8))
                rtab_ref[pl.ds(pl.multiple_of((j0 // 8) * 8, 8), 8), lanes] = rt8
                return run1 + cr0[SB_BK:SB_BK + 1], rt8

            def alive(run):
                return jnp.min(run - bound) < SB_DEAD

            top = i * per_q
            zbuf[0] = scores(top + 2)
            state = (jnp.zeros((1, bq), F32), jnp.full((8, bq), SB_UNSEEN, F32))
            state = pair(top + 2, 0, *state, _pair_mask(2, bq), False)
            state = pair(top, 1, *state, _pair_mask(0, bq), True)

            def step(c):
                it, pairs, _, run, rt8 = c
                j0 = top - 2 - 4 * it
                run, rt8 = pair(j0, 0, run, rt8, None, True)
                go = alive(run)
                run, rt8 = lax.cond(go, lambda r, t: pair(j0 - 2, 1, r, t, None, True), lambda r, t: (r, t), run, rt8)
                return it + 1, pairs + 1 + go.astype(jnp.int32), go & alive(run), run, rt8

            pairs = lax.while_loop(lambda c: (c[0] < i) & c[2], step, (0, 0, alive(state[0]), *state))[1]
            acc[...] += jnp.dot(v_ref[:, _keys(top - 2 * pairs)], wbuf[(pairs + 1) % 2], preferred_element_type=F32)
            o_ref[:, lanes] = acc[...].astype(o_ref.dtype)
            return 0

        lax.fori_loop(0, s // bq, query_block, 0)
        wait_carried()

    def head_spec(offset, rows):
        return pl.BlockSpec((None, rows, s), lambda h: (h + offset, 0, 0))

    hbm = pl.BlockSpec(memory_space=pl.ANY)
    c_ins, c_outs, c_sems = (exchange.ins, exchange.out_shapes, exchange.sems) if exchange else ([], [], [])
    outs = pl.pallas_call(
        body, name=name, grid=(nh,),
        in_specs=[head_spec(0, dh), head_spec(nh, dh), head_spec(2 * nh, dh),
                  pl.BlockSpec((SB_BK + 8, 2 * SB_BK), lambda h: (0, 0))] + [hbm] * len(c_ins),
        out_specs=[head_spec(0, dh), head_spec(0, nkb)] + [hbm] * len(c_outs),
        out_shape=[jax.ShapeDtypeStruct((nh, dh, s), BF16), jax.ShapeDtypeStruct((nh, nkb, s), F32)] + c_outs,
        scratch_shapes=[pltpu.VMEM((dh, bq), F32), pltpu.VMEM((2, 2 * SB_BK, bq), F32),
                        pltpu.VMEM((2, 2 * SB_BK, bq), BF16)] + c_sems,
        compiler_params=_params(("arbitrary",)),
    )(qkvt, qkvt, qkvt, _tri_rows(True), *c_ins)
    return outs[0], outs[1], outs[2:]


def sb_bwd(name, qkvt, dot_, rtab, exchange=None):
    nh, dh, s = qkvt.shape[0] // 3, qkvt.shape[1], qkvt.shape[2]
    bq = SB_BQ
    per_q = bq // SB_BK
    nkb = s // SB_BK

    def body(qt_ref, kt_ref, vt_ref, dot_ref, rtab_ref, ar_ref, af_ref, *rest):
        head = pl.program_id(0)
        (dqkv_ref,), (dq_acc, dk_acc, dv_acc, zbuf, dwbuf, dzbuf, wbuf), start_carried, wait_carried = \
            _carried(exchange, rest, 1, 7, head == 0, head == nh - 1)
        dq_ref, dk_ref, dv_ref = dqkv_ref.at[0], dqkv_ref.at[1], dqkv_ref.at[2]
        start_carried()
        dk_acc[...] = jnp.zeros_like(dk_acc)
        dv_acc[...] = jnp.zeros_like(dv_acc)
        tri_rev = ar_ref[...][:SB_BK]
        tri_fwd = af_ref[...]
        sub8 = lax.broadcasted_iota(jnp.int32, (8, bq), 0)

        def query_block(i, _):
            lanes = pl.ds(pl.multiple_of(i * bq, bq), bq)
            qtb = qt_ref[:, lanes] * Q_SCALE
            dotb = dot_ref[:, lanes]
            dq_acc[...] = jnp.zeros_like(dq_acc)
            last_j = i * per_q + 2
            seen = jnp.max(jnp.where(rtab_ref[:, lanes] < 0.1 * SB_UNSEEN, 1.0, 0.0), axis=1, keepdims=True)
            pairs = jnp.clip((jnp.sum(seen).astype(jnp.int32) - per_q) // 2, 0, 2 * i)
            odd = pairs % 2
            first_j = i * per_q - 2 * pairs

            def issue(j0, slot):
                zbuf[slot] = _contract0(kt_ref[:, _keys(j0)], qtb) * LOG2E
                dwbuf[slot] = _contract0(vt_ref[:, _keys(j0)], dotb)

            def retire(j0, slot):
                keys = _keys(j0)
                dq_acc[...] += jnp.dot(kt_ref[:, keys], dzbuf[slot], preferred_element_type=F32)
                dk_acc[:, keys] += _contract1(qtb, dzbuf[slot])
                dv_acc[:, keys] += _contract1(dotb, wbuf[slot])

            def pair(j0, slot, g_run, mask):
                zs = zbuf[slot]
                dw = dwbuf[slot]
                issue(jnp.minimum(j0 + 2, last_j), 1 - slot)
                retire(jnp.maximum(j0 - 2, first_j), 1 - slot)
                p_raw = _softplus2(zs)
                p = p_raw if mask is None else jnp.where(mask, p_raw, 0.0)
                c0 = jnp.dot(tri_rev, _hi_lo_rows(p[:SB_BK]), preferred_element_type=F32)
                c1 = jnp.dot(tri_rev, _hi_lo_rows(p[SB_BK:]), preferred_element_type=F32)
                rt8 = rtab_ref[pl.ds(pl.multiple_of((j0 // 8) * 8, 8), 8), lanes]
                r0 = _row_of(rt8, sub8, j0 % 8)
                r1 = _row_of(rt8, sub8, (j0 + 1) % 8)
                w = jnp.exp2(jnp.concatenate([zs[:SB_BK] - c0 - r0, zs[SB_BK:] - c1 - r1], axis=0))
                if mask is not None:
                    w = jnp.where(mask, w, 0.0)
                g = w * dw
                gg0 = jnp.dot(tri_fwd, _hi_lo_rows(g[:SB_BK]), preferred_element_type=F32)
                gg1 = jnp.dot(tri_fwd, _hi_lo_rows(g[SB_BK:]), preferred_element_type=F32)
                g_run1 = g_run + gg0[SB_BK:SB_BK + 1]
                g_pre = jnp.concatenate([gg0[:SB_BK] + g_run, gg1[:SB_BK] + g_run1], axis=0)
                dz = g - jnp.exp2(zs - p_raw) * g_pre
                if mask is not None:
                    dz = jnp.where(mask, dz, 0.0)
                dzbuf[slot] = dz.astype(BF16)
                wbuf[slot] = w.astype(BF16)
                return g_run1 + gg1[SB_BK:SB_BK + 1]

            issue(first_j, odd)
            dzbuf[...] = jnp.zeros(dzbuf.shape, BF16)
            wbuf[...] = jnp.zeros(wbuf.shape, BF16)

            def step(it, g_run):
                g_run = pair(4 * it, 0, g_run, None)
                return pair(4 * it + 2, 1, g_run, None)

            g_run = lax.cond(odd == 1, lambda g: pair(first_j, 1, g, None), lambda g: g, jnp.zeros((1, bq), F32))
            g_run = lax.fori_loop(i - pairs // 2, i, step, g_run)
            g_run = pair(last_j - 2, 0, g_run, _pair_mask(0, bq))
            pair(last_j, 1, g_run, _pair_mask(2, bq))
            retire(last_j, 1)
            dq_ref[:, lanes] = (dq_acc[...] * Q_SCALE).astype(dq_ref.dtype)
            return 0

        lax.fori_loop(0, s // bq, query_block, 0)
        dk_ref[...] = dk_acc[...].astype(dk_ref.dtype)
        dv_ref[...] = dv_acc[...].astype(dv_ref.dtype)
        wait_carried()

    def head_spec(offset, rows):
        return pl.BlockSpec((None, rows, s), lambda h: (h + offset, 0, 0))

    aspec = pl.BlockSpec((SB_BK + 8, 2 * SB_BK), lambda h: (0, 0))
    pair_f32 = pltpu.VMEM((2, 2 * SB_BK, bq), F32)
    pair_bf16 = pltpu.VMEM((2, 2 * SB_BK, bq), BF16)
    hbm = pl.BlockSpec(memory_space=pl.ANY)
    c_ins, c_outs, c_sems = (exchange.ins, exchange.out_shapes, exchange.sems) if exchange else ([], [], [])
    outs = pl.pallas_call(
        body, name=name, grid=(nh,),
        in_specs=[head_spec(0, dh), head_spec(nh, dh), head_spec(2 * nh, dh), head_spec(0, dh), head_spec(0, nkb),
                  aspec, aspec] + [hbm] * len(c_ins),
        out_specs=[pl.BlockSpec((3, None, dh, s), lambda h: (0, h, 0, 0))] + [hbm] * len(c_outs),
        out_shape=[jax.ShapeDtypeStruct((3, nh, dh, s), BF16)] + c_outs,
        scratch_shapes=[pltpu.VMEM((dh, bq), F32), pltpu.VMEM((dh, s), F32), pltpu.VMEM((dh, s), F32),
                        pair_f32, pair_f32, pair_bf16, pair_bf16] + c_sems,
        compiler_params=_params(("arbitrary",)),
    )(qkvt, qkvt, qkvt, dot_, rtab, _tri_rows(True), _tri_rows(False), *c_ins)
    return outs[0], outs[1:]


SWA_QB = 2


def _band_valid():
    kj = np.arange(2 * WINDOW)[:, None]
    dist = (np.arange(8 * WINDOW)[None, :] % WINDOW) + WINDOW - kj
    inside = (dist >= 0) & (dist < WINDOW)
    return jnp.asarray(np.stack([inside & (kj >= WINDOW), inside]), F32)


def _swa_probs(qt, kt, bias_t, valid, sink):
    sc = jnp.where(valid > 0.5, _contract0(kt, qt) + bias_t, NEG_INF)
    mx = jnp.maximum(jnp.max(sc, axis=0, keepdims=True), sink)
    p = jnp.exp(sc - mx)
    p_sink = jnp.exp(sink - mx)
    inv = 1.0 / (jnp.sum(p, axis=0, keepdims=True) + p_sink)
    return p, p_sink, inv


def _band(i):
    return pl.ds(pl.multiple_of(i * WINDOW, WINDOW), 2 * WINDOW)


def _heads_to_lanes(blk):
    return jnp.concatenate([blk[r * HEAD_DIM:(r + 1) * HEAD_DIM] for r in range(8)], axis=1)


def _lanes_to_heads(t):
    return jnp.concatenate([t[:, r * WINDOW:(r + 1) * WINDOW] for r in range(8)], axis=0)


def swa_fwd(name, qt, kpt, vpt, bias_t, sink_row):
    d, s = qt.shape
    ng, dh, sp = kpt.shape
    rows, cols = d // ng, SWA_QB * WINDOW
    assert (s // WINDOW) % SWA_QB == 0

    def body(q_ref, k_ref, v_ref, bias_ref, valid_ref, sink_ref, o_ref):
        for u in range(SWA_QB):
            i = pl.program_id(1) * SWA_QB + u
            lanes = slice(u * WINDOW, (u + 1) * WINDOW)
            qb = _heads_to_lanes(q_ref[:, lanes]) * Q_SCALE
            p, _, inv = _swa_probs(qb, k_ref[:, _band(i)], bias_ref[...], valid_ref[jnp.minimum(i, 1)], sink_ref[...])
            o_t = jnp.dot(v_ref[:, _band(i)], p.astype(BF16), preferred_element_type=F32) * inv
            o_ref[:, lanes] = _lanes_to_heads(o_t).astype(o_ref.dtype)

    qspec = pl.BlockSpec((rows, cols), lambda g, i: (g, i))
    kspec = pl.BlockSpec((None, dh, sp), lambda g, i: (g, 0, 0))
    return pl.pallas_call(
        body, name=name, grid=(ng, s // cols),
        in_specs=[qspec, kspec, kspec, pl.BlockSpec((None, 2 * WINDOW, 8 * WINDOW), lambda g, i: (g, 0, 0)),
                  pl.BlockSpec((2, 2 * WINDOW, 8 * WINDOW), lambda g, i: (0, 0, 0)),
                  pl.BlockSpec((None, 1, 8 * WINDOW), lambda g, i: (g, 0, 0))],
        out_specs=qspec,
        out_shape=jax.ShapeDtypeStruct(qt.shape, BF16),
        compiler_params=_params(("parallel", "arbitrary")),
    )(qt, kpt, vpt, bias_t, _band_valid(), sink_row)


def swa_bwd(name, qt, kpt, vpt, bias_t, sink_row, dot_, dk_in, dv_in):
    d, s = qt.shape
    ng, dh, sp = kpt.shape
    rows, cols = d // ng, SWA_QB * WINDOW

    def body(q_ref, k_ref, v_ref, bias_ref, valid_ref, sink_ref, do_ref, dki_ref, dvi_ref,
             dq_ref, dk_ref, dv_ref, db_ref, ds_ref):
        @pl.when(pl.program_id(1) == 0)
        def _():
            dk_ref[...] = dki_ref[...]
            dv_ref[...] = dvi_ref[...]
            db_ref[...] = jnp.zeros_like(db_ref)
            ds_ref[...] = jnp.zeros_like(ds_ref)

        for u in range(SWA_QB):
            i = pl.program_id(1) * SWA_QB + u
            band = _band(i)
            lanes = slice(u * WINDOW, (u + 1) * WINDOW)
            qb = _heads_to_lanes(q_ref[:, lanes]) * Q_SCALE
            dob = _heads_to_lanes(do_ref[:, lanes])
            kt = k_ref[:, band]
            p, p_sink, inv = _swa_probs(qb, kt, bias_ref[...], valid_ref[jnp.minimum(i, 1)], sink_ref[...])
            p = p * inv
            dp = _contract0(v_ref[:, band], dob)
            delta = jnp.sum(p * dp, axis=0, keepdims=True)
            dsc = p * (dp - delta)
            ds_ref[...] -= p_sink * inv * delta
            db_ref[...] += dsc
            dscb = dsc.astype(BF16)
            dq_t = jnp.dot(kt, dscb, preferred_element_type=F32) * Q_SCALE
            dq_ref[:, lanes] = _lanes_to_heads(dq_t).astype(dq_ref.dtype)
            dk_ref[:, band] += _contract1(qb, dscb)
            dv_ref[:, band] += _contract1(dob, p.astype(BF16))

    qspec = pl.BlockSpec((rows, cols), lambda g, i: (g, i))
    kspec = pl.BlockSpec((None, dh, sp), lambda g, i: (g, 0, 0))
    bspec = pl.BlockSpec((None, 2 * WINDOW, 8 * WINDOW), lambda g, i: (g, 0, 0))
    sspec = pl.BlockSpec((None, 1, 8 * WINDOW), lambda g, i: (g, 0, 0))
    return pl.pallas_call(
        body, name=name, grid=(ng, s // cols),
        in_specs=[qspec, kspec, kspec, bspec, pl.BlockSpec((2, 2 * WINDOW, 8 * WINDOW), lambda g, i: (0, 0, 0)), sspec,
                  qspec, kspec, kspec],
        out_specs=[qspec, kspec, kspec, bspec, sspec],
        out_shape=[jax.ShapeDtypeStruct(qt.shape, BF16), jax.ShapeDtypeStruct(kpt.shape, F32),
                   jax.ShapeDtypeStruct(kpt.shape, F32), jax.ShapeDtypeStruct(bias_t.shape, F32),
                   jax.ShapeDtypeStruct(sink_row.shape, F32)],
        compiler_params=_params(("parallel", "arbitrary")),
    )(qt, kpt, vpt, bias_t, _band_valid(), sink_row, dot_, dk_in, dv_in)


def _bucket_onehot():
    qi = np.arange(WINDOW)[:, None]
    kj = np.arange(2 * WINDOW)[None, :]
    n = np.maximum(qi + WINDOW - kj, 0)
    max_exact = N_BUCKETS // 2
    nf = np.maximum(n, 1).astype(np.float64)
    val = np.log(nf / max_exact) / math.log(WINDOW / max_exact) * (N_BUCKETS - max_exact)
    assert np.all(np.abs(val - np.round(val))[(n > max_exact) & (n < WINDOW)] > 1e-3)
    large = np.minimum(max_exact + val.astype(np.int64), N_BUCKETS - 1)
    bucket = np.where(n < max_exact, n, large).reshape(-1)
    onehot = np.zeros((128, bucket.size), np.float32)
    onehot[bucket, np.arange(bucket.size)] = 1.0
    return onehot


def _split3(x):
    a = x.astype(BF16)
    r = x - a.astype(F32)
    b = r.astype(BF16)
    c = (r - b.astype(F32)).astype(BF16)
    return a, b, c


def bias_table(rel_bias):
    nh = rel_bias.shape[1]
    oh = jnp.asarray(_bucket_onehot(), BF16)
    n = oh.shape[1]
    tn = 4096
    rb = jnp.zeros((nh, 128), F32).at[:, :N_BUCKETS].set(rel_bias.T)

    def body(rb_ref, oh_ref, o_ref):
        o_ref[...] = sum(jnp.dot(t, oh_ref[...], preferred_element_type=F32) for t in _split3(rb_ref[...]))

    return pl.pallas_call(
        body, name="bias_table", grid=(n // tn,),
        in_specs=[pl.BlockSpec((nh, 128), lambda i: (0, 0)), pl.BlockSpec((128, tn), lambda i: (0, i))],
        out_specs=pl.BlockSpec((nh, tn), lambda i: (0, i)),
        out_shape=jax.ShapeDtypeStruct((nh, n), F32),
        compiler_params=_params(("parallel",)),
    )(rb, oh)


def bias_table_grad(db0, db1):
    nh, n = db0.shape
    oh = jnp.asarray(_bucket_onehot(), BF16)
    tn = 4096

    def body(a_ref, b_ref, oh_ref, o_ref):
        @pl.when(pl.program_id(0) == 0)
        def _():
            o_ref[...] = jnp.zeros_like(o_ref)

        o_ref[...] += sum(lax.dot_general(t, oh_ref[...], (((1,), (1,)), ((), ())), preferred_element_type=F32)
                          for t in _split3(a_ref[...] + b_ref[...]))

    blk = pl.BlockSpec((nh, tn), lambda i: (0, i))
    return pl.pallas_call(
        body, name="bias_table_grad", grid=(n // tn,),
        in_specs=[blk, blk, pl.BlockSpec((128, tn), lambda i: (0, i))],
        out_specs=pl.BlockSpec((nh, 128), lambda i: (0, 0)),
        out_shape=jax.ShapeDtypeStruct((nh, 128), F32),
        compiler_params=_params(("arbitrary",)),
    )(db0, db1, oh)


def _owner_view(ref, name, d):
    if name == 'a_norm':
        return ref.at[d]
    if name in COL_SHARDED:
        n = ref.shape[2] // N_DEV
        return ref.at[:, :, pl.ds(pl.multiple_of(d * n, 128), n)]
    return ref.at[:, d]


def _place():
    return lax.axis_index("x"), lax.axis_index("y"), lax.axis_index("c")


def _dev(p):
    return 4 * p[0] + 2 * p[1] + p[2]


def _remote(src, dst, send_sem, recv_sem, to):
    return pltpu.make_async_remote_copy(src_ref=src, dst_ref=dst, send_sem=send_sem, recv_sem=recv_sem,
                                        device_id=to, device_id_type=MESH)


def _dma_sems(*shapes):
    return [pltpu.SemaphoreType.DMA(sh) for sh in shapes]


def comm_call(name, build, ins, out_shapes, sems, aliases=None):
    n_in, n_out = len(ins), len(out_shapes)

    def body(*refs):
        copies = build(refs[:n_in], refs[n_in:n_in + n_out], *refs[n_in + n_out:])
        for cp in copies:
            cp.start()
        for cp in copies:
            cp.wait()

    hbm = pl.BlockSpec(memory_space=pl.ANY)
    return pl.pallas_call(
        body, name=name, in_specs=[hbm] * n_in, out_specs=[hbm] * n_out, out_shape=list(out_shapes),
        scratch_shapes=sems, input_output_aliases=aliases or {},
    )(*ins)


def all_gather_weights(names, shards, full_shapes):
    n = len(names)

    def body(*refs):
        ins, outs = refs[:n], refs[n:2 * n]
        send_sems, recv_sems, local_sems = refs[2 * n:]
        x, y, c = _place()
        me, sibling = (x, y, c), (x, y, 1 - c)
        chips = [(1 - x, y), (x, 1 - y), (1 - x, 1 - y)]

        def copy(t, k, block, to, src=None):
            dst = _owner_view(outs[t], names[t], _dev(block))
            return _remote(dst if src is None else src, dst, send_sems.at[t, k], recv_sems.at[t, k], to)

        mine = [pltpu.make_async_copy(ins[t], _owner_view(outs[t], names[t], _dev(me)), local_sems.at[t])
                for t in range(n)]
        for cp in mine:
            cp.start()
        first = []
        for t in range(n):
            first.append(copy(t, 0, me, sibling, src=ins[t]))
            first += [copy(t, 1 + j, me, (*chip, c), src=ins[t]) for j, chip in enumerate(chips)]
        for cp in first:
            cp.start()
        passed = []
        for j, chip in enumerate(chips):
            for t in range(n):
                copy(t, 1 + j, (*chip, c), me).wait_recv()
                fwd = copy(t, 4 + j, (*chip, c), sibling)
                fwd.start()
                passed.append(fwd)
        for t in range(n):
            copy(t, 0, sibling, me).wait_recv()
            for j, chip in enumerate(chips):
                copy(t, 4 + j, (*chip, 1 - c), me).wait_recv()
        for cp in first + passed:
            cp.wait_send()
        for cp in mine:
            cp.wait()

    hbm = pl.BlockSpec(memory_space=pl.ANY)
    return pl.pallas_call(
        body, name="all_gather_layer0",
        in_specs=[hbm] * n, out_specs=[hbm] * n,
        out_shape=[jax.ShapeDtypeStruct(full_shapes[t], shards[t].dtype) for t in range(n)],
        scratch_shapes=_dma_sems((n, 7), (n, 7), (n,)),
    )(*shards)


def ag_direct(names, shards, full_shapes):
    n = len(names)

    def build(ins, outs, send_sems, recv_sems, local_sems, fwd_send_sems, fwd_recv_sems):
        x, y, c = _place()
        peers = [(x, y, 1 - c), (1 - x, y, c), (x, 1 - y, c), (1 - x, 1 - y, c)]
        copies = []
        for t in range(n):
            dst = _owner_view(outs[t], names[t], _dev((x, y, c)))
            copies.append(pltpu.make_async_copy(ins[t], dst, local_sems.at[t]))
            copies += [_remote(ins[t], dst, send_sems.at[t, k], recv_sems.at[t, k], to) for k, to in enumerate(peers)]
        return copies

    def forward(ins, outs, send_sems, recv_sems, local_sems, fwd_send_sems, fwd_recv_sems):
        x, y, c = _place()
        copies = []
        for t in range(n):
            for k, chip in enumerate([(1 - x, y), (x, 1 - y), (1 - x, 1 - y)]):
                view = _owner_view(outs[t], names[t], _dev((*chip, c)))
                copies.append(_remote(view, view, fwd_send_sems.at[t, k], fwd_recv_sems.at[t, k], (x, y, 1 - c)))
        return copies

    return Carry(build, shards, [jax.ShapeDtypeStruct(full_shapes[t], shards[t].dtype) for t in range(n)],
                 _dma_sems((n, 4), (n, 4), (n,), (n, 3), (n, 3)), then=forward)


def sibling_exchange(names, grads, part_shapes):
    n = len(names)

    def build(ins, outs, send_sems, recv_sems):
        x, y, c = _place()
        return [_remote(_owner_view(ins[t], names[t], 2 * q + 1 - c), outs[t].at[q], send_sems.at[t, q],
                        recv_sems.at[t, q], (x, y, 1 - c)) for t in range(n) for q in range(4)]

    return Carry(build, grads, [jax.ShapeDtypeStruct((4,) + part_shapes[t], BF16) for t in range(n)],
                 _dma_sems((n, 4), (n, 4)))


def chip_exchange(names, parts, part_shapes):
    n = len(names)

    def build(ins, outs, send_sems, recv_sems):
        x, y, c = _place()
        chips = [(1 - x, y), (x, 1 - y), (1 - x, 1 - y)]
        return [_remote(ins[t].at[2 * chip[0] + chip[1]], outs[t].at[k], send_sems.at[t, k], recv_sems.at[t, k],
                        (*chip, c)) for t in range(n) for k, chip in enumerate(chips)]

    return Carry(build, parts, [jax.ShapeDtypeStruct((3,) + part_shapes[t], BF16) for t in range(n)],
                 _dma_sems((n, 3), (n, 3)))


def all_gather_rows(x):
    r, w = x.shape

    def body(x_ref, out_ref, send_sems, recv_sems, local_sem):
        px, py, pc = _place()
        me = 4 * px + 2 * py + pc
        mine = pltpu.make_async_copy(x_ref, out_ref.at[me], local_sem)
        mine.start()
        copies = []
        for k in range(1, N_DEV):
            peer = (px ^ (k >> 2), py ^ ((k >> 1) & 1), pc ^ (k & 1))
            copies.append(pltpu.make_async_remote_copy(
                src_ref=x_ref, dst_ref=out_ref.at[me], send_sem=send_sems.at[k - 1], recv_sem=recv_sems.at[k - 1],
                device_id=peer, device_id_type=MESH))
        for cp in copies:
            cp.start()
        for k in range(1, N_DEV):
            peer_idx = me ^ k
            pltpu.make_async_remote_copy(
                src_ref=x_ref, dst_ref=out_ref.at[peer_idx], send_sem=send_sems.at[k - 1],
                recv_sem=recv_sems.at[k - 1], device_id=(px, py, pc), device_id_type=MESH).wait_recv()
        for cp in copies:
            cp.wait_send()
        mine.wait()

    vmem = pl.BlockSpec(memory_space=pltpu.VMEM)
    return pl.pallas_call(
        body, name="all_gather_small_grads",
        in_specs=[vmem], out_specs=vmem,
        out_shape=jax.ShapeDtypeStruct((N_DEV, r, w), x.dtype),
        scratch_shapes=[pltpu.SemaphoreType.DMA((N_DEV - 1,)), pltpu.SemaphoreType.DMA((N_DEV - 1,)),
                        pltpu.SemaphoreType.DMA],
    )(x)


def _adamw(w, g, m, v):
    m = ADAM_B1 * m + (1.0 - ADAM_B1) * g
    v = ADAM_B2 * v + (1.0 - ADAM_B2) * (g * g)
    m_hat = m / (1.0 - ADAM_B1 ** ADAM_STEP)
    v_hat = v / (1.0 - ADAM_B2 ** ADAM_STEP)
    return -ADAM_LR * (m_hat / (jnp.sqrt(v_hat) + ADAM_EPS) + ADAM_WD * w), m, v


def sibling_sum(name, col, grads, recv, core):
    _, nl, rows, cols = recv.shape
    tr = _tile(rows, 512)
    rspec = pl.BlockSpec((None, None, tr, cols), lambda q, l, i, c_ref: (q, l, i, 0))
    if col:
        gspec = pl.BlockSpec((None, tr, cols), lambda q, l, i, c_ref: (l, i, 2 * q + c_ref[0]))
    else:
        gspec = pl.BlockSpec((None, None, tr, cols), lambda q, l, i, c_ref: (l, 2 * q + c_ref[0], i, 0))

    def body(c_ref, g_ref, r_ref, o_ref):
        del c_ref
        o_ref[...] = (g_ref[...].astype(F32) + r_ref[...].astype(F32)).astype(BF16)

    return pl.pallas_call(
        body, name=name,
        grid_spec=pltpu.PrefetchScalarGridSpec(num_scalar_prefetch=1, grid=(4, nl, rows // tr),
                                               in_specs=[gspec, rspec], out_specs=rspec),
        out_shape=jax.ShapeDtypeStruct(recv.shape, BF16),
        compiler_params=_params(("parallel", "parallel", "parallel")),
    )(core.reshape(1), grads, recv)


def reduce_adamw(name, parts, recv, chip, w, m, v, l0, prev):
    _, nl, rows, cols = parts.shape
    tr = _tile(rows, 256)

    def body(q_ref, p_ref, r_ref, w_ref, m_ref, v_ref, *rest):
        del q_ref
        g_out, d_out, m_out, v_out = rest[-4:]
        g = ((p_ref[...].astype(F32) + r_ref[0].astype(F32)) + r_ref[1].astype(F32)) + r_ref[2].astype(F32)
        d, mn, vn = _adamw(w_ref[...], g, m_ref[...], v_ref[...])
        g_out[...] = g
        d_out[...] = d
        m_out[...] = mn
        v_out[...] = vn

    blk = pl.BlockSpec((None, tr, cols), lambda l, i, q_ref: (l0 + l, i, 0))
    prev = list(prev) if prev else []
    return pl.pallas_call(
        body, name=name,
        grid_spec=pltpu.PrefetchScalarGridSpec(
            num_scalar_prefetch=1, grid=(nl, rows // tr),
            in_specs=[pl.BlockSpec((None, None, tr, cols), lambda l, i, q_ref: (q_ref[0], l, i, 0)),
                      pl.BlockSpec((3, None, tr, cols), lambda l, i, q_ref: (0, l, i, 0)), blk, blk, blk]
            + [pl.BlockSpec(memory_space=pl.ANY)] * len(prev),
            out_specs=[blk] * 4),
        out_shape=[jax.ShapeDtypeStruct(w.shape, F32)] * 4,
        input_output_aliases={6 + i: i for i in range(len(prev))},
        compiler_params=_params(("parallel", "parallel")),
    )(chip.reshape(1), parts, recv, w, m, v, *prev)


def small_adamw(name, gathered, w, m, v):
    _, r, c = gathered.shape

    def body(ga_ref, w_ref, m_ref, v_ref, g_out, d_out, m_out, v_out):
        g = ga_ref[0]
        for d in range(1, N_DEV):
            g = g + ga_ref[d]
        dl, mn, vn = _adamw(w_ref[...], g, m_ref[...], v_ref[...])
        g_out[...] = g
        d_out[...] = dl
        m_out[...] = mn
        v_out[...] = vn

    return pl.pallas_call(
        body, name=name,
        out_shape=[jax.ShapeDtypeStruct((r, c), F32)] * 4,
        compiler_params=_params(),
    )(gathered, w, m, v)


def _rms(x, g):
    return x * lax.rsqrt(jnp.mean(x * x, axis=-1, keepdims=True) + EPS) * g


def _rms_bwd_epilogue(dn, x, dres, g):
    r = lax.rsqrt(jnp.mean(x * x, axis=-1, keepdims=True) + EPS)
    xh = x * r
    dyg = dn * g
    dx = dres + r * (dyg - xh * jnp.mean(dyg * xh, axis=-1, keepdims=True))
    return dx, dx, jnp.sum(dn * xh, axis=0, keepdims=True), jnp.sum(dx, axis=0, keepdims=True)


def _residual_then_norms(n_terms):
    def epilogue(acc, *ex):
        h = acc
        for t in ex[:n_terms]:
            h = h + t
        return (h,) + tuple(_rms(h, g) for g in ex[n_terms:])
    return epilogue


def local_step(x, target, small, ex):
    s, d = x.shape
    n_a, n_b = small['a_norm'].shape[0], small['b_norm'].shape[0]
    sg = {}
    gb = {}

    def fwd_mm(name, a, wname, layer, epilogue, extras, out_dtypes, **kw):
        return mm_nn(name, a, *ex.weight(wname, layer), epilogue, extras, out_dtypes, **kw)

    def dx_mm(name, dy, wname, layer, epilogue, extras, out_dtypes, **kw):
        return mm_nt(name, dy, *ex.weight(wname, layer), epilogue, extras, out_dtypes, **kw)

    def dw_mm(name, a, dy, wname, layer, **kw):
        key, slab, shape = ex.grad(wname, layer)
        gb[key] = mm_tn(name, a, dy, gb.get(key), shape, slab, **kw)

    plain = lambda acc: (acc,)
    plus_col = lambda acc, b: (acc + b,)

    bias_flat = bias_table(small['rel_bias'])
    bias_t = bias_flat.reshape(2, 8, WINDOW, 2 * WINDOW).transpose(0, 3, 1, 2).reshape(2, 2 * WINDOW, 8 * WINDOW)
    sink_rows = [jnp.repeat(small['b_sinks'][j], WINDOW).reshape(2, 1, 8 * WINDOW) for j in range(n_b)]

    gain = lambda g: g.reshape(1, -1)

    def mlp_fwd(h, n2, layer, next_gains):
        u, a = fwd_mm(f"mlp_up_fwd{layer}", n2, 'mlp_up', layer,
                      lambda acc: (acc, jnp.square(jnp.maximum(acc, 0.0))), (), (BF16, BF16))
        h2, *nexts = fwd_mm(f"mlp_down_fwd{layer}", a, 'mlp_down', layer, _residual_then_norms(1),
                            (h, *[gain(g) for g in next_gains]), (F32,) + (BF16,) * len(next_gains))
        return h2, nexts, (n2, u, a)

    h = x
    saved = []
    n1 = rms_fwd("a_norm_fwd0", h, small['a_norm'][0])
    for l in range(n_a):
        (qkvt,) = fwd_mm(f"a_qkv_fwd{l}", n1, 'a_wqkv', l, plain, (), (BF16,), out_t=True)
        qkvt = qkvt.reshape(3 * d // HEAD_DIM, HEAD_DIM, s)
        o_t, rtab, carried = sb_fwd(f"sb_fwd{l}", qkvt, ex.fwd_carry(l))
        ex.fwd_done(l, carried)
        o_t = o_t.reshape(d, s)
        h_mid, n2 = fwd_mm(f"a_wo_fwd{l}", o_t, 'a_wo', l, _residual_then_norms(1),
                           (h, gain(small['mlp_norm'][l])), (F32, BF16), a_t=True)
        next_gains = [small['a_norm'][l + 1]] if l + 1 < n_a else [small['b_norm'][0], small['kv_norm']]
        h_out, nexts, mlp_saved = mlp_fwd(h_mid, n2, l, next_gains)
        saved.append((h, n1, qkvt, o_t, rtab, h_mid, mlp_saved))
        h, n1 = h_out, nexts[0]
    h_kv, nkv = h, nexts[1]
    (kvt,) = fwd_mm("kv_fwd", nkv, 'w_kv', 0, plus_col, (small['b_kv'].reshape(-1, 1),), (BF16,), out_t=True)
    kvt = kvt.reshape(2, 2, HEAD_DIM, s)
    kpt, vpt = (jnp.pad(t, ((0, 0), (0, 0), (WINDOW, 0))) for t in (kvt[0], kvt[1]))
    for j in range(n_b):
        layer = n_a + j
        (qbt,) = fwd_mm(f"b_q_fwd{j}", n1, 'b_wq', j, plus_col, (small['b_bq'][j].reshape(-1, 1),), (BF16,),
                        out_t=True)
        o_t = swa_fwd(f"swa_fwd{j}", qbt, kpt, vpt, bias_t, sink_rows[j])
        h_mid, n2 = fwd_mm(f"b_wo_fwd{j}", o_t, 'b_wo', j, _residual_then_norms(2),
                           (h, gain(small['b_bo'][j]), gain(small['mlp_norm'][layer])), (F32, BF16), a_t=True)
        h_out, nexts, mlp_saved = mlp_fwd(h_mid, n2, layer, [small['b_norm'][j + 1]] if j + 1 < n_b else [])
        saved.append((h, n1, qbt, o_t, h_mid, mlp_saved))
        h, n1 = h_out, (nexts[0] if nexts else None)

    dh, dhb, dg_final, loss_b = loss_head(h, small['final_norm'], target)
    sg['final_norm'] = dg_final[0]
    sg['mlp_norm'] = [None] * (n_a + n_b)

    def mlp_bwd(dh, dhb, h_mid, mlp_saved, layer):
        n2, u, a = mlp_saved
        du, *carried = dx_mm(f"mlp_down_dx{layer}", dhb, 'mlp_down', layer,
                             lambda acc, uu: (acc * (2.0 * jnp.maximum(uu.astype(F32), 0.0)),), (u,), (BF16,),
                             exchange=ex.mlp_carry(layer, gb))
        ex.mlp_done(layer, carried)
        dw_mm(f"mlp_down_dw{layer}", a, dhb, 'mlp_down', layer)
        dh2, dh2b, dg, cs = dx_mm(f"mlp_up_dx{layer}", du, 'mlp_up', layer, _rms_bwd_epilogue,
                                  (h_mid, dh, gain(small['mlp_norm'][layer])), (F32, BF16), n_sums=2)
        dw_mm(f"mlp_up_dw{layer}", n2, du, 'mlp_up', layer)
        sg['mlp_norm'][layer] = dg[0]
        return dh2, dh2b, cs

    dkp = jnp.zeros(kpt.shape, F32)
    dvp = jnp.zeros(vpt.shape, F32)
    sg['b_norm'], sg['b_bq'], sg['b_bo'], sg['b_sinks'] = [None] * n_b, [None] * n_b, [None] * n_b, [None] * n_b
    dbias = [None] * n_b
    for j in reversed(range(n_b)):
        layer = n_a + j
        h_in, n1, qbt, o_t, h_mid, mlp_saved = saved[layer]
        dh, dhb, cs = mlp_bwd(dh, dhb, h_mid, mlp_saved, layer)
        sg['b_bo'][j] = cs[0]
        (do_t,) = dx_mm(f"b_wo_dx{j}", dhb, 'b_wo', j, plain, (), (BF16,), out_t=True)
        dw_mm(f"b_wo_dw{j}", o_t, dhb, 'b_wo', j, x_t=True)
        dq_t, dkp, dvp, dbias[j], dsink = swa_bwd(f"swa_bwd{j}", qbt, kpt, vpt, bias_t, sink_rows[j], do_t, dkp, dvp)
        sg['b_sinks'][j] = colsum(f"sink_grad{j}", dsink.reshape(16, WINDOW).T)[0]
        sg['b_bq'][j] = rowsum(f"b_bq_grad{j}", dq_t)
        dh, dhb, dg, _ = dx_mm(f"b_q_dx{j}", dq_t, 'b_wq', j, _rms_bwd_epilogue,
                               (h_in, dh, gain(small['b_norm'][j])), (F32, BF16), a_t=True, n_sums=2)
        dw_mm(f"b_q_dw{j}", n1, dq_t, 'b_wq', j, dy_t=True)
        sg['b_norm'][j] = dg[0]
    unt = lambda t: t.reshape(2, 2 * WINDOW, 8, WINDOW).transpose(0, 2, 3, 1).reshape(bias_flat.shape)
    sg['rel_bias'] = bias_table_grad(unt(dbias[0]), unt(dbias[1]))[:, :N_BUCKETS].T

    dkv_t = jnp.concatenate([dkp[:, :, WINDOW:], dvp[:, :, WINDOW:]], axis=0).reshape(-1, s)
    sg['b_kv'] = rowsum("b_kv_grad", dkv_t)
    dkvb = dkv_t.astype(BF16)
    dh, dhb, dg, _ = dx_mm("kv_dx", dkvb, 'w_kv', 0, _rms_bwd_epilogue, (h_kv, dh, gain(small['kv_norm'])),
                           (F32, BF16), a_t=True, n_sums=2)
    dw_mm("kv_dw", nkv, dkvb, 'w_kv', 0, dy_t=True)
    sg['kv_norm'] = dg[0]

    sg['a_norm'] = [None] * n_a
    for l in reversed(range(n_a)):
        h_in, n1, qkvt, o_t, rtab, h_mid, mlp_saved = saved[l]
        dh, dhb, _ = mlp_bwd(dh, dhb, h_mid, mlp_saved, l)
        (do_t,) = dx_mm(f"a_wo_dx{l}", dhb, 'a_wo', l, plain, (), (BF16,), out_t=True)
        dw_mm(f"a_wo_dw{l}", o_t, dhb, 'a_wo', l, x_t=True)
        dqkv_t, carried = sb_bwd(f"sb_bwd{l}", qkvt, do_t.reshape(d // HEAD_DIM, HEAD_DIM, s), rtab,
                                 ex.bwd_carry(l, gb))
        ex.bwd_done(l, carried)
        dqkv_t = dqkv_t.reshape(3 * d, s)
        dw_mm(f"a_qkv_dw{l}", n1, dqkv_t, 'a_wqkv', l, dy_t=True)
        dh, dhb, dg, _, *carried = dx_mm(f"a_qkv_dx{l}", dqkv_t, 'a_wqkv', l, _rms_bwd_epilogue,
                                         (h_in, dh, gain(small['a_norm'][l])), (F32, BF16), a_t=True, n_sums=2,
                                         exchange=ex.last_carry(gb) if l == 0 else None)
        if l == 0:
            ex.last_done(carried)
        sg['a_norm'][l] = dg[0]

    small_grads = {
        'a_norm': jnp.stack(sg['a_norm']), 'kv_norm': sg['kv_norm'], 'b_kv': sg['b_kv'],
        'b_norm': jnp.stack(sg['b_norm']), 'b_bq': jnp.stack(sg['b_bq']), 'b_sinks': jnp.stack(sg['b_sinks']),
        'b_bo': jnp.stack(sg['b_bo']), 'rel_bias': sg['rel_bias'], 'mlp_norm': jnp.stack(sg['mlp_norm']),
        'final_norm': sg['final_norm'],
    }
    return loss_b, dh, gb, small_grads


def _full_shape(name, shard_shape):
    if name in COL_SHARDED:
        return shard_shape[:2] + (N_DEV * shard_shape[2],)
    nl, r, n = shard_shape
    return (nl, N_DEV, r, n)


def _as_w3_shape(name, shard_shape):
    full = _full_shape(name, shard_shape)
    return full if name in COL_SHARDED else (full[0], full[1] * full[2], full[3])


def _as_w3(name, full):
    if name in COL_SHARDED:
        return full
    nl, nd, r, n = full.shape
    return full.reshape(nl, nd * r, n)


AG_GROUPS = {
    0: (('a_wqkv', 0, 1),),
    1: (('a_wo', 0, 2), ('mlp_up', 0, 2), ('mlp_down', 0, 2), ('a_wqkv', 1, 1)),
    2: (('mlp_up', 2, 2), ('mlp_down', 2, 2), ('b_wq', 0, 2), ('b_wo', 0, 2), ('w_kv', 0, 1)),
}
RS_GROUPS = {
    'A': (('mlp_up', 2, 2), ('mlp_down', 2, 2), ('b_wq', 0, 2), ('b_wo', 0, 2), ('w_kv', 0, 1)),
    'B': (('a_wqkv', 1, 1), ('a_wo', 0, 2), ('mlp_up', 0, 2), ('mlp_down', 0, 2)),
    'C': (('a_wqkv', 0, 1),),
}


class _Exchanges:
    def __init__(self, full0, shards, core, chip, w3, m3, v3):
        self.wbuf = {0: {n: _as_w3(n, full0[n]) for n, _, _ in AG_GROUPS[0]}}
        self.shards, self.core, self.chip = shards, core, chip
        self.w3, self.m3, self.v3 = w3, m3, v3
        self.shard_dims = {n: w3[n].shape[1:] for n in BIG}
        self.parts = {}
        self.out = {}

    def weight(self, name, layer):
        for group, members in AG_GROUPS.items():
            for n, l0, nl in members:
                if n == name and l0 <= layer < l0 + nl:
                    return self.wbuf[group][name], layer - l0
        raise KeyError((name, layer))

    def fwd_carry(self, layer):
        names = [n for n, _, _ in AG_GROUPS[layer + 1]]
        shards = [self.shards[layer + 1][n] for n in names]
        return ag_direct(names, shards, [_full_shape(n, sh.shape) for n, sh in zip(names, shards)])

    def fwd_done(self, layer, carried):
        names = [n for n, _, _ in AG_GROUPS[layer + 1]]
        self.wbuf[layer + 1] = {n: _as_w3(n, f) for n, f in zip(names, carried)}

    def grad(self, name, layer):
        for group, members in RS_GROUPS.items():
            for n, l0, nl in members:
                if n == name and l0 <= layer < l0 + nl:
                    return (group, name), layer - l0, _as_w3_shape(name, (nl,) + self.shard_dims[name])
        raise KeyError((name, layer))

    def _members(self, group):
        names = [n for n, _, _ in RS_GROUPS[group]]
        return names, [(nl,) + self.shard_dims[n] for n, _, nl in RS_GROUPS[group]]

    def _sibling_carry(self, group, gb):
        names, shapes = self._members(group)
        self.gfull = [gb[(group, n)].reshape(_full_shape(n, sh)) for n, sh in zip(names, shapes)]
        return sibling_exchange(names, self.gfull, shapes)

    def _sibling_done(self, group, recv):
        names, _ = self._members(group)
        self.parts[group] = [sibling_sum(f"rs_sibling_sum_{group}_{n}", n in COL_SHARDED, g, r, self.core)
                             for n, g, r in zip(names, self.gfull, recv)]

    def _sibling_stage(self, group, gb):
        ce = self._sibling_carry(group, gb)
        self._sibling_done(group, comm_call(f"rs_sibling_exchange_{group}", ce.build, ce.ins, ce.out_shapes, ce.sems))

    def mlp_carry(self, layer, gb):
        return self._sibling_carry('A', gb) if layer == 1 else None

    def mlp_done(self, layer, carried):
        if layer == 1:
            self._sibling_done('A', carried)

    def bwd_carry(self, layer, gb):
        group = 'A' if layer == 1 else 'B'
        if group not in self.parts:
            self._sibling_stage(group, gb)
        names, shapes = self._members(group)
        return chip_exchange(names, self.parts[group], shapes)

    def bwd_done(self, layer, carried):
        self._adamw('A' if layer == 1 else 'B', carried)

    def last_carry(self, gb):
        self._sibling_stage('C', gb)
        names, shapes = self._members('C')
        return chip_exchange(names, self.parts['C'], shapes)

    def last_done(self, carried):
        self._adamw('C', carried)

    def _adamw(self, group, recv2):
        for (n, l0, _), p, r in zip(RS_GROUPS[group], self.parts[group], recv2):
            self.out[n] = reduce_adamw(f"adamw_{group}_{n}", p, r, self.chip, self.w3[n], self.m3[n], self.v3[n],
                                       l0, self.out.get(n))


def _pack_small(vals):
    flat = jnp.concatenate([vals[n].reshape(-1).astype(F32) for n in SMALL] + [vals['loss'].reshape(-1)])
    rows = -(-flat.shape[0] // 1024) * 8
    return jnp.pad(flat, (0, rows * 128 - flat.shape[0])).reshape(rows, 128)


def _unpack_small(packed, shapes):
    flat = packed.reshape(-1)
    out, off = {}, 0
    for n in SMALL + ['loss']:
        size = int(np.prod(shapes[n]))
        out[n] = flat[off:off + size].reshape(shapes[n])
        off += size
    return out


def kernel(x, a_norm, a_wqkv, a_wo, kv_norm, w_kv, b_kv, b_norm, b_wq, b_bq, b_sinks, b_wo, b_bo, rel_bias, mlp_norm, mlp_up, mlp_down, final_norm, loss_target, m_a_norm, m_a_wqkv, m_a_wo, m_kv_norm, m_w_kv, m_b_kv, m_b_norm, m_b_wq, m_b_bq, m_b_sinks, m_b_wo, m_b_bo, m_rel_bias, m_mlp_norm, m_mlp_up, m_mlp_down, m_final_norm, v_a_norm, v_a_wqkv, v_a_wo, v_kv_norm, v_w_kv, v_b_kv, v_b_norm, v_b_wq, v_b_bq, v_b_sinks, v_b_wo, v_b_bo, v_rel_bias, v_mlp_norm, v_mlp_up, v_mlp_down, v_final_norm):
    w = dict(a_norm=a_norm, a_wqkv=a_wqkv, a_wo=a_wo, kv_norm=kv_norm, w_kv=w_kv, b_kv=b_kv, b_norm=b_norm,
             b_wq=b_wq, b_bq=b_bq, b_sinks=b_sinks, b_wo=b_wo, b_bo=b_bo, rel_bias=rel_bias, mlp_norm=mlp_norm,
             mlp_up=mlp_up, mlp_down=mlp_down, final_norm=final_norm)
    m = dict(a_norm=m_a_norm, a_wqkv=m_a_wqkv, a_wo=m_a_wo, kv_norm=m_kv_norm, w_kv=m_w_kv, b_kv=m_b_kv,
             b_norm=m_b_norm, b_wq=m_b_wq, b_bq=m_b_bq, b_sinks=m_b_sinks, b_wo=m_b_wo, b_bo=m_b_bo,
             rel_bias=m_rel_bias, mlp_norm=m_mlp_norm, mlp_up=m_mlp_up, mlp_down=m_mlp_down, final_norm=m_final_norm)
    v = dict(a_norm=v_a_norm, a_wqkv=v_a_wqkv, a_wo=v_a_wo, kv_norm=v_kv_norm, w_kv=v_w_kv, b_kv=v_b_kv,
             b_norm=v_b_norm, b_wq=v_b_wq, b_bq=v_b_bq, b_sinks=v_b_sinks, b_wo=v_b_wo, b_bo=v_b_bo,
             rel_bias=v_rel_bias, mlp_norm=v_mlp_norm, mlp_up=v_mlp_up, mlp_down=v_mlp_down, final_norm=v_final_norm)
    px, py, pc = _place()
    me = 4 * px + 2 * py + pc
    chip = (2 * px + py).astype(jnp.int32)
    core = pc.astype(jnp.int32)

    as3 = lambda t: t[None] if t.ndim == 2 else t
    w3, m3, v3 = ({n: as3(src[n]) for n in BIG} for src in (w, m, v))
    shards = {g: {n: w3[n][l0:l0 + nl].astype(BF16) for n, l0, nl in members} for g, members in AG_GROUPS.items()}
    an_pad = jnp.zeros((8, 128), F32).at[:a_norm.shape[0]].set(a_norm)
    names0 = [n for n, _, _ in AG_GROUPS[0]]
    full0 = all_gather_weights(names0 + ['a_norm'], [shards[0][n] for n in names0] + [an_pad],
                               [_full_shape(n, shards[0][n].shape) for n in names0] + [(N_DEV, 8, 128)])
    full0 = dict(zip(names0 + ['a_norm'], full0))
    n_a = a_norm.shape[0]
    small = {n: w[n] for n in SMALL}
    small['a_norm'] = full0['a_norm'][:, :n_a].transpose(1, 0, 2).reshape(n_a, -1)

    ex = _Exchanges(full0, shards, core, chip, w3, m3, v3)
    loss_b, grad_x, gb, sgrads = local_step(x[0], loss_target[0], small, ex)
    out = {n: [t.reshape(w[n].shape) for t in bufs] for n, bufs in ex.out.items()}

    sgrads['loss'] = loss_b[0, :1]
    gathered = all_gather_rows(_pack_small(sgrads))
    shapes = {n: w[n].shape for n in SMALL}
    shapes['a_norm'] = (n_a, a_norm.shape[1] * N_DEV)
    shapes['loss'] = (1,)
    zeros1 = jnp.zeros((1,), F32)

    def packed(src):
        vals = {n: src[n] for n in SMALL}
        vals['a_norm'] = jnp.zeros(shapes['a_norm'], F32)
        vals['loss'] = zeros1
        return _pack_small(vals)

    sm = small_adamw("adamw_small", gathered, packed(w), packed(m), packed(v))
    sm = [_unpack_small(t, shapes) for t in sm]
    g_an = lax.dynamic_slice_in_dim(sm[0]['a_norm'], me * a_norm.shape[1], a_norm.shape[1], axis=1)
    pad = lambda t: jnp.zeros((8, 128), F32).at[:n_a].set(t)
    gathered_an = jnp.zeros((N_DEV, 8, 128), F32).at[0].set(pad(g_an))
    an = small_adamw("adamw_a_norm", gathered_an, pad(a_norm), pad(m_a_norm), pad(v_a_norm))
    for i in range(4):
        sm[i]['a_norm'] = an[i][:n_a]
    for n in BIG:
        for i in range(4):
            sm[i][n] = out[n][i]
    loss = sm[0]['loss'][0]
    return (loss, grad_x[None], *[sm[0][n] for n in WEIGHTS], *[sm[1][n] for n in WEIGHTS],
            *[sm[2][n] for n in WEIGHTS], *[sm[3][n] for n in WEIGHTS])
```

```python
import math

import numpy as np
import jax
import jax.numpy as jnp
from jax import lax
from jax.experimental import pallas as pl
from jax.experimental.pallas import tpu as pltpu

F32 = jnp.float32
BF16 = jnp.bfloat16
MESH = pl.DeviceIdType.MESH

N_DEV = 8
HEAD_DIM = 64
WINDOW = 128
N_BUCKETS = 32
EPS = 1e-5
NEG_INF = -1e30
Q_SCALE = 1.0 / math.sqrt(HEAD_DIM)
LOG2E = 1.4426950408889634

ADAM_LR, ADAM_B1, ADAM_B2, ADAM_EPS, ADAM_WD, ADAM_STEP = 0.001, 0.9, 0.999, 1e-08, 0.01, 10

SB_BQ = 512
SB_BK = 128
SB_DEAD = 160.0
SB_UNSEEN = 1e30
ROW_TILE = 512
VMEM_LIMIT = 56 * 1024 * 1024

WEIGHTS = ['a_norm', 'a_wqkv', 'a_wo', 'kv_norm', 'w_kv', 'b_kv', 'b_norm', 'b_wq', 'b_bq', 'b_sinks', 'b_wo',
           'b_bo', 'rel_bias', 'mlp_norm', 'mlp_up', 'mlp_down', 'final_norm']
BIG = ['a_wqkv', 'a_wo', 'w_kv', 'b_wq', 'b_wo', 'mlp_up', 'mlp_down']
COL_SHARDED = ('a_wqkv', 'mlp_up')
SMALL = ['a_norm', 'kv_norm', 'b_kv', 'b_norm', 'b_bq', 'b_sinks', 'b_bo', 'rel_bias', 'mlp_norm', 'final_norm']


def _params(sem=None):
    return pltpu.CompilerParams(dimension_semantics=sem, vmem_limit_bytes=VMEM_LIMIT)


def _pick(n, cands):
    for c in cands:
        if n % c == 0:
            return c
    raise ValueError(n)


def _tile(n, want):
    return n if n <= want else _pick(n, (want, want // 2, want // 4))


MM_TILE_BUDGET = 36 * 1024 * 1024


def _row_tile(m, contraction, cols, streams):
    weight = 2 * contraction * cols * 2
    for rows in (2048, 1024, 512):
        if m % rows == 0 and weight + 2 * rows * (2 * contraction + cols * sum(streams)) <= MM_TILE_BUDGET:
            return rows
    return _tile(m, 512)


def mm_nn(name, a, w3, layer, epilogue, extras, out_dtypes, a_t=False, out_t=False):
    k, m = a.shape if a_t else a.shape[::-1]
    _, kw, n = w3.shape
    assert kw == k
    tn = _tile(n, 1024)
    tm = _row_tile(m, k, tn, [jnp.dtype(t).itemsize for t in out_dtypes]
                   + [e.dtype.itemsize for e in extras if e.size == m * n])
    ne, no = len(extras), len(out_dtypes)
    a_dim = 0 if a_t else 1

    def body(a_ref, w_ref, *rest):
        ex, outs = rest[:ne], rest[ne:ne + no]
        if out_t:
            acc = lax.dot_general(w_ref[...], a_ref[...], (((0,), (a_dim,)), ((), ())), preferred_element_type=F32)
        else:
            acc = lax.dot_general(a_ref[...], w_ref[...], (((a_dim,), (0,)), ((), ())), preferred_element_type=F32)
        for o, r in zip(outs, epilogue(acc, *[e[...] for e in ex])):
            o[...] = r.astype(o.dtype)

    if out_t:
        tile = pl.BlockSpec((tn, tm), lambda i, j: (j, i))
        vec = pl.BlockSpec((tn, 1), lambda i, j: (j, 0))
        out_shape = (n, m)
    else:
        tile = pl.BlockSpec((tm, tn), lambda i, j: (i, j))
        vec = pl.BlockSpec((1, tn), lambda i, j: (0, j))
        out_shape = (m, n)
    a_spec = pl.BlockSpec((k, tm), lambda i, j: (0, i)) if a_t else pl.BlockSpec((tm, k), lambda i, j: (i, 0))
    return pl.pallas_call(
        body, name=name, grid=(m // tm, n // tn),
        in_specs=[a_spec, pl.BlockSpec((None, k, tn), lambda i, j: (layer, 0, j))]
        + [tile if e.shape == out_shape else vec for e in extras],
        out_specs=[tile] * no,
        out_shape=[jax.ShapeDtypeStruct(out_shape, d) for d in out_dtypes],
        compiler_params=_params(("parallel", "parallel")),
    )(a, w3, *extras)


def mm_nt(name, dy, w3, layer, epilogue, extras, out_dtypes, a_t=False, out_t=False, n_sums=0, exchange=None):
    n, m = dy.shape if a_t else dy.shape[::-1]
    _, k, nw = w3.shape
    assert nw == n and not (out_t and n_sums)
    tko = _tile(k, 1024)
    tm = _row_tile(m, n, tko, [jnp.dtype(t).itemsize for t in out_dtypes]
                   + [e.dtype.itemsize for e in extras if e.size == m * k])
    ne, no = len(extras), len(out_dtypes)
    a_dim = 0 if a_t else 1

    def body(a_ref, w_ref, *rest):
        ex = rest[:ne]
        at = lambda step: (pl.program_id(0) == step[0]) & (pl.program_id(1) == step[1])
        results, _, start_carried, wait_carried = _carried(exchange, rest[ne:], no + n_sums, 0, at((0, 0)),
                                                           at((m // tm - 1, k // tko - 1)))
        outs, sums = results[:no], results[no:]
        start_carried()
        if out_t:
            acc = lax.dot_general(w_ref[...], a_ref[...], (((1,), (a_dim,)), ((), ())), preferred_element_type=F32)
        else:
            acc = lax.dot_general(a_ref[...], w_ref[...], (((a_dim,), (1,)), ((), ())), preferred_element_type=F32)
        res = epilogue(acc, *[e[...] for e in ex])
        for o, v in zip(outs, res):
            o[...] = v.astype(o.dtype)
        if n_sums:
            @pl.when(pl.program_id(0) == 0)
            def _():
                for o in sums:
                    o[...] = jnp.zeros_like(o)

            for o, v in zip(sums, res[no:]):
                o[...] += v
        wait_carried()

    if out_t:
        tile = pl.BlockSpec((tko, tm), lambda i, ko: (ko, i))
        out_shape = (k, m)
    else:
        tile = pl.BlockSpec((tm, tko), lambda i, ko: (i, ko))
        out_shape = (m, k)
    vec = pl.BlockSpec((1, tko), lambda i, ko: (0, ko))
    a_spec = pl.BlockSpec((n, tm), lambda i, ko: (0, i)) if a_t else pl.BlockSpec((tm, n), lambda i, ko: (i, 0))
    hbm = pl.BlockSpec(memory_space=pl.ANY)
    c_ins, c_outs, c_sems = (exchange.ins, exchange.out_shapes, exchange.sems) if exchange else ([], [], [])
    sequential = n_sums or exchange
    return pl.pallas_call(
        body, name=name, grid=(m // tm, k // tko),
        in_specs=[a_spec, pl.BlockSpec((None, tko, n), lambda i, ko: (layer, ko, 0))]
        + [tile if e.shape == out_shape else vec for e in extras] + [hbm] * len(c_ins),
        out_specs=[tile] * no + [vec] * n_sums + [hbm] * len(c_outs),
        out_shape=[jax.ShapeDtypeStruct(out_shape, d) for d in out_dtypes] + [jax.ShapeDtypeStruct((1, k), F32)] * n_sums
        + c_outs,
        scratch_shapes=c_sems,
        compiler_params=_params(("arbitrary" if sequential else "parallel", "arbitrary" if exchange else "parallel")),
    )(dy, w3, *extras, *c_ins)


def mm_tn(name, x, dy, gbuf, shape, layer, x_t=False, dy_t=False):
    k, s = x.shape if x_t else x.shape[::-1]
    _, kw, n = shape
    assert kw == k and dy.shape == ((n, s) if dy_t else (s, n))
    tkk = _tile(k, 512)
    tn = _tile(n, 1024)

    def body(x_ref, dy_ref, *rest):
        g_out = rest[-1]
        g_out[...] = lax.dot_general(x_ref[...], dy_ref[...], (((1 if x_t else 0,), (1 if dy_t else 0,)), ((), ())),
                                     preferred_element_type=F32).astype(g_out.dtype)

    prev = [] if gbuf is None else [gbuf]
    x_spec = pl.BlockSpec((tkk, s), lambda ki, j: (ki, 0)) if x_t else pl.BlockSpec((s, tkk), lambda ki, j: (0, ki))
    dy_spec = pl.BlockSpec((tn, s), lambda ki, j: (j, 0)) if dy_t else pl.BlockSpec((s, tn), lambda ki, j: (0, j))
    return pl.pallas_call(
        body, name=name, grid=(k // tkk, n // tn),
        in_specs=[x_spec, dy_spec] + [pl.BlockSpec(memory_space=pl.ANY)] * len(prev),
        out_specs=pl.BlockSpec((None, tkk, tn), lambda ki, j: (layer, ki, j)),
        out_shape=jax.ShapeDtypeStruct(shape, BF16),
        input_output_aliases={2: 0} if prev else {},
        compiler_params=_params(("parallel", "parallel")),
    )(x, dy, *prev)


def rms_fwd(name, h, g):
    s, d = h.shape
    tr = _pick(s, (ROW_TILE, 256, 128))

    def body(h_ref, g_ref, o_ref):
        x = h_ref[...]
        r = lax.rsqrt(jnp.mean(x * x, axis=-1, keepdims=True) + EPS)
        o_ref[...] = (x * r * g_ref[...]).astype(o_ref.dtype)

    return pl.pallas_call(
        body, name=name, grid=(s // tr,),
        in_specs=[pl.BlockSpec((tr, d), lambda i: (i, 0)), pl.BlockSpec((1, d), lambda i: (0, 0))],
        out_specs=pl.BlockSpec((tr, d), lambda i: (i, 0)),
        out_shape=jax.ShapeDtypeStruct((s, d), BF16),
        compiler_params=_params(("parallel",)),
    )(h, g.reshape(1, d))


def loss_head(h, g, target):
    s, d = h.shape
    tr = _pick(s, (ROW_TILE, 256, 128))

    def body(h_ref, g_ref, t_ref, dx_ref, dxb_ref, dg_ref, loss_ref):
        i = pl.program_id(0)
        x = h_ref[...]
        r = lax.rsqrt(jnp.mean(x * x, axis=-1, keepdims=True) + EPS)
        xh = x * r
        gw = g_ref[...]
        err = xh * gw - t_ref[...]
        dn_ = err * (1.0 / d)
        dyg = dn_ * gw
        dx = r * (dyg - xh * jnp.mean(dyg * xh, axis=-1, keepdims=True))
        dx_ref[...] = dx
        dxb_ref[...] = dx.astype(BF16)

        @pl.when(i == 0)
        def _():
            dg_ref[...] = jnp.zeros_like(dg_ref)
            loss_ref[...] = jnp.zeros_like(loss_ref)

        dg_ref[...] += jnp.sum(dn_ * xh, axis=0, keepdims=True)
        per_row = jnp.sum(err * err, axis=-1, keepdims=True) * (0.5 / d)
        loss_ref[...] += jnp.broadcast_to(jnp.sum(per_row, axis=0, keepdims=True), loss_ref.shape)

    row = pl.BlockSpec((tr, d), lambda i: (i, 0))
    vec = pl.BlockSpec((1, d), lambda i: (0, 0))
    return pl.pallas_call(
        body, name="loss_head", grid=(s // tr,),
        in_specs=[row, vec, row],
        out_specs=[row, row, vec, pl.BlockSpec((1, 128), lambda i: (0, 0))],
        out_shape=[jax.ShapeDtypeStruct((s, d), F32), jax.ShapeDtypeStruct((s, d), BF16),
                   jax.ShapeDtypeStruct((1, d), F32), jax.ShapeDtypeStruct((1, 128), F32)],
        compiler_params=_params(("arbitrary",)),
    )(h, g.reshape(1, d), target)


def colsum(name, x):
    s, n = x.shape
    tr = _pick(s, (ROW_TILE, 256, 128))

    def body(x_ref, o_ref):
        @pl.when(pl.program_id(0) == 0)
        def _():
            o_ref[...] = jnp.zeros_like(o_ref)

        o_ref[...] += jnp.sum(x_ref[...].astype(F32), axis=0, keepdims=True)

    return pl.pallas_call(
        body, name=name, grid=(s // tr,),
        in_specs=[pl.BlockSpec((tr, n), lambda i: (i, 0))],
        out_specs=pl.BlockSpec((1, n), lambda i: (0, 0)),
        out_shape=jax.ShapeDtypeStruct((1, n), F32),
        compiler_params=_params(("arbitrary",)),
    )(x)


def rowsum(name, x):
    n, s = x.shape
    ts = _pick(s, (1024, 512, 256, 128))

    def body(x_ref, o_ref):
        @pl.when(pl.program_id(0) == 0)
        def _():
            o_ref[...] = jnp.zeros_like(o_ref)

        o_ref[...] += jnp.sum(x_ref[...].astype(F32), axis=1, keepdims=True)

    return pl.pallas_call(
        body, name=name, grid=(s // ts,),
        in_specs=[pl.BlockSpec((n, ts), lambda i: (0, i))],
        out_specs=pl.BlockSpec((n, 1), lambda i: (0, 0)),
        out_shape=jax.ShapeDtypeStruct((n, 1), F32),
        compiler_params=_params(("arbitrary",)),
    )(x)[:, 0]


def _tri_rows(reverse):
    i = np.arange(SB_BK)
    tri = (i[None, :] >= i[:, None]) if reverse else (i[None, :] <= i[:, None])
    tri = np.concatenate([tri, tri], axis=1)
    return jnp.asarray(np.concatenate([tri, np.ones((8, 2 * SB_BK), bool)], axis=0), BF16)


def _hi_lo_rows(x):
    hi = x.astype(BF16)
    lo = (x - hi.astype(F32)).astype(BF16)
    return jnp.concatenate([hi, lo], axis=0)


def _softplus2(zs):
    neg_abs = lax.bitcast_convert_type(lax.bitcast_convert_type(zs, jnp.uint32) | jnp.uint32(0x80000000), F32)
    return jnp.maximum(zs, 0.0) + jnp.log2(1.0 + jnp.exp2(neg_abs))


def _pair_mask(first_rel_block, bq):
    key = lax.broadcasted_iota(jnp.int32, (2 * SB_BK, bq), 0) + first_rel_block * SB_BK
    qry = lax.broadcasted_iota(jnp.int32, (2 * SB_BK, bq), 1)
    return key < qry


def _row_of(table8, sub8, r):
    return jnp.sum(jnp.where(sub8 == r, table8, 0.0), axis=0, keepdims=True)


def _keys(j0):
    return pl.ds(pl.multiple_of(j0 * SB_BK, 2 * SB_BK), 2 * SB_BK)


class Carry:
    def __init__(self, build, ins, out_shapes, sems, then=None):
        self.build, self.ins, self.out_shapes, self.sems = build, list(ins), list(out_shapes), list(sems)
        self.then = then


def _carried(carry, rest, n_out, n_scratch, first, last):
    n_ci = len(carry.ins) if carry else 0
    n_co = len(carry.out_shapes) if carry else 0
    cin, outs = rest[:n_ci], rest[n_ci:n_ci + n_out]
    cout = rest[n_ci + n_out:n_ci + n_out + n_co]
    scratch = rest[n_ci + n_out + n_co:n_ci + n_out + n_co + n_scratch]
    csems = rest[n_ci + n_out + n_co + n_scratch:]

    def start():
        if carry:
            @pl.when(first)
            def _():
                for cp in carry.build(cin, cout, *csems):
                    cp.start()

    def wait():
        if carry:
            @pl.when(last)
            def _():
                for cp in carry.build(cin, cout, *csems):
                    cp.wait()
                if carry.then:
                    second = carry.then(cin, cout, *csems)
                    for cp in second:
                        cp.start()
                    for cp in second:
                        cp.wait()

    return outs, scratch, start, wait


def _contract0(a, b):
    return lax.dot_general(a, b, (((0,), (0,)), ((), ())), preferred_element_type=F32)


def _contract1(a, b):
    return lax.dot_general(a, b, (((1,), (1,)), ((), ())), preferred_element_type=F32)


def sb_fwd(name, qkvt, exchange=None):
    nh, dh, s = qkvt.shape[0] // 3, qkvt.shape[1], qkvt.shape[2]
    bq = SB_BQ
    per_q = bq // SB_BK
    nkb = s // SB_BK
    assert s % bq == 0 and per_q == 4 and nkb % 8 == 0

    def body(q_ref, k_ref, v_ref, a_ref, *rest):
        head = pl.program_id(0)
        (o_ref, rtab_ref), (acc, zbuf, wbuf), start_carried, wait_carried = _carried(
            exchange, rest, 2, 3, head == 0, head == nh - 1)
        start_carried()
        tri = a_ref[...]
        sub8 = lax.broadcasted_iota(jnp.int32, (8, bq), 0)
        rtab_ref[...] = jnp.full(rtab_ref.shape, SB_UNSEEN, F32)
        kf = k_ref[...].astype(F32)
        k_max2 = jnp.max(jnp.sum(kf * kf, axis=0, keepdims=True), axis=1, keepdims=True)

        def query_block(i, _):
            lanes = pl.ds(pl.multiple_of(i * bq, bq), bq)
            qb = q_ref[:, lanes] * Q_SCALE
            acc[...] = jnp.zeros_like(acc)
            qf = qb.astype(F32)
            bound = jnp.sqrt(jnp.sum(qf * qf, axis=0, keepdims=True) * k_max2) * (1.001 * LOG2E)

            def scores(j0):
                return _contract0(k_ref[:, _keys(j0)], qb) * LOG2E

            def pair(j0, slot, run, rt8, mask, has_prev):
                zs = zbuf[slot]
                zbuf[1 - slot] = scores(jnp.maximum(j0 - 2, 0))
                if has_prev:
                    acc[...] += jnp.dot(v_ref[:, _keys(j0 + 2)], wbuf[1 - slot], preferred_element_type=F32)
                p = _softplus2(zs)
                if mask is not None:
                    p = jnp.where(mask, p, 0.0)
                cr1 = jnp.dot(tri, _hi_lo_rows(p[SB_BK:]), preferred_element_type=F32)
                cr0 = jnp.dot(tri, _hi_lo_rows(p[:SB_BK]), preferred_element_type=F32)
                run1 = run + cr1[SB_BK:SB_BK + 1]
                w = jnp.exp2(jnp.concatenate([zs[:SB_BK] - cr0[:SB_BK] - run1, zs[SB_BK:] - cr1[:SB_BK] - run],
                                             axis=0))
                if mask is not None:
                    w = jnp.where(mask, w, 0.0)
                wbuf[slot] = w.astype(BF16)
                rt8 = jnp.where(j0 % 8 == 6, SB_UNSEEN, rt8)
                rt8 = jnp.where(sub8 == (j0 + 1) % 8, run, jnp.where(sub8 == j0 % 8, run1, rt8))
                rtab_ref[pl.ds(pl.multiple_of((j0 // 8) * 8, 8), 8), lanes] = rt8
                return run1 + cr0[SB_BK:SB_BK + 1], rt8

            def alive(run):
                return jnp.min(run - bound) < SB_DEAD

            top = i * per_q
            zbuf[0] = scores(top + 2)
            state = (jnp.zeros((1, bq), F32), jnp.full((8, bq), SB_UNSEEN, F32))
            state = pair(top + 2, 0, *state, _pair_mask(2, bq), False)
            state = pair(top, 1, *state, _pair_mask(0, bq), True)

            def step(c):
                it, pairs, _, run, rt8 = c
                j0 = top - 2 - 4 * it
                run, rt8 = pair(j0, 0, run, rt8, None, True)
                go = alive(run)
                run, rt8 = lax.cond(go, lambda r, t: pair(j0 - 2, 1, r, t, None, True), lambda r, t: (r, t), run, rt8)
                return it + 1, pairs + 1 + go.astype(jnp.int32), go & alive(run), run, rt8

            pairs = lax.while_loop(lambda c: (c[0] < i) & c[2], step, (0, 0, alive(state[0]), *state))[1]
            acc[...] += jnp.dot(v_ref[:, _keys(top - 2 * pairs)], wbuf[(pairs + 1) % 2], preferred_element_type=F32)
            o_ref[:, lanes] = acc[...].astype(o_ref.dtype)
            return 0

        lax.fori_loop(0, s // bq, query_block, 0)
        wait_carried()

    def head_spec(offset, rows):
        return pl.BlockSpec((None, rows, s), lambda h: (h + offset, 0, 0))

    hbm = pl.BlockSpec(memory_space=pl.ANY)
    c_ins, c_outs, c_sems = (exchange.ins, exchange.out_shapes, exchange.sems) if exchange else ([], [], [])
    outs = pl.pallas_call(
        body, name=name, grid=(nh,),
        in_specs=[head_spec(0, dh), head_spec(nh, dh), head_spec(2 * nh, dh),
                  pl.BlockSpec((SB_BK + 8, 2 * SB_BK), lambda h: (0, 0))] + [hbm] * len(c_ins),
        out_specs=[head_spec(0, dh), head_spec(0, nkb)] + [hbm] * len(c_outs),
        out_shape=[jax.ShapeDtypeStruct((nh, dh, s), BF16), jax.ShapeDtypeStruct((nh, nkb, s), F32)] + c_outs,
        scratch_shapes=[pltpu.VMEM((dh, bq), F32), pltpu.VMEM((2, 2 * SB_BK, bq), F32),
                        pltpu.VMEM((2, 2 * SB_BK, bq), BF16)] + c_sems,
        compiler_params=_params(("arbitrary",)),
    )(qkvt, qkvt, qkvt, _tri_rows(True), *c_ins)
    return outs[0], outs[1], outs[2:]


def sb_bwd(name, qkvt, dot_, rtab, exchange=None):
    nh, dh, s = qkvt.shape[0] // 3, qkvt.shape[1], qkvt.shape[2]
    bq = SB_BQ
    per_q = bq // SB_BK
    nkb = s // SB_BK

    def body(qt_ref, kt_ref, vt_ref, dot_ref, rtab_ref, ar_ref, af_ref, *rest):
        head = pl.program_id(0)
        (dqkv_ref,), (dq_acc, dk_acc, dv_acc, zbuf, dwbuf, dzbuf, wbuf), start_carried, wait_carried = \
            _carried(exchange, rest, 1, 7, head == 0, head == nh - 1)
        dq_ref, dk_ref, dv_ref = dqkv_ref.at[0], dqkv_ref.at[1], dqkv_ref.at[2]
        start_carried()
        dk_acc[...] = jnp.zeros_like(dk_acc)
        dv_acc[...] = jnp.zeros_like(dv_acc)
        tri_rev = ar_ref[...][:SB_BK]
        tri_fwd = af_ref[...]
        sub8 = lax.broadcasted_iota(jnp.int32, (8, bq), 0)

        def query_block(i, _):
            lanes = pl.ds(pl.multiple_of(i * bq, bq), bq)
            qtb = qt_ref[:, lanes] * Q_SCALE
            dotb = dot_ref[:, lanes]
            dq_acc[...] = jnp.zeros_like(dq_acc)
            last_j = i * per_q + 2
            seen = jnp.max(jnp.where(rtab_ref[:, lanes] < 0.1 * SB_UNSEEN, 1.0, 0.0), axis=1, keepdims=True)
            pairs = jnp.clip((jnp.sum(seen).astype(jnp.int32) - per_q) // 2, 0, 2 * i)
            odd = pairs % 2
            first_j = i * per_q - 2 * pairs

            def issue(j0, slot):
                zbuf[slot] = _contract0(kt_ref[:, _keys(j0)], qtb) * LOG2E
                dwbuf[slot] = _contract0(vt_ref[:, _keys(j0)], dotb)

            def retire(j0, slot):
                keys = _keys(j0)
                dq_acc[...] += jnp.dot(kt_ref[:, keys], dzbuf[slot], preferred_element_type=F32)
                dk_acc[:, keys] += _contract1(qtb, dzbuf[slot])
                dv_acc[:, keys] += _contract1(dotb, wbuf[slot])

            def pair(j0, slot, g_run, mask):
                zs = zbuf[slot]
                dw = dwbuf[slot]
                issue(jnp.minimum(j0 + 2, last_j), 1 - slot)
                retire(jnp.maximum(j0 - 2, first_j), 1 - slot)
                p_raw = _softplus2(zs)
                p = p_raw if mask is None else jnp.where(mask, p_raw, 0.0)
                c0 = jnp.dot(tri_rev, _hi_lo_rows(p[:SB_BK]), preferred_element_type=F32)
                c1 = jnp.dot(tri_rev, _hi_lo_rows(p[SB_BK:]), preferred_element_type=F32)
                rt8 = rtab_ref[pl.ds(pl.multiple_of((j0 // 8) * 8, 8), 8), lanes]
                r0 = _row_of(rt8, sub8, j0 % 8)
                r1 = _row_of(rt8, sub8, (j0 + 1) % 8)
                w = jnp.exp2(jnp.concatenate([zs[:SB_BK] - c0 - r0, zs[SB_BK:] - c1 - r1], axis=0))
                if mask is not None:
                    w = jnp.where(mask, w, 0.0)
                g = w * dw
                gg0 = jnp.dot(tri_fwd, _hi_lo_rows(g[:SB_BK]), preferred_element_type=F32)
                gg1 = jnp.dot(tri_fwd, _hi_lo_rows(g[SB_BK:]), preferred_element_type=F32)
                g_run1 = g_run + gg0[SB_BK:SB_BK + 1]
                g_pre = jnp.concatenate([gg0[:SB_BK] + g_run, gg1[:SB_BK] + g_run1], axis=0)
                dz = g - jnp.exp2(zs - p_raw) * g_pre
                if mask is not None:
                    dz = jnp.where(mask, dz, 0.0)
                dzbuf[slot] = dz.astype(BF16)
                wbuf[slot] = w.astype(BF16)
                return g_run1 + gg1[SB_BK:SB_BK + 1]

            issue(first_j, odd)
            dzbuf[...] = jnp.zeros(dzbuf.shape, BF16)
            wbuf[...] = jnp.zeros(wbuf.shape, BF16)

            def step(it, g_run):
                g_run = pair(4 * it, 0, g_run, None)
                return pair(4 * it + 2, 1, g_run, None)

            g_run = lax.cond(odd == 1, lambda g: pair(first_j, 1, g, None), lambda g: g, jnp.zeros((1, bq), F32))
            g_run = lax.fori_loop(i - pairs // 2, i, step, g_run)
            g_run = pair(last_j - 2, 0, g_run, _pair_mask(0, bq))
            pair(last_j, 1, g_run, _pair_mask(2, bq))
            retire(last_j, 1)
            dq_ref[:, lanes] = (dq_acc[...] * Q_SCALE).astype(dq_ref.dtype)
            return 0

        lax.fori_loop(0, s // bq, query_block, 0)
        dk_ref[...] = dk_acc[...].astype(dk_ref.dtype)
        dv_ref[...] = dv_acc[...].astype(dv_ref.dtype)
        wait_carried()

    def head_spec(offset, rows):
        return pl.BlockSpec((None, rows, s), lambda h: (h + offset, 0, 0))

    aspec = pl.BlockSpec((SB_BK + 8, 2 * SB_BK), lambda h: (0, 0))
    pair_f32 = pltpu.VMEM((2, 2 * SB_BK, bq), F32)
    pair_bf16 = pltpu.VMEM((2, 2 * SB_BK, bq), BF16)
    hbm = pl.BlockSpec(memory_space=pl.ANY)
    c_ins, c_outs, c_sems = (exchange.ins, exchange.out_shapes, exchange.sems) if exchange else ([], [], [])
    outs = pl.pallas_call(
        body, name=name, grid=(nh,),
        in_specs=[head_spec(0, dh), head_spec(nh, dh), head_spec(2 * nh, dh), head_spec(0, dh), head_spec(0, nkb),
                  aspec, aspec] + [hbm] * len(c_ins),
        out_specs=[pl.BlockSpec((3, None, dh, s), lambda h: (0, h, 0, 0))] + [hbm] * len(c_outs),
        out_shape=[jax.ShapeDtypeStruct((3, nh, dh, s), BF16)] + c_outs,
        scratch_shapes=[pltpu.VMEM((dh, bq), F32), pltpu.VMEM((dh, s), F32), pltpu.VMEM((dh, s), F32),
                        pair_f32, pair_f32, pair_bf16, pair_bf16] + c_sems,
        compiler_params=_params(("arbitrary",)),
    )(qkvt, qkvt, qkvt, dot_, rtab, _tri_rows(True), _tri_rows(False), *c_ins)
    return outs[0], outs[1:]


SWA_QB = 2


def _band_valid():
    kj = np.arange(2 * WINDOW)[:, None]
    dist = (np.arange(8 * WINDOW)[None, :] % WINDOW) + WINDOW - kj
    inside = (dist >= 0) & (dist < WINDOW)
    return jnp.asarray(np.stack([inside & (kj >= WINDOW), inside]), F32)


def _swa_probs(qt, kt, bias_t, valid, sink):
    sc = jnp.where(valid > 0.5, _contract0(kt, qt) + bias_t, NEG_INF)
    mx = jnp.maximum(jnp.max(sc, axis=0, keepdims=True), sink)
    p = jnp.exp(sc - mx)
    p_sink = jnp.exp(sink - mx)
    inv = 1.0 / (jnp.sum(p, axis=0, keepdims=True) + p_sink)
    return p, p_sink, inv


def _band(i):
    return pl.ds(pl.multiple_of(i * WINDOW, WINDOW), 2 * WINDOW)


def _heads_to_lanes(blk):
    return jnp.concatenate([blk[r * HEAD_DIM:(r + 1) * HEAD_DIM] for r in range(8)], axis=1)


def _lanes_to_heads(t):
    return jnp.concatenate([t[:, r * WINDOW:(r + 1) * WINDOW] for r in range(8)], axis=0)


def swa_fwd(name, qt, kpt, vpt, bias_t, sink_row):
    d, s = qt.shape
    ng, dh, sp = kpt.shape
    rows, cols = d // ng, SWA_QB * WINDOW
    assert (s // WINDOW) % SWA_QB == 0

    def body(q_ref, k_ref, v_ref, bias_ref, valid_ref, sink_ref, o_ref):
        for u in range(SWA_QB):
            i = pl.program_id(1) * SWA_QB + u
            lanes = slice(u * WINDOW, (u + 1) * WINDOW)
            qb = _heads_to_lanes(q_ref[:, lanes]) * Q_SCALE
            p, _, inv = _swa_probs(qb, k_ref[:, _band(i)], bias_ref[...], valid_ref[jnp.minimum(i, 1)], sink_ref[...])
            o_t = jnp.dot(v_ref[:, _band(i)], p.astype(BF16), preferred_element_type=F32) * inv
            o_ref[:, lanes] = _lanes_to_heads(o_t).astype(o_ref.dtype)

    qspec = pl.BlockSpec((rows, cols), lambda g, i: (g, i))
    kspec = pl.BlockSpec((None, dh, sp), lambda g, i: (g, 0, 0))
    return pl.pallas_call(
        body, name=name, grid=(ng, s // cols),
        in_specs=[qspec, kspec, kspec, pl.BlockSpec((None, 2 * WINDOW, 8 * WINDOW), lambda g, i: (g, 0, 0)),
                  pl.BlockSpec((2, 2 * WINDOW, 8 * WINDOW), lambda g, i: (0, 0, 0)),
                  pl.BlockSpec((None, 1, 8 * WINDOW), lambda g, i: (g, 0, 0))],
        out_specs=qspec,
        out_shape=jax.ShapeDtypeStruct(qt.shape, BF16),
        compiler_params=_params(("parallel", "arbitrary")),
    )(qt, kpt, vpt, bias_t, _band_valid(), sink_row)


def swa_bwd(name, qt, kpt, vpt, bias_t, sink_row, dot_, dk_in, dv_in):
    d, s = qt.shape
    ng, dh, sp = kpt.shape
    rows, cols = d // ng, SWA_QB * WINDOW

    def body(q_ref, k_ref, v_ref, bias_ref, valid_ref, sink_ref, do_ref, dki_ref, dvi_ref,
             dq_ref, dk_ref, dv_ref, db_ref, ds_ref):
        @pl.when(pl.program_id(1) == 0)
        def _():
            dk_ref[...] = dki_ref[...]
            dv_ref[...] = dvi_ref[...]
            db_ref[...] = jnp.zeros_like(db_ref)
            ds_ref[...] = jnp.zeros_like(ds_ref)

        for u in range(SWA_QB):
            i = pl.program_id(1) * SWA_QB + u
            band = _band(i)
            lanes = slice(u * WINDOW, (u + 1) * WINDOW)
            qb = _heads_to_lanes(q_ref[:, lanes]) * Q_SCALE
            dob = _heads_to_lanes(do_ref[:, lanes])
            kt = k_ref[:, band]
            p, p_sink, inv = _swa_probs(qb, kt, bias_ref[...], valid_ref[jnp.minimum(i, 1)], sink_ref[...])
            p = p * inv
            dp = _contract0(v_ref[:, band], dob)
            delta = jnp.sum(p * dp, axis=0, keepdims=True)
            dsc = p * (dp - delta)
            ds_ref[...] -= p_sink * inv * delta
            db_ref[...] += dsc
            dscb = dsc.astype(BF16)
            dq_t = jnp.dot(kt, dscb, preferred_element_type=F32) * Q_SCALE
            dq_ref[:, lanes] = _lanes_to_heads(dq_t).astype(dq_ref.dtype)
            dk_ref[:, band] += _contract1(qb, dscb)
            dv_ref[:, band] += _contract1(dob, p.astype(BF16))

    qspec = pl.BlockSpec((rows, cols), lambda g, i: (g, i))
    kspec = pl.BlockSpec((None, dh, sp), lambda g, i: (g, 0, 0))
    bspec = pl.BlockSpec((None, 2 * WINDOW, 8 * WINDOW), lambda g, i: (g, 0, 0))
    sspec = pl.BlockSpec((None, 1, 8 * WINDOW), lambda g, i: (g, 0, 0))
    return pl.pallas_call(
        body, name=name, grid=(ng, s // cols),
        in_specs=[qspec, kspec, kspec, bspec, pl.BlockSpec((2, 2 * WINDOW, 8 * WINDOW), lambda g, i: (0, 0, 0)), sspec,
                  qspec, kspec, kspec],
        out_specs=[qspec, kspec, kspec, bspec, sspec],
        out_shape=[jax.ShapeDtypeStruct(qt.shape, BF16), jax.ShapeDtypeStruct(kpt.shape, F32),
                   jax.ShapeDtypeStruct(kpt.shape, F32), jax.ShapeDtypeStruct(bias_t.shape, F32),
                   jax.ShapeDtypeStruct(sink_row.shape, F32)],
        compiler_params=_params(("parallel", "arbitrary")),
    )(qt, kpt, vpt, bias_t, _band_valid(), sink_row, dot_, dk_in, dv_in)


def _bucket_onehot():
    qi = np.arange(WINDOW)[:, None]
    kj = np.arange(2 * WINDOW)[None, :]
    n = np.maximum(qi + WINDOW - kj, 0)
    max_exact = N_BUCKETS // 2
    nf = np.maximum(n, 1).astype(np.float64)
    val = np.log(nf / max_exact) / math.log(WINDOW / max_exact) * (N_BUCKETS - max_exact)
    assert np.all(np.abs(val - np.round(val))[(n > max_exact) & (n < WINDOW)] > 1e-3)
    large = np.minimum(max_exact + val.astype(np.int64), N_BUCKETS - 1)
    bucket = np.where(n < max_exact, n, large).reshape(-1)
    onehot = np.zeros((128, bucket.size), np.float32)
    onehot[bucket, np.arange(bucket.size)] = 1.0
    return onehot


def _split3(x):
    a = x.astype(BF16)
    r = x - a.astype(F32)
    b = r.astype(BF16)
    c = (r - b.astype(F32)).astype(BF16)
    return a, b, c


def bias_table(rel_bias):
    nh = rel_bias.shape[1]
    oh = jnp.asarray(_bucket_onehot(), BF16)
    n = oh.shape[1]
    tn = 4096
    rb = jnp.zeros((nh, 128), F32).at[:, :N_BUCKETS].set(rel_bias.T)

    def body(rb_ref, oh_ref, o_ref):
        o_ref[...] = sum(jnp.dot(t, oh_ref[...], preferred_element_type=F32) for t in _split3(rb_ref[...]))

    return pl.pallas_call(
        body, name="bias_table", grid=(n // tn,),
        in_specs=[pl.BlockSpec((nh, 128), lambda i: (0, 0)), pl.BlockSpec((128, tn), lambda i: (0, i))],
        out_specs=pl.BlockSpec((nh, tn), lambda i: (0, i)),
        out_shape=jax.ShapeDtypeStruct((nh, n), F32),
        compiler_params=_params(("parallel",)),
    )(rb, oh)


def bias_table_grad(db0, db1):
    nh, n = db0.shape
    oh = jnp.asarray(_bucket_onehot(), BF16)
    tn = 4096

    def body(a_ref, b_ref, oh_ref, o_ref):
        @pl.when(pl.program_id(0) == 0)
        def _():
            o_ref[...] = jnp.zeros_like(o_ref)

        o_ref[...] += sum(lax.dot_general(t, oh_ref[...], (((1,), (1,)), ((), ())), preferred_element_type=F32)
                          for t in _split3(a_ref[...] + b_ref[...]))

    blk = pl.BlockSpec((nh, tn), lambda i: (0, i))
    return pl.pallas_call(
        body, name="bias_table_grad", grid=(n // tn,),
        in_specs=[blk, blk, pl.BlockSpec((128, tn), lambda i: (0, i))],
        out_specs=pl.BlockSpec((nh, 128), lambda i: (0, 0)),
        out_shape=jax.ShapeDtypeStruct((nh, 128), F32),
        compiler_params=_params(("arbitrary",)),
    )(db0, db1, oh)


def _owner_view(ref, name, d):
    if name == 'a_norm':
        return ref.at[d]
    if name in COL_SHARDED:
        n = ref.shape[2] // N_DEV
        return ref.at[:, :, pl.ds(pl.multiple_of(d * n, 128), n)]
    return ref.at[:, d]


def _place():
    return lax.axis_index("x"), lax.axis_index("y"), lax.axis_index("c")


def _dev(p):
    return 4 * p[0] + 2 * p[1] + p[2]


def _remote(src, dst, send_sem, recv_sem, to):
    return pltpu.make_async_remote_copy(src_ref=src, dst_ref=dst, send_sem=send_sem, recv_sem=recv_sem,
                                        device_id=to, device_id_type=MESH)


def _dma_sems(*shapes):
    return [pltpu.SemaphoreType.DMA(sh) for sh in shapes]


def comm_call(name, build, ins, out_shapes, sems):
    n_in, n_out = len(ins), len(out_shapes)

    def body(*refs):
        copies = build(refs[:n_in], refs[n_in:n_in + n_out], *refs[n_in + n_out:])
        for cp in copies:
            cp.start()
        for cp in copies:
            cp.wait()

    hbm = pl.BlockSpec(memory_space=pl.ANY)
    return pl.pallas_call(
        body, name=name, in_specs=[hbm] * n_in, out_specs=[hbm] * n_out, out_shape=list(out_shapes),
        scratch_shapes=sems,
    )(*ins)


def all_gather_weights(names, shards, full_shapes):
    n = len(names)

    def body(*refs):
        ins, outs = refs[:n], refs[n:2 * n]
        send_sems, recv_sems, local_sems = refs[2 * n:]
        x, y, c = _place()
        me, sibling = (x, y, c), (x, y, 1 - c)
        chips = [(1 - x, y), (x, 1 - y), (1 - x, 1 - y)]

        def copy(t, k, block, to, src=None):
            dst = _owner_view(outs[t], names[t], _dev(block))
            return _remote(dst if src is None else src, dst, send_sems.at[t, k], recv_sems.at[t, k], to)

        mine = [pltpu.make_async_copy(ins[t], _owner_view(outs[t], names[t], _dev(me)), local_sems.at[t])
                for t in range(n)]
        for cp in mine:
            cp.start()
        first = []
        for t in range(n):
            first.append(copy(t, 0, me, sibling, src=ins[t]))
            first += [copy(t, 1 + j, me, (*chip, c), src=ins[t]) for j, chip in enumerate(chips)]
        for cp in first:
            cp.start()
        passed = []
        for j, chip in enumerate(chips):
            for t in range(n):
                copy(t, 1 + j, (*chip, c), me).wait_recv()
                fwd = copy(t, 4 + j, (*chip, c), sibling)
                fwd.start()
                passed.append(fwd)
        for t in range(n):
            copy(t, 0, sibling, me).wait_recv()
            for j, chip in enumerate(chips):
                copy(t, 4 + j, (*chip, 1 - c), me).wait_recv()
        for cp in first + passed:
            cp.wait_send()
        for cp in mine:
            cp.wait()

    hbm = pl.BlockSpec(memory_space=pl.ANY)
    return pl.pallas_call(
        body, name="all_gather_layer0",
        in_specs=[hbm] * n, out_specs=[hbm] * n,
        out_shape=[jax.ShapeDtypeStruct(full_shapes[t], shards[t].dtype) for t in range(n)],
        scratch_shapes=_dma_sems((n, 7), (n, 7), (n,)),
    )(*shards)


def ag_direct(names, shards, full_shapes):
    n = len(names)

    def build(ins, outs, send_sems, recv_sems, local_sems, fwd_send_sems, fwd_recv_sems):
        x, y, c = _place()
        peers = [(x, y, 1 - c), (1 - x, y, c), (x, 1 - y, c), (1 - x, 1 - y, c)]
        copies = []
        for t in range(n):
            dst = _owner_view(outs[t], names[t], _dev((x, y, c)))
            copies.append(pltpu.make_async_copy(ins[t], dst, local_sems.at[t]))
            copies += [_remote(ins[t], dst, send_sems.at[t, k], recv_sems.at[t, k], to) for k, to in enumerate(peers)]
        return copies

    def forward(ins, outs, send_sems, recv_sems, local_sems, fwd_send_sems, fwd_recv_sems):
        x, y, c = _place()
        copies = []
        for t in range(n):
            for k, chip in enumerate([(1 - x, y), (x, 1 - y), (1 - x, 1 - y)]):
                view = _owner_view(outs[t], names[t], _dev((*chip, c)))
                copies.append(_remote(view, view, fwd_send_sems.at[t, k], fwd_recv_sems.at[t, k], (x, y, 1 - c)))
        return copies

    return Carry(build, shards, [jax.ShapeDtypeStruct(full_shapes[t], shards[t].dtype) for t in range(n)],
                 _dma_sems((n, 4), (n, 4), (n,), (n, 3), (n, 3)), then=forward)


def sibling_exchange(names, grads, part_shapes):
    n = len(names)

    def build(ins, outs, send_sems, recv_sems):
        x, y, c = _place()
        return [_remote(_owner_view(ins[t], names[t], 2 * q + 1 - c), outs[t].at[q], send_sems.at[t, q],
                        recv_sems.at[t, q], (x, y, 1 - c)) for t in range(n) for q in range(4)]

    return Carry(build, grads, [jax.ShapeDtypeStruct((4,) + part_shapes[t], BF16) for t in range(n)],
                 _dma_sems((n, 4), (n, 4)))


def chip_exchange(names, parts, part_shapes):
    n = len(names)

    def build(ins, outs, send_sems, recv_sems):
        x, y, c = _place()
        chips = [(1 - x, y), (x, 1 - y), (1 - x, 1 - y)]
        return [_remote(ins[t].at[2 * chip[0] + chip[1]], outs[t].at[k], send_sems.at[t, k], recv_sems.at[t, k],
                        (*chip, c)) for t in range(n) for k, chip in enumerate(chips)]

    return Carry(build, parts, [jax.ShapeDtypeStruct((3,) + part_shapes[t], BF16) for t in range(n)],
                 _dma_sems((n, 3), (n, 3)))


def all_gather_rows(x):
    r, w = x.shape

    def body(x_ref, out_ref, send_sems, recv_sems, local_sem):
        px, py, pc = _place()
        me = 4 * px + 2 * py + pc
        mine = pltpu.make_async_copy(x_ref, out_ref.at[me], local_sem)
        mine.start()
        copies = []
        for k in range(1, N_DEV):
            peer = (px ^ (k >> 2), py ^ ((k >> 1) & 1), pc ^ (k & 1))
            copies.append(pltpu.make_async_remote_copy(
                src_ref=x_ref, dst_ref=out_ref.at[me], send_sem=send_sems.at[k - 1], recv_sem=recv_sems.at[k - 1],
                device_id=peer, device_id_type=MESH))
        for cp in copies:
            cp.start()
        for k in range(1, N_DEV):
            peer_idx = me ^ k
            pltpu.make_async_remote_copy(
                src_ref=x_ref, dst_ref=out_ref.at[peer_idx], send_sem=send_sems.at[k - 1],
                recv_sem=recv_sems.at[k - 1], device_id=(px, py, pc), device_id_type=MESH).wait_recv()
        for cp in copies:
            cp.wait_send()
        mine.wait()

    vmem = pl.BlockSpec(memory_space=pltpu.VMEM)
    return pl.pallas_call(
        body, name="all_gather_small_grads",
        in_specs=[vmem], out_specs=vmem,
        out_shape=jax.ShapeDtypeStruct((N_DEV, r, w), x.dtype),
        scratch_shapes=[pltpu.SemaphoreType.DMA((N_DEV - 1,)), pltpu.SemaphoreType.DMA((N_DEV - 1,)),
                        pltpu.SemaphoreType.DMA],
    )(x)


def _adamw(w, g, m, v):
    m = ADAM_B1 * m + (1.0 - ADAM_B1) * g
    v = ADAM_B2 * v + (1.0 - ADAM_B2) * (g * g)
    m_hat = m / (1.0 - ADAM_B1 ** ADAM_STEP)
    v_hat = v / (1.0 - ADAM_B2 ** ADAM_STEP)
    return -ADAM_LR * (m_hat / (jnp.sqrt(v_hat) + ADAM_EPS) + ADAM_WD * w), m, v


def sibling_sum(name, col, grads, recv, core):
    _, nl, rows, cols = recv.shape
    tr = _tile(rows, 512)
    rspec = pl.BlockSpec((None, None, tr, cols), lambda q, l, i, c_ref: (q, l, i, 0))
    if col:
        gspec = pl.BlockSpec((None, tr, cols), lambda q, l, i, c_ref: (l, i, 2 * q + c_ref[0]))
    else:
        gspec = pl.BlockSpec((None, None, tr, cols), lambda q, l, i, c_ref: (l, 2 * q + c_ref[0], i, 0))

    def body(c_ref, g_ref, r_ref, o_ref):
        del c_ref
        o_ref[...] = (g_ref[...].astype(F32) + r_ref[...].astype(F32)).astype(BF16)

    return pl.pallas_call(
        body, name=name,
        grid_spec=pltpu.PrefetchScalarGridSpec(num_scalar_prefetch=1, grid=(4, nl, rows // tr),
                                               in_specs=[gspec, rspec], out_specs=rspec),
        out_shape=jax.ShapeDtypeStruct(recv.shape, BF16),
        compiler_params=_params(("parallel", "parallel", "parallel")),
    )(core.reshape(1), grads, recv)


def reduce_adamw(name, parts, recv, chip, w, m, v, l0, prev):
    _, nl, rows, cols = parts.shape
    tr = _tile(rows, 256)

    def body(q_ref, p_ref, r_ref, w_ref, m_ref, v_ref, *rest):
        del q_ref
        g_out, d_out, m_out, v_out = rest[-4:]
        g = ((p_ref[...].astype(F32) + r_ref[0].astype(F32)) + r_ref[1].astype(F32)) + r_ref[2].astype(F32)
        d, mn, vn = _adamw(w_ref[...], g, m_ref[...], v_ref[...])
        g_out[...] = g
        d_out[...] = d
        m_out[...] = mn
        v_out[...] = vn

    blk = pl.BlockSpec((None, tr, cols), lambda l, i, q_ref: (l0 + l, i, 0))
    prev = list(prev) if prev else []
    return pl.pallas_call(
        body, name=name,
        grid_spec=pltpu.PrefetchScalarGridSpec(
            num_scalar_prefetch=1, grid=(nl, rows // tr),
            in_specs=[pl.BlockSpec((None, None, tr, cols), lambda l, i, q_ref: (q_ref[0], l, i, 0)),
                      pl.BlockSpec((3, None, tr, cols), lambda l, i, q_ref: (0, l, i, 0)), blk, blk, blk]
            + [pl.BlockSpec(memory_space=pl.ANY)] * len(prev),
            out_specs=[blk] * 4),
        out_shape=[jax.ShapeDtypeStruct(w.shape, F32)] * 4,
        input_output_aliases={6 + i: i for i in range(len(prev))},
        compiler_params=_params(("parallel", "parallel")),
    )(chip.reshape(1), parts, recv, w, m, v, *prev)


def small_adamw(name, gathered, w, m, v):
    _, r, c = gathered.shape

    def body(ga_ref, w_ref, m_ref, v_ref, g_out, d_out, m_out, v_out):
        g = ga_ref[0]
        for d in range(1, N_DEV):
            g = g + ga_ref[d]
        dl, mn, vn = _adamw(w_ref[...], g, m_ref[...], v_ref[...])
        g_out[...] = g
        d_out[...] = dl
        m_out[...] = mn
        v_out[...] = vn

    return pl.pallas_call(
        body, name=name,
        out_shape=[jax.ShapeDtypeStruct((r, c), F32)] * 4,
        compiler_params=_params(),
    )(gathered, w, m, v)


def _rms(x, g):
    return x * lax.rsqrt(jnp.mean(x * x, axis=-1, keepdims=True) + EPS) * g


def _rms_bwd_epilogue(dn, x, dres, g):
    r = lax.rsqrt(jnp.mean(x * x, axis=-1, keepdims=True) + EPS)
    xh = x * r
    dyg = dn * g
    dx = dres + r * (dyg - xh * jnp.mean(dyg * xh, axis=-1, keepdims=True))
    return dx, dx, jnp.sum(dn * xh, axis=0, keepdims=True), jnp.sum(dx, axis=0, keepdims=True)


def _residual_then_norms(n_terms):
    def epilogue(acc, *ex):
        h = acc
        for t in ex[:n_terms]:
            h = h + t
        return (h,) + tuple(_rms(h, g) for g in ex[n_terms:])
    return epilogue


def local_step(x, target, small, ex):
    s, d = x.shape
    n_a, n_b = small['a_norm'].shape[0], small['b_norm'].shape[0]
    sg = {}
    gb = {}

    def fwd_mm(name, a, wname, layer, epilogue, extras, out_dtypes, **kw):
        return mm_nn(name, a, *ex.weight(wname, layer), epilogue, extras, out_dtypes, **kw)

    def dx_mm(name, dy, wname, layer, epilogue, extras, out_dtypes, **kw):
        return mm_nt(name, dy, *ex.weight(wname, layer), epilogue, extras, out_dtypes, **kw)

    def dw_mm(name, a, dy, wname, layer, **kw):
        key, slab, shape = ex.grad(wname, layer)
        gb[key] = mm_tn(name, a, dy, gb.get(key), shape, slab, **kw)

    plain = lambda acc: (acc,)
    plus_col = lambda acc, b: (acc + b,)

    bias_flat = bias_table(small['rel_bias'])
    bias_t = bias_flat.reshape(2, 8, WINDOW, 2 * WINDOW).transpose(0, 3, 1, 2).reshape(2, 2 * WINDOW, 8 * WINDOW)
    sink_rows = [jnp.repeat(small['b_sinks'][j], WINDOW).reshape(2, 1, 8 * WINDOW) for j in range(n_b)]

    gain = lambda g: g.reshape(1, -1)

    def mlp_fwd(h, n2, layer, next_gains):
        u, a = fwd_mm(f"mlp_up_fwd{layer}", n2, 'mlp_up', layer,
                      lambda acc: (acc, jnp.square(jnp.maximum(acc, 0.0))), (), (BF16, BF16))
        h2, *nexts = fwd_mm(f"mlp_down_fwd{layer}", a, 'mlp_down', layer, _residual_then_norms(1),
                            (h, *[gain(g) for g in next_gains]), (F32,) + (BF16,) * len(next_gains))
        return h2, nexts, (n2, u, a)

    h = x
    saved = []
    n1 = rms_fwd("a_norm_fwd0", h, small['a_norm'][0])
    for l in range(n_a):
        (qkvt,) = fwd_mm(f"a_qkv_fwd{l}", n1, 'a_wqkv', l, plain, (), (BF16,), out_t=True)
        qkvt = qkvt.reshape(3 * d // HEAD_DIM, HEAD_DIM, s)
        o_t, rtab, carried = sb_fwd(f"sb_fwd{l}", qkvt, ex.fwd_carry(l))
        ex.fwd_done(l, carried)
        o_t = o_t.reshape(d, s)
        h_mid, n2 = fwd_mm(f"a_wo_fwd{l}", o_t, 'a_wo', l, _residual_then_norms(1),
                           (h, gain(small['mlp_norm'][l])), (F32, BF16), a_t=True)
        next_gains = [small['a_norm'][l + 1]] if l + 1 < n_a else [small['b_norm'][0], small['kv_norm']]
        h_out, nexts, mlp_saved = mlp_fwd(h_mid, n2, l, next_gains)
        saved.append((h, n1, qkvt, o_t, rtab, h_mid, mlp_saved))
        h, n1 = h_out, nexts[0]
    h_kv, nkv = h, nexts[1]
    (kvt,) = fwd_mm("kv_fwd", nkv, 'w_kv', 0, plus_col, (small['b_kv'].reshape(-1, 1),), (BF16,), out_t=True)
    kvt = kvt.reshape(2, 2, HEAD_DIM, s)
    kpt, vpt = (jnp.pad(t, ((0, 0), (0, 0), (WINDOW, 0))) for t in (kvt[0], kvt[1]))
    for j in range(n_b):
        layer = n_a + j
        (qbt,) = fwd_mm(f"b_q_fwd{j}", n1, 'b_wq', j, plus_col, (small['b_bq'][j].reshape(-1, 1),), (BF16,),
                        out_t=True)
        o_t = swa_fwd(f"swa_fwd{j}", qbt, kpt, vpt, bias_t, sink_rows[j])
        h_mid, n2 = fwd_mm(f"b_wo_fwd{j}", o_t, 'b_wo', j, _residual_then_norms(2),
                           (h, gain(small['b_bo'][j]), gain(small['mlp_norm'][layer])), (F32, BF16), a_t=True)
        h_out, nexts, mlp_saved = mlp_fwd(h_mid, n2, layer, [small['b_norm'][j + 1]] if j + 1 < n_b else [])
        saved.append((h, n1, qbt, o_t, h_mid, mlp_saved))
        h, n1 = h_out, (nexts[0] if nexts else None)

    dh, dhb, dg_final, loss_b = loss_head(h, small['final_norm'], target)
    sg['final_norm'] = dg_final[0]
    sg['mlp_norm'] = [None] * (n_a + n_b)

    def mlp_bwd(dh, dhb, h_mid, mlp_saved, layer):
        n2, u, a = mlp_saved
        du, *carried = dx_mm(f"mlp_down_dx{layer}", dhb, 'mlp_down', layer,
                             lambda acc, uu: (acc * (2.0 * jnp.maximum(uu.astype(F32), 0.0)),), (u,), (BF16,),
                             exchange=ex.mlp_carry(layer, gb))
        ex.mlp_done(layer, carried)
        dw_mm(f"mlp_down_dw{layer}", a, dhb, 'mlp_down', layer)
        dh2, dh2b, dg, cs = dx_mm(f"mlp_up_dx{layer}", du, 'mlp_up', layer, _rms_bwd_epilogue,
                                  (h_mid, dh, gain(small['mlp_norm'][layer])), (F32, BF16), n_sums=2)
        dw_mm(f"mlp_up_dw{layer}", n2, du, 'mlp_up', layer)
        sg['mlp_norm'][layer] = dg[0]
        return dh2, dh2b, cs

    dkp = jnp.zeros(kpt.shape, F32)
    dvp = jnp.zeros(vpt.shape, F32)
    sg['b_norm'], sg['b_bq'], sg['b_bo'], sg['b_sinks'] = [None] * n_b, [None] * n_b, [None] * n_b, [None] * n_b
    dbias = [None] * n_b
    for j in reversed(range(n_b)):
        layer = n_a + j
        h_in, n1, qbt, o_t, h_mid, mlp_saved = saved[layer]
        dh, dhb, cs = mlp_bwd(dh, dhb, h_mid, mlp_saved, layer)
        sg['b_bo'][j] = cs[0]
        (do_t,) = dx_mm(f"b_wo_dx{j}", dhb, 'b_wo', j, plain, (), (BF16,), out_t=True)
        dw_mm(f"b_wo_dw{j}", o_t, dhb, 'b_wo', j, x_t=True)
        dq_t, dkp, dvp, dbias[j], dsink = swa_bwd(f"swa_bwd{j}", qbt, kpt, vpt, bias_t, sink_rows[j], do_t, dkp, dvp)
        sg['b_sinks'][j] = colsum(f"sink_grad{j}", dsink.reshape(16, WINDOW).T)[0]
        sg['b_bq'][j] = rowsum(f"b_bq_grad{j}", dq_t)
        dh, dhb, dg, _ = dx_mm(f"b_q_dx{j}", dq_t, 'b_wq', j, _rms_bwd_epilogue,
                               (h_in, dh, gain(small['b_norm'][j])), (F32, BF16), a_t=True, n_sums=2)
        dw_mm(f"b_q_dw{j}", n1, dq_t, 'b_wq', j, dy_t=True)
        sg['b_norm'][j] = dg[0]
    unt = lambda t: t.reshape(2, 2 * WINDOW, 8, WINDOW).transpose(0, 2, 3, 1).reshape(bias_flat.shape)
    sg['rel_bias'] = bias_table_grad(unt(dbias[0]), unt(dbias[1]))[:, :N_BUCKETS].T

    dkv_t = jnp.concatenate([dkp[:, :, WINDOW:], dvp[:, :, WINDOW:]], axis=0).reshape(-1, s)
    sg['b_kv'] = rowsum("b_kv_grad", dkv_t)
    dkvb = dkv_t.astype(BF16)
    dh, dhb, dg, _ = dx_mm("kv_dx", dkvb, 'w_kv', 0, _rms_bwd_epilogue, (h_kv, dh, gain(small['kv_norm'])),
                           (F32, BF16), a_t=True, n_sums=2)
    dw_mm("kv_dw", nkv, dkvb, 'w_kv', 0, dy_t=True)
    sg['kv_norm'] = dg[0]

    sg['a_norm'] = [None] * n_a
    for l in reversed(range(n_a)):
        h_in, n1, qkvt, o_t, rtab, h_mid, mlp_saved = saved[l]
        dh, dhb, _ = mlp_bwd(dh, dhb, h_mid, mlp_saved, l)
        (do_t,) = dx_mm(f"a_wo_dx{l}", dhb, 'a_wo', l, plain, (), (BF16,), out_t=True)
        dw_mm(f"a_wo_dw{l}", o_t, dhb, 'a_wo', l, x_t=True)
        dqkv_t, carried = sb_bwd(f"sb_bwd{l}", qkvt, do_t.reshape(d // HEAD_DIM, HEAD_DIM, s), rtab,
                                 ex.bwd_carry(l, gb))
        ex.bwd_done(l, carried)
        dqkv_t = dqkv_t.reshape(3 * d, s)
        dw_mm(f"a_qkv_dw{l}", n1, dqkv_t, 'a_wqkv', l, dy_t=True)
        dh, dhb, dg, _, *carried = dx_mm(f"a_qkv_dx{l}", dqkv_t, 'a_wqkv', l, _rms_bwd_epilogue,
                                         (h_in, dh, gain(small['a_norm'][l])), (F32, BF16), a_t=True, n_sums=2,
                                         exchange=ex.last_carry(gb) if l == 0 else None)
        if l == 0:
            ex.last_done(carried)
        sg['a_norm'][l] = dg[0]

    small_grads = {
        'a_norm': jnp.stack(sg['a_norm']), 'kv_norm': sg['kv_norm'], 'b_kv': sg['b_kv'],
        'b_norm': jnp.stack(sg['b_norm']), 'b_bq': jnp.stack(sg['b_bq']), 'b_sinks': jnp.stack(sg['b_sinks']),
        'b_bo': jnp.stack(sg['b_bo']), 'rel_bias': sg['rel_bias'], 'mlp_norm': jnp.stack(sg['mlp_norm']),
        'final_norm': sg['final_norm'],
    }
    return loss_b, dh, gb, small_grads


def _full_shape(name, shard_shape):
    if name in COL_SHARDED:
        return shard_shape[:2] + (N_DEV * shard_shape[2],)
    nl, r, n = shard_shape
    return (nl, N_DEV, r, n)


def _as_w3_shape(name, shard_shape):
    full = _full_shape(name, shard_shape)
    return full if name in COL_SHARDED else (full[0], full[1] * full[2], full[3])


def _as_w3(name, full):
    if name in COL_SHARDED:
        return full
    nl, nd, r, n = full.shape
    return full.reshape(nl, nd * r, n)


AG_GROUPS = {
    0: (('a_wqkv', 0, 1),),
    1: (('a_wo', 0, 2), ('mlp_up', 0, 2), ('mlp_down', 0, 2), ('a_wqkv', 1, 1)),
    2: (('mlp_up', 2, 2), ('mlp_down', 2, 2), ('b_wq', 0, 2), ('b_wo', 0, 2), ('w_kv', 0, 1)),
}
RS_GROUPS = {
    'A': (('mlp_up', 2, 2), ('mlp_down', 2, 2), ('b_wq', 0, 2), ('b_wo', 0, 2), ('w_kv', 0, 1)),
    'B1': (('a_wqkv', 1, 1), ('a_wo', 1, 1), ('mlp_up', 1, 1), ('mlp_down', 1, 1)),
    'B2': (('a_wo', 0, 1), ('mlp_up', 0, 1), ('mlp_down', 0, 1)),
    'C': (('a_wqkv', 0, 1),),
}
UNDER_MLP = {1: 'A', 0: 'B1'}
UNDER_SB_BWD = {1: ('A',), 0: ('B1', 'B2')}


class _Exchanges:
    def __init__(self, full0, shards, core, chip, w3, m3, v3):
        self.wbuf = {0: {n: _as_w3(n, full0[n]) for n, _, _ in AG_GROUPS[0]}}
        self.shards, self.core, self.chip = shards, core, chip
        self.w3, self.m3, self.v3 = w3, m3, v3
        self.shard_dims = {n: w3[n].shape[1:] for n in BIG}
        self.parts = {}
        self.gfull = {}
        self.out = {}

    def weight(self, name, layer):
        for group, members in AG_GROUPS.items():
            for n, l0, nl in members:
                if n == name and l0 <= layer < l0 + nl:
                    return self.wbuf[group][name], layer - l0
        raise KeyError((name, layer))

    def fwd_carry(self, layer):
        names = [n for n, _, _ in AG_GROUPS[layer + 1]]
        shards = [self.shards[layer + 1][n] for n in names]
        return ag_direct(names, shards, [_full_shape(n, sh.shape) for n, sh in zip(names, shards)])

    def fwd_done(self, layer, carried):
        names = [n for n, _, _ in AG_GROUPS[layer + 1]]
        self.wbuf[layer + 1] = {n: _as_w3(n, f) for n, f in zip(names, carried)}

    def grad(self, name, layer):
        for group, members in RS_GROUPS.items():
            for n, l0, nl in members:
                if n == name and l0 <= layer < l0 + nl:
                    return (group, name), layer - l0, _as_w3_shape(name, (nl,) + self.shard_dims[name])
        raise KeyError((name, layer))

    def _members(self, group):
        names = [n for n, _, _ in RS_GROUPS[group]]
        return names, [(nl,) + self.shard_dims[n] for n, _, nl in RS_GROUPS[group]]

    def _sibling_carry(self, group, gb):
        names, shapes = self._members(group)
        self.gfull[group] = [gb[(group, n)].reshape(_full_shape(n, sh)) for n, sh in zip(names, shapes)]
        return sibling_exchange(names, self.gfull[group], shapes)

    def _sibling_done(self, group, recv):
        names, _ = self._members(group)
        self.parts[group] = [sibling_sum(f"rs_sibling_sum_{group}_{n}", n in COL_SHARDED, g, r, self.core)
                             for n, g, r in zip(names, self.gfull[group], recv)]

    def _sibling_stage(self, group, gb):
        ce = self._sibling_carry(group, gb)
        self._sibling_done(group, comm_call(f"rs_sibling_exchange_{group}", ce.build, ce.ins, ce.out_shapes, ce.sems))

    def mlp_carry(self, layer, gb):
        return self._sibling_carry(UNDER_MLP[layer], gb) if layer in UNDER_MLP else None

    def mlp_done(self, layer, carried):
        if layer in UNDER_MLP:
            self._sibling_done(UNDER_MLP[layer], carried)

    def bwd_carry(self, layer, gb):
        names, parts, shapes = [], [], []
        for group in UNDER_SB_BWD[layer]:
            if group not in self.parts:
                self._sibling_stage(group, gb)
            names += self._members(group)[0]
            shapes += self._members(group)[1]
            parts += self.parts[group]
        return chip_exchange(names, parts, shapes)

    def bwd_done(self, layer, carried):
        for group in UNDER_SB_BWD[layer]:
            n = len(RS_GROUPS[group])
            self._adamw(group, carried[:n])
            carried = carried[n:]

    def last_carry(self, gb):
        self._sibling_stage('C', gb)
        names, shapes = self._members('C')
        return chip_exchange(names, self.parts['C'], shapes)

    def last_done(self, carried):
        self._adamw('C', carried)

    def _adamw(self, group, recv2):
        for (n, l0, _), p, r in zip(RS_GROUPS[group], self.parts[group], recv2):
            self.out[n] = reduce_adamw(f"adamw_{group}_{n}", p, r, self.chip, self.w3[n], self.m3[n], self.v3[n],
                                       l0, self.out.get(n))


def _pack_small(vals):
    flat = jnp.concatenate([vals[n].reshape(-1).astype(F32) for n in SMALL] + [vals['loss'].reshape(-1)])
    rows = -(-flat.shape[0] // 1024) * 8
    return jnp.pad(flat, (0, rows * 128 - flat.shape[0])).reshape(rows, 128)


def _unpack_small(packed, shapes):
    flat = packed.reshape(-1)
    out, off = {}, 0
    for n in SMALL + ['loss']:
        size = int(np.prod(shapes[n]))
        out[n] = flat[off:off + size].reshape(shapes[n])
        off += size
    return out


def kernel(x, a_norm, a_wqkv, a_wo, kv_norm, w_kv, b_kv, b_norm, b_wq, b_bq, b_sinks, b_wo, b_bo, rel_bias, mlp_norm, mlp_up, mlp_down, final_norm, loss_target, m_a_norm, m_a_wqkv, m_a_wo, m_kv_norm, m_w_kv, m_b_kv, m_b_norm, m_b_wq, m_b_bq, m_b_sinks, m_b_wo, m_b_bo, m_rel_bias, m_mlp_norm, m_mlp_up, m_mlp_down, m_final_norm, v_a_norm, v_a_wqkv, v_a_wo, v_kv_norm, v_w_kv, v_b_kv, v_b_norm, v_b_wq, v_b_bq, v_b_sinks, v_b_wo, v_b_bo, v_rel_bias, v_mlp_norm, v_mlp_up, v_mlp_down, v_final_norm):
    w = dict(a_norm=a_norm, a_wqkv=a_wqkv, a_wo=a_wo, kv_norm=kv_norm, w_kv=w_kv, b_kv=b_kv, b_norm=b_norm,
             b_wq=b_wq, b_bq=b_bq, b_sinks=b_sinks, b_wo=b_wo, b_bo=b_bo, rel_bias=rel_bias, mlp_norm=mlp_norm,
             mlp_up=mlp_up, mlp_down=mlp_down, final_norm=final_norm)
    m = dict(a_norm=m_a_norm, a_wqkv=m_a_wqkv, a_wo=m_a_wo, kv_norm=m_kv_norm, w_kv=m_w_kv, b_kv=m_b_kv,
             b_norm=m_b_norm, b_wq=m_b_wq, b_bq=m_b_bq, b_sinks=m_b_sinks, b_wo=m_b_wo, b_bo=m_b_bo,
             rel_bias=m_rel_bias, mlp_norm=m_mlp_norm, mlp_up=m_mlp_up, mlp_down=m_mlp_down, final_norm=m_final_norm)
    v = dict(a_norm=v_a_norm, a_wqkv=v_a_wqkv, a_wo=v_a_wo, kv_norm=v_kv_norm, w_kv=v_w_kv, b_kv=v_b_kv,
             b_norm=v_b_norm, b_wq=v_b_wq, b_bq=v_b_bq, b_sinks=v_b_sinks, b_wo=v_b_wo, b_bo=v_b_bo,
             rel_bias=v_rel_bias, mlp_norm=v_mlp_norm, mlp_up=v_mlp_up, mlp_down=v_mlp_down, final_norm=v_final_norm)
    px, py, pc = _place()
    me = 4 * px + 2 * py + pc
    chip = (2 * px + py).astype(jnp.int32)
    core = pc.astype(jnp.int32)

    as3 = lambda t: t[None] if t.ndim == 2 else t
    w3, m3, v3 = ({n: as3(src[n]) for n in BIG} for src in (w, m, v))
    shards = {g: {n: w3[n][l0:l0 + nl].astype(BF16) for n, l0, nl in members} for g, members in AG_GROUPS.items()}
    an_pad = jnp.zeros((8, 128), F32).at[:a_norm.shape[0]].set(a_norm)
    names0 = [n for n, _, _ in AG_GROUPS[0]]
    full0 = all_gather_weights(names0 + ['a_norm'], [shards[0][n] for n in names0] + [an_pad],
                               [_full_shape(n, shards[0][n].shape) for n in names0] + [(N_DEV, 8, 128)])
    full0 = dict(zip(names0 + ['a_norm'], full0))
    n_a = a_norm.shape[0]
    small = {n: w[n] for n in SMALL}
    small['a_norm'] = full0['a_norm'][:, :n_a].transpose(1, 0, 2).reshape(n_a, -1)

    ex = _Exchanges(full0, shards, core, chip, w3, m3, v3)
    loss_b, grad_x, gb, sgrads = local_step(x[0], loss_target[0], small, ex)
    out = {n: [t.reshape(w[n].shape) for t in bufs] for n, bufs in ex.out.items()}

    sgrads['loss'] = loss_b[0, :1]
    gathered = all_gather_rows(_pack_small(sgrads))
    shapes = {n: w[n].shape for n in SMALL}
    shapes['a_norm'] = (n_a, a_norm.shape[1] * N_DEV)
    shapes['loss'] = (1,)
    zeros1 = jnp.zeros((1,), F32)

    def packed(src):
        vals = {n: src[n] for n in SMALL}
        vals['a_norm'] = jnp.zeros(shapes['a_norm'], F32)
        vals['loss'] = zeros1
        return _pack_small(vals)

    sm = small_adamw("adamw_small", gathered, packed(w), packed(m), packed(v))
    sm = [_unpack_small(t, shapes) for t in sm]
    g_an = lax.dynamic_slice_in_dim(sm[0]['a_norm'], me * a_norm.shape[1], a_norm.shape[1], axis=1)
    pad = lambda t: jnp.zeros((8, 128), F32).at[:n_a].set(t)
    gathered_an = jnp.zeros((N_DEV, 8, 128), F32).at[0].set(pad(g_an))
    an = small_adamw("adamw_a_norm", gathered_an, pad(a_norm), pad(m_a_norm), pad(v_a_norm))
    for i in range(4):
        sm[i]['a_norm'] = an[i][:n_a]
    for n in BIG:
        for i in range(4):
            sm[i][n] = out[n][i]
    loss = sm[0]['loss'][0]
    return (loss, grad_x[None], *[sm[0][n] for n in WEIGHTS], *[sm[1][n] for n in WEIGHTS],
            *[sm[2][n] for n in WEIGHTS], *[sm[3][n] for n in WEIGHTS])
```

```python
import math

import numpy as np
import jax
import jax.numpy as jnp
from jax import lax
from jax.experimental import pallas as pl
from jax.experimental.pallas import tpu as pltpu

F32 = jnp.float32
BF16 = jnp.bfloat16
MESH = pl.DeviceIdType.MESH

N_DEV = 8
HEAD_DIM = 64
WINDOW = 128
N_BUCKETS = 32
EPS = 1e-5
NEG_INF = -1e30
Q_SCALE = 1.0 / math.sqrt(HEAD_DIM)
LOG2E = 1.4426950408889634

ADAM_LR, ADAM_B1, ADAM_B2, ADAM_EPS, ADAM_WD, ADAM_STEP = 0.001, 0.9, 0.999, 1e-08, 0.01, 10

SB_BQ = 512
SB_BK = 128
SB_DEAD = 160.0
SB_UNSEEN = 1e30
ROW_TILE = 512
VMEM_LIMIT = 56 * 1024 * 1024

WEIGHTS = ['a_norm', 'a_wqkv', 'a_wo', 'kv_norm', 'w_kv', 'b_kv', 'b_norm', 'b_wq', 'b_bq', 'b_sinks', 'b_wo',
           'b_bo', 'rel_bias', 'mlp_norm', 'mlp_up', 'mlp_down', 'final_norm']
BIG = ['a_wqkv', 'a_wo', 'w_kv', 'b_wq', 'b_wo', 'mlp_up', 'mlp_down']
COL_SHARDED = ('a_wqkv', 'mlp_up')
SMALL = ['a_norm', 'kv_norm', 'b_kv', 'b_norm', 'b_bq', 'b_sinks', 'b_bo', 'rel_bias', 'mlp_norm', 'final_norm']


def _params(sem=None):
    return pltpu.CompilerParams(dimension_semantics=sem, vmem_limit_bytes=VMEM_LIMIT)


def _pick(n, cands):
    for c in cands:
        if n % c == 0:
            return c
    raise ValueError(n)


def _tile(n, want):
    return n if n <= want else _pick(n, (want, want // 2, want // 4))


MM_TILE_BUDGET = 36 * 1024 * 1024


def _row_tile(m, contraction, cols, streams):
    weight = 2 * contraction * cols * 2
    for rows in (2048, 1024, 512):
        if m % rows == 0 and weight + 2 * rows * (2 * contraction + cols * sum(streams)) <= MM_TILE_BUDGET:
            return rows
    return _tile(m, 512)


def mm_nn(name, a, w3, layer, epilogue, extras, out_dtypes, a_t=False, out_t=False):
    k, m = a.shape if a_t else a.shape[::-1]
    _, kw, n = w3.shape
    assert kw == k
    tn = _tile(n, 1024)
    tm = _row_tile(m, k, tn, [jnp.dtype(t).itemsize for t in out_dtypes]
                   + [e.dtype.itemsize for e in extras if e.size == m * n])
    ne, no = len(extras), len(out_dtypes)
    a_dim = 0 if a_t else 1

    def body(a_ref, w_ref, *rest):
        ex, outs = rest[:ne], rest[ne:ne + no]
        if out_t:
            acc = lax.dot_general(w_ref[...], a_ref[...], (((0,), (a_dim,)), ((), ())), preferred_element_type=F32)
        else:
            acc = lax.dot_general(a_ref[...], w_ref[...], (((a_dim,), (0,)), ((), ())), preferred_element_type=F32)
        for o, r in zip(outs, epilogue(acc, *[e[...] for e in ex])):
            o[...] = r.astype(o.dtype)

    if out_t:
        tile = pl.BlockSpec((tn, tm), lambda i, j: (j, i))
        vec = pl.BlockSpec((tn, 1), lambda i, j: (j, 0))
        out_shape = (n, m)
    else:
        tile = pl.BlockSpec((tm, tn), lambda i, j: (i, j))
        vec = pl.BlockSpec((1, tn), lambda i, j: (0, j))
        out_shape = (m, n)
    a_spec = pl.BlockSpec((k, tm), lambda i, j: (0, i)) if a_t else pl.BlockSpec((tm, k), lambda i, j: (i, 0))
    return pl.pallas_call(
        body, name=name, grid=(m // tm, n // tn),
        in_specs=[a_spec, pl.BlockSpec((None, k, tn), lambda i, j: (layer, 0, j))]
        + [tile if e.shape == out_shape else vec for e in extras],
        out_specs=[tile] * no,
        out_shape=[jax.ShapeDtypeStruct(out_shape, d) for d in out_dtypes],
        compiler_params=_params(("parallel", "parallel")),
    )(a, w3, *extras)


def mm_nt(name, dy, w3, layer, epilogue, extras, out_dtypes, a_t=False, out_t=False, n_sums=0, exchange=None):
    n, m = dy.shape if a_t else dy.shape[::-1]
    _, k, nw = w3.shape
    assert nw == n and not (out_t and n_sums)
    tko = _tile(k, 1024)
    tm = _row_tile(m, n, tko, [jnp.dtype(t).itemsize for t in out_dtypes]
                   + [e.dtype.itemsize for e in extras if e.size == m * k])
    ne, no = len(extras), len(out_dtypes)
    a_dim = 0 if a_t else 1

    def body(a_ref, w_ref, *rest):
        ex = rest[:ne]
        at = lambda step: (pl.program_id(0) == step[0]) & (pl.program_id(1) == step[1])
        results, _, start_carried, wait_carried = _carried(exchange, rest[ne:], no + n_sums, 0, at((0, 0)),
                                                           at((m // tm - 1, k // tko - 1)))
        outs, sums = results[:no], results[no:]
        start_carried()
        if out_t:
            acc = lax.dot_general(w_ref[...], a_ref[...], (((1,), (a_dim,)), ((), ())), preferred_element_type=F32)
        else:
            acc = lax.dot_general(a_ref[...], w_ref[...], (((a_dim,), (1,)), ((), ())), preferred_element_type=F32)
        res = epilogue(acc, *[e[...] for e in ex])
        for o, v in zip(outs, res):
            o[...] = v.astype(o.dtype)
        if n_sums:
            @pl.when(pl.program_id(0) == 0)
            def _():
                for o in sums:
                    o[...] = jnp.zeros_like(o)

            for o, v in zip(sums, res[no:]):
                o[...] += v
        wait_carried()

    if out_t:
        tile = pl.BlockSpec((tko, tm), lambda i, ko: (ko, i))
        out_shape = (k, m)
    else:
        tile = pl.BlockSpec((tm, tko), lambda i, ko: (i, ko))
        out_shape = (m, k)
    vec = pl.BlockSpec((1, tko), lambda i, ko: (0, ko))
    a_spec = pl.BlockSpec((n, tm), lambda i, ko: (0, i)) if a_t else pl.BlockSpec((tm, n), lambda i, ko: (i, 0))
    hbm = pl.BlockSpec(memory_space=pl.ANY)
    c_ins, c_outs, c_sems = (exchange.ins, exchange.out_shapes, exchange.sems) if exchange else ([], [], [])
    sequential = n_sums or exchange
    return pl.pallas_call(
        body, name=name, grid=(m // tm, k // tko),
        in_specs=[a_spec, pl.BlockSpec((None, tko, n), lambda i, ko: (layer, ko, 0))]
        + [tile if e.shape == out_shape else vec for e in extras] + [hbm] * len(c_ins),
        out_specs=[tile] * no + [vec] * n_sums + [hbm] * len(c_outs),
        out_shape=[jax.ShapeDtypeStruct(out_shape, d) for d in out_dtypes] + [jax.ShapeDtypeStruct((1, k), F32)] * n_sums
        + c_outs,
        scratch_shapes=c_sems,
        compiler_params=_params(("arbitrary" if sequential else "parallel", "arbitrary" if exchange else "parallel")),
    )(dy, w3, *extras, *c_ins)


def mm_tn(name, x, dy, gbuf, shape, layer, x_t=False, dy_t=False):
    k, s = x.shape if x_t else x.shape[::-1]
    _, kw, n = shape
    assert kw == k and dy.shape == ((n, s) if dy_t else (s, n))
    tkk = _tile(k, 512)
    tn = _tile(n, 1024)

    def body(x_ref, dy_ref, *rest):
        g_out = rest[-1]
        g_out[...] = lax.dot_general(x_ref[...], dy_ref[...], (((1 if x_t else 0,), (1 if dy_t else 0,)), ((), ())),
                                     preferred_element_type=F32).astype(g_out.dtype)

    prev = [] if gbuf is None else [gbuf]
    x_spec = pl.BlockSpec((tkk, s), lambda ki, j: (ki, 0)) if x_t else pl.BlockSpec((s, tkk), lambda ki, j: (0, ki))
    dy_spec = pl.BlockSpec((tn, s), lambda ki, j: (j, 0)) if dy_t else pl.BlockSpec((s, tn), lambda ki, j: (0, j))
    return pl.pallas_call(
        body, name=name, grid=(k // tkk, n // tn),
        in_specs=[x_spec, dy_spec] + [pl.BlockSpec(memory_space=pl.ANY)] * len(prev),
        out_specs=pl.BlockSpec((None, tkk, tn), lambda ki, j: (layer, ki, j)),
        out_shape=jax.ShapeDtypeStruct(shape, BF16),
        input_output_aliases={2: 0} if prev else {},
        compiler_params=_params(("parallel", "parallel")),
    )(x, dy, *prev)


def rms_fwd(name, h, g):
    s, d = h.shape
    tr = _pick(s, (ROW_TILE, 256, 128))

    def body(h_ref, g_ref, o_ref):
        x = h_ref[...]
        r = lax.rsqrt(jnp.mean(x * x, axis=-1, keepdims=True) + EPS)
        o_ref[...] = (x * r * g_ref[...]).astype(o_ref.dtype)

    return pl.pallas_call(
        body, name=name, grid=(s // tr,),
        in_specs=[pl.BlockSpec((tr, d), lambda i: (i, 0)), pl.BlockSpec((1, d), lambda i: (0, 0))],
        out_specs=pl.BlockSpec((tr, d), lambda i: (i, 0)),
        out_shape=jax.ShapeDtypeStruct((s, d), BF16),
        compiler_params=_params(("parallel",)),
    )(h, g.reshape(1, d))


def loss_head(h, g, target):
    s, d = h.shape
    tr = _pick(s, (ROW_TILE, 256, 128))

    def body(h_ref, g_ref, t_ref, dx_ref, dxb_ref, dg_ref, loss_ref):
        i = pl.program_id(0)
        x = h_ref[...]
        r = lax.rsqrt(jnp.mean(x * x, axis=-1, keepdims=True) + EPS)
        xh = x * r
        gw = g_ref[...]
        err = xh * gw - t_ref[...]
        dn_ = err * (1.0 / d)
        dyg = dn_ * gw
        dx = r * (dyg - xh * jnp.mean(dyg * xh, axis=-1, keepdims=True))
        dx_ref[...] = dx
        dxb_ref[...] = dx.astype(BF16)

        @pl.when(i == 0)
        def _():
            dg_ref[...] = jnp.zeros_like(dg_ref)
            loss_ref[...] = jnp.zeros_like(loss_ref)

        dg_ref[...] += jnp.sum(dn_ * xh, axis=0, keepdims=True)
        per_row = jnp.sum(err * err, axis=-1, keepdims=True) * (0.5 / d)
        loss_ref[...] += jnp.broadcast_to(jnp.sum(per_row, axis=0, keepdims=True), loss_ref.shape)

    row = pl.BlockSpec((tr, d), lambda i: (i, 0))
    vec = pl.BlockSpec((1, d), lambda i: (0, 0))
    return pl.pallas_call(
        body, name="loss_head", grid=(s // tr,),
        in_specs=[row, vec, row],
        out_specs=[row, row, vec, pl.BlockSpec((1, 128), lambda i: (0, 0))],
        out_shape=[jax.ShapeDtypeStruct((s, d), F32), jax.ShapeDtypeStruct((s, d), BF16),
                   jax.ShapeDtypeStruct((1, d), F32), jax.ShapeDtypeStruct((1, 128), F32)],
        compiler_params=_params(("arbitrary",)),
    )(h, g.reshape(1, d), target)


def colsum(name, x):
    s, n = x.shape
    tr = _pick(s, (ROW_TILE, 256, 128))

    def body(x_ref, o_ref):
        @pl.when(pl.program_id(0) == 0)
        def _():
            o_ref[...] = jnp.zeros_like(o_ref)

        o_ref[...] += jnp.sum(x_ref[...].astype(F32), axis=0, keepdims=True)

    return pl.pallas_call(
        body, name=name, grid=(s // tr,),
        in_specs=[pl.BlockSpec((tr, n), lambda i: (i, 0))],
        out_specs=pl.BlockSpec((1, n), lambda i: (0, 0)),
        out_shape=jax.ShapeDtypeStruct((1, n), F32),
        compiler_params=_params(("arbitrary",)),
    )(x)


def rowsum(name, x):
    n, s = x.shape
    ts = _pick(s, (1024, 512, 256, 128))

    def body(x_ref, o_ref):
        @pl.when(pl.program_id(0) == 0)
        def _():
            o_ref[...] = jnp.zeros_like(o_ref)

        o_ref[...] += jnp.sum(x_ref[...].astype(F32), axis=1, keepdims=True)

    return pl.pallas_call(
        body, name=name, grid=(s // ts,),
        in_specs=[pl.BlockSpec((n, ts), lambda i: (0, i))],
        out_specs=pl.BlockSpec((n, 1), lambda i: (0, 0)),
        out_shape=jax.ShapeDtypeStruct((n, 1), F32),
        compiler_params=_params(("arbitrary",)),
    )(x)[:, 0]


def _tri_rows(reverse):
    i = np.arange(SB_BK)
    tri = (i[None, :] >= i[:, None]) if reverse else (i[None, :] <= i[:, None])
    tri = np.concatenate([tri, tri], axis=1)
    return jnp.asarray(np.concatenate([tri, np.ones((8, 2 * SB_BK), bool)], axis=0), BF16)


def _hi_lo_rows(x):
    hi = x.astype(BF16)
    lo = (x - hi.astype(F32)).astype(BF16)
    return jnp.concatenate([hi, lo], axis=0)


def _softplus2(zs):
    neg_abs = lax.bitcast_convert_type(lax.bitcast_convert_type(zs, jnp.uint32) | jnp.uint32(0x80000000), F32)
    return jnp.maximum(zs, 0.0) + jnp.log2(1.0 + jnp.exp2(neg_abs))


def _pair_mask(first_rel_block, bq):
    key = lax.broadcasted_iota(jnp.int32, (2 * SB_BK, bq), 0) + first_rel_block * SB_BK
    qry = lax.broadcasted_iota(jnp.int32, (2 * SB_BK, bq), 1)
    return key < qry


def _row_of(table8, sub8, r):
    return jnp.sum(jnp.where(sub8 == r, table8, 0.0), axis=0, keepdims=True)


def _keys(j0):
    return pl.ds(pl.multiple_of(j0 * SB_BK, 2 * SB_BK), 2 * SB_BK)


class Carry:
    def __init__(self, build, ins, out_shapes, sems, then=None):
        self.build, self.ins, self.out_shapes, self.sems = build, list(ins), list(out_shapes), list(sems)
        self.then = then


def _together(a, b):
    assert a.then is None and b.then is None
    ni, no, ns = len(a.ins), len(a.out_shapes), len(a.sems)

    def build(ins, outs, *sems):
        return a.build(ins[:ni], outs[:no], *sems[:ns]) + b.build(ins[ni:], outs[no:], *sems[ns:])

    return Carry(build, a.ins + b.ins, a.out_shapes + b.out_shapes, a.sems + b.sems)


def _carried(carry, rest, n_out, n_scratch, first, last):
    n_ci = len(carry.ins) if carry else 0
    n_co = len(carry.out_shapes) if carry else 0
    cin, outs = rest[:n_ci], rest[n_ci:n_ci + n_out]
    cout = rest[n_ci + n_out:n_ci + n_out + n_co]
    scratch = rest[n_ci + n_out + n_co:n_ci + n_out + n_co + n_scratch]
    csems = rest[n_ci + n_out + n_co + n_scratch:]

    def start():
        if carry:
            @pl.when(first)
            def _():
                for cp in carry.build(cin, cout, *csems):
                    cp.start()

    def wait():
        if carry:
            @pl.when(last)
            def _():
                for cp in carry.build(cin, cout, *csems):
                    cp.wait()
                if carry.then:
                    second = carry.then(cin, cout, *csems)
                    for cp in second:
                        cp.start()
                    for cp in second:
                        cp.wait()

    return outs, scratch, start, wait


def _contract0(a, b):
    return lax.dot_general(a, b, (((0,), (0,)), ((), ())), preferred_element_type=F32)


def _contract1(a, b):
    return lax.dot_general(a, b, (((1,), (1,)), ((), ())), preferred_element_type=F32)


def sb_fwd(name, qkvt, exchange=None):
    nh, dh, s = qkvt.shape[0] // 3, qkvt.shape[1], qkvt.shape[2]
    bq = SB_BQ
    per_q = bq // SB_BK
    nkb = s // SB_BK
    assert s % bq == 0 and per_q == 4 and nkb % 8 == 0

    def body(q_ref, k_ref, v_ref, a_ref, *rest):
        head = pl.program_id(0)
        (o_ref, rtab_ref), (acc, zbuf, wbuf), start_carried, wait_carried = _carried(
            exchange, rest, 2, 3, head == 0, head == nh - 1)
        start_carried()
        tri = a_ref[...]
        sub8 = lax.broadcasted_iota(jnp.int32, (8, bq), 0)
        rtab_ref[...] = jnp.full(rtab_ref.shape, SB_UNSEEN, F32)
        kf = k_ref[...].astype(F32)
        k_max2 = jnp.max(jnp.sum(kf * kf, axis=0, keepdims=True), axis=1, keepdims=True)

        def query_block(i, _):
            lanes = pl.ds(pl.multiple_of(i * bq, bq), bq)
            qb = q_ref[:, lanes] * Q_SCALE
            acc[...] = jnp.zeros_like(acc)
            qf = qb.astype(F32)
            bound = jnp.sqrt(jnp.sum(qf * qf, axis=0, keepdims=True) * k_max2) * (1.001 * LOG2E)

            def scores(j0):
                return _contract0(k_ref[:, _keys(j0)], qb) * LOG2E

            def pair(j0, slot, run, rt8, mask, has_prev):
                zs = zbuf[slot]
                zbuf[1 - slot] = scores(jnp.maximum(j0 - 2, 0))
                if has_prev:
                    acc[...] += jnp.dot(v_ref[:, _keys(j0 + 2)], wbuf[1 - slot], preferred_element_type=F32)
                p = _softplus2(zs)
                if mask is not None:
                    p = jnp.where(mask, p, 0.0)
                cr1 = jnp.dot(tri, _hi_lo_rows(p[SB_BK:]), preferred_element_type=F32)
                cr0 = jnp.dot(tri, _hi_lo_rows(p[:SB_BK]), preferred_element_type=F32)
                run1 = run + cr1[SB_BK:SB_BK + 1]
                w = jnp.exp2(jnp.concatenate([zs[:SB_BK] - cr0[:SB_BK] - run1, zs[SB_BK:] - cr1[:SB_BK] - run],
                                             axis=0))
                if mask is not None:
                    w = jnp.where(mask, w, 0.0)
                wbuf[slot] = w.astype(BF16)
                rt8 = jnp.where(j0 % 8 == 6, SB_UNSEEN, rt8)
                rt8 = jnp.where(sub8 == (j0 + 1) % 8, run, jnp.where(sub8 == j0 % 8, run1, rt8))
                rtab_ref[pl.ds(pl.multiple_of((j0 // 8) * 8, 8), 8), lanes] = rt8
                return run1 + cr0[SB_BK:SB_BK + 1], rt8

            def alive(run):
                return jnp.min(run - bound) < SB_DEAD

            top = i * per_q
            zbuf[0] = scores(top + 2)
            state = (jnp.zeros((1, bq), F32), jnp.full((8, bq), SB_UNSEEN, F32))
            state = pair(top + 2, 0, *state, _pair_mask(2, bq), False)
            state = pair(top, 1, *state, _pair_mask(0, bq), True)

            def step(c):
                it, pairs, _, run, rt8 = c
                j0 = top - 2 - 4 * it
                run, rt8 = pair(j0, 0, run, rt8, None, True)
                go = alive(run)
                run, rt8 = lax.cond(go, lambda r, t: pair(j0 - 2, 1, r, t, None, True), lambda r, t: (r, t), run, rt8)
                return it + 1, pairs + 1 + go.astype(jnp.int32), go & alive(run), run, rt8

            pairs = lax.while_loop(lambda c: (c[0] < i) & c[2], step, (0, 0, alive(state[0]), *state))[1]
            acc[...] += jnp.dot(v_ref[:, _keys(top - 2 * pairs)], wbuf[(pairs + 1) % 2], preferred_element_type=F32)
            o_ref[:, lanes] = acc[...].astype(o_ref.dtype)
            return 0

        lax.fori_loop(0, s // bq, query_block, 0)
        wait_carried()

    def head_spec(offset, rows):
        return pl.BlockSpec((None, rows, s), lambda h: (h + offset, 0, 0))

    hbm = pl.BlockSpec(memory_space=pl.ANY)
    c_ins, c_outs, c_sems = (exchange.ins, exchange.out_shapes, exchange.sems) if exchange else ([], [], [])
    outs = pl.pallas_call(
        body, name=name, grid=(nh,),
        in_specs=[head_spec(0, dh), head_spec(nh, dh), head_spec(2 * nh, dh),
                  pl.BlockSpec((SB_BK + 8, 2 * SB_BK), lambda h: (0, 0))] + [hbm] * len(c_ins),
        out_specs=[head_spec(0, dh), head_spec(0, nkb)] + [hbm] * len(c_outs),
        out_shape=[jax.ShapeDtypeStruct((nh, dh, s), BF16), jax.ShapeDtypeStruct((nh, nkb, s), F32)] + c_outs,
        scratch_shapes=[pltpu.VMEM((dh, bq), F32), pltpu.VMEM((2, 2 * SB_BK, bq), F32),
                        pltpu.VMEM((2, 2 * SB_BK, bq), BF16)] + c_sems,
        compiler_params=_params(("arbitrary",)),
    )(qkvt, qkvt, qkvt, _tri_rows(True), *c_ins)
    return outs[0], outs[1], outs[2:]


def sb_bwd(name, qkvt, dot_, rtab, exchange=None):
    nh, dh, s = qkvt.shape[0] // 3, qkvt.shape[1], qkvt.shape[2]
    bq = SB_BQ
    per_q = bq // SB_BK
    nkb = s // SB_BK

    def body(qt_ref, kt_ref, vt_ref, dot_ref, rtab_ref, ar_ref, af_ref, *rest):
        head = pl.program_id(0)
        (dqkv_ref,), (dq_acc, dk_acc, dv_acc, zbuf, dwbuf, dzbuf, wbuf), start_carried, wait_carried = \
            _carried(exchange, rest, 1, 7, head == 0, head == nh - 1)
        dq_ref, dk_ref, dv_ref = dqkv_ref.at[0], dqkv_ref.at[1], dqkv_ref.at[2]
        start_carried()
        dk_acc[...] = jnp.zeros_like(dk_acc)
        dv_acc[...] = jnp.zeros_like(dv_acc)
        tri_rev = ar_ref[...][:SB_BK]
        tri_fwd = af_ref[...]
        sub8 = lax.broadcasted_iota(jnp.int32, (8, bq), 0)

        def query_block(i, _):
            lanes = pl.ds(pl.multiple_of(i * bq, bq), bq)
            qtb = qt_ref[:, lanes] * Q_SCALE
            dotb = dot_ref[:, lanes]
            dq_acc[...] = jnp.zeros_like(dq_acc)
            last_j = i * per_q + 2
            seen = jnp.max(jnp.where(rtab_ref[:, lanes] < 0.1 * SB_UNSEEN, 1.0, 0.0), axis=1, keepdims=True)
            pairs = jnp.clip((jnp.sum(seen).astype(jnp.int32) - per_q) // 2, 0, 2 * i)
            odd = pairs % 2
            first_j = i * per_q - 2 * pairs

            def issue(j0, slot):
                zbuf[slot] = _contract0(kt_ref[:, _keys(j0)], qtb) * LOG2E
                dwbuf[slot] = _contract0(vt_ref[:, _keys(j0)], dotb)

            def retire(j0, slot):
                keys = _keys(j0)
                dq_acc[...] += jnp.dot(kt_ref[:, keys], dzbuf[slot], preferred_element_type=F32)
                dk_acc[:, keys] += _contract1(qtb, dzbuf[slot])
                dv_acc[:, keys] += _contract1(dotb, wbuf[slot])

            def pair(j0, slot, g_run, mask):
                zs = zbuf[slot]
                dw = dwbuf[slot]
                issue(jnp.minimum(j0 + 2, last_j), 1 - slot)
                retire(jnp.maximum(j0 - 2, first_j), 1 - slot)
                p_raw = _softplus2(zs)
                p = p_raw if mask is None else jnp.where(mask, p_raw, 0.0)
                c0 = jnp.dot(tri_rev, _hi_lo_rows(p[:SB_BK]), preferred_element_type=F32)
                c1 = jnp.dot(tri_rev, _hi_lo_rows(p[SB_BK:]), preferred_element_type=F32)
                rt8 = rtab_ref[pl.ds(pl.multiple_of((j0 // 8) * 8, 8), 8), lanes]
                r0 = _row_of(rt8, sub8, j0 % 8)
                r1 = _row_of(rt8, sub8, (j0 + 1) % 8)
                w = jnp.exp2(jnp.concatenate([zs[:SB_BK] - c0 - r0, zs[SB_BK:] - c1 - r1], axis=0))
                if mask is not None:
                    w = jnp.where(mask, w, 0.0)
                g = w * dw
                gg0 = jnp.dot(tri_fwd, _hi_lo_rows(g[:SB_BK]), preferred_element_type=F32)
                gg1 = jnp.dot(tri_fwd, _hi_lo_rows(g[SB_BK:]), preferred_element_type=F32)
                g_run1 = g_run + gg0[SB_BK:SB_BK + 1]
                g_pre = jnp.concatenate([gg0[:SB_BK] + g_run, gg1[:SB_BK] + g_run1], axis=0)
                dz = g - jnp.exp2(zs - p_raw) * g_pre
                if mask is not None:
                    dz = jnp.where(mask, dz, 0.0)
                dzbuf[slot] = dz.astype(BF16)
                wbuf[slot] = w.astype(BF16)
                return g_run1 + gg1[SB_BK:SB_BK + 1]

            issue(first_j, odd)
            dzbuf[...] = jnp.zeros(dzbuf.shape, BF16)
            wbuf[...] = jnp.zeros(wbuf.shape, BF16)

            def step(it, g_run):
                g_run = pair(4 * it, 0, g_run, None)
                return pair(4 * it + 2, 1, g_run, None)

            g_run = lax.cond(odd == 1, lambda g: pair(first_j, 1, g, None), lambda g: g, jnp.zeros((1, bq), F32))
            g_run = lax.fori_loop(i - pairs // 2, i, step, g_run)
            g_run = pair(last_j - 2, 0, g_run, _pair_mask(0, bq))
            pair(last_j, 1, g_run, _pair_mask(2, bq))
            retire(last_j, 1)
            dq_ref[:, lanes] = (dq_acc[...] * Q_SCALE).astype(dq_ref.dtype)
            return 0

        lax.fori_loop(0, s // bq, query_block, 0)
        dk_ref[...] = dk_acc[...].astype(dk_ref.dtype)
        dv_ref[...] = dv_acc[...].astype(dv_ref.dtype)
        wait_carried()

    def head_spec(offset, rows):
        return pl.BlockSpec((None, rows, s), lambda h: (h + offset, 0, 0))

    aspec = pl.BlockSpec((SB_BK + 8, 2 * SB_BK), lambda h: (0, 0))
    pair_f32 = pltpu.VMEM((2, 2 * SB_BK, bq), F32)
    pair_bf16 = pltpu.VMEM((2, 2 * SB_BK, bq), BF16)
    hbm = pl.BlockSpec(memory_space=pl.ANY)
    c_ins, c_outs, c_sems = (exchange.ins, exchange.out_shapes, exchange.sems) if exchange else ([], [], [])
    outs = pl.pallas_call(
        body, name=name, grid=(nh,),
        in_specs=[head_spec(0, dh), head_spec(nh, dh), head_spec(2 * nh, dh), head_spec(0, dh), head_spec(0, nkb),
                  aspec, aspec] + [hbm] * len(c_ins),
        out_specs=[pl.BlockSpec((3, None, dh, s), lambda h: (0, h, 0, 0))] + [hbm] * len(c_outs),
        out_shape=[jax.ShapeDtypeStruct((3, nh, dh, s), BF16)] + c_outs,
        scratch_shapes=[pltpu.VMEM((dh, bq), F32), pltpu.VMEM((dh, s), F32), pltpu.VMEM((dh, s), F32),
                        pair_f32, pair_f32, pair_bf16, pair_bf16] + c_sems,
        compiler_params=_params(("arbitrary",)),
    )(qkvt, qkvt, qkvt, dot_, rtab, _tri_rows(True), _tri_rows(False), *c_ins)
    return outs[0], outs[1:]


SWA_QB = 2


def _band_valid():
    kj = np.arange(2 * WINDOW)[:, None]
    dist = (np.arange(8 * WINDOW)[None, :] % WINDOW) + WINDOW - kj
    inside = (dist >= 0) & (dist < WINDOW)
    return jnp.asarray(np.stack([inside & (kj >= WINDOW), inside]), F32)


def _swa_probs(qt, kt, bias_t, valid, sink):
    sc = jnp.where(valid > 0.5, _contract0(kt, qt) + bias_t, NEG_INF)
    mx = jnp.maximum(jnp.max(sc, axis=0, keepdims=True), sink)
    p = jnp.exp(sc - mx)
    p_sink = jnp.exp(sink - mx)
    inv = 1.0 / (jnp.sum(p, axis=0, keepdims=True) + p_sink)
    return p, p_sink, inv


def _band(i):
    return pl.ds(pl.multiple_of(i * WINDOW, WINDOW), 2 * WINDOW)


def _heads_to_lanes(blk):
    return jnp.concatenate([blk[r * HEAD_DIM:(r + 1) * HEAD_DIM] for r in range(8)], axis=1)


def _lanes_to_heads(t):
    return jnp.concatenate([t[:, r * WINDOW:(r + 1) * WINDOW] for r in range(8)], axis=0)


def swa_fwd(name, qt, kpt, vpt, bias_t, sink_row):
    d, s = qt.shape
    ng, dh, sp = kpt.shape
    rows, cols = d // ng, SWA_QB * WINDOW
    assert (s // WINDOW) % SWA_QB == 0

    def body(q_ref, k_ref, v_ref, bias_ref, valid_ref, sink_ref, o_ref):
        for u in range(SWA_QB):
            i = pl.program_id(1) * SWA_QB + u
            lanes = slice(u * WINDOW, (u + 1) * WINDOW)
            qb = _heads_to_lanes(q_ref[:, lanes]) * Q_SCALE
            p, _, inv = _swa_probs(qb, k_ref[:, _band(i)], bias_ref[...], valid_ref[jnp.minimum(i, 1)], sink_ref[...])
            o_t = jnp.dot(v_ref[:, _band(i)], p.astype(BF16), preferred_element_type=F32) * inv
            o_ref[:, lanes] = _lanes_to_heads(o_t).astype(o_ref.dtype)

    qspec = pl.BlockSpec((rows, cols), lambda g, i: (g, i))
    kspec = pl.BlockSpec((None, dh, sp), lambda g, i: (g, 0, 0))
    return pl.pallas_call(
        body, name=name, grid=(ng, s // cols),
        in_specs=[qspec, kspec, kspec, pl.BlockSpec((None, 2 * WINDOW, 8 * WINDOW), lambda g, i: (g, 0, 0)),
                  pl.BlockSpec((2, 2 * WINDOW, 8 * WINDOW), lambda g, i: (0, 0, 0)),
                  pl.BlockSpec((None, 1, 8 * WINDOW), lambda g, i: (g, 0, 0))],
        out_specs=qspec,
        out_shape=jax.ShapeDtypeStruct(qt.shape, BF16),
        compiler_params=_params(("parallel", "arbitrary")),
    )(qt, kpt, vpt, bias_t, _band_valid(), sink_row)


def swa_bwd(name, qt, kpt, vpt, bias_t, sink_row, dot_, dk_in, dv_in):
    d, s = qt.shape
    ng, dh, sp = kpt.shape
    rows, cols = d // ng, SWA_QB * WINDOW

    def body(q_ref, k_ref, v_ref, bias_ref, valid_ref, sink_ref, do_ref, dki_ref, dvi_ref,
             dq_ref, dk_ref, dv_ref, db_ref, ds_ref):
        @pl.when(pl.program_id(1) == 0)
        def _():
            dk_ref[...] = dki_ref[...]
            dv_ref[...] = dvi_ref[...]
            db_ref[...] = jnp.zeros_like(db_ref)
            ds_ref[...] = jnp.zeros_like(ds_ref)

        for u in range(SWA_QB):
            i = pl.program_id(1) * SWA_QB + u
            band = _band(i)
            lanes = slice(u * WINDOW, (u + 1) * WINDOW)
            qb = _heads_to_lanes(q_ref[:, lanes]) * Q_SCALE
            dob = _heads_to_lanes(do_ref[:, lanes])
            kt = k_ref[:, band]
            p, p_sink, inv = _swa_probs(qb, kt, bias_ref[...], valid_ref[jnp.minimum(i, 1)], sink_ref[...])
            p = p * inv
            dp = _contract0(v_ref[:, band], dob)
            delta = jnp.sum(p * dp, axis=0, keepdims=True)
            dsc = p * (dp - delta)
            ds_ref[...] -= p_sink * inv * delta
            db_ref[...] += dsc
            dscb = dsc.astype(BF16)
            dq_t = jnp.dot(kt, dscb, preferred_element_type=F32) * Q_SCALE
            dq_ref[:, lanes] = _lanes_to_heads(dq_t).astype(dq_ref.dtype)
            dk_ref[:, band] += _contract1(qb, dscb)
            dv_ref[:, band] += _contract1(dob, p.astype(BF16))

    qspec = pl.BlockSpec((rows, cols), lambda g, i: (g, i))
    kspec = pl.BlockSpec((None, dh, sp), lambda g, i: (g, 0, 0))
    bspec = pl.BlockSpec((None, 2 * WINDOW, 8 * WINDOW), lambda g, i: (g, 0, 0))
    sspec = pl.BlockSpec((None, 1, 8 * WINDOW), lambda g, i: (g, 0, 0))
    return pl.pallas_call(
        body, name=name, grid=(ng, s // cols),
        in_specs=[qspec, kspec, kspec, bspec, pl.BlockSpec((2, 2 * WINDOW, 8 * WINDOW), lambda g, i: (0, 0, 0)), sspec,
                  qspec, kspec, kspec],
        out_specs=[qspec, kspec, kspec, bspec, sspec],
        out_shape=[jax.ShapeDtypeStruct(qt.shape, BF16), jax.ShapeDtypeStruct(kpt.shape, F32),
                   jax.ShapeDtypeStruct(kpt.shape, F32), jax.ShapeDtypeStruct(bias_t.shape, F32),
                   jax.ShapeDtypeStruct(sink_row.shape, F32)],
        compiler_params=_params(("parallel", "arbitrary")),
    )(qt, kpt, vpt, bias_t, _band_valid(), sink_row, dot_, dk_in, dv_in)


def _bucket_onehot():
    qi = np.arange(WINDOW)[:, None]
    kj = np.arange(2 * WINDOW)[None, :]
    n = np.maximum(qi + WINDOW - kj, 0)
    max_exact = N_BUCKETS // 2
    nf = np.maximum(n, 1).astype(np.float64)
    val = np.log(nf / max_exact) / math.log(WINDOW / max_exact) * (N_BUCKETS - max_exact)
    assert np.all(np.abs(val - np.round(val))[(n > max_exact) & (n < WINDOW)] > 1e-3)
    large = np.minimum(max_exact + val.astype(np.int64), N_BUCKETS - 1)
    bucket = np.where(n < max_exact, n, large).reshape(-1)
    onehot = np.zeros((128, bucket.size), np.float32)
    onehot[bucket, np.arange(bucket.size)] = 1.0
    return onehot


def _split3(x):
    a = x.astype(BF16)
    r = x - a.astype(F32)
    b = r.astype(BF16)
    c = (r - b.astype(F32)).astype(BF16)
    return a, b, c


def bias_table(rel_bias):
    nh = rel_bias.shape[1]
    oh = jnp.asarray(_bucket_onehot(), BF16)
    n = oh.shape[1]
    tn = 4096
    rb = jnp.zeros((nh, 128), F32).at[:, :N_BUCKETS].set(rel_bias.T)

    def body(rb_ref, oh_ref, o_ref):
        o_ref[...] = sum(jnp.dot(t, oh_ref[...], preferred_element_type=F32) for t in _split3(rb_ref[...]))

    return pl.pallas_call(
        body, name="bias_table", grid=(n // tn,),
        in_specs=[pl.BlockSpec((nh, 128), lambda i: (0, 0)), pl.BlockSpec((128, tn), lambda i: (0, i))],
        out_specs=pl.BlockSpec((nh, tn), lambda i: (0, i)),
        out_shape=jax.ShapeDtypeStruct((nh, n), F32),
        compiler_params=_params(("parallel",)),
    )(rb, oh)


def bias_table_grad(db0, db1):
    nh, n = db0.shape
    oh = jnp.asarray(_bucket_onehot(), BF16)
    tn = 4096

    def body(a_ref, b_ref, oh_ref, o_ref):
        @pl.when(pl.program_id(0) == 0)
        def _():
            o_ref[...] = jnp.zeros_like(o_ref)

        o_ref[...] += sum(lax.dot_general(t, oh_ref[...], (((1,), (1,)), ((), ())), preferred_element_type=F32)
                          for t in _split3(a_ref[...] + b_ref[...]))

    blk = pl.BlockSpec((nh, tn), lambda i: (0, i))
    return pl.pallas_call(
        body, name="bias_table_grad", grid=(n // tn,),
        in_specs=[blk, blk, pl.BlockSpec((128, tn), lambda i: (0, i))],
        out_specs=pl.BlockSpec((nh, 128), lambda i: (0, 0)),
        out_shape=jax.ShapeDtypeStruct((nh, 128), F32),
        compiler_params=_params(("arbitrary",)),
    )(db0, db1, oh)


def _owner_view(ref, name, d):
    if name == 'a_norm':
        return ref.at[d]
    if name in COL_SHARDED:
        n = ref.shape[2] // N_DEV
        return ref.at[:, :, pl.ds(pl.multiple_of(d * n, 128), n)]
    return ref.at[:, d]


def _place():
    return lax.axis_index("x"), lax.axis_index("y"), lax.axis_index("c")


def _dev(p):
    return 4 * p[0] + 2 * p[1] + p[2]


def _remote(src, dst, send_sem, recv_sem, to):
    return pltpu.make_async_remote_copy(src_ref=src, dst_ref=dst, send_sem=send_sem, recv_sem=recv_sem,
                                        device_id=to, device_id_type=MESH)


def _dma_sems(*shapes):
    return [pltpu.SemaphoreType.DMA(sh) for sh in shapes]


def comm_call(name, build, ins, out_shapes, sems):
    n_in, n_out = len(ins), len(out_shapes)

    def body(*refs):
        copies = build(refs[:n_in], refs[n_in:n_in + n_out], *refs[n_in + n_out:])
        for cp in copies:
            cp.start()
        for cp in copies:
            cp.wait()

    hbm = pl.BlockSpec(memory_space=pl.ANY)
    return pl.pallas_call(
        body, name=name, in_specs=[hbm] * n_in, out_specs=[hbm] * n_out, out_shape=list(out_shapes),
        scratch_shapes=sems,
    )(*ins)


def all_gather_weights(names, shards, full_shapes):
    n = len(names)

    def body(*refs):
        ins, outs = refs[:n], refs[n:2 * n]
        send_sems, recv_sems, local_sems = refs[2 * n:]
        x, y, c = _place()
        me, sibling = (x, y, c), (x, y, 1 - c)
        chips = [(1 - x, y), (x, 1 - y), (1 - x, 1 - y)]

        def copy(t, k, block, to, src=None):
            dst = _owner_view(outs[t], names[t], _dev(block))
            return _remote(dst if src is None else src, dst, send_sems.at[t, k], recv_sems.at[t, k], to)

        mine = [pltpu.make_async_copy(ins[t], _owner_view(outs[t], names[t], _dev(me)), local_sems.at[t])
                for t in range(n)]
        for cp in mine:
            cp.start()
        first = []
        for t in range(n):
            first.append(copy(t, 0, me, sibling, src=ins[t]))
            first += [copy(t, 1 + j, me, (*chip, c), src=ins[t]) for j, chip in enumerate(chips)]
        for cp in first:
            cp.start()
        passed = []
        for j, chip in enumerate(chips):
            for t in range(n):
                copy(t, 1 + j, (*chip, c), me).wait_recv()
                fwd = copy(t, 4 + j, (*chip, c), sibling)
                fwd.start()
                passed.append(fwd)
        for t in range(n):
            copy(t, 0, sibling, me).wait_recv()
            for j, chip in enumerate(chips):
                copy(t, 4 + j, (*chip, 1 - c), me).wait_recv()
        for cp in first + passed:
            cp.wait_send()
        for cp in mine:
            cp.wait()

    hbm = pl.BlockSpec(memory_space=pl.ANY)
    return pl.pallas_call(
        body, name="all_gather_layer0",
        in_specs=[hbm] * n, out_specs=[hbm] * n,
        out_shape=[jax.ShapeDtypeStruct(full_shapes[t], shards[t].dtype) for t in range(n)],
        scratch_shapes=_dma_sems((n, 7), (n, 7), (n,)),
    )(*shards)


def ag_direct(names, shards, full_shapes):
    n = len(names)

    def build(ins, outs, send_sems, recv_sems, local_sems, fwd_send_sems, fwd_recv_sems):
        x, y, c = _place()
        peers = [(x, y, 1 - c), (1 - x, y, c), (x, 1 - y, c), (1 - x, 1 - y, c)]
        copies = []
        for t in range(n):
            dst = _owner_view(outs[t], names[t], _dev((x, y, c)))
            copies.append(pltpu.make_async_copy(ins[t], dst, local_sems.at[t]))
            copies += [_remote(ins[t], dst, send_sems.at[t, k], recv_sems.at[t, k], to) for k, to in enumerate(peers)]
        return copies

    def forward(ins, outs, send_sems, recv_sems, local_sems, fwd_send_sems, fwd_recv_sems):
        x, y, c = _place()
        copies = []
        for t in range(n):
            for k, chip in enumerate([(1 - x, y), (x, 1 - y), (1 - x, 1 - y)]):
                view = _owner_view(outs[t], names[t], _dev((*chip, c)))
                copies.append(_remote(view, view, fwd_send_sems.at[t, k], fwd_recv_sems.at[t, k], (x, y, 1 - c)))
        return copies

    return Carry(build, shards, [jax.ShapeDtypeStruct(full_shapes[t], shards[t].dtype) for t in range(n)],
                 _dma_sems((n, 4), (n, 4), (n,), (n, 3), (n, 3)), then=forward)


def sibling_exchange(names, grads, part_shapes):
    n = len(names)

    def build(ins, outs, send_sems, recv_sems):
        x, y, c = _place()
        return [_remote(_owner_view(ins[t], names[t], 2 * q + 1 - c), outs[t].at[q], send_sems.at[t, q],
                        recv_sems.at[t, q], (x, y, 1 - c)) for t in range(n) for q in range(4)]

    return Carry(build, grads, [jax.ShapeDtypeStruct((4,) + part_shapes[t], BF16) for t in range(n)],
                 _dma_sems((n, 4), (n, 4)))


def chip_exchange(names, parts, part_shapes):
    n = len(names)

    def build(ins, outs, send_sems, recv_sems):
        x, y, c = _place()
        chips = [(1 - x, y), (x, 1 - y), (1 - x, 1 - y)]
        return [_remote(ins[t].at[2 * chip[0] + chip[1]], outs[t].at[k], send_sems.at[t, k], recv_sems.at[t, k],
                        (*chip, c)) for t in range(n) for k, chip in enumerate(chips)]

    return Carry(build, parts, [jax.ShapeDtypeStruct((3,) + part_shapes[t], BF16) for t in range(n)],
                 _dma_sems((n, 3), (n, 3)))


def all_gather_rows(x):
    r, w = x.shape

    def body(x_ref, out_ref, send_sems, recv_sems, local_sem):
        px, py, pc = _place()
        me = 4 * px + 2 * py + pc
        mine = pltpu.make_async_copy(x_ref, out_ref.at[me], local_sem)
        mine.start()
        copies = []
        for k in range(1, N_DEV):
            peer = (px ^ (k >> 2), py ^ ((k >> 1) & 1), pc ^ (k & 1))
            copies.append(pltpu.make_async_remote_copy(
                src_ref=x_ref, dst_ref=out_ref.at[me], send_sem=send_sems.at[k - 1], recv_sem=recv_sems.at[k - 1],
                device_id=peer, device_id_type=MESH))
        for cp in copies:
            cp.start()
        for k in range(1, N_DEV):
            peer_idx = me ^ k
            pltpu.make_async_remote_copy(
                src_ref=x_ref, dst_ref=out_ref.at[peer_idx], send_sem=send_sems.at[k - 1],
                recv_sem=recv_sems.at[k - 1], device_id=(px, py, pc), device_id_type=MESH).wait_recv()
        for cp in copies:
            cp.wait_send()
        mine.wait()

    vmem = pl.BlockSpec(memory_space=pltpu.VMEM)
    return pl.pallas_call(
        body, name="all_gather_small_grads",
        in_specs=[vmem], out_specs=vmem,
        out_shape=jax.ShapeDtypeStruct((N_DEV, r, w), x.dtype),
        scratch_shapes=[pltpu.SemaphoreType.DMA((N_DEV - 1,)), pltpu.SemaphoreType.DMA((N_DEV - 1,)),
                        pltpu.SemaphoreType.DMA],
    )(x)


def _adamw(w, g, m, v):
    m = ADAM_B1 * m + (1.0 - ADAM_B1) * g
    v = ADAM_B2 * v + (1.0 - ADAM_B2) * (g * g)
    m_hat = m / (1.0 - ADAM_B1 ** ADAM_STEP)
    v_hat = v / (1.0 - ADAM_B2 ** ADAM_STEP)
    return -ADAM_LR * (m_hat / (jnp.sqrt(v_hat) + ADAM_EPS) + ADAM_WD * w), m, v


def sibling_sum(name, col, grads, recv, core):
    _, nl, rows, cols = recv.shape
    tr = _tile(rows, 512)
    rspec = pl.BlockSpec((None, None, tr, cols), lambda q, l, i, c_ref: (q, l, i, 0))
    if col:
        gspec = pl.BlockSpec((None, tr, cols), lambda q, l, i, c_ref: (l, i, 2 * q + c_ref[0]))
    else:
        gspec = pl.BlockSpec((None, None, tr, cols), lambda q, l, i, c_ref: (l, 2 * q + c_ref[0], i, 0))

    def body(c_ref, g_ref, r_ref, o_ref):
        del c_ref
        o_ref[...] = (g_ref[...].astype(F32) + r_ref[...].astype(F32)).astype(BF16)

    return pl.pallas_call(
        body, name=name,
        grid_spec=pltpu.PrefetchScalarGridSpec(num_scalar_prefetch=1, grid=(4, nl, rows // tr),
                                               in_specs=[gspec, rspec], out_specs=rspec),
        out_shape=jax.ShapeDtypeStruct(recv.shape, BF16),
        compiler_params=_params(("parallel", "parallel", "parallel")),
    )(core.reshape(1), grads, recv)


def reduce_adamw(name, parts, recv, chip, w, m, v, l0, prev):
    _, nl, rows, cols = parts.shape
    tr = _tile(rows, 256)

    def body(q_ref, p_ref, r_ref, w_ref, m_ref, v_ref, *rest):
        del q_ref
        g_out, d_out, m_out, v_out = rest[-4:]
        g = ((p_ref[...].astype(F32) + r_ref[0].astype(F32)) + r_ref[1].astype(F32)) + r_ref[2].astype(F32)
        d, mn, vn = _adamw(w_ref[...], g, m_ref[...], v_ref[...])
        g_out[...] = g
        d_out[...] = d
        m_out[...] = mn
        v_out[...] = vn

    blk = pl.BlockSpec((None, tr, cols), lambda l, i, q_ref: (l0 + l, i, 0))
    prev = list(prev) if prev else []
    return pl.pallas_call(
        body, name=name,
        grid_spec=pltpu.PrefetchScalarGridSpec(
            num_scalar_prefetch=1, grid=(nl, rows // tr),
            in_specs=[pl.BlockSpec((None, None, tr, cols), lambda l, i, q_ref: (q_ref[0], l, i, 0)),
                      pl.BlockSpec((3, None, tr, cols), lambda l, i, q_ref: (0, l, i, 0)), blk, blk, blk]
            + [pl.BlockSpec(memory_space=pl.ANY)] * len(prev),
            out_specs=[blk] * 4),
        out_shape=[jax.ShapeDtypeStruct(w.shape, F32)] * 4,
        input_output_aliases={6 + i: i for i in range(len(prev))},
        compiler_params=_params(("parallel", "parallel")),
    )(chip.reshape(1), parts, recv, w, m, v, *prev)


def small_adamw(name, gathered, w, m, v):
    _, r, c = gathered.shape

    def body(ga_ref, w_ref, m_ref, v_ref, g_out, d_out, m_out, v_out):
        g = ga_ref[0]
        for d in range(1, N_DEV):
            g = g + ga_ref[d]
        dl, mn, vn = _adamw(w_ref[...], g, m_ref[...], v_ref[...])
        g_out[...] = g
        d_out[...] = dl
        m_out[...] = mn
        v_out[...] = vn

    return pl.pallas_call(
        body, name=name,
        out_shape=[jax.ShapeDtypeStruct((r, c), F32)] * 4,
        compiler_params=_params(),
    )(gathered, w, m, v)


def _rms(x, g):
    return x * lax.rsqrt(jnp.mean(x * x, axis=-1, keepdims=True) + EPS) * g


def _rms_bwd_epilogue(dn, x, dres, g):
    r = lax.rsqrt(jnp.mean(x * x, axis=-1, keepdims=True) + EPS)
    xh = x * r
    dyg = dn * g
    dx = dres + r * (dyg - xh * jnp.mean(dyg * xh, axis=-1, keepdims=True))
    return dx, dx, jnp.sum(dn * xh, axis=0, keepdims=True), jnp.sum(dx, axis=0, keepdims=True)


def _residual_then_norms(n_terms):
    def epilogue(acc, *ex):
        h = acc
        for t in ex[:n_terms]:
            h = h + t
        return (h,) + tuple(_rms(h, g) for g in ex[n_terms:])
    return epilogue


def local_step(x, target, small, ex):
    s, d = x.shape
    n_a, n_b = small['a_norm'].shape[0], small['b_norm'].shape[0]
    sg = {}
    gb = {}

    def fwd_mm(name, a, wname, layer, epilogue, extras, out_dtypes, **kw):
        return mm_nn(name, a, *ex.weight(wname, layer), epilogue, extras, out_dtypes, **kw)

    def dx_mm(name, dy, wname, layer, epilogue, extras, out_dtypes, **kw):
        return mm_nt(name, dy, *ex.weight(wname, layer), epilogue, extras, out_dtypes, **kw)

    def dw_mm(name, a, dy, wname, layer, **kw):
        key, slab, shape = ex.grad(wname, layer)
        gb[key] = mm_tn(name, a, dy, gb.get(key), shape, slab, **kw)

    plain = lambda acc: (acc,)
    plus_col = lambda acc, b: (acc + b,)

    bias_flat = bias_table(small['rel_bias'])
    bias_t = bias_flat.reshape(2, 8, WINDOW, 2 * WINDOW).transpose(0, 3, 1, 2).reshape(2, 2 * WINDOW, 8 * WINDOW)
    sink_rows = [jnp.repeat(small['b_sinks'][j], WINDOW).reshape(2, 1, 8 * WINDOW) for j in range(n_b)]

    gain = lambda g: g.reshape(1, -1)

    def mlp_fwd(h, n2, layer, next_gains):
        u, a = fwd_mm(f"mlp_up_fwd{layer}", n2, 'mlp_up', layer,
                      lambda acc: (acc, jnp.square(jnp.maximum(acc, 0.0))), (), (BF16, BF16))
        h2, *nexts = fwd_mm(f"mlp_down_fwd{layer}", a, 'mlp_down', layer, _residual_then_norms(1),
                            (h, *[gain(g) for g in next_gains]), (F32,) + (BF16,) * len(next_gains))
        return h2, nexts, (n2, u, a)

    h = x
    saved = []
    n1 = rms_fwd("a_norm_fwd0", h, small['a_norm'][0])
    for l in range(n_a):
        (qkvt,) = fwd_mm(f"a_qkv_fwd{l}", n1, 'a_wqkv', l, plain, (), (BF16,), out_t=True)
        qkvt = qkvt.reshape(3 * d // HEAD_DIM, HEAD_DIM, s)
        o_t, rtab, carried = sb_fwd(f"sb_fwd{l}", qkvt, ex.fwd_carry(l))
        ex.fwd_done(l, carried)
        o_t = o_t.reshape(d, s)
        h_mid, n2 = fwd_mm(f"a_wo_fwd{l}", o_t, 'a_wo', l, _residual_then_norms(1),
                           (h, gain(small['mlp_norm'][l])), (F32, BF16), a_t=True)
        next_gains = [small['a_norm'][l + 1]] if l + 1 < n_a else [small['b_norm'][0], small['kv_norm']]
        h_out, nexts, mlp_saved = mlp_fwd(h_mid, n2, l, next_gains)
        saved.append((h, n1, qkvt, o_t, rtab, h_mid, mlp_saved))
        h, n1 = h_out, nexts[0]
    h_kv, nkv = h, nexts[1]
    (kvt,) = fwd_mm("kv_fwd", nkv, 'w_kv', 0, plus_col, (small['b_kv'].reshape(-1, 1),), (BF16,), out_t=True)
    kvt = kvt.reshape(2, 2, HEAD_DIM, s)
    kpt, vpt = (jnp.pad(t, ((0, 0), (0, 0), (WINDOW, 0))) for t in (kvt[0], kvt[1]))
    for j in range(n_b):
        layer = n_a + j
        (qbt,) = fwd_mm(f"b_q_fwd{j}", n1, 'b_wq', j, plus_col, (small['b_bq'][j].reshape(-1, 1),), (BF16,),
                        out_t=True)
        o_t = swa_fwd(f"swa_fwd{j}", qbt, kpt, vpt, bias_t, sink_rows[j])
        h_mid, n2 = fwd_mm(f"b_wo_fwd{j}", o_t, 'b_wo', j, _residual_then_norms(2),
                           (h, gain(small['b_bo'][j]), gain(small['mlp_norm'][layer])), (F32, BF16), a_t=True)
        h_out, nexts, mlp_saved = mlp_fwd(h_mid, n2, layer, [small['b_norm'][j + 1]] if j + 1 < n_b else [])
        saved.append((h, n1, qbt, o_t, h_mid, mlp_saved))
        h, n1 = h_out, (nexts[0] if nexts else None)

    dh, dhb, dg_final, loss_b = loss_head(h, small['final_norm'], target)
    sg['final_norm'] = dg_final[0]
    sg['mlp_norm'] = [None] * (n_a + n_b)

    def mlp_bwd(dh, dhb, h_mid, mlp_saved, layer):
        n2, u, a = mlp_saved
        du, *carried = dx_mm(f"mlp_down_dx{layer}", dhb, 'mlp_down', layer,
                             lambda acc, uu: (acc * (2.0 * jnp.maximum(uu.astype(F32), 0.0)),), (u,), (BF16,),
                             exchange=ex.mlp_carry(layer, gb))
        ex.mlp_done(layer, carried)
        dw_mm(f"mlp_down_dw{layer}", a, dhb, 'mlp_down', layer)
        dh2, dh2b, dg, cs = dx_mm(f"mlp_up_dx{layer}", du, 'mlp_up', layer, _rms_bwd_epilogue,
                                  (h_mid, dh, gain(small['mlp_norm'][layer])), (F32, BF16), n_sums=2)
        dw_mm(f"mlp_up_dw{layer}", n2, du, 'mlp_up', layer)
        sg['mlp_norm'][layer] = dg[0]
        return dh2, dh2b, cs

    dkp = jnp.zeros(kpt.shape, F32)
    dvp = jnp.zeros(vpt.shape, F32)
    sg['b_norm'], sg['b_bq'], sg['b_bo'], sg['b_sinks'] = [None] * n_b, [None] * n_b, [None] * n_b, [None] * n_b
    dbias = [None] * n_b
    for j in reversed(range(n_b)):
        layer = n_a + j
        h_in, n1, qbt, o_t, h_mid, mlp_saved = saved[layer]
        dh, dhb, cs = mlp_bwd(dh, dhb, h_mid, mlp_saved, layer)
        sg['b_bo'][j] = cs[0]
        (do_t,) = dx_mm(f"b_wo_dx{j}", dhb, 'b_wo', j, plain, (), (BF16,), out_t=True)
        dw_mm(f"b_wo_dw{j}", o_t, dhb, 'b_wo', j, x_t=True)
        dq_t, dkp, dvp, dbias[j], dsink = swa_bwd(f"swa_bwd{j}", qbt, kpt, vpt, bias_t, sink_rows[j], do_t, dkp, dvp)
        sg['b_sinks'][j] = colsum(f"sink_grad{j}", dsink.reshape(16, WINDOW).T)[0]
        sg['b_bq'][j] = rowsum(f"b_bq_grad{j}", dq_t)
        dh, dhb, dg, _ = dx_mm(f"b_q_dx{j}", dq_t, 'b_wq', j, _rms_bwd_epilogue,
                               (h_in, dh, gain(small['b_norm'][j])), (F32, BF16), a_t=True, n_sums=2)
        dw_mm(f"b_q_dw{j}", n1, dq_t, 'b_wq', j, dy_t=True)
        sg['b_norm'][j] = dg[0]
    unt = lambda t: t.reshape(2, 2 * WINDOW, 8, WINDOW).transpose(0, 2, 3, 1).reshape(bias_flat.shape)
    sg['rel_bias'] = bias_table_grad(unt(dbias[0]), unt(dbias[1]))[:, :N_BUCKETS].T

    dkv_t = jnp.concatenate([dkp[:, :, WINDOW:], dvp[:, :, WINDOW:]], axis=0).reshape(-1, s)
    sg['b_kv'] = rowsum("b_kv_grad", dkv_t)
    dkvb = dkv_t.astype(BF16)
    dh, dhb, dg, _ = dx_mm("kv_dx", dkvb, 'w_kv', 0, _rms_bwd_epilogue, (h_kv, dh, gain(small['kv_norm'])),
                           (F32, BF16), a_t=True, n_sums=2)
    dw_mm("kv_dw", nkv, dkvb, 'w_kv', 0, dy_t=True)
    sg['kv_norm'] = dg[0]

    sg['a_norm'] = [None] * n_a
    for l in reversed(range(n_a)):
        h_in, n1, qkvt, o_t, rtab, h_mid, mlp_saved = saved[l]
        dh, dhb, _ = mlp_bwd(dh, dhb, h_mid, mlp_saved, l)
        (do_t,) = dx_mm(f"a_wo_dx{l}", dhb, 'a_wo', l, plain, (), (BF16,), out_t=True)
        dw_mm(f"a_wo_dw{l}", o_t, dhb, 'a_wo', l, x_t=True)
        dqkv_t, carried = sb_bwd(f"sb_bwd{l}", qkvt, do_t.reshape(d // HEAD_DIM, HEAD_DIM, s), rtab,
                                 ex.bwd_carry(l, gb))
        ex.bwd_done(l, carried)
        dqkv_t = dqkv_t.reshape(3 * d, s)
        dw_mm(f"a_qkv_dw{l}", n1, dqkv_t, 'a_wqkv', l, dy_t=True)
        dh, dhb, dg, _, *carried = dx_mm(f"a_qkv_dx{l}", dqkv_t, 'a_wqkv', l, _rms_bwd_epilogue,
                                         (h_in, dh, gain(small['a_norm'][l])), (F32, BF16), a_t=True, n_sums=2,
                                         exchange=ex.last_carry(gb) if l == 0 else None)
        if l == 0:
            ex.last_done(carried)
        sg['a_norm'][l] = dg[0]

    small_grads = {
        'a_norm': jnp.stack(sg['a_norm']), 'kv_norm': sg['kv_norm'], 'b_kv': sg['b_kv'],
        'b_norm': jnp.stack(sg['b_norm']), 'b_bq': jnp.stack(sg['b_bq']), 'b_sinks': jnp.stack(sg['b_sinks']),
        'b_bo': jnp.stack(sg['b_bo']), 'rel_bias': sg['rel_bias'], 'mlp_norm': jnp.stack(sg['mlp_norm']),
        'final_norm': sg['final_norm'],
    }
    return loss_b, dh, gb, small_grads


def _full_shape(name, shard_shape):
    if name in COL_SHARDED:
        return shard_shape[:2] + (N_DEV * shard_shape[2],)
    nl, r, n = shard_shape
    return (nl, N_DEV, r, n)


def _as_w3_shape(name, shard_shape):
    full = _full_shape(name, shard_shape)
    return full if name in COL_SHARDED else (full[0], full[1] * full[2], full[3])


def _as_w3(name, full):
    if name in COL_SHARDED:
        return full
    nl, nd, r, n = full.shape
    return full.reshape(nl, nd * r, n)


AG_GROUPS = {
    0: (('a_wqkv', 0, 1),),
    1: (('a_wo', 0, 2), ('mlp_up', 0, 2), ('mlp_down', 0, 2), ('a_wqkv', 1, 1)),
    2: (('mlp_up', 2, 2), ('mlp_down', 2, 2), ('b_wq', 0, 2), ('b_wo', 0, 2), ('w_kv', 0, 1)),
}
RS_GROUPS = {
    'A': (('mlp_up', 2, 2), ('mlp_down', 2, 2), ('b_wq', 0, 2), ('b_wo', 0, 2), ('w_kv', 0, 1)),
    'B1': (('a_wo', 1, 1), ('mlp_up', 1, 1), ('mlp_down', 1, 1)),
    'B2': (('a_wqkv', 1, 1),),
    'B3': (('a_wo', 0, 1), ('mlp_up', 0, 1), ('mlp_down', 0, 1)),
    'C': (('a_wqkv', 0, 1),),
}
UNDER_MLP = {1: 'A', 0: 'B2'}
SIBLING_UNDER_SB_BWD = {1: 'B1'}
UNDER_SB_BWD = {1: ('A',), 0: ('B1', 'B2', 'B3')}


class _Exchanges:
    def __init__(self, full0, shards, core, chip, w3, m3, v3):
        self.wbuf = {0: {n: _as_w3(n, full0[n]) for n, _, _ in AG_GROUPS[0]}}
        self.shards, self.core, self.chip = shards, core, chip
        self.w3, self.m3, self.v3 = w3, m3, v3
        self.shard_dims = {n: w3[n].shape[1:] for n in BIG}
        self.parts = {}
        self.gfull = {}
        self.out = {}

    def weight(self, name, layer):
        for group, members in AG_GROUPS.items():
            for n, l0, nl in members:
                if n == name and l0 <= layer < l0 + nl:
                    return self.wbuf[group][name], layer - l0
        raise KeyError((name, layer))

    def fwd_carry(self, layer):
        names = [n for n, _, _ in AG_GROUPS[layer + 1]]
        shards = [self.shards[layer + 1][n] for n in names]
        return ag_direct(names, shards, [_full_shape(n, sh.shape) for n, sh in zip(names, shards)])

    def fwd_done(self, layer, carried):
        names = [n for n, _, _ in AG_GROUPS[layer + 1]]
        self.wbuf[layer + 1] = {n: _as_w3(n, f) for n, f in zip(names, carried)}

    def grad(self, name, layer):
        for group, members in RS_GROUPS.items():
            for n, l0, nl in members:
                if n == name and l0 <= layer < l0 + nl:
                    return (group, name), layer - l0, _as_w3_shape(name, (nl,) + self.shard_dims[name])
        raise KeyError((name, layer))

    def _members(self, group):
        names = [n for n, _, _ in RS_GROUPS[group]]
        return names, [(nl,) + self.shard_dims[n] for n, _, nl in RS_GROUPS[group]]

    def _sibling_carry(self, group, gb):
        names, shapes = self._members(group)
        self.gfull[group] = [gb[(group, n)].reshape(_full_shape(n, sh)) for n, sh in zip(names, shapes)]
        return sibling_exchange(names, self.gfull[group], shapes)

    def _sibling_done(self, group, recv):
        names, _ = self._members(group)
        self.parts[group] = [sibling_sum(f"rs_sibling_sum_{group}_{n}", n in COL_SHARDED, g, r, self.core)
                             for n, g, r in zip(names, self.gfull[group], recv)]

    def _sibling_stage(self, group, gb):
        ce = self._sibling_carry(group, gb)
        self._sibling_done(group, comm_call(f"rs_sibling_exchange_{group}", ce.build, ce.ins, ce.out_shapes, ce.sems))

    def mlp_carry(self, layer, gb):
        return self._sibling_carry(UNDER_MLP[layer], gb) if layer in UNDER_MLP else None

    def mlp_done(self, layer, carried):
        if layer in UNDER_MLP:
            self._sibling_done(UNDER_MLP[layer], carried)

    def bwd_carry(self, layer, gb):
        names, parts, shapes = [], [], []
        for group in UNDER_SB_BWD[layer]:
            if group not in self.parts:
                self._sibling_stage(group, gb)
            names += self._members(group)[0]
            shapes += self._members(group)[1]
            parts += self.parts[group]
        exchange = chip_exchange(names, parts, shapes)
        if layer in SIBLING_UNDER_SB_BWD:
            exchange = _together(exchange, self._sibling_carry(SIBLING_UNDER_SB_BWD[layer], gb))
        return exchange

    def bwd_done(self, layer, carried):
        for group in UNDER_SB_BWD[layer]:
            n = len(RS_GROUPS[group])
            self._adamw(group, carried[:n])
            carried = carried[n:]
        if layer in SIBLING_UNDER_SB_BWD:
            self._sibling_done(SIBLING_UNDER_SB_BWD[layer], carried)

    def last_carry(self, gb):
        self._sibling_stage('C', gb)
        names, shapes = self._members('C')
        return chip_exchange(names, self.parts['C'], shapes)

    def last_done(self, carried):
        self._adamw('C', carried)

    def _adamw(self, group, recv2):
        for (n, l0, _), p, r in zip(RS_GROUPS[group], self.parts[group], recv2):
            self.out[n] = reduce_adamw(f"adamw_{group}_{n}", p, r, self.chip, self.w3[n], self.m3[n], self.v3[n],
                                       l0, self.out.get(n))


def _pack_small(vals):
    flat = jnp.concatenate([vals[n].reshape(-1).astype(F32) for n in SMALL] + [vals['loss'].reshape(-1)])
    rows = -(-flat.shape[0] // 1024) * 8
    return jnp.pad(flat, (0, rows * 128 - flat.shape[0])).reshape(rows, 128)


def _unpack_small(packed, shapes):
    flat = packed.reshape(-1)
    out, off = {}, 0
    for n in SMALL + ['loss']:
        size = int(np.prod(shapes[n]))
        out[n] = flat[off:off + size].reshape(shapes[n])
        off += size
    return out


def kernel(x, a_norm, a_wqkv, a_wo, kv_norm, w_kv, b_kv, b_norm, b_wq, b_bq, b_sinks, b_wo, b_bo, rel_bias, mlp_norm, mlp_up, mlp_down, final_norm, loss_target, m_a_norm, m_a_wqkv, m_a_wo, m_kv_norm, m_w_kv, m_b_kv, m_b_norm, m_b_wq, m_b_bq, m_b_sinks, m_b_wo, m_b_bo, m_rel_bias, m_mlp_norm, m_mlp_up, m_mlp_down, m_final_norm, v_a_norm, v_a_wqkv, v_a_wo, v_kv_norm, v_w_kv, v_b_kv, v_b_norm, v_b_wq, v_b_bq, v_b_sinks, v_b_wo, v_b_bo, v_rel_bias, v_mlp_norm, v_mlp_up, v_mlp_down, v_final_norm):
    w = dict(a_norm=a_norm, a_wqkv=a_wqkv, a_wo=a_wo, kv_norm=kv_norm, w_kv=w_kv, b_kv=b_kv, b_norm=b_norm,
             b_wq=b_wq, b_bq=b_bq, b_sinks=b_sinks, b_wo=b_wo, b_bo=b_bo, rel_bias=rel_bias, mlp_norm=mlp_norm,
             mlp_up=mlp_up, mlp_down=mlp_down, final_norm=final_norm)
    m = dict(a_norm=m_a_norm, a_wqkv=m_a_wqkv, a_wo=m_a_wo, kv_norm=m_kv_norm, w_kv=m_w_kv, b_kv=m_b_kv,
             b_norm=m_b_norm, b_wq=m_b_wq, b_bq=m_b_bq, b_sinks=m_b_sinks, b_wo=m_b_wo, b_bo=m_b_bo,
             rel_bias=m_rel_bias, mlp_norm=m_mlp_norm, mlp_up=m_mlp_up, mlp_down=m_mlp_down, final_norm=m_final_norm)
    v = dict(a_norm=v_a_norm, a_wqkv=v_a_wqkv, a_wo=v_a_wo, kv_norm=v_kv_norm, w_kv=v_w_kv, b_kv=v_b_kv,
             b_norm=v_b_norm, b_wq=v_b_wq, b_bq=v_b_bq, b_sinks=v_b_sinks, b_wo=v_b_wo, b_bo=v_b_bo,
             rel_bias=v_rel_bias, mlp_norm=v_mlp_norm, mlp_up=v_mlp_up, mlp_down=v_mlp_down, final_norm=v_final_norm)
    px, py, pc = _place()
    me = 4 * px + 2 * py + pc
    chip = (2 * px + py).astype(jnp.int32)
    core = pc.astype(jnp.int32)

    as3 = lambda t: t[None] if t.ndim == 2 else t
    w3, m3, v3 = ({n: as3(src[n]) for n in BIG} for src in (w, m, v))
    shards = {g: {n: w3[n][l0:l0 + nl].astype(BF16) for n, l0, nl in members} for g, members in AG_GROUPS.items()}
    an_pad = jnp.zeros((8, 128), F32).at[:a_norm.shape[0]].set(a_norm)
    names0 = [n for n, _, _ in AG_GROUPS[0]]
    full0 = all_gather_weights(names0 + ['a_norm'], [shards[0][n] for n in names0] + [an_pad],
                               [_full_shape(n, shards[0][n].shape) for n in names0] + [(N_DEV, 8, 128)])
    full0 = dict(zip(names0 + ['a_norm'], full0))
    n_a = a_norm.shape[0]
    small = {n: w[n] for n in SMALL}
    small['a_norm'] = full0['a_norm'][:, :n_a].transpose(1, 0, 2).reshape(n_a, -1)

    ex = _Exchanges(full0, shards, core, chip, w3, m3, v3)
    loss_b, grad_x, gb, sgrads = local_step(x[0], loss_target[0], small, ex)
    out = {n: [t.reshape(w[n].shape) for t in bufs] for n, bufs in ex.out.items()}

    sgrads['loss'] = loss_b[0, :1]
    gathered = all_gather_rows(_pack_small(sgrads))
    shapes = {n: w[n].shape for n in SMALL}
    shapes['a_norm'] = (n_a, a_norm.shape[1] * N_DEV)
    shapes['loss'] = (1,)
    zeros1 = jnp.zeros((1,), F32)

    def packed(src):
        vals = {n: src[n] for n in SMALL}
        vals['a_norm'] = jnp.zeros(shapes['a_norm'], F32)
        vals['loss'] = zeros1
        return _pack_small(vals)

    sm = small_adamw("adamw_small", gathered, packed(w), packed(m), packed(v))
    sm = [_unpack_small(t, shapes) for t in sm]
    g_an = lax.dynamic_slice_in_dim(sm[0]['a_norm'], me * a_norm.shape[1], a_norm.shape[1], axis=1)
    pad = lambda t: jnp.zeros((8, 128), F32).at[:n_a].set(t)
    gathered_an = jnp.zeros((N_DEV, 8, 128), F32).at[0].set(pad(g_an))
    an = small_adamw("adamw_a_norm", gathered_an, pad(a_norm), pad(m_a_norm), pad(v_a_norm))
    for i in range(4):
        sm[i]['a_norm'] = an[i][:n_a]
    for n in BIG:
        for i in range(4):
            sm[i][n] = out[n][i]
    loss = sm[0]['loss'][0]
    return (loss, grad_x[None], *[sm[0][n] for n in WEIGHTS], *[sm[1][n] for n in WEIGHTS],
            *[sm[2][n] for n in WEIGHTS], *[sm[3][n] for n in WEIGHTS])
```

```python
import math

import numpy as np
import jax
import jax.numpy as jnp
from jax import lax
from jax.experimental import pallas as pl
from jax.experimental.pallas import tpu as pltpu

F32 = jnp.float32
BF16 = jnp.bfloat16
MESH = pl.DeviceIdType.MESH

N_DEV = 8
HEAD_DIM = 64
WINDOW = 128
N_BUCKETS = 32
EPS = 1e-5
NEG_INF = -1e30
Q_SCALE = 1.0 / math.sqrt(HEAD_DIM)
LOG2E = 1.4426950408889634

ADAM_LR, ADAM_B1, ADAM_B2, ADAM_EPS, ADAM_WD, ADAM_STEP = 0.001, 0.9, 0.999, 1e-08, 0.01, 10

SB_BQ = 512
SB_BK = 128
SB_DEAD = 160.0
SB_UNSEEN = 1e30
ROW_TILE = 512
VMEM_LIMIT = 56 * 1024 * 1024

WEIGHTS = ['a_norm', 'a_wqkv', 'a_wo', 'kv_norm', 'w_kv', 'b_kv', 'b_norm', 'b_wq', 'b_bq', 'b_sinks', 'b_wo',
           'b_bo', 'rel_bias', 'mlp_norm', 'mlp_up', 'mlp_down', 'final_norm']
BIG = ['a_wqkv', 'a_wo', 'w_kv', 'b_wq', 'b_wo', 'mlp_up', 'mlp_down']
COL_SHARDED = ('a_wqkv', 'mlp_up')
SMALL = ['a_norm', 'kv_norm', 'b_kv', 'b_norm', 'b_bq', 'b_sinks', 'b_bo', 'rel_bias', 'mlp_norm', 'final_norm']


def _params(sem=None):
    return pltpu.CompilerParams(dimension_semantics=sem, vmem_limit_bytes=VMEM_LIMIT)


def _pick(n, cands):
    for c in cands:
        if n % c == 0:
            return c
    raise ValueError(n)


def _tile(n, want):
    return n if n <= want else _pick(n, (want, want // 2, want // 4))


MM_TILE_BUDGET = 36 * 1024 * 1024


def _row_tile(m, contraction, cols, streams):
    weight = 2 * contraction * cols * 2
    for rows in (2048, 1024, 512):
        if m % rows == 0 and weight + 2 * rows * (2 * contraction + cols * sum(streams)) <= MM_TILE_BUDGET:
            return rows
    return _tile(m, 512)


def mm_nn(name, a, w3, layer, epilogue, extras, out_dtypes, a_t=False, out_t=False):
    k, m = a.shape if a_t else a.shape[::-1]
    _, kw, n = w3.shape
    assert kw == k
    tn = _tile(n, 1024)
    tm = _row_tile(m, k, tn, [jnp.dtype(t).itemsize for t in out_dtypes]
                   + [e.dtype.itemsize for e in extras if e.size == m * n])
    ne, no = len(extras), len(out_dtypes)
    a_dim = 0 if a_t else 1

    def body(a_ref, w_ref, *rest):
        ex, outs = rest[:ne], rest[ne:ne + no]
        if out_t:
            acc = lax.dot_general(w_ref[...], a_ref[...], (((0,), (a_dim,)), ((), ())), preferred_element_type=F32)
        else:
            acc = lax.dot_general(a_ref[...], w_ref[...], (((a_dim,), (0,)), ((), ())), preferred_element_type=F32)
        for o, r in zip(outs, epilogue(acc, *[e[...] for e in ex])):
            o[...] = r.astype(o.dtype)

    if out_t:
        tile = pl.BlockSpec((tn, tm), lambda i, j: (j, i))
        vec = pl.BlockSpec((tn, 1), lambda i, j: (j, 0))
        out_shape = (n, m)
    else:
        tile = pl.BlockSpec((tm, tn), lambda i, j: (i, j))
        vec = pl.BlockSpec((1, tn), lambda i, j: (0, j))
        out_shape = (m, n)
    a_spec = pl.BlockSpec((k, tm), lambda i, j: (0, i)) if a_t else pl.BlockSpec((tm, k), lambda i, j: (i, 0))
    return pl.pallas_call(
        body, name=name, grid=(m // tm, n // tn),
        in_specs=[a_spec, pl.BlockSpec((None, k, tn), lambda i, j: (layer, 0, j))]
        + [tile if e.shape == out_shape else vec for e in extras],
        out_specs=[tile] * no,
        out_shape=[jax.ShapeDtypeStruct(out_shape, d) for d in out_dtypes],
        compiler_params=_params(("parallel", "parallel")),
    )(a, w3, *extras)


def mm_nt(name, dy, w3, layer, epilogue, extras, out_dtypes, a_t=False, out_t=False, n_sums=0, exchange=None):
    n, m = dy.shape if a_t else dy.shape[::-1]
    _, k, nw = w3.shape
    assert nw == n and not (out_t and n_sums)
    tko = _tile(k, 1024)
    tm = _row_tile(m, n, tko, [jnp.dtype(t).itemsize for t in out_dtypes]
                   + [e.dtype.itemsize for e in extras if e.size == m * k])
    ne, no = len(extras), len(out_dtypes)
    a_dim = 0 if a_t else 1

    def body(a_ref, w_ref, *rest):
        ex = rest[:ne]
        at = lambda step: (pl.program_id(0) == step[0]) & (pl.program_id(1) == step[1])
        results, _, start_carried, wait_carried = _carried(exchange, rest[ne:], no + n_sums, 0, at((0, 0)),
                                                           at((m // tm - 1, k // tko - 1)))
        outs, sums = results[:no], results[no:]
        start_carried()
        if out_t:
            acc = lax.dot_general(w_ref[...], a_ref[...], (((1,), (a_dim,)), ((), ())), preferred_element_type=F32)
        else:
            acc = lax.dot_general(a_ref[...], w_ref[...], (((a_dim,), (1,)), ((), ())), preferred_element_type=F32)
        res = epilogue(acc, *[e[...] for e in ex])
        for o, v in zip(outs, res):
            o[...] = v.astype(o.dtype)
        if n_sums:
            @pl.when(pl.program_id(0) == 0)
            def _():
                for o in sums:
                    o[...] = jnp.zeros_like(o)

            for o, v in zip(sums, res[no:]):
                o[...] += v
        wait_carried()

    if out_t:
        tile = pl.BlockSpec((tko, tm), lambda i, ko: (ko, i))
        out_shape = (k, m)
    else:
        tile = pl.BlockSpec((tm, tko), lambda i, ko: (i, ko))
        out_shape = (m, k)
    vec = pl.BlockSpec((1, tko), lambda i, ko: (0, ko))
    a_spec = pl.BlockSpec((n, tm), lambda i, ko: (0, i)) if a_t else pl.BlockSpec((tm, n), lambda i, ko: (i, 0))
    hbm = pl.BlockSpec(memory_space=pl.ANY)
    c_ins, c_outs, c_sems = (exchange.ins, exchange.out_shapes, exchange.sems) if exchange else ([], [], [])
    sequential = n_sums or exchange
    return pl.pallas_call(
        body, name=name, grid=(m // tm, k // tko),
        in_specs=[a_spec, pl.BlockSpec((None, tko, n), lambda i, ko: (layer, ko, 0))]
        + [tile if e.shape == out_shape else vec for e in extras] + [hbm] * len(c_ins),
        out_specs=[tile] * no + [vec] * n_sums + [hbm] * len(c_outs),
        out_shape=[jax.ShapeDtypeStruct(out_shape, d) for d in out_dtypes] + [jax.ShapeDtypeStruct((1, k), F32)] * n_sums
        + c_outs,
        scratch_shapes=c_sems,
        compiler_params=_params(("arbitrary" if sequential else "parallel", "arbitrary" if exchange else "parallel")),
    )(dy, w3, *extras, *c_ins)


def mm_tn(name, x, dy, gbuf, shape, layer, x_t=False, dy_t=False):
    k, s = x.shape if x_t else x.shape[::-1]
    _, kw, n = shape
    assert kw == k and dy.shape == ((n, s) if dy_t else (s, n))
    tkk = _tile(k, 512)
    tn = _tile(n, 1024)

    def body(x_ref, dy_ref, *rest):
        g_out = rest[-1]
        g_out[...] = lax.dot_general(x_ref[...], dy_ref[...], (((1 if x_t else 0,), (1 if dy_t else 0,)), ((), ())),
                                     preferred_element_type=F32).astype(g_out.dtype)

    prev = [] if gbuf is None else [gbuf]
    x_spec = pl.BlockSpec((tkk, s), lambda ki, j: (ki, 0)) if x_t else pl.BlockSpec((s, tkk), lambda ki, j: (0, ki))
    dy_spec = pl.BlockSpec((tn, s), lambda ki, j: (j, 0)) if dy_t else pl.BlockSpec((s, tn), lambda ki, j: (0, j))
    return pl.pallas_call(
        body, name=name, grid=(k // tkk, n // tn),
        in_specs=[x_spec, dy_spec] + [pl.BlockSpec(memory_space=pl.ANY)] * len(prev),
        out_specs=pl.BlockSpec((None, tkk, tn), lambda ki, j: (layer, ki, j)),
        out_shape=jax.ShapeDtypeStruct(shape, BF16),
        input_output_aliases={2: 0} if prev else {},
        compiler_params=_params(("parallel", "parallel")),
    )(x, dy, *prev)


def rms_fwd(name, h, g):
    s, d = h.shape
    tr = _pick(s, (ROW_TILE, 256, 128))

    def body(h_ref, g_ref, o_ref):
        x = h_ref[...]
        r = lax.rsqrt(jnp.mean(x * x, axis=-1, keepdims=True) + EPS)
        o_ref[...] = (x * r * g_ref[...]).astype(o_ref.dtype)

    return pl.pallas_call(
        body, name=name, grid=(s // tr,),
        in_specs=[pl.BlockSpec((tr, d), lambda i: (i, 0)), pl.BlockSpec((1, d), lambda i: (0, 0))],
        out_specs=pl.BlockSpec((tr, d), lambda i: (i, 0)),
        out_shape=jax.ShapeDtypeStruct((s, d), BF16),
        compiler_params=_params(("parallel",)),
    )(h, g.reshape(1, d))


def loss_head(h, g, target):
    s, d = h.shape
    tr = _pick(s, (ROW_TILE, 256, 128))

    def body(h_ref, g_ref, t_ref, dx_ref, dxb_ref, dg_ref, loss_ref):
        i = pl.program_id(0)
        x = h_ref[...]
        r = lax.rsqrt(jnp.mean(x * x, axis=-1, keepdims=True) + EPS)
        xh = x * r
        gw = g_ref[...]
        err = xh * gw - t_ref[...]
        dn_ = err * (1.0 / d)
        dyg = dn_ * gw
        dx = r * (dyg - xh * jnp.mean(dyg * xh, axis=-1, keepdims=True))
        dx_ref[...] = dx
        dxb_ref[...] = dx.astype(BF16)

        @pl.when(i == 0)
        def _():
            dg_ref[...] = jnp.zeros_like(dg_ref)
            loss_ref[...] = jnp.zeros_like(loss_ref)

        dg_ref[...] += jnp.sum(dn_ * xh, axis=0, keepdims=True)
        per_row = jnp.sum(err * err, axis=-1, keepdims=True) * (0.5 / d)
        loss_ref[...] += jnp.broadcast_to(jnp.sum(per_row, axis=0, keepdims=True), loss_ref.shape)

    row = pl.BlockSpec((tr, d), lambda i: (i, 0))
    vec = pl.BlockSpec((1, d), lambda i: (0, 0))
    return pl.pallas_call(
        body, name="loss_head", grid=(s // tr,),
        in_specs=[row, vec, row],
        out_specs=[row, row, vec, pl.BlockSpec((1, 128), lambda i: (0, 0))],
        out_shape=[jax.ShapeDtypeStruct((s, d), F32), jax.ShapeDtypeStruct((s, d), BF16),
                   jax.ShapeDtypeStruct((1, d), F32), jax.ShapeDtypeStruct((1, 128), F32)],
        compiler_params=_params(("arbitrary",)),
    )(h, g.reshape(1, d), target)


def colsum(name, x):
    s, n = x.shape
    tr = _pick(s, (ROW_TILE, 256, 128))

    def body(x_ref, o_ref):
        @pl.when(pl.program_id(0) == 0)
        def _():
            o_ref[...] = jnp.zeros_like(o_ref)

        o_ref[...] += jnp.sum(x_ref[...].astype(F32), axis=0, keepdims=True)

    return pl.pallas_call(
        body, name=name, grid=(s // tr,),
        in_specs=[pl.BlockSpec((tr, n), lambda i: (i, 0))],
        out_specs=pl.BlockSpec((1, n), lambda i: (0, 0)),
        out_shape=jax.ShapeDtypeStruct((1, n), F32),
        compiler_params=_params(("arbitrary",)),
    )(x)


def rowsum(name, x):
    n, s = x.shape
    ts = _pick(s, (1024, 512, 256, 128))

    def body(x_ref, o_ref):
        @pl.when(pl.program_id(0) == 0)
        def _():
            o_ref[...] = jnp.zeros_like(o_ref)

        o_ref[...] += jnp.sum(x_ref[...].astype(F32), axis=1, keepdims=True)

    return pl.pallas_call(
        body, name=name, grid=(s // ts,),
        in_specs=[pl.BlockSpec((n, ts), lambda i: (0, i))],
        out_specs=pl.BlockSpec((n, 1), lambda i: (0, 0)),
        out_shape=jax.ShapeDtypeStruct((n, 1), F32),
        compiler_params=_params(("arbitrary",)),
    )(x)[:, 0]


def _tri_rows(reverse):
    i = np.arange(SB_BK)
    tri = (i[None, :] >= i[:, None]) if reverse else (i[None, :] <= i[:, None])
    tri = np.concatenate([tri, tri], axis=1)
    return jnp.asarray(np.concatenate([tri, np.ones((8, 2 * SB_BK), bool)], axis=0), BF16)


def _hi_lo_rows(x):
    hi = x.astype(BF16)
    lo = (x - hi.astype(F32)).astype(BF16)
    return jnp.concatenate([hi, lo], axis=0)


def _softplus2(zs):
    neg_abs = lax.bitcast_convert_type(lax.bitcast_convert_type(zs, jnp.uint32) | jnp.uint32(0x80000000), F32)
    return jnp.maximum(zs, 0.0) + jnp.log2(1.0 + jnp.exp2(neg_abs))


def _pair_mask(first_rel_block, bq):
    key = lax.broadcasted_iota(jnp.int32, (2 * SB_BK, bq), 0) + first_rel_block * SB_BK
    qry = lax.broadcasted_iota(jnp.int32, (2 * SB_BK, bq), 1)
    return key < qry


def _row_of(table8, sub8, r):
    return jnp.sum(jnp.where(sub8 == r, table8, 0.0), axis=0, keepdims=True)


def _keys(j0):
    return pl.ds(pl.multiple_of(j0 * SB_BK, 2 * SB_BK), 2 * SB_BK)


class Carry:
    def __init__(self, build, ins, out_shapes, sems, then=None):
        self.build, self.ins, self.out_shapes, self.sems = build, list(ins), list(out_shapes), list(sems)
        self.then = then


def _together(a, b):
    assert a.then is None and b.then is None
    ni, no, ns = len(a.ins), len(a.out_shapes), len(a.sems)

    def build(ins, outs, *sems):
        return a.build(ins[:ni], outs[:no], *sems[:ns]) + b.build(ins[ni:], outs[no:], *sems[ns:])

    return Carry(build, a.ins + b.ins, a.out_shapes + b.out_shapes, a.sems + b.sems)


def _carried(carry, rest, n_out, n_scratch, first, last):
    n_ci = len(carry.ins) if carry else 0
    n_co = len(carry.out_shapes) if carry else 0
    cin, outs = rest[:n_ci], rest[n_ci:n_ci + n_out]
    cout = rest[n_ci + n_out:n_ci + n_out + n_co]
    scratch = rest[n_ci + n_out + n_co:n_ci + n_out + n_co + n_scratch]
    csems = rest[n_ci + n_out + n_co + n_scratch:]

    def start():
        if carry:
            @pl.when(first)
            def _():
                for cp in carry.build(cin, cout, *csems):
                    cp.start()

    def wait():
        if carry:
            @pl.when(last)
            def _():
                for cp in carry.build(cin, cout, *csems):
                    cp.wait()
                if carry.then:
                    second = carry.then(cin, cout, *csems)
                    for cp in second:
                        cp.start()
                    for cp in second:
                        cp.wait()

    return outs, scratch, start, wait


def _contract0(a, b):
    return lax.dot_general(a, b, (((0,), (0,)), ((), ())), preferred_element_type=F32)


def _contract1(a, b):
    return lax.dot_general(a, b, (((1,), (1,)), ((), ())), preferred_element_type=F32)


def sb_fwd(name, qkvt, exchange=None):
    nh, dh, s = qkvt.shape[0] // 3, qkvt.shape[1], qkvt.shape[2]
    bq = SB_BQ
    per_q = bq // SB_BK
    nkb = s // SB_BK
    assert s % bq == 0 and per_q == 4 and nkb % 8 == 0

    def body(q_ref, k_ref, v_ref, a_ref, *rest):
        head = pl.program_id(0)
        (o_ref, rtab_ref), (acc, zbuf, wbuf), start_carried, wait_carried = _carried(
            exchange, rest, 2, 3, head == 0, head == nh - 1)
        start_carried()
        tri = a_ref[...]
        sub8 = lax.broadcasted_iota(jnp.int32, (8, bq), 0)
        rtab_ref[...] = jnp.full(rtab_ref.shape, SB_UNSEEN, F32)
        kf = k_ref[...].astype(F32)
        k_max2 = jnp.max(jnp.sum(kf * kf, axis=0, keepdims=True), axis=1, keepdims=True)

        def query_block(i, _):
            lanes = pl.ds(pl.multiple_of(i * bq, bq), bq)
            qb = q_ref[:, lanes] * Q_SCALE
            acc[...] = jnp.zeros_like(acc)
            qf = qb.astype(F32)
            bound = jnp.sqrt(jnp.sum(qf * qf, axis=0, keepdims=True) * k_max2) * (1.001 * LOG2E)

            def scores(j0):
                return _contract0(k_ref[:, _keys(j0)], qb) * LOG2E

            def pair(j0, slot, run, rt8, mask, has_prev):
                zs = zbuf[slot]
                zbuf[1 - slot] = scores(jnp.maximum(j0 - 2, 0))
                if has_prev:
                    acc[...] += jnp.dot(v_ref[:, _keys(j0 + 2)], wbuf[1 - slot], preferred_element_type=F32)
                p = _softplus2(zs)
                if mask is not None:
                    p = jnp.where(mask, p, 0.0)
                cr1 = jnp.dot(tri, _hi_lo_rows(p[SB_BK:]), preferred_element_type=F32)
                cr0 = jnp.dot(tri, _hi_lo_rows(p[:SB_BK]), preferred_element_type=F32)
                run1 = run + cr1[SB_BK:SB_BK + 1]
                w = jnp.exp2(jnp.concatenate([zs[:SB_BK] - cr0[:SB_BK] - run1, zs[SB_BK:] - cr1[:SB_BK] - run],
                                             axis=0))
                if mask is not None:
                    w = jnp.where(mask, w, 0.0)
                wbuf[slot] = w.astype(BF16)
                rt8 = jnp.where(j0 % 8 == 6, SB_UNSEEN, rt8)
                rt8 = jnp.where(sub8 == (j0 + 1) % 8, run, jnp.where(sub8 == j0 % 8, run1, rt8))
                rtab_ref[pl.ds(pl.multiple_of((j0 // 8) * 8, 8), 8), lanes] = rt8
                return run1 + cr0[SB_BK:SB_BK + 1], rt8

            def alive(run):
                return jnp.min(run - bound) < SB_DEAD

            top = i * per_q
            zbuf[0] = scores(top + 2)
            state = (jnp.zeros((1, bq), F32), jnp.full((8, bq), SB_UNSEEN, F32))
            state = pair(top + 2, 0, *state, _pair_mask(2, bq), False)
            state = pair(top, 1, *state, _pair_mask(0, bq), True)

            def step(c):
                it, pairs, _, run, rt8 = c
                j0 = top - 2 - 4 * it
                run, rt8 = pair(j0, 0, run, rt8, None, True)
                go = alive(run)
                run, rt8 = lax.cond(go, lambda r, t: pair(j0 - 2, 1, r, t, None, True), lambda r, t: (r, t), run, rt8)
                return it + 1, pairs + 1 + go.astype(jnp.int32), go & alive(run), run, rt8

            pairs = lax.while_loop(lambda c: (c[0] < i) & c[2], step, (0, 0, alive(state[0]), *state))[1]
            acc[...] += jnp.dot(v_ref[:, _keys(top - 2 * pairs)], wbuf[(pairs + 1) % 2], preferred_element_type=F32)
            o_ref[:, lanes] = acc[...].astype(o_ref.dtype)
            return 0

        lax.fori_loop(0, s // bq, query_block, 0)
        wait_carried()

    def head_spec(offset, rows):
        return pl.BlockSpec((None, rows, s), lambda h: (h + offset, 0, 0))

    hbm = pl.BlockSpec(memory_space=pl.ANY)
    c_ins, c_outs, c_sems = (exchange.ins, exchange.out_shapes, exchange.sems) if exchange else ([], [], [])
    outs = pl.pallas_call(
        body, name=name, grid=(nh,),
        in_specs=[head_spec(0, dh), head_spec(nh, dh), head_spec(2 * nh, dh),
                  pl.BlockSpec((SB_BK + 8, 2 * SB_BK), lambda h: (0, 0))] + [hbm] * len(c_ins),
        out_specs=[head_spec(0, dh), head_spec(0, nkb)] + [hbm] * len(c_outs),
        out_shape=[jax.ShapeDtypeStruct((nh, dh, s), BF16), jax.ShapeDtypeStruct((nh, nkb, s), F32)] + c_outs,
        scratch_shapes=[pltpu.VMEM((dh, bq), F32), pltpu.VMEM((2, 2 * SB_BK, bq), F32),
                        pltpu.VMEM((2, 2 * SB_BK, bq), BF16)] + c_sems,
        compiler_params=_params(("arbitrary",)),
    )(qkvt, qkvt, qkvt, _tri_rows(True), *c_ins)
    return outs[0], outs[1], outs[2:]


def sb_bwd(name, qkvt, dot_, rtab, exchange=None):
    nh, dh, s = qkvt.shape[0] // 3, qkvt.shape[1], qkvt.shape[2]
    bq = SB_BQ
    per_q = bq // SB_BK
    nkb = s // SB_BK

    def body(qt_ref, kt_ref, vt_ref, dot_ref, rtab_ref, ar_ref, af_ref, *rest):
        head = pl.program_id(0)
        (dqkv_ref,), (dq_acc, dk_acc, dv_acc, zbuf, dwbuf, dzbuf, wbuf), start_carried, wait_carried = \
            _carried(exchange, rest, 1, 7, head == 0, head == nh - 1)
        dq_ref, dk_ref, dv_ref = dqkv_ref.at[0], dqkv_ref.at[1], dqkv_ref.at[2]
        start_carried()
        dk_acc[...] = jnp.zeros_like(dk_acc)
        dv_acc[...] = jnp.zeros_like(dv_acc)
        tri_rev = ar_ref[...][:SB_BK]
        tri_fwd = af_ref[...]
        sub8 = lax.broadcasted_iota(jnp.int32, (8, bq), 0)

        def query_block(i, _):
            lanes = pl.ds(pl.multiple_of(i * bq, bq), bq)
            qtb = qt_ref[:, lanes] * Q_SCALE
            dotb = dot_ref[:, lanes]
            dq_acc[...] = jnp.zeros_like(dq_acc)
            last_j = i * per_q + 2
            seen = jnp.max(jnp.where(rtab_ref[:, lanes] < 0.1 * SB_UNSEEN, 1.0, 0.0), axis=1, keepdims=True)
            pairs = jnp.clip((jnp.sum(seen).astype(jnp.int32) - per_q) // 2, 0, 2 * i)
            odd = pairs % 2
            first_j = i * per_q - 2 * pairs

            def issue(j0, slot):
                zbuf[slot] = _contract0(kt_ref[:, _keys(j0)], qtb) * LOG2E
                dwbuf[slot] = _contract0(vt_ref[:, _keys(j0)], dotb)

            def retire(j0, slot):
                keys = _keys(j0)
                dq_acc[...] += jnp.dot(kt_ref[:, keys], dzbuf[slot], preferred_element_type=F32)
                dk_acc[:, keys] += _contract1(qtb, dzbuf[slot])
                dv_acc[:, keys] += _contract1(dotb, wbuf[slot])

            def pair(j0, slot, g_run, mask):
                zs = zbuf[slot]
                dw = dwbuf[slot]
                issue(jnp.minimum(j0 + 2, last_j), 1 - slot)
                retire(jnp.maximum(j0 - 2, first_j), 1 - slot)
                p_raw = _softplus2(zs)
                p = p_raw if mask is None else jnp.where(mask, p_raw, 0.0)
                c0 = jnp.dot(tri_rev, _hi_lo_rows(p[:SB_BK]), preferred_element_type=F32)
                c1 = jnp.dot(tri_rev, _hi_lo_rows(p[SB_BK:]), preferred_element_type=F32)
                rt8 = rtab_ref[pl.ds(pl.multiple_of((j0 // 8) * 8, 8), 8), lanes]
                r0 = _row_of(rt8, sub8, j0 % 8)
                r1 = _row_of(rt8, sub8, (j0 + 1) % 8)
                w = jnp.exp2(jnp.concatenate([zs[:SB_BK] - c0 - r0, zs[SB_BK:] - c1 - r1], axis=0))
                if mask is not None:
                    w = jnp.where(mask, w, 0.0)
                g = w * dw
                gg0 = jnp.dot(tri_fwd, _hi_lo_rows(g[:SB_BK]), preferred_element_type=F32)
                gg1 = jnp.dot(tri_fwd, _hi_lo_rows(g[SB_BK:]), preferred_element_type=F32)
                g_run1 = g_run + gg0[SB_BK:SB_BK + 1]
                g_pre = jnp.concatenate([gg0[:SB_BK] + g_run, gg1[:SB_BK] + g_run1], axis=0)
                dz = g - jnp.exp2(zs - p_raw) * g_pre
                if mask is not None:
                    dz = jnp.where(mask, dz, 0.0)
                dzbuf[slot] = dz.astype(BF16)
                wbuf[slot] = w.astype(BF16)
                return g_run1 + gg1[SB_BK:SB_BK + 1]

            issue(first_j, odd)
            dzbuf[...] = jnp.zeros(dzbuf.shape, BF16)
            wbuf[...] = jnp.zeros(wbuf.shape, BF16)

            def step(it, g_run):
                g_run = pair(4 * it, 0, g_run, None)
                return pair(4 * it + 2, 1, g_run, None)

            g_run = lax.cond(odd == 1, lambda g: pair(first_j, 1, g, None), lambda g: g, jnp.zeros((1, bq), F32))
            g_run = lax.fori_loop(i - pairs // 2, i, step, g_run)
            g_run = pair(last_j - 2, 0, g_run, _pair_mask(0, bq))
            pair(last_j, 1, g_run, _pair_mask(2, bq))
            retire(last_j, 1)
            dq_ref[:, lanes] = (dq_acc[...] * Q_SCALE).astype(dq_ref.dtype)
            return 0

        lax.fori_loop(0, s // bq, query_block, 0)
        dk_ref[...] = dk_acc[...].astype(dk_ref.dtype)
        dv_ref[...] = dv_acc[...].astype(dv_ref.dtype)
        wait_carried()

    def head_spec(offset, rows):
        return pl.BlockSpec((None, rows, s), lambda h: (h + offset, 0, 0))

    aspec = pl.BlockSpec((SB_BK + 8, 2 * SB_BK), lambda h: (0, 0))
    pair_f32 = pltpu.VMEM((2, 2 * SB_BK, bq), F32)
    pair_bf16 = pltpu.VMEM((2, 2 * SB_BK, bq), BF16)
    hbm = pl.BlockSpec(memory_space=pl.ANY)
    c_ins, c_outs, c_sems = (exchange.ins, exchange.out_shapes, exchange.sems) if exchange else ([], [], [])
    outs = pl.pallas_call(
        body, name=name, grid=(nh,),
        in_specs=[head_spec(0, dh), head_spec(nh, dh), head_spec(2 * nh, dh), head_spec(0, dh), head_spec(0, nkb),
                  aspec, aspec] + [hbm] * len(c_ins),
        out_specs=[pl.BlockSpec((3, None, dh, s), lambda h: (0, h, 0, 0))] + [hbm] * len(c_outs),
        out_shape=[jax.ShapeDtypeStruct((3, nh, dh, s), BF16)] + c_outs,
        scratch_shapes=[pltpu.VMEM((dh, bq), F32), pltpu.VMEM((dh, s), F32), pltpu.VMEM((dh, s), F32),
                        pair_f32, pair_f32, pair_bf16, pair_bf16] + c_sems,
        compiler_params=_params(("arbitrary",)),
    )(qkvt, qkvt, qkvt, dot_, rtab, _tri_rows(True), _tri_rows(False), *c_ins)
    return outs[0], outs[1:]


SWA_QB = 2


def _band_valid():
    kj = np.arange(2 * WINDOW)[:, None]
    dist = (np.arange(8 * WINDOW)[None, :] % WINDOW) + WINDOW - kj
    inside = (dist >= 0) & (dist < WINDOW)
    return jnp.asarray(np.stack([inside & (kj >= WINDOW), inside]), F32)


def _swa_probs(qt, kt, bias_t, valid, sink):
    sc = jnp.where(valid > 0.5, _contract0(kt, qt) + bias_t, NEG_INF)
    mx = jnp.maximum(jnp.max(sc, axis=0, keepdims=True), sink)
    p = jnp.exp(sc - mx)
    p_sink = jnp.exp(sink - mx)
    inv = 1.0 / (jnp.sum(p, axis=0, keepdims=True) + p_sink)
    return p, p_sink, inv


def _band(i):
    return pl.ds(pl.multiple_of(i * WINDOW, WINDOW), 2 * WINDOW)


def _heads_to_lanes(blk):
    return jnp.concatenate([blk[r * HEAD_DIM:(r + 1) * HEAD_DIM] for r in range(8)], axis=1)


def _lanes_to_heads(t):
    return jnp.concatenate([t[:, r * WINDOW:(r + 1) * WINDOW] for r in range(8)], axis=0)


def swa_fwd(name, qt, kpt, vpt, bias_t, sink_row):
    d, s = qt.shape
    ng, dh, sp = kpt.shape
    rows, cols = d // ng, SWA_QB * WINDOW
    assert (s // WINDOW) % SWA_QB == 0

    def body(q_ref, k_ref, v_ref, bias_ref, valid_ref, sink_ref, o_ref):
        for u in range(SWA_QB):
            i = pl.program_id(1) * SWA_QB + u
            lanes = slice(u * WINDOW, (u + 1) * WINDOW)
            qb = _heads_to_lanes(q_ref[:, lanes]) * Q_SCALE
            p, _, inv = _swa_probs(qb, k_ref[:, _band(i)], bias_ref[...], valid_ref[jnp.minimum(i, 1)], sink_ref[...])
            o_t = jnp.dot(v_ref[:, _band(i)], p.astype(BF16), preferred_element_type=F32) * inv
            o_ref[:, lanes] = _lanes_to_heads(o_t).astype(o_ref.dtype)

    qspec = pl.BlockSpec((rows, cols), lambda g, i: (g, i))
    kspec = pl.BlockSpec((None, dh, sp), lambda g, i: (g, 0, 0))
    return pl.pallas_call(
        body, name=name, grid=(ng, s // cols),
        in_specs=[qspec, kspec, kspec, pl.BlockSpec((None, 2 * WINDOW, 8 * WINDOW), lambda g, i: (g, 0, 0)),
                  pl.BlockSpec((2, 2 * WINDOW, 8 * WINDOW), lambda g, i: (0, 0, 0)),
                  pl.BlockSpec((None, 1, 8 * WINDOW), lambda g, i: (g, 0, 0))],
        out_specs=qspec,
        out_shape=jax.ShapeDtypeStruct(qt.shape, BF16),
        compiler_params=_params(("parallel", "arbitrary")),
    )(qt, kpt, vpt, bias_t, _band_valid(), sink_row)


def swa_bwd(name, qt, kpt, vpt, bias_t, sink_row, dot_, dk_in, dv_in):
    d, s = qt.shape
    ng, dh, sp = kpt.shape
    rows, cols = d // ng, SWA_QB * WINDOW

    def body(q_ref, k_ref, v_ref, bias_ref, valid_ref, sink_ref, do_ref, dki_ref, dvi_ref,
             dq_ref, dk_ref, dv_ref, db_ref, ds_ref):
        @pl.when(pl.program_id(1) == 0)
        def _():
            dk_ref[...] = dki_ref[...]
            dv_ref[...] = dvi_ref[...]
            db_ref[...] = jnp.zeros_like(db_ref)
            ds_ref[...] = jnp.zeros_like(ds_ref)

        for u in range(SWA_QB):
            i = pl.program_id(1) * SWA_QB + u
            band = _band(i)
            lanes = slice(u * WINDOW, (u + 1) * WINDOW)
            qb = _heads_to_lanes(q_ref[:, lanes]) * Q_SCALE
            dob = _heads_to_lanes(do_ref[:, lanes])
            kt = k_ref[:, band]
            p, p_sink, inv = _swa_probs(qb, kt, bias_ref[...], valid_ref[jnp.minimum(i, 1)], sink_ref[...])
            p = p * inv
            dp = _contract0(v_ref[:, band], dob)
            delta = jnp.sum(p * dp, axis=0, keepdims=True)
            dsc = p * (dp - delta)
            ds_ref[...] -= p_sink * inv * delta
            db_ref[...] += dsc
            dscb = dsc.astype(BF16)
            dq_t = jnp.dot(kt, dscb, preferred_element_type=F32) * Q_SCALE
            dq_ref[:, lanes] = _lanes_to_heads(dq_t).astype(dq_ref.dtype)
            dk_ref[:, band] += _contract1(qb, dscb)
            dv_ref[:, band] += _contract1(dob, p.astype(BF16))

    qspec = pl.BlockSpec((rows, cols), lambda g, i: (g, i))
    kspec = pl.BlockSpec((None, dh, sp), lambda g, i: (g, 0, 0))
    bspec = pl.BlockSpec((None, 2 * WINDOW, 8 * WINDOW), lambda g, i: (g, 0, 0))
    sspec = pl.BlockSpec((None, 1, 8 * WINDOW), lambda g, i: (g, 0, 0))
    return pl.pallas_call(
        body, name=name, grid=(ng, s // cols),
        in_specs=[qspec, kspec, kspec, bspec, pl.BlockSpec((2, 2 * WINDOW, 8 * WINDOW), lambda g, i: (0, 0, 0)), sspec,
                  qspec, kspec, kspec],
        out_specs=[qspec, kspec, kspec, bspec, sspec],
        out_shape=[jax.ShapeDtypeStruct(qt.shape, BF16), jax.ShapeDtypeStruct(kpt.shape, F32),
                   jax.ShapeDtypeStruct(kpt.shape, F32), jax.ShapeDtypeStruct(bias_t.shape, F32),
                   jax.ShapeDtypeStruct(sink_row.shape, F32)],
        compiler_params=_params(("parallel", "arbitrary")),
    )(qt, kpt, vpt, bias_t, _band_valid(), sink_row, dot_, dk_in, dv_in)


def _bucket_onehot():
    qi = np.arange(WINDOW)[:, None]
    kj = np.arange(2 * WINDOW)[None, :]
    n = np.maximum(qi + WINDOW - kj, 0)
    max_exact = N_BUCKETS // 2
    nf = np.maximum(n, 1).astype(np.float64)
    val = np.log(nf / max_exact) / math.log(WINDOW / max_exact) * (N_BUCKETS - max_exact)
    assert np.all(np.abs(val - np.round(val))[(n > max_exact) & (n < WINDOW)] > 1e-3)
    large = np.minimum(max_exact + val.astype(np.int64), N_BUCKETS - 1)
    bucket = np.where(n < max_exact, n, large).reshape(-1)
    onehot = np.zeros((128, bucket.size), np.float32)
    onehot[bucket, np.arange(bucket.size)] = 1.0
    return onehot


def _split3(x):
    a = x.astype(BF16)
    r = x - a.astype(F32)
    b = r.astype(BF16)
    c = (r - b.astype(F32)).astype(BF16)
    return a, b, c


def bias_table(rel_bias):
    nh = rel_bias.shape[1]
    oh = jnp.asarray(_bucket_onehot(), BF16)
    n = oh.shape[1]
    tn = 4096
    rb = jnp.zeros((nh, 128), F32).at[:, :N_BUCKETS].set(rel_bias.T)

    def body(rb_ref, oh_ref, o_ref):
        o_ref[...] = sum(jnp.dot(t, oh_ref[...], preferred_element_type=F32) for t in _split3(rb_ref[...]))

    return pl.pallas_call(
        body, name="bias_table", grid=(n // tn,),
        in_specs=[pl.BlockSpec((nh, 128), lambda i: (0, 0)), pl.BlockSpec((128, tn), lambda i: (0, i))],
        out_specs=pl.BlockSpec((nh, tn), lambda i: (0, i)),
        out_shape=jax.ShapeDtypeStruct((nh, n), F32),
        compiler_params=_params(("parallel",)),
    )(rb, oh)


def bias_table_grad(db0, db1):
    nh, n = db0.shape
    oh = jnp.asarray(_bucket_onehot(), BF16)
    tn = 4096

    def body(a_ref, b_ref, oh_ref, o_ref):
        @pl.when(pl.program_id(0) == 0)
        def _():
            o_ref[...] = jnp.zeros_like(o_ref)

        o_ref[...] += sum(lax.dot_general(t, oh_ref[...], (((1,), (1,)), ((), ())), preferred_element_type=F32)
                          for t in _split3(a_ref[...] + b_ref[...]))

    blk = pl.BlockSpec((nh, tn), lambda i: (0, i))
    return pl.pallas_call(
        body, name="bias_table_grad", grid=(n // tn,),
        in_specs=[blk, blk, pl.BlockSpec((128, tn), lambda i: (0, i))],
        out_specs=pl.BlockSpec((nh, 128), lambda i: (0, 0)),
        out_shape=jax.ShapeDtypeStruct((nh, 128), F32),
        compiler_params=_params(("arbitrary",)),
    )(db0, db1, oh)


def _owner_view(ref, name, d):
    if name == 'a_norm':
        return ref.at[d]
    if name in COL_SHARDED:
        n = ref.shape[2] // N_DEV
        return ref.at[:, :, pl.ds(pl.multiple_of(d * n, 128), n)]
    return ref.at[:, d]


def _place():
    return lax.axis_index("x"), lax.axis_index("y"), lax.axis_index("c")


def _dev(p):
    return 4 * p[0] + 2 * p[1] + p[2]


def _remote(src, dst, send_sem, recv_sem, to):
    return pltpu.make_async_remote_copy(src_ref=src, dst_ref=dst, send_sem=send_sem, recv_sem=recv_sem,
                                        device_id=to, device_id_type=MESH)


def _dma_sems(*shapes):
    return [pltpu.SemaphoreType.DMA(sh) for sh in shapes]


def comm_call(name, build, ins, out_shapes, sems):
    n_in, n_out = len(ins), len(out_shapes)

    def body(*refs):
        copies = build(refs[:n_in], refs[n_in:n_in + n_out], *refs[n_in + n_out:])
        for cp in copies:
            cp.start()
        for cp in copies:
            cp.wait()

    hbm = pl.BlockSpec(memory_space=pl.ANY)
    return pl.pallas_call(
        body, name=name, in_specs=[hbm] * n_in, out_specs=[hbm] * n_out, out_shape=list(out_shapes),
        scratch_shapes=sems,
    )(*ins)


def all_gather_weights(names, shards, full_shapes):
    n = len(names)

    def body(*refs):
        ins, outs = refs[:n], refs[n:2 * n]
        send_sems, recv_sems, local_sems = refs[2 * n:]
        x, y, c = _place()
        me, sibling = (x, y, c), (x, y, 1 - c)
        chips = [(1 - x, y), (x, 1 - y), (1 - x, 1 - y)]

        def copy(t, k, block, to, src=None):
            dst = _owner_view(outs[t], names[t], _dev(block))
            return _remote(dst if src is None else src, dst, send_sems.at[t, k], recv_sems.at[t, k], to)

        mine = [pltpu.make_async_copy(ins[t], _owner_view(outs[t], names[t], _dev(me)), local_sems.at[t])
                for t in range(n)]
        for cp in mine:
            cp.start()
        first = []
        for t in range(n):
            first.append(copy(t, 0, me, sibling, src=ins[t]))
            first += [copy(t, 1 + j, me, (*chip, c), src=ins[t]) for j, chip in enumerate(chips)]
        for cp in first:
            cp.start()
        passed = []
        for j, chip in enumerate(chips):
            for t in range(n):
                copy(t, 1 + j, (*chip, c), me).wait_recv()
                fwd = copy(t, 4 + j, (*chip, c), sibling)
                fwd.start()
                passed.append(fwd)
        for t in range(n):
            copy(t, 0, sibling, me).wait_recv()
            for j, chip in enumerate(chips):
                copy(t, 4 + j, (*chip, 1 - c), me).wait_recv()
        for cp in first + passed:
            cp.wait_send()
        for cp in mine:
            cp.wait()

    hbm = pl.BlockSpec(memory_space=pl.ANY)
    return pl.pallas_call(
        body, name="all_gather_layer0",
        in_specs=[hbm] * n, out_specs=[hbm] * n,
        out_shape=[jax.ShapeDtypeStruct(full_shapes[t], shards[t].dtype) for t in range(n)],
        scratch_shapes=_dma_sems((n, 7), (n, 7), (n,)),
    )(*shards)


def ag_direct(names, shards, full_shapes):
    n = len(names)

    def build(ins, outs, send_sems, recv_sems, local_sems, fwd_send_sems, fwd_recv_sems):
        x, y, c = _place()
        peers = [(x, y, 1 - c), (1 - x, y, c), (x, 1 - y, c), (1 - x, 1 - y, c)]
        copies = []
        for t in range(n):
            dst = _owner_view(outs[t], names[t], _dev((x, y, c)))
            copies.append(pltpu.make_async_copy(ins[t], dst, local_sems.at[t]))
            copies += [_remote(ins[t], dst, send_sems.at[t, k], recv_sems.at[t, k], to) for k, to in enumerate(peers)]
        return copies

    def forward(ins, outs, send_sems, recv_sems, local_sems, fwd_send_sems, fwd_recv_sems):
        x, y, c = _place()
        copies = []
        for t in range(n):
            for k, chip in enumerate([(1 - x, y), (x, 1 - y), (1 - x, 1 - y)]):
                view = _owner_view(outs[t], names[t], _dev((*chip, c)))
                copies.append(_remote(view, view, fwd_send_sems.at[t, k], fwd_recv_sems.at[t, k], (x, y, 1 - c)))
        return copies

    return Carry(build, shards, [jax.ShapeDtypeStruct(full_shapes[t], shards[t].dtype) for t in range(n)],
                 _dma_sems((n, 4), (n, 4), (n,), (n, 3), (n, 3)), then=forward)


def sibling_exchange(names, grads, part_shapes):
    n = len(names)

    def build(ins, outs, send_sems, recv_sems):
        x, y, c = _place()
        return [_remote(_owner_view(ins[t], names[t], 2 * q + 1 - c), outs[t].at[q], send_sems.at[t, q],
                        recv_sems.at[t, q], (x, y, 1 - c)) for t in range(n) for q in range(4)]

    return Carry(build, grads, [jax.ShapeDtypeStruct((4,) + part_shapes[t], BF16) for t in range(n)],
                 _dma_sems((n, 4), (n, 4)))


def chip_exchange(names, parts, part_shapes):
    n = len(names)

    def build(ins, outs, send_sems, recv_sems):
        x, y, c = _place()
        chips = [(1 - x, y), (x, 1 - y), (1 - x, 1 - y)]
        return [_remote(ins[t].at[2 * chip[0] + chip[1]], outs[t].at[k], send_sems.at[t, k], recv_sems.at[t, k],
                        (*chip, c)) for t in range(n) for k, chip in enumerate(chips)]

    return Carry(build, parts, [jax.ShapeDtypeStruct((3,) + part_shapes[t], BF16) for t in range(n)],
                 _dma_sems((n, 3), (n, 3)))


def all_gather_rows(x):
    r, w = x.shape

    def body(x_ref, out_ref, send_sems, recv_sems, local_sem):
        px, py, pc = _place()
        me = 4 * px + 2 * py + pc
        mine = pltpu.make_async_copy(x_ref, out_ref.at[me], local_sem)
        mine.start()
        copies = []
        for k in range(1, N_DEV):
            peer = (px ^ (k >> 2), py ^ ((k >> 1) & 1), pc ^ (k & 1))
            copies.append(pltpu.make_async_remote_copy(
                src_ref=x_ref, dst_ref=out_ref.at[me], send_sem=send_sems.at[k - 1], recv_sem=recv_sems.at[k - 1],
                device_id=peer, device_id_type=MESH))
        for cp in copies:
            cp.start()
        for k in range(1, N_DEV):
            peer_idx = me ^ k
            pltpu.make_async_remote_copy(
                src_ref=x_ref, dst_ref=out_ref.at[peer_idx], send_sem=send_sems.at[k - 1],
                recv_sem=recv_sems.at[k - 1], device_id=(px, py, pc), device_id_type=MESH).wait_recv()
        for cp in copies:
            cp.wait_send()
        mine.wait()

    vmem = pl.BlockSpec(memory_space=pltpu.VMEM)
    return pl.pallas_call(
        body, name="all_gather_small_grads",
        in_specs=[vmem], out_specs=vmem,
        out_shape=jax.ShapeDtypeStruct((N_DEV, r, w), x.dtype),
        scratch_shapes=[pltpu.SemaphoreType.DMA((N_DEV - 1,)), pltpu.SemaphoreType.DMA((N_DEV - 1,)),
                        pltpu.SemaphoreType.DMA],
    )(x)


def _adamw(w, g, m, v):
    m = ADAM_B1 * m + (1.0 - ADAM_B1) * g
    v = ADAM_B2 * v + (1.0 - ADAM_B2) * (g * g)
    m_hat = m / (1.0 - ADAM_B1 ** ADAM_STEP)
    v_hat = v / (1.0 - ADAM_B2 ** ADAM_STEP)
    return -ADAM_LR * (m_hat / (jnp.sqrt(v_hat) + ADAM_EPS) + ADAM_WD * w), m, v


def sibling_sum(name, col, grads, recv, core):
    _, nl, rows, cols = recv.shape
    tr = _tile(rows, 512)
    rspec = pl.BlockSpec((None, None, tr, cols), lambda q, l, i, c_ref: (q, l, i, 0))
    if col:
        gspec = pl.BlockSpec((None, tr, cols), lambda q, l, i, c_ref: (l, i, 2 * q + c_ref[0]))
    else:
        gspec = pl.BlockSpec((None, None, tr, cols), lambda q, l, i, c_ref: (l, 2 * q + c_ref[0], i, 0))

    def body(c_ref, g_ref, r_ref, o_ref):
        del c_ref
        o_ref[...] = (g_ref[...].astype(F32) + r_ref[...].astype(F32)).astype(BF16)

    return pl.pallas_call(
        body, name=name,
        grid_spec=pltpu.PrefetchScalarGridSpec(num_scalar_prefetch=1, grid=(4, nl, rows // tr),
                                               in_specs=[gspec, rspec], out_specs=rspec),
        out_shape=jax.ShapeDtypeStruct(recv.shape, BF16),
        compiler_params=_params(("parallel", "parallel", "parallel")),
    )(core.reshape(1), grads, recv)


def reduce_adamw(name, parts, recv, chip, w, m, v, l0, prev):
    _, nl, rows, cols = parts.shape
    tr = _tile(rows, 256)

    def body(q_ref, p_ref, r_ref, w_ref, m_ref, v_ref, *rest):
        del q_ref
        g_out, d_out, m_out, v_out = rest[-4:]
        g = ((p_ref[...].astype(F32) + r_ref[0].astype(F32)) + r_ref[1].astype(F32)) + r_ref[2].astype(F32)
        d, mn, vn = _adamw(w_ref[...], g, m_ref[...], v_ref[...])
        g_out[...] = g
        d_out[...] = d
        m_out[...] = mn
        v_out[...] = vn

    blk = pl.BlockSpec((None, tr, cols), lambda l, i, q_ref: (l0 + l, i, 0))
    prev = list(prev) if prev else []
    return pl.pallas_call(
        body, name=name,
        grid_spec=pltpu.PrefetchScalarGridSpec(
            num_scalar_prefetch=1, grid=(nl, rows // tr),
            in_specs=[pl.BlockSpec((None, None, tr, cols), lambda l, i, q_ref: (q_ref[0], l, i, 0)),
                      pl.BlockSpec((3, None, tr, cols), lambda l, i, q_ref: (0, l, i, 0)), blk, blk, blk]
            + [pl.BlockSpec(memory_space=pl.ANY)] * len(prev),
            out_specs=[blk] * 4),
        out_shape=[jax.ShapeDtypeStruct(w.shape, F32)] * 4,
        input_output_aliases={6 + i: i for i in range(len(prev))},
        compiler_params=_params(("parallel", "parallel")),
    )(chip.reshape(1), parts, recv, w, m, v, *prev)


def small_adamw(name, gathered, w, m, v):
    _, r, c = gathered.shape

    def body(ga_ref, w_ref, m_ref, v_ref, g_out, d_out, m_out, v_out):
        g = ga_ref[0]
        for d in range(1, N_DEV):
            g = g + ga_ref[d]
        dl, mn, vn = _adamw(w_ref[...], g, m_ref[...], v_ref[...])
        g_out[...] = g
        d_out[...] = dl
        m_out[...] = mn
        v_out[...] = vn

    return pl.pallas_call(
        body, name=name,
        out_shape=[jax.ShapeDtypeStruct((r, c), F32)] * 4,
        compiler_params=_params(),
    )(gathered, w, m, v)


def _rms(x, g):
    return x * lax.rsqrt(jnp.mean(x * x, axis=-1, keepdims=True) + EPS) * g


def _rms_bwd_epilogue(dn, x, dres, g):
    r = lax.rsqrt(jnp.mean(x * x, axis=-1, keepdims=True) + EPS)
    xh = x * r
    dyg = dn * g
    dx = dres + r * (dyg - xh * jnp.mean(dyg * xh, axis=-1, keepdims=True))
    return dx, dx, jnp.sum(dn * xh, axis=0, keepdims=True), jnp.sum(dx, axis=0, keepdims=True)


def _residual_then_norms(n_terms):
    def epilogue(acc, *ex):
        h = acc
        for t in ex[:n_terms]:
            h = h + t
        return (h,) + tuple(_rms(h, g) for g in ex[n_terms:])
    return epilogue


def local_step(x, target, small, ex):
    s, d = x.shape
    n_a, n_b = small['a_norm'].shape[0], small['b_norm'].shape[0]
    sg = {}
    gb = {}

    def fwd_mm(name, a, wname, layer, epilogue, extras, out_dtypes, **kw):
        return mm_nn(name, a, *ex.weight(wname, layer), epilogue, extras, out_dtypes, **kw)

    def dx_mm(name, dy, wname, layer, epilogue, extras, out_dtypes, **kw):
        return mm_nt(name, dy, *ex.weight(wname, layer), epilogue, extras, out_dtypes, **kw)

    def dw_mm(name, a, dy, wname, layer, **kw):
        key, slab, shape = ex.grad(wname, layer)
        gb[key] = mm_tn(name, a, dy, gb.get(key), shape, slab, **kw)

    plain = lambda acc: (acc,)
    plus_col = lambda acc, b: (acc + b,)

    bias_flat = bias_table(small['rel_bias'])
    bias_t = bias_flat.reshape(2, 8, WINDOW, 2 * WINDOW).transpose(0, 3, 1, 2).reshape(2, 2 * WINDOW, 8 * WINDOW)
    sink_rows = [jnp.repeat(small['b_sinks'][j], WINDOW).reshape(2, 1, 8 * WINDOW) for j in range(n_b)]

    gain = lambda g: g.reshape(1, -1)

    def mlp_fwd(h, n2, layer, next_gains):
        u, a = fwd_mm(f"mlp_up_fwd{layer}", n2, 'mlp_up', layer,
                      lambda acc: (acc, jnp.square(jnp.maximum(acc, 0.0))), (), (BF16, BF16))
        h2, *nexts = fwd_mm(f"mlp_down_fwd{layer}", a, 'mlp_down', layer, _residual_then_norms(1),
                            (h, *[gain(g) for g in next_gains]), (F32,) + (BF16,) * len(next_gains))
        return h2, nexts, (n2, u, a)

    h = x
    saved = []
    n1 = rms_fwd("a_norm_fwd0", h, small['a_norm'][0])
    for l in range(n_a):
        (qkvt,) = fwd_mm(f"a_qkv_fwd{l}", n1, 'a_wqkv', l, plain, (), (BF16,), out_t=True)
        qkvt = qkvt.reshape(3 * d // HEAD_DIM, HEAD_DIM, s)
        o_t, rtab, carried = sb_fwd(f"sb_fwd{l}", qkvt, ex.fwd_carry(l))
        ex.fwd_done(l, carried)
        o_t = o_t.reshape(d, s)
        h_mid, n2 = fwd_mm(f"a_wo_fwd{l}", o_t, 'a_wo', l, _residual_then_norms(1),
                           (h, gain(small['mlp_norm'][l])), (F32, BF16), a_t=True)
        next_gains = [small['a_norm'][l + 1]] if l + 1 < n_a else [small['b_norm'][0], small['kv_norm']]
        h_out, nexts, mlp_saved = mlp_fwd(h_mid, n2, l, next_gains)
        saved.append((h, n1, qkvt, o_t, rtab, h_mid, mlp_saved))
        h, n1 = h_out, nexts[0]
    h_kv, nkv = h, nexts[1]
    (kvt,) = fwd_mm("kv_fwd", nkv, 'w_kv', 0, plus_col, (small['b_kv'].reshape(-1, 1),), (BF16,), out_t=True)
    kvt = kvt.reshape(2, 2, HEAD_DIM, s)
    kpt, vpt = (jnp.pad(t, ((0, 0), (0, 0), (WINDOW, 0))) for t in (kvt[0], kvt[1]))
    for j in range(n_b):
        layer = n_a + j
        (qbt,) = fwd_mm(f"b_q_fwd{j}", n1, 'b_wq', j, plus_col, (small['b_bq'][j].reshape(-1, 1),), (BF16,),
                        out_t=True)
        o_t = swa_fwd(f"swa_fwd{j}", qbt, kpt, vpt, bias_t, sink_rows[j])
        h_mid, n2 = fwd_mm(f"b_wo_fwd{j}", o_t, 'b_wo', j, _residual_then_norms(2),
                           (h, gain(small['b_bo'][j]), gain(small['mlp_norm'][layer])), (F32, BF16), a_t=True)
        h_out, nexts, mlp_saved = mlp_fwd(h_mid, n2, layer, [small['b_norm'][j + 1]] if j + 1 < n_b else [])
        saved.append((h, n1, qbt, o_t, h_mid, mlp_saved))
        h, n1 = h_out, (nexts[0] if nexts else None)

    dh, dhb, dg_final, loss_b = loss_head(h, small['final_norm'], target)
    sg['final_norm'] = dg_final[0]
    sg['mlp_norm'] = [None] * (n_a + n_b)

    def mlp_bwd(dh, dhb, h_mid, mlp_saved, layer):
        n2, u, a = mlp_saved
        du, *carried = dx_mm(f"mlp_down_dx{layer}", dhb, 'mlp_down', layer,
                             lambda acc, uu: (acc * (2.0 * jnp.maximum(uu.astype(F32), 0.0)),), (u,), (BF16,),
                             exchange=ex.mlp_carry(layer, gb))
        ex.mlp_done(layer, carried)
        dw_mm(f"mlp_down_dw{layer}", a, dhb, 'mlp_down', layer)
        dh2, dh2b, dg, cs = dx_mm(f"mlp_up_dx{layer}", du, 'mlp_up', layer, _rms_bwd_epilogue,
                                  (h_mid, dh, gain(small['mlp_norm'][layer])), (F32, BF16), n_sums=2)
        dw_mm(f"mlp_up_dw{layer}", n2, du, 'mlp_up', layer)
        sg['mlp_norm'][layer] = dg[0]
        return dh2, dh2b, cs

    dkp = jnp.zeros(kpt.shape, F32)
    dvp = jnp.zeros(vpt.shape, F32)
    sg['b_norm'], sg['b_bq'], sg['b_bo'], sg['b_sinks'] = [None] * n_b, [None] * n_b, [None] * n_b, [None] * n_b
    dbias = [None] * n_b
    for j in reversed(range(n_b)):
        layer = n_a + j
        h_in, n1, qbt, o_t, h_mid, mlp_saved = saved[layer]
        dh, dhb, cs = mlp_bwd(dh, dhb, h_mid, mlp_saved, layer)
        sg['b_bo'][j] = cs[0]
        (do_t,) = dx_mm(f"b_wo_dx{j}", dhb, 'b_wo', j, plain, (), (BF16,), out_t=True)
        dw_mm(f"b_wo_dw{j}", o_t, dhb, 'b_wo', j, x_t=True)
        dq_t, dkp, dvp, dbias[j], dsink = swa_bwd(f"swa_bwd{j}", qbt, kpt, vpt, bias_t, sink_rows[j], do_t, dkp, dvp)
        sg['b_sinks'][j] = colsum(f"sink_grad{j}", dsink.reshape(16, WINDOW).T)[0]
        sg['b_bq'][j] = rowsum(f"b_bq_grad{j}", dq_t)
        dh, dhb, dg, _ = dx_mm(f"b_q_dx{j}", dq_t, 'b_wq', j, _rms_bwd_epilogue,
                               (h_in, dh, gain(small['b_norm'][j])), (F32, BF16), a_t=True, n_sums=2)
        dw_mm(f"b_q_dw{j}", n1, dq_t, 'b_wq', j, dy_t=True)
        sg['b_norm'][j] = dg[0]
    unt = lambda t: t.reshape(2, 2 * WINDOW, 8, WINDOW).transpose(0, 2, 3, 1).reshape(bias_flat.shape)
    sg['rel_bias'] = bias_table_grad(unt(dbias[0]), unt(dbias[1]))[:, :N_BUCKETS].T

    dkv_t = jnp.concatenate([dkp[:, :, WINDOW:], dvp[:, :, WINDOW:]], axis=0).reshape(-1, s)
    sg['b_kv'] = rowsum("b_kv_grad", dkv_t)
    dkvb = dkv_t.astype(BF16)
    dh, dhb, dg, _ = dx_mm("kv_dx", dkvb, 'w_kv', 0, _rms_bwd_epilogue, (h_kv, dh, gain(small['kv_norm'])),
                           (F32, BF16), a_t=True, n_sums=2)
    dw_mm("kv_dw", nkv, dkvb, 'w_kv', 0, dy_t=True)
    sg['kv_norm'] = dg[0]

    sg['a_norm'] = [None] * n_a
    for l in reversed(range(n_a)):
        h_in, n1, qkvt, o_t, rtab, h_mid, mlp_saved = saved[l]
        dh, dhb, _ = mlp_bwd(dh, dhb, h_mid, mlp_saved, l)
        (do_t,) = dx_mm(f"a_wo_dx{l}", dhb, 'a_wo', l, plain, (), (BF16,), out_t=True)
        dw_mm(f"a_wo_dw{l}", o_t, dhb, 'a_wo', l, x_t=True)
        dqkv_t, carried = sb_bwd(f"sb_bwd{l}", qkvt, do_t.reshape(d // HEAD_DIM, HEAD_DIM, s), rtab,
                                 ex.bwd_carry(l, gb))
        ex.bwd_done(l, carried)
        dqkv_t = dqkv_t.reshape(3 * d, s)
        dw_mm(f"a_qkv_dw{l}", n1, dqkv_t, 'a_wqkv', l, dy_t=True)
        dh, dhb, dg, _, *carried = dx_mm(f"a_qkv_dx{l}", dqkv_t, 'a_wqkv', l, _rms_bwd_epilogue,
                                         (h_in, dh, gain(small['a_norm'][l])), (F32, BF16), a_t=True, n_sums=2,
                                         exchange=ex.last_carry(gb) if l == 0 else None)
        if l == 0:
            ex.last_done(carried)
        sg['a_norm'][l] = dg[0]

    small_grads = {
        'a_norm': jnp.stack(sg['a_norm']), 'kv_norm': sg['kv_norm'], 'b_kv': sg['b_kv'],
        'b_norm': jnp.stack(sg['b_norm']), 'b_bq': jnp.stack(sg['b_bq']), 'b_sinks': jnp.stack(sg['b_sinks']),
        'b_bo': jnp.stack(sg['b_bo']), 'rel_bias': sg['rel_bias'], 'mlp_norm': jnp.stack(sg['mlp_norm']),
        'final_norm': sg['final_norm'],
    }
    return loss_b, dh, gb, small_grads


def _full_shape(name, shard_shape):
    if name in COL_SHARDED:
        return shard_shape[:2] + (N_DEV * shard_shape[2],)
    nl, r, n = shard_shape
    return (nl, N_DEV, r, n)


def _as_w3_shape(name, shard_shape):
    full = _full_shape(name, shard_shape)
    return full if name in COL_SHARDED else (full[0], full[1] * full[2], full[3])


def _as_w3(name, full):
    if name in COL_SHARDED:
        return full
    nl, nd, r, n = full.shape
    return full.reshape(nl, nd * r, n)


AG_GROUPS = {
    0: (('a_wqkv', 0, 1),),
    1: (('a_wo', 0, 2), ('mlp_up', 0, 2), ('mlp_down', 0, 2), ('a_wqkv', 1, 1)),
    2: (('mlp_up', 2, 2), ('mlp_down', 2, 2), ('b_wq', 0, 2), ('b_wo', 0, 2), ('w_kv', 0, 1)),
}
RS_GROUPS = {
    'A': (('mlp_up', 2, 2), ('mlp_down', 2, 2), ('b_wq', 0, 2), ('b_wo', 0, 2), ('w_kv', 0, 1)),
    'B1': (('a_wo', 1, 1), ('mlp_up', 1, 1), ('mlp_down', 1, 1)),
    'B2': (('a_wqkv', 1, 1), ('a_wo', 0, 1), ('mlp_up', 0, 1), ('mlp_down', 0, 1)),
    'C': (('a_wqkv', 0, 1),),
}
UNDER_MLP = {1: 'A'}
SIBLING_UNDER_SB_BWD = {1: 'B1'}
UNDER_SB_BWD = {1: ('A',), 0: ('B1', 'B2')}


class _Exchanges:
    def __init__(self, full0, shards, core, chip, w3, m3, v3):
        self.wbuf = {0: {n: _as_w3(n, full0[n]) for n, _, _ in AG_GROUPS[0]}}
        self.shards, self.core, self.chip = shards, core, chip
        self.w3, self.m3, self.v3 = w3, m3, v3
        self.shard_dims = {n: w3[n].shape[1:] for n in BIG}
        self.parts = {}
        self.gfull = {}
        self.out = {}

    def weight(self, name, layer):
        for group, members in AG_GROUPS.items():
            for n, l0, nl in members:
                if n == name and l0 <= layer < l0 + nl:
                    return self.wbuf[group][name], layer - l0
        raise KeyError((name, layer))

    def fwd_carry(self, layer):
        names = [n for n, _, _ in AG_GROUPS[layer + 1]]
        shards = [self.shards[layer + 1][n] for n in names]
        return ag_direct(names, shards, [_full_shape(n, sh.shape) for n, sh in zip(names, shards)])

    def fwd_done(self, layer, carried):
        names = [n for n, _, _ in AG_GROUPS[layer + 1]]
        self.wbuf[layer + 1] = {n: _as_w3(n, f) for n, f in zip(names, carried)}

    def grad(self, name, layer):
        for group, members in RS_GROUPS.items():
            for n, l0, nl in members:
                if n == name and l0 <= layer < l0 + nl:
                    return (group, name), layer - l0, _as_w3_shape(name, (nl,) + self.shard_dims[name])
        raise KeyError((name, layer))

    def _members(self, group):
        names = [n for n, _, _ in RS_GROUPS[group]]
        return names, [(nl,) + self.shard_dims[n] for n, _, nl in RS_GROUPS[group]]

    def _sibling_carry(self, group, gb):
        names, shapes = self._members(group)
        self.gfull[group] = [gb[(group, n)].reshape(_full_shape(n, sh)) for n, sh in zip(names, shapes)]
        return sibling_exchange(names, self.gfull[group], shapes)

    def _sibling_done(self, group, recv):
        names, _ = self._members(group)
        self.parts[group] = [sibling_sum(f"rs_sibling_sum_{group}_{n}", n in COL_SHARDED, g, r, self.core)
                             for n, g, r in zip(names, self.gfull[group], recv)]

    def _sibling_stage(self, group, gb):
        ce = self._sibling_carry(group, gb)
        self._sibling_done(group, comm_call(f"rs_sibling_exchange_{group}", ce.build, ce.ins, ce.out_shapes, ce.sems))

    def mlp_carry(self, layer, gb):
        return self._sibling_carry(UNDER_MLP[layer], gb) if layer in UNDER_MLP else None

    def mlp_done(self, layer, carried):
        if layer in UNDER_MLP:
            self._sibling_done(UNDER_MLP[layer], carried)

    def bwd_carry(self, layer, gb):
        names, parts, shapes = [], [], []
        for group in UNDER_SB_BWD[layer]:
            if group not in self.parts:
                self._sibling_stage(group, gb)
            names += self._members(group)[0]
            shapes += self._members(group)[1]
            parts += self.parts[group]
        exchange = chip_exchange(names, parts, shapes)
        if layer in SIBLING_UNDER_SB_BWD:
            exchange = _together(exchange, self._sibling_carry(SIBLING_UNDER_SB_BWD[layer], gb))
        return exchange

    def bwd_done(self, layer, carried):
        for group in UNDER_SB_BWD[layer]:
            n = len(RS_GROUPS[group])
            self._adamw(group, carried[:n])
            carried = carried[n:]
        if layer in SIBLING_UNDER_SB_BWD:
            self._sibling_done(SIBLING_UNDER_SB_BWD[layer], carried)

    def last_carry(self, gb):
        self._sibling_stage('C', gb)
        names, shapes = self._members('C')
        return chip_exchange(names, self.parts['C'], shapes)

    def last_done(self, carried):
        self._adamw('C', carried)

    def _adamw(self, group, recv2):
        for (n, l0, _), p, r in zip(RS_GROUPS[group], self.parts[group], recv2):
            self.out[n] = reduce_adamw(f"adamw_{group}_{n}", p, r, self.chip, self.w3[n], self.m3[n], self.v3[n],
                                       l0, self.out.get(n))


def _pack_small(vals):
    flat = jnp.concatenate([vals[n].reshape(-1).astype(F32) for n in SMALL] + [vals['loss'].reshape(-1)])
    rows = -(-flat.shape[0] // 1024) * 8
    return jnp.pad(flat, (0, rows * 128 - flat.shape[0])).reshape(rows, 128)


def _unpack_small(packed, shapes):
    flat = packed.reshape(-1)
    out, off = {}, 0
    for n in SMALL + ['loss']:
        size = int(np.prod(shapes[n]))
        out[n] = flat[off:off + size].reshape(shapes[n])
        off += size
    return out


def kernel(x, a_norm, a_wqkv, a_wo, kv_norm, w_kv, b_kv, b_norm, b_wq, b_bq, b_sinks, b_wo, b_bo, rel_bias, mlp_norm, mlp_up, mlp_down, final_norm, loss_target, m_a_norm, m_a_wqkv, m_a_wo, m_kv_norm, m_w_kv, m_b_kv, m_b_norm, m_b_wq, m_b_bq, m_b_sinks, m_b_wo, m_b_bo, m_rel_bias, m_mlp_norm, m_mlp_up, m_mlp_down, m_final_norm, v_a_norm, v_a_wqkv, v_a_wo, v_kv_norm, v_w_kv, v_b_kv, v_b_norm, v_b_wq, v_b_bq, v_b_sinks, v_b_wo, v_b_bo, v_rel_bias, v_mlp_norm, v_mlp_up, v_mlp_down, v_final_norm):
    w = dict(a_norm=a_norm, a_wqkv=a_wqkv, a_wo=a_wo, kv_norm=kv_norm, w_kv=w_kv, b_kv=b_kv, b_norm=b_norm,
             b_wq=b_wq, b_bq=b_bq, b_sinks=b_sinks, b_wo=b_wo, b_bo=b_bo, rel_bias=rel_bias, mlp_norm=mlp_norm,
             mlp_up=mlp_up, mlp_down=mlp_down, final_norm=final_norm)
    m = dict(a_norm=m_a_norm, a_wqkv=m_a_wqkv, a_wo=m_a_wo, kv_norm=m_kv_norm, w_kv=m_w_kv, b_kv=m_b_kv,
             b_norm=m_b_norm, b_wq=m_b_wq, b_bq=m_b_bq, b_sinks=m_b_sinks, b_wo=m_b_wo, b_bo=m_b_bo,
             rel_bias=m_rel_bias, mlp_norm=m_mlp_norm, mlp_up=m_mlp_up, mlp_down=m_mlp_down, final_norm=m_final_norm)
    v = dict(a_norm=v_a_norm, a_wqkv=v_a_wqkv, a_wo=v_a_wo, kv_norm=v_kv_norm, w_kv=v_w_kv, b_kv=v_b_kv,
             b_norm=v_b_norm, b_wq=v_b_wq, b_bq=v_b_bq, b_sinks=v_b_sinks, b_wo=v_b_wo, b_bo=v_b_bo,
             rel_bias=v_rel_bias, mlp_norm=v_mlp_norm, mlp_up=v_mlp_up, mlp_down=v_mlp_down, final_norm=v_final_norm)
    px, py, pc = _place()
    me = 4 * px + 2 * py + pc
    chip = (2 * px + py).astype(jnp.int32)
    core = pc.astype(jnp.int32)

    as3 = lambda t: t[None] if t.ndim == 2 else t
    w3, m3, v3 = ({n: as3(src[n]) for n in BIG} for src in (w, m, v))
    shards = {g: {n: w3[n][l0:l0 + nl].astype(BF16) for n, l0, nl in members} for g, members in AG_GROUPS.items()}
    an_pad = jnp.zeros((8, 128), F32).at[:a_norm.shape[0]].set(a_norm)
    names0 = [n for n, _, _ in AG_GROUPS[0]]
    full0 = all_gather_weights(names0 + ['a_norm'], [shards[0][n] for n in names0] + [an_pad],
                               [_full_shape(n, shards[0][n].shape) for n in names0] + [(N_DEV, 8, 128)])
    full0 = dict(zip(names0 + ['a_norm'], full0))
    n_a = a_norm.shape[0]
    small = {n: w[n] for n in SMALL}
    small['a_norm'] = full0['a_norm'][:, :n_a].transpose(1, 0, 2).reshape(n_a, -1)

    ex = _Exchanges(full0, shards, core, chip, w3, m3, v3)
    loss_b, grad_x, gb, sgrads = local_step(x[0], loss_target[0], small, ex)
    out = {n: [t.reshape(w[n].shape) for t in bufs] for n, bufs in ex.out.items()}

    sgrads['loss'] = loss_b[0, :1]
    gathered = all_gather_rows(_pack_small(sgrads))
    shapes = {n: w[n].shape for n in SMALL}
    shapes['a_norm'] = (n_a, a_norm.shape[1] * N_DEV)
    shapes['loss'] = (1,)
    zeros1 = jnp.zeros((1,), F32)

    def packed(src):
        vals = {n: src[n] for n in SMALL}
        vals['a_norm'] = jnp.zeros(shapes['a_norm'], F32)
        vals['loss'] = zeros1
        return _pack_small(vals)

    sm = small_adamw("adamw_small", gathered, packed(w), packed(m), packed(v))
    sm = [_unpack_small(t, shapes) for t in sm]
    g_an = lax.dynamic_slice_in_dim(sm[0]['a_norm'], me * a_norm.shape[1], a_norm.shape[1], axis=1)
    pad = lambda t: jnp.zeros((8, 128), F32).at[:n_a].set(t)
    gathered_an = jnp.zeros((N_DEV, 8, 128), F32).at[0].set(pad(g_an))
    an = small_adamw("adamw_a_norm", gathered_an, pad(a_norm), pad(m_a_norm), pad(v_a_norm))
    for i in range(4):
        sm[i]['a_norm'] = an[i][:n_a]
    for n in BIG:
        for i in range(4):
            sm[i][n] = out[n][i]
    loss = sm[0]['loss'][0]
    return (loss, grad_x[None], *[sm[0][n] for n in WEIGHTS], *[sm[1][n] for n in WEIGHTS],
            *[sm[2][n] for n in WEIGHTS], *[sm[3][n] for n in WEIGHTS])
```

```python
import math

import numpy as np
import jax
import jax.numpy as jnp
from jax import lax
from jax.experimental import pallas as pl
from jax.experimental.pallas import tpu as pltpu

F32 = jnp.float32
BF16 = jnp.bfloat16
MESH = pl.DeviceIdType.MESH

N_DEV = 8
HEAD_DIM = 64
WINDOW = 128
N_BUCKETS = 32
EPS = 1e-5
NEG_INF = -1e30
Q_SCALE = 1.0 / math.sqrt(HEAD_DIM)
LOG2E = 1.4426950408889634

ADAM_LR, ADAM_B1, ADAM_B2, ADAM_EPS, ADAM_WD, ADAM_STEP = 0.001, 0.9, 0.999, 1e-08, 0.01, 10

SB_BQ = 512
SB_BK = 128
SB_DEAD = 160.0
SB_UNSEEN = 1e30
ROW_TILE = 512
VMEM_LIMIT = 56 * 1024 * 1024

WEIGHTS = ['a_norm', 'a_wqkv', 'a_wo', 'kv_norm', 'w_kv', 'b_kv', 'b_norm', 'b_wq', 'b_bq', 'b_sinks', 'b_wo',
           'b_bo', 'rel_bias', 'mlp_norm', 'mlp_up', 'mlp_down', 'final_norm']
BIG = ['a_wqkv', 'a_wo', 'w_kv', 'b_wq', 'b_wo', 'mlp_up', 'mlp_down']
COL_SHARDED = ('a_wqkv', 'mlp_up')
SMALL = ['a_norm', 'kv_norm', 'b_kv', 'b_norm', 'b_bq', 'b_sinks', 'b_bo', 'rel_bias', 'mlp_norm', 'final_norm']


def _params(sem=None):
    return pltpu.CompilerParams(dimension_semantics=sem, vmem_limit_bytes=VMEM_LIMIT)


def _pick(n, cands):
    for c in cands:
        if n % c == 0:
            return c
    raise ValueError(n)


def _tile(n, want):
    return n if n <= want else _pick(n, (want, want // 2, want // 4))


MM_TILE_BUDGET = 36 * 1024 * 1024


def _row_tile(m, contraction, cols, streams):
    weight = 2 * contraction * cols * 2
    for rows in (2048, 1024, 512):
        if m % rows == 0 and weight + 2 * rows * (2 * contraction + cols * sum(streams)) <= MM_TILE_BUDGET:
            return rows
    return _tile(m, 512)


def mm_nn(name, a, w3, layer, epilogue, extras, out_dtypes, a_t=False, out_t=False):
    k, m = a.shape if a_t else a.shape[::-1]
    _, kw, n = w3.shape
    assert kw == k
    tn = _tile(n, 1024)
    tm = _row_tile(m, k, tn, [jnp.dtype(t).itemsize for t in out_dtypes]
                   + [e.dtype.itemsize for e in extras if e.size == m * n])
    ne, no = len(extras), len(out_dtypes)
    a_dim = 0 if a_t else 1

    def body(a_ref, w_ref, *rest):
        ex, outs = rest[:ne], rest[ne:ne + no]
        if out_t:
            acc = lax.dot_general(w_ref[...], a_ref[...], (((0,), (a_dim,)), ((), ())), preferred_element_type=F32)
        else:
            acc = lax.dot_general(a_ref[...], w_ref[...], (((a_dim,), (0,)), ((), ())), preferred_element_type=F32)
        for o, r in zip(outs, epilogue(acc, *[e[...] for e in ex])):
            o[...] = r.astype(o.dtype)

    if out_t:
        tile = pl.BlockSpec((tn, tm), lambda i, j: (j, i))
        vec = pl.BlockSpec((tn, 1), lambda i, j: (j, 0))
        out_shape = (n, m)
    else:
        tile = pl.BlockSpec((tm, tn), lambda i, j: (i, j))
        vec = pl.BlockSpec((1, tn), lambda i, j: (0, j))
        out_shape = (m, n)
    a_spec = pl.BlockSpec((k, tm), lambda i, j: (0, i)) if a_t else pl.BlockSpec((tm, k), lambda i, j: (i, 0))
    return pl.pallas_call(
        body, name=name, grid=(m // tm, n // tn),
        in_specs=[a_spec, pl.BlockSpec((None, k, tn), lambda i, j: (layer, 0, j))]
        + [tile if e.shape == out_shape else vec for e in extras],
        out_specs=[tile] * no,
        out_shape=[jax.ShapeDtypeStruct(out_shape, d) for d in out_dtypes],
        compiler_params=_params(("parallel", "parallel")),
    )(a, w3, *extras)


def mm_nt(name, dy, w3, layer, epilogue, extras, out_dtypes, a_t=False, out_t=False, n_sums=0, exchange=None):
    n, m = dy.shape if a_t else dy.shape[::-1]
    _, k, nw = w3.shape
    assert nw == n and not (out_t and n_sums)
    tko = _tile(k, 1024)
    tm = _row_tile(m, n, tko, [jnp.dtype(t).itemsize for t in out_dtypes]
                   + [e.dtype.itemsize for e in extras if e.size == m * k])
    ne, no = len(extras), len(out_dtypes)
    a_dim = 0 if a_t else 1

    def body(a_ref, w_ref, *rest):
        ex = rest[:ne]
        at = lambda step: (pl.program_id(0) == step[0]) & (pl.program_id(1) == step[1])
        results, _, start_carried, wait_carried = _carried(exchange, rest[ne:], no + n_sums, 0, at((0, 0)),
                                                           at((m // tm - 1, k // tko - 1)))
        outs, sums = results[:no], results[no:]
        start_carried()
        if out_t:
            acc = lax.dot_general(w_ref[...], a_ref[...], (((1,), (a_dim,)), ((), ())), preferred_element_type=F32)
        else:
            acc = lax.dot_general(a_ref[...], w_ref[...], (((a_dim,), (1,)), ((), ())), preferred_element_type=F32)
        res = epilogue(acc, *[e[...] for e in ex])
        for o, v in zip(outs, res):
            o[...] = v.astype(o.dtype)
        if n_sums:
            @pl.when(pl.program_id(0) == 0)
            def _():
                for o in sums:
                    o[...] = jnp.zeros_like(o)

            for o, v in zip(sums, res[no:]):
                o[...] += v
        wait_carried()

    if out_t:
        tile = pl.BlockSpec((tko, tm), lambda i, ko: (ko, i))
        out_shape = (k, m)
    else:
        tile = pl.BlockSpec((tm, tko), lambda i, ko: (i, ko))
        out_shape = (m, k)
    vec = pl.BlockSpec((1, tko), lambda i, ko: (0, ko))
    a_spec = pl.BlockSpec((n, tm), lambda i, ko: (0, i)) if a_t else pl.BlockSpec((tm, n), lambda i, ko: (i, 0))
    hbm = pl.BlockSpec(memory_space=pl.ANY)
    c_ins, c_outs, c_sems = (exchange.ins, exchange.out_shapes, exchange.sems) if exchange else ([], [], [])
    sequential = n_sums or exchange
    return pl.pallas_call(
        body, name=name, grid=(m // tm, k // tko),
        in_specs=[a_spec, pl.BlockSpec((None, tko, n), lambda i, ko: (layer, ko, 0))]
        + [tile if e.shape == out_shape else vec for e in extras] + [hbm] * len(c_ins),
        out_specs=[tile] * no + [vec] * n_sums + [hbm] * len(c_outs),
        out_shape=[jax.ShapeDtypeStruct(out_shape, d) for d in out_dtypes] + [jax.ShapeDtypeStruct((1, k), F32)] * n_sums
        + c_outs,
        scratch_shapes=c_sems,
        compiler_params=_params(("arbitrary" if sequential else "parallel", "arbitrary" if exchange else "parallel")),
    )(dy, w3, *extras, *c_ins)


def mm_tn(name, x, dy, gbuf, shape, layer, x_t=False, dy_t=False):
    k, s = x.shape if x_t else x.shape[::-1]
    _, kw, n = shape
    assert kw == k and dy.shape == ((n, s) if dy_t else (s, n))
    tkk = _tile(k, 512)
    tn = _tile(n, 1024)

    def body(x_ref, dy_ref, *rest):
        g_out = rest[-1]
        g_out[...] = lax.dot_general(x_ref[...], dy_ref[...], (((1 if x_t else 0,), (1 if dy_t else 0,)), ((), ())),
                                     preferred_element_type=F32).astype(g_out.dtype)

    prev = [] if gbuf is None else [gbuf]
    x_spec = pl.BlockSpec((tkk, s), lambda ki, j: (ki, 0)) if x_t else pl.BlockSpec((s, tkk), lambda ki, j: (0, ki))
    dy_spec = pl.BlockSpec((tn, s), lambda ki, j: (j, 0)) if dy_t else pl.BlockSpec((s, tn), lambda ki, j: (0, j))
    return pl.pallas_call(
        body, name=name, grid=(k // tkk, n // tn),
        in_specs=[x_spec, dy_spec] + [pl.BlockSpec(memory_space=pl.ANY)] * len(prev),
        out_specs=pl.BlockSpec((None, tkk, tn), lambda ki, j: (layer, ki, j)),
        out_shape=jax.ShapeDtypeStruct(shape, BF16),
        input_output_aliases={2: 0} if prev else {},
        compiler_params=_params(("parallel", "parallel")),
    )(x, dy, *prev)


def rms_fwd(name, h, g):
    s, d = h.shape
    tr = _pick(s, (ROW_TILE, 256, 128))

    def body(h_ref, g_ref, o_ref):
        x = h_ref[...]
        r = lax.rsqrt(jnp.mean(x * x, axis=-1, keepdims=True) + EPS)
        o_ref[...] = (x * r * g_ref[...]).astype(o_ref.dtype)

    return pl.pallas_call(
        body, name=name, grid=(s // tr,),
        in_specs=[pl.BlockSpec((tr, d), lambda i: (i, 0)), pl.BlockSpec((1, d), lambda i: (0, 0))],
        out_specs=pl.BlockSpec((tr, d), lambda i: (i, 0)),
        out_shape=jax.ShapeDtypeStruct((s, d), BF16),
        compiler_params=_params(("parallel",)),
    )(h, g.reshape(1, d))


def loss_head(h, g, target):
    s, d = h.shape
    tr = _pick(s, (ROW_TILE, 256, 128))

    def body(h_ref, g_ref, t_ref, dx_ref, dxb_ref, dg_ref, loss_ref):
        i = pl.program_id(0)
        x = h_ref[...]
        r = lax.rsqrt(jnp.mean(x * x, axis=-1, keepdims=True) + EPS)
        xh = x * r
        gw = g_ref[...]
        err = xh * gw - t_ref[...]
        dn_ = err * (1.0 / d)
        dyg = dn_ * gw
        dx = r * (dyg - xh * jnp.mean(dyg * xh, axis=-1, keepdims=True))
        dx_ref[...] = dx
        dxb_ref[...] = dx.astype(BF16)

        @pl.when(i == 0)
        def _():
            dg_ref[...] = jnp.zeros_like(dg_ref)
            loss_ref[...] = jnp.zeros_like(loss_ref)

        dg_ref[...] += jnp.sum(dn_ * xh, axis=0, keepdims=True)
        per_row = jnp.sum(err * err, axis=-1, keepdims=True) * (0.5 / d)
        loss_ref[...] += jnp.broadcast_to(jnp.sum(per_row, axis=0, keepdims=True), loss_ref.shape)

    row = pl.BlockSpec((tr, d), lambda i: (i, 0))
    vec = pl.BlockSpec((1, d), lambda i: (0, 0))
    return pl.pallas_call(
        body, name="loss_head", grid=(s // tr,),
        in_specs=[row, vec, row],
        out_specs=[row, row, vec, pl.BlockSpec((1, 128), lambda i: (0, 0))],
        out_shape=[jax.ShapeDtypeStruct((s, d), F32), jax.ShapeDtypeStruct((s, d), BF16),
                   jax.ShapeDtypeStruct((1, d), F32), jax.ShapeDtypeStruct((1, 128), F32)],
        compiler_params=_params(("arbitrary",)),
    )(h, g.reshape(1, d), target)


def colsum(name, x):
    s, n = x.shape
    tr = _pick(s, (ROW_TILE, 256, 128))

    def body(x_ref, o_ref):
        @pl.when(pl.program_id(0) == 0)
        def _():
            o_ref[...] = jnp.zeros_like(o_ref)

        o_ref[...] += jnp.sum(x_ref[...].astype(F32), axis=0, keepdims=True)

    return pl.pallas_call(
        body, name=name, grid=(s // tr,),
        in_specs=[pl.BlockSpec((tr, n), lambda i: (i, 0))],
        out_specs=pl.BlockSpec((1, n), lambda i: (0, 0)),
        out_shape=jax.ShapeDtypeStruct((1, n), F32),
        compiler_params=_params(("arbitrary",)),
    )(x)


def rowsum(name, x):
    n, s = x.shape
    ts = _pick(s, (1024, 512, 256, 128))

    def body(x_ref, o_ref):
        @pl.when(pl.program_id(0) == 0)
        def _():
            o_ref[...] = jnp.zeros_like(o_ref)

        o_ref[...] += jnp.sum(x_ref[...].astype(F32), axis=1, keepdims=True)

    return pl.pallas_call(
        body, name=name, grid=(s // ts,),
        in_specs=[pl.BlockSpec((n, ts), lambda i: (0, i))],
        out_specs=pl.BlockSpec((n, 1), lambda i: (0, 0)),
        out_shape=jax.ShapeDtypeStruct((n, 1), F32),
        compiler_params=_params(("arbitrary",)),
    )(x)[:, 0]


def _tri_rows(reverse):
    i = np.arange(SB_BK)
    tri = (i[None, :] >= i[:, None]) if reverse else (i[None, :] <= i[:, None])
    tri = np.concatenate([tri, tri], axis=1)
    return jnp.asarray(np.concatenate([tri, np.ones((8, 2 * SB_BK), bool)], axis=0), BF16)


def _hi_lo_rows(x):
    hi = x.astype(BF16)
    lo = (x - hi.astype(F32)).astype(BF16)
    return jnp.concatenate([hi, lo], axis=0)


def _softplus2(zs):
    neg_abs = lax.bitcast_convert_type(lax.bitcast_convert_type(zs, jnp.uint32) | jnp.uint32(0x80000000), F32)
    return jnp.maximum(zs, 0.0) + jnp.log2(1.0 + jnp.exp2(neg_abs))


def _pair_mask(first_rel_block, bq):
    key = lax.broadcasted_iota(jnp.int32, (2 * SB_BK, bq), 0) + first_rel_block * SB_BK
    qry = lax.broadcasted_iota(jnp.int32, (2 * SB_BK, bq), 1)
    return key < qry


def _row_of(table8, sub8, r):
    return jnp.sum(jnp.where(sub8 == r, table8, 0.0), axis=0, keepdims=True)


def _keys(j0):
    return pl.ds(pl.multiple_of(j0 * SB_BK, 2 * SB_BK), 2 * SB_BK)


class Carry:
    def __init__(self, build, ins, out_shapes, sems, then=None):
        self.build, self.ins, self.out_shapes, self.sems = build, list(ins), list(out_shapes), list(sems)
        self.then = then


def _together(a, b):
    assert a.then is None and b.then is None
    ni, no, ns = len(a.ins), len(a.out_shapes), len(a.sems)

    def build(ins, outs, *sems):
        return a.build(ins[:ni], outs[:no], *sems[:ns]) + b.build(ins[ni:], outs[no:], *sems[ns:])

    return Carry(build, a.ins + b.ins, a.out_shapes + b.out_shapes, a.sems + b.sems)


def _carried(carry, rest, n_out, n_scratch, first, last):
    n_ci = len(carry.ins) if carry else 0
    n_co = len(carry.out_shapes) if carry else 0
    cin, outs = rest[:n_ci], rest[n_ci:n_ci + n_out]
    cout = rest[n_ci + n_out:n_ci + n_out + n_co]
    scratch = rest[n_ci + n_out + n_co:n_ci + n_out + n_co + n_scratch]
    csems = rest[n_ci + n_out + n_co + n_scratch:]

    def start():
        if carry:
            @pl.when(first)
            def _():
                for cp in carry.build(cin, cout, *csems):
                    cp.start()

    def wait():
        if carry:
            @pl.when(last)
            def _():
                for cp in carry.build(cin, cout, *csems):
                    cp.wait()
                if carry.then:
                    second = carry.then(cin, cout, *csems)
                    for cp in second:
                        cp.start()
                    for cp in second:
                        cp.wait()

    return outs, scratch, start, wait


def _contract0(a, b):
    return lax.dot_general(a, b, (((0,), (0,)), ((), ())), preferred_element_type=F32)


def _contract1(a, b):
    return lax.dot_general(a, b, (((1,), (1,)), ((), ())), preferred_element_type=F32)


def sb_fwd(name, qkvt, exchange=None):
    nh, dh, s = qkvt.shape[0] // 3, qkvt.shape[1], qkvt.shape[2]
    bq = SB_BQ
    per_q = bq // SB_BK
    nkb = s // SB_BK
    assert s % bq == 0 and per_q == 4 and nkb % 8 == 0

    def body(q_ref, k_ref, v_ref, a_ref, *rest):
        head = pl.program_id(0)
        (o_ref, rtab_ref), (acc, zbuf, wbuf), start_carried, wait_carried = _carried(
            exchange, rest, 2, 3, head == 0, head == nh - 1)
        start_carried()
        tri = a_ref[...]
        sub8 = lax.broadcasted_iota(jnp.int32, (8, bq), 0)
        rtab_ref[...] = jnp.full(rtab_ref.shape, SB_UNSEEN, F32)
        kf = k_ref[...].astype(F32)
        k_max2 = jnp.max(jnp.sum(kf * kf, axis=0, keepdims=True), axis=1, keepdims=True)

        def query_block(i, _):
            lanes = pl.ds(pl.multiple_of(i * bq, bq), bq)
            qb = q_ref[:, lanes] * Q_SCALE
            acc[...] = jnp.zeros_like(acc)
            qf = qb.astype(F32)
            bound = jnp.sqrt(jnp.sum(qf * qf, axis=0, keepdims=True) * k_max2) * (1.001 * LOG2E)

            def scores(j0):
                return _contract0(k_ref[:, _keys(j0)], qb) * LOG2E

            def pair(j0, slot, run, rt8, mask, has_prev):
                zs = zbuf[slot]
                zbuf[1 - slot] = scores(jnp.maximum(j0 - 2, 0))
                if has_prev:
                    acc[...] += jnp.dot(v_ref[:, _keys(j0 + 2)], wbuf[1 - slot], preferred_element_type=F32)
                p = _softplus2(zs)
                if mask is not None:
                    p = jnp.where(mask, p, 0.0)
                cr1 = jnp.dot(tri, _hi_lo_rows(p[SB_BK:]), preferred_element_type=F32)
                cr0 = jnp.dot(tri, _hi_lo_rows(p[:SB_BK]), preferred_element_type=F32)
                run1 = run + cr1[SB_BK:SB_BK + 1]
                w = jnp.exp2(jnp.concatenate([zs[:SB_BK] - cr0[:SB_BK] - run1, zs[SB_BK:] - cr1[:SB_BK] - run],
                                             axis=0))
                if mask is not None:
                    w = jnp.where(mask, w, 0.0)
                wbuf[slot] = w.astype(BF16)
                rt8 = jnp.where(j0 % 8 == 6, SB_UNSEEN, rt8)
                rt8 = jnp.where(sub8 == (j0 + 1) % 8, run, jnp.where(sub8 == j0 % 8, run1, rt8))
                rtab_ref[pl.ds(pl.multiple_of((j0 // 8) * 8, 8), 8), lanes] = rt8
                return run1 + cr0[SB_BK:SB_BK + 1], rt8

            def alive(run):
                return jnp.min(run - bound) < SB_DEAD

            top = i * per_q
            zbuf[0] = scores(top + 2)
            state = (jnp.zeros((1, bq), F32), jnp.full((8, bq), SB_UNSEEN, F32))
            state = pair(top + 2, 0, *state, _pair_mask(2, bq), False)
            state = pair(top, 1, *state, _pair_mask(0, bq), True)

            def step(c):
                it, pairs, _, run, rt8 = c
                j0 = top - 2 - 4 * it
                run, rt8 = pair(j0, 0, run, rt8, None, True)
                go = alive(run)
                run, rt8 = lax.cond(go, lambda r, t: pair(j0 - 2, 1, r, t, None, True), lambda r, t: (r, t), run, rt8)
                return it + 1, pairs + 1 + go.astype(jnp.int32), go & alive(run), run, rt8

            pairs = lax.while_loop(lambda c: (c[0] < i) & c[2], step, (0, 0, alive(state[0]), *state))[1]
            acc[...] += jnp.dot(v_ref[:, _keys(top - 2 * pairs)], wbuf[(pairs + 1) % 2], preferred_element_type=F32)
            o_ref[:, lanes] = acc[...].astype(o_ref.dtype)
            return 0

        lax.fori_loop(0, s // bq, query_block, 0)
        wait_carried()

    def head_spec(offset, rows):
        return pl.BlockSpec((None, rows, s), lambda h: (h + offset, 0, 0))

    hbm = pl.BlockSpec(memory_space=pl.ANY)
    c_ins, c_outs, c_sems = (exchange.ins, exchange.out_shapes, exchange.sems) if exchange else ([], [], [])
    outs = pl.pallas_call(
        body, name=name, grid=(nh,),
        in_specs=[head_spec(0, dh), head_spec(nh, dh), head_spec(2 * nh, dh),
                  pl.BlockSpec((SB_BK + 8, 2 * SB_BK), lambda h: (0, 0))] + [hbm] * len(c_ins),
        out_specs=[head_spec(0, dh), head_spec(0, nkb)] + [hbm] * len(c_outs),
        out_shape=[jax.ShapeDtypeStruct((nh, dh, s), BF16), jax.ShapeDtypeStruct((nh, nkb, s), F32)] + c_outs,
        scratch_shapes=[pltpu.VMEM((dh, bq), F32), pltpu.VMEM((2, 2 * SB_BK, bq), F32),
                        pltpu.VMEM((2, 2 * SB_BK, bq), BF16)] + c_sems,
        compiler_params=_params(("arbitrary",)),
    )(qkvt, qkvt, qkvt, _tri_rows(True), *c_ins)
    return outs[0], outs[1], outs[2:]


def sb_bwd(name, qkvt, dot_, rtab, exchange=None):
    nh, dh, s = qkvt.shape[0] // 3, qkvt.shape[1], qkvt.shape[2]
    bq = SB_BQ
    per_q = bq // SB_BK
    nkb = s // SB_BK

    def body(qt_ref, kt_ref, vt_ref, dot_ref, rtab_ref, ar_ref, af_ref, *rest):
        head = pl.program_id(0)
        (dqkv_ref,), (dq_acc, dk_acc, dv_acc, zbuf, dwbuf, dzbuf, wbuf), start_carried, wait_carried = \
            _carried(exchange, rest, 1, 7, head == 0, head == nh - 1)
        dq_ref, dk_ref, dv_ref = dqkv_ref.at[0], dqkv_ref.at[1], dqkv_ref.at[2]
        start_carried()
        dk_acc[...] = jnp.zeros_like(dk_acc)
        dv_acc[...] = jnp.zeros_like(dv_acc)
        tri_rev = ar_ref[...][:SB_BK]
        tri_fwd = af_ref[...]
        sub8 = lax.broadcasted_iota(jnp.int32, (8, bq), 0)

        def query_block(i, _):
            lanes = pl.ds(pl.multiple_of(i * bq, bq), bq)
            qtb = qt_ref[:, lanes] * Q_SCALE
            dotb = dot_ref[:, lanes]
            dq_acc[...] = jnp.zeros_like(dq_acc)
            last_j = i * per_q + 2
            seen = jnp.max(jnp.where(rtab_ref[:, lanes] < 0.1 * SB_UNSEEN, 1.0, 0.0), axis=1, keepdims=True)
            pairs = jnp.clip((jnp.sum(seen).astype(jnp.int32) - per_q) // 2, 0, 2 * i)
            odd = pairs % 2
            first_j = i * per_q - 2 * pairs

            def issue(j0, slot):
                zbuf[slot] = _contract0(kt_ref[:, _keys(j0)], qtb) * LOG2E
                dwbuf[slot] = _contract0(vt_ref[:, _keys(j0)], dotb)

            def retire(j0, slot):
                keys = _keys(j0)
                dq_acc[...] += jnp.dot(kt_ref[:, keys], dzbuf[slot], preferred_element_type=F32)
                dk_acc[:, keys] += _contract1(qtb, dzbuf[slot])
                dv_acc[:, keys] += _contract1(dotb, wbuf[slot])

            def pair(j0, slot, g_run, mask):
                zs = zbuf[slot]
                dw = dwbuf[slot]
                issue(jnp.minimum(j0 + 2, last_j), 1 - slot)
                retire(jnp.maximum(j0 - 2, first_j), 1 - slot)
                p_raw = _softplus2(zs)
                p = p_raw if mask is None else jnp.where(mask, p_raw, 0.0)
                c0 = jnp.dot(tri_rev, _hi_lo_rows(p[:SB_BK]), preferred_element_type=F32)
                c1 = jnp.dot(tri_rev, _hi_lo_rows(p[SB_BK:]), preferred_element_type=F32)
                rt8 = rtab_ref[pl.ds(pl.multiple_of((j0 // 8) * 8, 8), 8), lanes]
                r0 = _row_of(rt8, sub8, j0 % 8)
                r1 = _row_of(rt8, sub8, (j0 + 1) % 8)
                w = jnp.exp2(jnp.concatenate([zs[:SB_BK] - c0 - r0, zs[SB_BK:] - c1 - r1], axis=0))
                if mask is not None:
                    w = jnp.where(mask, w, 0.0)
                g = w * dw
                gg0 = jnp.dot(tri_fwd, _hi_lo_rows(g[:SB_BK]), preferred_element_type=F32)
                gg1 = jnp.dot(tri_fwd, _hi_lo_rows(g[SB_BK:]), preferred_element_type=F32)
                g_run1 = g_run + gg0[SB_BK:SB_BK + 1]
                g_pre = jnp.concatenate([gg0[:SB_BK] + g_run, gg1[:SB_BK] + g_run1], axis=0)
                dz = g - jnp.exp2(zs - p_raw) * g_pre
                if mask is not None:
                    dz = jnp.where(mask, dz, 0.0)
                dzbuf[slot] = dz.astype(BF16)
                wbuf[slot] = w.astype(BF16)
                return g_run1 + gg1[SB_BK:SB_BK + 1]

            issue(first_j, odd)
            dzbuf[...] = jnp.zeros(dzbuf.shape, BF16)
            wbuf[...] = jnp.zeros(wbuf.shape, BF16)

            def step(it, g_run):
                g_run = pair(4 * it, 0, g_run, None)
                return pair(4 * it + 2, 1, g_run, None)

            g_run = lax.cond(odd == 1, lambda g: pair(first_j, 1, g, None), lambda g: g, jnp.zeros((1, bq), F32))
            g_run = lax.fori_loop(i - pairs // 2, i, step, g_run)
            g_run = pair(last_j - 2, 0, g_run, _pair_mask(0, bq))
            pair(last_j, 1, g_run, _pair_mask(2, bq))
            retire(last_j, 1)
            dq_ref[:, lanes] = (dq_acc[...] * Q_SCALE).astype(dq_ref.dtype)
            return 0

        lax.fori_loop(0, s // bq, query_block, 0)
        dk_ref[...] = dk_acc[...].astype(dk_ref.dtype)
        dv_ref[...] = dv_acc[...].astype(dv_ref.dtype)
        wait_carried()

    def head_spec(offset, rows):
        return pl.BlockSpec((None, rows, s), lambda h: (h + offset, 0, 0))

    aspec = pl.BlockSpec((SB_BK + 8, 2 * SB_BK), lambda h: (0, 0))
    pair_f32 = pltpu.VMEM((2, 2 * SB_BK, bq), F32)
    pair_bf16 = pltpu.VMEM((2, 2 * SB_BK, bq), BF16)
    hbm = pl.BlockSpec(memory_space=pl.ANY)
    c_ins, c_outs, c_sems = (exchange.ins, exchange.out_shapes, exchange.sems) if exchange else ([], [], [])
    outs = pl.pallas_call(
        body, name=name, grid=(nh,),
        in_specs=[head_spec(0, dh), head_spec(nh, dh), head_spec(2 * nh, dh), head_spec(0, dh), head_spec(0, nkb),
                  aspec, aspec] + [hbm] * len(c_ins),
        out_specs=[pl.BlockSpec((3, None, dh, s), lambda h: (0, h, 0, 0))] + [hbm] * len(c_outs),
        out_shape=[jax.ShapeDtypeStruct((3, nh, dh, s), BF16)] + c_outs,
        scratch_shapes=[pltpu.VMEM((dh, bq), F32), pltpu.VMEM((dh, s), F32), pltpu.VMEM((dh, s), F32),
                        pair_f32, pair_f32, pair_bf16, pair_bf16] + c_sems,
        compiler_params=_params(("arbitrary",)),
    )(qkvt, qkvt, qkvt, dot_, rtab, _tri_rows(True), _tri_rows(False), *c_ins)
    return outs[0], outs[1:]


SWA_QB = 4


def _band_valid():
    kj = np.arange(2 * WINDOW)[:, None]
    dist = (np.arange(8 * WINDOW)[None, :] % WINDOW) + WINDOW - kj
    inside = (dist >= 0) & (dist < WINDOW)
    return jnp.asarray(np.stack([inside & (kj >= WINDOW), inside]), F32)


def _swa_probs(qt, kt, bias_t, valid, sink):
    sc = jnp.where(valid > 0.5, _contract0(kt, qt) + bias_t, NEG_INF)
    mx = jnp.maximum(jnp.max(sc, axis=0, keepdims=True), sink)
    p = jnp.exp(sc - mx)
    p_sink = jnp.exp(sink - mx)
    inv = 1.0 / (jnp.sum(p, axis=0, keepdims=True) + p_sink)
    return p, p_sink, inv


def _band(i):
    return pl.ds(pl.multiple_of(i * WINDOW, WINDOW), 2 * WINDOW)


def _heads_to_lanes(blk):
    return jnp.concatenate([blk[r * HEAD_DIM:(r + 1) * HEAD_DIM] for r in range(8)], axis=1)


def _lanes_to_heads(t):
    return jnp.concatenate([t[:, r * WINDOW:(r + 1) * WINDOW] for r in range(8)], axis=0)


def swa_fwd(name, qt, kpt, vpt, bias_t, sink_row):
    d, s = qt.shape
    ng, dh, sp = kpt.shape
    rows, cols = d // ng, SWA_QB * WINDOW
    assert (s // WINDOW) % SWA_QB == 0

    def body(q_ref, k_ref, v_ref, bias_ref, valid_ref, sink_ref, o_ref):
        for u in range(SWA_QB):
            i = pl.program_id(1) * SWA_QB + u
            lanes = slice(u * WINDOW, (u + 1) * WINDOW)
            qb = _heads_to_lanes(q_ref[:, lanes]) * Q_SCALE
            p, _, inv = _swa_probs(qb, k_ref[:, _band(i)], bias_ref[...], valid_ref[jnp.minimum(i, 1)], sink_ref[...])
            o_t = jnp.dot(v_ref[:, _band(i)], p.astype(BF16), preferred_element_type=F32) * inv
            o_ref[:, lanes] = _lanes_to_heads(o_t).astype(o_ref.dtype)

    qspec = pl.BlockSpec((rows, cols), lambda g, i: (g, i))
    kspec = pl.BlockSpec((None, dh, sp), lambda g, i: (g, 0, 0))
    return pl.pallas_call(
        body, name=name, grid=(ng, s // cols),
        in_specs=[qspec, kspec, kspec, pl.BlockSpec((None, 2 * WINDOW, 8 * WINDOW), lambda g, i: (g, 0, 0)),
                  pl.BlockSpec((2, 2 * WINDOW, 8 * WINDOW), lambda g, i: (0, 0, 0)),
                  pl.BlockSpec((None, 1, 8 * WINDOW), lambda g, i: (g, 0, 0))],
        out_specs=qspec,
        out_shape=jax.ShapeDtypeStruct(qt.shape, BF16),
        compiler_params=_params(("parallel", "arbitrary")),
    )(qt, kpt, vpt, bias_t, _band_valid(), sink_row)


def swa_bwd(name, qt, kpt, vpt, bias_t, sink_row, dot_, dk_in, dv_in):
    d, s = qt.shape
    ng, dh, sp = kpt.shape
    rows, cols = d // ng, SWA_QB * WINDOW

    def body(q_ref, k_ref, v_ref, bias_ref, valid_ref, sink_ref, do_ref, dki_ref, dvi_ref,
             dq_ref, dk_ref, dv_ref, db_ref, ds_ref):
        @pl.when(pl.program_id(1) == 0)
        def _():
            dk_ref[...] = dki_ref[...]
            dv_ref[...] = dvi_ref[...]
            db_ref[...] = jnp.zeros_like(db_ref)
            ds_ref[...] = jnp.zeros_like(ds_ref)

        for u in range(SWA_QB):
            i = pl.program_id(1) * SWA_QB + u
            band = _band(i)
            lanes = slice(u * WINDOW, (u + 1) * WINDOW)
            qb = _heads_to_lanes(q_ref[:, lanes]) * Q_SCALE
            dob = _heads_to_lanes(do_ref[:, lanes])
            kt = k_ref[:, band]
            p, p_sink, inv = _swa_probs(qb, kt, bias_ref[...], valid_ref[jnp.minimum(i, 1)], sink_ref[...])
            p = p * inv
            dp = _contract0(v_ref[:, band], dob)
            delta = jnp.sum(p * dp, axis=0, keepdims=True)
            dsc = p * (dp - delta)
            ds_ref[...] -= p_sink * inv * delta
            db_ref[...] += dsc
            dscb = dsc.astype(BF16)
            dq_t = jnp.dot(kt, dscb, preferred_element_type=F32) * Q_SCALE
            dq_ref[:, lanes] = _lanes_to_heads(dq_t).astype(dq_ref.dtype)
            dk_ref[:, band] += _contract1(qb, dscb)
            dv_ref[:, band] += _contract1(dob, p.astype(BF16))

    qspec = pl.BlockSpec((rows, cols), lambda g, i: (g, i))
    kspec = pl.BlockSpec((None, dh, sp), lambda g, i: (g, 0, 0))
    bspec = pl.BlockSpec((None, 2 * WINDOW, 8 * WINDOW), lambda g, i: (g, 0, 0))
    sspec = pl.BlockSpec((None, 1, 8 * WINDOW), lambda g, i: (g, 0, 0))
    return pl.pallas_call(
        body, name=name, grid=(ng, s // cols),
        in_specs=[qspec, kspec, kspec, bspec, pl.BlockSpec((2, 2 * WINDOW, 8 * WINDOW), lambda g, i: (0, 0, 0)), sspec,
                  qspec, kspec, kspec],
        out_specs=[qspec, kspec, kspec, bspec, sspec],
        out_shape=[jax.ShapeDtypeStruct(qt.shape, BF16), jax.ShapeDtypeStruct(kpt.shape, F32),
                   jax.ShapeDtypeStruct(kpt.shape, F32), jax.ShapeDtypeStruct(bias_t.shape, F32),
                   jax.ShapeDtypeStruct(sink_row.shape, F32)],
        compiler_params=_params(("parallel", "arbitrary")),
    )(qt, kpt, vpt, bias_t, _band_valid(), sink_row, dot_, dk_in, dv_in)


def _bucket_onehot():
    qi = np.arange(WINDOW)[:, None]
    kj = np.arange(2 * WINDOW)[None, :]
    n = np.maximum(qi + WINDOW - kj, 0)
    max_exact = N_BUCKETS // 2
    nf = np.maximum(n, 1).astype(np.float64)
    val = np.log(nf / max_exact) / math.log(WINDOW / max_exact) * (N_BUCKETS - max_exact)
    assert np.all(np.abs(val - np.round(val))[(n > max_exact) & (n < WINDOW)] > 1e-3)
    large = np.minimum(max_exact + val.astype(np.int64), N_BUCKETS - 1)
    bucket = np.where(n < max_exact, n, large).reshape(-1)
    onehot = np.zeros((128, bucket.size), np.float32)
    onehot[bucket, np.arange(bucket.size)] = 1.0
    return onehot


def _split3(x):
    a = x.astype(BF16)
    r = x - a.astype(F32)
    b = r.astype(BF16)
    c = (r - b.astype(F32)).astype(BF16)
    return a, b, c


def bias_table(rel_bias):
    nh = rel_bias.shape[1]
    oh = jnp.asarray(_bucket_onehot(), BF16)
    n = oh.shape[1]
    tn = 4096
    rb = jnp.zeros((nh, 128), F32).at[:, :N_BUCKETS].set(rel_bias.T)

    def body(rb_ref, oh_ref, o_ref):
        o_ref[...] = sum(jnp.dot(t, oh_ref[...], preferred_element_type=F32) for t in _split3(rb_ref[...]))

    return pl.pallas_call(
        body, name="bias_table", grid=(n // tn,),
        in_specs=[pl.BlockSpec((nh, 128), lambda i: (0, 0)), pl.BlockSpec((128, tn), lambda i: (0, i))],
        out_specs=pl.BlockSpec((nh, tn), lambda i: (0, i)),
        out_shape=jax.ShapeDtypeStruct((nh, n), F32),
        compiler_params=_params(("parallel",)),
    )(rb, oh)


def bias_table_grad(db0, db1):
    nh, n = db0.shape
    oh = jnp.asarray(_bucket_onehot(), BF16)
    tn = 4096

    def body(a_ref, b_ref, oh_ref, o_ref):
        @pl.when(pl.program_id(0) == 0)
        def _():
            o_ref[...] = jnp.zeros_like(o_ref)

        o_ref[...] += sum(lax.dot_general(t, oh_ref[...], (((1,), (1,)), ((), ())), preferred_element_type=F32)
                          for t in _split3(a_ref[...] + b_ref[...]))

    blk = pl.BlockSpec((nh, tn), lambda i: (0, i))
    return pl.pallas_call(
        body, name="bias_table_grad", grid=(n // tn,),
        in_specs=[blk, blk, pl.BlockSpec((128, tn), lambda i: (0, i))],
        out_specs=pl.BlockSpec((nh, 128), lambda i: (0, 0)),
        out_shape=jax.ShapeDtypeStruct((nh, 128), F32),
        compiler_params=_params(("arbitrary",)),
    )(db0, db1, oh)


def _owner_view(ref, name, d):
    if name == 'a_norm':
        return ref.at[d]
    if name in COL_SHARDED:
        n = ref.shape[2] // N_DEV
        return ref.at[:, :, pl.ds(pl.multiple_of(d * n, 128), n)]
    return ref.at[:, d]


def _place():
    return lax.axis_index("x"), lax.axis_index("y"), lax.axis_index("c")


def _dev(p):
    return 4 * p[0] + 2 * p[1] + p[2]


def _remote(src, dst, send_sem, recv_sem, to):
    return pltpu.make_async_remote_copy(src_ref=src, dst_ref=dst, send_sem=send_sem, recv_sem=recv_sem,
                                        device_id=to, device_id_type=MESH)


def _dma_sems(*shapes):
    return [pltpu.SemaphoreType.DMA(sh) for sh in shapes]


def comm_call(name, build, ins, out_shapes, sems):
    n_in, n_out = len(ins), len(out_shapes)

    def body(*refs):
        copies = build(refs[:n_in], refs[n_in:n_in + n_out], *refs[n_in + n_out:])
        for cp in copies:
            cp.start()
        for cp in copies:
            cp.wait()

    hbm = pl.BlockSpec(memory_space=pl.ANY)
    return pl.pallas_call(
        body, name=name, in_specs=[hbm] * n_in, out_specs=[hbm] * n_out, out_shape=list(out_shapes),
        scratch_shapes=sems,
    )(*ins)


def all_gather_weights(names, shards, full_shapes):
    n = len(names)

    def body(*refs):
        ins, outs = refs[:n], refs[n:2 * n]
        send_sems, recv_sems, local_sems = refs[2 * n:]
        x, y, c = _place()
        me, sibling = (x, y, c), (x, y, 1 - c)
        chips = [(1 - x, y), (x, 1 - y), (1 - x, 1 - y)]

        def copy(t, k, block, to, src=None):
            dst = _owner_view(outs[t], names[t], _dev(block))
            return _remote(dst if src is None else src, dst, send_sems.at[t, k], recv_sems.at[t, k], to)

        mine = [pltpu.make_async_copy(ins[t], _owner_view(outs[t], names[t], _dev(me)), local_sems.at[t])
                for t in range(n)]
        for cp in mine:
            cp.start()
        first = []
        for t in range(n):
            first.append(copy(t, 0, me, sibling, src=ins[t]))
            first += [copy(t, 1 + j, me, (*chip, c), src=ins[t]) for j, chip in enumerate(chips)]
        for cp in first:
            cp.start()
        passed = []
        for j, chip in enumerate(chips):
            for t in range(n):
                copy(t, 1 + j, (*chip, c), me).wait_recv()
                fwd = copy(t, 4 + j, (*chip, c), sibling)
                fwd.start()
                passed.append(fwd)
        for t in range(n):
            copy(t, 0, sibling, me).wait_recv()
            for j, chip in enumerate(chips):
                copy(t, 4 + j, (*chip, 1 - c), me).wait_recv()
        for cp in first + passed:
            cp.wait_send()
        for cp in mine:
            cp.wait()

    hbm = pl.BlockSpec(memory_space=pl.ANY)
    return pl.pallas_call(
        body, name="all_gather_layer0",
        in_specs=[hbm] * n, out_specs=[hbm] * n,
        out_shape=[jax.ShapeDtypeStruct(full_shapes[t], shards[t].dtype) for t in range(n)],
        scratch_shapes=_dma_sems((n, 7), (n, 7), (n,)),
    )(*shards)


def ag_direct(names, shards, full_shapes):
    n = len(names)

    def build(ins, outs, send_sems, recv_sems, local_sems, fwd_send_sems, fwd_recv_sems):
        x, y, c = _place()
        peers = [(x, y, 1 - c), (1 - x, y, c), (x, 1 - y, c), (1 - x, 1 - y, c)]
        copies = []
        for t in range(n):
            dst = _owner_view(outs[t], names[t], _dev((x, y, c)))
            copies.append(pltpu.make_async_copy(ins[t], dst, local_sems.at[t]))
            copies += [_remote(ins[t], dst, send_sems.at[t, k], recv_sems.at[t, k], to) for k, to in enumerate(peers)]
        return copies

    def forward(ins, outs, send_sems, recv_sems, local_sems, fwd_send_sems, fwd_recv_sems):
        x, y, c = _place()
        copies = []
        for t in range(n):
            for k, chip in enumerate([(1 - x, y), (x, 1 - y), (1 - x, 1 - y)]):
                view = _owner_view(outs[t], names[t], _dev((*chip, c)))
                copies.append(_remote(view, view, fwd_send_sems.at[t, k], fwd_recv_sems.at[t, k], (x, y, 1 - c)))
        return copies

    return Carry(build, shards, [jax.ShapeDtypeStruct(full_shapes[t], shards[t].dtype) for t in range(n)],
                 _dma_sems((n, 4), (n, 4), (n,), (n, 3), (n, 3)), then=forward)


def sibling_exchange(names, grads, part_shapes):
    n = len(names)

    def build(ins, outs, send_sems, recv_sems):
        x, y, c = _place()
        return [_remote(_owner_view(ins[t], names[t], 2 * q + 1 - c), outs[t].at[q], send_sems.at[t, q],
                        recv_sems.at[t, q], (x, y, 1 - c)) for t in range(n) for q in range(4)]

    return Carry(build, grads, [jax.ShapeDtypeStruct((4,) + part_shapes[t], BF16) for t in range(n)],
                 _dma_sems((n, 4), (n, 4)))


def chip_exchange(names, parts, part_shapes):
    n = len(names)

    def build(ins, outs, send_sems, recv_sems):
        x, y, c = _place()
        chips = [(1 - x, y), (x, 1 - y), (1 - x, 1 - y)]
        return [_remote(ins[t].at[2 * chip[0] + chip[1]], outs[t].at[k], send_sems.at[t, k], recv_sems.at[t, k],
                        (*chip, c)) for t in range(n) for k, chip in enumerate(chips)]

    return Carry(build, parts, [jax.ShapeDtypeStruct((3,) + part_shapes[t], BF16) for t in range(n)],
                 _dma_sems((n, 3), (n, 3)))


def all_gather_rows(x):
    r, w = x.shape

    def body(x_ref, out_ref, send_sems, recv_sems, local_sem):
        px, py, pc = _place()
        me = 4 * px + 2 * py + pc
        mine = pltpu.make_async_copy(x_ref, out_ref.at[me], local_sem)
        mine.start()
        copies = []
        for k in range(1, N_DEV):
            peer = (px ^ (k >> 2), py ^ ((k >> 1) & 1), pc ^ (k & 1))
            copies.append(pltpu.make_async_remote_copy(
                src_ref=x_ref, dst_ref=out_ref.at[me], send_sem=send_sems.at[k - 1], recv_sem=recv_sems.at[k - 1],
                device_id=peer, device_id_type=MESH))
        for cp in copies:
            cp.start()
        for k in range(1, N_DEV):
            peer_idx = me ^ k
            pltpu.make_async_remote_copy(
                src_ref=x_ref, dst_ref=out_ref.at[peer_idx], send_sem=send_sems.at[k - 1],
                recv_sem=recv_sems.at[k - 1], device_id=(px, py, pc), device_id_type=MESH).wait_recv()
        for cp in copies:
            cp.wait_send()
        mine.wait()

    vmem = pl.BlockSpec(memory_space=pltpu.VMEM)
    return pl.pallas_call(
        body, name="all_gather_small_grads",
        in_specs=[vmem], out_specs=vmem,
        out_shape=jax.ShapeDtypeStruct((N_DEV, r, w), x.dtype),
        scratch_shapes=[pltpu.SemaphoreType.DMA((N_DEV - 1,)), pltpu.SemaphoreType.DMA((N_DEV - 1,)),
                        pltpu.SemaphoreType.DMA],
    )(x)


def _adamw(w, g, m, v):
    m = ADAM_B1 * m + (1.0 - ADAM_B1) * g
    v = ADAM_B2 * v + (1.0 - ADAM_B2) * (g * g)
    m_hat = m / (1.0 - ADAM_B1 ** ADAM_STEP)
    v_hat = v / (1.0 - ADAM_B2 ** ADAM_STEP)
    return -ADAM_LR * (m_hat / (jnp.sqrt(v_hat) + ADAM_EPS) + ADAM_WD * w), m, v


def sibling_sum(name, col, grads, recv, core):
    _, nl, rows, cols = recv.shape
    tr = _tile(rows, 512)
    rspec = pl.BlockSpec((None, None, tr, cols), lambda q, l, i, c_ref: (q, l, i, 0))
    if col:
        gspec = pl.BlockSpec((None, tr, cols), lambda q, l, i, c_ref: (l, i, 2 * q + c_ref[0]))
    else:
        gspec = pl.BlockSpec((None, None, tr, cols), lambda q, l, i, c_ref: (l, 2 * q + c_ref[0], i, 0))

    def body(c_ref, g_ref, r_ref, o_ref):
        del c_ref
        o_ref[...] = (g_ref[...].astype(F32) + r_ref[...].astype(F32)).astype(BF16)

    return pl.pallas_call(
        body, name=name,
        grid_spec=pltpu.PrefetchScalarGridSpec(num_scalar_prefetch=1, grid=(4, nl, rows // tr),
                                               in_specs=[gspec, rspec], out_specs=rspec),
        out_shape=jax.ShapeDtypeStruct(recv.shape, BF16),
        compiler_params=_params(("parallel", "parallel", "parallel")),
    )(core.reshape(1), grads, recv)


def reduce_adamw(name, parts, recv, chip, w, m, v, l0, prev):
    _, nl, rows, cols = parts.shape
    tr = _tile(rows, 256)

    def body(q_ref, p_ref, r_ref, w_ref, m_ref, v_ref, *rest):
        del q_ref
        g_out, d_out, m_out, v_out = rest[-4:]
        g = ((p_ref[...].astype(F32) + r_ref[0].astype(F32)) + r_ref[1].astype(F32)) + r_ref[2].astype(F32)
        d, mn, vn = _adamw(w_ref[...], g, m_ref[...], v_ref[...])
        g_out[...] = g
        d_out[...] = d
        m_out[...] = mn
        v_out[...] = vn

    blk = pl.BlockSpec((None, tr, cols), lambda l, i, q_ref: (l0 + l, i, 0))
    prev = list(prev) if prev else []
    return pl.pallas_call(
        body, name=name,
        grid_spec=pltpu.PrefetchScalarGridSpec(
            num_scalar_prefetch=1, grid=(nl, rows // tr),
            in_specs=[pl.BlockSpec((None, None, tr, cols), lambda l, i, q_ref: (q_ref[0], l, i, 0)),
                      pl.BlockSpec((3, None, tr, cols), lambda l, i, q_ref: (0, l, i, 0)), blk, blk, blk]
            + [pl.BlockSpec(memory_space=pl.ANY)] * len(prev),
            out_specs=[blk] * 4),
        out_shape=[jax.ShapeDtypeStruct(w.shape, F32)] * 4,
        input_output_aliases={6 + i: i for i in range(len(prev))},
        compiler_params=_params(("parallel", "parallel")),
    )(chip.reshape(1), parts, recv, w, m, v, *prev)


def small_adamw(name, gathered, w, m, v):
    _, r, c = gathered.shape

    def body(ga_ref, w_ref, m_ref, v_ref, g_out, d_out, m_out, v_out):
        g = ga_ref[0]
        for d in range(1, N_DEV):
            g = g + ga_ref[d]
        dl, mn, vn = _adamw(w_ref[...], g, m_ref[...], v_ref[...])
        g_out[...] = g
        d_out[...] = dl
        m_out[...] = mn
        v_out[...] = vn

    return pl.pallas_call(
        body, name=name,
        out_shape=[jax.ShapeDtypeStruct((r, c), F32)] * 4,
        compiler_params=_params(),
    )(gathered, w, m, v)


def _rms(x, g):
    return x * lax.rsqrt(jnp.mean(x * x, axis=-1, keepdims=True) + EPS) * g


def _rms_bwd_epilogue(dn, x, dres, g):
    r = lax.rsqrt(jnp.mean(x * x, axis=-1, keepdims=True) + EPS)
    xh = x * r
    dyg = dn * g
    dx = dres + r * (dyg - xh * jnp.mean(dyg * xh, axis=-1, keepdims=True))
    return dx, dx, jnp.sum(dn * xh, axis=0, keepdims=True), jnp.sum(dx, axis=0, keepdims=True)


def _residual_then_norms(n_terms):
    def epilogue(acc, *ex):
        h = acc
        for t in ex[:n_terms]:
            h = h + t
        return (h,) + tuple(_rms(h, g) for g in ex[n_terms:])
    return epilogue


def local_step(x, target, small, ex):
    s, d = x.shape
    n_a, n_b = small['a_norm'].shape[0], small['b_norm'].shape[0]
    sg = {}
    gb = {}

    def fwd_mm(name, a, wname, layer, epilogue, extras, out_dtypes, **kw):
        return mm_nn(name, a, *ex.weight(wname, layer), epilogue, extras, out_dtypes, **kw)

    def dx_mm(name, dy, wname, layer, epilogue, extras, out_dtypes, **kw):
        return mm_nt(name, dy, *ex.weight(wname, layer), epilogue, extras, out_dtypes, **kw)

    def dw_mm(name, a, dy, wname, layer, **kw):
        key, slab, shape = ex.grad(wname, layer)
        gb[key] = mm_tn(name, a, dy, gb.get(key), shape, slab, **kw)

    plain = lambda acc: (acc,)
    plus_col = lambda acc, b: (acc + b,)

    bias_flat = bias_table(small['rel_bias'])
    bias_t = bias_flat.reshape(2, 8, WINDOW, 2 * WINDOW).transpose(0, 3, 1, 2).reshape(2, 2 * WINDOW, 8 * WINDOW)
    sink_rows = [jnp.repeat(small['b_sinks'][j], WINDOW).reshape(2, 1, 8 * WINDOW) for j in range(n_b)]

    gain = lambda g: g.reshape(1, -1)

    def mlp_fwd(h, n2, layer, next_gains):
        u, a = fwd_mm(f"mlp_up_fwd{layer}", n2, 'mlp_up', layer,
                      lambda acc: (acc, jnp.square(jnp.maximum(acc, 0.0))), (), (BF16, BF16))
        h2, *nexts = fwd_mm(f"mlp_down_fwd{layer}", a, 'mlp_down', layer, _residual_then_norms(1),
                            (h, *[gain(g) for g in next_gains]), (F32,) + (BF16,) * len(next_gains))
        return h2, nexts, (n2, u, a)

    h = x
    saved = []
    n1 = rms_fwd("a_norm_fwd0", h, small['a_norm'][0])
    for l in range(n_a):
        (qkvt,) = fwd_mm(f"a_qkv_fwd{l}", n1, 'a_wqkv', l, plain, (), (BF16,), out_t=True)
        qkvt = qkvt.reshape(3 * d // HEAD_DIM, HEAD_DIM, s)
        o_t, rtab, carried = sb_fwd(f"sb_fwd{l}", qkvt, ex.fwd_carry(l))
        ex.fwd_done(l, carried)
        o_t = o_t.reshape(d, s)
        h_mid, n2 = fwd_mm(f"a_wo_fwd{l}", o_t, 'a_wo', l, _residual_then_norms(1),
                           (h, gain(small['mlp_norm'][l])), (F32, BF16), a_t=True)
        next_gains = [small['a_norm'][l + 1]] if l + 1 < n_a else [small['b_norm'][0], small['kv_norm']]
        h_out, nexts, mlp_saved = mlp_fwd(h_mid, n2, l, next_gains)
        saved.append((h, n1, qkvt, o_t, rtab, h_mid, mlp_saved))
        h, n1 = h_out, nexts[0]
    h_kv, nkv = h, nexts[1]
    (kvt,) = fwd_mm("kv_fwd", nkv, 'w_kv', 0, plus_col, (small['b_kv'].reshape(-1, 1),), (BF16,), out_t=True)
    kvt = kvt.reshape(2, 2, HEAD_DIM, s)
    kpt, vpt = (jnp.pad(t, ((0, 0), (0, 0), (WINDOW, 0))) for t in (kvt[0], kvt[1]))
    for j in range(n_b):
        layer = n_a + j
        (qbt,) = fwd_mm(f"b_q_fwd{j}", n1, 'b_wq', j, plus_col, (small['b_bq'][j].reshape(-1, 1),), (BF16,),
                        out_t=True)
        o_t = swa_fwd(f"swa_fwd{j}", qbt, kpt, vpt, bias_t, sink_rows[j])
        h_mid, n2 = fwd_mm(f"b_wo_fwd{j}", o_t, 'b_wo', j, _residual_then_norms(2),
                           (h, gain(small['b_bo'][j]), gain(small['mlp_norm'][layer])), (F32, BF16), a_t=True)
        h_out, nexts, mlp_saved = mlp_fwd(h_mid, n2, layer, [small['b_norm'][j + 1]] if j + 1 < n_b else [])
        saved.append((h, n1, qbt, o_t, h_mid, mlp_saved))
        h, n1 = h_out, (nexts[0] if nexts else None)

    dh, dhb, dg_final, loss_b = loss_head(h, small['final_norm'], target)
    sg['final_norm'] = dg_final[0]
    sg['mlp_norm'] = [None] * (n_a + n_b)

    def mlp_bwd(dh, dhb, h_mid, mlp_saved, layer):
        n2, u, a = mlp_saved
        du, *carried = dx_mm(f"mlp_down_dx{layer}", dhb, 'mlp_down', layer,
                             lambda acc, uu: (acc * (2.0 * jnp.maximum(uu.astype(F32), 0.0)),), (u,), (BF16,),
                             exchange=ex.mlp_carry(layer, gb))
        ex.mlp_done(layer, carried)
        dw_mm(f"mlp_down_dw{layer}", a, dhb, 'mlp_down', layer)
        dh2, dh2b, dg, cs = dx_mm(f"mlp_up_dx{layer}", du, 'mlp_up', layer, _rms_bwd_epilogue,
                                  (h_mid, dh, gain(small['mlp_norm'][layer])), (F32, BF16), n_sums=2)
        dw_mm(f"mlp_up_dw{layer}", n2, du, 'mlp_up', layer)
        sg['mlp_norm'][layer] = dg[0]
        return dh2, dh2b, cs

    dkp = jnp.zeros(kpt.shape, F32)
    dvp = jnp.zeros(vpt.shape, F32)
    sg['b_norm'], sg['b_bq'], sg['b_bo'], sg['b_sinks'] = [None] * n_b, [None] * n_b, [None] * n_b, [None] * n_b
    dbias = [None] * n_b
    for j in reversed(range(n_b)):
        layer = n_a + j
        h_in, n1, qbt, o_t, h_mid, mlp_saved = saved[layer]
        dh, dhb, cs = mlp_bwd(dh, dhb, h_mid, mlp_saved, layer)
        sg['b_bo'][j] = cs[0]
        (do_t,) = dx_mm(f"b_wo_dx{j}", dhb, 'b_wo', j, plain, (), (BF16,), out_t=True)
        dw_mm(f"b_wo_dw{j}", o_t, dhb, 'b_wo', j, x_t=True)
        dq_t, dkp, dvp, dbias[j], dsink = swa_bwd(f"swa_bwd{j}", qbt, kpt, vpt, bias_t, sink_rows[j], do_t, dkp, dvp)
        sg['b_sinks'][j] = colsum(f"sink_grad{j}", dsink.reshape(16, WINDOW).T)[0]
        sg['b_bq'][j] = rowsum(f"b_bq_grad{j}", dq_t)
        dh, dhb, dg, _ = dx_mm(f"b_q_dx{j}", dq_t, 'b_wq', j, _rms_bwd_epilogue,
                               (h_in, dh, gain(small['b_norm'][j])), (F32, BF16), a_t=True, n_sums=2)
        dw_mm(f"b_q_dw{j}", n1, dq_t, 'b_wq', j, dy_t=True)
        sg['b_norm'][j] = dg[0]
    unt = lambda t: t.reshape(2, 2 * WINDOW, 8, WINDOW).transpose(0, 2, 3, 1).reshape(bias_flat.shape)
    sg['rel_bias'] = bias_table_grad(unt(dbias[0]), unt(dbias[1]))[:, :N_BUCKETS].T

    dkv_t = jnp.concatenate([dkp[:, :, WINDOW:], dvp[:, :, WINDOW:]], axis=0).reshape(-1, s)
    sg['b_kv'] = rowsum("b_kv_grad", dkv_t)
    dkvb = dkv_t.astype(BF16)
    dh, dhb, dg, _ = dx_mm("kv_dx", dkvb, 'w_kv', 0, _rms_bwd_epilogue, (h_kv, dh, gain(small['kv_norm'])),
                           (F32, BF16), a_t=True, n_sums=2)
    dw_mm("kv_dw", nkv, dkvb, 'w_kv', 0, dy_t=True)
    sg['kv_norm'] = dg[0]

    sg['a_norm'] = [None] * n_a
    for l in reversed(range(n_a)):
        h_in, n1, qkvt, o_t, rtab, h_mid, mlp_saved = saved[l]
        dh, dhb, _ = mlp_bwd(dh, dhb, h_mid, mlp_saved, l)
        (do_t,) = dx_mm(f"a_wo_dx{l}", dhb, 'a_wo', l, plain, (), (BF16,), out_t=True)
        dw_mm(f"a_wo_dw{l}", o_t, dhb, 'a_wo', l, x_t=True)
        dqkv_t, carried = sb_bwd(f"sb_bwd{l}", qkvt, do_t.reshape(d // HEAD_DIM, HEAD_DIM, s), rtab,
                                 ex.bwd_carry(l, gb))
        ex.bwd_done(l, carried)
        dqkv_t = dqkv_t.reshape(3 * d, s)
        dw_mm(f"a_qkv_dw{l}", n1, dqkv_t, 'a_wqkv', l, dy_t=True)
        dh, dhb, dg, _, *carried = dx_mm(f"a_qkv_dx{l}", dqkv_t, 'a_wqkv', l, _rms_bwd_epilogue,
                                         (h_in, dh, gain(small['a_norm'][l])), (F32, BF16), a_t=True, n_sums=2,
                                         exchange=ex.last_carry(gb) if l == 0 else None)
        if l == 0:
            ex.last_done(carried)
        sg['a_norm'][l] = dg[0]

    small_grads = {
        'a_norm': jnp.stack(sg['a_norm']), 'kv_norm': sg['kv_norm'], 'b_kv': sg['b_kv'],
        'b_norm': jnp.stack(sg['b_norm']), 'b_bq': jnp.stack(sg['b_bq']), 'b_sinks': jnp.stack(sg['b_sinks']),
        'b_bo': jnp.stack(sg['b_bo']), 'rel_bias': sg['rel_bias'], 'mlp_norm': jnp.stack(sg['mlp_norm']),
        'final_norm': sg['final_norm'],
    }
    return loss_b, dh, gb, small_grads


def _full_shape(name, shard_shape):
    if name in COL_SHARDED:
        return shard_shape[:2] + (N_DEV * shard_shape[2],)
    nl, r, n = shard_shape
    return (nl, N_DEV, r, n)


def _as_w3_shape(name, shard_shape):
    full = _full_shape(name, shard_shape)
    return full if name in COL_SHARDED else (full[0], full[1] * full[2], full[3])


def _as_w3(name, full):
    if name in COL_SHARDED:
        return full
    nl, nd, r, n = full.shape
    return full.reshape(nl, nd * r, n)


AG_GROUPS = {
    0: (('a_wqkv', 0, 1),),
    1: (('a_wo', 0, 2), ('mlp_up', 0, 2), ('mlp_down', 0, 2), ('a_wqkv', 1, 1)),
    2: (('mlp_up', 2, 2), ('mlp_down', 2, 2), ('b_wq', 0, 2), ('b_wo', 0, 2), ('w_kv', 0, 1)),
}
RS_GROUPS = {
    'A': (('mlp_up', 2, 2), ('mlp_down', 2, 2), ('b_wq', 0, 2), ('b_wo', 0, 2), ('w_kv', 0, 1)),
    'B1': (('a_wo', 1, 1), ('mlp_up', 1, 1), ('mlp_down', 1, 1)),
    'B2': (('a_wqkv', 1, 1),),
    'B3': (('a_wo', 0, 1), ('mlp_up', 0, 1), ('mlp_down', 0, 1)),
    'C': (('a_wqkv', 0, 1),),
}
UNDER_MLP = {1: 'A', 0: 'B2'}
SIBLING_UNDER_SB_BWD = {1: 'B1'}
UNDER_SB_BWD = {1: ('A',), 0: ('B1', 'B2', 'B3')}


class _Exchanges:
    def __init__(self, full0, shards, core, chip, w3, m3, v3):
        self.wbuf = {0: {n: _as_w3(n, full0[n]) for n, _, _ in AG_GROUPS[0]}}
        self.shards, self.core, self.chip = shards, core, chip
        self.w3, self.m3, self.v3 = w3, m3, v3
        self.shard_dims = {n: w3[n].shape[1:] for n in BIG}
        self.parts = {}
        self.gfull = {}
        self.out = {}

    def weight(self, name, layer):
        for group, members in AG_GROUPS.items():
            for n, l0, nl in members:
                if n == name and l0 <= layer < l0 + nl:
                    return self.wbuf[group][name], layer - l0
        raise KeyError((name, layer))

    def fwd_carry(self, layer):
        names = [n for n, _, _ in AG_GROUPS[layer + 1]]
        shards = [self.shards[layer + 1][n] for n in names]
        return ag_direct(names, shards, [_full_shape(n, sh.shape) for n, sh in zip(names, shards)])

    def fwd_done(self, layer, carried):
        names = [n for n, _, _ in AG_GROUPS[layer + 1]]
        self.wbuf[layer + 1] = {n: _as_w3(n, f) for n, f in zip(names, carried)}

    def grad(self, name, layer):
        for group, members in RS_GROUPS.items():
            for n, l0, nl in members:
                if n == name and l0 <= layer < l0 + nl:
                    return (group, name), layer - l0, _as_w3_shape(name, (nl,) + self.shard_dims[name])
        raise KeyError((name, layer))

    def _members(self, group):
        names = [n for n, _, _ in RS_GROUPS[group]]
        return names, [(nl,) + self.shard_dims[n] for n, _, nl in RS_GROUPS[group]]

    def _sibling_carry(self, group, gb):
        names, shapes = self._members(group)
        self.gfull[group] = [gb[(group, n)].reshape(_full_shape(n, sh)) for n, sh in zip(names, shapes)]
        return sibling_exchange(names, self.gfull[group], shapes)

    def _sibling_done(self, group, recv):
        names, _ = self._members(group)
        self.parts[group] = [sibling_sum(f"rs_sibling_sum_{group}_{n}", n in COL_SHARDED, g, r, self.core)
                             for n, g, r in zip(names, self.gfull[group], recv)]

    def _sibling_stage(self, group, gb):
        ce = self._sibling_carry(group, gb)
        self._sibling_done(group, comm_call(f"rs_sibling_exchange_{group}", ce.build, ce.ins, ce.out_shapes, ce.sems))

    def mlp_carry(self, layer, gb):
        return self._sibling_carry(UNDER_MLP[layer], gb) if layer in UNDER_MLP else None

    def mlp_done(self, layer, carried):
        if layer in UNDER_MLP:
            self._sibling_done(UNDER_MLP[layer], carried)

    def bwd_carry(self, layer, gb):
        names, parts, shapes = [], [], []
        for group in UNDER_SB_BWD[layer]:
            if group not in self.parts:
                self._sibling_stage(group, gb)
            names += self._members(group)[0]
            shapes += self._members(group)[1]
            parts += self.parts[group]
        exchange = chip_exchange(names, parts, shapes)
        if layer in SIBLING_UNDER_SB_BWD:
            exchange = _together(exchange, self._sibling_carry(SIBLING_UNDER_SB_BWD[layer], gb))
        return exchange

    def bwd_done(self, layer, carried):
        for group in UNDER_SB_BWD[layer]:
            n = len(RS_GROUPS[group])
            self._adamw(group, carried[:n])
            carried = carried[n:]
        if layer in SIBLING_UNDER_SB_BWD:
            self._sibling_done(SIBLING_UNDER_SB_BWD[layer], carried)

    def last_carry(self, gb):
        self._sibling_stage('C', gb)
        names, shapes = self._members('C')
        return chip_exchange(names, self.parts['C'], shapes)

    def last_done(self, carried):
        self._adamw('C', carried)

    def _adamw(self, group, recv2):
        for (n, l0, _), p, r in zip(RS_GROUPS[group], self.parts[group], recv2):
            self.out[n] = reduce_adamw(f"adamw_{group}_{n}", p, r, self.chip, self.w3[n], self.m3[n], self.v3[n],
                                       l0, self.out.get(n))


def _pack_small(vals):
    flat = jnp.concatenate([vals[n].reshape(-1).astype(F32) for n in SMALL] + [vals['loss'].reshape(-1)])
    rows = -(-flat.shape[0] // 1024) * 8
    return jnp.pad(flat, (0, rows * 128 - flat.shape[0])).reshape(rows, 128)


def _unpack_small(packed, shapes):
    flat = packed.reshape(-1)
    out, off = {}, 0
    for n in SMALL + ['loss']:
        size = int(np.prod(shapes[n]))
        out[n] = flat[off:off + size].reshape(shapes[n])
        off += size
    return out


def kernel(x, a_norm, a_wqkv, a_wo, kv_norm, w_kv, b_kv, b_norm, b_wq, b_bq, b_sinks, b_wo, b_bo, rel_bias, mlp_norm, mlp_up, mlp_down, final_norm, loss_target, m_a_norm, m_a_wqkv, m_a_wo, m_kv_norm, m_w_kv, m_b_kv, m_b_norm, m_b_wq, m_b_bq, m_b_sinks, m_b_wo, m_b_bo, m_rel_bias, m_mlp_norm, m_mlp_up, m_mlp_down, m_final_norm, v_a_norm, v_a_wqkv, v_a_wo, v_kv_norm, v_w_kv, v_b_kv, v_b_norm, v_b_wq, v_b_bq, v_b_sinks, v_b_wo, v_b_bo, v_rel_bias, v_mlp_norm, v_mlp_up, v_mlp_down, v_final_norm):
    w = dict(a_norm=a_norm, a_wqkv=a_wqkv, a_wo=a_wo, kv_norm=kv_norm, w_kv=w_kv, b_kv=b_kv, b_norm=b_norm,
             b_wq=b_wq, b_bq=b_bq, b_sinks=b_sinks, b_wo=b_wo, b_bo=b_bo, rel_bias=rel_bias, mlp_norm=mlp_norm,
             mlp_up=mlp_up, mlp_down=mlp_down, final_norm=final_norm)
    m = dict(a_norm=m_a_norm, a_wqkv=m_a_wqkv, a_wo=m_a_wo, kv_norm=m_kv_norm, w_kv=m_w_kv, b_kv=m_b_kv,
             b_norm=m_b_norm, b_wq=m_b_wq, b_bq=m_b_bq, b_sinks=m_b_sinks, b_wo=m_b_wo, b_bo=m_b_bo,
             rel_bias=m_rel_bias, mlp_norm=m_mlp_norm, mlp_up=m_mlp_up, mlp_down=m_mlp_down, final_norm=m_final_norm)
    v = dict(a_norm=v_a_norm, a_wqkv=v_a_wqkv, a_wo=v_a_wo, kv_norm=v_kv_norm, w_kv=v_w_kv, b_kv=v_b_kv,
             b_norm=v_b_norm, b_wq=v_b_wq, b_bq=v_b_bq, b_sinks=v_b_sinks, b_wo=v_b_wo, b_bo=v_b_bo,
             rel_bias=v_rel_bias, mlp_norm=v_mlp_norm, mlp_up=v_mlp_up, mlp_down=v_mlp_down, final_norm=v_final_norm)
    px, py, pc = _place()
    me = 4 * px + 2 * py + pc
    chip = (2 * px + py).astype(jnp.int32)
    core = pc.astype(jnp.int32)

    as3 = lambda t: t[None] if t.ndim == 2 else t
    w3, m3, v3 = ({n: as3(src[n]) for n in BIG} for src in (w, m, v))
    shards = {g: {n: w3[n][l0:l0 + nl].astype(BF16) for n, l0, nl in members} for g, members in AG_GROUPS.items()}
    an_pad = jnp.zeros((8, 128), F32).at[:a_norm.shape[0]].set(a_norm)
    names0 = [n for n, _, _ in AG_GROUPS[0]]
    full0 = all_gather_weights(names0 + ['a_norm'], [shards[0][n] for n in names0] + [an_pad],
                               [_full_shape(n, shards[0][n].shape) for n in names0] + [(N_DEV, 8, 128)])
    full0 = dict(zip(names0 + ['a_norm'], full0))
    n_a = a_norm.shape[0]
    small = {n: w[n] for n in SMALL}
    small['a_norm'] = full0['a_norm'][:, :n_a].transpose(1, 0, 2).reshape(n_a, -1)

    ex = _Exchanges(full0, shards, core, chip, w3, m3, v3)
    loss_b, grad_x, gb, sgrads = local_step(x[0], loss_target[0], small, ex)
    out = {n: [t.reshape(w[n].shape) for t in bufs] for n, bufs in ex.out.items()}

    sgrads['loss'] = loss_b[0, :1]
    gathered = all_gather_rows(_pack_small(sgrads))
    shapes = {n: w[n].shape for n in SMALL}
    shapes['a_norm'] = (n_a, a_norm.shape[1] * N_DEV)
    shapes['loss'] = (1,)
    zeros1 = jnp.zeros((1,), F32)

    def packed(src):
        vals = {n: src[n] for n in SMALL}
        vals['a_norm'] = jnp.zeros(shapes['a_norm'], F32)
        vals['loss'] = zeros1
        return _pack_small(vals)

    sm = small_adamw("adamw_small", gathered, packed(w), packed(m), packed(v))
    sm = [_unpack_small(t, shapes) for t in sm]
    g_an = lax.dynamic_slice_in_dim(sm[0]['a_norm'], me * a_norm.shape[1], a_norm.shape[1], axis=1)
    pad = lambda t: jnp.zeros((8, 128), F32).at[:n_a].set(t)
    gathered_an = jnp.zeros((N_DEV, 8, 128), F32).at[0].set(pad(g_an))
    an = small_adamw("adamw_a_norm", gathered_an, pad(a_norm), pad(m_a_norm), pad(v_a_norm))
    for i in range(4):
        sm[i]['a_norm'] = an[i][:n_a]
    for n in BIG:
        for i in range(4):
            sm[i][n] = out[n][i]
    loss = sm[0]['loss'][0]
    return (loss, grad_x[None], *[sm[0][n] for n in WEIGHTS], *[sm[1][n] for n in WEIGHTS],
            *[sm[2][n] for n in WEIGHTS], *[sm[3][n] for n in WEIGHTS])
```

```python
import math

import numpy as np
import jax
import jax.numpy as jnp
from jax import lax
from jax.experimental import pallas as pl
from jax.experimental.pallas import tpu as pltpu

F32 = jnp.float32
BF16 = jnp.bfloat16
MESH = pl.DeviceIdType.MESH

N_DEV = 8
HEAD_DIM = 64
WINDOW = 128
N_BUCKETS = 32
EPS = 1e-5
NEG_INF = -1e30
Q_SCALE = 1.0 / math.sqrt(HEAD_DIM)
LOG2E = 1.4426950408889634

ADAM_LR, ADAM_B1, ADAM_B2, ADAM_EPS, ADAM_WD, ADAM_STEP = 0.001, 0.9, 0.999, 1e-08, 0.01, 10

SB_BQ = 512
SB_BK = 128
SB_DEAD = 160.0
SB_UNSEEN = 1e30
ROW_TILE = 512
VMEM_LIMIT = 56 * 1024 * 1024

WEIGHTS = ['a_norm', 'a_wqkv', 'a_wo', 'kv_norm', 'w_kv', 'b_kv', 'b_norm', 'b_wq', 'b_bq', 'b_sinks', 'b_wo',
           'b_bo', 'rel_bias', 'mlp_norm', 'mlp_up', 'mlp_down', 'final_norm']
BIG = ['a_wqkv', 'a_wo', 'w_kv', 'b_wq', 'b_wo', 'mlp_up', 'mlp_down']
COL_SHARDED = ('a_wqkv', 'mlp_up')
SMALL = ['a_norm', 'kv_norm', 'b_kv', 'b_norm', 'b_bq', 'b_sinks', 'b_bo', 'rel_bias', 'mlp_norm', 'final_norm']


def _params(sem=None):
    return pltpu.CompilerParams(dimension_semantics=sem, vmem_limit_bytes=VMEM_LIMIT)


def _pick(n, cands):
    for c in cands:
        if n % c == 0:
            return c
    raise ValueError(n)


def _tile(n, want):
    return n if n <= want else _pick(n, (want, want // 2, want // 4))


MM_TILE_BUDGET = 36 * 1024 * 1024


def _row_tile(m, contraction, cols, streams):
    weight = 2 * contraction * cols * 2
    for rows in (2048, 1024, 512):
        if m % rows == 0 and weight + 2 * rows * (2 * contraction + cols * sum(streams)) <= MM_TILE_BUDGET:
            return rows
    return _tile(m, 512)


def mm_nn(name, a, w3, layer, epilogue, extras, out_dtypes, a_t=False, out_t=False):
    k, m = a.shape if a_t else a.shape[::-1]
    _, kw, n = w3.shape
    assert kw == k
    tn = _tile(n, 1024)
    tm = _row_tile(m, k, tn, [jnp.dtype(t).itemsize for t in out_dtypes]
                   + [e.dtype.itemsize for e in extras if e.size == m * n])
    ne, no = len(extras), len(out_dtypes)
    a_dim = 0 if a_t else 1

    def body(a_ref, w_ref, *rest):
        ex, outs = rest[:ne], rest[ne:ne + no]
        if out_t:
            acc = lax.dot_general(w_ref[...], a_ref[...], (((0,), (a_dim,)), ((), ())), preferred_element_type=F32)
        else:
            acc = lax.dot_general(a_ref[...], w_ref[...], (((a_dim,), (0,)), ((), ())), preferred_element_type=F32)
        for o, r in zip(outs, epilogue(acc, *[e[...] for e in ex])):
            o[...] = r.astype(o.dtype)

    if out_t:
        tile = pl.BlockSpec((tn, tm), lambda i, j: (j, i))
        vec = pl.BlockSpec((tn, 1), lambda i, j: (j, 0))
        out_shape = (n, m)
    else:
        tile = pl.BlockSpec((tm, tn), lambda i, j: (i, j))
        vec = pl.BlockSpec((1, tn), lambda i, j: (0, j))
        out_shape = (m, n)
    a_spec = pl.BlockSpec((k, tm), lambda i, j: (0, i)) if a_t else pl.BlockSpec((tm, k), lambda i, j: (i, 0))
    return pl.pallas_call(
        body, name=name, grid=(m // tm, n // tn),
        in_specs=[a_spec, pl.BlockSpec((None, k, tn), lambda i, j: (layer, 0, j))]
        + [tile if e.shape == out_shape else vec for e in extras],
        out_specs=[tile] * no,
        out_shape=[jax.ShapeDtypeStruct(out_shape, d) for d in out_dtypes],
        compiler_params=_params(("parallel", "parallel")),
    )(a, w3, *extras)


def mm_nt(name, dy, w3, layer, epilogue, extras, out_dtypes, a_t=False, out_t=False, n_sums=0, exchange=None):
    n, m = dy.shape if a_t else dy.shape[::-1]
    _, k, nw = w3.shape
    assert nw == n and not (out_t and n_sums)
    tko = _tile(k, 1024)
    tm = _row_tile(m, n, tko, [jnp.dtype(t).itemsize for t in out_dtypes]
                   + [e.dtype.itemsize for e in extras if e.size == m * k])
    ne, no = len(extras), len(out_dtypes)
    a_dim = 0 if a_t else 1

    def body(a_ref, w_ref, *rest):
        ex = rest[:ne]
        at = lambda step: (pl.program_id(0) == step[0]) & (pl.program_id(1) == step[1])
        results, _, start_carried, wait_carried = _carried(exchange, rest[ne:], no + n_sums, 0, at((0, 0)),
                                                           at((m // tm - 1, k // tko - 1)))
        outs, sums = results[:no], results[no:]
        start_carried()
        if out_t:
            acc = lax.dot_general(w_ref[...], a_ref[...], (((1,), (a_dim,)), ((), ())), preferred_element_type=F32)
        else:
            acc = lax.dot_general(a_ref[...], w_ref[...], (((a_dim,), (1,)), ((), ())), preferred_element_type=F32)
        res = epilogue(acc, *[e[...] for e in ex])
        for o, v in zip(outs, res):
            o[...] = v.astype(o.dtype)
        if n_sums:
            @pl.when(pl.program_id(0) == 0)
            def _():
                for o in sums:
                    o[...] = jnp.zeros_like(o)

            for o, v in zip(sums, res[no:]):
                o[...] += v
        wait_carried()

    if out_t:
        tile = pl.BlockSpec((tko, tm), lambda i, ko: (ko, i))
        out_shape = (k, m)
    else:
        tile = pl.BlockSpec((tm, tko), lambda i, ko: (i, ko))
        out_shape = (m, k)
    vec = pl.BlockSpec((1, tko), lambda i, ko: (0, ko))
    a_spec = pl.BlockSpec((n, tm), lambda i, ko: (0, i)) if a_t else pl.BlockSpec((tm, n), lambda i, ko: (i, 0))
    hbm = pl.BlockSpec(memory_space=pl.ANY)
    c_ins, c_outs, c_sems = (exchange.ins, exchange.out_shapes, exchange.sems) if exchange else ([], [], [])
    sequential = n_sums or exchange
    return pl.pallas_call(
        body, name=name, grid=(m // tm, k // tko),
        in_specs=[a_spec, pl.BlockSpec((None, tko, n), lambda i, ko: (layer, ko, 0))]
        + [tile if e.shape == out_shape else vec for e in extras] + [hbm] * len(c_ins),
        out_specs=[tile] * no + [vec] * n_sums + [hbm] * len(c_outs),
        out_shape=[jax.ShapeDtypeStruct(out_shape, d) for d in out_dtypes] + [jax.ShapeDtypeStruct((1, k), F32)] * n_sums
        + c_outs,
        scratch_shapes=c_sems,
        compiler_params=_params(("arbitrary" if sequential else "parallel", "arbitrary" if exchange else "parallel")),
    )(dy, w3, *extras, *c_ins)


def mm_tn(name, x, dy, gbuf, shape, layer, x_t=False, dy_t=False):
    k, s = x.shape if x_t else x.shape[::-1]
    _, kw, n = shape
    assert kw == k and dy.shape == ((n, s) if dy_t else (s, n))
    tkk = _tile(k, 512)
    tn = _tile(n, 1024)

    def body(x_ref, dy_ref, *rest):
        g_out = rest[-1]
        g_out[...] = lax.dot_general(x_ref[...], dy_ref[...], (((1 if x_t else 0,), (1 if dy_t else 0,)), ((), ())),
                                     preferred_element_type=F32).astype(g_out.dtype)

    prev = [] if gbuf is None else [gbuf]
    x_spec = pl.BlockSpec((tkk, s), lambda ki, j: (ki, 0)) if x_t else pl.BlockSpec((s, tkk), lambda ki, j: (0, ki))
    dy_spec = pl.BlockSpec((tn, s), lambda ki, j: (j, 0)) if dy_t else pl.BlockSpec((s, tn), lambda ki, j: (0, j))
    return pl.pallas_call(
        body, name=name, grid=(k // tkk, n // tn),
        in_specs=[x_spec, dy_spec] + [pl.BlockSpec(memory_space=pl.ANY)] * len(prev),
        out_specs=pl.BlockSpec((None, tkk, tn), lambda ki, j: (layer, ki, j)),
        out_shape=jax.ShapeDtypeStruct(shape, BF16),
        input_output_aliases={2: 0} if prev else {},
        compiler_params=_params(("parallel", "parallel")),
    )(x, dy, *prev)


def rms_fwd(name, h, g):
    s, d = h.shape
    tr = _pick(s, (ROW_TILE, 256, 128))

    def body(h_ref, g_ref, o_ref):
        x = h_ref[...]
        r = lax.rsqrt(jnp.mean(x * x, axis=-1, keepdims=True) + EPS)
        o_ref[...] = (x * r * g_ref[...]).astype(o_ref.dtype)

    return pl.pallas_call(
        body, name=name, grid=(s // tr,),
        in_specs=[pl.BlockSpec((tr, d), lambda i: (i, 0)), pl.BlockSpec((1, d), lambda i: (0, 0))],
        out_specs=pl.BlockSpec((tr, d), lambda i: (i, 0)),
        out_shape=jax.ShapeDtypeStruct((s, d), BF16),
        compiler_params=_params(("parallel",)),
    )(h, g.reshape(1, d))


def loss_head(h, g, target):
    s, d = h.shape
    tr = _pick(s, (ROW_TILE, 256, 128))

    def body(h_ref, g_ref, t_ref, dx_ref, dxb_ref, dg_ref, loss_ref):
        i = pl.program_id(0)
        x = h_ref[...]
        r = lax.rsqrt(jnp.mean(x * x, axis=-1, keepdims=True) + EPS)
        xh = x * r
        gw = g_ref[...]
        err = xh * gw - t_ref[...]
        dn_ = err * (1.0 / d)
        dyg = dn_ * gw
        dx = r * (dyg - xh * jnp.mean(dyg * xh, axis=-1, keepdims=True))
        dx_ref[...] = dx
        dxb_ref[...] = dx.astype(BF16)

        @pl.when(i == 0)
        def _():
            dg_ref[...] = jnp.zeros_like(dg_ref)
            loss_ref[...] = jnp.zeros_like(loss_ref)

        dg_ref[...] += jnp.sum(dn_ * xh, axis=0, keepdims=True)
        per_row = jnp.sum(err * err, axis=-1, keepdims=True) * (0.5 / d)
        loss_ref[...] += jnp.broadcast_to(jnp.sum(per_row, axis=0, keepdims=True), loss_ref.shape)

    row = pl.BlockSpec((tr, d), lambda i: (i, 0))
    vec = pl.BlockSpec((1, d), lambda i: (0, 0))
    return pl.pallas_call(
        body, name="loss_head", grid=(s // tr,),
        in_specs=[row, vec, row],
        out_specs=[row, row, vec, pl.BlockSpec((1, 128), lambda i: (0, 0))],
        out_shape=[jax.ShapeDtypeStruct((s, d), F32), jax.ShapeDtypeStruct((s, d), BF16),
                   jax.ShapeDtypeStruct((1, d), F32), jax.ShapeDtypeStruct((1, 128), F32)],
        compiler_params=_params(("arbitrary",)),
    )(h, g.reshape(1, d), target)


def colsum(name, x):
    s, n = x.shape
    tr = _pick(s, (ROW_TILE, 256, 128))

    def body(x_ref, o_ref):
        @pl.when(pl.program_id(0) == 0)
        def _():
            o_ref[...] = jnp.zeros_like(o_ref)

        o_ref[...] += jnp.sum(x_ref[...].astype(F32), axis=0, keepdims=True)

    return pl.pallas_call(
        body, name=name, grid=(s // tr,),
        in_specs=[pl.BlockSpec((tr, n), lambda i: (i, 0))],
        out_specs=pl.BlockSpec((1, n), lambda i: (0, 0)),
        out_shape=jax.ShapeDtypeStruct((1, n), F32),
        compiler_params=_params(("arbitrary",)),
    )(x)


def rowsum(name, x):
    n, s = x.shape
    ts = _pick(s, (1024, 512, 256, 128))

    def body(x_ref, o_ref):
        @pl.when(pl.program_id(0) == 0)
        def _():
            o_ref[...] = jnp.zeros_like(o_ref)

        o_ref[...] += jnp.sum(x_ref[...].astype(F32), axis=1, keepdims=True)

    return pl.pallas_call(
        body, name=name, grid=(s // ts,),
        in_specs=[pl.BlockSpec((n, ts), lambda i: (0, i))],
        out_specs=pl.BlockSpec((n, 1), lambda i: (0, 0)),
        out_shape=jax.ShapeDtypeStruct((n, 1), F32),
        compiler_params=_params(("arbitrary",)),
    )(x)[:, 0]


def _tri_rows(reverse):
    i = np.arange(SB_BK)
    tri = (i[None, :] >= i[:, None]) if reverse else (i[None, :] <= i[:, None])
    tri = np.concatenate([tri, tri], axis=1)
    return jnp.asarray(np.concatenate([tri, np.ones((8, 2 * SB_BK), bool)], axis=0), BF16)


def _hi_lo_rows(x):
    hi = x.astype(BF16)
    lo = (x - hi.astype(F32)).astype(BF16)
    return jnp.concatenate([hi, lo], axis=0)


def _softplus2(zs):
    neg_abs = lax.bitcast_convert_type(lax.bitcast_convert_type(zs, jnp.uint32) | jnp.uint32(0x80000000), F32)
    return jnp.maximum(zs, 0.0) + jnp.log2(1.0 + jnp.exp2(neg_abs))


def _pair_mask(first_rel_block, bq):
    key = lax.broadcasted_iota(jnp.int32, (2 * SB_BK, bq), 0) + first_rel_block * SB_BK
    qry = lax.broadcasted_iota(jnp.int32, (2 * SB_BK, bq), 1)
    return key < qry


def _row_of(table8, sub8, r):
    return jnp.sum(jnp.where(sub8 == r, table8, 0.0), axis=0, keepdims=True)


def _keys(j0):
    return pl.ds(pl.multiple_of(j0 * SB_BK, 2 * SB_BK), 2 * SB_BK)


class Carry:
    def __init__(self, build, ins, out_shapes, sems, then=None):
        self.build, self.ins, self.out_shapes, self.sems = build, list(ins), list(out_shapes), list(sems)
        self.then = then


def _together(a, b):
    assert a.then is None and b.then is None
    ni, no, ns = len(a.ins), len(a.out_shapes), len(a.sems)

    def build(ins, outs, *sems):
        return a.build(ins[:ni], outs[:no], *sems[:ns]) + b.build(ins[ni:], outs[no:], *sems[ns:])

    return Carry(build, a.ins + b.ins, a.out_shapes + b.out_shapes, a.sems + b.sems)


def _carried(carry, rest, n_out, n_scratch, first, last):
    n_ci = len(carry.ins) if carry else 0
    n_co = len(carry.out_shapes) if carry else 0
    cin, outs = rest[:n_ci], rest[n_ci:n_ci + n_out]
    cout = rest[n_ci + n_out:n_ci + n_out + n_co]
    scratch = rest[n_ci + n_out + n_co:n_ci + n_out + n_co + n_scratch]
    csems = rest[n_ci + n_out + n_co + n_scratch:]

    def start():
        if carry:
            @pl.when(first)
            def _():
                for cp in carry.build(cin, cout, *csems):
                    cp.start()

    def wait():
        if carry:
            @pl.when(last)
            def _():
                for cp in carry.build(cin, cout, *csems):
                    cp.wait()
                if carry.then:
                    second = carry.then(cin, cout, *csems)
                    for cp in second:
                        cp.start()
                    for cp in second:
                        cp.wait()

    return outs, scratch, start, wait


def _contract0(a, b):
    return lax.dot_general(a, b, (((0,), (0,)), ((), ())), preferred_element_type=F32)


def _contract1(a, b):
    return lax.dot_general(a, b, (((1,), (1,)), ((), ())), preferred_element_type=F32)


def sb_fwd(name, qkvt, exchange=None):
    nh, dh, s = qkvt.shape[0] // 3, qkvt.shape[1], qkvt.shape[2]
    bq = SB_BQ
    per_q = bq // SB_BK
    nkb = s // SB_BK
    assert s % bq == 0 and per_q == 4 and nkb % 8 == 0

    def body(q_ref, k_ref, v_ref, a_ref, *rest):
        head = pl.program_id(0)
        (o_ref, rtab_ref), (acc, zbuf, wbuf), start_carried, wait_carried = _carried(
            exchange, rest, 2, 3, head == 0, head == nh - 1)
        start_carried()
        tri = a_ref[...]
        sub8 = lax.broadcasted_iota(jnp.int32, (8, bq), 0)
        rtab_ref[...] = jnp.full(rtab_ref.shape, SB_UNSEEN, F32)
        kf = k_ref[...].astype(F32)
        k_max2 = jnp.max(jnp.sum(kf * kf, axis=0, keepdims=True), axis=1, keepdims=True)

        def query_block(i, _):
            lanes = pl.ds(pl.multiple_of(i * bq, bq), bq)
            qb = q_ref[:, lanes] * Q_SCALE
            acc[...] = jnp.zeros_like(acc)
            qf = qb.astype(F32)
            bound = jnp.sqrt(jnp.sum(qf * qf, axis=0, keepdims=True) * k_max2) * (1.001 * LOG2E)

            def scores(j0):
                return _contract0(k_ref[:, _keys(j0)], qb) * LOG2E

            def pair(j0, slot, run, rt8, mask, has_prev):
                zs = zbuf[slot]
                zbuf[1 - slot] = scores(jnp.maximum(j0 - 2, 0))
                if has_prev:
                    acc[...] += jnp.dot(v_ref[:, _keys(j0 + 2)], wbuf[1 - slot], preferred_element_type=F32)
                p = _softplus2(zs)
                if mask is not None:
                    p = jnp.where(mask, p, 0.0)
                cr1 = jnp.dot(tri, _hi_lo_rows(p[SB_BK:]), preferred_element_type=F32)
                cr0 = jnp.dot(tri, _hi_lo_rows(p[:SB_BK]), preferred_element_type=F32)
                run1 = run + cr1[SB_BK:SB_BK + 1]
                w = jnp.exp2(jnp.concatenate([zs[:SB_BK] - cr0[:SB_BK] - run1, zs[SB_BK:] - cr1[:SB_BK] - run],
                                             axis=0))
                if mask is not None:
                    w = jnp.where(mask, w, 0.0)
                wbuf[slot] = w.astype(BF16)
                rt8 = jnp.where(j0 % 8 == 6, SB_UNSEEN, rt8)
                rt8 = jnp.where(sub8 == (j0 + 1) % 8, run, jnp.where(sub8 == j0 % 8, run1, rt8))
                rtab_ref[pl.ds(pl.multiple_of((j0 // 8) * 8, 8), 8), lanes] = rt8
                return run1 + cr0[SB_BK:SB_BK + 1], rt8

            def alive(run):
                return jnp.min(run - bound) < SB_DEAD

            top = i * per_q
            zbuf[0] = scores(top + 2)
            state = (jnp.zeros((1, bq), F32), jnp.full((8, bq), SB_UNSEEN, F32))
            state = pair(top + 2, 0, *state, _pair_mask(2, bq), False)
            state = pair(top, 1, *state, _pair_mask(0, bq), True)

            def step(c):
                it, pairs, _, run, rt8 = c
                j0 = top - 2 - 4 * it
                run, rt8 = pair(j0, 0, run, rt8, None, True)
                go = alive(run)
                run, rt8 = lax.cond(go, lambda r, t: pair(j0 - 2, 1, r, t, None, True), lambda r, t: (r, t), run, rt8)
                return it + 1, pairs + 1 + go.astype(jnp.int32), go & alive(run), run, rt8

            pairs = lax.while_loop(lambda c: (c[0] < i) & c[2], step, (0, 0, alive(state[0]), *state))[1]
            acc[...] += jnp.dot(v_ref[:, _keys(top - 2 * pairs)], wbuf[(pairs + 1) % 2], preferred_element_type=F32)
            o_ref[:, lanes] = acc[...].astype(o_ref.dtype)
            return 0

        lax.fori_loop(0, s // bq, query_block, 0)
        wait_carried()

    def head_spec(offset, rows):
        return pl.BlockSpec((None, rows, s), lambda h: (h + offset, 0, 0))

    hbm = pl.BlockSpec(memory_space=pl.ANY)
    c_ins, c_outs, c_sems = (exchange.ins, exchange.out_shapes, exchange.sems) if exchange else ([], [], [])
    outs = pl.pallas_call(
        body, name=name, grid=(nh,),
        in_specs=[head_spec(0, dh), head_spec(nh, dh), head_spec(2 * nh, dh),
                  pl.BlockSpec((SB_BK + 8, 2 * SB_BK), lambda h: (0, 0))] + [hbm] * len(c_ins),
        out_specs=[head_spec(0, dh), head_spec(0, nkb)] + [hbm] * len(c_outs),
        out_shape=[jax.ShapeDtypeStruct((nh, dh, s), BF16), jax.ShapeDtypeStruct((nh, nkb, s), F32)] + c_outs,
        scratch_shapes=[pltpu.VMEM((dh, bq), F32), pltpu.VMEM((2, 2 * SB_BK, bq), F32),
                        pltpu.VMEM((2, 2 * SB_BK, bq), BF16)] + c_sems,
        compiler_params=_params(("arbitrary",)),
    )(qkvt, qkvt, qkvt, _tri_rows(True), *c_ins)
    return outs[0], outs[1], outs[2:]


def sb_bwd(name, qkvt, dot_, rtab, exchange=None):
    nh, dh, s = qkvt.shape[0] // 3, qkvt.shape[1], qkvt.shape[2]
    bq = SB_BQ
    per_q = bq // SB_BK
    nkb = s // SB_BK

    def body(qt_ref, kt_ref, vt_ref, dot_ref, rtab_ref, ar_ref, af_ref, *rest):
        head = pl.program_id(0)
        (dqkv_ref,), (dq_acc, dk_acc, dv_acc, zbuf, dwbuf, dzbuf, wbuf), start_carried, wait_carried = \
            _carried(exchange, rest, 1, 7, head == 0, head == nh - 1)
        dq_ref, dk_ref, dv_ref = dqkv_ref.at[0], dqkv_ref.at[1], dqkv_ref.at[2]
        start_carried()
        dk_acc[...] = jnp.zeros_like(dk_acc)
        dv_acc[...] = jnp.zeros_like(dv_acc)
        tri_rev = ar_ref[...][:SB_BK]
        tri_fwd = af_ref[...]
        sub8 = lax.broadcasted_iota(jnp.int32, (8, bq), 0)

        def query_block(i, _):
            lanes = pl.ds(pl.multiple_of(i * bq, bq), bq)
            qtb = qt_ref[:, lanes] * Q_SCALE
            dotb = dot_ref[:, lanes]
            dq_acc[...] = jnp.zeros_like(dq_acc)
            last_j = i * per_q + 2
            seen = jnp.max(jnp.where(rtab_ref[:, lanes] < 0.1 * SB_UNSEEN, 1.0, 0.0), axis=1, keepdims=True)
            pairs = jnp.clip((jnp.sum(seen).astype(jnp.int32) - per_q) // 2, 0, 2 * i)
            odd = pairs % 2
            first_j = i * per_q - 2 * pairs

            def issue(j0, slot):
                zbuf[slot] = _contract0(kt_ref[:, _keys(j0)], qtb) * LOG2E
                dwbuf[slot] = _contract0(vt_ref[:, _keys(j0)], dotb)

            def retire(j0, slot):
                keys = _keys(j0)
                dq_acc[...] += jnp.dot(kt_ref[:, keys], dzbuf[slot], preferred_element_type=F32)
                dk_acc[:, keys] += _contract1(qtb, dzbuf[slot])
                dv_acc[:, keys] += _contract1(dotb, wbuf[slot])

            def pair(j0, slot, g_run, mask):
                zs = zbuf[slot]
                dw = dwbuf[slot]
                issue(jnp.minimum(j0 + 2, last_j), 1 - slot)
                retire(jnp.maximum(j0 - 2, first_j), 1 - slot)
                p_raw = _softplus2(zs)
                p = p_raw if mask is None else jnp.where(mask, p_raw, 0.0)
                c0 = jnp.dot(tri_rev, _hi_lo_rows(p[:SB_BK]), preferred_element_type=F32)
                c1 = jnp.dot(tri_rev, _hi_lo_rows(p[SB_BK:]), preferred_element_type=F32)
                rt8 = rtab_ref[pl.ds(pl.multiple_of((j0 // 8) * 8, 8), 8), lanes]
                r0 = _row_of(rt8, sub8, j0 % 8)
                r1 = _row_of(rt8, sub8, (j0 + 1) % 8)
                w = jnp.exp2(jnp.concatenate([zs[:SB_BK] - c0 - r0, zs[SB_BK:] - c1 - r1], axis=0))
                if mask is not None:
                    w = jnp.where(mask, w, 0.0)
                g = w * dw
                gg0 = jnp.dot(tri_fwd, _hi_lo_rows(g[:SB_BK]), preferred_element_type=F32)
                gg1 = jnp.dot(tri_fwd, _hi_lo_rows(g[SB_BK:]), preferred_element_type=F32)
                g_run1 = g_run + gg0[SB_BK:SB_BK + 1]
                g_pre = jnp.concatenate([gg0[:SB_BK] + g_run, gg1[:SB_BK] + g_run1], axis=0)
                dz = g - jnp.exp2(zs - p_raw) * g_pre
                if mask is not None:
                    dz = jnp.where(mask, dz, 0.0)
                dzbuf[slot] = dz.astype(BF16)
                wbuf[slot] = w.astype(BF16)
                return g_run1 + gg1[SB_BK:SB_BK + 1]

            issue(first_j, odd)
            dzbuf[...] = jnp.zeros(dzbuf.shape, BF16)
            wbuf[...] = jnp.zeros(wbuf.shape, BF16)

            def step(it, g_run):
                g_run = pair(4 * it, 0, g_run, None)
                return pair(4 * it + 2, 1, g_run, None)

            g_run = lax.cond(odd == 1, lambda g: pair(first_j, 1, g, None), lambda g: g, jnp.zeros((1, bq), F32))
            g_run = lax.fori_loop(i - pairs // 2, i, step, g_run)
            g_run = pair(last_j - 2, 0, g_run, _pair_mask(0, bq))
            pair(last_j, 1, g_run, _pair_mask(2, bq))
            retire(last_j, 1)
            dq_ref[:, lanes] = (dq_acc[...] * Q_SCALE).astype(dq_ref.dtype)
            return 0

        lax.fori_loop(0, s // bq, query_block, 0)
        dk_ref[...] = dk_acc[...].astype(dk_ref.dtype)
        dv_ref[...] = dv_acc[...].astype(dv_ref.dtype)
        wait_carried()

    def head_spec(offset, rows):
        return pl.BlockSpec((None, rows, s), lambda h: (h + offset, 0, 0))

    aspec = pl.BlockSpec((SB_BK + 8, 2 * SB_BK), lambda h: (0, 0))
    pair_f32 = pltpu.VMEM((2, 2 * SB_BK, bq), F32)
    pair_bf16 = pltpu.VMEM((2, 2 * SB_BK, bq), BF16)
    hbm = pl.BlockSpec(memory_space=pl.ANY)
    c_ins, c_outs, c_sems = (exchange.ins, exchange.out_shapes, exchange.sems) if exchange else ([], [], [])
    outs = pl.pallas_call(
        body, name=name, grid=(nh,),
        in_specs=[head_spec(0, dh), head_spec(nh, dh), head_spec(2 * nh, dh), head_spec(0, dh), head_spec(0, nkb),
                  aspec, aspec] + [hbm] * len(c_ins),
        out_specs=[pl.BlockSpec((3, None, dh, s), lambda h: (0, h, 0, 0))] + [hbm] * len(c_outs),
        out_shape=[jax.ShapeDtypeStruct((3, nh, dh, s), BF16)] + c_outs,
        scratch_shapes=[pltpu.VMEM((dh, bq), F32), pltpu.VMEM((dh, s), F32), pltpu.VMEM((dh, s), F32),
                        pair_f32, pair_f32, pair_bf16, pair_bf16] + c_sems,
        compiler_params=_params(("arbitrary",)),
    )(qkvt, qkvt, qkvt, dot_, rtab, _tri_rows(True), _tri_rows(False), *c_ins)
    return outs[0], outs[1:]


SWA_QB = 8


def _band_valid():
    kj = np.arange(2 * WINDOW)[:, None]
    dist = (np.arange(8 * WINDOW)[None, :] % WINDOW) + WINDOW - kj
    inside = (dist >= 0) & (dist < WINDOW)
    return jnp.asarray(np.stack([inside & (kj >= WINDOW), inside]), F32)


def _swa_probs(qt, kt, bias_t, valid, sink):
    sc = jnp.where(valid > 0.5, _contract0(kt, qt) + bias_t, NEG_INF)
    mx = jnp.maximum(jnp.max(sc, axis=0, keepdims=True), sink)
    p = jnp.exp(sc - mx)
    p_sink = jnp.exp(sink - mx)
    inv = 1.0 / (jnp.sum(p, axis=0, keepdims=True) + p_sink)
    return p, p_sink, inv


def _band(i):
    return pl.ds(pl.multiple_of(i * WINDOW, WINDOW), 2 * WINDOW)


def _heads_to_lanes(blk):
    return jnp.concatenate([blk[r * HEAD_DIM:(r + 1) * HEAD_DIM] for r in range(8)], axis=1)


def _lanes_to_heads(t):
    return jnp.concatenate([t[:, r * WINDOW:(r + 1) * WINDOW] for r in range(8)], axis=0)


def swa_fwd(name, qt, kpt, vpt, bias_t, sink_row):
    d, s = qt.shape
    ng, dh, sp = kpt.shape
    rows, cols = d // ng, SWA_QB * WINDOW
    assert (s // WINDOW) % SWA_QB == 0

    def body(q_ref, k_ref, v_ref, bias_ref, valid_ref, sink_ref, o_ref):
        for u in range(SWA_QB):
            i = pl.program_id(1) * SWA_QB + u
            lanes = slice(u * WINDOW, (u + 1) * WINDOW)
            qb = _heads_to_lanes(q_ref[:, lanes]) * Q_SCALE
            p, _, inv = _swa_probs(qb, k_ref[:, _band(i)], bias_ref[...], valid_ref[jnp.minimum(i, 1)], sink_ref[...])
            o_t = jnp.dot(v_ref[:, _band(i)], p.astype(BF16), preferred_element_type=F32) * inv
            o_ref[:, lanes] = _lanes_to_heads(o_t).astype(o_ref.dtype)

    qspec = pl.BlockSpec((rows, cols), lambda g, i: (g, i))
    kspec = pl.BlockSpec((None, dh, sp), lambda g, i: (g, 0, 0))
    return pl.pallas_call(
        body, name=name, grid=(ng, s // cols),
        in_specs=[qspec, kspec, kspec, pl.BlockSpec((None, 2 * WINDOW, 8 * WINDOW), lambda g, i: (g, 0, 0)),
                  pl.BlockSpec((2, 2 * WINDOW, 8 * WINDOW), lambda g, i: (0, 0, 0)),
                  pl.BlockSpec((None, 1, 8 * WINDOW), lambda g, i: (g, 0, 0))],
        out_specs=qspec,
        out_shape=jax.ShapeDtypeStruct(qt.shape, BF16),
        compiler_params=_params(("parallel", "arbitrary")),
    )(qt, kpt, vpt, bias_t, _band_valid(), sink_row)


def swa_bwd(name, qt, kpt, vpt, bias_t, sink_row, dot_, dk_in, dv_in):
    d, s = qt.shape
    ng, dh, sp = kpt.shape
    rows, cols = d // ng, SWA_QB * WINDOW

    def body(q_ref, k_ref, v_ref, bias_ref, valid_ref, sink_ref, do_ref, dki_ref, dvi_ref,
             dq_ref, dk_ref, dv_ref, db_ref, ds_ref):
        @pl.when(pl.program_id(1) == 0)
        def _():
            dk_ref[...] = dki_ref[...]
            dv_ref[...] = dvi_ref[...]
            db_ref[...] = jnp.zeros_like(db_ref)
            ds_ref[...] = jnp.zeros_like(ds_ref)

        for u in range(SWA_QB):
            i = pl.program_id(1) * SWA_QB + u
            band = _band(i)
            lanes = slice(u * WINDOW, (u + 1) * WINDOW)
            qb = _heads_to_lanes(q_ref[:, lanes]) * Q_SCALE
            dob = _heads_to_lanes(do_ref[:, lanes])
            kt = k_ref[:, band]
            p, p_sink, inv = _swa_probs(qb, kt, bias_ref[...], valid_ref[jnp.minimum(i, 1)], sink_ref[...])
            p = p * inv
            dp = _contract0(v_ref[:, band], dob)
            delta = jnp.sum(p * dp, axis=0, keepdims=True)
            dsc = p * (dp - delta)
            ds_ref[...] -= p_sink * inv * delta
            db_ref[...] += dsc
            dscb = dsc.astype(BF16)
            dq_t = jnp.dot(kt, dscb, preferred_element_type=F32) * Q_SCALE
            dq_ref[:, lanes] = _lanes_to_heads(dq_t).astype(dq_ref.dtype)
            dk_ref[:, band] += _contract1(qb, dscb)
            dv_ref[:, band] += _contract1(dob, p.astype(BF16))

    qspec = pl.BlockSpec((rows, cols), lambda g, i: (g, i))
    kspec = pl.BlockSpec((None, dh, sp), lambda g, i: (g, 0, 0))
    bspec = pl.BlockSpec((None, 2 * WINDOW, 8 * WINDOW), lambda g, i: (g, 0, 0))
    sspec = pl.BlockSpec((None, 1, 8 * WINDOW), lambda g, i: (g, 0, 0))
    return pl.pallas_call(
        body, name=name, grid=(ng, s // cols),
        in_specs=[qspec, kspec, kspec, bspec, pl.BlockSpec((2, 2 * WINDOW, 8 * WINDOW), lambda g, i: (0, 0, 0)), sspec,
                  qspec, kspec, kspec],
        out_specs=[qspec, kspec, kspec, bspec, sspec],
        out_shape=[jax.ShapeDtypeStruct(qt.shape, BF16), jax.ShapeDtypeStruct(kpt.shape, F32),
                   jax.ShapeDtypeStruct(kpt.shape, F32), jax.ShapeDtypeStruct(bias_t.shape, F32),
                   jax.ShapeDtypeStruct(sink_row.shape, F32)],
        compiler_params=_params(("parallel", "arbitrary")),
    )(qt, kpt, vpt, bias_t, _band_valid(), sink_row, dot_, dk_in, dv_in)


def _bucket_onehot():
    qi = np.arange(WINDOW)[:, None]
    kj = np.arange(2 * WINDOW)[None, :]
    n = np.maximum(qi + WINDOW - kj, 0)
    max_exact = N_BUCKETS // 2
    nf = np.maximum(n, 1).astype(np.float64)
    val = np.log(nf / max_exact) / math.log(WINDOW / max_exact) * (N_BUCKETS - max_exact)
    assert np.all(np.abs(val - np.round(val))[(n > max_exact) & (n < WINDOW)] > 1e-3)
    large = np.minimum(max_exact + val.astype(np.int64), N_BUCKETS - 1)
    bucket = np.where(n < max_exact, n, large).reshape(-1)
    onehot = np.zeros((128, bucket.size), np.float32)
    onehot[bucket, np.arange(bucket.size)] = 1.0
    return onehot


def _split3(x):
    a = x.astype(BF16)
    r = x - a.astype(F32)
    b = r.astype(BF16)
    c = (r - b.astype(F32)).astype(BF16)
    return a, b, c


def bias_table(rel_bias):
    nh = rel_bias.shape[1]
    oh = jnp.asarray(_bucket_onehot(), BF16)
    n = oh.shape[1]
    tn = 4096
    rb = jnp.zeros((nh, 128), F32).at[:, :N_BUCKETS].set(rel_bias.T)

    def body(rb_ref, oh_ref, o_ref):
        o_ref[...] = sum(jnp.dot(t, oh_ref[...], preferred_element_type=F32) for t in _split3(rb_ref[...]))

    return pl.pallas_call(
        body, name="bias_table", grid=(n // tn,),
        in_specs=[pl.BlockSpec((nh, 128), lambda i: (0, 0)), pl.BlockSpec((128, tn), lambda i: (0, i))],
        out_specs=pl.BlockSpec((nh, tn), lambda i: (0, i)),
        out_shape=jax.ShapeDtypeStruct((nh, n), F32),
        compiler_params=_params(("parallel",)),
    )(rb, oh)


def bias_table_grad(db0, db1):
    nh, n = db0.shape
    oh = jnp.asarray(_bucket_onehot(), BF16)
    tn = 4096

    def body(a_ref, b_ref, oh_ref, o_ref):
        @pl.when(pl.program_id(0) == 0)
        def _():
            o_ref[...] = jnp.zeros_like(o_ref)

        o_ref[...] += sum(lax.dot_general(t, oh_ref[...], (((1,), (1,)), ((), ())), preferred_element_type=F32)
                          for t in _split3(a_ref[...] + b_ref[...]))

    blk = pl.BlockSpec((nh, tn), lambda i: (0, i))
    return pl.pallas_call(
        body, name="bias_table_grad", grid=(n // tn,),
        in_specs=[blk, blk, pl.BlockSpec((128, tn), lambda i: (0, i))],
        out_specs=pl.BlockSpec((nh, 128), lambda i: (0, 0)),
        out_shape=jax.ShapeDtypeStruct((nh, 128), F32),
        compiler_params=_params(("arbitrary",)),
    )(db0, db1, oh)


def _owner_view(ref, name, d):
    if name == 'a_norm':
        return ref.at[d]
    if name in COL_SHARDED:
        n = ref.shape[2] // N_DEV
        return ref.at[:, :, pl.ds(pl.multiple_of(d * n, 128), n)]
    return ref.at[:, d]


def _place():
    return lax.axis_index("x"), lax.axis_index("y"), lax.axis_index("c")


def _dev(p):
    return 4 * p[0] + 2 * p[1] + p[2]


def _remote(src, dst, send_sem, recv_sem, to):
    return pltpu.make_async_remote_copy(src_ref=src, dst_ref=dst, send_sem=send_sem, recv_sem=recv_sem,
                                        device_id=to, device_id_type=MESH)


def _dma_sems(*shapes):
    return [pltpu.SemaphoreType.DMA(sh) for sh in shapes]


def comm_call(name, build, ins, out_shapes, sems):
    n_in, n_out = len(ins), len(out_shapes)

    def body(*refs):
        copies = build(refs[:n_in], refs[n_in:n_in + n_out], *refs[n_in + n_out:])
        for cp in copies:
            cp.start()
        for cp in copies:
            cp.wait()

    hbm = pl.BlockSpec(memory_space=pl.ANY)
    return pl.pallas_call(
        body, name=name, in_specs=[hbm] * n_in, out_specs=[hbm] * n_out, out_shape=list(out_shapes),
        scratch_shapes=sems,
    )(*ins)


def all_gather_weights(names, shards, full_shapes):
    n = len(names)

    def body(*refs):
        ins, outs = refs[:n], refs[n:2 * n]
        send_sems, recv_sems, local_sems = refs[2 * n:]
        x, y, c = _place()
        me, sibling = (x, y, c), (x, y, 1 - c)
        chips = [(1 - x, y), (x, 1 - y), (1 - x, 1 - y)]

        def copy(t, k, block, to, src=None):
            dst = _owner_view(outs[t], names[t], _dev(block))
            return _remote(dst if src is None else src, dst, send_sems.at[t, k], recv_sems.at[t, k], to)

        mine = [pltpu.make_async_copy(ins[t], _owner_view(outs[t], names[t], _dev(me)), local_sems.at[t])
                for t in range(n)]
        for cp in mine:
            cp.start()
        first = []
        for t in range(n):
            first.append(copy(t, 0, me, sibling, src=ins[t]))
            first += [copy(t, 1 + j, me, (*chip, c), src=ins[t]) for j, chip in enumerate(chips)]
        for cp in first:
            cp.start()
        passed = []
        for j, chip in enumerate(chips):
            for t in range(n):
                copy(t, 1 + j, (*chip, c), me).wait_recv()
                fwd = copy(t, 4 + j, (*chip, c), sibling)
                fwd.start()
                passed.append(fwd)
        for t in range(n):
            copy(t, 0, sibling, me).wait_recv()
            for j, chip in enumerate(chips):
                copy(t, 4 + j, (*chip, 1 - c), me).wait_recv()
        for cp in first + passed:
            cp.wait_send()
        for cp in mine:
            cp.wait()

    hbm = pl.BlockSpec(memory_space=pl.ANY)
    return pl.pallas_call(
        body, name="all_gather_layer0",
        in_specs=[hbm] * n, out_specs=[hbm] * n,
        out_shape=[jax.ShapeDtypeStruct(full_shapes[t], shards[t].dtype) for t in range(n)],
        scratch_shapes=_dma_sems((n, 7), (n, 7), (n,)),
    )(*shards)


def ag_direct(names, shards, full_shapes):
    n = len(names)

    def build(ins, outs, send_sems, recv_sems, local_sems, fwd_send_sems, fwd_recv_sems):
        x, y, c = _place()
        peers = [(x, y, 1 - c), (1 - x, y, c), (x, 1 - y, c), (1 - x, 1 - y, c)]
        copies = []
        for t in range(n):
            dst = _owner_view(outs[t], names[t], _dev((x, y, c)))
            copies.append(pltpu.make_async_copy(ins[t], dst, local_sems.at[t]))
            copies += [_remote(ins[t], dst, send_sems.at[t, k], recv_sems.at[t, k], to) for k, to in enumerate(peers)]
        return copies

    def forward(ins, outs, send_sems, recv_sems, local_sems, fwd_send_sems, fwd_recv_sems):
        x, y, c = _place()
        copies = []
        for t in range(n):
            for k, chip in enumerate([(1 - x, y), (x, 1 - y), (1 - x, 1 - y)]):
                view = _owner_view(outs[t], names[t], _dev((*chip, c)))
                copies.append(_remote(view, view, fwd_send_sems.at[t, k], fwd_recv_sems.at[t, k], (x, y, 1 - c)))
        return copies

    return Carry(build, shards, [jax.ShapeDtypeStruct(full_shapes[t], shards[t].dtype) for t in range(n)],
                 _dma_sems((n, 4), (n, 4), (n,), (n, 3), (n, 3)), then=forward)


def sibling_exchange(names, grads, part_shapes):
    n = len(names)

    def build(ins, outs, send_sems, recv_sems):
        x, y, c = _place()
        return [_remote(_owner_view(ins[t], names[t], 2 * q + 1 - c), outs[t].at[q], send_sems.at[t, q],
                        recv_sems.at[t, q], (x, y, 1 - c)) for t in range(n) for q in range(4)]

    return Carry(build, grads, [jax.ShapeDtypeStruct((4,) + part_shapes[t], BF16) for t in range(n)],
                 _dma_sems((n, 4), (n, 4)))


def chip_exchange(names, parts, part_shapes):
    n = len(names)

    def build(ins, outs, send_sems, recv_sems):
        x, y, c = _place()
        chips = [(1 - x, y), (x, 1 - y), (1 - x, 1 - y)]
        return [_remote(ins[t].at[2 * chip[0] + chip[1]], outs[t].at[k], send_sems.at[t, k], recv_sems.at[t, k],
                        (*chip, c)) for t in range(n) for k, chip in enumerate(chips)]

    return Carry(build, parts, [jax.ShapeDtypeStruct((3,) + part_shapes[t], BF16) for t in range(n)],
                 _dma_sems((n, 3), (n, 3)))


def all_gather_rows(x):
    r, w = x.shape

    def body(x_ref, out_ref, send_sems, recv_sems, local_sem):
        px, py, pc = _place()
        me = 4 * px + 2 * py + pc
        mine = pltpu.make_async_copy(x_ref, out_ref.at[me], local_sem)
        mine.start()
        copies = []
        for k in range(1, N_DEV):
            peer = (px ^ (k >> 2), py ^ ((k >> 1) & 1), pc ^ (k & 1))
            copies.append(pltpu.make_async_remote_copy(
                src_ref=x_ref, dst_ref=out_ref.at[me], send_sem=send_sems.at[k - 1], recv_sem=recv_sems.at[k - 1],
                device_id=peer, device_id_type=MESH))
        for cp in copies:
            cp.start()
        for k in range(1, N_DEV):
            peer_idx = me ^ k
            pltpu.make_async_remote_copy(
                src_ref=x_ref, dst_ref=out_ref.at[peer_idx], send_sem=send_sems.at[k - 1],
                recv_sem=recv_sems.at[k - 1], device_id=(px, py, pc), device_id_type=MESH).wait_recv()
        for cp in copies:
            cp.wait_send()
        mine.wait()

    vmem = pl.BlockSpec(memory_space=pltpu.VMEM)
    return pl.pallas_call(
        body, name="all_gather_small_grads",
        in_specs=[vmem], out_specs=vmem,
        out_shape=jax.ShapeDtypeStruct((N_DEV, r, w), x.dtype),
        scratch_shapes=[pltpu.SemaphoreType.DMA((N_DEV - 1,)), pltpu.SemaphoreType.DMA((N_DEV - 1,)),
                        pltpu.SemaphoreType.DMA],
    )(x)


def _adamw(w, g, m, v):
    m = ADAM_B1 * m + (1.0 - ADAM_B1) * g
    v = ADAM_B2 * v + (1.0 - ADAM_B2) * (g * g)
    m_hat = m / (1.0 - ADAM_B1 ** ADAM_STEP)
    v_hat = v / (1.0 - ADAM_B2 ** ADAM_STEP)
    return -ADAM_LR * (m_hat / (jnp.sqrt(v_hat) + ADAM_EPS) + ADAM_WD * w), m, v


def sibling_sum(name, col, grads, recv, core):
    _, nl, rows, cols = recv.shape
    tr = _tile(rows, 512)
    rspec = pl.BlockSpec((None, None, tr, cols), lambda q, l, i, c_ref: (q, l, i, 0))
    if col:
        gspec = pl.BlockSpec((None, tr, cols), lambda q, l, i, c_ref: (l, i, 2 * q + c_ref[0]))
    else:
        gspec = pl.BlockSpec((None, None, tr, cols), lambda q, l, i, c_ref: (l, 2 * q + c_ref[0], i, 0))

    def body(c_ref, g_ref, r_ref, o_ref):
        del c_ref
        o_ref[...] = (g_ref[...].astype(F32) + r_ref[...].astype(F32)).astype(BF16)

    return pl.pallas_call(
        body, name=name,
        grid_spec=pltpu.PrefetchScalarGridSpec(num_scalar_prefetch=1, grid=(4, nl, rows // tr),
                                               in_specs=[gspec, rspec], out_specs=rspec),
        out_shape=jax.ShapeDtypeStruct(recv.shape, BF16),
        compiler_params=_params(("parallel", "parallel", "parallel")),
    )(core.reshape(1), grads, recv)


def reduce_adamw(name, parts, recv, chip, w, m, v, l0, prev):
    _, nl, rows, cols = parts.shape
    tr = _tile(rows, 256)

    def body(q_ref, p_ref, r_ref, w_ref, m_ref, v_ref, *rest):
        del q_ref
        g_out, d_out, m_out, v_out = rest[-4:]
        g = ((p_ref[...].astype(F32) + r_ref[0].astype(F32)) + r_ref[1].astype(F32)) + r_ref[2].astype(F32)
        d, mn, vn = _adamw(w_ref[...], g, m_ref[...], v_ref[...])
        g_out[...] = g
        d_out[...] = d
        m_out[...] = mn
        v_out[...] = vn

    blk = pl.BlockSpec((None, tr, cols), lambda l, i, q_ref: (l0 + l, i, 0))
    prev = list(prev) if prev else []
    return pl.pallas_call(
        body, name=name,
        grid_spec=pltpu.PrefetchScalarGridSpec(
            num_scalar_prefetch=1, grid=(nl, rows // tr),
            in_specs=[pl.BlockSpec((None, None, tr, cols), lambda l, i, q_ref: (q_ref[0], l, i, 0)),
                      pl.BlockSpec((3, None, tr, cols), lambda l, i, q_ref: (0, l, i, 0)), blk, blk, blk]
            + [pl.BlockSpec(memory_space=pl.ANY)] * len(prev),
            out_specs=[blk] * 4),
        out_shape=[jax.ShapeDtypeStruct(w.shape, F32)] * 4,
        input_output_aliases={6 + i: i for i in range(len(prev))},
        compiler_params=_params(("parallel", "parallel")),
    )(chip.reshape(1), parts, recv, w, m, v, *prev)


def small_adamw(name, gathered, w, m, v):
    _, r, c = gathered.shape

    def body(ga_ref, w_ref, m_ref, v_ref, g_out, d_out, m_out, v_out):
        g = ga_ref[0]
        for d in range(1, N_DEV):
            g = g + ga_ref[d]
        dl, mn, vn = _adamw(w_ref[...], g, m_ref[...], v_ref[...])
        g_out[...] = g
        d_out[...] = dl
        m_out[...] = mn
        v_out[...] = vn

    return pl.pallas_call(
        body, name=name,
        out_shape=[jax.ShapeDtypeStruct((r, c), F32)] * 4,
        compiler_params=_params(),
    )(gathered, w, m, v)


def _rms(x, g):
    return x * lax.rsqrt(jnp.mean(x * x, axis=-1, keepdims=True) + EPS) * g


def _rms_bwd_epilogue(dn, x, dres, g):
    r = lax.rsqrt(jnp.mean(x * x, axis=-1, keepdims=True) + EPS)
    xh = x * r
    dyg = dn * g
    dx = dres + r * (dyg - xh * jnp.mean(dyg * xh, axis=-1, keepdims=True))
    return dx, dx, jnp.sum(dn * xh, axis=0, keepdims=True), jnp.sum(dx, axis=0, keepdims=True)


def _residual_then_norms(n_terms):
    def epilogue(acc, *ex):
        h = acc
        for t in ex[:n_terms]:
            h = h + t
        return (h,) + tuple(_rms(h, g) for g in ex[n_terms:])
    return epilogue


def local_step(x, target, small, ex):
    s, d = x.shape
    n_a, n_b = small['a_norm'].shape[0], small['b_norm'].shape[0]
    sg = {}
    gb = {}

    def fwd_mm(name, a, wname, layer, epilogue, extras, out_dtypes, **kw):
        return mm_nn(name, a, *ex.weight(wname, layer), epilogue, extras, out_dtypes, **kw)

    def dx_mm(name, dy, wname, layer, epilogue, extras, out_dtypes, **kw):
        return mm_nt(name, dy, *ex.weight(wname, layer), epilogue, extras, out_dtypes, **kw)

    def dw_mm(name, a, dy, wname, layer, **kw):
        key, slab, shape = ex.grad(wname, layer)
        gb[key] = mm_tn(name, a, dy, gb.get(key), shape, slab, **kw)

    plain = lambda acc: (acc,)
    plus_col = lambda acc, b: (acc + b,)

    bias_flat = bias_table(small['rel_bias'])
    bias_t = bias_flat.reshape(2, 8, WINDOW, 2 * WINDOW).transpose(0, 3, 1, 2).reshape(2, 2 * WINDOW, 8 * WINDOW)
    sink_rows = [jnp.repeat(small['b_sinks'][j], WINDOW).reshape(2, 1, 8 * WINDOW) for j in range(n_b)]

    gain = lambda g: g.reshape(1, -1)

    def mlp_fwd(h, n2, layer, next_gains):
        u, a = fwd_mm(f"mlp_up_fwd{layer}", n2, 'mlp_up', layer,
                      lambda acc: (acc, jnp.square(jnp.maximum(acc, 0.0))), (), (BF16, BF16))
        h2, *nexts = fwd_mm(f"mlp_down_fwd{layer}", a, 'mlp_down', layer, _residual_then_norms(1),
                            (h, *[gain(g) for g in next_gains]), (F32,) + (BF16,) * len(next_gains))
        return h2, nexts, (n2, u, a)

    h = x
    saved = []
    n1 = rms_fwd("a_norm_fwd0", h, small['a_norm'][0])
    for l in range(n_a):
        (qkvt,) = fwd_mm(f"a_qkv_fwd{l}", n1, 'a_wqkv', l, plain, (), (BF16,), out_t=True)
        qkvt = qkvt.reshape(3 * d // HEAD_DIM, HEAD_DIM, s)
        o_t, rtab, carried = sb_fwd(f"sb_fwd{l}", qkvt, ex.fwd_carry(l))
        ex.fwd_done(l, carried)
        o_t = o_t.reshape(d, s)
        h_mid, n2 = fwd_mm(f"a_wo_fwd{l}", o_t, 'a_wo', l, _residual_then_norms(1),
                           (h, gain(small['mlp_norm'][l])), (F32, BF16), a_t=True)
        next_gains = [small['a_norm'][l + 1]] if l + 1 < n_a else [small['b_norm'][0], small['kv_norm']]
        h_out, nexts, mlp_saved = mlp_fwd(h_mid, n2, l, next_gains)
        saved.append((h, n1, qkvt, o_t, rtab, h_mid, mlp_saved))
        h, n1 = h_out, nexts[0]
    h_kv, nkv = h, nexts[1]
    (kvt,) = fwd_mm("kv_fwd", nkv, 'w_kv', 0, plus_col, (small['b_kv'].reshape(-1, 1),), (BF16,), out_t=True)
    kvt = kvt.reshape(2, 2, HEAD_DIM, s)
    kpt, vpt = (jnp.pad(t, ((0, 0), (0, 0), (WINDOW, 0))) for t in (kvt[0], kvt[1]))
    for j in range(n_b):
        layer = n_a + j
        (qbt,) = fwd_mm(f"b_q_fwd{j}", n1, 'b_wq', j, plus_col, (small['b_bq'][j].reshape(-1, 1),), (BF16,),
                        out_t=True)
        o_t = swa_fwd(f"swa_fwd{j}", qbt, kpt, vpt, bias_t, sink_rows[j])
        h_mid, n2 = fwd_mm(f"b_wo_fwd{j}", o_t, 'b_wo', j, _residual_then_norms(2),
                           (h, gain(small['b_bo'][j]), gain(small['mlp_norm'][layer])), (F32, BF16), a_t=True)
        h_out, nexts, mlp_saved = mlp_fwd(h_mid, n2, layer, [small['b_norm'][j + 1]] if j + 1 < n_b else [])
        saved.append((h, n1, qbt, o_t, h_mid, mlp_saved))
        h, n1 = h_out, (nexts[0] if nexts else None)

    dh, dhb, dg_final, loss_b = loss_head(h, small['final_norm'], target)
    sg['final_norm'] = dg_final[0]
    sg['mlp_norm'] = [None] * (n_a + n_b)

    def mlp_bwd(dh, dhb, h_mid, mlp_saved, layer):
        n2, u, a = mlp_saved
        du, *carried = dx_mm(f"mlp_down_dx{layer}", dhb, 'mlp_down', layer,
                             lambda acc, uu: (acc * (2.0 * jnp.maximum(uu.astype(F32), 0.0)),), (u,), (BF16,),
                             exchange=ex.mlp_carry(layer, gb))
        ex.mlp_done(layer, carried)
        dw_mm(f"mlp_down_dw{layer}", a, dhb, 'mlp_down', layer)
        dh2, dh2b, dg, cs = dx_mm(f"mlp_up_dx{layer}", du, 'mlp_up', layer, _rms_bwd_epilogue,
                                  (h_mid, dh, gain(small['mlp_norm'][layer])), (F32, BF16), n_sums=2)
        dw_mm(f"mlp_up_dw{layer}", n2, du, 'mlp_up', layer)
        sg['mlp_norm'][layer] = dg[0]
        return dh2, dh2b, cs

    dkp = jnp.zeros(kpt.shape, F32)
    dvp = jnp.zeros(vpt.shape, F32)
    sg['b_norm'], sg['b_bq'], sg['b_bo'], sg['b_sinks'] = [None] * n_b, [None] * n_b, [None] * n_b, [None] * n_b
    dbias = [None] * n_b
    for j in reversed(range(n_b)):
        layer = n_a + j
        h_in, n1, qbt, o_t, h_mid, mlp_saved = saved[layer]
        dh, dhb, cs = mlp_bwd(dh, dhb, h_mid, mlp_saved, layer)
        sg['b_bo'][j] = cs[0]
        (do_t,) = dx_mm(f"b_wo_dx{j}", dhb, 'b_wo', j, plain, (), (BF16,), out_t=True)
        dw_mm(f"b_wo_dw{j}", o_t, dhb, 'b_wo', j, x_t=True)
        dq_t, dkp, dvp, dbias[j], dsink = swa_bwd(f"swa_bwd{j}", qbt, kpt, vpt, bias_t, sink_rows[j], do_t, dkp, dvp)
        sg['b_sinks'][j] = colsum(f"sink_grad{j}", dsink.reshape(16, WINDOW).T)[0]
        sg['b_bq'][j] = rowsum(f"b_bq_grad{j}", dq_t)
        dh, dhb, dg, _ = dx_mm(f"b_q_dx{j}", dq_t, 'b_wq', j, _rms_bwd_epilogue,
                               (h_in, dh, gain(small['b_norm'][j])), (F32, BF16), a_t=True, n_sums=2)
        dw_mm(f"b_q_dw{j}", n1, dq_t, 'b_wq', j, dy_t=True)
        sg['b_norm'][j] = dg[0]
    unt = lambda t: t.reshape(2, 2 * WINDOW, 8, WINDOW).transpose(0, 2, 3, 1).reshape(bias_flat.shape)
    sg['rel_bias'] = bias_table_grad(unt(dbias[0]), unt(dbias[1]))[:, :N_BUCKETS].T

    dkv_t = jnp.concatenate([dkp[:, :, WINDOW:], dvp[:, :, WINDOW:]], axis=0).reshape(-1, s)
    sg['b_kv'] = rowsum("b_kv_grad", dkv_t)
    dkvb = dkv_t.astype(BF16)
    dh, dhb, dg, _ = dx_mm("kv_dx", dkvb, 'w_kv', 0, _rms_bwd_epilogue, (h_kv, dh, gain(small['kv_norm'])),
                           (F32, BF16), a_t=True, n_sums=2)
    dw_mm("kv_dw", nkv, dkvb, 'w_kv', 0, dy_t=True)
    sg['kv_norm'] = dg[0]

    sg['a_norm'] = [None] * n_a
    for l in reversed(range(n_a)):
        h_in, n1, qkvt, o_t, rtab, h_mid, mlp_saved = saved[l]
        dh, dhb, _ = mlp_bwd(dh, dhb, h_mid, mlp_saved, l)
        (do_t,) = dx_mm(f"a_wo_dx{l}", dhb, 'a_wo', l, plain, (), (BF16,), out_t=True)
        dw_mm(f"a_wo_dw{l}", o_t, dhb, 'a_wo', l, x_t=True)
        dqkv_t, carried = sb_bwd(f"sb_bwd{l}", qkvt, do_t.reshape(d // HEAD_DIM, HEAD_DIM, s), rtab,
                                 ex.bwd_carry(l, gb))
        ex.bwd_done(l, carried)
        dqkv_t = dqkv_t.reshape(3 * d, s)
        dw_mm(f"a_qkv_dw{l}", n1, dqkv_t, 'a_wqkv', l, dy_t=True)
        dh, dhb, dg, _, *carried = dx_mm(f"a_qkv_dx{l}", dqkv_t, 'a_wqkv', l, _rms_bwd_epilogue,
                                         (h_in, dh, gain(small['a_norm'][l])), (F32, BF16), a_t=True, n_sums=2,
                                         exchange=ex.last_carry(gb) if l == 0 else None)
        if l == 0:
            ex.last_done(carried)
        sg['a_norm'][l] = dg[0]

    small_grads = {
        'a_norm': jnp.stack(sg['a_norm']), 'kv_norm': sg['kv_norm'], 'b_kv': sg['b_kv'],
        'b_norm': jnp.stack(sg['b_norm']), 'b_bq': jnp.stack(sg['b_bq']), 'b_sinks': jnp.stack(sg['b_sinks']),
        'b_bo': jnp.stack(sg['b_bo']), 'rel_bias': sg['rel_bias'], 'mlp_norm': jnp.stack(sg['mlp_norm']),
        'final_norm': sg['final_norm'],
    }
    return loss_b, dh, gb, small_grads


def _full_shape(name, shard_shape):
    if name in COL_SHARDED:
        return shard_shape[:2] + (N_DEV * shard_shape[2],)
    nl, r, n = shard_shape
    return (nl, N_DEV, r, n)


def _as_w3_shape(name, shard_shape):
    full = _full_shape(name, shard_shape)
    return full if name in COL_SHARDED else (full[0], full[1] * full[2], full[3])


def _as_w3(name, full):
    if name in COL_SHARDED:
        return full
    nl, nd, r, n = full.shape
    return full.reshape(nl, nd * r, n)


AG_GROUPS = {
    0: (('a_wqkv', 0, 1),),
    1: (('a_wo', 0, 2), ('mlp_up', 0, 2), ('mlp_down', 0, 2), ('a_wqkv', 1, 1)),
    2: (('mlp_up', 2, 2), ('mlp_down', 2, 2), ('b_wq', 0, 2), ('b_wo', 0, 2), ('w_kv', 0, 1)),
}
RS_GROUPS = {
    'A': (('mlp_up', 2, 2), ('mlp_down', 2, 2), ('b_wq', 0, 2), ('b_wo', 0, 2), ('w_kv', 0, 1)),
    'B1': (('a_wo', 1, 1), ('mlp_up', 1, 1), ('mlp_down', 1, 1)),
    'B2': (('a_wqkv', 1, 1),),
    'B3': (('a_wo', 0, 1), ('mlp_up', 0, 1), ('mlp_down', 0, 1)),
    'C': (('a_wqkv', 0, 1),),
}
UNDER_MLP = {1: 'A', 0: 'B2'}
SIBLING_UNDER_SB_BWD = {1: 'B1'}
UNDER_SB_BWD = {1: ('A',), 0: ('B1', 'B2', 'B3')}


class _Exchanges:
    def __init__(self, full0, shards, core, chip, w3, m3, v3):
        self.wbuf = {0: {n: _as_w3(n, full0[n]) for n, _, _ in AG_GROUPS[0]}}
        self.shards, self.core, self.chip = shards, core, chip
        self.w3, self.m3, self.v3 = w3, m3, v3
        self.shard_dims = {n: w3[n].shape[1:] for n in BIG}
        self.parts = {}
        self.gfull = {}
        self.out = {}

    def weight(self, name, layer):
        for group, members in AG_GROUPS.items():
            for n, l0, nl in members:
                if n == name and l0 <= layer < l0 + nl:
                    return self.wbuf[group][name], layer - l0
        raise KeyError((name, layer))

    def fwd_carry(self, layer):
        names = [n for n, _, _ in AG_GROUPS[layer + 1]]
        shards = [self.shards[layer + 1][n] for n in names]
        return ag_direct(names, shards, [_full_shape(n, sh.shape) for n, sh in zip(names, shards)])

    def fwd_done(self, layer, carried):
        names = [n for n, _, _ in AG_GROUPS[layer + 1]]
        self.wbuf[layer + 1] = {n: _as_w3(n, f) for n, f in zip(names, carried)}

    def grad(self, name, layer):
        for group, members in RS_GROUPS.items():
            for n, l0, nl in members:
                if n == name and l0 <= layer < l0 + nl:
                    return (group, name), layer - l0, _as_w3_shape(name, (nl,) + self.shard_dims[name])
        raise KeyError((name, layer))

    def _members(self, group):
        names = [n for n, _, _ in RS_GROUPS[group]]
        return names, [(nl,) + self.shard_dims[n] for n, _, nl in RS_GROUPS[group]]

    def _sibling_carry(self, group, gb):
        names, shapes = self._members(group)
        self.gfull[group] = [gb[(group, n)].reshape(_full_shape(n, sh)) for n, sh in zip(names, shapes)]
        return sibling_exchange(names, self.gfull[group], shapes)

    def _sibling_done(self, group, recv):
        names, _ = self._members(group)
        self.parts[group] = [sibling_sum(f"rs_sibling_sum_{group}_{n}", n in COL_SHARDED, g, r, self.core)
                             for n, g, r in zip(names, self.gfull[group], recv)]

    def _sibling_stage(self, group, gb):
        ce = self._sibling_carry(group, gb)
        self._sibling_done(group, comm_call(f"rs_sibling_exchange_{group}", ce.build, ce.ins, ce.out_shapes, ce.sems))

    def mlp_carry(self, layer, gb):
        return self._sibling_carry(UNDER_MLP[layer], gb) if layer in UNDER_MLP else None

    def mlp_done(self, layer, carried):
        if layer in UNDER_MLP:
            self._sibling_done(UNDER_MLP[layer], carried)

    def bwd_carry(self, layer, gb):
        names, parts, shapes = [], [], []
        for group in UNDER_SB_BWD[layer]:
            if group not in self.parts:
                self._sibling_stage(group, gb)
            names += self._members(group)[0]
            shapes += self._members(group)[1]
            parts += self.parts[group]
        exchange = chip_exchange(names, parts, shapes)
        if layer in SIBLING_UNDER_SB_BWD:
            exchange = _together(exchange, self._sibling_carry(SIBLING_UNDER_SB_BWD[layer], gb))
        return exchange

    def bwd_done(self, layer, carried):
        for group in UNDER_SB_BWD[layer]:
            n = len(RS_GROUPS[group])
            self._adamw(group, carried[:n])
            carried = carried[n:]
        if layer in SIBLING_UNDER_SB_BWD:
            self._sibling_done(SIBLING_UNDER_SB_BWD[layer], carried)

    def last_carry(self, gb):
        self._sibling_stage('C', gb)
        names, shapes = self._members('C')
        return chip_exchange(names, self.parts['C'], shapes)

    def last_done(self, carried):
        self._adamw('C', carried)

    def _adamw(self, group, recv2):
        for (n, l0, _), p, r in zip(RS_GROUPS[group], self.parts[group], recv2):
            self.out[n] = reduce_adamw(f"adamw_{group}_{n}", p, r, self.chip, self.w3[n], self.m3[n], self.v3[n],
                                       l0, self.out.get(n))


def _pack_small(vals):
    flat = jnp.concatenate([vals[n].reshape(-1).astype(F32) for n in SMALL] + [vals['loss'].reshape(-1)])
    rows = -(-flat.shape[0] // 1024) * 8
    return jnp.pad(flat, (0, rows * 128 - flat.shape[0])).reshape(rows, 128)


def _unpack_small(packed, shapes):
    flat = packed.reshape(-1)
    out, off = {}, 0
    for n in SMALL + ['loss']:
        size = int(np.prod(shapes[n]))
        out[n] = flat[off:off + size].reshape(shapes[n])
        off += size
    return out


def kernel(x, a_norm, a_wqkv, a_wo, kv_norm, w_kv, b_kv, b_norm, b_wq, b_bq, b_sinks, b_wo, b_bo, rel_bias, mlp_norm, mlp_up, mlp_down, final_norm, loss_target, m_a_norm, m_a_wqkv, m_a_wo, m_kv_norm, m_w_kv, m_b_kv, m_b_norm, m_b_wq, m_b_bq, m_b_sinks, m_b_wo, m_b_bo, m_rel_bias, m_mlp_norm, m_mlp_up, m_mlp_down, m_final_norm, v_a_norm, v_a_wqkv, v_a_wo, v_kv_norm, v_w_kv, v_b_kv, v_b_norm, v_b_wq, v_b_bq, v_b_sinks, v_b_wo, v_b_bo, v_rel_bias, v_mlp_norm, v_mlp_up, v_mlp_down, v_final_norm):
    w = dict(a_norm=a_norm, a_wqkv=a_wqkv, a_wo=a_wo, kv_norm=kv_norm, w_kv=w_kv, b_kv=b_kv, b_norm=b_norm,
             b_wq=b_wq, b_bq=b_bq, b_sinks=b_sinks, b_wo=b_wo, b_bo=b_bo, rel_bias=rel_bias, mlp_norm=mlp_norm,
             mlp_up=mlp_up, mlp_down=mlp_down, final_norm=final_norm)
    m = dict(a_norm=m_a_norm, a_wqkv=m_a_wqkv, a_wo=m_a_wo, kv_norm=m_kv_norm, w_kv=m_w_kv, b_kv=m_b_kv,
             b_norm=m_b_norm, b_wq=m_b_wq, b_bq=m_b_bq, b_sinks=m_b_sinks, b_wo=m_b_wo, b_bo=m_b_bo,
             rel_bias=m_rel_bias, mlp_norm=m_mlp_norm, mlp_up=m_mlp_up, mlp_down=m_mlp_down, final_norm=m_final_norm)
    v = dict(a_norm=v_a_norm, a_wqkv=v_a_wqkv, a_wo=v_a_wo, kv_norm=v_kv_norm, w_kv=v_w_kv, b_kv=v_b_kv,
             b_norm=v_b_norm, b_wq=v_b_wq, b_bq=v_b_bq, b_sinks=v_b_sinks, b_wo=v_b_wo, b_bo=v_b_bo,
             rel_bias=v_rel_bias, mlp_norm=v_mlp_norm, mlp_up=v_mlp_up, mlp_down=v_mlp_down, final_norm=v_final_norm)
    px, py, pc = _place()
    me = 4 * px + 2 * py + pc
    chip = (2 * px + py).astype(jnp.int32)
    core = pc.astype(jnp.int32)

    as3 = lambda t: t[None] if t.ndim == 2 else t
    w3, m3, v3 = ({n: as3(src[n]) for n in BIG} for src in (w, m, v))
    shards = {g: {n: w3[n][l0:l0 + nl].astype(BF16) for n, l0, nl in members} for g, members in AG_GROUPS.items()}
    an_pad = jnp.zeros((8, 128), F32).at[:a_norm.shape[0]].set(a_norm)
    names0 = [n for n, _, _ in AG_GROUPS[0]]
    full0 = all_gather_weights(names0 + ['a_norm'], [shards[0][n] for n in names0] + [an_pad],
                               [_full_shape(n, shards[0][n].shape) for n in names0] + [(N_DEV, 8, 128)])
    full0 = dict(zip(names0 + ['a_norm'], full0))
    n_a = a_norm.shape[0]
    small = {n: w[n] for n in SMALL}
    small['a_norm'] = full0['a_norm'][:, :n_a].transpose(1, 0, 2).reshape(n_a, -1)

    ex = _Exchanges(full0, shards, core, chip, w3, m3, v3)
    loss_b, grad_x, gb, sgrads = local_step(x[0], loss_target[0], small, ex)
    out = {n: [t.reshape(w[n].shape) for t in bufs] for n, bufs in ex.out.items()}

    sgrads['loss'] = loss_b[0, :1]
    gathered = all_gather_rows(_pack_small(sgrads))
    shapes = {n: w[n].shape for n in SMALL}
    shapes['a_norm'] = (n_a, a_norm.shape[1] * N_DEV)
    shapes['loss'] = (1,)
    zeros1 = jnp.zeros((1,), F32)

    def packed(src):
        vals = {n: src[n] for n in SMALL}
        vals['a_norm'] = jnp.zeros(shapes['a_norm'], F32)
        vals['loss'] = zeros1
        return _pack_small(vals)

    sm = small_adamw("adamw_small", gathered, packed(w), packed(m), packed(v))
    sm = [_unpack_small(t, shapes) for t in sm]
    g_an = lax.dynamic_slice_in_dim(sm[0]['a_norm'], me * a_norm.shape[1], a_norm.shape[1], axis=1)
    pad = lambda t: jnp.zeros((8, 128), F32).at[:n_a].set(t)
    gathered_an = jnp.zeros((N_DEV, 8, 128), F32).at[0].set(pad(g_an))
    an = small_adamw("adamw_a_norm", gathered_an, pad(a_norm), pad(m_a_norm), pad(v_a_norm))
    for i in range(4):
        sm[i]['a_norm'] = an[i][:n_a]
    for n in BIG:
        for i in range(4):
            sm[i][n] = out[n][i]
    loss = sm[0]['loss'][0]
    return (loss, grad_x[None], *[sm[0][n] for n in WEIGHTS], *[sm[1][n] for n in WEIGHTS],
            *[sm[2][n] for n in WEIGHTS], *[sm[3][n] for n in WEIGHTS])
```

```python
import math

import numpy as np
import jax
import jax.numpy as jnp
from jax import lax
from jax.experimental import pallas as pl
from jax.experimental.pallas import tpu as pltpu

F32 = jnp.float32
BF16 = jnp.bfloat16
MESH = pl.DeviceIdType.MESH

N_DEV = 8
HEAD_DIM = 64
WINDOW = 128
N_BUCKETS = 32
EPS = 1e-5
NEG_INF = -1e30
Q_SCALE = 1.0 / math.sqrt(HEAD_DIM)
LOG2E = 1.4426950408889634

ADAM_LR, ADAM_B1, ADAM_B2, ADAM_EPS, ADAM_WD, ADAM_STEP = 0.001, 0.9, 0.999, 1e-08, 0.01, 10

SB_BQ = 512
SB_BK = 128
SB_DEAD = 160.0
SB_UNSEEN = 1e30
ROW_TILE = 512
VMEM_LIMIT = 56 * 1024 * 1024

WEIGHTS = ['a_norm', 'a_wqkv', 'a_wo', 'kv_norm', 'w_kv', 'b_kv', 'b_norm', 'b_wq', 'b_bq', 'b_sinks', 'b_wo',
           'b_bo', 'rel_bias', 'mlp_norm', 'mlp_up', 'mlp_down', 'final_norm']
BIG = ['a_wqkv', 'a_wo', 'w_kv', 'b_wq', 'b_wo', 'mlp_up', 'mlp_down']
COL_SHARDED = ('a_wqkv', 'mlp_up')
SMALL = ['a_norm', 'kv_norm', 'b_kv', 'b_norm', 'b_bq', 'b_sinks', 'b_bo', 'rel_bias', 'mlp_norm', 'final_norm']


def _params(sem=None):
    return pltpu.CompilerParams(dimension_semantics=sem, vmem_limit_bytes=VMEM_LIMIT)


def _pick(n, cands):
    for c in cands:
        if n % c == 0:
            return c
    raise ValueError(n)


def _tile(n, want):
    return n if n <= want else _pick(n, (want, want // 2, want // 4))


MM_TILE_BUDGET = 36 * 1024 * 1024


def _row_tile(m, contraction, cols, streams):
    weight = 2 * contraction * cols * 2
    for rows in (2048, 1024, 512):
        if m % rows == 0 and weight + 2 * rows * (2 * contraction + cols * sum(streams)) <= MM_TILE_BUDGET:
            return rows
    return _tile(m, 512)


def mm_nn(name, a, w3, layer, epilogue, extras, out_dtypes, a_t=False, out_t=False):
    k, m = a.shape if a_t else a.shape[::-1]
    _, kw, n = w3.shape
    assert kw == k
    tn = _tile(n, 1024)
    tm = _row_tile(m, k, tn, [jnp.dtype(t).itemsize for t in out_dtypes]
                   + [e.dtype.itemsize for e in extras if e.size == m * n])
    ne, no = len(extras), len(out_dtypes)
    a_dim = 0 if a_t else 1

    def body(a_ref, w_ref, *rest):
        ex, outs = rest[:ne], rest[ne:ne + no]
        if out_t:
            acc = lax.dot_general(w_ref[...], a_ref[...], (((0,), (a_dim,)), ((), ())), preferred_element_type=F32)
        else:
            acc = lax.dot_general(a_ref[...], w_ref[...], (((a_dim,), (0,)), ((), ())), preferred_element_type=F32)
        for o, r in zip(outs, epilogue(acc, *[e[...] for e in ex])):
            o[...] = r.astype(o.dtype)

    if out_t:
        tile = pl.BlockSpec((tn, tm), lambda i, j: (j, i))
        vec = pl.BlockSpec((tn, 1), lambda i, j: (j, 0))
        out_shape = (n, m)
    else:
        tile = pl.BlockSpec((tm, tn), lambda i, j: (i, j))
        vec = pl.BlockSpec((1, tn), lambda i, j: (0, j))
        out_shape = (m, n)
    a_spec = pl.BlockSpec((k, tm), lambda i, j: (0, i)) if a_t else pl.BlockSpec((tm, k), lambda i, j: (i, 0))
    return pl.pallas_call(
        body, name=name, grid=(m // tm, n // tn),
        in_specs=[a_spec, pl.BlockSpec((None, k, tn), lambda i, j: (layer, 0, j))]
        + [tile if e.shape == out_shape else vec for e in extras],
        out_specs=[tile] * no,
        out_shape=[jax.ShapeDtypeStruct(out_shape, d) for d in out_dtypes],
        compiler_params=_params(("parallel", "parallel")),
    )(a, w3, *extras)


def mm_nt(name, dy, w3, layer, epilogue, extras, out_dtypes, a_t=False, out_t=False, n_sums=0, exchange=None):
    n, m = dy.shape if a_t else dy.shape[::-1]
    _, k, nw = w3.shape
    assert nw == n and not (out_t and n_sums)
    tko = _tile(k, 1024)
    tm = _row_tile(m, n, tko, [jnp.dtype(t).itemsize for t in out_dtypes]
                   + [e.dtype.itemsize for e in extras if e.size == m * k])
    ne, no = len(extras), len(out_dtypes)
    a_dim = 0 if a_t else 1

    def body(a_ref, w_ref, *rest):
        ex = rest[:ne]
        at = lambda step: (pl.program_id(0) == step[0]) & (pl.program_id(1) == step[1])
        results, _, start_carried, wait_carried = _carried(exchange, rest[ne:], no + n_sums, 0, at((0, 0)),
                                                           at((m // tm - 1, k // tko - 1)))
        outs, sums = results[:no], results[no:]
        start_carried()
        if out_t:
            acc = lax.dot_general(w_ref[...], a_ref[...], (((1,), (a_dim,)), ((), ())), preferred_element_type=F32)
        else:
            acc = lax.dot_general(a_ref[...], w_ref[...], (((a_dim,), (1,)), ((), ())), preferred_element_type=F32)
        res = epilogue(acc, *[e[...] for e in ex])
        for o, v in zip(outs, res):
            o[...] = v.astype(o.dtype)
        if n_sums:
            @pl.when(pl.program_id(0) == 0)
            def _():
                for o in sums:
                    o[...] = jnp.zeros_like(o)

            for o, v in zip(sums, res[no:]):
                o[...] += v
        wait_carried()

    if out_t:
        tile = pl.BlockSpec((tko, tm), lambda i, ko: (ko, i))
        out_shape = (k, m)
    else:
        tile = pl.BlockSpec((tm, tko), lambda i, ko: (i, ko))
        out_shape = (m, k)
    vec = pl.BlockSpec((1, tko), lambda i, ko: (0, ko))
    a_spec = pl.BlockSpec((n, tm), lambda i, ko: (0, i)) if a_t else pl.BlockSpec((tm, n), lambda i, ko: (i, 0))
    hbm = pl.BlockSpec(memory_space=pl.ANY)
    c_ins, c_outs, c_sems = (exchange.ins, exchange.out_shapes, exchange.sems) if exchange else ([], [], [])
    sequential = n_sums or exchange
    return pl.pallas_call(
        body, name=name, grid=(m // tm, k // tko),
        in_specs=[a_spec, pl.BlockSpec((None, tko, n), lambda i, ko: (layer, ko, 0))]
        + [tile if e.shape == out_shape else vec for e in extras] + [hbm] * len(c_ins),
        out_specs=[tile] * no + [vec] * n_sums + [hbm] * len(c_outs),
        out_shape=[jax.ShapeDtypeStruct(out_shape, d) for d in out_dtypes] + [jax.ShapeDtypeStruct((1, k), F32)] * n_sums
        + c_outs,
        scratch_shapes=c_sems,
        compiler_params=_params(("arbitrary" if sequential else "parallel", "arbitrary" if exchange else "parallel")),
    )(dy, w3, *extras, *c_ins)


def mm_tn(name, x, dy, gbuf, shape, layer, x_t=False, dy_t=False):
    k, s = x.shape if x_t else x.shape[::-1]
    _, kw, n = shape
    assert kw == k and dy.shape == ((n, s) if dy_t else (s, n))
    tkk = _tile(k, 512)
    tn = _tile(n, 1024)

    def body(x_ref, dy_ref, *rest):
        g_out = rest[-1]
        g_out[...] = lax.dot_general(x_ref[...], dy_ref[...], (((1 if x_t else 0,), (1 if dy_t else 0,)), ((), ())),
                                     preferred_element_type=F32).astype(g_out.dtype)

    prev = [] if gbuf is None else [gbuf]
    x_spec = pl.BlockSpec((tkk, s), lambda ki, j: (ki, 0)) if x_t else pl.BlockSpec((s, tkk), lambda ki, j: (0, ki))
    dy_spec = pl.BlockSpec((tn, s), lambda ki, j: (j, 0)) if dy_t else pl.BlockSpec((s, tn), lambda ki, j: (0, j))
    return pl.pallas_call(
        body, name=name, grid=(k // tkk, n // tn),
        in_specs=[x_spec, dy_spec] + [pl.BlockSpec(memory_space=pl.ANY)] * len(prev),
        out_specs=pl.BlockSpec((None, tkk, tn), lambda ki, j: (layer, ki, j)),
        out_shape=jax.ShapeDtypeStruct(shape, BF16),
        input_output_aliases={2: 0} if prev else {},
        compiler_params=_params(("parallel", "parallel")),
    )(x, dy, *prev)


def rms_fwd(name, h, g):
    s, d = h.shape
    tr = _pick(s, (ROW_TILE, 256, 128))

    def body(h_ref, g_ref, o_ref):
        x = h_ref[...]
        r = lax.rsqrt(jnp.mean(x * x, axis=-1, keepdims=True) + EPS)
        o_ref[...] = (x * r * g_ref[...]).astype(o_ref.dtype)

    return pl.pallas_call(
        body, name=name, grid=(s // tr,),
        in_specs=[pl.BlockSpec((tr, d), lambda i: (i, 0)), pl.BlockSpec((1, d), lambda i: (0, 0))],
        out_specs=pl.BlockSpec((tr, d), lambda i: (i, 0)),
        out_shape=jax.ShapeDtypeStruct((s, d), BF16),
        compiler_params=_params(("parallel",)),
    )(h, g.reshape(1, d))


def loss_head(h, g, target):
    s, d = h.shape
    tr = _pick(s, (ROW_TILE, 256, 128))

    def body(h_ref, g_ref, t_ref, dx_ref, dxb_ref, dg_ref, loss_ref):
        i = pl.program_id(0)
        x = h_ref[...]
        r = lax.rsqrt(jnp.mean(x * x, axis=-1, keepdims=True) + EPS)
        xh = x * r
        gw = g_ref[...]
        err = xh * gw - t_ref[...]
        dn_ = err * (1.0 / d)
        dyg = dn_ * gw
        dx = r * (dyg - xh * jnp.mean(dyg * xh, axis=-1, keepdims=True))
        dx_ref[...] = dx
        dxb_ref[...] = dx.astype(BF16)

        @pl.when(i == 0)
        def _():
            dg_ref[...] = jnp.zeros_like(dg_ref)
            loss_ref[...] = jnp.zeros_like(loss_ref)

        dg_ref[...] += jnp.sum(dn_ * xh, axis=0, keepdims=True)
        per_row = jnp.sum(err * err, axis=-1, keepdims=True) * (0.5 / d)
        loss_ref[...] += jnp.broadcast_to(jnp.sum(per_row, axis=0, keepdims=True), loss_ref.shape)

    row = pl.BlockSpec((tr, d), lambda i: (i, 0))
    vec = pl.BlockSpec((1, d), lambda i: (0, 0))
    return pl.pallas_call(
        body, name="loss_head", grid=(s // tr,),
        in_specs=[row, vec, row],
        out_specs=[row, row, vec, pl.BlockSpec((1, 128), lambda i: (0, 0))],
        out_shape=[jax.ShapeDtypeStruct((s, d), F32), jax.ShapeDtypeStruct((s, d), BF16),
                   jax.ShapeDtypeStruct((1, d), F32), jax.ShapeDtypeStruct((1, 128), F32)],
        compiler_params=_params(("arbitrary",)),
    )(h, g.reshape(1, d), target)


def colsum(name, x):
    s, n = x.shape
    tr = _pick(s, (ROW_TILE, 256, 128))

    def body(x_ref, o_ref):
        @pl.when(pl.program_id(0) == 0)
        def _():
            o_ref[...] = jnp.zeros_like(o_ref)

        o_ref[...] += jnp.sum(x_ref[...].astype(F32), axis=0, keepdims=True)

    return pl.pallas_call(
        body, name=name, grid=(s // tr,),
        in_specs=[pl.BlockSpec((tr, n), lambda i: (i, 0))],
        out_specs=pl.BlockSpec((1, n), lambda i: (0, 0)),
        out_shape=jax.ShapeDtypeStruct((1, n), F32),
        compiler_params=_params(("arbitrary",)),
    )(x)


def rowsum(name, x):
    n, s = x.shape
    ts = _pick(s, (1024, 512, 256, 128))

    def body(x_ref, o_ref):
        @pl.when(pl.program_id(0) == 0)
        def _():
            o_ref[...] = jnp.zeros_like(o_ref)

        o_ref[...] += jnp.sum(x_ref[...].astype(F32), axis=1, keepdims=True)

    return pl.pallas_call(
        body, name=name, grid=(s // ts,),
        in_specs=[pl.BlockSpec((n, ts), lambda i: (0, i))],
        out_specs=pl.BlockSpec((n, 1), lambda i: (0, 0)),
        out_shape=jax.ShapeDtypeStruct((n, 1), F32),
        compiler_params=_params(("arbitrary",)),
    )(x)[:, 0]


def _tri_rows(reverse):
    i = np.arange(SB_BK)
    tri = (i[None, :] >= i[:, None]) if reverse else (i[None, :] <= i[:, None])
    tri = np.concatenate([tri, tri], axis=1)
    return jnp.asarray(np.concatenate([tri, np.ones((8, 2 * SB_BK), bool)], axis=0), BF16)


def _hi_lo_rows(x):
    hi = x.astype(BF16)
    lo = (x - hi.astype(F32)).astype(BF16)
    return jnp.concatenate([hi, lo], axis=0)


def _softplus2(zs):
    neg_abs = lax.bitcast_convert_type(lax.bitcast_convert_type(zs, jnp.uint32) | jnp.uint32(0x80000000), F32)
    return jnp.maximum(zs, 0.0) + jnp.log2(1.0 + jnp.exp2(neg_abs))


def _pair_mask(first_rel_block, bq):
    key = lax.broadcasted_iota(jnp.int32, (2 * SB_BK, bq), 0) + first_rel_block * SB_BK
    qry = lax.broadcasted_iota(jnp.int32, (2 * SB_BK, bq), 1)
    return key < qry


def _row_of(table8, sub8, r):
    return jnp.sum(jnp.where(sub8 == r, table8, 0.0), axis=0, keepdims=True)


def _keys(j0):
    return pl.ds(pl.multiple_of(j0 * SB_BK, 2 * SB_BK), 2 * SB_BK)


class Carry:
    def __init__(self, build, ins, out_shapes, sems, then=None):
        self.build, self.ins, self.out_shapes, self.sems = build, list(ins), list(out_shapes), list(sems)
        self.then = then


def _together(a, b):
    assert a.then is None and b.then is None
    ni, no, ns = len(a.ins), len(a.out_shapes), len(a.sems)

    def build(ins, outs, *sems):
        return a.build(ins[:ni], outs[:no], *sems[:ns]) + b.build(ins[ni:], outs[no:], *sems[ns:])

    return Carry(build, a.ins + b.ins, a.out_shapes + b.out_shapes, a.sems + b.sems)


def _carried(carry, rest, n_out, n_scratch, first, last):
    n_ci = len(carry.ins) if carry else 0
    n_co = len(carry.out_shapes) if carry else 0
    cin, outs = rest[:n_ci], rest[n_ci:n_ci + n_out]
    cout = rest[n_ci + n_out:n_ci + n_out + n_co]
    scratch = rest[n_ci + n_out + n_co:n_ci + n_out + n_co + n_scratch]
    csems = rest[n_ci + n_out + n_co + n_scratch:]

    def start():
        if carry:
            @pl.when(first)
            def _():
                for cp in carry.build(cin, cout, *csems):
                    cp.start()

    def wait():
        if carry:
            @pl.when(last)
            def _():
                for cp in carry.build(cin, cout, *csems):
                    cp.wait()
                if carry.then:
                    second = carry.then(cin, cout, *csems)
                    for cp in second:
                        cp.start()
                    for cp in second:
                        cp.wait()

    return outs, scratch, start, wait


def _contract0(a, b):
    return lax.dot_general(a, b, (((0,), (0,)), ((), ())), preferred_element_type=F32)


def _contract1(a, b):
    return lax.dot_general(a, b, (((1,), (1,)), ((), ())), preferred_element_type=F32)


def sb_fwd(name, qkvt, exchange=None):
    nh, dh, s = qkvt.shape[0] // 3, qkvt.shape[1], qkvt.shape[2]
    bq = SB_BQ
    per_q = bq // SB_BK
    nkb = s // SB_BK
    assert s % bq == 0 and per_q == 4 and nkb % 8 == 0

    def body(q_ref, k_ref, v_ref, a_ref, *rest):
        head = pl.program_id(0)
        (o_ref, rtab_ref), (acc, zbuf, wbuf), start_carried, wait_carried = _carried(
            exchange, rest, 2, 3, head == 0, head == nh - 1)
        start_carried()
        tri = a_ref[...]
        sub8 = lax.broadcasted_iota(jnp.int32, (8, bq), 0)
        rtab_ref[...] = jnp.full(rtab_ref.shape, SB_UNSEEN, F32)
        kf = k_ref[...].astype(F32)
        k_max2 = jnp.max(jnp.sum(kf * kf, axis=0, keepdims=True), axis=1, keepdims=True)

        def query_block(i, _):
            lanes = pl.ds(pl.multiple_of(i * bq, bq), bq)
            qb = q_ref[:, lanes] * Q_SCALE
            acc[...] = jnp.zeros_like(acc)
            qf = qb.astype(F32)
            bound = jnp.sqrt(jnp.sum(qf * qf, axis=0, keepdims=True) * k_max2) * (1.001 * LOG2E)

            def scores(j0):
                return _contract0(k_ref[:, _keys(j0)], qb) * LOG2E

            def pair(j0, slot, run, rt8, mask, has_prev):
                zs = zbuf[slot]
                zbuf[1 - slot] = scores(jnp.maximum(j0 - 2, 0))
                if has_prev:
                    acc[...] += jnp.dot(v_ref[:, _keys(j0 + 2)], wbuf[1 - slot], preferred_element_type=F32)
                p = _softplus2(zs)
                if mask is not None:
                    p = jnp.where(mask, p, 0.0)
                cr1 = jnp.dot(tri, _hi_lo_rows(p[SB_BK:]), preferred_element_type=F32)
                cr0 = jnp.dot(tri, _hi_lo_rows(p[:SB_BK]), preferred_element_type=F32)
                run1 = run + cr1[SB_BK:SB_BK + 1]
                w = jnp.exp2(jnp.concatenate([zs[:SB_BK] - cr0[:SB_BK] - run1, zs[SB_BK:] - cr1[:SB_BK] - run],
                                             axis=0))
                if mask is not None:
                    w = jnp.where(mask, w, 0.0)
                wbuf[slot] = w.astype(BF16)
                rt8 = jnp.where(j0 % 8 == 6, SB_UNSEEN, rt8)
                rt8 = jnp.where(sub8 == (j0 + 1) % 8, run, jnp.where(sub8 == j0 % 8, run1, rt8))
                rtab_ref[pl.ds(pl.multiple_of((j0 // 8) * 8, 8), 8), lanes] = rt8
                return run1 + cr0[SB_BK:SB_BK + 1], rt8

            def alive(run):
                return jnp.min(run - bound) < SB_DEAD

            top = i * per_q
            zbuf[0] = scores(top + 2)
            state = (jnp.zeros((1, bq), F32), jnp.full((8, bq), SB_UNSEEN, F32))
            state = pair(top + 2, 0, *state, _pair_mask(2, bq), False)
            state = pair(top, 1, *state, _pair_mask(0, bq), True)

            def step(c):
                it, pairs, _, run, rt8 = c
                j0 = top - 2 - 4 * it
                run, rt8 = pair(j0, 0, run, rt8, None, True)
                go = alive(run)
                run, rt8 = lax.cond(go, lambda r, t: pair(j0 - 2, 1, r, t, None, True), lambda r, t: (r, t), run, rt8)
                return it + 1, pairs + 1 + go.astype(jnp.int32), go & alive(run), run, rt8

            pairs = lax.while_loop(lambda c: (c[0] < i) & c[2], step, (0, 0, alive(state[0]), *state))[1]
            acc[...] += jnp.dot(v_ref[:, _keys(top - 2 * pairs)], wbuf[(pairs + 1) % 2], preferred_element_type=F32)
            o_ref[:, lanes] = acc[...].astype(o_ref.dtype)
            return 0

        lax.fori_loop(0, s // bq, query_block, 0)
        wait_carried()

    def head_spec(offset, rows):
        return pl.BlockSpec((None, rows, s), lambda h: (h + offset, 0, 0))

    hbm = pl.BlockSpec(memory_space=pl.ANY)
    c_ins, c_outs, c_sems = (exchange.ins, exchange.out_shapes, exchange.sems) if exchange else ([], [], [])
    outs = pl.pallas_call(
        body, name=name, grid=(nh,),
        in_specs=[head_spec(0, dh), head_spec(nh, dh), head_spec(2 * nh, dh),
                  pl.BlockSpec((SB_BK + 8, 2 * SB_BK), lambda h: (0, 0))] + [hbm] * len(c_ins),
        out_specs=[head_spec(0, dh), head_spec(0, nkb)] + [hbm] * len(c_outs),
        out_shape=[jax.ShapeDtypeStruct((nh, dh, s), BF16), jax.ShapeDtypeStruct((nh, nkb, s), F32)] + c_outs,
        scratch_shapes=[pltpu.VMEM((dh, bq), F32), pltpu.VMEM((2, 2 * SB_BK, bq), F32),
                        pltpu.VMEM((2, 2 * SB_BK, bq), BF16)] + c_sems,
        compiler_params=_params(("arbitrary",)),
    )(qkvt, qkvt, qkvt, _tri_rows(True), *c_ins)
    return outs[0], outs[1], outs[2:]


def sb_bwd(name, qkvt, dot_, rtab, exchange=None):
    nh, dh, s = qkvt.shape[0] // 3, qkvt.shape[1], qkvt.shape[2]
    bq = SB_BQ
    per_q = bq // SB_BK
    nkb = s // SB_BK

    def body(qt_ref, kt_ref, vt_ref, dot_ref, rtab_ref, ar_ref, af_ref, *rest):
        head = pl.program_id(0)
        (dqkv_ref,), (dq_acc, dk_acc, dv_acc, zbuf, dwbuf, dzbuf, wbuf), start_carried, wait_carried = \
            _carried(exchange, rest, 1, 7, head == 0, head == nh - 1)
        dq_ref, dk_ref, dv_ref = dqkv_ref.at[0], dqkv_ref.at[1], dqkv_ref.at[2]
        start_carried()
        dk_acc[...] = jnp.zeros_like(dk_acc)
        dv_acc[...] = jnp.zeros_like(dv_acc)
        tri_rev = ar_ref[...][:SB_BK]
        tri_fwd = af_ref[...]
        sub8 = lax.broadcasted_iota(jnp.int32, (8, bq), 0)

        def query_block(i, _):
            lanes = pl.ds(pl.multiple_of(i * bq, bq), bq)
            qtb = qt_ref[:, lanes] * Q_SCALE
            dotb = dot_ref[:, lanes]
            dq_acc[...] = jnp.zeros_like(dq_acc)
            last_j = i * per_q + 2
            seen = jnp.max(jnp.where(rtab_ref[:, lanes] < 0.1 * SB_UNSEEN, 1.0, 0.0), axis=1, keepdims=True)
            pairs = jnp.clip((jnp.sum(seen).astype(jnp.int32) - per_q) // 2, 0, 2 * i)
            odd = pairs % 2
            first_j = i * per_q - 2 * pairs

            def issue(j0, slot):
                zbuf[slot] = _contract0(kt_ref[:, _keys(j0)], qtb) * LOG2E
                dwbuf[slot] = _contract0(vt_ref[:, _keys(j0)], dotb)

            def retire(j0, slot):
                keys = _keys(j0)
                dq_acc[...] += jnp.dot(kt_ref[:, keys], dzbuf[slot], preferred_element_type=F32)
                dk_acc[:, keys] += _contract1(qtb, dzbuf[slot])
                dv_acc[:, keys] += _contract1(dotb, wbuf[slot])

            def pair(j0, slot, g_run, mask):
                zs = zbuf[slot]
                dw = dwbuf[slot]
                issue(jnp.minimum(j0 + 2, last_j), 1 - slot)
                retire(jnp.maximum(j0 - 2, first_j), 1 - slot)
                p_raw = _softplus2(zs)
                p = p_raw if mask is None else jnp.where(mask, p_raw, 0.0)
                c0 = jnp.dot(tri_rev, _hi_lo_rows(p[:SB_BK]), preferred_element_type=F32)
                c1 = jnp.dot(tri_rev, _hi_lo_rows(p[SB_BK:]), preferred_element_type=F32)
                rt8 = rtab_ref[pl.ds(pl.multiple_of((j0 // 8) * 8, 8), 8), lanes]
                r0 = _row_of(rt8, sub8, j0 % 8)
                r1 = _row_of(rt8, sub8, (j0 + 1) % 8)
                w = jnp.exp2(jnp.concatenate([zs[:SB_BK] - c0 - r0, zs[SB_BK:] - c1 - r1], axis=0))
                if mask is not None:
                    w = jnp.where(mask, w, 0.0)
                g = w * dw
                gg0 = jnp.dot(tri_fwd, _hi_lo_rows(g[:SB_BK]), preferred_element_type=F32)
                gg1 = jnp.dot(tri_fwd, _hi_lo_rows(g[SB_BK:]), preferred_element_type=F32)
                g_run1 = g_run + gg0[SB_BK:SB_BK + 1]
                g_pre = jnp.concatenate([gg0[:SB_BK] + g_run, gg1[:SB_BK] + g_run1], axis=0)
                dz = g - jnp.exp2(zs - p_raw) * g_pre
                if mask is not None:
                    dz = jnp.where(mask, dz, 0.0)
                dzbuf[slot] = dz.astype(BF16)
                wbuf[slot] = w.astype(BF16)
                return g_run1 + gg1[SB_BK:SB_BK + 1]

            issue(first_j, odd)
            dzbuf[...] = jnp.zeros(dzbuf.shape, BF16)
            wbuf[...] = jnp.zeros(wbuf.shape, BF16)

            def step(it, g_run):
                g_run = pair(4 * it, 0, g_run, None)
                return pair(4 * it + 2, 1, g_run, None)

            g_run = lax.cond(odd == 1, lambda g: pair(first_j, 1, g, None), lambda g: g, jnp.zeros((1, bq), F32))
            g_run = lax.fori_loop(i - pairs // 2, i, step, g_run)
            g_run = pair(last_j - 2, 0, g_run, _pair_mask(0, bq))
            pair(last_j, 1, g_run, _pair_mask(2, bq))
            retire(last_j, 1)
            dq_ref[:, lanes] = (dq_acc[...] * Q_SCALE).astype(dq_ref.dtype)
            return 0

        lax.fori_loop(0, s // bq, query_block, 0)
        dk_ref[...] = dk_acc[...].astype(dk_ref.dtype)
        dv_ref[...] = dv_acc[...].astype(dv_ref.dtype)
        wait_carried()

    def head_spec(offset, rows):
        return pl.BlockSpec((None, rows, s), lambda h: (h + offset, 0, 0))

    aspec = pl.BlockSpec((SB_BK + 8, 2 * SB_BK), lambda h: (0, 0))
    pair_f32 = pltpu.VMEM((2, 2 * SB_BK, bq), F32)
    pair_bf16 = pltpu.VMEM((2, 2 * SB_BK, bq), BF16)
    hbm = pl.BlockSpec(memory_space=pl.ANY)
    c_ins, c_outs, c_sems = (exchange.ins, exchange.out_shapes, exchange.sems) if exchange else ([], [], [])
    outs = pl.pallas_call(
        body, name=name, grid=(nh,),
        in_specs=[head_spec(0, dh), head_spec(nh, dh), head_spec(2 * nh, dh), head_spec(0, dh), head_spec(0, nkb),
                  aspec, aspec] + [hbm] * len(c_ins),
        out_specs=[pl.BlockSpec((3, None, dh, s), lambda h: (0, h, 0, 0))] + [hbm] * len(c_outs),
        out_shape=[jax.ShapeDtypeStruct((3, nh, dh, s), BF16)] + c_outs,
        scratch_shapes=[pltpu.VMEM((dh, bq), F32), pltpu.VMEM((dh, s), F32), pltpu.VMEM((dh, s), F32),
                        pair_f32, pair_f32, pair_bf16, pair_bf16] + c_sems,
        compiler_params=_params(("arbitrary",)),
    )(qkvt, qkvt, qkvt, dot_, rtab, _tri_rows(True), _tri_rows(False), *c_ins)
    return outs[0], outs[1:]


SWA_QB = 4


def _band_valid():
    kj = np.arange(2 * WINDOW)[:, None]
    dist = (np.arange(8 * WINDOW)[None, :] % WINDOW) + WINDOW - kj
    inside = (dist >= 0) & (dist < WINDOW)
    return jnp.asarray(np.stack([inside & (kj >= WINDOW), inside]), F32)


def _swa_probs(qt, kt, bias_t, valid, sink):
    sc = jnp.where(valid > 0.5, _contract0(kt, qt) + bias_t, NEG_INF)
    mx = jnp.maximum(jnp.max(sc, axis=0, keepdims=True), sink)
    p = jnp.exp(sc - mx)
    p_sink = jnp.exp(sink - mx)
    inv = 1.0 / (jnp.sum(p, axis=0, keepdims=True) + p_sink)
    return p, p_sink, inv


def _band(i):
    return pl.ds(pl.multiple_of(i * WINDOW, WINDOW), 2 * WINDOW)


def _heads_to_lanes(blk):
    return jnp.concatenate([blk[r * HEAD_DIM:(r + 1) * HEAD_DIM] for r in range(8)], axis=1)


def _lanes_to_heads(t):
    return jnp.concatenate([t[:, r * WINDOW:(r + 1) * WINDOW] for r in range(8)], axis=0)


def swa_fwd(name, qt, kpt, vpt, bias_t, sink_row):
    d, s = qt.shape
    ng, dh, sp = kpt.shape
    rows, cols = d // ng, SWA_QB * WINDOW
    assert (s // WINDOW) % SWA_QB == 0

    def body(q_ref, k_ref, v_ref, bias_ref, valid_ref, sink_ref, o_ref):
        for u in range(SWA_QB):
            i = pl.program_id(1) * SWA_QB + u
            lanes = slice(u * WINDOW, (u + 1) * WINDOW)
            qb = _heads_to_lanes(q_ref[:, lanes]) * Q_SCALE
            p, _, inv = _swa_probs(qb, k_ref[:, _band(i)], bias_ref[...], valid_ref[jnp.minimum(i, 1)], sink_ref[...])
            o_t = jnp.dot(v_ref[:, _band(i)], p.astype(BF16), preferred_element_type=F32) * inv
            o_ref[:, lanes] = _lanes_to_heads(o_t).astype(o_ref.dtype)

    qspec = pl.BlockSpec((rows, cols), lambda g, i: (g, i))
    kspec = pl.BlockSpec((None, dh, sp), lambda g, i: (g, 0, 0))
    return pl.pallas_call(
        body, name=name, grid=(ng, s // cols),
        in_specs=[qspec, kspec, kspec, pl.BlockSpec((None, 2 * WINDOW, 8 * WINDOW), lambda g, i: (g, 0, 0)),
                  pl.BlockSpec((2, 2 * WINDOW, 8 * WINDOW), lambda g, i: (0, 0, 0)),
                  pl.BlockSpec((None, 1, 8 * WINDOW), lambda g, i: (g, 0, 0))],
        out_specs=qspec,
        out_shape=jax.ShapeDtypeStruct(qt.shape, BF16),
        compiler_params=_params(("parallel", "arbitrary")),
    )(qt, kpt, vpt, bias_t, _band_valid(), sink_row)


def swa_bwd(name, qt, kpt, vpt, bias_t, sink_row, dot_, dk_in, dv_in):
    d, s = qt.shape
    ng, dh, sp = kpt.shape
    rows, cols = d // ng, SWA_QB * WINDOW

    def body(q_ref, k_ref, v_ref, bias_ref, valid_ref, sink_ref, do_ref, dki_ref, dvi_ref,
             dq_ref, dk_ref, dv_ref, db_ref, ds_ref):
        @pl.when(pl.program_id(1) == 0)
        def _():
            dk_ref[...] = dki_ref[...]
            dv_ref[...] = dvi_ref[...]
            db_ref[...] = jnp.zeros_like(db_ref)
            ds_ref[...] = jnp.zeros_like(ds_ref)

        for u in range(SWA_QB):
            i = pl.program_id(1) * SWA_QB + u
            band = _band(i)
            lanes = slice(u * WINDOW, (u + 1) * WINDOW)
            qb = _heads_to_lanes(q_ref[:, lanes]) * Q_SCALE
            dob = _heads_to_lanes(do_ref[:, lanes])
            kt = k_ref[:, band]
            p, p_sink, inv = _swa_probs(qb, kt, bias_ref[...], valid_ref[jnp.minimum(i, 1)], sink_ref[...])
            p = p * inv
            dp = _contract0(v_ref[:, band], dob)
            delta = jnp.sum(p * dp, axis=0, keepdims=True)
            dsc = p * (dp - delta)
            ds_ref[...] -= p_sink * inv * delta
            db_ref[...] += dsc
            dscb = dsc.astype(BF16)
            dq_t = jnp.dot(kt, dscb, preferred_element_type=F32) * Q_SCALE
            dq_ref[:, lanes] = _lanes_to_heads(dq_t).astype(dq_ref.dtype)
            dk_ref[:, band] += _contract1(qb, dscb)
            dv_ref[:, band] += _contract1(dob, p.astype(BF16))

    qspec = pl.BlockSpec((rows, cols), lambda g, i: (g, i))
    kspec = pl.BlockSpec((None, dh, sp), lambda g, i: (g, 0, 0))
    bspec = pl.BlockSpec((None, 2 * WINDOW, 8 * WINDOW), lambda g, i: (g, 0, 0))
    sspec = pl.BlockSpec((None, 1, 8 * WINDOW), lambda g, i: (g, 0, 0))
    return pl.pallas_call(
        body, name=name, grid=(ng, s // cols),
        in_specs=[qspec, kspec, kspec, bspec, pl.BlockSpec((2, 2 * WINDOW, 8 * WINDOW), lambda g, i: (0, 0, 0)), sspec,
                  qspec, kspec, kspec],
        out_specs=[qspec, kspec, kspec, bspec, sspec],
        out_shape=[jax.ShapeDtypeStruct(qt.shape, BF16), jax.ShapeDtypeStruct(kpt.shape, F32),
                   jax.ShapeDtypeStruct(kpt.shape, F32), jax.ShapeDtypeStruct(bias_t.shape, F32),
                   jax.ShapeDtypeStruct(sink_row.shape, F32)],
        compiler_params=_params(("parallel", "arbitrary")),
    )(qt, kpt, vpt, bias_t, _band_valid(), sink_row, dot_, dk_in, dv_in)


def _bucket_onehot():
    qi = np.arange(WINDOW)[:, None]
    kj = np.arange(2 * WINDOW)[None, :]
    n = np.maximum(qi + WINDOW - kj, 0)
    max_exact = N_BUCKETS // 2
    nf = np.maximum(n, 1).astype(np.float64)
    val = np.log(nf / max_exact) / math.log(WINDOW / max_exact) * (N_BUCKETS - max_exact)
    assert np.all(np.abs(val - np.round(val))[(n > max_exact) & (n < WINDOW)] > 1e-3)
    large = np.minimum(max_exact + val.astype(np.int64), N_BUCKETS - 1)
    bucket = np.where(n < max_exact, n, large).reshape(-1)
    onehot = np.zeros((128, bucket.size), np.float32)
    onehot[bucket, np.arange(bucket.size)] = 1.0
    return onehot


def _split3(x):
    a = x.astype(BF16)
    r = x - a.astype(F32)
    b = r.astype(BF16)
    c = (r - b.astype(F32)).astype(BF16)
    return a, b, c


def bias_table(rel_bias):
    nh = rel_bias.shape[1]
    oh = jnp.asarray(_bucket_onehot(), BF16)
    n = oh.shape[1]
    tn = 4096
    rb = jnp.zeros((nh, 128), F32).at[:, :N_BUCKETS].set(rel_bias.T)

    def body(rb_ref, oh_ref, o_ref):
        o_ref[...] = sum(jnp.dot(t, oh_ref[...], preferred_element_type=F32) for t in _split3(rb_ref[...]))

    return pl.pallas_call(
        body, name="bias_table", grid=(n // tn,),
        in_specs=[pl.BlockSpec((nh, 128), lambda i: (0, 0)), pl.BlockSpec((128, tn), lambda i: (0, i))],
        out_specs=pl.BlockSpec((nh, tn), lambda i: (0, i)),
        out_shape=jax.ShapeDtypeStruct((nh, n), F32),
        compiler_params=_params(("parallel",)),
    )(rb, oh)


def bias_table_grad(db0, db1):
    nh, n = db0.shape
    oh = jnp.asarray(_bucket_onehot(), BF16)
    tn = 4096

    def body(a_ref, b_ref, oh_ref, o_ref):
        @pl.when(pl.program_id(0) == 0)
        def _():
            o_ref[...] = jnp.zeros_like(o_ref)

        o_ref[...] += sum(lax.dot_general(t, oh_ref[...], (((1,), (1,)), ((), ())), preferred_element_type=F32)
                          for t in _split3(a_ref[...] + b_ref[...]))

    blk = pl.BlockSpec((nh, tn), lambda i: (0, i))
    return pl.pallas_call(
        body, name="bias_table_grad", grid=(n // tn,),
        in_specs=[blk, blk, pl.BlockSpec((128, tn), lambda i: (0, i))],
        out_specs=pl.BlockSpec((nh, 128), lambda i: (0, 0)),
        out_shape=jax.ShapeDtypeStruct((nh, 128), F32),
        compiler_params=_params(("arbitrary",)),
    )(db0, db1, oh)


def _owner_view(ref, name, d):
    if name == 'a_norm':
        return ref.at[d]
    if name in COL_SHARDED:
        n = ref.shape[2] // N_DEV
        return ref.at[:, :, pl.ds(pl.multiple_of(d * n, 128), n)]
    return ref.at[:, d]


def _place():
    return lax.axis_index("x"), lax.axis_index("y"), lax.axis_index("c")


def _dev(p):
    return 4 * p[0] + 2 * p[1] + p[2]


def _remote(src, dst, send_sem, recv_sem, to):
    return pltpu.make_async_remote_copy(src_ref=src, dst_ref=dst, send_sem=send_sem, recv_sem=recv_sem,
                                        device_id=to, device_id_type=MESH)


def _dma_sems(*shapes):
    return [pltpu.SemaphoreType.DMA(sh) for sh in shapes]


def comm_call(name, build, ins, out_shapes, sems):
    n_in, n_out = len(ins), len(out_shapes)

    def body(*refs):
        copies = build(refs[:n_in], refs[n_in:n_in + n_out], *refs[n_in + n_out:])
        for cp in copies:
            cp.start()
        for cp in copies:
            cp.wait()

    hbm = pl.BlockSpec(memory_space=pl.ANY)
    return pl.pallas_call(
        body, name=name, in_specs=[hbm] * n_in, out_specs=[hbm] * n_out, out_shape=list(out_shapes),
        scratch_shapes=sems,
    )(*ins)


def all_gather_weights(names, shards, full_shapes):
    n = len(names)

    def body(*refs):
        ins, outs = refs[:n], refs[n:2 * n]
        send_sems, recv_sems, local_sems = refs[2 * n:]
        x, y, c = _place()
        me, sibling = (x, y, c), (x, y, 1 - c)
        chips = [(1 - x, y), (x, 1 - y), (1 - x, 1 - y)]

        def copy(t, k, block, to, src=None):
            dst = _owner_view(outs[t], names[t], _dev(block))
            return _remote(dst if src is None else src, dst, send_sems.at[t, k], recv_sems.at[t, k], to)

        mine = [pltpu.make_async_copy(ins[t], _owner_view(outs[t], names[t], _dev(me)), local_sems.at[t])
                for t in range(n)]
        for cp in mine:
            cp.start()
        first = []
        for t in range(n):
            first.append(copy(t, 0, me, sibling, src=ins[t]))
            first += [copy(t, 1 + j, me, (*chip, c), src=ins[t]) for j, chip in enumerate(chips)]
        for cp in first:
            cp.start()
        passed = []
        for j, chip in enumerate(chips):
            for t in range(n):
                copy(t, 1 + j, (*chip, c), me).wait_recv()
                fwd = copy(t, 4 + j, (*chip, c), sibling)
                fwd.start()
                passed.append(fwd)
        for t in range(n):
            copy(t, 0, sibling, me).wait_recv()
            for j, chip in enumerate(chips):
                copy(t, 4 + j, (*chip, 1 - c), me).wait_recv()
        for cp in first + passed:
            cp.wait_send()
        for cp in mine:
            cp.wait()

    hbm = pl.BlockSpec(memory_space=pl.ANY)
    return pl.pallas_call(
        body, name="all_gather_layer0",
        in_specs=[hbm] * n, out_specs=[hbm] * n,
        out_shape=[jax.ShapeDtypeStruct(full_shapes[t], shards[t].dtype) for t in range(n)],
        scratch_shapes=_dma_sems((n, 7), (n, 7), (n,)),
    )(*shards)


def ag_direct(names, shards, full_shapes):
    n = len(names)

    def build(ins, outs, send_sems, recv_sems, local_sems, fwd_send_sems, fwd_recv_sems):
        x, y, c = _place()
        peers = [(x, y, 1 - c), (1 - x, y, c), (x, 1 - y, c), (1 - x, 1 - y, c)]
        copies = []
        for t in range(n):
            dst = _owner_view(outs[t], names[t], _dev((x, y, c)))
            copies.append(pltpu.make_async_copy(ins[t], dst, local_sems.at[t]))
            copies += [_remote(ins[t], dst, send_sems.at[t, k], recv_sems.at[t, k], to) for k, to in enumerate(peers)]
        return copies

    def forward(ins, outs, send_sems, recv_sems, local_sems, fwd_send_sems, fwd_recv_sems):
        x, y, c = _place()
        copies = []
        for t in range(n):
            for k, chip in enumerate([(1 - x, y), (x, 1 - y), (1 - x, 1 - y)]):
                view = _owner_view(outs[t], names[t], _dev((*chip, c)))
                copies.append(_remote(view, view, fwd_send_sems.at[t, k], fwd_recv_sems.at[t, k], (x, y, 1 - c)))
        return copies

    return Carry(build, shards, [jax.ShapeDtypeStruct(full_shapes[t], shards[t].dtype) for t in range(n)],
                 _dma_sems((n, 4), (n, 4), (n,), (n, 3), (n, 3)), then=forward)


def sibling_exchange(names, grads, part_shapes):
    n = len(names)

    def build(ins, outs, send_sems, recv_sems):
        x, y, c = _place()
        return [_remote(_owner_view(ins[t], names[t], 2 * q + 1 - c), outs[t].at[q], send_sems.at[t, q],
                        recv_sems.at[t, q], (x, y, 1 - c)) for t in range(n) for q in range(4)]

    return Carry(build, grads, [jax.ShapeDtypeStruct((4,) + part_shapes[t], BF16) for t in range(n)],
                 _dma_sems((n, 4), (n, 4)))


def chip_exchange(names, parts, part_shapes):
    n = len(names)

    def build(ins, outs, send_sems, recv_sems):
        x, y, c = _place()
        chips = [(1 - x, y), (x, 1 - y), (1 - x, 1 - y)]
        return [_remote(ins[t].at[2 * chip[0] + chip[1]], outs[t].at[k], send_sems.at[t, k], recv_sems.at[t, k],
                        (*chip, c)) for t in range(n) for k, chip in enumerate(chips)]

    return Carry(build, parts, [jax.ShapeDtypeStruct((3,) + part_shapes[t], BF16) for t in range(n)],
                 _dma_sems((n, 3), (n, 3)))


def all_gather_rows(x):
    r, w = x.shape

    def body(x_ref, out_ref, send_sems, recv_sems, local_sem):
        px, py, pc = _place()
        me = 4 * px + 2 * py + pc
        mine = pltpu.make_async_copy(x_ref, out_ref.at[me], local_sem)
        mine.start()
        copies = []
        for k in range(1, N_DEV):
            peer = (px ^ (k >> 2), py ^ ((k >> 1) & 1), pc ^ (k & 1))
            copies.append(pltpu.make_async_remote_copy(
                src_ref=x_ref, dst_ref=out_ref.at[me], send_sem=send_sems.at[k - 1], recv_sem=recv_sems.at[k - 1],
                device_id=peer, device_id_type=MESH))
        for cp in copies:
            cp.start()
        for k in range(1, N_DEV):
            peer_idx = me ^ k
            pltpu.make_async_remote_copy(
                src_ref=x_ref, dst_ref=out_ref.at[peer_idx], send_sem=send_sems.at[k - 1],
                recv_sem=recv_sems.at[k - 1], device_id=(px, py, pc), device_id_type=MESH).wait_recv()
        for cp in copies:
            cp.wait_send()
        mine.wait()

    vmem = pl.BlockSpec(memory_space=pltpu.VMEM)
    return pl.pallas_call(
        body, name="all_gather_small_grads",
        in_specs=[vmem], out_specs=vmem,
        out_shape=jax.ShapeDtypeStruct((N_DEV, r, w), x.dtype),
        scratch_shapes=[pltpu.SemaphoreType.DMA((N_DEV - 1,)), pltpu.SemaphoreType.DMA((N_DEV - 1,)),
                        pltpu.SemaphoreType.DMA],
    )(x)


def _adamw(w, g, m, v):
    m = ADAM_B1 * m + (1.0 - ADAM_B1) * g
    v = ADAM_B2 * v + (1.0 - ADAM_B2) * (g * g)
    m_hat = m / (1.0 - ADAM_B1 ** ADAM_STEP)
    v_hat = v / (1.0 - ADAM_B2 ** ADAM_STEP)
    return -ADAM_LR * (m_hat / (jnp.sqrt(v_hat) + ADAM_EPS) + ADAM_WD * w), m, v


def sibling_sum(name, col, grads, recv, core):
    _, nl, rows, cols = recv.shape
    tr = _tile(rows, 1024)
    rspec = pl.BlockSpec((None, None, tr, cols), lambda q, l, i, c_ref: (q, l, i, 0))
    if col:
        gspec = pl.BlockSpec((None, tr, cols), lambda q, l, i, c_ref: (l, i, 2 * q + c_ref[0]))
    else:
        gspec = pl.BlockSpec((None, None, tr, cols), lambda q, l, i, c_ref: (l, 2 * q + c_ref[0], i, 0))

    def body(c_ref, g_ref, r_ref, o_ref):
        del c_ref
        o_ref[...] = (g_ref[...].astype(F32) + r_ref[...].astype(F32)).astype(BF16)

    return pl.pallas_call(
        body, name=name,
        grid_spec=pltpu.PrefetchScalarGridSpec(num_scalar_prefetch=1, grid=(4, nl, rows // tr),
                                               in_specs=[gspec, rspec], out_specs=rspec),
        out_shape=jax.ShapeDtypeStruct(recv.shape, BF16),
        compiler_params=_params(("parallel", "parallel", "parallel")),
    )(core.reshape(1), grads, recv)


def reduce_adamw(name, parts, recv, chip, w, m, v, l0, prev):
    _, nl, rows, cols = parts.shape
    tr = _tile(rows, 512)

    def body(q_ref, p_ref, r_ref, w_ref, m_ref, v_ref, *rest):
        del q_ref
        g_out, d_out, m_out, v_out = rest[-4:]
        g = ((p_ref[...].astype(F32) + r_ref[0].astype(F32)) + r_ref[1].astype(F32)) + r_ref[2].astype(F32)
        d, mn, vn = _adamw(w_ref[...], g, m_ref[...], v_ref[...])
        g_out[...] = g
        d_out[...] = d
        m_out[...] = mn
        v_out[...] = vn

    blk = pl.BlockSpec((None, tr, cols), lambda l, i, q_ref: (l0 + l, i, 0))
    prev = list(prev) if prev else []
    return pl.pallas_call(
        body, name=name,
        grid_spec=pltpu.PrefetchScalarGridSpec(
            num_scalar_prefetch=1, grid=(nl, rows // tr),
            in_specs=[pl.BlockSpec((None, None, tr, cols), lambda l, i, q_ref: (q_ref[0], l, i, 0)),
                      pl.BlockSpec((3, None, tr, cols), lambda l, i, q_ref: (0, l, i, 0)), blk, blk, blk]
            + [pl.BlockSpec(memory_space=pl.ANY)] * len(prev),
            out_specs=[blk] * 4),
        out_shape=[jax.ShapeDtypeStruct(w.shape, F32)] * 4,
        input_output_aliases={6 + i: i for i in range(len(prev))},
        compiler_params=_params(("parallel", "parallel")),
    )(chip.reshape(1), parts, recv, w, m, v, *prev)


def small_adamw(name, gathered, w, m, v):
    _, r, c = gathered.shape

    def body(ga_ref, w_ref, m_ref, v_ref, g_out, d_out, m_out, v_out):
        g = ga_ref[0]
        for d in range(1, N_DEV):
            g = g + ga_ref[d]
        dl, mn, vn = _adamw(w_ref[...], g, m_ref[...], v_ref[...])
        g_out[...] = g
        d_out[...] = dl
        m_out[...] = mn
        v_out[...] = vn

    return pl.pallas_call(
        body, name=name,
        out_shape=[jax.ShapeDtypeStruct((r, c), F32)] * 4,
        compiler_params=_params(),
    )(gathered, w, m, v)


def _rms(x, g):
    return x * lax.rsqrt(jnp.mean(x * x, axis=-1, keepdims=True) + EPS) * g


def _rms_bwd_epilogue(dn, x, dres, g):
    r = lax.rsqrt(jnp.mean(x * x, axis=-1, keepdims=True) + EPS)
    xh = x * r
    dyg = dn * g
    dx = dres + r * (dyg - xh * jnp.mean(dyg * xh, axis=-1, keepdims=True))
    return dx, dx, jnp.sum(dn * xh, axis=0, keepdims=True), jnp.sum(dx, axis=0, keepdims=True)


def _residual_then_norms(n_terms):
    def epilogue(acc, *ex):
        h = acc
        for t in ex[:n_terms]:
            h = h + t
        return (h,) + tuple(_rms(h, g) for g in ex[n_terms:])
    return epilogue


def local_step(x, target, small, ex):
    s, d = x.shape
    n_a, n_b = small['a_norm'].shape[0], small['b_norm'].shape[0]
    sg = {}
    gb = {}

    def fwd_mm(name, a, wname, layer, epilogue, extras, out_dtypes, **kw):
        return mm_nn(name, a, *ex.weight(wname, layer), epilogue, extras, out_dtypes, **kw)

    def dx_mm(name, dy, wname, layer, epilogue, extras, out_dtypes, **kw):
        return mm_nt(name, dy, *ex.weight(wname, layer), epilogue, extras, out_dtypes, **kw)

    def dw_mm(name, a, dy, wname, layer, **kw):
        key, slab, shape = ex.grad(wname, layer)
        gb[key] = mm_tn(name, a, dy, gb.get(key), shape, slab, **kw)

    plain = lambda acc: (acc,)
    plus_col = lambda acc, b: (acc + b,)

    bias_flat = bias_table(small['rel_bias'])
    bias_t = bias_flat.reshape(2, 8, WINDOW, 2 * WINDOW).transpose(0, 3, 1, 2).reshape(2, 2 * WINDOW, 8 * WINDOW)
    sink_rows = [jnp.repeat(small['b_sinks'][j], WINDOW).reshape(2, 1, 8 * WINDOW) for j in range(n_b)]

    gain = lambda g: g.reshape(1, -1)

    def mlp_fwd(h, n2, layer, next_gains):
        u, a = fwd_mm(f"mlp_up_fwd{layer}", n2, 'mlp_up', layer,
                      lambda acc: (acc, jnp.square(jnp.maximum(acc, 0.0))), (), (BF16, BF16))
        h2, *nexts = fwd_mm(f"mlp_down_fwd{layer}", a, 'mlp_down', layer, _residual_then_norms(1),
                            (h, *[gain(g) for g in next_gains]), (F32,) + (BF16,) * len(next_gains))
        return h2, nexts, (n2, u, a)

    h = x
    saved = []
    n1 = rms_fwd("a_norm_fwd0", h, small['a_norm'][0])
    for l in range(n_a):
        (qkvt,) = fwd_mm(f"a_qkv_fwd{l}", n1, 'a_wqkv', l, plain, (), (BF16,), out_t=True)
        qkvt = qkvt.reshape(3 * d // HEAD_DIM, HEAD_DIM, s)
        o_t, rtab, carried = sb_fwd(f"sb_fwd{l}", qkvt, ex.fwd_carry(l))
        ex.fwd_done(l, carried)
        o_t = o_t.reshape(d, s)
        h_mid, n2 = fwd_mm(f"a_wo_fwd{l}", o_t, 'a_wo', l, _residual_then_norms(1),
                           (h, gain(small['mlp_norm'][l])), (F32, BF16), a_t=True)
        next_gains = [small['a_norm'][l + 1]] if l + 1 < n_a else [small['b_norm'][0], small['kv_norm']]
        h_out, nexts, mlp_saved = mlp_fwd(h_mid, n2, l, next_gains)
        saved.append((h, n1, qkvt, o_t, rtab, h_mid, mlp_saved))
        h, n1 = h_out, nexts[0]
    h_kv, nkv = h, nexts[1]
    (kvt,) = fwd_mm("kv_fwd", nkv, 'w_kv', 0, plus_col, (small['b_kv'].reshape(-1, 1),), (BF16,), out_t=True)
    kvt = kvt.reshape(2, 2, HEAD_DIM, s)
    kpt, vpt = (jnp.pad(t, ((0, 0), (0, 0), (WINDOW, 0))) for t in (kvt[0], kvt[1]))
    for j in range(n_b):
        layer = n_a + j
        (qbt,) = fwd_mm(f"b_q_fwd{j}", n1, 'b_wq', j, plus_col, (small['b_bq'][j].reshape(-1, 1),), (BF16,),
                        out_t=True)
        o_t = swa_fwd(f"swa_fwd{j}", qbt, kpt, vpt, bias_t, sink_rows[j])
        h_mid, n2 = fwd_mm(f"b_wo_fwd{j}", o_t, 'b_wo', j, _residual_then_norms(2),
                           (h, gain(small['b_bo'][j]), gain(small['mlp_norm'][layer])), (F32, BF16), a_t=True)
        h_out, nexts, mlp_saved = mlp_fwd(h_mid, n2, layer, [small['b_norm'][j + 1]] if j + 1 < n_b else [])
        saved.append((h, n1, qbt, o_t, h_mid, mlp_saved))
        h, n1 = h_out, (nexts[0] if nexts else None)

    dh, dhb, dg_final, loss_b = loss_head(h, small['final_norm'], target)
    sg['final_norm'] = dg_final[0]
    sg['mlp_norm'] = [None] * (n_a + n_b)

    def mlp_bwd(dh, dhb, h_mid, mlp_saved, layer):
        n2, u, a = mlp_saved
        du, *carried = dx_mm(f"mlp_down_dx{layer}", dhb, 'mlp_down', layer,
                             lambda acc, uu: (acc * (2.0 * jnp.maximum(uu.astype(F32), 0.0)),), (u,), (BF16,),
                             exchange=ex.mlp_carry(layer, gb))
        ex.mlp_done(layer, carried)
        dw_mm(f"mlp_down_dw{layer}", a, dhb, 'mlp_down', layer)
        dh2, dh2b, dg, cs = dx_mm(f"mlp_up_dx{layer}", du, 'mlp_up', layer, _rms_bwd_epilogue,
                                  (h_mid, dh, gain(small['mlp_norm'][layer])), (F32, BF16), n_sums=2)
        dw_mm(f"mlp_up_dw{layer}", n2, du, 'mlp_up', layer)
        sg['mlp_norm'][layer] = dg[0]
        return dh2, dh2b, cs

    dkp = jnp.zeros(kpt.shape, F32)
    dvp = jnp.zeros(vpt.shape, F32)
    sg['b_norm'], sg['b_bq'], sg['b_bo'], sg['b_sinks'] = [None] * n_b, [None] * n_b, [None] * n_b, [None] * n_b
    dbias = [None] * n_b
    for j in reversed(range(n_b)):
        layer = n_a + j
        h_in, n1, qbt, o_t, h_mid, mlp_saved = saved[layer]
        dh, dhb, cs = mlp_bwd(dh, dhb, h_mid, mlp_saved, layer)
        sg['b_bo'][j] = cs[0]
        (do_t,) = dx_mm(f"b_wo_dx{j}", dhb, 'b_wo', j, plain, (), (BF16,), out_t=True)
        dw_mm(f"b_wo_dw{j}", o_t, dhb, 'b_wo', j, x_t=True)
        dq_t, dkp, dvp, dbias[j], dsink = swa_bwd(f"swa_bwd{j}", qbt, kpt, vpt, bias_t, sink_rows[j], do_t, dkp, dvp)
        sg['b_sinks'][j] = colsum(f"sink_grad{j}", dsink.reshape(16, WINDOW).T)[0]
        sg['b_bq'][j] = rowsum(f"b_bq_grad{j}", dq_t)
        dh, dhb, dg, _ = dx_mm(f"b_q_dx{j}", dq_t, 'b_wq', j, _rms_bwd_epilogue,
                               (h_in, dh, gain(small['b_norm'][j])), (F32, BF16), a_t=True, n_sums=2)
        dw_mm(f"b_q_dw{j}", n1, dq_t, 'b_wq', j, dy_t=True)
        sg['b_norm'][j] = dg[0]
    unt = lambda t: t.reshape(2, 2 * WINDOW, 8, WINDOW).transpose(0, 2, 3, 1).reshape(bias_flat.shape)
    sg['rel_bias'] = bias_table_grad(unt(dbias[0]), unt(dbias[1]))[:, :N_BUCKETS].T

    dkv_t = jnp.concatenate([dkp[:, :, WINDOW:], dvp[:, :, WINDOW:]], axis=0).reshape(-1, s)
    sg['b_kv'] = rowsum("b_kv_grad", dkv_t)
    dkvb = dkv_t.astype(BF16)
    dh, dhb, dg, _ = dx_mm("kv_dx", dkvb, 'w_kv', 0, _rms_bwd_epilogue, (h_kv, dh, gain(small['kv_norm'])),
                           (F32, BF16), a_t=True, n_sums=2)
    dw_mm("kv_dw", nkv, dkvb, 'w_kv', 0, dy_t=True)
    sg['kv_norm'] = dg[0]

    sg['a_norm'] = [None] * n_a
    for l in reversed(range(n_a)):
        h_in, n1, qkvt, o_t, rtab, h_mid, mlp_saved = saved[l]
        dh, dhb, _ = mlp_bwd(dh, dhb, h_mid, mlp_saved, l)
        (do_t,) = dx_mm(f"a_wo_dx{l}", dhb, 'a_wo', l, plain, (), (BF16,), out_t=True)
        dw_mm(f"a_wo_dw{l}", o_t, dhb, 'a_wo', l, x_t=True)
        dqkv_t, carried = sb_bwd(f"sb_bwd{l}", qkvt, do_t.reshape(d // HEAD_DIM, HEAD_DIM, s), rtab,
                                 ex.bwd_carry(l, gb))
        ex.bwd_done(l, carried)
        dqkv_t = dqkv_t.reshape(3 * d, s)
        dw_mm(f"a_qkv_dw{l}", n1, dqkv_t, 'a_wqkv', l, dy_t=True)
        dh, dhb, dg, _, *carried = dx_mm(f"a_qkv_dx{l}", dqkv_t, 'a_wqkv', l, _rms_bwd_epilogue,
                                         (h_in, dh, gain(small['a_norm'][l])), (F32, BF16), a_t=True, n_sums=2,
                                         exchange=ex.last_carry(gb) if l == 0 else None)
        if l == 0:
            ex.last_done(carried)
        sg['a_norm'][l] = dg[0]

    small_grads = {
        'a_norm': jnp.stack(sg['a_norm']), 'kv_norm': sg['kv_norm'], 'b_kv': sg['b_kv'],
        'b_norm': jnp.stack(sg['b_norm']), 'b_bq': jnp.stack(sg['b_bq']), 'b_sinks': jnp.stack(sg['b_sinks']),
        'b_bo': jnp.stack(sg['b_bo']), 'rel_bias': sg['rel_bias'], 'mlp_norm': jnp.stack(sg['mlp_norm']),
        'final_norm': sg['final_norm'],
    }
    return loss_b, dh, gb, small_grads


def _full_shape(name, shard_shape):
    if name in COL_SHARDED:
        return shard_shape[:2] + (N_DEV * shard_shape[2],)
    nl, r, n = shard_shape
    return (nl, N_DEV, r, n)


def _as_w3_shape(name, shard_shape):
    full = _full_shape(name, shard_shape)
    return full if name in COL_SHARDED else (full[0], full[1] * full[2], full[3])


def _as_w3(name, full):
    if name in COL_SHARDED:
        return full
    nl, nd, r, n = full.shape
    return full.reshape(nl, nd * r, n)


AG_GROUPS = {
    0: (('a_wqkv', 0, 1),),
    1: (('a_wo', 0, 2), ('mlp_up', 0, 2), ('mlp_down', 0, 2), ('a_wqkv', 1, 1)),
    2: (('mlp_up', 2, 2), ('mlp_down', 2, 2), ('b_wq', 0, 2), ('b_wo', 0, 2), ('w_kv', 0, 1)),
}
RS_GROUPS = {
    'A': (('mlp_up', 2, 2), ('mlp_down', 2, 2), ('b_wq', 0, 2), ('b_wo', 0, 2), ('w_kv', 0, 1)),
    'B1': (('a_wo', 1, 1), ('mlp_up', 1, 1), ('mlp_down', 1, 1)),
    'B2': (('a_wqkv', 1, 1),),
    'B3': (('a_wo', 0, 1), ('mlp_up', 0, 1), ('mlp_down', 0, 1)),
    'C': (('a_wqkv', 0, 1),),
}
UNDER_MLP = {1: 'A', 0: 'B2'}
SIBLING_UNDER_SB_BWD = {1: 'B1'}
UNDER_SB_BWD = {1: ('A',), 0: ('B1', 'B2', 'B3')}


class _Exchanges:
    def __init__(self, full0, shards, core, chip, w3, m3, v3):
        self.wbuf = {0: {n: _as_w3(n, full0[n]) for n, _, _ in AG_GROUPS[0]}}
        self.shards, self.core, self.chip = shards, core, chip
        self.w3, self.m3, self.v3 = w3, m3, v3
        self.shard_dims = {n: w3[n].shape[1:] for n in BIG}
        self.parts = {}
        self.gfull = {}
        self.out = {}

    def weight(self, name, layer):
        for group, members in AG_GROUPS.items():
            for n, l0, nl in members:
                if n == name and l0 <= layer < l0 + nl:
                    return self.wbuf[group][name], layer - l0
        raise KeyError((name, layer))

    def fwd_carry(self, layer):
        names = [n for n, _, _ in AG_GROUPS[layer + 1]]
        shards = [self.shards[layer + 1][n] for n in names]
        return ag_direct(names, shards, [_full_shape(n, sh.shape) for n, sh in zip(names, shards)])

    def fwd_done(self, layer, carried):
        names = [n for n, _, _ in AG_GROUPS[layer + 1]]
        self.wbuf[layer + 1] = {n: _as_w3(n, f) for n, f in zip(names, carried)}

    def grad(self, name, layer):
        for group, members in RS_GROUPS.items():
            for n, l0, nl in members:
                if n == name and l0 <= layer < l0 + nl:
                    return (group, name), layer - l0, _as_w3_shape(name, (nl,) + self.shard_dims[name])
        raise KeyError((name, layer))

    def _members(self, group):
        names = [n for n, _, _ in RS_GROUPS[group]]
        return names, [(nl,) + self.shard_dims[n] for n, _, nl in RS_GROUPS[group]]

    def _sibling_carry(self, group, gb):
        names, shapes = self._members(group)
        self.gfull[group] = [gb[(group, n)].reshape(_full_shape(n, sh)) for n, sh in zip(names, shapes)]
        return sibling_exchange(names, self.gfull[group], shapes)

    def _sibling_done(self, group, recv):
        names, _ = self._members(group)
        self.parts[group] = [sibling_sum(f"rs_sibling_sum_{group}_{n}", n in COL_SHARDED, g, r, self.core)
                             for n, g, r in zip(names, self.gfull[group], recv)]

    def _sibling_stage(self, group, gb):
        ce = self._sibling_carry(group, gb)
        self._sibling_done(group, comm_call(f"rs_sibling_exchange_{group}", ce.build, ce.ins, ce.out_shapes, ce.sems))

    def mlp_carry(self, layer, gb):
        return self._sibling_carry(UNDER_MLP[layer], gb) if layer in UNDER_MLP else None

    def mlp_done(self, layer, carried):
        if layer in UNDER_MLP:
            self._sibling_done(UNDER_MLP[layer], carried)

    def bwd_carry(self, layer, gb):
        names, parts, shapes = [], [], []
        for group in UNDER_SB_BWD[layer]:
            if group not in self.parts:
                self._sibling_stage(group, gb)
            names += self._members(group)[0]
            shapes += self._members(group)[1]
            parts += self.parts[group]
        exchange = chip_exchange(names, parts, shapes)
        if layer in SIBLING_UNDER_SB_BWD:
            exchange = _together(exchange, self._sibling_carry(SIBLING_UNDER_SB_BWD[layer], gb))
        return exchange

    def bwd_done(self, layer, carried):
        for group in UNDER_SB_BWD[layer]:
            n = len(RS_GROUPS[group])
            self._adamw(group, carried[:n])
            carried = carried[n:]
        if layer in SIBLING_UNDER_SB_BWD:
            self._sibling_done(SIBLING_UNDER_SB_BWD[layer], carried)

    def last_carry(self, gb):
        self._sibling_stage('C', gb)
        names, shapes = self._members('C')
        return chip_exchange(names, self.parts['C'], shapes)

    def last_done(self, carried):
        self._adamw('C', carried)

    def _adamw(self, group, recv2):
        for (n, l0, _), p, r in zip(RS_GROUPS[group], self.parts[group], recv2):
            self.out[n] = reduce_adamw(f"adamw_{group}_{n}", p, r, self.chip, self.w3[n], self.m3[n], self.v3[n],
                                       l0, self.out.get(n))


def _pack_small(vals):
    flat = jnp.concatenate([vals[n].reshape(-1).astype(F32) for n in SMALL] + [vals['loss'].reshape(-1)])
    rows = -(-flat.shape[0] // 1024) * 8
    return jnp.pad(flat, (0, rows * 128 - flat.shape[0])).reshape(rows, 128)


def _unpack_small(packed, shapes):
    flat = packed.reshape(-1)
    out, off = {}, 0
    for n in SMALL + ['loss']:
        size = int(np.prod(shapes[n]))
        out[n] = flat[off:off + size].reshape(shapes[n])
        off += size
    return out


def kernel(x, a_norm, a_wqkv, a_wo, kv_norm, w_kv, b_kv, b_norm, b_wq, b_bq, b_sinks, b_wo, b_bo, rel_bias, mlp_norm, mlp_up, mlp_down, final_norm, loss_target, m_a_norm, m_a_wqkv, m_a_wo, m_kv_norm, m_w_kv, m_b_kv, m_b_norm, m_b_wq, m_b_bq, m_b_sinks, m_b_wo, m_b_bo, m_rel_bias, m_mlp_norm, m_mlp_up, m_mlp_down, m_final_norm, v_a_norm, v_a_wqkv, v_a_wo, v_kv_norm, v_w_kv, v_b_kv, v_b_norm, v_b_wq, v_b_bq, v_b_sinks, v_b_wo, v_b_bo, v_rel_bias, v_mlp_norm, v_mlp_up, v_mlp_down, v_final_norm):
    w = dict(a_norm=a_norm, a_wqkv=a_wqkv, a_wo=a_wo, kv_norm=kv_norm, w_kv=w_kv, b_kv=b_kv, b_norm=b_norm,
             b_wq=b_wq, b_bq=b_bq, b_sinks=b_sinks, b_wo=b_wo, b_bo=b_bo, rel_bias=rel_bias, mlp_norm=mlp_norm,
             mlp_up=mlp_up, mlp_down=mlp_down, final_norm=final_norm)
    m = dict(a_norm=m_a_norm, a_wqkv=m_a_wqkv, a_wo=m_a_wo, kv_norm=m_kv_norm, w_kv=m_w_kv, b_kv=m_b_kv,
             b_norm=m_b_norm, b_wq=m_b_wq, b_bq=m_b_bq, b_sinks=m_b_sinks, b_wo=m_b_wo, b_bo=m_b_bo,
             rel_bias=m_rel_bias, mlp_norm=m_mlp_norm, mlp_up=m_mlp_up, mlp_down=m_mlp_down, final_norm=m_final_norm)
    v = dict(a_norm=v_a_norm, a_wqkv=v_a_wqkv, a_wo=v_a_wo, kv_norm=v_kv_norm, w_kv=v_w_kv, b_kv=v_b_kv,
             b_norm=v_b_norm, b_wq=v_b_wq, b_bq=v_b_bq, b_sinks=v_b_sinks, b_wo=v_b_wo, b_bo=v_b_bo,
             rel_bias=v_rel_bias, mlp_norm=v_mlp_norm, mlp_up=v_mlp_up, mlp_down=v_mlp_down, final_norm=v_final_norm)
    px, py, pc = _place()
    me = 4 * px + 2 * py + pc
    chip = (2 * px + py).astype(jnp.int32)
    core = pc.astype(jnp.int32)

    as3 = lambda t: t[None] if t.ndim == 2 else t
    w3, m3, v3 = ({n: as3(src[n]) for n in BIG} for src in (w, m, v))
    shards = {g: {n: w3[n][l0:l0 + nl].astype(BF16) for n, l0, nl in members} for g, members in AG_GROUPS.items()}
    an_pad = jnp.zeros((8, 128), F32).at[:a_norm.shape[0]].set(a_norm)
    names0 = [n for n, _, _ in AG_GROUPS[0]]
    full0 = all_gather_weights(names0 + ['a_norm'], [shards[0][n] for n in names0] + [an_pad],
                               [_full_shape(n, shards[0][n].shape) for n in names0] + [(N_DEV, 8, 128)])
    full0 = dict(zip(names0 + ['a_norm'], full0))
    n_a = a_norm.shape[0]
    small = {n: w[n] for n in SMALL}
    small['a_norm'] = full0['a_norm'][:, :n_a].transpose(1, 0, 2).reshape(n_a, -1)

    ex = _Exchanges(full0, shards, core, chip, w3, m3, v3)
    loss_b, grad_x, gb, sgrads = local_step(x[0], loss_target[0], small, ex)
    out = {n: [t.reshape(w[n].shape) for t in bufs] for n, bufs in ex.out.items()}

    sgrads['loss'] = loss_b[0, :1]
    gathered = all_gather_rows(_pack_small(sgrads))
    shapes = {n: w[n].shape for n in SMALL}
    shapes['a_norm'] = (n_a, a_norm.shape[1] * N_DEV)
    shapes['loss'] = (1,)
    zeros1 = jnp.zeros((1,), F32)

    def packed(src):
        vals = {n: src[n] for n in SMALL}
        vals['a_norm'] = jnp.zeros(shapes['a_norm'], F32)
        vals['loss'] = zeros1
        return _pack_small(vals)

    sm = small_adamw("adamw_small", gathered, packed(w), packed(m), packed(v))
    sm = [_unpack_small(t, shapes) for t in sm]
    g_an = lax.dynamic_slice_in_dim(sm[0]['a_norm'], me * a_norm.shape[1], a_norm.shape[1], axis=1)
    pad = lambda t: jnp.zeros((8, 128), F32).at[:n_a].set(t)
    gathered_an = jnp.zeros((N_DEV, 8, 128), F32).at[0].set(pad(g_an))
    an = small_adamw("adamw_a_norm", gathered_an, pad(a_norm), pad(m_a_norm), pad(v_a_norm))
    for i in range(4):
        sm[i]['a_norm'] = an[i][:n_a]
    for n in BIG:
        for i in range(4):
            sm[i][n] = out[n][i]
    loss = sm[0]['loss'][0]
    return (loss, grad_x[None], *[sm[0][n] for n in WEIGHTS], *[sm[1][n] for n in WEIGHTS],
            *[sm[2][n] for n in WEIGHTS], *[sm[3][n] for n in WEIGHTS])
```

```python
import math

import numpy as np
import jax
import jax.numpy as jnp
from jax import lax
from jax.experimental import pallas as pl
from jax.experimental.pallas import tpu as pltpu

F32 = jnp.float32
BF16 = jnp.bfloat16
MESH = pl.DeviceIdType.MESH

N_DEV = 8
HEAD_DIM = 64
WINDOW = 128
N_BUCKETS = 32
EPS = 1e-5
NEG_INF = -1e30
Q_SCALE = 1.0 / math.sqrt(HEAD_DIM)
LOG2E = 1.4426950408889634

ADAM_LR, ADAM_B1, ADAM_B2, ADAM_EPS, ADAM_WD, ADAM_STEP = 0.001, 0.9, 0.999, 1e-08, 0.01, 10

SB_BQ = 512
SB_BK = 128
SB_DEAD = 160.0
SB_UNSEEN = 1e30
ROW_TILE = 512
VMEM_LIMIT = 56 * 1024 * 1024

WEIGHTS = ['a_norm', 'a_wqkv', 'a_wo', 'kv_norm', 'w_kv', 'b_kv', 'b_norm', 'b_wq', 'b_bq', 'b_sinks', 'b_wo',
           'b_bo', 'rel_bias', 'mlp_norm', 'mlp_up', 'mlp_down', 'final_norm']
BIG = ['a_wqkv', 'a_wo', 'w_kv', 'b_wq', 'b_wo', 'mlp_up', 'mlp_down']
COL_SHARDED = ('a_wqkv', 'mlp_up')
SMALL = ['a_norm', 'kv_norm', 'b_kv', 'b_norm', 'b_bq', 'b_sinks', 'b_bo', 'rel_bias', 'mlp_norm', 'final_norm']


def _params(sem=None):
    return pltpu.CompilerParams(dimension_semantics=sem, vmem_limit_bytes=VMEM_LIMIT)


def _pick(n, cands):
    for c in cands:
        if n % c == 0:
            return c
    raise ValueError(n)


def _tile(n, want):
    return n if n <= want else _pick(n, (want, want // 2, want // 4))


MM_TILE_BUDGET = 36 * 1024 * 1024


def _row_tile(m, contraction, cols, streams):
    weight = 2 * contraction * cols * 2
    for rows in (2048, 1024, 512):
        if m % rows == 0 and weight + 2 * rows * (2 * contraction + cols * sum(streams)) <= MM_TILE_BUDGET:
            return rows
    return _tile(m, 512)


def mm_nn(name, a, w3, layer, epilogue, extras, out_dtypes, a_t=False, out_t=False):
    k, m = a.shape if a_t else a.shape[::-1]
    _, kw, n = w3.shape
    assert kw == k
    tn = _tile(n, 1024)
    tm = _row_tile(m, k, tn, [jnp.dtype(t).itemsize for t in out_dtypes]
                   + [e.dtype.itemsize for e in extras if e.size == m * n])
    ne, no = len(extras), len(out_dtypes)
    a_dim = 0 if a_t else 1

    def body(a_ref, w_ref, *rest):
        ex, outs = rest[:ne], rest[ne:ne + no]
        if out_t:
            acc = lax.dot_general(w_ref[...], a_ref[...], (((0,), (a_dim,)), ((), ())), preferred_element_type=F32)
        else:
            acc = lax.dot_general(a_ref[...], w_ref[...], (((a_dim,), (0,)), ((), ())), preferred_element_type=F32)
        for o, r in zip(outs, epilogue(acc, *[e[...] for e in ex])):
            o[...] = r.astype(o.dtype)

    if out_t:
        tile = pl.BlockSpec((tn, tm), lambda i, j: (j, i))
        vec = pl.BlockSpec((tn, 1), lambda i, j: (j, 0))
        out_shape = (n, m)
    else:
        tile = pl.BlockSpec((tm, tn), lambda i, j: (i, j))
        vec = pl.BlockSpec((1, tn), lambda i, j: (0, j))
        out_shape = (m, n)
    a_spec = pl.BlockSpec((k, tm), lambda i, j: (0, i)) if a_t else pl.BlockSpec((tm, k), lambda i, j: (i, 0))
    return pl.pallas_call(
        body, name=name, grid=(m // tm, n // tn),
        in_specs=[a_spec, pl.BlockSpec((None, k, tn), lambda i, j: (layer, 0, j))]
        + [tile if e.shape == out_shape else vec for e in extras],
        out_specs=[tile] * no,
        out_shape=[jax.ShapeDtypeStruct(out_shape, d) for d in out_dtypes],
        compiler_params=_params(("parallel", "parallel")),
    )(a, w3, *extras)


def mm_nt(name, dy, w3, layer, epilogue, extras, out_dtypes, a_t=False, out_t=False, n_sums=0, exchange=None):
    n, m = dy.shape if a_t else dy.shape[::-1]
    _, k, nw = w3.shape
    assert nw == n and not (out_t and n_sums)
    tko = _tile(k, 1024)
    tm = _row_tile(m, n, tko, [jnp.dtype(t).itemsize for t in out_dtypes]
                   + [e.dtype.itemsize for e in extras if e.size == m * k])
    ne, no = len(extras), len(out_dtypes)
    a_dim = 0 if a_t else 1

    def body(a_ref, w_ref, *rest):
        ex = rest[:ne]
        at = lambda step: (pl.program_id(0) == step[0]) & (pl.program_id(1) == step[1])
        results, _, start_carried, wait_carried = _carried(exchange, rest[ne:], no + n_sums, 0, at((0, 0)),
                                                           at((m // tm - 1, k // tko - 1)))
        outs, sums = results[:no], results[no:]
        start_carried()
        if out_t:
            acc = lax.dot_general(w_ref[...], a_ref[...], (((1,), (a_dim,)), ((), ())), preferred_element_type=F32)
        else:
            acc = lax.dot_general(a_ref[...], w_ref[...], (((a_dim,), (1,)), ((), ())), preferred_element_type=F32)
        res = epilogue(acc, *[e[...] for e in ex])
        for o, v in zip(outs, res):
            o[...] = v.astype(o.dtype)
        if n_sums:
            @pl.when(pl.program_id(0) == 0)
            def _():
                for o in sums:
                    o[...] = jnp.zeros_like(o)

            for o, v in zip(sums, res[no:]):
                o[...] += v
        wait_carried()

    if out_t:
        tile = pl.BlockSpec((tko, tm), lambda i, ko: (ko, i))
        out_shape = (k, m)
    else:
        tile = pl.BlockSpec((tm, tko), lambda i, ko: (i, ko))
        out_shape = (m, k)
    vec = pl.BlockSpec((1, tko), lambda i, ko: (0, ko))
    a_spec = pl.BlockSpec((n, tm), lambda i, ko: (0, i)) if a_t else pl.BlockSpec((tm, n), lambda i, ko: (i, 0))
    hbm = pl.BlockSpec(memory_space=pl.ANY)
    c_ins, c_outs, c_sems = (exchange.ins, exchange.out_shapes, exchange.sems) if exchange else ([], [], [])
    sequential = n_sums or exchange
    return pl.pallas_call(
        body, name=name, grid=(m // tm, k // tko),
        in_specs=[a_spec, pl.BlockSpec((None, tko, n), lambda i, ko: (layer, ko, 0))]
        + [tile if e.shape == out_shape else vec for e in extras] + [hbm] * len(c_ins),
        out_specs=[tile] * no + [vec] * n_sums + [hbm] * len(c_outs),
        out_shape=[jax.ShapeDtypeStruct(out_shape, d) for d in out_dtypes] + [jax.ShapeDtypeStruct((1, k), F32)] * n_sums
        + c_outs,
        scratch_shapes=c_sems,
        compiler_params=_params(("arbitrary" if sequential else "parallel", "arbitrary" if exchange else "parallel")),
    )(dy, w3, *extras, *c_ins)


def mm_tn(name, x, dy, gbuf, shape, layer, x_t=False, dy_t=False):
    k, s = x.shape if x_t else x.shape[::-1]
    _, kw, n = shape
    assert kw == k and dy.shape == ((n, s) if dy_t else (s, n))
    tkk = _tile(k, 512)
    tn = _tile(n, 1024)

    def body(x_ref, dy_ref, *rest):
        g_out = rest[-1]
        g_out[...] = lax.dot_general(x_ref[...], dy_ref[...], (((1 if x_t else 0,), (1 if dy_t else 0,)), ((), ())),
                                     preferred_element_type=F32).astype(g_out.dtype)

    prev = [] if gbuf is None else [gbuf]
    x_spec = pl.BlockSpec((tkk, s), lambda ki, j: (ki, 0)) if x_t else pl.BlockSpec((s, tkk), lambda ki, j: (0, ki))
    dy_spec = pl.BlockSpec((tn, s), lambda ki, j: (j, 0)) if dy_t else pl.BlockSpec((s, tn), lambda ki, j: (0, j))
    return pl.pallas_call(
        body, name=name, grid=(k // tkk, n // tn),
        in_specs=[x_spec, dy_spec] + [pl.BlockSpec(memory_space=pl.ANY)] * len(prev),
        out_specs=pl.BlockSpec((None, tkk, tn), lambda ki, j: (layer, ki, j)),
        out_shape=jax.ShapeDtypeStruct(shape, BF16),
        input_output_aliases={2: 0} if prev else {},
        compiler_params=_params(("parallel", "parallel")),
    )(x, dy, *prev)


def rms_fwd(name, h, g):
    s, d = h.shape
    tr = _pick(s, (ROW_TILE, 256, 128))

    def body(h_ref, g_ref, o_ref):
        x = h_ref[...]
        r = lax.rsqrt(jnp.mean(x * x, axis=-1, keepdims=True) + EPS)
        o_ref[...] = (x * r * g_ref[...]).astype(o_ref.dtype)

    return pl.pallas_call(
        body, name=name, grid=(s // tr,),
        in_specs=[pl.BlockSpec((tr, d), lambda i: (i, 0)), pl.BlockSpec((1, d), lambda i: (0, 0))],
        out_specs=pl.BlockSpec((tr, d), lambda i: (i, 0)),
        out_shape=jax.ShapeDtypeStruct((s, d), BF16),
        compiler_params=_params(("parallel",)),
    )(h, g.reshape(1, d))


def loss_head(h, g, target):
    s, d = h.shape
    tr = _pick(s, (ROW_TILE, 256, 128))

    def body(h_ref, g_ref, t_ref, dx_ref, dxb_ref, dg_ref, loss_ref):
        i = pl.program_id(0)
        x = h_ref[...]
        r = lax.rsqrt(jnp.mean(x * x, axis=-1, keepdims=True) + EPS)
        xh = x * r
        gw = g_ref[...]
        err = xh * gw - t_ref[...]
        dn_ = err * (1.0 / d)
        dyg = dn_ * gw
        dx = r * (dyg - xh * jnp.mean(dyg * xh, axis=-1, keepdims=True))
        dx_ref[...] = dx
        dxb_ref[...] = dx.astype(BF16)

        @pl.when(i == 0)
        def _():
            dg_ref[...] = jnp.zeros_like(dg_ref)
            loss_ref[...] = jnp.zeros_like(loss_ref)

        dg_ref[...] += jnp.sum(dn_ * xh, axis=0, keepdims=True)
        per_row = jnp.sum(err * err, axis=-1, keepdims=True) * (0.5 / d)
        loss_ref[...] += jnp.broadcast_to(jnp.sum(per_row, axis=0, keepdims=True), loss_ref.shape)

    row = pl.BlockSpec((tr, d), lambda i: (i, 0))
    vec = pl.BlockSpec((1, d), lambda i: (0, 0))
    return pl.pallas_call(
        body, name="loss_head", grid=(s // tr,),
        in_specs=[row, vec, row],
        out_specs=[row, row, vec, pl.BlockSpec((1, 128), lambda i: (0, 0))],
        out_shape=[jax.ShapeDtypeStruct((s, d), F32), jax.ShapeDtypeStruct((s, d), BF16),
                   jax.ShapeDtypeStruct((1, d), F32), jax.ShapeDtypeStruct((1, 128), F32)],
        compiler_params=_params(("arbitrary",)),
    )(h, g.reshape(1, d), target)


def colsum(name, x):
    s, n = x.shape
    tr = _pick(s, (ROW_TILE, 256, 128))

    def body(x_ref, o_ref):
        @pl.when(pl.program_id(0) == 0)
        def _():
            o_ref[...] = jnp.zeros_like(o_ref)

        o_ref[...] += jnp.sum(x_ref[...].astype(F32), axis=0, keepdims=True)

    return pl.pallas_call(
        body, name=name, grid=(s // tr,),
        in_specs=[pl.BlockSpec((tr, n), lambda i: (i, 0))],
        out_specs=pl.BlockSpec((1, n), lambda i: (0, 0)),
        out_shape=jax.ShapeDtypeStruct((1, n), F32),
        compiler_params=_params(("arbitrary",)),
    )(x)


def rowsum(name, x):
    n, s = x.shape
    ts = _pick(s, (1024, 512, 256, 128))

    def body(x_ref, o_ref):
        @pl.when(pl.program_id(0) == 0)
        def _():
            o_ref[...] = jnp.zeros_like(o_ref)

        o_ref[...] += jnp.sum(x_ref[...].astype(F32), axis=1, keepdims=True)

    return pl.pallas_call(
        body, name=name, grid=(s // ts,),
        in_specs=[pl.BlockSpec((n, ts), lambda i: (0, i))],
        out_specs=pl.BlockSpec((n, 1), lambda i: (0, 0)),
        out_shape=jax.ShapeDtypeStruct((n, 1), F32),
        compiler_params=_params(("arbitrary",)),
    )(x)[:, 0]


def _tri_rows(reverse):
    i = np.arange(SB_BK)
    tri = (i[None, :] >= i[:, None]) if reverse else (i[None, :] <= i[:, None])
    tri = np.concatenate([tri, tri], axis=1)
    return jnp.asarray(np.concatenate([tri, np.ones((8, 2 * SB_BK), bool)], axis=0), BF16)


def _hi_lo_rows(x):
    hi = x.astype(BF16)
    lo = (x - hi.astype(F32)).astype(BF16)
    return jnp.concatenate([hi, lo], axis=0)


def _softplus2(zs):
    neg_abs = lax.bitcast_convert_type(lax.bitcast_convert_type(zs, jnp.uint32) | jnp.uint32(0x80000000), F32)
    return jnp.maximum(zs, 0.0) + jnp.log2(1.0 + jnp.exp2(neg_abs))


def _pair_mask(first_rel_block, bq):
    key = lax.broadcasted_iota(jnp.int32, (2 * SB_BK, bq), 0) + first_rel_block * SB_BK
    qry = lax.broadcasted_iota(jnp.int32, (2 * SB_BK, bq), 1)
    return key < qry


def _row_of(table8, sub8, r):
    return jnp.sum(jnp.where(sub8 == r, table8, 0.0), axis=0, keepdims=True)


def _keys(j0):
    return pl.ds(pl.multiple_of(j0 * SB_BK, 2 * SB_BK), 2 * SB_BK)


class Carry:
    def __init__(self, build, ins, out_shapes, sems, then=None):
        self.build, self.ins, self.out_shapes, self.sems = build, list(ins), list(out_shapes), list(sems)
        self.then = then


def _together(a, b):
    assert a.then is None and b.then is None
    ni, no, ns = len(a.ins), len(a.out_shapes), len(a.sems)

    def build(ins, outs, *sems):
        return a.build(ins[:ni], outs[:no], *sems[:ns]) + b.build(ins[ni:], outs[no:], *sems[ns:])

    return Carry(build, a.ins + b.ins, a.out_shapes + b.out_shapes, a.sems + b.sems)


def _carried(carry, rest, n_out, n_scratch, first, last):
    n_ci = len(carry.ins) if carry else 0
    n_co = len(carry.out_shapes) if carry else 0
    cin, outs = rest[:n_ci], rest[n_ci:n_ci + n_out]
    cout = rest[n_ci + n_out:n_ci + n_out + n_co]
    scratch = rest[n_ci + n_out + n_co:n_ci + n_out + n_co + n_scratch]
    csems = rest[n_ci + n_out + n_co + n_scratch:]

    def start():
        if carry:
            @pl.when(first)
            def _():
                for cp in carry.build(cin, cout, *csems):
                    cp.start()

    def wait():
        if carry:
            @pl.when(last)
            def _():
                for cp in carry.build(cin, cout, *csems):
                    cp.wait()
                if carry.then:
                    second = carry.then(cin, cout, *csems)
                    for cp in second:
                        cp.start()
                    for cp in second:
                        cp.wait()

    return outs, scratch, start, wait


def _contract0(a, b):
    return lax.dot_general(a, b, (((0,), (0,)), ((), ())), preferred_element_type=F32)


def _contract1(a, b):
    return lax.dot_general(a, b, (((1,), (1,)), ((), ())), preferred_element_type=F32)


def sb_fwd(name, qkvt, exchange=None):
    nh, dh, s = qkvt.shape[0] // 3, qkvt.shape[1], qkvt.shape[2]
    bq = SB_BQ
    per_q = bq // SB_BK
    nkb = s // SB_BK
    assert s % bq == 0 and per_q == 4 and nkb % 8 == 0

    def body(q_ref, k_ref, v_ref, a_ref, *rest):
        head = pl.program_id(0)
        (o_ref, rtab_ref), (acc, zbuf, wbuf), start_carried, wait_carried = _carried(
            exchange, rest, 2, 3, head == 0, head == nh - 1)
        start_carried()
        tri = a_ref[...]
        sub8 = lax.broadcasted_iota(jnp.int32, (8, bq), 0)
        rtab_ref[...] = jnp.full(rtab_ref.shape, SB_UNSEEN, F32)
        kf = k_ref[...].astype(F32)
        k_max2 = jnp.max(jnp.sum(kf * kf, axis=0, keepdims=True), axis=1, keepdims=True)

        def query_block(i, _):
            lanes = pl.ds(pl.multiple_of(i * bq, bq), bq)
            qb = q_ref[:, lanes] * Q_SCALE
            acc[...] = jnp.zeros_like(acc)
            qf = qb.astype(F32)
            bound = jnp.sqrt(jnp.sum(qf * qf, axis=0, keepdims=True) * k_max2) * (1.001 * LOG2E)

            def scores(j0):
                return _contract0(k_ref[:, _keys(j0)], qb) * LOG2E

            def pair(j0, slot, run, rt8, mask, has_prev):
                zs = zbuf[slot]
                zbuf[1 - slot] = scores(jnp.maximum(j0 - 2, 0))
                if has_prev:
                    acc[...] += jnp.dot(v_ref[:, _keys(j0 + 2)], wbuf[1 - slot], preferred_element_type=F32)
                p = _softplus2(zs)
                if mask is not None:
                    p = jnp.where(mask, p, 0.0)
                cr1 = jnp.dot(tri, _hi_lo_rows(p[SB_BK:]), preferred_element_type=F32)
                cr0 = jnp.dot(tri, _hi_lo_rows(p[:SB_BK]), preferred_element_type=F32)
                run1 = run + cr1[SB_BK:SB_BK + 1]
                w = jnp.exp2(jnp.concatenate([zs[:SB_BK] - cr0[:SB_BK] - run1, zs[SB_BK:] - cr1[:SB_BK] - run],
                                             axis=0))
                if mask is not None:
                    w = jnp.where(mask, w, 0.0)
                wbuf[slot] = w.astype(BF16)
                rt8 = jnp.where(j0 % 8 == 6, SB_UNSEEN, rt8)
                rt8 = jnp.where(sub8 == (j0 + 1) % 8, run, jnp.where(sub8 == j0 % 8, run1, rt8))
                rtab_ref[pl.ds(pl.multiple_of((j0 // 8) * 8, 8), 8), lanes] = rt8
                return run1 + cr0[SB_BK:SB_BK + 1], rt8

            def alive(run):
                return jnp.min(run - bound) < SB_DEAD

            top = i * per_q
            zbuf[0] = scores(top + 2)
            state = (jnp.zeros((1, bq), F32), jnp.full((8, bq), SB_UNSEEN, F32))
            state = pair(top + 2, 0, *state, _pair_mask(2, bq), False)
            state = pair(top, 1, *state, _pair_mask(0, bq), True)

            def step(c):
                it, pairs, _, run, rt8 = c
                j0 = top - 2 - 4 * it
                run, rt8 = pair(j0, 0, run, rt8, None, True)
                go = alive(run)
                run, rt8 = lax.cond(go, lambda r, t: pair(j0 - 2, 1, r, t, None, True), lambda r, t: (r, t), run, rt8)
                return it + 1, pairs + 1 + go.astype(jnp.int32), go & alive(run), run, rt8

            pairs = lax.while_loop(lambda c: (c[0] < i) & c[2], step, (0, 0, alive(state[0]), *state))[1]
            acc[...] += jnp.dot(v_ref[:, _keys(top - 2 * pairs)], wbuf[(pairs + 1) % 2], preferred_element_type=F32)
            o_ref[:, lanes] = acc[...].astype(o_ref.dtype)
            return 0

        lax.fori_loop(0, s // bq, query_block, 0)
        wait_carried()

    def head_spec(offset, rows):
        return pl.BlockSpec((None, rows, s), lambda h: (h + offset, 0, 0))

    hbm = pl.BlockSpec(memory_space=pl.ANY)
    c_ins, c_outs, c_sems = (exchange.ins, exchange.out_shapes, exchange.sems) if exchange else ([], [], [])
    outs = pl.pallas_call(
        body, name=name, grid=(nh,),
        in_specs=[head_spec(0, dh), head_spec(nh, dh), head_spec(2 * nh, dh),
                  pl.BlockSpec((SB_BK + 8, 2 * SB_BK), lambda h: (0, 0))] + [hbm] * len(c_ins),
        out_specs=[head_spec(0, dh), head_spec(0, nkb)] + [hbm] * len(c_outs),
        out_shape=[jax.ShapeDtypeStruct((nh, dh, s), BF16), jax.ShapeDtypeStruct((nh, nkb, s), F32)] + c_outs,
        scratch_shapes=[pltpu.VMEM((dh, bq), F32), pltpu.VMEM((2, 2 * SB_BK, bq), F32),
                        pltpu.VMEM((2, 2 * SB_BK, bq), BF16)] + c_sems,
        compiler_params=_params(("arbitrary",)),
    )(qkvt, qkvt, qkvt, _tri_rows(True), *c_ins)
    return outs[0], outs[1], outs[2:]


def sb_bwd(name, qkvt, dot_, rtab, exchange=None):
    nh, dh, s = qkvt.shape[0] // 3, qkvt.shape[1], qkvt.shape[2]
    bq = SB_BQ
    per_q = bq // SB_BK
    nkb = s // SB_BK

    def body(qt_ref, kt_ref, vt_ref, dot_ref, rtab_ref, ar_ref, af_ref, *rest):
        head = pl.program_id(0)
        (dqkv_ref,), (dq_acc, dk_acc, dv_acc, zbuf, dwbuf, dzbuf, wbuf), start_carried, wait_carried = \
            _carried(exchange, rest, 1, 7, head == 0, head == nh - 1)
        dq_ref, dk_ref, dv_ref = dqkv_ref.at[0], dqkv_ref.at[1], dqkv_ref.at[2]
        start_carried()
        dk_acc[...] = jnp.zeros_like(dk_acc)
        dv_acc[...] = jnp.zeros_like(dv_acc)
        tri_rev = ar_ref[...][:SB_BK]
        tri_fwd = af_ref[...]
        sub8 = lax.broadcasted_iota(jnp.int32, (8, bq), 0)

        def query_block(i, _):
            lanes = pl.ds(pl.multiple_of(i * bq, bq), bq)
            qtb = qt_ref[:, lanes] * Q_SCALE
            dotb = dot_ref[:, lanes]
            dq_acc[...] = jnp.zeros_like(dq_acc)
            last_j = i * per_q + 2
            seen = jnp.max(jnp.where(rtab_ref[:, lanes] < 0.1 * SB_UNSEEN, 1.0, 0.0), axis=1, keepdims=True)
            pairs = jnp.clip((jnp.sum(seen).astype(jnp.int32) - per_q) // 2, 0, 2 * i)
            odd = pairs % 2
            first_j = i * per_q - 2 * pairs

            def issue(j0, slot):
                zbuf[slot] = _contract0(kt_ref[:, _keys(j0)], qtb) * LOG2E
                dwbuf[slot] = _contract0(vt_ref[:, _keys(j0)], dotb)

            def retire(j0, slot):
                keys = _keys(j0)
                dq_acc[...] += jnp.dot(kt_ref[:, keys], dzbuf[slot], preferred_element_type=F32)
                dk_acc[:, keys] += _contract1(qtb, dzbuf[slot])
                dv_acc[:, keys] += _contract1(dotb, wbuf[slot])

            def pair(j0, slot, g_run, mask):
                zs = zbuf[slot]
                dw = dwbuf[slot]
                issue(jnp.minimum(j0 + 2, last_j), 1 - slot)
                retire(jnp.maximum(j0 - 2, first_j), 1 - slot)
                p_raw = _softplus2(zs)
                p = p_raw if mask is None else jnp.where(mask, p_raw, 0.0)
                c0 = jnp.dot(tri_rev, _hi_lo_rows(p[:SB_BK]), preferred_element_type=F32)
                c1 = jnp.dot(tri_rev, _hi_lo_rows(p[SB_BK:]), preferred_element_type=F32)
                rt8 = rtab_ref[pl.ds(pl.multiple_of((j0 // 8) * 8, 8), 8), lanes]
                r0 = _row_of(rt8, sub8, j0 % 8)
                r1 = _row_of(rt8, sub8, (j0 + 1) % 8)
                w = jnp.exp2(jnp.concatenate([zs[:SB_BK] - c0 - r0, zs[SB_BK:] - c1 - r1], axis=0))
                if mask is not None:
                    w = jnp.where(mask, w, 0.0)
                g = w * dw
                gg0 = jnp.dot(tri_fwd, _hi_lo_rows(g[:SB_BK]), preferred_element_type=F32)
                gg1 = jnp.dot(tri_fwd, _hi_lo_rows(g[SB_BK:]), preferred_element_type=F32)
                g_run1 = g_run + gg0[SB_BK:SB_BK + 1]
                g_pre = jnp.concatenate([gg0[:SB_BK] + g_run, gg1[:SB_BK] + g_run1], axis=0)
                dz = g - jnp.exp2(zs - p_raw) * g_pre
                if mask is not None:
                    dz = jnp.where(mask, dz, 0.0)
                dzbuf[slot] = dz.astype(BF16)
                wbuf[slot] = w.astype(BF16)
                return g_run1 + gg1[SB_BK:SB_BK + 1]

            issue(first_j, odd)
            dzbuf[...] = jnp.zeros(dzbuf.shape, BF16)
            wbuf[...] = jnp.zeros(wbuf.shape, BF16)

            def step(it, g_run):
                g_run = pair(4 * it, 0, g_run, None)
                return pair(4 * it + 2, 1, g_run, None)

            g_run = lax.cond(odd == 1, lambda g: pair(first_j, 1, g, None), lambda g: g, jnp.zeros((1, bq), F32))
            g_run = lax.fori_loop(i - pairs // 2, i, step, g_run)
            g_run = pair(last_j - 2, 0, g_run, _pair_mask(0, bq))
            pair(last_j, 1, g_run, _pair_mask(2, bq))
            retire(last_j, 1)
            dq_ref[:, lanes] = (dq_acc[...] * Q_SCALE).astype(dq_ref.dtype)
            return 0

        lax.fori_loop(0, s // bq, query_block, 0)
        dk_ref[...] = dk_acc[...].astype(dk_ref.dtype)
        dv_ref[...] = dv_acc[...].astype(dv_ref.dtype)
        wait_carried()

    def head_spec(offset, rows):
        return pl.BlockSpec((None, rows, s), lambda h: (h + offset, 0, 0))

    aspec = pl.BlockSpec((SB_BK + 8, 2 * SB_BK), lambda h: (0, 0))
    pair_f32 = pltpu.VMEM((2, 2 * SB_BK, bq), F32)
    pair_bf16 = pltpu.VMEM((2, 2 * SB_BK, bq), BF16)
    hbm = pl.BlockSpec(memory_space=pl.ANY)
    c_ins, c_outs, c_sems = (exchange.ins, exchange.out_shapes, exchange.sems) if exchange else ([], [], [])
    outs = pl.pallas_call(
        body, name=name, grid=(nh,),
        in_specs=[head_spec(0, dh), head_spec(nh, dh), head_spec(2 * nh, dh), head_spec(0, dh), head_spec(0, nkb),
                  aspec, aspec] + [hbm] * len(c_ins),
        out_specs=[pl.BlockSpec((3, None, dh, s), lambda h: (0, h, 0, 0))] + [hbm] * len(c_outs),
        out_shape=[jax.ShapeDtypeStruct((3, nh, dh, s), BF16)] + c_outs,
        scratch_shapes=[pltpu.VMEM((dh, bq), F32), pltpu.VMEM((dh, s), F32), pltpu.VMEM((dh, s), F32),
                        pair_f32, pair_f32, pair_bf16, pair_bf16] + c_sems,
        compiler_params=_params(("arbitrary",)),
    )(qkvt, qkvt, qkvt, dot_, rtab, _tri_rows(True), _tri_rows(False), *c_ins)
    return outs[0], outs[1:]


SWA_QB = 4


def _band_valid():
    kj = np.arange(2 * WINDOW)[:, None]
    dist = (np.arange(8 * WINDOW)[None, :] % WINDOW) + WINDOW - kj
    inside = (dist >= 0) & (dist < WINDOW)
    return jnp.asarray(np.stack([inside & (kj >= WINDOW), inside]), F32)


def _swa_probs(qt, kt, bias_t, valid, sink):
    sc = jnp.where(valid > 0.5, _contract0(kt, qt) + bias_t, NEG_INF)
    mx = jnp.maximum(jnp.max(sc, axis=0, keepdims=True), sink)
    p = jnp.exp(sc - mx)
    p_sink = jnp.exp(sink - mx)
    inv = 1.0 / (jnp.sum(p, axis=0, keepdims=True) + p_sink)
    return p, p_sink, inv


def _band(i):
    return pl.ds(pl.multiple_of(i * WINDOW, WINDOW), 2 * WINDOW)


def _heads_to_lanes(blk):
    return jnp.concatenate([blk[r * HEAD_DIM:(r + 1) * HEAD_DIM] for r in range(8)], axis=1)


def _lanes_to_heads(t):
    return jnp.concatenate([t[:, r * WINDOW:(r + 1) * WINDOW] for r in range(8)], axis=0)


def swa_fwd(name, qt, kpt, vpt, bias_t, sink_row):
    d, s = qt.shape
    ng, dh, sp = kpt.shape
    rows, cols = d // ng, SWA_QB * WINDOW
    assert (s // WINDOW) % SWA_QB == 0

    def body(q_ref, k_ref, v_ref, bias_ref, valid_ref, sink_ref, o_ref):
        for u in range(SWA_QB):
            i = pl.program_id(1) * SWA_QB + u
            lanes = slice(u * WINDOW, (u + 1) * WINDOW)
            qb = _heads_to_lanes(q_ref[:, lanes]) * Q_SCALE
            p, _, inv = _swa_probs(qb, k_ref[:, _band(i)], bias_ref[...], valid_ref[jnp.minimum(i, 1)], sink_ref[...])
            o_t = jnp.dot(v_ref[:, _band(i)], p.astype(BF16), preferred_element_type=F32) * inv
            o_ref[:, lanes] = _lanes_to_heads(o_t).astype(o_ref.dtype)

    qspec = pl.BlockSpec((rows, cols), lambda g, i: (g, i))
    kspec = pl.BlockSpec((None, dh, sp), lambda g, i: (g, 0, 0))
    return pl.pallas_call(
        body, name=name, grid=(ng, s // cols),
        in_specs=[qspec, kspec, kspec, pl.BlockSpec((None, 2 * WINDOW, 8 * WINDOW), lambda g, i: (g, 0, 0)),
                  pl.BlockSpec((2, 2 * WINDOW, 8 * WINDOW), lambda g, i: (0, 0, 0)),
                  pl.BlockSpec((None, 1, 8 * WINDOW), lambda g, i: (g, 0, 0))],
        out_specs=qspec,
        out_shape=jax.ShapeDtypeStruct(qt.shape, BF16),
        compiler_params=_params(("parallel", "arbitrary")),
    )(qt, kpt, vpt, bias_t, _band_valid(), sink_row)


def swa_bwd(name, qt, kpt, vpt, bias_t, sink_row, dot_, dk_in, dv_in):
    d, s = qt.shape
    ng, dh, sp = kpt.shape
    rows, cols = d // ng, SWA_QB * WINDOW

    def body(q_ref, k_ref, v_ref, bias_ref, valid_ref, sink_ref, do_ref, dki_ref, dvi_ref,
             dq_ref, dk_ref, dv_ref, db_ref, ds_ref):
        @pl.when(pl.program_id(1) == 0)
        def _():
            dk_ref[...] = dki_ref[...]
            dv_ref[...] = dvi_ref[...]
            db_ref[...] = jnp.zeros_like(db_ref)
            ds_ref[...] = jnp.zeros_like(ds_ref)

        for u in range(SWA_QB):
            i = pl.program_id(1) * SWA_QB + u
            band = _band(i)
            lanes = slice(u * WINDOW, (u + 1) * WINDOW)
            qb = _heads_to_lanes(q_ref[:, lanes]) * Q_SCALE
            dob = _heads_to_lanes(do_ref[:, lanes])
            kt = k_ref[:, band]
            p, p_sink, inv = _swa_probs(qb, kt, bias_ref[...], valid_ref[jnp.minimum(i, 1)], sink_ref[...])
            p = p * inv
            dp = _contract0(v_ref[:, band], dob)
            delta = jnp.sum(p * dp, axis=0, keepdims=True)
            dsc = p * (dp - delta)
            ds_ref[...] -= p_sink * inv * delta
            db_ref[...] += dsc
            dscb = dsc.astype(BF16)
            dq_t = jnp.dot(kt, dscb, preferred_element_type=F32) * Q_SCALE
            dq_ref[:, lanes] = _lanes_to_heads(dq_t).astype(dq_ref.dtype)
            dk_ref[:, band] += _contract1(qb, dscb)
            dv_ref[:, band] += _contract1(dob, p.astype(BF16))

    qspec = pl.BlockSpec((rows, cols), lambda g, i: (g, i))
    kspec = pl.BlockSpec((None, dh, sp), lambda g, i: (g, 0, 0))
    bspec = pl.BlockSpec((None, 2 * WINDOW, 8 * WINDOW), lambda g, i: (g, 0, 0))
    sspec = pl.BlockSpec((None, 1, 8 * WINDOW), lambda g, i: (g, 0, 0))
    return pl.pallas_call(
        body, name=name, grid=(ng, s // cols),
        in_specs=[qspec, kspec, kspec, bspec, pl.BlockSpec((2, 2 * WINDOW, 8 * WINDOW), lambda g, i: (0, 0, 0)), sspec,
                  qspec, kspec, kspec],
        out_specs=[qspec, kspec, kspec, bspec, sspec],
        out_shape=[jax.ShapeDtypeStruct(qt.shape, BF16), jax.ShapeDtypeStruct(kpt.shape, F32),
                   jax.ShapeDtypeStruct(kpt.shape, F32), jax.ShapeDtypeStruct(bias_t.shape, F32),
                   jax.ShapeDtypeStruct(sink_row.shape, F32)],
        compiler_params=_params(("parallel", "arbitrary")),
    )(qt, kpt, vpt, bias_t, _band_valid(), sink_row, dot_, dk_in, dv_in)


def _bucket_onehot():
    qi = np.arange(WINDOW)[:, None]
    kj = np.arange(2 * WINDOW)[None, :]
    n = np.maximum(qi + WINDOW - kj, 0)
    max_exact = N_BUCKETS // 2
    nf = np.maximum(n, 1).astype(np.float64)
    val = np.log(nf / max_exact) / math.log(WINDOW / max_exact) * (N_BUCKETS - max_exact)
    assert np.all(np.abs(val - np.round(val))[(n > max_exact) & (n < WINDOW)] > 1e-3)
    large = np.minimum(max_exact + val.astype(np.int64), N_BUCKETS - 1)
    bucket = np.where(n < max_exact, n, large).reshape(-1)
    onehot = np.zeros((128, bucket.size), np.float32)
    onehot[bucket, np.arange(bucket.size)] = 1.0
    return onehot


def _split3(x):
    a = x.astype(BF16)
    r = x - a.astype(F32)
    b = r.astype(BF16)
    c = (r - b.astype(F32)).astype(BF16)
    return a, b, c


def bias_table(rel_bias):
    nh = rel_bias.shape[1]
    oh = jnp.asarray(_bucket_onehot(), BF16)
    n = oh.shape[1]
    tn = 4096
    rb = jnp.zeros((nh, 128), F32).at[:, :N_BUCKETS].set(rel_bias.T)

    def body(rb_ref, oh_ref, o_ref):
        o_ref[...] = sum(jnp.dot(t, oh_ref[...], preferred_element_type=F32) for t in _split3(rb_ref[...]))

    return pl.pallas_call(
        body, name="bias_table", grid=(n // tn,),
        in_specs=[pl.BlockSpec((nh, 128), lambda i: (0, 0)), pl.BlockSpec((128, tn), lambda i: (0, i))],
        out_specs=pl.BlockSpec((nh, tn), lambda i: (0, i)),
        out_shape=jax.ShapeDtypeStruct((nh, n), F32),
        compiler_params=_params(("parallel",)),
    )(rb, oh)


def bias_table_grad(db0, db1):
    nh, n = db0.shape
    oh = jnp.asarray(_bucket_onehot(), BF16)
    tn = 4096

    def body(a_ref, b_ref, oh_ref, o_ref):
        @pl.when(pl.program_id(0) == 0)
        def _():
            o_ref[...] = jnp.zeros_like(o_ref)

        o_ref[...] += sum(lax.dot_general(t, oh_ref[...], (((1,), (1,)), ((), ())), preferred_element_type=F32)
                          for t in _split3(a_ref[...] + b_ref[...]))

    blk = pl.BlockSpec((nh, tn), lambda i: (0, i))
    return pl.pallas_call(
        body, name="bias_table_grad", grid=(n // tn,),
        in_specs=[blk, blk, pl.BlockSpec((128, tn), lambda i: (0, i))],
        out_specs=pl.BlockSpec((nh, 128), lambda i: (0, 0)),
        out_shape=jax.ShapeDtypeStruct((nh, 128), F32),
        compiler_params=_params(("arbitrary",)),
    )(db0, db1, oh)


def _owner_view(ref, name, d):
    if name == 'a_norm':
        return ref.at[d]
    if name in COL_SHARDED:
        n = ref.shape[2] // N_DEV
        return ref.at[:, :, pl.ds(pl.multiple_of(d * n, 128), n)]
    return ref.at[:, d]


def _place():
    return lax.axis_index("x"), lax.axis_index("y"), lax.axis_index("c")


def _dev(p):
    return 4 * p[0] + 2 * p[1] + p[2]


def _remote(src, dst, send_sem, recv_sem, to):
    return pltpu.make_async_remote_copy(src_ref=src, dst_ref=dst, send_sem=send_sem, recv_sem=recv_sem,
                                        device_id=to, device_id_type=MESH)


def _dma_sems(*shapes):
    return [pltpu.SemaphoreType.DMA(sh) for sh in shapes]


def comm_call(name, build, ins, out_shapes, sems):
    n_in, n_out = len(ins), len(out_shapes)

    def body(*refs):
        copies = build(refs[:n_in], refs[n_in:n_in + n_out], *refs[n_in + n_out:])
        for cp in copies:
            cp.start()
        for cp in copies:
            cp.wait()

    hbm = pl.BlockSpec(memory_space=pl.ANY)
    return pl.pallas_call(
        body, name=name, in_specs=[hbm] * n_in, out_specs=[hbm] * n_out, out_shape=list(out_shapes),
        scratch_shapes=sems,
    )(*ins)


def all_gather_weights(names, shards, full_shapes):
    n = len(names)

    def body(*refs):
        ins, outs = refs[:n], refs[n:2 * n]
        send_sems, recv_sems, local_sems = refs[2 * n:]
        x, y, c = _place()
        me, sibling = (x, y, c), (x, y, 1 - c)
        chips = [(1 - x, y), (x, 1 - y), (1 - x, 1 - y)]

        def copy(t, k, block, to, src=None):
            dst = _owner_view(outs[t], names[t], _dev(block))
            return _remote(dst if src is None else src, dst, send_sems.at[t, k], recv_sems.at[t, k], to)

        mine = [pltpu.make_async_copy(ins[t], _owner_view(outs[t], names[t], _dev(me)), local_sems.at[t])
                for t in range(n)]
        for cp in mine:
            cp.start()
        first = []
        for t in range(n):
            first.append(copy(t, 0, me, sibling, src=ins[t]))
            first += [copy(t, 1 + j, me, (*chip, c), src=ins[t]) for j, chip in enumerate(chips)]
        for cp in first:
            cp.start()
        passed = []
        for j, chip in enumerate(chips):
            for t in range(n):
                copy(t, 1 + j, (*chip, c), me).wait_recv()
                fwd = copy(t, 4 + j, (*chip, c), sibling)
                fwd.start()
                passed.append(fwd)
        for t in range(n):
            copy(t, 0, sibling, me).wait_recv()
            for j, chip in enumerate(chips):
                copy(t, 4 + j, (*chip, 1 - c), me).wait_recv()
        for cp in first + passed:
            cp.wait_send()
        for cp in mine:
            cp.wait()

    hbm = pl.BlockSpec(memory_space=pl.ANY)
    return pl.pallas_call(
        body, name="all_gather_layer0",
        in_specs=[hbm] * n, out_specs=[hbm] * n,
        out_shape=[jax.ShapeDtypeStruct(full_shapes[t], shards[t].dtype) for t in range(n)],
        scratch_shapes=_dma_sems((n, 7), (n, 7), (n,)),
    )(*shards)


def ag_direct(names, shards, full_shapes):
    n = len(names)

    def build(ins, outs, send_sems, recv_sems, local_sems, fwd_send_sems, fwd_recv_sems):
        x, y, c = _place()
        peers = [(x, y, 1 - c), (1 - x, y, c), (x, 1 - y, c), (1 - x, 1 - y, c)]
        copies = []
        for t in range(n):
            dst = _owner_view(outs[t], names[t], _dev((x, y, c)))
            copies.append(pltpu.make_async_copy(ins[t], dst, local_sems.at[t]))
            copies += [_remote(ins[t], dst, send_sems.at[t, k], recv_sems.at[t, k], to) for k, to in enumerate(peers)]
        return copies

    def forward(ins, outs, send_sems, recv_sems, local_sems, fwd_send_sems, fwd_recv_sems):
        x, y, c = _place()
        copies = []
        for t in range(n):
            for k, chip in enumerate([(1 - x, y), (x, 1 - y), (1 - x, 1 - y)]):
                view = _owner_view(outs[t], names[t], _dev((*chip, c)))
                copies.append(_remote(view, view, fwd_send_sems.at[t, k], fwd_recv_sems.at[t, k], (x, y, 1 - c)))
        return copies

    return Carry(build, shards, [jax.ShapeDtypeStruct(full_shapes[t], shards[t].dtype) for t in range(n)],
                 _dma_sems((n, 4), (n, 4), (n,), (n, 3), (n, 3)), then=forward)


def sibling_exchange(names, grads, part_shapes):
    n = len(names)

    def build(ins, outs, send_sems, recv_sems):
        x, y, c = _place()
        return [_remote(_owner_view(ins[t], names[t], 2 * q + 1 - c), outs[t].at[q], send_sems.at[t, q],
                        recv_sems.at[t, q], (x, y, 1 - c)) for t in range(n) for q in range(4)]

    return Carry(build, grads, [jax.ShapeDtypeStruct((4,) + part_shapes[t], BF16) for t in range(n)],
                 _dma_sems((n, 4), (n, 4)))


def chip_exchange(names, parts, part_shapes):
    n = len(names)

    def build(ins, outs, send_sems, recv_sems):
        x, y, c = _place()
        chips = [(1 - x, y), (x, 1 - y), (1 - x, 1 - y)]
        return [_remote(ins[t].at[2 * chip[0] + chip[1]], outs[t].at[k], send_sems.at[t, k], recv_sems.at[t, k],
                        (*chip, c)) for t in range(n) for k, chip in enumerate(chips)]

    return Carry(build, parts, [jax.ShapeDtypeStruct((3,) + part_shapes[t], BF16) for t in range(n)],
                 _dma_sems((n, 3), (n, 3)))


def all_gather_rows(x):
    r, w = x.shape

    def body(x_ref, out_ref, send_sems, recv_sems, local_sem):
        px, py, pc = _place()
        me = 4 * px + 2 * py + pc
        mine = pltpu.make_async_copy(x_ref, out_ref.at[me], local_sem)
        mine.start()
        copies = []
        for k in range(1, N_DEV):
            peer = (px ^ (k >> 2), py ^ ((k >> 1) & 1), pc ^ (k & 1))
            copies.append(pltpu.make_async_remote_copy(
                src_ref=x_ref, dst_ref=out_ref.at[me], send_sem=send_sems.at[k - 1], recv_sem=recv_sems.at[k - 1],
                device_id=peer, device_id_type=MESH))
        for cp in copies:
            cp.start()
        for k in range(1, N_DEV):
            peer_idx = me ^ k
            pltpu.make_async_remote_copy(
                src_ref=x_ref, dst_ref=out_ref.at[peer_idx], send_sem=send_sems.at[k - 1],
                recv_sem=recv_sems.at[k - 1], device_id=(px, py, pc), device_id_type=MESH).wait_recv()
        for cp in copies:
            cp.wait_send()
        mine.wait()

    vmem = pl.BlockSpec(memory_space=pltpu.VMEM)
    return pl.pallas_call(
        body, name="all_gather_small_grads",
        in_specs=[vmem], out_specs=vmem,
        out_shape=jax.ShapeDtypeStruct((N_DEV, r, w), x.dtype),
        scratch_shapes=[pltpu.SemaphoreType.DMA((N_DEV - 1,)), pltpu.SemaphoreType.DMA((N_DEV - 1,)),
                        pltpu.SemaphoreType.DMA],
    )(x)


def _adamw(w, g, m, v):
    m = ADAM_B1 * m + (1.0 - ADAM_B1) * g
    v = ADAM_B2 * v + (1.0 - ADAM_B2) * (g * g)
    m_hat = m / (1.0 - ADAM_B1 ** ADAM_STEP)
    v_hat = v / (1.0 - ADAM_B2 ** ADAM_STEP)
    return -ADAM_LR * (m_hat / (jnp.sqrt(v_hat) + ADAM_EPS) + ADAM_WD * w), m, v


def sibling_sum(name, col, grads, recv, core):
    _, nl, rows, cols = recv.shape
    tr = _tile(rows, 1024)
    rspec = pl.BlockSpec((None, None, tr, cols), lambda q, l, i, c_ref: (q, l, i, 0))
    if col:
        gspec = pl.BlockSpec((None, tr, cols), lambda q, l, i, c_ref: (l, i, 2 * q + c_ref[0]))
    else:
        gspec = pl.BlockSpec((None, None, tr, cols), lambda q, l, i, c_ref: (l, 2 * q + c_ref[0], i, 0))

    def body(c_ref, g_ref, r_ref, o_ref):
        del c_ref
        o_ref[...] = (g_ref[...].astype(F32) + r_ref[...].astype(F32)).astype(BF16)

    return pl.pallas_call(
        body, name=name,
        grid_spec=pltpu.PrefetchScalarGridSpec(num_scalar_prefetch=1, grid=(4, nl, rows // tr),
                                               in_specs=[gspec, rspec], out_specs=rspec),
        out_shape=jax.ShapeDtypeStruct(recv.shape, BF16),
        compiler_params=_params(("parallel", "parallel", "parallel")),
    )(core.reshape(1), grads, recv)


def reduce_adamw(name, parts, recv, chip, w, m, v, l0, prev):
    _, nl, rows, cols = parts.shape
    tr = _tile(rows, 1024)

    def body(q_ref, p_ref, r_ref, w_ref, m_ref, v_ref, *rest):
        del q_ref
        g_out, d_out, m_out, v_out = rest[-4:]
        g = ((p_ref[...].astype(F32) + r_ref[0].astype(F32)) + r_ref[1].astype(F32)) + r_ref[2].astype(F32)
        d, mn, vn = _adamw(w_ref[...], g, m_ref[...], v_ref[...])
        g_out[...] = g
        d_out[...] = d
        m_out[...] = mn
        v_out[...] = vn

    blk = pl.BlockSpec((None, tr, cols), lambda l, i, q_ref: (l0 + l, i, 0))
    prev = list(prev) if prev else []
    return pl.pallas_call(
        body, name=name,
        grid_spec=pltpu.PrefetchScalarGridSpec(
            num_scalar_prefetch=1, grid=(nl, rows // tr),
            in_specs=[pl.BlockSpec((None, None, tr, cols), lambda l, i, q_ref: (q_ref[0], l, i, 0)),
                      pl.BlockSpec((3, None, tr, cols), lambda l, i, q_ref: (0, l, i, 0)), blk, blk, blk]
            + [pl.BlockSpec(memory_space=pl.ANY)] * len(prev),
            out_specs=[blk] * 4),
        out_shape=[jax.ShapeDtypeStruct(w.shape, F32)] * 4,
        input_output_aliases={6 + i: i for i in range(len(prev))},
        compiler_params=_params(("parallel", "parallel")),
    )(chip.reshape(1), parts, recv, w, m, v, *prev)


def small_adamw(name, gathered, w, m, v):
    _, r, c = gathered.shape

    def body(ga_ref, w_ref, m_ref, v_ref, g_out, d_out, m_out, v_out):
        g = ga_ref[0]
        for d in range(1, N_DEV):
            g = g + ga_ref[d]
        dl, mn, vn = _adamw(w_ref[...], g, m_ref[...], v_ref[...])
        g_out[...] = g
        d_out[...] = dl
        m_out[...] = mn
        v_out[...] = vn

    return pl.pallas_call(
        body, name=name,
        out_shape=[jax.ShapeDtypeStruct((r, c), F32)] * 4,
        compiler_params=_params(),
    )(gathered, w, m, v)


def _rms(x, g):
    return x * lax.rsqrt(jnp.mean(x * x, axis=-1, keepdims=True) + EPS) * g


def _rms_bwd_epilogue(dn, x, dres, g):
    r = lax.rsqrt(jnp.mean(x * x, axis=-1, keepdims=True) + EPS)
    xh = x * r
    dyg = dn * g
    dx = dres + r * (dyg - xh * jnp.mean(dyg * xh, axis=-1, keepdims=True))
    return dx, dx, jnp.sum(dn * xh, axis=0, keepdims=True), jnp.sum(dx, axis=0, keepdims=True)


def _residual_then_norms(n_terms):
    def epilogue(acc, *ex):
        h = acc
        for t in ex[:n_terms]:
            h = h + t
        return (h,) + tuple(_rms(h, g) for g in ex[n_terms:])
    return epilogue


def local_step(x, target, small, ex):
    s, d = x.shape
    n_a, n_b = small['a_norm'].shape[0], small['b_norm'].shape[0]
    sg = {}
    gb = {}

    def fwd_mm(name, a, wname, layer, epilogue, extras, out_dtypes, **kw):
        return mm_nn(name, a, *ex.weight(wname, layer), epilogue, extras, out_dtypes, **kw)

    def dx_mm(name, dy, wname, layer, epilogue, extras, out_dtypes, **kw):
        return mm_nt(name, dy, *ex.weight(wname, layer), epilogue, extras, out_dtypes, **kw)

    def dw_mm(name, a, dy, wname, layer, **kw):
        key, slab, shape = ex.grad(wname, layer)
        gb[key] = mm_tn(name, a, dy, gb.get(key), shape, slab, **kw)

    plain = lambda acc: (acc,)
    plus_col = lambda acc, b: (acc + b,)

    bias_flat = bias_table(small['rel_bias'])
    bias_t = bias_flat.reshape(2, 8, WINDOW, 2 * WINDOW).transpose(0, 3, 1, 2).reshape(2, 2 * WINDOW, 8 * WINDOW)
    sink_rows = [jnp.repeat(small['b_sinks'][j], WINDOW).reshape(2, 1, 8 * WINDOW) for j in range(n_b)]

    gain = lambda g: g.reshape(1, -1)

    def mlp_fwd(h, n2, layer, next_gains):
        u, a = fwd_mm(f"mlp_up_fwd{layer}", n2, 'mlp_up', layer,
                      lambda acc: (acc, jnp.square(jnp.maximum(acc, 0.0))), (), (BF16, BF16))
        h2, *nexts = fwd_mm(f"mlp_down_fwd{layer}", a, 'mlp_down', layer, _residual_then_norms(1),
                            (h, *[gain(g) for g in next_gains]), (F32,) + (BF16,) * len(next_gains))
        return h2, nexts, (n2, u, a)

    h = x
    saved = []
    n1 = rms_fwd("a_norm_fwd0", h, small['a_norm'][0])
    for l in range(n_a):
        (qkvt,) = fwd_mm(f"a_qkv_fwd{l}", n1, 'a_wqkv', l, plain, (), (BF16,), out_t=True)
        qkvt = qkvt.reshape(3 * d // HEAD_DIM, HEAD_DIM, s)
        o_t, rtab, carried = sb_fwd(f"sb_fwd{l}", qkvt, ex.fwd_carry(l))
        ex.fwd_done(l, carried)
        o_t = o_t.reshape(d, s)
        h_mid, n2 = fwd_mm(f"a_wo_fwd{l}", o_t, 'a_wo', l, _residual_then_norms(1),
                           (h, gain(small['mlp_norm'][l])), (F32, BF16), a_t=True)
        next_gains = [small['a_norm'][l + 1]] if l + 1 < n_a else [small['b_norm'][0], small['kv_norm']]
        h_out, nexts, mlp_saved = mlp_fwd(h_mid, n2, l, next_gains)
        saved.append((h, n1, qkvt, o_t, rtab, h_mid, mlp_saved))
        h, n1 = h_out, nexts[0]
    h_kv, nkv = h, nexts[1]
    (kvt,) = fwd_mm("kv_fwd", nkv, 'w_kv', 0, plus_col, (small['b_kv'].reshape(-1, 1),), (BF16,), out_t=True)
    kvt = kvt.reshape(2, 2, HEAD_DIM, s)
    kpt, vpt = (jnp.pad(t, ((0, 0), (0, 0), (WINDOW, 0))) for t in (kvt[0], kvt[1]))
    for j in range(n_b):
        layer = n_a + j
        (qbt,) = fwd_mm(f"b_q_fwd{j}", n1, 'b_wq', j, plus_col, (small['b_bq'][j].reshape(-1, 1),), (BF16,),
                        out_t=True)
        o_t = swa_fwd(f"swa_fwd{j}", qbt, kpt, vpt, bias_t, sink_rows[j])
        h_mid, n2 = fwd_mm(f"b_wo_fwd{j}", o_t, 'b_wo', j, _residual_then_norms(2),
                           (h, gain(small['b_bo'][j]), gain(small['mlp_norm'][layer])), (F32, BF16), a_t=True)
        h_out, nexts, mlp_saved = mlp_fwd(h_mid, n2, layer, [small['b_norm'][j + 1]] if j + 1 < n_b else [])
        saved.append((h, n1, qbt, o_t, h_mid, mlp_saved))
        h, n1 = h_out, (nexts[0] if nexts else None)

    dh, dhb, dg_final, loss_b = loss_head(h, small['final_norm'], target)
    sg['final_norm'] = dg_final[0]
    sg['mlp_norm'] = [None] * (n_a + n_b)

    def mlp_bwd(dh, dhb, h_mid, mlp_saved, layer):
        n2, u, a = mlp_saved
        du, *carried = dx_mm(f"mlp_down_dx{layer}", dhb, 'mlp_down', layer,
                             lambda acc, uu: (acc * (2.0 * jnp.maximum(uu.astype(F32), 0.0)),), (u,), (BF16,),
                             exchange=ex.mlp_carry(layer, gb))
        ex.mlp_done(layer, carried)
        dw_mm(f"mlp_down_dw{layer}", a, dhb, 'mlp_down', layer)
        dh2, dh2b, dg, cs = dx_mm(f"mlp_up_dx{layer}", du, 'mlp_up', layer, _rms_bwd_epilogue,
                                  (h_mid, dh, gain(small['mlp_norm'][layer])), (F32, BF16), n_sums=2)
        dw_mm(f"mlp_up_dw{layer}", n2, du, 'mlp_up', layer)
        sg['mlp_norm'][layer] = dg[0]
        return dh2, dh2b, cs

    dkp = jnp.zeros(kpt.shape, F32)
    dvp = jnp.zeros(vpt.shape, F32)
    sg['b_norm'], sg['b_bq'], sg['b_bo'], sg['b_sinks'] = [None] * n_b, [None] * n_b, [None] * n_b, [None] * n_b
    dbias = [None] * n_b
    for j in reversed(range(n_b)):
        layer = n_a + j
        h_in, n1, qbt, o_t, h_mid, mlp_saved = saved[layer]
        dh, dhb, cs = mlp_bwd(dh, dhb, h_mid, mlp_saved, layer)
        sg['b_bo'][j] = cs[0]
        (do_t,) = dx_mm(f"b_wo_dx{j}", dhb, 'b_wo', j, plain, (), (BF16,), out_t=True)
        dw_mm(f"b_wo_dw{j}", o_t, dhb, 'b_wo', j, x_t=True)
        dq_t, dkp, dvp, dbias[j], dsink = swa_bwd(f"swa_bwd{j}", qbt, kpt, vpt, bias_t, sink_rows[j], do_t, dkp, dvp)
        sg['b_sinks'][j] = colsum(f"sink_grad{j}", dsink.reshape(16, WINDOW).T)[0]
        sg['b_bq'][j] = rowsum(f"b_bq_grad{j}", dq_t)
        dh, dhb, dg, _ = dx_mm(f"b_q_dx{j}", dq_t, 'b_wq', j, _rms_bwd_epilogue,
                               (h_in, dh, gain(small['b_norm'][j])), (F32, BF16), a_t=True, n_sums=2)
        dw_mm(f"b_q_dw{j}", n1, dq_t, 'b_wq', j, dy_t=True)
        sg['b_norm'][j] = dg[0]
    unt = lambda t: t.reshape(2, 2 * WINDOW, 8, WINDOW).transpose(0, 2, 3, 1).reshape(bias_flat.shape)
    sg['rel_bias'] = bias_table_grad(unt(dbias[0]), unt(dbias[1]))[:, :N_BUCKETS].T

    dkv_t = jnp.concatenate([dkp[:, :, WINDOW:], dvp[:, :, WINDOW:]], axis=0).reshape(-1, s)
    sg['b_kv'] = rowsum("b_kv_grad", dkv_t)
    dkvb = dkv_t.astype(BF16)
    dh, dhb, dg, _ = dx_mm("kv_dx", dkvb, 'w_kv', 0, _rms_bwd_epilogue, (h_kv, dh, gain(small['kv_norm'])),
                           (F32, BF16), a_t=True, n_sums=2)
    dw_mm("kv_dw", nkv, dkvb, 'w_kv', 0, dy_t=True)
    sg['kv_norm'] = dg[0]

    sg['a_norm'] = [None] * n_a
    for l in reversed(range(n_a)):
        h_in, n1, qkvt, o_t, rtab, h_mid, mlp_saved = saved[l]
        dh, dhb, _ = mlp_bwd(dh, dhb, h_mid, mlp_saved, l)
        (do_t,) = dx_mm(f"a_wo_dx{l}", dhb, 'a_wo', l, plain, (), (BF16,), out_t=True)
        dw_mm(f"a_wo_dw{l}", o_t, dhb, 'a_wo', l, x_t=True)
        dqkv_t, carried = sb_bwd(f"sb_bwd{l}", qkvt, do_t.reshape(d // HEAD_DIM, HEAD_DIM, s), rtab,
                                 ex.bwd_carry(l, gb))
        ex.bwd_done(l, carried)
        dqkv_t = dqkv_t.reshape(3 * d, s)
        dw_mm(f"a_qkv_dw{l}", n1, dqkv_t, 'a_wqkv', l, dy_t=True)
        dh, dhb, dg, _, *carried = dx_mm(f"a_qkv_dx{l}", dqkv_t, 'a_wqkv', l, _rms_bwd_epilogue,
                                         (h_in, dh, gain(small['a_norm'][l])), (F32, BF16), a_t=True, n_sums=2,
                                         exchange=ex.last_carry(gb) if l == 0 else None)
        if l == 0:
            ex.last_done(carried)
        sg['a_norm'][l] = dg[0]

    small_grads = {
        'a_norm': jnp.stack(sg['a_norm']), 'kv_norm': sg['kv_norm'], 'b_kv': sg['b_kv'],
        'b_norm': jnp.stack(sg['b_norm']), 'b_bq': jnp.stack(sg['b_bq']), 'b_sinks': jnp.stack(sg['b_sinks']),
        'b_bo': jnp.stack(sg['b_bo']), 'rel_bias': sg['rel_bias'], 'mlp_norm': jnp.stack(sg['mlp_norm']),
        'final_norm': sg['final_norm'],
    }
    return loss_b, dh, gb, small_grads


def _full_shape(name, shard_shape):
    if name in COL_SHARDED:
        return shard_shape[:2] + (N_DEV * shard_shape[2],)
    nl, r, n = shard_shape
    return (nl, N_DEV, r, n)


def _as_w3_shape(name, shard_shape):
    full = _full_shape(name, shard_shape)
    return full if name in COL_SHARDED else (full[0], full[1] * full[2], full[3])


def _as_w3(name, full):
    if name in COL_SHARDED:
        return full
    nl, nd, r, n = full.shape
    return full.reshape(nl, nd * r, n)


AG_GROUPS = {
    0: (('a_wqkv', 0, 1),),
    1: (('a_wo', 0, 2), ('mlp_up', 0, 2), ('mlp_down', 0, 2), ('a_wqkv', 1, 1)),
    2: (('mlp_up', 2, 2), ('mlp_down', 2, 2), ('b_wq', 0, 2), ('b_wo', 0, 2), ('w_kv', 0, 1)),
}
RS_GROUPS = {
    'A': (('mlp_up', 2, 2), ('mlp_down', 2, 2), ('b_wq', 0, 2), ('b_wo', 0, 2), ('w_kv', 0, 1)),
    'B1': (('a_wo', 1, 1), ('mlp_up', 1, 1), ('mlp_down', 1, 1)),
    'B2': (('a_wqkv', 1, 1),),
    'B3': (('a_wo', 0, 1), ('mlp_up', 0, 1), ('mlp_down', 0, 1)),
    'C': (('a_wqkv', 0, 1),),
}
UNDER_MLP = {1: 'A', 0: 'B2'}
SIBLING_UNDER_SB_BWD = {1: 'B1'}
UNDER_SB_BWD = {1: ('A',), 0: ('B1', 'B2', 'B3')}


class _Exchanges:
    def __init__(self, full0, shards, core, chip, w3, m3, v3):
        self.wbuf = {0: {n: _as_w3(n, full0[n]) for n, _, _ in AG_GROUPS[0]}}
        self.shards, self.core, self.chip = shards, core, chip
        self.w3, self.m3, self.v3 = w3, m3, v3
        self.shard_dims = {n: w3[n].shape[1:] for n in BIG}
        self.parts = {}
        self.gfull = {}
        self.out = {}

    def weight(self, name, layer):
        for group, members in AG_GROUPS.items():
            for n, l0, nl in members:
                if n == name and l0 <= layer < l0 + nl:
                    return self.wbuf[group][name], layer - l0
        raise KeyError((name, layer))

    def fwd_carry(self, layer):
        names = [n for n, _, _ in AG_GROUPS[layer + 1]]
        shards = [self.shards[layer + 1][n] for n in names]
        return ag_direct(names, shards, [_full_shape(n, sh.shape) for n, sh in zip(names, shards)])

    def fwd_done(self, layer, carried):
        names = [n for n, _, _ in AG_GROUPS[layer + 1]]
        self.wbuf[layer + 1] = {n: _as_w3(n, f) for n, f in zip(names, carried)}

    def grad(self, name, layer):
        for group, members in RS_GROUPS.items():
            for n, l0, nl in members:
                if n == name and l0 <= layer < l0 + nl:
                    return (group, name), layer - l0, _as_w3_shape(name, (nl,) + self.shard_dims[name])
        raise KeyError((name, layer))

    def _members(self, group):
        names = [n for n, _, _ in RS_GROUPS[group]]
        return names, [(nl,) + self.shard_dims[n] for n, _, nl in RS_GROUPS[group]]

    def _sibling_carry(self, group, gb):
        names, shapes = self._members(group)
        self.gfull[group] = [gb[(group, n)].reshape(_full_shape(n, sh)) for n, sh in zip(names, shapes)]
        return sibling_exchange(names, self.gfull[group], shapes)

    def _sibling_done(self, group, recv):
        names, _ = self._members(group)
        self.parts[group] = [sibling_sum(f"rs_sibling_sum_{group}_{n}", n in COL_SHARDED, g, r, self.core)
                             for n, g, r in zip(names, self.gfull[group], recv)]

    def _sibling_stage(self, group, gb):
        ce = self._sibling_carry(group, gb)
        self._sibling_done(group, comm_call(f"rs_sibling_exchange_{group}", ce.build, ce.ins, ce.out_shapes, ce.sems))

    def mlp_carry(self, layer, gb):
        return self._sibling_carry(UNDER_MLP[layer], gb) if layer in UNDER_MLP else None

    def mlp_done(self, layer, carried):
        if layer in UNDER_MLP:
            self._sibling_done(UNDER_MLP[layer], carried)

    def bwd_carry(self, layer, gb):
        names, parts, shapes = [], [], []
        for group in UNDER_SB_BWD[layer]:
            if group not in self.parts:
                self._sibling_stage(group, gb)
            names += self._members(group)[0]
            shapes += self._members(group)[1]
            parts += self.parts[group]
        exchange = chip_exchange(names, parts, shapes)
        if layer in SIBLING_UNDER_SB_BWD:
            exchange = _together(exchange, self._sibling_carry(SIBLING_UNDER_SB_BWD[layer], gb))
        return exchange

    def bwd_done(self, layer, carried):
        for group in UNDER_SB_BWD[layer]:
            n = len(RS_GROUPS[group])
            self._adamw(group, carried[:n])
            carried = carried[n:]
        if layer in SIBLING_UNDER_SB_BWD:
            self._sibling_done(SIBLING_UNDER_SB_BWD[layer], carried)

    def last_carry(self, gb):
        self._sibling_stage('C', gb)
        names, shapes = self._members('C')
        return chip_exchange(names, self.parts['C'], shapes)

    def last_done(self, carried):
        self._adamw('C', carried)

    def _adamw(self, group, recv2):
        for (n, l0, _), p, r in zip(RS_GROUPS[group], self.parts[group], recv2):
            self.out[n] = reduce_adamw(f"adamw_{group}_{n}", p, r, self.chip, self.w3[n], self.m3[n], self.v3[n],
                                       l0, self.out.get(n))


def _pack_small(vals):
    flat = jnp.concatenate([vals[n].reshape(-1).astype(F32) for n in SMALL] + [vals['loss'].reshape(-1)])
    rows = -(-flat.shape[0] // 1024) * 8
    return jnp.pad(flat, (0, rows * 128 - flat.shape[0])).reshape(rows, 128)


def _unpack_small(packed, shapes):
    flat = packed.reshape(-1)
    out, off = {}, 0
    for n in SMALL + ['loss']:
        size = int(np.prod(shapes[n]))
        out[n] = flat[off:off + size].reshape(shapes[n])
        off += size
    return out


def kernel(x, a_norm, a_wqkv, a_wo, kv_norm, w_kv, b_kv, b_norm, b_wq, b_bq, b_sinks, b_wo, b_bo, rel_bias, mlp_norm, mlp_up, mlp_down, final_norm, loss_target, m_a_norm, m_a_wqkv, m_a_wo, m_kv_norm, m_w_kv, m_b_kv, m_b_norm, m_b_wq, m_b_bq, m_b_sinks, m_b_wo, m_b_bo, m_rel_bias, m_mlp_norm, m_mlp_up, m_mlp_down, m_final_norm, v_a_norm, v_a_wqkv, v_a_wo, v_kv_norm, v_w_kv, v_b_kv, v_b_norm, v_b_wq, v_b_bq, v_b_sinks, v_b_wo, v_b_bo, v_rel_bias, v_mlp_norm, v_mlp_up, v_mlp_down, v_final_norm):
    w = dict(a_norm=a_norm, a_wqkv=a_wqkv, a_wo=a_wo, kv_norm=kv_norm, w_kv=w_kv, b_kv=b_kv, b_norm=b_norm,
             b_wq=b_wq, b_bq=b_bq, b_sinks=b_sinks, b_wo=b_wo, b_bo=b_bo, rel_bias=rel_bias, mlp_norm=mlp_norm,
             mlp_up=mlp_up, mlp_down=mlp_down, final_norm=final_norm)
    m = dict(a_norm=m_a_norm, a_wqkv=m_a_wqkv, a_wo=m_a_wo, kv_norm=m_kv_norm, w_kv=m_w_kv, b_kv=m_b_kv,
             b_norm=m_b_norm, b_wq=m_b_wq, b_bq=m_b_bq, b_sinks=m_b_sinks, b_wo=m_b_wo, b_bo=m_b_bo,
             rel_bias=m_rel_bias, mlp_norm=m_mlp_norm, mlp_up=m_mlp_up, mlp_down=m_mlp_down, final_norm=m_final_norm)
    v = dict(a_norm=v_a_norm, a_wqkv=v_a_wqkv, a_wo=v_a_wo, kv_norm=v_kv_norm, w_kv=v_w_kv, b_kv=v_b_kv,
             b_norm=v_b_norm, b_wq=v_b_wq, b_bq=v_b_bq, b_sinks=v_b_sinks, b_wo=v_b_wo, b_bo=v_b_bo,
             rel_bias=v_rel_bias, mlp_norm=v_mlp_norm, mlp_up=v_mlp_up, mlp_down=v_mlp_down, final_norm=v_final_norm)
    px, py, pc = _place()
    me = 4 * px + 2 * py + pc
    chip = (2 * px + py).astype(jnp.int32)
    core = pc.astype(jnp.int32)

    as3 = lambda t: t[None] if t.ndim == 2 else t
    w3, m3, v3 = ({n: as3(src[n]) for n in BIG} for src in (w, m, v))
    shards = {g: {n: w3[n][l0:l0 + nl].astype(BF16) for n, l0, nl in members} for g, members in AG_GROUPS.items()}
    an_pad = jnp.zeros((8, 128), F32).at[:a_norm.shape[0]].set(a_norm)
    names0 = [n for n, _, _ in AG_GROUPS[0]]
    full0 = all_gather_weights(names0 + ['a_norm'], [shards[0][n] for n in names0] + [an_pad],
                               [_full_shape(n, shards[0][n].shape) for n in names0] + [(N_DEV, 8, 128)])
    full0 = dict(zip(names0 + ['a_norm'], full0))
    n_a = a_norm.shape[0]
    small = {n: w[n] for n in SMALL}
    small['a_norm'] = full0['a_norm'][:, :n_a].transpose(1, 0, 2).reshape(n_a, -1)

    ex = _Exchanges(full0, shards, core, chip, w3, m3, v3)
    loss_b, grad_x, gb, sgrads = local_step(x[0], loss_target[0], small, ex)
    out = {n: [t.reshape(w[n].shape) for t in bufs] for n, bufs in ex.out.items()}

    sgrads['loss'] = loss_b[0, :1]
    gathered = all_gather_rows(_pack_small(sgrads))
    shapes = {n: w[n].shape for n in SMALL}
    shapes['a_norm'] = (n_a, a_norm.shape[1] * N_DEV)
    shapes['loss'] = (1,)
    zeros1 = jnp.zeros((1,), F32)

    def packed(src):
        vals = {n: src[n] for n in SMALL}
        vals['a_norm'] = jnp.zeros(shapes['a_norm'], F32)
        vals['loss'] = zeros1
        return _pack_small(vals)

    sm = small_adamw("adamw_small", gathered, packed(w), packed(m), packed(v))
    sm = [_unpack_small(t, shapes) for t in sm]
    g_an = lax.dynamic_slice_in_dim(sm[0]['a_norm'], me * a_norm.shape[1], a_norm.shape[1], axis=1)
    pad = lambda t: jnp.zeros((8, 128), F32).at[:n_a].set(t)
    gathered_an = jnp.zeros((N_DEV, 8, 128), F32).at[0].set(pad(g_an))
    an = small_adamw("adamw_a_norm", gathered_an, pad(a_norm), pad(m_a_norm), pad(v_a_norm))
    for i in range(4):
        sm[i]['a_norm'] = an[i][:n_a]
    for n in BIG:
        for i in range(4):
            sm[i][n] = out[n][i]
    loss = sm[0]['loss'][0]
    return (loss, grad_x[None], *[sm[0][n] for n in WEIGHTS], *[sm[1][n] for n in WEIGHTS],
            *[sm[2][n] for n in WEIGHTS], *[sm[3][n] for n in WEIGHTS])
```

```python
import math

import numpy as np
import jax
import jax.numpy as jnp
from jax import lax
from jax.experimental import pallas as pl
from jax.experimental.pallas import tpu as pltpu

F32 = jnp.float32
BF16 = jnp.bfloat16
MESH = pl.DeviceIdType.MESH

N_DEV = 8
HEAD_DIM = 64
WINDOW = 128
N_BUCKETS = 32
EPS = 1e-5
NEG_INF = -1e30
Q_SCALE = 1.0 / math.sqrt(HEAD_DIM)
LOG2E = 1.4426950408889634

ADAM_LR, ADAM_B1, ADAM_B2, ADAM_EPS, ADAM_WD, ADAM_STEP = 0.001, 0.9, 0.999, 1e-08, 0.01, 10

SB_BQ = 512
SB_BK = 128
SB_DEAD = 160.0
SB_UNSEEN = 1e30
ROW_TILE = 512
VMEM_LIMIT = 56 * 1024 * 1024

WEIGHTS = ['a_norm', 'a_wqkv', 'a_wo', 'kv_norm', 'w_kv', 'b_kv', 'b_norm', 'b_wq', 'b_bq', 'b_sinks', 'b_wo',
           'b_bo', 'rel_bias', 'mlp_norm', 'mlp_up', 'mlp_down', 'final_norm']
BIG = ['a_wqkv', 'a_wo', 'w_kv', 'b_wq', 'b_wo', 'mlp_up', 'mlp_down']
COL_SHARDED = ('a_wqkv', 'mlp_up')
SMALL = ['a_norm', 'kv_norm', 'b_kv', 'b_norm', 'b_bq', 'b_sinks', 'b_bo', 'rel_bias', 'mlp_norm', 'final_norm']


def _params(sem=None):
    return pltpu.CompilerParams(dimension_semantics=sem, vmem_limit_bytes=VMEM_LIMIT)


def _pick(n, cands):
    for c in cands:
        if n % c == 0:
            return c
    raise ValueError(n)


def _tile(n, want):
    return n if n <= want else _pick(n, (want, want // 2, want // 4))


MM_TILE_BUDGET = 36 * 1024 * 1024


def _row_tile(m, contraction, cols, streams):
    weight = 2 * contraction * cols * 2
    for rows in (2048, 1024, 512):
        if m % rows == 0 and weight + 2 * rows * (2 * contraction + cols * sum(streams)) <= MM_TILE_BUDGET:
            return rows
    return _tile(m, 512)


def mm_nn(name, a, w3, layer, epilogue, extras, out_dtypes, a_t=False, out_t=False):
    k, m = a.shape if a_t else a.shape[::-1]
    _, kw, n = w3.shape
    assert kw == k
    tn = _tile(n, 1024)
    tm = _row_tile(m, k, tn, [jnp.dtype(t).itemsize for t in out_dtypes]
                   + [e.dtype.itemsize for e in extras if e.size == m * n])
    ne, no = len(extras), len(out_dtypes)
    a_dim = 0 if a_t else 1

    def body(a_ref, w_ref, *rest):
        ex, outs = rest[:ne], rest[ne:ne + no]
        if out_t:
            acc = lax.dot_general(w_ref[...], a_ref[...], (((0,), (a_dim,)), ((), ())), preferred_element_type=F32)
        else:
            acc = lax.dot_general(a_ref[...], w_ref[...], (((a_dim,), (0,)), ((), ())), preferred_element_type=F32)
        for o, r in zip(outs, epilogue(acc, *[e[...] for e in ex])):
            o[...] = r.astype(o.dtype)

    if out_t:
        tile = pl.BlockSpec((tn, tm), lambda i, j: (j, i))
        vec = pl.BlockSpec((tn, 1), lambda i, j: (j, 0))
        out_shape = (n, m)
    else:
        tile = pl.BlockSpec((tm, tn), lambda i, j: (i, j))
        vec = pl.BlockSpec((1, tn), lambda i, j: (0, j))
        out_shape = (m, n)
    a_spec = pl.BlockSpec((k, tm), lambda i, j: (0, i)) if a_t else pl.BlockSpec((tm, k), lambda i, j: (i, 0))
    return pl.pallas_call(
        body, name=name, grid=(m // tm, n // tn),
        in_specs=[a_spec, pl.BlockSpec((None, k, tn), lambda i, j: (layer, 0, j))]
        + [tile if e.shape == out_shape else vec for e in extras],
        out_specs=[tile] * no,
        out_shape=[jax.ShapeDtypeStruct(out_shape, d) for d in out_dtypes],
        compiler_params=_params(("parallel", "parallel")),
    )(a, w3, *extras)


def mm_nt(name, dy, w3, layer, epilogue, extras, out_dtypes, a_t=False, out_t=False, n_sums=0, exchange=None):
    n, m = dy.shape if a_t else dy.shape[::-1]
    _, k, nw = w3.shape
    assert nw == n and not (out_t and n_sums)
    tko = _tile(k, 1024)
    tm = _row_tile(m, n, tko, [jnp.dtype(t).itemsize for t in out_dtypes]
                   + [e.dtype.itemsize for e in extras if e.size == m * k])
    ne, no = len(extras), len(out_dtypes)
    a_dim = 0 if a_t else 1

    def body(a_ref, w_ref, *rest):
        ex = rest[:ne]
        at = lambda step: (pl.program_id(0) == step[0]) & (pl.program_id(1) == step[1])
        results, _, start_carried, wait_carried = _carried(exchange, rest[ne:], no + n_sums, 0, at((0, 0)),
                                                           at((m // tm - 1, k // tko - 1)))
        outs, sums = results[:no], results[no:]
        start_carried()
        if out_t:
            acc = lax.dot_general(w_ref[...], a_ref[...], (((1,), (a_dim,)), ((), ())), preferred_element_type=F32)
        else:
            acc = lax.dot_general(a_ref[...], w_ref[...], (((a_dim,), (1,)), ((), ())), preferred_element_type=F32)
        res = epilogue(acc, *[e[...] for e in ex])
        for o, v in zip(outs, res):
            o[...] = v.astype(o.dtype)
        if n_sums:
            @pl.when(pl.program_id(0) == 0)
            def _():
                for o in sums:
                    o[...] = jnp.zeros_like(o)

            for o, v in zip(sums, res[no:]):
                o[...] += v
        wait_carried()

    if out_t:
        tile = pl.BlockSpec((tko, tm), lambda i, ko: (ko, i))
        out_shape = (k, m)
    else:
        tile = pl.BlockSpec((tm, tko), lambda i, ko: (i, ko))
        out_shape = (m, k)
    vec = pl.BlockSpec((1, tko), lambda i, ko: (0, ko))
    a_spec = pl.BlockSpec((n, tm), lambda i, ko: (0, i)) if a_t else pl.BlockSpec((tm, n), lambda i, ko: (i, 0))
    hbm = pl.BlockSpec(memory_space=pl.ANY)
    c_ins, c_outs, c_sems = (exchange.ins, exchange.out_shapes, exchange.sems) if exchange else ([], [], [])
    sequential = n_sums or exchange
    return pl.pallas_call(
        body, name=name, grid=(m // tm, k // tko),
        in_specs=[a_spec, pl.BlockSpec((None, tko, n), lambda i, ko: (layer, ko, 0))]
        + [tile if e.shape == out_shape else vec for e in extras] + [hbm] * len(c_ins),
        out_specs=[tile] * no + [vec] * n_sums + [hbm] * len(c_outs),
        out_shape=[jax.ShapeDtypeStruct(out_shape, d) for d in out_dtypes] + [jax.ShapeDtypeStruct((1, k), F32)] * n_sums
        + c_outs,
        scratch_shapes=c_sems,
        compiler_params=_params(("arbitrary" if sequential else "parallel", "arbitrary" if exchange else "parallel")),
    )(dy, w3, *extras, *c_ins)


def mm_tn(name, x, dy, gbuf, shape, layer, x_t=False, dy_t=False):
    k, s = x.shape if x_t else x.shape[::-1]
    _, kw, n = shape
    assert kw == k and dy.shape == ((n, s) if dy_t else (s, n))
    tkk = _tile(k, 512)
    tn = _tile(n, 1024)

    def body(x_ref, dy_ref, *rest):
        g_out = rest[-1]
        g_out[...] = lax.dot_general(x_ref[...], dy_ref[...], (((1 if x_t else 0,), (1 if dy_t else 0,)), ((), ())),
                                     preferred_element_type=F32).astype(g_out.dtype)

    prev = [] if gbuf is None else [gbuf]
    x_spec = pl.BlockSpec((tkk, s), lambda ki, j: (ki, 0)) if x_t else pl.BlockSpec((s, tkk), lambda ki, j: (0, ki))
    dy_spec = pl.BlockSpec((tn, s), lambda ki, j: (j, 0)) if dy_t else pl.BlockSpec((s, tn), lambda ki, j: (0, j))
    return pl.pallas_call(
        body, name=name, grid=(k // tkk, n // tn),
        in_specs=[x_spec, dy_spec] + [pl.BlockSpec(memory_space=pl.ANY)] * len(prev),
        out_specs=pl.BlockSpec((None, tkk, tn), lambda ki, j: (layer, ki, j)),
        out_shape=jax.ShapeDtypeStruct(shape, BF16),
        input_output_aliases={2: 0} if prev else {},
        compiler_params=_params(("parallel", "parallel")),
    )(x, dy, *prev)


def rms_fwd(name, h, g):
    s, d = h.shape
    tr = _pick(s, (ROW_TILE, 256, 128))

    def body(h_ref, g_ref, o_ref):
        x = h_ref[...]
        r = lax.rsqrt(jnp.mean(x * x, axis=-1, keepdims=True) + EPS)
        o_ref[...] = (x * r * g_ref[...]).astype(o_ref.dtype)

    return pl.pallas_call(
        body, name=name, grid=(s // tr,),
        in_specs=[pl.BlockSpec((tr, d), lambda i: (i, 0)), pl.BlockSpec((1, d), lambda i: (0, 0))],
        out_specs=pl.BlockSpec((tr, d), lambda i: (i, 0)),
        out_shape=jax.ShapeDtypeStruct((s, d), BF16),
        compiler_params=_params(("parallel",)),
    )(h, g.reshape(1, d))


def loss_head(h, g, target):
    s, d = h.shape
    tr = _pick(s, (ROW_TILE, 256, 128))

    def body(h_ref, g_ref, t_ref, dx_ref, dxb_ref, dg_ref, loss_ref):
        i = pl.program_id(0)
        x = h_ref[...]
        r = lax.rsqrt(jnp.mean(x * x, axis=-1, keepdims=True) + EPS)
        xh = x * r
        gw = g_ref[...]
        err = xh * gw - t_ref[...]
        dn_ = err * (1.0 / d)
        dyg = dn_ * gw
        dx = r * (dyg - xh * jnp.mean(dyg * xh, axis=-1, keepdims=True))
        dx_ref[...] = dx
        dxb_ref[...] = dx.astype(BF16)

        @pl.when(i == 0)
        def _():
            dg_ref[...] = jnp.zeros_like(dg_ref)
            loss_ref[...] = jnp.zeros_like(loss_ref)

        dg_ref[...] += jnp.sum(dn_ * xh, axis=0, keepdims=True)
        per_row = jnp.sum(err * err, axis=-1, keepdims=True) * (0.5 / d)
        loss_ref[...] += jnp.broadcast_to(jnp.sum(per_row, axis=0, keepdims=True), loss_ref.shape)

    row = pl.BlockSpec((tr, d), lambda i: (i, 0))
    vec = pl.BlockSpec((1, d), lambda i: (0, 0))
    return pl.pallas_call(
        body, name="loss_head", grid=(s // tr,),
        in_specs=[row, vec, row],
        out_specs=[row, row, vec, pl.BlockSpec((1, 128), lambda i: (0, 0))],
        out_shape=[jax.ShapeDtypeStruct((s, d), F32), jax.ShapeDtypeStruct((s, d), BF16),
                   jax.ShapeDtypeStruct((1, d), F32), jax.ShapeDtypeStruct((1, 128), F32)],
        compiler_params=_params(("arbitrary",)),
    )(h, g.reshape(1, d), target)


def colsum(name, x):
    s, n = x.shape
    tr = _pick(s, (ROW_TILE, 256, 128))

    def body(x_ref, o_ref):
        @pl.when(pl.program_id(0) == 0)
        def _():
            o_ref[...] = jnp.zeros_like(o_ref)

        o_ref[...] += jnp.sum(x_ref[...].astype(F32), axis=0, keepdims=True)

    return pl.pallas_call(
        body, name=name, grid=(s // tr,),
        in_specs=[pl.BlockSpec((tr, n), lambda i: (i, 0))],
        out_specs=pl.BlockSpec((1, n), lambda i: (0, 0)),
        out_shape=jax.ShapeDtypeStruct((1, n), F32),
        compiler_params=_params(("arbitrary",)),
    )(x)


def rowsum(name, x):
    n, s = x.shape
    ts = _pick(s, (1024, 512, 256, 128))

    def body(x_ref, o_ref):
        @pl.when(pl.program_id(0) == 0)
        def _():
            o_ref[...] = jnp.zeros_like(o_ref)

        o_ref[...] += jnp.sum(x_ref[...].astype(F32), axis=1, keepdims=True)

    return pl.pallas_call(
        body, name=name, grid=(s // ts,),
        in_specs=[pl.BlockSpec((n, ts), lambda i: (0, i))],
        out_specs=pl.BlockSpec((n, 1), lambda i: (0, 0)),
        out_shape=jax.ShapeDtypeStruct((n, 1), F32),
        compiler_params=_params(("arbitrary",)),
    )(x)[:, 0]


def _tri_rows(reverse):
    i = np.arange(SB_BK)
    tri = (i[None, :] >= i[:, None]) if reverse else (i[None, :] <= i[:, None])
    tri = np.concatenate([tri, tri], axis=1)
    return jnp.asarray(np.concatenate([tri, np.ones((8, 2 * SB_BK), bool)], axis=0), BF16)


def _hi_lo_rows(x):
    hi = x.astype(BF16)
    lo = (x - hi.astype(F32)).astype(BF16)
    return jnp.concatenate([hi, lo], axis=0)


def _softplus2(zs):
    neg_abs = lax.bitcast_convert_type(lax.bitcast_convert_type(zs, jnp.uint32) | jnp.uint32(0x80000000), F32)
    return jnp.maximum(zs, 0.0) + jnp.log2(1.0 + jnp.exp2(neg_abs))


def _pair_mask(first_rel_block, bq):
    key = lax.broadcasted_iota(jnp.int32, (2 * SB_BK, bq), 0) + first_rel_block * SB_BK
    qry = lax.broadcasted_iota(jnp.int32, (2 * SB_BK, bq), 1)
    return key < qry


def _row_of(table8, sub8, r):
    return jnp.sum(jnp.where(sub8 == r, table8, 0.0), axis=0, keepdims=True)


def _keys(j0):
    return pl.ds(pl.multiple_of(j0 * SB_BK, 2 * SB_BK), 2 * SB_BK)


class Carry:
    def __init__(self, build, ins, out_shapes, sems, then=None):
        self.build, self.ins, self.out_shapes, self.sems = build, list(ins), list(out_shapes), list(sems)
        self.then = then


def _together(a, b):
    assert a.then is None and b.then is None
    ni, no, ns = len(a.ins), len(a.out_shapes), len(a.sems)

    def build(ins, outs, *sems):
        return a.build(ins[:ni], outs[:no], *sems[:ns]) + b.build(ins[ni:], outs[no:], *sems[ns:])

    return Carry(build, a.ins + b.ins, a.out_shapes + b.out_shapes, a.sems + b.sems)


def _carried(carry, rest, n_out, n_scratch, first, last):
    n_ci = len(carry.ins) if carry else 0
    n_co = len(carry.out_shapes) if carry else 0
    cin, outs = rest[:n_ci], rest[n_ci:n_ci + n_out]
    cout = rest[n_ci + n_out:n_ci + n_out + n_co]
    scratch = rest[n_ci + n_out + n_co:n_ci + n_out + n_co + n_scratch]
    csems = rest[n_ci + n_out + n_co + n_scratch:]

    def start():
        if carry:
            @pl.when(first)
            def _():
                for cp in carry.build(cin, cout, *csems):
                    cp.start()

    def wait():
        if carry:
            @pl.when(last)
            def _():
                for cp in carry.build(cin, cout, *csems):
                    cp.wait()
                if carry.then:
                    second = carry.then(cin, cout, *csems)
                    for cp in second:
                        cp.start()
                    for cp in second:
                        cp.wait()

    return outs, scratch, start, wait


def _contract0(a, b):
    return lax.dot_general(a, b, (((0,), (0,)), ((), ())), preferred_element_type=F32)


def _contract1(a, b):
    return lax.dot_general(a, b, (((1,), (1,)), ((), ())), preferred_element_type=F32)


def sb_fwd(name, qkvt, exchange=None):
    nh, dh, s = qkvt.shape[0] // 3, qkvt.shape[1], qkvt.shape[2]
    bq = SB_BQ
    per_q = bq // SB_BK
    nkb = s // SB_BK
    assert s % bq == 0 and per_q == 4 and nkb % 8 == 0

    def body(q_ref, k_ref, v_ref, a_ref, *rest):
        head = pl.program_id(0)
        (o_ref, rtab_ref), (acc, zbuf, wbuf), start_carried, wait_carried = _carried(
            exchange, rest, 2, 3, head == 0, head == nh - 1)
        start_carried()
        tri = a_ref[...]
        sub8 = lax.broadcasted_iota(jnp.int32, (8, bq), 0)
        rtab_ref[...] = jnp.full(rtab_ref.shape, SB_UNSEEN, F32)
        kf = k_ref[...].astype(F32)
        k_max2 = jnp.max(jnp.sum(kf * kf, axis=0, keepdims=True), axis=1, keepdims=True)

        def query_block(i, _):
            lanes = pl.ds(pl.multiple_of(i * bq, bq), bq)
            qb = q_ref[:, lanes] * Q_SCALE
            acc[...] = jnp.zeros_like(acc)
            qf = qb.astype(F32)
            bound = jnp.sqrt(jnp.sum(qf * qf, axis=0, keepdims=True) * k_max2) * (1.001 * LOG2E)

            def scores(j0):
                return _contract0(k_ref[:, _keys(j0)], qb) * LOG2E

            def pair(j0, slot, run, rt8, mask, has_prev):
                zs = zbuf[slot]
                zbuf[1 - slot] = scores(jnp.maximum(j0 - 2, 0))
                if has_prev:
                    acc[...] += jnp.dot(v_ref[:, _keys(j0 + 2)], wbuf[1 - slot], preferred_element_type=F32)
                p = _softplus2(zs)
                if mask is not None:
                    p = jnp.where(mask, p, 0.0)
                cr1 = jnp.dot(tri, _hi_lo_rows(p[SB_BK:]), preferred_element_type=F32)
                cr0 = jnp.dot(tri, _hi_lo_rows(p[:SB_BK]), preferred_element_type=F32)
                run1 = run + cr1[SB_BK:SB_BK + 1]
                w = jnp.exp2(jnp.concatenate([zs[:SB_BK] - cr0[:SB_BK] - run1, zs[SB_BK:] - cr1[:SB_BK] - run],
                                             axis=0))
                if mask is not None:
                    w = jnp.where(mask, w, 0.0)
                wbuf[slot] = w.astype(BF16)
                rt8 = jnp.where(j0 % 8 == 6, SB_UNSEEN, rt8)
                rt8 = jnp.where(sub8 == (j0 + 1) % 8, run, jnp.where(sub8 == j0 % 8, run1, rt8))
                rtab_ref[pl.ds(pl.multiple_of((j0 // 8) * 8, 8), 8), lanes] = rt8
                return run1 + cr0[SB_BK:SB_BK + 1], rt8

            def alive(run):
                return jnp.min(run - bound) < SB_DEAD

            top = i * per_q
            zbuf[0] = scores(top + 2)
            state = (jnp.zeros((1, bq), F32), jnp.full((8, bq), SB_UNSEEN, F32))
            state = pair(top + 2, 0, *state, _pair_mask(2, bq), False)
            state = pair(top, 1, *state, _pair_mask(0, bq), True)

            def step(c):
                it, pairs, _, run, rt8 = c
                j0 = top - 2 - 4 * it
                run, rt8 = pair(j0, 0, run, rt8, None, True)
                go = alive(run)
                run, rt8 = lax.cond(go, lambda r, t: pair(j0 - 2, 1, r, t, None, True), lambda r, t: (r, t), run, rt8)
                return it + 1, pairs + 1 + go.astype(jnp.int32), go & alive(run), run, rt8

            pairs = lax.while_loop(lambda c: (c[0] < i) & c[2], step, (0, 0, alive(state[0]), *state))[1]
            acc[...] += jnp.dot(v_ref[:, _keys(top - 2 * pairs)], wbuf[(pairs + 1) % 2], preferred_element_type=F32)
            o_ref[:, lanes] = acc[...].astype(o_ref.dtype)
            return 0

        lax.fori_loop(0, s // bq, query_block, 0)
        wait_carried()

    def head_spec(offset, rows):
        return pl.BlockSpec((None, rows, s), lambda h: (h + offset, 0, 0))

    hbm = pl.BlockSpec(memory_space=pl.ANY)
    c_ins, c_outs, c_sems = (exchange.ins, exchange.out_shapes, exchange.sems) if exchange else ([], [], [])
    outs = pl.pallas_call(
        body, name=name, grid=(nh,),
        in_specs=[head_spec(0, dh), head_spec(nh, dh), head_spec(2 * nh, dh),
                  pl.BlockSpec((SB_BK + 8, 2 * SB_BK), lambda h: (0, 0))] + [hbm] * len(c_ins),
        out_specs=[head_spec(0, dh), head_spec(0, nkb)] + [hbm] * len(c_outs),
        out_shape=[jax.ShapeDtypeStruct((nh, dh, s), BF16), jax.ShapeDtypeStruct((nh, nkb, s), F32)] + c_outs,
        scratch_shapes=[pltpu.VMEM((dh, bq), F32), pltpu.VMEM((2, 2 * SB_BK, bq), F32),
                        pltpu.VMEM((2, 2 * SB_BK, bq), BF16)] + c_sems,
        compiler_params=_params(("arbitrary",)),
    )(qkvt, qkvt, qkvt, _tri_rows(True), *c_ins)
    return outs[0], outs[1], outs[2:]


def sb_bwd(name, qkvt, dot_, rtab, exchange=None):
    nh, dh, s = qkvt.shape[0] // 3, qkvt.shape[1], qkvt.shape[2]
    bq = SB_BQ
    per_q = bq // SB_BK
    nkb = s // SB_BK

    def body(qt_ref, kt_ref, vt_ref, dot_ref, rtab_ref, ar_ref, af_ref, *rest):
        head = pl.program_id(0)
        (dqkv_ref,), (dq_acc, dk_acc, dv_acc, zbuf, dwbuf, dzbuf, wbuf), start_carried, wait_carried = \
            _carried(exchange, rest, 1, 7, head == 0, head == nh - 1)
        dq_ref, dk_ref, dv_ref = dqkv_ref.at[0], dqkv_ref.at[1], dqkv_ref.at[2]
        start_carried()
        dk_acc[...] = jnp.zeros_like(dk_acc)
        dv_acc[...] = jnp.zeros_like(dv_acc)
        tri_rev = ar_ref[...][:SB_BK]
        tri_fwd = af_ref[...]
        sub8 = lax.broadcasted_iota(jnp.int32, (8, bq), 0)

        def query_block(i, _):
            lanes = pl.ds(pl.multiple_of(i * bq, bq), bq)
            qtb = qt_ref[:, lanes] * Q_SCALE
            dotb = dot_ref[:, lanes]
            dq_acc[...] = jnp.zeros_like(dq_acc)
            last_j = i * per_q + 2
            seen = jnp.max(jnp.where(rtab_ref[:, lanes] < 0.1 * SB_UNSEEN, 1.0, 0.0), axis=1, keepdims=True)
            pairs = jnp.clip((jnp.sum(seen).astype(jnp.int32) - per_q) // 2, 0, 2 * i)
            odd = pairs % 2
            first_j = i * per_q - 2 * pairs

            def issue(j0, slot):
                zbuf[slot] = _contract0(kt_ref[:, _keys(j0)], qtb) * LOG2E
                dwbuf[slot] = _contract0(vt_ref[:, _keys(j0)], dotb)

            def retire(j0, slot):
                keys = _keys(j0)
                dq_acc[...] += jnp.dot(kt_ref[:, keys], dzbuf[slot], preferred_element_type=F32)
                dk_acc[:, keys] += _contract1(qtb, dzbuf[slot])
                dv_acc[:, keys] += _contract1(dotb, wbuf[slot])

            def pair(j0, slot, g_run, mask):
                zs = zbuf[slot]
                dw = dwbuf[slot]
                issue(jnp.minimum(j0 + 2, last_j), 1 - slot)
                retire(jnp.maximum(j0 - 2, first_j), 1 - slot)
                p_raw = _softplus2(zs)
                p = p_raw if mask is None else jnp.where(mask, p_raw, 0.0)
                c0 = jnp.dot(tri_rev, _hi_lo_rows(p[:SB_BK]), preferred_element_type=F32)
                c1 = jnp.dot(tri_rev, _hi_lo_rows(p[SB_BK:]), preferred_element_type=F32)
                rt8 = rtab_ref[pl.ds(pl.multiple_of((j0 // 8) * 8, 8), 8), lanes]
                r0 = _row_of(rt8, sub8, j0 % 8)
                r1 = _row_of(rt8, sub8, (j0 + 1) % 8)
                w = jnp.exp2(jnp.concatenate([zs[:SB_BK] - c0 - r0, zs[SB_BK:] - c1 - r1], axis=0))
                if mask is not None:
                    w = jnp.where(mask, w, 0.0)
                g = w * dw
                gg0 = jnp.dot(tri_fwd, _hi_lo_rows(g[:SB_BK]), preferred_element_type=F32)
                gg1 = jnp.dot(tri_fwd, _hi_lo_rows(g[SB_BK:]), preferred_element_type=F32)
                g_run1 = g_run + gg0[SB_BK:SB_BK + 1]
                g_pre = jnp.concatenate([gg0[:SB_BK] + g_run, gg1[:SB_BK] + g_run1], axis=0)
                dz = g - jnp.exp2(zs - p_raw) * g_pre
                if mask is not None:
                    dz = jnp.where(mask, dz, 0.0)
                dzbuf[slot] = dz.astype(BF16)
                wbuf[slot] = w.astype(BF16)
                return g_run1 + gg1[SB_BK:SB_BK + 1]

            issue(first_j, odd)
            dzbuf[...] = jnp.zeros(dzbuf.shape, BF16)
            wbuf[...] = jnp.zeros(wbuf.shape, BF16)

            def step(it, g_run):
                g_run = pair(4 * it, 0, g_run, None)
                return pair(4 * it + 2, 1, g_run, None)

            g_run = lax.cond(odd == 1, lambda g: pair(first_j, 1, g, None), lambda g: g, jnp.zeros((1, bq), F32))
            g_run = lax.fori_loop(i - pairs // 2, i, step, g_run)
            g_run = pair(last_j - 2, 0, g_run, _pair_mask(0, bq))
            pair(last_j, 1, g_run, _pair_mask(2, bq))
            retire(last_j, 1)
            dq_ref[:, lanes] = (dq_acc[...] * Q_SCALE).astype(dq_ref.dtype)
            return 0

        lax.fori_loop(0, s // bq, query_block, 0)
        dk_ref[...] = dk_acc[...].astype(dk_ref.dtype)
        dv_ref[...] = dv_acc[...].astype(dv_ref.dtype)
        wait_carried()

    def head_spec(offset, rows):
        return pl.BlockSpec((None, rows, s), lambda h: (h + offset, 0, 0))

    aspec = pl.BlockSpec((SB_BK + 8, 2 * SB_BK), lambda h: (0, 0))
    pair_f32 = pltpu.VMEM((2, 2 * SB_BK, bq), F32)
    pair_bf16 = pltpu.VMEM((2, 2 * SB_BK, bq), BF16)
    hbm = pl.BlockSpec(memory_space=pl.ANY)
    c_ins, c_outs, c_sems = (exchange.ins, exchange.out_shapes, exchange.sems) if exchange else ([], [], [])
    outs = pl.pallas_call(
        body, name=name, grid=(nh,),
        in_specs=[head_spec(0, dh), head_spec(nh, dh), head_spec(2 * nh, dh), head_spec(0, dh), head_spec(0, nkb),
                  aspec, aspec] + [hbm] * len(c_ins),
        out_specs=[pl.BlockSpec((3, None, dh, s), lambda h: (0, h, 0, 0))] + [hbm] * len(c_outs),
        out_shape=[jax.ShapeDtypeStruct((3, nh, dh, s), BF16)] + c_outs,
        scratch_shapes=[pltpu.VMEM((dh, bq), F32), pltpu.VMEM((dh, s), F32), pltpu.VMEM((dh, s), F32),
                        pair_f32, pair_f32, pair_bf16, pair_bf16] + c_sems,
        compiler_params=_params(("arbitrary",)),
    )(qkvt, qkvt, qkvt, dot_, rtab, _tri_rows(True), _tri_rows(False), *c_ins)
    return outs[0], outs[1:]


SWA_QB = 4


def _band_valid():
    kj = np.arange(2 * WINDOW)[:, None]
    dist = (np.arange(8 * WINDOW)[None, :] % WINDOW) + WINDOW - kj
    inside = (dist >= 0) & (dist < WINDOW)
    return jnp.asarray(np.stack([inside & (kj >= WINDOW), inside]), F32)


def _swa_probs(qt, kt, bias_t, valid, sink):
    sc = jnp.where(valid > 0.5, _contract0(kt, qt) + bias_t, NEG_INF)
    mx = jnp.maximum(jnp.max(sc, axis=0, keepdims=True), sink)
    p = jnp.exp(sc - mx)
    p_sink = jnp.exp(sink - mx)
    inv = 1.0 / (jnp.sum(p, axis=0, keepdims=True) + p_sink)
    return p, p_sink, inv


def _band(i):
    return pl.ds(pl.multiple_of(i * WINDOW, WINDOW), 2 * WINDOW)


def _heads_to_lanes(blk):
    return jnp.concatenate([blk[r * HEAD_DIM:(r + 1) * HEAD_DIM] for r in range(8)], axis=1)


def _lanes_to_heads(t):
    return jnp.concatenate([t[:, r * WINDOW:(r + 1) * WINDOW] for r in range(8)], axis=0)


def swa_fwd(name, qt, kpt, vpt, bias_t, sink_row):
    d, s = qt.shape
    ng, dh, sp = kpt.shape
    rows, cols = d // ng, SWA_QB * WINDOW
    assert (s // WINDOW) % SWA_QB == 0

    def body(q_ref, k_ref, v_ref, bias_ref, valid_ref, sink_ref, o_ref):
        for u in range(SWA_QB):
            i = pl.program_id(1) * SWA_QB + u
            lanes = slice(u * WINDOW, (u + 1) * WINDOW)
            qb = _heads_to_lanes(q_ref[:, lanes]) * Q_SCALE
            p, _, inv = _swa_probs(qb, k_ref[:, _band(i)], bias_ref[...], valid_ref[jnp.minimum(i, 1)], sink_ref[...])
            o_t = jnp.dot(v_ref[:, _band(i)], p.astype(BF16), preferred_element_type=F32) * inv
            o_ref[:, lanes] = _lanes_to_heads(o_t).astype(o_ref.dtype)

    qspec = pl.BlockSpec((rows, cols), lambda g, i: (g, i))
    kspec = pl.BlockSpec((None, dh, sp), lambda g, i: (g, 0, 0))
    return pl.pallas_call(
        body, name=name, grid=(ng, s // cols),
        in_specs=[qspec, kspec, kspec, pl.BlockSpec((None, 2 * WINDOW, 8 * WINDOW), lambda g, i: (g, 0, 0)),
                  pl.BlockSpec((2, 2 * WINDOW, 8 * WINDOW), lambda g, i: (0, 0, 0)),
                  pl.BlockSpec((None, 1, 8 * WINDOW), lambda g, i: (g, 0, 0))],
        out_specs=qspec,
        out_shape=jax.ShapeDtypeStruct(qt.shape, BF16),
        compiler_params=_params(("parallel", "arbitrary")),
    )(qt, kpt, vpt, bias_t, _band_valid(), sink_row)


def swa_bwd(name, qt, kpt, vpt, bias_t, sink_row, dot_, dk_in, dv_in):
    d, s = qt.shape
    ng, dh, sp = kpt.shape
    rows, cols = d // ng, SWA_QB * WINDOW

    def body(q_ref, k_ref, v_ref, bias_ref, valid_ref, sink_ref, do_ref, dki_ref, dvi_ref,
             dq_ref, dk_ref, dv_ref, db_ref, ds_ref):
        @pl.when(pl.program_id(1) == 0)
        def _():
            dk_ref[...] = dki_ref[...]
            dv_ref[...] = dvi_ref[...]
            db_ref[...] = jnp.zeros_like(db_ref)
            ds_ref[...] = jnp.zeros_like(ds_ref)

        for u in range(SWA_QB):
            i = pl.program_id(1) * SWA_QB + u
            band = _band(i)
            lanes = slice(u * WINDOW, (u + 1) * WINDOW)
            qb = _heads_to_lanes(q_ref[:, lanes]) * Q_SCALE
            dob = _heads_to_lanes(do_ref[:, lanes])
            kt = k_ref[:, band]
            p, p_sink, inv = _swa_probs(qb, kt, bias_ref[...], valid_ref[jnp.minimum(i, 1)], sink_ref[...])
            p = p * inv
            dp = _contract0(v_ref[:, band], dob)
            delta = jnp.sum(p * dp, axis=0, keepdims=True)
            dsc = p * (dp - delta)
            ds_ref[...] -= p_sink * inv * delta
            db_ref[...] += dsc
            dscb = dsc.astype(BF16)
            dq_t = jnp.dot(kt, dscb, preferred_element_type=F32) * Q_SCALE
            dq_ref[:, lanes] = _lanes_to_heads(dq_t).astype(dq_ref.dtype)
            dk_ref[:, band] += _contract1(qb, dscb)
            dv_ref[:, band] += _contract1(dob, p.astype(BF16))

    qspec = pl.BlockSpec((rows, cols), lambda g, i: (g, i))
    kspec = pl.BlockSpec((None, dh, sp), lambda g, i: (g, 0, 0))
    bspec = pl.BlockSpec((None, 2 * WINDOW, 8 * WINDOW), lambda g, i: (g, 0, 0))
    sspec = pl.BlockSpec((None, 1, 8 * WINDOW), lambda g, i: (g, 0, 0))
    return pl.pallas_call(
        body, name=name, grid=(ng, s // cols),
        in_specs=[qspec, kspec, kspec, bspec, pl.BlockSpec((2, 2 * WINDOW, 8 * WINDOW), lambda g, i: (0, 0, 0)), sspec,
                  qspec, kspec, kspec],
        out_specs=[qspec, kspec, kspec, bspec, sspec],
        out_shape=[jax.ShapeDtypeStruct(qt.shape, BF16), jax.ShapeDtypeStruct(kpt.shape, F32),
                   jax.ShapeDtypeStruct(kpt.shape, F32), jax.ShapeDtypeStruct(bias_t.shape, F32),
                   jax.ShapeDtypeStruct(sink_row.shape, F32)],
        compiler_params=_params(("parallel", "arbitrary")),
    )(qt, kpt, vpt, bias_t, _band_valid(), sink_row, dot_, dk_in, dv_in)


def _bucket_onehot():
    qi = np.arange(WINDOW)[:, None]
    kj = np.arange(2 * WINDOW)[None, :]
    n = np.maximum(qi + WINDOW - kj, 0)
    max_exact = N_BUCKETS // 2
    nf = np.maximum(n, 1).astype(np.float64)
    val = np.log(nf / max_exact) / math.log(WINDOW / max_exact) * (N_BUCKETS - max_exact)
    assert np.all(np.abs(val - np.round(val))[(n > max_exact) & (n < WINDOW)] > 1e-3)
    large = np.minimum(max_exact + val.astype(np.int64), N_BUCKETS - 1)
    bucket = np.where(n < max_exact, n, large).reshape(-1)
    onehot = np.zeros((128, bucket.size), np.float32)
    onehot[bucket, np.arange(bucket.size)] = 1.0
    return onehot


def _split3(x):
    a = x.astype(BF16)
    r = x - a.astype(F32)
    b = r.astype(BF16)
    c = (r - b.astype(F32)).astype(BF16)
    return a, b, c


def bias_table(rel_bias):
    nh = rel_bias.shape[1]
    oh = jnp.asarray(_bucket_onehot(), BF16)
    n = oh.shape[1]
    tn = 4096
    rb = jnp.zeros((nh, 128), F32).at[:, :N_BUCKETS].set(rel_bias.T)

    def body(rb_ref, oh_ref, o_ref):
        o_ref[...] = sum(jnp.dot(t, oh_ref[...], preferred_element_type=F32) for t in _split3(rb_ref[...]))

    return pl.pallas_call(
        body, name="bias_table", grid=(n // tn,),
        in_specs=[pl.BlockSpec((nh, 128), lambda i: (0, 0)), pl.BlockSpec((128, tn), lambda i: (0, i))],
        out_specs=pl.BlockSpec((nh, tn), lambda i: (0, i)),
        out_shape=jax.ShapeDtypeStruct((nh, n), F32),
        compiler_params=_params(("parallel",)),
    )(rb, oh)


def bias_table_grad(db0, db1):
    nh, n = db0.shape
    oh = jnp.asarray(_bucket_onehot(), BF16)
    tn = 4096

    def body(a_ref, b_ref, oh_ref, o_ref):
        @pl.when(pl.program_id(0) == 0)
        def _():
            o_ref[...] = jnp.zeros_like(o_ref)

        o_ref[...] += sum(lax.dot_general(t, oh_ref[...], (((1,), (1,)), ((), ())), preferred_element_type=F32)
                          for t in _split3(a_ref[...] + b_ref[...]))

    blk = pl.BlockSpec((nh, tn), lambda i: (0, i))
    return pl.pallas_call(
        body, name="bias_table_grad", grid=(n // tn,),
        in_specs=[blk, blk, pl.BlockSpec((128, tn), lambda i: (0, i))],
        out_specs=pl.BlockSpec((nh, 128), lambda i: (0, 0)),
        out_shape=jax.ShapeDtypeStruct((nh, 128), F32),
        compiler_params=_params(("arbitrary",)),
    )(db0, db1, oh)


def _owner_view(ref, name, d):
    if name == 'a_norm':
        return ref.at[d]
    if name in COL_SHARDED:
        n = ref.shape[2] // N_DEV
        return ref.at[:, :, pl.ds(pl.multiple_of(d * n, 128), n)]
    return ref.at[:, d]


def _place():
    return lax.axis_index("x"), lax.axis_index("y"), lax.axis_index("c")


def _dev(p):
    return 4 * p[0] + 2 * p[1] + p[2]


def _remote(src, dst, send_sem, recv_sem, to):
    return pltpu.make_async_remote_copy(src_ref=src, dst_ref=dst, send_sem=send_sem, recv_sem=recv_sem,
                                        device_id=to, device_id_type=MESH)


def _dma_sems(*shapes):
    return [pltpu.SemaphoreType.DMA(sh) for sh in shapes]


def comm_call(name, build, ins, out_shapes, sems):
    n_in, n_out = len(ins), len(out_shapes)

    def body(*refs):
        copies = build(refs[:n_in], refs[n_in:n_in + n_out], *refs[n_in + n_out:])
        for cp in copies:
            cp.start()
        for cp in copies:
            cp.wait()

    hbm = pl.BlockSpec(memory_space=pl.ANY)
    return pl.pallas_call(
        body, name=name, in_specs=[hbm] * n_in, out_specs=[hbm] * n_out, out_shape=list(out_shapes),
        scratch_shapes=sems,
    )(*ins)


def all_gather_weights(names, shards, full_shapes):
    n = len(names)

    def body(*refs):
        ins, outs = refs[:n], refs[n:2 * n]
        send_sems, recv_sems, local_sems = refs[2 * n:]
        x, y, c = _place()
        me, sibling = (x, y, c), (x, y, 1 - c)
        chips = [(1 - x, y), (x, 1 - y), (1 - x, 1 - y)]

        def copy(t, k, block, to, src=None):
            dst = _owner_view(outs[t], names[t], _dev(block))
            return _remote(dst if src is None else src, dst, send_sems.at[t, k], recv_sems.at[t, k], to)

        mine = [pltpu.make_async_copy(ins[t], _owner_view(outs[t], names[t], _dev(me)), local_sems.at[t])
                for t in range(n)]
        for cp in mine:
            cp.start()
        first = []
        for t in range(n):
            first.append(copy(t, 0, me, sibling, src=ins[t]))
            first += [copy(t, 1 + j, me, (*chip, c), src=ins[t]) for j, chip in enumerate(chips)]
        for cp in first:
            cp.start()
        passed = []
        for j, chip in enumerate(chips):
            for t in range(n):
                copy(t, 1 + j, (*chip, c), me).wait_recv()
                fwd = copy(t, 4 + j, (*chip, c), sibling)
                fwd.start()
                passed.append(fwd)
        for t in range(n):
            copy(t, 0, sibling, me).wait_recv()
            for j, chip in enumerate(chips):
                copy(t, 4 + j, (*chip, 1 - c), me).wait_recv()
        for cp in first + passed:
            cp.wait_send()
        for cp in mine:
            cp.wait()

    hbm = pl.BlockSpec(memory_space=pl.ANY)
    return pl.pallas_call(
        body, name="all_gather_layer0",
        in_specs=[hbm] * n, out_specs=[hbm] * n,
        out_shape=[jax.ShapeDtypeStruct(full_shapes[t], shards[t].dtype) for t in range(n)],
        scratch_shapes=_dma_sems((n, 7), (n, 7), (n,)),
    )(*shards)


def ag_direct(names, shards, full_shapes):
    n = len(names)

    def build(ins, outs, send_sems, recv_sems, local_sems, fwd_send_sems, fwd_recv_sems):
        x, y, c = _place()
        peers = [(x, y, 1 - c), (1 - x, y, c), (x, 1 - y, c), (1 - x, 1 - y, c)]
        copies = []
        for t in range(n):
            dst = _owner_view(outs[t], names[t], _dev((x, y, c)))
            copies.append(pltpu.make_async_copy(ins[t], dst, local_sems.at[t]))
            copies += [_remote(ins[t], dst, send_sems.at[t, k], recv_sems.at[t, k], to) for k, to in enumerate(peers)]
        return copies

    def forward(ins, outs, send_sems, recv_sems, local_sems, fwd_send_sems, fwd_recv_sems):
        x, y, c = _place()
        copies = []
        for t in range(n):
            for k, chip in enumerate([(1 - x, y), (x, 1 - y), (1 - x, 1 - y)]):
                view = _owner_view(outs[t], names[t], _dev((*chip, c)))
                copies.append(_remote(view, view, fwd_send_sems.at[t, k], fwd_recv_sems.at[t, k], (x, y, 1 - c)))
        return copies

    return Carry(build, shards, [jax.ShapeDtypeStruct(full_shapes[t], shards[t].dtype) for t in range(n)],
                 _dma_sems((n, 4), (n, 4), (n,), (n, 3), (n, 3)), then=forward)


def sibling_exchange(names, grads, part_shapes):
    n = len(names)

    def build(ins, outs, send_sems, recv_sems):
        x, y, c = _place()
        return [_remote(_owner_view(ins[t], names[t], 2 * q + 1 - c), outs[t].at[q], send_sems.at[t, q],
                        recv_sems.at[t, q], (x, y, 1 - c)) for t in range(n) for q in range(4)]

    return Carry(build, grads, [jax.ShapeDtypeStruct((4,) + part_shapes[t], BF16) for t in range(n)],
                 _dma_sems((n, 4), (n, 4)))


def chip_exchange(names, parts, part_shapes):
    n = len(names)

    def build(ins, outs, send_sems, recv_sems):
        x, y, c = _place()
        chips = [(1 - x, y), (x, 1 - y), (1 - x, 1 - y)]
        return [_remote(ins[t].at[2 * chip[0] + chip[1]], outs[t].at[k], send_sems.at[t, k], recv_sems.at[t, k],
                        (*chip, c)) for t in range(n) for k, chip in enumerate(chips)]

    return Carry(build, parts, [jax.ShapeDtypeStruct((3,) + part_shapes[t], BF16) for t in range(n)],
                 _dma_sems((n, 3), (n, 3)))


def all_gather_rows(x):
    r, w = x.shape

    def body(x_ref, out_ref, send_sems, recv_sems, local_sem):
        px, py, pc = _place()
        me = 4 * px + 2 * py + pc
        mine = pltpu.make_async_copy(x_ref, out_ref.at[me], local_sem)
        mine.start()
        copies = []
        for k in range(1, N_DEV):
            peer = (px ^ (k >> 2), py ^ ((k >> 1) & 1), pc ^ (k & 1))
            copies.append(pltpu.make_async_remote_copy(
                src_ref=x_ref, dst_ref=out_ref.at[me], send_sem=send_sems.at[k - 1], recv_sem=recv_sems.at[k - 1],
                device_id=peer, device_id_type=MESH))
        for cp in copies:
            cp.start()
        for k in range(1, N_DEV):
            peer_idx = me ^ k
            pltpu.make_async_remote_copy(
                src_ref=x_ref, dst_ref=out_ref.at[peer_idx], send_sem=send_sems.at[k - 1],
                recv_sem=recv_sems.at[k - 1], device_id=(px, py, pc), device_id_type=MESH).wait_recv()
        for cp in copies:
            cp.wait_send()
        mine.wait()

    vmem = pl.BlockSpec(memory_space=pltpu.VMEM)
    return pl.pallas_call(
        body, name="all_gather_small_grads",
        in_specs=[vmem], out_specs=vmem,
        out_shape=jax.ShapeDtypeStruct((N_DEV, r, w), x.dtype),
        scratch_shapes=[pltpu.SemaphoreType.DMA((N_DEV - 1,)), pltpu.SemaphoreType.DMA((N_DEV - 1,)),
                        pltpu.SemaphoreType.DMA],
    )(x)


def _adamw(w, g, m, v):
    m = ADAM_B1 * m + (1.0 - ADAM_B1) * g
    v = ADAM_B2 * v + (1.0 - ADAM_B2) * (g * g)
    m_hat = m / (1.0 - ADAM_B1 ** ADAM_STEP)
    v_hat = v / (1.0 - ADAM_B2 ** ADAM_STEP)
    return -ADAM_LR * (m_hat / (jnp.sqrt(v_hat) + ADAM_EPS) + ADAM_WD * w), m, v


def sibling_sum(name, col, grads, recv, core):
    _, nl, rows, cols = recv.shape
    tr = _tile(rows, 1024)
    rspec = pl.BlockSpec((None, None, tr, cols), lambda q, l, i, c_ref: (q, l, i, 0))
    if col:
        gspec = pl.BlockSpec((None, tr, cols), lambda q, l, i, c_ref: (l, i, 2 * q + c_ref[0]))
    else:
        gspec = pl.BlockSpec((None, None, tr, cols), lambda q, l, i, c_ref: (l, 2 * q + c_ref[0], i, 0))

    def body(c_ref, g_ref, r_ref, o_ref):
        del c_ref
        o_ref[...] = (g_ref[...].astype(F32) + r_ref[...].astype(F32)).astype(BF16)

    return pl.pallas_call(
        body, name=name,
        grid_spec=pltpu.PrefetchScalarGridSpec(num_scalar_prefetch=1, grid=(4, nl, rows // tr),
                                               in_specs=[gspec, rspec], out_specs=rspec),
        out_shape=jax.ShapeDtypeStruct(recv.shape, BF16),
        compiler_params=_params(("parallel", "parallel", "parallel")),
    )(core.reshape(1), grads, recv)


def reduce_adamw(name, parts, recv, chip, w, m, v, l0, prev):
    _, nl, rows, cols = parts.shape
    tr = _tile(rows, 512)

    def body(q_ref, p_ref, r_ref, w_ref, m_ref, v_ref, *rest):
        del q_ref
        g_out, d_out, m_out, v_out = rest[-4:]
        g = ((p_ref[...].astype(F32) + r_ref[0].astype(F32)) + r_ref[1].astype(F32)) + r_ref[2].astype(F32)
        d, mn, vn = _adamw(w_ref[...], g, m_ref[...], v_ref[...])
        g_out[...] = g
        d_out[...] = d
        m_out[...] = mn
        v_out[...] = vn

    blk = pl.BlockSpec((None, tr, cols), lambda l, i, q_ref: (l0 + l, i, 0))
    prev = list(prev) if prev else []
    return pl.pallas_call(
        body, name=name,
        grid_spec=pltpu.PrefetchScalarGridSpec(
            num_scalar_prefetch=1, grid=(nl, rows // tr),
            in_specs=[pl.BlockSpec((None, None, tr, cols), lambda l, i, q_ref: (q_ref[0], l, i, 0)),
                      pl.BlockSpec((3, None, tr, cols), lambda l, i, q_ref: (0, l, i, 0)), blk, blk, blk]
            + [pl.BlockSpec(memory_space=pl.ANY)] * len(prev),
            out_specs=[blk] * 4),
        out_shape=[jax.ShapeDtypeStruct(w.shape, F32)] * 4,
        input_output_aliases={6 + i: i for i in range(len(prev))},
        compiler_params=_params(("parallel", "parallel")),
    )(chip.reshape(1), parts, recv, w, m, v, *prev)


def small_adamw(name, gathered, w, m, v):
    _, r, c = gathered.shape

    def body(ga_ref, w_ref, m_ref, v_ref, g_out, d_out, m_out, v_out):
        g = ga_ref[0]
        for d in range(1, N_DEV):
            g = g + ga_ref[d]
        dl, mn, vn = _adamw(w_ref[...], g, m_ref[...], v_ref[...])
        g_out[...] = g
        d_out[...] = dl
        m_out[...] = mn
        v_out[...] = vn

    return pl.pallas_call(
        body, name=name,
        out_shape=[jax.ShapeDtypeStruct((r, c), F32)] * 4,
        compiler_params=_params(),
    )(gathered, w, m, v)


def _rms(x, g):
    return x * lax.rsqrt(jnp.mean(x * x, axis=-1, keepdims=True) + EPS) * g


def _rms_bwd_epilogue(dn, x, dres, g):
    r = lax.rsqrt(jnp.mean(x * x, axis=-1, keepdims=True) + EPS)
    xh = x * r
    dyg = dn * g
    dx = dres + r * (dyg - xh * jnp.mean(dyg * xh, axis=-1, keepdims=True))
    return dx, dx, jnp.sum(dn * xh, axis=0, keepdims=True), jnp.sum(dx, axis=0, keepdims=True)


def _residual_then_norms(n_terms):
    def epilogue(acc, *ex):
        h = acc
        for t in ex[:n_terms]:
            h = h + t
        return (h,) + tuple(_rms(h, g) for g in ex[n_terms:])
    return epilogue


def local_step(x, target, small, ex):
    s, d = x.shape
    n_a, n_b = small['a_norm'].shape[0], small['b_norm'].shape[0]
    sg = {}
    gb = {}

    def fwd_mm(name, a, wname, layer, epilogue, extras, out_dtypes, **kw):
        return mm_nn(name, a, *ex.weight(wname, layer), epilogue, extras, out_dtypes, **kw)

    def dx_mm(name, dy, wname, layer, epilogue, extras, out_dtypes, **kw):
        return mm_nt(name, dy, *ex.weight(wname, layer), epilogue, extras, out_dtypes, **kw)

    def dw_mm(name, a, dy, wname, layer, **kw):
        key, slab, shape = ex.grad(wname, layer)
        gb[key] = mm_tn(name, a, dy, gb.get(key), shape, slab, **kw)

    plain = lambda acc: (acc,)
    plus_col = lambda acc, b: (acc + b,)

    bias_flat = bias_table(small['rel_bias'])
    bias_t = bias_flat.reshape(2, 8, WINDOW, 2 * WINDOW).transpose(0, 3, 1, 2).reshape(2, 2 * WINDOW, 8 * WINDOW)
    sink_rows = [jnp.repeat(small['b_sinks'][j], WINDOW).reshape(2, 1, 8 * WINDOW) for j in range(n_b)]

    gain = lambda g: g.reshape(1, -1)

    def mlp_fwd(h, n2, layer, next_gains):
        u, a = fwd_mm(f"mlp_up_fwd{layer}", n2, 'mlp_up', layer,
                      lambda acc: (acc, jnp.square(jnp.maximum(acc, 0.0))), (), (BF16, BF16))
        h2, *nexts = fwd_mm(f"mlp_down_fwd{layer}", a, 'mlp_down', layer, _residual_then_norms(1),
                            (h, *[gain(g) for g in next_gains]), (F32,) + (BF16,) * len(next_gains))
        return h2, nexts, (n2, u, a)

    h = x
    saved = []
    n1 = rms_fwd("a_norm_fwd0", h, small['a_norm'][0])
    for l in range(n_a):
        (qkvt,) = fwd_mm(f"a_qkv_fwd{l}", n1, 'a_wqkv', l, plain, (), (BF16,), out_t=True)
        qkvt = qkvt.reshape(3 * d // HEAD_DIM, HEAD_DIM, s)
        o_t, rtab, carried = sb_fwd(f"sb_fwd{l}", qkvt, ex.fwd_carry(l))
        ex.fwd_done(l, carried)
        o_t = o_t.reshape(d, s)
        h_mid, n2 = fwd_mm(f"a_wo_fwd{l}", o_t, 'a_wo', l, _residual_then_norms(1),
                           (h, gain(small['mlp_norm'][l])), (F32, BF16), a_t=True)
        next_gains = [small['a_norm'][l + 1]] if l + 1 < n_a else [small['b_norm'][0], small['kv_norm']]
        h_out, nexts, mlp_saved = mlp_fwd(h_mid, n2, l, next_gains)
        saved.append((h, n1, qkvt, o_t, rtab, h_mid, mlp_saved))
        h, n1 = h_out, nexts[0]
    h_kv, nkv = h, nexts[1]
    (kvt,) = fwd_mm("kv_fwd", nkv, 'w_kv', 0, plus_col, (small['b_kv'].reshape(-1, 1),), (BF16,), out_t=True)
    kvt = kvt.reshape(2, 2, HEAD_DIM, s)
    kpt, vpt = (jnp.pad(t, ((0, 0), (0, 0), (WINDOW, 0))) for t in (kvt[0], kvt[1]))
    for j in range(n_b):
        layer = n_a + j
        (qbt,) = fwd_mm(f"b_q_fwd{j}", n1, 'b_wq', j, plus_col, (small['b_bq'][j].reshape(-1, 1),), (BF16,),
                        out_t=True)
        o_t = swa_fwd(f"swa_fwd{j}", qbt, kpt, vpt, bias_t, sink_rows[j])
        h_mid, n2 = fwd_mm(f"b_wo_fwd{j}", o_t, 'b_wo', j, _residual_then_norms(2),
                           (h, gain(small['b_bo'][j]), gain(small['mlp_norm'][layer])), (F32, BF16), a_t=True)
        h_out, nexts, mlp_saved = mlp_fwd(h_mid, n2, layer, [small['b_norm'][j + 1]] if j + 1 < n_b else [])
        saved.append((h, n1, qbt, o_t, h_mid, mlp_saved))
        h, n1 = h_out, (nexts[0] if nexts else None)

    dh, dhb, dg_final, loss_b = loss_head(h, small['final_norm'], target)
    sg['final_norm'] = dg_final[0]
    sg['mlp_norm'] = [None] * (n_a + n_b)

    def mlp_bwd(dh, dhb, h_mid, mlp_saved, layer):
        n2, u, a = mlp_saved
        du, *carried = dx_mm(f"mlp_down_dx{layer}", dhb, 'mlp_down', layer,
                             lambda acc, uu: (acc * (2.0 * jnp.maximum(uu.astype(F32), 0.0)),), (u,), (BF16,),
                             exchange=ex.mlp_carry(layer, gb))
        ex.mlp_done(layer, carried)
        dw_mm(f"mlp_down_dw{layer}", a, dhb, 'mlp_down', layer)
        dh2, dh2b, dg, cs = dx_mm(f"mlp_up_dx{layer}", du, 'mlp_up', layer, _rms_bwd_epilogue,
                                  (h_mid, dh, gain(small['mlp_norm'][layer])), (F32, BF16), n_sums=2)
        dw_mm(f"mlp_up_dw{layer}", n2, du, 'mlp_up', layer)
        sg['mlp_norm'][layer] = dg[0]
        return dh2, dh2b, cs

    dkp = jnp.zeros(kpt.shape, F32)
    dvp = jnp.zeros(vpt.shape, F32)
    sg['b_norm'], sg['b_bq'], sg['b_bo'], sg['b_sinks'] = [None] * n_b, [None] * n_b, [None] * n_b, [None] * n_b
    dbias = [None] * n_b
    for j in reversed(range(n_b)):
        layer = n_a + j
        h_in, n1, qbt, o_t, h_mid, mlp_saved = saved[layer]
        dh, dhb, cs = mlp_bwd(dh, dhb, h_mid, mlp_saved, layer)
        sg['b_bo'][j] = cs[0]
        (do_t,) = dx_mm(f"b_wo_dx{j}", dhb, 'b_wo', j, plain, (), (BF16,), out_t=True)
        dw_mm(f"b_wo_dw{j}", o_t, dhb, 'b_wo', j, x_t=True)
        dq_t, dkp, dvp, dbias[j], dsink = swa_bwd(f"swa_bwd{j}", qbt, kpt, vpt, bias_t, sink_rows[j], do_t, dkp, dvp)
        sg['b_sinks'][j] = colsum(f"sink_grad{j}", dsink.reshape(16, WINDOW).T)[0]
        sg['b_bq'][j] = rowsum(f"b_bq_grad{j}", dq_t)
        dh, dhb, dg, _ = dx_mm(f"b_q_dx{j}", dq_t, 'b_wq', j, _rms_bwd_epilogue,
                               (h_in, dh, gain(small['b_norm'][j])), (F32, BF16), a_t=True, n_sums=2)
        dw_mm(f"b_q_dw{j}", n1, dq_t, 'b_wq', j, dy_t=True)
        sg['b_norm'][j] = dg[0]
    unt = lambda t: t.reshape(2, 2 * WINDOW, 8, WINDOW).transpose(0, 2, 3, 1).reshape(bias_flat.shape)
    sg['rel_bias'] = bias_table_grad(unt(dbias[0]), unt(dbias[1]))[:, :N_BUCKETS].T

    dkv_t = jnp.concatenate([dkp[:, :, WINDOW:], dvp[:, :, WINDOW:]], axis=0).reshape(-1, s)
    sg['b_kv'] = rowsum("b_kv_grad", dkv_t)
    dkvb = dkv_t.astype(BF16)
    dh, dhb, dg, _ = dx_mm("kv_dx", dkvb, 'w_kv', 0, _rms_bwd_epilogue, (h_kv, dh, gain(small['kv_norm'])),
                           (F32, BF16), a_t=True, n_sums=2)
    dw_mm("kv_dw", nkv, dkvb, 'w_kv', 0, dy_t=True)
    sg['kv_norm'] = dg[0]

    sg['a_norm'] = [None] * n_a
    for l in reversed(range(n_a)):
        h_in, n1, qkvt, o_t, rtab, h_mid, mlp_saved = saved[l]
        dh, dhb, _ = mlp_bwd(dh, dhb, h_mid, mlp_saved, l)
        (do_t,) = dx_mm(f"a_wo_dx{l}", dhb, 'a_wo', l, plain, (), (BF16,), out_t=True)
        dw_mm(f"a_wo_dw{l}", o_t, dhb, 'a_wo', l, x_t=True)
        dqkv_t, carried = sb_bwd(f"sb_bwd{l}", qkvt, do_t.reshape(d // HEAD_DIM, HEAD_DIM, s), rtab,
                                 ex.bwd_carry(l, gb))
        ex.bwd_done(l, carried)
        dqkv_t = dqkv_t.reshape(3 * d, s)
        dw_mm(f"a_qkv_dw{l}", n1, dqkv_t, 'a_wqkv', l, dy_t=True)
        dh, dhb, dg, _, *carried = dx_mm(f"a_qkv_dx{l}", dqkv_t, 'a_wqkv', l, _rms_bwd_epilogue,
                                         (h_in, dh, gain(small['a_norm'][l])), (F32, BF16), a_t=True, n_sums=2,
                                         exchange=ex.last_carry(gb) if l == 0 else None)
        if l == 0:
            ex.last_done(carried)
        sg['a_norm'][l] = dg[0]

    small_grads = {
        'a_norm': jnp.stack(sg['a_norm']), 'kv_norm': sg['kv_norm'], 'b_kv': sg['b_kv'],
        'b_norm': jnp.stack(sg['b_norm']), 'b_bq': jnp.stack(sg['b_bq']), 'b_sinks': jnp.stack(sg['b_sinks']),
        'b_bo': jnp.stack(sg['b_bo']), 'rel_bias': sg['rel_bias'], 'mlp_norm': jnp.stack(sg['mlp_norm']),
        'final_norm': sg['final_norm'],
    }
    return loss_b, dh, gb, small_grads


def _full_shape(name, shard_shape):
    if name in COL_SHARDED:
        return shard_shape[:2] + (N_DEV * shard_shape[2],)
    nl, r, n = shard_shape
    return (nl, N_DEV, r, n)


def _as_w3_shape(name, shard_shape):
    full = _full_shape(name, shard_shape)
    return full if name in COL_SHARDED else (full[0], full[1] * full[2], full[3])


def _as_w3(name, full):
    if name in COL_SHARDED:
        return full
    nl, nd, r, n = full.shape
    return full.reshape(nl, nd * r, n)


AG_GROUPS = {
    0: (('a_wqkv', 0, 1),),
    1: (('a_wo', 0, 2), ('mlp_up', 0, 2), ('mlp_down', 0, 2), ('a_wqkv', 1, 1)),
    2: (('mlp_up', 2, 2), ('mlp_down', 2, 2), ('b_wq', 0, 2), ('b_wo', 0, 2), ('w_kv', 0, 1)),
}
RS_GROUPS = {
    'A': (('mlp_up', 2, 2), ('mlp_down', 2, 2), ('b_wq', 0, 2), ('b_wo', 0, 2), ('w_kv', 0, 1)),
    'B1': (('a_wo', 1, 1), ('mlp_up', 1, 1), ('mlp_down', 1, 1)),
    'B2': (('a_wqkv', 1, 1),),
    'B3': (('a_wo', 0, 1), ('mlp_up', 0, 1), ('mlp_down', 0, 1)),
    'C': (('a_wqkv', 0, 1),),
}
UNDER_MLP = {1: 'A', 0: 'B2'}
SIBLING_UNDER_SB_BWD = {1: 'B1'}
UNDER_SB_BWD = {1: ('A',), 0: ('B1', 'B2', 'B3')}


class _Exchanges:
    def __init__(self, full0, shards, core, chip, w3, m3, v3):
        self.wbuf = {0: {n: _as_w3(n, full0[n]) for n, _, _ in AG_GROUPS[0]}}
        self.shards, self.core, self.chip = shards, core, chip
        self.w3, self.m3, self.v3 = w3, m3, v3
        self.shard_dims = {n: w3[n].shape[1:] for n in BIG}
        self.parts = {}
        self.gfull = {}
        self.out = {}

    def weight(self, name, layer):
        for group, members in AG_GROUPS.items():
            for n, l0, nl in members:
                if n == name and l0 <= layer < l0 + nl:
                    return self.wbuf[group][name], layer - l0
        raise KeyError((name, layer))

    def fwd_carry(self, layer):
        names = [n for n, _, _ in AG_GROUPS[layer + 1]]
        shards = [self.shards[layer + 1][n] for n in names]
        return ag_direct(names, shards, [_full_shape(n, sh.shape) for n, sh in zip(names, shards)])

    def fwd_done(self, layer, carried):
        names = [n for n, _, _ in AG_GROUPS[layer + 1]]
        self.wbuf[layer + 1] = {n: _as_w3(n, f) for n, f in zip(names, carried)}

    def grad(self, name, layer):
        for group, members in RS_GROUPS.items():
            for n, l0, nl in members:
                if n == name and l0 <= layer < l0 + nl:
                    return (group, name), layer - l0, _as_w3_shape(name, (nl,) + self.shard_dims[name])
        raise KeyError((name, layer))

    def _members(self, group):
        names = [n for n, _, _ in RS_GROUPS[group]]
        return names, [(nl,) + self.shard_dims[n] for n, _, nl in RS_GROUPS[group]]

    def _sibling_carry(self, group, gb):
        names, shapes = self._members(group)
        self.gfull[group] = [gb[(group, n)].reshape(_full_shape(n, sh)) for n, sh in zip(names, shapes)]
        return sibling_exchange(names, self.gfull[group], shapes)

    def _sibling_done(self, group, recv):
        names, _ = self._members(group)
        self.parts[group] = [sibling_sum(f"rs_sibling_sum_{group}_{n}", n in COL_SHARDED, g, r, self.core)
                             for n, g, r in zip(names, self.gfull[group], recv)]

    def _sibling_stage(self, group, gb):
        ce = self._sibling_carry(group, gb)
        self._sibling_done(group, comm_call(f"rs_sibling_exchange_{group}", ce.build, ce.ins, ce.out_shapes, ce.sems))

    def mlp_carry(self, layer, gb):
        return self._sibling_carry(UNDER_MLP[layer], gb) if layer in UNDER_MLP else None

    def mlp_done(self, layer, carried):
        if layer in UNDER_MLP:
            self._sibling_done(UNDER_MLP[layer], carried)

    def bwd_carry(self, layer, gb):
        names, parts, shapes = [], [], []
        for group in UNDER_SB_BWD[layer]:
            if group not in self.parts:
                self._sibling_stage(group, gb)
            names += self._members(group)[0]
            shapes += self._members(group)[1]
            parts += self.parts[group]
        exchange = chip_exchange(names, parts, shapes)
        if layer in SIBLING_UNDER_SB_BWD:
            exchange = _together(exchange, self._sibling_carry(SIBLING_UNDER_SB_BWD[layer], gb))
        return exchange

    def bwd_done(self, layer, carried):
        for group in UNDER_SB_BWD[layer]:
            n = len(RS_GROUPS[group])
            self._adamw(group, carried[:n])
            carried = carried[n:]
        if layer in SIBLING_UNDER_SB_BWD:
            self._sibling_done(SIBLING_UNDER_SB_BWD[layer], carried)

    def last_carry(self, gb):
        self._sibling_stage('C', gb)
        names, shapes = self._members('C')
        return chip_exchange(names, self.parts['C'], shapes)

    def last_done(self, carried):
        self._adamw('C', carried)

    def _adamw(self, group, recv2):
        for (n, l0, _), p, r in zip(RS_GROUPS[group], self.parts[group], recv2):
            self.out[n] = reduce_adamw(f"adamw_{group}_{n}", p, r, self.chip, self.w3[n], self.m3[n], self.v3[n],
                                       l0, self.out.get(n))


PACK_LANES = 1024


def _small_layout(shapes):
    out, r0 = [], 0
    for n in SMALL + ['loss']:
        size = int(np.prod(shapes[n]))
        k, lanes = (size // PACK_LANES, PACK_LANES) if size % PACK_LANES == 0 else (1, size)
        assert size == k * lanes
        out.append((n, r0, k, lanes))
        r0 += k
    return out, -(-r0 // 8) * 8


def _pack_small(vals, layout, rows):
    parts = [jnp.pad(vals[n].astype(F32).reshape(k, lanes), ((0, 0), (0, PACK_LANES - lanes)))
             for n, _, k, lanes in layout]
    used = sum(k for _, _, k, _ in layout)
    return jnp.concatenate(parts + [jnp.zeros((rows - used, PACK_LANES), F32)], axis=0)


def small_adamw_all(gathered, layout, w, m, v):
    _, r, c = gathered.shape
    params = [n for n, _, _, _ in layout if n in w]
    views = {n: (k, lanes) for n, _, k, lanes in layout}
    n_in = 3 * len(params)

    def body(ga_ref, *refs):
        wmv, outs, g_ref = refs[:n_in], list(refs[n_in:-1]), refs[-1]
        g = ga_ref[0]
        for d in range(1, N_DEV):
            g = g + ga_ref[d]
        g_ref[...] = g
        for n, r0, k, lanes in layout:
            gi = g_ref[r0:r0 + k, :lanes]
            if n in w:
                i = 3 * params.index(n)
                dl, mn, vn = _adamw(wmv[i][...], gi, wmv[i + 1][...], wmv[i + 2][...])
                for val in (gi, dl, mn, vn):
                    outs.pop(0)[...] = val
            else:
                outs.pop(0)[...] = gi

    out_shape, sizes = [], []
    for n, _, k, lanes in layout:
        sizes.append(4 if n in w else 1)
        out_shape += [jax.ShapeDtypeStruct((k, lanes), F32)] * sizes[-1]
    flat = pl.pallas_call(
        body, name="adamw_small",
        out_shape=out_shape,
        scratch_shapes=[pltpu.VMEM((r, c), F32)],
        compiler_params=_params(),
    )(gathered, *[src[n].reshape(views[n]) for n in params for src in (w, m, v)])
    out, at = {}, 0
    for (n, _, _, _), size in zip(layout, sizes):
        out[n] = flat[at:at + size]
        at += size
    return out


def kernel(x, a_norm, a_wqkv, a_wo, kv_norm, w_kv, b_kv, b_norm, b_wq, b_bq, b_sinks, b_wo, b_bo, rel_bias, mlp_norm, mlp_up, mlp_down, final_norm, loss_target, m_a_norm, m_a_wqkv, m_a_wo, m_kv_norm, m_w_kv, m_b_kv, m_b_norm, m_b_wq, m_b_bq, m_b_sinks, m_b_wo, m_b_bo, m_rel_bias, m_mlp_norm, m_mlp_up, m_mlp_down, m_final_norm, v_a_norm, v_a_wqkv, v_a_wo, v_kv_norm, v_w_kv, v_b_kv, v_b_norm, v_b_wq, v_b_bq, v_b_sinks, v_b_wo, v_b_bo, v_rel_bias, v_mlp_norm, v_mlp_up, v_mlp_down, v_final_norm):
    w = dict(a_norm=a_norm, a_wqkv=a_wqkv, a_wo=a_wo, kv_norm=kv_norm, w_kv=w_kv, b_kv=b_kv, b_norm=b_norm,
             b_wq=b_wq, b_bq=b_bq, b_sinks=b_sinks, b_wo=b_wo, b_bo=b_bo, rel_bias=rel_bias, mlp_norm=mlp_norm,
             mlp_up=mlp_up, mlp_down=mlp_down, final_norm=final_norm)
    m = dict(a_norm=m_a_norm, a_wqkv=m_a_wqkv, a_wo=m_a_wo, kv_norm=m_kv_norm, w_kv=m_w_kv, b_kv=m_b_kv,
             b_norm=m_b_norm, b_wq=m_b_wq, b_bq=m_b_bq, b_sinks=m_b_sinks, b_wo=m_b_wo, b_bo=m_b_bo,
             rel_bias=m_rel_bias, mlp_norm=m_mlp_norm, mlp_up=m_mlp_up, mlp_down=m_mlp_down, final_norm=m_final_norm)
    v = dict(a_norm=v_a_norm, a_wqkv=v_a_wqkv, a_wo=v_a_wo, kv_norm=v_kv_norm, w_kv=v_w_kv, b_kv=v_b_kv,
             b_norm=v_b_norm, b_wq=v_b_wq, b_bq=v_b_bq, b_sinks=v_b_sinks, b_wo=v_b_wo, b_bo=v_b_bo,
             rel_bias=v_rel_bias, mlp_norm=v_mlp_norm, mlp_up=v_mlp_up, mlp_down=v_mlp_down, final_norm=v_final_norm)
    px, py, pc = _place()
    me = 4 * px + 2 * py + pc
    chip = (2 * px + py).astype(jnp.int32)
    core = pc.astype(jnp.int32)

    as3 = lambda t: t[None] if t.ndim == 2 else t
    w3, m3, v3 = ({n: as3(src[n]) for n in BIG} for src in (w, m, v))
    shards = {g: {n: w3[n][l0:l0 + nl].astype(BF16) for n, l0, nl in members} for g, members in AG_GROUPS.items()}
    an_pad = jnp.zeros((8, 128), F32).at[:a_norm.shape[0]].set(a_norm)
    names0 = [n for n, _, _ in AG_GROUPS[0]]
    full0 = all_gather_weights(names0 + ['a_norm'], [shards[0][n] for n in names0] + [an_pad],
                               [_full_shape(n, shards[0][n].shape) for n in names0] + [(N_DEV, 8, 128)])
    full0 = dict(zip(names0 + ['a_norm'], full0))
    n_a = a_norm.shape[0]
    small = {n: w[n] for n in SMALL}
    small['a_norm'] = full0['a_norm'][:, :n_a].transpose(1, 0, 2).reshape(n_a, -1)

    ex = _Exchanges(full0, shards, core, chip, w3, m3, v3)
    loss_b, grad_x, gb, sgrads = local_step(x[0], loss_target[0], small, ex)
    out = {n: [t.reshape(w[n].shape) for t in bufs] for n, bufs in ex.out.items()}

    sgrads['loss'] = loss_b[0, :1]
    shapes = {n: w[n].shape for n in SMALL}
    shapes['a_norm'] = (n_a, a_norm.shape[1] * N_DEV)
    shapes['loss'] = (1,)
    layout, rows = _small_layout(shapes)
    gathered = all_gather_rows(_pack_small(sgrads, layout, rows))
    replicated = [n for n in SMALL if n != 'a_norm']
    res = small_adamw_all(gathered, layout, *({n: src[n] for n in replicated} for src in (w, m, v)))
    sm = [{n: res[n][i].reshape(shapes[n]) for n in replicated} for i in range(4)]
    g_an = lax.dynamic_slice_in_dim(res['a_norm'][0], me * a_norm.shape[1], a_norm.shape[1], axis=1)
    pad = lambda t: jnp.zeros((8, 128), F32).at[:n_a].set(t)
    gathered_an = jnp.zeros((N_DEV, 8, 128), F32).at[0].set(pad(g_an))
    an = small_adamw("adamw_a_norm", gathered_an, pad(a_norm), pad(m_a_norm), pad(v_a_norm))
    for i in range(4):
        sm[i]['a_norm'] = an[i][:n_a]
    for n in BIG:
        for i in range(4):
            sm[i][n] = out[n][i]
    loss = res['loss'][0][0, 0]
    return (loss, grad_x[None], *[sm[0][n] for n in WEIGHTS], *[sm[1][n] for n in WEIGHTS],
            *[sm[2][n] for n in WEIGHTS], *[sm[3][n] for n in WEIGHTS])
```

```python
import math

import numpy as np
import jax
import jax.numpy as jnp
from jax import lax
from jax.experimental import pallas as pl
from jax.experimental.pallas import tpu as pltpu

F32 = jnp.float32
BF16 = jnp.bfloat16
MESH = pl.DeviceIdType.MESH

N_DEV = 8
HEAD_DIM = 64
WINDOW = 128
N_BUCKETS = 32
EPS = 1e-5
NEG_INF = -1e30
Q_SCALE = 1.0 / math.sqrt(HEAD_DIM)
LOG2E = 1.4426950408889634

ADAM_LR, ADAM_B1, ADAM_B2, ADAM_EPS, ADAM_WD, ADAM_STEP = 0.001, 0.9, 0.999, 1e-08, 0.01, 10

SB_BQ = 512
SB_BK = 128
SB_DEAD = 160.0
SB_UNSEEN = 1e30
ROW_TILE = 512
VMEM_LIMIT = 56 * 1024 * 1024

WEIGHTS = ['a_norm', 'a_wqkv', 'a_wo', 'kv_norm', 'w_kv', 'b_kv', 'b_norm', 'b_wq', 'b_bq', 'b_sinks', 'b_wo',
           'b_bo', 'rel_bias', 'mlp_norm', 'mlp_up', 'mlp_down', 'final_norm']
BIG = ['a_wqkv', 'a_wo', 'w_kv', 'b_wq', 'b_wo', 'mlp_up', 'mlp_down']
COL_SHARDED = ('a_wqkv', 'mlp_up')
SMALL = ['a_norm', 'kv_norm', 'b_kv', 'b_norm', 'b_bq', 'b_sinks', 'b_bo', 'rel_bias', 'mlp_norm', 'final_norm']


def _params(sem=None):
    return pltpu.CompilerParams(dimension_semantics=sem, vmem_limit_bytes=VMEM_LIMIT)


def _pick(n, cands):
    for c in cands:
        if n % c == 0:
            return c
    raise ValueError(n)


def _tile(n, want):
    return n if n <= want else _pick(n, (want, want // 2, want // 4))


MM_TILE_BUDGET = 36 * 1024 * 1024


def _row_tile(m, contraction, cols, streams):
    weight = 2 * contraction * cols * 2
    for rows in (2048, 1024, 512):
        if m % rows == 0 and weight + 2 * rows * (2 * contraction + cols * sum(streams)) <= MM_TILE_BUDGET:
            return rows
    return _tile(m, 512)


def mm_nn(name, a, w3, layer, epilogue, extras, out_dtypes, a_t=False, out_t=False):
    k, m = a.shape if a_t else a.shape[::-1]
    _, kw, n = w3.shape
    assert kw == k
    tn = _tile(n, 1024)
    tm = _row_tile(m, k, tn, [jnp.dtype(t).itemsize for t in out_dtypes]
                   + [e.dtype.itemsize for e in extras if e.size == m * n])
    ne, no = len(extras), len(out_dtypes)
    a_dim = 0 if a_t else 1

    def body(a_ref, w_ref, *rest):
        ex, outs = rest[:ne], rest[ne:ne + no]
        if out_t:
            acc = lax.dot_general(w_ref[...], a_ref[...], (((0,), (a_dim,)), ((), ())), preferred_element_type=F32)
        else:
            acc = lax.dot_general(a_ref[...], w_ref[...], (((a_dim,), (0,)), ((), ())), preferred_element_type=F32)
        for o, r in zip(outs, epilogue(acc, *[e[...] for e in ex])):
            o[...] = r.astype(o.dtype)

    if out_t:
        tile = pl.BlockSpec((tn, tm), lambda i, j: (j, i))
        vec = pl.BlockSpec((tn, 1), lambda i, j: (j, 0))
        out_shape = (n, m)
    else:
        tile = pl.BlockSpec((tm, tn), lambda i, j: (i, j))
        vec = pl.BlockSpec((1, tn), lambda i, j: (0, j))
        out_shape = (m, n)
    a_spec = pl.BlockSpec((k, tm), lambda i, j: (0, i)) if a_t else pl.BlockSpec((tm, k), lambda i, j: (i, 0))
    return pl.pallas_call(
        body, name=name, grid=(m // tm, n // tn),
        in_specs=[a_spec, pl.BlockSpec((None, k, tn), lambda i, j: (layer, 0, j))]
        + [tile if e.shape == out_shape else vec for e in extras],
        out_specs=[tile] * no,
        out_shape=[jax.ShapeDtypeStruct(out_shape, d) for d in out_dtypes],
        compiler_params=_params(("parallel", "parallel")),
    )(a, w3, *extras)


def mm_nt(name, dy, w3, layer, epilogue, extras, out_dtypes, a_t=False, out_t=False, n_sums=0, exchange=None):
    n, m = dy.shape if a_t else dy.shape[::-1]
    _, k, nw = w3.shape
    assert nw == n and not (out_t and n_sums)
    tko = _tile(k, 1024)
    tm = _row_tile(m, n, tko, [jnp.dtype(t).itemsize for t in out_dtypes]
                   + [e.dtype.itemsize for e in extras if e.size == m * k])
    ne, no = len(extras), len(out_dtypes)
    a_dim = 0 if a_t else 1

    def body(a_ref, w_ref, *rest):
        ex = rest[:ne]
        at = lambda step: (pl.program_id(0) == step[0]) & (pl.program_id(1) == step[1])
        results, _, start_carried, wait_carried = _carried(exchange, rest[ne:], no + n_sums, 0, at((0, 0)),
                                                           at((m // tm - 1, k // tko - 1)))
        outs, sums = results[:no], results[no:]
        start_carried()
        if out_t:
            acc = lax.dot_general(w_ref[...], a_ref[...], (((1,), (a_dim,)), ((), ())), preferred_element_type=F32)
        else:
            acc = lax.dot_general(a_ref[...], w_ref[...], (((a_dim,), (1,)), ((), ())), preferred_element_type=F32)
        res = epilogue(acc, *[e[...] for e in ex])
        for o, v in zip(outs, res):
            o[...] = v.astype(o.dtype)
        if n_sums:
            @pl.when(pl.program_id(0) == 0)
            def _():
                for o in sums:
                    o[...] = jnp.zeros_like(o)

            for o, v in zip(sums, res[no:]):
                o[...] += v
        wait_carried()

    if out_t:
        tile = pl.BlockSpec((tko, tm), lambda i, ko: (ko, i))
        out_shape = (k, m)
    else:
        tile = pl.BlockSpec((tm, tko), lambda i, ko: (i, ko))
        out_shape = (m, k)
    vec = pl.BlockSpec((1, tko), lambda i, ko: (0, ko))
    a_spec = pl.BlockSpec((n, tm), lambda i, ko: (0, i)) if a_t else pl.BlockSpec((tm, n), lambda i, ko: (i, 0))
    hbm = pl.BlockSpec(memory_space=pl.ANY)
    c_ins, c_outs, c_sems = (exchange.ins, exchange.out_shapes, exchange.sems) if exchange else ([], [], [])
    sequential = n_sums or exchange
    return pl.pallas_call(
        body, name=name, grid=(m // tm, k // tko),
        in_specs=[a_spec, pl.BlockSpec((None, tko, n), lambda i, ko: (layer, ko, 0))]
        + [tile if e.shape == out_shape else vec for e in extras] + [hbm] * len(c_ins),
        out_specs=[tile] * no + [vec] * n_sums + [hbm] * len(c_outs),
        out_shape=[jax.ShapeDtypeStruct(out_shape, d) for d in out_dtypes] + [jax.ShapeDtypeStruct((1, k), F32)] * n_sums
        + c_outs,
        scratch_shapes=c_sems,
        compiler_params=_params(("arbitrary" if sequential else "parallel", "arbitrary" if exchange else "parallel")),
    )(dy, w3, *extras, *c_ins)


def mm_tn(name, x, dy, gbuf, shape, layer, x_t=False, dy_t=False):
    k, s = x.shape if x_t else x.shape[::-1]
    _, kw, n = shape
    assert kw == k and dy.shape == ((n, s) if dy_t else (s, n))
    tkk = _tile(k, 512)
    tn = _tile(n, 1024)

    def body(x_ref, dy_ref, *rest):
        g_out = rest[-1]
        g_out[...] = lax.dot_general(x_ref[...], dy_ref[...], (((1 if x_t else 0,), (1 if dy_t else 0,)), ((), ())),
                                     preferred_element_type=F32).astype(g_out.dtype)

    prev = [] if gbuf is None else [gbuf]
    x_spec = pl.BlockSpec((tkk, s), lambda ki, j: (ki, 0)) if x_t else pl.BlockSpec((s, tkk), lambda ki, j: (0, ki))
    dy_spec = pl.BlockSpec((tn, s), lambda ki, j: (j, 0)) if dy_t else pl.BlockSpec((s, tn), lambda ki, j: (0, j))
    return pl.pallas_call(
        body, name=name, grid=(k // tkk, n // tn),
        in_specs=[x_spec, dy_spec] + [pl.BlockSpec(memory_space=pl.ANY)] * len(prev),
        out_specs=pl.BlockSpec((None, tkk, tn), lambda ki, j: (layer, ki, j)),
        out_shape=jax.ShapeDtypeStruct(shape, BF16),
        input_output_aliases={2: 0} if prev else {},
        compiler_params=_params(("parallel", "parallel")),
    )(x, dy, *prev)


def rms_fwd(name, h, g):
    s, d = h.shape
    tr = _pick(s, (ROW_TILE, 256, 128))

    def body(h_ref, g_ref, o_ref):
        x = h_ref[...]
        r = lax.rsqrt(jnp.mean(x * x, axis=-1, keepdims=True) + EPS)
        o_ref[...] = (x * r * g_ref[...]).astype(o_ref.dtype)

    return pl.pallas_call(
        body, name=name, grid=(s // tr,),
        in_specs=[pl.BlockSpec((tr, d), lambda i: (i, 0)), pl.BlockSpec((1, d), lambda i: (0, 0))],
        out_specs=pl.BlockSpec((tr, d), lambda i: (i, 0)),
        out_shape=jax.ShapeDtypeStruct((s, d), BF16),
        compiler_params=_params(("parallel",)),
    )(h, g.reshape(1, d))


def loss_head(h, g, target):
    s, d = h.shape
    tr = _pick(s, (ROW_TILE, 256, 128))

    def body(h_ref, g_ref, t_ref, dx_ref, dxb_ref, dg_ref, loss_ref):
        i = pl.program_id(0)
        x = h_ref[...]
        r = lax.rsqrt(jnp.mean(x * x, axis=-1, keepdims=True) + EPS)
        xh = x * r
        gw = g_ref[...]
        err = xh * gw - t_ref[...]
        dn_ = err * (1.0 / d)
        dyg = dn_ * gw
        dx = r * (dyg - xh * jnp.mean(dyg * xh, axis=-1, keepdims=True))
        dx_ref[...] = dx
        dxb_ref[...] = dx.astype(BF16)

        @pl.when(i == 0)
        def _():
            dg_ref[...] = jnp.zeros_like(dg_ref)
            loss_ref[...] = jnp.zeros_like(loss_ref)

        dg_ref[...] += jnp.sum(dn_ * xh, axis=0, keepdims=True)
        per_row = jnp.sum(err * err, axis=-1, keepdims=True) * (0.5 / d)
        loss_ref[...] += jnp.broadcast_to(jnp.sum(per_row, axis=0, keepdims=True), loss_ref.shape)

    row = pl.BlockSpec((tr, d), lambda i: (i, 0))
    vec = pl.BlockSpec((1, d), lambda i: (0, 0))
    return pl.pallas_call(
        body, name="loss_head", grid=(s // tr,),
        in_specs=[row, vec, row],
        out_specs=[row, row, vec, pl.BlockSpec((1, 128), lambda i: (0, 0))],
        out_shape=[jax.ShapeDtypeStruct((s, d), F32), jax.ShapeDtypeStruct((s, d), BF16),
                   jax.ShapeDtypeStruct((1, d), F32), jax.ShapeDtypeStruct((1, 128), F32)],
        compiler_params=_params(("arbitrary",)),
    )(h, g.reshape(1, d), target)


def colsum(name, x):
    s, n = x.shape
    tr = _pick(s, (ROW_TILE, 256, 128))

    def body(x_ref, o_ref):
        @pl.when(pl.program_id(0) == 0)
        def _():
            o_ref[...] = jnp.zeros_like(o_ref)

        o_ref[...] += jnp.sum(x_ref[...].astype(F32), axis=0, keepdims=True)

    return pl.pallas_call(
        body, name=name, grid=(s // tr,),
        in_specs=[pl.BlockSpec((tr, n), lambda i: (i, 0))],
        out_specs=pl.BlockSpec((1, n), lambda i: (0, 0)),
        out_shape=jax.ShapeDtypeStruct((1, n), F32),
        compiler_params=_params(("arbitrary",)),
    )(x)


def rowsum(name, x):
    n, s = x.shape
    ts = _pick(s, (1024, 512, 256, 128))

    def body(x_ref, o_ref):
        @pl.when(pl.program_id(0) == 0)
        def _():
            o_ref[...] = jnp.zeros_like(o_ref)

        o_ref[...] += jnp.sum(x_ref[...].astype(F32), axis=1, keepdims=True)

    return pl.pallas_call(
        body, name=name, grid=(s // ts,),
        in_specs=[pl.BlockSpec((n, ts), lambda i: (0, i))],
        out_specs=pl.BlockSpec((n, 1), lambda i: (0, 0)),
        out_shape=jax.ShapeDtypeStruct((n, 1), F32),
        compiler_params=_params(("arbitrary",)),
    )(x)[:, 0]


def _tri_rows(reverse):
    i = np.arange(SB_BK)
    tri = (i[None, :] >= i[:, None]) if reverse else (i[None, :] <= i[:, None])
    tri = np.concatenate([tri, tri], axis=1)
    return jnp.asarray(np.concatenate([tri, np.ones((8, 2 * SB_BK), bool)], axis=0), BF16)


def _hi_lo_rows(x):
    hi = x.astype(BF16)
    lo = (x - hi.astype(F32)).astype(BF16)
    return jnp.concatenate([hi, lo], axis=0)


def _softplus2(zs):
    neg_abs = lax.bitcast_convert_type(lax.bitcast_convert_type(zs, jnp.uint32) | jnp.uint32(0x80000000), F32)
    return jnp.maximum(zs, 0.0) + jnp.log2(1.0 + jnp.exp2(neg_abs))


def _pair_mask(first_rel_block, bq):
    key = lax.broadcasted_iota(jnp.int32, (2 * SB_BK, bq), 0) + first_rel_block * SB_BK
    qry = lax.broadcasted_iota(jnp.int32, (2 * SB_BK, bq), 1)
    return key < qry


def _row_of(table8, sub8, r):
    return jnp.sum(jnp.where(sub8 == r, table8, 0.0), axis=0, keepdims=True)


def _keys(j0):
    return pl.ds(pl.multiple_of(j0 * SB_BK, 2 * SB_BK), 2 * SB_BK)


class Carry:
    def __init__(self, build, ins, out_shapes, sems, then=None):
        self.build, self.ins, self.out_shapes, self.sems = build, list(ins), list(out_shapes), list(sems)
        self.then = then


def _together(a, b):
    assert a.then is None and b.then is None
    ni, no, ns = len(a.ins), len(a.out_shapes), len(a.sems)

    def build(ins, outs, *sems):
        return a.build(ins[:ni], outs[:no], *sems[:ns]) + b.build(ins[ni:], outs[no:], *sems[ns:])

    return Carry(build, a.ins + b.ins, a.out_shapes + b.out_shapes, a.sems + b.sems)


def _carried(carry, rest, n_out, n_scratch, first, last):
    n_ci = len(carry.ins) if carry else 0
    n_co = len(carry.out_shapes) if carry else 0
    cin, outs = rest[:n_ci], rest[n_ci:n_ci + n_out]
    cout = rest[n_ci + n_out:n_ci + n_out + n_co]
    scratch = rest[n_ci + n_out + n_co:n_ci + n_out + n_co + n_scratch]
    csems = rest[n_ci + n_out + n_co + n_scratch:]

    def start():
        if carry:
            @pl.when(first)
            def _():
                for cp in carry.build(cin, cout, *csems):
                    cp.start()

    def wait():
        if carry:
            @pl.when(last)
            def _():
                for cp in carry.build(cin, cout, *csems):
                    cp.wait()
                if carry.then:
                    second = carry.then(cin, cout, *csems)
                    for cp in second:
                        cp.start()
                    for cp in second:
                        cp.wait()

    return outs, scratch, start, wait


def _contract0(a, b):
    return lax.dot_general(a, b, (((0,), (0,)), ((), ())), preferred_element_type=F32)


def _contract1(a, b):
    return lax.dot_general(a, b, (((1,), (1,)), ((), ())), preferred_element_type=F32)


def sb_fwd(name, qkvt, exchange=None):
    nh, dh, s = qkvt.shape[0] // 3, qkvt.shape[1], qkvt.shape[2]
    bq = SB_BQ
    per_q = bq // SB_BK
    nkb = s // SB_BK
    assert s % bq == 0 and per_q == 4 and nkb % 8 == 0

    def body(q_ref, k_ref, v_ref, a_ref, *rest):
        head = pl.program_id(0)
        (o_ref, rtab_ref), (acc, zbuf, wbuf), start_carried, wait_carried = _carried(
            exchange, rest, 2, 3, head == 0, head == nh - 1)
        start_carried()
        tri = a_ref[...]
        sub8 = lax.broadcasted_iota(jnp.int32, (8, bq), 0)
        rtab_ref[...] = jnp.full(rtab_ref.shape, SB_UNSEEN, F32)
        kf = k_ref[...].astype(F32)
        k_max2 = jnp.max(jnp.sum(kf * kf, axis=0, keepdims=True), axis=1, keepdims=True)

        def query_block(i, _):
            lanes = pl.ds(pl.multiple_of(i * bq, bq), bq)
            qb = q_ref[:, lanes] * Q_SCALE
            acc[...] = jnp.zeros_like(acc)
            qf = qb.astype(F32)
            bound = jnp.sqrt(jnp.sum(qf * qf, axis=0, keepdims=True) * k_max2) * (1.001 * LOG2E)

            def scores(j0):
                return _contract0(k_ref[:, _keys(j0)], qb) * LOG2E

            def pair(j0, slot, run, rt8, mask, has_prev):
                zs = zbuf[slot]
                zbuf[1 - slot] = scores(jnp.maximum(j0 - 2, 0))
                if has_prev:
                    acc[...] += jnp.dot(v_ref[:, _keys(j0 + 2)], wbuf[1 - slot], preferred_element_type=F32)
                p = _softplus2(zs)
                if mask is not None:
                    p = jnp.where(mask, p, 0.0)
                cr1 = jnp.dot(tri, _hi_lo_rows(p[SB_BK:]), preferred_element_type=F32)
                cr0 = jnp.dot(tri, _hi_lo_rows(p[:SB_BK]), preferred_element_type=F32)
                run1 = run + cr1[SB_BK:SB_BK + 1]
                w = jnp.exp2(jnp.concatenate([zs[:SB_BK] - cr0[:SB_BK] - run1, zs[SB_BK:] - cr1[:SB_BK] - run],
                                             axis=0))
                if mask is not None:
                    w = jnp.where(mask, w, 0.0)
                wbuf[slot] = w.astype(BF16)
                rt8 = jnp.where(j0 % 8 == 6, SB_UNSEEN, rt8)
                rt8 = jnp.where(sub8 == (j0 + 1) % 8, run, jnp.where(sub8 == j0 % 8, run1, rt8))
                rtab_ref[pl.ds(pl.multiple_of((j0 // 8) * 8, 8), 8), lanes] = rt8
                return run1 + cr0[SB_BK:SB_BK + 1], rt8

            def alive(run):
                return jnp.min(run - bound) < SB_DEAD

            top = i * per_q
            zbuf[0] = scores(top + 2)
            state = (jnp.zeros((1, bq), F32), jnp.full((8, bq), SB_UNSEEN, F32))
            state = pair(top + 2, 0, *state, _pair_mask(2, bq), False)
            state = pair(top, 1, *state, _pair_mask(0, bq), True)

            def step(c):
                it, pairs, _, run, rt8 = c
                j0 = top - 2 - 4 * it
                run, rt8 = pair(j0, 0, run, rt8, None, True)
                go = alive(run)
                run, rt8 = lax.cond(go, lambda r, t: pair(j0 - 2, 1, r, t, None, True), lambda r, t: (r, t), run, rt8)
                return it + 1, pairs + 1 + go.astype(jnp.int32), go & alive(run), run, rt8

            pairs = lax.while_loop(lambda c: (c[0] < i) & c[2], step, (0, 0, alive(state[0]), *state))[1]
            acc[...] += jnp.dot(v_ref[:, _keys(top - 2 * pairs)], wbuf[(pairs + 1) % 2], preferred_element_type=F32)
            o_ref[:, lanes] = acc[...].astype(o_ref.dtype)
            return 0

        lax.fori_loop(0, s // bq, query_block, 0)
        wait_carried()

    def head_spec(offset, rows):
        return pl.BlockSpec((None, rows, s), lambda h: (h + offset, 0, 0))

    hbm = pl.BlockSpec(memory_space=pl.ANY)
    c_ins, c_outs, c_sems = (exchange.ins, exchange.out_shapes, exchange.sems) if exchange else ([], [], [])
    outs = pl.pallas_call(
        body, name=name, grid=(nh,),
        in_specs=[head_spec(0, dh), head_spec(nh, dh), head_spec(2 * nh, dh),
                  pl.BlockSpec((SB_BK + 8, 2 * SB_BK), lambda h: (0, 0))] + [hbm] * len(c_ins),
        out_specs=[head_spec(0, dh), head_spec(0, nkb)] + [hbm] * len(c_outs),
        out_shape=[jax.ShapeDtypeStruct((nh, dh, s), BF16), jax.ShapeDtypeStruct((nh, nkb, s), F32)] + c_outs,
        scratch_shapes=[pltpu.VMEM((dh, bq), F32), pltpu.VMEM((2, 2 * SB_BK, bq), F32),
                        pltpu.VMEM((2, 2 * SB_BK, bq), BF16)] + c_sems,
        compiler_params=_params(("arbitrary",)),
    )(qkvt, qkvt, qkvt, _tri_rows(True), *c_ins)
    return outs[0], outs[1], outs[2:]


def sb_bwd(name, qkvt, dot_, rtab, exchange=None):
    nh, dh, s = qkvt.shape[0] // 3, qkvt.shape[1], qkvt.shape[2]
    bq = SB_BQ
    per_q = bq // SB_BK
    nkb = s // SB_BK

    def body(qt_ref, kt_ref, vt_ref, dot_ref, rtab_ref, ar_ref, af_ref, *rest):
        head = pl.program_id(0)
        (dqkv_ref,), (dq_acc, dk_acc, dv_acc, zbuf, dwbuf, dzbuf, wbuf), start_carried, wait_carried = \
            _carried(exchange, rest, 1, 7, head == 0, head == nh - 1)
        dq_ref, dk_ref, dv_ref = dqkv_ref.at[0], dqkv_ref.at[1], dqkv_ref.at[2]
        start_carried()
        dk_acc[...] = jnp.zeros_like(dk_acc)
        dv_acc[...] = jnp.zeros_like(dv_acc)
        tri_rev = ar_ref[...][:SB_BK]
        tri_fwd = af_ref[...]
        sub8 = lax.broadcasted_iota(jnp.int32, (8, bq), 0)

        def query_block(i, _):
            lanes = pl.ds(pl.multiple_of(i * bq, bq), bq)
            qtb = qt_ref[:, lanes] * Q_SCALE
            dotb = dot_ref[:, lanes]
            dq_acc[...] = jnp.zeros_like(dq_acc)
            last_j = i * per_q + 2
            seen = jnp.max(jnp.where(rtab_ref[:, lanes] < 0.1 * SB_UNSEEN, 1.0, 0.0), axis=1, keepdims=True)
            pairs = jnp.clip((jnp.sum(seen).astype(jnp.int32) - per_q) // 2, 0, 2 * i)
            odd = pairs % 2
            first_j = i * per_q - 2 * pairs

            def issue(j0, slot):
                zbuf[slot] = _contract0(kt_ref[:, _keys(j0)], qtb) * LOG2E
                dwbuf[slot] = _contract0(vt_ref[:, _keys(j0)], dotb)

            def retire(j0, slot):
                keys = _keys(j0)
                dq_acc[...] += jnp.dot(kt_ref[:, keys], dzbuf[slot], preferred_element_type=F32)
                dk_acc[:, keys] += _contract1(qtb, dzbuf[slot])
                dv_acc[:, keys] += _contract1(dotb, wbuf[slot])

            def pair(j0, slot, g_run, mask):
                zs = zbuf[slot]
                dw = dwbuf[slot]
                issue(jnp.minimum(j0 + 2, last_j), 1 - slot)
                retire(jnp.maximum(j0 - 2, first_j), 1 - slot)
                p_raw = _softplus2(zs)
                p = p_raw if mask is None else jnp.where(mask, p_raw, 0.0)
                c0 = jnp.dot(tri_rev, _hi_lo_rows(p[:SB_BK]), preferred_element_type=F32)
                c1 = jnp.dot(tri_rev, _hi_lo_rows(p[SB_BK:]), preferred_element_type=F32)
                rt8 = rtab_ref[pl.ds(pl.multiple_of((j0 // 8) * 8, 8), 8), lanes]
                r0 = _row_of(rt8, sub8, j0 % 8)
                r1 = _row_of(rt8, sub8, (j0 + 1) % 8)
                w = jnp.exp2(jnp.concatenate([zs[:SB_BK] - c0 - r0, zs[SB_BK:] - c1 - r1], axis=0))
                if mask is not None:
                    w = jnp.where(mask, w, 0.0)
                g = w * dw
                gg0 = jnp.dot(tri_fwd, _hi_lo_rows(g[:SB_BK]), preferred_element_type=F32)
                gg1 = jnp.dot(tri_fwd, _hi_lo_rows(g[SB_BK:]), preferred_element_type=F32)
                g_run1 = g_run + gg0[SB_BK:SB_BK + 1]
                g_pre = jnp.concatenate([gg0[:SB_BK] + g_run, gg1[:SB_BK] + g_run1], axis=0)
                dz = g - jnp.exp2(zs - p_raw) * g_pre
                if mask is not None:
                    dz = jnp.where(mask, dz, 0.0)
                dzbuf[slot] = dz.astype(BF16)
                wbuf[slot] = w.astype(BF16)
                return g_run1 + gg1[SB_BK:SB_BK + 1]

            issue(first_j, odd)
            dzbuf[...] = jnp.zeros(dzbuf.shape, BF16)
            wbuf[...] = jnp.zeros(wbuf.shape, BF16)

            def step(it, g_run):
                g_run = pair(4 * it, 0, g_run, None)
                return pair(4 * it + 2, 1, g_run, None)

            g_run = lax.cond(odd == 1, lambda g: pair(first_j, 1, g, None), lambda g: g, jnp.zeros((1, bq), F32))
            g_run = lax.fori_loop(i - pairs // 2, i, step, g_run)
            g_run = pair(last_j - 2, 0, g_run, _pair_mask(0, bq))
            pair(last_j, 1, g_run, _pair_mask(2, bq))
            retire(last_j, 1)
            dq_ref[:, lanes] = (dq_acc[...] * Q_SCALE).astype(dq_ref.dtype)
            return 0

        lax.fori_loop(0, s // bq, query_block, 0)
        dk_ref[...] = dk_acc[...].astype(dk_ref.dtype)
        dv_ref[...] = dv_acc[...].astype(dv_ref.dtype)
        wait_carried()

    def head_spec(offset, rows):
        return pl.BlockSpec((None, rows, s), lambda h: (h + offset, 0, 0))

    aspec = pl.BlockSpec((SB_BK + 8, 2 * SB_BK), lambda h: (0, 0))
    pair_f32 = pltpu.VMEM((2, 2 * SB_BK, bq), F32)
    pair_bf16 = pltpu.VMEM((2, 2 * SB_BK, bq), BF16)
    hbm = pl.BlockSpec(memory_space=pl.ANY)
    c_ins, c_outs, c_sems = (exchange.ins, exchange.out_shapes, exchange.sems) if exchange else ([], [], [])
    outs = pl.pallas_call(
        body, name=name, grid=(nh,),
        in_specs=[head_spec(0, dh), head_spec(nh, dh), head_spec(2 * nh, dh), head_spec(0, dh), head_spec(0, nkb),
                  aspec, aspec] + [hbm] * len(c_ins),
        out_specs=[pl.BlockSpec((3, None, dh, s), lambda h: (0, h, 0, 0))] + [hbm] * len(c_outs),
        out_shape=[jax.ShapeDtypeStruct((3, nh, dh, s), BF16)] + c_outs,
        scratch_shapes=[pltpu.VMEM((dh, bq), F32), pltpu.VMEM((dh, s), F32), pltpu.VMEM((dh, s), F32),
                        pair_f32, pair_f32, pair_bf16, pair_bf16] + c_sems,
        compiler_params=_params(("arbitrary",)),
    )(qkvt, qkvt, qkvt, dot_, rtab, _tri_rows(True), _tri_rows(False), *c_ins)
    return outs[0], outs[1:]


SWA_QB = 4


def _band_valid():
    kj = np.arange(2 * WINDOW)[:, None]
    dist = (np.arange(8 * WINDOW)[None, :] % WINDOW) + WINDOW - kj
    inside = (dist >= 0) & (dist < WINDOW)
    return jnp.asarray(np.stack([inside & (kj >= WINDOW), inside]), F32)


def _swa_probs(qt, kt, bias_t, valid, sink):
    sc = jnp.where(valid > 0.5, _contract0(kt, qt) + bias_t, NEG_INF)
    mx = jnp.maximum(jnp.max(sc, axis=0, keepdims=True), sink)
    p = jnp.exp(sc - mx)
    p_sink = jnp.exp(sink - mx)
    inv = 1.0 / (jnp.sum(p, axis=0, keepdims=True) + p_sink)
    return p, p_sink, inv


def _band(i):
    return pl.ds(pl.multiple_of(i * WINDOW, WINDOW), 2 * WINDOW)


def _heads_to_lanes(blk):
    return jnp.concatenate([blk[r * HEAD_DIM:(r + 1) * HEAD_DIM] for r in range(8)], axis=1)


def _lanes_to_heads(t):
    return jnp.concatenate([t[:, r * WINDOW:(r + 1) * WINDOW] for r in range(8)], axis=0)


def swa_fwd(name, qt, kpt, vpt, bias_t, sink_row):
    d, s = qt.shape
    ng, dh, sp = kpt.shape
    rows, cols = d // ng, SWA_QB * WINDOW
    assert (s // WINDOW) % SWA_QB == 0

    def body(q_ref, k_ref, v_ref, bias_ref, valid_ref, sink_ref, o_ref):
        for u in range(SWA_QB):
            i = pl.program_id(1) * SWA_QB + u
            lanes = slice(u * WINDOW, (u + 1) * WINDOW)
            qb = _heads_to_lanes(q_ref[:, lanes]) * Q_SCALE
            p, _, inv = _swa_probs(qb, k_ref[:, _band(i)], bias_ref[...], valid_ref[jnp.minimum(i, 1)], sink_ref[...])
            o_t = jnp.dot(v_ref[:, _band(i)], p.astype(BF16), preferred_element_type=F32) * inv
            o_ref[:, lanes] = _lanes_to_heads(o_t).astype(o_ref.dtype)

    qspec = pl.BlockSpec((rows, cols), lambda g, i: (g, i))
    kspec = pl.BlockSpec((None, dh, sp), lambda g, i: (g, 0, 0))
    return pl.pallas_call(
        body, name=name, grid=(ng, s // cols),
        in_specs=[qspec, kspec, kspec, pl.BlockSpec((None, 2 * WINDOW, 8 * WINDOW), lambda g, i: (g, 0, 0)),
                  pl.BlockSpec((2, 2 * WINDOW, 8 * WINDOW), lambda g, i: (0, 0, 0)),
                  pl.BlockSpec((None, 1, 8 * WINDOW), lambda g, i: (g, 0, 0))],
        out_specs=qspec,
        out_shape=jax.ShapeDtypeStruct(qt.shape, BF16),
        compiler_params=_params(("parallel", "arbitrary")),
    )(qt, kpt, vpt, bias_t, _band_valid(), sink_row)


def swa_bwd(name, qt, kpt, vpt, bias_t, sink_row, dot_, dk_in, dv_in):
    d, s = qt.shape
    ng, dh, sp = kpt.shape
    rows, cols = d // ng, SWA_QB * WINDOW

    def body(q_ref, k_ref, v_ref, bias_ref, valid_ref, sink_ref, do_ref, dki_ref, dvi_ref,
             dq_ref, dk_ref, dv_ref, db_ref, ds_ref):
        @pl.when(pl.program_id(1) == 0)
        def _():
            dk_ref[...] = dki_ref[...]
            dv_ref[...] = dvi_ref[...]
            db_ref[...] = jnp.zeros_like(db_ref)
            ds_ref[...] = jnp.zeros_like(ds_ref)

        for u in range(SWA_QB):
            i = pl.program_id(1) * SWA_QB + u
            band = _band(i)
            lanes = slice(u * WINDOW, (u + 1) * WINDOW)
            qb = _heads_to_lanes(q_ref[:, lanes]) * Q_SCALE
            dob = _heads_to_lanes(do_ref[:, lanes])
            kt = k_ref[:, band]
            p, p_sink, inv = _swa_probs(qb, kt, bias_ref[...], valid_ref[jnp.minimum(i, 1)], sink_ref[...])
            p = p * inv
            dp = _contract0(v_ref[:, band], dob)
            delta = jnp.sum(p * dp, axis=0, keepdims=True)
            dsc = p * (dp - delta)
            ds_ref[...] -= p_sink * inv * delta
            db_ref[...] += dsc
            dscb = dsc.astype(BF16)
            dq_t = jnp.dot(kt, dscb, preferred_element_type=F32) * Q_SCALE
            dq_ref[:, lanes] = _lanes_to_heads(dq_t).astype(dq_ref.dtype)
            dk_ref[:, band] += _contract1(qb, dscb)
            dv_ref[:, band] += _contract1(dob, p.astype(BF16))

    qspec = pl.BlockSpec((rows, cols), lambda g, i: (g, i))
    kspec = pl.BlockSpec((None, dh, sp), lambda g, i: (g, 0, 0))
    bspec = pl.BlockSpec((None, 2 * WINDOW, 8 * WINDOW), lambda g, i: (g, 0, 0))
    sspec = pl.BlockSpec((None, 1, 8 * WINDOW), lambda g, i: (g, 0, 0))
    return pl.pallas_call(
        body, name=name, grid=(ng, s // cols),
        in_specs=[qspec, kspec, kspec, bspec, pl.BlockSpec((2, 2 * WINDOW, 8 * WINDOW), lambda g, i: (0, 0, 0)), sspec,
                  qspec, kspec, kspec],
        out_specs=[qspec, kspec, kspec, bspec, sspec],
        out_shape=[jax.ShapeDtypeStruct(qt.shape, BF16), jax.ShapeDtypeStruct(kpt.shape, F32),
                   jax.ShapeDtypeStruct(kpt.shape, F32), jax.ShapeDtypeStruct(bias_t.shape, F32),
                   jax.ShapeDtypeStruct(sink_row.shape, F32)],
        compiler_params=_params(("parallel", "arbitrary")),
    )(qt, kpt, vpt, bias_t, _band_valid(), sink_row, dot_, dk_in, dv_in)


def _bucket_onehot():
    qi = np.arange(WINDOW)[:, None]
    kj = np.arange(2 * WINDOW)[None, :]
    n = np.maximum(qi + WINDOW - kj, 0)
    max_exact = N_BUCKETS // 2
    nf = np.maximum(n, 1).astype(np.float64)
    val = np.log(nf / max_exact) / math.log(WINDOW / max_exact) * (N_BUCKETS - max_exact)
    assert np.all(np.abs(val - np.round(val))[(n > max_exact) & (n < WINDOW)] > 1e-3)
    large = np.minimum(max_exact + val.astype(np.int64), N_BUCKETS - 1)
    bucket = np.where(n < max_exact, n, large).reshape(-1)
    onehot = np.zeros((128, bucket.size), np.float32)
    onehot[bucket, np.arange(bucket.size)] = 1.0
    return onehot


def _split3(x):
    a = x.astype(BF16)
    r = x - a.astype(F32)
    b = r.astype(BF16)
    c = (r - b.astype(F32)).astype(BF16)
    return a, b, c


def bias_table(rel_bias):
    nh = rel_bias.shape[1]
    oh = jnp.asarray(_bucket_onehot(), BF16)
    n = oh.shape[1]
    tn = 4096
    rb = jnp.zeros((nh, 128), F32).at[:, :N_BUCKETS].set(rel_bias.T)

    def body(rb_ref, oh_ref, o_ref):
        o_ref[...] = sum(jnp.dot(t, oh_ref[...], preferred_element_type=F32) for t in _split3(rb_ref[...]))

    return pl.pallas_call(
        body, name="bias_table", grid=(n // tn,),
        in_specs=[pl.BlockSpec((nh, 128), lambda i: (0, 0)), pl.BlockSpec((128, tn), lambda i: (0, i))],
        out_specs=pl.BlockSpec((nh, tn), lambda i: (0, i)),
        out_shape=jax.ShapeDtypeStruct((nh, n), F32),
        compiler_params=_params(("parallel",)),
    )(rb, oh)


def bias_table_grad(db0, db1):
    nh, n = db0.shape
    oh = jnp.asarray(_bucket_onehot(), BF16)
    tn = 4096

    def body(a_ref, b_ref, oh_ref, o_ref):
        @pl.when(pl.program_id(0) == 0)
        def _():
            o_ref[...] = jnp.zeros_like(o_ref)

        o_ref[...] += sum(lax.dot_general(t, oh_ref[...], (((1,), (1,)), ((), ())), preferred_element_type=F32)
                          for t in _split3(a_ref[...] + b_ref[...]))

    blk = pl.BlockSpec((nh, tn), lambda i: (0, i))
    return pl.pallas_call(
        body, name="bias_table_grad", grid=(n // tn,),
        in_specs=[blk, blk, pl.BlockSpec((128, tn), lambda i: (0, i))],
        out_specs=pl.BlockSpec((nh, 128), lambda i: (0, 0)),
        out_shape=jax.ShapeDtypeStruct((nh, 128), F32),
        compiler_params=_params(("arbitrary",)),
    )(db0, db1, oh)


def _owner_view(ref, name, d):
    if name == 'a_norm':
        return ref.at[d]
    if name in COL_SHARDED:
        n = ref.shape[2] // N_DEV
        return ref.at[:, :, pl.ds(pl.multiple_of(d * n, 128), n)]
    return ref.at[:, d]


def _place():
    return lax.axis_index("x"), lax.axis_index("y"), lax.axis_index("c")


def _dev(p):
    return 4 * p[0] + 2 * p[1] + p[2]


def _remote(src, dst, send_sem, recv_sem, to):
    return pltpu.make_async_remote_copy(src_ref=src, dst_ref=dst, send_sem=send_sem, recv_sem=recv_sem,
                                        device_id=to, device_id_type=MESH)


def _dma_sems(*shapes):
    return [pltpu.SemaphoreType.DMA(sh) for sh in shapes]


def comm_call(name, build, ins, out_shapes, sems):
    n_in, n_out = len(ins), len(out_shapes)

    def body(*refs):
        copies = build(refs[:n_in], refs[n_in:n_in + n_out], *refs[n_in + n_out:])
        for cp in copies:
            cp.start()
        for cp in copies:
            cp.wait()

    hbm = pl.BlockSpec(memory_space=pl.ANY)
    return pl.pallas_call(
        body, name=name, in_specs=[hbm] * n_in, out_specs=[hbm] * n_out, out_shape=list(out_shapes),
        scratch_shapes=sems,
    )(*ins)


def all_gather_weights(names, shards, full_shapes):
    n = len(names)

    def body(*refs):
        ins, outs = refs[:n], refs[n:2 * n]
        send_sems, recv_sems, local_sems = refs[2 * n:]
        x, y, c = _place()
        me, sibling = (x, y, c), (x, y, 1 - c)
        chips = [(1 - x, y), (x, 1 - y), (1 - x, 1 - y)]

        def copy(t, k, block, to, src=None):
            dst = _owner_view(outs[t], names[t], _dev(block))
            return _remote(dst if src is None else src, dst, send_sems.at[t, k], recv_sems.at[t, k], to)

        mine = [pltpu.make_async_copy(ins[t], _owner_view(outs[t], names[t], _dev(me)), local_sems.at[t])
                for t in range(n)]
        for cp in mine:
            cp.start()
        first = []
        for t in range(n):
            first.append(copy(t, 0, me, sibling, src=ins[t]))
            first += [copy(t, 1 + j, me, (*chip, c), src=ins[t]) for j, chip in enumerate(chips)]
        for cp in first:
            cp.start()
        passed = []
        for j, chip in enumerate(chips):
            for t in range(n):
                copy(t, 1 + j, (*chip, c), me).wait_recv()
                fwd = copy(t, 4 + j, (*chip, c), sibling)
                fwd.start()
                passed.append(fwd)
        for t in range(n):
            copy(t, 0, sibling, me).wait_recv()
            for j, chip in enumerate(chips):
                copy(t, 4 + j, (*chip, 1 - c), me).wait_recv()
        for cp in first + passed:
            cp.wait_send()
        for cp in mine:
            cp.wait()

    hbm = pl.BlockSpec(memory_space=pl.ANY)
    return pl.pallas_call(
        body, name="all_gather_layer0",
        in_specs=[hbm] * n, out_specs=[hbm] * n,
        out_shape=[jax.ShapeDtypeStruct(full_shapes[t], shards[t].dtype) for t in range(n)],
        scratch_shapes=_dma_sems((n, 7), (n, 7), (n,)),
    )(*shards)


def ag_direct(names, shards, full_shapes):
    n = len(names)

    def build(ins, outs, send_sems, recv_sems, local_sems, fwd_send_sems, fwd_recv_sems):
        x, y, c = _place()
        peers = [(x, y, 1 - c), (1 - x, y, c), (x, 1 - y, c), (1 - x, 1 - y, c)]
        copies = []
        for t in range(n):
            dst = _owner_view(outs[t], names[t], _dev((x, y, c)))
            copies.append(pltpu.make_async_copy(ins[t], dst, local_sems.at[t]))
            copies += [_remote(ins[t], dst, send_sems.at[t, k], recv_sems.at[t, k], to) for k, to in enumerate(peers)]
        return copies

    def forward(ins, outs, send_sems, recv_sems, local_sems, fwd_send_sems, fwd_recv_sems):
        x, y, c = _place()
        copies = []
        for t in range(n):
            for k, chip in enumerate([(1 - x, y), (x, 1 - y), (1 - x, 1 - y)]):
                view = _owner_view(outs[t], names[t], _dev((*chip, c)))
                copies.append(_remote(view, view, fwd_send_sems.at[t, k], fwd_recv_sems.at[t, k], (x, y, 1 - c)))
        return copies

    return Carry(build, shards, [jax.ShapeDtypeStruct(full_shapes[t], shards[t].dtype) for t in range(n)],
                 _dma_sems((n, 4), (n, 4), (n,), (n, 3), (n, 3)), then=forward)


def sibling_exchange(names, grads, part_shapes):
    n = len(names)

    def build(ins, outs, send_sems, recv_sems):
        x, y, c = _place()
        return [_remote(_owner_view(ins[t], names[t], 2 * q + 1 - c), outs[t].at[q], send_sems.at[t, q],
                        recv_sems.at[t, q], (x, y, 1 - c)) for t in range(n) for q in range(4)]

    return Carry(build, grads, [jax.ShapeDtypeStruct((4,) + part_shapes[t], BF16) for t in range(n)],
                 _dma_sems((n, 4), (n, 4)))


def chip_exchange(names, parts, part_shapes):
    n = len(names)

    def build(ins, outs, send_sems, recv_sems):
        x, y, c = _place()
        chips = [(1 - x, y), (x, 1 - y), (1 - x, 1 - y)]
        return [_remote(ins[t].at[2 * chip[0] + chip[1]], outs[t].at[k], send_sems.at[t, k], recv_sems.at[t, k],
                        (*chip, c)) for t in range(n) for k, chip in enumerate(chips)]

    return Carry(build, parts, [jax.ShapeDtypeStruct((3,) + part_shapes[t], BF16) for t in range(n)],
                 _dma_sems((n, 3), (n, 3)))


def all_gather_rows(x):
    r, w = x.shape

    def body(x_ref, out_ref, send_sems, recv_sems, local_sem):
        px, py, pc = _place()
        me = 4 * px + 2 * py + pc
        mine = pltpu.make_async_copy(x_ref, out_ref.at[me], local_sem)
        mine.start()
        copies = []
        for k in range(1, N_DEV):
            peer = (px ^ (k >> 2), py ^ ((k >> 1) & 1), pc ^ (k & 1))
            copies.append(pltpu.make_async_remote_copy(
                src_ref=x_ref, dst_ref=out_ref.at[me], send_sem=send_sems.at[k - 1], recv_sem=recv_sems.at[k - 1],
                device_id=peer, device_id_type=MESH))
        for cp in copies:
            cp.start()
        for k in range(1, N_DEV):
            peer_idx = me ^ k
            pltpu.make_async_remote_copy(
                src_ref=x_ref, dst_ref=out_ref.at[peer_idx], send_sem=send_sems.at[k - 1],
                recv_sem=recv_sems.at[k - 1], device_id=(px, py, pc), device_id_type=MESH).wait_recv()
        for cp in copies:
            cp.wait_send()
        mine.wait()

    vmem = pl.BlockSpec(memory_space=pltpu.VMEM)
    return pl.pallas_call(
        body, name="all_gather_small_grads",
        in_specs=[vmem], out_specs=vmem,
        out_shape=jax.ShapeDtypeStruct((N_DEV, r, w), x.dtype),
        scratch_shapes=[pltpu.SemaphoreType.DMA((N_DEV - 1,)), pltpu.SemaphoreType.DMA((N_DEV - 1,)),
                        pltpu.SemaphoreType.DMA],
    )(x)


def _adamw(w, g, m, v):
    m = ADAM_B1 * m + (1.0 - ADAM_B1) * g
    v = ADAM_B2 * v + (1.0 - ADAM_B2) * (g * g)
    m_hat = m / (1.0 - ADAM_B1 ** ADAM_STEP)
    v_hat = v / (1.0 - ADAM_B2 ** ADAM_STEP)
    return -ADAM_LR * (m_hat / (jnp.sqrt(v_hat) + ADAM_EPS) + ADAM_WD * w), m, v


def sibling_sum(name, col, grads, recv, core):
    _, nl, rows, cols = recv.shape
    tr = _tile(rows, 1024)
    rspec = pl.BlockSpec((None, None, tr, cols), lambda q, l, i, c_ref: (q, l, i, 0))
    if col:
        gspec = pl.BlockSpec((None, tr, cols), lambda q, l, i, c_ref: (l, i, 2 * q + c_ref[0]))
    else:
        gspec = pl.BlockSpec((None, None, tr, cols), lambda q, l, i, c_ref: (l, 2 * q + c_ref[0], i, 0))

    def body(c_ref, g_ref, r_ref, o_ref):
        del c_ref
        o_ref[...] = (g_ref[...].astype(F32) + r_ref[...].astype(F32)).astype(BF16)

    return pl.pallas_call(
        body, name=name,
        grid_spec=pltpu.PrefetchScalarGridSpec(num_scalar_prefetch=1, grid=(4, nl, rows // tr),
                                               in_specs=[gspec, rspec], out_specs=rspec),
        out_shape=jax.ShapeDtypeStruct(recv.shape, BF16),
        compiler_params=_params(("parallel", "parallel", "parallel")),
    )(core.reshape(1), grads, recv)


def reduce_adamw(name, parts, recv, chip, w, m, v, l0, prev):
    _, nl, rows, cols = parts.shape
    tr = _tile(rows, 512)

    def body(q_ref, p_ref, r_ref, w_ref, m_ref, v_ref, *rest):
        del q_ref
        g_out, d_out, m_out, v_out = rest[-4:]
        g = ((p_ref[...].astype(F32) + r_ref[0].astype(F32)) + r_ref[1].astype(F32)) + r_ref[2].astype(F32)
        d, mn, vn = _adamw(w_ref[...], g, m_ref[...], v_ref[...])
        g_out[...] = g
        d_out[...] = d
        m_out[...] = mn
        v_out[...] = vn

    blk = pl.BlockSpec((None, tr, cols), lambda l, i, q_ref: (l0 + l, i, 0))
    prev = list(prev) if prev else []
    return pl.pallas_call(
        body, name=name,
        grid_spec=pltpu.PrefetchScalarGridSpec(
            num_scalar_prefetch=1, grid=(nl, rows // tr),
            in_specs=[pl.BlockSpec((None, None, tr, cols), lambda l, i, q_ref: (q_ref[0], l, i, 0)),
                      pl.BlockSpec((3, None, tr, cols), lambda l, i, q_ref: (0, l, i, 0)), blk, blk, blk]
            + [pl.BlockSpec(memory_space=pl.ANY)] * len(prev),
            out_specs=[blk] * 4),
        out_shape=[jax.ShapeDtypeStruct(w.shape, F32)] * 4,
        input_output_aliases={6 + i: i for i in range(len(prev))},
        compiler_params=_params(("parallel", "parallel")),
    )(chip.reshape(1), parts, recv, w, m, v, *prev)


def small_adamw(name, gathered, w, m, v):
    _, r, c = gathered.shape

    def body(ga_ref, w_ref, m_ref, v_ref, g_out, d_out, m_out, v_out):
        g = ga_ref[0]
        for d in range(1, N_DEV):
            g = g + ga_ref[d]
        dl, mn, vn = _adamw(w_ref[...], g, m_ref[...], v_ref[...])
        g_out[...] = g
        d_out[...] = dl
        m_out[...] = mn
        v_out[...] = vn

    return pl.pallas_call(
        body, name=name,
        out_shape=[jax.ShapeDtypeStruct((r, c), F32)] * 4,
        compiler_params=_params(),
    )(gathered, w, m, v)


def _rms(x, g):
    return x * lax.rsqrt(jnp.mean(x * x, axis=-1, keepdims=True) + EPS) * g


def _rms_bwd_epilogue(dn, x, dres, g):
    r = lax.rsqrt(jnp.mean(x * x, axis=-1, keepdims=True) + EPS)
    xh = x * r
    dyg = dn * g
    dx = dres + r * (dyg - xh * jnp.mean(dyg * xh, axis=-1, keepdims=True))
    return dx, dx, jnp.sum(dn * xh, axis=0, keepdims=True), jnp.sum(dx, axis=0, keepdims=True)


def _residual_then_norms(n_terms):
    def epilogue(acc, *ex):
        h = acc
        for t in ex[:n_terms]:
            h = h + t
        return (h,) + tuple(_rms(h, g) for g in ex[n_terms:])
    return epilogue


def local_step(x, target, small, ex):
    s, d = x.shape
    n_a, n_b = small['a_norm'].shape[0], small['b_norm'].shape[0]
    sg = {}
    gb = {}

    def fwd_mm(name, a, wname, layer, epilogue, extras, out_dtypes, **kw):
        return mm_nn(name, a, *ex.weight(wname, layer), epilogue, extras, out_dtypes, **kw)

    def dx_mm(name, dy, wname, layer, epilogue, extras, out_dtypes, **kw):
        return mm_nt(name, dy, *ex.weight(wname, layer), epilogue, extras, out_dtypes, **kw)

    def dw_mm(name, a, dy, wname, layer, **kw):
        key, slab, shape = ex.grad(wname, layer)
        gb[key] = mm_tn(name, a, dy, gb.get(key), shape, slab, **kw)

    plain = lambda acc: (acc,)
    plus_col = lambda acc, b: (acc + b,)

    bias_flat = bias_table(small['rel_bias'])
    bias_t = bias_flat.reshape(2, 8, WINDOW, 2 * WINDOW).transpose(0, 3, 1, 2).reshape(2, 2 * WINDOW, 8 * WINDOW)
    sink_rows = [jnp.repeat(small['b_sinks'][j], WINDOW).reshape(2, 1, 8 * WINDOW) for j in range(n_b)]

    gain = lambda g: g.reshape(1, -1)

    def mlp_fwd(h, n2, layer, next_gains):
        u, a = fwd_mm(f"mlp_up_fwd{layer}", n2, 'mlp_up', layer,
                      lambda acc: (acc, jnp.square(jnp.maximum(acc, 0.0))), (), (BF16, BF16))
        h2, *nexts = fwd_mm(f"mlp_down_fwd{layer}", a, 'mlp_down', layer, _residual_then_norms(1),
                            (h, *[gain(g) for g in next_gains]), (F32,) + (BF16,) * len(next_gains))
        return h2, nexts, (n2, u, a)

    h = x
    saved = []
    n1 = rms_fwd("a_norm_fwd0", h, small['a_norm'][0])
    for l in range(n_a):
        (qkvt,) = fwd_mm(f"a_qkv_fwd{l}", n1, 'a_wqkv', l, plain, (), (BF16,), out_t=True)
        qkvt = qkvt.reshape(3 * d // HEAD_DIM, HEAD_DIM, s)
        o_t, rtab, carried = sb_fwd(f"sb_fwd{l}", qkvt, ex.fwd_carry(l))
        ex.fwd_done(l, carried)
        o_t = o_t.reshape(d, s)
        h_mid, n2 = fwd_mm(f"a_wo_fwd{l}", o_t, 'a_wo', l, _residual_then_norms(1),
                           (h, gain(small['mlp_norm'][l])), (F32, BF16), a_t=True)
        next_gains = [small['a_norm'][l + 1]] if l + 1 < n_a else [small['b_norm'][0], small['kv_norm']]
        h_out, nexts, mlp_saved = mlp_fwd(h_mid, n2, l, next_gains)
        saved.append((h, n1, qkvt, o_t, rtab, h_mid, mlp_saved))
        h, n1 = h_out, nexts[0]
    h_kv, nkv = h, nexts[1]
    (kvt,) = fwd_mm("kv_fwd", nkv, 'w_kv', 0, plus_col, (small['b_kv'].reshape(-1, 1),), (BF16,), out_t=True)
    kvt = kvt.reshape(2, 2, HEAD_DIM, s)
    kpt, vpt = (jnp.pad(t, ((0, 0), (0, 0), (WINDOW, 0))) for t in (kvt[0], kvt[1]))
    for j in range(n_b):
        layer = n_a + j
        (qbt,) = fwd_mm(f"b_q_fwd{j}", n1, 'b_wq', j, plus_col, (small['b_bq'][j].reshape(-1, 1),), (BF16,),
                        out_t=True)
        o_t = swa_fwd(f"swa_fwd{j}", qbt, kpt, vpt, bias_t, sink_rows[j])
        h_mid, n2 = fwd_mm(f"b_wo_fwd{j}", o_t, 'b_wo', j, _residual_then_norms(2),
                           (h, gain(small['b_bo'][j]), gain(small['mlp_norm'][layer])), (F32, BF16), a_t=True)
        h_out, nexts, mlp_saved = mlp_fwd(h_mid, n2, layer, [small['b_norm'][j + 1]] if j + 1 < n_b else [])
        saved.append((h, n1, qbt, o_t, h_mid, mlp_saved))
        h, n1 = h_out, (nexts[0] if nexts else None)

    dh, dhb, dg_final, loss_b = loss_head(h, small['final_norm'], target)
    sg['final_norm'] = dg_final[0]
    sg['mlp_norm'] = [None] * (n_a + n_b)

    def mlp_bwd(dh, dhb, h_mid, mlp_saved, layer):
        n2, u, a = mlp_saved
        du, *carried = dx_mm(f"mlp_down_dx{layer}", dhb, 'mlp_down', layer,
                             lambda acc, uu: (acc * (2.0 * jnp.maximum(uu.astype(F32), 0.0)),), (u,), (BF16,),
                             exchange=ex.mlp_carry(layer, gb))
        ex.mlp_done(layer, carried)
        dw_mm(f"mlp_down_dw{layer}", a, dhb, 'mlp_down', layer)
        dh2, dh2b, dg, cs = dx_mm(f"mlp_up_dx{layer}", du, 'mlp_up', layer, _rms_bwd_epilogue,
                                  (h_mid, dh, gain(small['mlp_norm'][layer])), (F32, BF16), n_sums=2)
        dw_mm(f"mlp_up_dw{layer}", n2, du, 'mlp_up', layer)
        sg['mlp_norm'][layer] = dg[0]
        return dh2, dh2b, cs

    dkp = jnp.zeros(kpt.shape, F32)
    dvp = jnp.zeros(vpt.shape, F32)
    sg['b_norm'], sg['b_bq'], sg['b_bo'], sg['b_sinks'] = [None] * n_b, [None] * n_b, [None] * n_b, [None] * n_b
    dbias = [None] * n_b
    for j in reversed(range(n_b)):
        layer = n_a + j
        h_in, n1, qbt, o_t, h_mid, mlp_saved = saved[layer]
        dh, dhb, cs = mlp_bwd(dh, dhb, h_mid, mlp_saved, layer)
        sg['b_bo'][j] = cs[0]
        (do_t,) = dx_mm(f"b_wo_dx{j}", dhb, 'b_wo', j, plain, (), (BF16,), out_t=True)
        dw_mm(f"b_wo_dw{j}", o_t, dhb, 'b_wo', j, x_t=True)
        dq_t, dkp, dvp, dbias[j], dsink = swa_bwd(f"swa_bwd{j}", qbt, kpt, vpt, bias_t, sink_rows[j], do_t, dkp, dvp)
        sg['b_sinks'][j] = colsum(f"sink_grad{j}", dsink.reshape(16, WINDOW).T)[0]
        sg['b_bq'][j] = rowsum(f"b_bq_grad{j}", dq_t)
        dh, dhb, dg, _ = dx_mm(f"b_q_dx{j}", dq_t, 'b_wq', j, _rms_bwd_epilogue,
                               (h_in, dh, gain(small['b_norm'][j])), (F32, BF16), a_t=True, n_sums=2)
        dw_mm(f"b_q_dw{j}", n1, dq_t, 'b_wq', j, dy_t=True)
        sg['b_norm'][j] = dg[0]
    unt = lambda t: t.reshape(2, 2 * WINDOW, 8, WINDOW).transpose(0, 2, 3, 1).reshape(bias_flat.shape)
    sg['rel_bias'] = bias_table_grad(unt(dbias[0]), unt(dbias[1]))[:, :N_BUCKETS].T

    dkv_t = jnp.concatenate([dkp[:, :, WINDOW:], dvp[:, :, WINDOW:]], axis=0).reshape(-1, s)
    sg['b_kv'] = rowsum("b_kv_grad", dkv_t)
    dkvb = dkv_t.astype(BF16)
    dh, dhb, dg, _ = dx_mm("kv_dx", dkvb, 'w_kv', 0, _rms_bwd_epilogue, (h_kv, dh, gain(small['kv_norm'])),
                           (F32, BF16), a_t=True, n_sums=2)
    dw_mm("kv_dw", nkv, dkvb, 'w_kv', 0, dy_t=True)
    sg['kv_norm'] = dg[0]

    sg['a_norm'] = [None] * n_a
    for l in reversed(range(n_a)):
        h_in, n1, qkvt, o_t, rtab, h_mid, mlp_saved = saved[l]
        dh, dhb, _ = mlp_bwd(dh, dhb, h_mid, mlp_saved, l)
        (do_t,) = dx_mm(f"a_wo_dx{l}", dhb, 'a_wo', l, plain, (), (BF16,), out_t=True)
        dw_mm(f"a_wo_dw{l}", o_t, dhb, 'a_wo', l, x_t=True)
        dqkv_t, carried = sb_bwd(f"sb_bwd{l}", qkvt, do_t.reshape(d // HEAD_DIM, HEAD_DIM, s), rtab,
                                 ex.bwd_carry(l, gb))
        ex.bwd_done(l, carried)
        dqkv_t = dqkv_t.reshape(3 * d, s)
        dw_mm(f"a_qkv_dw{l}", n1, dqkv_t, 'a_wqkv', l, dy_t=True)
        dh, dhb, dg, _, *carried = dx_mm(f"a_qkv_dx{l}", dqkv_t, 'a_wqkv', l, _rms_bwd_epilogue,
                                         (h_in, dh, gain(small['a_norm'][l])), (F32, BF16), a_t=True, n_sums=2,
                                         exchange=ex.last_carry(gb) if l == 0 else None)
        if l == 0:
            ex.last_done(carried)
        sg['a_norm'][l] = dg[0]

    small_grads = {
        'a_norm': jnp.stack(sg['a_norm']), 'kv_norm': sg['kv_norm'], 'b_kv': sg['b_kv'],
        'b_norm': jnp.stack(sg['b_norm']), 'b_bq': jnp.stack(sg['b_bq']), 'b_sinks': jnp.stack(sg['b_sinks']),
        'b_bo': jnp.stack(sg['b_bo']), 'rel_bias': sg['rel_bias'], 'mlp_norm': jnp.stack(sg['mlp_norm']),
        'final_norm': sg['final_norm'],
    }
    return loss_b, dh, gb, small_grads


def _full_shape(name, shard_shape):
    if name in COL_SHARDED:
        return shard_shape[:2] + (N_DEV * shard_shape[2],)
    nl, r, n = shard_shape
    return (nl, N_DEV, r, n)


def _as_w3_shape(name, shard_shape):
    full = _full_shape(name, shard_shape)
    return full if name in COL_SHARDED else (full[0], full[1] * full[2], full[3])


def _as_w3(name, full):
    if name in COL_SHARDED:
        return full
    nl, nd, r, n = full.shape
    return full.reshape(nl, nd * r, n)


AG_GROUPS = {
    0: (('a_wqkv', 0, 1),),
    1: (('a_wo', 0, 2), ('mlp_up', 0, 2), ('mlp_down', 0, 2), ('a_wqkv', 1, 1)),
    2: (('mlp_up', 2, 2), ('mlp_down', 2, 2), ('b_wq', 0, 2), ('b_wo', 0, 2), ('w_kv', 0, 1)),
}
RS_GROUPS = {
    'A': (('mlp_up', 2, 2), ('mlp_down', 2, 2), ('b_wq', 0, 2), ('b_wo', 0, 2), ('w_kv', 0, 1)),
    'B1': (('a_wo', 1, 1), ('mlp_up', 1, 1), ('mlp_down', 1, 1)),
    'B2': (('a_wqkv', 1, 1),),
    'B3': (('a_wo', 0, 1), ('mlp_up', 0, 1), ('mlp_down', 0, 1)),
    'C': (('a_wqkv', 0, 1),),
}
UNDER_MLP = {1: 'A', 0: 'B2'}
SIBLING_UNDER_SB_BWD = {1: 'B1'}
UNDER_SB_BWD = {1: ('A',), 0: ('B1', 'B2', 'B3')}


class _Exchanges:
    def __init__(self, full0, shards, core, chip, w3, m3, v3):
        self.wbuf = {0: {n: _as_w3(n, full0[n]) for n, _, _ in AG_GROUPS[0]}}
        self.shards, self.core, self.chip = shards, core, chip
        self.w3, self.m3, self.v3 = w3, m3, v3
        self.shard_dims = {n: w3[n].shape[1:] for n in BIG}
        self.parts = {}
        self.gfull = {}
        self.out = {}

    def weight(self, name, layer):
        for group, members in AG_GROUPS.items():
            for n, l0, nl in members:
                if n == name and l0 <= layer < l0 + nl:
                    return self.wbuf[group][name], layer - l0
        raise KeyError((name, layer))

    def fwd_carry(self, layer):
        names = [n for n, _, _ in AG_GROUPS[layer + 1]]
        shards = [self.shards[layer + 1][n] for n in names]
        return ag_direct(names, shards, [_full_shape(n, sh.shape) for n, sh in zip(names, shards)])

    def fwd_done(self, layer, carried):
        names = [n for n, _, _ in AG_GROUPS[layer + 1]]
        self.wbuf[layer + 1] = {n: _as_w3(n, f) for n, f in zip(names, carried)}

    def grad(self, name, layer):
        for group, members in RS_GROUPS.items():
            for n, l0, nl in members:
                if n == name and l0 <= layer < l0 + nl:
                    return (group, name), layer - l0, _as_w3_shape(name, (nl,) + self.shard_dims[name])
        raise KeyError((name, layer))

    def _members(self, group):
        names = [n for n, _, _ in RS_GROUPS[group]]
        return names, [(nl,) + self.shard_dims[n] for n, _, nl in RS_GROUPS[group]]

    def _sibling_carry(self, group, gb):
        names, shapes = self._members(group)
        self.gfull[group] = [gb[(group, n)].reshape(_full_shape(n, sh)) for n, sh in zip(names, shapes)]
        return sibling_exchange(names, self.gfull[group], shapes)

    def _sibling_done(self, group, recv):
        names, _ = self._members(group)
        self.parts[group] = [sibling_sum(f"rs_sibling_sum_{group}_{n}", n in COL_SHARDED, g, r, self.core)
                             for n, g, r in zip(names, self.gfull[group], recv)]

    def _sibling_stage(self, group, gb):
        ce = self._sibling_carry(group, gb)
        self._sibling_done(group, comm_call(f"rs_sibling_exchange_{group}", ce.build, ce.ins, ce.out_shapes, ce.sems))

    def mlp_carry(self, layer, gb):
        return self._sibling_carry(UNDER_MLP[layer], gb) if layer in UNDER_MLP else None

    def mlp_done(self, layer, carried):
        if layer in UNDER_MLP:
            self._sibling_done(UNDER_MLP[layer], carried)

    def bwd_carry(self, layer, gb):
        names, parts, shapes = [], [], []
        for group in UNDER_SB_BWD[layer]:
            if group not in self.parts:
                self._sibling_stage(group, gb)
            names += self._members(group)[0]
            shapes += self._members(group)[1]
            parts += self.parts[group]
        exchange = chip_exchange(names, parts, shapes)
        if layer in SIBLING_UNDER_SB_BWD:
            exchange = _together(exchange, self._sibling_carry(SIBLING_UNDER_SB_BWD[layer], gb))
        return exchange

    def bwd_done(self, layer, carried):
        for group in UNDER_SB_BWD[layer]:
            n = len(RS_GROUPS[group])
            self._adamw(group, carried[:n])
            carried = carried[n:]
        if layer in SIBLING_UNDER_SB_BWD:
            self._sibling_done(SIBLING_UNDER_SB_BWD[layer], carried)

    def last_carry(self, gb):
        self._sibling_stage('C', gb)
        names, shapes = self._members('C')
        return chip_exchange(names, self.parts['C'], shapes)

    def last_done(self, carried):
        self._adamw('C', carried)

    def _adamw(self, group, recv2):
        for (n, l0, _), p, r in zip(RS_GROUPS[group], self.parts[group], recv2):
            self.out[n] = reduce_adamw(f"adamw_{group}_{n}", p, r, self.chip, self.w3[n], self.m3[n], self.v3[n],
                                       l0, self.out.get(n))


PACK_LANES = 1024


def _small_layout(shapes):
    out, r0 = [], 0
    for n in SMALL + ['loss']:
        size = int(np.prod(shapes[n]))
        if size % PACK_LANES == 0:
            k, lanes = size // PACK_LANES, PACK_LANES
        elif len(shapes[n]) == 2 and shapes[n][0] <= 8:
            k, lanes = shapes[n]
        else:
            k, lanes = 1, size
        assert size == k * lanes and lanes <= PACK_LANES
        out.append((n, r0, k, lanes))
        r0 += k
    return out, -(-r0 // 8) * 8


def _pack_small(vals, layout, rows):
    parts = [jnp.pad(vals[n].astype(F32).reshape(k, lanes), ((0, 0), (0, PACK_LANES - lanes)))
             for n, _, k, lanes in layout]
    used = sum(k for _, _, k, _ in layout)
    return jnp.concatenate(parts + [jnp.zeros((rows - used, PACK_LANES), F32)], axis=0)


def small_adamw_all(gathered, layout, w, m, v):
    _, r, c = gathered.shape
    params = [n for n, _, _, _ in layout if n in w]
    views = {n: (k, lanes) for n, _, k, lanes in layout}
    n_in = 3 * len(params)

    def body(ga_ref, *refs):
        wmv, outs, g_ref = refs[:n_in], list(refs[n_in:-1]), refs[-1]
        g = ga_ref[0]
        for d in range(1, N_DEV):
            g = g + ga_ref[d]
        g_ref[...] = g
        for n, r0, k, lanes in layout:
            gi = g_ref[r0:r0 + k, :lanes]
            if n in w:
                i = 3 * params.index(n)
                dl, mn, vn = _adamw(wmv[i][...], gi, wmv[i + 1][...], wmv[i + 2][...])
                for val in (gi, dl, mn, vn):
                    outs.pop(0)[...] = val
            else:
                outs.pop(0)[...] = gi

    out_shape, sizes = [], []
    for n, _, k, lanes in layout:
        sizes.append(4 if n in w else 1)
        out_shape += [jax.ShapeDtypeStruct((k, lanes), F32)] * sizes[-1]
    flat = pl.pallas_call(
        body, name="adamw_small",
        out_shape=out_shape,
        scratch_shapes=[pltpu.VMEM((r, c), F32)],
        compiler_params=_params(),
    )(gathered, *[src[n].reshape(views[n]) for n in params for src in (w, m, v)])
    out, at = {}, 0
    for (n, _, _, _), size in zip(layout, sizes):
        out[n] = flat[at:at + size]
        at += size
    return out


def kernel(x, a_norm, a_wqkv, a_wo, kv_norm, w_kv, b_kv, b_norm, b_wq, b_bq, b_sinks, b_wo, b_bo, rel_bias, mlp_norm, mlp_up, mlp_down, final_norm, loss_target, m_a_norm, m_a_wqkv, m_a_wo, m_kv_norm, m_w_kv, m_b_kv, m_b_norm, m_b_wq, m_b_bq, m_b_sinks, m_b_wo, m_b_bo, m_rel_bias, m_mlp_norm, m_mlp_up, m_mlp_down, m_final_norm, v_a_norm, v_a_wqkv, v_a_wo, v_kv_norm, v_w_kv, v_b_kv, v_b_norm, v_b_wq, v_b_bq, v_b_sinks, v_b_wo, v_b_bo, v_rel_bias, v_mlp_norm, v_mlp_up, v_mlp_down, v_final_norm):
    w = dict(a_norm=a_norm, a_wqkv=a_wqkv, a_wo=a_wo, kv_norm=kv_norm, w_kv=w_kv, b_kv=b_kv, b_norm=b_norm,
             b_wq=b_wq, b_bq=b_bq, b_sinks=b_sinks, b_wo=b_wo, b_bo=b_bo, rel_bias=rel_bias, mlp_norm=mlp_norm,
             mlp_up=mlp_up, mlp_down=mlp_down, final_norm=final_norm)
    m = dict(a_norm=m_a_norm, a_wqkv=m_a_wqkv, a_wo=m_a_wo, kv_norm=m_kv_norm, w_kv=m_w_kv, b_kv=m_b_kv,
             b_norm=m_b_norm, b_wq=m_b_wq, b_bq=m_b_bq, b_sinks=m_b_sinks, b_wo=m_b_wo, b_bo=m_b_bo,
             rel_bias=m_rel_bias, mlp_norm=m_mlp_norm, mlp_up=m_mlp_up, mlp_down=m_mlp_down, final_norm=m_final_norm)
    v = dict(a_norm=v_a_norm, a_wqkv=v_a_wqkv, a_wo=v_a_wo, kv_norm=v_kv_norm, w_kv=v_w_kv, b_kv=v_b_kv,
             b_norm=v_b_norm, b_wq=v_b_wq, b_bq=v_b_bq, b_sinks=v_b_sinks, b_wo=v_b_wo, b_bo=v_b_bo,
             rel_bias=v_rel_bias, mlp_norm=v_mlp_norm, mlp_up=v_mlp_up, mlp_down=v_mlp_down, final_norm=v_final_norm)
    px, py, pc = _place()
    me = 4 * px + 2 * py + pc
    chip = (2 * px + py).astype(jnp.int32)
    core = pc.astype(jnp.int32)

    as3 = lambda t: t[None] if t.ndim == 2 else t
    w3, m3, v3 = ({n: as3(src[n]) for n in BIG} for src in (w, m, v))
    shards = {g: {n: w3[n][l0:l0 + nl].astype(BF16) for n, l0, nl in members} for g, members in AG_GROUPS.items()}
    an_pad = jnp.zeros((8, 128), F32).at[:a_norm.shape[0]].set(a_norm)
    names0 = [n for n, _, _ in AG_GROUPS[0]]
    full0 = all_gather_weights(names0 + ['a_norm'], [shards[0][n] for n in names0] + [an_pad],
                               [_full_shape(n, shards[0][n].shape) for n in names0] + [(N_DEV, 8, 128)])
    full0 = dict(zip(names0 + ['a_norm'], full0))
    n_a = a_norm.shape[0]
    small = {n: w[n] for n in SMALL}
    small['a_norm'] = full0['a_norm'][:, :n_a].transpose(1, 0, 2).reshape(n_a, -1)

    ex = _Exchanges(full0, shards, core, chip, w3, m3, v3)
    loss_b, grad_x, gb, sgrads = local_step(x[0], loss_target[0], small, ex)
    out = {n: [t.reshape(w[n].shape) for t in bufs] for n, bufs in ex.out.items()}

    sgrads['loss'] = loss_b[0, :1]
    shapes = {n: w[n].shape for n in SMALL}
    shapes['a_norm'] = (n_a, a_norm.shape[1] * N_DEV)
    shapes['loss'] = (1,)
    layout, rows = _small_layout(shapes)
    gathered = all_gather_rows(_pack_small(sgrads, layout, rows))
    replicated = [n for n in SMALL if n != 'a_norm']
    res = small_adamw_all(gathered, layout, *({n: src[n] for n in replicated} for src in (w, m, v)))
    sm = [{n: res[n][i].reshape(shapes[n]) for n in replicated} for i in range(4)]
    g_an = lax.dynamic_slice_in_dim(res['a_norm'][0], me * a_norm.shape[1], a_norm.shape[1], axis=1)
    pad = lambda t: jnp.zeros((8, 128), F32).at[:n_a].set(t)
    gathered_an = jnp.zeros((N_DEV, 8, 128), F32).at[0].set(pad(g_an))
    an = small_adamw("adamw_a_norm", gathered_an, pad(a_norm), pad(m_a_norm), pad(v_a_norm))
    for i in range(4):
        sm[i]['a_norm'] = an[i][:n_a]
    for n in BIG:
        for i in range(4):
            sm[i][n] = out[n][i]
    loss = res['loss'][0][0, 0]
    return (loss, grad_x[None], *[sm[0][n] for n in WEIGHTS], *[sm[1][n] for n in WEIGHTS],
            *[sm[2][n] for n in WEIGHTS], *[sm[3][n] for n in WEIGHTS])
```
